```python
import jax, jax.numpy as jnp
from jax import lax
import numpy as np

D_MODEL = 1024
BATCH = 16
SEQ = 2048
DEPTH = 1

GRID_W = 64
CTX_LEN = 256

RET_HEADS = 4
RET_DK = 256
RET_DV = 512
RET_CHUNK = 128
RET_QK_W = RET_HEADS * RET_DK
RET_V_W = RET_HEADS * RET_DV

ATT_HEADS = 8
ATT_KV_HEADS = 2
ATT_HEAD_DIM = 128
ATT_Q_W = ATT_HEADS * ATT_HEAD_DIM
ATT_KV_W = ATT_KV_HEADS * ATT_HEAD_DIM
ROPE_THETA = 10000.0
Q_BLOCK = 128

NORM_EPS = 1e-6

IN_SPLITS = (RET_QK_W, RET_V_W, ATT_KV_W, ATT_KV_W,
             RET_QK_W, RET_V_W, ATT_Q_W, ATT_Q_W, D_MODEL, D_MODEL)
KV_COLS = RET_QK_W + RET_V_W + 2 * ATT_KV_W
IN_COLS = KV_COLS + RET_QK_W + RET_V_W + 2 * ATT_Q_W + 2 * D_MODEL

kernel_name = "hybrid_retention_gqa_prefix_dit_block"


def rms_norm(x, w=None):
    xf = x.astype(jnp.float32)
    y = xf * lax.rsqrt(jnp.mean(xf * xf, axis=-1, keepdims=True) + NORM_EPS)
    if w is not None:
        y = y * w.astype(jnp.float32)
    return y.astype(x.dtype)


def split_cols(p, widths):
    out, start = [], 0
    for w in widths:
        out.append(p[..., start:start + w])
        start += w
    return out


def adaln_params(cvec, w_ada, b_ada):
    mod = jax.nn.silu(cvec) @ w_ada + b_ada
    return split_cols(mod, (D_MODEL, D_MODEL, D_MODEL))


def axial_rope(L):
    rows = L // GRID_W
    row = jnp.repeat(jnp.arange(rows, dtype=jnp.float32), GRID_W)
    col = jnp.tile(jnp.arange(GRID_W, dtype=jnp.float32), rows)
    half = ATT_HEAD_DIM // 2
    freqs = ROPE_THETA ** (-jnp.arange(0, half, 2, dtype=jnp.float32) / half)
    ang = jnp.concatenate([row[:, None] * freqs, col[:, None] * freqs], axis=-1)
    return jnp.cos(ang), jnp.sin(ang)


def apply_rope(x, cos, sin):
    B, L, H, D = x.shape
    xp = x.astype(jnp.float32).reshape(B, L, H, D // 2, 2)
    x0, x1 = xp[..., 0], xp[..., 1]
    c, s = cos[None, :, None, :], sin[None, :, None, :]
    out = jnp.stack([x0 * c - x1 * s, x0 * s + x1 * c], axis=-1)
    return out.reshape(B, L, H, D).astype(x.dtype)


def ret_kv(k_flat, v_flat):
    B, L, _ = k_flat.shape
    k = k_flat.reshape(B, L, RET_HEADS, RET_DK) * (RET_DK ** -0.5)
    v = v_flat.reshape(B, L, RET_HEADS, RET_DV)
    return k, v


def retention_scan(q, k, v, log_gamma, s0):
    B, L, H, _ = q.shape
    n = L // RET_CHUNK

    def chunks(t):
        return t.astype(jnp.float32).reshape(B, n, RET_CHUNK, H, t.shape[-1]).transpose(1, 0, 3, 2, 4)

    qc, kc, vc = chunks(q), chunks(k), chunks(v)
    idx = jnp.arange(RET_CHUNK, dtype=jnp.float32)
    lg = log_gamma.astype(jnp.float32)[:, None]
    rel = idx[:, None] - idx[None, :]
    decay_mask = jnp.where(rel >= 0, jnp.exp(lg[:, :, None] * jnp.maximum(rel, 0.0)), 0.0)
    q_decay = jnp.exp(lg * (idx + 1.0))
    k_decay = jnp.exp(lg * (RET_CHUNK - 1.0 - idx))
    chunk_decay = jnp.exp(lg[:, 0] * RET_CHUNK)

    def step(s, inp):
        qi, ki, vi = inp
        scores = jnp.einsum('bhid,bhjd->bhij', qi, ki) * decay_mask
        intra = jnp.einsum('bhij,bhje->bhie', scores, vi)
        cross = jnp.einsum('bhid,bhde->bhie', qi * q_decay[..., None], s)
        s_new = s * chunk_decay[:, None, None] + jnp.einsum('bhjd,bhje->bhde', ki * k_decay[..., None], vi)
        return s_new, intra + cross

    _, out = lax.scan(step, s0.astype(jnp.float32), (qc, kc, vc))
    return out.transpose(1, 0, 3, 2, 4).reshape(B, L, H, v.shape[-1])


def ctx_final_states(k, v, log_gamma):
    L = k.shape[1]
    pos = jnp.arange(L, dtype=jnp.float32)
    w_f = jnp.exp(log_gamma[0][:, None] * (L - 1.0 - pos))
    w_b = jnp.exp(log_gamma[1][:, None] * pos)
    kf, vf = k.astype(jnp.float32), v.astype(jnp.float32)
    s_f = jnp.einsum('blhd,hl,blhe->bhde', kf, w_f, vf)
    s_b = jnp.einsum('blhd,hl,blhe->bhde', kf, w_b, vf)
    return s_f, s_b


def retention_branch(q_flat, k, v, gate_flat, log_gamma, states):
    B, L, _ = q_flat.shape
    q = q_flat.reshape(B, L, RET_HEADS, RET_DK)
    o_f = retention_scan(q, k, v, log_gamma[0], states[0])
    o_b = retention_scan(q[:, ::-1], k[:, ::-1], v[:, ::-1], log_gamma[1], states[1])[:, ::-1]
    o = rms_norm(o_f + o_b)
    return o.reshape(B, L, RET_V_W).astype(gate_flat.dtype) * jax.nn.silu(gate_flat)


def att_heads(t_flat, n_heads, norm_w, rope):
    B, L, _ = t_flat.shape
    t = rms_norm(t_flat.reshape(B, L, n_heads, ATT_HEAD_DIM), norm_w)
    if rope is not None:
        t = apply_rope(t, rope[0], rope[1])
    return t


def gqa_block_attention(q, k_all, v_all):
    B, S, Hq, D = q.shape
    nb = S // Q_BLOCK
    rep = Hq // ATT_KV_HEADS
    qb = q.reshape(B, nb, Q_BLOCK, ATT_KV_HEADS, rep, D).transpose(1, 0, 2, 3, 4, 5)
    scale = D ** -0.5

    def one_block(qi):
        s = jnp.einsum('bqgrd,bkgd->bgrqk', qi, k_all).astype(jnp.float32) * scale
        p = jax.nn.softmax(s, axis=-1)
        return jnp.einsum('bgrqk,bkgd->bqgrd', p.astype(v_all.dtype), v_all)

    o = lax.map(one_block, qb)
    return o.transpose(1, 0, 2, 3, 4, 5).reshape(B, S, Hq * D)


def merge_branches(y_ret, y_att, mg_ret, mg_att, w_o_ret, w_o_att, w_out):
    y = jax.nn.sigmoid(mg_ret) * (y_ret @ w_o_ret) + jax.nn.sigmoid(mg_att) * (y_att @ w_o_att)
    return y @ w_out


def trunk_layer(x, ctx, c, c_ctx, norm_w, w_ada, b_ada, w_in, ret_log2_decay,
                q_norm_w, k_norm_w, w_o_ret, w_o_att, w_out, update_ctx):
    B, S, _ = x.shape
    shift, scale, gate = adaln_params(c, w_ada, b_ada)
    shift_c, scale_c, gate_c = adaln_params(c_ctx, w_ada, b_ada)
    hx = rms_norm(x, norm_w) * (1 + scale[:, None]) + shift[:, None]
    hc = rms_norm(ctx, norm_w) * (1 + scale_c) + shift_c
    log_gamma = jnp.log1p(-jnp.exp2(ret_log2_decay.astype(jnp.float32)))

    px = hx @ w_in
    ret_k, ret_v, att_k, att_v, ret_q, ret_g, att_q, att_g, mg_ret, mg_att = split_cols(px, IN_SPLITS)
    pc = hc @ (w_in if update_ctx else w_in[:, :KV_COLS])
    c_ret_k, c_ret_v, c_att_k, c_att_v = split_cols(pc, IN_SPLITS[:4])

    kc, vc = ret_kv(c_ret_k, c_ret_v)
    ctx_states = ctx_final_states(kc, vc, log_gamma)
    kca = att_heads(c_att_k, ATT_KV_HEADS, k_norm_w, None)
    vca = c_att_v.reshape(B, CTX_LEN, ATT_KV_HEADS, ATT_HEAD_DIM)

    kx, vx = ret_kv(ret_k, ret_v)
    y_ret = retention_branch(ret_q, kx, vx, ret_g, log_gamma, ctx_states)

    rope = axial_rope(S)
    qxa = att_heads(att_q, ATT_HEADS, q_norm_w, rope)
    kxa = att_heads(att_k, ATT_KV_HEADS, k_norm_w, rope)
    vxa = att_v.reshape(B, S, ATT_KV_HEADS, ATT_HEAD_DIM)
    k_all = jnp.concatenate([kxa, kca], axis=1)
    v_all = jnp.concatenate([vxa, vca], axis=1)
    y_att = gqa_block_attention(qxa, k_all, v_all) * jax.nn.silu(att_g)

    out = merge_branches(y_ret, y_att, mg_ret, mg_att, w_o_ret, w_o_att, w_out)
    x_new = x + gate[:, None] * out

    if update_ctx:
        _, _, _, _, c_ret_q, c_ret_g, c_att_q, c_att_g, c_mg_ret, c_mg_att = split_cols(pc, IN_SPLITS)
        zero = jnp.zeros((B, RET_HEADS, RET_DK, RET_DV), jnp.float32)
        yc_ret = retention_branch(c_ret_q, kc, vc, c_ret_g, log_gamma, (zero, zero))
        qca = att_heads(c_att_q, ATT_HEADS, q_norm_w, None)
        yc_att = gqa_block_attention(qca, kca, vca) * jax.nn.silu(c_att_g)
        out_c = merge_branches(yc_ret, yc_att, c_mg_ret, c_mg_att, w_o_ret, w_o_att, w_out)
        ctx = ctx + gate_c * out_c
    return x_new, ctx


def _fwd_setup_inputs(seed: int = 0) -> dict:
    key = jax.random.key(seed)
    ks = jax.random.split(key, 15)
    f32 = jnp.float32
    nrm = lambda k, shape: jax.random.normal(k, shape, f32)
    x = nrm(ks[0], (BATCH, SEQ, D_MODEL))
    c = nrm(ks[1], (BATCH, D_MODEL))
    ctx = nrm(ks[2], (BATCH, CTX_LEN, D_MODEL))
    c_ctx = nrm(ks[3], (D_MODEL,))
    norm_w = 1.0 + 0.02 * nrm(ks[4], (DEPTH, D_MODEL))
    w_ada = nrm(ks[5], (DEPTH, D_MODEL, 3 * D_MODEL)) * (0.5 * D_MODEL ** -0.5)
    b_ada = 0.02 * nrm(ks[6], (DEPTH, 3 * D_MODEL))
    w_in = nrm(ks[7], (DEPTH, D_MODEL, IN_COLS)) * (D_MODEL ** -0.5)
    base = -5.0 - jnp.arange(RET_HEADS, dtype=f32)
    ret_log2_decay = base[None, None, :] + 0.1 * nrm(ks[8], (DEPTH, 2, RET_HEADS))
    q_norm_w = 1.0 + 0.02 * nrm(ks[9], (DEPTH, ATT_HEAD_DIM))
    k_norm_w = 1.0 + 0.02 * nrm(ks[10], (DEPTH, ATT_HEAD_DIM))
    w_o_ret = nrm(ks[11], (DEPTH, RET_V_W, D_MODEL)) * (RET_V_W ** -0.5)
    w_o_att = nrm(ks[12], (DEPTH, ATT_Q_W, D_MODEL)) * (ATT_Q_W ** -0.5)
    w_out = nrm(ks[13], (DEPTH, D_MODEL, D_MODEL)) * (D_MODEL ** -0.5)
    return {"x": x, "c": c, "ctx": ctx, "c_ctx": c_ctx, "norm_w": norm_w,
            "w_ada": w_ada, "b_ada": b_ada, "w_in": w_in, "ret_log2_decay": ret_log2_decay,
            "q_norm_w": q_norm_w, "k_norm_w": k_norm_w, "w_o_ret": w_o_ret,
            "w_o_att": w_o_att, "w_out": w_out}


def _fwd_reference(x, c, ctx, c_ctx, norm_w, w_ada, b_ada, w_in, ret_log2_decay,
              q_norm_w, k_norm_w, w_o_ret, w_o_att, w_out):
    for layer in range(DEPTH):
        x, ctx = trunk_layer(x, ctx, c, c_ctx, norm_w[layer], w_ada[layer], b_ada[layer],
                             w_in[layer], ret_log2_decay[layer], q_norm_w[layer], k_norm_w[layer],
                             w_o_ret[layer], w_o_att[layer], w_out[layer],
                             update_ctx=(layer < DEPTH - 1))
    return x


import jax as _jax
import jax.numpy as _jnp

TWIN_FORMAT = 'train_step'
FWD_PARAMS = ['x', 'c', 'ctx', 'c_ctx', 'norm_w', 'w_ada', 'b_ada', 'w_in', 'ret_log2_decay', 'q_norm_w', 'k_norm_w', 'w_o_ret', 'w_o_att', 'w_out']
TWIN_WEIGHTS = ['c_ctx', 'norm_w', 'w_ada', 'b_ada', 'w_in', 'ret_log2_decay', 'q_norm_w', 'k_norm_w', 'w_o_ret', 'w_o_att', 'w_out']
TWIN_DIFF_INPUT = 'x'
TWIN_INPUTS = ['x', 'c', 'ctx', 'c_ctx', 'norm_w', 'w_ada', 'b_ada', 'w_in', 'ret_log2_decay', 'q_norm_w', 'k_norm_w', 'w_o_ret', 'w_o_att', 'w_out', 'loss_target', 'm_c_ctx', 'm_norm_w', 'm_w_ada', 'm_b_ada', 'm_w_in', 'm_ret_log2_decay', 'm_q_norm_w', 'm_k_norm_w', 'm_w_o_ret', 'm_w_o_att', 'm_w_out', 'v_c_ctx', 'v_norm_w', 'v_w_ada', 'v_b_ada', 'v_w_in', 'v_ret_log2_decay', 'v_q_norm_w', 'v_k_norm_w', 'v_w_o_ret', 'v_w_o_att', 'v_w_out']
TWIN_OUTPUTS = ['loss', 'grad_x', 'grad_c_ctx', 'grad_norm_w', 'grad_w_ada', 'grad_b_ada', 'grad_w_in', 'grad_ret_log2_decay', 'grad_q_norm_w', 'grad_k_norm_w', 'grad_w_o_ret', 'grad_w_o_att', 'grad_w_out', 'delta_c_ctx', 'delta_norm_w', 'delta_w_ada', 'delta_b_ada', 'delta_w_in', 'delta_ret_log2_decay', 'delta_q_norm_w', 'delta_k_norm_w', 'delta_w_o_ret', 'delta_w_o_att', 'delta_w_out', 'new_m_c_ctx', 'new_m_norm_w', 'new_m_w_ada', 'new_m_b_ada', 'new_m_w_in', 'new_m_ret_log2_decay', 'new_m_q_norm_w', 'new_m_k_norm_w', 'new_m_w_o_ret', 'new_m_w_o_att', 'new_m_w_out', 'new_v_c_ctx', 'new_v_norm_w', 'new_v_w_ada', 'new_v_b_ada', 'new_v_w_in', 'new_v_ret_log2_decay', 'new_v_q_norm_w', 'new_v_k_norm_w', 'new_v_w_o_ret', 'new_v_w_o_att', 'new_v_w_out']
TWIN_LEAF_KINDS = {'loss': 'loss', 'grad_x': 'grad_x', 'grad_c_ctx': 'grad_w', 'grad_norm_w': 'grad_w', 'grad_w_ada': 'grad_w', 'grad_b_ada': 'grad_w', 'grad_w_in': 'grad_w', 'grad_ret_log2_decay': 'grad_w', 'grad_q_norm_w': 'grad_w', 'grad_k_norm_w': 'grad_w', 'grad_w_o_ret': 'grad_w', 'grad_w_o_att': 'grad_w', 'grad_w_out': 'grad_w', 'delta_c_ctx': 'delta_w', 'delta_norm_w': 'delta_w', 'delta_w_ada': 'delta_w', 'delta_b_ada': 'delta_w', 'delta_w_in': 'delta_w', 'delta_ret_log2_decay': 'delta_w', 'delta_q_norm_w': 'delta_w', 'delta_k_norm_w': 'delta_w', 'delta_w_o_ret': 'delta_w', 'delta_w_o_att': 'delta_w', 'delta_w_out': 'delta_w', 'new_m_c_ctx': 'new_m', 'new_m_norm_w': 'new_m', 'new_m_w_ada': 'new_m', 'new_m_b_ada': 'new_m', 'new_m_w_in': 'new_m', 'new_m_ret_log2_decay': 'new_m', 'new_m_q_norm_w': 'new_m', 'new_m_k_norm_w': 'new_m', 'new_m_w_o_ret': 'new_m', 'new_m_w_o_att': 'new_m', 'new_m_w_out': 'new_m', 'new_v_c_ctx': 'new_v', 'new_v_norm_w': 'new_v', 'new_v_w_ada': 'new_v', 'new_v_b_ada': 'new_v', 'new_v_w_in': 'new_v', 'new_v_ret_log2_decay': 'new_v', 'new_v_q_norm_w': 'new_v', 'new_v_k_norm_w': 'new_v', 'new_v_w_o_ret': 'new_v', 'new_v_w_o_att': 'new_v', 'new_v_w_out': 'new_v'}


def _forward(args):
    return _fwd_reference(*[args[k] for k in FWD_PARAMS])


def _output_shape():
    out = _jax.eval_shape(lambda: _forward(_fwd_setup_inputs(0)))
    return out.shape, out.dtype

N_MICROBATCH = 1
ADAM_LR = 0.001
ADAM_B1 = 0.9
ADAM_B2 = 0.999
ADAM_EPS = 1e-08
ADAM_WD = 0.01
ADAM_STEP = 10
PER_EXAMPLE_BATCH_AXIS = {'x': 0, 'c': 0, 'ctx': 0, 'loss_target': 0}
SHARED_INPUTS = []
_WEIGHT_DTYPES = {'c_ctx': _jnp.float32, 'norm_w': _jnp.float32, 'w_ada': _jnp.float32, 'b_ada': _jnp.float32, 'w_in': _jnp.float32, 'ret_log2_decay': _jnp.float32, 'q_norm_w': _jnp.float32, 'k_norm_w': _jnp.float32, 'w_o_ret': _jnp.float32, 'w_o_att': _jnp.float32, 'w_out': _jnp.float32}
MOMENT_SCALE = {'c_ctx': 1.648034e-02, 'norm_w': 4.708044e-01, 'w_ada': 2.282142e-01, 'b_ada': 4.653795e-01, 'w_in': 2.459199e-02, 'ret_log2_decay': 8.521348e-02, 'q_norm_w': 7.933596e-03, 'k_norm_w': 7.802238e-03, 'w_o_ret': 3.154113e-02, 'w_o_att': 1.334411e-02, 'w_out': 3.425674e-02}


def _to_microbatches(a, axis):
    t = _jnp.moveaxis(a, axis, 0)
    t = t.reshape((N_MICROBATCH, t.shape[0] // N_MICROBATCH) + t.shape[1:])
    return _jnp.moveaxis(t, 1, axis + 1)


def setup_inputs(seed: int = 0) -> dict:
    inp = _fwd_setup_inputs(seed)
    key = _jax.random.fold_in(_jax.random.key(seed), 7919)
    shape, _ = _output_shape()
    out = dict(inp)
    out["loss_target"] = _jax.random.normal(_jax.random.fold_in(key, 0), shape, _jnp.float32)
    for i, name in enumerate(TWIN_WEIGHTS):
        w = inp[name].astype(_jnp.float32)
        if MOMENT_SCALE is None:
            s = _jnp.sqrt(_jnp.mean(_jnp.square(w)) + 1e-30)
        else:
            s = MOMENT_SCALE[name]
        km, kv = _jax.random.split(_jax.random.fold_in(key, i + 1))
        out[name] = w
        out["m_" + name] = s * _jax.random.normal(km, w.shape, _jnp.float32)
        out["v_" + name] = (s * s) * _jax.random.uniform(kv, w.shape, _jnp.float32, 0.5, 1.5)
    if N_MICROBATCH > 1:
        for name, axis in PER_EXAMPLE_BATCH_AXIS.items():
            out[name] = _to_microbatches(out[name], axis)
    return {'x': out['x'], 'c': out['c'], 'ctx': out['ctx'], 'c_ctx': out['c_ctx'], 'norm_w': out['norm_w'], 'w_ada': out['w_ada'], 'b_ada': out['b_ada'], 'w_in': out['w_in'], 'ret_log2_decay': out['ret_log2_decay'], 'q_norm_w': out['q_norm_w'], 'k_norm_w': out['k_norm_w'], 'w_o_ret': out['w_o_ret'], 'w_o_att': out['w_o_att'], 'w_out': out['w_out'], 'loss_target': out['loss_target'], 'm_c_ctx': out['m_c_ctx'], 'm_norm_w': out['m_norm_w'], 'm_w_ada': out['m_w_ada'], 'm_b_ada': out['m_b_ada'], 'm_w_in': out['m_w_in'], 'm_ret_log2_decay': out['m_ret_log2_decay'], 'm_q_norm_w': out['m_q_norm_w'], 'm_k_norm_w': out['m_k_norm_w'], 'm_w_o_ret': out['m_w_o_ret'], 'm_w_o_att': out['m_w_o_att'], 'm_w_out': out['m_w_out'], 'v_c_ctx': out['v_c_ctx'], 'v_norm_w': out['v_norm_w'], 'v_w_ada': out['v_w_ada'], 'v_b_ada': out['v_b_ada'], 'v_w_in': out['v_w_in'], 'v_ret_log2_decay': out['v_ret_log2_decay'], 'v_q_norm_w': out['v_q_norm_w'], 'v_k_norm_w': out['v_k_norm_w'], 'v_w_o_ret': out['v_w_o_ret'], 'v_w_o_att': out['v_w_o_att'], 'v_w_out': out['v_w_out']}


def _loss(weights, diff, rest, loss_target):
    with _jax.named_scope("forward"):
        args = {**rest, TWIN_DIFF_INPUT: diff, **{k: w.astype(_WEIGHT_DTYPES[k]) for k, w in weights.items()}}
        y = _forward(args)
    with _jax.named_scope("loss_head"):
        err = _jnp.square(y.astype(_jnp.float32) - loss_target)
        return 0.5 * _jnp.sum(_jnp.mean(err, axis=-1)) if err.ndim else 0.5 * err


def _adamw(w, g, m, v):
    m = ADAM_B1 * m + (1.0 - ADAM_B1) * g
    v = ADAM_B2 * v + (1.0 - ADAM_B2) * _jnp.square(g)
    m_hat = m / (1.0 - ADAM_B1 ** ADAM_STEP)
    v_hat = v / (1.0 - ADAM_B2 ** ADAM_STEP)
    delta = -ADAM_LR * (m_hat / (_jnp.sqrt(v_hat) + ADAM_EPS) + ADAM_WD * w)
    return delta, m, v


def reference(x, c, ctx, c_ctx, norm_w, w_ada, b_ada, w_in, ret_log2_decay, q_norm_w, k_norm_w, w_o_ret, w_o_att, w_out, loss_target, m_c_ctx, m_norm_w, m_w_ada, m_b_ada, m_w_in, m_ret_log2_decay, m_q_norm_w, m_k_norm_w, m_w_o_ret, m_w_o_att, m_w_out, v_c_ctx, v_norm_w, v_w_ada, v_b_ada, v_w_in, v_ret_log2_decay, v_q_norm_w, v_k_norm_w, v_w_o_ret, v_w_o_att, v_w_out):
    given = dict(x=x, c=c, ctx=ctx, c_ctx=c_ctx, norm_w=norm_w, w_ada=w_ada, b_ada=b_ada, w_in=w_in, ret_log2_decay=ret_log2_decay, q_norm_w=q_norm_w, k_norm_w=k_norm_w, w_o_ret=w_o_ret, w_o_att=w_o_att, w_out=w_out, loss_target=loss_target, m_c_ctx=m_c_ctx, m_norm_w=m_norm_w, m_w_ada=m_w_ada, m_b_ada=m_b_ada, m_w_in=m_w_in, m_ret_log2_decay=m_ret_log2_decay, m_q_norm_w=m_q_norm_w, m_k_norm_w=m_k_norm_w, m_w_o_ret=m_w_o_ret, m_w_o_att=m_w_o_att, m_w_out=m_w_out, v_c_ctx=v_c_ctx, v_norm_w=v_norm_w, v_w_ada=v_w_ada, v_b_ada=v_b_ada, v_w_in=v_w_in, v_ret_log2_decay=v_ret_log2_decay, v_q_norm_w=v_q_norm_w, v_k_norm_w=v_k_norm_w, v_w_o_ret=v_w_o_ret, v_w_o_att=v_w_o_att, v_w_out=v_w_out)
    weights = {n: given[n] for n in TWIN_WEIGHTS}
    shared = {n: given[n] for n in SHARED_INPUTS}
    per_example = {n: given[n] for n in ['x', 'c', 'ctx']}
    grad_fn = _jax.value_and_grad(_loss, argnums=(0, 1))

    def one_microbatch(ex, loss_target):
        ex = dict(ex)
        diff = ex.pop(TWIN_DIFF_INPUT)
        return grad_fn(weights, diff, {**shared, **ex}, loss_target)

    if N_MICROBATCH == 1:
        loss, (grad_w, grad_x) = one_microbatch(per_example, given["loss_target"])
    else:
        def body(carry, xs):
            loss_sum, grad_sum = carry
            l_k, (gw_k, gx_k) = one_microbatch(xs[0], xs[1])
            with _jax.named_scope("update"):
                return (loss_sum + l_k, _jax.tree.map(_jnp.add, grad_sum, gw_k)), gx_k

        init = (_jnp.zeros((), _jnp.float32), _jax.tree.map(_jnp.zeros_like, weights))
        (loss, grad_w), grad_x = _jax.lax.scan(body, init, (per_example, given["loss_target"]))
    with _jax.named_scope("update"):
        delta_w, new_m, new_v = {}, {}, {}
        for n in TWIN_WEIGHTS:
            delta_w[n], new_m[n], new_v[n] = _adamw(weights[n], grad_w[n], given["m_" + n], given["v_" + n])
    return (loss, grad_x, *[grad_w[n] for n in TWIN_WEIGHTS], *[delta_w[n] for n in TWIN_WEIGHTS],
            *[new_m[n] for n in TWIN_WEIGHTS], *[new_v[n] for n in TWIN_WEIGHTS])
```

```python
import functools

import jax
import jax.numpy as jnp
from jax import lax
from jax.experimental import pallas as pl
from jax.experimental.pallas import tpu as pltpu

F32 = jnp.float32
BF = jnp.bfloat16
SDS = jax.ShapeDtypeStruct
MESH = pl.DeviceIdType.MESH
ANY = pl.BlockSpec(memory_space=pl.ANY)
SMEM = pl.BlockSpec(memory_space=pltpu.SMEM)

D_MODEL = 1024
GRID_W = 64
RET_HEADS = 4
RET_DK = 256
RET_DV = 512
RET_CHUNK = 128
ATT_HEADS = 8
ATT_KV_HEADS = 2
ATT_REP = ATT_HEADS // ATT_KV_HEADS
ATT_HEAD_DIM = 128
ROPE_THETA = 10000.0
NORM_EPS = 1e-6
IN_COLS = 10752
KV_COLS = 3584
C_RK, C_RV, C_AK, C_AV, C_RQ, C_RG, C_AQ, C_AG, C_MR, C_MA = 0, 1024, 3072, 3328, 3584, 4608, 6656, 7680, 8704, 9728
N_SHARD = 4
ADA_W = 3 * D_MODEL // N_SHARD
IN_W = IN_COLS // N_SHARD
IN_BLK = IN_W // 3
N_IN_BLK = IN_COLS // IN_BLK
TM = 512
ADAM_LR, ADAM_B1, ADAM_B2, ADAM_EPS, ADAM_WD, ADAM_STEP = 0.001, 0.9, 0.999, 1e-08, 0.01, 10
MIB = 1024 * 1024


def _cp(sem=None, vmem_mb=None, **kw):
    if sem is not None:
        kw["dimension_semantics"] = sem
    if vmem_mb is not None:
        kw["vmem_limit_bytes"] = vmem_mb * MIB
    return pltpu.CompilerParams(**kw)


def _dot(a, b, ca=1, cb=0):
    return lax.dot_general(a.astype(BF), b.astype(BF), (((ca,), (cb,)), ((), ())), preferred_element_type=F32)


def _sigmoid(x):
    return 1.0 / (1.0 + jnp.exp(-x))


def _sum_all(x):
    return jnp.sum(jnp.sum(x, axis=1, keepdims=True), axis=0, keepdims=True)


def _swap_pairs(x):
    ax = x.ndim - 1
    lane = lax.broadcasted_iota(jnp.int32, x.shape, ax)
    nxt = pltpu.roll(x, x.shape[ax] - 1, ax)
    prv = pltpu.roll(x, 1, ax)
    return jnp.where(lane % 2 == 0, nxt, prv)


def _rms(x):
    return lax.rsqrt(jnp.mean(x * x, axis=-1, keepdims=True) + NORM_EPS)


def _rms_bwd(dxh, xh, r):
    return r * (dxh - xh * jnp.mean(dxh * xh, axis=-1, keepdims=True))


def _adaln_fwd(cvec8, w_ada_g, b_ada):
    def body(c_ref, w_ref, b_ref, o_ref):
        cv = c_ref[...]
        sc = (cv * _sigmoid(cv)).astype(BF)
        for s in range(N_SHARD):
            cols = slice(s * ADA_W, (s + 1) * ADA_W)
            o_ref[:, cols] = jnp.dot(sc, w_ref[s], preferred_element_type=F32) + b_ref[:, cols]

    return pl.pallas_call(body, out_shape=SDS((8, 3 * D_MODEL), F32), name="adaln_fwd",
                          compiler_params=_cp(vmem_mb=32))(cvec8, w_ada_g, b_ada)


def _adaln_bwd(cvec8, dmod8, w_ada_g):
    def body(c_ref, d_ref, w_ref, gw_ref, gb_ref, dc_ref):
        cv = c_ref[...]
        sg = _sigmoid(cv)
        sc = cv * sg
        dm = d_ref[...]
        gb_ref[...] = jnp.sum(dm, axis=0, keepdims=True)
        dsc = jnp.zeros((8, D_MODEL), F32)
        for s in range(N_SHARD):
            cols = slice(s * ADA_W, (s + 1) * ADA_W)
            gw_ref[s] = _dot(sc, dm[:, cols], 0, 0)
            dsc = dsc + _dot(dm[:, cols], w_ref[s], 1, 1)
        dc_ref[...] = dsc * (sg * (1.0 + cv * (1.0 - sg)))

    return pl.pallas_call(
        body, name="adaln_bwd",
        out_shape=(SDS((N_SHARD, D_MODEL, ADA_W), F32), SDS((1, 3 * D_MODEL), F32), SDS((8, D_MODEL), F32)),
        compiler_params=_cp(vmem_mb=48))(cvec8, dmod8, w_ada_g)


def _normproj(x_all, norm_w, scale3, shift3, w_in_g, tiles_per_sample, n_samp):
    rows = x_all.shape[0]

    def samp(i):
        return jnp.minimum(i // tiles_per_sample, n_samp)

    def body(x_ref, nw_ref, sc_ref, sh_ref, w_ref, px_ref, hxt_ref, hx_s):
        @pl.when(pl.program_id(1) == 0)
        def _():
            x = x_ref[...]
            h = x * _rms(x) * nw_ref[...] * (1.0 + sc_ref[...]) + sh_ref[...]
            hx_s[...] = h.astype(BF)
            hxt_ref[...] = h.T.astype(BF)

        px_ref[...] = jnp.dot(hx_s[...], w_ref[...], preferred_element_type=F32)

    return pl.pallas_call(
        body, name="normproj", grid=(rows // TM, N_IN_BLK),
        in_specs=[pl.BlockSpec((TM, D_MODEL), lambda i, j: (i, 0)),
                  pl.BlockSpec((1, D_MODEL), lambda i, j: (0, 0)),
                  pl.BlockSpec((None, 1, D_MODEL), lambda i, j: (samp(i), 0, 0)),
                  pl.BlockSpec((None, 1, D_MODEL), lambda i, j: (samp(i), 0, 0)),
                  pl.BlockSpec((None, D_MODEL, IN_BLK), lambda i, j: (j // 3, 0, j % 3))],
        out_specs=(pl.BlockSpec((TM, IN_BLK), lambda i, j: (i, j)),
                   pl.BlockSpec((D_MODEL, TM), lambda i, j: (0, i))),
        out_shape=(SDS((rows, IN_COLS), F32), SDS((D_MODEL, rows), BF)),
        scratch_shapes=[pltpu.VMEM((TM, D_MODEL), BF)],
        compiler_params=_cp(("parallel", "arbitrary"), 40))(x_all, norm_w, scale3, shift3, w_in_g)


def _norm_bwd(x_all, dhx, gx_res, norm_w, scale3, shift3, tiles_per_sample, n_samp):
    rows = x_all.shape[0]
    n_lat = tiles_per_sample * n_samp
    del shift3

    def samp(i):
        return jnp.minimum(i // tiles_per_sample, n_samp)

    def lat(i):
        return jnp.minimum(i, n_lat - 1)

    def body(x_ref, dh_ref, gr_ref, nw_ref, sc_ref, gx_ref, dsh_ref, dsc_ref, dnw_ref):
        i = pl.program_id(0)
        x = x_ref[...]
        r = _rms(x)
        xh = x * r
        nw = nw_ref[...]
        dh = dh_ref[...]
        first = jnp.logical_or(i % tiles_per_sample == 0, i >= n_lat)

        @pl.when(first)
        def _():
            dsh_ref[...] = jnp.zeros_like(dsh_ref)
            dsc_ref[...] = jnp.zeros_like(dsc_ref)

        @pl.when(i == 0)
        def _():
            dnw_ref[...] = jnp.zeros_like(dnw_ref)

        dsh_ref[...] += jnp.sum(dh, axis=0, keepdims=True)
        dsc_ref[...] += jnp.sum(dh * (xh * nw), axis=0, keepdims=True)
        du = dh * (1.0 + sc_ref[...])
        dnw_ref[...] += jnp.sum(du * xh, axis=0, keepdims=True)

        @pl.when(i < n_lat)
        def _():
            gx_ref[...] = gr_ref[...] + _rms_bwd(du * nw, xh, r)

    return pl.pallas_call(
        body, name="norm_bwd", grid=(rows // TM,),
        in_specs=[pl.BlockSpec((TM, D_MODEL), lambda i: (i, 0)),
                  pl.BlockSpec((TM, D_MODEL), lambda i: (i, 0)),
                  pl.BlockSpec((TM, D_MODEL), lambda i: (lat(i), 0)),
                  pl.BlockSpec((1, D_MODEL), lambda i: (0, 0)),
                  pl.BlockSpec((None, 1, D_MODEL), lambda i: (samp(i), 0, 0))],
        out_specs=(pl.BlockSpec((TM, D_MODEL), lambda i: (lat(i), 0)),
                   pl.BlockSpec((None, 1, D_MODEL), lambda i: (samp(i), 0, 0)),
                   pl.BlockSpec((None, 1, D_MODEL), lambda i: (samp(i), 0, 0)),
                   pl.BlockSpec((1, D_MODEL), lambda i: (0, 0))),
        out_shape=(SDS((n_lat * TM, D_MODEL), F32), SDS((n_samp + 1, 1, D_MODEL), F32),
                   SDS((n_samp + 1, 1, D_MODEL), F32), SDS((1, D_MODEL), F32)),
        compiler_params=_cp(("arbitrary",), 40))(x_all, dhx, gx_res, norm_w, scale3)


def _gw_in(hxt, dpx_all):
    rows = dpx_all.shape[0]

    def body(h_ref, d_ref, o_ref):
        @pl.when(pl.program_id(1) == 0)
        def _():
            o_ref[...] = jnp.zeros_like(o_ref)

        o_ref[...] += jnp.dot(h_ref[...], d_ref[...], preferred_element_type=F32)

    return pl.pallas_call(
        body, name="gw_in", grid=(N_IN_BLK, rows // TM),
        in_specs=[pl.BlockSpec((D_MODEL, TM), lambda j, i: (0, i)),
                  pl.BlockSpec((TM, IN_BLK), lambda j, i: (i, j))],
        out_specs=pl.BlockSpec((None, D_MODEL, IN_BLK), lambda j, i: (j // 3, 0, j % 3)),
        out_shape=SDS((N_SHARD, D_MODEL, IN_W), F32),
        compiler_params=_cp(("parallel", "arbitrary"), 40))(hxt, dpx_all)


def _dhx(dpx_all, w_in_g):
    rows = dpx_all.shape[0]

    def body(d_ref, w_ref, o_ref):
        @pl.when(pl.program_id(1) == 0)
        def _():
            o_ref[...] = jnp.zeros_like(o_ref)

        o_ref[...] += lax.dot_general(d_ref[...], w_ref[...], (((1,), (1,)), ((), ())), preferred_element_type=F32)

    return pl.pallas_call(
        body, name="dhx", grid=(rows // TM, N_IN_BLK),
        in_specs=[pl.BlockSpec((TM, IN_BLK), lambda i, j: (i, j)),
                  pl.BlockSpec((None, D_MODEL, IN_BLK), lambda i, j: (j // 3, 0, j % 3))],
        out_specs=pl.BlockSpec((TM, D_MODEL), lambda i, j: (i, 0)),
        out_shape=SDS((rows, D_MODEL), F32),
        compiler_params=_cp(("parallel", "arbitrary"), 40))(dpx_all, w_in_g)


def _decays(lgv, d):
    c = RET_CHUNK
    ii = lax.broadcasted_iota(jnp.int32, (c, 1), 0).astype(F32)
    jj = lax.broadcasted_iota(jnp.int32, (1, c), 1).astype(F32)
    a_i = jnp.where(d == 0, ii, c - 1.0 - ii)
    a_j = jnp.where(d == 0, jj, c - 1.0 - jj)
    rel = a_i - a_j
    mask = jnp.where(rel >= 0, jnp.exp(lgv * jnp.maximum(rel, 0.0)), 0.0)
    qd = jnp.exp(lgv * (a_i + 1.0))
    kd = jnp.exp(lgv * (c - 1.0 - a_i))
    gc = jnp.exp(jnp.full((1, 1), lgv * c, F32))
    return a_i, rel, mask, qd, kd, gc


def _ctx_state_fwd(px, lg, n_samp, t_lat, lc):
    rb = t_lat // lc

    def body(lg_ref, k_ref, v_ref, o_ref):
        h = pl.program_id(1)
        k = k_ref[...] * (RET_DK ** -0.5)
        v = v_ref[...]
        pos = lax.broadcasted_iota(jnp.int32, (lc, 1), 0).astype(F32)
        o_ref[0] = _dot(k * jnp.exp(lg_ref[0, h] * (lc - 1.0 - pos)), v, 0, 0)
        o_ref[1] = _dot(k * jnp.exp(lg_ref[1, h] * pos), v, 0, 0)

    return pl.pallas_call(
        body, name="ctx_state_fwd", grid=(n_samp, RET_HEADS),
        in_specs=[SMEM,
                  pl.BlockSpec((lc, RET_DK), lambda b, h: (rb + b, C_RK // RET_DK + h)),
                  pl.BlockSpec((lc, RET_DV), lambda b, h: (rb + b, C_RV // RET_DV + h))],
        out_specs=pl.BlockSpec((None, 2, None, RET_DK, RET_DV), lambda b, h: (b, 0, h, 0, 0)),
        out_shape=SDS((n_samp, 2, RET_HEADS, RET_DK, RET_DV), F32),
        compiler_params=_cp(("parallel", "parallel")))(lg, px, px)


def _ctx_state_bwd(px, dstates, lg, n_samp, t_lat, lc):
    rb = t_lat // lc

    def body(lg_ref, k_ref, v_ref, ds_ref, dk_ref, dv_ref, dlg_ref):
        h = pl.program_id(1)
        k = k_ref[...] * (RET_DK ** -0.5)
        v = v_ref[...]
        pos = lax.broadcasted_iota(jnp.int32, (lc, 1), 0).astype(F32)
        e_f = lc - 1.0 - pos
        kw_f = k * jnp.exp(lg_ref[0, h] * e_f)
        kw_b = k * jnp.exp(lg_ref[1, h] * pos)
        y_f = _dot(v, ds_ref[0], 1, 1)
        y_b = _dot(v, ds_ref[1], 1, 1)
        dk = y_f * jnp.exp(lg_ref[0, h] * e_f) + y_b * jnp.exp(lg_ref[1, h] * pos)
        dk_ref[...] = (dk * (RET_DK ** -0.5)).astype(BF)
        dv_ref[...] = (_dot(kw_f, ds_ref[0]) + _dot(kw_b, ds_ref[1])).astype(BF)
        t_f = _sum_all(e_f * kw_f * y_f)
        t_b = _sum_all(pos * kw_b * y_b)
        sub = lax.broadcasted_iota(jnp.int32, (8, 128), 0)
        dlg_ref[...] = jnp.where(sub == 0, t_f, jnp.where(sub == 1, t_b, 0.0))

    return pl.pallas_call(
        body, name="ctx_state_bwd", grid=(n_samp, RET_HEADS),
        in_specs=[SMEM,
                  pl.BlockSpec((lc, RET_DK), lambda b, h: (rb + b, C_RK // RET_DK + h)),
                  pl.BlockSpec((lc, RET_DV), lambda b, h: (rb + b, C_RV // RET_DV + h)),
                  pl.BlockSpec((None, 2, None, RET_DK, RET_DV), lambda b, h: (b, 0, h, 0, 0))],
        out_specs=(pl.BlockSpec((lc, RET_DK), lambda b, h: (b, h)),
                   pl.BlockSpec((lc, RET_DV), lambda b, h: (b, h)),
                   pl.BlockSpec((None, None, 8, 128), lambda b, h: (b, h, 0, 0))),
        out_shape=(SDS((n_samp * lc, RET_HEADS * RET_DK), BF), SDS((n_samp * lc, RET_HEADS * RET_DV), BF),
                   SDS((n_samp, RET_HEADS, 8, 128), F32)),
        compiler_params=_cp(("parallel", "parallel")))(lg, px, px, dstates)


def _ret_fwd(px, states0, lg, n_samp, seq):
    c = RET_CHUNK
    nc = seq // c
    t_lat = n_samp * seq

    def chunk(n, d):
        return n + d * (nc - 1 - 2 * n)

    def body(lg_ref, q_ref, k_ref, v_ref, s0_ref, o_ref, st_ref, s_s):
        h, d, n = pl.program_id(1), pl.program_id(2), pl.program_id(3)

        @pl.when(n == 0)
        def _():
            s_s[...] = s0_ref[...]

        _, _, mask, qd, kd, gc = _decays(lg_ref[d, h], d)
        q = q_ref[...]
        k = k_ref[...] * (RET_DK ** -0.5)
        v = v_ref[...]
        s = s_s[...]
        st_ref[...] = s.astype(BF)
        sc = _dot(q, k, 1, 1) * mask
        o_ref[...] = _dot(sc, v) + _dot(q * qd, s)
        s_s[...] = s * gc + _dot(k * kd, v, 0, 0)

    return pl.pallas_call(
        body, name="ret_fwd", grid=(n_samp, RET_HEADS, 2, nc),
        in_specs=[SMEM,
                  pl.BlockSpec((c, RET_DK), lambda b, h, d, n: (b * nc + chunk(n, d), C_RQ // RET_DK + h)),
                  pl.BlockSpec((c, RET_DK), lambda b, h, d, n: (b * nc + chunk(n, d), C_RK // RET_DK + h)),
                  pl.BlockSpec((c, RET_DV), lambda b, h, d, n: (b * nc + chunk(n, d), C_RV // RET_DV + h)),
                  pl.BlockSpec((None, None, None, RET_DK, RET_DV), lambda b, h, d, n: (b, d, h, 0, 0))],
        out_specs=(pl.BlockSpec((None, c, RET_DV), lambda b, h, d, n: (d, b * nc + chunk(n, d), h)),
                   pl.BlockSpec((None, None, None, None, RET_DK, RET_DV), lambda b, h, d, n: (b, h, d, n, 0, 0))),
        out_shape=(SDS((2, t_lat, RET_HEADS * RET_DV), F32),
                   SDS((n_samp, RET_HEADS, 2, nc, RET_DK, RET_DV), BF)),
        scratch_shapes=[pltpu.VMEM((RET_DK, RET_DV), F32)],
        compiler_params=_cp(("parallel", "parallel", "parallel", "arbitrary")))(lg, px, px, px, states0)


def _ret_bwd(px, do, saved, lg, n_samp, seq):
    c = RET_CHUNK
    nc = seq // c
    t_lat = n_samp * seq

    def chunk(n, d):
        m = nc - 1 - n
        return m + d * (nc - 1 - 2 * m)

    def body(lg_ref, q_ref, k_ref, v_ref, do_ref, st_ref, dq_ref, dk_ref, dv_ref, ds0_ref, dlg_ref, ds_s, acc_s):
        h, d, n = pl.program_id(1), pl.program_id(2), pl.program_id(3)

        @pl.when(n == 0)
        def _():
            ds_s[...] = jnp.zeros_like(ds_s)
            acc_s[...] = jnp.zeros_like(acc_s)

        a_i, rel, mask, qd, kd, gc = _decays(lg_ref[d, h], d)
        q = q_ref[...]
        k = k_ref[...] * (RET_DK ** -0.5)
        qb, kb, vb, dob = q.astype(BF), k.astype(BF), v_ref[...].astype(BF), do_ref[...].astype(BF)
        sb = st_ref[...]
        ds = ds_s[...]
        dsb = ds.astype(BF)
        raw = _dot(qb, kb, 1, 1)
        sc = raw * mask
        dsc = _dot(dob, vb, 1, 1) * mask
        dscb = dsc.astype(BF)
        x = _dot(dob, sb, 1, 1)
        y = _dot(vb, dsb, 1, 1)
        qq = q * qd
        kk = k * kd
        dq_ref[...] = _dot(dscb, kb) + x * qd
        dk_ref[...] = _dot(dscb, qb, 0, 0) + y * kd
        dv_ref[...] = _dot(sc, dob, 0, 0) + _dot(kk, dsb)
        ds_new = ds * gc + _dot(qq, dob, 0, 0)
        t = (_sum_all(dsc * raw * rel) + _sum_all((a_i + 1.0) * qq * x)
             + _sum_all((c - 1.0 - a_i) * kk * y) + c * gc * _sum_all(ds * sb.astype(F32)))
        acc_s[...] += t
        ds_s[...] = ds_new

        @pl.when(n == nc - 1)
        def _():
            ds0_ref[...] = ds_new
            dlg_ref[...] = acc_s[...]

    return pl.pallas_call(
        body, name="ret_bwd", grid=(n_samp, RET_HEADS, 2, nc),
        in_specs=[SMEM,
                  pl.BlockSpec((c, RET_DK), lambda b, h, d, n: (b * nc + chunk(n, d), C_RQ // RET_DK + h)),
                  pl.BlockSpec((c, RET_DK), lambda b, h, d, n: (b * nc + chunk(n, d), C_RK // RET_DK + h)),
                  pl.BlockSpec((c, RET_DV), lambda b, h, d, n: (b * nc + chunk(n, d), C_RV // RET_DV + h)),
                  pl.BlockSpec((c, RET_DV), lambda b, h, d, n: (b * nc + chunk(n, d), h)),
                  pl.BlockSpec((None, None, None, None, RET_DK, RET_DV),
                               lambda b, h, d, n: (b, h, d, nc - 1 - n, 0, 0))],
        out_specs=(pl.BlockSpec((None, c, RET_DK), lambda b, h, d, n: (d, b * nc + chunk(n, d), h)),
                   pl.BlockSpec((None, c, RET_DK), lambda b, h, d, n: (d, b * nc + chunk(n, d), h)),
                   pl.BlockSpec((None, c, RET_DV), lambda b, h, d, n: (d, b * nc + chunk(n, d), h)),
                   pl.BlockSpec((None, None, None, RET_DK, RET_DV), lambda b, h, d, n: (b, d, h, 0, 0)),
                   pl.BlockSpec((None, None, None, 8, 128), lambda b, h, d, n: (b, h, d, 0, 0))),
        out_shape=(SDS((2, t_lat, RET_HEADS * RET_DK), F32), SDS((2, t_lat, RET_HEADS * RET_DK), F32),
                   SDS((2, t_lat, RET_HEADS * RET_DV), F32),
                   SDS((n_samp, 2, RET_HEADS, RET_DK, RET_DV), F32), SDS((n_samp, RET_HEADS, 2, 8, 128), F32)),
        scratch_shapes=[pltpu.VMEM((RET_DK, RET_DV), F32), pltpu.VMEM((8, 128), F32)],
        compiler_params=_cp(("parallel", "parallel", "parallel", "arbitrary")))(lg, px, px, px, do, saved)


def _ret_combine(dq2, dk2, dv2):
    t_lat = dq2.shape[1]
    tm = 256

    def body(q_ref, k_ref, v_ref, oq_ref, ok_ref, ov_ref):
        oq_ref[...] = (q_ref[0] + q_ref[1]).astype(BF)
        ok_ref[...] = ((k_ref[0] + k_ref[1]) * (RET_DK ** -0.5)).astype(BF)
        ov_ref[...] = (v_ref[0] + v_ref[1]).astype(BF)

    wq, wv = RET_HEADS * RET_DK, RET_HEADS * RET_DV
    return pl.pallas_call(
        body, name="ret_combine", grid=(t_lat // tm,),
        in_specs=[pl.BlockSpec((2, tm, wq), lambda i: (0, i, 0)),
                  pl.BlockSpec((2, tm, wq), lambda i: (0, i, 0)),
                  pl.BlockSpec((2, tm, wv), lambda i: (0, i, 0))],
        out_specs=(pl.BlockSpec((tm, wq), lambda i: (i, 0)),
                   pl.BlockSpec((tm, wq), lambda i: (i, 0)),
                   pl.BlockSpec((tm, wv), lambda i: (i, 0))),
        out_shape=(SDS((t_lat, wq), BF), SDS((t_lat, wq), BF), SDS((t_lat, wv), BF)),
        compiler_params=_cp(("parallel",), 40))(dq2, dk2, dv2)


def _retnorm_fwd(o2, px):
    t_lat = o2.shape[1]

    def body(o_ref, g_ref, y_ref):
        o = o_ref[0] + o_ref[1]
        g = g_ref[...]
        y_ref[...] = (o * _rms(o) * (g * _sigmoid(g))).astype(BF)

    return pl.pallas_call(
        body, name="retnorm_fwd", grid=(t_lat // TM, RET_HEADS),
        in_specs=[pl.BlockSpec((2, TM, RET_DV), lambda i, h: (0, i, h)),
                  pl.BlockSpec((TM, RET_DV), lambda i, h: (i, C_RG // RET_DV + h))],
        out_specs=pl.BlockSpec((TM, RET_DV), lambda i, h: (i, h)),
        out_shape=SDS((t_lat, RET_HEADS * RET_DV), BF),
        compiler_params=_cp(("parallel", "parallel")))(o2, px)


def _retnorm_bwd(dy, o2, px):
    t_lat = o2.shape[1]

    def body(dy_ref, o_ref, g_ref, do_ref, dg_ref):
        o = o_ref[0] + o_ref[1]
        r = _rms(o)
        on = o * r
        g = g_ref[...]
        sg = _sigmoid(g)
        dy_ = dy_ref[...]
        dg_ref[...] = (dy_ * on * (sg * (1.0 + g * (1.0 - sg)))).astype(BF)
        do_ref[...] = _rms_bwd(dy_ * (g * sg), on, r)

    return pl.pallas_call(
        body, name="retnorm_bwd", grid=(t_lat // TM, RET_HEADS),
        in_specs=[pl.BlockSpec((TM, RET_DV), lambda i, h: (i, h)),
                  pl.BlockSpec((2, TM, RET_DV), lambda i, h: (0, i, h)),
                  pl.BlockSpec((TM, RET_DV), lambda i, h: (i, C_RG // RET_DV + h))],
        out_specs=(pl.BlockSpec((TM, RET_DV), lambda i, h: (i, h)),
                   pl.BlockSpec((TM, RET_DV), lambda i, h: (i, h))),
        out_shape=(SDS((t_lat, RET_HEADS * RET_DV), F32), SDS((t_lat, RET_HEADS * RET_DV), BF)),
        compiler_params=_cp(("parallel", "parallel")))(dy, o2, px)


def _norm_rope(x, w, cos, sin):
    xn = x * _rms(x) * w
    return xn * cos + _swap_pairs(xn) * sin


def _norm_rope_bwd(dy, x, w, cos, sin):
    dxn = dy * cos + _swap_pairs(dy * sin)
    r = _rms(x)
    xh = x * r
    return _rms_bwd(dxn * w, xh, r), jnp.sum(dxn * xh, axis=0, keepdims=True)


def _att_prep_q(px, cos_all, sin_all, qnw, t_lat):
    hd = ATT_HEAD_DIM
    wblk = ATT_REP * hd

    def body(x_ref, cos_ref, sin_ref, w_ref, o_ref):
        for r in range(ATT_REP):
            cols = slice(r * hd, (r + 1) * hd)
            o_ref[:, cols] = _norm_rope(x_ref[:, cols], w_ref[...], cos_ref[...], sin_ref[...]).astype(BF)

    return pl.pallas_call(
        body, name="att_prep_q", grid=(t_lat // TM, ATT_KV_HEADS),
        in_specs=[pl.BlockSpec((TM, wblk), lambda i, g: (i, C_AQ // wblk + g)),
                  pl.BlockSpec((TM, hd), lambda i, g: (i, 0)),
                  pl.BlockSpec((TM, hd), lambda i, g: (i, 0)),
                  pl.BlockSpec((1, hd), lambda i, g: (0, 0))],
        out_specs=pl.BlockSpec((TM, wblk), lambda i, g: (i, g)),
        out_shape=SDS((t_lat, ATT_HEADS * hd), BF),
        compiler_params=_cp(("parallel", "parallel")))(px, cos_all, sin_all, qnw)


def _att_prep_kv(px, cos_all, sin_all, knw):
    rows = px.shape[0]
    hd = ATT_HEAD_DIM
    kvw = ATT_KV_HEADS * hd

    def body(x_ref, cos_ref, sin_ref, w_ref, k_ref, v_ref):
        for g in range(ATT_KV_HEADS):
            cols = slice(g * hd, (g + 1) * hd)
            k_ref[:, cols] = _norm_rope(x_ref[:, cols], w_ref[...], cos_ref[...], sin_ref[...]).astype(BF)
        v_ref[...] = x_ref[:, kvw:].astype(BF)

    return pl.pallas_call(
        body, name="att_prep_kv", grid=(rows // TM,),
        in_specs=[pl.BlockSpec((TM, 2 * kvw), lambda i: (i, C_AK // (2 * kvw))),
                  pl.BlockSpec((TM, hd), lambda i: (i, 0)),
                  pl.BlockSpec((TM, hd), lambda i: (i, 0)),
                  pl.BlockSpec((1, hd), lambda i: (0, 0))],
        out_specs=(pl.BlockSpec((TM, kvw), lambda i: (i, 0)), pl.BlockSpec((TM, kvw), lambda i: (i, 0))),
        out_shape=(SDS((rows, kvw), BF), SDS((rows, kvw), BF)),
        compiler_params=_cp(("parallel",)))(px, cos_all, sin_all, knw)


def _att_kv_bwd(dk_tok, dv_tok, px, cos_all, sin_all, knw):
    rows = px.shape[0]
    hd = ATT_HEAD_DIM
    kvw = ATT_KV_HEADS * hd

    def body(dk_ref, dv_ref, x_ref, cos_ref, sin_ref, w_ref, o_ref, gw_ref):
        @pl.when(pl.program_id(0) == 0)
        def _():
            gw_ref[...] = jnp.zeros_like(gw_ref)

        for g in range(ATT_KV_HEADS):
            cols = slice(g * hd, (g + 1) * hd)
            dx, gw = _norm_rope_bwd(dk_ref[:, cols], x_ref[:, cols], w_ref[...], cos_ref[...], sin_ref[...])
            o_ref[:, cols] = dx.astype(BF)
            gw_ref[...] += gw
        o_ref[:, kvw:] = dv_ref[...].astype(BF)

    return pl.pallas_call(
        body, name="att_kv_bwd", grid=(rows // TM,),
        in_specs=[pl.BlockSpec((TM, kvw), lambda i: (i, 0)),
                  pl.BlockSpec((TM, kvw), lambda i: (i, 0)),
                  pl.BlockSpec((TM, 2 * kvw), lambda i: (i, C_AK // (2 * kvw))),
                  pl.BlockSpec((TM, hd), lambda i: (i, 0)),
                  pl.BlockSpec((TM, hd), lambda i: (i, 0)),
                  pl.BlockSpec((1, hd), lambda i: (0, 0))],
        out_specs=(pl.BlockSpec((TM, 2 * kvw), lambda i: (i, 0)), pl.BlockSpec((1, hd), lambda i: (0, 0))),
        out_shape=(SDS((rows, 2 * kvw), BF), SDS((1, hd), F32)),
        compiler_params=_cp(("arbitrary",)))(dk_tok, dv_tok, px, cos_all, sin_all, knw)


def _stack_heads(ref_or_val):
    hd = ATT_HEAD_DIM
    return jnp.concatenate([ref_or_val[:, r * hd:(r + 1) * hd] for r in range(ATT_REP)], axis=0)


def _att_scores(q4, kl, kc):
    scale = ATT_HEAD_DIM ** -0.5
    sl = _dot(q4, kl, 1, 1) * scale
    sc = _dot(q4, kc, 1, 1) * scale
    m = jnp.maximum(jnp.max(sl, axis=-1, keepdims=True), jnp.max(sc, axis=-1, keepdims=True))
    el = jnp.exp(sl - m)
    ec = jnp.exp(sc - m)
    denom = jnp.sum(el, axis=-1, keepdims=True) + jnp.sum(ec, axis=-1, keepdims=True)
    return el, ec, denom


def _att_fwd(qn, kn, vn, px, n_samp, seq, lc):
    hd = ATT_HEAD_DIM
    tq = 128
    nq = seq // tq
    wblk = ATT_REP * hd
    cb = n_samp * seq // lc
    t_lat = n_samp * seq

    def body(q_ref, kl_ref, kc_ref, vl_ref, vc_ref, g_ref, y_ref, o_ref):
        q4 = _stack_heads(q_ref)
        el, ec, denom = _att_scores(q4, kl_ref[...], kc_ref[...])
        o4 = (_dot(el, vl_ref[...]) + _dot(ec, vc_ref[...])) / denom
        for r in range(ATT_REP):
            cols = slice(r * hd, (r + 1) * hd)
            o = o4[r * tq:(r + 1) * tq]
            g = g_ref[:, cols]
            o_ref[:, cols] = o
            y_ref[:, cols] = (o * (g * _sigmoid(g))).astype(BF)

    return pl.pallas_call(
        body, name="att_fwd", grid=(n_samp, ATT_KV_HEADS, nq),
        in_specs=[pl.BlockSpec((tq, wblk), lambda b, g, i: (b * nq + i, g)),
                  pl.BlockSpec((seq, hd), lambda b, g, i: (b, g)),
                  pl.BlockSpec((lc, hd), lambda b, g, i: (cb + b, g)),
                  pl.BlockSpec((seq, hd), lambda b, g, i: (b, g)),
                  pl.BlockSpec((lc, hd), lambda b, g, i: (cb + b, g)),
                  pl.BlockSpec((tq, wblk), lambda b, g, i: (b * nq + i, C_AG // wblk + g))],
        out_specs=(pl.BlockSpec((tq, wblk), lambda b, g, i: (b * nq + i, g)),
                   pl.BlockSpec((tq, wblk), lambda b, g, i: (b * nq + i, g))),
        out_shape=(SDS((t_lat, ATT_HEADS * hd), BF), SDS((t_lat, ATT_HEADS * hd), F32)),
        compiler_params=_cp(("parallel", "parallel", "parallel"), 48))(qn, kn, kn, vn, vn, px)


def _att_bwd(qn, kn, vn, px, o_att, dy_att, cos_all, sin_all, qnw, n_samp, seq, lc):
    hd = ATT_HEAD_DIM
    tq = 128
    nq = seq // tq
    wblk = ATT_REP * hd
    cb = n_samp * seq // lc
    t_lat = n_samp * seq
    kvw = ATT_KV_HEADS * hd
    scale = hd ** -0.5

    def body(q_ref, kl_ref, kc_ref, vl_ref, vc_ref, g_ref, o_ref, dy_ref, x_ref, cos_ref, sin_ref, w_ref,
             dq_ref, dg_ref, dkl_ref, dkc_ref, dvl_ref, dvc_ref, gw_ref, akl, akc, avl, avc, aw):
        i = pl.program_id(2)

        @pl.when(i == 0)
        def _():
            akl[...] = jnp.zeros_like(akl)
            akc[...] = jnp.zeros_like(akc)
            avl[...] = jnp.zeros_like(avl)
            avc[...] = jnp.zeros_like(avc)
            aw[...] = jnp.zeros_like(aw)

        dos = []
        for r in range(ATT_REP):
            cols = slice(r * hd, (r + 1) * hd)
            g = g_ref[:, cols]
            sg = _sigmoid(g)
            dy = dy_ref[:, cols]
            dg_ref[:, cols] = (dy * o_ref[:, cols] * (sg * (1.0 + g * (1.0 - sg)))).astype(BF)
            dos.append(dy * (g * sg))
        do4 = jnp.concatenate(dos, axis=0)
        o4 = _stack_heads(o_ref)
        q4 = _stack_heads(q_ref)
        delta = jnp.sum(do4 * o4, axis=-1, keepdims=True)
        el, ec, denom = _att_scores(q4, kl_ref[...], kc_ref[...])
        inv = 1.0 / denom
        p_l = el * inv
        p_c = ec * inv
        dob = do4.astype(BF)
        avl[...] += _dot(p_l, dob, 0, 0)
        avc[...] += _dot(p_c, dob, 0, 0)
        ds_l = (p_l * (_dot(dob, vl_ref[...], 1, 1) - delta) * scale).astype(BF)
        ds_c = (p_c * (_dot(dob, vc_ref[...], 1, 1) - delta) * scale).astype(BF)
        dq4 = _dot(ds_l, kl_ref[...]) + _dot(ds_c, kc_ref[...])
        akl[...] += _dot(ds_l, q4, 0, 0)
        akc[...] += _dot(ds_c, q4, 0, 0)
        for r in range(ATT_REP):
            cols = slice(r * hd, (r + 1) * hd)
            dx, gw = _norm_rope_bwd(dq4[r * tq:(r + 1) * tq], x_ref[:, cols], w_ref[...], cos_ref[...], sin_ref[...])
            dq_ref[:, cols] = dx.astype(BF)
            aw[...] += gw

        @pl.when(i == nq - 1)
        def _():
            dkl_ref[...] = akl[...]
            dkc_ref[...] = akc[...]
            dvl_ref[...] = avl[...]
            dvc_ref[...] = avc[...]
            gw_ref[...] = aw[...]

    return pl.pallas_call(
        body, name="att_bwd", grid=(n_samp, ATT_KV_HEADS, nq),
        in_specs=[pl.BlockSpec((tq, wblk), lambda b, g, i: (b * nq + i, g)),
                  pl.BlockSpec((seq, hd), lambda b, g, i: (b, g)),
                  pl.BlockSpec((lc, hd), lambda b, g, i: (cb + b, g)),
                  pl.BlockSpec((seq, hd), lambda b, g, i: (b, g)),
                  pl.BlockSpec((lc, hd), lambda b, g, i: (cb + b, g)),
                  pl.BlockSpec((tq, wblk), lambda b, g, i: (b * nq + i, C_AG // wblk + g)),
                  pl.BlockSpec((tq, wblk), lambda b, g, i: (b * nq + i, g)),
                  pl.BlockSpec((tq, wblk), lambda b, g, i: (b * nq + i, g)),
                  pl.BlockSpec((tq, wblk), lambda b, g, i: (b * nq + i, C_AQ // wblk + g)),
                  pl.BlockSpec((tq, hd), lambda b, g, i: (b * nq + i, 0)),
                  pl.BlockSpec((tq, hd), lambda b, g, i: (b * nq + i, 0)),
                  pl.BlockSpec((1, hd), lambda b, g, i: (0, 0))],
        out_specs=(pl.BlockSpec((tq, wblk), lambda b, g, i: (b * nq + i, g)),
                   pl.BlockSpec((tq, wblk), lambda b, g, i: (b * nq + i, g)),
                   pl.BlockSpec((seq, hd), lambda b, g, i: (b, g)),
                   pl.BlockSpec((lc, hd), lambda b, g, i: (b, g)),
                   pl.BlockSpec((seq, hd), lambda b, g, i: (b, g)),
                   pl.BlockSpec((lc, hd), lambda b, g, i: (b, g)),
                   pl.BlockSpec((None, None, 1, hd), lambda b, g, i: (b, g, 0, 0))),
        out_shape=(SDS((t_lat, ATT_HEADS * hd), BF), SDS((t_lat, ATT_HEADS * hd), BF),
                   SDS((t_lat, kvw), F32), SDS((n_samp * lc, kvw), F32),
                   SDS((t_lat, kvw), F32), SDS((n_samp * lc, kvw), F32),
                   SDS((n_samp, ATT_KV_HEADS, 1, hd), F32)),
        scratch_shapes=[pltpu.VMEM((seq, hd), F32), pltpu.VMEM((lc, hd), F32),
                        pltpu.VMEM((seq, hd), F32), pltpu.VMEM((lc, hd), F32), pltpu.VMEM((1, hd), F32)],
        compiler_params=_cp(("parallel", "parallel", "arbitrary"), 56))(
            qn, kn, kn, vn, vn, px, o_att, dy_att, px, cos_all, sin_all, qnw)


def _merge(x_lat, target, y_ret, y_att, px, gate3, w_o_ret, w_o_att, w_out, tiles_per_sample):
    t_lat = x_lat.shape[0]
    tm = 256
    n_t = t_lat // tm
    per = tiles_per_sample * (TM // tm)
    d = D_MODEL
    rv = RET_HEADS * RET_DV
    n_samp = gate3.shape[0] - 1

    def body(x_ref, t_ref, yr_ref, ya_ref, mr0, mr1, ma0, ma1, gt_ref, wor_ref, woa_ref, wout_ref,
             gx_ref, dyr_ref, dya_ref, dmg_ref, loss_ref, dgt_ref, gwor_hbm, gwoa_hbm, gwout_hbm,
             aor, aoa, aout):
        i = pl.program_id(0)

        @pl.when(i == 0)
        def _():
            aor[...] = jnp.zeros_like(aor)
            aoa[...] = jnp.zeros_like(aoa)
            aout[...] = jnp.zeros_like(aout)
            loss_ref[...] = jnp.zeros_like(loss_ref)

        @pl.when(i % per == 0)
        def _():
            dgt_ref[...] = jnp.zeros_like(dgt_ref)

        yr = yr_ref[...]
        ya = ya_ref[...]
        a = jnp.dot(yr, wor_ref[...], preferred_element_type=F32)
        b = jnp.dot(ya, woa_ref[...], preferred_element_type=F32)
        sr = _sigmoid(jnp.concatenate([mr0[...], mr1[...]], axis=1))
        sa = _sigmoid(jnp.concatenate([ma0[...], ma1[...]], axis=1))
        yb = (sr * a + sa * b).astype(BF)
        out = jnp.dot(yb, wout_ref[...], preferred_element_type=F32)
        gate = gt_ref[...]
        err = x_ref[...] + gate * out - t_ref[...]
        loss_ref[...] += 0.5 * _sum_all(err * err) * (1.0 / d)
        dy_tok = err * (1.0 / d)
        gx_ref[...] = dy_tok
        dgt_ref[...] += jnp.sum(dy_tok * out, axis=0, keepdims=True)
        dout = (dy_tok * gate).astype(BF)
        aout[...] += _dot(yb, dout, 0, 0)
        dyy = _dot(dout, wout_ref[...], 1, 1)
        da = (dyy * sr).astype(BF)
        db = (dyy * sa).astype(BF)
        dmg_ref[:, :d] = (dyy * a * (sr * (1.0 - sr))).astype(BF)
        dmg_ref[:, d:] = (dyy * b * (sa * (1.0 - sa))).astype(BF)
        aor[...] += _dot(yr, da, 0, 0)
        aoa[...] += _dot(ya, db, 0, 0)
        dyr_ref[...] = _dot(da, wor_ref[...], 1, 1)
        dya_ref[...] = _dot(db, woa_ref[...], 1, 1)

        @pl.when(i == n_t - 1)
        def _():
            pltpu.sync_copy(aor, gwor_hbm)
            pltpu.sync_copy(aoa, gwoa_hbm)
            pltpu.sync_copy(aout, gwout_hbm)

    half = d // 2
    return pl.pallas_call(
        body, name="merge", grid=(n_t,),
        in_specs=[pl.BlockSpec((tm, d), lambda i: (i, 0)),
                  pl.BlockSpec((tm, d), lambda i: (i, 0)),
                  pl.BlockSpec((tm, rv), lambda i: (i, 0)),
                  pl.BlockSpec((tm, d), lambda i: (i, 0)),
                  pl.BlockSpec((tm, half), lambda i: (i, C_MR // half)),
                  pl.BlockSpec((tm, half), lambda i: (i, C_MR // half + 1)),
                  pl.BlockSpec((tm, half), lambda i: (i, C_MA // half)),
                  pl.BlockSpec((tm, half), lambda i: (i, C_MA // half + 1)),
                  pl.BlockSpec((None, 1, d), lambda i: (i // per, 0, 0)),
                  pl.BlockSpec((rv, d), lambda i: (0, 0)),
                  pl.BlockSpec((d, d), lambda i: (0, 0)),
                  pl.BlockSpec((d, d), lambda i: (0, 0))],
        out_specs=(pl.BlockSpec((tm, d), lambda i: (i, 0)),
                   pl.BlockSpec((tm, rv), lambda i: (i, 0)),
                   pl.BlockSpec((tm, d), lambda i: (i, 0)),
                   pl.BlockSpec((tm, 2 * d), lambda i: (i, 0)),
                   pl.BlockSpec((8, 128), lambda i: (0, 0)),
                   pl.BlockSpec((None, 1, d), lambda i: (i // per, 0, 0)),
                   ANY, ANY, ANY),
        out_shape=(SDS((t_lat, d), F32), SDS((t_lat, rv), F32), SDS((t_lat, d), F32), SDS((t_lat, 2 * d), BF),
                   SDS((8, 128), F32), SDS((n_samp, 1, d), F32),
                   SDS((rv, d), F32), SDS((d, d), F32), SDS((d, d), F32)),
        scratch_shapes=[pltpu.VMEM((rv, d), F32), pltpu.VMEM((d, d), F32), pltpu.VMEM((d, d), F32)],
        compiler_params=_cp(("arbitrary",), 56))(
            x_lat, target, y_ret, y_att, px, px, px, px, gate3, w_o_ret, w_o_att, w_out)


def _place():
    x, y, c = lax.axis_index("x"), lax.axis_index("y"), lax.axis_index("c")
    chips = [(1 - x, y), (x, 1 - y), (1 - x, 1 - y)]
    return x, y, c, chips


def _remote(src, dst, send_sem, recv_sem, to):
    return pltpu.make_async_remote_copy(src_ref=src, dst_ref=dst, send_sem=send_sem, recv_sem=recv_sem,
                                        device_id=to, device_id_type=MESH)


def _all_gather_weights(shards):
    n = len(shards)

    def body(*refs):
        ins, outs = refs[:n], refs[n:2 * n]
        send_sems, recv_sems, local_sems = refs[2 * n:]
        x, y, c, chips = _place()
        sibling = (x, y, 1 - c)
        me = 2 * x + y

        def half(ref, s):
            h = ref.shape[1] // 2
            return ref.at[s, pl.ds(c * h, h), :]

        def own_half(ref):
            h = ref.shape[0] // 2
            return ref.at[pl.ds(c * h, h), :]

        local = [pltpu.make_async_copy(ins[a], outs[a].at[me], local_sems.at[a]) for a in range(n)]
        for cp in local:
            cp.start()
        first = []
        for a in range(n):
            for j, chip in enumerate(chips):
                k = a * 3 + j
                first.append(_remote(own_half(ins[a]), half(outs[a], me), send_sems.at[k], recv_sems.at[k],
                                     (*chip, c)))
        for cp in first:
            cp.start()
        passed = []
        for a in range(n):
            for j, chip in enumerate(chips):
                k = a * 3 + j
                src_chip = 2 * chip[0] + chip[1]
                win = half(outs[a], src_chip)
                _remote(win, win, send_sems.at[k], recv_sems.at[k], (*chip, c)).wait_recv()
                fw = _remote(win, win, send_sems.at[3 * n + k], recv_sems.at[3 * n + k], sibling)
                fw.start()
                passed.append(fw)
        for a in range(n):
            for j, chip in enumerate(chips):
                k = a * 3 + j
                src_chip = 2 * chip[0] + chip[1]
                h = outs[a].shape[1] // 2
                win = outs[a].at[src_chip, pl.ds((1 - c) * h, h), :]
                _remote(win, win, send_sems.at[3 * n + k], recv_sems.at[3 * n + k], sibling).wait_recv()
        for cp in first + passed:
            cp.wait_send()
        for cp in local:
            cp.wait()

    return pl.pallas_call(
        body, name="all_gather_weights",
        in_specs=[ANY] * n, out_specs=tuple([ANY] * n),
        out_shape=tuple(SDS((N_SHARD,) + s.shape, s.dtype) for s in shards),
        scratch_shapes=[pltpu.SemaphoreType.DMA((6 * n,)), pltpu.SemaphoreType.DMA((6 * n,)),
                        pltpu.SemaphoreType.DMA((n,))],
        compiler_params=_cp(has_side_effects=True))(*shards)


def _swap_halves(grads):
    n = len(grads)

    def body(*refs):
        ins, outs = refs[:n], refs[n:2 * n]
        send_sems, recv_sems, local_sems = refs[2 * n:]
        x, y, c, _ = _place()
        sibling = (x, y, 1 - c)

        def half(ref, which):
            h = ref.shape[1] // 2
            return ref.at[:, pl.ds(which * h, h), :]

        local = [pltpu.make_async_copy(half(ins[a], c), outs[a].at[c], local_sems.at[a]) for a in range(n)]
        sends = [_remote(half(ins[a], 1 - c), outs[a].at[c], send_sems.at[a], recv_sems.at[a], sibling)
                 for a in range(n)]
        for cp in local + sends:
            cp.start()
        for a in range(n):
            _remote(half(ins[a], c), outs[a].at[1 - c], send_sems.at[a], recv_sems.at[a], sibling).wait_recv()
        for cp in sends:
            cp.wait_send()
        for cp in local:
            cp.wait()

    return pl.pallas_call(
        body, name="swap_halves",
        in_specs=[ANY] * n, out_specs=tuple([ANY] * n),
        out_shape=tuple(SDS((2, g.shape[0], g.shape[1] // 2, g.shape[2]), g.dtype) for g in grads),
        scratch_shapes=[pltpu.SemaphoreType.DMA((n,)), pltpu.SemaphoreType.DMA((n,)),
                        pltpu.SemaphoreType.DMA((n,))],
        compiler_params=_cp(has_side_effects=True))(*grads)


def _exchange_shards(parts):
    n = len(parts)

    def body(*refs):
        ins, outs = refs[:n], refs[n:2 * n]
        send_sems, recv_sems, local_sems = refs[2 * n:]
        x, y, c, chips = _place()
        me = 2 * x + y
        local = [pltpu.make_async_copy(ins[a].at[me], outs[a].at[me], local_sems.at[a]) for a in range(n)]
        sends = []
        for a in range(n):
            for j, chip in enumerate(chips):
                k = a * 3 + j
                sends.append(_remote(ins[a].at[2 * chip[0] + chip[1]], outs[a].at[me],
                                     send_sems.at[k], recv_sems.at[k], (*chip, c)))
        for cp in local + sends:
            cp.start()
        for a in range(n):
            for j, chip in enumerate(chips):
                k = a * 3 + j
                slot = outs[a].at[2 * chip[0] + chip[1]]
                _remote(slot, slot, send_sems.at[k], recv_sems.at[k], (*chip, c)).wait_recv()
        for cp in sends:
            cp.wait_send()
        for cp in local:
            cp.wait()

    return pl.pallas_call(
        body, name="exchange_shards",
        in_specs=[ANY] * n, out_specs=tuple([ANY] * n),
        out_shape=tuple(SDS(p.shape, p.dtype) for p in parts),
        scratch_shapes=[pltpu.SemaphoreType.DMA((3 * n,)), pltpu.SemaphoreType.DMA((3 * n,)),
                        pltpu.SemaphoreType.DMA((n,))],
        compiler_params=_cp(has_side_effects=True))(*parts)


def _join_halves(halves):
    n = len(halves)

    def body(*refs):
        ins, outs = refs[:n], refs[n:2 * n]
        send_sems, recv_sems, local_sems = refs[2 * n:]
        x, y, c, _ = _place()
        sibling = (x, y, 1 - c)

        def win(ref, which):
            h = ref.shape[0] // 2
            return ref.at[pl.ds(which * h, h), :]

        local = [pltpu.make_async_copy(ins[a], win(outs[a], c), local_sems.at[a]) for a in range(n)]
        sends = [_remote(ins[a], win(outs[a], c), send_sems.at[a], recv_sems.at[a], sibling) for a in range(n)]
        for cp in local + sends:
            cp.start()
        for a in range(n):
            _remote(ins[a], win(outs[a], 1 - c), send_sems.at[a], recv_sems.at[a], sibling).wait_recv()
        for cp in sends:
            cp.wait_send()
        for cp in local:
            cp.wait()

    return pl.pallas_call(
        body, name="join_halves",
        in_specs=[ANY] * n, out_specs=tuple([ANY] * n),
        out_shape=tuple(SDS((2 * h.shape[0], h.shape[1]), h.dtype) for h in halves),
        scratch_shapes=[pltpu.SemaphoreType.DMA((n,)), pltpu.SemaphoreType.DMA((n,)),
                        pltpu.SemaphoreType.DMA((n,))],
        compiler_params=_cp(has_side_effects=True))(*halves)


def _sum_slots(p, shard_major_out):
    slots = p.shape[0]
    rest = p.shape[1:]
    tr = min(rest[-2], 256)

    def body(p_ref, o_ref):
        acc = p_ref[0]
        for s in range(1, slots):
            acc = acc + p_ref[s]
        o_ref[...] = acc

    if shard_major_out:
        return pl.pallas_call(
            body, name="sum_slots", grid=(rest[0], rest[1] // tr),
            in_specs=[pl.BlockSpec((slots, None, tr, rest[2]), lambda s, i: (0, s, i, 0))],
            out_specs=pl.BlockSpec((None, tr, rest[2]), lambda s, i: (s, i, 0)),
            out_shape=SDS(rest, p.dtype),
            compiler_params=_cp(("parallel", "parallel"), 40))(p)
    return pl.pallas_call(
        body, name="sum_slots", grid=(rest[0] // tr,),
        in_specs=[pl.BlockSpec((slots, tr, rest[1]), lambda i: (0, i, 0))],
        out_specs=pl.BlockSpec((tr, rest[1]), lambda i: (i, 0)),
        out_shape=SDS(rest, p.dtype),
        compiler_params=_cp(("parallel",), 40))(p)


def _all_reduce_small(block):
    rows, cols = block.shape
    n_dev = 8

    def body(x_ref, o_ref, buf, send_sems, recv_sems, local_sem):
        x, y, c, chips = _place()
        me, sibling = (x, y, c), (x, y, 1 - c)

        def slot(px_, py_, pc_):
            return buf.at[4 * px_ + 2 * py_ + pc_]

        def copy(k, who, to, src=None):
            return _remote(slot(*who) if src is None else src, slot(*who), send_sems.at[k], recv_sems.at[k], to)

        mine = pltpu.make_async_copy(x_ref, slot(*me), local_sem)
        mine.start()
        first = [copy(0, me, sibling, src=x_ref)]
        first += [copy(1 + j, me, (*chip, c), src=x_ref) for j, chip in enumerate(chips)]
        for cp in first:
            cp.start()
        passed = [copy(4 + j, (*chip, c), sibling) for j, chip in enumerate(chips)]
        for j, chip in enumerate(chips):
            copy(1 + j, (*chip, c), me).wait_recv()
            passed[j].start()
        copy(0, sibling, me).wait_recv()
        for j, chip in enumerate(chips):
            copy(4 + j, (*chip, 1 - c), me).wait_recv()
        for cp in first + passed:
            cp.wait_send()
        mine.wait()
        acc = buf[0]
        for s in range(1, n_dev):
            acc = acc + buf[s]
        o_ref[...] = acc

    return pl.pallas_call(
        body, name="all_reduce_small",
        in_specs=[pl.BlockSpec(memory_space=pltpu.VMEM)],
        out_specs=pl.BlockSpec(memory_space=pltpu.VMEM),
        out_shape=SDS((rows, cols), F32),
        scratch_shapes=[pltpu.VMEM((n_dev, rows, cols), F32), pltpu.SemaphoreType.DMA((7,)),
                        pltpu.SemaphoreType.DMA((7,)), pltpu.SemaphoreType.DMA],
        compiler_params=_cp(has_side_effects=True))(block)


def _adam_math(w, g, m, v):
    m = ADAM_B1 * m + (1.0 - ADAM_B1) * g
    v = ADAM_B2 * v + (1.0 - ADAM_B2) * (g * g)
    m_hat = m / (1.0 - ADAM_B1 ** ADAM_STEP)
    v_hat = v / (1.0 - ADAM_B2 ** ADAM_STEP)
    delta = -ADAM_LR * (m_hat / (jnp.sqrt(v_hat) + ADAM_EPS) + ADAM_WD * w)
    return delta, m, v


def _adamw(w, g, m, v):
    rows, cols = w.shape
    tr = min(rows, 256)

    def body(w_ref, g_ref, m_ref, v_ref, d_ref, nm_ref, nv_ref):
        d_ref[...], nm_ref[...], nv_ref[...] = _adam_math(w_ref[...], g_ref[...], m_ref[...], v_ref[...])

    spec = pl.BlockSpec((tr, cols), lambda i: (i, 0))
    return pl.pallas_call(
        body, name="adamw", grid=(rows // tr,), in_specs=[spec] * 4, out_specs=(spec,) * 3,
        out_shape=(SDS(w.shape, F32),) * 3, compiler_params=_cp(("parallel",), 40))(w, g, m, v)


def _adamw_small(w, g, m, v):
    def body(w_ref, g_ref, m_ref, v_ref, go_ref, d_ref, nm_ref, nv_ref):
        w = w_ref[...]
        g = g_ref[...]
        sub = lax.broadcasted_iota(jnp.int32, w.shape, 0)
        lane = lax.broadcasted_iota(jnp.int32, w.shape, 1)
        is_ret = jnp.logical_and(sub == 5, lane < 2 * RET_HEADS)
        u = jnp.exp(jnp.where(is_ret, w, -1.0) * jnp.log(2.0))
        g = jnp.where(is_ret, g * (-u * jnp.log(2.0) / (1.0 - u)), g)
        go_ref[...] = g
        d_ref[...], nm_ref[...], nv_ref[...] = _adam_math(w, g, m_ref[...], v_ref[...])

    return pl.pallas_call(body, name="adamw_small", out_shape=(SDS(w.shape, F32),) * 4)(w, g, m, v)


def _rope_tables(seq, n_samp, n_ctx_rows):
    rows = seq // GRID_W
    row = jnp.repeat(jnp.arange(rows, dtype=F32), GRID_W)
    col = jnp.tile(jnp.arange(GRID_W, dtype=F32), rows)
    half = ATT_HEAD_DIM // 2
    freqs = ROPE_THETA ** (-jnp.arange(0, half, 2, dtype=F32) / half)
    ang = jnp.concatenate([row[:, None] * freqs, col[:, None] * freqs], axis=-1)
    cos, sin = jnp.cos(ang), jnp.sin(ang)
    cos_f = jnp.repeat(cos, 2, axis=1)
    sin_s = jnp.stack([-sin, sin], axis=-1).reshape(seq, ATT_HEAD_DIM)
    cos_all = jnp.concatenate([jnp.tile(cos_f, (n_samp, 1)), jnp.ones((n_ctx_rows, ATT_HEAD_DIM), F32)], axis=0)
    sin_all = jnp.concatenate([jnp.tile(sin_s, (n_samp, 1)), jnp.zeros((n_ctx_rows, ATT_HEAD_DIM), F32)], axis=0)
    return cos_all, sin_all


def _pack_small(c_ctx, norm_w, b_ada, ret, qn, kn):
    d = D_MODEL
    row5 = jnp.concatenate([ret.reshape(-1), jnp.zeros((128 - 2 * RET_HEADS,), F32), qn.reshape(-1), kn.reshape(-1),
                            jnp.zeros((d - 384,), F32)])
    return jnp.concatenate([c_ctx.reshape(1, d), norm_w.reshape(1, d), b_ada.reshape(3, d), row5.reshape(1, d),
                            jnp.zeros((2, d), F32)], axis=0)


def _unpack_small(p):
    d = D_MODEL
    return (p[0], p[1:2], p[2:5].reshape(1, 3 * d), p[5, :2 * RET_HEADS].reshape(1, 2, RET_HEADS),
            p[5:6, 128:256], p[5:6, 256:384])


def _local_step(x, c, ctx, c_ctx, norm_w, b_ada, ret_log2_decay, q_norm_w, k_norm_w, loss_target,
                w_ada_g, w_in_g, w_o_ret, w_o_att, w_out):
    n_samp, seq, d = x.shape
    lc = ctx.shape[1]
    t_lat, t_ctx = n_samp * seq, n_samp * lc
    assert seq % TM == 0 and t_ctx % TM == 0 and t_lat % lc == 0 and seq % GRID_W == 0
    tps = seq // TM

    x_lat = x.reshape(t_lat, d)
    x_all = jnp.concatenate([x_lat, ctx.reshape(t_ctx, d)], axis=0)
    cvec8 = jnp.concatenate([c, c_ctx.reshape(1, d), jnp.zeros((8 - n_samp - 1, d), F32)], axis=0)
    lg = jnp.log1p(-jnp.exp2(ret_log2_decay.reshape(2, RET_HEADS)))
    cos_all, sin_all = _rope_tables(seq, n_samp, t_ctx)

    mod8 = _adaln_fwd(cvec8, w_ada_g, b_ada)
    mod3 = mod8[:n_samp + 1]
    shift3 = mod3[:, None, 0:d]
    scale3 = mod3[:, None, d:2 * d]
    gate3 = mod3[:, None, 2 * d:3 * d]

    px, hxt = _normproj(x_all, norm_w, scale3, shift3, w_in_g, tps, n_samp)

    states0 = _ctx_state_fwd(px, lg, n_samp, t_lat, lc)
    o2, saved = _ret_fwd(px, states0, lg, n_samp, seq)
    y_ret = _retnorm_fwd(o2, px)

    qn = _att_prep_q(px, cos_all, sin_all, q_norm_w, t_lat)
    kn, vn = _att_prep_kv(px, cos_all, sin_all, k_norm_w)
    y_att, o_att = _att_fwd(qn, kn, vn, px, n_samp, seq, lc)

    (gx_res, dy_ret, dy_att, dmg, loss8, dgate, g_w_o_ret, g_w_o_att, g_w_out) = _merge(
        x_lat, loss_target.reshape(t_lat, d), y_ret, y_att, px, gate3, w_o_ret, w_o_att, w_out, tps)

    d_att_q, d_att_g, dkl, dkc, dvl, dvc, gqw = _att_bwd(
        qn, kn, vn, px, o_att, dy_att, cos_all, sin_all, q_norm_w, n_samp, seq, lc)
    d_att_kv, gkw = _att_kv_bwd(jnp.concatenate([dkl, dkc], axis=0), jnp.concatenate([dvl, dvc], axis=0),
                                px, cos_all, sin_all, k_norm_w)

    do, d_ret_g = _retnorm_bwd(dy_ret, o2, px)
    dq2, dk2, dv2, dstates, dlg_lat = _ret_bwd(px, do, saved, lg, n_samp, seq)
    d_ret_q, d_ret_k, d_ret_v = _ret_combine(dq2, dk2, dv2)
    dc_ret_k, dc_ret_v, dlg_ctx = _ctx_state_bwd(px, dstates, lg, n_samp, t_lat, lc)

    dpx_lat = jnp.concatenate([d_ret_k, d_ret_v, d_att_kv[:t_lat], d_ret_q, d_ret_g, d_att_q, d_att_g, dmg], axis=1)
    dpx_ctx = jnp.concatenate([dc_ret_k, dc_ret_v, d_att_kv[t_lat:], jnp.zeros((t_ctx, IN_COLS - KV_COLS), BF)],
                              axis=1)
    dpx_all = jnp.concatenate([dpx_lat, dpx_ctx], axis=0)

    g_w_in = _gw_in(hxt, dpx_all)
    dhx = _dhx(dpx_all, w_in_g)
    grad_x, dshift, dscale, g_norm_w = _norm_bwd(x_all, dhx, gx_res, norm_w, scale3, shift3, tps, n_samp)

    dgate_all = jnp.concatenate([dgate, jnp.zeros((1, 1, d), F32)], axis=0)
    dmod3 = jnp.concatenate([dshift, dscale, dgate_all], axis=2).reshape(n_samp + 1, 3 * d)
    dmod8 = jnp.concatenate([dmod3, jnp.zeros((8 - n_samp - 1, 3 * d), F32)], axis=0)
    g_w_ada, g_b_ada, dc8 = _adaln_bwd(cvec8, dmod8, w_ada_g)

    g_lg = (jnp.sum(dlg_lat[:, :, :, 0, 0], axis=0).T
            + jnp.stack([jnp.sum(dlg_ctx[:, :, 0, 0], axis=0), jnp.sum(dlg_ctx[:, :, 1, 0], axis=0)], axis=0))
    small = _pack_small(dc8[n_samp], g_norm_w, g_b_ada, g_lg, jnp.sum(gqw, axis=(0, 1, 2)), gkw)
    return (loss8[0, 0], grad_x.reshape(n_samp, seq, d),
            (g_w_ada, g_w_in, g_w_o_ret, g_w_o_att, g_w_out), small)


def kernel(x, c, ctx, c_ctx, norm_w, w_ada, b_ada, w_in, ret_log2_decay, q_norm_w, k_norm_w, w_o_ret, w_o_att, w_out, loss_target, m_c_ctx, m_norm_w, m_w_ada, m_b_ada, m_w_in, m_ret_log2_decay, m_q_norm_w, m_k_norm_w, m_w_o_ret, m_w_o_att, m_w_out, v_c_ctx, v_norm_w, v_w_ada, v_b_ada, v_w_in, v_ret_log2_decay, v_q_norm_w, v_k_norm_w, v_w_o_ret, v_w_o_att, v_w_out):
    big_w = (w_ada[0], w_in[0], w_o_ret[0], w_o_att[0], w_out[0])
    big_m = (m_w_ada[0], m_w_in[0], m_w_o_ret[0], m_w_o_att[0], m_w_out[0])
    big_v = (v_w_ada[0], v_w_in[0], v_w_o_ret[0], v_w_o_att[0], v_w_out[0])

    gathered = _all_gather_weights(tuple(w.astype(BF) for w in big_w))
    w_ada_g, w_in_g = gathered[0], gathered[1]
    w_o_ret_f = gathered[2].reshape(-1, D_MODEL)
    w_o_att_f = gathered[3].reshape(-1, D_MODEL)
    w_out_f = gathered[4].reshape(-1, D_MODEL)

    loss_local, grad_x, big_g, small_g = _local_step(
        x, c, ctx, c_ctx, norm_w[0:1], b_ada[0:1], ret_log2_decay[0], q_norm_w[0:1], k_norm_w[0:1], loss_target,
        w_ada_g, w_in_g, w_o_ret_f, w_o_att_f, w_out_f)
    loss = lax.psum(loss_local, ("x", "y", "c"))

    g_sm = (big_g[0], big_g[1], big_g[2].reshape(N_SHARD, -1, D_MODEL), big_g[3].reshape(N_SHARD, -1, D_MODEL),
            big_g[4].reshape(N_SHARD, -1, D_MODEL))
    pairs = _swap_halves(g_sm)
    chip_sums = tuple(_sum_slots(p, True) for p in pairs)
    quads = _exchange_shards(chip_sums)
    halves = tuple(_sum_slots(q, False) for q in quads)
    big_grad = _join_halves(halves)

    small_grad_in = _all_reduce_small(small_g)
    small_w = _pack_small(c_ctx, norm_w, b_ada, ret_log2_decay, q_norm_w, k_norm_w)
    small_m = _pack_small(m_c_ctx, m_norm_w, m_b_ada, m_ret_log2_decay, m_q_norm_w, m_k_norm_w)
    small_v = _pack_small(v_c_ctx, v_norm_w, v_b_ada, v_ret_log2_decay, v_q_norm_w, v_k_norm_w)
    small_grad, small_delta, small_nm, small_nv = _adamw_small(small_w, small_grad_in, small_m, small_v)

    big_delta, big_nm, big_nv = [], [], []
    for w, g, m, v in zip(big_w, big_grad, big_m, big_v):
        dlt, nm, nv = _adamw(w, g, m, v)
        big_delta.append(dlt[None])
        big_nm.append(nm[None])
        big_nv.append(nv[None])
    big_grad = [g[None] for g in big_grad]

    def order(small_packed, big):
        s = _unpack_small(small_packed)
        return (s[0], s[1], big[0], s[2], big[1], s[3], s[4], s[5], big[2], big[3], big[4])

    return (loss, grad_x, *order(small_grad, big_grad), *order(small_delta, big_delta),
            *order(small_nm, big_nm), *order(small_nv, big_nv))
```

```python
import functools

import jax
import jax.numpy as jnp
from jax import lax
from jax.experimental import pallas as pl
from jax.experimental.pallas import tpu as pltpu

F32 = jnp.float32
BF = jnp.bfloat16
SDS = jax.ShapeDtypeStruct
MESH = pl.DeviceIdType.MESH
ANY = pl.BlockSpec(memory_space=pl.ANY)
SMEM = pl.BlockSpec(memory_space=pltpu.SMEM)

D_MODEL = 1024
GRID_W = 64
RET_HEADS = 4
RET_DK = 256
RET_DV = 512
RET_CHUNK = 128
ATT_HEADS = 8
ATT_KV_HEADS = 2
ATT_REP = ATT_HEADS // ATT_KV_HEADS
ATT_HEAD_DIM = 128
ROPE_THETA = 10000.0
NORM_EPS = 1e-6
IN_COLS = 10752
KV_COLS = 3584
C_RK, C_RV, C_AK, C_AV, C_RQ, C_RG, C_AQ, C_AG, C_MR, C_MA = 0, 1024, 3072, 3328, 3584, 4608, 6656, 7680, 8704, 9728
N_SHARD = 4
ADA_W = 3 * D_MODEL // N_SHARD
IN_W = IN_COLS // N_SHARD
IN_BLK = IN_W // 3
N_IN_BLK = IN_COLS // IN_BLK
TM = 512
ADAM_LR, ADAM_B1, ADAM_B2, ADAM_EPS, ADAM_WD, ADAM_STEP = 0.001, 0.9, 0.999, 1e-08, 0.01, 10
MIB = 1024 * 1024


def _cp(sem=None, vmem_mb=None, **kw):
    if sem is not None:
        kw["dimension_semantics"] = sem
    if vmem_mb is not None:
        kw["vmem_limit_bytes"] = vmem_mb * MIB
    return pltpu.CompilerParams(**kw)


def _dot(a, b, ca=1, cb=0):
    return lax.dot_general(a.astype(BF), b.astype(BF), (((ca,), (cb,)), ((), ())), preferred_element_type=F32)


def _sigmoid(x):
    return 1.0 / (1.0 + jnp.exp(-x))


def _sum_all(x):
    return jnp.sum(jnp.sum(x, axis=1, keepdims=True), axis=0, keepdims=True)


def _swap_pairs(x):
    ax = x.ndim - 1
    lane = lax.broadcasted_iota(jnp.int32, x.shape, ax)
    nxt = pltpu.roll(x, x.shape[ax] - 1, ax)
    prv = pltpu.roll(x, 1, ax)
    return jnp.where(lane % 2 == 0, nxt, prv)


def _rms(x):
    return lax.rsqrt(jnp.mean(x * x, axis=-1, keepdims=True) + NORM_EPS)


def _rms_bwd(dxh, xh, r):
    return r * (dxh - xh * jnp.mean(dxh * xh, axis=-1, keepdims=True))


def _adaln_fwd(cvec8, w_ada_g, b_ada):
    def body(c_ref, w_ref, b_ref, o_ref):
        cv = c_ref[...]
        sc = (cv * _sigmoid(cv)).astype(BF)
        for s in range(N_SHARD):
            cols = slice(s * ADA_W, (s + 1) * ADA_W)
            o_ref[:, cols] = jnp.dot(sc, w_ref[s], preferred_element_type=F32) + b_ref[:, cols]

    return pl.pallas_call(body, out_shape=SDS((8, 3 * D_MODEL), F32), name="adaln_fwd",
                          compiler_params=_cp(vmem_mb=32))(cvec8, w_ada_g, b_ada)


def _adaln_bwd(cvec8, dmod8, w_ada_g):
    def body(c_ref, d_ref, w_ref, gw_ref, gb_ref, dc_ref):
        cv = c_ref[...]
        sg = _sigmoid(cv)
        sc = cv * sg
        dm = d_ref[...]
        gb_ref[...] = jnp.sum(dm, axis=0, keepdims=True)
        dsc = jnp.zeros((8, D_MODEL), F32)
        for s in range(N_SHARD):
            cols = slice(s * ADA_W, (s + 1) * ADA_W)
            gw_ref[s] = _dot(sc, dm[:, cols], 0, 0)
            dsc = dsc + _dot(dm[:, cols], w_ref[s], 1, 1)
        dc_ref[...] = dsc * (sg * (1.0 + cv * (1.0 - sg)))

    return pl.pallas_call(
        body, name="adaln_bwd",
        out_shape=(SDS((N_SHARD, D_MODEL, ADA_W), F32), SDS((1, 3 * D_MODEL), F32), SDS((8, D_MODEL), F32)),
        compiler_params=_cp(vmem_mb=48))(cvec8, dmod8, w_ada_g)


def _normproj(x_all, norm_w, scale3, shift3, w_in_g, tiles_per_sample, n_samp):
    rows = x_all.shape[0]

    def samp(i):
        return jnp.minimum(i // tiles_per_sample, n_samp)

    def body(x_ref, nw_ref, sc_ref, sh_ref, w_ref, px_ref, hxt_ref, hx_s):
        @pl.when(pl.program_id(1) == 0)
        def _():
            x = x_ref[...]
            h = x * _rms(x) * nw_ref[...] * (1.0 + sc_ref[...]) + sh_ref[...]
            hx_s[...] = h.astype(BF)
            hxt_ref[...] = h.T.astype(BF)

        px_ref[...] = jnp.dot(hx_s[...], w_ref[...], preferred_element_type=F32)

    return pl.pallas_call(
        body, name="normproj", grid=(rows // TM, N_IN_BLK),
        in_specs=[pl.BlockSpec((TM, D_MODEL), lambda i, j: (i, 0)),
                  pl.BlockSpec((1, D_MODEL), lambda i, j: (0, 0)),
                  pl.BlockSpec((None, 1, D_MODEL), lambda i, j: (samp(i), 0, 0)),
                  pl.BlockSpec((None, 1, D_MODEL), lambda i, j: (samp(i), 0, 0)),
                  pl.BlockSpec((None, D_MODEL, IN_BLK), lambda i, j: (j // 3, 0, j % 3))],
        out_specs=(pl.BlockSpec((TM, IN_BLK), lambda i, j: (i, j)),
                   pl.BlockSpec((D_MODEL, TM), lambda i, j: (0, i))),
        out_shape=(SDS((rows, IN_COLS), F32), SDS((D_MODEL, rows), BF)),
        scratch_shapes=[pltpu.VMEM((TM, D_MODEL), BF)],
        compiler_params=_cp(("parallel", "arbitrary"), 40))(x_all, norm_w, scale3, shift3, w_in_g)


def _norm_bwd(x_all, dhx, gx_res, norm_w, scale3, shift3, tiles_per_sample, n_samp):
    rows = x_all.shape[0]
    n_lat = tiles_per_sample * n_samp
    del shift3

    def samp(i):
        return jnp.minimum(i // tiles_per_sample, n_samp)

    def lat(i):
        return jnp.minimum(i, n_lat - 1)

    def body(x_ref, dh_ref, gr_ref, nw_ref, sc_ref, gx_ref, dsh_ref, dsc_ref, dnw_ref):
        i = pl.program_id(0)
        x = x_ref[...]
        r = _rms(x)
        xh = x * r
        nw = nw_ref[...]
        dh = dh_ref[...]
        first = jnp.logical_or(i % tiles_per_sample == 0, i >= n_lat)

        @pl.when(first)
        def _():
            dsh_ref[...] = jnp.zeros_like(dsh_ref)
            dsc_ref[...] = jnp.zeros_like(dsc_ref)

        @pl.when(i == 0)
        def _():
            dnw_ref[...] = jnp.zeros_like(dnw_ref)

        dsh_ref[...] += jnp.sum(dh, axis=0, keepdims=True)
        dsc_ref[...] += jnp.sum(dh * (xh * nw), axis=0, keepdims=True)
        du = dh * (1.0 + sc_ref[...])
        dnw_ref[...] += jnp.sum(du * xh, axis=0, keepdims=True)

        @pl.when(i < n_lat)
        def _():
            gx_ref[...] = gr_ref[...] + _rms_bwd(du * nw, xh, r)

    return pl.pallas_call(
        body, name="norm_bwd", grid=(rows // TM,),
        in_specs=[pl.BlockSpec((TM, D_MODEL), lambda i: (i, 0)),
                  pl.BlockSpec((TM, D_MODEL), lambda i: (i, 0)),
                  pl.BlockSpec((TM, D_MODEL), lambda i: (lat(i), 0)),
                  pl.BlockSpec((1, D_MODEL), lambda i: (0, 0)),
                  pl.BlockSpec((None, 1, D_MODEL), lambda i: (samp(i), 0, 0))],
        out_specs=(pl.BlockSpec((TM, D_MODEL), lambda i: (lat(i), 0)),
                   pl.BlockSpec((None, 1, D_MODEL), lambda i: (samp(i), 0, 0)),
                   pl.BlockSpec((None, 1, D_MODEL), lambda i: (samp(i), 0, 0)),
                   pl.BlockSpec((1, D_MODEL), lambda i: (0, 0))),
        out_shape=(SDS((n_lat * TM, D_MODEL), F32), SDS((n_samp + 1, 1, D_MODEL), F32),
                   SDS((n_samp + 1, 1, D_MODEL), F32), SDS((1, D_MODEL), F32)),
        compiler_params=_cp(("arbitrary",), 40))(x_all, dhx, gx_res, norm_w, scale3)


def _gw_in(hxt, dpx_all):
    rows = dpx_all.shape[0]

    def body(h_ref, d_ref, o_ref):
        @pl.when(pl.program_id(1) == 0)
        def _():
            o_ref[...] = jnp.zeros_like(o_ref)

        o_ref[...] += jnp.dot(h_ref[...], d_ref[...], preferred_element_type=F32)

    return pl.pallas_call(
        body, name="gw_in", grid=(N_IN_BLK, rows // TM),
        in_specs=[pl.BlockSpec((D_MODEL, TM), lambda j, i: (0, i)),
                  pl.BlockSpec((TM, IN_BLK), lambda j, i: (i, j))],
        out_specs=pl.BlockSpec((None, D_MODEL, IN_BLK), lambda j, i: (j // 3, 0, j % 3)),
        out_shape=SDS((N_SHARD, D_MODEL, IN_W), F32),
        compiler_params=_cp(("parallel", "arbitrary"), 40))(hxt, dpx_all)


def _dhx(dpx_all, w_in_g):
    rows = dpx_all.shape[0]

    def body(d_ref, w_ref, o_ref):
        @pl.when(pl.program_id(1) == 0)
        def _():
            o_ref[...] = jnp.zeros_like(o_ref)

        o_ref[...] += lax.dot_general(d_ref[...], w_ref[...], (((1,), (1,)), ((), ())), preferred_element_type=F32)

    return pl.pallas_call(
        body, name="dhx", grid=(rows // TM, N_IN_BLK),
        in_specs=[pl.BlockSpec((TM, IN_BLK), lambda i, j: (i, j)),
                  pl.BlockSpec((None, D_MODEL, IN_BLK), lambda i, j: (j // 3, 0, j % 3))],
        out_specs=pl.BlockSpec((TM, D_MODEL), lambda i, j: (i, 0)),
        out_shape=SDS((rows, D_MODEL), F32),
        compiler_params=_cp(("parallel", "arbitrary"), 40))(dpx_all, w_in_g)


def _decays(lgv, d):
    c = RET_CHUNK
    ii = lax.broadcasted_iota(jnp.int32, (c, 1), 0).astype(F32)
    jj = lax.broadcasted_iota(jnp.int32, (1, c), 1).astype(F32)
    a_i = jnp.where(d == 0, ii, c - 1.0 - ii)
    a_j = jnp.where(d == 0, jj, c - 1.0 - jj)
    rel = a_i - a_j
    mask = jnp.where(rel >= 0, jnp.exp(lgv * jnp.maximum(rel, 0.0)), 0.0)
    qd = jnp.exp(lgv * (a_i + 1.0))
    kd = jnp.exp(lgv * (c - 1.0 - a_i))
    gc = jnp.exp(jnp.full((1, 1), lgv * c, F32))
    return a_i, rel, mask, qd, kd, gc


def _ctx_state_fwd(px, lg, n_samp, t_lat, lc):
    rb = t_lat // lc

    def body(lg_ref, k_ref, v_ref, o_ref):
        h = pl.program_id(1)
        k = k_ref[...] * (RET_DK ** -0.5)
        v = v_ref[...]
        pos = lax.broadcasted_iota(jnp.int32, (lc, 1), 0).astype(F32)
        o_ref[0] = _dot(k * jnp.exp(lg_ref[0, h] * (lc - 1.0 - pos)), v, 0, 0)
        o_ref[1] = _dot(k * jnp.exp(lg_ref[1, h] * pos), v, 0, 0)

    return pl.pallas_call(
        body, name="ctx_state_fwd", grid=(n_samp, RET_HEADS),
        in_specs=[SMEM,
                  pl.BlockSpec((lc, RET_DK), lambda b, h: (rb + b, C_RK // RET_DK + h)),
                  pl.BlockSpec((lc, RET_DV), lambda b, h: (rb + b, C_RV // RET_DV + h))],
        out_specs=pl.BlockSpec((None, 2, None, RET_DK, RET_DV), lambda b, h: (b, 0, h, 0, 0)),
        out_shape=SDS((n_samp, 2, RET_HEADS, RET_DK, RET_DV), F32),
        compiler_params=_cp(("parallel", "parallel")))(lg, px, px)


def _ctx_state_bwd(px, dstates, lg, n_samp, t_lat, lc):
    rb = t_lat // lc

    def body(lg_ref, k_ref, v_ref, ds_ref, dk_ref, dv_ref, dlg_ref):
        h = pl.program_id(1)
        k = k_ref[...] * (RET_DK ** -0.5)
        v = v_ref[...]
        pos = lax.broadcasted_iota(jnp.int32, (lc, 1), 0).astype(F32)
        e_f = lc - 1.0 - pos
        kw_f = k * jnp.exp(lg_ref[0, h] * e_f)
        kw_b = k * jnp.exp(lg_ref[1, h] * pos)
        y_f = _dot(v, ds_ref[0], 1, 1)
        y_b = _dot(v, ds_ref[1], 1, 1)
        dk = y_f * jnp.exp(lg_ref[0, h] * e_f) + y_b * jnp.exp(lg_ref[1, h] * pos)
        dk_ref[...] = (dk * (RET_DK ** -0.5)).astype(BF)
        dv_ref[...] = (_dot(kw_f, ds_ref[0]) + _dot(kw_b, ds_ref[1])).astype(BF)
        t_f = _sum_all(e_f * kw_f * y_f)
        t_b = _sum_all(pos * kw_b * y_b)
        sub = lax.broadcasted_iota(jnp.int32, (8, 128), 0)
        dlg_ref[...] = jnp.where(sub == 0, t_f, jnp.where(sub == 1, t_b, 0.0))

    return pl.pallas_call(
        body, name="ctx_state_bwd", grid=(n_samp, RET_HEADS),
        in_specs=[SMEM,
                  pl.BlockSpec((lc, RET_DK), lambda b, h: (rb + b, C_RK // RET_DK + h)),
                  pl.BlockSpec((lc, RET_DV), lambda b, h: (rb + b, C_RV // RET_DV + h)),
                  pl.BlockSpec((None, 2, None, RET_DK, RET_DV), lambda b, h: (b, 0, h, 0, 0))],
        out_specs=(pl.BlockSpec((lc, RET_DK), lambda b, h: (b, h)),
                   pl.BlockSpec((lc, RET_DV), lambda b, h: (b, h)),
                   pl.BlockSpec((None, None, 8, 128), lambda b, h: (b, h, 0, 0))),
        out_shape=(SDS((n_samp * lc, RET_HEADS * RET_DK), BF), SDS((n_samp * lc, RET_HEADS * RET_DV), BF),
                   SDS((n_samp, RET_HEADS, 8, 128), F32)),
        compiler_params=_cp(("parallel", "parallel")))(lg, px, px, dstates)


def _ret_fwd(px, states0, lg, n_samp, seq):
    c = RET_CHUNK
    nc = seq // c
    t_lat = n_samp * seq

    def chunk(n, d):
        return n + d * (nc - 1 - 2 * n)

    def body(lg_ref, q_ref, k_ref, v_ref, s0_ref, o_ref, st_ref, s_s):
        h, d, n = pl.program_id(1), pl.program_id(2), pl.program_id(3)

        @pl.when(n == 0)
        def _():
            s_s[...] = s0_ref[...]

        _, _, mask, qd, kd, gc = _decays(lg_ref[d, h], d)
        q = q_ref[...]
        k = k_ref[...] * (RET_DK ** -0.5)
        v = v_ref[...]
        s = s_s[...]
        st_ref[...] = s.astype(BF)
        sc = _dot(q, k, 1, 1) * mask
        o_ref[...] = _dot(sc, v) + _dot(q * qd, s)
        s_s[...] = s * gc + _dot(k * kd, v, 0, 0)

    return pl.pallas_call(
        body, name="ret_fwd", grid=(n_samp, RET_HEADS, 2, nc),
        in_specs=[SMEM,
                  pl.BlockSpec((c, RET_DK), lambda b, h, d, n: (b * nc + chunk(n, d), C_RQ // RET_DK + h)),
                  pl.BlockSpec((c, RET_DK), lambda b, h, d, n: (b * nc + chunk(n, d), C_RK // RET_DK + h)),
                  pl.BlockSpec((c, RET_DV), lambda b, h, d, n: (b * nc + chunk(n, d), C_RV // RET_DV + h)),
                  pl.BlockSpec((None, None, None, RET_DK, RET_DV), lambda b, h, d, n: (b, d, h, 0, 0))],
        out_specs=(pl.BlockSpec((None, c, RET_DV), lambda b, h, d, n: (d, b * nc + chunk(n, d), h)),
                   pl.BlockSpec((None, None, None, None, RET_DK, RET_DV), lambda b, h, d, n: (b, h, d, n, 0, 0))),
        out_shape=(SDS((2, t_lat, RET_HEADS * RET_DV), F32),
                   SDS((n_samp, RET_HEADS, 2, nc, RET_DK, RET_DV), BF)),
        scratch_shapes=[pltpu.VMEM((RET_DK, RET_DV), F32)],
        compiler_params=_cp(("parallel", "parallel", "parallel", "arbitrary")))(lg, px, px, px, states0)


def _ret_bwd(px, do, saved, lg, n_samp, seq):
    c = RET_CHUNK
    nc = seq // c
    t_lat = n_samp * seq

    def chunk(n, d):
        m = nc - 1 - n
        return m + d * (nc - 1 - 2 * m)

    def body(lg_ref, q_ref, k_ref, v_ref, do_ref, st_ref, dq_ref, dk_ref, dv_ref, ds0_ref, dlg_ref, ds_s, acc_s):
        h, d, n = pl.program_id(1), pl.program_id(2), pl.program_id(3)

        @pl.when(n == 0)
        def _():
            ds_s[...] = jnp.zeros_like(ds_s)
            acc_s[...] = jnp.zeros_like(acc_s)

        a_i, rel, mask, qd, kd, gc = _decays(lg_ref[d, h], d)
        q = q_ref[...]
        k = k_ref[...] * (RET_DK ** -0.5)
        qb, kb, vb, dob = q.astype(BF), k.astype(BF), v_ref[...].astype(BF), do_ref[...].astype(BF)
        sb = st_ref[...]
        ds = ds_s[...]
        dsb = ds.astype(BF)
        raw = _dot(qb, kb, 1, 1)
        sc = raw * mask
        dsc = _dot(dob, vb, 1, 1) * mask
        dscb = dsc.astype(BF)
        x = _dot(dob, sb, 1, 1)
        y = _dot(vb, dsb, 1, 1)
        qq = q * qd
        kk = k * kd
        dq_ref[...] = _dot(dscb, kb) + x * qd
        dk_ref[...] = _dot(dscb, qb, 0, 0) + y * kd
        dv_ref[...] = _dot(sc, dob, 0, 0) + _dot(kk, dsb)
        ds_new = ds * gc + _dot(qq, dob, 0, 0)
        t = (_sum_all(dsc * raw * rel) + _sum_all((a_i + 1.0) * qq * x)
             + _sum_all((c - 1.0 - a_i) * kk * y) + c * gc * _sum_all(ds * sb.astype(F32)))
        acc_s[...] += t
        ds_s[...] = ds_new

        @pl.when(n == nc - 1)
        def _():
            ds0_ref[...] = ds_new
            dlg_ref[...] = acc_s[...]

    return pl.pallas_call(
        body, name="ret_bwd", grid=(n_samp, RET_HEADS, 2, nc),
        in_specs=[SMEM,
                  pl.BlockSpec((c, RET_DK), lambda b, h, d, n: (b * nc + chunk(n, d), C_RQ // RET_DK + h)),
                  pl.BlockSpec((c, RET_DK), lambda b, h, d, n: (b * nc + chunk(n, d), C_RK // RET_DK + h)),
                  pl.BlockSpec((c, RET_DV), lambda b, h, d, n: (b * nc + chunk(n, d), C_RV // RET_DV + h)),
                  pl.BlockSpec((c, RET_DV), lambda b, h, d, n: (b * nc + chunk(n, d), h)),
                  pl.BlockSpec((None, None, None, None, RET_DK, RET_DV),
                               lambda b, h, d, n: (b, h, d, nc - 1 - n, 0, 0))],
        out_specs=(pl.BlockSpec((None, c, RET_DK), lambda b, h, d, n: (d, b * nc + chunk(n, d), h)),
                   pl.BlockSpec((None, c, RET_DK), lambda b, h, d, n: (d, b * nc + chunk(n, d), h)),
                   pl.BlockSpec((None, c, RET_DV), lambda b, h, d, n: (d, b * nc + chunk(n, d), h)),
                   pl.BlockSpec((None, None, None, RET_DK, RET_DV), lambda b, h, d, n: (b, d, h, 0, 0)),
                   pl.BlockSpec((None, None, None, 8, 128), lambda b, h, d, n: (b, h, d, 0, 0))),
        out_shape=(SDS((2, t_lat, RET_HEADS * RET_DK), F32), SDS((2, t_lat, RET_HEADS * RET_DK), F32),
                   SDS((2, t_lat, RET_HEADS * RET_DV), F32),
                   SDS((n_samp, 2, RET_HEADS, RET_DK, RET_DV), F32), SDS((n_samp, RET_HEADS, 2, 8, 128), F32)),
        scratch_shapes=[pltpu.VMEM((RET_DK, RET_DV), F32), pltpu.VMEM((8, 128), F32)],
        compiler_params=_cp(("parallel", "parallel", "parallel", "arbitrary")))(lg, px, px, px, do, saved)


def _ret_combine(dq2, dk2, dv2):
    t_lat = dq2.shape[1]
    tm = 256

    def body(q_ref, k_ref, v_ref, oq_ref, ok_ref, ov_ref):
        oq_ref[...] = (q_ref[0] + q_ref[1]).astype(BF)
        ok_ref[...] = ((k_ref[0] + k_ref[1]) * (RET_DK ** -0.5)).astype(BF)
        ov_ref[...] = (v_ref[0] + v_ref[1]).astype(BF)

    wq, wv = RET_HEADS * RET_DK, RET_HEADS * RET_DV
    return pl.pallas_call(
        body, name="ret_combine", grid=(t_lat // tm,),
        in_specs=[pl.BlockSpec((2, tm, wq), lambda i: (0, i, 0)),
                  pl.BlockSpec((2, tm, wq), lambda i: (0, i, 0)),
                  pl.BlockSpec((2, tm, wv), lambda i: (0, i, 0))],
        out_specs=(pl.BlockSpec((tm, wq), lambda i: (i, 0)),
                   pl.BlockSpec((tm, wq), lambda i: (i, 0)),
                   pl.BlockSpec((tm, wv), lambda i: (i, 0))),
        out_shape=(SDS((t_lat, wq), BF), SDS((t_lat, wq), BF), SDS((t_lat, wv), BF)),
        compiler_params=_cp(("parallel",), 40))(dq2, dk2, dv2)


def _retnorm_fwd(o2, px):
    t_lat = o2.shape[1]

    def body(o_ref, g_ref, y_ref):
        o = o_ref[0] + o_ref[1]
        g = g_ref[...]
        y_ref[...] = (o * _rms(o) * (g * _sigmoid(g))).astype(BF)

    return pl.pallas_call(
        body, name="retnorm_fwd", grid=(t_lat // TM, RET_HEADS),
        in_specs=[pl.BlockSpec((2, TM, RET_DV), lambda i, h: (0, i, h)),
                  pl.BlockSpec((TM, RET_DV), lambda i, h: (i, C_RG // RET_DV + h))],
        out_specs=pl.BlockSpec((TM, RET_DV), lambda i, h: (i, h)),
        out_shape=SDS((t_lat, RET_HEADS * RET_DV), BF),
        compiler_params=_cp(("parallel", "parallel")))(o2, px)


def _retnorm_bwd(dy, o2, px):
    t_lat = o2.shape[1]

    def body(dy_ref, o_ref, g_ref, do_ref, dg_ref):
        o = o_ref[0] + o_ref[1]
        r = _rms(o)
        on = o * r
        g = g_ref[...]
        sg = _sigmoid(g)
        dy_ = dy_ref[...]
        dg_ref[...] = (dy_ * on * (sg * (1.0 + g * (1.0 - sg)))).astype(BF)
        do_ref[...] = _rms_bwd(dy_ * (g * sg), on, r)

    return pl.pallas_call(
        body, name="retnorm_bwd", grid=(t_lat // TM, RET_HEADS),
        in_specs=[pl.BlockSpec((TM, RET_DV), lambda i, h: (i, h)),
                  pl.BlockSpec((2, TM, RET_DV), lambda i, h: (0, i, h)),
                  pl.BlockSpec((TM, RET_DV), lambda i, h: (i, C_RG // RET_DV + h))],
        out_specs=(pl.BlockSpec((TM, RET_DV), lambda i, h: (i, h)),
                   pl.BlockSpec((TM, RET_DV), lambda i, h: (i, h))),
        out_shape=(SDS((t_lat, RET_HEADS * RET_DV), F32), SDS((t_lat, RET_HEADS * RET_DV), BF)),
        compiler_params=_cp(("parallel", "parallel")))(dy, o2, px)


def _norm_rope(x, w, cos, sin):
    xn = x * _rms(x) * w
    return xn * cos + _swap_pairs(xn) * sin


def _norm_rope_bwd(dy, x, w, cos, sin):
    dxn = dy * cos + _swap_pairs(dy * sin)
    r = _rms(x)
    xh = x * r
    return _rms_bwd(dxn * w, xh, r), jnp.sum(dxn * xh, axis=0, keepdims=True)


def _att_prep_q(px, cos_all, sin_all, qnw, t_lat):
    hd = ATT_HEAD_DIM
    wblk = ATT_REP * hd

    def body(x_ref, cos_ref, sin_ref, w_ref, o_ref):
        for r in range(ATT_REP):
            cols = slice(r * hd, (r + 1) * hd)
            o_ref[:, cols] = _norm_rope(x_ref[:, cols], w_ref[...], cos_ref[...], sin_ref[...]).astype(BF)

    return pl.pallas_call(
        body, name="att_prep_q", grid=(t_lat // TM, ATT_KV_HEADS),
        in_specs=[pl.BlockSpec((TM, wblk), lambda i, g: (i, C_AQ // wblk + g)),
                  pl.BlockSpec((TM, hd), lambda i, g: (i, 0)),
                  pl.BlockSpec((TM, hd), lambda i, g: (i, 0)),
                  pl.BlockSpec((1, hd), lambda i, g: (0, 0))],
        out_specs=pl.BlockSpec((TM, wblk), lambda i, g: (i, g)),
        out_shape=SDS((t_lat, ATT_HEADS * hd), BF),
        compiler_params=_cp(("parallel", "parallel")))(px, cos_all, sin_all, qnw)


def _att_prep_kv(px, cos_all, sin_all, knw):
    rows = px.shape[0]
    hd = ATT_HEAD_DIM
    kvw = ATT_KV_HEADS * hd

    def body(x_ref, cos_ref, sin_ref, w_ref, k_ref, v_ref):
        for g in range(ATT_KV_HEADS):
            cols = slice(g * hd, (g + 1) * hd)
            k_ref[:, cols] = _norm_rope(x_ref[:, cols], w_ref[...], cos_ref[...], sin_ref[...]).astype(BF)
        v_ref[...] = x_ref[:, kvw:].astype(BF)

    return pl.pallas_call(
        body, name="att_prep_kv", grid=(rows // TM,),
        in_specs=[pl.BlockSpec((TM, 2 * kvw), lambda i: (i, C_AK // (2 * kvw))),
                  pl.BlockSpec((TM, hd), lambda i: (i, 0)),
                  pl.BlockSpec((TM, hd), lambda i: (i, 0)),
                  pl.BlockSpec((1, hd), lambda i: (0, 0))],
        out_specs=(pl.BlockSpec((TM, kvw), lambda i: (i, 0)), pl.BlockSpec((TM, kvw), lambda i: (i, 0))),
        out_shape=(SDS((rows, kvw), BF), SDS((rows, kvw), BF)),
        compiler_params=_cp(("parallel",)))(px, cos_all, sin_all, knw)


def _att_kv_bwd(dk_tok, dv_tok, px, cos_all, sin_all, knw):
    rows = px.shape[0]
    hd = ATT_HEAD_DIM
    kvw = ATT_KV_HEADS * hd

    def body(dk_ref, dv_ref, x_ref, cos_ref, sin_ref, w_ref, o_ref, gw_ref):
        @pl.when(pl.program_id(0) == 0)
        def _():
            gw_ref[...] = jnp.zeros_like(gw_ref)

        for g in range(ATT_KV_HEADS):
            cols = slice(g * hd, (g + 1) * hd)
            dx, gw = _norm_rope_bwd(dk_ref[:, cols], x_ref[:, cols], w_ref[...], cos_ref[...], sin_ref[...])
            o_ref[:, cols] = dx.astype(BF)
            gw_ref[...] += gw
        o_ref[:, kvw:] = dv_ref[...].astype(BF)

    return pl.pallas_call(
        body, name="att_kv_bwd", grid=(rows // TM,),
        in_specs=[pl.BlockSpec((TM, kvw), lambda i: (i, 0)),
                  pl.BlockSpec((TM, kvw), lambda i: (i, 0)),
                  pl.BlockSpec((TM, 2 * kvw), lambda i: (i, C_AK // (2 * kvw))),
                  pl.BlockSpec((TM, hd), lambda i: (i, 0)),
                  pl.BlockSpec((TM, hd), lambda i: (i, 0)),
                  pl.BlockSpec((1, hd), lambda i: (0, 0))],
        out_specs=(pl.BlockSpec((TM, 2 * kvw), lambda i: (i, 0)), pl.BlockSpec((1, hd), lambda i: (0, 0))),
        out_shape=(SDS((rows, 2 * kvw), BF), SDS((1, hd), F32)),
        compiler_params=_cp(("arbitrary",)))(dk_tok, dv_tok, px, cos_all, sin_all, knw)


def _stack_heads(ref_or_val):
    hd = ATT_HEAD_DIM
    return jnp.concatenate([ref_or_val[:, r * hd:(r + 1) * hd] for r in range(ATT_REP)], axis=0)


def _att_scores(q4, kl, kc):
    scale = ATT_HEAD_DIM ** -0.5
    sl = _dot(q4, kl, 1, 1) * scale
    sc = _dot(q4, kc, 1, 1) * scale
    m = jnp.maximum(jnp.max(sl, axis=-1, keepdims=True), jnp.max(sc, axis=-1, keepdims=True))
    el = jnp.exp(sl - m)
    ec = jnp.exp(sc - m)
    denom = jnp.sum(el, axis=-1, keepdims=True) + jnp.sum(ec, axis=-1, keepdims=True)
    return el, ec, denom


def _att_fwd(qn, kn, vn, px, n_samp, seq, lc):
    hd = ATT_HEAD_DIM
    tq = 128
    nq = seq // tq
    wblk = ATT_REP * hd
    cb = n_samp * seq // lc
    t_lat = n_samp * seq

    def body(q_ref, kl_ref, kc_ref, vl_ref, vc_ref, g_ref, y_ref, o_ref):
        q4 = _stack_heads(q_ref)
        el, ec, denom = _att_scores(q4, kl_ref[...], kc_ref[...])
        o4 = (_dot(el, vl_ref[...]) + _dot(ec, vc_ref[...])) / denom
        for r in range(ATT_REP):
            cols = slice(r * hd, (r + 1) * hd)
            o = o4[r * tq:(r + 1) * tq]
            g = g_ref[:, cols]
            o_ref[:, cols] = o
            y_ref[:, cols] = (o * (g * _sigmoid(g))).astype(BF)

    return pl.pallas_call(
        body, name="att_fwd", grid=(n_samp, ATT_KV_HEADS, nq),
        in_specs=[pl.BlockSpec((tq, wblk), lambda b, g, i: (b * nq + i, g)),
                  pl.BlockSpec((seq, hd), lambda b, g, i: (b, g)),
                  pl.BlockSpec((lc, hd), lambda b, g, i: (cb + b, g)),
                  pl.BlockSpec((seq, hd), lambda b, g, i: (b, g)),
                  pl.BlockSpec((lc, hd), lambda b, g, i: (cb + b, g)),
                  pl.BlockSpec((tq, wblk), lambda b, g, i: (b * nq + i, C_AG // wblk + g))],
        out_specs=(pl.BlockSpec((tq, wblk), lambda b, g, i: (b * nq + i, g)),
                   pl.BlockSpec((tq, wblk), lambda b, g, i: (b * nq + i, g))),
        out_shape=(SDS((t_lat, ATT_HEADS * hd), BF), SDS((t_lat, ATT_HEADS * hd), F32)),
        compiler_params=_cp(("parallel", "parallel", "parallel"), 48))(qn, kn, kn, vn, vn, px)


def _att_bwd(qn, kn, vn, px, o_att, dy_att, cos_all, sin_all, qnw, n_samp, seq, lc):
    hd = ATT_HEAD_DIM
    tq = 128
    nq = seq // tq
    wblk = ATT_REP * hd
    cb = n_samp * seq // lc
    t_lat = n_samp * seq
    kvw = ATT_KV_HEADS * hd
    scale = hd ** -0.5

    def body(q_ref, kl_ref, kc_ref, vl_ref, vc_ref, g_ref, o_ref, dy_ref, x_ref, cos_ref, sin_ref, w_ref,
             dq_ref, dg_ref, dkl_ref, dkc_ref, dvl_ref, dvc_ref, gw_ref, akl, akc, avl, avc, aw):
        i = pl.program_id(2)

        @pl.when(i == 0)
        def _():
            akl[...] = jnp.zeros_like(akl)
            akc[...] = jnp.zeros_like(akc)
            avl[...] = jnp.zeros_like(avl)
            avc[...] = jnp.zeros_like(avc)
            aw[...] = jnp.zeros_like(aw)

        dos = []
        for r in range(ATT_REP):
            cols = slice(r * hd, (r + 1) * hd)
            g = g_ref[:, cols]
            sg = _sigmoid(g)
            dy = dy_ref[:, cols]
            dg_ref[:, cols] = (dy * o_ref[:, cols] * (sg * (1.0 + g * (1.0 - sg)))).astype(BF)
            dos.append(dy * (g * sg))
        do4 = jnp.concatenate(dos, axis=0)
        o4 = _stack_heads(o_ref)
        q4 = _stack_heads(q_ref)
        delta = jnp.sum(do4 * o4, axis=-1, keepdims=True)
        el, ec, denom = _att_scores(q4, kl_ref[...], kc_ref[...])
        inv = 1.0 / denom
        p_l = el * inv
        p_c = ec * inv
        dob = do4.astype(BF)
        avl[...] += _dot(p_l, dob, 0, 0)
        avc[...] += _dot(p_c, dob, 0, 0)
        ds_l = (p_l * (_dot(dob, vl_ref[...], 1, 1) - delta) * scale).astype(BF)
        ds_c = (p_c * (_dot(dob, vc_ref[...], 1, 1) - delta) * scale).astype(BF)
        dq4 = _dot(ds_l, kl_ref[...]) + _dot(ds_c, kc_ref[...])
        akl[...] += _dot(ds_l, q4, 0, 0)
        akc[...] += _dot(ds_c, q4, 0, 0)
        for r in range(ATT_REP):
            cols = slice(r * hd, (r + 1) * hd)
            dx, gw = _norm_rope_bwd(dq4[r * tq:(r + 1) * tq], x_ref[:, cols], w_ref[...], cos_ref[...], sin_ref[...])
            dq_ref[:, cols] = dx.astype(BF)
            aw[...] += gw

        @pl.when(i == nq - 1)
        def _():
            dkl_ref[...] = akl[...]
            dkc_ref[...] = akc[...]
            dvl_ref[...] = avl[...]
            dvc_ref[...] = avc[...]
            gw_ref[...] = aw[...]

    return pl.pallas_call(
        body, name="att_bwd", grid=(n_samp, ATT_KV_HEADS, nq),
        in_specs=[pl.BlockSpec((tq, wblk), lambda b, g, i: (b * nq + i, g)),
                  pl.BlockSpec((seq, hd), lambda b, g, i: (b, g)),
                  pl.BlockSpec((lc, hd), lambda b, g, i: (cb + b, g)),
                  pl.BlockSpec((seq, hd), lambda b, g, i: (b, g)),
                  pl.BlockSpec((lc, hd), lambda b, g, i: (cb + b, g)),
                  pl.BlockSpec((tq, wblk), lambda b, g, i: (b * nq + i, C_AG // wblk + g)),
                  pl.BlockSpec((tq, wblk), lambda b, g, i: (b * nq + i, g)),
                  pl.BlockSpec((tq, wblk), lambda b, g, i: (b * nq + i, g)),
                  pl.BlockSpec((tq, wblk), lambda b, g, i: (b * nq + i, C_AQ // wblk + g)),
                  pl.BlockSpec((tq, hd), lambda b, g, i: (b * nq + i, 0)),
                  pl.BlockSpec((tq, hd), lambda b, g, i: (b * nq + i, 0)),
                  pl.BlockSpec((1, hd), lambda b, g, i: (0, 0))],
        out_specs=(pl.BlockSpec((tq, wblk), lambda b, g, i: (b * nq + i, g)),
                   pl.BlockSpec((tq, wblk), lambda b, g, i: (b * nq + i, g)),
                   pl.BlockSpec((seq, hd), lambda b, g, i: (b, g)),
                   pl.BlockSpec((lc, hd), lambda b, g, i: (b, g)),
                   pl.BlockSpec((seq, hd), lambda b, g, i: (b, g)),
                   pl.BlockSpec((lc, hd), lambda b, g, i: (b, g)),
                   pl.BlockSpec((None, None, 1, hd), lambda b, g, i: (b, g, 0, 0))),
        out_shape=(SDS((t_lat, ATT_HEADS * hd), BF), SDS((t_lat, ATT_HEADS * hd), BF),
                   SDS((t_lat, kvw), F32), SDS((n_samp * lc, kvw), F32),
                   SDS((t_lat, kvw), F32), SDS((n_samp * lc, kvw), F32),
                   SDS((n_samp, ATT_KV_HEADS, 1, hd), F32)),
        scratch_shapes=[pltpu.VMEM((seq, hd), F32), pltpu.VMEM((lc, hd), F32),
                        pltpu.VMEM((seq, hd), F32), pltpu.VMEM((lc, hd), F32), pltpu.VMEM((1, hd), F32)],
        compiler_params=_cp(("parallel", "parallel", "arbitrary"), 56))(
            qn, kn, kn, vn, vn, px, o_att, dy_att, px, cos_all, sin_all, qnw)


def _merge(x_lat, target, y_ret, y_att, px, gate3, w_o_ret, w_o_att, w_out, tiles_per_sample):
    t_lat = x_lat.shape[0]
    tm = 256
    n_t = t_lat // tm
    per = tiles_per_sample * (TM // tm)
    d = D_MODEL
    rv = RET_HEADS * RET_DV
    n_samp = gate3.shape[0] - 1

    def body(x_ref, t_ref, yr_ref, ya_ref, mr0, mr1, ma0, ma1, gt_ref, wor_ref, woa_ref, wout_ref,
             gx_ref, dyr_ref, dya_ref, dmg_ref, loss_ref, dgt_ref, gwor_hbm, gwoa_hbm, gwout_hbm,
             aor, aoa, aout):
        i = pl.program_id(0)

        @pl.when(i == 0)
        def _():
            aor[...] = jnp.zeros_like(aor)
            aoa[...] = jnp.zeros_like(aoa)
            aout[...] = jnp.zeros_like(aout)
            loss_ref[...] = jnp.zeros_like(loss_ref)

        @pl.when(i % per == 0)
        def _():
            dgt_ref[...] = jnp.zeros_like(dgt_ref)

        yr = yr_ref[...]
        ya = ya_ref[...]
        a = jnp.dot(yr, wor_ref[...], preferred_element_type=F32)
        b = jnp.dot(ya, woa_ref[...], preferred_element_type=F32)
        sr = _sigmoid(jnp.concatenate([mr0[...], mr1[...]], axis=1))
        sa = _sigmoid(jnp.concatenate([ma0[...], ma1[...]], axis=1))
        yb = (sr * a + sa * b).astype(BF)
        out = jnp.dot(yb, wout_ref[...], preferred_element_type=F32)
        gate = gt_ref[...]
        err = x_ref[...] + gate * out - t_ref[...]
        loss_ref[...] += 0.5 * _sum_all(err * err) * (1.0 / d)
        dy_tok = err * (1.0 / d)
        gx_ref[...] = dy_tok
        dgt_ref[...] += jnp.sum(dy_tok * out, axis=0, keepdims=True)
        dout = (dy_tok * gate).astype(BF)
        aout[...] += _dot(yb, dout, 0, 0)
        dyy = _dot(dout, wout_ref[...], 1, 1)
        da = (dyy * sr).astype(BF)
        db = (dyy * sa).astype(BF)
        dmg_ref[:, :d] = (dyy * a * (sr * (1.0 - sr))).astype(BF)
        dmg_ref[:, d:] = (dyy * b * (sa * (1.0 - sa))).astype(BF)
        aor[...] += _dot(yr, da, 0, 0)
        aoa[...] += _dot(ya, db, 0, 0)
        dyr_ref[...] = _dot(da, wor_ref[...], 1, 1)
        dya_ref[...] = _dot(db, woa_ref[...], 1, 1)

        @pl.when(i == n_t - 1)
        def _():
            pltpu.sync_copy(aor, gwor_hbm)
            pltpu.sync_copy(aoa, gwoa_hbm)
            pltpu.sync_copy(aout, gwout_hbm)

    half = d // 2
    return pl.pallas_call(
        body, name="merge", grid=(n_t,),
        in_specs=[pl.BlockSpec((tm, d), lambda i: (i, 0)),
                  pl.BlockSpec((tm, d), lambda i: (i, 0)),
                  pl.BlockSpec((tm, rv), lambda i: (i, 0)),
                  pl.BlockSpec((tm, d), lambda i: (i, 0)),
                  pl.BlockSpec((tm, half), lambda i: (i, C_MR // half)),
                  pl.BlockSpec((tm, half), lambda i: (i, C_MR // half + 1)),
                  pl.BlockSpec((tm, half), lambda i: (i, C_MA // half)),
                  pl.BlockSpec((tm, half), lambda i: (i, C_MA // half + 1)),
                  pl.BlockSpec((None, 1, d), lambda i: (i // per, 0, 0)),
                  pl.BlockSpec((rv, d), lambda i: (0, 0)),
                  pl.BlockSpec((d, d), lambda i: (0, 0)),
                  pl.BlockSpec((d, d), lambda i: (0, 0))],
        out_specs=(pl.BlockSpec((tm, d), lambda i: (i, 0)),
                   pl.BlockSpec((tm, rv), lambda i: (i, 0)),
                   pl.BlockSpec((tm, d), lambda i: (i, 0)),
                   pl.BlockSpec((tm, 2 * d), lambda i: (i, 0)),
                   pl.BlockSpec((8, 128), lambda i: (0, 0)),
                   pl.BlockSpec((None, 1, d), lambda i: (i // per, 0, 0)),
                   ANY, ANY, ANY),
        out_shape=(SDS((t_lat, d), F32), SDS((t_lat, rv), F32), SDS((t_lat, d), F32), SDS((t_lat, 2 * d), BF),
                   SDS((8, 128), F32), SDS((n_samp, 1, d), F32),
                   SDS((rv, d), F32), SDS((d, d), F32), SDS((d, d), F32)),
        scratch_shapes=[pltpu.VMEM((rv, d), F32), pltpu.VMEM((d, d), F32), pltpu.VMEM((d, d), F32)],
        compiler_params=_cp(("arbitrary",), 56))(
            x_lat, target, y_ret, y_att, px, px, px, px, gate3, w_o_ret, w_o_att, w_out)


def _place():
    x, y, c = lax.axis_index("x"), lax.axis_index("y"), lax.axis_index("c")
    chips = [(1 - x, y), (x, 1 - y), (1 - x, 1 - y)]
    return x, y, c, chips


def _remote(src, dst, send_sem, recv_sem, to):
    return pltpu.make_async_remote_copy(src_ref=src, dst_ref=dst, send_sem=send_sem, recv_sem=recv_sem,
                                        device_id=to, device_id_type=MESH)


def _pieces(src, dst, max_bytes=1 << 20):
    lead, (rows, cols) = src.shape[:-2], src.shape[-2:]
    row_bytes = cols * jnp.dtype(src.dtype).itemsize
    per = rows
    while per * row_bytes > max_bytes and per % 32 == 0:
        per //= 2
    out = []
    for idx in ([()] if not lead else [(i,) for i in range(lead[0])]):
        for r0 in range(0, rows, per):
            sl = (*idx, pl.ds(r0, per), slice(None))
            out.append((src.at[sl], dst.at[sl]))
    return out


def _start_remote(src, dst, send_sem, recv_sem, to):
    for s_piece, d_piece in _pieces(src, dst):
        _remote(s_piece, d_piece, send_sem, recv_sem, to).start()
    return _remote(src, dst, send_sem, recv_sem, to)


def _start_local(src, dst, sem):
    for s_piece, d_piece in _pieces(src, dst):
        pltpu.make_async_copy(s_piece, d_piece, sem).start()
    return pltpu.make_async_copy(src, dst, sem)


def _all_gather_weights(shards):
    n = len(shards)

    def body(*refs):
        ins, outs = refs[:n], refs[n:2 * n]
        send_sems, recv_sems, local_sems = refs[2 * n:]
        x, y, c, chips = _place()
        sibling = (x, y, 1 - c)
        me = 2 * x + y

        def half(ref, s):
            h = ref.shape[1] // 2
            return ref.at[s, pl.ds(c * h, h), :]

        def own_half(ref):
            h = ref.shape[0] // 2
            return ref.at[pl.ds(c * h, h), :]

        local = [_start_local(ins[a], outs[a].at[me], local_sems.at[a]) for a in range(n)]
        first = []
        for a in range(n):
            for j, chip in enumerate(chips):
                k = a * 3 + j
                first.append(_start_remote(own_half(ins[a]), half(outs[a], me), send_sems.at[k], recv_sems.at[k],
                                           (*chip, c)))
        passed = []
        for a in range(n):
            for j, chip in enumerate(chips):
                k = a * 3 + j
                src_chip = 2 * chip[0] + chip[1]
                win = half(outs[a], src_chip)
                _remote(win, win, send_sems.at[k], recv_sems.at[k], (*chip, c)).wait_recv()
                passed.append(_start_remote(win, win, send_sems.at[3 * n + k], recv_sems.at[3 * n + k], sibling))
        for a in range(n):
            for j, chip in enumerate(chips):
                k = a * 3 + j
                src_chip = 2 * chip[0] + chip[1]
                h = outs[a].shape[1] // 2
                win = outs[a].at[src_chip, pl.ds((1 - c) * h, h), :]
                _remote(win, win, send_sems.at[3 * n + k], recv_sems.at[3 * n + k], sibling).wait_recv()
        for cp in first + passed:
            cp.wait_send()
        for cp in local:
            cp.wait()

    return pl.pallas_call(
        body, name="all_gather_weights",
        in_specs=[ANY] * n, out_specs=tuple([ANY] * n),
        out_shape=tuple(SDS((N_SHARD,) + s.shape, s.dtype) for s in shards),
        scratch_shapes=[pltpu.SemaphoreType.DMA((6 * n,)), pltpu.SemaphoreType.DMA((6 * n,)),
                        pltpu.SemaphoreType.DMA((n,))],
        compiler_params=_cp(has_side_effects=True))(*shards)


def _swap_halves(grads):
    n = len(grads)

    def body(*refs):
        ins, outs = refs[:n], refs[n:2 * n]
        send_sems, recv_sems, local_sems = refs[2 * n:]
        x, y, c, _ = _place()
        sibling = (x, y, 1 - c)

        def half(ref, which):
            h = ref.shape[1] // 2
            return ref.at[:, pl.ds(which * h, h), :]

        sends = [_start_remote(half(ins[a], 1 - c), outs[a].at[c], send_sems.at[a], recv_sems.at[a], sibling)
                 for a in range(n)]
        local = [_start_local(half(ins[a], c), outs[a].at[c], local_sems.at[a]) for a in range(n)]
        for a in range(n):
            _remote(half(ins[a], c), outs[a].at[1 - c], send_sems.at[a], recv_sems.at[a], sibling).wait_recv()
        for cp in sends:
            cp.wait_send()
        for cp in local:
            cp.wait()

    return pl.pallas_call(
        body, name="swap_halves",
        in_specs=[ANY] * n, out_specs=tuple([ANY] * n),
        out_shape=tuple(SDS((2, g.shape[0], g.shape[1] // 2, g.shape[2]), g.dtype) for g in grads),
        scratch_shapes=[pltpu.SemaphoreType.DMA((n,)), pltpu.SemaphoreType.DMA((n,)),
                        pltpu.SemaphoreType.DMA((n,))],
        compiler_params=_cp(has_side_effects=True))(*grads)


def _exchange_shards(parts):
    n = len(parts)

    def body(*refs):
        ins, outs = refs[:n], refs[n:2 * n]
        send_sems, recv_sems, local_sems = refs[2 * n:]
        x, y, c, chips = _place()
        me = 2 * x + y
        sends = []
        for a in range(n):
            for j, chip in enumerate(chips):
                k = a * 3 + j
                sends.append(_start_remote(ins[a].at[2 * chip[0] + chip[1]], outs[a].at[me],
                                           send_sems.at[k], recv_sems.at[k], (*chip, c)))
        local = [_start_local(ins[a].at[me], outs[a].at[me], local_sems.at[a]) for a in range(n)]
        for a in range(n):
            for j, chip in enumerate(chips):
                k = a * 3 + j
                slot = outs[a].at[2 * chip[0] + chip[1]]
                _remote(slot, slot, send_sems.at[k], recv_sems.at[k], (*chip, c)).wait_recv()
        for cp in sends:
            cp.wait_send()
        for cp in local:
            cp.wait()

    return pl.pallas_call(
        body, name="exchange_shards",
        in_specs=[ANY] * n, out_specs=tuple([ANY] * n),
        out_shape=tuple(SDS(p.shape, p.dtype) for p in parts),
        scratch_shapes=[pltpu.SemaphoreType.DMA((3 * n,)), pltpu.SemaphoreType.DMA((3 * n,)),
                        pltpu.SemaphoreType.DMA((n,))],
        compiler_params=_cp(has_side_effects=True))(*parts)


def _join_halves(halves):
    n = len(halves)

    def body(*refs):
        ins, outs = refs[:n], refs[n:2 * n]
        send_sems, recv_sems, local_sems = refs[2 * n:]
        x, y, c, _ = _place()
        sibling = (x, y, 1 - c)

        def win(ref, which):
            h = ref.shape[0] // 2
            return ref.at[pl.ds(which * h, h), :]

        sends = [_start_remote(ins[a], win(outs[a], c), send_sems.at[a], recv_sems.at[a], sibling)
                 for a in range(n)]
        local = [_start_local(ins[a], win(outs[a], c), local_sems.at[a]) for a in range(n)]
        for a in range(n):
            _remote(ins[a], win(outs[a], 1 - c), send_sems.at[a], recv_sems.at[a], sibling).wait_recv()
        for cp in sends:
            cp.wait_send()
        for cp in local:
            cp.wait()

    return pl.pallas_call(
        body, name="join_halves",
        in_specs=[ANY] * n, out_specs=tuple([ANY] * n),
        out_shape=tuple(SDS((2 * h.shape[0], h.shape[1]), h.dtype) for h in halves),
        scratch_shapes=[pltpu.SemaphoreType.DMA((n,)), pltpu.SemaphoreType.DMA((n,)),
                        pltpu.SemaphoreType.DMA((n,))],
        compiler_params=_cp(has_side_effects=True))(*halves)


def _sum_slots(p, shard_major_out):
    slots = p.shape[0]
    rest = p.shape[1:]
    tr = min(rest[-2], 256)

    def body(p_ref, o_ref):
        acc = p_ref[0]
        for s in range(1, slots):
            acc = acc + p_ref[s]
        o_ref[...] = acc

    if shard_major_out:
        return pl.pallas_call(
            body, name="sum_slots", grid=(rest[0], rest[1] // tr),
            in_specs=[pl.BlockSpec((slots, None, tr, rest[2]), lambda s, i: (0, s, i, 0))],
            out_specs=pl.BlockSpec((None, tr, rest[2]), lambda s, i: (s, i, 0)),
            out_shape=SDS(rest, p.dtype),
            compiler_params=_cp(("parallel", "parallel"), 40))(p)
    return pl.pallas_call(
        body, name="sum_slots", grid=(rest[0] // tr,),
        in_specs=[pl.BlockSpec((slots, tr, rest[1]), lambda i: (0, i, 0))],
        out_specs=pl.BlockSpec((tr, rest[1]), lambda i: (i, 0)),
        out_shape=SDS(rest, p.dtype),
        compiler_params=_cp(("parallel",), 40))(p)


def _all_reduce_small(block):
    rows, cols = block.shape
    n_dev = 8

    def body(x_ref, o_ref, buf, send_sems, recv_sems, local_sem):
        x, y, c, chips = _place()
        me, sibling = (x, y, c), (x, y, 1 - c)

        def slot(px_, py_, pc_):
            return buf.at[4 * px_ + 2 * py_ + pc_]

        def copy(k, who, to, src=None):
            return _remote(slot(*who) if src is None else src, slot(*who), send_sems.at[k], recv_sems.at[k], to)

        mine = pltpu.make_async_copy(x_ref, slot(*me), local_sem)
        mine.start()
        first = [copy(0, me, sibling, src=x_ref)]
        first += [copy(1 + j, me, (*chip, c), src=x_ref) for j, chip in enumerate(chips)]
        for cp in first:
            cp.start()
        passed = [copy(4 + j, (*chip, c), sibling) for j, chip in enumerate(chips)]
        for j, chip in enumerate(chips):
            copy(1 + j, (*chip, c), me).wait_recv()
            passed[j].start()
        copy(0, sibling, me).wait_recv()
        for j, chip in enumerate(chips):
            copy(4 + j, (*chip, 1 - c), me).wait_recv()
        for cp in first + passed:
            cp.wait_send()
        mine.wait()
        acc = buf[0]
        for s in range(1, n_dev):
            acc = acc + buf[s]
        o_ref[...] = acc

    return pl.pallas_call(
        body, name="all_reduce_small",
        in_specs=[pl.BlockSpec(memory_space=pltpu.VMEM)],
        out_specs=pl.BlockSpec(memory_space=pltpu.VMEM),
        out_shape=SDS((rows, cols), F32),
        scratch_shapes=[pltpu.VMEM((n_dev, rows, cols), F32), pltpu.SemaphoreType.DMA((7,)),
                        pltpu.SemaphoreType.DMA((7,)), pltpu.SemaphoreType.DMA],
        compiler_params=_cp(has_side_effects=True))(block)


def _adam_math(w, g, m, v):
    m = ADAM_B1 * m + (1.0 - ADAM_B1) * g
    v = ADAM_B2 * v + (1.0 - ADAM_B2) * (g * g)
    m_hat = m / (1.0 - ADAM_B1 ** ADAM_STEP)
    v_hat = v / (1.0 - ADAM_B2 ** ADAM_STEP)
    delta = -ADAM_LR * (m_hat / (jnp.sqrt(v_hat) + ADAM_EPS) + ADAM_WD * w)
    return delta, m, v


def _adamw(w, g, m, v):
    rows, cols = w.shape
    tr = min(rows, 256)

    def body(w_ref, g_ref, m_ref, v_ref, d_ref, nm_ref, nv_ref):
        d_ref[...], nm_ref[...], nv_ref[...] = _adam_math(w_ref[...], g_ref[...], m_ref[...], v_ref[...])

    spec = pl.BlockSpec((tr, cols), lambda i: (i, 0))
    return pl.pallas_call(
        body, name="adamw", grid=(rows // tr,), in_specs=[spec] * 4, out_specs=(spec,) * 3,
        out_shape=(SDS(w.shape, F32),) * 3, compiler_params=_cp(("parallel",), 40))(w, g, m, v)


def _adamw_small(w, g, m, v):
    def body(w_ref, g_ref, m_ref, v_ref, go_ref, d_ref, nm_ref, nv_ref):
        w = w_ref[...]
        g = g_ref[...]
        sub = lax.broadcasted_iota(jnp.int32, w.shape, 0)
        lane = lax.broadcasted_iota(jnp.int32, w.shape, 1)
        is_ret = jnp.logical_and(sub == 5, lane < 2 * RET_HEADS)
        u = jnp.exp(jnp.where(is_ret, w, -1.0) * jnp.log(2.0))
        g = jnp.where(is_ret, g * (-u * jnp.log(2.0) / (1.0 - u)), g)
        go_ref[...] = g
        d_ref[...], nm_ref[...], nv_ref[...] = _adam_math(w, g, m_ref[...], v_ref[...])

    return pl.pallas_call(body, name="adamw_small", out_shape=(SDS(w.shape, F32),) * 4)(w, g, m, v)


def _rope_tables(seq, n_samp, n_ctx_rows):
    rows = seq // GRID_W
    row = jnp.repeat(jnp.arange(rows, dtype=F32), GRID_W)
    col = jnp.tile(jnp.arange(GRID_W, dtype=F32), rows)
    half = ATT_HEAD_DIM // 2
    freqs = ROPE_THETA ** (-jnp.arange(0, half, 2, dtype=F32) / half)
    ang = jnp.concatenate([row[:, None] * freqs, col[:, None] * freqs], axis=-1)
    cos, sin = jnp.cos(ang), jnp.sin(ang)
    cos_f = jnp.repeat(cos, 2, axis=1)
    sin_s = jnp.stack([-sin, sin], axis=-1).reshape(seq, ATT_HEAD_DIM)
    cos_all = jnp.concatenate([jnp.tile(cos_f, (n_samp, 1)), jnp.ones((n_ctx_rows, ATT_HEAD_DIM), F32)], axis=0)
    sin_all = jnp.concatenate([jnp.tile(sin_s, (n_samp, 1)), jnp.zeros((n_ctx_rows, ATT_HEAD_DIM), F32)], axis=0)
    return cos_all, sin_all


def _pack_small(c_ctx, norm_w, b_ada, ret, qn, kn):
    d = D_MODEL
    row5 = jnp.concatenate([ret.reshape(-1), jnp.zeros((128 - 2 * RET_HEADS,), F32), qn.reshape(-1), kn.reshape(-1),
                            jnp.zeros((d - 384,), F32)])
    return jnp.concatenate([c_ctx.reshape(1, d), norm_w.reshape(1, d), b_ada.reshape(3, d), row5.reshape(1, d),
                            jnp.zeros((2, d), F32)], axis=0)


def _unpack_small(p):
    d = D_MODEL
    return (p[0], p[1:2], p[2:5].reshape(1, 3 * d), p[5, :2 * RET_HEADS].reshape(1, 2, RET_HEADS),
            p[5:6, 128:256], p[5:6, 256:384])


def _local_step(x, c, ctx, c_ctx, norm_w, b_ada, ret_log2_decay, q_norm_w, k_norm_w, loss_target,
                w_ada_g, w_in_g, w_o_ret, w_o_att, w_out):
    n_samp, seq, d = x.shape
    lc = ctx.shape[1]
    t_lat, t_ctx = n_samp * seq, n_samp * lc
    assert seq % TM == 0 and t_ctx % TM == 0 and t_lat % lc == 0 and seq % GRID_W == 0
    tps = seq // TM

    x_lat = x.reshape(t_lat, d)
    x_all = jnp.concatenate([x_lat, ctx.reshape(t_ctx, d)], axis=0)
    cvec8 = jnp.concatenate([c, c_ctx.reshape(1, d), jnp.zeros((8 - n_samp - 1, d), F32)], axis=0)
    lg = jnp.log1p(-jnp.exp2(ret_log2_decay.reshape(2, RET_HEADS)))
    cos_all, sin_all = _rope_tables(seq, n_samp, t_ctx)

    mod8 = _adaln_fwd(cvec8, w_ada_g, b_ada)
    mod3 = mod8[:n_samp + 1]
    shift3 = mod3[:, None, 0:d]
    scale3 = mod3[:, None, d:2 * d]
    gate3 = mod3[:, None, 2 * d:3 * d]

    px, hxt = _normproj(x_all, norm_w, scale3, shift3, w_in_g, tps, n_samp)

    states0 = _ctx_state_fwd(px, lg, n_samp, t_lat, lc)
    o2, saved = _ret_fwd(px, states0, lg, n_samp, seq)
    y_ret = _retnorm_fwd(o2, px)

    qn = _att_prep_q(px, cos_all, sin_all, q_norm_w, t_lat)
    kn, vn = _att_prep_kv(px, cos_all, sin_all, k_norm_w)
    y_att, o_att = _att_fwd(qn, kn, vn, px, n_samp, seq, lc)

    (gx_res, dy_ret, dy_att, dmg, loss8, dgate, g_w_o_ret, g_w_o_att, g_w_out) = _merge(
        x_lat, loss_target.reshape(t_lat, d), y_ret, y_att, px, gate3, w_o_ret, w_o_att, w_out, tps)

    d_att_q, d_att_g, dkl, dkc, dvl, dvc, gqw = _att_bwd(
        qn, kn, vn, px, o_att, dy_att, cos_all, sin_all, q_norm_w, n_samp, seq, lc)
    d_att_kv, gkw = _att_kv_bwd(jnp.concatenate([dkl, dkc], axis=0), jnp.concatenate([dvl, dvc], axis=0),
                                px, cos_all, sin_all, k_norm_w)

    do, d_ret_g = _retnorm_bwd(dy_ret, o2, px)
    dq2, dk2, dv2, dstates, dlg_lat = _ret_bwd(px, do, saved, lg, n_samp, seq)
    d_ret_q, d_ret_k, d_ret_v = _ret_combine(dq2, dk2, dv2)
    dc_ret_k, dc_ret_v, dlg_ctx = _ctx_state_bwd(px, dstates, lg, n_samp, t_lat, lc)

    dpx_lat = jnp.concatenate([d_ret_k, d_ret_v, d_att_kv[:t_lat], d_ret_q, d_ret_g, d_att_q, d_att_g, dmg], axis=1)
    dpx_ctx = jnp.concatenate([dc_ret_k, dc_ret_v, d_att_kv[t_lat:], jnp.zeros((t_ctx, IN_COLS - KV_COLS), BF)],
                              axis=1)
    dpx_all = jnp.concatenate([dpx_lat, dpx_ctx], axis=0)

    g_w_in = _gw_in(hxt, dpx_all)
    dhx = _dhx(dpx_all, w_in_g)
    grad_x, dshift, dscale, g_norm_w = _norm_bwd(x_all, dhx, gx_res, norm_w, scale3, shift3, tps, n_samp)

    dgate_all = jnp.concatenate([dgate, jnp.zeros((1, 1, d), F32)], axis=0)
    dmod3 = jnp.concatenate([dshift, dscale, dgate_all], axis=2).reshape(n_samp + 1, 3 * d)
    dmod8 = jnp.concatenate([dmod3, jnp.zeros((8 - n_samp - 1, 3 * d), F32)], axis=0)
    g_w_ada, g_b_ada, dc8 = _adaln_bwd(cvec8, dmod8, w_ada_g)

    g_lg = (jnp.sum(dlg_lat[:, :, :, 0, 0], axis=0).T
            + jnp.stack([jnp.sum(dlg_ctx[:, :, 0, 0], axis=0), jnp.sum(dlg_ctx[:, :, 1, 0], axis=0)], axis=0))
    small = _pack_small(dc8[n_samp], g_norm_w, g_b_ada, g_lg, jnp.sum(gqw, axis=(0, 1, 2)), gkw)
    return (loss8[0, 0], grad_x.reshape(n_samp, seq, d),
            (g_w_ada, g_w_in, g_w_o_ret, g_w_o_att, g_w_out), small)


def kernel(x, c, ctx, c_ctx, norm_w, w_ada, b_ada, w_in, ret_log2_decay, q_norm_w, k_norm_w, w_o_ret, w_o_att, w_out, loss_target, m_c_ctx, m_norm_w, m_w_ada, m_b_ada, m_w_in, m_ret_log2_decay, m_q_norm_w, m_k_norm_w, m_w_o_ret, m_w_o_att, m_w_out, v_c_ctx, v_norm_w, v_w_ada, v_b_ada, v_w_in, v_ret_log2_decay, v_q_norm_w, v_k_norm_w, v_w_o_ret, v_w_o_att, v_w_out):
    big_w = (w_ada[0], w_in[0], w_o_ret[0], w_o_att[0], w_out[0])
    big_m = (m_w_ada[0], m_w_in[0], m_w_o_ret[0], m_w_o_att[0], m_w_out[0])
    big_v = (v_w_ada[0], v_w_in[0], v_w_o_ret[0], v_w_o_att[0], v_w_out[0])

    gathered = _all_gather_weights(tuple(w.astype(BF) for w in big_w))
    w_ada_g, w_in_g = gathered[0], gathered[1]
    w_o_ret_f = gathered[2].reshape(-1, D_MODEL)
    w_o_att_f = gathered[3].reshape(-1, D_MODEL)
    w_out_f = gathered[4].reshape(-1, D_MODEL)

    loss_local, grad_x, big_g, small_g = _local_step(
        x, c, ctx, c_ctx, norm_w[0:1], b_ada[0:1], ret_log2_decay[0], q_norm_w[0:1], k_norm_w[0:1], loss_target,
        w_ada_g, w_in_g, w_o_ret_f, w_o_att_f, w_out_f)
    loss = lax.psum(loss_local, ("x", "y", "c"))

    g_sm = (big_g[0], big_g[1], big_g[2].reshape(N_SHARD, -1, D_MODEL), big_g[3].reshape(N_SHARD, -1, D_MODEL),
            big_g[4].reshape(N_SHARD, -1, D_MODEL))
    pairs = _swap_halves(g_sm)
    chip_sums = tuple(_sum_slots(p, True) for p in pairs)
    quads = _exchange_shards(chip_sums)
    halves = tuple(_sum_slots(q, False) for q in quads)
    big_grad = _join_halves(halves)

    small_grad_in = _all_reduce_small(small_g)
    small_w = _pack_small(c_ctx, norm_w, b_ada, ret_log2_decay, q_norm_w, k_norm_w)
    small_m = _pack_small(m_c_ctx, m_norm_w, m_b_ada, m_ret_log2_decay, m_q_norm_w, m_k_norm_w)
    small_v = _pack_small(v_c_ctx, v_norm_w, v_b_ada, v_ret_log2_decay, v_q_norm_w, v_k_norm_w)
    small_grad, small_delta, small_nm, small_nv = _adamw_small(small_w, small_grad_in, small_m, small_v)

    big_delta, big_nm, big_nv = [], [], []
    for w, g, m, v in zip(big_w, big_grad, big_m, big_v):
        dlt, nm, nv = _adamw(w, g, m, v)
        big_delta.append(dlt[None])
        big_nm.append(nm[None])
        big_nv.append(nv[None])
    big_grad = [g[None] for g in big_grad]

    def order(small_packed, big):
        s = _unpack_small(small_packed)
        return (s[0], s[1], big[0], s[2], big[1], s[3], s[4], s[5], big[2], big[3], big[4])

    return (loss, grad_x, *order(small_grad, big_grad), *order(small_delta, big_delta),
            *order(small_nm, big_nm), *order(small_nv, big_nv))
```

```python
import functools

import jax
import jax.numpy as jnp
from jax import lax
from jax.experimental import pallas as pl
from jax.experimental.pallas import tpu as pltpu

F32 = jnp.float32
BF = jnp.bfloat16
SDS = jax.ShapeDtypeStruct
MESH = pl.DeviceIdType.MESH
ANY = pl.BlockSpec(memory_space=pl.ANY)
SMEM = pl.BlockSpec(memory_space=pltpu.SMEM)

D_MODEL = 1024
GRID_W = 64
RET_HEADS = 4
RET_DK = 256
RET_DV = 512
RET_CHUNK = 128
ATT_HEADS = 8
ATT_KV_HEADS = 2
ATT_REP = ATT_HEADS // ATT_KV_HEADS
ATT_HEAD_DIM = 128
ROPE_THETA = 10000.0
NORM_EPS = 1e-6
IN_COLS = 10752
KV_COLS = 3584
C_RK, C_RV, C_AK, C_AV, C_RQ, C_RG, C_AQ, C_AG, C_MR, C_MA = 0, 1024, 3072, 3328, 3584, 4608, 6656, 7680, 8704, 9728
N_SHARD = 4
ADA_W = 3 * D_MODEL // N_SHARD
IN_W = IN_COLS // N_SHARD
IN_BLK = IN_W // 3
N_IN_BLK = IN_COLS // IN_BLK
TM = 512
ADAM_LR, ADAM_B1, ADAM_B2, ADAM_EPS, ADAM_WD, ADAM_STEP = 0.001, 0.9, 0.999, 1e-08, 0.01, 10
MIB = 1024 * 1024


def _cp(sem=None, vmem_mb=None, **kw):
    if sem is not None:
        kw["dimension_semantics"] = sem
    if vmem_mb is not None:
        kw["vmem_limit_bytes"] = vmem_mb * MIB
    return pltpu.CompilerParams(**kw)


def _dot(a, b, ca=1, cb=0):
    return lax.dot_general(a.astype(BF), b.astype(BF), (((ca,), (cb,)), ((), ())), preferred_element_type=F32)


def _sigmoid(x):
    return 1.0 / (1.0 + jnp.exp(-x))


def _sum_all(x):
    return jnp.sum(jnp.sum(x, axis=1, keepdims=True), axis=0, keepdims=True)


def _swap_pairs(x):
    ax = x.ndim - 1
    lane = lax.broadcasted_iota(jnp.int32, x.shape, ax)
    nxt = pltpu.roll(x, x.shape[ax] - 1, ax)
    prv = pltpu.roll(x, 1, ax)
    return jnp.where(lane % 2 == 0, nxt, prv)


def _rms(x):
    return lax.rsqrt(jnp.mean(x * x, axis=-1, keepdims=True) + NORM_EPS)


def _rms_bwd(dxh, xh, r):
    return r * (dxh - xh * jnp.mean(dxh * xh, axis=-1, keepdims=True))


def _adaln_fwd(cvec8, w_ada_g, b_ada):
    def body(c_ref, w_ref, b_ref, o_ref):
        cv = c_ref[...]
        sc = (cv * _sigmoid(cv)).astype(BF)
        for s in range(N_SHARD):
            cols = slice(s * ADA_W, (s + 1) * ADA_W)
            o_ref[:, cols] = jnp.dot(sc, w_ref[s], preferred_element_type=F32) + b_ref[:, cols]

    return pl.pallas_call(body, out_shape=SDS((8, 3 * D_MODEL), F32), name="adaln_fwd",
                          compiler_params=_cp(vmem_mb=32))(cvec8, w_ada_g, b_ada)


def _adaln_bwd(cvec8, dmod8, w_ada_g):
    def body(c_ref, d_ref, w_ref, gw_ref, gb_ref, dc_ref):
        cv = c_ref[...]
        sg = _sigmoid(cv)
        sc = cv * sg
        dm = d_ref[...]
        gb_ref[...] = jnp.sum(dm, axis=0, keepdims=True)
        dsc = jnp.zeros((8, D_MODEL), F32)
        for s in range(N_SHARD):
            cols = slice(s * ADA_W, (s + 1) * ADA_W)
            gw_ref[s] = _dot(sc, dm[:, cols], 0, 0)
            dsc = dsc + _dot(dm[:, cols], w_ref[s], 1, 1)
        dc_ref[...] = dsc * (sg * (1.0 + cv * (1.0 - sg)))

    return pl.pallas_call(
        body, name="adaln_bwd",
        out_shape=(SDS((N_SHARD, D_MODEL, ADA_W), F32), SDS((1, 3 * D_MODEL), F32), SDS((8, D_MODEL), F32)),
        compiler_params=_cp(vmem_mb=48))(cvec8, dmod8, w_ada_g)


def _normproj(x_all, norm_w, scale3, shift3, w_in_g, tiles_per_sample, n_samp):
    rows = x_all.shape[0]

    def samp(i):
        return jnp.minimum(i // tiles_per_sample, n_samp)

    def body(x_ref, nw_ref, sc_ref, sh_ref, w_ref, px_ref, hxt_ref, hx_s):
        @pl.when(pl.program_id(1) == 0)
        def _():
            x = x_ref[...]
            h = x * _rms(x) * nw_ref[...] * (1.0 + sc_ref[...]) + sh_ref[...]
            hx_s[...] = h.astype(BF)
            hxt_ref[...] = h.T.astype(BF)

        px_ref[...] = jnp.dot(hx_s[...], w_ref[...], preferred_element_type=F32)

    return pl.pallas_call(
        body, name="normproj", grid=(rows // TM, N_IN_BLK),
        in_specs=[pl.BlockSpec((TM, D_MODEL), lambda i, j: (i, 0)),
                  pl.BlockSpec((1, D_MODEL), lambda i, j: (0, 0)),
                  pl.BlockSpec((None, 1, D_MODEL), lambda i, j: (samp(i), 0, 0)),
                  pl.BlockSpec((None, 1, D_MODEL), lambda i, j: (samp(i), 0, 0)),
                  pl.BlockSpec((None, D_MODEL, IN_BLK), lambda i, j: (j // 3, 0, j % 3))],
        out_specs=(pl.BlockSpec((TM, IN_BLK), lambda i, j: (i, j)),
                   pl.BlockSpec((D_MODEL, TM), lambda i, j: (0, i))),
        out_shape=(SDS((rows, IN_COLS), F32), SDS((D_MODEL, rows), BF)),
        scratch_shapes=[pltpu.VMEM((TM, D_MODEL), BF)],
        compiler_params=_cp(("parallel", "arbitrary"), 40))(x_all, norm_w, scale3, shift3, w_in_g)


def _norm_bwd(x_all, dhx, gx_res, norm_w, scale3, shift3, tiles_per_sample, n_samp):
    rows = x_all.shape[0]
    n_lat = tiles_per_sample * n_samp
    del shift3

    def samp(i):
        return jnp.minimum(i // tiles_per_sample, n_samp)

    def lat(i):
        return jnp.minimum(i, n_lat - 1)

    def body(x_ref, dh_ref, gr_ref, nw_ref, sc_ref, gx_ref, dsh_ref, dsc_ref, dnw_ref):
        i = pl.program_id(0)
        x = x_ref[...]
        r = _rms(x)
        xh = x * r
        nw = nw_ref[...]
        dh = dh_ref[...]
        first = jnp.logical_or(i % tiles_per_sample == 0, i >= n_lat)

        @pl.when(first)
        def _():
            dsh_ref[...] = jnp.zeros_like(dsh_ref)
            dsc_ref[...] = jnp.zeros_like(dsc_ref)

        @pl.when(i == 0)
        def _():
            dnw_ref[...] = jnp.zeros_like(dnw_ref)

        dsh_ref[...] += jnp.sum(dh, axis=0, keepdims=True)
        dsc_ref[...] += jnp.sum(dh * (xh * nw), axis=0, keepdims=True)
        du = dh * (1.0 + sc_ref[...])
        dnw_ref[...] += jnp.sum(du * xh, axis=0, keepdims=True)

        @pl.when(i < n_lat)
        def _():
            gx_ref[...] = gr_ref[...] + _rms_bwd(du * nw, xh, r)

    return pl.pallas_call(
        body, name="norm_bwd", grid=(rows // TM,),
        in_specs=[pl.BlockSpec((TM, D_MODEL), lambda i: (i, 0)),
                  pl.BlockSpec((TM, D_MODEL), lambda i: (i, 0)),
                  pl.BlockSpec((TM, D_MODEL), lambda i: (lat(i), 0)),
                  pl.BlockSpec((1, D_MODEL), lambda i: (0, 0)),
                  pl.BlockSpec((None, 1, D_MODEL), lambda i: (samp(i), 0, 0))],
        out_specs=(pl.BlockSpec((TM, D_MODEL), lambda i: (lat(i), 0)),
                   pl.BlockSpec((None, 1, D_MODEL), lambda i: (samp(i), 0, 0)),
                   pl.BlockSpec((None, 1, D_MODEL), lambda i: (samp(i), 0, 0)),
                   pl.BlockSpec((1, D_MODEL), lambda i: (0, 0))),
        out_shape=(SDS((n_lat * TM, D_MODEL), F32), SDS((n_samp + 1, 1, D_MODEL), F32),
                   SDS((n_samp + 1, 1, D_MODEL), F32), SDS((1, D_MODEL), F32)),
        compiler_params=_cp(("arbitrary",), 40))(x_all, dhx, gx_res, norm_w, scale3)


def _gw_in(hxt, dpx_all):
    rows = dpx_all.shape[0]

    def body(h_ref, d_ref, o_ref):
        @pl.when(pl.program_id(1) == 0)
        def _():
            o_ref[...] = jnp.zeros_like(o_ref)

        o_ref[...] += jnp.dot(h_ref[...], d_ref[...], preferred_element_type=F32)

    return pl.pallas_call(
        body, name="gw_in", grid=(N_IN_BLK, rows // TM),
        in_specs=[pl.BlockSpec((D_MODEL, TM), lambda j, i: (0, i)),
                  pl.BlockSpec((TM, IN_BLK), lambda j, i: (i, j))],
        out_specs=pl.BlockSpec((None, D_MODEL, IN_BLK), lambda j, i: (j // 3, 0, j % 3)),
        out_shape=SDS((N_SHARD, D_MODEL, IN_W), F32),
        compiler_params=_cp(("parallel", "arbitrary"), 40))(hxt, dpx_all)


def _dhx(dpx_all, w_in_g):
    rows = dpx_all.shape[0]

    def body(d_ref, w_ref, o_ref):
        @pl.when(pl.program_id(1) == 0)
        def _():
            o_ref[...] = jnp.zeros_like(o_ref)

        o_ref[...] += lax.dot_general(d_ref[...], w_ref[...], (((1,), (1,)), ((), ())), preferred_element_type=F32)

    return pl.pallas_call(
        body, name="dhx", grid=(rows // TM, N_IN_BLK),
        in_specs=[pl.BlockSpec((TM, IN_BLK), lambda i, j: (i, j)),
                  pl.BlockSpec((None, D_MODEL, IN_BLK), lambda i, j: (j // 3, 0, j % 3))],
        out_specs=pl.BlockSpec((TM, D_MODEL), lambda i, j: (i, 0)),
        out_shape=SDS((rows, D_MODEL), F32),
        compiler_params=_cp(("parallel", "arbitrary"), 40))(dpx_all, w_in_g)


def _decays(lgv, d):
    c = RET_CHUNK
    ii = lax.broadcasted_iota(jnp.int32, (c, 1), 0).astype(F32)
    jj = lax.broadcasted_iota(jnp.int32, (1, c), 1).astype(F32)
    a_i = jnp.where(d == 0, ii, c - 1.0 - ii)
    a_j = jnp.where(d == 0, jj, c - 1.0 - jj)
    rel = a_i - a_j
    mask = jnp.where(rel >= 0, jnp.exp(lgv * jnp.maximum(rel, 0.0)), 0.0)
    qd = jnp.exp(lgv * (a_i + 1.0))
    kd = jnp.exp(lgv * (c - 1.0 - a_i))
    gc = jnp.exp(jnp.full((1, 1), lgv * c, F32))
    return a_i, rel, mask, qd, kd, gc


def _ctx_state_fwd(px, lg, n_samp, t_lat, lc):
    rb = t_lat // lc

    def body(lg_ref, k_ref, v_ref, o_ref):
        h = pl.program_id(1)
        k = k_ref[...] * (RET_DK ** -0.5)
        v = v_ref[...]
        pos = lax.broadcasted_iota(jnp.int32, (lc, 1), 0).astype(F32)
        o_ref[0] = _dot(k * jnp.exp(lg_ref[0, h] * (lc - 1.0 - pos)), v, 0, 0)
        o_ref[1] = _dot(k * jnp.exp(lg_ref[1, h] * pos), v, 0, 0)

    return pl.pallas_call(
        body, name="ctx_state_fwd", grid=(n_samp, RET_HEADS),
        in_specs=[SMEM,
                  pl.BlockSpec((lc, RET_DK), lambda b, h: (rb + b, C_RK // RET_DK + h)),
                  pl.BlockSpec((lc, RET_DV), lambda b, h: (rb + b, C_RV // RET_DV + h))],
        out_specs=pl.BlockSpec((None, 2, None, RET_DK, RET_DV), lambda b, h: (b, 0, h, 0, 0)),
        out_shape=SDS((n_samp, 2, RET_HEADS, RET_DK, RET_DV), F32),
        compiler_params=_cp(("parallel", "parallel")))(lg, px, px)


def _ctx_state_bwd(px, dstates, lg, n_samp, t_lat, lc):
    rb = t_lat // lc

    def body(lg_ref, k_ref, v_ref, ds_ref, dk_ref, dv_ref, dlg_ref):
        h = pl.program_id(1)
        k = k_ref[...] * (RET_DK ** -0.5)
        v = v_ref[...]
        pos = lax.broadcasted_iota(jnp.int32, (lc, 1), 0).astype(F32)
        e_f = lc - 1.0 - pos
        kw_f = k * jnp.exp(lg_ref[0, h] * e_f)
        kw_b = k * jnp.exp(lg_ref[1, h] * pos)
        y_f = _dot(v, ds_ref[0], 1, 1)
        y_b = _dot(v, ds_ref[1], 1, 1)
        dk = y_f * jnp.exp(lg_ref[0, h] * e_f) + y_b * jnp.exp(lg_ref[1, h] * pos)
        dk_ref[...] = (dk * (RET_DK ** -0.5)).astype(BF)
        dv_ref[...] = (_dot(kw_f, ds_ref[0]) + _dot(kw_b, ds_ref[1])).astype(BF)
        t_f = _sum_all(e_f * kw_f * y_f)
        t_b = _sum_all(pos * kw_b * y_b)
        sub = lax.broadcasted_iota(jnp.int32, (8, 128), 0)
        dlg_ref[...] = jnp.where(sub == 0, t_f, jnp.where(sub == 1, t_b, 0.0))

    return pl.pallas_call(
        body, name="ctx_state_bwd", grid=(n_samp, RET_HEADS),
        in_specs=[SMEM,
                  pl.BlockSpec((lc, RET_DK), lambda b, h: (rb + b, C_RK // RET_DK + h)),
                  pl.BlockSpec((lc, RET_DV), lambda b, h: (rb + b, C_RV // RET_DV + h)),
                  pl.BlockSpec((None, 2, None, RET_DK, RET_DV), lambda b, h: (b, 0, h, 0, 0))],
        out_specs=(pl.BlockSpec((lc, RET_DK), lambda b, h: (b, h)),
                   pl.BlockSpec((lc, RET_DV), lambda b, h: (b, h)),
                   pl.BlockSpec((None, None, 8, 128), lambda b, h: (b, h, 0, 0))),
        out_shape=(SDS((n_samp * lc, RET_HEADS * RET_DK), BF), SDS((n_samp * lc, RET_HEADS * RET_DV), BF),
                   SDS((n_samp, RET_HEADS, 8, 128), F32)),
        compiler_params=_cp(("parallel", "parallel")))(lg, px, px, dstates)


def _ret_fwd(px, states0, lg, n_samp, seq):
    c = RET_CHUNK
    nc = seq // c
    t_lat = n_samp * seq

    def chunk(n, d):
        return n + d * (nc - 1 - 2 * n)

    def body(lg_ref, q_ref, k_ref, v_ref, s0_ref, o_ref, st_ref, s_s):
        h, d, n = pl.program_id(1), pl.program_id(2), pl.program_id(3)

        @pl.when(n == 0)
        def _():
            s_s[...] = s0_ref[...]

        _, _, mask, qd, kd, gc = _decays(lg_ref[d, h], d)
        q = q_ref[...]
        k = k_ref[...] * (RET_DK ** -0.5)
        v = v_ref[...]
        s = s_s[...]
        st_ref[...] = s.astype(BF)
        sc = _dot(q, k, 1, 1) * mask
        o_ref[...] = _dot(sc, v) + _dot(q * qd, s)
        s_s[...] = s * gc + _dot(k * kd, v, 0, 0)

    return pl.pallas_call(
        body, name="ret_fwd", grid=(n_samp, RET_HEADS, 2, nc),
        in_specs=[SMEM,
                  pl.BlockSpec((c, RET_DK), lambda b, h, d, n: (b * nc + chunk(n, d), C_RQ // RET_DK + h)),
                  pl.BlockSpec((c, RET_DK), lambda b, h, d, n: (b * nc + chunk(n, d), C_RK // RET_DK + h)),
                  pl.BlockSpec((c, RET_DV), lambda b, h, d, n: (b * nc + chunk(n, d), C_RV // RET_DV + h)),
                  pl.BlockSpec((None, None, None, RET_DK, RET_DV), lambda b, h, d, n: (b, d, h, 0, 0))],
        out_specs=(pl.BlockSpec((None, c, RET_DV), lambda b, h, d, n: (d, b * nc + chunk(n, d), h)),
                   pl.BlockSpec((None, None, None, None, RET_DK, RET_DV), lambda b, h, d, n: (b, h, d, n, 0, 0))),
        out_shape=(SDS((2, t_lat, RET_HEADS * RET_DV), F32),
                   SDS((n_samp, RET_HEADS, 2, nc, RET_DK, RET_DV), BF)),
        scratch_shapes=[pltpu.VMEM((RET_DK, RET_DV), F32)],
        compiler_params=_cp(("parallel", "parallel", "parallel", "arbitrary")))(lg, px, px, px, states0)


def _ret_bwd(px, do, saved, lg, n_samp, seq):
    c = RET_CHUNK
    nc = seq // c
    t_lat = n_samp * seq

    def chunk(n, d):
        m = nc - 1 - n
        return m + d * (nc - 1 - 2 * m)

    def body(lg_ref, q_ref, k_ref, v_ref, do_ref, st_ref, dq_ref, dk_ref, dv_ref, ds0_ref, dlg_ref, ds_s, acc_s):
        h, d, n = pl.program_id(1), pl.program_id(2), pl.program_id(3)

        @pl.when(n == 0)
        def _():
            ds_s[...] = jnp.zeros_like(ds_s)
            acc_s[...] = jnp.zeros_like(acc_s)

        a_i, rel, mask, qd, kd, gc = _decays(lg_ref[d, h], d)
        q = q_ref[...]
        k = k_ref[...] * (RET_DK ** -0.5)
        qb, kb, vb, dob = q.astype(BF), k.astype(BF), v_ref[...].astype(BF), do_ref[...].astype(BF)
        sb = st_ref[...]
        ds = ds_s[...]
        dsb = ds.astype(BF)
        raw = _dot(qb, kb, 1, 1)
        sc = raw * mask
        dsc = _dot(dob, vb, 1, 1) * mask
        dscb = dsc.astype(BF)
        x = _dot(dob, sb, 1, 1)
        y = _dot(vb, dsb, 1, 1)
        qq = q * qd
        kk = k * kd
        dq_ref[...] = _dot(dscb, kb) + x * qd
        dk_ref[...] = _dot(dscb, qb, 0, 0) + y * kd
        dv_ref[...] = _dot(sc, dob, 0, 0) + _dot(kk, dsb)
        ds_new = ds * gc + _dot(qq, dob, 0, 0)
        t = (_sum_all(dsc * raw * rel) + _sum_all((a_i + 1.0) * qq * x)
             + _sum_all((c - 1.0 - a_i) * kk * y) + c * gc * _sum_all(ds * sb.astype(F32)))
        acc_s[...] += t
        ds_s[...] = ds_new

        @pl.when(n == nc - 1)
        def _():
            ds0_ref[...] = ds_new
            dlg_ref[...] = acc_s[...]

    return pl.pallas_call(
        body, name="ret_bwd", grid=(n_samp, RET_HEADS, 2, nc),
        in_specs=[SMEM,
                  pl.BlockSpec((c, RET_DK), lambda b, h, d, n: (b * nc + chunk(n, d), C_RQ // RET_DK + h)),
                  pl.BlockSpec((c, RET_DK), lambda b, h, d, n: (b * nc + chunk(n, d), C_RK // RET_DK + h)),
                  pl.BlockSpec((c, RET_DV), lambda b, h, d, n: (b * nc + chunk(n, d), C_RV // RET_DV + h)),
                  pl.BlockSpec((c, RET_DV), lambda b, h, d, n: (b * nc + chunk(n, d), h)),
                  pl.BlockSpec((None, None, None, None, RET_DK, RET_DV),
                               lambda b, h, d, n: (b, h, d, nc - 1 - n, 0, 0))],
        out_specs=(pl.BlockSpec((None, c, RET_DK), lambda b, h, d, n: (d, b * nc + chunk(n, d), h)),
                   pl.BlockSpec((None, c, RET_DK), lambda b, h, d, n: (d, b * nc + chunk(n, d), h)),
                   pl.BlockSpec((None, c, RET_DV), lambda b, h, d, n: (d, b * nc + chunk(n, d), h)),
                   pl.BlockSpec((None, None, None, RET_DK, RET_DV), lambda b, h, d, n: (b, d, h, 0, 0)),
                   pl.BlockSpec((None, None, None, 8, 128), lambda b, h, d, n: (b, h, d, 0, 0))),
        out_shape=(SDS((2, t_lat, RET_HEADS * RET_DK), F32), SDS((2, t_lat, RET_HEADS * RET_DK), F32),
                   SDS((2, t_lat, RET_HEADS * RET_DV), F32),
                   SDS((n_samp, 2, RET_HEADS, RET_DK, RET_DV), F32), SDS((n_samp, RET_HEADS, 2, 8, 128), F32)),
        scratch_shapes=[pltpu.VMEM((RET_DK, RET_DV), F32), pltpu.VMEM((8, 128), F32)],
        compiler_params=_cp(("parallel", "parallel", "parallel", "arbitrary")))(lg, px, px, px, do, saved)


def _ret_combine(dq2, dk2, dv2):
    t_lat = dq2.shape[1]
    tm = 256

    def body(q_ref, k_ref, v_ref, oq_ref, ok_ref, ov_ref):
        oq_ref[...] = (q_ref[0] + q_ref[1]).astype(BF)
        ok_ref[...] = ((k_ref[0] + k_ref[1]) * (RET_DK ** -0.5)).astype(BF)
        ov_ref[...] = (v_ref[0] + v_ref[1]).astype(BF)

    wq, wv = RET_HEADS * RET_DK, RET_HEADS * RET_DV
    return pl.pallas_call(
        body, name="ret_combine", grid=(t_lat // tm,),
        in_specs=[pl.BlockSpec((2, tm, wq), lambda i: (0, i, 0)),
                  pl.BlockSpec((2, tm, wq), lambda i: (0, i, 0)),
                  pl.BlockSpec((2, tm, wv), lambda i: (0, i, 0))],
        out_specs=(pl.BlockSpec((tm, wq), lambda i: (i, 0)),
                   pl.BlockSpec((tm, wq), lambda i: (i, 0)),
                   pl.BlockSpec((tm, wv), lambda i: (i, 0))),
        out_shape=(SDS((t_lat, wq), BF), SDS((t_lat, wq), BF), SDS((t_lat, wv), BF)),
        compiler_params=_cp(("parallel",), 40))(dq2, dk2, dv2)


def _retnorm_fwd(o2, px):
    t_lat = o2.shape[1]

    def body(o_ref, g_ref, y_ref):
        o = o_ref[0] + o_ref[1]
        g = g_ref[...]
        y_ref[...] = (o * _rms(o) * (g * _sigmoid(g))).astype(BF)

    return pl.pallas_call(
        body, name="retnorm_fwd", grid=(t_lat // TM, RET_HEADS),
        in_specs=[pl.BlockSpec((2, TM, RET_DV), lambda i, h: (0, i, h)),
                  pl.BlockSpec((TM, RET_DV), lambda i, h: (i, C_RG // RET_DV + h))],
        out_specs=pl.BlockSpec((TM, RET_DV), lambda i, h: (i, h)),
        out_shape=SDS((t_lat, RET_HEADS * RET_DV), BF),
        compiler_params=_cp(("parallel", "parallel")))(o2, px)


def _retnorm_bwd(dy, o2, px):
    t_lat = o2.shape[1]

    def body(dy_ref, o_ref, g_ref, do_ref, dg_ref):
        o = o_ref[0] + o_ref[1]
        r = _rms(o)
        on = o * r
        g = g_ref[...]
        sg = _sigmoid(g)
        dy_ = dy_ref[...]
        dg_ref[...] = (dy_ * on * (sg * (1.0 + g * (1.0 - sg)))).astype(BF)
        do_ref[...] = _rms_bwd(dy_ * (g * sg), on, r)

    return pl.pallas_call(
        body, name="retnorm_bwd", grid=(t_lat // TM, RET_HEADS),
        in_specs=[pl.BlockSpec((TM, RET_DV), lambda i, h: (i, h)),
                  pl.BlockSpec((2, TM, RET_DV), lambda i, h: (0, i, h)),
                  pl.BlockSpec((TM, RET_DV), lambda i, h: (i, C_RG // RET_DV + h))],
        out_specs=(pl.BlockSpec((TM, RET_DV), lambda i, h: (i, h)),
                   pl.BlockSpec((TM, RET_DV), lambda i, h: (i, h))),
        out_shape=(SDS((t_lat, RET_HEADS * RET_DV), F32), SDS((t_lat, RET_HEADS * RET_DV), BF)),
        compiler_params=_cp(("parallel", "parallel")))(dy, o2, px)


def _norm_rope(x, w, cos, sin):
    xn = x * _rms(x) * w
    return xn * cos + _swap_pairs(xn) * sin


def _norm_rope_bwd(dy, x, w, cos, sin):
    dxn = dy * cos + _swap_pairs(dy * sin)
    r = _rms(x)
    xh = x * r
    return _rms_bwd(dxn * w, xh, r), jnp.sum(dxn * xh, axis=0, keepdims=True)


def _att_prep_q(px, cos_all, sin_all, qnw, t_lat):
    hd = ATT_HEAD_DIM
    wblk = ATT_REP * hd

    def body(x_ref, cos_ref, sin_ref, w_ref, o_ref):
        for r in range(ATT_REP):
            cols = slice(r * hd, (r + 1) * hd)
            o_ref[:, cols] = _norm_rope(x_ref[:, cols], w_ref[...], cos_ref[...], sin_ref[...]).astype(BF)

    return pl.pallas_call(
        body, name="att_prep_q", grid=(t_lat // TM, ATT_KV_HEADS),
        in_specs=[pl.BlockSpec((TM, wblk), lambda i, g: (i, C_AQ // wblk + g)),
                  pl.BlockSpec((TM, hd), lambda i, g: (i, 0)),
                  pl.BlockSpec((TM, hd), lambda i, g: (i, 0)),
                  pl.BlockSpec((1, hd), lambda i, g: (0, 0))],
        out_specs=pl.BlockSpec((TM, wblk), lambda i, g: (i, g)),
        out_shape=SDS((t_lat, ATT_HEADS * hd), BF),
        compiler_params=_cp(("parallel", "parallel")))(px, cos_all, sin_all, qnw)


def _att_prep_kv(px, cos_all, sin_all, knw):
    rows = px.shape[0]
    hd = ATT_HEAD_DIM
    kvw = ATT_KV_HEADS * hd

    def body(x_ref, cos_ref, sin_ref, w_ref, k_ref, v_ref):
        for g in range(ATT_KV_HEADS):
            cols = slice(g * hd, (g + 1) * hd)
            k_ref[:, cols] = _norm_rope(x_ref[:, cols], w_ref[...], cos_ref[...], sin_ref[...]).astype(BF)
        v_ref[...] = x_ref[:, kvw:].astype(BF)

    return pl.pallas_call(
        body, name="att_prep_kv", grid=(rows // TM,),
        in_specs=[pl.BlockSpec((TM, 2 * kvw), lambda i: (i, C_AK // (2 * kvw))),
                  pl.BlockSpec((TM, hd), lambda i: (i, 0)),
                  pl.BlockSpec((TM, hd), lambda i: (i, 0)),
                  pl.BlockSpec((1, hd), lambda i: (0, 0))],
        out_specs=(pl.BlockSpec((TM, kvw), lambda i: (i, 0)), pl.BlockSpec((TM, kvw), lambda i: (i, 0))),
        out_shape=(SDS((rows, kvw), BF), SDS((rows, kvw), BF)),
        compiler_params=_cp(("parallel",)))(px, cos_all, sin_all, knw)


def _att_kv_bwd(dk_tok, dv_tok, px, cos_all, sin_all, knw):
    rows = px.shape[0]
    hd = ATT_HEAD_DIM
    kvw = ATT_KV_HEADS * hd

    def body(dk_ref, dv_ref, x_ref, cos_ref, sin_ref, w_ref, o_ref, gw_ref):
        @pl.when(pl.program_id(0) == 0)
        def _():
            gw_ref[...] = jnp.zeros_like(gw_ref)

        for g in range(ATT_KV_HEADS):
            cols = slice(g * hd, (g + 1) * hd)
            dx, gw = _norm_rope_bwd(dk_ref[:, cols], x_ref[:, cols], w_ref[...], cos_ref[...], sin_ref[...])
            o_ref[:, cols] = dx.astype(BF)
            gw_ref[...] += gw
        o_ref[:, kvw:] = dv_ref[...].astype(BF)

    return pl.pallas_call(
        body, name="att_kv_bwd", grid=(rows // TM,),
        in_specs=[pl.BlockSpec((TM, kvw), lambda i: (i, 0)),
                  pl.BlockSpec((TM, kvw), lambda i: (i, 0)),
                  pl.BlockSpec((TM, 2 * kvw), lambda i: (i, C_AK // (2 * kvw))),
                  pl.BlockSpec((TM, hd), lambda i: (i, 0)),
                  pl.BlockSpec((TM, hd), lambda i: (i, 0)),
                  pl.BlockSpec((1, hd), lambda i: (0, 0))],
        out_specs=(pl.BlockSpec((TM, 2 * kvw), lambda i: (i, 0)), pl.BlockSpec((1, hd), lambda i: (0, 0))),
        out_shape=(SDS((rows, 2 * kvw), BF), SDS((1, hd), F32)),
        compiler_params=_cp(("arbitrary",)))(dk_tok, dv_tok, px, cos_all, sin_all, knw)


def _stack_heads(ref_or_val):
    hd = ATT_HEAD_DIM
    return jnp.concatenate([ref_or_val[:, r * hd:(r + 1) * hd] for r in range(ATT_REP)], axis=0)


def _att_scores(q4, kl, kc):
    scale = ATT_HEAD_DIM ** -0.5
    sl = _dot(q4, kl, 1, 1) * scale
    sc = _dot(q4, kc, 1, 1) * scale
    m = jnp.maximum(jnp.max(sl, axis=-1, keepdims=True), jnp.max(sc, axis=-1, keepdims=True))
    el = jnp.exp(sl - m)
    ec = jnp.exp(sc - m)
    denom = jnp.sum(el, axis=-1, keepdims=True) + jnp.sum(ec, axis=-1, keepdims=True)
    return el, ec, denom


def _att_fwd(qn, kn, vn, px, n_samp, seq, lc):
    hd = ATT_HEAD_DIM
    tq = 128
    nq = seq // tq
    wblk = ATT_REP * hd
    cb = n_samp * seq // lc
    t_lat = n_samp * seq

    def body(q_ref, kl_ref, kc_ref, vl_ref, vc_ref, g_ref, y_ref, o_ref):
        q4 = _stack_heads(q_ref)
        el, ec, denom = _att_scores(q4, kl_ref[...], kc_ref[...])
        o4 = (_dot(el, vl_ref[...]) + _dot(ec, vc_ref[...])) / denom
        for r in range(ATT_REP):
            cols = slice(r * hd, (r + 1) * hd)
            o = o4[r * tq:(r + 1) * tq]
            g = g_ref[:, cols]
            o_ref[:, cols] = o
            y_ref[:, cols] = (o * (g * _sigmoid(g))).astype(BF)

    return pl.pallas_call(
        body, name="att_fwd", grid=(n_samp, ATT_KV_HEADS, nq),
        in_specs=[pl.BlockSpec((tq, wblk), lambda b, g, i: (b * nq + i, g)),
                  pl.BlockSpec((seq, hd), lambda b, g, i: (b, g)),
                  pl.BlockSpec((lc, hd), lambda b, g, i: (cb + b, g)),
                  pl.BlockSpec((seq, hd), lambda b, g, i: (b, g)),
                  pl.BlockSpec((lc, hd), lambda b, g, i: (cb + b, g)),
                  pl.BlockSpec((tq, wblk), lambda b, g, i: (b * nq + i, C_AG // wblk + g))],
        out_specs=(pl.BlockSpec((tq, wblk), lambda b, g, i: (b * nq + i, g)),
                   pl.BlockSpec((tq, wblk), lambda b, g, i: (b * nq + i, g))),
        out_shape=(SDS((t_lat, ATT_HEADS * hd), BF), SDS((t_lat, ATT_HEADS * hd), F32)),
        compiler_params=_cp(("parallel", "parallel", "parallel"), 48))(qn, kn, kn, vn, vn, px)


def _att_bwd(qn, kn, vn, px, o_att, dy_att, cos_all, sin_all, qnw, n_samp, seq, lc):
    hd = ATT_HEAD_DIM
    tq = 128
    nq = seq // tq
    wblk = ATT_REP * hd
    cb = n_samp * seq // lc
    t_lat = n_samp * seq
    kvw = ATT_KV_HEADS * hd
    scale = hd ** -0.5

    def body(q_ref, kl_ref, kc_ref, vl_ref, vc_ref, g_ref, o_ref, dy_ref, x_ref, cos_ref, sin_ref, w_ref,
             dq_ref, dg_ref, dkl_ref, dkc_ref, dvl_ref, dvc_ref, gw_ref, akl, akc, avl, avc, aw):
        i = pl.program_id(2)

        @pl.when(i == 0)
        def _():
            akl[...] = jnp.zeros_like(akl)
            akc[...] = jnp.zeros_like(akc)
            avl[...] = jnp.zeros_like(avl)
            avc[...] = jnp.zeros_like(avc)
            aw[...] = jnp.zeros_like(aw)

        dos = []
        for r in range(ATT_REP):
            cols = slice(r * hd, (r + 1) * hd)
            g = g_ref[:, cols]
            sg = _sigmoid(g)
            dy = dy_ref[:, cols]
            dg_ref[:, cols] = (dy * o_ref[:, cols] * (sg * (1.0 + g * (1.0 - sg)))).astype(BF)
            dos.append(dy * (g * sg))
        do4 = jnp.concatenate(dos, axis=0)
        o4 = _stack_heads(o_ref)
        q4 = _stack_heads(q_ref)
        delta = jnp.sum(do4 * o4, axis=-1, keepdims=True)
        el, ec, denom = _att_scores(q4, kl_ref[...], kc_ref[...])
        inv = 1.0 / denom
        p_l = el * inv
        p_c = ec * inv
        dob = do4.astype(BF)
        avl[...] += _dot(p_l, dob, 0, 0)
        avc[...] += _dot(p_c, dob, 0, 0)
        ds_l = (p_l * (_dot(dob, vl_ref[...], 1, 1) - delta) * scale).astype(BF)
        ds_c = (p_c * (_dot(dob, vc_ref[...], 1, 1) - delta) * scale).astype(BF)
        dq4 = _dot(ds_l, kl_ref[...]) + _dot(ds_c, kc_ref[...])
        akl[...] += _dot(ds_l, q4, 0, 0)
        akc[...] += _dot(ds_c, q4, 0, 0)
        for r in range(ATT_REP):
            cols = slice(r * hd, (r + 1) * hd)
            dx, gw = _norm_rope_bwd(dq4[r * tq:(r + 1) * tq], x_ref[:, cols], w_ref[...], cos_ref[...], sin_ref[...])
            dq_ref[:, cols] = dx.astype(BF)
            aw[...] += gw

        @pl.when(i == nq - 1)
        def _():
            dkl_ref[...] = akl[...]
            dkc_ref[...] = akc[...]
            dvl_ref[...] = avl[...]
            dvc_ref[...] = avc[...]
            gw_ref[...] = aw[...]

    return pl.pallas_call(
        body, name="att_bwd", grid=(n_samp, ATT_KV_HEADS, nq),
        in_specs=[pl.BlockSpec((tq, wblk), lambda b, g, i: (b * nq + i, g)),
                  pl.BlockSpec((seq, hd), lambda b, g, i: (b, g)),
                  pl.BlockSpec((lc, hd), lambda b, g, i: (cb + b, g)),
                  pl.BlockSpec((seq, hd), lambda b, g, i: (b, g)),
                  pl.BlockSpec((lc, hd), lambda b, g, i: (cb + b, g)),
                  pl.BlockSpec((tq, wblk), lambda b, g, i: (b * nq + i, C_AG // wblk + g)),
                  pl.BlockSpec((tq, wblk), lambda b, g, i: (b * nq + i, g)),
                  pl.BlockSpec((tq, wblk), lambda b, g, i: (b * nq + i, g)),
                  pl.BlockSpec((tq, wblk), lambda b, g, i: (b * nq + i, C_AQ // wblk + g)),
                  pl.BlockSpec((tq, hd), lambda b, g, i: (b * nq + i, 0)),
                  pl.BlockSpec((tq, hd), lambda b, g, i: (b * nq + i, 0)),
                  pl.BlockSpec((1, hd), lambda b, g, i: (0, 0))],
        out_specs=(pl.BlockSpec((tq, wblk), lambda b, g, i: (b * nq + i, g)),
                   pl.BlockSpec((tq, wblk), lambda b, g, i: (b * nq + i, g)),
                   pl.BlockSpec((seq, hd), lambda b, g, i: (b, g)),
                   pl.BlockSpec((lc, hd), lambda b, g, i: (b, g)),
                   pl.BlockSpec((seq, hd), lambda b, g, i: (b, g)),
                   pl.BlockSpec((lc, hd), lambda b, g, i: (b, g)),
                   pl.BlockSpec((None, None, 1, hd), lambda b, g, i: (b, g, 0, 0))),
        out_shape=(SDS((t_lat, ATT_HEADS * hd), BF), SDS((t_lat, ATT_HEADS * hd), BF),
                   SDS((t_lat, kvw), F32), SDS((n_samp * lc, kvw), F32),
                   SDS((t_lat, kvw), F32), SDS((n_samp * lc, kvw), F32),
                   SDS((n_samp, ATT_KV_HEADS, 1, hd), F32)),
        scratch_shapes=[pltpu.VMEM((seq, hd), F32), pltpu.VMEM((lc, hd), F32),
                        pltpu.VMEM((seq, hd), F32), pltpu.VMEM((lc, hd), F32), pltpu.VMEM((1, hd), F32)],
        compiler_params=_cp(("parallel", "parallel", "arbitrary"), 56))(
            qn, kn, kn, vn, vn, px, o_att, dy_att, px, cos_all, sin_all, qnw)


def _merge(x_lat, target, y_ret, y_att, px, gate3, w_o_ret, w_o_att, w_out, tiles_per_sample):
    t_lat = x_lat.shape[0]
    tm = 256
    n_t = t_lat // tm
    per = tiles_per_sample * (TM // tm)
    d = D_MODEL
    rv = RET_HEADS * RET_DV
    n_samp = gate3.shape[0] - 1

    def body(x_ref, t_ref, yr_ref, ya_ref, mr0, mr1, ma0, ma1, gt_ref, wor_ref, woa_ref, wout_ref,
             gx_ref, dyr_ref, dya_ref, dmg_ref, loss_ref, dgt_ref, gwor_hbm, gwoa_hbm, gwout_hbm,
             aor, aoa, aout):
        i = pl.program_id(0)

        @pl.when(i == 0)
        def _():
            aor[...] = jnp.zeros_like(aor)
            aoa[...] = jnp.zeros_like(aoa)
            aout[...] = jnp.zeros_like(aout)
            loss_ref[...] = jnp.zeros_like(loss_ref)

        @pl.when(i % per == 0)
        def _():
            dgt_ref[...] = jnp.zeros_like(dgt_ref)

        yr = yr_ref[...]
        ya = ya_ref[...]
        a = jnp.dot(yr, wor_ref[...], preferred_element_type=F32)
        b = jnp.dot(ya, woa_ref[...], preferred_element_type=F32)
        sr = _sigmoid(jnp.concatenate([mr0[...], mr1[...]], axis=1))
        sa = _sigmoid(jnp.concatenate([ma0[...], ma1[...]], axis=1))
        yb = (sr * a + sa * b).astype(BF)
        out = jnp.dot(yb, wout_ref[...], preferred_element_type=F32)
        gate = gt_ref[...]
        err = x_ref[...] + gate * out - t_ref[...]
        loss_ref[...] += 0.5 * _sum_all(err * err) * (1.0 / d)
        dy_tok = err * (1.0 / d)
        gx_ref[...] = dy_tok
        dgt_ref[...] += jnp.sum(dy_tok * out, axis=0, keepdims=True)
        dout = (dy_tok * gate).astype(BF)
        aout[...] += _dot(yb, dout, 0, 0)
        dyy = _dot(dout, wout_ref[...], 1, 1)
        da = (dyy * sr).astype(BF)
        db = (dyy * sa).astype(BF)
        dmg_ref[:, :d] = (dyy * a * (sr * (1.0 - sr))).astype(BF)
        dmg_ref[:, d:] = (dyy * b * (sa * (1.0 - sa))).astype(BF)
        aor[...] += _dot(yr, da, 0, 0)
        aoa[...] += _dot(ya, db, 0, 0)
        dyr_ref[...] = _dot(da, wor_ref[...], 1, 1)
        dya_ref[...] = _dot(db, woa_ref[...], 1, 1)

        @pl.when(i == n_t - 1)
        def _():
            pltpu.sync_copy(aor, gwor_hbm)
            pltpu.sync_copy(aoa, gwoa_hbm)
            pltpu.sync_copy(aout, gwout_hbm)

    half = d // 2
    return pl.pallas_call(
        body, name="merge", grid=(n_t,),
        in_specs=[pl.BlockSpec((tm, d), lambda i: (i, 0)),
                  pl.BlockSpec((tm, d), lambda i: (i, 0)),
                  pl.BlockSpec((tm, rv), lambda i: (i, 0)),
                  pl.BlockSpec((tm, d), lambda i: (i, 0)),
                  pl.BlockSpec((tm, half), lambda i: (i, C_MR // half)),
                  pl.BlockSpec((tm, half), lambda i: (i, C_MR // half + 1)),
                  pl.BlockSpec((tm, half), lambda i: (i, C_MA // half)),
                  pl.BlockSpec((tm, half), lambda i: (i, C_MA // half + 1)),
                  pl.BlockSpec((None, 1, d), lambda i: (i // per, 0, 0)),
                  pl.BlockSpec((rv, d), lambda i: (0, 0)),
                  pl.BlockSpec((d, d), lambda i: (0, 0)),
                  pl.BlockSpec((d, d), lambda i: (0, 0))],
        out_specs=(pl.BlockSpec((tm, d), lambda i: (i, 0)),
                   pl.BlockSpec((tm, rv), lambda i: (i, 0)),
                   pl.BlockSpec((tm, d), lambda i: (i, 0)),
                   pl.BlockSpec((tm, 2 * d), lambda i: (i, 0)),
                   pl.BlockSpec((8, 128), lambda i: (0, 0)),
                   pl.BlockSpec((None, 1, d), lambda i: (i // per, 0, 0)),
                   ANY, ANY, ANY),
        out_shape=(SDS((t_lat, d), F32), SDS((t_lat, rv), F32), SDS((t_lat, d), F32), SDS((t_lat, 2 * d), BF),
                   SDS((8, 128), F32), SDS((n_samp, 1, d), F32),
                   SDS((rv, d), F32), SDS((d, d), F32), SDS((d, d), F32)),
        scratch_shapes=[pltpu.VMEM((rv, d), F32), pltpu.VMEM((d, d), F32), pltpu.VMEM((d, d), F32)],
        compiler_params=_cp(("arbitrary",), 56))(
            x_lat, target, y_ret, y_att, px, px, px, px, gate3, w_o_ret, w_o_att, w_out)


def _place():
    x, y, c = lax.axis_index("x"), lax.axis_index("y"), lax.axis_index("c")
    chips = [(1 - x, y), (x, 1 - y), (1 - x, 1 - y)]
    return x, y, c, chips


def _remote(src, dst, send_sem, recv_sem, to):
    return pltpu.make_async_remote_copy(src_ref=src, dst_ref=dst, send_sem=send_sem, recv_sem=recv_sem,
                                        device_id=to, device_id_type=MESH)


def _place_ids():
    x, y, c = lax.axis_index("x"), lax.axis_index("y"), lax.axis_index("c")
    return jnp.stack([x, y, c, 2 * x + y]).astype(jnp.int32)


def _cast_place(w, ids):
    rows, cols = w.shape
    tr = min(rows, 256)

    def body(ids_ref, w_ref, o_ref):
        o_ref[...] = w_ref[...].astype(BF)

    return pl.pallas_call(
        body, name="cast_place",
        grid_spec=pltpu.PrefetchScalarGridSpec(
            num_scalar_prefetch=1, grid=(rows // tr,),
            in_specs=[pl.BlockSpec((tr, cols), lambda i, ids_ref: (i, 0))],
            out_specs=pl.BlockSpec((None, tr, cols), lambda i, ids_ref: (ids_ref[3], i, 0))),
        out_shape=SDS((N_SHARD, rows, cols), BF),
        compiler_params=_cp(("parallel",), 40))(ids, w)


def _all_gather_weights(bufs):
    n = len(bufs)

    def body(*refs):
        outs = refs[n:2 * n]
        send_sems, recv_sems = refs[2 * n:]
        x, y, c, chips = _place()
        sibling = (x, y, 1 - c)
        me = 2 * x + y

        def half(ref, s, which):
            h = ref.shape[1] // 2
            return ref.at[s, pl.ds(which * h, h), :]

        first = []
        for a in range(n):
            for j, chip in enumerate(chips):
                k = a * 3 + j
                win = half(outs[a], me, c)
                first.append(_remote(win, win, send_sems.at[k], recv_sems.at[k], (*chip, c)))
        for cp in first:
            cp.start()
        passed = []
        for a in range(n):
            for j, chip in enumerate(chips):
                k = a * 3 + j
                win = half(outs[a], 2 * chip[0] + chip[1], c)
                _remote(win, win, send_sems.at[k], recv_sems.at[k], (*chip, c)).wait_recv()
                fw = _remote(win, win, send_sems.at[3 * n + k], recv_sems.at[3 * n + k], sibling)
                fw.start()
                passed.append(fw)
        for a in range(n):
            for j, chip in enumerate(chips):
                k = a * 3 + j
                win = half(outs[a], 2 * chip[0] + chip[1], 1 - c)
                _remote(win, win, send_sems.at[3 * n + k], recv_sems.at[3 * n + k], sibling).wait_recv()
        for cp in first + passed:
            cp.wait_send()

    return pl.pallas_call(
        body, name="all_gather_weights",
        in_specs=[ANY] * n, out_specs=tuple([ANY] * n),
        out_shape=tuple(SDS(b.shape, b.dtype) for b in bufs),
        input_output_aliases={a: a for a in range(n)},
        scratch_shapes=[pltpu.SemaphoreType.DMA((6 * n,)), pltpu.SemaphoreType.DMA((6 * n,))],
        compiler_params=_cp(has_side_effects=True))(*bufs)


def _swap_halves(grads):
    n = len(grads)

    def body(*refs):
        ins, outs = refs[:n], refs[n:2 * n]
        send_sems, recv_sems = refs[2 * n:]
        x, y, c, _ = _place()
        sibling = (x, y, 1 - c)

        def half(ref, which):
            h = ref.shape[1] // 2
            return ref.at[:, pl.ds(which * h, h), :]

        sends = [_remote(half(ins[a], 1 - c), outs[a], send_sems.at[a], recv_sems.at[a], sibling)
                 for a in range(n)]
        for cp in sends:
            cp.start()
        for cp in sends:
            cp.wait_recv()
        for cp in sends:
            cp.wait_send()

    return pl.pallas_call(
        body, name="swap_halves",
        in_specs=[ANY] * n, out_specs=tuple([ANY] * n),
        out_shape=tuple(SDS((g.shape[0], g.shape[1] // 2, g.shape[2]), g.dtype) for g in grads),
        scratch_shapes=[pltpu.SemaphoreType.DMA((n,)), pltpu.SemaphoreType.DMA((n,))],
        compiler_params=_cp(has_side_effects=True))(*grads)


def _chip_sum(g, p, ids):
    n_s, rows, cols = g.shape
    h = rows // 2
    tr = min(h, 256)
    nb = h // tr

    def body(ids_ref, g_ref, p_ref, o_ref):
        o_ref[...] = g_ref[...] + p_ref[...]

    return pl.pallas_call(
        body, name="chip_sum",
        grid_spec=pltpu.PrefetchScalarGridSpec(
            num_scalar_prefetch=1, grid=(n_s, nb),
            in_specs=[pl.BlockSpec((None, tr, cols), lambda s, i, ids_ref: (s, ids_ref[2] * nb + i, 0)),
                      pl.BlockSpec((None, tr, cols), lambda s, i, ids_ref: (s, i, 0))],
            out_specs=pl.BlockSpec((None, tr, cols), lambda s, i, ids_ref: (s, i, 0))),
        out_shape=SDS((n_s, h, cols), g.dtype),
        compiler_params=_cp(("parallel", "parallel"), 40))(ids, g, p)


def _exchange_shards(parts):
    n = len(parts)

    def body(*refs):
        ins, outs = refs[:n], refs[n:2 * n]
        send_sems, recv_sems = refs[2 * n:]
        x, y, c, chips = _place()
        sends = []
        for a in range(n):
            for j, chip in enumerate(chips):
                k = a * 3 + j
                sends.append(_remote(ins[a].at[2 * chip[0] + chip[1]], outs[a].at[j],
                                     send_sems.at[k], recv_sems.at[k], (*chip, c)))
        for cp in sends:
            cp.start()
        for cp in sends:
            cp.wait_recv()
        for cp in sends:
            cp.wait_send()

    return pl.pallas_call(
        body, name="exchange_shards",
        in_specs=[ANY] * n, out_specs=tuple([ANY] * n),
        out_shape=tuple(SDS((3,) + p.shape[1:], p.dtype) for p in parts),
        scratch_shapes=[pltpu.SemaphoreType.DMA((3 * n,)), pltpu.SemaphoreType.DMA((3 * n,))],
        compiler_params=_cp(has_side_effects=True))(*parts)


def _shard_sum(t, q, ids):
    _, h, cols = t.shape
    tr = min(h, 256)
    nb = h // tr

    def body(ids_ref, t_ref, q_ref, o_ref):
        o_ref[...] = ((t_ref[...] + q_ref[0]) + q_ref[1]) + q_ref[2]

    return pl.pallas_call(
        body, name="shard_sum",
        grid_spec=pltpu.PrefetchScalarGridSpec(
            num_scalar_prefetch=1, grid=(nb,),
            in_specs=[pl.BlockSpec((None, tr, cols), lambda i, ids_ref: (ids_ref[3], i, 0)),
                      pl.BlockSpec((3, tr, cols), lambda i, ids_ref: (0, i, 0))],
            out_specs=pl.BlockSpec((tr, cols), lambda i, ids_ref: (ids_ref[2] * nb + i, 0))),
        out_shape=SDS((2 * h, cols), t.dtype),
        compiler_params=_cp(("parallel",), 40))(ids, t, q)


def _join_halves(bufs):
    n = len(bufs)

    def body(*refs):
        outs = refs[n:2 * n]
        send_sems, recv_sems = refs[2 * n:]
        x, y, c, _ = _place()
        sibling = (x, y, 1 - c)

        def win(ref, which):
            h = ref.shape[0] // 2
            return ref.at[pl.ds(which * h, h), :]

        sends = [_remote(win(outs[a], c), win(outs[a], c), send_sems.at[a], recv_sems.at[a], sibling)
                 for a in range(n)]
        for cp in sends:
            cp.start()
        for a in range(n):
            other = win(outs[a], 1 - c)
            _remote(other, other, send_sems.at[a], recv_sems.at[a], sibling).wait_recv()
        for cp in sends:
            cp.wait_send()

    return pl.pallas_call(
        body, name="join_halves",
        in_specs=[ANY] * n, out_specs=tuple([ANY] * n),
        out_shape=tuple(SDS(b.shape, b.dtype) for b in bufs),
        input_output_aliases={a: a for a in range(n)},
        scratch_shapes=[pltpu.SemaphoreType.DMA((n,)), pltpu.SemaphoreType.DMA((n,))],
        compiler_params=_cp(has_side_effects=True))(*bufs)


def _all_reduce_small(block):
    rows, cols = block.shape
    n_dev = 8

    def body(x_ref, o_ref, buf, send_sems, recv_sems, local_sem):
        x, y, c, chips = _place()
        me, sibling = (x, y, c), (x, y, 1 - c)

        def slot(px_, py_, pc_):
            return buf.at[4 * px_ + 2 * py_ + pc_]

        def copy(k, who, to, src=None):
            return _remote(slot(*who) if src is None else src, slot(*who), send_sems.at[k], recv_sems.at[k], to)

        mine = pltpu.make_async_copy(x_ref, slot(*me), local_sem)
        mine.start()
        first = [copy(0, me, sibling, src=x_ref)]
        first += [copy(1 + j, me, (*chip, c), src=x_ref) for j, chip in enumerate(chips)]
        for cp in first:
            cp.start()
        passed = [copy(4 + j, (*chip, c), sibling) for j, chip in enumerate(chips)]
        for j, chip in enumerate(chips):
            copy(1 + j, (*chip, c), me).wait_recv()
            passed[j].start()
        copy(0, sibling, me).wait_recv()
        for j, chip in enumerate(chips):
            copy(4 + j, (*chip, 1 - c), me).wait_recv()
        for cp in first + passed:
            cp.wait_send()
        mine.wait()
        acc = buf[0]
        for s in range(1, n_dev):
            acc = acc + buf[s]
        o_ref[...] = acc

    return pl.pallas_call(
        body, name="all_reduce_small",
        in_specs=[pl.BlockSpec(memory_space=pltpu.VMEM)],
        out_specs=pl.BlockSpec(memory_space=pltpu.VMEM),
        out_shape=SDS((rows, cols), F32),
        scratch_shapes=[pltpu.VMEM((n_dev, rows, cols), F32), pltpu.SemaphoreType.DMA((7,)),
                        pltpu.SemaphoreType.DMA((7,)), pltpu.SemaphoreType.DMA],
        compiler_params=_cp(has_side_effects=True))(block)


def _adam_math(w, g, m, v):
    m = ADAM_B1 * m + (1.0 - ADAM_B1) * g
    v = ADAM_B2 * v + (1.0 - ADAM_B2) * (g * g)
    m_hat = m / (1.0 - ADAM_B1 ** ADAM_STEP)
    v_hat = v / (1.0 - ADAM_B2 ** ADAM_STEP)
    delta = -ADAM_LR * (m_hat / (jnp.sqrt(v_hat) + ADAM_EPS) + ADAM_WD * w)
    return delta, m, v


def _adamw(w, g, m, v):
    rows, cols = w.shape
    tr = min(rows, 256)

    def body(w_ref, g_ref, m_ref, v_ref, d_ref, nm_ref, nv_ref):
        d_ref[...], nm_ref[...], nv_ref[...] = _adam_math(w_ref[...], g_ref[...], m_ref[...], v_ref[...])

    spec = pl.BlockSpec((tr, cols), lambda i: (i, 0))
    return pl.pallas_call(
        body, name="adamw", grid=(rows // tr,), in_specs=[spec] * 4, out_specs=(spec,) * 3,
        out_shape=(SDS(w.shape, F32),) * 3, compiler_params=_cp(("parallel",), 40))(w, g, m, v)


def _adamw_small(w, g, m, v):
    def body(w_ref, g_ref, m_ref, v_ref, go_ref, d_ref, nm_ref, nv_ref):
        w = w_ref[...]
        g = g_ref[...]
        sub = lax.broadcasted_iota(jnp.int32, w.shape, 0)
        lane = lax.broadcasted_iota(jnp.int32, w.shape, 1)
        is_ret = jnp.logical_and(sub == 5, lane < 2 * RET_HEADS)
        u = jnp.exp(jnp.where(is_ret, w, -1.0) * jnp.log(2.0))
        g = jnp.where(is_ret, g * (-u * jnp.log(2.0) / (1.0 - u)), g)
        go_ref[...] = g
        d_ref[...], nm_ref[...], nv_ref[...] = _adam_math(w, g, m_ref[...], v_ref[...])

    return pl.pallas_call(body, name="adamw_small", out_shape=(SDS(w.shape, F32),) * 4)(w, g, m, v)


def _rope_tables(seq, n_samp, n_ctx_rows):
    rows = seq // GRID_W
    row = jnp.repeat(jnp.arange(rows, dtype=F32), GRID_W)
    col = jnp.tile(jnp.arange(GRID_W, dtype=F32), rows)
    half = ATT_HEAD_DIM // 2
    freqs = ROPE_THETA ** (-jnp.arange(0, half, 2, dtype=F32) / half)
    ang = jnp.concatenate([row[:, None] * freqs, col[:, None] * freqs], axis=-1)
    cos, sin = jnp.cos(ang), jnp.sin(ang)
    cos_f = jnp.repeat(cos, 2, axis=1)
    sin_s = jnp.stack([-sin, sin], axis=-1).reshape(seq, ATT_HEAD_DIM)
    cos_all = jnp.concatenate([jnp.tile(cos_f, (n_samp, 1)), jnp.ones((n_ctx_rows, ATT_HEAD_DIM), F32)], axis=0)
    sin_all = jnp.concatenate([jnp.tile(sin_s, (n_samp, 1)), jnp.zeros((n_ctx_rows, ATT_HEAD_DIM), F32)], axis=0)
    return cos_all, sin_all


def _pack_small(c_ctx, norm_w, b_ada, ret, qn, kn):
    d = D_MODEL
    row5 = jnp.concatenate([ret.reshape(-1), jnp.zeros((128 - 2 * RET_HEADS,), F32), qn.reshape(-1), kn.reshape(-1),
                            jnp.zeros((d - 384,), F32)])
    return jnp.concatenate([c_ctx.reshape(1, d), norm_w.reshape(1, d), b_ada.reshape(3, d), row5.reshape(1, d),
                            jnp.zeros((2, d), F32)], axis=0)


def _unpack_small(p):
    d = D_MODEL
    return (p[0], p[1:2], p[2:5].reshape(1, 3 * d), p[5, :2 * RET_HEADS].reshape(1, 2, RET_HEADS),
            p[5:6, 128:256], p[5:6, 256:384])


def _local_step(x, c, ctx, c_ctx, norm_w, b_ada, ret_log2_decay, q_norm_w, k_norm_w, loss_target,
                w_ada_g, w_in_g, w_o_ret, w_o_att, w_out):
    n_samp, seq, d = x.shape
    lc = ctx.shape[1]
    t_lat, t_ctx = n_samp * seq, n_samp * lc
    assert seq % TM == 0 and t_ctx % TM == 0 and t_lat % lc == 0 and seq % GRID_W == 0
    tps = seq // TM

    x_lat = x.reshape(t_lat, d)
    x_all = jnp.concatenate([x_lat, ctx.reshape(t_ctx, d)], axis=0)
    cvec8 = jnp.concatenate([c, c_ctx.reshape(1, d), jnp.zeros((8 - n_samp - 1, d), F32)], axis=0)
    lg = jnp.log1p(-jnp.exp2(ret_log2_decay.reshape(2, RET_HEADS)))
    cos_all, sin_all = _rope_tables(seq, n_samp, t_ctx)

    mod8 = _adaln_fwd(cvec8, w_ada_g, b_ada)
    mod3 = mod8[:n_samp + 1]
    shift3 = mod3[:, None, 0:d]
    scale3 = mod3[:, None, d:2 * d]
    gate3 = mod3[:, None, 2 * d:3 * d]

    px, hxt = _normproj(x_all, norm_w, scale3, shift3, w_in_g, tps, n_samp)

    states0 = _ctx_state_fwd(px, lg, n_samp, t_lat, lc)
    o2, saved = _ret_fwd(px, states0, lg, n_samp, seq)
    y_ret = _retnorm_fwd(o2, px)

    qn = _att_prep_q(px, cos_all, sin_all, q_norm_w, t_lat)
    kn, vn = _att_prep_kv(px, cos_all, sin_all, k_norm_w)
    y_att, o_att = _att_fwd(qn, kn, vn, px, n_samp, seq, lc)

    (gx_res, dy_ret, dy_att, dmg, loss8, dgate, g_w_o_ret, g_w_o_att, g_w_out) = _merge(
        x_lat, loss_target.reshape(t_lat, d), y_ret, y_att, px, gate3, w_o_ret, w_o_att, w_out, tps)

    d_att_q, d_att_g, dkl, dkc, dvl, dvc, gqw = _att_bwd(
        qn, kn, vn, px, o_att, dy_att, cos_all, sin_all, q_norm_w, n_samp, seq, lc)
    d_att_kv, gkw = _att_kv_bwd(jnp.concatenate([dkl, dkc], axis=0), jnp.concatenate([dvl, dvc], axis=0),
                                px, cos_all, sin_all, k_norm_w)

    do, d_ret_g = _retnorm_bwd(dy_ret, o2, px)
    dq2, dk2, dv2, dstates, dlg_lat = _ret_bwd(px, do, saved, lg, n_samp, seq)
    d_ret_q, d_ret_k, d_ret_v = _ret_combine(dq2, dk2, dv2)
    dc_ret_k, dc_ret_v, dlg_ctx = _ctx_state_bwd(px, dstates, lg, n_samp, t_lat, lc)

    dpx_lat = jnp.concatenate([d_ret_k, d_ret_v, d_att_kv[:t_lat], d_ret_q, d_ret_g, d_att_q, d_att_g, dmg], axis=1)
    dpx_ctx = jnp.concatenate([dc_ret_k, dc_ret_v, d_att_kv[t_lat:], jnp.zeros((t_ctx, IN_COLS - KV_COLS), BF)],
                              axis=1)
    dpx_all = jnp.concatenate([dpx_lat, dpx_ctx], axis=0)

    g_w_in = _gw_in(hxt, dpx_all)
    dhx = _dhx(dpx_all, w_in_g)
    grad_x, dshift, dscale, g_norm_w = _norm_bwd(x_all, dhx, gx_res, norm_w, scale3, shift3, tps, n_samp)

    dgate_all = jnp.concatenate([dgate, jnp.zeros((1, 1, d), F32)], axis=0)
    dmod3 = jnp.concatenate([dshift, dscale, dgate_all], axis=2).reshape(n_samp + 1, 3 * d)
    dmod8 = jnp.concatenate([dmod3, jnp.zeros((8 - n_samp - 1, 3 * d), F32)], axis=0)
    g_w_ada, g_b_ada, dc8 = _adaln_bwd(cvec8, dmod8, w_ada_g)

    g_lg = (jnp.sum(dlg_lat[:, :, :, 0, 0], axis=0).T
            + jnp.stack([jnp.sum(dlg_ctx[:, :, 0, 0], axis=0), jnp.sum(dlg_ctx[:, :, 1, 0], axis=0)], axis=0))
    small = _pack_small(dc8[n_samp], g_norm_w, g_b_ada, g_lg, jnp.sum(gqw, axis=(0, 1, 2)), gkw)
    return (loss8[0, 0], grad_x.reshape(n_samp, seq, d),
            (g_w_ada, g_w_in, g_w_o_ret, g_w_o_att, g_w_out), small)


def kernel(x, c, ctx, c_ctx, norm_w, w_ada, b_ada, w_in, ret_log2_decay, q_norm_w, k_norm_w, w_o_ret, w_o_att, w_out, loss_target, m_c_ctx, m_norm_w, m_w_ada, m_b_ada, m_w_in, m_ret_log2_decay, m_q_norm_w, m_k_norm_w, m_w_o_ret, m_w_o_att, m_w_out, v_c_ctx, v_norm_w, v_w_ada, v_b_ada, v_w_in, v_ret_log2_decay, v_q_norm_w, v_k_norm_w, v_w_o_ret, v_w_o_att, v_w_out):
    big_w = (w_ada[0], w_in[0], w_o_ret[0], w_o_att[0], w_out[0])
    big_m = (m_w_ada[0], m_w_in[0], m_w_o_ret[0], m_w_o_att[0], m_w_out[0])
    big_v = (v_w_ada[0], v_w_in[0], v_w_o_ret[0], v_w_o_att[0], v_w_out[0])

    ids = _place_ids()
    gathered = _all_gather_weights(tuple(_cast_place(w, ids) for w in big_w))
    w_ada_g, w_in_g = gathered[0], gathered[1]
    w_o_ret_f = gathered[2].reshape(-1, D_MODEL)
    w_o_att_f = gathered[3].reshape(-1, D_MODEL)
    w_out_f = gathered[4].reshape(-1, D_MODEL)

    loss_local, grad_x, big_g, small_g = _local_step(
        x, c, ctx, c_ctx, norm_w[0:1], b_ada[0:1], ret_log2_decay[0], q_norm_w[0:1], k_norm_w[0:1], loss_target,
        w_ada_g, w_in_g, w_o_ret_f, w_o_att_f, w_out_f)
    loss = lax.psum(loss_local, ("x", "y", "c"))

    g_sm = (big_g[0], big_g[1], big_g[2].reshape(N_SHARD, -1, D_MODEL), big_g[3].reshape(N_SHARD, -1, D_MODEL),
            big_g[4].reshape(N_SHARD, -1, D_MODEL))
    from_sibling = _swap_halves(g_sm)
    chip_sums = tuple(_chip_sum(g, p, ids) for g, p in zip(g_sm, from_sibling))
    from_chips = _exchange_shards(chip_sums)
    big_grad = _join_halves(tuple(_shard_sum(t, q, ids) for t, q in zip(chip_sums, from_chips)))

    small_grad_in = _all_reduce_small(small_g)
    small_w = _pack_small(c_ctx, norm_w, b_ada, ret_log2_decay, q_norm_w, k_norm_w)
    small_m = _pack_small(m_c_ctx, m_norm_w, m_b_ada, m_ret_log2_decay, m_q_norm_w, m_k_norm_w)
    small_v = _pack_small(v_c_ctx, v_norm_w, v_b_ada, v_ret_log2_decay, v_q_norm_w, v_k_norm_w)
    small_grad, small_delta, small_nm, small_nv = _adamw_small(small_w, small_grad_in, small_m, small_v)

    big_delta, big_nm, big_nv = [], [], []
    for w, g, m, v in zip(big_w, big_grad, big_m, big_v):
        dlt, nm, nv = _adamw(w, g, m, v)
        big_delta.append(dlt[None])
        big_nm.append(nm[None])
        big_nv.append(nv[None])
    big_grad = [g[None] for g in big_grad]

    def order(small_packed, big):
        s = _unpack_small(small_packed)
        return (s[0], s[1], big[0], s[2], big[1], s[3], s[4], s[5], big[2], big[3], big[4])

    return (loss, grad_x, *order(small_grad, big_grad), *order(small_delta, big_delta),
            *order(small_nm, big_nm), *order(small_nv, big_nv))
```

```python
import functools

import jax
import jax.numpy as jnp
from jax import lax
from jax.experimental import pallas as pl
from jax.experimental.pallas import tpu as pltpu

F32 = jnp.float32
BF = jnp.bfloat16
SDS = jax.ShapeDtypeStruct
MESH = pl.DeviceIdType.MESH
ANY = pl.BlockSpec(memory_space=pl.ANY)
SMEM = pl.BlockSpec(memory_space=pltpu.SMEM)

D_MODEL = 1024
GRID_W = 64
RET_HEADS = 4
RET_DK = 256
RET_DV = 512
RET_CHUNK = 128
ATT_HEADS = 8
ATT_KV_HEADS = 2
ATT_REP = ATT_HEADS // ATT_KV_HEADS
ATT_HEAD_DIM = 128
ROPE_THETA = 10000.0
NORM_EPS = 1e-6
IN_COLS = 10752
KV_COLS = 3584
C_RK, C_RV, C_AK, C_AV, C_RQ, C_RG, C_AQ, C_AG, C_MR, C_MA = 0, 1024, 3072, 3328, 3584, 4608, 6656, 7680, 8704, 9728
N_SHARD = 4
ADA_W = 3 * D_MODEL // N_SHARD
IN_W = IN_COLS // N_SHARD
IN_BLK = IN_W // 3
N_IN_BLK = IN_COLS // IN_BLK
TM = 512
ADAM_LR, ADAM_B1, ADAM_B2, ADAM_EPS, ADAM_WD, ADAM_STEP = 0.001, 0.9, 0.999, 1e-08, 0.01, 10
MIB = 1024 * 1024


def _cp(sem=None, vmem_mb=None, **kw):
    if sem is not None:
        kw["dimension_semantics"] = sem
    if vmem_mb is not None:
        kw["vmem_limit_bytes"] = vmem_mb * MIB
    return pltpu.CompilerParams(**kw)


def _dot(a, b, ca=1, cb=0):
    return lax.dot_general(a.astype(BF), b.astype(BF), (((ca,), (cb,)), ((), ())), preferred_element_type=F32)


def _sigmoid(x):
    return 1.0 / (1.0 + jnp.exp(-x))


def _sum_all(x):
    return jnp.sum(jnp.sum(x, axis=1, keepdims=True), axis=0, keepdims=True)


def _swap_pairs(x):
    ax = x.ndim - 1
    lane = lax.broadcasted_iota(jnp.int32, x.shape, ax)
    nxt = pltpu.roll(x, x.shape[ax] - 1, ax)
    prv = pltpu.roll(x, 1, ax)
    return jnp.where(lane % 2 == 0, nxt, prv)


def _rms(x):
    return lax.rsqrt(jnp.mean(x * x, axis=-1, keepdims=True) + NORM_EPS)


def _rms_bwd(dxh, xh, r):
    return r * (dxh - xh * jnp.mean(dxh * xh, axis=-1, keepdims=True))


def _adaln_fwd(cvec8, w_ada_g, b_ada):
    def body(c_ref, w_ref, b_ref, o_ref):
        cv = c_ref[...]
        sc = (cv * _sigmoid(cv)).astype(BF)
        for s in range(N_SHARD):
            cols = slice(s * ADA_W, (s + 1) * ADA_W)
            o_ref[:, cols] = jnp.dot(sc, w_ref[s], preferred_element_type=F32) + b_ref[:, cols]

    return pl.pallas_call(body, out_shape=SDS((8, 3 * D_MODEL), F32), name="adaln_fwd",
                          compiler_params=_cp(vmem_mb=32))(cvec8, w_ada_g, b_ada)


def _adaln_bwd(cvec8, dmod8, w_ada_g):
    def body(c_ref, d_ref, w_ref, gw_ref, gb_ref, dc_ref):
        cv = c_ref[...]
        sg = _sigmoid(cv)
        sc = cv * sg
        dm = d_ref[...]
        gb_ref[...] = jnp.sum(dm, axis=0, keepdims=True)
        dsc = jnp.zeros((8, D_MODEL), F32)
        for s in range(N_SHARD):
            cols = slice(s * ADA_W, (s + 1) * ADA_W)
            gw_ref[s] = _dot(sc, dm[:, cols], 0, 0)
            dsc = dsc + _dot(dm[:, cols], w_ref[s], 1, 1)
        dc_ref[...] = dsc * (sg * (1.0 + cv * (1.0 - sg)))

    return pl.pallas_call(
        body, name="adaln_bwd",
        out_shape=(SDS((N_SHARD, D_MODEL, ADA_W), F32), SDS((1, 3 * D_MODEL), F32), SDS((8, D_MODEL), F32)),
        compiler_params=_cp(vmem_mb=48))(cvec8, dmod8, w_ada_g)


def _normproj(x_all, norm_w, scale3, shift3, w_in_g, tiles_per_sample, n_samp):
    rows = x_all.shape[0]

    def samp(i):
        return jnp.minimum(i // tiles_per_sample, n_samp)

    def body(x_ref, nw_ref, sc_ref, sh_ref, w_ref, px_ref, hxt_ref, hx_s):
        @pl.when(pl.program_id(1) == 0)
        def _():
            x = x_ref[...]
            h = x * _rms(x) * nw_ref[...] * (1.0 + sc_ref[...]) + sh_ref[...]
            hx_s[...] = h.astype(BF)
            hxt_ref[...] = h.T.astype(BF)

        px_ref[...] = jnp.dot(hx_s[...], w_ref[...], preferred_element_type=F32)

    return pl.pallas_call(
        body, name="normproj", grid=(rows // TM, N_IN_BLK),
        in_specs=[pl.BlockSpec((TM, D_MODEL), lambda i, j: (i, 0)),
                  pl.BlockSpec((1, D_MODEL), lambda i, j: (0, 0)),
                  pl.BlockSpec((None, 1, D_MODEL), lambda i, j: (samp(i), 0, 0)),
                  pl.BlockSpec((None, 1, D_MODEL), lambda i, j: (samp(i), 0, 0)),
                  pl.BlockSpec((None, D_MODEL, IN_BLK), lambda i, j: (j // 3, 0, j % 3))],
        out_specs=(pl.BlockSpec((TM, IN_BLK), lambda i, j: (i, j)),
                   pl.BlockSpec((D_MODEL, TM), lambda i, j: (0, i))),
        out_shape=(SDS((rows, IN_COLS), F32), SDS((D_MODEL, rows), BF)),
        scratch_shapes=[pltpu.VMEM((TM, D_MODEL), BF)],
        compiler_params=_cp(("parallel", "arbitrary"), 40))(x_all, norm_w, scale3, shift3, w_in_g)


def _norm_bwd(x_all, dhx, gx_res, norm_w, scale3, shift3, tiles_per_sample, n_samp):
    rows = x_all.shape[0]
    n_lat = tiles_per_sample * n_samp
    del shift3

    def samp(i):
        return jnp.minimum(i // tiles_per_sample, n_samp)

    def lat(i):
        return jnp.minimum(i, n_lat - 1)

    def body(x_ref, dh_ref, gr_ref, nw_ref, sc_ref, gx_ref, dsh_ref, dsc_ref, dnw_ref):
        i = pl.program_id(0)
        x = x_ref[...]
        r = _rms(x)
        xh = x * r
        nw = nw_ref[...]
        dh = dh_ref[...]
        first = jnp.logical_or(i % tiles_per_sample == 0, i >= n_lat)

        @pl.when(first)
        def _():
            dsh_ref[...] = jnp.zeros_like(dsh_ref)
            dsc_ref[...] = jnp.zeros_like(dsc_ref)

        @pl.when(i == 0)
        def _():
            dnw_ref[...] = jnp.zeros_like(dnw_ref)

        dsh_ref[...] += jnp.sum(dh, axis=0, keepdims=True)
        dsc_ref[...] += jnp.sum(dh * (xh * nw), axis=0, keepdims=True)
        du = dh * (1.0 + sc_ref[...])
        dnw_ref[...] += jnp.sum(du * xh, axis=0, keepdims=True)

        @pl.when(i < n_lat)
        def _():
            gx_ref[...] = gr_ref[...] + _rms_bwd(du * nw, xh, r)

    return pl.pallas_call(
        body, name="norm_bwd", grid=(rows // TM,),
        in_specs=[pl.BlockSpec((TM, D_MODEL), lambda i: (i, 0)),
                  pl.BlockSpec((TM, D_MODEL), lambda i: (i, 0)),
                  pl.BlockSpec((TM, D_MODEL), lambda i: (lat(i), 0)),
                  pl.BlockSpec((1, D_MODEL), lambda i: (0, 0)),
                  pl.BlockSpec((None, 1, D_MODEL), lambda i: (samp(i), 0, 0))],
        out_specs=(pl.BlockSpec((TM, D_MODEL), lambda i: (lat(i), 0)),
                   pl.BlockSpec((None, 1, D_MODEL), lambda i: (samp(i), 0, 0)),
                   pl.BlockSpec((None, 1, D_MODEL), lambda i: (samp(i), 0, 0)),
                   pl.BlockSpec((1, D_MODEL), lambda i: (0, 0))),
        out_shape=(SDS((n_lat * TM, D_MODEL), F32), SDS((n_samp + 1, 1, D_MODEL), F32),
                   SDS((n_samp + 1, 1, D_MODEL), F32), SDS((1, D_MODEL), F32)),
        compiler_params=_cp(("arbitrary",), 40))(x_all, dhx, gx_res, norm_w, scale3)


def _gw_in(hxt, dpx_all):
    rows = dpx_all.shape[0]

    def body(h_ref, d_ref, o_ref):
        @pl.when(pl.program_id(1) == 0)
        def _():
            o_ref[...] = jnp.zeros_like(o_ref)

        o_ref[...] += jnp.dot(h_ref[...], d_ref[...], preferred_element_type=F32)

    return pl.pallas_call(
        body, name="gw_in", grid=(N_IN_BLK, rows // TM),
        in_specs=[pl.BlockSpec((D_MODEL, TM), lambda j, i: (0, i)),
                  pl.BlockSpec((TM, IN_BLK), lambda j, i: (i, j))],
        out_specs=pl.BlockSpec((None, D_MODEL, IN_BLK), lambda j, i: (j // 3, 0, j % 3)),
        out_shape=SDS((N_SHARD, D_MODEL, IN_W), F32),
        compiler_params=_cp(("parallel", "arbitrary"), 40))(hxt, dpx_all)


def _dhx(dpx_all, w_in_g):
    rows = dpx_all.shape[0]

    def body(d_ref, w_ref, o_ref):
        @pl.when(pl.program_id(1) == 0)
        def _():
            o_ref[...] = jnp.zeros_like(o_ref)

        o_ref[...] += lax.dot_general(d_ref[...], w_ref[...], (((1,), (1,)), ((), ())), preferred_element_type=F32)

    return pl.pallas_call(
        body, name="dhx", grid=(rows // TM, N_IN_BLK),
        in_specs=[pl.BlockSpec((TM, IN_BLK), lambda i, j: (i, j)),
                  pl.BlockSpec((None, D_MODEL, IN_BLK), lambda i, j: (j // 3, 0, j % 3))],
        out_specs=pl.BlockSpec((TM, D_MODEL), lambda i, j: (i, 0)),
        out_shape=SDS((rows, D_MODEL), F32),
        compiler_params=_cp(("parallel", "arbitrary"), 40))(dpx_all, w_in_g)


def _decays(lgv, d):
    c = RET_CHUNK
    ii = lax.broadcasted_iota(jnp.int32, (c, 1), 0).astype(F32)
    jj = lax.broadcasted_iota(jnp.int32, (1, c), 1).astype(F32)
    a_i = jnp.where(d == 0, ii, c - 1.0 - ii)
    a_j = jnp.where(d == 0, jj, c - 1.0 - jj)
    rel = a_i - a_j
    mask = jnp.where(rel >= 0, jnp.exp(lgv * jnp.maximum(rel, 0.0)), 0.0)
    qd = jnp.exp(lgv * (a_i + 1.0))
    kd = jnp.exp(lgv * (c - 1.0 - a_i))
    gc = jnp.exp(jnp.full((1, 1), lgv * c, F32))
    return a_i, rel, mask, qd, kd, gc


def _ctx_state_fwd(px, lg, n_samp, t_lat, lc):
    rb = t_lat // lc

    def body(lg_ref, k_ref, v_ref, o_ref):
        h = pl.program_id(1)
        k = k_ref[...] * (RET_DK ** -0.5)
        v = v_ref[...]
        pos = lax.broadcasted_iota(jnp.int32, (lc, 1), 0).astype(F32)
        o_ref[0] = _dot(k * jnp.exp(lg_ref[0, h] * (lc - 1.0 - pos)), v, 0, 0)
        o_ref[1] = _dot(k * jnp.exp(lg_ref[1, h] * pos), v, 0, 0)

    return pl.pallas_call(
        body, name="ctx_state_fwd", grid=(n_samp, RET_HEADS),
        in_specs=[SMEM,
                  pl.BlockSpec((lc, RET_DK), lambda b, h: (rb + b, C_RK // RET_DK + h)),
                  pl.BlockSpec((lc, RET_DV), lambda b, h: (rb + b, C_RV // RET_DV + h))],
        out_specs=pl.BlockSpec((None, 2, None, RET_DK, RET_DV), lambda b, h: (b, 0, h, 0, 0)),
        out_shape=SDS((n_samp, 2, RET_HEADS, RET_DK, RET_DV), F32),
        compiler_params=_cp(("parallel", "parallel")))(lg, px, px)


def _ctx_state_bwd(px, dstates, lg, n_samp, t_lat, lc):
    rb = t_lat // lc

    def body(lg_ref, k_ref, v_ref, ds_ref, dk_ref, dv_ref, dlg_ref):
        h = pl.program_id(1)
        k = k_ref[...] * (RET_DK ** -0.5)
        v = v_ref[...]
        pos = lax.broadcasted_iota(jnp.int32, (lc, 1), 0).astype(F32)
        e_f = lc - 1.0 - pos
        kw_f = k * jnp.exp(lg_ref[0, h] * e_f)
        kw_b = k * jnp.exp(lg_ref[1, h] * pos)
        y_f = _dot(v, ds_ref[0], 1, 1)
        y_b = _dot(v, ds_ref[1], 1, 1)
        dk = y_f * jnp.exp(lg_ref[0, h] * e_f) + y_b * jnp.exp(lg_ref[1, h] * pos)
        dk_ref[...] = (dk * (RET_DK ** -0.5)).astype(BF)
        dv_ref[...] = (_dot(kw_f, ds_ref[0]) + _dot(kw_b, ds_ref[1])).astype(BF)
        t_f = _sum_all(e_f * kw_f * y_f)
        t_b = _sum_all(pos * kw_b * y_b)
        sub = lax.broadcasted_iota(jnp.int32, (8, 128), 0)
        dlg_ref[...] = jnp.where(sub == 0, t_f, jnp.where(sub == 1, t_b, 0.0))

    return pl.pallas_call(
        body, name="ctx_state_bwd", grid=(n_samp, RET_HEADS),
        in_specs=[SMEM,
                  pl.BlockSpec((lc, RET_DK), lambda b, h: (rb + b, C_RK // RET_DK + h)),
                  pl.BlockSpec((lc, RET_DV), lambda b, h: (rb + b, C_RV // RET_DV + h)),
                  pl.BlockSpec((None, 2, None, RET_DK, RET_DV), lambda b, h: (b, 0, h, 0, 0))],
        out_specs=(pl.BlockSpec((lc, RET_DK), lambda b, h: (b, h)),
                   pl.BlockSpec((lc, RET_DV), lambda b, h: (b, h)),
                   pl.BlockSpec((None, None, 8, 128), lambda b, h: (b, h, 0, 0))),
        out_shape=(SDS((n_samp * lc, RET_HEADS * RET_DK), BF), SDS((n_samp * lc, RET_HEADS * RET_DV), BF),
                   SDS((n_samp, RET_HEADS, 8, 128), F32)),
        compiler_params=_cp(("parallel", "parallel")))(lg, px, px, dstates)


def _ret_specs(row_f, row_b):
    c = RET_CHUNK
    wq = RET_HEADS * RET_DK // 2
    wv = RET_HEADS * RET_DV // 2
    specs = []
    for row in (row_f, row_b):
        specs += [pl.BlockSpec((c, wq), lambda b, n, row=row: (row(b, n), C_RQ // wq)),
                  pl.BlockSpec((c, wq), lambda b, n, row=row: (row(b, n), C_RQ // wq + 1)),
                  pl.BlockSpec((c, 2 * wq), lambda b, n, row=row: (row(b, n), C_RK // (2 * wq))),
                  pl.BlockSpec((c, wv), lambda b, n, row=row: (row(b, n), C_RV // wv)),
                  pl.BlockSpec((c, wv), lambda b, n, row=row: (row(b, n), C_RV // wv + 1))]
    return specs


def _ret_head(refs, h):
    q0, q1, k_ref, v0, v1 = refs
    hh = h % 2
    q = (q0, q1)[h // 2][:, hh * RET_DK:(hh + 1) * RET_DK]
    k = k_ref[:, h * RET_DK:(h + 1) * RET_DK] * (RET_DK ** -0.5)
    v = (v0, v1)[h // 2][:, hh * RET_DV:(hh + 1) * RET_DV]
    return q, k, v


def _ret_fwd(px, states0, lg, n_samp, seq):
    c = RET_CHUNK
    nc = seq // c
    t_lat = n_samp * seq
    wo = RET_HEADS * RET_DV

    def row_f(b, n):
        return b * nc + n

    def row_b(b, n):
        return b * nc + nc - 1 - n

    def body(lg_ref, *refs):
        ins, (s0_ref, of_ref, ob_ref, st_ref, s_s) = refs[:10], refs[10:]

        @pl.when(pl.program_id(1) == 0)
        def _():
            s_s[...] = s0_ref[...]

        for d, o_ref in ((0, of_ref), (1, ob_ref)):
            for h in range(RET_HEADS):
                _, _, mask, qd, kd, gc = _decays(lg_ref[d, h], d)
                q, k, v = _ret_head(ins[5 * d:5 * d + 5], h)
                s = s_s[d, h]
                st_ref[h, d] = s.astype(BF)
                sc = _dot(q, k, 1, 1) * mask
                o_ref[:, h * RET_DV:(h + 1) * RET_DV] = _dot(sc, v) + _dot(q * qd, s)
                s_s[d, h] = s * gc + _dot(k * kd, v, 0, 0)

    return pl.pallas_call(
        body, name="ret_fwd", grid=(n_samp, nc),
        in_specs=[SMEM] + _ret_specs(row_f, row_b) + [
            pl.BlockSpec((None, 2, RET_HEADS, RET_DK, RET_DV), lambda b, n: (b, 0, 0, 0, 0))],
        out_specs=(pl.BlockSpec((c, wo), lambda b, n: (row_f(b, n), 0)),
                   pl.BlockSpec((c, wo), lambda b, n: (row_b(b, n), 0)),
                   pl.BlockSpec((None, RET_HEADS, 2, None, RET_DK, RET_DV), lambda b, n: (b, 0, 0, n, 0, 0))),
        out_shape=(SDS((t_lat, wo), F32), SDS((t_lat, wo), F32),
                   SDS((n_samp, RET_HEADS, 2, nc, RET_DK, RET_DV), BF)),
        scratch_shapes=[pltpu.VMEM((2, RET_HEADS, RET_DK, RET_DV), F32)],
        compiler_params=_cp(("parallel", "arbitrary"), 48))(lg, *([px] * 10), states0)


def _ret_bwd(px, do, saved, lg, n_samp, seq):
    c = RET_CHUNK
    nc = seq // c
    t_lat = n_samp * seq
    wq, wo = RET_HEADS * RET_DK, RET_HEADS * RET_DV

    def row_f(b, n):
        return b * nc + nc - 1 - n

    def row_b(b, n):
        return b * nc + n

    def body(lg_ref, *refs):
        ins = refs[:10]
        (dof_ref, dob_ref, st_ref, dqf, dkf, dvf, dqb, dkb, dvb, ds0_ref, dlg_ref, ds_s, acc_s) = refs[10:]
        n = pl.program_id(1)

        @pl.when(n == 0)
        def _():
            ds_s[...] = jnp.zeros_like(ds_s)
            acc_s[...] = jnp.zeros_like(acc_s)

        for d, (do_ref, dq_ref, dk_ref, dv_ref) in enumerate(((dof_ref, dqf, dkf, dvf), (dob_ref, dqb, dkb, dvb))):
            for h in range(RET_HEADS):
                a_i, rel, mask, qd, kd, gc = _decays(lg_ref[d, h], d)
                q, k, v = _ret_head(ins[5 * d:5 * d + 5], h)
                qb, kb, vb = q.astype(BF), k.astype(BF), v.astype(BF)
                dob = do_ref[:, h * RET_DV:(h + 1) * RET_DV].astype(BF)
                sb = st_ref[h, d]
                ds = ds_s[d, h]
                dsb = ds.astype(BF)
                raw = _dot(qb, kb, 1, 1)
                sc = raw * mask
                dsc = _dot(dob, vb, 1, 1) * mask
                dscb = dsc.astype(BF)
                x = _dot(dob, sb, 1, 1)
                y = _dot(vb, dsb, 1, 1)
                qq = q * qd
                kk = k * kd
                dq_ref[:, h * RET_DK:(h + 1) * RET_DK] = _dot(dscb, kb) + x * qd
                dk_ref[:, h * RET_DK:(h + 1) * RET_DK] = _dot(dscb, qb, 0, 0) + y * kd
                dv_ref[:, h * RET_DV:(h + 1) * RET_DV] = _dot(sc, dob, 0, 0) + _dot(kk, dsb)
                t = (_sum_all(dsc * raw * rel) + _sum_all((a_i + 1.0) * qq * x)
                     + _sum_all((c - 1.0 - a_i) * kk * y) + c * gc * _sum_all(ds * sb.astype(F32)))
                acc_s[4 * d + h:4 * d + h + 1, :] += t
                ds_s[d, h] = ds * gc + _dot(qq, dob, 0, 0)

        @pl.when(n == nc - 1)
        def _():
            ds0_ref[...] = ds_s[...]
            dlg_ref[...] = acc_s[...]

    do_spec_f = pl.BlockSpec((c, wo), lambda b, n: (row_f(b, n), 0))
    do_spec_b = pl.BlockSpec((c, wo), lambda b, n: (row_b(b, n), 0))
    dq_spec_f = pl.BlockSpec((c, wq), lambda b, n: (row_f(b, n), 0))
    dq_spec_b = pl.BlockSpec((c, wq), lambda b, n: (row_b(b, n), 0))
    return pl.pallas_call(
        body, name="ret_bwd", grid=(n_samp, nc),
        in_specs=[SMEM] + _ret_specs(row_f, row_b) + [
            do_spec_f, do_spec_b,
            pl.BlockSpec((None, RET_HEADS, 2, None, RET_DK, RET_DV), lambda b, n: (b, 0, 0, nc - 1 - n, 0, 0))],
        out_specs=(dq_spec_f, dq_spec_f, do_spec_f, dq_spec_b, dq_spec_b, do_spec_b,
                   pl.BlockSpec((None, 2, RET_HEADS, RET_DK, RET_DV), lambda b, n: (b, 0, 0, 0, 0)),
                   pl.BlockSpec((None, 8, 128), lambda b, n: (b, 0, 0))),
        out_shape=(SDS((t_lat, wq), F32), SDS((t_lat, wq), F32), SDS((t_lat, wo), F32),
                   SDS((t_lat, wq), F32), SDS((t_lat, wq), F32), SDS((t_lat, wo), F32),
                   SDS((n_samp, 2, RET_HEADS, RET_DK, RET_DV), F32), SDS((n_samp, 8, 128), F32)),
        scratch_shapes=[pltpu.VMEM((2, RET_HEADS, RET_DK, RET_DV), F32), pltpu.VMEM((8, 128), F32)],
        compiler_params=_cp(("parallel", "arbitrary"), 56))(lg, *([px] * 10), do, do, saved)


def _ret_combine(dqf, dkf, dvf, dqb, dkb, dvb):
    t_lat = dqf.shape[0]
    tm = 256

    def body(qf, kf, vf, qb, kb, vb, oq_ref, ok_ref, ov_ref):
        oq_ref[...] = (qf[...] + qb[...]).astype(BF)
        ok_ref[...] = ((kf[...] + kb[...]) * (RET_DK ** -0.5)).astype(BF)
        ov_ref[...] = (vf[...] + vb[...]).astype(BF)

    wq, wv = RET_HEADS * RET_DK, RET_HEADS * RET_DV
    sq = pl.BlockSpec((tm, wq), lambda i: (i, 0))
    sv = pl.BlockSpec((tm, wv), lambda i: (i, 0))
    return pl.pallas_call(
        body, name="ret_combine", grid=(t_lat // tm,),
        in_specs=[sq, sq, sv, sq, sq, sv], out_specs=(sq, sq, sv),
        out_shape=(SDS((t_lat, wq), BF), SDS((t_lat, wq), BF), SDS((t_lat, wv), BF)),
        compiler_params=_cp(("parallel",), 40))(dqf, dkf, dvf, dqb, dkb, dvb)


def _retnorm_fwd(o_f, o_b, px):
    t_lat = o_f.shape[0]

    def body(of_ref, ob_ref, g_ref, y_ref):
        o = of_ref[...] + ob_ref[...]
        g = g_ref[...]
        y_ref[...] = (o * _rms(o) * (g * _sigmoid(g))).astype(BF)

    so = pl.BlockSpec((TM, RET_DV), lambda i, h: (i, h))
    return pl.pallas_call(
        body, name="retnorm_fwd", grid=(t_lat // TM, RET_HEADS),
        in_specs=[so, so, pl.BlockSpec((TM, RET_DV), lambda i, h: (i, C_RG // RET_DV + h))],
        out_specs=so,
        out_shape=SDS((t_lat, RET_HEADS * RET_DV), BF),
        compiler_params=_cp(("parallel", "parallel")))(o_f, o_b, px)


def _retnorm_bwd(dy, o_f, o_b, px):
    t_lat = o_f.shape[0]

    def body(dy_ref, of_ref, ob_ref, g_ref, do_ref, dg_ref):
        o = of_ref[...] + ob_ref[...]
        r = _rms(o)
        on = o * r
        g = g_ref[...]
        sg = _sigmoid(g)
        dy_ = dy_ref[...]
        dg_ref[...] = (dy_ * on * (sg * (1.0 + g * (1.0 - sg)))).astype(BF)
        do_ref[...] = _rms_bwd(dy_ * (g * sg), on, r)

    so = pl.BlockSpec((TM, RET_DV), lambda i, h: (i, h))
    return pl.pallas_call(
        body, name="retnorm_bwd", grid=(t_lat // TM, RET_HEADS),
        in_specs=[so, so, so, pl.BlockSpec((TM, RET_DV), lambda i, h: (i, C_RG // RET_DV + h))],
        out_specs=(so, so),
        out_shape=(SDS((t_lat, RET_HEADS * RET_DV), F32), SDS((t_lat, RET_HEADS * RET_DV), BF)),
        compiler_params=_cp(("parallel", "parallel")))(dy, o_f, o_b, px)


def _norm_rope(x, w, cos, sin):
    xn = x * _rms(x) * w
    return xn * cos + _swap_pairs(xn) * sin


def _norm_rope_bwd(dy, x, w, cos, sin):
    dxn = dy * cos + _swap_pairs(dy * sin)
    r = _rms(x)
    xh = x * r
    return _rms_bwd(dxn * w, xh, r), jnp.sum(dxn * xh, axis=0, keepdims=True)


def _att_prep_q(px, cos_all, sin_all, qnw, t_lat):
    hd = ATT_HEAD_DIM
    wblk = ATT_REP * hd

    def body(x_ref, cos_ref, sin_ref, w_ref, o_ref):
        for r in range(ATT_REP):
            cols = slice(r * hd, (r + 1) * hd)
            qr = _norm_rope(x_ref[:, cols], w_ref[...], cos_ref[...], sin_ref[...])
            o_ref[:, cols] = (qr * (hd ** -0.5)).astype(BF)

    return pl.pallas_call(
        body, name="att_prep_q", grid=(t_lat // TM, ATT_KV_HEADS),
        in_specs=[pl.BlockSpec((TM, wblk), lambda i, g: (i, C_AQ // wblk + g)),
                  pl.BlockSpec((TM, hd), lambda i, g: (i, 0)),
                  pl.BlockSpec((TM, hd), lambda i, g: (i, 0)),
                  pl.BlockSpec((1, hd), lambda i, g: (0, 0))],
        out_specs=pl.BlockSpec((TM, wblk), lambda i, g: (i, g)),
        out_shape=SDS((t_lat, ATT_HEADS * hd), BF),
        compiler_params=_cp(("parallel", "parallel")))(px, cos_all, sin_all, qnw)


def _att_prep_kv(px, cos_all, sin_all, knw):
    rows = px.shape[0]
    hd = ATT_HEAD_DIM
    kvw = ATT_KV_HEADS * hd

    def body(x_ref, cos_ref, sin_ref, w_ref, k_ref, v_ref):
        for g in range(ATT_KV_HEADS):
            cols = slice(g * hd, (g + 1) * hd)
            k_ref[:, cols] = _norm_rope(x_ref[:, cols], w_ref[...], cos_ref[...], sin_ref[...]).astype(BF)
        v_ref[...] = x_ref[:, kvw:].astype(BF)

    return pl.pallas_call(
        body, name="att_prep_kv", grid=(rows // TM,),
        in_specs=[pl.BlockSpec((TM, 2 * kvw), lambda i: (i, C_AK // (2 * kvw))),
                  pl.BlockSpec((TM, hd), lambda i: (i, 0)),
                  pl.BlockSpec((TM, hd), lambda i: (i, 0)),
                  pl.BlockSpec((1, hd), lambda i: (0, 0))],
        out_specs=(pl.BlockSpec((TM, kvw), lambda i: (i, 0)), pl.BlockSpec((TM, kvw), lambda i: (i, 0))),
        out_shape=(SDS((rows, kvw), BF), SDS((rows, kvw), BF)),
        compiler_params=_cp(("parallel",)))(px, cos_all, sin_all, knw)


def _att_kv_bwd(dk_tok, dv_tok, px, cos_all, sin_all, knw):
    rows = px.shape[0]
    hd = ATT_HEAD_DIM
    kvw = ATT_KV_HEADS * hd

    def body(dk_ref, dv_ref, x_ref, cos_ref, sin_ref, w_ref, o_ref, gw_ref):
        @pl.when(pl.program_id(0) == 0)
        def _():
            gw_ref[...] = jnp.zeros_like(gw_ref)

        for g in range(ATT_KV_HEADS):
            cols = slice(g * hd, (g + 1) * hd)
            dx, gw = _norm_rope_bwd(dk_ref[:, cols], x_ref[:, cols], w_ref[...], cos_ref[...], sin_ref[...])
            o_ref[:, cols] = dx.astype(BF)
            gw_ref[...] += gw
        o_ref[:, kvw:] = dv_ref[...].astype(BF)

    return pl.pallas_call(
        body, name="att_kv_bwd", grid=(rows // TM,),
        in_specs=[pl.BlockSpec((TM, kvw), lambda i: (i, 0)),
                  pl.BlockSpec((TM, kvw), lambda i: (i, 0)),
                  pl.BlockSpec((TM, 2 * kvw), lambda i: (i, C_AK // (2 * kvw))),
                  pl.BlockSpec((TM, hd), lambda i: (i, 0)),
                  pl.BlockSpec((TM, hd), lambda i: (i, 0)),
                  pl.BlockSpec((1, hd), lambda i: (0, 0))],
        out_specs=(pl.BlockSpec((TM, 2 * kvw), lambda i: (i, 0)), pl.BlockSpec((1, hd), lambda i: (0, 0))),
        out_shape=(SDS((rows, 2 * kvw), BF), SDS((1, hd), F32)),
        compiler_params=_cp(("arbitrary",)))(dk_tok, dv_tok, px, cos_all, sin_all, knw)


def _stack_heads(ref_or_val):
    hd = ATT_HEAD_DIM
    return jnp.concatenate([ref_or_val[:, r * hd:(r + 1) * hd] for r in range(ATT_REP)], axis=0)


def _att_scores(q, kl, kc):
    sl = _dot(q, kl, 1, 1)
    sc = _dot(q, kc, 1, 1)
    m = jnp.maximum(jnp.max(sl, axis=-1, keepdims=True), jnp.max(sc, axis=-1, keepdims=True))
    el = jnp.exp(sl - m)
    ec = jnp.exp(sc - m)
    denom = jnp.sum(el, axis=-1, keepdims=True) + jnp.sum(ec, axis=-1, keepdims=True)
    return el, ec, denom


def _att_fwd(qn, kn, vn, px, n_samp, seq, lc):
    hd = ATT_HEAD_DIM
    tq = 128
    nq = seq // tq
    wblk = ATT_REP * hd
    cb = n_samp * seq // lc
    t_lat = n_samp * seq

    def body(q_ref, kl_ref, kc_ref, vl_ref, vc_ref, g_ref, y_ref, o_ref):
        for r in range(ATT_REP):
            cols = slice(r * hd, (r + 1) * hd)
            el, ec, denom = _att_scores(q_ref[:, cols], kl_ref[...], kc_ref[...])
            o = (_dot(el, vl_ref[...]) + _dot(ec, vc_ref[...])) / denom
            g = g_ref[:, cols]
            o_ref[:, cols] = o
            y_ref[:, cols] = (o * (g * _sigmoid(g))).astype(BF)

    return pl.pallas_call(
        body, name="att_fwd", grid=(n_samp, ATT_KV_HEADS, nq),
        in_specs=[pl.BlockSpec((tq, wblk), lambda b, g, i: (b * nq + i, g)),
                  pl.BlockSpec((seq, hd), lambda b, g, i: (b, g)),
                  pl.BlockSpec((lc, hd), lambda b, g, i: (cb + b, g)),
                  pl.BlockSpec((seq, hd), lambda b, g, i: (b, g)),
                  pl.BlockSpec((lc, hd), lambda b, g, i: (cb + b, g)),
                  pl.BlockSpec((tq, wblk), lambda b, g, i: (b * nq + i, C_AG // wblk + g))],
        out_specs=(pl.BlockSpec((tq, wblk), lambda b, g, i: (b * nq + i, g)),
                   pl.BlockSpec((tq, wblk), lambda b, g, i: (b * nq + i, g))),
        out_shape=(SDS((t_lat, ATT_HEADS * hd), BF), SDS((t_lat, ATT_HEADS * hd), F32)),
        compiler_params=_cp(("parallel", "parallel", "parallel"), 48))(qn, kn, kn, vn, vn, px)


def _att_bwd(qn, kn, vn, px, o_att, dy_att, cos_all, sin_all, qnw, n_samp, seq, lc):
    hd = ATT_HEAD_DIM
    tq = 128
    nq = seq // tq
    wblk = ATT_REP * hd
    cb = n_samp * seq // lc
    t_lat = n_samp * seq
    kvw = ATT_KV_HEADS * hd
    scale = hd ** -0.5

    def body(q_ref, kl_ref, kc_ref, vl_ref, vc_ref, g_ref, o_ref, dy_ref, x_ref, cos_ref, sin_ref, w_ref,
             dq_ref, dg_ref, dkl_ref, dkc_ref, dvl_ref, dvc_ref, gw_ref, akl, akc, avl, avc, aw):
        i = pl.program_id(2)

        @pl.when(i == 0)
        def _():
            akl[...] = jnp.zeros_like(akl)
            akc[...] = jnp.zeros_like(akc)
            avl[...] = jnp.zeros_like(avl)
            avc[...] = jnp.zeros_like(avc)
            aw[...] = jnp.zeros_like(aw)

        dobs, pls, pcs, dsls, dscs = [], [], [], [], []
        for r in range(ATT_REP):
            cols = slice(r * hd, (r + 1) * hd)
            g = g_ref[:, cols]
            sg = _sigmoid(g)
            dy = dy_ref[:, cols]
            o = o_ref[:, cols]
            dg_ref[:, cols] = (dy * o * (sg * (1.0 + g * (1.0 - sg)))).astype(BF)
            do = dy * (g * sg)
            delta = jnp.sum(do * o, axis=-1, keepdims=True)
            el, ec, denom = _att_scores(q_ref[:, cols], kl_ref[...], kc_ref[...])
            inv = 1.0 / denom
            p_l = el * inv
            p_c = ec * inv
            dob = do.astype(BF)
            ds_l = (p_l * (_dot(dob, vl_ref[...], 1, 1) - delta)).astype(BF)
            ds_c = (p_c * (_dot(dob, vc_ref[...], 1, 1) - delta)).astype(BF)
            dq = (_dot(ds_l, kl_ref[...]) + _dot(ds_c, kc_ref[...])) * scale
            dx, gw = _norm_rope_bwd(dq, x_ref[:, cols], w_ref[...], cos_ref[...], sin_ref[...])
            dq_ref[:, cols] = dx.astype(BF)
            aw[...] += gw
            dobs.append(dob)
            pls.append(p_l.astype(BF))
            pcs.append(p_c.astype(BF))
            dsls.append(ds_l)
            dscs.append(ds_c)
        do4 = jnp.concatenate(dobs, axis=0)
        q4 = _stack_heads(q_ref)
        avl[...] += _dot(jnp.concatenate(pls, axis=0), do4, 0, 0)
        avc[...] += _dot(jnp.concatenate(pcs, axis=0), do4, 0, 0)
        akl[...] += _dot(jnp.concatenate(dsls, axis=0), q4, 0, 0)
        akc[...] += _dot(jnp.concatenate(dscs, axis=0), q4, 0, 0)

        @pl.when(i == nq - 1)
        def _():
            dkl_ref[...] = akl[...]
            dkc_ref[...] = akc[...]
            dvl_ref[...] = avl[...]
            dvc_ref[...] = avc[...]
            gw_ref[...] = aw[...]

    return pl.pallas_call(
        body, name="att_bwd", grid=(n_samp, ATT_KV_HEADS, nq),
        in_specs=[pl.BlockSpec((tq, wblk), lambda b, g, i: (b * nq + i, g)),
                  pl.BlockSpec((seq, hd), lambda b, g, i: (b, g)),
                  pl.BlockSpec((lc, hd), lambda b, g, i: (cb + b, g)),
                  pl.BlockSpec((seq, hd), lambda b, g, i: (b, g)),
                  pl.BlockSpec((lc, hd), lambda b, g, i: (cb + b, g)),
                  pl.BlockSpec((tq, wblk), lambda b, g, i: (b * nq + i, C_AG // wblk + g)),
                  pl.BlockSpec((tq, wblk), lambda b, g, i: (b * nq + i, g)),
                  pl.BlockSpec((tq, wblk), lambda b, g, i: (b * nq + i, g)),
                  pl.BlockSpec((tq, wblk), lambda b, g, i: (b * nq + i, C_AQ // wblk + g)),
                  pl.BlockSpec((tq, hd), lambda b, g, i: (b * nq + i, 0)),
                  pl.BlockSpec((tq, hd), lambda b, g, i: (b * nq + i, 0)),
                  pl.BlockSpec((1, hd), lambda b, g, i: (0, 0))],
        out_specs=(pl.BlockSpec((tq, wblk), lambda b, g, i: (b * nq + i, g)),
                   pl.BlockSpec((tq, wblk), lambda b, g, i: (b * nq + i, g)),
                   pl.BlockSpec((seq, hd), lambda b, g, i: (b, g)),
                   pl.BlockSpec((lc, hd), lambda b, g, i: (b, g)),
                   pl.BlockSpec((seq, hd), lambda b, g, i: (b, g)),
                   pl.BlockSpec((lc, hd), lambda b, g, i: (b, g)),
                   pl.BlockSpec((None, None, 1, hd), lambda b, g, i: (b, g, 0, 0))),
        out_shape=(SDS((t_lat, ATT_HEADS * hd), BF), SDS((t_lat, ATT_HEADS * hd), BF),
                   SDS((t_lat, kvw), F32), SDS((n_samp * lc, kvw), F32),
                   SDS((t_lat, kvw), F32), SDS((n_samp * lc, kvw), F32),
                   SDS((n_samp, ATT_KV_HEADS, 1, hd), F32)),
        scratch_shapes=[pltpu.VMEM((seq, hd), F32), pltpu.VMEM((lc, hd), F32),
                        pltpu.VMEM((seq, hd), F32), pltpu.VMEM((lc, hd), F32), pltpu.VMEM((1, hd), F32)],
        compiler_params=_cp(("parallel", "parallel", "arbitrary"), 56))(
            qn, kn, kn, vn, vn, px, o_att, dy_att, px, cos_all, sin_all, qnw)


def _merge(x_lat, target, y_ret, y_att, px, gate3, w_o_ret, w_o_att, w_out, tiles_per_sample):
    t_lat = x_lat.shape[0]
    tm = 256
    n_t = t_lat // tm
    per = tiles_per_sample * (TM // tm)
    d = D_MODEL
    rv = RET_HEADS * RET_DV
    n_samp = gate3.shape[0] - 1

    def body(x_ref, t_ref, yr_ref, ya_ref, mr0, mr1, ma0, ma1, gt_ref, wor_ref, woa_ref, wout_ref,
             gx_ref, dyr_ref, dya_ref, dmg_ref, loss_ref, dgt_ref, gwor_hbm, gwoa_hbm, gwout_hbm,
             aor, aoa, aout):
        i = pl.program_id(0)

        @pl.when(i == 0)
        def _():
            aor[...] = jnp.zeros_like(aor)
            aoa[...] = jnp.zeros_like(aoa)
            aout[...] = jnp.zeros_like(aout)
            loss_ref[...] = jnp.zeros_like(loss_ref)

        @pl.when(i % per == 0)
        def _():
            dgt_ref[...] = jnp.zeros_like(dgt_ref)

        yr = yr_ref[...]
        ya = ya_ref[...]
        a = jnp.dot(yr, wor_ref[...], preferred_element_type=F32)
        b = jnp.dot(ya, woa_ref[...], preferred_element_type=F32)
        sr = _sigmoid(jnp.concatenate([mr0[...], mr1[...]], axis=1))
        sa = _sigmoid(jnp.concatenate([ma0[...], ma1[...]], axis=1))
        yb = (sr * a + sa * b).astype(BF)
        out = jnp.dot(yb, wout_ref[...], preferred_element_type=F32)
        gate = gt_ref[...]
        err = x_ref[...] + gate * out - t_ref[...]
        loss_ref[...] += 0.5 * _sum_all(err * err) * (1.0 / d)
        dy_tok = err * (1.0 / d)
        gx_ref[...] = dy_tok
        dgt_ref[...] += jnp.sum(dy_tok * out, axis=0, keepdims=True)
        dout = (dy_tok * gate).astype(BF)
        aout[...] += _dot(yb, dout, 0, 0)
        dyy = _dot(dout, wout_ref[...], 1, 1)
        da = (dyy * sr).astype(BF)
        db = (dyy * sa).astype(BF)
        dmg_ref[:, :d] = (dyy * a * (sr * (1.0 - sr))).astype(BF)
        dmg_ref[:, d:] = (dyy * b * (sa * (1.0 - sa))).astype(BF)
        aor[...] += _dot(yr, da, 0, 0)
        aoa[...] += _dot(ya, db, 0, 0)
        dyr_ref[...] = _dot(da, wor_ref[...], 1, 1)
        dya_ref[...] = _dot(db, woa_ref[...], 1, 1)

        @pl.when(i == n_t - 1)
        def _():
            pltpu.sync_copy(aor, gwor_hbm)
            pltpu.sync_copy(aoa, gwoa_hbm)
            pltpu.sync_copy(aout, gwout_hbm)

    half = d // 2
    return pl.pallas_call(
        body, name="merge", grid=(n_t,),
        in_specs=[pl.BlockSpec((tm, d), lambda i: (i, 0)),
                  pl.BlockSpec((tm, d), lambda i: (i, 0)),
                  pl.BlockSpec((tm, rv), lambda i: (i, 0)),
                  pl.BlockSpec((tm, d), lambda i: (i, 0)),
                  pl.BlockSpec((tm, half), lambda i: (i, C_MR // half)),
                  pl.BlockSpec((tm, half), lambda i: (i, C_MR // half + 1)),
                  pl.BlockSpec((tm, half), lambda i: (i, C_MA // half)),
                  pl.BlockSpec((tm, half), lambda i: (i, C_MA // half + 1)),
                  pl.BlockSpec((None, 1, d), lambda i: (i // per, 0, 0)),
                  pl.BlockSpec((rv, d), lambda i: (0, 0)),
                  pl.BlockSpec((d, d), lambda i: (0, 0)),
                  pl.BlockSpec((d, d), lambda i: (0, 0))],
        out_specs=(pl.BlockSpec((tm, d), lambda i: (i, 0)),
                   pl.BlockSpec((tm, rv), lambda i: (i, 0)),
                   pl.BlockSpec((tm, d), lambda i: (i, 0)),
                   pl.BlockSpec((tm, 2 * d), lambda i: (i, 0)),
                   pl.BlockSpec((8, 128), lambda i: (0, 0)),
                   pl.BlockSpec((None, 1, d), lambda i: (i // per, 0, 0)),
                   ANY, ANY, ANY),
        out_shape=(SDS((t_lat, d), F32), SDS((t_lat, rv), F32), SDS((t_lat, d), F32), SDS((t_lat, 2 * d), BF),
                   SDS((8, 128), F32), SDS((n_samp, 1, d), F32),
                   SDS((rv, d), F32), SDS((d, d), F32), SDS((d, d), F32)),
        scratch_shapes=[pltpu.VMEM((rv, d), F32), pltpu.VMEM((d, d), F32), pltpu.VMEM((d, d), F32)],
        compiler_params=_cp(("arbitrary",), 56))(
            x_lat, target, y_ret, y_att, px, px, px, px, gate3, w_o_ret, w_o_att, w_out)


def _place():
    x, y, c = lax.axis_index("x"), lax.axis_index("y"), lax.axis_index("c")
    chips = [(1 - x, y), (x, 1 - y), (1 - x, 1 - y)]
    return x, y, c, chips


def _remote(src, dst, send_sem, recv_sem, to):
    return pltpu.make_async_remote_copy(src_ref=src, dst_ref=dst, send_sem=send_sem, recv_sem=recv_sem,
                                        device_id=to, device_id_type=MESH)


def _place_ids():
    x, y, c = lax.axis_index("x"), lax.axis_index("y"), lax.axis_index("c")
    return jnp.stack([x, y, c, 2 * x + y]).astype(jnp.int32)


def _cast_place(w, ids):
    rows, cols = w.shape
    tr = min(rows, 256)

    def body(ids_ref, w_ref, o_ref):
        o_ref[...] = w_ref[...].astype(BF)

    return pl.pallas_call(
        body, name="cast_place",
        grid_spec=pltpu.PrefetchScalarGridSpec(
            num_scalar_prefetch=1, grid=(rows // tr,),
            in_specs=[pl.BlockSpec((tr, cols), lambda i, ids_ref: (i, 0))],
            out_specs=pl.BlockSpec((None, tr, cols), lambda i, ids_ref: (ids_ref[3], i, 0))),
        out_shape=SDS((N_SHARD, rows, cols), BF),
        compiler_params=_cp(("parallel",), 40))(ids, w)


def _all_gather_weights(bufs):
    n = len(bufs)

    def body(*refs):
        outs = refs[n:2 * n]
        send_sems, recv_sems = refs[2 * n:]
        x, y, c, chips = _place()
        sibling = (x, y, 1 - c)
        me = 2 * x + y

        def half(ref, s, which):
            h = ref.shape[1] // 2
            return ref.at[s, pl.ds(which * h, h), :]

        first = []
        for a in range(n):
            for j, chip in enumerate(chips):
                k = a * 3 + j
                win = half(outs[a], me, c)
                first.append(_remote(win, win, send_sems.at[k], recv_sems.at[k], (*chip, c)))
        for cp in first:
            cp.start()
        passed = []
        for a in range(n):
            for j, chip in enumerate(chips):
                k = a * 3 + j
                win = half(outs[a], 2 * chip[0] + chip[1], c)
                _remote(win, win, send_sems.at[k], recv_sems.at[k], (*chip, c)).wait_recv()
                fw = _remote(win, win, send_sems.at[3 * n + k], recv_sems.at[3 * n + k], sibling)
                fw.start()
                passed.append(fw)
        for a in range(n):
            for j, chip in enumerate(chips):
                k = a * 3 + j
                win = half(outs[a], 2 * chip[0] + chip[1], 1 - c)
                _remote(win, win, send_sems.at[3 * n + k], recv_sems.at[3 * n + k], sibling).wait_recv()
        for cp in first + passed:
            cp.wait_send()

    return pl.pallas_call(
        body, name="all_gather_weights",
        in_specs=[ANY] * n, out_specs=tuple([ANY] * n),
        out_shape=tuple(SDS(b.shape, b.dtype) for b in bufs),
        input_output_aliases={a: a for a in range(n)},
        scratch_shapes=[pltpu.SemaphoreType.DMA((6 * n,)), pltpu.SemaphoreType.DMA((6 * n,))],
        compiler_params=_cp(has_side_effects=True))(*bufs)


def _swap_halves(grads):
    n = len(grads)

    def body(*refs):
        ins, outs = refs[:n], refs[n:2 * n]
        send_sems, recv_sems = refs[2 * n:]
        x, y, c, _ = _place()
        sibling = (x, y, 1 - c)

        def half(ref, which):
            h = ref.shape[1] // 2
            return ref.at[:, pl.ds(which * h, h), :]

        sends = [_remote(half(ins[a], 1 - c), outs[a], send_sems.at[a], recv_sems.at[a], sibling)
                 for a in range(n)]
        for cp in sends:
            cp.start()
        for cp in sends:
            cp.wait_recv()
        for cp in sends:
            cp.wait_send()

    return pl.pallas_call(
        body, name="swap_halves",
        in_specs=[ANY] * n, out_specs=tuple([ANY] * n),
        out_shape=tuple(SDS((g.shape[0], g.shape[1] // 2, g.shape[2]), g.dtype) for g in grads),
        scratch_shapes=[pltpu.SemaphoreType.DMA((n,)), pltpu.SemaphoreType.DMA((n,))],
        compiler_params=_cp(has_side_effects=True))(*grads)


def _chip_sum(g, p, ids):
    n_s, rows, cols = g.shape
    h = rows // 2
    tr = min(h, 256)
    nb = h // tr

    def body(ids_ref, g_ref, p_ref, o_ref, o16_ref):
        t = g_ref[...] + p_ref[...]
        o_ref[...] = t
        o16_ref[...] = t.astype(BF)

    out_spec = pl.BlockSpec((None, tr, cols), lambda s, i, ids_ref: (s, i, 0))
    return pl.pallas_call(
        body, name="chip_sum",
        grid_spec=pltpu.PrefetchScalarGridSpec(
            num_scalar_prefetch=1, grid=(n_s, nb),
            in_specs=[pl.BlockSpec((None, tr, cols), lambda s, i, ids_ref: (s, ids_ref[2] * nb + i, 0)),
                      pl.BlockSpec((None, tr, cols), lambda s, i, ids_ref: (s, i, 0))],
            out_specs=(out_spec, out_spec)),
        out_shape=(SDS((n_s, h, cols), g.dtype), SDS((n_s, h, cols), BF)),
        compiler_params=_cp(("parallel", "parallel"), 40))(ids, g, p)


def _exchange_shards(parts):
    n = len(parts)

    def body(*refs):
        ins, outs = refs[:n], refs[n:2 * n]
        send_sems, recv_sems = refs[2 * n:]
        x, y, c, chips = _place()
        sends = []
        for a in range(n):
            for j, chip in enumerate(chips):
                k = a * 3 + j
                sends.append(_remote(ins[a].at[2 * chip[0] + chip[1]], outs[a].at[j],
                                     send_sems.at[k], recv_sems.at[k], (*chip, c)))
        for cp in sends:
            cp.start()
        for cp in sends:
            cp.wait_recv()
        for cp in sends:
            cp.wait_send()

    return pl.pallas_call(
        body, name="exchange_shards",
        in_specs=[ANY] * n, out_specs=tuple([ANY] * n),
        out_shape=tuple(SDS((3,) + p.shape[1:], p.dtype) for p in parts),
        scratch_shapes=[pltpu.SemaphoreType.DMA((3 * n,)), pltpu.SemaphoreType.DMA((3 * n,))],
        compiler_params=_cp(has_side_effects=True))(*parts)


def _shard_sum(t, q, ids):
    _, h, cols = t.shape
    tr = min(h, 256)
    nb = h // tr

    def body(ids_ref, t_ref, q_ref, o_ref):
        o_ref[...] = ((t_ref[...] + q_ref[0].astype(F32)) + q_ref[1].astype(F32)) + q_ref[2].astype(F32)

    return pl.pallas_call(
        body, name="shard_sum",
        grid_spec=pltpu.PrefetchScalarGridSpec(
            num_scalar_prefetch=1, grid=(nb,),
            in_specs=[pl.BlockSpec((None, tr, cols), lambda i, ids_ref: (ids_ref[3], i, 0)),
                      pl.BlockSpec((3, tr, cols), lambda i, ids_ref: (0, i, 0))],
            out_specs=pl.BlockSpec((tr, cols), lambda i, ids_ref: (ids_ref[2] * nb + i, 0))),
        out_shape=SDS((2 * h, cols), t.dtype),
        compiler_params=_cp(("parallel",), 40))(ids, t, q)


def _join_halves(bufs):
    n = len(bufs)

    def body(*refs):
        outs = refs[n:2 * n]
        send_sems, recv_sems = refs[2 * n:]
        x, y, c, _ = _place()
        sibling = (x, y, 1 - c)

        def win(ref, which):
            h = ref.shape[0] // 2
            return ref.at[pl.ds(which * h, h), :]

        sends = [_remote(win(outs[a], c), win(outs[a], c), send_sems.at[a], recv_sems.at[a], sibling)
                 for a in range(n)]
        for cp in sends:
            cp.start()
        for a in range(n):
            other = win(outs[a], 1 - c)
            _remote(other, other, send_sems.at[a], recv_sems.at[a], sibling).wait_recv()
        for cp in sends:
            cp.wait_send()

    return pl.pallas_call(
        body, name="join_halves",
        in_specs=[ANY] * n, out_specs=tuple([ANY] * n),
        out_shape=tuple(SDS(b.shape, b.dtype) for b in bufs),
        input_output_aliases={a: a for a in range(n)},
        scratch_shapes=[pltpu.SemaphoreType.DMA((n,)), pltpu.SemaphoreType.DMA((n,))],
        compiler_params=_cp(has_side_effects=True))(*bufs)


def _all_reduce_small(block):
    rows, cols = block.shape
    n_dev = 8

    def body(x_ref, o_ref, buf, send_sems, recv_sems, local_sem):
        x, y, c, chips = _place()
        me, sibling = (x, y, c), (x, y, 1 - c)

        def slot(px_, py_, pc_):
            return buf.at[4 * px_ + 2 * py_ + pc_]

        def copy(k, who, to, src=None):
            return _remote(slot(*who) if src is None else src, slot(*who), send_sems.at[k], recv_sems.at[k], to)

        mine = pltpu.make_async_copy(x_ref, slot(*me), local_sem)
        mine.start()
        first = [copy(0, me, sibling, src=x_ref)]
        first += [copy(1 + j, me, (*chip, c), src=x_ref) for j, chip in enumerate(chips)]
        for cp in first:
            cp.start()
        passed = [copy(4 + j, (*chip, c), sibling) for j, chip in enumerate(chips)]
        for j, chip in enumerate(chips):
            copy(1 + j, (*chip, c), me).wait_recv()
            passed[j].start()
        copy(0, sibling, me).wait_recv()
        for j, chip in enumerate(chips):
            copy(4 + j, (*chip, 1 - c), me).wait_recv()
        for cp in first + passed:
            cp.wait_send()
        mine.wait()
        acc = buf[0]
        for s in range(1, n_dev):
            acc = acc + buf[s]
        o_ref[...] = acc

    return pl.pallas_call(
        body, name="all_reduce_small",
        in_specs=[pl.BlockSpec(memory_space=pltpu.VMEM)],
        out_specs=pl.BlockSpec(memory_space=pltpu.VMEM),
        out_shape=SDS((rows, cols), F32),
        scratch_shapes=[pltpu.VMEM((n_dev, rows, cols), F32), pltpu.SemaphoreType.DMA((7,)),
                        pltpu.SemaphoreType.DMA((7,)), pltpu.SemaphoreType.DMA],
        compiler_params=_cp(has_side_effects=True))(block)


def _adam_math(w, g, m, v):
    m = ADAM_B1 * m + (1.0 - ADAM_B1) * g
    v = ADAM_B2 * v + (1.0 - ADAM_B2) * (g * g)
    m_hat = m / (1.0 - ADAM_B1 ** ADAM_STEP)
    v_hat = v / (1.0 - ADAM_B2 ** ADAM_STEP)
    delta = -ADAM_LR * (m_hat / (jnp.sqrt(v_hat) + ADAM_EPS) + ADAM_WD * w)
    return delta, m, v


def _adamw(w, g, m, v):
    rows, cols = w.shape
    tr = min(rows, 256)

    def body(w_ref, g_ref, m_ref, v_ref, d_ref, nm_ref, nv_ref):
        d_ref[...], nm_ref[...], nv_ref[...] = _adam_math(w_ref[...], g_ref[...], m_ref[...], v_ref[...])

    spec = pl.BlockSpec((tr, cols), lambda i: (i, 0))
    return pl.pallas_call(
        body, name="adamw", grid=(rows // tr,), in_specs=[spec] * 4, out_specs=(spec,) * 3,
        out_shape=(SDS(w.shape, F32),) * 3, compiler_params=_cp(("parallel",), 40))(w, g, m, v)


def _adamw_small(w, g, m, v):
    def body(w_ref, g_ref, m_ref, v_ref, go_ref, d_ref, nm_ref, nv_ref):
        w = w_ref[...]
        g = g_ref[...]
        sub = lax.broadcasted_iota(jnp.int32, w.shape, 0)
        lane = lax.broadcasted_iota(jnp.int32, w.shape, 1)
        is_ret = jnp.logical_and(sub == 5, lane < 2 * RET_HEADS)
        u = jnp.exp(jnp.where(is_ret, w, -1.0) * jnp.log(2.0))
        g = jnp.where(is_ret, g * (-u * jnp.log(2.0) / (1.0 - u)), g)
        go_ref[...] = g
        d_ref[...], nm_ref[...], nv_ref[...] = _adam_math(w, g, m_ref[...], v_ref[...])

    return pl.pallas_call(body, name="adamw_small", out_shape=(SDS(w.shape, F32),) * 4)(w, g, m, v)


def _rope_tables(seq, n_samp, n_ctx_rows):
    rows = seq // GRID_W
    row = jnp.repeat(jnp.arange(rows, dtype=F32), GRID_W)
    col = jnp.tile(jnp.arange(GRID_W, dtype=F32), rows)
    half = ATT_HEAD_DIM // 2
    freqs = ROPE_THETA ** (-jnp.arange(0, half, 2, dtype=F32) / half)
    ang = jnp.concatenate([row[:, None] * freqs, col[:, None] * freqs], axis=-1)
    cos, sin = jnp.cos(ang), jnp.sin(ang)
    cos_f = jnp.repeat(cos, 2, axis=1)
    sin_s = jnp.stack([-sin, sin], axis=-1).reshape(seq, ATT_HEAD_DIM)
    cos_all = jnp.concatenate([jnp.tile(cos_f, (n_samp, 1)), jnp.ones((n_ctx_rows, ATT_HEAD_DIM), F32)], axis=0)
    sin_all = jnp.concatenate([jnp.tile(sin_s, (n_samp, 1)), jnp.zeros((n_ctx_rows, ATT_HEAD_DIM), F32)], axis=0)
    return cos_all, sin_all


def _pack_small(c_ctx, norm_w, b_ada, ret, qn, kn):
    d = D_MODEL
    row5 = jnp.concatenate([ret.reshape(-1), jnp.zeros((128 - 2 * RET_HEADS,), F32), qn.reshape(-1), kn.reshape(-1),
                            jnp.zeros((d - 384,), F32)])
    return jnp.concatenate([c_ctx.reshape(1, d), norm_w.reshape(1, d), b_ada.reshape(3, d), row5.reshape(1, d),
                            jnp.zeros((2, d), F32)], axis=0)


def _unpack_small(p):
    d = D_MODEL
    return (p[0], p[1:2], p[2:5].reshape(1, 3 * d), p[5, :2 * RET_HEADS].reshape(1, 2, RET_HEADS),
            p[5:6, 128:256], p[5:6, 256:384])


def _local_step(x, c, ctx, c_ctx, norm_w, b_ada, ret_log2_decay, q_norm_w, k_norm_w, loss_target,
                w_ada_g, w_in_g, w_o_ret, w_o_att, w_out):
    n_samp, seq, d = x.shape
    lc = ctx.shape[1]
    t_lat, t_ctx = n_samp * seq, n_samp * lc
    assert seq % TM == 0 and t_ctx % TM == 0 and t_lat % lc == 0 and seq % GRID_W == 0
    tps = seq // TM

    x_lat = x.reshape(t_lat, d)
    x_all = jnp.concatenate([x_lat, ctx.reshape(t_ctx, d)], axis=0)
    cvec8 = jnp.concatenate([c, c_ctx.reshape(1, d), jnp.zeros((8 - n_samp - 1, d), F32)], axis=0)
    lg = jnp.log1p(-jnp.exp2(ret_log2_decay.reshape(2, RET_HEADS)))
    cos_all, sin_all = _rope_tables(seq, n_samp, t_ctx)

    mod8 = _adaln_fwd(cvec8, w_ada_g, b_ada)
    mod3 = mod8[:n_samp + 1]
    shift3 = mod3[:, None, 0:d]
    scale3 = mod3[:, None, d:2 * d]
    gate3 = mod3[:, None, 2 * d:3 * d]

    px, hxt = _normproj(x_all, norm_w, scale3, shift3, w_in_g, tps, n_samp)

    states0 = _ctx_state_fwd(px, lg, n_samp, t_lat, lc)
    o_f, o_b, saved = _ret_fwd(px, states0, lg, n_samp, seq)
    y_ret = _retnorm_fwd(o_f, o_b, px)

    qn = _att_prep_q(px, cos_all, sin_all, q_norm_w, t_lat)
    kn, vn = _att_prep_kv(px, cos_all, sin_all, k_norm_w)
    y_att, o_att = _att_fwd(qn, kn, vn, px, n_samp, seq, lc)

    (gx_res, dy_ret, dy_att, dmg, loss8, dgate, g_w_o_ret, g_w_o_att, g_w_out) = _merge(
        x_lat, loss_target.reshape(t_lat, d), y_ret, y_att, px, gate3, w_o_ret, w_o_att, w_out, tps)

    d_att_q, d_att_g, dkl, dkc, dvl, dvc, gqw = _att_bwd(
        qn, kn, vn, px, o_att, dy_att, cos_all, sin_all, q_norm_w, n_samp, seq, lc)
    d_att_kv, gkw = _att_kv_bwd(jnp.concatenate([dkl, dkc], axis=0), jnp.concatenate([dvl, dvc], axis=0),
                                px, cos_all, sin_all, k_norm_w)

    do, d_ret_g = _retnorm_bwd(dy_ret, o_f, o_b, px)
    *d_dirs, dstates, dlg_lat = _ret_bwd(px, do, saved, lg, n_samp, seq)
    d_ret_q, d_ret_k, d_ret_v = _ret_combine(*d_dirs)
    dc_ret_k, dc_ret_v, dlg_ctx = _ctx_state_bwd(px, dstates, lg, n_samp, t_lat, lc)

    dpx_lat = jnp.concatenate([d_ret_k, d_ret_v, d_att_kv[:t_lat], d_ret_q, d_ret_g, d_att_q, d_att_g, dmg], axis=1)
    dpx_ctx = jnp.concatenate([dc_ret_k, dc_ret_v, d_att_kv[t_lat:], jnp.zeros((t_ctx, IN_COLS - KV_COLS), BF)],
                              axis=1)
    dpx_all = jnp.concatenate([dpx_lat, dpx_ctx], axis=0)

    g_w_in = _gw_in(hxt, dpx_all)
    dhx = _dhx(dpx_all, w_in_g)
    grad_x, dshift, dscale, g_norm_w = _norm_bwd(x_all, dhx, gx_res, norm_w, scale3, shift3, tps, n_samp)

    dgate_all = jnp.concatenate([dgate, jnp.zeros((1, 1, d), F32)], axis=0)
    dmod3 = jnp.concatenate([dshift, dscale, dgate_all], axis=2).reshape(n_samp + 1, 3 * d)
    dmod8 = jnp.concatenate([dmod3, jnp.zeros((8 - n_samp - 1, 3 * d), F32)], axis=0)
    g_w_ada, g_b_ada, dc8 = _adaln_bwd(cvec8, dmod8, w_ada_g)

    g_lg = (jnp.sum(dlg_lat[:, :, 0], axis=0).reshape(2, RET_HEADS)
            + jnp.stack([jnp.sum(dlg_ctx[:, :, 0, 0], axis=0), jnp.sum(dlg_ctx[:, :, 1, 0], axis=0)], axis=0))
    small = _pack_small(dc8[n_samp], g_norm_w, g_b_ada, g_lg, jnp.sum(gqw, axis=(0, 1, 2)), gkw)
    return (loss8[0, 0], grad_x.reshape(n_samp, seq, d),
            (g_w_ada, g_w_in, g_w_o_ret, g_w_o_att, g_w_out), small)


def kernel(x, c, ctx, c_ctx, norm_w, w_ada, b_ada, w_in, ret_log2_decay, q_norm_w, k_norm_w, w_o_ret, w_o_att, w_out, loss_target, m_c_ctx, m_norm_w, m_w_ada, m_b_ada, m_w_in, m_ret_log2_decay, m_q_norm_w, m_k_norm_w, m_w_o_ret, m_w_o_att, m_w_out, v_c_ctx, v_norm_w, v_w_ada, v_b_ada, v_w_in, v_ret_log2_decay, v_q_norm_w, v_k_norm_w, v_w_o_ret, v_w_o_att, v_w_out):
    big_w = (w_ada[0], w_in[0], w_o_ret[0], w_o_att[0], w_out[0])
    big_m = (m_w_ada[0], m_w_in[0], m_w_o_ret[0], m_w_o_att[0], m_w_out[0])
    big_v = (v_w_ada[0], v_w_in[0], v_w_o_ret[0], v_w_o_att[0], v_w_out[0])

    ids = _place_ids()
    gathered = _all_gather_weights(tuple(_cast_place(w, ids) for w in big_w))
    w_ada_g, w_in_g = gathered[0], gathered[1]
    w_o_ret_f = gathered[2].reshape(-1, D_MODEL)
    w_o_att_f = gathered[3].reshape(-1, D_MODEL)
    w_out_f = gathered[4].reshape(-1, D_MODEL)

    loss_local, grad_x, big_g, small_g = _local_step(
        x, c, ctx, c_ctx, norm_w[0:1], b_ada[0:1], ret_log2_decay[0], q_norm_w[0:1], k_norm_w[0:1], loss_target,
        w_ada_g, w_in_g, w_o_ret_f, w_o_att_f, w_out_f)
    loss = lax.psum(loss_local, ("x", "y", "c"))

    g_sm = (big_g[0], big_g[1], big_g[2].reshape(N_SHARD, -1, D_MODEL), big_g[3].reshape(N_SHARD, -1, D_MODEL),
            big_g[4].reshape(N_SHARD, -1, D_MODEL))
    from_sibling = _swap_halves(g_sm)
    chip_sums = tuple(_chip_sum(g, p, ids) for g, p in zip(g_sm, from_sibling))
    from_chips = _exchange_shards(tuple(t16 for _, t16 in chip_sums))
    big_grad = _join_halves(tuple(_shard_sum(t, q, ids) for (t, _), q in zip(chip_sums, from_chips)))

    small_grad_in = _all_reduce_small(small_g)
    small_w = _pack_small(c_ctx, norm_w, b_ada, ret_log2_decay, q_norm_w, k_norm_w)
    small_m = _pack_small(m_c_ctx, m_norm_w, m_b_ada, m_ret_log2_decay, m_q_norm_w, m_k_norm_w)
    small_v = _pack_small(v_c_ctx, v_norm_w, v_b_ada, v_ret_log2_decay, v_q_norm_w, v_k_norm_w)
    small_grad, small_delta, small_nm, small_nv = _adamw_small(small_w, small_grad_in, small_m, small_v)

    big_delta, big_nm, big_nv = [], [], []
    for w, g, m, v in zip(big_w, big_grad, big_m, big_v):
        dlt, nm, nv = _adamw(w, g, m, v)
        big_delta.append(dlt[None])
        big_nm.append(nm[None])
        big_nv.append(nv[None])
    big_grad = [g[None] for g in big_grad]

    def order(small_packed, big):
        s = _unpack_small(small_packed)
        return (s[0], s[1], big[0], s[2], big[1], s[3], s[4], s[5], big[2], big[3], big[4])

    return (loss, grad_x, *order(small_grad, big_grad), *order(small_delta, big_delta),
            *order(small_nm, big_nm), *order(small_nv, big_nv))
```

```python
import functools

import jax
import jax.numpy as jnp
from jax import lax
from jax.experimental import pallas as pl
from jax.experimental.pallas import tpu as pltpu

F32 = jnp.float32
BF = jnp.bfloat16
SDS = jax.ShapeDtypeStruct
MESH = pl.DeviceIdType.MESH
ANY = pl.BlockSpec(memory_space=pl.ANY)
SMEM = pl.BlockSpec(memory_space=pltpu.SMEM)

D_MODEL = 1024
GRID_W = 64
RET_HEADS = 4
RET_DK = 256
RET_DV = 512
RET_CHUNK = 128
ATT_HEADS = 8
ATT_KV_HEADS = 2
ATT_REP = ATT_HEADS // ATT_KV_HEADS
ATT_HEAD_DIM = 128
ROPE_THETA = 10000.0
NORM_EPS = 1e-6
IN_COLS = 10752
KV_COLS = 3584
C_RK, C_RV, C_AK, C_AV, C_RQ, C_RG, C_AQ, C_AG, C_MR, C_MA = 0, 1024, 3072, 3328, 3584, 4608, 6656, 7680, 8704, 9728
N_SHARD = 4
ADA_W = 3 * D_MODEL // N_SHARD
IN_W = IN_COLS // N_SHARD
IN_BLK = IN_W // 3
N_IN_BLK = IN_COLS // IN_BLK
TM = 512
ADAM_LR, ADAM_B1, ADAM_B2, ADAM_EPS, ADAM_WD, ADAM_STEP = 0.001, 0.9, 0.999, 1e-08, 0.01, 10
MIB = 1024 * 1024


def _cp(sem=None, vmem_mb=None, **kw):
    if sem is not None:
        kw["dimension_semantics"] = sem
    if vmem_mb is not None:
        kw["vmem_limit_bytes"] = vmem_mb * MIB
    return pltpu.CompilerParams(**kw)


def _dot(a, b, ca=1, cb=0):
    return lax.dot_general(a.astype(BF), b.astype(BF), (((ca,), (cb,)), ((), ())), preferred_element_type=F32)


def _sigmoid(x):
    return 1.0 / (1.0 + jnp.exp(-x))


def _sum_all(x):
    return jnp.sum(jnp.sum(x, axis=1, keepdims=True), axis=0, keepdims=True)


def _swap_pairs(x):
    ax = x.ndim - 1
    lane = lax.broadcasted_iota(jnp.int32, x.shape, ax)
    nxt = pltpu.roll(x, x.shape[ax] - 1, ax)
    prv = pltpu.roll(x, 1, ax)
    return jnp.where(lane % 2 == 0, nxt, prv)


def _rms(x):
    return lax.rsqrt(jnp.mean(x * x, axis=-1, keepdims=True) + NORM_EPS)


def _rms_bwd(dxh, xh, r):
    return r * (dxh - xh * jnp.mean(dxh * xh, axis=-1, keepdims=True))


def _adaln_fwd(cvec8, w_ada_g, b_ada):
    def body(c_ref, w_ref, b_ref, o_ref):
        cv = c_ref[...]
        sc = (cv * _sigmoid(cv)).astype(BF)
        for s in range(N_SHARD):
            cols = slice(s * ADA_W, (s + 1) * ADA_W)
            o_ref[:, cols] = jnp.dot(sc, w_ref[s], preferred_element_type=F32) + b_ref[:, cols]

    return pl.pallas_call(body, out_shape=SDS((8, 3 * D_MODEL), F32), name="adaln_fwd",
                          compiler_params=_cp(vmem_mb=32))(cvec8, w_ada_g, b_ada)


def _adaln_bwd(cvec8, dmod8, w_ada_g):
    def body(c_ref, d_ref, w_ref, gw_ref, gb_ref, dc_ref):
        cv = c_ref[...]
        sg = _sigmoid(cv)
        sc = cv * sg
        dm = d_ref[...]
        gb_ref[...] = jnp.sum(dm, axis=0, keepdims=True)
        dsc = jnp.zeros((8, D_MODEL), F32)
        for s in range(N_SHARD):
            cols = slice(s * ADA_W, (s + 1) * ADA_W)
            gw_ref[s] = _dot(sc, dm[:, cols], 0, 0)
            dsc = dsc + _dot(dm[:, cols], w_ref[s], 1, 1)
        dc_ref[...] = dsc * (sg * (1.0 + cv * (1.0 - sg)))

    return pl.pallas_call(
        body, name="adaln_bwd",
        out_shape=(SDS((N_SHARD, D_MODEL, ADA_W), F32), SDS((1, 3 * D_MODEL), F32), SDS((8, D_MODEL), F32)),
        compiler_params=_cp(vmem_mb=48))(cvec8, dmod8, w_ada_g)


def _normproj(x_lat, x_ctx, norm_w, scale3, shift3, w_in_g, tiles_per_sample, n_samp):
    n_lat = x_lat.shape[0] // TM
    rows = x_lat.shape[0] + x_ctx.shape[0]

    def samp(i):
        return jnp.minimum(i // tiles_per_sample, n_samp)

    def body(x_ref, c_ref, nw_ref, sc_ref, sh_ref, w_ref, px_ref, hxt_ref, hx_s):
        is_lat = pl.program_id(0) < n_lat

        @pl.when(pl.program_id(1) == 0)
        def _():
            x = jnp.where(is_lat, x_ref[...], c_ref[...])
            h = x * _rms(x) * nw_ref[...] * (1.0 + sc_ref[...]) + sh_ref[...]
            hx_s[...] = h.astype(BF)
            hxt_ref[...] = h.T.astype(BF)

        px_ref[...] = jnp.dot(hx_s[...], w_ref[...], preferred_element_type=F32)

    return pl.pallas_call(
        body, name="normproj", grid=(rows // TM, N_IN_BLK),
        in_specs=[pl.BlockSpec((TM, D_MODEL), lambda i, j: (jnp.minimum(i, n_lat - 1), 0)),
                  pl.BlockSpec((TM, D_MODEL), lambda i, j: (jnp.maximum(i - n_lat, 0), 0)),
                  pl.BlockSpec((1, D_MODEL), lambda i, j: (0, 0)),
                  pl.BlockSpec((None, 1, D_MODEL), lambda i, j: (samp(i), 0, 0)),
                  pl.BlockSpec((None, 1, D_MODEL), lambda i, j: (samp(i), 0, 0)),
                  pl.BlockSpec((None, D_MODEL, IN_BLK), lambda i, j: (j // 3, 0, j % 3))],
        out_specs=(pl.BlockSpec((TM, IN_BLK), lambda i, j: (i, j)),
                   pl.BlockSpec((D_MODEL, TM), lambda i, j: (0, i))),
        out_shape=(SDS((rows, IN_COLS), F32), SDS((D_MODEL, rows), BF)),
        scratch_shapes=[pltpu.VMEM((TM, D_MODEL), BF)],
        compiler_params=_cp(("parallel", "arbitrary"), 40))(x_lat, x_ctx, norm_w, scale3, shift3, w_in_g)


def _norm_bwd(x_lat, x_ctx, dhx, gx_res, norm_w, scale3, tiles_per_sample, n_samp):
    rows = x_lat.shape[0] + x_ctx.shape[0]
    n_lat = tiles_per_sample * n_samp

    def samp(i):
        return jnp.minimum(i // tiles_per_sample, n_samp)

    def lat(i):
        return jnp.minimum(i, n_lat - 1)

    def body(x_ref, c_ref, dh_ref, gr_ref, nw_ref, sc_ref, gx_ref, dsh_ref, dsc_ref, dnw_ref):
        i = pl.program_id(0)
        x = jnp.where(i < n_lat, x_ref[...], c_ref[...])
        r = _rms(x)
        xh = x * r
        nw = nw_ref[...]
        dh = dh_ref[...]
        first = jnp.logical_or(i % tiles_per_sample == 0, i >= n_lat)

        @pl.when(first)
        def _():
            dsh_ref[...] = jnp.zeros_like(dsh_ref)
            dsc_ref[...] = jnp.zeros_like(dsc_ref)

        @pl.when(i == 0)
        def _():
            dnw_ref[...] = jnp.zeros_like(dnw_ref)

        dsh_ref[...] += jnp.sum(dh, axis=0, keepdims=True)
        dsc_ref[...] += jnp.sum(dh * (xh * nw), axis=0, keepdims=True)
        du = dh * (1.0 + sc_ref[...])
        dnw_ref[...] += jnp.sum(du * xh, axis=0, keepdims=True)

        @pl.when(i < n_lat)
        def _():
            gx_ref[...] = gr_ref[...] + _rms_bwd(du * nw, xh, r)

    return pl.pallas_call(
        body, name="norm_bwd", grid=(rows // TM,),
        in_specs=[pl.BlockSpec((TM, D_MODEL), lambda i: (lat(i), 0)),
                  pl.BlockSpec((TM, D_MODEL), lambda i: (jnp.maximum(i - n_lat, 0), 0)),
                  pl.BlockSpec((TM, D_MODEL), lambda i: (i, 0)),
                  pl.BlockSpec((TM, D_MODEL), lambda i: (lat(i), 0)),
                  pl.BlockSpec((1, D_MODEL), lambda i: (0, 0)),
                  pl.BlockSpec((None, 1, D_MODEL), lambda i: (samp(i), 0, 0))],
        out_specs=(pl.BlockSpec((TM, D_MODEL), lambda i: (lat(i), 0)),
                   pl.BlockSpec((None, 1, D_MODEL), lambda i: (samp(i), 0, 0)),
                   pl.BlockSpec((None, 1, D_MODEL), lambda i: (samp(i), 0, 0)),
                   pl.BlockSpec((1, D_MODEL), lambda i: (0, 0))),
        out_shape=(SDS((n_lat * TM, D_MODEL), F32), SDS((n_samp + 1, 1, D_MODEL), F32),
                   SDS((n_samp + 1, 1, D_MODEL), F32), SDS((1, D_MODEL), F32)),
        compiler_params=_cp(("arbitrary",), 40))(x_lat, x_ctx, dhx, gx_res, norm_w, scale3)


def _gw_in(hxt, dpx_all):
    rows = dpx_all.shape[0]

    def body(h_ref, d_ref, o_ref):
        @pl.when(pl.program_id(1) == 0)
        def _():
            o_ref[...] = jnp.zeros_like(o_ref)

        o_ref[...] += jnp.dot(h_ref[...], d_ref[...], preferred_element_type=F32)

    return pl.pallas_call(
        body, name="gw_in", grid=(N_IN_BLK, rows // TM),
        in_specs=[pl.BlockSpec((D_MODEL, TM), lambda j, i: (0, i)),
                  pl.BlockSpec((TM, IN_BLK), lambda j, i: (i, j))],
        out_specs=pl.BlockSpec((None, D_MODEL, IN_BLK), lambda j, i: (j // 3, 0, j % 3)),
        out_shape=SDS((N_SHARD, D_MODEL, IN_W), F32),
        compiler_params=_cp(("parallel", "arbitrary"), 40))(hxt, dpx_all)


def _dhx(dpx_all, w_in_g):
    rows = dpx_all.shape[0]

    def body(d_ref, w_ref, o_ref):
        @pl.when(pl.program_id(1) == 0)
        def _():
            o_ref[...] = jnp.zeros_like(o_ref)

        o_ref[...] += lax.dot_general(d_ref[...], w_ref[...], (((1,), (1,)), ((), ())), preferred_element_type=F32)

    return pl.pallas_call(
        body, name="dhx", grid=(rows // TM, N_IN_BLK),
        in_specs=[pl.BlockSpec((TM, IN_BLK), lambda i, j: (i, j)),
                  pl.BlockSpec((None, D_MODEL, IN_BLK), lambda i, j: (j // 3, 0, j % 3))],
        out_specs=pl.BlockSpec((TM, D_MODEL), lambda i, j: (i, 0)),
        out_shape=SDS((rows, D_MODEL), F32),
        compiler_params=_cp(("parallel", "arbitrary"), 40))(dpx_all, w_in_g)


def _decays(lgv, d):
    c = RET_CHUNK
    ii = lax.broadcasted_iota(jnp.int32, (c, 1), 0).astype(F32)
    jj = lax.broadcasted_iota(jnp.int32, (1, c), 1).astype(F32)
    a_i = jnp.where(d == 0, ii, c - 1.0 - ii)
    a_j = jnp.where(d == 0, jj, c - 1.0 - jj)
    rel = a_i - a_j
    mask = jnp.where(rel >= 0, jnp.exp(lgv * jnp.maximum(rel, 0.0)), 0.0)
    qd = jnp.exp(lgv * (a_i + 1.0))
    kd = jnp.exp(lgv * (c - 1.0 - a_i))
    gc = jnp.exp(jnp.full((1, 1), lgv * c, F32))
    return a_i, rel, mask, qd, kd, gc


def _ctx_state_fwd(px, lg, n_samp, t_lat, lc):
    rb = t_lat // lc

    def body(lg_ref, k_ref, v_ref, o_ref):
        h = pl.program_id(1)
        k = k_ref[...] * (RET_DK ** -0.5)
        v = v_ref[...]
        pos = lax.broadcasted_iota(jnp.int32, (lc, 1), 0).astype(F32)
        o_ref[0] = _dot(k * jnp.exp(lg_ref[0, h] * (lc - 1.0 - pos)), v, 0, 0)
        o_ref[1] = _dot(k * jnp.exp(lg_ref[1, h] * pos), v, 0, 0)

    return pl.pallas_call(
        body, name="ctx_state_fwd", grid=(n_samp, RET_HEADS),
        in_specs=[SMEM,
                  pl.BlockSpec((lc, RET_DK), lambda b, h: (rb + b, C_RK // RET_DK + h)),
                  pl.BlockSpec((lc, RET_DV), lambda b, h: (rb + b, C_RV // RET_DV + h))],
        out_specs=pl.BlockSpec((None, 2, None, RET_DK, RET_DV), lambda b, h: (b, 0, h, 0, 0)),
        out_shape=SDS((n_samp, 2, RET_HEADS, RET_DK, RET_DV), F32),
        compiler_params=_cp(("parallel", "parallel")))(lg, px, px)


def _ctx_state_bwd(dpx, px, dstates, lg, n_samp, t_lat, lc):
    rb = t_lat // lc
    kspec = pl.BlockSpec((lc, RET_DK), lambda b, h: (rb + b, C_RK // RET_DK + h))
    vspec = pl.BlockSpec((lc, RET_DV), lambda b, h: (rb + b, C_RV // RET_DV + h))
    sspec = pl.BlockSpec((None, 2, None, RET_DK, RET_DV), lambda b, h: (b, 0, h, 0, 0))

    def weights(lg_ref, h):
        pos = lax.broadcasted_iota(jnp.int32, (lc, 1), 0).astype(F32)
        e_f = lc - 1.0 - pos
        return pos, e_f, jnp.exp(lg_ref[0, h] * e_f), jnp.exp(lg_ref[1, h] * pos)

    def k_body(lg_ref, dpx_hbm, k_ref, v_ref, ds_ref, dk_ref, dlg_ref):
        pos, e_f, w_f, w_b = weights(lg_ref, pl.program_id(1))
        k = k_ref[...] * (RET_DK ** -0.5)
        y_f = _dot(v_ref[...], ds_ref[0], 1, 1) * w_f
        y_b = _dot(v_ref[...], ds_ref[1], 1, 1) * w_b
        dk_ref[...] = ((y_f + y_b) * (RET_DK ** -0.5)).astype(BF)
        t_f = _sum_all(e_f * k * y_f)
        t_b = _sum_all(pos * k * y_b)
        sub = lax.broadcasted_iota(jnp.int32, (8, 128), 0)
        dlg_ref[...] = jnp.where(sub == 0, t_f, jnp.where(sub == 1, t_b, 0.0))

    def v_body(lg_ref, dpx_hbm, k_ref, ds_ref, dv_ref):
        _, _, w_f, w_b = weights(lg_ref, pl.program_id(1))
        k = k_ref[...] * (RET_DK ** -0.5)
        dv_ref[...] = (_dot(k * w_f, ds_ref[0]) + _dot(k * w_b, ds_ref[1])).astype(BF)

    dpx, dlg = pl.pallas_call(
        k_body, name="ctx_state_bwd_k", grid=(n_samp, RET_HEADS), input_output_aliases={1: 0},
        in_specs=[SMEM, ANY, kspec, vspec, sspec],
        out_specs=(kspec, pl.BlockSpec((None, None, 8, 128), lambda b, h: (b, h, 0, 0))),
        out_shape=(SDS(dpx.shape, dpx.dtype), SDS((n_samp, RET_HEADS, 8, 128), F32)),
        compiler_params=_cp(("parallel", "parallel")))(lg, dpx, px, px, dstates)
    dpx = pl.pallas_call(
        v_body, name="ctx_state_bwd_v", grid=(n_samp, RET_HEADS), input_output_aliases={1: 0},
        in_specs=[SMEM, ANY, kspec, sspec], out_specs=vspec, out_shape=SDS(dpx.shape, dpx.dtype),
        compiler_params=_cp(("parallel", "parallel")))(lg, dpx, px, dstates)
    return dpx, dlg


def _zero_ctx_tail(dpx, t_lat):
    wb = 512
    n_ctx = (dpx.shape[0] - t_lat) // TM

    def body(dpx_hbm, o_ref):
        o_ref[...] = jnp.zeros_like(o_ref)

    return pl.pallas_call(
        body, name="zero_ctx_tail", grid=(n_ctx, (IN_COLS - KV_COLS) // wb), input_output_aliases={0: 0},
        in_specs=[ANY], out_specs=pl.BlockSpec((TM, wb), lambda i, j: (t_lat // TM + i, KV_COLS // wb + j)),
        out_shape=SDS(dpx.shape, dpx.dtype),
        compiler_params=_cp(("parallel", "parallel")))(dpx)


def _ret_specs(row_f, row_b):
    c = RET_CHUNK
    wq = RET_HEADS * RET_DK // 2
    wv = RET_HEADS * RET_DV // 2
    specs = []
    for row in (row_f, row_b):
        specs += [pl.BlockSpec((c, wq), lambda b, n, row=row: (row(b, n), C_RQ // wq)),
                  pl.BlockSpec((c, wq), lambda b, n, row=row: (row(b, n), C_RQ // wq + 1)),
                  pl.BlockSpec((c, 2 * wq), lambda b, n, row=row: (row(b, n), C_RK // (2 * wq))),
                  pl.BlockSpec((c, wv), lambda b, n, row=row: (row(b, n), C_RV // wv)),
                  pl.BlockSpec((c, wv), lambda b, n, row=row: (row(b, n), C_RV // wv + 1))]
    return specs


def _ret_head(refs, h):
    q0, q1, k_ref, v0, v1 = refs
    hh = h % 2
    q = (q0, q1)[h // 2][:, hh * RET_DK:(hh + 1) * RET_DK]
    k = k_ref[:, h * RET_DK:(h + 1) * RET_DK] * (RET_DK ** -0.5)
    v = (v0, v1)[h // 2][:, hh * RET_DV:(hh + 1) * RET_DV]
    return q, k, v


def _ret_fwd(px, states0, lg, n_samp, seq):
    c = RET_CHUNK
    nc = seq // c
    t_lat = n_samp * seq
    wo = RET_HEADS * RET_DV

    def row_f(b, n):
        return b * nc + n

    def row_b(b, n):
        return b * nc + nc - 1 - n

    def body(lg_ref, *refs):
        ins, (s0_ref, of_ref, ob_ref, st_ref, s_s) = refs[:10], refs[10:]

        @pl.when(pl.program_id(1) == 0)
        def _():
            s_s[...] = s0_ref[...]

        for d, o_ref in ((0, of_ref), (1, ob_ref)):
            for h in range(RET_HEADS):
                _, _, mask, qd, kd, gc = _decays(lg_ref[d, h], d)
                q, k, v = _ret_head(ins[5 * d:5 * d + 5], h)
                s = s_s[d, h]
                st_ref[h, d] = s.astype(BF)
                sc = _dot(q, k, 1, 1) * mask
                o_ref[:, h * RET_DV:(h + 1) * RET_DV] = _dot(sc, v) + _dot(q * qd, s)
                s_s[d, h] = s * gc + _dot(k * kd, v, 0, 0)

    return pl.pallas_call(
        body, name="ret_fwd", grid=(n_samp, nc),
        in_specs=[SMEM] + _ret_specs(row_f, row_b) + [
            pl.BlockSpec((None, 2, RET_HEADS, RET_DK, RET_DV), lambda b, n: (b, 0, 0, 0, 0))],
        out_specs=(pl.BlockSpec((c, wo), lambda b, n: (row_f(b, n), 0)),
                   pl.BlockSpec((c, wo), lambda b, n: (row_b(b, n), 0)),
                   pl.BlockSpec((None, RET_HEADS, 2, None, RET_DK, RET_DV), lambda b, n: (b, 0, 0, n, 0, 0))),
        out_shape=(SDS((t_lat, wo), F32), SDS((t_lat, wo), F32),
                   SDS((n_samp, RET_HEADS, 2, nc, RET_DK, RET_DV), BF)),
        scratch_shapes=[pltpu.VMEM((2, RET_HEADS, RET_DK, RET_DV), F32)],
        compiler_params=_cp(("parallel", "arbitrary"), 48))(lg, *([px] * 10), states0)


def _ret_bwd(px, do, saved, lg, n_samp, seq):
    c = RET_CHUNK
    nc = seq // c
    t_lat = n_samp * seq
    wq, wo = RET_HEADS * RET_DK, RET_HEADS * RET_DV

    def row_f(b, n):
        return b * nc + nc - 1 - n

    def row_b(b, n):
        return b * nc + n

    def body(lg_ref, *refs):
        ins = refs[:10]
        (dof_ref, dob_ref, st_ref, dqf, dkf, dvf, dqb, dkb, dvb, ds0_ref, dlg_ref, ds_s, acc_s) = refs[10:]
        n = pl.program_id(1)

        @pl.when(n == 0)
        def _():
            ds_s[...] = jnp.zeros_like(ds_s)
            acc_s[...] = jnp.zeros_like(acc_s)

        for d, (do_ref, dq_ref, dk_ref, dv_ref) in enumerate(((dof_ref, dqf, dkf, dvf), (dob_ref, dqb, dkb, dvb))):
            for h in range(RET_HEADS):
                a_i, rel, mask, qd, kd, gc = _decays(lg_ref[d, h], d)
                q, k, v = _ret_head(ins[5 * d:5 * d + 5], h)
                qb, kb, vb = q.astype(BF), k.astype(BF), v.astype(BF)
                dob = do_ref[:, h * RET_DV:(h + 1) * RET_DV].astype(BF)
                sb = st_ref[h, d]
                ds = ds_s[d, h]
                dsb = ds.astype(BF)
                raw = _dot(qb, kb, 1, 1)
                sc = raw * mask
                dsc = _dot(dob, vb, 1, 1) * mask
                dscb = dsc.astype(BF)
                x = _dot(dob, sb, 1, 1)
                y = _dot(vb, dsb, 1, 1)
                qq = q * qd
                kk = k * kd
                dq_ref[:, h * RET_DK:(h + 1) * RET_DK] = _dot(dscb, kb) + x * qd
                dk_ref[:, h * RET_DK:(h + 1) * RET_DK] = _dot(dscb, qb, 0, 0) + y * kd
                dv_ref[:, h * RET_DV:(h + 1) * RET_DV] = _dot(sc, dob, 0, 0) + _dot(kk, dsb)
                t = (_sum_all(dsc * raw * rel) + _sum_all((a_i + 1.0) * qq * x)
                     + _sum_all((c - 1.0 - a_i) * kk * y) + c * gc * _sum_all(ds * sb.astype(F32)))
                acc_s[4 * d + h:4 * d + h + 1, :] += t
                ds_s[d, h] = ds * gc + _dot(qq, dob, 0, 0)

        @pl.when(n == nc - 1)
        def _():
            ds0_ref[...] = ds_s[...]
            dlg_ref[...] = acc_s[...]

    do_spec_f = pl.BlockSpec((c, wo), lambda b, n: (row_f(b, n), 0))
    do_spec_b = pl.BlockSpec((c, wo), lambda b, n: (row_b(b, n), 0))
    dq_spec_f = pl.BlockSpec((c, wq), lambda b, n: (row_f(b, n), 0))
    dq_spec_b = pl.BlockSpec((c, wq), lambda b, n: (row_b(b, n), 0))
    return pl.pallas_call(
        body, name="ret_bwd", grid=(n_samp, nc),
        in_specs=[SMEM] + _ret_specs(row_f, row_b) + [
            do_spec_f, do_spec_b,
            pl.BlockSpec((None, RET_HEADS, 2, None, RET_DK, RET_DV), lambda b, n: (b, 0, 0, nc - 1 - n, 0, 0))],
        out_specs=(dq_spec_f, dq_spec_f, do_spec_f, dq_spec_b, dq_spec_b, do_spec_b,
                   pl.BlockSpec((None, 2, RET_HEADS, RET_DK, RET_DV), lambda b, n: (b, 0, 0, 0, 0)),
                   pl.BlockSpec((None, 8, 128), lambda b, n: (b, 0, 0))),
        out_shape=(SDS((t_lat, wq), F32), SDS((t_lat, wq), F32), SDS((t_lat, wo), F32),
                   SDS((t_lat, wq), F32), SDS((t_lat, wq), F32), SDS((t_lat, wo), F32),
                   SDS((n_samp, 2, RET_HEADS, RET_DK, RET_DV), F32), SDS((n_samp, 8, 128), F32)),
        scratch_shapes=[pltpu.VMEM((2, RET_HEADS, RET_DK, RET_DV), F32), pltpu.VMEM((8, 128), F32)],
        compiler_params=_cp(("parallel", "arbitrary"), 56))(lg, *([px] * 10), do, do, saved)


def _combine_into(dpx, a, b, col0, scale):
    t_lat, width = a.shape
    wb = 512
    assert col0 % wb == 0 and width % wb == 0

    def body(dpx_hbm, a_ref, b_ref, o_ref):
        o_ref[...] = ((a_ref[...] + b_ref[...]) * scale).astype(BF)

    src = pl.BlockSpec((TM, wb), lambda i, j: (i, j))
    return pl.pallas_call(
        body, name="combine_into", grid=(t_lat // TM, width // wb), input_output_aliases={0: 0},
        in_specs=[ANY, src, src], out_specs=pl.BlockSpec((TM, wb), lambda i, j: (i, col0 // wb + j)),
        out_shape=SDS(dpx.shape, dpx.dtype),
        compiler_params=_cp(("parallel", "parallel")))(dpx, a, b)


def _retnorm_fwd(o_f, o_b, px):
    t_lat = o_f.shape[0]

    def body(of_ref, ob_ref, g_ref, y_ref):
        o = of_ref[...] + ob_ref[...]
        g = g_ref[...]
        y_ref[...] = (o * _rms(o) * (g * _sigmoid(g))).astype(BF)

    so = pl.BlockSpec((TM, RET_DV), lambda i, h: (i, h))
    return pl.pallas_call(
        body, name="retnorm_fwd", grid=(t_lat // TM, RET_HEADS),
        in_specs=[so, so, pl.BlockSpec((TM, RET_DV), lambda i, h: (i, C_RG // RET_DV + h))],
        out_specs=so,
        out_shape=SDS((t_lat, RET_HEADS * RET_DV), BF),
        compiler_params=_cp(("parallel", "parallel")))(o_f, o_b, px)


def _retnorm_bwd(dpx, dy, o_f, o_b, px):
    t_lat = o_f.shape[0]

    def body(dpx_hbm, dy_ref, of_ref, ob_ref, g_ref, do_ref, dg_ref):
        o = of_ref[...] + ob_ref[...]
        r = _rms(o)
        on = o * r
        g = g_ref[...]
        sg = _sigmoid(g)
        dy_ = dy_ref[...]
        dg_ref[...] = (dy_ * on * (sg * (1.0 + g * (1.0 - sg)))).astype(BF)
        do_ref[...] = _rms_bwd(dy_ * (g * sg), on, r)

    so = pl.BlockSpec((TM, RET_DV), lambda i, h: (i, h))
    gcol = pl.BlockSpec((TM, RET_DV), lambda i, h: (i, C_RG // RET_DV + h))
    return pl.pallas_call(
        body, name="retnorm_bwd", grid=(t_lat // TM, RET_HEADS), input_output_aliases={0: 1},
        in_specs=[ANY, so, so, so, gcol],
        out_specs=(so, gcol),
        out_shape=(SDS((t_lat, RET_HEADS * RET_DV), F32), SDS(dpx.shape, dpx.dtype)),
        compiler_params=_cp(("parallel", "parallel")))(dpx, dy, o_f, o_b, px)


def _norm_rope(x, w, cos, sin):
    xn = x * _rms(x) * w
    return xn * cos + _swap_pairs(xn) * sin


def _norm_rope_bwd(dy, x, w, cos, sin):
    dxn = dy * cos + _swap_pairs(dy * sin)
    r = _rms(x)
    xh = x * r
    return _rms_bwd(dxn * w, xh, r), jnp.sum(dxn * xh, axis=0, keepdims=True)


def _att_prep_q(px, cos_all, sin_all, qnw, t_lat):
    hd = ATT_HEAD_DIM
    wblk = ATT_REP * hd

    def body(x_ref, cos_ref, sin_ref, w_ref, o_ref):
        for r in range(ATT_REP):
            cols = slice(r * hd, (r + 1) * hd)
            qr = _norm_rope(x_ref[:, cols], w_ref[...], cos_ref[...], sin_ref[...])
            o_ref[:, cols] = (qr * (hd ** -0.5)).astype(BF)

    return pl.pallas_call(
        body, name="att_prep_q", grid=(t_lat // TM, ATT_KV_HEADS),
        in_specs=[pl.BlockSpec((TM, wblk), lambda i, g: (i, C_AQ // wblk + g)),
                  pl.BlockSpec((TM, hd), lambda i, g: (i, 0)),
                  pl.BlockSpec((TM, hd), lambda i, g: (i, 0)),
                  pl.BlockSpec((1, hd), lambda i, g: (0, 0))],
        out_specs=pl.BlockSpec((TM, wblk), lambda i, g: (i, g)),
        out_shape=SDS((t_lat, ATT_HEADS * hd), BF),
        compiler_params=_cp(("parallel", "parallel")))(px, cos_all, sin_all, qnw)


def _att_prep_kv(px, cos_all, sin_all, knw):
    rows = px.shape[0]
    hd = ATT_HEAD_DIM
    kvw = ATT_KV_HEADS * hd

    def body(x_ref, cos_ref, sin_ref, w_ref, k_ref, v_ref):
        for g in range(ATT_KV_HEADS):
            cols = slice(g * hd, (g + 1) * hd)
            k_ref[:, cols] = _norm_rope(x_ref[:, cols], w_ref[...], cos_ref[...], sin_ref[...]).astype(BF)
        v_ref[...] = x_ref[:, kvw:].astype(BF)

    return pl.pallas_call(
        body, name="att_prep_kv", grid=(rows // TM,),
        in_specs=[pl.BlockSpec((TM, 2 * kvw), lambda i: (i, C_AK // (2 * kvw))),
                  pl.BlockSpec((TM, hd), lambda i: (i, 0)),
                  pl.BlockSpec((TM, hd), lambda i: (i, 0)),
                  pl.BlockSpec((1, hd), lambda i: (0, 0))],
        out_specs=(pl.BlockSpec((TM, kvw), lambda i: (i, 0)), pl.BlockSpec((TM, kvw), lambda i: (i, 0))),
        out_shape=(SDS((rows, kvw), BF), SDS((rows, kvw), BF)),
        compiler_params=_cp(("parallel",)))(px, cos_all, sin_all, knw)


def _att_kv_bwd(dpx, dkl, dkc, dvl, dvc, px, cos_all, sin_all, knw):
    rows = px.shape[0]
    hd = ATT_HEAD_DIM
    kvw = ATT_KV_HEADS * hd
    n_lat = dkl.shape[0] // TM
    assert dkc.shape[0] == TM

    def body(dpx_hbm, dkl_ref, dkc_ref, dvl_ref, dvc_ref, x_ref, cos_ref, sin_ref, w_ref, o_ref, gw_ref):
        i = pl.program_id(0)

        @pl.when(i == 0)
        def _():
            gw_ref[...] = jnp.zeros_like(gw_ref)

        is_lat = i < n_lat
        dk = jnp.where(is_lat, dkl_ref[...], dkc_ref[...])
        dv = jnp.where(is_lat, dvl_ref[...], dvc_ref[...])
        for g in range(ATT_KV_HEADS):
            cols = slice(g * hd, (g + 1) * hd)
            dx, gw = _norm_rope_bwd(dk[:, cols], x_ref[:, cols], w_ref[...], cos_ref[...], sin_ref[...])
            o_ref[:, cols] = dx.astype(BF)
            gw_ref[...] += gw
        o_ref[:, kvw:] = dv.astype(BF)

    lat = pl.BlockSpec((TM, kvw), lambda i: (jnp.minimum(i, n_lat - 1), 0))
    ctx = pl.BlockSpec((TM, kvw), lambda i: (0, 0))
    kvcol = pl.BlockSpec((TM, 2 * kvw), lambda i: (i, C_AK // (2 * kvw)))
    return pl.pallas_call(
        body, name="att_kv_bwd", grid=(rows // TM,), input_output_aliases={0: 0},
        in_specs=[ANY, lat, ctx, lat, ctx, kvcol,
                  pl.BlockSpec((TM, hd), lambda i: (i, 0)),
                  pl.BlockSpec((TM, hd), lambda i: (i, 0)),
                  pl.BlockSpec((1, hd), lambda i: (0, 0))],
        out_specs=(kvcol, pl.BlockSpec((1, hd), lambda i: (0, 0))),
        out_shape=(SDS(dpx.shape, dpx.dtype), SDS((1, hd), F32)),
        compiler_params=_cp(("arbitrary",)))(dpx, dkl, dkc, dvl, dvc, px, cos_all, sin_all, knw)


def _stack_heads(ref_or_val):
    hd = ATT_HEAD_DIM
    return jnp.concatenate([ref_or_val[:, r * hd:(r + 1) * hd] for r in range(ATT_REP)], axis=0)


def _att_scores(q, kl, kc):
    sl = _dot(q, kl, 1, 1)
    sc = _dot(q, kc, 1, 1)
    m = jnp.maximum(jnp.max(sl, axis=-1, keepdims=True), jnp.max(sc, axis=-1, keepdims=True))
    el = jnp.exp(sl - m)
    ec = jnp.exp(sc - m)
    denom = jnp.sum(el, axis=-1, keepdims=True) + jnp.sum(ec, axis=-1, keepdims=True)
    return el, ec, denom


def _att_fwd(qn, kn, vn, px, n_samp, seq, lc):
    hd = ATT_HEAD_DIM
    tq = 128
    nq = seq // tq
    wblk = ATT_REP * hd
    cb = n_samp * seq // lc
    t_lat = n_samp * seq

    def body(q_ref, kl_ref, kc_ref, vl_ref, vc_ref, g_ref, y_ref, o_ref):
        for r in range(ATT_REP):
            cols = slice(r * hd, (r + 1) * hd)
            el, ec, denom = _att_scores(q_ref[:, cols], kl_ref[...], kc_ref[...])
            o = (_dot(el, vl_ref[...]) + _dot(ec, vc_ref[...])) / denom
            g = g_ref[:, cols]
            o_ref[:, cols] = o
            y_ref[:, cols] = (o * (g * _sigmoid(g))).astype(BF)

    return pl.pallas_call(
        body, name="att_fwd", grid=(n_samp, ATT_KV_HEADS, nq),
        in_specs=[pl.BlockSpec((tq, wblk), lambda b, g, i: (b * nq + i, g)),
                  pl.BlockSpec((seq, hd), lambda b, g, i: (b, g)),
                  pl.BlockSpec((lc, hd), lambda b, g, i: (cb + b, g)),
                  pl.BlockSpec((seq, hd), lambda b, g, i: (b, g)),
                  pl.BlockSpec((lc, hd), lambda b, g, i: (cb + b, g)),
                  pl.BlockSpec((tq, wblk), lambda b, g, i: (b * nq + i, C_AG // wblk + g))],
        out_specs=(pl.BlockSpec((tq, wblk), lambda b, g, i: (b * nq + i, g)),
                   pl.BlockSpec((tq, wblk), lambda b, g, i: (b * nq + i, g))),
        out_shape=(SDS((t_lat, ATT_HEADS * hd), BF), SDS((t_lat, ATT_HEADS * hd), F32)),
        compiler_params=_cp(("parallel", "parallel", "parallel"), 48))(qn, kn, kn, vn, vn, px)


def _att_gate_bwd(dpx, dy_att, o_att, px):
    t_lat = dy_att.shape[0]
    wblk = ATT_REP * ATT_HEAD_DIM

    def body(dpx_hbm, dy_ref, o_ref, g_ref, out_ref):
        g = g_ref[...]
        sg = _sigmoid(g)
        out_ref[...] = (dy_ref[...] * o_ref[...] * (sg * (1.0 + g * (1.0 - sg)))).astype(BF)

    blk = pl.BlockSpec((TM, wblk), lambda i, j: (i, j))
    gcol = pl.BlockSpec((TM, wblk), lambda i, j: (i, C_AG // wblk + j))
    return pl.pallas_call(
        body, name="att_gate_bwd", grid=(t_lat // TM, ATT_KV_HEADS),
        in_specs=[ANY, blk, blk, gcol], out_specs=gcol, out_shape=SDS(dpx.shape, dpx.dtype),
        input_output_aliases={0: 0},
        compiler_params=_cp(("parallel", "parallel")))(dpx, dy_att, o_att, px)


def _att_bwd(dpx, qn, kn, vn, px, o_att, dy_att, cos_all, sin_all, qnw, n_samp, seq, lc):
    hd = ATT_HEAD_DIM
    tq = 128
    nq = seq // tq
    wblk = ATT_REP * hd
    cb = n_samp * seq // lc
    t_lat = n_samp * seq
    kvw = ATT_KV_HEADS * hd
    scale = hd ** -0.5

    def body(dpx_hbm, q_ref, kl_ref, kc_ref, vl_ref, vc_ref, g_ref, o_ref, dy_ref, x_ref, cos_ref, sin_ref, w_ref,
             dq_ref, dkl_ref, dkc_ref, dvl_ref, dvc_ref, gw_ref, akl, akc, avl, avc, aw):
        i = pl.program_id(2)

        @pl.when(i == 0)
        def _():
            akl[...] = jnp.zeros_like(akl)
            akc[...] = jnp.zeros_like(akc)
            avl[...] = jnp.zeros_like(avl)
            avc[...] = jnp.zeros_like(avc)
            aw[...] = jnp.zeros_like(aw)

        dobs, pls, pcs, dsls, dscs = [], [], [], [], []
        for r in range(ATT_REP):
            cols = slice(r * hd, (r + 1) * hd)
            g = g_ref[:, cols]
            sg = _sigmoid(g)
            dy = dy_ref[:, cols]
            do = dy * (g * sg)
            delta = jnp.sum(do * o_ref[:, cols], axis=-1, keepdims=True)
            el, ec, denom = _att_scores(q_ref[:, cols], kl_ref[...], kc_ref[...])
            inv = 1.0 / denom
            p_l = el * inv
            p_c = ec * inv
            dob = do.astype(BF)
            ds_l = (p_l * (_dot(dob, vl_ref[...], 1, 1) - delta)).astype(BF)
            ds_c = (p_c * (_dot(dob, vc_ref[...], 1, 1) - delta)).astype(BF)
            dq = (_dot(ds_l, kl_ref[...]) + _dot(ds_c, kc_ref[...])) * scale
            dx, gw = _norm_rope_bwd(dq, x_ref[:, cols], w_ref[...], cos_ref[...], sin_ref[...])
            dq_ref[:, cols] = dx.astype(BF)
            aw[...] += gw
            dobs.append(dob)
            pls.append(p_l.astype(BF))
            pcs.append(p_c.astype(BF))
            dsls.append(ds_l)
            dscs.append(ds_c)
        do4 = jnp.concatenate(dobs, axis=0)
        q4 = _stack_heads(q_ref)
        avl[...] += _dot(jnp.concatenate(pls, axis=0), do4, 0, 0)
        avc[...] += _dot(jnp.concatenate(pcs, axis=0), do4, 0, 0)
        akl[...] += _dot(jnp.concatenate(dsls, axis=0), q4, 0, 0)
        akc[...] += _dot(jnp.concatenate(dscs, axis=0), q4, 0, 0)

        @pl.when(i == nq - 1)
        def _():
            dkl_ref[...] = akl[...]
            dkc_ref[...] = akc[...]
            dvl_ref[...] = avl[...]
            dvc_ref[...] = avc[...]
            gw_ref[...] = aw[...]

    return pl.pallas_call(
        body, name="att_bwd", grid=(n_samp, ATT_KV_HEADS, nq), input_output_aliases={0: 0},
        in_specs=[ANY,
                  pl.BlockSpec((tq, wblk), lambda b, g, i: (b * nq + i, g)),
                  pl.BlockSpec((seq, hd), lambda b, g, i: (b, g)),
                  pl.BlockSpec((lc, hd), lambda b, g, i: (cb + b, g)),
                  pl.BlockSpec((seq, hd), lambda b, g, i: (b, g)),
                  pl.BlockSpec((lc, hd), lambda b, g, i: (cb + b, g)),
                  pl.BlockSpec((tq, wblk), lambda b, g, i: (b * nq + i, C_AG // wblk + g)),
                  pl.BlockSpec((tq, wblk), lambda b, g, i: (b * nq + i, g)),
                  pl.BlockSpec((tq, wblk), lambda b, g, i: (b * nq + i, g)),
                  pl.BlockSpec((tq, wblk), lambda b, g, i: (b * nq + i, C_AQ // wblk + g)),
                  pl.BlockSpec((tq, hd), lambda b, g, i: (b * nq + i, 0)),
                  pl.BlockSpec((tq, hd), lambda b, g, i: (b * nq + i, 0)),
                  pl.BlockSpec((1, hd), lambda b, g, i: (0, 0))],
        out_specs=(pl.BlockSpec((tq, wblk), lambda b, g, i: (b * nq + i, C_AQ // wblk + g)),
                   pl.BlockSpec((seq, hd), lambda b, g, i: (b, g)),
                   pl.BlockSpec((lc, hd), lambda b, g, i: (b, g)),
                   pl.BlockSpec((seq, hd), lambda b, g, i: (b, g)),
                   pl.BlockSpec((lc, hd), lambda b, g, i: (b, g)),
                   pl.BlockSpec((None, None, 1, hd), lambda b, g, i: (b, g, 0, 0))),
        out_shape=(SDS(dpx.shape, dpx.dtype),
                   SDS((t_lat, kvw), F32), SDS((n_samp * lc, kvw), F32),
                   SDS((t_lat, kvw), F32), SDS((n_samp * lc, kvw), F32),
                   SDS((n_samp, ATT_KV_HEADS, 1, hd), F32)),
        scratch_shapes=[pltpu.VMEM((seq, hd), F32), pltpu.VMEM((lc, hd), F32),
                        pltpu.VMEM((seq, hd), F32), pltpu.VMEM((lc, hd), F32), pltpu.VMEM((1, hd), F32)],
        compiler_params=_cp(("parallel", "parallel", "arbitrary"), 56))(
            dpx, qn, kn, kn, vn, vn, px, o_att, dy_att, px, cos_all, sin_all, qnw)


def _merge(x_lat, target, y_ret, y_att, px, gate3, w_o_ret, w_o_att, w_out, tiles_per_sample):
    t_lat = x_lat.shape[0]
    tm = 256
    n_t = t_lat // tm
    per = tiles_per_sample * (TM // tm)
    d = D_MODEL
    rv = RET_HEADS * RET_DV
    n_samp = gate3.shape[0] - 1

    def body(x_ref, t_ref, yr_ref, ya_ref, mr0, mr1, ma0, ma1, gt_ref, wor_ref, woa_ref, wout_ref,
             gx_ref, dyr_ref, dya_ref, dpx_hbm, loss_ref, dgt_ref, gwor_hbm, gwoa_hbm, gwout_hbm,
             aor, aoa, aout, dmg_ref, dmg_sem):
        i = pl.program_id(0)

        def dmg_copy(step):
            rows = pl.ds(pl.multiple_of(step * tm, tm), tm)
            return pltpu.make_async_copy(dmg_ref, dpx_hbm.at[rows, pl.ds(C_MR, 2 * d)], dmg_sem)

        @pl.when(i == 0)
        def _():
            aor[...] = jnp.zeros_like(aor)
            aoa[...] = jnp.zeros_like(aoa)
            aout[...] = jnp.zeros_like(aout)
            loss_ref[...] = jnp.zeros_like(loss_ref)

        @pl.when(i % per == 0)
        def _():
            dgt_ref[...] = jnp.zeros_like(dgt_ref)

        yr = yr_ref[...]
        ya = ya_ref[...]
        a = jnp.dot(yr, wor_ref[...], preferred_element_type=F32)
        b = jnp.dot(ya, woa_ref[...], preferred_element_type=F32)
        sr = _sigmoid(jnp.concatenate([mr0[...], mr1[...]], axis=1))
        sa = _sigmoid(jnp.concatenate([ma0[...], ma1[...]], axis=1))
        yb = (sr * a + sa * b).astype(BF)
        out = jnp.dot(yb, wout_ref[...], preferred_element_type=F32)
        gate = gt_ref[...]
        err = x_ref[...] + gate * out - t_ref[...]
        loss_ref[...] += 0.5 * _sum_all(err * err) * (1.0 / d)
        dy_tok = err * (1.0 / d)
        gx_ref[...] = dy_tok
        dgt_ref[...] += jnp.sum(dy_tok * out, axis=0, keepdims=True)
        dout = (dy_tok * gate).astype(BF)
        aout[...] += _dot(yb, dout, 0, 0)
        dyy = _dot(dout, wout_ref[...], 1, 1)
        da = (dyy * sr).astype(BF)
        db = (dyy * sa).astype(BF)
        @pl.when(i > 0)
        def _():
            dmg_copy(i - 1).wait()

        dmg_ref[:, :d] = (dyy * a * (sr * (1.0 - sr))).astype(BF)
        dmg_ref[:, d:] = (dyy * b * (sa * (1.0 - sa))).astype(BF)
        dmg_copy(i).start()
        aor[...] += _dot(yr, da, 0, 0)
        aoa[...] += _dot(ya, db, 0, 0)
        dyr_ref[...] = _dot(da, wor_ref[...], 1, 1)
        dya_ref[...] = _dot(db, woa_ref[...], 1, 1)

        @pl.when(i == n_t - 1)
        def _():
            dmg_copy(i).wait()
            pltpu.sync_copy(aor, gwor_hbm)
            pltpu.sync_copy(aoa, gwoa_hbm)
            pltpu.sync_copy(aout, gwout_hbm)

    half = d // 2
    return pl.pallas_call(
        body, name="merge", grid=(n_t,),
        in_specs=[pl.BlockSpec((tm, d), lambda i: (i, 0)),
                  pl.BlockSpec((tm, d), lambda i: (i, 0)),
                  pl.BlockSpec((tm, rv), lambda i: (i, 0)),
                  pl.BlockSpec((tm, d), lambda i: (i, 0)),
                  pl.BlockSpec((tm, half), lambda i: (i, C_MR // half)),
                  pl.BlockSpec((tm, half), lambda i: (i, C_MR // half + 1)),
                  pl.BlockSpec((tm, half), lambda i: (i, C_MA // half)),
                  pl.BlockSpec((tm, half), lambda i: (i, C_MA // half + 1)),
                  pl.BlockSpec((None, 1, d), lambda i: (i // per, 0, 0)),
                  pl.BlockSpec((rv, d), lambda i: (0, 0)),
                  pl.BlockSpec((d, d), lambda i: (0, 0)),
                  pl.BlockSpec((d, d), lambda i: (0, 0))],
        out_specs=(pl.BlockSpec((tm, d), lambda i: (i, 0)),
                   pl.BlockSpec((tm, rv), lambda i: (i, 0)),
                   pl.BlockSpec((tm, d), lambda i: (i, 0)),
                   ANY,
                   pl.BlockSpec((8, 128), lambda i: (0, 0)),
                   pl.BlockSpec((None, 1, d), lambda i: (i // per, 0, 0)),
                   ANY, ANY, ANY),
        out_shape=(SDS((t_lat, d), F32), SDS((t_lat, rv), F32), SDS((t_lat, d), F32),
                   SDS((px.shape[0], IN_COLS), BF),
                   SDS((8, 128), F32), SDS((n_samp, 1, d), F32),
                   SDS((rv, d), F32), SDS((d, d), F32), SDS((d, d), F32)),
        scratch_shapes=[pltpu.VMEM((rv, d), F32), pltpu.VMEM((d, d), F32), pltpu.VMEM((d, d), F32),
                        pltpu.VMEM((tm, 2 * d), BF), pltpu.SemaphoreType.DMA],
        compiler_params=_cp(("arbitrary",), 56))(
            x_lat, target, y_ret, y_att, px, px, px, px, gate3, w_o_ret, w_o_att, w_out)


def _place():
    x, y, c = lax.axis_index("x"), lax.axis_index("y"), lax.axis_index("c")
    chips = [(1 - x, y), (x, 1 - y), (1 - x, 1 - y)]
    return x, y, c, chips


def _remote(src, dst, send_sem, recv_sem, to):
    return pltpu.make_async_remote_copy(src_ref=src, dst_ref=dst, send_sem=send_sem, recv_sem=recv_sem,
                                        device_id=to, device_id_type=MESH)


def _place_ids():
    x, y, c = lax.axis_index("x"), lax.axis_index("y"), lax.axis_index("c")
    return jnp.stack([x, y, c, 2 * x + y]).astype(jnp.int32)


def _cast_place(w, ids):
    rows, cols = w.shape
    tr = min(rows, 256)

    def body(ids_ref, w_ref, o_ref):
        o_ref[...] = w_ref[...].astype(BF)

    return pl.pallas_call(
        body, name="cast_place",
        grid_spec=pltpu.PrefetchScalarGridSpec(
            num_scalar_prefetch=1, grid=(rows // tr,),
            in_specs=[pl.BlockSpec((tr, cols), lambda i, ids_ref: (i, 0))],
            out_specs=pl.BlockSpec((None, tr, cols), lambda i, ids_ref: (ids_ref[3], i, 0))),
        out_shape=SDS((N_SHARD, rows, cols), BF),
        compiler_params=_cp(("parallel",), 40))(ids, w)


def _all_gather_weights(bufs):
    n = len(bufs)

    def body(*refs):
        outs = refs[n:2 * n]
        send_sems, recv_sems = refs[2 * n:]
        x, y, c, chips = _place()
        sibling = (x, y, 1 - c)
        me = 2 * x + y

        def half(ref, s, which):
            h = ref.shape[1] // 2
            return ref.at[s, pl.ds(which * h, h), :]

        first = []
        for a in range(n):
            for j, chip in enumerate(chips):
                k = a * 3 + j
                win = half(outs[a], me, c)
                first.append(_remote(win, win, send_sems.at[k], recv_sems.at[k], (*chip, c)))
        for cp in first:
            cp.start()
        passed = []
        for a in range(n):
            for j, chip in enumerate(chips):
                k = a * 3 + j
                win = half(outs[a], 2 * chip[0] + chip[1], c)
                _remote(win, win, send_sems.at[k], recv_sems.at[k], (*chip, c)).wait_recv()
                fw = _remote(win, win, send_sems.at[3 * n + k], recv_sems.at[3 * n + k], sibling)
                fw.start()
                passed.append(fw)
        for a in range(n):
            for j, chip in enumerate(chips):
                k = a * 3 + j
                win = half(outs[a], 2 * chip[0] + chip[1], 1 - c)
                _remote(win, win, send_sems.at[3 * n + k], recv_sems.at[3 * n + k], sibling).wait_recv()
        for cp in first + passed:
            cp.wait_send()

    return pl.pallas_call(
        body, name="all_gather_weights",
        in_specs=[ANY] * n, out_specs=tuple([ANY] * n),
        out_shape=tuple(SDS(b.shape, b.dtype) for b in bufs),
        input_output_aliases={a: a for a in range(n)},
        scratch_shapes=[pltpu.SemaphoreType.DMA((6 * n,)), pltpu.SemaphoreType.DMA((6 * n,))],
        compiler_params=_cp(has_side_effects=True))(*bufs)


def _swap_halves(grads):
    n = len(grads)

    def body(*refs):
        ins, outs = refs[:n], refs[n:2 * n]
        send_sems, recv_sems = refs[2 * n:]
        x, y, c, _ = _place()
        sibling = (x, y, 1 - c)

        def half(ref, which):
            h = ref.shape[1] // 2
            return ref.at[:, pl.ds(which * h, h), :]

        sends = [_remote(half(ins[a], 1 - c), outs[a], send_sems.at[a], recv_sems.at[a], sibling)
                 for a in range(n)]
        for cp in sends:
            cp.start()
        for cp in sends:
            cp.wait_recv()
        for cp in sends:
            cp.wait_send()

    return pl.pallas_call(
        body, name="swap_halves",
        in_specs=[ANY] * n, out_specs=tuple([ANY] * n),
        out_shape=tuple(SDS((g.shape[0], g.shape[1] // 2, g.shape[2]), g.dtype) for g in grads),
        scratch_shapes=[pltpu.SemaphoreType.DMA((n,)), pltpu.SemaphoreType.DMA((n,))],
        compiler_params=_cp(has_side_effects=True))(*grads)


def _chip_sum(g, p, ids):
    n_s, rows, cols = g.shape
    h = rows // 2
    tr = min(h, 256)
    nb = h // tr

    def body(ids_ref, g_ref, p_ref, o_ref, o16_ref):
        t = g_ref[...] + p_ref[...]
        o_ref[...] = t
        o16_ref[...] = t.astype(BF)

    out_spec = pl.BlockSpec((None, tr, cols), lambda s, i, ids_ref: (s, i, 0))
    return pl.pallas_call(
        body, name="chip_sum",
        grid_spec=pltpu.PrefetchScalarGridSpec(
            num_scalar_prefetch=1, grid=(n_s, nb),
            in_specs=[pl.BlockSpec((None, tr, cols), lambda s, i, ids_ref: (s, ids_ref[2] * nb + i, 0)),
                      pl.BlockSpec((None, tr, cols), lambda s, i, ids_ref: (s, i, 0))],
            out_specs=(out_spec, out_spec)),
        out_shape=(SDS((n_s, h, cols), g.dtype), SDS((n_s, h, cols), BF)),
        compiler_params=_cp(("parallel", "parallel"), 40))(ids, g, p)


def _exchange_shards(parts):
    n = len(parts)

    def body(*refs):
        ins, outs = refs[:n], refs[n:2 * n]
        send_sems, recv_sems = refs[2 * n:]
        x, y, c, chips = _place()
        sends = []
        for a in range(n):
            for j, chip in enumerate(chips):
                k = a * 3 + j
                sends.append(_remote(ins[a].at[2 * chip[0] + chip[1]], outs[a].at[j],
                                     send_sems.at[k], recv_sems.at[k], (*chip, c)))
        for cp in sends:
            cp.start()
        for cp in sends:
            cp.wait_recv()
        for cp in sends:
            cp.wait_send()

    return pl.pallas_call(
        body, name="exchange_shards",
        in_specs=[ANY] * n, out_specs=tuple([ANY] * n),
        out_shape=tuple(SDS((3,) + p.shape[1:], p.dtype) for p in parts),
        scratch_shapes=[pltpu.SemaphoreType.DMA((3 * n,)), pltpu.SemaphoreType.DMA((3 * n,))],
        compiler_params=_cp(has_side_effects=True))(*parts)


def _shard_sum(t, q, ids):
    _, h, cols = t.shape
    tr = min(h, 256)
    nb = h // tr

    def body(ids_ref, t_ref, q_ref, o_ref):
        o_ref[...] = ((t_ref[...] + q_ref[0].astype(F32)) + q_ref[1].astype(F32)) + q_ref[2].astype(F32)

    return pl.pallas_call(
        body, name="shard_sum",
        grid_spec=pltpu.PrefetchScalarGridSpec(
            num_scalar_prefetch=1, grid=(nb,),
            in_specs=[pl.BlockSpec((None, tr, cols), lambda i, ids_ref: (ids_ref[3], i, 0)),
                      pl.BlockSpec((3, tr, cols), lambda i, ids_ref: (0, i, 0))],
            out_specs=pl.BlockSpec((tr, cols), lambda i, ids_ref: (ids_ref[2] * nb + i, 0))),
        out_shape=SDS((2 * h, cols), t.dtype),
        compiler_params=_cp(("parallel",), 40))(ids, t, q)


def _join_halves(bufs):
    n = len(bufs)

    def body(*refs):
        outs = refs[n:2 * n]
        send_sems, recv_sems = refs[2 * n:]
        x, y, c, _ = _place()
        sibling = (x, y, 1 - c)

        def win(ref, which):
            h = ref.shape[0] // 2
            return ref.at[pl.ds(which * h, h), :]

        sends = [_remote(win(outs[a], c), win(outs[a], c), send_sems.at[a], recv_sems.at[a], sibling)
                 for a in range(n)]
        for cp in sends:
            cp.start()
        for a in range(n):
            other = win(outs[a], 1 - c)
            _remote(other, other, send_sems.at[a], recv_sems.at[a], sibling).wait_recv()
        for cp in sends:
            cp.wait_send()

    return pl.pallas_call(
        body, name="join_halves",
        in_specs=[ANY] * n, out_specs=tuple([ANY] * n),
        out_shape=tuple(SDS(b.shape, b.dtype) for b in bufs),
        input_output_aliases={a: a for a in range(n)},
        scratch_shapes=[pltpu.SemaphoreType.DMA((n,)), pltpu.SemaphoreType.DMA((n,))],
        compiler_params=_cp(has_side_effects=True))(*bufs)


def _all_reduce_small(block):
    rows, cols = block.shape
    n_dev = 8

    def body(x_ref, o_ref, buf, send_sems, recv_sems, local_sem):
        x, y, c, chips = _place()
        me, sibling = (x, y, c), (x, y, 1 - c)

        def slot(px_, py_, pc_):
            return buf.at[4 * px_ + 2 * py_ + pc_]

        def copy(k, who, to, src=None):
            return _remote(slot(*who) if src is None else src, slot(*who), send_sems.at[k], recv_sems.at[k], to)

        mine = pltpu.make_async_copy(x_ref, slot(*me), local_sem)
        mine.start()
        first = [copy(0, me, sibling, src=x_ref)]
        first += [copy(1 + j, me, (*chip, c), src=x_ref) for j, chip in enumerate(chips)]
        for cp in first:
            cp.start()
        passed = [copy(4 + j, (*chip, c), sibling) for j, chip in enumerate(chips)]
        for j, chip in enumerate(chips):
            copy(1 + j, (*chip, c), me).wait_recv()
            passed[j].start()
        copy(0, sibling, me).wait_recv()
        for j, chip in enumerate(chips):
            copy(4 + j, (*chip, 1 - c), me).wait_recv()
        for cp in first + passed:
            cp.wait_send()
        mine.wait()
        acc = buf[0]
        for s in range(1, n_dev):
            acc = acc + buf[s]
        o_ref[...] = acc

    return pl.pallas_call(
        body, name="all_reduce_small",
        in_specs=[pl.BlockSpec(memory_space=pltpu.VMEM)],
        out_specs=pl.BlockSpec(memory_space=pltpu.VMEM),
        out_shape=SDS((rows, cols), F32),
        scratch_shapes=[pltpu.VMEM((n_dev, rows, cols), F32), pltpu.SemaphoreType.DMA((7,)),
                        pltpu.SemaphoreType.DMA((7,)), pltpu.SemaphoreType.DMA],
        compiler_params=_cp(has_side_effects=True))(block)


def _adam_math(w, g, m, v):
    m = ADAM_B1 * m + (1.0 - ADAM_B1) * g
    v = ADAM_B2 * v + (1.0 - ADAM_B2) * (g * g)
    m_hat = m / (1.0 - ADAM_B1 ** ADAM_STEP)
    v_hat = v / (1.0 - ADAM_B2 ** ADAM_STEP)
    delta = -ADAM_LR * (m_hat / (jnp.sqrt(v_hat) + ADAM_EPS) + ADAM_WD * w)
    return delta, m, v


def _adamw(w, g, m, v):
    rows, cols = w.shape
    tr = min(rows, 256)

    def body(w_ref, g_ref, m_ref, v_ref, d_ref, nm_ref, nv_ref):
        d_ref[...], nm_ref[...], nv_ref[...] = _adam_math(w_ref[...], g_ref[...], m_ref[...], v_ref[...])

    spec = pl.BlockSpec((tr, cols), lambda i: (i, 0))
    return pl.pallas_call(
        body, name="adamw", grid=(rows // tr,), in_specs=[spec] * 4, out_specs=(spec,) * 3,
        out_shape=(SDS(w.shape, F32),) * 3, compiler_params=_cp(("parallel",), 40))(w, g, m, v)


def _adamw_small(w, g, m, v):
    def body(w_ref, g_ref, m_ref, v_ref, go_ref, d_ref, nm_ref, nv_ref):
        w = w_ref[...]
        g = g_ref[...]
        sub = lax.broadcasted_iota(jnp.int32, w.shape, 0)
        lane = lax.broadcasted_iota(jnp.int32, w.shape, 1)
        is_ret = jnp.logical_and(sub == 5, lane < 2 * RET_HEADS)
        u = jnp.exp(jnp.where(is_ret, w, -1.0) * jnp.log(2.0))
        g = jnp.where(is_ret, g * (-u * jnp.log(2.0) / (1.0 - u)), g)
        go_ref[...] = g
        d_ref[...], nm_ref[...], nv_ref[...] = _adam_math(w, g, m_ref[...], v_ref[...])

    return pl.pallas_call(body, name="adamw_small", out_shape=(SDS(w.shape, F32),) * 4)(w, g, m, v)


def _rope_tables(seq, n_samp, n_ctx_rows):
    rows = seq // GRID_W
    row = jnp.repeat(jnp.arange(rows, dtype=F32), GRID_W)
    col = jnp.tile(jnp.arange(GRID_W, dtype=F32), rows)
    half = ATT_HEAD_DIM // 2
    freqs = ROPE_THETA ** (-jnp.arange(0, half, 2, dtype=F32) / half)
    ang = jnp.concatenate([row[:, None] * freqs, col[:, None] * freqs], axis=-1)
    cos, sin = jnp.cos(ang), jnp.sin(ang)
    cos_f = jnp.repeat(cos, 2, axis=1)
    sin_s = jnp.stack([-sin, sin], axis=-1).reshape(seq, ATT_HEAD_DIM)
    cos_all = jnp.concatenate([jnp.tile(cos_f, (n_samp, 1)), jnp.ones((n_ctx_rows, ATT_HEAD_DIM), F32)], axis=0)
    sin_all = jnp.concatenate([jnp.tile(sin_s, (n_samp, 1)), jnp.zeros((n_ctx_rows, ATT_HEAD_DIM), F32)], axis=0)
    return cos_all, sin_all


def _pack_small(c_ctx, norm_w, b_ada, ret, qn, kn):
    d = D_MODEL
    row5 = jnp.concatenate([ret.reshape(-1), jnp.zeros((128 - 2 * RET_HEADS,), F32), qn.reshape(-1), kn.reshape(-1),
                            jnp.zeros((d - 384,), F32)])
    return jnp.concatenate([c_ctx.reshape(1, d), norm_w.reshape(1, d), b_ada.reshape(3, d), row5.reshape(1, d),
                            jnp.zeros((2, d), F32)], axis=0)


def _unpack_small(p):
    d = D_MODEL
    return (p[0], p[1:2], p[2:5].reshape(1, 3 * d), p[5, :2 * RET_HEADS].reshape(1, 2, RET_HEADS),
            p[5:6, 128:256], p[5:6, 256:384])


def _local_step(x, c, ctx, c_ctx, norm_w, b_ada, ret_log2_decay, q_norm_w, k_norm_w, loss_target,
                w_ada_g, w_in_g, w_o_ret, w_o_att, w_out):
    n_samp, seq, d = x.shape
    lc = ctx.shape[1]
    t_lat, t_ctx = n_samp * seq, n_samp * lc
    assert seq % TM == 0 and t_ctx == TM and t_lat % lc == 0 and seq % GRID_W == 0
    tps = seq // TM

    x_lat = x.reshape(t_lat, d)
    x_ctx = ctx.reshape(t_ctx, d)
    cvec8 = jnp.concatenate([c, c_ctx.reshape(1, d), jnp.zeros((8 - n_samp - 1, d), F32)], axis=0)
    lg = jnp.log1p(-jnp.exp2(ret_log2_decay.reshape(2, RET_HEADS)))
    cos_all, sin_all = _rope_tables(seq, n_samp, t_ctx)

    mod8 = _adaln_fwd(cvec8, w_ada_g, b_ada)
    mod3 = mod8[:n_samp + 1]
    shift3 = mod3[:, None, 0:d]
    scale3 = mod3[:, None, d:2 * d]
    gate3 = mod3[:, None, 2 * d:3 * d]

    px, hxt = _normproj(x_lat, x_ctx, norm_w, scale3, shift3, w_in_g, tps, n_samp)

    states0 = _ctx_state_fwd(px, lg, n_samp, t_lat, lc)
    o_f, o_b, saved = _ret_fwd(px, states0, lg, n_samp, seq)
    y_ret = _retnorm_fwd(o_f, o_b, px)

    qn = _att_prep_q(px, cos_all, sin_all, q_norm_w, t_lat)
    kn, vn = _att_prep_kv(px, cos_all, sin_all, k_norm_w)
    y_att, o_att = _att_fwd(qn, kn, vn, px, n_samp, seq, lc)

    (gx_res, dy_ret, dy_att, dpx, loss8, dgate, g_w_o_ret, g_w_o_att, g_w_out) = _merge(
        x_lat, loss_target.reshape(t_lat, d), y_ret, y_att, px, gate3, w_o_ret, w_o_att, w_out, tps)

    dpx = _att_gate_bwd(dpx, dy_att, o_att, px)
    dpx, dkl, dkc, dvl, dvc, gqw = _att_bwd(
        dpx, qn, kn, vn, px, o_att, dy_att, cos_all, sin_all, q_norm_w, n_samp, seq, lc)
    dpx, gkw = _att_kv_bwd(dpx, dkl, dkc, dvl, dvc, px, cos_all, sin_all, k_norm_w)

    do, dpx = _retnorm_bwd(dpx, dy_ret, o_f, o_b, px)
    dqf, dkf, dvf, dqb, dkb, dvb, dstates, dlg_lat = _ret_bwd(px, do, saved, lg, n_samp, seq)
    dpx = _combine_into(dpx, dqf, dqb, C_RQ, 1.0)
    dpx = _combine_into(dpx, dkf, dkb, C_RK, RET_DK ** -0.5)
    dpx = _combine_into(dpx, dvf, dvb, C_RV, 1.0)
    dpx, dlg_ctx = _ctx_state_bwd(dpx, px, dstates, lg, n_samp, t_lat, lc)
    dpx = _zero_ctx_tail(dpx, t_lat)

    g_w_in = _gw_in(hxt, dpx)
    dhx = _dhx(dpx, w_in_g)
    grad_x, dshift, dscale, g_norm_w = _norm_bwd(x_lat, x_ctx, dhx, gx_res, norm_w, scale3, tps, n_samp)

    dgate_all = jnp.concatenate([dgate, jnp.zeros((1, 1, d), F32)], axis=0)
    dmod3 = jnp.concatenate([dshift, dscale, dgate_all], axis=2).reshape(n_samp + 1, 3 * d)
    dmod8 = jnp.concatenate([dmod3, jnp.zeros((8 - n_samp - 1, 3 * d), F32)], axis=0)
    g_w_ada, g_b_ada, dc8 = _adaln_bwd(cvec8, dmod8, w_ada_g)

    g_lg = (jnp.sum(dlg_lat[:, :, 0], axis=0).reshape(2, RET_HEADS)
            + jnp.stack([jnp.sum(dlg_ctx[:, :, 0, 0], axis=0), jnp.sum(dlg_ctx[:, :, 1, 0], axis=0)], axis=0))
    small = _pack_small(dc8[n_samp], g_norm_w, g_b_ada, g_lg, jnp.sum(gqw, axis=(0, 1, 2)), gkw)
    return (loss8[0, 0], grad_x.reshape(n_samp, seq, d),
            (g_w_ada, g_w_in, g_w_o_ret, g_w_o_att, g_w_out), small)


def kernel(x, c, ctx, c_ctx, norm_w, w_ada, b_ada, w_in, ret_log2_decay, q_norm_w, k_norm_w, w_o_ret, w_o_att, w_out, loss_target, m_c_ctx, m_norm_w, m_w_ada, m_b_ada, m_w_in, m_ret_log2_decay, m_q_norm_w, m_k_norm_w, m_w_o_ret, m_w_o_att, m_w_out, v_c_ctx, v_norm_w, v_w_ada, v_b_ada, v_w_in, v_ret_log2_decay, v_q_norm_w, v_k_norm_w, v_w_o_ret, v_w_o_att, v_w_out):
    big_w = (w_ada[0], w_in[0], w_o_ret[0], w_o_att[0], w_out[0])
    big_m = (m_w_ada[0], m_w_in[0], m_w_o_ret[0], m_w_o_att[0], m_w_out[0])
    big_v = (v_w_ada[0], v_w_in[0], v_w_o_ret[0], v_w_o_att[0], v_w_out[0])

    ids = _place_ids()
    gathered = _all_gather_weights(tuple(_cast_place(w, ids) for w in big_w))
    w_ada_g, w_in_g = gathered[0], gathered[1]
    w_o_ret_f = gathered[2].reshape(-1, D_MODEL)
    w_o_att_f = gathered[3].reshape(-1, D_MODEL)
    w_out_f = gathered[4].reshape(-1, D_MODEL)

    loss_local, grad_x, big_g, small_g = _local_step(
        x, c, ctx, c_ctx, norm_w[0:1], b_ada[0:1], ret_log2_decay[0], q_norm_w[0:1], k_norm_w[0:1], loss_target,
        w_ada_g, w_in_g, w_o_ret_f, w_o_att_f, w_out_f)
    loss = lax.psum(loss_local, ("x", "y", "c"))

    g_sm = (big_g[0], big_g[1], big_g[2].reshape(N_SHARD, -1, D_MODEL), big_g[3].reshape(N_SHARD, -1, D_MODEL),
            big_g[4].reshape(N_SHARD, -1, D_MODEL))
    from_sibling = _swap_halves(g_sm)
    chip_sums = tuple(_chip_sum(g, p, ids) for g, p in zip(g_sm, from_sibling))
    from_chips = _exchange_shards(tuple(t16 for _, t16 in chip_sums))
    big_grad = _join_halves(tuple(_shard_sum(t, q, ids) for (t, _), q in zip(chip_sums, from_chips)))

    small_grad_in = _all_reduce_small(small_g)
    small_w = _pack_small(c_ctx, norm_w, b_ada, ret_log2_decay, q_norm_w, k_norm_w)
    small_m = _pack_small(m_c_ctx, m_norm_w, m_b_ada, m_ret_log2_decay, m_q_norm_w, m_k_norm_w)
    small_v = _pack_small(v_c_ctx, v_norm_w, v_b_ada, v_ret_log2_decay, v_q_norm_w, v_k_norm_w)
    small_grad, small_delta, small_nm, small_nv = _adamw_small(small_w, small_grad_in, small_m, small_v)

    big_delta, big_nm, big_nv = [], [], []
    for w, g, m, v in zip(big_w, big_grad, big_m, big_v):
        dlt, nm, nv = _adamw(w, g, m, v)
        big_delta.append(dlt[None])
        big_nm.append(nm[None])
        big_nv.append(nv[None])
    big_grad = [g[None] for g in big_grad]

    def order(small_packed, big):
        s = _unpack_small(small_packed)
        return (s[0], s[1], big[0], s[2], big[1], s[3], s[4], s[5], big[2], big[3], big[4])

    return (loss, grad_x, *order(small_grad, big_grad), *order(small_delta, big_delta),
            *order(small_nm, big_nm), *order(small_nv, big_nv))
```

```python
import functools

import jax
import jax.numpy as jnp
from jax import lax
from jax.experimental import pallas as pl
from jax.experimental.pallas import tpu as pltpu

F32 = jnp.float32
BF = jnp.bfloat16
SDS = jax.ShapeDtypeStruct
MESH = pl.DeviceIdType.MESH
ANY = pl.BlockSpec(memory_space=pl.ANY)
SMEM = pl.BlockSpec(memory_space=pltpu.SMEM)

D_MODEL = 1024
GRID_W = 64
RET_HEADS = 4
RET_DK = 256
RET_DV = 512
RET_CHUNK = 128
ATT_HEADS = 8
ATT_KV_HEADS = 2
ATT_REP = ATT_HEADS // ATT_KV_HEADS
ATT_HEAD_DIM = 128
ROPE_THETA = 10000.0
NORM_EPS = 1e-6
IN_COLS = 10752
KV_COLS = 3584
C_RK, C_RV, C_AK, C_AV, C_RQ, C_RG, C_AQ, C_AG, C_MR, C_MA = 0, 1024, 3072, 3328, 3584, 4608, 6656, 7680, 8704, 9728
N_SHARD = 4
ADA_W = 3 * D_MODEL // N_SHARD
IN_W = IN_COLS // N_SHARD
IN_BLK = IN_W // 3
N_IN_BLK = IN_COLS // IN_BLK
TM = 512
ADAM_LR, ADAM_B1, ADAM_B2, ADAM_EPS, ADAM_WD, ADAM_STEP = 0.001, 0.9, 0.999, 1e-08, 0.01, 10
MIB = 1024 * 1024


def _cp(sem=None, vmem_mb=None, **kw):
    if sem is not None:
        kw["dimension_semantics"] = sem
    if vmem_mb is not None:
        kw["vmem_limit_bytes"] = vmem_mb * MIB
    return pltpu.CompilerParams(**kw)


def _dot(a, b, ca=1, cb=0):
    return lax.dot_general(a.astype(BF), b.astype(BF), (((ca,), (cb,)), ((), ())), preferred_element_type=F32)


def _sigmoid(x):
    return 1.0 / (1.0 + jnp.exp(-x))


def _sum_all(x):
    return jnp.sum(jnp.sum(x, axis=1, keepdims=True), axis=0, keepdims=True)


def _swap_pairs(x):
    ax = x.ndim - 1
    lane = lax.broadcasted_iota(jnp.int32, x.shape, ax)
    nxt = pltpu.roll(x, x.shape[ax] - 1, ax)
    prv = pltpu.roll(x, 1, ax)
    return jnp.where(lane % 2 == 0, nxt, prv)


def _rms(x):
    return lax.rsqrt(jnp.mean(x * x, axis=-1, keepdims=True) + NORM_EPS)


def _rms_bwd(dxh, xh, r):
    return r * (dxh - xh * jnp.mean(dxh * xh, axis=-1, keepdims=True))


def _adaln_fwd(cvec8, w_ada_g, b_ada):
    def body(c_ref, w_ref, b_ref, o_ref):
        cv = c_ref[...]
        sc = (cv * _sigmoid(cv)).astype(BF)
        for s in range(N_SHARD):
            cols = slice(s * ADA_W, (s + 1) * ADA_W)
            o_ref[:, cols] = jnp.dot(sc, w_ref[s], preferred_element_type=F32) + b_ref[:, cols]

    return pl.pallas_call(body, out_shape=SDS((8, 3 * D_MODEL), F32), name="adaln_fwd",
                          compiler_params=_cp(vmem_mb=32))(cvec8, w_ada_g, b_ada)


def _adaln_bwd(cvec8, dmod8, w_ada_g):
    def body(c_ref, d_ref, w_ref, gw_ref, gb_ref, dc_ref):
        cv = c_ref[...]
        sg = _sigmoid(cv)
        sc = cv * sg
        dm = d_ref[...]
        gb_ref[...] = jnp.sum(dm, axis=0, keepdims=True)
        dsc = jnp.zeros((8, D_MODEL), F32)
        for s in range(N_SHARD):
            cols = slice(s * ADA_W, (s + 1) * ADA_W)
            gw_ref[s] = _dot(sc, dm[:, cols], 0, 0)
            dsc = dsc + _dot(dm[:, cols], w_ref[s], 1, 1)
        dc_ref[...] = dsc * (sg * (1.0 + cv * (1.0 - sg)))

    return pl.pallas_call(
        body, name="adaln_bwd",
        out_shape=(SDS((N_SHARD, D_MODEL, ADA_W), F32), SDS((1, 3 * D_MODEL), F32), SDS((8, D_MODEL), F32)),
        compiler_params=_cp(vmem_mb=48))(cvec8, dmod8, w_ada_g)


def _big_rows(rows):
    return 1536 if rows % 1536 == 0 else TM


def _norm_fwd(x_lat, x_ctx, norm_w, scale3, shift3, tiles_per_sample, n_samp):
    n_lat = x_lat.shape[0] // TM
    rows = x_lat.shape[0] + x_ctx.shape[0]

    def samp(i):
        return jnp.minimum(i // tiles_per_sample, n_samp)

    def body(x_ref, c_ref, nw_ref, sc_ref, sh_ref, hx_ref, hxt_ref):
        x = jnp.where(pl.program_id(0) < n_lat, x_ref[...], c_ref[...])
        h = x * _rms(x) * nw_ref[...] * (1.0 + sc_ref[...]) + sh_ref[...]
        hx_ref[...] = h.astype(BF)
        hxt_ref[...] = h.T.astype(BF)

    return pl.pallas_call(
        body, name="norm_fwd", grid=(rows // TM,),
        in_specs=[pl.BlockSpec((TM, D_MODEL), lambda i: (jnp.minimum(i, n_lat - 1), 0)),
                  pl.BlockSpec((TM, D_MODEL), lambda i: (jnp.maximum(i - n_lat, 0), 0)),
                  pl.BlockSpec((1, D_MODEL), lambda i: (0, 0)),
                  pl.BlockSpec((None, 1, D_MODEL), lambda i: (samp(i), 0, 0)),
                  pl.BlockSpec((None, 1, D_MODEL), lambda i: (samp(i), 0, 0))],
        out_specs=(pl.BlockSpec((TM, D_MODEL), lambda i: (i, 0)),
                   pl.BlockSpec((D_MODEL, TM), lambda i: (0, i))),
        out_shape=(SDS((rows, D_MODEL), BF), SDS((D_MODEL, rows), BF)),
        compiler_params=_cp(("parallel",), 40))(x_lat, x_ctx, norm_w, scale3, shift3)


def _in_proj(hx, w_in_g):
    rows = hx.shape[0]
    tb = _big_rows(rows)

    def body(h_ref, w_ref, px_ref):
        px_ref[...] = jnp.dot(h_ref[...], w_ref[...], preferred_element_type=F32).astype(BF)

    return pl.pallas_call(
        body, name="in_proj", grid=(N_IN_BLK, rows // tb),
        in_specs=[pl.BlockSpec((tb, D_MODEL), lambda j, i: (i, 0)),
                  pl.BlockSpec((None, D_MODEL, IN_BLK), lambda j, i: (j // 3, 0, j % 3))],
        out_specs=pl.BlockSpec((tb, IN_BLK), lambda j, i: (i, j)),
        out_shape=SDS((rows, IN_COLS), BF),
        compiler_params=_cp(("parallel", "parallel"), 40))(hx, w_in_g)


def _norm_bwd(x_lat, x_ctx, dhx, gx_res, norm_w, scale3, tiles_per_sample, n_samp):
    rows = x_lat.shape[0] + x_ctx.shape[0]
    n_lat = tiles_per_sample * n_samp

    def samp(i):
        return jnp.minimum(i // tiles_per_sample, n_samp)

    def lat(i):
        return jnp.minimum(i, n_lat - 1)

    def body(x_ref, c_ref, dh_ref, gr_ref, nw_ref, sc_ref, gx_ref, dsh_ref, dsc_ref, dnw_ref):
        i = pl.program_id(0)
        x = jnp.where(i < n_lat, x_ref[...], c_ref[...])
        r = _rms(x)
        xh = x * r
        nw = nw_ref[...]
        dh = dh_ref[...]
        first = jnp.logical_or(i % tiles_per_sample == 0, i >= n_lat)

        @pl.when(first)
        def _():
            dsh_ref[...] = jnp.zeros_like(dsh_ref)
            dsc_ref[...] = jnp.zeros_like(dsc_ref)

        @pl.when(i == 0)
        def _():
            dnw_ref[...] = jnp.zeros_like(dnw_ref)

        dsh_ref[...] += jnp.sum(dh, axis=0, keepdims=True)
        dsc_ref[...] += jnp.sum(dh * (xh * nw), axis=0, keepdims=True)
        du = dh * (1.0 + sc_ref[...])
        dnw_ref[...] += jnp.sum(du * xh, axis=0, keepdims=True)

        @pl.when(i < n_lat)
        def _():
            gx_ref[...] = gr_ref[...] + _rms_bwd(du * nw, xh, r)

    return pl.pallas_call(
        body, name="norm_bwd", grid=(rows // TM,),
        in_specs=[pl.BlockSpec((TM, D_MODEL), lambda i: (lat(i), 0)),
                  pl.BlockSpec((TM, D_MODEL), lambda i: (jnp.maximum(i - n_lat, 0), 0)),
                  pl.BlockSpec((TM, D_MODEL), lambda i: (i, 0)),
                  pl.BlockSpec((TM, D_MODEL), lambda i: (lat(i), 0)),
                  pl.BlockSpec((1, D_MODEL), lambda i: (0, 0)),
                  pl.BlockSpec((None, 1, D_MODEL), lambda i: (samp(i), 0, 0))],
        out_specs=(pl.BlockSpec((TM, D_MODEL), lambda i: (lat(i), 0)),
                   pl.BlockSpec((None, 1, D_MODEL), lambda i: (samp(i), 0, 0)),
                   pl.BlockSpec((None, 1, D_MODEL), lambda i: (samp(i), 0, 0)),
                   pl.BlockSpec((1, D_MODEL), lambda i: (0, 0))),
        out_shape=(SDS((n_lat * TM, D_MODEL), F32), SDS((n_samp + 1, 1, D_MODEL), F32),
                   SDS((n_samp + 1, 1, D_MODEL), F32), SDS((1, D_MODEL), F32)),
        compiler_params=_cp(("arbitrary",), 40))(x_lat, x_ctx, dhx, gx_res, norm_w, scale3)


def _gw_in(hxt, dpx_all):
    rows = dpx_all.shape[0]
    tb = _big_rows(rows)

    def body(h_ref, d_ref, o_ref):
        @pl.when(pl.program_id(1) == 0)
        def _():
            o_ref[...] = jnp.zeros_like(o_ref)

        o_ref[...] += jnp.dot(h_ref[...], d_ref[...], preferred_element_type=F32)

    return pl.pallas_call(
        body, name="gw_in", grid=(N_IN_BLK, rows // tb),
        in_specs=[pl.BlockSpec((D_MODEL, tb), lambda j, i: (0, i)),
                  pl.BlockSpec((tb, IN_BLK), lambda j, i: (i, j))],
        out_specs=pl.BlockSpec((None, D_MODEL, IN_BLK), lambda j, i: (j // 3, 0, j % 3)),
        out_shape=SDS((N_SHARD, D_MODEL, IN_W), F32),
        compiler_params=_cp(("parallel", "arbitrary"), 40))(hxt, dpx_all)


def _dhx(dpx_all, w_in_g):
    rows = dpx_all.shape[0]
    tb = _big_rows(rows)

    def body(d_ref, w_ref, o_ref):
        @pl.when(pl.program_id(1) == 0)
        def _():
            o_ref[...] = jnp.zeros_like(o_ref)

        o_ref[...] += lax.dot_general(d_ref[...], w_ref[...], (((1,), (1,)), ((), ())), preferred_element_type=F32)

    return pl.pallas_call(
        body, name="dhx", grid=(rows // tb, N_IN_BLK),
        in_specs=[pl.BlockSpec((tb, IN_BLK), lambda i, j: (i, j)),
                  pl.BlockSpec((None, D_MODEL, IN_BLK), lambda i, j: (j // 3, 0, j % 3))],
        out_specs=pl.BlockSpec((tb, D_MODEL), lambda i, j: (i, 0)),
        out_shape=SDS((rows, D_MODEL), F32),
        compiler_params=_cp(("parallel", "arbitrary"), 40))(dpx_all, w_in_g)


def _decays(lgv, d):
    c = RET_CHUNK
    ii = lax.broadcasted_iota(jnp.int32, (c, 1), 0).astype(F32)
    jj = lax.broadcasted_iota(jnp.int32, (1, c), 1).astype(F32)
    a_i = jnp.where(d == 0, ii, c - 1.0 - ii)
    a_j = jnp.where(d == 0, jj, c - 1.0 - jj)
    rel = a_i - a_j
    mask = jnp.where(rel >= 0, jnp.exp(lgv * jnp.maximum(rel, 0.0)), 0.0)
    qd = jnp.exp(lgv * (a_i + 1.0))
    kd = jnp.exp(lgv * (c - 1.0 - a_i))
    gc = jnp.exp(jnp.full((1, 1), lgv * c, F32))
    return a_i, rel, mask, qd, kd, gc


def _ctx_state_fwd(px, lg, n_samp, t_lat, lc):
    rb = t_lat // lc

    def body(lg_ref, k_ref, v_ref, o_ref):
        h = pl.program_id(1)
        k = k_ref[...].astype(F32) * (RET_DK ** -0.5)
        v = v_ref[...]
        pos = lax.broadcasted_iota(jnp.int32, (lc, 1), 0).astype(F32)
        o_ref[0] = _dot(k * jnp.exp(lg_ref[0, h] * (lc - 1.0 - pos)), v, 0, 0)
        o_ref[1] = _dot(k * jnp.exp(lg_ref[1, h] * pos), v, 0, 0)

    return pl.pallas_call(
        body, name="ctx_state_fwd", grid=(n_samp, RET_HEADS),
        in_specs=[SMEM,
                  pl.BlockSpec((lc, RET_DK), lambda b, h: (rb + b, C_RK // RET_DK + h)),
                  pl.BlockSpec((lc, RET_DV), lambda b, h: (rb + b, C_RV // RET_DV + h))],
        out_specs=pl.BlockSpec((None, 2, None, RET_DK, RET_DV), lambda b, h: (b, 0, h, 0, 0)),
        out_shape=SDS((n_samp, 2, RET_HEADS, RET_DK, RET_DV), F32),
        compiler_params=_cp(("parallel", "parallel")))(lg, px, px)


def _ctx_state_bwd(dpx, px, dstates, lg, n_samp, t_lat, lc):
    rb = t_lat // lc
    kspec = pl.BlockSpec((lc, RET_DK), lambda b, h: (rb + b, C_RK // RET_DK + h))
    vspec = pl.BlockSpec((lc, RET_DV), lambda b, h: (rb + b, C_RV // RET_DV + h))
    sspec = pl.BlockSpec((None, 2, None, RET_DK, RET_DV), lambda b, h: (b, 0, h, 0, 0))

    def weights(lg_ref, h):
        pos = lax.broadcasted_iota(jnp.int32, (lc, 1), 0).astype(F32)
        e_f = lc - 1.0 - pos
        return pos, e_f, jnp.exp(lg_ref[0, h] * e_f), jnp.exp(lg_ref[1, h] * pos)

    def k_body(lg_ref, dpx_hbm, k_ref, v_ref, ds_ref, dk_ref, dlg_ref):
        pos, e_f, w_f, w_b = weights(lg_ref, pl.program_id(1))
        k = k_ref[...].astype(F32) * (RET_DK ** -0.5)
        y_f = _dot(v_ref[...], ds_ref[0], 1, 1) * w_f
        y_b = _dot(v_ref[...], ds_ref[1], 1, 1) * w_b
        dk_ref[...] = ((y_f + y_b) * (RET_DK ** -0.5)).astype(BF)
        t_f = _sum_all(e_f * k * y_f)
        t_b = _sum_all(pos * k * y_b)
        sub = lax.broadcasted_iota(jnp.int32, (8, 128), 0)
        dlg_ref[...] = jnp.where(sub == 0, t_f, jnp.where(sub == 1, t_b, 0.0))

    def v_body(lg_ref, dpx_hbm, k_ref, ds_ref, dv_ref):
        _, _, w_f, w_b = weights(lg_ref, pl.program_id(1))
        k = k_ref[...].astype(F32) * (RET_DK ** -0.5)
        dv_ref[...] = (_dot(k * w_f, ds_ref[0]) + _dot(k * w_b, ds_ref[1])).astype(BF)

    dpx, dlg = pl.pallas_call(
        k_body, name="ctx_state_bwd_k", grid=(n_samp, RET_HEADS), input_output_aliases={1: 0},
        in_specs=[SMEM, ANY, kspec, vspec, sspec],
        out_specs=(kspec, pl.BlockSpec((None, None, 8, 128), lambda b, h: (b, h, 0, 0))),
        out_shape=(SDS(dpx.shape, dpx.dtype), SDS((n_samp, RET_HEADS, 8, 128), F32)),
        compiler_params=_cp(("parallel", "parallel")))(lg, dpx, px, px, dstates)
    dpx = pl.pallas_call(
        v_body, name="ctx_state_bwd_v", grid=(n_samp, RET_HEADS), input_output_aliases={1: 0},
        in_specs=[SMEM, ANY, kspec, sspec], out_specs=vspec, out_shape=SDS(dpx.shape, dpx.dtype),
        compiler_params=_cp(("parallel", "parallel")))(lg, dpx, px, dstates)
    return dpx, dlg


def _zero_ctx_tail(dpx, t_lat):
    wb = 512
    n_ctx = (dpx.shape[0] - t_lat) // TM

    def body(dpx_hbm, o_ref):
        o_ref[...] = jnp.zeros_like(o_ref)

    return pl.pallas_call(
        body, name="zero_ctx_tail", grid=(n_ctx, (IN_COLS - KV_COLS) // wb), input_output_aliases={0: 0},
        in_specs=[ANY], out_specs=pl.BlockSpec((TM, wb), lambda i, j: (t_lat // TM + i, KV_COLS // wb + j)),
        out_shape=SDS(dpx.shape, dpx.dtype),
        compiler_params=_cp(("parallel", "parallel")))(dpx)


def _ret_specs(row_f, row_b):
    c = RET_CHUNK
    wq = RET_HEADS * RET_DK // 2
    wv = RET_HEADS * RET_DV // 2
    specs = []
    for row in (row_f, row_b):
        specs += [pl.BlockSpec((c, wq), lambda b, n, row=row: (row(b, n), C_RQ // wq)),
                  pl.BlockSpec((c, wq), lambda b, n, row=row: (row(b, n), C_RQ // wq + 1)),
                  pl.BlockSpec((c, 2 * wq), lambda b, n, row=row: (row(b, n), C_RK // (2 * wq))),
                  pl.BlockSpec((c, wv), lambda b, n, row=row: (row(b, n), C_RV // wv)),
                  pl.BlockSpec((c, wv), lambda b, n, row=row: (row(b, n), C_RV // wv + 1))]
    return specs


def _ret_head(refs, h):
    q0, q1, k_ref, v0, v1 = refs
    hh = h % 2
    q = (q0, q1)[h // 2][:, hh * RET_DK:(hh + 1) * RET_DK].astype(F32)
    k = k_ref[:, h * RET_DK:(h + 1) * RET_DK].astype(F32) * (RET_DK ** -0.5)
    v = (v0, v1)[h // 2][:, hh * RET_DV:(hh + 1) * RET_DV]
    return q, k, v


def _ret_fwd(px, states0, lg, n_samp, seq):
    c = RET_CHUNK
    nc = seq // c
    t_lat = n_samp * seq
    wo = RET_HEADS * RET_DV

    def row_f(b, n):
        return b * nc + n

    def row_b(b, n):
        return b * nc + nc - 1 - n

    def body(lg_ref, *refs):
        ins, (s0_ref, of_ref, ob_ref, st_ref, s_s) = refs[:10], refs[10:]

        @pl.when(pl.program_id(1) == 0)
        def _():
            s_s[...] = s0_ref[...]

        for d, o_ref in ((0, of_ref), (1, ob_ref)):
            for h in range(RET_HEADS):
                _, _, mask, qd, kd, gc = _decays(lg_ref[d, h], d)
                q, k, v = _ret_head(ins[5 * d:5 * d + 5], h)
                s = s_s[d, h]
                st_ref[h, d] = s.astype(BF)
                sc = _dot(q, k, 1, 1) * mask
                o_ref[:, h * RET_DV:(h + 1) * RET_DV] = _dot(sc, v) + _dot(q * qd, s)
                s_s[d, h] = s * gc + _dot(k * kd, v, 0, 0)

    return pl.pallas_call(
        body, name="ret_fwd", grid=(n_samp, nc),
        in_specs=[SMEM] + _ret_specs(row_f, row_b) + [
            pl.BlockSpec((None, 2, RET_HEADS, RET_DK, RET_DV), lambda b, n: (b, 0, 0, 0, 0))],
        out_specs=(pl.BlockSpec((c, wo), lambda b, n: (row_f(b, n), 0)),
                   pl.BlockSpec((c, wo), lambda b, n: (row_b(b, n), 0)),
                   pl.BlockSpec((None, RET_HEADS, 2, None, RET_DK, RET_DV), lambda b, n: (b, 0, 0, n, 0, 0))),
        out_shape=(SDS((t_lat, wo), F32), SDS((t_lat, wo), F32),
                   SDS((n_samp, RET_HEADS, 2, nc, RET_DK, RET_DV), BF)),
        scratch_shapes=[pltpu.VMEM((2, RET_HEADS, RET_DK, RET_DV), F32)],
        compiler_params=_cp(("parallel", "arbitrary"), 48))(lg, *([px] * 10), states0)


def _ret_bwd(px, do, saved, lg, n_samp, seq):
    c = RET_CHUNK
    nc = seq // c
    t_lat = n_samp * seq
    wq, wo = RET_HEADS * RET_DK, RET_HEADS * RET_DV

    def row_f(b, n):
        return b * nc + nc - 1 - n

    def row_b(b, n):
        return b * nc + n

    def body(lg_ref, *refs):
        ins = refs[:10]
        (dof_ref, dob_ref, st_ref, dqf, dkf, dvf, dqb, dkb, dvb, ds0_ref, dlg_ref, ds_s, acc_s) = refs[10:]
        n = pl.program_id(1)

        @pl.when(n == 0)
        def _():
            ds_s[...] = jnp.zeros_like(ds_s)
            acc_s[...] = jnp.zeros_like(acc_s)

        for d, (do_ref, dq_ref, dk_ref, dv_ref) in enumerate(((dof_ref, dqf, dkf, dvf), (dob_ref, dqb, dkb, dvb))):
            for h in range(RET_HEADS):
                a_i, rel, mask, qd, kd, gc = _decays(lg_ref[d, h], d)
                q, k, v = _ret_head(ins[5 * d:5 * d + 5], h)
                qb, kb, vb = q.astype(BF), k.astype(BF), v.astype(BF)
                dob = do_ref[:, h * RET_DV:(h + 1) * RET_DV].astype(BF)
                sb = st_ref[h, d]
                ds = ds_s[d, h]
                dsb = ds.astype(BF)
                raw = _dot(qb, kb, 1, 1)
                sc = raw * mask
                dsc = _dot(dob, vb, 1, 1) * mask
                dscb = dsc.astype(BF)
                x = _dot(dob, sb, 1, 1)
                y = _dot(vb, dsb, 1, 1)
                qq = q * qd
                kk = k * kd
                dq_ref[:, h * RET_DK:(h + 1) * RET_DK] = _dot(dscb, kb) + x * qd
                dk_ref[:, h * RET_DK:(h + 1) * RET_DK] = _dot(dscb, qb, 0, 0) + y * kd
                dv_ref[:, h * RET_DV:(h + 1) * RET_DV] = _dot(sc, dob, 0, 0) + _dot(kk, dsb)
                t = (_sum_all(dsc * raw * rel) + _sum_all((a_i + 1.0) * qq * x)
                     + _sum_all((c - 1.0 - a_i) * kk * y) + c * gc * _sum_all(ds * sb.astype(F32)))
                acc_s[4 * d + h:4 * d + h + 1, :] += t
                ds_s[d, h] = ds * gc + _dot(qq, dob, 0, 0)

        @pl.when(n == nc - 1)
        def _():
            ds0_ref[...] = ds_s[...]
            dlg_ref[...] = acc_s[...]

    do_spec_f = pl.BlockSpec((c, wo), lambda b, n: (row_f(b, n), 0))
    do_spec_b = pl.BlockSpec((c, wo), lambda b, n: (row_b(b, n), 0))
    dq_spec_f = pl.BlockSpec((c, wq), lambda b, n: (row_f(b, n), 0))
    dq_spec_b = pl.BlockSpec((c, wq), lambda b, n: (row_b(b, n), 0))
    return pl.pallas_call(
        body, name="ret_bwd", grid=(n_samp, nc),
        in_specs=[SMEM] + _ret_specs(row_f, row_b) + [
            do_spec_f, do_spec_b,
            pl.BlockSpec((None, RET_HEADS, 2, None, RET_DK, RET_DV), lambda b, n: (b, 0, 0, nc - 1 - n, 0, 0))],
        out_specs=(dq_spec_f, dq_spec_f, do_spec_f, dq_spec_b, dq_spec_b, do_spec_b,
                   pl.BlockSpec((None, 2, RET_HEADS, RET_DK, RET_DV), lambda b, n: (b, 0, 0, 0, 0)),
                   pl.BlockSpec((None, 8, 128), lambda b, n: (b, 0, 0))),
        out_shape=(SDS((t_lat, wq), F32), SDS((t_lat, wq), F32), SDS((t_lat, wo), F32),
                   SDS((t_lat, wq), F32), SDS((t_lat, wq), F32), SDS((t_lat, wo), F32),
                   SDS((n_samp, 2, RET_HEADS, RET_DK, RET_DV), F32), SDS((n_samp, 8, 128), F32)),
        scratch_shapes=[pltpu.VMEM((2, RET_HEADS, RET_DK, RET_DV), F32), pltpu.VMEM((8, 128), F32)],
        compiler_params=_cp(("parallel", "arbitrary"), 56))(lg, *([px] * 10), do, do, saved)


def _combine_into(dpx, a, b, col0, scale):
    t_lat, width = a.shape
    wb = 512
    assert col0 % wb == 0 and width % wb == 0

    def body(dpx_hbm, a_ref, b_ref, o_ref):
        o_ref[...] = ((a_ref[...] + b_ref[...]) * scale).astype(BF)

    src = pl.BlockSpec((TM, wb), lambda i, j: (i, j))
    return pl.pallas_call(
        body, name="combine_into", grid=(t_lat // TM, width // wb), input_output_aliases={0: 0},
        in_specs=[ANY, src, src], out_specs=pl.BlockSpec((TM, wb), lambda i, j: (i, col0 // wb + j)),
        out_shape=SDS(dpx.shape, dpx.dtype),
        compiler_params=_cp(("parallel", "parallel")))(dpx, a, b)


def _retnorm_fwd(o_f, o_b, px):
    t_lat = o_f.shape[0]

    def body(of_ref, ob_ref, g_ref, y_ref):
        o = of_ref[...] + ob_ref[...]
        g = g_ref[...].astype(F32)
        y_ref[...] = (o * _rms(o) * (g * _sigmoid(g))).astype(BF)

    so = pl.BlockSpec((TM, RET_DV), lambda i, h: (i, h))
    return pl.pallas_call(
        body, name="retnorm_fwd", grid=(t_lat // TM, RET_HEADS),
        in_specs=[so, so, pl.BlockSpec((TM, RET_DV), lambda i, h: (i, C_RG // RET_DV + h))],
        out_specs=so,
        out_shape=SDS((t_lat, RET_HEADS * RET_DV), BF),
        compiler_params=_cp(("parallel", "parallel")))(o_f, o_b, px)


def _retnorm_bwd(dpx, dy, o_f, o_b, px):
    t_lat = o_f.shape[0]

    def body(dpx_hbm, dy_ref, of_ref, ob_ref, g_ref, do_ref, dg_ref):
        o = of_ref[...] + ob_ref[...]
        r = _rms(o)
        on = o * r
        g = g_ref[...].astype(F32)
        sg = _sigmoid(g)
        dy_ = dy_ref[...]
        dg_ref[...] = (dy_ * on * (sg * (1.0 + g * (1.0 - sg)))).astype(BF)
        do_ref[...] = _rms_bwd(dy_ * (g * sg), on, r)

    so = pl.BlockSpec((TM, RET_DV), lambda i, h: (i, h))
    gcol = pl.BlockSpec((TM, RET_DV), lambda i, h: (i, C_RG // RET_DV + h))
    return pl.pallas_call(
        body, name="retnorm_bwd", grid=(t_lat // TM, RET_HEADS), input_output_aliases={0: 1},
        in_specs=[ANY, so, so, so, gcol],
        out_specs=(so, gcol),
        out_shape=(SDS((t_lat, RET_HEADS * RET_DV), F32), SDS(dpx.shape, dpx.dtype)),
        compiler_params=_cp(("parallel", "parallel")))(dpx, dy, o_f, o_b, px)


def _norm_rope(x, w, cos, sin):
    xn = x * _rms(x) * w
    return xn * cos + _swap_pairs(xn) * sin


def _norm_rope_bwd(dy, x, w, cos, sin):
    dxn = dy * cos + _swap_pairs(dy * sin)
    r = _rms(x)
    xh = x * r
    return _rms_bwd(dxn * w, xh, r), jnp.sum(dxn * xh, axis=0, keepdims=True)


def _att_prep_q(px, cos_all, sin_all, qnw, t_lat):
    hd = ATT_HEAD_DIM
    wblk = ATT_REP * hd

    def body(x_ref, cos_ref, sin_ref, w_ref, o_ref):
        for r in range(ATT_REP):
            cols = slice(r * hd, (r + 1) * hd)
            qr = _norm_rope(x_ref[:, cols].astype(F32), w_ref[...], cos_ref[...], sin_ref[...])
            o_ref[:, cols] = (qr * (hd ** -0.5)).astype(BF)

    return pl.pallas_call(
        body, name="att_prep_q", grid=(t_lat // TM, ATT_KV_HEADS),
        in_specs=[pl.BlockSpec((TM, wblk), lambda i, g: (i, C_AQ // wblk + g)),
                  pl.BlockSpec((TM, hd), lambda i, g: (i, 0)),
                  pl.BlockSpec((TM, hd), lambda i, g: (i, 0)),
                  pl.BlockSpec((1, hd), lambda i, g: (0, 0))],
        out_specs=pl.BlockSpec((TM, wblk), lambda i, g: (i, g)),
        out_shape=SDS((t_lat, ATT_HEADS * hd), BF),
        compiler_params=_cp(("parallel", "parallel")))(px, cos_all, sin_all, qnw)


def _att_prep_kv(px, cos_all, sin_all, knw):
    rows = px.shape[0]
    hd = ATT_HEAD_DIM
    kvw = ATT_KV_HEADS * hd

    def body(x_ref, cos_ref, sin_ref, w_ref, k_ref, v_ref):
        for g in range(ATT_KV_HEADS):
            cols = slice(g * hd, (g + 1) * hd)
            k_ref[:, cols] = _norm_rope(x_ref[:, cols].astype(F32), w_ref[...], cos_ref[...],
                                        sin_ref[...]).astype(BF)
        v_ref[...] = x_ref[:, kvw:].astype(BF)

    return pl.pallas_call(
        body, name="att_prep_kv", grid=(rows // TM,),
        in_specs=[pl.BlockSpec((TM, 2 * kvw), lambda i: (i, C_AK // (2 * kvw))),
                  pl.BlockSpec((TM, hd), lambda i: (i, 0)),
                  pl.BlockSpec((TM, hd), lambda i: (i, 0)),
                  pl.BlockSpec((1, hd), lambda i: (0, 0))],
        out_specs=(pl.BlockSpec((TM, kvw), lambda i: (i, 0)), pl.BlockSpec((TM, kvw), lambda i: (i, 0))),
        out_shape=(SDS((rows, kvw), BF), SDS((rows, kvw), BF)),
        compiler_params=_cp(("parallel",)))(px, cos_all, sin_all, knw)


def _att_kv_bwd(dpx, dkl, dkc, dvl, dvc, px, cos_all, sin_all, knw):
    rows = px.shape[0]
    hd = ATT_HEAD_DIM
    kvw = ATT_KV_HEADS * hd
    n_lat = dkl.shape[0] // TM
    assert dkc.shape[0] == TM

    def body(dpx_hbm, dkl_ref, dkc_ref, dvl_ref, dvc_ref, x_ref, cos_ref, sin_ref, w_ref, o_ref, gw_ref):
        i = pl.program_id(0)

        @pl.when(i == 0)
        def _():
            gw_ref[...] = jnp.zeros_like(gw_ref)

        is_lat = i < n_lat
        dk = jnp.where(is_lat, dkl_ref[...], dkc_ref[...])
        dv = jnp.where(is_lat, dvl_ref[...], dvc_ref[...])
        for g in range(ATT_KV_HEADS):
            cols = slice(g * hd, (g + 1) * hd)
            dx, gw = _norm_rope_bwd(dk[:, cols], x_ref[:, cols].astype(F32), w_ref[...], cos_ref[...], sin_ref[...])
            o_ref[:, cols] = dx.astype(BF)
            gw_ref[...] += gw
        o_ref[:, kvw:] = dv.astype(BF)

    lat = pl.BlockSpec((TM, kvw), lambda i: (jnp.minimum(i, n_lat - 1), 0))
    ctx = pl.BlockSpec((TM, kvw), lambda i: (0, 0))
    kvcol = pl.BlockSpec((TM, 2 * kvw), lambda i: (i, C_AK // (2 * kvw)))
    return pl.pallas_call(
        body, name="att_kv_bwd", grid=(rows // TM,), input_output_aliases={0: 0},
        in_specs=[ANY, lat, ctx, lat, ctx, kvcol,
                  pl.BlockSpec((TM, hd), lambda i: (i, 0)),
                  pl.BlockSpec((TM, hd), lambda i: (i, 0)),
                  pl.BlockSpec((1, hd), lambda i: (0, 0))],
        out_specs=(kvcol, pl.BlockSpec((1, hd), lambda i: (0, 0))),
        out_shape=(SDS(dpx.shape, dpx.dtype), SDS((1, hd), F32)),
        compiler_params=_cp(("arbitrary",)))(dpx, dkl, dkc, dvl, dvc, px, cos_all, sin_all, knw)


def _stack_heads(ref_or_val):
    hd = ATT_HEAD_DIM
    return jnp.concatenate([ref_or_val[:, r * hd:(r + 1) * hd] for r in range(ATT_REP)], axis=0)


def _att_scores(q, kl, kc):
    sl = _dot(q, kl, 1, 1)
    sc = _dot(q, kc, 1, 1)
    m = jnp.maximum(jnp.max(sl, axis=-1, keepdims=True), jnp.max(sc, axis=-1, keepdims=True))
    el = jnp.exp(sl - m)
    ec = jnp.exp(sc - m)
    denom = jnp.sum(el, axis=-1, keepdims=True) + jnp.sum(ec, axis=-1, keepdims=True)
    return el, ec, denom


def _att_fwd(qn, kn, vn, px, n_samp, seq, lc):
    hd = ATT_HEAD_DIM
    tq = 128
    nq = seq // tq
    wblk = ATT_REP * hd
    cb = n_samp * seq // lc
    t_lat = n_samp * seq

    def body(q_ref, kl_ref, kc_ref, vl_ref, vc_ref, g_ref, y_ref, o_ref):
        for r in range(ATT_REP):
            cols = slice(r * hd, (r + 1) * hd)
            el, ec, denom = _att_scores(q_ref[:, cols], kl_ref[...], kc_ref[...])
            o = (_dot(el, vl_ref[...]) + _dot(ec, vc_ref[...])) / denom
            g = g_ref[:, cols].astype(F32)
            o_ref[:, cols] = o
            y_ref[:, cols] = (o * (g * _sigmoid(g))).astype(BF)

    return pl.pallas_call(
        body, name="att_fwd", grid=(n_samp, ATT_KV_HEADS, nq),
        in_specs=[pl.BlockSpec((tq, wblk), lambda b, g, i: (b * nq + i, g)),
                  pl.BlockSpec((seq, hd), lambda b, g, i: (b, g)),
                  pl.BlockSpec((lc, hd), lambda b, g, i: (cb + b, g)),
                  pl.BlockSpec((seq, hd), lambda b, g, i: (b, g)),
                  pl.BlockSpec((lc, hd), lambda b, g, i: (cb + b, g)),
                  pl.BlockSpec((tq, wblk), lambda b, g, i: (b * nq + i, C_AG // wblk + g))],
        out_specs=(pl.BlockSpec((tq, wblk), lambda b, g, i: (b * nq + i, g)),
                   pl.BlockSpec((tq, wblk), lambda b, g, i: (b * nq + i, g))),
        out_shape=(SDS((t_lat, ATT_HEADS * hd), BF), SDS((t_lat, ATT_HEADS * hd), F32)),
        compiler_params=_cp(("parallel", "parallel", "parallel"), 48))(qn, kn, kn, vn, vn, px)


def _att_gate_bwd(dpx, dy_att, o_att, px):
    t_lat = dy_att.shape[0]
    wblk = ATT_REP * ATT_HEAD_DIM

    def body(dpx_hbm, dy_ref, o_ref, g_ref, out_ref):
        g = g_ref[...].astype(F32)
        sg = _sigmoid(g)
        out_ref[...] = (dy_ref[...] * o_ref[...] * (sg * (1.0 + g * (1.0 - sg)))).astype(BF)

    blk = pl.BlockSpec((TM, wblk), lambda i, j: (i, j))
    gcol = pl.BlockSpec((TM, wblk), lambda i, j: (i, C_AG // wblk + j))
    return pl.pallas_call(
        body, name="att_gate_bwd", grid=(t_lat // TM, ATT_KV_HEADS),
        in_specs=[ANY, blk, blk, gcol], out_specs=gcol, out_shape=SDS(dpx.shape, dpx.dtype),
        input_output_aliases={0: 0},
        compiler_params=_cp(("parallel", "parallel")))(dpx, dy_att, o_att, px)


def _att_bwd(dpx, qn, kn, vn, px, o_att, dy_att, cos_all, sin_all, qnw, n_samp, seq, lc):
    hd = ATT_HEAD_DIM
    tq = 128
    nq = seq // tq
    wblk = ATT_REP * hd
    cb = n_samp * seq // lc
    t_lat = n_samp * seq
    kvw = ATT_KV_HEADS * hd
    scale = hd ** -0.5

    def body(dpx_hbm, q_ref, kl_ref, kc_ref, vl_ref, vc_ref, g_ref, o_ref, dy_ref, x_ref, cos_ref, sin_ref, w_ref,
             dq_ref, dkl_ref, dkc_ref, dvl_ref, dvc_ref, gw_ref, akl, akc, avl, avc, aw):
        i = pl.program_id(2)

        @pl.when(i == 0)
        def _():
            akl[...] = jnp.zeros_like(akl)
            akc[...] = jnp.zeros_like(akc)
            avl[...] = jnp.zeros_like(avl)
            avc[...] = jnp.zeros_like(avc)
            aw[...] = jnp.zeros_like(aw)

        dobs, pls, pcs, dsls, dscs = [], [], [], [], []
        for r in range(ATT_REP):
            cols = slice(r * hd, (r + 1) * hd)
            g = g_ref[:, cols].astype(F32)
            sg = _sigmoid(g)
            dy = dy_ref[:, cols]
            do = dy * (g * sg)
            delta = jnp.sum(do * o_ref[:, cols], axis=-1, keepdims=True)
            el, ec, denom = _att_scores(q_ref[:, cols], kl_ref[...], kc_ref[...])
            inv = 1.0 / denom
            p_l = el * inv
            p_c = ec * inv
            dob = do.astype(BF)
            ds_l = (p_l * (_dot(dob, vl_ref[...], 1, 1) - delta)).astype(BF)
            ds_c = (p_c * (_dot(dob, vc_ref[...], 1, 1) - delta)).astype(BF)
            dq = (_dot(ds_l, kl_ref[...]) + _dot(ds_c, kc_ref[...])) * scale
            dx, gw = _norm_rope_bwd(dq, x_ref[:, cols].astype(F32), w_ref[...], cos_ref[...], sin_ref[...])
            dq_ref[:, cols] = dx.astype(BF)
            aw[...] += gw
            dobs.append(dob)
            pls.append(p_l.astype(BF))
            pcs.append(p_c.astype(BF))
            dsls.append(ds_l)
            dscs.append(ds_c)
        do4 = jnp.concatenate(dobs, axis=0)
        q4 = _stack_heads(q_ref)
        avl[...] += _dot(jnp.concatenate(pls, axis=0), do4, 0, 0)
        avc[...] += _dot(jnp.concatenate(pcs, axis=0), do4, 0, 0)
        akl[...] += _dot(jnp.concatenate(dsls, axis=0), q4, 0, 0)
        akc[...] += _dot(jnp.concatenate(dscs, axis=0), q4, 0, 0)

        @pl.when(i == nq - 1)
        def _():
            dkl_ref[...] = akl[...]
            dkc_ref[...] = akc[...]
            dvl_ref[...] = avl[...]
            dvc_ref[...] = avc[...]
            gw_ref[...] = aw[...]

    return pl.pallas_call(
        body, name="att_bwd", grid=(n_samp, ATT_KV_HEADS, nq), input_output_aliases={0: 0},
        in_specs=[ANY,
                  pl.BlockSpec((tq, wblk), lambda b, g, i: (b * nq + i, g)),
                  pl.BlockSpec((seq, hd), lambda b, g, i: (b, g)),
                  pl.BlockSpec((lc, hd), lambda b, g, i: (cb + b, g)),
                  pl.BlockSpec((seq, hd), lambda b, g, i: (b, g)),
                  pl.BlockSpec((lc, hd), lambda b, g, i: (cb + b, g)),
                  pl.BlockSpec((tq, wblk), lambda b, g, i: (b * nq + i, C_AG // wblk + g)),
                  pl.BlockSpec((tq, wblk), lambda b, g, i: (b * nq + i, g)),
                  pl.BlockSpec((tq, wblk), lambda b, g, i: (b * nq + i, g)),
                  pl.BlockSpec((tq, wblk), lambda b, g, i: (b * nq + i, C_AQ // wblk + g)),
                  pl.BlockSpec((tq, hd), lambda b, g, i: (b * nq + i, 0)),
                  pl.BlockSpec((tq, hd), lambda b, g, i: (b * nq + i, 0)),
                  pl.BlockSpec((1, hd), lambda b, g, i: (0, 0))],
        out_specs=(pl.BlockSpec((tq, wblk), lambda b, g, i: (b * nq + i, C_AQ // wblk + g)),
                   pl.BlockSpec((seq, hd), lambda b, g, i: (b, g)),
                   pl.BlockSpec((lc, hd), lambda b, g, i: (b, g)),
                   pl.BlockSpec((seq, hd), lambda b, g, i: (b, g)),
                   pl.BlockSpec((lc, hd), lambda b, g, i: (b, g)),
                   pl.BlockSpec((None, None, 1, hd), lambda b, g, i: (b, g, 0, 0))),
        out_shape=(SDS(dpx.shape, dpx.dtype),
                   SDS((t_lat, kvw), F32), SDS((n_samp * lc, kvw), F32),
                   SDS((t_lat, kvw), F32), SDS((n_samp * lc, kvw), F32),
                   SDS((n_samp, ATT_KV_HEADS, 1, hd), F32)),
        scratch_shapes=[pltpu.VMEM((seq, hd), F32), pltpu.VMEM((lc, hd), F32),
                        pltpu.VMEM((seq, hd), F32), pltpu.VMEM((lc, hd), F32), pltpu.VMEM((1, hd), F32)],
        compiler_params=_cp(("parallel", "parallel", "arbitrary"), 56))(
            dpx, qn, kn, kn, vn, vn, px, o_att, dy_att, px, cos_all, sin_all, qnw)


def _merge(x_lat, target, y_ret, y_att, px, gate3, w_o_ret, w_o_att, w_out, tiles_per_sample):
    t_lat = x_lat.shape[0]
    tm = 256
    n_t = t_lat // tm
    per = tiles_per_sample * (TM // tm)
    d = D_MODEL
    rv = RET_HEADS * RET_DV
    n_samp = gate3.shape[0] - 1

    def body(x_ref, t_ref, yr_ref, ya_ref, mr0, mr1, ma0, ma1, gt_ref, wor_ref, woa_ref, wout_ref,
             gx_ref, dyr_ref, dya_ref, dpx_hbm, loss_ref, dgt_ref, gwor_hbm, gwoa_hbm, gwout_hbm,
             aor, aoa, aout, dmg_ref, dmg_sem):
        i = pl.program_id(0)

        def dmg_copy(step):
            rows = pl.ds(pl.multiple_of(step * tm, tm), tm)
            return pltpu.make_async_copy(dmg_ref, dpx_hbm.at[rows, pl.ds(C_MR, 2 * d)], dmg_sem)

        @pl.when(i == 0)
        def _():
            aor[...] = jnp.zeros_like(aor)
            aoa[...] = jnp.zeros_like(aoa)
            aout[...] = jnp.zeros_like(aout)
            loss_ref[...] = jnp.zeros_like(loss_ref)

        @pl.when(i % per == 0)
        def _():
            dgt_ref[...] = jnp.zeros_like(dgt_ref)

        yr = yr_ref[...]
        ya = ya_ref[...]
        a = jnp.dot(yr, wor_ref[...], preferred_element_type=F32)
        b = jnp.dot(ya, woa_ref[...], preferred_element_type=F32)
        sr = _sigmoid(jnp.concatenate([mr0[...], mr1[...]], axis=1).astype(F32))
        sa = _sigmoid(jnp.concatenate([ma0[...], ma1[...]], axis=1).astype(F32))
        yb = (sr * a + sa * b).astype(BF)
        out = jnp.dot(yb, wout_ref[...], preferred_element_type=F32)
        gate = gt_ref[...]
        err = x_ref[...] + gate * out - t_ref[...]
        loss_ref[...] += 0.5 * _sum_all(err * err) * (1.0 / d)
        dy_tok = err * (1.0 / d)
        gx_ref[...] = dy_tok
        dgt_ref[...] += jnp.sum(dy_tok * out, axis=0, keepdims=True)
        dout = (dy_tok * gate).astype(BF)
        aout[...] += _dot(yb, dout, 0, 0)
        dyy = _dot(dout, wout_ref[...], 1, 1)
        da = (dyy * sr).astype(BF)
        db = (dyy * sa).astype(BF)
        @pl.when(i > 0)
        def _():
            dmg_copy(i - 1).wait()

        dmg_ref[:, :d] = (dyy * a * (sr * (1.0 - sr))).astype(BF)
        dmg_ref[:, d:] = (dyy * b * (sa * (1.0 - sa))).astype(BF)
        dmg_copy(i).start()
        aor[...] += _dot(yr, da, 0, 0)
        aoa[...] += _dot(ya, db, 0, 0)
        dyr_ref[...] = _dot(da, wor_ref[...], 1, 1)
        dya_ref[...] = _dot(db, woa_ref[...], 1, 1)

        @pl.when(i == n_t - 1)
        def _():
            dmg_copy(i).wait()
            pltpu.sync_copy(aor, gwor_hbm)
            pltpu.sync_copy(aoa, gwoa_hbm)
            pltpu.sync_copy(aout, gwout_hbm)

    half = d // 2
    return pl.pallas_call(
        body, name="merge", grid=(n_t,),
        in_specs=[pl.BlockSpec((tm, d), lambda i: (i, 0)),
                  pl.BlockSpec((tm, d), lambda i: (i, 0)),
                  pl.BlockSpec((tm, rv), lambda i: (i, 0)),
                  pl.BlockSpec((tm, d), lambda i: (i, 0)),
                  pl.BlockSpec((tm, half), lambda i: (i, C_MR // half)),
                  pl.BlockSpec((tm, half), lambda i: (i, C_MR // half + 1)),
                  pl.BlockSpec((tm, half), lambda i: (i, C_MA // half)),
                  pl.BlockSpec((tm, half), lambda i: (i, C_MA // half + 1)),
                  pl.BlockSpec((None, 1, d), lambda i: (i // per, 0, 0)),
                  pl.BlockSpec((rv, d), lambda i: (0, 0)),
                  pl.BlockSpec((d, d), lambda i: (0, 0)),
                  pl.BlockSpec((d, d), lambda i: (0, 0))],
        out_specs=(pl.BlockSpec((tm, d), lambda i: (i, 0)),
                   pl.BlockSpec((tm, rv), lambda i: (i, 0)),
                   pl.BlockSpec((tm, d), lambda i: (i, 0)),
                   ANY,
                   pl.BlockSpec((8, 128), lambda i: (0, 0)),
                   pl.BlockSpec((None, 1, d), lambda i: (i // per, 0, 0)),
                   ANY, ANY, ANY),
        out_shape=(SDS((t_lat, d), F32), SDS((t_lat, rv), F32), SDS((t_lat, d), F32),
                   SDS((px.shape[0], IN_COLS), BF),
                   SDS((8, 128), F32), SDS((n_samp, 1, d), F32),
                   SDS((rv, d), F32), SDS((d, d), F32), SDS((d, d), F32)),
        scratch_shapes=[pltpu.VMEM((rv, d), F32), pltpu.VMEM((d, d), F32), pltpu.VMEM((d, d), F32),
                        pltpu.VMEM((tm, 2 * d), BF), pltpu.SemaphoreType.DMA],
        compiler_params=_cp(("arbitrary",), 56))(
            x_lat, target, y_ret, y_att, px, px, px, px, gate3, w_o_ret, w_o_att, w_out)


def _place():
    x, y, c = lax.axis_index("x"), lax.axis_index("y"), lax.axis_index("c")
    chips = [(1 - x, y), (x, 1 - y), (1 - x, 1 - y)]
    return x, y, c, chips


def _remote(src, dst, send_sem, recv_sem, to):
    return pltpu.make_async_remote_copy(src_ref=src, dst_ref=dst, send_sem=send_sem, recv_sem=recv_sem,
                                        device_id=to, device_id_type=MESH)


def _place_ids():
    x, y, c = lax.axis_index("x"), lax.axis_index("y"), lax.axis_index("c")
    return jnp.stack([x, y, c, 2 * x + y]).astype(jnp.int32)


def _cast_place(w, ids):
    rows, cols = w.shape
    tr = min(rows, 256)

    def body(ids_ref, w_ref, o_ref):
        o_ref[...] = w_ref[...].astype(BF)

    return pl.pallas_call(
        body, name="cast_place",
        grid_spec=pltpu.PrefetchScalarGridSpec(
            num_scalar_prefetch=1, grid=(rows // tr,),
            in_specs=[pl.BlockSpec((tr, cols), lambda i, ids_ref: (i, 0))],
            out_specs=pl.BlockSpec((None, tr, cols), lambda i, ids_ref: (ids_ref[3], i, 0))),
        out_shape=SDS((N_SHARD, rows, cols), BF),
        compiler_params=_cp(("parallel",), 40))(ids, w)


def _all_gather_weights(bufs):
    n = len(bufs)

    def body(*refs):
        outs = refs[n:2 * n]
        send_sems, recv_sems = refs[2 * n:]
        x, y, c, chips = _place()
        sibling = (x, y, 1 - c)
        me = 2 * x + y

        def half(ref, s, which):
            h = ref.shape[1] // 2
            return ref.at[s, pl.ds(which * h, h), :]

        first = []
        for a in range(n):
            for j, chip in enumerate(chips):
                k = a * 3 + j
                win = half(outs[a], me, c)
                first.append(_remote(win, win, send_sems.at[k], recv_sems.at[k], (*chip, c)))
        for cp in first:
            cp.start()
        passed = []
        for a in range(n):
            for j, chip in enumerate(chips):
                k = a * 3 + j
                win = half(outs[a], 2 * chip[0] + chip[1], c)
                _remote(win, win, send_sems.at[k], recv_sems.at[k], (*chip, c)).wait_recv()
                fw = _remote(win, win, send_sems.at[3 * n + k], recv_sems.at[3 * n + k], sibling)
                fw.start()
                passed.append(fw)
        for a in range(n):
            for j, chip in enumerate(chips):
                k = a * 3 + j
                win = half(outs[a], 2 * chip[0] + chip[1], 1 - c)
                _remote(win, win, send_sems.at[3 * n + k], recv_sems.at[3 * n + k], sibling).wait_recv()
        for cp in first + passed:
            cp.wait_send()

    return pl.pallas_call(
        body, name="all_gather_weights",
        in_specs=[ANY] * n, out_specs=tuple([ANY] * n),
        out_shape=tuple(SDS(b.shape, b.dtype) for b in bufs),
        input_output_aliases={a: a for a in range(n)},
        scratch_shapes=[pltpu.SemaphoreType.DMA((6 * n,)), pltpu.SemaphoreType.DMA((6 * n,))],
        compiler_params=_cp(has_side_effects=True))(*bufs)


def _swap_halves(grads):
    n = len(grads)

    def body(*refs):
        ins, outs = refs[:n], refs[n:2 * n]
        send_sems, recv_sems = refs[2 * n:]
        x, y, c, _ = _place()
        sibling = (x, y, 1 - c)

        def half(ref, which):
            h = ref.shape[1] // 2
            return ref.at[:, pl.ds(which * h, h), :]

        sends = [_remote(half(ins[a], 1 - c), outs[a], send_sems.at[a], recv_sems.at[a], sibling)
                 for a in range(n)]
        for cp in sends:
            cp.start()
        for cp in sends:
            cp.wait_recv()
        for cp in sends:
            cp.wait_send()

    return pl.pallas_call(
        body, name="swap_halves",
        in_specs=[ANY] * n, out_specs=tuple([ANY] * n),
        out_shape=tuple(SDS((g.shape[0], g.shape[1] // 2, g.shape[2]), g.dtype) for g in grads),
        scratch_shapes=[pltpu.SemaphoreType.DMA((n,)), pltpu.SemaphoreType.DMA((n,))],
        compiler_params=_cp(has_side_effects=True))(*grads)


def _chip_sum(g, p, ids):
    n_s, rows, cols = g.shape
    h = rows // 2
    tr = min(h, 256)
    nb = h // tr

    def body(ids_ref, g_ref, p_ref, o_ref, o16_ref):
        t = g_ref[...] + p_ref[...]
        o_ref[...] = t
        o16_ref[...] = t.astype(BF)

    out_spec = pl.BlockSpec((None, tr, cols), lambda s, i, ids_ref: (s, i, 0))
    return pl.pallas_call(
        body, name="chip_sum",
        grid_spec=pltpu.PrefetchScalarGridSpec(
            num_scalar_prefetch=1, grid=(n_s, nb),
            in_specs=[pl.BlockSpec((None, tr, cols), lambda s, i, ids_ref: (s, ids_ref[2] * nb + i, 0)),
                      pl.BlockSpec((None, tr, cols), lambda s, i, ids_ref: (s, i, 0))],
            out_specs=(out_spec, out_spec)),
        out_shape=(SDS((n_s, h, cols), g.dtype), SDS((n_s, h, cols), BF)),
        compiler_params=_cp(("parallel", "parallel"), 40))(ids, g, p)


def _exchange_shards(parts):
    n = len(parts)

    def body(*refs):
        ins, outs = refs[:n], refs[n:2 * n]
        send_sems, recv_sems = refs[2 * n:]
        x, y, c, chips = _place()
        sends = []
        for a in range(n):
            for j, chip in enumerate(chips):
                k = a * 3 + j
                sends.append(_remote(ins[a].at[2 * chip[0] + chip[1]], outs[a].at[j],
                                     send_sems.at[k], recv_sems.at[k], (*chip, c)))
        for cp in sends:
            cp.start()
        for cp in sends:
            cp.wait_recv()
        for cp in sends:
            cp.wait_send()

    return pl.pallas_call(
        body, name="exchange_shards",
        in_specs=[ANY] * n, out_specs=tuple([ANY] * n),
        out_shape=tuple(SDS((3,) + p.shape[1:], p.dtype) for p in parts),
        scratch_shapes=[pltpu.SemaphoreType.DMA((3 * n,)), pltpu.SemaphoreType.DMA((3 * n,))],
        compiler_params=_cp(has_side_effects=True))(*parts)


def _shard_sum(t, q, ids):
    _, h, cols = t.shape
    tr = min(h, 256)
    nb = h // tr

    def body(ids_ref, t_ref, q_ref, o_ref):
        o_ref[...] = ((t_ref[...] + q_ref[0].astype(F32)) + q_ref[1].astype(F32)) + q_ref[2].astype(F32)

    return pl.pallas_call(
        body, name="shard_sum",
        grid_spec=pltpu.PrefetchScalarGridSpec(
            num_scalar_prefetch=1, grid=(nb,),
            in_specs=[pl.BlockSpec((None, tr, cols), lambda i, ids_ref: (ids_ref[3], i, 0)),
                      pl.BlockSpec((3, tr, cols), lambda i, ids_ref: (0, i, 0))],
            out_specs=pl.BlockSpec((tr, cols), lambda i, ids_ref: (ids_ref[2] * nb + i, 0))),
        out_shape=SDS((2 * h, cols), t.dtype),
        compiler_params=_cp(("parallel",), 40))(ids, t, q)


def _join_halves(bufs):
    n = len(bufs)

    def body(*refs):
        outs = refs[n:2 * n]
        send_sems, recv_sems = refs[2 * n:]
        x, y, c, _ = _place()
        sibling = (x, y, 1 - c)

        def win(ref, which):
            h = ref.shape[0] // 2
            return ref.at[pl.ds(which * h, h), :]

        sends = [_remote(win(outs[a], c), win(outs[a], c), send_sems.at[a], recv_sems.at[a], sibling)
                 for a in range(n)]
        for cp in sends:
            cp.start()
        for a in range(n):
            other = win(outs[a], 1 - c)
            _remote(other, other, send_sems.at[a], recv_sems.at[a], sibling).wait_recv()
        for cp in sends:
            cp.wait_send()

    return pl.pallas_call(
        body, name="join_halves",
        in_specs=[ANY] * n, out_specs=tuple([ANY] * n),
        out_shape=tuple(SDS(b.shape, b.dtype) for b in bufs),
        input_output_aliases={a: a for a in range(n)},
        scratch_shapes=[pltpu.SemaphoreType.DMA((n,)), pltpu.SemaphoreType.DMA((n,))],
        compiler_params=_cp(has_side_effects=True))(*bufs)


def _all_reduce_small(block):
    rows, cols = block.shape
    n_dev = 8

    def body(x_ref, o_ref, buf, send_sems, recv_sems, local_sem):
        x, y, c, chips = _place()
        me, sibling = (x, y, c), (x, y, 1 - c)

        def slot(px_, py_, pc_):
            return buf.at[4 * px_ + 2 * py_ + pc_]

        def copy(k, who, to, src=None):
            return _remote(slot(*who) if src is None else src, slot(*who), send_sems.at[k], recv_sems.at[k], to)

        mine = pltpu.make_async_copy(x_ref, slot(*me), local_sem)
        mine.start()
        first = [copy(0, me, sibling, src=x_ref)]
        first += [copy(1 + j, me, (*chip, c), src=x_ref) for j, chip in enumerate(chips)]
        for cp in first:
            cp.start()
        passed = [copy(4 + j, (*chip, c), sibling) for j, chip in enumerate(chips)]
        for j, chip in enumerate(chips):
            copy(1 + j, (*chip, c), me).wait_recv()
            passed[j].start()
        copy(0, sibling, me).wait_recv()
        for j, chip in enumerate(chips):
            copy(4 + j, (*chip, 1 - c), me).wait_recv()
        for cp in first + passed:
            cp.wait_send()
        mine.wait()
        acc = buf[0]
        for s in range(1, n_dev):
            acc = acc + buf[s]
        o_ref[...] = acc

    return pl.pallas_call(
        body, name="all_reduce_small",
        in_specs=[pl.BlockSpec(memory_space=pltpu.VMEM)],
        out_specs=pl.BlockSpec(memory_space=pltpu.VMEM),
        out_shape=SDS((rows, cols), F32),
        scratch_shapes=[pltpu.VMEM((n_dev, rows, cols), F32), pltpu.SemaphoreType.DMA((7,)),
                        pltpu.SemaphoreType.DMA((7,)), pltpu.SemaphoreType.DMA],
        compiler_params=_cp(has_side_effects=True))(block)


def _adam_math(w, g, m, v):
    m = ADAM_B1 * m + (1.0 - ADAM_B1) * g
    v = ADAM_B2 * v + (1.0 - ADAM_B2) * (g * g)
    m_hat = m / (1.0 - ADAM_B1 ** ADAM_STEP)
    v_hat = v / (1.0 - ADAM_B2 ** ADAM_STEP)
    delta = -ADAM_LR * (m_hat / (jnp.sqrt(v_hat) + ADAM_EPS) + ADAM_WD * w)
    return delta, m, v


def _adamw(w, g, m, v):
    rows, cols = w.shape
    tr = min(rows, 256)

    def body(w_ref, g_ref, m_ref, v_ref, d_ref, nm_ref, nv_ref):
        d_ref[...], nm_ref[...], nv_ref[...] = _adam_math(w_ref[...], g_ref[...], m_ref[...], v_ref[...])

    spec = pl.BlockSpec((tr, cols), lambda i: (i, 0))
    return pl.pallas_call(
        body, name="adamw", grid=(rows // tr,), in_specs=[spec] * 4, out_specs=(spec,) * 3,
        out_shape=(SDS(w.shape, F32),) * 3, compiler_params=_cp(("parallel",), 40))(w, g, m, v)


def _adamw_small(w, g, m, v):
    def body(w_ref, g_ref, m_ref, v_ref, go_ref, d_ref, nm_ref, nv_ref):
        w = w_ref[...]
        g = g_ref[...]
        sub = lax.broadcasted_iota(jnp.int32, w.shape, 0)
        lane = lax.broadcasted_iota(jnp.int32, w.shape, 1)
        is_ret = jnp.logical_and(sub == 5, lane < 2 * RET_HEADS)
        u = jnp.exp(jnp.where(is_ret, w, -1.0) * jnp.log(2.0))
        g = jnp.where(is_ret, g * (-u * jnp.log(2.0) / (1.0 - u)), g)
        go_ref[...] = g
        d_ref[...], nm_ref[...], nv_ref[...] = _adam_math(w, g, m_ref[...], v_ref[...])

    return pl.pallas_call(body, name="adamw_small", out_shape=(SDS(w.shape, F32),) * 4)(w, g, m, v)


def _rope_tables(seq, n_samp, n_ctx_rows):
    rows = seq // GRID_W
    row = jnp.repeat(jnp.arange(rows, dtype=F32), GRID_W)
    col = jnp.tile(jnp.arange(GRID_W, dtype=F32), rows)
    half = ATT_HEAD_DIM // 2
    freqs = ROPE_THETA ** (-jnp.arange(0, half, 2, dtype=F32) / half)
    ang = jnp.concatenate([row[:, None] * freqs, col[:, None] * freqs], axis=-1)
    cos, sin = jnp.cos(ang), jnp.sin(ang)
    cos_f = jnp.repeat(cos, 2, axis=1)
    sin_s = jnp.stack([-sin, sin], axis=-1).reshape(seq, ATT_HEAD_DIM)
    cos_all = jnp.concatenate([jnp.tile(cos_f, (n_samp, 1)), jnp.ones((n_ctx_rows, ATT_HEAD_DIM), F32)], axis=0)
    sin_all = jnp.concatenate([jnp.tile(sin_s, (n_samp, 1)), jnp.zeros((n_ctx_rows, ATT_HEAD_DIM), F32)], axis=0)
    return cos_all, sin_all


def _pack_small(c_ctx, norm_w, b_ada, ret, qn, kn):
    d = D_MODEL
    row5 = jnp.concatenate([ret.reshape(-1), jnp.zeros((128 - 2 * RET_HEADS,), F32), qn.reshape(-1), kn.reshape(-1),
                            jnp.zeros((d - 384,), F32)])
    return jnp.concatenate([c_ctx.reshape(1, d), norm_w.reshape(1, d), b_ada.reshape(3, d), row5.reshape(1, d),
                            jnp.zeros((2, d), F32)], axis=0)


def _unpack_small(p):
    d = D_MODEL
    return (p[0], p[1:2], p[2:5].reshape(1, 3 * d), p[5, :2 * RET_HEADS].reshape(1, 2, RET_HEADS),
            p[5:6, 128:256], p[5:6, 256:384])


def _local_step(x, c, ctx, c_ctx, norm_w, b_ada, ret_log2_decay, q_norm_w, k_norm_w, loss_target,
                w_ada_g, w_in_g, w_o_ret, w_o_att, w_out):
    n_samp, seq, d = x.shape
    lc = ctx.shape[1]
    t_lat, t_ctx = n_samp * seq, n_samp * lc
    assert seq % TM == 0 and t_ctx == TM and t_lat % lc == 0 and seq % GRID_W == 0
    tps = seq // TM

    x_lat = x.reshape(t_lat, d)
    x_ctx = ctx.reshape(t_ctx, d)
    cvec8 = jnp.concatenate([c, c_ctx.reshape(1, d), jnp.zeros((8 - n_samp - 1, d), F32)], axis=0)
    lg = jnp.log1p(-jnp.exp2(ret_log2_decay.reshape(2, RET_HEADS)))
    cos_all, sin_all = _rope_tables(seq, n_samp, t_ctx)

    mod8 = _adaln_fwd(cvec8, w_ada_g, b_ada)
    mod3 = mod8[:n_samp + 1]
    shift3 = mod3[:, None, 0:d]
    scale3 = mod3[:, None, d:2 * d]
    gate3 = mod3[:, None, 2 * d:3 * d]

    hx, hxt = _norm_fwd(x_lat, x_ctx, norm_w, scale3, shift3, tps, n_samp)
    px = _in_proj(hx, w_in_g)

    states0 = _ctx_state_fwd(px, lg, n_samp, t_lat, lc)
    o_f, o_b, saved = _ret_fwd(px, states0, lg, n_samp, seq)
    y_ret = _retnorm_fwd(o_f, o_b, px)

    qn = _att_prep_q(px, cos_all, sin_all, q_norm_w, t_lat)
    kn, vn = _att_prep_kv(px, cos_all, sin_all, k_norm_w)
    y_att, o_att = _att_fwd(qn, kn, vn, px, n_samp, seq, lc)

    (gx_res, dy_ret, dy_att, dpx, loss8, dgate, g_w_o_ret, g_w_o_att, g_w_out) = _merge(
        x_lat, loss_target.reshape(t_lat, d), y_ret, y_att, px, gate3, w_o_ret, w_o_att, w_out, tps)

    dpx = _att_gate_bwd(dpx, dy_att, o_att, px)
    dpx, dkl, dkc, dvl, dvc, gqw = _att_bwd(
        dpx, qn, kn, vn, px, o_att, dy_att, cos_all, sin_all, q_norm_w, n_samp, seq, lc)
    dpx, gkw = _att_kv_bwd(dpx, dkl, dkc, dvl, dvc, px, cos_all, sin_all, k_norm_w)

    do, dpx = _retnorm_bwd(dpx, dy_ret, o_f, o_b, px)
    dqf, dkf, dvf, dqb, dkb, dvb, dstates, dlg_lat = _ret_bwd(px, do, saved, lg, n_samp, seq)
    dpx = _combine_into(dpx, dqf, dqb, C_RQ, 1.0)
    dpx = _combine_into(dpx, dkf, dkb, C_RK, RET_DK ** -0.5)
    dpx = _combine_into(dpx, dvf, dvb, C_RV, 1.0)
    dpx, dlg_ctx = _ctx_state_bwd(dpx, px, dstates, lg, n_samp, t_lat, lc)
    dpx = _zero_ctx_tail(dpx, t_lat)

    g_w_in = _gw_in(hxt, dpx)
    dhx = _dhx(dpx, w_in_g)
    grad_x, dshift, dscale, g_norm_w = _norm_bwd(x_lat, x_ctx, dhx, gx_res, norm_w, scale3, tps, n_samp)

    dgate_all = jnp.concatenate([dgate, jnp.zeros((1, 1, d), F32)], axis=0)
    dmod3 = jnp.concatenate([dshift, dscale, dgate_all], axis=2).reshape(n_samp + 1, 3 * d)
    dmod8 = jnp.concatenate([dmod3, jnp.zeros((8 - n_samp - 1, 3 * d), F32)], axis=0)
    g_w_ada, g_b_ada, dc8 = _adaln_bwd(cvec8, dmod8, w_ada_g)

    g_lg = (jnp.sum(dlg_lat[:, :, 0], axis=0).reshape(2, RET_HEADS)
            + jnp.stack([jnp.sum(dlg_ctx[:, :, 0, 0], axis=0), jnp.sum(dlg_ctx[:, :, 1, 0], axis=0)], axis=0))
    small = _pack_small(dc8[n_samp], g_norm_w, g_b_ada, g_lg, jnp.sum(gqw, axis=(0, 1, 2)), gkw)
    return (loss8[0, 0], grad_x.reshape(n_samp, seq, d),
            (g_w_ada, g_w_in, g_w_o_ret, g_w_o_att, g_w_out), small)


def kernel(x, c, ctx, c_ctx, norm_w, w_ada, b_ada, w_in, ret_log2_decay, q_norm_w, k_norm_w, w_o_ret, w_o_att, w_out, loss_target, m_c_ctx, m_norm_w, m_w_ada, m_b_ada, m_w_in, m_ret_log2_decay, m_q_norm_w, m_k_norm_w, m_w_o_ret, m_w_o_att, m_w_out, v_c_ctx, v_norm_w, v_w_ada, v_b_ada, v_w_in, v_ret_log2_decay, v_q_norm_w, v_k_norm_w, v_w_o_ret, v_w_o_att, v_w_out):
    big_w = (w_ada[0], w_in[0], w_o_ret[0], w_o_att[0], w_out[0])
    big_m = (m_w_ada[0], m_w_in[0], m_w_o_ret[0], m_w_o_att[0], m_w_out[0])
    big_v = (v_w_ada[0], v_w_in[0], v_w_o_ret[0], v_w_o_att[0], v_w_out[0])

    ids = _place_ids()
    gathered = _all_gather_weights(tuple(_cast_place(w, ids) for w in big_w))
    w_ada_g, w_in_g = gathered[0], gathered[1]
    w_o_ret_f = gathered[2].reshape(-1, D_MODEL)
    w_o_att_f = gathered[3].reshape(-1, D_MODEL)
    w_out_f = gathered[4].reshape(-1, D_MODEL)

    loss_local, grad_x, big_g, small_g = _local_step(
        x, c, ctx, c_ctx, norm_w[0:1], b_ada[0:1], ret_log2_decay[0], q_norm_w[0:1], k_norm_w[0:1], loss_target,
        w_ada_g, w_in_g, w_o_ret_f, w_o_att_f, w_out_f)
    loss = lax.psum(loss_local, ("x", "y", "c"))

    g_sm = (big_g[0], big_g[1], big_g[2].reshape(N_SHARD, -1, D_MODEL), big_g[3].reshape(N_SHARD, -1, D_MODEL),
            big_g[4].reshape(N_SHARD, -1, D_MODEL))
    from_sibling = _swap_halves(g_sm)
    chip_sums = tuple(_chip_sum(g, p, ids) for g, p in zip(g_sm, from_sibling))
    from_chips = _exchange_shards(tuple(t16 for _, t16 in chip_sums))
    big_grad = _join_halves(tuple(_shard_sum(t, q, ids) for (t, _), q in zip(chip_sums, from_chips)))

    small_grad_in = _all_reduce_small(small_g)
    small_w = _pack_small(c_ctx, norm_w, b_ada, ret_log2_decay, q_norm_w, k_norm_w)
    small_m = _pack_small(m_c_ctx, m_norm_w, m_b_ada, m_ret_log2_decay, m_q_norm_w, m_k_norm_w)
    small_v = _pack_small(v_c_ctx, v_norm_w, v_b_ada, v_ret_log2_decay, v_q_norm_w, v_k_norm_w)
    small_grad, small_delta, small_nm, small_nv = _adamw_small(small_w, small_grad_in, small_m, small_v)

    big_delta, big_nm, big_nv = [], [], []
    for w, g, m, v in zip(big_w, big_grad, big_m, big_v):
        dlt, nm, nv = _adamw(w, g, m, v)
        big_delta.append(dlt[None])
        big_nm.append(nm[None])
        big_nv.append(nv[None])
    big_grad = [g[None] for g in big_grad]

    def order(small_packed, big):
        s = _unpack_small(small_packed)
        return (s[0], s[1], big[0], s[2], big[1], s[3], s[4], s[5], big[2], big[3], big[4])

    return (loss, grad_x, *order(small_grad, big_grad), *order(small_delta, big_delta),
            *order(small_nm, big_nm), *order(small_nv, big_nv))
```

```python
import functools

import jax
import jax.numpy as jnp
from jax import lax
from jax.experimental import pallas as pl
from jax.experimental.pallas import tpu as pltpu

F32 = jnp.float32
BF = jnp.bfloat16
SDS = jax.ShapeDtypeStruct
MESH = pl.DeviceIdType.MESH
ANY = pl.BlockSpec(memory_space=pl.ANY)
SMEM = pl.BlockSpec(memory_space=pltpu.SMEM)

D_MODEL = 1024
GRID_W = 64
RET_HEADS = 4
RET_DK = 256
RET_DV = 512
RET_CHUNK = 128
ATT_HEADS = 8
ATT_KV_HEADS = 2
ATT_REP = ATT_HEADS // ATT_KV_HEADS
ATT_HEAD_DIM = 128
ROPE_THETA = 10000.0
NORM_EPS = 1e-6
IN_COLS = 10752
KV_COLS = 3584
C_RK, C_RV, C_AK, C_AV, C_RQ, C_RG, C_AQ, C_AG, C_MR, C_MA = 0, 1024, 3072, 3328, 3584, 4608, 6656, 7680, 8704, 9728
N_SHARD = 4
ADA_W = 3 * D_MODEL // N_SHARD
IN_W = IN_COLS // N_SHARD
IN_BLK = IN_W // 3
N_IN_BLK = IN_COLS // IN_BLK
TM = 512
ADAM_LR, ADAM_B1, ADAM_B2, ADAM_EPS, ADAM_WD, ADAM_STEP = 0.001, 0.9, 0.999, 1e-08, 0.01, 10
MIB = 1024 * 1024


def _cp(sem=None, vmem_mb=None, **kw):
    if sem is not None:
        kw["dimension_semantics"] = sem
    if vmem_mb is not None:
        kw["vmem_limit_bytes"] = vmem_mb * MIB
    return pltpu.CompilerParams(**kw)


def _dot(a, b, ca=1, cb=0):
    return lax.dot_general(a.astype(BF), b.astype(BF), (((ca,), (cb,)), ((), ())), preferred_element_type=F32)


def _sigmoid(x):
    return 1.0 / (1.0 + jnp.exp(-x))


def _sum_all(x):
    return jnp.sum(jnp.sum(x, axis=1, keepdims=True), axis=0, keepdims=True)


def _swap_pairs(x):
    ax = x.ndim - 1
    lane = lax.broadcasted_iota(jnp.int32, x.shape, ax)
    nxt = pltpu.roll(x, x.shape[ax] - 1, ax)
    prv = pltpu.roll(x, 1, ax)
    return jnp.where(lane % 2 == 0, nxt, prv)


def _rms(x):
    return lax.rsqrt(jnp.mean(x * x, axis=-1, keepdims=True) + NORM_EPS)


def _rms_bwd(dxh, xh, r):
    return r * (dxh - xh * jnp.mean(dxh * xh, axis=-1, keepdims=True))


def _adaln_fwd(cvec8, w_ada_g, b_ada):
    def body(c_ref, w_ref, b_ref, o_ref):
        cv = c_ref[...]
        sc = (cv * _sigmoid(cv)).astype(BF)
        for s in range(N_SHARD):
            cols = slice(s * ADA_W, (s + 1) * ADA_W)
            o_ref[:, cols] = jnp.dot(sc, w_ref[s], preferred_element_type=F32) + b_ref[:, cols]

    return pl.pallas_call(body, out_shape=SDS((8, 3 * D_MODEL), F32), name="adaln_fwd",
                          compiler_params=_cp(vmem_mb=32))(cvec8, w_ada_g, b_ada)


def _adaln_bwd(cvec8, dmod8, w_ada_g):
    def body(c_ref, d_ref, w_ref, gw_ref, gb_ref, dc_ref):
        cv = c_ref[...]
        sg = _sigmoid(cv)
        sc = cv * sg
        dm = d_ref[...]
        gb_ref[...] = jnp.sum(dm, axis=0, keepdims=True)
        dsc = jnp.zeros((8, D_MODEL), F32)
        for s in range(N_SHARD):
            cols = slice(s * ADA_W, (s + 1) * ADA_W)
            gw_ref[s] = _dot(sc, dm[:, cols], 0, 0)
            dsc = dsc + _dot(dm[:, cols], w_ref[s], 1, 1)
        dc_ref[...] = dsc * (sg * (1.0 + cv * (1.0 - sg)))

    return pl.pallas_call(
        body, name="adaln_bwd",
        out_shape=(SDS((N_SHARD, D_MODEL, ADA_W), F32), SDS((1, 3 * D_MODEL), F32), SDS((8, D_MODEL), F32)),
        compiler_params=_cp(vmem_mb=48))(cvec8, dmod8, w_ada_g)


def _big_rows(rows):
    return 1536 if rows % 1536 == 0 else TM


def _norm_fwd(x_lat, x_ctx, norm_w, scale3, shift3, tiles_per_sample, n_samp):
    n_lat = x_lat.shape[0] // TM
    rows = x_lat.shape[0] + x_ctx.shape[0]

    def samp(i):
        return jnp.minimum(i // tiles_per_sample, n_samp)

    def body(x_ref, c_ref, nw_ref, sc_ref, sh_ref, hx_ref, hxt_ref):
        x = jnp.where(pl.program_id(0) < n_lat, x_ref[...], c_ref[...])
        h = x * _rms(x) * nw_ref[...] * (1.0 + sc_ref[...]) + sh_ref[...]
        hx_ref[...] = h.astype(BF)
        hxt_ref[...] = h.T.astype(BF)

    return pl.pallas_call(
        body, name="norm_fwd", grid=(rows // TM,),
        in_specs=[pl.BlockSpec((TM, D_MODEL), lambda i: (jnp.minimum(i, n_lat - 1), 0)),
                  pl.BlockSpec((TM, D_MODEL), lambda i: (jnp.maximum(i - n_lat, 0), 0)),
                  pl.BlockSpec((1, D_MODEL), lambda i: (0, 0)),
                  pl.BlockSpec((None, 1, D_MODEL), lambda i: (samp(i), 0, 0)),
                  pl.BlockSpec((None, 1, D_MODEL), lambda i: (samp(i), 0, 0))],
        out_specs=(pl.BlockSpec((TM, D_MODEL), lambda i: (i, 0)),
                   pl.BlockSpec((D_MODEL, TM), lambda i: (0, i))),
        out_shape=(SDS((rows, D_MODEL), BF), SDS((D_MODEL, rows), BF)),
        compiler_params=_cp(("parallel",), 40))(x_lat, x_ctx, norm_w, scale3, shift3)


def _in_proj(hx, w_in_g):
    rows = hx.shape[0]
    tb = _big_rows(rows)

    def body(h_ref, w_ref, px_ref):
        px_ref[...] = jnp.dot(h_ref[...], w_ref[...], preferred_element_type=F32).astype(BF)

    return pl.pallas_call(
        body, name="in_proj", grid=(N_IN_BLK, rows // tb),
        in_specs=[pl.BlockSpec((tb, D_MODEL), lambda j, i: (i, 0)),
                  pl.BlockSpec((None, D_MODEL, IN_BLK), lambda j, i: (j // 3, 0, j % 3))],
        out_specs=pl.BlockSpec((tb, IN_BLK), lambda j, i: (i, j)),
        out_shape=SDS((rows, IN_COLS), BF),
        compiler_params=_cp(("parallel", "parallel"), 40))(hx, w_in_g)


def _norm_bwd(x_lat, x_ctx, dhx, gx_res, norm_w, scale3, tiles_per_sample, n_samp):
    rows = x_lat.shape[0] + x_ctx.shape[0]
    n_lat = tiles_per_sample * n_samp

    def samp(i):
        return jnp.minimum(i // tiles_per_sample, n_samp)

    def lat(i):
        return jnp.minimum(i, n_lat - 1)

    def body(x_ref, c_ref, dh_ref, gr_ref, nw_ref, sc_ref, gx_ref, dsh_ref, dsc_ref, dnw_ref):
        i = pl.program_id(0)
        x = jnp.where(i < n_lat, x_ref[...], c_ref[...])
        r = _rms(x)
        xh = x * r
        nw = nw_ref[...]
        dh = dh_ref[...]
        first = jnp.logical_or(i % tiles_per_sample == 0, i >= n_lat)

        @pl.when(first)
        def _():
            dsh_ref[...] = jnp.zeros_like(dsh_ref)
            dsc_ref[...] = jnp.zeros_like(dsc_ref)

        @pl.when(i == 0)
        def _():
            dnw_ref[...] = jnp.zeros_like(dnw_ref)

        dsh_ref[...] += jnp.sum(dh, axis=0, keepdims=True)
        dsc_ref[...] += jnp.sum(dh * (xh * nw), axis=0, keepdims=True)
        du = dh * (1.0 + sc_ref[...])
        dnw_ref[...] += jnp.sum(du * xh, axis=0, keepdims=True)

        @pl.when(i < n_lat)
        def _():
            gx_ref[...] = gr_ref[...] + _rms_bwd(du * nw, xh, r)

    return pl.pallas_call(
        body, name="norm_bwd", grid=(rows // TM,),
        in_specs=[pl.BlockSpec((TM, D_MODEL), lambda i: (lat(i), 0)),
                  pl.BlockSpec((TM, D_MODEL), lambda i: (jnp.maximum(i - n_lat, 0), 0)),
                  pl.BlockSpec((TM, D_MODEL), lambda i: (i, 0)),
                  pl.BlockSpec((TM, D_MODEL), lambda i: (lat(i), 0)),
                  pl.BlockSpec((1, D_MODEL), lambda i: (0, 0)),
                  pl.BlockSpec((None, 1, D_MODEL), lambda i: (samp(i), 0, 0))],
        out_specs=(pl.BlockSpec((TM, D_MODEL), lambda i: (lat(i), 0)),
                   pl.BlockSpec((None, 1, D_MODEL), lambda i: (samp(i), 0, 0)),
                   pl.BlockSpec((None, 1, D_MODEL), lambda i: (samp(i), 0, 0)),
                   pl.BlockSpec((1, D_MODEL), lambda i: (0, 0))),
        out_shape=(SDS((n_lat * TM, D_MODEL), F32), SDS((n_samp + 1, 1, D_MODEL), F32),
                   SDS((n_samp + 1, 1, D_MODEL), F32), SDS((1, D_MODEL), F32)),
        compiler_params=_cp(("arbitrary",), 40))(x_lat, x_ctx, dhx, gx_res, norm_w, scale3)


def _gw_in(hxt, dpx_all):
    rows = dpx_all.shape[0]
    tb = _big_rows(rows)

    def body(h_ref, d_ref, o_ref):
        @pl.when(pl.program_id(1) == 0)
        def _():
            o_ref[...] = jnp.zeros_like(o_ref)

        o_ref[...] += jnp.dot(h_ref[...], d_ref[...], preferred_element_type=F32)

    return pl.pallas_call(
        body, name="gw_in", grid=(N_IN_BLK, rows // tb),
        in_specs=[pl.BlockSpec((D_MODEL, tb), lambda j, i: (0, i)),
                  pl.BlockSpec((tb, IN_BLK), lambda j, i: (i, j))],
        out_specs=pl.BlockSpec((None, D_MODEL, IN_BLK), lambda j, i: (j // 3, 0, j % 3)),
        out_shape=SDS((N_SHARD, D_MODEL, IN_W), F32),
        compiler_params=_cp(("parallel", "arbitrary"), 40))(hxt, dpx_all)


def _dhx(dpx_all, w_in_g):
    rows = dpx_all.shape[0]
    tb = _big_rows(rows)

    def body(d_ref, w_ref, o_ref):
        @pl.when(pl.program_id(1) == 0)
        def _():
            o_ref[...] = jnp.zeros_like(o_ref)

        o_ref[...] += lax.dot_general(d_ref[...], w_ref[...], (((1,), (1,)), ((), ())), preferred_element_type=F32)

    return pl.pallas_call(
        body, name="dhx", grid=(rows // tb, N_IN_BLK),
        in_specs=[pl.BlockSpec((tb, IN_BLK), lambda i, j: (i, j)),
                  pl.BlockSpec((None, D_MODEL, IN_BLK), lambda i, j: (j // 3, 0, j % 3))],
        out_specs=pl.BlockSpec((tb, D_MODEL), lambda i, j: (i, 0)),
        out_shape=SDS((rows, D_MODEL), F32),
        compiler_params=_cp(("parallel", "arbitrary"), 40))(dpx_all, w_in_g)


def _decays(lgv, d):
    c = RET_CHUNK
    ii = lax.broadcasted_iota(jnp.int32, (c, 1), 0).astype(F32)
    jj = lax.broadcasted_iota(jnp.int32, (1, c), 1).astype(F32)
    a_i = jnp.where(d == 0, ii, c - 1.0 - ii)
    a_j = jnp.where(d == 0, jj, c - 1.0 - jj)
    rel = a_i - a_j
    mask = jnp.where(rel >= 0, jnp.exp(lgv * jnp.maximum(rel, 0.0)), 0.0)
    qd = jnp.exp(lgv * (a_i + 1.0))
    kd = jnp.exp(lgv * (c - 1.0 - a_i))
    gc = jnp.exp(jnp.full((1, 1), lgv * c, F32))
    return a_i, rel, mask, qd, kd, gc


def _ctx_state_fwd(px, lg, n_samp, t_lat, lc):
    rb = t_lat // lc

    def body(lg_ref, k_ref, v_ref, o_ref):
        h = pl.program_id(1)
        k = k_ref[...].astype(F32) * (RET_DK ** -0.5)
        v = v_ref[...]
        pos = lax.broadcasted_iota(jnp.int32, (lc, 1), 0).astype(F32)
        o_ref[0] = _dot(k * jnp.exp(lg_ref[0, h] * (lc - 1.0 - pos)), v, 0, 0)
        o_ref[1] = _dot(k * jnp.exp(lg_ref[1, h] * pos), v, 0, 0)

    return pl.pallas_call(
        body, name="ctx_state_fwd", grid=(n_samp, RET_HEADS),
        in_specs=[SMEM,
                  pl.BlockSpec((lc, RET_DK), lambda b, h: (rb + b, C_RK // RET_DK + h)),
                  pl.BlockSpec((lc, RET_DV), lambda b, h: (rb + b, C_RV // RET_DV + h))],
        out_specs=pl.BlockSpec((None, 2, None, RET_DK, RET_DV), lambda b, h: (b, 0, h, 0, 0)),
        out_shape=SDS((n_samp, 2, RET_HEADS, RET_DK, RET_DV), F32),
        compiler_params=_cp(("parallel", "parallel")))(lg, px, px)


def _ctx_state_bwd(dpx, px, dstates, lg, n_samp, t_lat, lc):
    rb = t_lat // lc
    kspec = pl.BlockSpec((lc, RET_DK), lambda b, h: (rb + b, C_RK // RET_DK + h))
    vspec = pl.BlockSpec((lc, RET_DV), lambda b, h: (rb + b, C_RV // RET_DV + h))
    sspec = pl.BlockSpec((None, 2, None, RET_DK, RET_DV), lambda b, h: (b, 0, h, 0, 0))

    def weights(lg_ref, h):
        pos = lax.broadcasted_iota(jnp.int32, (lc, 1), 0).astype(F32)
        e_f = lc - 1.0 - pos
        return pos, e_f, jnp.exp(lg_ref[0, h] * e_f), jnp.exp(lg_ref[1, h] * pos)

    def k_body(lg_ref, dpx_hbm, k_ref, v_ref, ds_ref, dk_ref, dlg_ref):
        pos, e_f, w_f, w_b = weights(lg_ref, pl.program_id(1))
        k = k_ref[...].astype(F32) * (RET_DK ** -0.5)
        y_f = _dot(v_ref[...], ds_ref[0], 1, 1) * w_f
        y_b = _dot(v_ref[...], ds_ref[1], 1, 1) * w_b
        dk_ref[...] = ((y_f + y_b) * (RET_DK ** -0.5)).astype(BF)
        t_f = _sum_all(e_f * k * y_f)
        t_b = _sum_all(pos * k * y_b)
        sub = lax.broadcasted_iota(jnp.int32, (8, 128), 0)
        dlg_ref[...] = jnp.where(sub == 0, t_f, jnp.where(sub == 1, t_b, 0.0))

    def v_body(lg_ref, dpx_hbm, k_ref, ds_ref, dv_ref):
        _, _, w_f, w_b = weights(lg_ref, pl.program_id(1))
        k = k_ref[...].astype(F32) * (RET_DK ** -0.5)
        dv_ref[...] = (_dot(k * w_f, ds_ref[0]) + _dot(k * w_b, ds_ref[1])).astype(BF)

    dpx, dlg = pl.pallas_call(
        k_body, name="ctx_state_bwd_k", grid=(n_samp, RET_HEADS), input_output_aliases={1: 0},
        in_specs=[SMEM, ANY, kspec, vspec, sspec],
        out_specs=(kspec, pl.BlockSpec((None, None, 8, 128), lambda b, h: (b, h, 0, 0))),
        out_shape=(SDS(dpx.shape, dpx.dtype), SDS((n_samp, RET_HEADS, 8, 128), F32)),
        compiler_params=_cp(("parallel", "parallel")))(lg, dpx, px, px, dstates)
    dpx = pl.pallas_call(
        v_body, name="ctx_state_bwd_v", grid=(n_samp, RET_HEADS), input_output_aliases={1: 0},
        in_specs=[SMEM, ANY, kspec, sspec], out_specs=vspec, out_shape=SDS(dpx.shape, dpx.dtype),
        compiler_params=_cp(("parallel", "parallel")))(lg, dpx, px, dstates)
    return dpx, dlg


def _zero_ctx_tail(dpx, t_lat):
    wb = 512
    n_ctx = (dpx.shape[0] - t_lat) // TM

    def body(dpx_hbm, o_ref):
        o_ref[...] = jnp.zeros_like(o_ref)

    return pl.pallas_call(
        body, name="zero_ctx_tail", grid=(n_ctx, (IN_COLS - KV_COLS) // wb), input_output_aliases={0: 0},
        in_specs=[ANY], out_specs=pl.BlockSpec((TM, wb), lambda i, j: (t_lat // TM + i, KV_COLS // wb + j)),
        out_shape=SDS(dpx.shape, dpx.dtype),
        compiler_params=_cp(("parallel", "parallel")))(dpx)


def _ret_specs(row_f, row_b):
    c = RET_CHUNK
    wq = RET_HEADS * RET_DK // 2
    wv = RET_HEADS * RET_DV // 2
    specs = []
    for row in (row_f, row_b):
        specs += [pl.BlockSpec((c, wq), lambda b, n, row=row: (row(b, n), C_RQ // wq)),
                  pl.BlockSpec((c, wq), lambda b, n, row=row: (row(b, n), C_RQ // wq + 1)),
                  pl.BlockSpec((c, 2 * wq), lambda b, n, row=row: (row(b, n), C_RK // (2 * wq))),
                  pl.BlockSpec((c, wv), lambda b, n, row=row: (row(b, n), C_RV // wv)),
                  pl.BlockSpec((c, wv), lambda b, n, row=row: (row(b, n), C_RV // wv + 1))]
    return specs


def _ret_head(refs, h):
    q0, q1, k_ref, v0, v1 = refs
    hh = h % 2
    q = (q0, q1)[h // 2][:, hh * RET_DK:(hh + 1) * RET_DK].astype(F32)
    k = k_ref[:, h * RET_DK:(h + 1) * RET_DK].astype(F32) * (RET_DK ** -0.5)
    v = (v0, v1)[h // 2][:, hh * RET_DV:(hh + 1) * RET_DV]
    return q, k, v


def _ret_fwd(px, states0, lg, n_samp, seq):
    c = RET_CHUNK
    nc = seq // c
    t_lat = n_samp * seq
    wo = RET_HEADS * RET_DV

    def row_f(b, n):
        return b * nc + n

    def row_b(b, n):
        return b * nc + nc - 1 - n

    def body(lg_ref, *refs):
        ins, (s0_ref, of_ref, ob_ref, st_ref, s_s) = refs[:10], refs[10:]

        @pl.when(pl.program_id(1) == 0)
        def _():
            s_s[...] = s0_ref[...]

        for d, o_ref in ((0, of_ref), (1, ob_ref)):
            for h in range(RET_HEADS):
                _, _, mask, qd, kd, gc = _decays(lg_ref[d, h], d)
                q, k, v = _ret_head(ins[5 * d:5 * d + 5], h)
                s = s_s[d, h]
                st_ref[h, d] = s.astype(BF)
                sc = _dot(q, k, 1, 1) * mask
                o_ref[:, h * RET_DV:(h + 1) * RET_DV] = (_dot(sc, v) + _dot(q * qd, s)).astype(BF)
                s_s[d, h] = s * gc + _dot(k * kd, v, 0, 0)

    return pl.pallas_call(
        body, name="ret_fwd", grid=(n_samp, nc),
        in_specs=[SMEM] + _ret_specs(row_f, row_b) + [
            pl.BlockSpec((None, 2, RET_HEADS, RET_DK, RET_DV), lambda b, n: (b, 0, 0, 0, 0))],
        out_specs=(pl.BlockSpec((c, wo), lambda b, n: (row_f(b, n), 0)),
                   pl.BlockSpec((c, wo), lambda b, n: (row_b(b, n), 0)),
                   pl.BlockSpec((None, RET_HEADS, 2, None, RET_DK, RET_DV), lambda b, n: (b, 0, 0, n, 0, 0))),
        out_shape=(SDS((t_lat, wo), BF), SDS((t_lat, wo), BF),
                   SDS((n_samp, RET_HEADS, 2, nc, RET_DK, RET_DV), BF)),
        scratch_shapes=[pltpu.VMEM((2, RET_HEADS, RET_DK, RET_DV), F32)],
        compiler_params=_cp(("parallel", "arbitrary"), 48))(lg, *([px] * 10), states0)


def _ret_bwd(px, do, saved, lg, n_samp, seq):
    c = RET_CHUNK
    nc = seq // c
    t_lat = n_samp * seq
    wq, wo = RET_HEADS * RET_DK, RET_HEADS * RET_DV

    def row_f(b, n):
        return b * nc + nc - 1 - n

    def row_b(b, n):
        return b * nc + n

    def body(lg_ref, *refs):
        ins = refs[:10]
        (dof_ref, dob_ref, st_ref, dqf, dkf, dvf, dqb, dkb, dvb, ds0_ref, dlg_ref, ds_s, acc_s) = refs[10:]
        n = pl.program_id(1)

        @pl.when(n == 0)
        def _():
            ds_s[...] = jnp.zeros_like(ds_s)
            acc_s[...] = jnp.zeros_like(acc_s)

        for d, (do_ref, dq_ref, dk_ref, dv_ref) in enumerate(((dof_ref, dqf, dkf, dvf), (dob_ref, dqb, dkb, dvb))):
            for h in range(RET_HEADS):
                a_i, rel, mask, qd, kd, gc = _decays(lg_ref[d, h], d)
                q, k, v = _ret_head(ins[5 * d:5 * d + 5], h)
                qb, kb, vb = q.astype(BF), k.astype(BF), v.astype(BF)
                dob = do_ref[:, h * RET_DV:(h + 1) * RET_DV].astype(BF)
                sb = st_ref[h, d]
                ds = ds_s[d, h]
                dsb = ds.astype(BF)
                raw = _dot(qb, kb, 1, 1)
                sc = raw * mask
                dsc = _dot(dob, vb, 1, 1) * mask
                dscb = dsc.astype(BF)
                x = _dot(dob, sb, 1, 1)
                y = _dot(vb, dsb, 1, 1)
                qq = q * qd
                kk = k * kd
                dq_ref[:, h * RET_DK:(h + 1) * RET_DK] = (_dot(dscb, kb) + x * qd).astype(BF)
                dk_ref[:, h * RET_DK:(h + 1) * RET_DK] = (_dot(dscb, qb, 0, 0) + y * kd).astype(BF)
                dv_ref[:, h * RET_DV:(h + 1) * RET_DV] = (_dot(sc, dob, 0, 0) + _dot(kk, dsb)).astype(BF)
                t = (_sum_all(dsc * raw * rel) + _sum_all((a_i + 1.0) * qq * x)
                     + _sum_all((c - 1.0 - a_i) * kk * y) + c * gc * _sum_all(ds * sb.astype(F32)))
                acc_s[4 * d + h:4 * d + h + 1, :] += t
                ds_s[d, h] = ds * gc + _dot(qq, dob, 0, 0)

        @pl.when(n == nc - 1)
        def _():
            ds0_ref[...] = ds_s[...]
            dlg_ref[...] = acc_s[...]

    do_spec_f = pl.BlockSpec((c, wo), lambda b, n: (row_f(b, n), 0))
    do_spec_b = pl.BlockSpec((c, wo), lambda b, n: (row_b(b, n), 0))
    dq_spec_f = pl.BlockSpec((c, wq), lambda b, n: (row_f(b, n), 0))
    dq_spec_b = pl.BlockSpec((c, wq), lambda b, n: (row_b(b, n), 0))
    return pl.pallas_call(
        body, name="ret_bwd", grid=(n_samp, nc),
        in_specs=[SMEM] + _ret_specs(row_f, row_b) + [
            do_spec_f, do_spec_b,
            pl.BlockSpec((None, RET_HEADS, 2, None, RET_DK, RET_DV), lambda b, n: (b, 0, 0, nc - 1 - n, 0, 0))],
        out_specs=(dq_spec_f, dq_spec_f, do_spec_f, dq_spec_b, dq_spec_b, do_spec_b,
                   pl.BlockSpec((None, 2, RET_HEADS, RET_DK, RET_DV), lambda b, n: (b, 0, 0, 0, 0)),
                   pl.BlockSpec((None, 8, 128), lambda b, n: (b, 0, 0))),
        out_shape=(SDS((t_lat, wq), BF), SDS((t_lat, wq), BF), SDS((t_lat, wo), BF),
                   SDS((t_lat, wq), BF), SDS((t_lat, wq), BF), SDS((t_lat, wo), BF),
                   SDS((n_samp, 2, RET_HEADS, RET_DK, RET_DV), F32), SDS((n_samp, 8, 128), F32)),
        scratch_shapes=[pltpu.VMEM((2, RET_HEADS, RET_DK, RET_DV), F32), pltpu.VMEM((8, 128), F32)],
        compiler_params=_cp(("parallel", "arbitrary"), 56))(lg, *([px] * 10), do, do, saved)


def _combine_into(dpx, a, b, col0, scale):
    t_lat, width = a.shape
    wb = 512
    assert col0 % wb == 0 and width % wb == 0

    def body(dpx_hbm, a_ref, b_ref, o_ref):
        o_ref[...] = ((a_ref[...].astype(F32) + b_ref[...].astype(F32)) * scale).astype(BF)

    src = pl.BlockSpec((TM, wb), lambda i, j: (i, j))
    return pl.pallas_call(
        body, name="combine_into", grid=(t_lat // TM, width // wb), input_output_aliases={0: 0},
        in_specs=[ANY, src, src], out_specs=pl.BlockSpec((TM, wb), lambda i, j: (i, col0 // wb + j)),
        out_shape=SDS(dpx.shape, dpx.dtype),
        compiler_params=_cp(("parallel", "parallel")))(dpx, a, b)


def _retnorm_fwd(o_f, o_b, px):
    t_lat = o_f.shape[0]

    def body(of_ref, ob_ref, g_ref, y_ref):
        o = of_ref[...].astype(F32) + ob_ref[...].astype(F32)
        g = g_ref[...].astype(F32)
        y_ref[...] = (o * _rms(o) * (g * _sigmoid(g))).astype(BF)

    so = pl.BlockSpec((TM, RET_DV), lambda i, h: (i, h))
    return pl.pallas_call(
        body, name="retnorm_fwd", grid=(t_lat // TM, RET_HEADS),
        in_specs=[so, so, pl.BlockSpec((TM, RET_DV), lambda i, h: (i, C_RG // RET_DV + h))],
        out_specs=so,
        out_shape=SDS((t_lat, RET_HEADS * RET_DV), BF),
        compiler_params=_cp(("parallel", "parallel")))(o_f, o_b, px)


def _retnorm_bwd(dpx, dy, o_f, o_b, px):
    t_lat = o_f.shape[0]

    def body(dpx_hbm, dy_ref, of_ref, ob_ref, g_ref, do_ref, dg_ref):
        o = of_ref[...].astype(F32) + ob_ref[...].astype(F32)
        r = _rms(o)
        on = o * r
        g = g_ref[...].astype(F32)
        sg = _sigmoid(g)
        dy_ = dy_ref[...].astype(F32)
        dg_ref[...] = (dy_ * on * (sg * (1.0 + g * (1.0 - sg)))).astype(BF)
        do_ref[...] = _rms_bwd(dy_ * (g * sg), on, r).astype(BF)

    so = pl.BlockSpec((TM, RET_DV), lambda i, h: (i, h))
    gcol = pl.BlockSpec((TM, RET_DV), lambda i, h: (i, C_RG // RET_DV + h))
    return pl.pallas_call(
        body, name="retnorm_bwd", grid=(t_lat // TM, RET_HEADS), input_output_aliases={0: 1},
        in_specs=[ANY, so, so, so, gcol],
        out_specs=(so, gcol),
        out_shape=(SDS((t_lat, RET_HEADS * RET_DV), BF), SDS(dpx.shape, dpx.dtype)),
        compiler_params=_cp(("parallel", "parallel")))(dpx, dy, o_f, o_b, px)


def _norm_rope(x, w, cos, sin):
    xn = x * _rms(x) * w
    return xn * cos + _swap_pairs(xn) * sin


def _norm_rope_bwd(dy, x, w, cos, sin):
    dxn = dy * cos + _swap_pairs(dy * sin)
    r = _rms(x)
    xh = x * r
    return _rms_bwd(dxn * w, xh, r), jnp.sum(dxn * xh, axis=0, keepdims=True)


def _att_prep_q(px, cos_all, sin_all, qnw, t_lat):
    hd = ATT_HEAD_DIM
    wblk = ATT_REP * hd

    def body(x_ref, cos_ref, sin_ref, w_ref, o_ref):
        for r in range(ATT_REP):
            cols = slice(r * hd, (r + 1) * hd)
            qr = _norm_rope(x_ref[:, cols].astype(F32), w_ref[...], cos_ref[...], sin_ref[...])
            o_ref[:, cols] = (qr * (hd ** -0.5)).astype(BF)

    return pl.pallas_call(
        body, name="att_prep_q", grid=(t_lat // TM, ATT_KV_HEADS),
        in_specs=[pl.BlockSpec((TM, wblk), lambda i, g: (i, C_AQ // wblk + g)),
                  pl.BlockSpec((TM, hd), lambda i, g: (i, 0)),
                  pl.BlockSpec((TM, hd), lambda i, g: (i, 0)),
                  pl.BlockSpec((1, hd), lambda i, g: (0, 0))],
        out_specs=pl.BlockSpec((TM, wblk), lambda i, g: (i, g)),
        out_shape=SDS((t_lat, ATT_HEADS * hd), BF),
        compiler_params=_cp(("parallel", "parallel")))(px, cos_all, sin_all, qnw)


def _att_prep_kv(px, cos_all, sin_all, knw):
    rows = px.shape[0]
    hd = ATT_HEAD_DIM
    kvw = ATT_KV_HEADS * hd

    def body(x_ref, cos_ref, sin_ref, w_ref, k_ref, v_ref):
        for g in range(ATT_KV_HEADS):
            cols = slice(g * hd, (g + 1) * hd)
            k_ref[:, cols] = _norm_rope(x_ref[:, cols].astype(F32), w_ref[...], cos_ref[...],
                                        sin_ref[...]).astype(BF)
        v_ref[...] = x_ref[:, kvw:].astype(BF)

    return pl.pallas_call(
        body, name="att_prep_kv", grid=(rows // TM,),
        in_specs=[pl.BlockSpec((TM, 2 * kvw), lambda i: (i, C_AK // (2 * kvw))),
                  pl.BlockSpec((TM, hd), lambda i: (i, 0)),
                  pl.BlockSpec((TM, hd), lambda i: (i, 0)),
                  pl.BlockSpec((1, hd), lambda i: (0, 0))],
        out_specs=(pl.BlockSpec((TM, kvw), lambda i: (i, 0)), pl.BlockSpec((TM, kvw), lambda i: (i, 0))),
        out_shape=(SDS((rows, kvw), BF), SDS((rows, kvw), BF)),
        compiler_params=_cp(("parallel",)))(px, cos_all, sin_all, knw)


def _att_kv_bwd(dpx, dkl, dkc, dvl, dvc, px, cos_all, sin_all, knw):
    rows = px.shape[0]
    hd = ATT_HEAD_DIM
    kvw = ATT_KV_HEADS * hd
    n_lat = dkl.shape[0] // TM
    assert dkc.shape[0] == TM

    def body(dpx_hbm, dkl_ref, dkc_ref, dvl_ref, dvc_ref, x_ref, cos_ref, sin_ref, w_ref, o_ref, gw_ref):
        i = pl.program_id(0)

        @pl.when(i == 0)
        def _():
            gw_ref[...] = jnp.zeros_like(gw_ref)

        is_lat = i < n_lat
        dk = jnp.where(is_lat, dkl_ref[...], dkc_ref[...])
        dv = jnp.where(is_lat, dvl_ref[...], dvc_ref[...])
        for g in range(ATT_KV_HEADS):
            cols = slice(g * hd, (g + 1) * hd)
            dx, gw = _norm_rope_bwd(dk[:, cols], x_ref[:, cols].astype(F32), w_ref[...], cos_ref[...], sin_ref[...])
            o_ref[:, cols] = dx.astype(BF)
            gw_ref[...] += gw
        o_ref[:, kvw:] = dv.astype(BF)

    lat = pl.BlockSpec((TM, kvw), lambda i: (jnp.minimum(i, n_lat - 1), 0))
    ctx = pl.BlockSpec((TM, kvw), lambda i: (0, 0))
    kvcol = pl.BlockSpec((TM, 2 * kvw), lambda i: (i, C_AK // (2 * kvw)))
    return pl.pallas_call(
        body, name="att_kv_bwd", grid=(rows // TM,), input_output_aliases={0: 0},
        in_specs=[ANY, lat, ctx, lat, ctx, kvcol,
                  pl.BlockSpec((TM, hd), lambda i: (i, 0)),
                  pl.BlockSpec((TM, hd), lambda i: (i, 0)),
                  pl.BlockSpec((1, hd), lambda i: (0, 0))],
        out_specs=(kvcol, pl.BlockSpec((1, hd), lambda i: (0, 0))),
        out_shape=(SDS(dpx.shape, dpx.dtype), SDS((1, hd), F32)),
        compiler_params=_cp(("arbitrary",)))(dpx, dkl, dkc, dvl, dvc, px, cos_all, sin_all, knw)


def _stack_heads(ref_or_val):
    hd = ATT_HEAD_DIM
    return jnp.concatenate([ref_or_val[:, r * hd:(r + 1) * hd] for r in range(ATT_REP)], axis=0)


def _att_scores(q, kl, kc):
    sl = _dot(q, kl, 1, 1)
    sc = _dot(q, kc, 1, 1)
    m = jnp.maximum(jnp.max(sl, axis=-1, keepdims=True), jnp.max(sc, axis=-1, keepdims=True))
    el = jnp.exp(sl - m)
    ec = jnp.exp(sc - m)
    denom = jnp.sum(el, axis=-1, keepdims=True) + jnp.sum(ec, axis=-1, keepdims=True)
    return el, ec, denom


def _att_fwd(qn, kn, vn, px, n_samp, seq, lc):
    hd = ATT_HEAD_DIM
    tq = 128
    nq = seq // tq
    wblk = ATT_REP * hd
    cb = n_samp * seq // lc
    t_lat = n_samp * seq

    def body(q_ref, kl_ref, kc_ref, vl_ref, vc_ref, g_ref, y_ref, o_ref):
        for r in range(ATT_REP):
            cols = slice(r * hd, (r + 1) * hd)
            el, ec, denom = _att_scores(q_ref[:, cols], kl_ref[...], kc_ref[...])
            o = (_dot(el, vl_ref[...]) + _dot(ec, vc_ref[...])) / denom
            g = g_ref[:, cols].astype(F32)
            o_ref[:, cols] = o.astype(BF)
            y_ref[:, cols] = (o * (g * _sigmoid(g))).astype(BF)

    return pl.pallas_call(
        body, name="att_fwd", grid=(n_samp, ATT_KV_HEADS, nq),
        in_specs=[pl.BlockSpec((tq, wblk), lambda b, g, i: (b * nq + i, g)),
                  pl.BlockSpec((seq, hd), lambda b, g, i: (b, g)),
                  pl.BlockSpec((lc, hd), lambda b, g, i: (cb + b, g)),
                  pl.BlockSpec((seq, hd), lambda b, g, i: (b, g)),
                  pl.BlockSpec((lc, hd), lambda b, g, i: (cb + b, g)),
                  pl.BlockSpec((tq, wblk), lambda b, g, i: (b * nq + i, C_AG // wblk + g))],
        out_specs=(pl.BlockSpec((tq, wblk), lambda b, g, i: (b * nq + i, g)),
                   pl.BlockSpec((tq, wblk), lambda b, g, i: (b * nq + i, g))),
        out_shape=(SDS((t_lat, ATT_HEADS * hd), BF), SDS((t_lat, ATT_HEADS * hd), BF)),
        compiler_params=_cp(("parallel", "parallel", "parallel"), 48))(qn, kn, kn, vn, vn, px)


def _att_gate_bwd(dpx, dy_att, o_att, px):
    t_lat = dy_att.shape[0]
    wblk = ATT_REP * ATT_HEAD_DIM

    def body(dpx_hbm, dy_ref, o_ref, g_ref, out_ref):
        g = g_ref[...].astype(F32)
        sg = _sigmoid(g)
        out_ref[...] = (dy_ref[...].astype(F32) * o_ref[...].astype(F32) * (sg * (1.0 + g * (1.0 - sg)))).astype(BF)

    blk = pl.BlockSpec((TM, wblk), lambda i, j: (i, j))
    gcol = pl.BlockSpec((TM, wblk), lambda i, j: (i, C_AG // wblk + j))
    return pl.pallas_call(
        body, name="att_gate_bwd", grid=(t_lat // TM, ATT_KV_HEADS),
        in_specs=[ANY, blk, blk, gcol], out_specs=gcol, out_shape=SDS(dpx.shape, dpx.dtype),
        input_output_aliases={0: 0},
        compiler_params=_cp(("parallel", "parallel")))(dpx, dy_att, o_att, px)


def _att_bwd(dpx, qn, kn, vn, px, o_att, dy_att, cos_all, sin_all, qnw, n_samp, seq, lc):
    hd = ATT_HEAD_DIM
    tq = 128
    nq = seq // tq
    wblk = ATT_REP * hd
    cb = n_samp * seq // lc
    t_lat = n_samp * seq
    kvw = ATT_KV_HEADS * hd
    scale = hd ** -0.5

    def body(dpx_hbm, q_ref, kl_ref, kc_ref, vl_ref, vc_ref, g_ref, o_ref, dy_ref, x_ref, cos_ref, sin_ref, w_ref,
             dq_ref, dkl_ref, dkc_ref, dvl_ref, dvc_ref, gw_ref, akl, akc, avl, avc, aw):
        i = pl.program_id(2)

        @pl.when(i == 0)
        def _():
            akl[...] = jnp.zeros_like(akl)
            akc[...] = jnp.zeros_like(akc)
            avl[...] = jnp.zeros_like(avl)
            avc[...] = jnp.zeros_like(avc)
            aw[...] = jnp.zeros_like(aw)

        dobs, pls, pcs, dsls, dscs = [], [], [], [], []
        for r in range(ATT_REP):
            cols = slice(r * hd, (r + 1) * hd)
            g = g_ref[:, cols].astype(F32)
            sg = _sigmoid(g)
            dy = dy_ref[:, cols].astype(F32)
            do = dy * (g * sg)
            delta = jnp.sum(do * o_ref[:, cols].astype(F32), axis=-1, keepdims=True)
            el, ec, denom = _att_scores(q_ref[:, cols], kl_ref[...], kc_ref[...])
            inv = 1.0 / denom
            p_l = el * inv
            p_c = ec * inv
            dob = do.astype(BF)
            ds_l = (p_l * (_dot(dob, vl_ref[...], 1, 1) - delta)).astype(BF)
            ds_c = (p_c * (_dot(dob, vc_ref[...], 1, 1) - delta)).astype(BF)
            dq = (_dot(ds_l, kl_ref[...]) + _dot(ds_c, kc_ref[...])) * scale
            dx, gw = _norm_rope_bwd(dq, x_ref[:, cols].astype(F32), w_ref[...], cos_ref[...], sin_ref[...])
            dq_ref[:, cols] = dx.astype(BF)
            aw[...] += gw
            dobs.append(dob)
            pls.append(p_l.astype(BF))
            pcs.append(p_c.astype(BF))
            dsls.append(ds_l)
            dscs.append(ds_c)
        do4 = jnp.concatenate(dobs, axis=0)
        q4 = _stack_heads(q_ref)
        avl[...] += _dot(jnp.concatenate(pls, axis=0), do4, 0, 0)
        avc[...] += _dot(jnp.concatenate(pcs, axis=0), do4, 0, 0)
        akl[...] += _dot(jnp.concatenate(dsls, axis=0), q4, 0, 0)
        akc[...] += _dot(jnp.concatenate(dscs, axis=0), q4, 0, 0)

        @pl.when(i == nq - 1)
        def _():
            dkl_ref[...] = akl[...]
            dkc_ref[...] = akc[...]
            dvl_ref[...] = avl[...]
            dvc_ref[...] = avc[...]
            gw_ref[...] = aw[...]

    return pl.pallas_call(
        body, name="att_bwd", grid=(n_samp, ATT_KV_HEADS, nq), input_output_aliases={0: 0},
        in_specs=[ANY,
                  pl.BlockSpec((tq, wblk), lambda b, g, i: (b * nq + i, g)),
                  pl.BlockSpec((seq, hd), lambda b, g, i: (b, g)),
                  pl.BlockSpec((lc, hd), lambda b, g, i: (cb + b, g)),
                  pl.BlockSpec((seq, hd), lambda b, g, i: (b, g)),
                  pl.BlockSpec((lc, hd), lambda b, g, i: (cb + b, g)),
                  pl.BlockSpec((tq, wblk), lambda b, g, i: (b * nq + i, C_AG // wblk + g)),
                  pl.BlockSpec((tq, wblk), lambda b, g, i: (b * nq + i, g)),
                  pl.BlockSpec((tq, wblk), lambda b, g, i: (b * nq + i, g)),
                  pl.BlockSpec((tq, wblk), lambda b, g, i: (b * nq + i, C_AQ // wblk + g)),
                  pl.BlockSpec((tq, hd), lambda b, g, i: (b * nq + i, 0)),
                  pl.BlockSpec((tq, hd), lambda b, g, i: (b * nq + i, 0)),
                  pl.BlockSpec((1, hd), lambda b, g, i: (0, 0))],
        out_specs=(pl.BlockSpec((tq, wblk), lambda b, g, i: (b * nq + i, C_AQ // wblk + g)),
                   pl.BlockSpec((seq, hd), lambda b, g, i: (b, g)),
                   pl.BlockSpec((lc, hd), lambda b, g, i: (b, g)),
                   pl.BlockSpec((seq, hd), lambda b, g, i: (b, g)),
                   pl.BlockSpec((lc, hd), lambda b, g, i: (b, g)),
                   pl.BlockSpec((None, None, 1, hd), lambda b, g, i: (b, g, 0, 0))),
        out_shape=(SDS(dpx.shape, dpx.dtype),
                   SDS((t_lat, kvw), F32), SDS((n_samp * lc, kvw), F32),
                   SDS((t_lat, kvw), F32), SDS((n_samp * lc, kvw), F32),
                   SDS((n_samp, ATT_KV_HEADS, 1, hd), F32)),
        scratch_shapes=[pltpu.VMEM((seq, hd), F32), pltpu.VMEM((lc, hd), F32),
                        pltpu.VMEM((seq, hd), F32), pltpu.VMEM((lc, hd), F32), pltpu.VMEM((1, hd), F32)],
        compiler_params=_cp(("parallel", "parallel", "arbitrary"), 56))(
            dpx, qn, kn, kn, vn, vn, px, o_att, dy_att, px, cos_all, sin_all, qnw)


def _merge(x_lat, target, y_ret, y_att, px, gate3, w_o_ret, w_o_att, w_out, tiles_per_sample):
    t_lat = x_lat.shape[0]
    tm = 256
    n_t = t_lat // tm
    per = tiles_per_sample * (TM // tm)
    d = D_MODEL
    rv = RET_HEADS * RET_DV
    n_samp = gate3.shape[0] - 1

    def body(x_ref, t_ref, yr_ref, ya_ref, mr0, mr1, ma0, ma1, gt_ref, wor_ref, woa_ref, wout_ref,
             gx_ref, dyr_ref, dya_ref, dpx_hbm, loss_ref, dgt_ref, gwor_hbm, gwoa_hbm, gwout_hbm,
             aor, aoa, aout, dmg_ref, dmg_sem):
        i = pl.program_id(0)

        def dmg_copy(step):
            rows = pl.ds(pl.multiple_of(step * tm, tm), tm)
            return pltpu.make_async_copy(dmg_ref, dpx_hbm.at[rows, pl.ds(C_MR, 2 * d)], dmg_sem)

        @pl.when(i == 0)
        def _():
            aor[...] = jnp.zeros_like(aor)
            aoa[...] = jnp.zeros_like(aoa)
            aout[...] = jnp.zeros_like(aout)
            loss_ref[...] = jnp.zeros_like(loss_ref)

        @pl.when(i % per == 0)
        def _():
            dgt_ref[...] = jnp.zeros_like(dgt_ref)

        yr = yr_ref[...]
        ya = ya_ref[...]
        a = jnp.dot(yr, wor_ref[...], preferred_element_type=F32)
        b = jnp.dot(ya, woa_ref[...], preferred_element_type=F32)
        sr = _sigmoid(jnp.concatenate([mr0[...], mr1[...]], axis=1).astype(F32))
        sa = _sigmoid(jnp.concatenate([ma0[...], ma1[...]], axis=1).astype(F32))
        yb = (sr * a + sa * b).astype(BF)
        out = jnp.dot(yb, wout_ref[...], preferred_element_type=F32)
        gate = gt_ref[...]
        err = x_ref[...] + gate * out - t_ref[...]
        loss_ref[...] += 0.5 * _sum_all(err * err) * (1.0 / d)
        dy_tok = err * (1.0 / d)
        gx_ref[...] = dy_tok
        dgt_ref[...] += jnp.sum(dy_tok * out, axis=0, keepdims=True)
        dout = (dy_tok * gate).astype(BF)
        aout[...] += _dot(yb, dout, 0, 0)
        dyy = _dot(dout, wout_ref[...], 1, 1)
        da = (dyy * sr).astype(BF)
        db = (dyy * sa).astype(BF)
        @pl.when(i > 0)
        def _():
            dmg_copy(i - 1).wait()

        dmg_ref[:, :d] = (dyy * a * (sr * (1.0 - sr))).astype(BF)
        dmg_ref[:, d:] = (dyy * b * (sa * (1.0 - sa))).astype(BF)
        dmg_copy(i).start()
        aor[...] += _dot(yr, da, 0, 0)
        aoa[...] += _dot(ya, db, 0, 0)
        dyr_ref[...] = _dot(da, wor_ref[...], 1, 1).astype(BF)
        dya_ref[...] = _dot(db, woa_ref[...], 1, 1).astype(BF)

        @pl.when(i == n_t - 1)
        def _():
            dmg_copy(i).wait()
            pltpu.sync_copy(aor, gwor_hbm)
            pltpu.sync_copy(aoa, gwoa_hbm)
            pltpu.sync_copy(aout, gwout_hbm)

    half = d // 2
    return pl.pallas_call(
        body, name="merge", grid=(n_t,),
        in_specs=[pl.BlockSpec((tm, d), lambda i: (i, 0)),
                  pl.BlockSpec((tm, d), lambda i: (i, 0)),
                  pl.BlockSpec((tm, rv), lambda i: (i, 0)),
                  pl.BlockSpec((tm, d), lambda i: (i, 0)),
                  pl.BlockSpec((tm, half), lambda i: (i, C_MR // half)),
                  pl.BlockSpec((tm, half), lambda i: (i, C_MR // half + 1)),
                  pl.BlockSpec((tm, half), lambda i: (i, C_MA // half)),
                  pl.BlockSpec((tm, half), lambda i: (i, C_MA // half + 1)),
                  pl.BlockSpec((None, 1, d), lambda i: (i // per, 0, 0)),
                  pl.BlockSpec((rv, d), lambda i: (0, 0)),
                  pl.BlockSpec((d, d), lambda i: (0, 0)),
                  pl.BlockSpec((d, d), lambda i: (0, 0))],
        out_specs=(pl.BlockSpec((tm, d), lambda i: (i, 0)),
                   pl.BlockSpec((tm, rv), lambda i: (i, 0)),
                   pl.BlockSpec((tm, d), lambda i: (i, 0)),
                   ANY,
                   pl.BlockSpec((8, 128), lambda i: (0, 0)),
                   pl.BlockSpec((None, 1, d), lambda i: (i // per, 0, 0)),
                   ANY, ANY, ANY),
        out_shape=(SDS((t_lat, d), F32), SDS((t_lat, rv), BF), SDS((t_lat, d), BF),
                   SDS((px.shape[0], IN_COLS), BF),
                   SDS((8, 128), F32), SDS((n_samp, 1, d), F32),
                   SDS((rv, d), F32), SDS((d, d), F32), SDS((d, d), F32)),
        scratch_shapes=[pltpu.VMEM((rv, d), F32), pltpu.VMEM((d, d), F32), pltpu.VMEM((d, d), F32),
                        pltpu.VMEM((tm, 2 * d), BF), pltpu.SemaphoreType.DMA],
        compiler_params=_cp(("arbitrary",), 56))(
            x_lat, target, y_ret, y_att, px, px, px, px, gate3, w_o_ret, w_o_att, w_out)


def _place():
    x, y, c = lax.axis_index("x"), lax.axis_index("y"), lax.axis_index("c")
    chips = [(1 - x, y), (x, 1 - y), (1 - x, 1 - y)]
    return x, y, c, chips


def _remote(src, dst, send_sem, recv_sem, to):
    return pltpu.make_async_remote_copy(src_ref=src, dst_ref=dst, send_sem=send_sem, recv_sem=recv_sem,
                                        device_id=to, device_id_type=MESH)


def _place_ids():
    x, y, c = lax.axis_index("x"), lax.axis_index("y"), lax.axis_index("c")
    return jnp.stack([x, y, c, 2 * x + y]).astype(jnp.int32)


def _cast_place(w, ids):
    rows, cols = w.shape
    tr = min(rows, 256)

    def body(ids_ref, w_ref, o_ref):
        o_ref[...] = w_ref[...].astype(BF)

    return pl.pallas_call(
        body, name="cast_place",
        grid_spec=pltpu.PrefetchScalarGridSpec(
            num_scalar_prefetch=1, grid=(rows // tr,),
            in_specs=[pl.BlockSpec((tr, cols), lambda i, ids_ref: (i, 0))],
            out_specs=pl.BlockSpec((None, tr, cols), lambda i, ids_ref: (ids_ref[3], i, 0))),
        out_shape=SDS((N_SHARD, rows, cols), BF),
        compiler_params=_cp(("parallel",), 40))(ids, w)


def _all_gather_weights(bufs):
    n = len(bufs)

    def body(*refs):
        outs = refs[n:2 * n]
        send_sems, recv_sems = refs[2 * n:]
        x, y, c, chips = _place()
        sibling = (x, y, 1 - c)
        me = 2 * x + y

        def half(ref, s, which):
            h = ref.shape[1] // 2
            return ref.at[s, pl.ds(which * h, h), :]

        first = []
        for a in range(n):
            for j, chip in enumerate(chips):
                k = a * 3 + j
                win = half(outs[a], me, c)
                first.append(_remote(win, win, send_sems.at[k], recv_sems.at[k], (*chip, c)))
        for cp in first:
            cp.start()
        passed = []
        for a in range(n):
            for j, chip in enumerate(chips):
                k = a * 3 + j
                win = half(outs[a], 2 * chip[0] + chip[1], c)
                _remote(win, win, send_sems.at[k], recv_sems.at[k], (*chip, c)).wait_recv()
                fw = _remote(win, win, send_sems.at[3 * n + k], recv_sems.at[3 * n + k], sibling)
                fw.start()
                passed.append(fw)
        for a in range(n):
            for j, chip in enumerate(chips):
                k = a * 3 + j
                win = half(outs[a], 2 * chip[0] + chip[1], 1 - c)
                _remote(win, win, send_sems.at[3 * n + k], recv_sems.at[3 * n + k], sibling).wait_recv()
        for cp in first + passed:
            cp.wait_send()

    return pl.pallas_call(
        body, name="all_gather_weights",
        in_specs=[ANY] * n, out_specs=tuple([ANY] * n),
        out_shape=tuple(SDS(b.shape, b.dtype) for b in bufs),
        input_output_aliases={a: a for a in range(n)},
        scratch_shapes=[pltpu.SemaphoreType.DMA((6 * n,)), pltpu.SemaphoreType.DMA((6 * n,))],
        compiler_params=_cp(has_side_effects=True))(*bufs)


def _swap_halves(grads):
    n = len(grads)

    def body(*refs):
        ins, outs = refs[:n], refs[n:2 * n]
        send_sems, recv_sems = refs[2 * n:]
        x, y, c, _ = _place()
        sibling = (x, y, 1 - c)

        def half(ref, which):
            h = ref.shape[1] // 2
            return ref.at[:, pl.ds(which * h, h), :]

        sends = [_remote(half(ins[a], 1 - c), outs[a], send_sems.at[a], recv_sems.at[a], sibling)
                 for a in range(n)]
        for cp in sends:
            cp.start()
        for cp in sends:
            cp.wait_recv()
        for cp in sends:
            cp.wait_send()

    return pl.pallas_call(
        body, name="swap_halves",
        in_specs=[ANY] * n, out_specs=tuple([ANY] * n),
        out_shape=tuple(SDS((g.shape[0], g.shape[1] // 2, g.shape[2]), g.dtype) for g in grads),
        scratch_shapes=[pltpu.SemaphoreType.DMA((n,)), pltpu.SemaphoreType.DMA((n,))],
        compiler_params=_cp(has_side_effects=True))(*grads)


def _chip_sum(g, p, ids):
    n_s, rows, cols = g.shape
    h = rows // 2
    tr = min(h, 256)
    nb = h // tr

    def body(ids_ref, g_ref, p_ref, o_ref, o16_ref):
        t = g_ref[...] + p_ref[...]
        o_ref[...] = t
        o16_ref[...] = t.astype(BF)

    out_spec = pl.BlockSpec((None, tr, cols), lambda s, i, ids_ref: (s, i, 0))
    return pl.pallas_call(
        body, name="chip_sum",
        grid_spec=pltpu.PrefetchScalarGridSpec(
            num_scalar_prefetch=1, grid=(n_s, nb),
            in_specs=[pl.BlockSpec((None, tr, cols), lambda s, i, ids_ref: (s, ids_ref[2] * nb + i, 0)),
                      pl.BlockSpec((None, tr, cols), lambda s, i, ids_ref: (s, i, 0))],
            out_specs=(out_spec, out_spec)),
        out_shape=(SDS((n_s, h, cols), g.dtype), SDS((n_s, h, cols), BF)),
        compiler_params=_cp(("parallel", "parallel"), 40))(ids, g, p)


def _exchange_shards(parts):
    n = len(parts)

    def body(*refs):
        ins, outs = refs[:n], refs[n:2 * n]
        send_sems, recv_sems = refs[2 * n:]
        x, y, c, chips = _place()
        sends = []
        for a in range(n):
            for j, chip in enumerate(chips):
                k = a * 3 + j
                sends.append(_remote(ins[a].at[2 * chip[0] + chip[1]], outs[a].at[j],
                                     send_sems.at[k], recv_sems.at[k], (*chip, c)))
        for cp in sends:
            cp.start()
        for cp in sends:
            cp.wait_recv()
        for cp in sends:
            cp.wait_send()

    return pl.pallas_call(
        body, name="exchange_shards",
        in_specs=[ANY] * n, out_specs=tuple([ANY] * n),
        out_shape=tuple(SDS((3,) + p.shape[1:], p.dtype) for p in parts),
        scratch_shapes=[pltpu.SemaphoreType.DMA((3 * n,)), pltpu.SemaphoreType.DMA((3 * n,))],
        compiler_params=_cp(has_side_effects=True))(*parts)


def _shard_sum(t, q, ids):
    _, h, cols = t.shape
    tr = min(h, 256)
    nb = h // tr

    def body(ids_ref, t_ref, q_ref, o_ref):
        o_ref[...] = ((t_ref[...] + q_ref[0].astype(F32)) + q_ref[1].astype(F32)) + q_ref[2].astype(F32)

    return pl.pallas_call(
        body, name="shard_sum",
        grid_spec=pltpu.PrefetchScalarGridSpec(
            num_scalar_prefetch=1, grid=(nb,),
            in_specs=[pl.BlockSpec((None, tr, cols), lambda i, ids_ref: (ids_ref[3], i, 0)),
                      pl.BlockSpec((3, tr, cols), lambda i, ids_ref: (0, i, 0))],
            out_specs=pl.BlockSpec((tr, cols), lambda i, ids_ref: (ids_ref[2] * nb + i, 0))),
        out_shape=SDS((2 * h, cols), t.dtype),
        compiler_params=_cp(("parallel",), 40))(ids, t, q)


def _join_halves(bufs):
    n = len(bufs)

    def body(*refs):
        outs = refs[n:2 * n]
        send_sems, recv_sems = refs[2 * n:]
        x, y, c, _ = _place()
        sibling = (x, y, 1 - c)

        def win(ref, which):
            h = ref.shape[0] // 2
            return ref.at[pl.ds(which * h, h), :]

        sends = [_remote(win(outs[a], c), win(outs[a], c), send_sems.at[a], recv_sems.at[a], sibling)
                 for a in range(n)]
        for cp in sends:
            cp.start()
        for a in range(n):
            other = win(outs[a], 1 - c)
            _remote(other, other, send_sems.at[a], recv_sems.at[a], sibling).wait_recv()
        for cp in sends:
            cp.wait_send()

    return pl.pallas_call(
        body, name="join_halves",
        in_specs=[ANY] * n, out_specs=tuple([ANY] * n),
        out_shape=tuple(SDS(b.shape, b.dtype) for b in bufs),
        input_output_aliases={a: a for a in range(n)},
        scratch_shapes=[pltpu.SemaphoreType.DMA((n,)), pltpu.SemaphoreType.DMA((n,))],
        compiler_params=_cp(has_side_effects=True))(*bufs)


def _all_reduce_small(block):
    rows, cols = block.shape
    n_dev = 8

    def body(x_ref, o_ref, buf, send_sems, recv_sems, local_sem):
        x, y, c, chips = _place()
        me, sibling = (x, y, c), (x, y, 1 - c)

        def slot(px_, py_, pc_):
            return buf.at[4 * px_ + 2 * py_ + pc_]

        def copy(k, who, to, src=None):
            return _remote(slot(*who) if src is None else src, slot(*who), send_sems.at[k], recv_sems.at[k], to)

        mine = pltpu.make_async_copy(x_ref, slot(*me), local_sem)
        mine.start()
        first = [copy(0, me, sibling, src=x_ref)]
        first += [copy(1 + j, me, (*chip, c), src=x_ref) for j, chip in enumerate(chips)]
        for cp in first:
            cp.start()
        passed = [copy(4 + j, (*chip, c), sibling) for j, chip in enumerate(chips)]
        for j, chip in enumerate(chips):
            copy(1 + j, (*chip, c), me).wait_recv()
            passed[j].start()
        copy(0, sibling, me).wait_recv()
        for j, chip in enumerate(chips):
            copy(4 + j, (*chip, 1 - c), me).wait_recv()
        for cp in first + passed:
            cp.wait_send()
        mine.wait()
        acc = buf[0]
        for s in range(1, n_dev):
            acc = acc + buf[s]
        o_ref[...] = acc

    return pl.pallas_call(
        body, name="all_reduce_small",
        in_specs=[pl.BlockSpec(memory_space=pltpu.VMEM)],
        out_specs=pl.BlockSpec(memory_space=pltpu.VMEM),
        out_shape=SDS((rows, cols), F32),
        scratch_shapes=[pltpu.VMEM((n_dev, rows, cols), F32), pltpu.SemaphoreType.DMA((7,)),
                        pltpu.SemaphoreType.DMA((7,)), pltpu.SemaphoreType.DMA],
        compiler_params=_cp(has_side_effects=True))(block)


def _adam_math(w, g, m, v):
    m = ADAM_B1 * m + (1.0 - ADAM_B1) * g
    v = ADAM_B2 * v + (1.0 - ADAM_B2) * (g * g)
    m_hat = m / (1.0 - ADAM_B1 ** ADAM_STEP)
    v_hat = v / (1.0 - ADAM_B2 ** ADAM_STEP)
    delta = -ADAM_LR * (m_hat / (jnp.sqrt(v_hat) + ADAM_EPS) + ADAM_WD * w)
    return delta, m, v


def _adamw(w, g, m, v):
    rows, cols = w.shape
    tr = min(rows, 256)

    def body(w_ref, g_ref, m_ref, v_ref, d_ref, nm_ref, nv_ref):
        d_ref[...], nm_ref[...], nv_ref[...] = _adam_math(w_ref[...], g_ref[...], m_ref[...], v_ref[...])

    spec = pl.BlockSpec((tr, cols), lambda i: (i, 0))
    return pl.pallas_call(
        body, name="adamw", grid=(rows // tr,), in_specs=[spec] * 4, out_specs=(spec,) * 3,
        out_shape=(SDS(w.shape, F32),) * 3, compiler_params=_cp(("parallel",), 40))(w, g, m, v)


def _adamw_small(w, g, m, v):
    def body(w_ref, g_ref, m_ref, v_ref, go_ref, d_ref, nm_ref, nv_ref):
        w = w_ref[...]
        g = g_ref[...]
        sub = lax.broadcasted_iota(jnp.int32, w.shape, 0)
        lane = lax.broadcasted_iota(jnp.int32, w.shape, 1)
        is_ret = jnp.logical_and(sub == 5, lane < 2 * RET_HEADS)
        u = jnp.exp(jnp.where(is_ret, w, -1.0) * jnp.log(2.0))
        g = jnp.where(is_ret, g * (-u * jnp.log(2.0) / (1.0 - u)), g)
        go_ref[...] = g
        d_ref[...], nm_ref[...], nv_ref[...] = _adam_math(w, g, m_ref[...], v_ref[...])

    return pl.pallas_call(body, name="adamw_small", out_shape=(SDS(w.shape, F32),) * 4)(w, g, m, v)


def _rope_tables(seq, n_samp, n_ctx_rows):
    rows = seq // GRID_W
    row = jnp.repeat(jnp.arange(rows, dtype=F32), GRID_W)
    col = jnp.tile(jnp.arange(GRID_W, dtype=F32), rows)
    half = ATT_HEAD_DIM // 2
    freqs = ROPE_THETA ** (-jnp.arange(0, half, 2, dtype=F32) / half)
    ang = jnp.concatenate([row[:, None] * freqs, col[:, None] * freqs], axis=-1)
    cos, sin = jnp.cos(ang), jnp.sin(ang)
    cos_f = jnp.repeat(cos, 2, axis=1)
    sin_s = jnp.stack([-sin, sin], axis=-1).reshape(seq, ATT_HEAD_DIM)
    cos_all = jnp.concatenate([jnp.tile(cos_f, (n_samp, 1)), jnp.ones((n_ctx_rows, ATT_HEAD_DIM), F32)], axis=0)
    sin_all = jnp.concatenate([jnp.tile(sin_s, (n_samp, 1)), jnp.zeros((n_ctx_rows, ATT_HEAD_DIM), F32)], axis=0)
    return cos_all, sin_all


def _pack_small(c_ctx, norm_w, b_ada, ret, qn, kn):
    d = D_MODEL
    row5 = jnp.concatenate([ret.reshape(-1), jnp.zeros((128 - 2 * RET_HEADS,), F32), qn.reshape(-1), kn.reshape(-1),
                            jnp.zeros((d - 384,), F32)])
    return jnp.concatenate([c_ctx.reshape(1, d), norm_w.reshape(1, d), b_ada.reshape(3, d), row5.reshape(1, d),
                            jnp.zeros((2, d), F32)], axis=0)


def _unpack_small(p):
    d = D_MODEL
    return (p[0], p[1:2], p[2:5].reshape(1, 3 * d), p[5, :2 * RET_HEADS].reshape(1, 2, RET_HEADS),
            p[5:6, 128:256], p[5:6, 256:384])


def _local_step(x, c, ctx, c_ctx, norm_w, b_ada, ret_log2_decay, q_norm_w, k_norm_w, loss_target,
                w_ada_g, w_in_g, w_o_ret, w_o_att, w_out):
    n_samp, seq, d = x.shape
    lc = ctx.shape[1]
    t_lat, t_ctx = n_samp * seq, n_samp * lc
    assert seq % TM == 0 and t_ctx == TM and t_lat % lc == 0 and seq % GRID_W == 0
    tps = seq // TM

    x_lat = x.reshape(t_lat, d)
    x_ctx = ctx.reshape(t_ctx, d)
    cvec8 = jnp.concatenate([c, c_ctx.reshape(1, d), jnp.zeros((8 - n_samp - 1, d), F32)], axis=0)
    lg = jnp.log1p(-jnp.exp2(ret_log2_decay.reshape(2, RET_HEADS)))
    cos_all, sin_all = _rope_tables(seq, n_samp, t_ctx)

    mod8 = _adaln_fwd(cvec8, w_ada_g, b_ada)
    mod3 = mod8[:n_samp + 1]
    shift3 = mod3[:, None, 0:d]
    scale3 = mod3[:, None, d:2 * d]
    gate3 = mod3[:, None, 2 * d:3 * d]

    hx, hxt = _norm_fwd(x_lat, x_ctx, norm_w, scale3, shift3, tps, n_samp)
    px = _in_proj(hx, w_in_g)

    states0 = _ctx_state_fwd(px, lg, n_samp, t_lat, lc)
    o_f, o_b, saved = _ret_fwd(px, states0, lg, n_samp, seq)
    y_ret = _retnorm_fwd(o_f, o_b, px)

    qn = _att_prep_q(px, cos_all, sin_all, q_norm_w, t_lat)
    kn, vn = _att_prep_kv(px, cos_all, sin_all, k_norm_w)
    y_att, o_att = _att_fwd(qn, kn, vn, px, n_samp, seq, lc)

    (gx_res, dy_ret, dy_att, dpx, loss8, dgate, g_w_o_ret, g_w_o_att, g_w_out) = _merge(
        x_lat, loss_target.reshape(t_lat, d), y_ret, y_att, px, gate3, w_o_ret, w_o_att, w_out, tps)

    dpx = _att_gate_bwd(dpx, dy_att, o_att, px)
    dpx, dkl, dkc, dvl, dvc, gqw = _att_bwd(
        dpx, qn, kn, vn, px, o_att, dy_att, cos_all, sin_all, q_norm_w, n_samp, seq, lc)
    dpx, gkw = _att_kv_bwd(dpx, dkl, dkc, dvl, dvc, px, cos_all, sin_all, k_norm_w)

    do, dpx = _retnorm_bwd(dpx, dy_ret, o_f, o_b, px)
    dqf, dkf, dvf, dqb, dkb, dvb, dstates, dlg_lat = _ret_bwd(px, do, saved, lg, n_samp, seq)
    dpx = _combine_into(dpx, dqf, dqb, C_RQ, 1.0)
    dpx = _combine_into(dpx, dkf, dkb, C_RK, RET_DK ** -0.5)
    dpx = _combine_into(dpx, dvf, dvb, C_RV, 1.0)
    dpx, dlg_ctx = _ctx_state_bwd(dpx, px, dstates, lg, n_samp, t_lat, lc)
    dpx = _zero_ctx_tail(dpx, t_lat)

    g_w_in = _gw_in(hxt, dpx)
    dhx = _dhx(dpx, w_in_g)
    grad_x, dshift, dscale, g_norm_w = _norm_bwd(x_lat, x_ctx, dhx, gx_res, norm_w, scale3, tps, n_samp)

    dgate_all = jnp.concatenate([dgate, jnp.zeros((1, 1, d), F32)], axis=0)
    dmod3 = jnp.concatenate([dshift, dscale, dgate_all], axis=2).reshape(n_samp + 1, 3 * d)
    dmod8 = jnp.concatenate([dmod3, jnp.zeros((8 - n_samp - 1, 3 * d), F32)], axis=0)
    g_w_ada, g_b_ada, dc8 = _adaln_bwd(cvec8, dmod8, w_ada_g)

    g_lg = (jnp.sum(dlg_lat[:, :, 0], axis=0).reshape(2, RET_HEADS)
            + jnp.stack([jnp.sum(dlg_ctx[:, :, 0, 0], axis=0), jnp.sum(dlg_ctx[:, :, 1, 0], axis=0)], axis=0))
    small = _pack_small(dc8[n_samp], g_norm_w, g_b_ada, g_lg, jnp.sum(gqw, axis=(0, 1, 2)), gkw)
    return (loss8[0, 0], grad_x.reshape(n_samp, seq, d),
            (g_w_ada, g_w_in, g_w_o_ret, g_w_o_att, g_w_out), small)


def kernel(x, c, ctx, c_ctx, norm_w, w_ada, b_ada, w_in, ret_log2_decay, q_norm_w, k_norm_w, w_o_ret, w_o_att, w_out, loss_target, m_c_ctx, m_norm_w, m_w_ada, m_b_ada, m_w_in, m_ret_log2_decay, m_q_norm_w, m_k_norm_w, m_w_o_ret, m_w_o_att, m_w_out, v_c_ctx, v_norm_w, v_w_ada, v_b_ada, v_w_in, v_ret_log2_decay, v_q_norm_w, v_k_norm_w, v_w_o_ret, v_w_o_att, v_w_out):
    big_w = (w_ada[0], w_in[0], w_o_ret[0], w_o_att[0], w_out[0])
    big_m = (m_w_ada[0], m_w_in[0], m_w_o_ret[0], m_w_o_att[0], m_w_out[0])
    big_v = (v_w_ada[0], v_w_in[0], v_w_o_ret[0], v_w_o_att[0], v_w_out[0])

    ids = _place_ids()
    gathered = _all_gather_weights(tuple(_cast_place(w, ids) for w in big_w))
    w_ada_g, w_in_g = gathered[0], gathered[1]
    w_o_ret_f = gathered[2].reshape(-1, D_MODEL)
    w_o_att_f = gathered[3].reshape(-1, D_MODEL)
    w_out_f = gathered[4].reshape(-1, D_MODEL)

    loss_local, grad_x, big_g, small_g = _local_step(
        x, c, ctx, c_ctx, norm_w[0:1], b_ada[0:1], ret_log2_decay[0], q_norm_w[0:1], k_norm_w[0:1], loss_target,
        w_ada_g, w_in_g, w_o_ret_f, w_o_att_f, w_out_f)
    loss = lax.psum(loss_local, ("x", "y", "c"))

    g_sm = (big_g[0], big_g[1], big_g[2].reshape(N_SHARD, -1, D_MODEL), big_g[3].reshape(N_SHARD, -1, D_MODEL),
            big_g[4].reshape(N_SHARD, -1, D_MODEL))
    from_sibling = _swap_halves(g_sm)
    chip_sums = tuple(_chip_sum(g, p, ids) for g, p in zip(g_sm, from_sibling))
    from_chips = _exchange_shards(tuple(t16 for _, t16 in chip_sums))
    big_grad = _join_halves(tuple(_shard_sum(t, q, ids) for (t, _), q in zip(chip_sums, from_chips)))

    small_grad_in = _all_reduce_small(small_g)
    small_w = _pack_small(c_ctx, norm_w, b_ada, ret_log2_decay, q_norm_w, k_norm_w)
    small_m = _pack_small(m_c_ctx, m_norm_w, m_b_ada, m_ret_log2_decay, m_q_norm_w, m_k_norm_w)
    small_v = _pack_small(v_c_ctx, v_norm_w, v_b_ada, v_ret_log2_decay, v_q_norm_w, v_k_norm_w)
    small_grad, small_delta, small_nm, small_nv = _adamw_small(small_w, small_grad_in, small_m, small_v)

    big_delta, big_nm, big_nv = [], [], []
    for w, g, m, v in zip(big_w, big_grad, big_m, big_v):
        dlt, nm, nv = _adamw(w, g, m, v)
        big_delta.append(dlt[None])
        big_nm.append(nm[None])
        big_nv.append(nv[None])
    big_grad = [g[None] for g in big_grad]

    def order(small_packed, big):
        s = _unpack_small(small_packed)
        return (s[0], s[1], big[0], s[2], big[1], s[3], s[4], s[5], big[2], big[3], big[4])

    return (loss, grad_x, *order(small_grad, big_grad), *order(small_delta, big_delta),
            *order(small_nm, big_nm), *order(small_nv, big_nv))
```

```python
import functools
from typing import NamedTuple

import jax
import jax.numpy as jnp
from jax import lax
from jax.experimental import pallas as pl
from jax.experimental.pallas import tpu as pltpu

F32 = jnp.float32
BF = jnp.bfloat16
SDS = jax.ShapeDtypeStruct
MESH = pl.DeviceIdType.MESH
ANY = pl.BlockSpec(memory_space=pl.ANY)
SMEM = pl.BlockSpec(memory_space=pltpu.SMEM)

D_MODEL = 1024
GRID_W = 64
RET_HEADS = 4
RET_DK = 256
RET_DV = 512
RET_CHUNK = 128
ATT_HEADS = 8
ATT_KV_HEADS = 2
ATT_REP = ATT_HEADS // ATT_KV_HEADS
ATT_HEAD_DIM = 128
ROPE_THETA = 10000.0
NORM_EPS = 1e-6
IN_COLS = 10752
KV_COLS = 3584
C_RK, C_RV, C_AK, C_AV, C_RQ, C_RG, C_AQ, C_AG, C_MR, C_MA = 0, 1024, 3072, 3328, 3584, 4608, 6656, 7680, 8704, 9728
N_SHARD = 4
ADA_W = 3 * D_MODEL // N_SHARD
IN_W = IN_COLS // N_SHARD
IN_BLK = IN_W // 3
N_IN_BLK = IN_COLS // IN_BLK
TM = 512
ADAM_LR, ADAM_B1, ADAM_B2, ADAM_EPS, ADAM_WD, ADAM_STEP = 0.001, 0.9, 0.999, 1e-08, 0.01, 10
MIB = 1024 * 1024


def _cp(sem=None, vmem_mb=None, **kw):
    if sem is not None:
        kw["dimension_semantics"] = sem
    if vmem_mb is not None:
        kw["vmem_limit_bytes"] = vmem_mb * MIB
    return pltpu.CompilerParams(**kw)


def _dot(a, b, ca=1, cb=0):
    return lax.dot_general(a.astype(BF), b.astype(BF), (((ca,), (cb,)), ((), ())), preferred_element_type=F32)


def _sigmoid(x):
    return 1.0 / (1.0 + jnp.exp(-x))


def _sum_all(x):
    return jnp.sum(jnp.sum(x, axis=1, keepdims=True), axis=0, keepdims=True)


def _swap_pairs(x):
    ax = x.ndim - 1
    lane = lax.broadcasted_iota(jnp.int32, x.shape, ax)
    nxt = pltpu.roll(x, x.shape[ax] - 1, ax)
    prv = pltpu.roll(x, 1, ax)
    return jnp.where(lane % 2 == 0, nxt, prv)


def _rms(x):
    return lax.rsqrt(jnp.mean(x * x, axis=-1, keepdims=True) + NORM_EPS)


def _rms_bwd(dxh, xh, r):
    return r * (dxh - xh * jnp.mean(dxh * xh, axis=-1, keepdims=True))


class _Comm(NamedTuple):
    name: str
    ins: tuple
    out_shapes: tuple
    aliases: dict
    n_sems: int
    phases: tuple


def _join_comms(*comms):
    comms = [cm for cm in comms if cm is not None]
    if len(comms) <= 1:
        return comms[0] if comms else None
    offs, i_off, o_off, s_off = [], 0, 0, 0
    for cm in comms:
        offs.append((i_off, o_off, s_off))
        i_off, o_off, s_off = i_off + len(cm.ins), o_off + len(cm.out_shapes), s_off + cm.n_sems

    def phase(k):
        def run(ins, outs, ssem, rsem, base):
            sends, recvs = [], []
            for cm, (io, oo, so) in zip(comms, offs):
                if k < len(cm.phases):
                    s, r = cm.phases[k](ins[io:io + len(cm.ins)], outs[oo:oo + len(cm.out_shapes)], ssem, rsem,
                                        base + so)
                    sends += s
                    recvs += r
            return sends, recvs
        return run

    aliases = {}
    for cm, (io, oo, _) in zip(comms, offs):
        aliases.update({io + a: oo + b for a, b in cm.aliases.items()})
    return _Comm("+".join(cm.name for cm in comms), sum((cm.ins for cm in comms), ()),
                 sum((cm.out_shapes for cm in comms), ()), aliases, s_off,
                 tuple(phase(k) for k in range(max(len(cm.phases) for cm in comms))))


def _run_phases(comm, cins, couts, ssem, rsem, first_started):
    for k, phase in enumerate(comm.phases):
        sends, recvs = phase(cins, couts, ssem, rsem, 0)
        if k > 0 or not first_started:
            for cp in sends:
                cp.start()
        for cp in recvs:
            cp.wait_recv()
        for cp in sends:
            cp.wait_send()


def _call(body, args, comm=None, *, name, grid, in_specs, out_specs, out_shape, scratch_shapes=(),
          compiler_params, aliases=None, prefetch=None):
    single = not isinstance(out_shape, (tuple, list))
    out_specs_t = (out_specs,) if single else tuple(out_specs)
    out_shape_t = (out_shape,) if single else tuple(out_shape)
    n_pre = 0 if prefetch is None else 1
    n_in, n_out, n_sc = len(in_specs), len(out_specs_t), len(scratch_shapes)
    io_alias = {n_pre + a: b for a, b in (aliases or {}).items()}
    if comm is None:
        kernel_body, cin, cout, csems = body, [], [], []
    else:
        n_ci, n_co = len(comm.ins), len(comm.out_shapes)
        cin, cout = [ANY] * n_ci, [ANY] * n_co
        csems = [pltpu.SemaphoreType.DMA((comm.n_sems,)), pltpu.SemaphoreType.DMA((comm.n_sems,))]
        io_alias.update({n_pre + n_in + a: n_out + b for a, b in comm.aliases.items()})

        def kernel_body(*refs):
            pre, refs = refs[:n_pre], refs[n_pre:]
            ins, cins = refs[:n_in], refs[n_in:n_in + n_ci]
            outs = refs[n_in + n_ci:n_in + n_ci + n_out]
            couts = refs[n_in + n_ci + n_out:n_in + n_ci + n_out + n_co]
            scratch = refs[n_in + n_ci + n_out + n_co:n_in + n_ci + n_out + n_co + n_sc]
            ssem, rsem = refs[-2:]
            first = functools.reduce(jnp.logical_and, [pl.program_id(k) == 0 for k in range(len(grid))])
            last = functools.reduce(jnp.logical_and, [pl.program_id(k) == grid[k] - 1 for k in range(len(grid))])

            @pl.when(first)
            def _():
                for cp in comm.phases[0](cins, couts, ssem, rsem, 0)[0]:
                    cp.start()

            body(*pre, *ins, *outs, *scratch)

            @pl.when(last)
            def _():
                _run_phases(comm, cins, couts, ssem, rsem, True)

        name = name + "+" + comm.name

    all_in, all_out = list(in_specs) + cin, out_specs_t + tuple(cout)
    shapes = out_shape_t + (tuple(comm.out_shapes) if comm is not None else ())
    scratch = list(scratch_shapes) + csems
    if prefetch is None:
        res = pl.pallas_call(kernel_body, name=name, grid=grid, in_specs=all_in, out_specs=all_out, out_shape=shapes,
                             scratch_shapes=scratch, input_output_aliases=io_alias,
                             compiler_params=compiler_params)(*args, *(comm.ins if comm is not None else ()))
    else:
        res = pl.pallas_call(
            kernel_body, name=name, out_shape=shapes, input_output_aliases=io_alias, compiler_params=compiler_params,
            grid_spec=pltpu.PrefetchScalarGridSpec(num_scalar_prefetch=1, grid=grid, in_specs=all_in,
                                                   out_specs=all_out, scratch_shapes=scratch))(
                                                       prefetch, *args, *(comm.ins if comm is not None else ()))
    own = res[0] if single else tuple(res[:n_out])
    return own if comm is None else (own, tuple(res[n_out:]))


def _run_comm(comm):
    n_ci, n_co = len(comm.ins), len(comm.out_shapes)

    def body(*refs):
        _run_phases(comm, refs[:n_ci], refs[n_ci:n_ci + n_co], refs[-2], refs[-1], False)

    return pl.pallas_call(
        body, name=comm.name, in_specs=[ANY] * n_ci, out_specs=tuple([ANY] * n_co), out_shape=tuple(comm.out_shapes),
        input_output_aliases=dict(comm.aliases),
        scratch_shapes=[pltpu.SemaphoreType.DMA((comm.n_sems,)), pltpu.SemaphoreType.DMA((comm.n_sems,))],
        compiler_params=_cp(has_side_effects=True))(*comm.ins)


def _adaln_fwd(cvec8, w_ada_g, b_ada):
    def body(c_ref, w_ref, b_ref, o_ref):
        cv = c_ref[...]
        sc = (cv * _sigmoid(cv)).astype(BF)
        for s in range(N_SHARD):
            cols = slice(s * ADA_W, (s + 1) * ADA_W)
            o_ref[:, cols] = jnp.dot(sc, w_ref[s], preferred_element_type=F32) + b_ref[:, cols]

    return pl.pallas_call(body, out_shape=SDS((8, 3 * D_MODEL), F32), name="adaln_fwd",
                          compiler_params=_cp(vmem_mb=32))(cvec8, w_ada_g, b_ada)


def _adaln_bwd(cvec8, dmod8, w_ada_g):
    def body(c_ref, d_ref, w_ref, gw_ref, gb_ref, dc_ref):
        cv = c_ref[...]
        sg = _sigmoid(cv)
        sc = cv * sg
        dm = d_ref[...]
        gb_ref[...] = jnp.sum(dm, axis=0, keepdims=True)
        dsc = jnp.zeros((8, D_MODEL), F32)
        for s in range(N_SHARD):
            cols = slice(s * ADA_W, (s + 1) * ADA_W)
            gw_ref[s] = _dot(sc, dm[:, cols], 0, 0)
            dsc = dsc + _dot(dm[:, cols], w_ref[s], 1, 1)
        dc_ref[...] = dsc * (sg * (1.0 + cv * (1.0 - sg)))

    return pl.pallas_call(
        body, name="adaln_bwd",
        out_shape=(SDS((N_SHARD, D_MODEL, ADA_W), F32), SDS((1, 3 * D_MODEL), F32), SDS((8, D_MODEL), F32)),
        compiler_params=_cp(vmem_mb=48))(cvec8, dmod8, w_ada_g)


def _big_rows(rows):
    return 1536 if rows % 1536 == 0 else TM


def _norm_fwd(x_lat, x_ctx, norm_w, scale3, shift3, tiles_per_sample, n_samp):
    n_lat = x_lat.shape[0] // TM
    rows = x_lat.shape[0] + x_ctx.shape[0]

    def samp(i):
        return jnp.minimum(i // tiles_per_sample, n_samp)

    def body(x_ref, c_ref, nw_ref, sc_ref, sh_ref, hx_ref, hxt_ref):
        x = jnp.where(pl.program_id(0) < n_lat, x_ref[...], c_ref[...])
        h = x * _rms(x) * nw_ref[...] * (1.0 + sc_ref[...]) + sh_ref[...]
        hx_ref[...] = h.astype(BF)
        hxt_ref[...] = h.T.astype(BF)

    return pl.pallas_call(
        body, name="norm_fwd", grid=(rows // TM,),
        in_specs=[pl.BlockSpec((TM, D_MODEL), lambda i: (jnp.minimum(i, n_lat - 1), 0)),
                  pl.BlockSpec((TM, D_MODEL), lambda i: (jnp.maximum(i - n_lat, 0), 0)),
                  pl.BlockSpec((1, D_MODEL), lambda i: (0, 0)),
                  pl.BlockSpec((None, 1, D_MODEL), lambda i: (samp(i), 0, 0)),
                  pl.BlockSpec((None, 1, D_MODEL), lambda i: (samp(i), 0, 0))],
        out_specs=(pl.BlockSpec((TM, D_MODEL), lambda i: (i, 0)),
                   pl.BlockSpec((D_MODEL, TM), lambda i: (0, i))),
        out_shape=(SDS((rows, D_MODEL), BF), SDS((D_MODEL, rows), BF)),
        compiler_params=_cp(("parallel",), 40))(x_lat, x_ctx, norm_w, scale3, shift3)


def _in_proj(hx, w_in_g, ids, first, count, px=None, comm=None):
    rows = hx.shape[0]
    tb = _big_rows(rows)

    def shard(j, ids_ref):
        return ids_ref[4 + first + j // 3]

    def body(ids_ref, h_ref, w_ref, *rest):
        px_ref = rest[-1]
        px_ref[...] = jnp.dot(h_ref[...], w_ref[...], preferred_element_type=F32).astype(BF)

    args, in_specs, aliases = [hx, w_in_g], [
        pl.BlockSpec((tb, D_MODEL), lambda j, i, ids_ref: (i, 0)),
        pl.BlockSpec((None, D_MODEL, IN_BLK), lambda j, i, ids_ref: (shard(j, ids_ref), 0, j % 3))], None
    if px is not None:
        args, in_specs, aliases = args + [px], in_specs + [ANY], {2: 0}
    return _call(body, args, comm, name="in_proj", grid=(3 * count, rows // tb), in_specs=in_specs,
                 out_specs=pl.BlockSpec((tb, IN_BLK), lambda j, i, ids_ref: (i, 3 * shard(j, ids_ref) + j % 3)),
                 out_shape=SDS((rows, IN_COLS), BF), aliases=aliases, prefetch=ids,
                 compiler_params=_cp(("arbitrary", "arbitrary"), 40))


def _norm_bwd(x_lat, x_ctx, dhx, gx_res, norm_w, scale3, tiles_per_sample, n_samp, comm=None):
    rows = x_lat.shape[0] + x_ctx.shape[0]
    n_lat = tiles_per_sample * n_samp

    def samp(i):
        return jnp.minimum(i // tiles_per_sample, n_samp)

    def lat(i):
        return jnp.minimum(i, n_lat - 1)

    def body(x_ref, c_ref, dh_ref, gr_ref, nw_ref, sc_ref, gx_ref, dsh_ref, dsc_ref, dnw_ref):
        i = pl.program_id(0)
        x = jnp.where(i < n_lat, x_ref[...], c_ref[...])
        r = _rms(x)
        xh = x * r
        nw = nw_ref[...]
        dh = dh_ref[...]
        first = jnp.logical_or(i % tiles_per_sample == 0, i >= n_lat)

        @pl.when(first)
        def _():
            dsh_ref[...] = jnp.zeros_like(dsh_ref)
            dsc_ref[...] = jnp.zeros_like(dsc_ref)

        @pl.when(i == 0)
        def _():
            dnw_ref[...] = jnp.zeros_like(dnw_ref)

        dsh_ref[...] += jnp.sum(dh, axis=0, keepdims=True)
        dsc_ref[...] += jnp.sum(dh * (xh * nw), axis=0, keepdims=True)
        du = dh * (1.0 + sc_ref[...])
        dnw_ref[...] += jnp.sum(du * xh, axis=0, keepdims=True)

        @pl.when(i < n_lat)
        def _():
            gx_ref[...] = gr_ref[...] + _rms_bwd(du * nw, xh, r)

    return _call(
        body, [x_lat, x_ctx, dhx, gx_res, norm_w, scale3], comm, name="norm_bwd", grid=(rows // TM,),
        in_specs=[pl.BlockSpec((TM, D_MODEL), lambda i: (lat(i), 0)),
                  pl.BlockSpec((TM, D_MODEL), lambda i: (jnp.maximum(i - n_lat, 0), 0)),
                  pl.BlockSpec((TM, D_MODEL), lambda i: (i, 0)),
                  pl.BlockSpec((TM, D_MODEL), lambda i: (lat(i), 0)),
                  pl.BlockSpec((1, D_MODEL), lambda i: (0, 0)),
                  pl.BlockSpec((None, 1, D_MODEL), lambda i: (samp(i), 0, 0))],
        out_specs=(pl.BlockSpec((TM, D_MODEL), lambda i: (lat(i), 0)),
                   pl.BlockSpec((None, 1, D_MODEL), lambda i: (samp(i), 0, 0)),
                   pl.BlockSpec((None, 1, D_MODEL), lambda i: (samp(i), 0, 0)),
                   pl.BlockSpec((1, D_MODEL), lambda i: (0, 0))),
        out_shape=(SDS((n_lat * TM, D_MODEL), F32), SDS((n_samp + 1, 1, D_MODEL), F32),
                   SDS((n_samp + 1, 1, D_MODEL), F32), SDS((1, D_MODEL), F32)),
        compiler_params=_cp(("arbitrary",), 40))


def _gw_in(hxt, dpx_all, part, n_parts, comm=None):
    rows = dpx_all.shape[0]
    tb = _big_rows(rows)
    dp = D_MODEL // n_parts

    def body(h_ref, d_ref, o_ref):
        @pl.when(pl.program_id(1) == 0)
        def _():
            o_ref[...] = jnp.zeros_like(o_ref)

        o_ref[...] += jnp.dot(h_ref[...], d_ref[...], preferred_element_type=F32)

    return _call(body, [hxt, dpx_all], comm, name="gw_in", grid=(N_IN_BLK, rows // tb),
                 in_specs=[pl.BlockSpec((dp, tb), lambda j, i: (part, i)),
                           pl.BlockSpec((tb, IN_BLK), lambda j, i: (i, j))],
                 out_specs=pl.BlockSpec((None, dp, IN_BLK), lambda j, i: (j // 3, 0, j % 3)),
                 out_shape=SDS((N_SHARD, dp, IN_W), F32),
                 compiler_params=_cp(("arbitrary", "arbitrary"), 40))


def _dhx(dpx_all, w_in_g, tile0, n_tiles, dhx=None, comm=None):
    rows = dpx_all.shape[0]
    tb = _big_rows(rows)

    def body(d_ref, w_ref, *rest):
        o_ref = rest[-1]

        @pl.when(pl.program_id(1) == 0)
        def _():
            o_ref[...] = jnp.zeros_like(o_ref)

        o_ref[...] += lax.dot_general(d_ref[...], w_ref[...], (((1,), (1,)), ((), ())), preferred_element_type=F32)

    args, in_specs, aliases = [dpx_all, w_in_g], [
        pl.BlockSpec((tb, IN_BLK), lambda i, j: (tile0 + i, j)),
        pl.BlockSpec((None, D_MODEL, IN_BLK), lambda i, j: (j // 3, 0, j % 3))], None
    if dhx is not None:
        args, in_specs, aliases = args + [dhx], in_specs + [ANY], {2: 0}
    return _call(body, args, comm, name="dhx", grid=(n_tiles, N_IN_BLK), in_specs=in_specs,
                 out_specs=pl.BlockSpec((tb, D_MODEL), lambda i, j: (tile0 + i, 0)),
                 out_shape=SDS((rows, D_MODEL), F32), aliases=aliases,
                 compiler_params=_cp(("arbitrary", "arbitrary"), 40))


def _decays(lgv, d):
    c = RET_CHUNK
    ii = lax.broadcasted_iota(jnp.int32, (c, 1), 0).astype(F32)
    jj = lax.broadcasted_iota(jnp.int32, (1, c), 1).astype(F32)
    a_i = jnp.where(d == 0, ii, c - 1.0 - ii)
    a_j = jnp.where(d == 0, jj, c - 1.0 - jj)
    rel = a_i - a_j
    mask = jnp.where(rel >= 0, jnp.exp(lgv * jnp.maximum(rel, 0.0)), 0.0)
    qd = jnp.exp(lgv * (a_i + 1.0))
    kd = jnp.exp(lgv * (c - 1.0 - a_i))
    gc = jnp.exp(jnp.full((1, 1), lgv * c, F32))
    return a_i, rel, mask, qd, kd, gc


def _ctx_state_fwd(px, lg, n_samp, t_lat, lc):
    rb = t_lat // lc

    def body(lg_ref, k_ref, v_ref, o_ref):
        h = pl.program_id(1)
        k = k_ref[...].astype(F32) * (RET_DK ** -0.5)
        v = v_ref[...]
        pos = lax.broadcasted_iota(jnp.int32, (lc, 1), 0).astype(F32)
        o_ref[0] = _dot(k * jnp.exp(lg_ref[0, h] * (lc - 1.0 - pos)), v, 0, 0)
        o_ref[1] = _dot(k * jnp.exp(lg_ref[1, h] * pos), v, 0, 0)

    return pl.pallas_call(
        body, name="ctx_state_fwd", grid=(n_samp, RET_HEADS),
        in_specs=[SMEM,
                  pl.BlockSpec((lc, RET_DK), lambda b, h: (rb + b, C_RK // RET_DK + h)),
                  pl.BlockSpec((lc, RET_DV), lambda b, h: (rb + b, C_RV // RET_DV + h))],
        out_specs=pl.BlockSpec((None, 2, None, RET_DK, RET_DV), lambda b, h: (b, 0, h, 0, 0)),
        out_shape=SDS((n_samp, 2, RET_HEADS, RET_DK, RET_DV), F32),
        compiler_params=_cp(("parallel", "parallel")))(lg, px, px)


def _ctx_state_bwd(dpx, px, dstates, lg, n_samp, t_lat, lc):
    rb = t_lat // lc
    kspec = pl.BlockSpec((lc, RET_DK), lambda b, h: (rb + b, C_RK // RET_DK + h))
    vspec = pl.BlockSpec((lc, RET_DV), lambda b, h: (rb + b, C_RV // RET_DV + h))
    sspec = pl.BlockSpec((None, 2, None, RET_DK, RET_DV), lambda b, h: (b, 0, h, 0, 0))

    def weights(lg_ref, h):
        pos = lax.broadcasted_iota(jnp.int32, (lc, 1), 0).astype(F32)
        e_f = lc - 1.0 - pos
        return pos, e_f, jnp.exp(lg_ref[0, h] * e_f), jnp.exp(lg_ref[1, h] * pos)

    def k_body(lg_ref, dpx_hbm, k_ref, v_ref, ds_ref, dk_ref, dlg_ref):
        pos, e_f, w_f, w_b = weights(lg_ref, pl.program_id(1))
        k = k_ref[...].astype(F32) * (RET_DK ** -0.5)
        y_f = _dot(v_ref[...], ds_ref[0], 1, 1) * w_f
        y_b = _dot(v_ref[...], ds_ref[1], 1, 1) * w_b
        dk_ref[...] = ((y_f + y_b) * (RET_DK ** -0.5)).astype(BF)
        t_f = _sum_all(e_f * k * y_f)
        t_b = _sum_all(pos * k * y_b)
        sub = lax.broadcasted_iota(jnp.int32, (8, 128), 0)
        dlg_ref[...] = jnp.where(sub == 0, t_f, jnp.where(sub == 1, t_b, 0.0))

    def v_body(lg_ref, dpx_hbm, k_ref, ds_ref, dv_ref):
        _, _, w_f, w_b = weights(lg_ref, pl.program_id(1))
        k = k_ref[...].astype(F32) * (RET_DK ** -0.5)
        dv_ref[...] = (_dot(k * w_f, ds_ref[0]) + _dot(k * w_b, ds_ref[1])).astype(BF)

    dpx, dlg = pl.pallas_call(
        k_body, name="ctx_state_bwd_k", grid=(n_samp, RET_HEADS), input_output_aliases={1: 0},
        in_specs=[SMEM, ANY, kspec, vspec, sspec],
        out_specs=(kspec, pl.BlockSpec((None, None, 8, 128), lambda b, h: (b, h, 0, 0))),
        out_shape=(SDS(dpx.shape, dpx.dtype), SDS((n_samp, RET_HEADS, 8, 128), F32)),
        compiler_params=_cp(("parallel", "parallel")))(lg, dpx, px, px, dstates)
    dpx = pl.pallas_call(
        v_body, name="ctx_state_bwd_v", grid=(n_samp, RET_HEADS), input_output_aliases={1: 0},
        in_specs=[SMEM, ANY, kspec, sspec], out_specs=vspec, out_shape=SDS(dpx.shape, dpx.dtype),
        compiler_params=_cp(("parallel", "parallel")))(lg, dpx, px, dstates)
    return dpx, dlg


def _zero_ctx_tail(dpx, t_lat):
    wb = 512
    n_ctx = (dpx.shape[0] - t_lat) // TM

    def body(dpx_hbm, o_ref):
        o_ref[...] = jnp.zeros_like(o_ref)

    return pl.pallas_call(
        body, name="zero_ctx_tail", grid=(n_ctx, (IN_COLS - KV_COLS) // wb), input_output_aliases={0: 0},
        in_specs=[ANY], out_specs=pl.BlockSpec((TM, wb), lambda i, j: (t_lat // TM + i, KV_COLS // wb + j)),
        out_shape=SDS(dpx.shape, dpx.dtype),
        compiler_params=_cp(("parallel", "parallel")))(dpx)


def _ret_specs(row_f, row_b):
    c = RET_CHUNK
    wq = RET_HEADS * RET_DK // 2
    wv = RET_HEADS * RET_DV // 2
    specs = []
    for row in (row_f, row_b):
        specs += [pl.BlockSpec((c, wq), lambda b, n, row=row: (row(b, n), C_RQ // wq)),
                  pl.BlockSpec((c, wq), lambda b, n, row=row: (row(b, n), C_RQ // wq + 1)),
                  pl.BlockSpec((c, 2 * wq), lambda b, n, row=row: (row(b, n), C_RK // (2 * wq))),
                  pl.BlockSpec((c, wv), lambda b, n, row=row: (row(b, n), C_RV // wv)),
                  pl.BlockSpec((c, wv), lambda b, n, row=row: (row(b, n), C_RV // wv + 1))]
    return specs


def _ret_head(refs, h):
    q0, q1, k_ref, v0, v1 = refs
    hh = h % 2
    q = (q0, q1)[h // 2][:, hh * RET_DK:(hh + 1) * RET_DK].astype(F32)
    k = k_ref[:, h * RET_DK:(h + 1) * RET_DK].astype(F32) * (RET_DK ** -0.5)
    v = (v0, v1)[h // 2][:, hh * RET_DV:(hh + 1) * RET_DV]
    return q, k, v


def _ret_fwd(px, states0, lg, n_samp, seq, comm=None):
    c = RET_CHUNK
    nc = seq // c
    t_lat = n_samp * seq
    wo = RET_HEADS * RET_DV

    def row_f(b, n):
        return b * nc + n

    def row_b(b, n):
        return b * nc + nc - 1 - n

    def body(lg_ref, *refs):
        ins, (s0_ref, of_ref, ob_ref, st_ref, s_s) = refs[:10], refs[10:]

        @pl.when(pl.program_id(1) == 0)
        def _():
            s_s[...] = s0_ref[...]

        for d, o_ref in ((0, of_ref), (1, ob_ref)):
            for h in range(RET_HEADS):
                _, _, mask, qd, kd, gc = _decays(lg_ref[d, h], d)
                q, k, v = _ret_head(ins[5 * d:5 * d + 5], h)
                s = s_s[d, h]
                st_ref[h, d] = s.astype(BF)
                sc = _dot(q, k, 1, 1) * mask
                o_ref[:, h * RET_DV:(h + 1) * RET_DV] = (_dot(sc, v) + _dot(q * qd, s)).astype(BF)
                s_s[d, h] = s * gc + _dot(k * kd, v, 0, 0)

    return _call(
        body, [lg] + [px] * 10 + [states0], comm, name="ret_fwd", grid=(n_samp, nc),
        in_specs=[SMEM] + _ret_specs(row_f, row_b) + [
            pl.BlockSpec((None, 2, RET_HEADS, RET_DK, RET_DV), lambda b, n: (b, 0, 0, 0, 0))],
        out_specs=(pl.BlockSpec((c, wo), lambda b, n: (row_f(b, n), 0)),
                   pl.BlockSpec((c, wo), lambda b, n: (row_b(b, n), 0)),
                   pl.BlockSpec((None, RET_HEADS, 2, None, RET_DK, RET_DV), lambda b, n: (b, 0, 0, n, 0, 0))),
        out_shape=(SDS((t_lat, wo), BF), SDS((t_lat, wo), BF),
                   SDS((n_samp, RET_HEADS, 2, nc, RET_DK, RET_DV), BF)),
        scratch_shapes=[pltpu.VMEM((2, RET_HEADS, RET_DK, RET_DV), F32)],
        compiler_params=_cp(("arbitrary", "arbitrary"), 48))


def _ret_bwd(px, do, saved, lg, n_samp, seq, comm=None):
    c = RET_CHUNK
    nc = seq // c
    t_lat = n_samp * seq
    wq, wo = RET_HEADS * RET_DK, RET_HEADS * RET_DV

    def row_f(b, n):
        return b * nc + nc - 1 - n

    def row_b(b, n):
        return b * nc + n

    def body(lg_ref, *refs):
        ins = refs[:10]
        (dof_ref, dob_ref, st_ref, dqf, dkf, dvf, dqb, dkb, dvb, ds0_ref, dlg_ref, ds_s, acc_s) = refs[10:]
        n = pl.program_id(1)

        @pl.when(n == 0)
        def _():
            ds_s[...] = jnp.zeros_like(ds_s)
            acc_s[...] = jnp.zeros_like(acc_s)

        for d, (do_ref, dq_ref, dk_ref, dv_ref) in enumerate(((dof_ref, dqf, dkf, dvf), (dob_ref, dqb, dkb, dvb))):
            for h in range(RET_HEADS):
                a_i, rel, mask, qd, kd, gc = _decays(lg_ref[d, h], d)
                q, k, v = _ret_head(ins[5 * d:5 * d + 5], h)
                qb, kb, vb = q.astype(BF), k.astype(BF), v.astype(BF)
                dob = do_ref[:, h * RET_DV:(h + 1) * RET_DV].astype(BF)
                sb = st_ref[h, d]
                ds = ds_s[d, h]
                dsb = ds.astype(BF)
                raw = _dot(qb, kb, 1, 1)
                sc = raw * mask
                dsc = _dot(dob, vb, 1, 1) * mask
                dscb = dsc.astype(BF)
                x = _dot(dob, sb, 1, 1)
                y = _dot(vb, dsb, 1, 1)
                qq = q * qd
                kk = k * kd
                dq_ref[:, h * RET_DK:(h + 1) * RET_DK] = (_dot(dscb, kb) + x * qd).astype(BF)
                dk_ref[:, h * RET_DK:(h + 1) * RET_DK] = (_dot(dscb, qb, 0, 0) + y * kd).astype(BF)
                dv_ref[:, h * RET_DV:(h + 1) * RET_DV] = (_dot(sc, dob, 0, 0) + _dot(kk, dsb)).astype(BF)
                t = (_sum_all(dsc * raw * rel) + _sum_all((a_i + 1.0) * qq * x)
                     + _sum_all((c - 1.0 - a_i) * kk * y) + c * gc * _sum_all(ds * sb.astype(F32)))
                acc_s[4 * d + h:4 * d + h + 1, :] += t
                ds_s[d, h] = ds * gc + _dot(qq, dob, 0, 0)

        @pl.when(n == nc - 1)
        def _():
            ds0_ref[...] = ds_s[...]
            dlg_ref[...] = acc_s[...]

    do_spec_f = pl.BlockSpec((c, wo), lambda b, n: (row_f(b, n), 0))
    do_spec_b = pl.BlockSpec((c, wo), lambda b, n: (row_b(b, n), 0))
    dq_spec_f = pl.BlockSpec((c, wq), lambda b, n: (row_f(b, n), 0))
    dq_spec_b = pl.BlockSpec((c, wq), lambda b, n: (row_b(b, n), 0))
    return _call(
        body, [lg] + [px] * 10 + [do, do, saved], comm, name="ret_bwd", grid=(n_samp, nc),
        in_specs=[SMEM] + _ret_specs(row_f, row_b) + [
            do_spec_f, do_spec_b,
            pl.BlockSpec((None, RET_HEADS, 2, None, RET_DK, RET_DV), lambda b, n: (b, 0, 0, nc - 1 - n, 0, 0))],
        out_specs=(dq_spec_f, dq_spec_f, do_spec_f, dq_spec_b, dq_spec_b, do_spec_b,
                   pl.BlockSpec((None, 2, RET_HEADS, RET_DK, RET_DV), lambda b, n: (b, 0, 0, 0, 0)),
                   pl.BlockSpec((None, 8, 128), lambda b, n: (b, 0, 0))),
        out_shape=(SDS((t_lat, wq), BF), SDS((t_lat, wq), BF), SDS((t_lat, wo), BF),
                   SDS((t_lat, wq), BF), SDS((t_lat, wq), BF), SDS((t_lat, wo), BF),
                   SDS((n_samp, 2, RET_HEADS, RET_DK, RET_DV), F32), SDS((n_samp, 8, 128), F32)),
        scratch_shapes=[pltpu.VMEM((2, RET_HEADS, RET_DK, RET_DV), F32), pltpu.VMEM((8, 128), F32)],
        compiler_params=_cp(("arbitrary", "arbitrary"), 56))


def _combine_into(dpx, a, b, col0, scale):
    t_lat, width = a.shape
    wb = 512
    assert col0 % wb == 0 and width % wb == 0

    def body(dpx_hbm, a_ref, b_ref, o_ref):
        o_ref[...] = ((a_ref[...].astype(F32) + b_ref[...].astype(F32)) * scale).astype(BF)

    src = pl.BlockSpec((TM, wb), lambda i, j: (i, j))
    return pl.pallas_call(
        body, name="combine_into", grid=(t_lat // TM, width // wb), input_output_aliases={0: 0},
        in_specs=[ANY, src, src], out_specs=pl.BlockSpec((TM, wb), lambda i, j: (i, col0 // wb + j)),
        out_shape=SDS(dpx.shape, dpx.dtype),
        compiler_params=_cp(("parallel", "parallel")))(dpx, a, b)


def _retnorm_fwd(o_f, o_b, px):
    t_lat = o_f.shape[0]

    def body(of_ref, ob_ref, g_ref, y_ref):
        o = of_ref[...].astype(F32) + ob_ref[...].astype(F32)
        g = g_ref[...].astype(F32)
        y_ref[...] = (o * _rms(o) * (g * _sigmoid(g))).astype(BF)

    so = pl.BlockSpec((TM, RET_DV), lambda i, h: (i, h))
    return pl.pallas_call(
        body, name="retnorm_fwd", grid=(t_lat // TM, RET_HEADS),
        in_specs=[so, so, pl.BlockSpec((TM, RET_DV), lambda i, h: (i, C_RG // RET_DV + h))],
        out_specs=so,
        out_shape=SDS((t_lat, RET_HEADS * RET_DV), BF),
        compiler_params=_cp(("parallel", "parallel")))(o_f, o_b, px)


def _retnorm_bwd(dpx, dy, o_f, o_b, px):
    t_lat = o_f.shape[0]

    def body(dpx_hbm, dy_ref, of_ref, ob_ref, g_ref, do_ref, dg_ref):
        o = of_ref[...].astype(F32) + ob_ref[...].astype(F32)
        r = _rms(o)
        on = o * r
        g = g_ref[...].astype(F32)
        sg = _sigmoid(g)
        dy_ = dy_ref[...].astype(F32)
        dg_ref[...] = (dy_ * on * (sg * (1.0 + g * (1.0 - sg)))).astype(BF)
        do_ref[...] = _rms_bwd(dy_ * (g * sg), on, r).astype(BF)

    so = pl.BlockSpec((TM, RET_DV), lambda i, h: (i, h))
    gcol = pl.BlockSpec((TM, RET_DV), lambda i, h: (i, C_RG // RET_DV + h))
    return pl.pallas_call(
        body, name="retnorm_bwd", grid=(t_lat // TM, RET_HEADS), input_output_aliases={0: 1},
        in_specs=[ANY, so, so, so, gcol],
        out_specs=(so, gcol),
        out_shape=(SDS((t_lat, RET_HEADS * RET_DV), BF), SDS(dpx.shape, dpx.dtype)),
        compiler_params=_cp(("parallel", "parallel")))(dpx, dy, o_f, o_b, px)


def _norm_rope(x, w, cos, sin):
    xn = x * _rms(x) * w
    return xn * cos + _swap_pairs(xn) * sin


def _norm_rope_bwd(dy, x, w, cos, sin):
    dxn = dy * cos + _swap_pairs(dy * sin)
    r = _rms(x)
    xh = x * r
    return _rms_bwd(dxn * w, xh, r), jnp.sum(dxn * xh, axis=0, keepdims=True)


def _att_prep_q(px, cos_all, sin_all, qnw, t_lat):
    hd = ATT_HEAD_DIM
    wblk = ATT_REP * hd

    def body(x_ref, cos_ref, sin_ref, w_ref, o_ref):
        for r in range(ATT_REP):
            cols = slice(r * hd, (r + 1) * hd)
            qr = _norm_rope(x_ref[:, cols].astype(F32), w_ref[...], cos_ref[...], sin_ref[...])
            o_ref[:, cols] = (qr * (hd ** -0.5)).astype(BF)

    return pl.pallas_call(
        body, name="att_prep_q", grid=(t_lat // TM, ATT_KV_HEADS),
        in_specs=[pl.BlockSpec((TM, wblk), lambda i, g: (i, C_AQ // wblk + g)),
                  pl.BlockSpec((TM, hd), lambda i, g: (i, 0)),
                  pl.BlockSpec((TM, hd), lambda i, g: (i, 0)),
                  pl.BlockSpec((1, hd), lambda i, g: (0, 0))],
        out_specs=pl.BlockSpec((TM, wblk), lambda i, g: (i, g)),
        out_shape=SDS((t_lat, ATT_HEADS * hd), BF),
        compiler_params=_cp(("parallel", "parallel")))(px, cos_all, sin_all, qnw)


def _att_prep_kv(px, cos_all, sin_all, knw):
    rows = px.shape[0]
    hd = ATT_HEAD_DIM
    kvw = ATT_KV_HEADS * hd

    def body(x_ref, cos_ref, sin_ref, w_ref, k_ref, v_ref):
        for g in range(ATT_KV_HEADS):
            cols = slice(g * hd, (g + 1) * hd)
            k_ref[:, cols] = _norm_rope(x_ref[:, cols].astype(F32), w_ref[...], cos_ref[...],
                                        sin_ref[...]).astype(BF)
        v_ref[...] = x_ref[:, kvw:].astype(BF)

    return pl.pallas_call(
        body, name="att_prep_kv", grid=(rows // TM,),
        in_specs=[pl.BlockSpec((TM, 2 * kvw), lambda i: (i, C_AK // (2 * kvw))),
                  pl.BlockSpec((TM, hd), lambda i: (i, 0)),
                  pl.BlockSpec((TM, hd), lambda i: (i, 0)),
                  pl.BlockSpec((1, hd), lambda i: (0, 0))],
        out_specs=(pl.BlockSpec((TM, kvw), lambda i: (i, 0)), pl.BlockSpec((TM, kvw), lambda i: (i, 0))),
        out_shape=(SDS((rows, kvw), BF), SDS((rows, kvw), BF)),
        compiler_params=_cp(("parallel",)))(px, cos_all, sin_all, knw)


def _att_kv_bwd(dpx, dkl, dkc, dvl, dvc, px, cos_all, sin_all, knw):
    rows = px.shape[0]
    hd = ATT_HEAD_DIM
    kvw = ATT_KV_HEADS * hd
    n_lat = dkl.shape[0] // TM
    assert dkc.shape[0] == TM

    def body(dpx_hbm, dkl_ref, dkc_ref, dvl_ref, dvc_ref, x_ref, cos_ref, sin_ref, w_ref, o_ref, gw_ref):
        i = pl.program_id(0)

        @pl.when(i == 0)
        def _():
            gw_ref[...] = jnp.zeros_like(gw_ref)

        is_lat = i < n_lat
        dk = jnp.where(is_lat, dkl_ref[...], dkc_ref[...])
        dv = jnp.where(is_lat, dvl_ref[...], dvc_ref[...])
        for g in range(ATT_KV_HEADS):
            cols = slice(g * hd, (g + 1) * hd)
            dx, gw = _norm_rope_bwd(dk[:, cols], x_ref[:, cols].astype(F32), w_ref[...], cos_ref[...], sin_ref[...])
            o_ref[:, cols] = dx.astype(BF)
            gw_ref[...] += gw
        o_ref[:, kvw:] = dv.astype(BF)

    lat = pl.BlockSpec((TM, kvw), lambda i: (jnp.minimum(i, n_lat - 1), 0))
    ctx = pl.BlockSpec((TM, kvw), lambda i: (0, 0))
    kvcol = pl.BlockSpec((TM, 2 * kvw), lambda i: (i, C_AK // (2 * kvw)))
    return pl.pallas_call(
        body, name="att_kv_bwd", grid=(rows // TM,), input_output_aliases={0: 0},
        in_specs=[ANY, lat, ctx, lat, ctx, kvcol,
                  pl.BlockSpec((TM, hd), lambda i: (i, 0)),
                  pl.BlockSpec((TM, hd), lambda i: (i, 0)),
                  pl.BlockSpec((1, hd), lambda i: (0, 0))],
        out_specs=(kvcol, pl.BlockSpec((1, hd), lambda i: (0, 0))),
        out_shape=(SDS(dpx.shape, dpx.dtype), SDS((1, hd), F32)),
        compiler_params=_cp(("arbitrary",)))(dpx, dkl, dkc, dvl, dvc, px, cos_all, sin_all, knw)


def _stack_heads(ref_or_val):
    hd = ATT_HEAD_DIM
    return jnp.concatenate([ref_or_val[:, r * hd:(r + 1) * hd] for r in range(ATT_REP)], axis=0)


def _att_scores(q, kl, kc):
    sl = _dot(q, kl, 1, 1)
    sc = _dot(q, kc, 1, 1)
    m = jnp.maximum(jnp.max(sl, axis=-1, keepdims=True), jnp.max(sc, axis=-1, keepdims=True))
    el = jnp.exp(sl - m)
    ec = jnp.exp(sc - m)
    denom = jnp.sum(el, axis=-1, keepdims=True) + jnp.sum(ec, axis=-1, keepdims=True)
    return el, ec, denom


def _att_fwd(qn, kn, vn, px, n_samp, seq, lc):
    hd = ATT_HEAD_DIM
    tq = 128
    nq = seq // tq
    wblk = ATT_REP * hd
    cb = n_samp * seq // lc
    t_lat = n_samp * seq

    def body(q_ref, kl_ref, kc_ref, vl_ref, vc_ref, g_ref, y_ref, o_ref):
        for r in range(ATT_REP):
            cols = slice(r * hd, (r + 1) * hd)
            el, ec, denom = _att_scores(q_ref[:, cols], kl_ref[...], kc_ref[...])
            o = (_dot(el, vl_ref[...]) + _dot(ec, vc_ref[...])) / denom
            g = g_ref[:, cols].astype(F32)
            o_ref[:, cols] = o.astype(BF)
            y_ref[:, cols] = (o * (g * _sigmoid(g))).astype(BF)

    return pl.pallas_call(
        body, name="att_fwd", grid=(n_samp, ATT_KV_HEADS, nq),
        in_specs=[pl.BlockSpec((tq, wblk), lambda b, g, i: (b * nq + i, g)),
                  pl.BlockSpec((seq, hd), lambda b, g, i: (b, g)),
                  pl.BlockSpec((lc, hd), lambda b, g, i: (cb + b, g)),
                  pl.BlockSpec((seq, hd), lambda b, g, i: (b, g)),
                  pl.BlockSpec((lc, hd), lambda b, g, i: (cb + b, g)),
                  pl.BlockSpec((tq, wblk), lambda b, g, i: (b * nq + i, C_AG // wblk + g))],
        out_specs=(pl.BlockSpec((tq, wblk), lambda b, g, i: (b * nq + i, g)),
                   pl.BlockSpec((tq, wblk), lambda b, g, i: (b * nq + i, g))),
        out_shape=(SDS((t_lat, ATT_HEADS * hd), BF), SDS((t_lat, ATT_HEADS * hd), BF)),
        compiler_params=_cp(("parallel", "parallel", "parallel"), 48))(qn, kn, kn, vn, vn, px)


def _att_gate_bwd(dpx, dy_att, o_att, px):
    t_lat = dy_att.shape[0]
    wblk = ATT_REP * ATT_HEAD_DIM

    def body(dpx_hbm, dy_ref, o_ref, g_ref, out_ref):
        g = g_ref[...].astype(F32)
        sg = _sigmoid(g)
        out_ref[...] = (dy_ref[...].astype(F32) * o_ref[...].astype(F32) * (sg * (1.0 + g * (1.0 - sg)))).astype(BF)

    blk = pl.BlockSpec((TM, wblk), lambda i, j: (i, j))
    gcol = pl.BlockSpec((TM, wblk), lambda i, j: (i, C_AG // wblk + j))
    return pl.pallas_call(
        body, name="att_gate_bwd", grid=(t_lat // TM, ATT_KV_HEADS),
        in_specs=[ANY, blk, blk, gcol], out_specs=gcol, out_shape=SDS(dpx.shape, dpx.dtype),
        input_output_aliases={0: 0},
        compiler_params=_cp(("parallel", "parallel")))(dpx, dy_att, o_att, px)


def _att_bwd(dpx, qn, kn, vn, px, o_att, dy_att, cos_all, sin_all, qnw, n_samp, seq, lc, comm=None):
    hd = ATT_HEAD_DIM
    tq = 128
    nq = seq // tq
    wblk = ATT_REP * hd
    cb = n_samp * seq // lc
    t_lat = n_samp * seq
    kvw = ATT_KV_HEADS * hd
    scale = hd ** -0.5

    def body(dpx_hbm, q_ref, kl_ref, kc_ref, vl_ref, vc_ref, g_ref, o_ref, dy_ref, x_ref, cos_ref, sin_ref, w_ref,
             dq_ref, dkl_ref, dkc_ref, dvl_ref, dvc_ref, gw_ref, akl, akc, avl, avc, aw):
        i = pl.program_id(2)

        @pl.when(i == 0)
        def _():
            akl[...] = jnp.zeros_like(akl)
            akc[...] = jnp.zeros_like(akc)
            avl[...] = jnp.zeros_like(avl)
            avc[...] = jnp.zeros_like(avc)
            aw[...] = jnp.zeros_like(aw)

        dobs, pls, pcs, dsls, dscs = [], [], [], [], []
        for r in range(ATT_REP):
            cols = slice(r * hd, (r + 1) * hd)
            g = g_ref[:, cols].astype(F32)
            sg = _sigmoid(g)
            dy = dy_ref[:, cols].astype(F32)
            do = dy * (g * sg)
            delta = jnp.sum(do * o_ref[:, cols].astype(F32), axis=-1, keepdims=True)
            el, ec, denom = _att_scores(q_ref[:, cols], kl_ref[...], kc_ref[...])
            inv = 1.0 / denom
            p_l = el * inv
            p_c = ec * inv
            dob = do.astype(BF)
            ds_l = (p_l * (_dot(dob, vl_ref[...], 1, 1) - delta)).astype(BF)
            ds_c = (p_c * (_dot(dob, vc_ref[...], 1, 1) - delta)).astype(BF)
            dq = (_dot(ds_l, kl_ref[...]) + _dot(ds_c, kc_ref[...])) * scale
            dx, gw = _norm_rope_bwd(dq, x_ref[:, cols].astype(F32), w_ref[...], cos_ref[...], sin_ref[...])
            dq_ref[:, cols] = dx.astype(BF)
            aw[...] += gw
            dobs.append(dob)
            pls.append(p_l.astype(BF))
            pcs.append(p_c.astype(BF))
            dsls.append(ds_l)
            dscs.append(ds_c)
        do4 = jnp.concatenate(dobs, axis=0)
        q4 = _stack_heads(q_ref)
        avl[...] += _dot(jnp.concatenate(pls, axis=0), do4, 0, 0)
        avc[...] += _dot(jnp.concatenate(pcs, axis=0), do4, 0, 0)
        akl[...] += _dot(jnp.concatenate(dsls, axis=0), q4, 0, 0)
        akc[...] += _dot(jnp.concatenate(dscs, axis=0), q4, 0, 0)

        @pl.when(i == nq - 1)
        def _():
            dkl_ref[...] = akl[...]
            dkc_ref[...] = akc[...]
            dvl_ref[...] = avl[...]
            dvc_ref[...] = avc[...]
            gw_ref[...] = aw[...]

    return _call(
        body, [dpx, qn, kn, kn, vn, vn, px, o_att, dy_att, px, cos_all, sin_all, qnw], comm,
        name="att_bwd", grid=(n_samp, ATT_KV_HEADS, nq), aliases={0: 0},
        in_specs=[ANY,
                  pl.BlockSpec((tq, wblk), lambda b, g, i: (b * nq + i, g)),
                  pl.BlockSpec((seq, hd), lambda b, g, i: (b, g)),
                  pl.BlockSpec((lc, hd), lambda b, g, i: (cb + b, g)),
                  pl.BlockSpec((seq, hd), lambda b, g, i: (b, g)),
                  pl.BlockSpec((lc, hd), lambda b, g, i: (cb + b, g)),
                  pl.BlockSpec((tq, wblk), lambda b, g, i: (b * nq + i, C_AG // wblk + g)),
                  pl.BlockSpec((tq, wblk), lambda b, g, i: (b * nq + i, g)),
                  pl.BlockSpec((tq, wblk), lambda b, g, i: (b * nq + i, g)),
                  pl.BlockSpec((tq, wblk), lambda b, g, i: (b * nq + i, C_AQ // wblk + g)),
                  pl.BlockSpec((tq, hd), lambda b, g, i: (b * nq + i, 0)),
                  pl.BlockSpec((tq, hd), lambda b, g, i: (b * nq + i, 0)),
                  pl.BlockSpec((1, hd), lambda b, g, i: (0, 0))],
        out_specs=(pl.BlockSpec((tq, wblk), lambda b, g, i: (b * nq + i, C_AQ // wblk + g)),
                   pl.BlockSpec((seq, hd), lambda b, g, i: (b, g)),
                   pl.BlockSpec((lc, hd), lambda b, g, i: (b, g)),
                   pl.BlockSpec((seq, hd), lambda b, g, i: (b, g)),
                   pl.BlockSpec((lc, hd), lambda b, g, i: (b, g)),
                   pl.BlockSpec((None, None, 1, hd), lambda b, g, i: (b, g, 0, 0))),
        out_shape=(SDS(dpx.shape, dpx.dtype),
                   SDS((t_lat, kvw), F32), SDS((n_samp * lc, kvw), F32),
                   SDS((t_lat, kvw), F32), SDS((n_samp * lc, kvw), F32),
                   SDS((n_samp, ATT_KV_HEADS, 1, hd), F32)),
        scratch_shapes=[pltpu.VMEM((seq, hd), F32), pltpu.VMEM((lc, hd), F32),
                        pltpu.VMEM((seq, hd), F32), pltpu.VMEM((lc, hd), F32), pltpu.VMEM((1, hd), F32)],
        compiler_params=_cp(("arbitrary", "arbitrary", "arbitrary"), 56))


def _merge(x_lat, target, y_ret, y_att, px, gate3, w_o_ret, w_o_att, w_out, tiles_per_sample):
    t_lat = x_lat.shape[0]
    tm = 256
    n_t = t_lat // tm
    per = tiles_per_sample * (TM // tm)
    d = D_MODEL
    rv = RET_HEADS * RET_DV
    n_samp = gate3.shape[0] - 1

    def body(x_ref, t_ref, yr_ref, ya_ref, mr0, mr1, ma0, ma1, gt_ref, wor_ref, woa_ref, wout_ref,
             gx_ref, dyr_ref, dya_ref, dpx_hbm, loss_ref, dgt_ref, gwor_hbm, gwoa_hbm, gwout_hbm,
             aor, aoa, aout, dmg_ref, dmg_sem):
        i = pl.program_id(0)

        def dmg_copy(step):
            rows = pl.ds(pl.multiple_of(step * tm, tm), tm)
            return pltpu.make_async_copy(dmg_ref, dpx_hbm.at[rows, pl.ds(C_MR, 2 * d)], dmg_sem)

        @pl.when(i == 0)
        def _():
            aor[...] = jnp.zeros_like(aor)
            aoa[...] = jnp.zeros_like(aoa)
            aout[...] = jnp.zeros_like(aout)
            loss_ref[...] = jnp.zeros_like(loss_ref)

        @pl.when(i % per == 0)
        def _():
            dgt_ref[...] = jnp.zeros_like(dgt_ref)

        yr = yr_ref[...]
        ya = ya_ref[...]
        a = jnp.dot(yr, wor_ref[...], preferred_element_type=F32)
        b = jnp.dot(ya, woa_ref[...], preferred_element_type=F32)
        sr = _sigmoid(jnp.concatenate([mr0[...], mr1[...]], axis=1).astype(F32))
        sa = _sigmoid(jnp.concatenate([ma0[...], ma1[...]], axis=1).astype(F32))
        yb = (sr * a + sa * b).astype(BF)
        out = jnp.dot(yb, wout_ref[...], preferred_element_type=F32)
        gate = gt_ref[...]
        err = x_ref[...] + gate * out - t_ref[...]
        loss_ref[...] += 0.5 * _sum_all(err * err) * (1.0 / d)
        dy_tok = err * (1.0 / d)
        gx_ref[...] = dy_tok
        dgt_ref[...] += jnp.sum(dy_tok * out, axis=0, keepdims=True)
        dout = (dy_tok * gate).astype(BF)
        aout[...] += _dot(yb, dout, 0, 0)
        dyy = _dot(dout, wout_ref[...], 1, 1)
        da = (dyy * sr).astype(BF)
        db = (dyy * sa).astype(BF)
        @pl.when(i > 0)
        def _():
            dmg_copy(i - 1).wait()

        dmg_ref[:, :d] = (dyy * a * (sr * (1.0 - sr))).astype(BF)
        dmg_ref[:, d:] = (dyy * b * (sa * (1.0 - sa))).astype(BF)
        dmg_copy(i).start()
        aor[...] += _dot(yr, da, 0, 0)
        aoa[...] += _dot(ya, db, 0, 0)
        dyr_ref[...] = _dot(da, wor_ref[...], 1, 1).astype(BF)
        dya_ref[...] = _dot(db, woa_ref[...], 1, 1).astype(BF)

        @pl.when(i == n_t - 1)
        def _():
            dmg_copy(i).wait()
            pltpu.sync_copy(aor, gwor_hbm)
            pltpu.sync_copy(aoa, gwoa_hbm)
            pltpu.sync_copy(aout, gwout_hbm)

    half = d // 2
    return pl.pallas_call(
        body, name="merge", grid=(n_t,),
        in_specs=[pl.BlockSpec((tm, d), lambda i: (i, 0)),
                  pl.BlockSpec((tm, d), lambda i: (i, 0)),
                  pl.BlockSpec((tm, rv), lambda i: (i, 0)),
                  pl.BlockSpec((tm, d), lambda i: (i, 0)),
                  pl.BlockSpec((tm, half), lambda i: (i, C_MR // half)),
                  pl.BlockSpec((tm, half), lambda i: (i, C_MR // half + 1)),
                  pl.BlockSpec((tm, half), lambda i: (i, C_MA // half)),
                  pl.BlockSpec((tm, half), lambda i: (i, C_MA // half + 1)),
                  pl.BlockSpec((None, 1, d), lambda i: (i // per, 0, 0)),
                  pl.BlockSpec((rv, d), lambda i: (0, 0)),
                  pl.BlockSpec((d, d), lambda i: (0, 0)),
                  pl.BlockSpec((d, d), lambda i: (0, 0))],
        out_specs=(pl.BlockSpec((tm, d), lambda i: (i, 0)),
                   pl.BlockSpec((tm, rv), lambda i: (i, 0)),
                   pl.BlockSpec((tm, d), lambda i: (i, 0)),
                   ANY,
                   pl.BlockSpec((8, 128), lambda i: (0, 0)),
                   pl.BlockSpec((None, 1, d), lambda i: (i // per, 0, 0)),
                   ANY, ANY, ANY),
        out_shape=(SDS((t_lat, d), F32), SDS((t_lat, rv), BF), SDS((t_lat, d), BF),
                   SDS((px.shape[0], IN_COLS), BF),
                   SDS((8, 128), F32), SDS((n_samp, 1, d), F32),
                   SDS((rv, d), F32), SDS((d, d), F32), SDS((d, d), F32)),
        scratch_shapes=[pltpu.VMEM((rv, d), F32), pltpu.VMEM((d, d), F32), pltpu.VMEM((d, d), F32),
                        pltpu.VMEM((tm, 2 * d), BF), pltpu.SemaphoreType.DMA],
        compiler_params=_cp(("arbitrary",), 56))(
            x_lat, target, y_ret, y_att, px, px, px, px, gate3, w_o_ret, w_o_att, w_out)


def _place():
    x, y, c = lax.axis_index("x"), lax.axis_index("y"), lax.axis_index("c")
    chips = [(1 - x, y), (x, 1 - y), (1 - x, 1 - y)]
    return x, y, c, chips


def _remote(src, dst, send_sem, recv_sem, to):
    return pltpu.make_async_remote_copy(src_ref=src, dst_ref=dst, send_sem=send_sem, recv_sem=recv_sem,
                                        device_id=to, device_id_type=MESH)


def _place_ids():
    x, y, c = lax.axis_index("x"), lax.axis_index("y"), lax.axis_index("c")
    me = 2 * x + y
    return jnp.stack([x, y, c, me, me, 2 * (1 - x) + y, 2 * x + 1 - y, 2 * (1 - x) + 1 - y]).astype(jnp.int32)


def _ag_comm(bufs, rels):
    n, m = len(bufs), len(rels)

    def half(ref, s, which):
        h = ref.shape[1] // 2
        return ref.at[s, pl.ds(which * h, h), :]

    def ici(ins, outs, ssem, rsem, base):
        x, y, c, chips = _place()
        sends, recvs = [], []
        for a in range(n):
            for jj, j in enumerate(rels):
                k, chip = base + a * m + jj, chips[j]
                mine, theirs = half(outs[a], 2 * x + y, c), half(outs[a], 2 * chip[0] + chip[1], c)
                sends.append(_remote(mine, mine, ssem.at[k], rsem.at[k], (*chip, c)))
                recvs.append(_remote(theirs, theirs, ssem.at[k], rsem.at[k], (*chip, c)))
        return sends, recvs

    def d2d(ins, outs, ssem, rsem, base):
        x, y, c, chips = _place()
        sends, recvs = [], []
        for a in range(n):
            for jj, j in enumerate(rels):
                k, s = base + (n + a) * m + jj, 2 * chips[j][0] + chips[j][1]
                sends.append(_remote(half(outs[a], s, c), half(outs[a], s, c), ssem.at[k], rsem.at[k], (x, y, 1 - c)))
                recvs.append(_remote(half(outs[a], s, 1 - c), half(outs[a], s, 1 - c), ssem.at[k], rsem.at[k],
                                     (x, y, 1 - c)))
        return sends, recvs

    return _Comm("all_gather", tuple(bufs), tuple(SDS(b.shape, b.dtype) for b in bufs), {a: a for a in range(n)},
                 2 * n * m, (ici, d2d))


def _swap_comm(grads):
    n = len(grads)

    def phase(ins, outs, ssem, rsem, base):
        x, y, c, _ = _place()
        sends = []
        for a in range(n):
            h = ins[a].shape[1] // 2
            sends.append(_remote(ins[a].at[:, pl.ds((1 - c) * h, h), :], outs[a], ssem.at[base + a],
                                 rsem.at[base + a], (x, y, 1 - c)))
        return sends, sends

    return _Comm("swap_halves", tuple(grads),
                 tuple(SDS((g.shape[0], g.shape[1] // 2, g.shape[2]), g.dtype) for g in grads), {}, n, (phase,))


def _exchange_comm(parts):
    n = len(parts)

    def phase(ins, outs, ssem, rsem, base):
        x, y, c, chips = _place()
        sends = []
        for a in range(n):
            for j, chip in enumerate(chips):
                k = base + 3 * a + j
                sends.append(_remote(ins[a].at[2 * chip[0] + chip[1]], outs[a].at[j], ssem.at[k], rsem.at[k],
                                     (*chip, c)))
        return sends, sends

    return _Comm("exchange_shards", tuple(parts), tuple(SDS((3,) + p.shape[1:], p.dtype) for p in parts), {}, 3 * n,
                 (phase,))


def _join_comm(bufs, n_parts=1):
    n = len(bufs)

    def phase(ins, outs, ssem, rsem, base):
        x, y, c, _ = _place()
        sends, recvs = [], []
        for a in range(n):
            h = outs[a].shape[0] // (2 * n_parts)
            for p in range(n_parts):
                k = base + a * n_parts + p
                mine = outs[a].at[pl.ds((2 * p + c) * h, h), :]
                other = outs[a].at[pl.ds((2 * p + 1 - c) * h, h), :]
                sends.append(_remote(mine, mine, ssem.at[k], rsem.at[k], (x, y, 1 - c)))
                recvs.append(_remote(other, other, ssem.at[k], rsem.at[k], (x, y, 1 - c)))
        return sends, recvs

    return _Comm("join_halves", tuple(bufs), tuple(SDS(b.shape, b.dtype) for b in bufs), {a: a for a in range(n)},
                 n * n_parts, (phase,))


def _cast_place(w, ids):
    rows, cols = w.shape
    tr = min(rows, 256)

    def body(ids_ref, w_ref, o_ref):
        o_ref[...] = w_ref[...].astype(BF)

    return pl.pallas_call(
        body, name="cast_place",
        grid_spec=pltpu.PrefetchScalarGridSpec(
            num_scalar_prefetch=1, grid=(rows // tr,),
            in_specs=[pl.BlockSpec((tr, cols), lambda i, ids_ref: (i, 0))],
            out_specs=pl.BlockSpec((None, tr, cols), lambda i, ids_ref: (ids_ref[3], i, 0))),
        out_shape=SDS((N_SHARD, rows, cols), BF),
        compiler_params=_cp(("parallel",), 40))(ids, w)


def _all_gather_weights(bufs):
    n = len(bufs)

    def body(*refs):
        outs = refs[n:2 * n]
        send_sems, recv_sems = refs[2 * n:]
        x, y, c, chips = _place()
        sibling = (x, y, 1 - c)
        me = 2 * x + y

        def half(ref, s, which):
            h = ref.shape[1] // 2
            return ref.at[s, pl.ds(which * h, h), :]

        first = []
        for a in range(n):
            for j, chip in enumerate(chips):
                k = a * 3 + j
                win = half(outs[a], me, c)
                first.append(_remote(win, win, send_sems.at[k], recv_sems.at[k], (*chip, c)))
        for cp in first:
            cp.start()
        passed = []
        for a in range(n):
            for j, chip in enumerate(chips):
                k = a * 3 + j
                win = half(outs[a], 2 * chip[0] + chip[1], c)
                _remote(win, win, send_sems.at[k], recv_sems.at[k], (*chip, c)).wait_recv()
                fw = _remote(win, win, send_sems.at[3 * n + k], recv_sems.at[3 * n + k], sibling)
                fw.start()
                passed.append(fw)
        for a in range(n):
            for j, chip in enumerate(chips):
                k = a * 3 + j
                win = half(outs[a], 2 * chip[0] + chip[1], 1 - c)
                _remote(win, win, send_sems.at[3 * n + k], recv_sems.at[3 * n + k], sibling).wait_recv()
        for cp in first + passed:
            cp.wait_send()

    return pl.pallas_call(
        body, name="all_gather_weights",
        in_specs=[ANY] * n, out_specs=tuple([ANY] * n),
        out_shape=tuple(SDS(b.shape, b.dtype) for b in bufs),
        input_output_aliases={a: a for a in range(n)},
        scratch_shapes=[pltpu.SemaphoreType.DMA((6 * n,)), pltpu.SemaphoreType.DMA((6 * n,))],
        compiler_params=_cp(has_side_effects=True))(*bufs)


def _swap_halves(grads):
    n = len(grads)

    def body(*refs):
        ins, outs = refs[:n], refs[n:2 * n]
        send_sems, recv_sems = refs[2 * n:]
        x, y, c, _ = _place()
        sibling = (x, y, 1 - c)

        def half(ref, which):
            h = ref.shape[1] // 2
            return ref.at[:, pl.ds(which * h, h), :]

        sends = [_remote(half(ins[a], 1 - c), outs[a], send_sems.at[a], recv_sems.at[a], sibling)
                 for a in range(n)]
        for cp in sends:
            cp.start()
        for cp in sends:
            cp.wait_recv()
        for cp in sends:
            cp.wait_send()

    return pl.pallas_call(
        body, name="swap_halves",
        in_specs=[ANY] * n, out_specs=tuple([ANY] * n),
        out_shape=tuple(SDS((g.shape[0], g.shape[1] // 2, g.shape[2]), g.dtype) for g in grads),
        scratch_shapes=[pltpu.SemaphoreType.DMA((n,)), pltpu.SemaphoreType.DMA((n,))],
        compiler_params=_cp(has_side_effects=True))(*grads)


def _chip_sum(g, p, ids):
    n_s, rows, cols = g.shape
    h = rows // 2
    tr = min(h, 256)
    nb = h // tr

    def body(ids_ref, g_ref, p_ref, o_ref, o16_ref):
        t = g_ref[...] + p_ref[...]
        o_ref[...] = t
        o16_ref[...] = t.astype(BF)

    out_spec = pl.BlockSpec((None, tr, cols), lambda s, i, ids_ref: (s, i, 0))
    return pl.pallas_call(
        body, name="chip_sum",
        grid_spec=pltpu.PrefetchScalarGridSpec(
            num_scalar_prefetch=1, grid=(n_s, nb),
            in_specs=[pl.BlockSpec((None, tr, cols), lambda s, i, ids_ref: (s, ids_ref[2] * nb + i, 0)),
                      pl.BlockSpec((None, tr, cols), lambda s, i, ids_ref: (s, i, 0))],
            out_specs=(out_spec, out_spec)),
        out_shape=(SDS((n_s, h, cols), g.dtype), SDS((n_s, h, cols), BF)),
        compiler_params=_cp(("parallel", "parallel"), 40))(ids, g, p)


def _exchange_shards(parts):
    n = len(parts)

    def body(*refs):
        ins, outs = refs[:n], refs[n:2 * n]
        send_sems, recv_sems = refs[2 * n:]
        x, y, c, chips = _place()
        sends = []
        for a in range(n):
            for j, chip in enumerate(chips):
                k = a * 3 + j
                sends.append(_remote(ins[a].at[2 * chip[0] + chip[1]], outs[a].at[j],
                                     send_sems.at[k], recv_sems.at[k], (*chip, c)))
        for cp in sends:
            cp.start()
        for cp in sends:
            cp.wait_recv()
        for cp in sends:
            cp.wait_send()

    return pl.pallas_call(
        body, name="exchange_shards",
        in_specs=[ANY] * n, out_specs=tuple([ANY] * n),
        out_shape=tuple(SDS((3,) + p.shape[1:], p.dtype) for p in parts),
        scratch_shapes=[pltpu.SemaphoreType.DMA((3 * n,)), pltpu.SemaphoreType.DMA((3 * n,))],
        compiler_params=_cp(has_side_effects=True))(*parts)


def _shard_sum(t, q, ids, part=0, n_parts=1, buf=None):
    _, h, cols = t.shape
    tr = min(h, 256)
    nb = h // tr

    def body(ids_ref, t_ref, q_ref, *rest):
        rest[-1][...] = ((t_ref[...] + q_ref[0].astype(F32)) + q_ref[1].astype(F32)) + q_ref[2].astype(F32)

    args, in_specs, aliases = [t, q], [
        pl.BlockSpec((None, tr, cols), lambda i, ids_ref: (ids_ref[3], i, 0)),
        pl.BlockSpec((3, tr, cols), lambda i, ids_ref: (0, i, 0))], None
    if buf is not None:
        args, in_specs, aliases = args + [buf], in_specs + [ANY], {2: 0}
    return _call(body, args, None, name="shard_sum", grid=(nb,), in_specs=in_specs,
                 out_specs=pl.BlockSpec((tr, cols), lambda i, ids_ref: ((2 * part + ids_ref[2]) * nb + i, 0)),
                 out_shape=SDS((2 * h * n_parts, cols), t.dtype), aliases=aliases, prefetch=ids,
                 compiler_params=_cp(("parallel",), 40))


def _join_halves(bufs):
    n = len(bufs)

    def body(*refs):
        outs = refs[n:2 * n]
        send_sems, recv_sems = refs[2 * n:]
        x, y, c, _ = _place()
        sibling = (x, y, 1 - c)

        def win(ref, which):
            h = ref.shape[0] // 2
            return ref.at[pl.ds(which * h, h), :]

        sends = [_remote(win(outs[a], c), win(outs[a], c), send_sems.at[a], recv_sems.at[a], sibling)
                 for a in range(n)]
        for cp in sends:
            cp.start()
        for a in range(n):
            other = win(outs[a], 1 - c)
            _remote(other, other, send_sems.at[a], recv_sems.at[a], sibling).wait_recv()
        for cp in sends:
            cp.wait_send()

    return pl.pallas_call(
        body, name="join_halves",
        in_specs=[ANY] * n, out_specs=tuple([ANY] * n),
        out_shape=tuple(SDS(b.shape, b.dtype) for b in bufs),
        input_output_aliases={a: a for a in range(n)},
        scratch_shapes=[pltpu.SemaphoreType.DMA((n,)), pltpu.SemaphoreType.DMA((n,))],
        compiler_params=_cp(has_side_effects=True))(*bufs)


def _all_reduce_small(block):
    rows, cols = block.shape
    n_dev = 8

    def body(x_ref, o_ref, buf, send_sems, recv_sems, local_sem):
        x, y, c, chips = _place()
        me, sibling = (x, y, c), (x, y, 1 - c)

        def slot(px_, py_, pc_):
            return buf.at[4 * px_ + 2 * py_ + pc_]

        def copy(k, who, to, src=None):
            return _remote(slot(*who) if src is None else src, slot(*who), send_sems.at[k], recv_sems.at[k], to)

        mine = pltpu.make_async_copy(x_ref, slot(*me), local_sem)
        mine.start()
        first = [copy(0, me, sibling, src=x_ref)]
        first += [copy(1 + j, me, (*chip, c), src=x_ref) for j, chip in enumerate(chips)]
        for cp in first:
            cp.start()
        passed = [copy(4 + j, (*chip, c), sibling) for j, chip in enumerate(chips)]
        for j, chip in enumerate(chips):
            copy(1 + j, (*chip, c), me).wait_recv()
            passed[j].start()
        copy(0, sibling, me).wait_recv()
        for j, chip in enumerate(chips):
            copy(4 + j, (*chip, 1 - c), me).wait_recv()
        for cp in first + passed:
            cp.wait_send()
        mine.wait()
        acc = buf[0]
        for s in range(1, n_dev):
            acc = acc + buf[s]
        o_ref[...] = acc

    return pl.pallas_call(
        body, name="all_reduce_small",
        in_specs=[pl.BlockSpec(memory_space=pltpu.VMEM)],
        out_specs=pl.BlockSpec(memory_space=pltpu.VMEM),
        out_shape=SDS((rows, cols), F32),
        scratch_shapes=[pltpu.VMEM((n_dev, rows, cols), F32), pltpu.SemaphoreType.DMA((7,)),
                        pltpu.SemaphoreType.DMA((7,)), pltpu.SemaphoreType.DMA],
        compiler_params=_cp(has_side_effects=True))(block)


def _adam_math(w, g, m, v):
    m = ADAM_B1 * m + (1.0 - ADAM_B1) * g
    v = ADAM_B2 * v + (1.0 - ADAM_B2) * (g * g)
    m_hat = m / (1.0 - ADAM_B1 ** ADAM_STEP)
    v_hat = v / (1.0 - ADAM_B2 ** ADAM_STEP)
    delta = -ADAM_LR * (m_hat / (jnp.sqrt(v_hat) + ADAM_EPS) + ADAM_WD * w)
    return delta, m, v


def _adamw(w, g, m, v):
    rows, cols = w.shape
    tr = min(rows, 256)

    def body(w_ref, g_ref, m_ref, v_ref, d_ref, nm_ref, nv_ref):
        d_ref[...], nm_ref[...], nv_ref[...] = _adam_math(w_ref[...], g_ref[...], m_ref[...], v_ref[...])

    spec = pl.BlockSpec((tr, cols), lambda i: (i, 0))
    return pl.pallas_call(
        body, name="adamw", grid=(rows // tr,), in_specs=[spec] * 4, out_specs=(spec,) * 3,
        out_shape=(SDS(w.shape, F32),) * 3, compiler_params=_cp(("parallel",), 40))(w, g, m, v)


def _adamw_small(w, g, m, v):
    def body(w_ref, g_ref, m_ref, v_ref, go_ref, d_ref, nm_ref, nv_ref):
        w = w_ref[...]
        g = g_ref[...]
        sub = lax.broadcasted_iota(jnp.int32, w.shape, 0)
        lane = lax.broadcasted_iota(jnp.int32, w.shape, 1)
        is_ret = jnp.logical_and(sub == 5, lane < 2 * RET_HEADS)
        u = jnp.exp(jnp.where(is_ret, w, -1.0) * jnp.log(2.0))
        g = jnp.where(is_ret, g * (-u * jnp.log(2.0) / (1.0 - u)), g)
        go_ref[...] = g
        d_ref[...], nm_ref[...], nv_ref[...] = _adam_math(w, g, m_ref[...], v_ref[...])

    return pl.pallas_call(body, name="adamw_small", out_shape=(SDS(w.shape, F32),) * 4)(w, g, m, v)


def _rope_tables(seq, n_samp, n_ctx_rows):
    rows = seq // GRID_W
    row = jnp.repeat(jnp.arange(rows, dtype=F32), GRID_W)
    col = jnp.tile(jnp.arange(GRID_W, dtype=F32), rows)
    half = ATT_HEAD_DIM // 2
    freqs = ROPE_THETA ** (-jnp.arange(0, half, 2, dtype=F32) / half)
    ang = jnp.concatenate([row[:, None] * freqs, col[:, None] * freqs], axis=-1)
    cos, sin = jnp.cos(ang), jnp.sin(ang)
    cos_f = jnp.repeat(cos, 2, axis=1)
    sin_s = jnp.stack([-sin, sin], axis=-1).reshape(seq, ATT_HEAD_DIM)
    cos_all = jnp.concatenate([jnp.tile(cos_f, (n_samp, 1)), jnp.ones((n_ctx_rows, ATT_HEAD_DIM), F32)], axis=0)
    sin_all = jnp.concatenate([jnp.tile(sin_s, (n_samp, 1)), jnp.zeros((n_ctx_rows, ATT_HEAD_DIM), F32)], axis=0)
    return cos_all, sin_all


def _pack_small(c_ctx, norm_w, b_ada, ret, qn, kn):
    d = D_MODEL
    row5 = jnp.concatenate([ret.reshape(-1), jnp.zeros((128 - 2 * RET_HEADS,), F32), qn.reshape(-1), kn.reshape(-1),
                            jnp.zeros((d - 384,), F32)])
    return jnp.concatenate([c_ctx.reshape(1, d), norm_w.reshape(1, d), b_ada.reshape(3, d), row5.reshape(1, d),
                            jnp.zeros((2, d), F32)], axis=0)


def _unpack_small(p):
    d = D_MODEL
    return (p[0], p[1:2], p[2:5].reshape(1, 3 * d), p[5, :2 * RET_HEADS].reshape(1, 2, RET_HEADS),
            p[5:6, 128:256], p[5:6, 256:384])


def _step(x, c, ctx, c_ctx, norm_w, b_ada, ret_log2_decay, q_norm_w, k_norm_w, loss_target, weights, ids, dist):
    n_samp, seq, d = x.shape
    lc = ctx.shape[1]
    t_lat, t_ctx = n_samp * seq, n_samp * lc
    assert seq % TM == 0 and t_ctx == TM and t_lat % lc == 0 and seq % GRID_W == 0
    tps = seq // TM

    x_lat = x.reshape(t_lat, d)
    x_ctx = ctx.reshape(t_ctx, d)
    cvec8 = jnp.concatenate([c, c_ctx.reshape(1, d), jnp.zeros((8 - n_samp - 1, d), F32)], axis=0)
    lg = jnp.log1p(-jnp.exp2(ret_log2_decay.reshape(2, RET_HEADS)))
    cos_all, sin_all = _rope_tables(seq, n_samp, t_ctx)

    w_ada_b, w_in_b, w_or_b, w_oa_b, w_out_b = weights
    w_ada_g = _run_comm(_ag_comm((w_ada_b,), (0, 1, 2)))[0] if dist else w_ada_b
    mod8 = _adaln_fwd(cvec8, w_ada_g, b_ada)
    mod3 = mod8[:n_samp + 1]
    shift3 = mod3[:, None, 0:d]
    scale3 = mod3[:, None, d:2 * d]
    gate3 = mod3[:, None, 2 * d:3 * d]

    hx, hxt = _norm_fwd(x_lat, x_ctx, norm_w, scale3, shift3, tps, n_samp)
    if dist:
        px, (w_in_1,) = _in_proj(hx, w_in_b, ids, 0, 1, comm=_ag_comm((w_in_b,), (0, 1)))
        px, (w_in_g,) = _in_proj(hx, w_in_1, ids, 1, 2, px=px, comm=_ag_comm((w_in_1,), (2,)))
        px = _in_proj(hx, w_in_g, ids, 3, 1, px=px)
    else:
        w_in_g = w_in_b
        px = _in_proj(hx, w_in_g, ids, 0, N_SHARD)

    states0 = _ctx_state_fwd(px, lg, n_samp, t_lat, lc)
    if dist:
        (o_f, o_b, saved), w_o = _ret_fwd(px, states0, lg, n_samp, seq,
                                          comm=_ag_comm((w_or_b, w_oa_b, w_out_b), (0, 1, 2)))
    else:
        (o_f, o_b, saved), w_o = _ret_fwd(px, states0, lg, n_samp, seq), (w_or_b, w_oa_b, w_out_b)
    w_o_ret, w_o_att, w_out = (w.reshape(-1, d) for w in w_o)
    y_ret = _retnorm_fwd(o_f, o_b, px)

    qn = _att_prep_q(px, cos_all, sin_all, q_norm_w, t_lat)
    kn, vn = _att_prep_kv(px, cos_all, sin_all, k_norm_w)
    y_att, o_att = _att_fwd(qn, kn, vn, px, n_samp, seq, lc)

    (gx_res, dy_ret, dy_att, dpx, loss8, dgate, g_w_o_ret, g_w_o_att, g_w_out) = _merge(
        x_lat, loss_target.reshape(t_lat, d), y_ret, y_att, px, gate3, w_o_ret, w_o_att, w_out, tps)

    g_a = [g.reshape(N_SHARD, -1, d) for g in (g_w_o_ret, g_w_o_att, g_w_out)]
    dpx = _att_gate_bwd(dpx, dy_att, o_att, px)
    res = _att_bwd(dpx, qn, kn, vn, px, o_att, dy_att, cos_all, sin_all, q_norm_w, n_samp, seq, lc,
                   comm=_swap_comm(g_a) if dist else None)
    (dpx, dkl, dkc, dvl, dvc, gqw), sib_a = res if dist else (res, None)
    dpx, gkw = _att_kv_bwd(dpx, dkl, dkc, dvl, dvc, px, cos_all, sin_all, k_norm_w)
    if dist:
        t_a = [_chip_sum(g, p, ids) for g, p in zip(g_a, sib_a)]

    do, dpx = _retnorm_bwd(dpx, dy_ret, o_f, o_b, px)
    res = _ret_bwd(px, do, saved, lg, n_samp, seq,
                   comm=_exchange_comm([t16 for _, t16 in t_a]) if dist else None)
    (dqf, dkf, dvf, dqb, dkb, dvb, dstates, dlg_lat), q_a = res if dist else (res, None)
    if dist:
        r_a = [_shard_sum(t, q, ids) for (t, _), q in zip(t_a, q_a)]
    dpx = _combine_into(dpx, dqf, dqb, C_RQ, 1.0)
    dpx = _combine_into(dpx, dkf, dkb, C_RK, RET_DK ** -0.5)
    dpx = _combine_into(dpx, dvf, dvb, C_RV, 1.0)
    dpx, dlg_ctx = _ctx_state_bwd(dpx, px, dstates, lg, n_samp, t_lat, lc)
    dpx = _zero_ctx_tail(dpx, t_lat)

    n_tiles = dpx.shape[0] // _big_rows(dpx.shape[0])
    if dist:
        g_b1 = _gw_in(hxt, dpx, 0, 2)
        g_b2, (sib_b1, *r_a) = _gw_in(hxt, dpx, 1, 2, comm=_join_comms(_swap_comm([g_b1]), _join_comm(r_a)))
        t_b1, t16_b1 = _chip_sum(g_b1, sib_b1, ids)
        dhx, (q_b1, sib_b2) = _dhx(dpx, w_in_g, 0, 1,
                                   comm=_join_comms(_exchange_comm([t16_b1]), _swap_comm([g_b2])))
        t_b2, t16_b2 = _chip_sum(g_b2, sib_b2, ids)
        r_b = _shard_sum(t_b1, q_b1, ids, 0, 2)
        dhx, (q_b2,) = _dhx(dpx, w_in_g, 1, n_tiles - 1, dhx=dhx, comm=_exchange_comm([t16_b2]))
        r_b = _shard_sum(t_b2, q_b2, ids, 1, 2, buf=r_b)
        (grad_x, dshift, dscale, g_norm_w), (r_b,) = _norm_bwd(
            x_lat, x_ctx, dhx, gx_res, norm_w, scale3, tps, n_samp, comm=_join_comm([r_b], 2))
    else:
        g_w_in = _gw_in(hxt, dpx, 0, 1)
        dhx = _dhx(dpx, w_in_g, 0, n_tiles)
        grad_x, dshift, dscale, g_norm_w = _norm_bwd(x_lat, x_ctx, dhx, gx_res, norm_w, scale3, tps, n_samp)

    dgate_all = jnp.concatenate([dgate, jnp.zeros((1, 1, d), F32)], axis=0)
    dmod3 = jnp.concatenate([dshift, dscale, dgate_all], axis=2).reshape(n_samp + 1, 3 * d)
    dmod8 = jnp.concatenate([dmod3, jnp.zeros((8 - n_samp - 1, 3 * d), F32)], axis=0)
    g_w_ada, g_b_ada, dc8 = _adaln_bwd(cvec8, dmod8, w_ada_g)
    if dist:
        (sib_c,) = _run_comm(_swap_comm([g_w_ada]))
        t_c, t16_c = _chip_sum(g_w_ada, sib_c, ids)
        (q_c,) = _run_comm(_exchange_comm([t16_c]))
        (r_c,) = _run_comm(_join_comm([_shard_sum(t_c, q_c, ids)]))
        big = (r_c, r_b, *r_a)
    else:
        big = (g_w_ada, g_w_in, g_w_o_ret, g_w_o_att, g_w_out)

    g_lg = (jnp.sum(dlg_lat[:, :, 0], axis=0).reshape(2, RET_HEADS)
            + jnp.stack([jnp.sum(dlg_ctx[:, :, 0, 0], axis=0), jnp.sum(dlg_ctx[:, :, 1, 0], axis=0)], axis=0))
    small = _pack_small(dc8[n_samp], g_norm_w, g_b_ada, g_lg, jnp.sum(gqw, axis=(0, 1, 2)), gkw)
    return loss8[0, 0], grad_x.reshape(n_samp, seq, d), big, small


def kernel(x, c, ctx, c_ctx, norm_w, w_ada, b_ada, w_in, ret_log2_decay, q_norm_w, k_norm_w, w_o_ret, w_o_att, w_out, loss_target, m_c_ctx, m_norm_w, m_w_ada, m_b_ada, m_w_in, m_ret_log2_decay, m_q_norm_w, m_k_norm_w, m_w_o_ret, m_w_o_att, m_w_out, v_c_ctx, v_norm_w, v_w_ada, v_b_ada, v_w_in, v_ret_log2_decay, v_q_norm_w, v_k_norm_w, v_w_o_ret, v_w_o_att, v_w_out):
    big_w = (w_ada[0], w_in[0], w_o_ret[0], w_o_att[0], w_out[0])
    big_m = (m_w_ada[0], m_w_in[0], m_w_o_ret[0], m_w_o_att[0], m_w_out[0])
    big_v = (v_w_ada[0], v_w_in[0], v_w_o_ret[0], v_w_o_att[0], v_w_out[0])

    ids = _place_ids()
    loss_local, grad_x, big_grad, small_g = _step(
        x, c, ctx, c_ctx, norm_w[0:1], b_ada[0:1], ret_log2_decay[0], q_norm_w[0:1], k_norm_w[0:1], loss_target,
        tuple(_cast_place(w, ids) for w in big_w), ids, True)
    loss = lax.psum(loss_local, ("x", "y", "c"))

    small_grad_in = _all_reduce_small(small_g)
    small_w = _pack_small(c_ctx, norm_w, b_ada, ret_log2_decay, q_norm_w, k_norm_w)
    small_m = _pack_small(m_c_ctx, m_norm_w, m_b_ada, m_ret_log2_decay, m_q_norm_w, m_k_norm_w)
    small_v = _pack_small(v_c_ctx, v_norm_w, v_b_ada, v_ret_log2_decay, v_q_norm_w, v_k_norm_w)
    small_grad, small_delta, small_nm, small_nv = _adamw_small(small_w, small_grad_in, small_m, small_v)

    big_delta, big_nm, big_nv = [], [], []
    for w, g, m, v in zip(big_w, big_grad, big_m, big_v):
        dlt, nm, nv = _adamw(w, g, m, v)
        big_delta.append(dlt[None])
        big_nm.append(nm[None])
        big_nv.append(nv[None])
    big_grad = [g[None] for g in big_grad]

    def order(small_packed, big):
        s = _unpack_small(small_packed)
        return (s[0], s[1], big[0], s[2], big[1], s[3], s[4], s[5], big[2], big[3], big[4])

    return (loss, grad_x, *order(small_grad, big_grad), *order(small_delta, big_delta),
            *order(small_nm, big_nm), *order(small_nv, big_nv))
```

```python
import functools
from typing import NamedTuple

import jax
import jax.numpy as jnp
from jax import lax
from jax.experimental import pallas as pl
from jax.experimental.pallas import tpu as pltpu

F32 = jnp.float32
BF = jnp.bfloat16
SDS = jax.ShapeDtypeStruct
MESH = pl.DeviceIdType.MESH
ANY = pl.BlockSpec(memory_space=pl.ANY)
SMEM = pl.BlockSpec(memory_space=pltpu.SMEM)

D_MODEL = 1024
GRID_W = 64
RET_HEADS = 4
RET_DK = 256
RET_DV = 512
RET_CHUNK = 128
ATT_HEADS = 8
ATT_KV_HEADS = 2
ATT_REP = ATT_HEADS // ATT_KV_HEADS
ATT_HEAD_DIM = 128
ROPE_THETA = 10000.0
NORM_EPS = 1e-6
IN_COLS = 10752
KV_COLS = 3584
C_RK, C_RV, C_AK, C_AV, C_RQ, C_RG, C_AQ, C_AG, C_MR, C_MA = 0, 1024, 3072, 3328, 3584, 4608, 6656, 7680, 8704, 9728
N_SHARD = 4
ADA_W = 3 * D_MODEL // N_SHARD
IN_W = IN_COLS // N_SHARD
IN_BLK = IN_W // 3
N_IN_BLK = IN_COLS // IN_BLK
TM = 512
ADAM_LR, ADAM_B1, ADAM_B2, ADAM_EPS, ADAM_WD, ADAM_STEP = 0.001, 0.9, 0.999, 1e-08, 0.01, 10
MIB = 1024 * 1024


def _cp(sem=None, vmem_mb=None, **kw):
    if sem is not None:
        kw["dimension_semantics"] = sem
    if vmem_mb is not None:
        kw["vmem_limit_bytes"] = vmem_mb * MIB
    return pltpu.CompilerParams(**kw)


def _dot(a, b, ca=1, cb=0):
    return lax.dot_general(a.astype(BF), b.astype(BF), (((ca,), (cb,)), ((), ())), preferred_element_type=F32)


def _sigmoid(x):
    return 1.0 / (1.0 + jnp.exp(-x))


def _sum_all(x):
    return jnp.sum(jnp.sum(x, axis=1, keepdims=True), axis=0, keepdims=True)


def _swap_pairs(x):
    ax = x.ndim - 1
    lane = lax.broadcasted_iota(jnp.int32, x.shape, ax)
    nxt = pltpu.roll(x, x.shape[ax] - 1, ax)
    prv = pltpu.roll(x, 1, ax)
    return jnp.where(lane % 2 == 0, nxt, prv)


def _rms(x):
    return lax.rsqrt(jnp.mean(x * x, axis=-1, keepdims=True) + NORM_EPS)


def _rms_bwd(dxh, xh, r):
    return r * (dxh - xh * jnp.mean(dxh * xh, axis=-1, keepdims=True))


class _Comm(NamedTuple):
    name: str
    ins: tuple
    out_shapes: tuple
    aliases: dict
    n_sems: int
    phases: tuple
    arg_aliases: tuple = ()


def _join_comms(*comms):
    comms = [cm for cm in comms if cm is not None]
    if len(comms) <= 1:
        return comms[0] if comms else None
    offs, i_off, o_off, s_off = [], 0, 0, 0
    for cm in comms:
        offs.append((i_off, o_off, s_off))
        i_off, o_off, s_off = i_off + len(cm.ins), o_off + len(cm.out_shapes), s_off + cm.n_sems

    def phase(k):
        def run(ins, outs, ssem, rsem, base):
            sends, recvs = [], []
            for cm, (io, oo, so) in zip(comms, offs):
                if k < len(cm.phases):
                    s, r = cm.phases[k](ins[io:io + len(cm.ins)], outs[oo:oo + len(cm.out_shapes)], ssem, rsem,
                                        base + so)
                    sends += s
                    recvs += r
            return sends, recvs
        return run

    aliases, arg_aliases = {}, ()
    for cm, (io, oo, _) in zip(comms, offs):
        aliases.update({io + a: oo + b for a, b in cm.aliases.items()})
        arg_aliases += tuple((a, oo + b) for a, b in cm.arg_aliases)
    return _Comm("+".join(cm.name for cm in comms), sum((cm.ins for cm in comms), ()),
                 sum((cm.out_shapes for cm in comms), ()), aliases, s_off,
                 tuple(phase(k) for k in range(max(len(cm.phases) for cm in comms))), arg_aliases)


def _run_phases(comm, cins, couts, ssem, rsem, first_started):
    for k, phase in enumerate(comm.phases):
        sends, recvs = phase(cins, couts, ssem, rsem, 0)
        if k > 0 or not first_started:
            for cp in sends:
                cp.start()
        for cp in recvs:
            cp.wait_recv()
        for cp in sends:
            cp.wait_send()


def _call(body, args, comm=None, *, name, grid, in_specs, out_specs, out_shape, scratch_shapes=(),
          compiler_params, aliases=None, prefetch=None):
    single = not isinstance(out_shape, (tuple, list))
    out_specs_t = (out_specs,) if single else tuple(out_specs)
    out_shape_t = (out_shape,) if single else tuple(out_shape)
    n_pre = 0 if prefetch is None else 1
    n_in, n_out, n_sc = len(in_specs), len(out_specs_t), len(scratch_shapes)
    io_alias = {n_pre + a: b for a, b in (aliases or {}).items()}
    if comm is None:
        kernel_body, cin, cout, csems = body, [], [], []
    else:
        n_ci, n_co = len(comm.ins), len(comm.out_shapes)
        cin, cout = [ANY] * n_ci, [ANY] * n_co
        csems = [pltpu.SemaphoreType.DMA((comm.n_sems,)), pltpu.SemaphoreType.DMA((comm.n_sems,))]
        io_alias.update({n_pre + n_in + a: n_out + b for a, b in comm.aliases.items()})
        io_alias.update({n_pre + a: n_out + b for a, b in comm.arg_aliases})

        def kernel_body(*refs):
            pre, refs = refs[:n_pre], refs[n_pre:]
            ins, cins = refs[:n_in], refs[n_in:n_in + n_ci]
            outs = refs[n_in + n_ci:n_in + n_ci + n_out]
            couts = refs[n_in + n_ci + n_out:n_in + n_ci + n_out + n_co]
            scratch = refs[n_in + n_ci + n_out + n_co:n_in + n_ci + n_out + n_co + n_sc]
            ssem, rsem = refs[-2:]
            first = functools.reduce(jnp.logical_and, [pl.program_id(k) == 0 for k in range(len(grid))])
            last = functools.reduce(jnp.logical_and, [pl.program_id(k) == grid[k] - 1 for k in range(len(grid))])

            @pl.when(first)
            def _():
                for cp in comm.phases[0](cins, couts, ssem, rsem, 0)[0]:
                    cp.start()

            body(*pre, *ins, *outs, *scratch)

            @pl.when(last)
            def _():
                _run_phases(comm, cins, couts, ssem, rsem, True)

        name = name + "+" + comm.name

    all_in, all_out = list(in_specs) + cin, out_specs_t + tuple(cout)
    shapes = out_shape_t + (tuple(comm.out_shapes) if comm is not None else ())
    scratch = list(scratch_shapes) + csems
    if prefetch is None:
        res = pl.pallas_call(kernel_body, name=name, grid=grid, in_specs=all_in, out_specs=all_out, out_shape=shapes,
                             scratch_shapes=scratch, input_output_aliases=io_alias,
                             compiler_params=compiler_params)(*args, *(comm.ins if comm is not None else ()))
    else:
        res = pl.pallas_call(
            kernel_body, name=name, out_shape=shapes, input_output_aliases=io_alias, compiler_params=compiler_params,
            grid_spec=pltpu.PrefetchScalarGridSpec(num_scalar_prefetch=1, grid=grid, in_specs=all_in,
                                                   out_specs=all_out, scratch_shapes=scratch))(
                                                       prefetch, *args, *(comm.ins if comm is not None else ()))
    own = res[0] if single else tuple(res[:n_out])
    return own if comm is None else (own, tuple(res[n_out:]))


def _run_comm(comm):
    n_ci, n_co = len(comm.ins), len(comm.out_shapes)

    def body(*refs):
        _run_phases(comm, refs[:n_ci], refs[n_ci:n_ci + n_co], refs[-2], refs[-1], False)

    return pl.pallas_call(
        body, name=comm.name, in_specs=[ANY] * n_ci, out_specs=tuple([ANY] * n_co), out_shape=tuple(comm.out_shapes),
        input_output_aliases=dict(comm.aliases),
        scratch_shapes=[pltpu.SemaphoreType.DMA((comm.n_sems,)), pltpu.SemaphoreType.DMA((comm.n_sems,))],
        compiler_params=_cp(has_side_effects=True))(*comm.ins)


def _adaln_fwd(cvec8, w_ada_g, b_ada):
    def body(c_ref, w_ref, b_ref, o_ref):
        cv = c_ref[...]
        sc = (cv * _sigmoid(cv)).astype(BF)
        for s in range(N_SHARD):
            cols = slice(s * ADA_W, (s + 1) * ADA_W)
            o_ref[:, cols] = jnp.dot(sc, w_ref[s], preferred_element_type=F32) + b_ref[:, cols]

    return pl.pallas_call(body, out_shape=SDS((8, 3 * D_MODEL), F32), name="adaln_fwd",
                          compiler_params=_cp(vmem_mb=32))(cvec8, w_ada_g, b_ada)


def _adaln_bwd(cvec8, dmod8, w_ada_g):
    def body(c_ref, d_ref, w_ref, gw_ref, gb_ref, dc_ref):
        cv = c_ref[...]
        sg = _sigmoid(cv)
        sc = cv * sg
        dm = d_ref[...]
        gb_ref[...] = jnp.sum(dm, axis=0, keepdims=True)
        dsc = jnp.zeros((8, D_MODEL), F32)
        for s in range(N_SHARD):
            cols = slice(s * ADA_W, (s + 1) * ADA_W)
            gw_ref[s] = _dot(sc, dm[:, cols], 0, 0)
            dsc = dsc + _dot(dm[:, cols], w_ref[s], 1, 1)
        dc_ref[...] = dsc * (sg * (1.0 + cv * (1.0 - sg)))

    return pl.pallas_call(
        body, name="adaln_bwd",
        out_shape=(SDS((N_SHARD, D_MODEL, ADA_W), F32), SDS((1, 3 * D_MODEL), F32), SDS((8, D_MODEL), F32)),
        compiler_params=_cp(vmem_mb=48))(cvec8, dmod8, w_ada_g)


def _big_rows(rows):
    return 1536 if rows % 1536 == 0 else TM


def _norm_fwd(x_lat, x_ctx, norm_w, scale3, shift3, tiles_per_sample, n_samp):
    n_lat = x_lat.shape[0] // TM
    rows = x_lat.shape[0] + x_ctx.shape[0]

    def samp(i):
        return jnp.minimum(i // tiles_per_sample, n_samp)

    def body(x_ref, c_ref, nw_ref, sc_ref, sh_ref, hx_ref, hxt_ref):
        x = jnp.where(pl.program_id(0) < n_lat, x_ref[...], c_ref[...])
        h = x * _rms(x) * nw_ref[...] * (1.0 + sc_ref[...]) + sh_ref[...]
        hx_ref[...] = h.astype(BF)
        hxt_ref[...] = h.T.astype(BF)

    return pl.pallas_call(
        body, name="norm_fwd", grid=(rows // TM,),
        in_specs=[pl.BlockSpec((TM, D_MODEL), lambda i: (jnp.minimum(i, n_lat - 1), 0)),
                  pl.BlockSpec((TM, D_MODEL), lambda i: (jnp.maximum(i - n_lat, 0), 0)),
                  pl.BlockSpec((1, D_MODEL), lambda i: (0, 0)),
                  pl.BlockSpec((None, 1, D_MODEL), lambda i: (samp(i), 0, 0)),
                  pl.BlockSpec((None, 1, D_MODEL), lambda i: (samp(i), 0, 0))],
        out_specs=(pl.BlockSpec((TM, D_MODEL), lambda i: (i, 0)),
                   pl.BlockSpec((D_MODEL, TM), lambda i: (0, i))),
        out_shape=(SDS((rows, D_MODEL), BF), SDS((D_MODEL, rows), BF)),
        compiler_params=_cp(("parallel",), 40))(x_lat, x_ctx, norm_w, scale3, shift3)


def _in_proj(hx, w_in_g, ids, first, count, px=None, comm=None):
    rows = hx.shape[0]
    tb = _big_rows(rows)

    def shard(j, ids_ref):
        return ids_ref[4 + first + j // 3]

    def body(ids_ref, h_ref, w_ref, *rest):
        px_ref = rest[-1]
        px_ref[...] = jnp.dot(h_ref[...], w_ref[...], preferred_element_type=F32).astype(BF)

    args, in_specs, aliases = [hx, w_in_g], [
        pl.BlockSpec((tb, D_MODEL), lambda j, i, ids_ref: (i, 0)),
        pl.BlockSpec((None, D_MODEL, IN_BLK), lambda j, i, ids_ref: (shard(j, ids_ref), 0, j % 3))], None
    if px is not None:
        args, in_specs, aliases = args + [px], in_specs + [ANY], {2: 0}
    return _call(body, args, comm, name="in_proj", grid=(3 * count, rows // tb), in_specs=in_specs,
                 out_specs=pl.BlockSpec((tb, IN_BLK), lambda j, i, ids_ref: (i, 3 * shard(j, ids_ref) + j % 3)),
                 out_shape=SDS((rows, IN_COLS), BF), aliases=aliases, prefetch=ids,
                 compiler_params=_cp(("arbitrary", "arbitrary"), 40))


def _norm_bwd(x_lat, x_ctx, dhx, gx_res, norm_w, scale3, tiles_per_sample, n_samp, comm=None):
    rows = x_lat.shape[0] + x_ctx.shape[0]
    n_lat = tiles_per_sample * n_samp

    def samp(i):
        return jnp.minimum(i // tiles_per_sample, n_samp)

    def lat(i):
        return jnp.minimum(i, n_lat - 1)

    def body(x_ref, c_ref, dh_ref, gr_ref, nw_ref, sc_ref, gx_ref, dsh_ref, dsc_ref, dnw_ref):
        i = pl.program_id(0)
        x = jnp.where(i < n_lat, x_ref[...], c_ref[...])
        r = _rms(x)
        xh = x * r
        nw = nw_ref[...]
        dh = dh_ref[...]
        first = jnp.logical_or(i % tiles_per_sample == 0, i >= n_lat)

        @pl.when(first)
        def _():
            dsh_ref[...] = jnp.zeros_like(dsh_ref)
            dsc_ref[...] = jnp.zeros_like(dsc_ref)

        @pl.when(i == 0)
        def _():
            dnw_ref[...] = jnp.zeros_like(dnw_ref)

        dsh_ref[...] += jnp.sum(dh, axis=0, keepdims=True)
        dsc_ref[...] += jnp.sum(dh * (xh * nw), axis=0, keepdims=True)
        du = dh * (1.0 + sc_ref[...])
        dnw_ref[...] += jnp.sum(du * xh, axis=0, keepdims=True)

        @pl.when(i < n_lat)
        def _():
            gx_ref[...] = gr_ref[...] + _rms_bwd(du * nw, xh, r)

    return _call(
        body, [x_lat, x_ctx, dhx, gx_res, norm_w, scale3], comm, name="norm_bwd", grid=(rows // TM,),
        in_specs=[pl.BlockSpec((TM, D_MODEL), lambda i: (lat(i), 0)),
                  pl.BlockSpec((TM, D_MODEL), lambda i: (jnp.maximum(i - n_lat, 0), 0)),
                  pl.BlockSpec((TM, D_MODEL), lambda i: (i, 0)),
                  pl.BlockSpec((TM, D_MODEL), lambda i: (lat(i), 0)),
                  pl.BlockSpec((1, D_MODEL), lambda i: (0, 0)),
                  pl.BlockSpec((None, 1, D_MODEL), lambda i: (samp(i), 0, 0))],
        out_specs=(pl.BlockSpec((TM, D_MODEL), lambda i: (lat(i), 0)),
                   pl.BlockSpec((None, 1, D_MODEL), lambda i: (samp(i), 0, 0)),
                   pl.BlockSpec((None, 1, D_MODEL), lambda i: (samp(i), 0, 0)),
                   pl.BlockSpec((1, D_MODEL), lambda i: (0, 0))),
        out_shape=(SDS((n_lat * TM, D_MODEL), F32), SDS((n_samp + 1, 1, D_MODEL), F32),
                   SDS((n_samp + 1, 1, D_MODEL), F32), SDS((1, D_MODEL), F32)),
        compiler_params=_cp(("arbitrary",), 40))


def _gw_in(hxt, dpx_all, part, n_parts, comm=None):
    rows = dpx_all.shape[0]
    tb = _big_rows(rows)
    dp = D_MODEL // n_parts

    def body(h_ref, d_ref, o_ref):
        @pl.when(pl.program_id(1) == 0)
        def _():
            o_ref[...] = jnp.zeros_like(o_ref)

        o_ref[...] += jnp.dot(h_ref[...], d_ref[...], preferred_element_type=F32)

    return _call(body, [hxt, dpx_all], comm, name="gw_in", grid=(N_IN_BLK, rows // tb),
                 in_specs=[pl.BlockSpec((dp, tb), lambda j, i: (part, i)),
                           pl.BlockSpec((tb, IN_BLK), lambda j, i: (i, j))],
                 out_specs=pl.BlockSpec((None, dp, IN_BLK), lambda j, i: (j // 3, 0, j % 3)),
                 out_shape=SDS((N_SHARD, dp, IN_W), F32),
                 compiler_params=_cp(("arbitrary", "arbitrary"), 40))


def _dhx(dpx_all, w_in_g, tile0, n_tiles, dhx=None, comm=None):
    rows = dpx_all.shape[0]
    tb = _big_rows(rows)

    def body(d_ref, w_ref, *rest):
        o_ref = rest[-1]

        @pl.when(pl.program_id(1) == 0)
        def _():
            o_ref[...] = jnp.zeros_like(o_ref)

        o_ref[...] += lax.dot_general(d_ref[...], w_ref[...], (((1,), (1,)), ((), ())), preferred_element_type=F32)

    args, in_specs, aliases = [dpx_all, w_in_g], [
        pl.BlockSpec((tb, IN_BLK), lambda i, j: (tile0 + i, j)),
        pl.BlockSpec((None, D_MODEL, IN_BLK), lambda i, j: (j // 3, 0, j % 3))], None
    if dhx is not None:
        args, in_specs, aliases = args + [dhx], in_specs + [ANY], {2: 0}
    return _call(body, args, comm, name="dhx", grid=(n_tiles, N_IN_BLK), in_specs=in_specs,
                 out_specs=pl.BlockSpec((tb, D_MODEL), lambda i, j: (tile0 + i, 0)),
                 out_shape=SDS((rows, D_MODEL), F32), aliases=aliases,
                 compiler_params=_cp(("arbitrary", "arbitrary"), 40))


def _decays(lgv, d):
    c = RET_CHUNK
    ii = lax.broadcasted_iota(jnp.int32, (c, 1), 0).astype(F32)
    jj = lax.broadcasted_iota(jnp.int32, (1, c), 1).astype(F32)
    a_i = jnp.where(d == 0, ii, c - 1.0 - ii)
    a_j = jnp.where(d == 0, jj, c - 1.0 - jj)
    rel = a_i - a_j
    mask = jnp.where(rel >= 0, jnp.exp(lgv * jnp.maximum(rel, 0.0)), 0.0)
    qd = jnp.exp(lgv * (a_i + 1.0))
    kd = jnp.exp(lgv * (c - 1.0 - a_i))
    gc = jnp.exp(jnp.full((1, 1), lgv * c, F32))
    return a_i, rel, mask, qd, kd, gc


def _ctx_state_fwd(px, lg, n_samp, t_lat, lc):
    rb = t_lat // lc

    def body(lg_ref, k_ref, v_ref, o_ref):
        h = pl.program_id(1)
        k = k_ref[...].astype(F32) * (RET_DK ** -0.5)
        v = v_ref[...]
        pos = lax.broadcasted_iota(jnp.int32, (lc, 1), 0).astype(F32)
        o_ref[0] = _dot(k * jnp.exp(lg_ref[0, h] * (lc - 1.0 - pos)), v, 0, 0)
        o_ref[1] = _dot(k * jnp.exp(lg_ref[1, h] * pos), v, 0, 0)

    return pl.pallas_call(
        body, name="ctx_state_fwd", grid=(n_samp, RET_HEADS),
        in_specs=[SMEM,
                  pl.BlockSpec((lc, RET_DK), lambda b, h: (rb + b, C_RK // RET_DK + h)),
                  pl.BlockSpec((lc, RET_DV), lambda b, h: (rb + b, C_RV // RET_DV + h))],
        out_specs=pl.BlockSpec((None, 2, None, RET_DK, RET_DV), lambda b, h: (b, 0, h, 0, 0)),
        out_shape=SDS((n_samp, 2, RET_HEADS, RET_DK, RET_DV), F32),
        compiler_params=_cp(("parallel", "parallel")))(lg, px, px)


def _ctx_state_bwd(dpx, px, dstates, lg, n_samp, t_lat, lc):
    rb = t_lat // lc
    kspec = pl.BlockSpec((lc, RET_DK), lambda b, h: (rb + b, C_RK // RET_DK + h))
    vspec = pl.BlockSpec((lc, RET_DV), lambda b, h: (rb + b, C_RV // RET_DV + h))
    sspec = pl.BlockSpec((None, 2, None, RET_DK, RET_DV), lambda b, h: (b, 0, h, 0, 0))

    def weights(lg_ref, h):
        pos = lax.broadcasted_iota(jnp.int32, (lc, 1), 0).astype(F32)
        e_f = lc - 1.0 - pos
        return pos, e_f, jnp.exp(lg_ref[0, h] * e_f), jnp.exp(lg_ref[1, h] * pos)

    def k_body(lg_ref, dpx_hbm, k_ref, v_ref, ds_ref, dk_ref, dlg_ref):
        pos, e_f, w_f, w_b = weights(lg_ref, pl.program_id(1))
        k = k_ref[...].astype(F32) * (RET_DK ** -0.5)
        y_f = _dot(v_ref[...], ds_ref[0], 1, 1) * w_f
        y_b = _dot(v_ref[...], ds_ref[1], 1, 1) * w_b
        dk_ref[...] = ((y_f + y_b) * (RET_DK ** -0.5)).astype(BF)
        t_f = _sum_all(e_f * k * y_f)
        t_b = _sum_all(pos * k * y_b)
        sub = lax.broadcasted_iota(jnp.int32, (8, 128), 0)
        dlg_ref[...] = jnp.where(sub == 0, t_f, jnp.where(sub == 1, t_b, 0.0))

    def v_body(lg_ref, dpx_hbm, k_ref, ds_ref, dv_ref):
        _, _, w_f, w_b = weights(lg_ref, pl.program_id(1))
        k = k_ref[...].astype(F32) * (RET_DK ** -0.5)
        dv_ref[...] = (_dot(k * w_f, ds_ref[0]) + _dot(k * w_b, ds_ref[1])).astype(BF)

    dpx, dlg = pl.pallas_call(
        k_body, name="ctx_state_bwd_k", grid=(n_samp, RET_HEADS), input_output_aliases={1: 0},
        in_specs=[SMEM, ANY, kspec, vspec, sspec],
        out_specs=(kspec, pl.BlockSpec((None, None, 8, 128), lambda b, h: (b, h, 0, 0))),
        out_shape=(SDS(dpx.shape, dpx.dtype), SDS((n_samp, RET_HEADS, 8, 128), F32)),
        compiler_params=_cp(("parallel", "parallel")))(lg, dpx, px, px, dstates)
    dpx = pl.pallas_call(
        v_body, name="ctx_state_bwd_v", grid=(n_samp, RET_HEADS), input_output_aliases={1: 0},
        in_specs=[SMEM, ANY, kspec, sspec], out_specs=vspec, out_shape=SDS(dpx.shape, dpx.dtype),
        compiler_params=_cp(("parallel", "parallel")))(lg, dpx, px, dstates)
    return dpx, dlg


def _zero_ctx_tail(dpx, t_lat):
    wb = 512
    n_ctx = (dpx.shape[0] - t_lat) // TM

    def body(dpx_hbm, o_ref):
        o_ref[...] = jnp.zeros_like(o_ref)

    return pl.pallas_call(
        body, name="zero_ctx_tail", grid=(n_ctx, (IN_COLS - KV_COLS) // wb), input_output_aliases={0: 0},
        in_specs=[ANY], out_specs=pl.BlockSpec((TM, wb), lambda i, j: (t_lat // TM + i, KV_COLS // wb + j)),
        out_shape=SDS(dpx.shape, dpx.dtype),
        compiler_params=_cp(("parallel", "parallel")))(dpx)


def _ret_specs(row_f, row_b):
    c = RET_CHUNK
    wq = RET_HEADS * RET_DK // 2
    wv = RET_HEADS * RET_DV // 2
    specs = []
    for row in (row_f, row_b):
        specs += [pl.BlockSpec((c, wq), lambda b, n, row=row: (row(b, n), C_RQ // wq)),
                  pl.BlockSpec((c, wq), lambda b, n, row=row: (row(b, n), C_RQ // wq + 1)),
                  pl.BlockSpec((c, 2 * wq), lambda b, n, row=row: (row(b, n), C_RK // (2 * wq))),
                  pl.BlockSpec((c, wv), lambda b, n, row=row: (row(b, n), C_RV // wv)),
                  pl.BlockSpec((c, wv), lambda b, n, row=row: (row(b, n), C_RV // wv + 1))]
    return specs


def _ret_head(refs, h):
    q0, q1, k_ref, v0, v1 = refs
    hh = h % 2
    q = (q0, q1)[h // 2][:, hh * RET_DK:(hh + 1) * RET_DK].astype(F32)
    k = k_ref[:, h * RET_DK:(h + 1) * RET_DK].astype(F32) * (RET_DK ** -0.5)
    v = (v0, v1)[h // 2][:, hh * RET_DV:(hh + 1) * RET_DV]
    return q, k, v


def _ret_fwd(px, states0, lg, n_samp, seq, comm=None):
    c = RET_CHUNK
    nc = seq // c
    t_lat = n_samp * seq
    wo = RET_HEADS * RET_DV

    def row_f(b, n):
        return b * nc + n

    def row_b(b, n):
        return b * nc + nc - 1 - n

    def body(lg_ref, *refs):
        ins, (s0_ref, of_ref, ob_ref, st_ref, s_s) = refs[:10], refs[10:]

        @pl.when(pl.program_id(1) == 0)
        def _():
            s_s[...] = s0_ref[...]

        for d, o_ref in ((0, of_ref), (1, ob_ref)):
            for h in range(RET_HEADS):
                _, _, mask, qd, kd, gc = _decays(lg_ref[d, h], d)
                q, k, v = _ret_head(ins[5 * d:5 * d + 5], h)
                s = s_s[d, h]
                st_ref[h, d] = s.astype(BF)
                sc = _dot(q, k, 1, 1) * mask
                o_ref[:, h * RET_DV:(h + 1) * RET_DV] = (_dot(sc, v) + _dot(q * qd, s)).astype(BF)
                s_s[d, h] = s * gc + _dot(k * kd, v, 0, 0)

    return _call(
        body, [lg] + [px] * 10 + [states0], comm, name="ret_fwd", grid=(n_samp, nc),
        in_specs=[SMEM] + _ret_specs(row_f, row_b) + [
            pl.BlockSpec((None, 2, RET_HEADS, RET_DK, RET_DV), lambda b, n: (b, 0, 0, 0, 0))],
        out_specs=(pl.BlockSpec((c, wo), lambda b, n: (row_f(b, n), 0)),
                   pl.BlockSpec((c, wo), lambda b, n: (row_b(b, n), 0)),
                   pl.BlockSpec((None, RET_HEADS, 2, None, RET_DK, RET_DV), lambda b, n: (b, 0, 0, n, 0, 0))),
        out_shape=(SDS((t_lat, wo), BF), SDS((t_lat, wo), BF),
                   SDS((n_samp, RET_HEADS, 2, nc, RET_DK, RET_DV), BF)),
        scratch_shapes=[pltpu.VMEM((2, RET_HEADS, RET_DK, RET_DV), F32)],
        compiler_params=_cp(("arbitrary", "arbitrary"), 48))


def _ret_bwd(px, do, saved, lg, n_samp, seq, comm=None):
    c = RET_CHUNK
    nc = seq // c
    t_lat = n_samp * seq
    wq, wo = RET_HEADS * RET_DK, RET_HEADS * RET_DV

    def row_f(b, n):
        return b * nc + nc - 1 - n

    def row_b(b, n):
        return b * nc + n

    def body(lg_ref, *refs):
        ins = refs[:10]
        (dof_ref, dob_ref, st_ref, dqf, dkf, dvf, dqb, dkb, dvb, ds0_ref, dlg_ref, ds_s, acc_s) = refs[10:]
        n = pl.program_id(1)

        @pl.when(n == 0)
        def _():
            ds_s[...] = jnp.zeros_like(ds_s)
            acc_s[...] = jnp.zeros_like(acc_s)

        for d, (do_ref, dq_ref, dk_ref, dv_ref) in enumerate(((dof_ref, dqf, dkf, dvf), (dob_ref, dqb, dkb, dvb))):
            for h in range(RET_HEADS):
                a_i, rel, mask, qd, kd, gc = _decays(lg_ref[d, h], d)
                q, k, v = _ret_head(ins[5 * d:5 * d + 5], h)
                qb, kb, vb = q.astype(BF), k.astype(BF), v.astype(BF)
                dob = do_ref[:, h * RET_DV:(h + 1) * RET_DV].astype(BF)
                sb = st_ref[h, d]
                ds = ds_s[d, h]
                dsb = ds.astype(BF)
                raw = _dot(qb, kb, 1, 1)
                sc = raw * mask
                dsc = _dot(dob, vb, 1, 1) * mask
                dscb = dsc.astype(BF)
                x = _dot(dob, sb, 1, 1)
                y = _dot(vb, dsb, 1, 1)
                qq = q * qd
                kk = k * kd
                dq_ref[:, h * RET_DK:(h + 1) * RET_DK] = (_dot(dscb, kb) + x * qd).astype(BF)
                dk_ref[:, h * RET_DK:(h + 1) * RET_DK] = (_dot(dscb, qb, 0, 0) + y * kd).astype(BF)
                dv_ref[:, h * RET_DV:(h + 1) * RET_DV] = (_dot(sc, dob, 0, 0) + _dot(kk, dsb)).astype(BF)
                t = (_sum_all(dsc * raw * rel) + _sum_all((a_i + 1.0) * qq * x)
                     + _sum_all((c - 1.0 - a_i) * kk * y) + c * gc * _sum_all(ds * sb.astype(F32)))
                acc_s[4 * d + h:4 * d + h + 1, :] += t
                ds_s[d, h] = ds * gc + _dot(qq, dob, 0, 0)

        @pl.when(n == nc - 1)
        def _():
            ds0_ref[...] = ds_s[...]
            dlg_ref[...] = acc_s[...]

    do_spec_f = pl.BlockSpec((c, wo), lambda b, n: (row_f(b, n), 0))
    do_spec_b = pl.BlockSpec((c, wo), lambda b, n: (row_b(b, n), 0))
    dq_spec_f = pl.BlockSpec((c, wq), lambda b, n: (row_f(b, n), 0))
    dq_spec_b = pl.BlockSpec((c, wq), lambda b, n: (row_b(b, n), 0))
    return _call(
        body, [lg] + [px] * 10 + [do, do, saved], comm, name="ret_bwd", grid=(n_samp, nc),
        in_specs=[SMEM] + _ret_specs(row_f, row_b) + [
            do_spec_f, do_spec_b,
            pl.BlockSpec((None, RET_HEADS, 2, None, RET_DK, RET_DV), lambda b, n: (b, 0, 0, nc - 1 - n, 0, 0))],
        out_specs=(dq_spec_f, dq_spec_f, do_spec_f, dq_spec_b, dq_spec_b, do_spec_b,
                   pl.BlockSpec((None, 2, RET_HEADS, RET_DK, RET_DV), lambda b, n: (b, 0, 0, 0, 0)),
                   pl.BlockSpec((None, 8, 128), lambda b, n: (b, 0, 0))),
        out_shape=(SDS((t_lat, wq), BF), SDS((t_lat, wq), BF), SDS((t_lat, wo), BF),
                   SDS((t_lat, wq), BF), SDS((t_lat, wq), BF), SDS((t_lat, wo), BF),
                   SDS((n_samp, 2, RET_HEADS, RET_DK, RET_DV), F32), SDS((n_samp, 8, 128), F32)),
        scratch_shapes=[pltpu.VMEM((2, RET_HEADS, RET_DK, RET_DV), F32), pltpu.VMEM((8, 128), F32)],
        compiler_params=_cp(("arbitrary", "arbitrary"), 56))


def _combine_into(dpx, a, b, col0, scale):
    t_lat, width = a.shape
    wb = 512
    assert col0 % wb == 0 and width % wb == 0

    def body(dpx_hbm, a_ref, b_ref, o_ref):
        o_ref[...] = ((a_ref[...].astype(F32) + b_ref[...].astype(F32)) * scale).astype(BF)

    src = pl.BlockSpec((TM, wb), lambda i, j: (i, j))
    return pl.pallas_call(
        body, name="combine_into", grid=(t_lat // TM, width // wb), input_output_aliases={0: 0},
        in_specs=[ANY, src, src], out_specs=pl.BlockSpec((TM, wb), lambda i, j: (i, col0 // wb + j)),
        out_shape=SDS(dpx.shape, dpx.dtype),
        compiler_params=_cp(("parallel", "parallel")))(dpx, a, b)


def _retnorm_fwd(o_f, o_b, px):
    t_lat = o_f.shape[0]

    def body(of_ref, ob_ref, g_ref, y_ref):
        o = of_ref[...].astype(F32) + ob_ref[...].astype(F32)
        g = g_ref[...].astype(F32)
        y_ref[...] = (o * _rms(o) * (g * _sigmoid(g))).astype(BF)

    so = pl.BlockSpec((TM, RET_DV), lambda i, h: (i, h))
    return pl.pallas_call(
        body, name="retnorm_fwd", grid=(t_lat // TM, RET_HEADS),
        in_specs=[so, so, pl.BlockSpec((TM, RET_DV), lambda i, h: (i, C_RG // RET_DV + h))],
        out_specs=so,
        out_shape=SDS((t_lat, RET_HEADS * RET_DV), BF),
        compiler_params=_cp(("parallel", "parallel")))(o_f, o_b, px)


def _retnorm_bwd(dpx, dy, o_f, o_b, px):
    t_lat = o_f.shape[0]

    def body(dpx_hbm, dy_ref, of_ref, ob_ref, g_ref, do_ref, dg_ref):
        o = of_ref[...].astype(F32) + ob_ref[...].astype(F32)
        r = _rms(o)
        on = o * r
        g = g_ref[...].astype(F32)
        sg = _sigmoid(g)
        dy_ = dy_ref[...].astype(F32)
        dg_ref[...] = (dy_ * on * (sg * (1.0 + g * (1.0 - sg)))).astype(BF)
        do_ref[...] = _rms_bwd(dy_ * (g * sg), on, r).astype(BF)

    so = pl.BlockSpec((TM, RET_DV), lambda i, h: (i, h))
    gcol = pl.BlockSpec((TM, RET_DV), lambda i, h: (i, C_RG // RET_DV + h))
    return pl.pallas_call(
        body, name="retnorm_bwd", grid=(t_lat // TM, RET_HEADS), input_output_aliases={0: 1},
        in_specs=[ANY, so, so, so, gcol],
        out_specs=(so, gcol),
        out_shape=(SDS((t_lat, RET_HEADS * RET_DV), BF), SDS(dpx.shape, dpx.dtype)),
        compiler_params=_cp(("parallel", "parallel")))(dpx, dy, o_f, o_b, px)


def _norm_rope(x, w, cos, sin):
    xn = x * _rms(x) * w
    return xn * cos + _swap_pairs(xn) * sin


def _norm_rope_bwd(dy, x, w, cos, sin):
    dxn = dy * cos + _swap_pairs(dy * sin)
    r = _rms(x)
    xh = x * r
    return _rms_bwd(dxn * w, xh, r), jnp.sum(dxn * xh, axis=0, keepdims=True)


def _att_prep_q(px, cos_all, sin_all, qnw, t_lat):
    hd = ATT_HEAD_DIM
    wblk = ATT_REP * hd

    def body(x_ref, cos_ref, sin_ref, w_ref, o_ref):
        for r in range(ATT_REP):
            cols = slice(r * hd, (r + 1) * hd)
            qr = _norm_rope(x_ref[:, cols].astype(F32), w_ref[...], cos_ref[...], sin_ref[...])
            o_ref[:, cols] = (qr * (hd ** -0.5)).astype(BF)

    return pl.pallas_call(
        body, name="att_prep_q", grid=(t_lat // TM, ATT_KV_HEADS),
        in_specs=[pl.BlockSpec((TM, wblk), lambda i, g: (i, C_AQ // wblk + g)),
                  pl.BlockSpec((TM, hd), lambda i, g: (i, 0)),
                  pl.BlockSpec((TM, hd), lambda i, g: (i, 0)),
                  pl.BlockSpec((1, hd), lambda i, g: (0, 0))],
        out_specs=pl.BlockSpec((TM, wblk), lambda i, g: (i, g)),
        out_shape=SDS((t_lat, ATT_HEADS * hd), BF),
        compiler_params=_cp(("parallel", "parallel")))(px, cos_all, sin_all, qnw)


def _att_prep_kv(px, cos_all, sin_all, knw):
    rows = px.shape[0]
    hd = ATT_HEAD_DIM
    kvw = ATT_KV_HEADS * hd

    def body(x_ref, cos_ref, sin_ref, w_ref, k_ref, v_ref):
        for g in range(ATT_KV_HEADS):
            cols = slice(g * hd, (g + 1) * hd)
            k_ref[:, cols] = _norm_rope(x_ref[:, cols].astype(F32), w_ref[...], cos_ref[...],
                                        sin_ref[...]).astype(BF)
        v_ref[...] = x_ref[:, kvw:].astype(BF)

    return pl.pallas_call(
        body, name="att_prep_kv", grid=(rows // TM,),
        in_specs=[pl.BlockSpec((TM, 2 * kvw), lambda i: (i, C_AK // (2 * kvw))),
                  pl.BlockSpec((TM, hd), lambda i: (i, 0)),
                  pl.BlockSpec((TM, hd), lambda i: (i, 0)),
                  pl.BlockSpec((1, hd), lambda i: (0, 0))],
        out_specs=(pl.BlockSpec((TM, kvw), lambda i: (i, 0)), pl.BlockSpec((TM, kvw), lambda i: (i, 0))),
        out_shape=(SDS((rows, kvw), BF), SDS((rows, kvw), BF)),
        compiler_params=_cp(("parallel",)))(px, cos_all, sin_all, knw)


def _att_kv_bwd(dpx, dkl, dkc, dvl, dvc, px, cos_all, sin_all, knw):
    rows = px.shape[0]
    hd = ATT_HEAD_DIM
    kvw = ATT_KV_HEADS * hd
    n_lat = dkl.shape[0] // TM
    assert dkc.shape[0] == TM

    def body(dpx_hbm, dkl_ref, dkc_ref, dvl_ref, dvc_ref, x_ref, cos_ref, sin_ref, w_ref, o_ref, gw_ref):
        i = pl.program_id(0)

        @pl.when(i == 0)
        def _():
            gw_ref[...] = jnp.zeros_like(gw_ref)

        is_lat = i < n_lat
        dk = jnp.where(is_lat, dkl_ref[...], dkc_ref[...])
        dv = jnp.where(is_lat, dvl_ref[...], dvc_ref[...])
        for g in range(ATT_KV_HEADS):
            cols = slice(g * hd, (g + 1) * hd)
            dx, gw = _norm_rope_bwd(dk[:, cols], x_ref[:, cols].astype(F32), w_ref[...], cos_ref[...], sin_ref[...])
            o_ref[:, cols] = dx.astype(BF)
            gw_ref[...] += gw
        o_ref[:, kvw:] = dv.astype(BF)

    lat = pl.BlockSpec((TM, kvw), lambda i: (jnp.minimum(i, n_lat - 1), 0))
    ctx = pl.BlockSpec((TM, kvw), lambda i: (0, 0))
    kvcol = pl.BlockSpec((TM, 2 * kvw), lambda i: (i, C_AK // (2 * kvw)))
    return pl.pallas_call(
        body, name="att_kv_bwd", grid=(rows // TM,), input_output_aliases={0: 0},
        in_specs=[ANY, lat, ctx, lat, ctx, kvcol,
                  pl.BlockSpec((TM, hd), lambda i: (i, 0)),
                  pl.BlockSpec((TM, hd), lambda i: (i, 0)),
                  pl.BlockSpec((1, hd), lambda i: (0, 0))],
        out_specs=(kvcol, pl.BlockSpec((1, hd), lambda i: (0, 0))),
        out_shape=(SDS(dpx.shape, dpx.dtype), SDS((1, hd), F32)),
        compiler_params=_cp(("arbitrary",)))(dpx, dkl, dkc, dvl, dvc, px, cos_all, sin_all, knw)


def _stack_heads(ref_or_val):
    hd = ATT_HEAD_DIM
    return jnp.concatenate([ref_or_val[:, r * hd:(r + 1) * hd] for r in range(ATT_REP)], axis=0)


def _att_scores(q, kl, kc):
    sl = _dot(q, kl, 1, 1)
    sc = _dot(q, kc, 1, 1)
    m = jnp.maximum(jnp.max(sl, axis=-1, keepdims=True), jnp.max(sc, axis=-1, keepdims=True))
    el = jnp.exp(sl - m)
    ec = jnp.exp(sc - m)
    denom = jnp.sum(el, axis=-1, keepdims=True) + jnp.sum(ec, axis=-1, keepdims=True)
    return el, ec, denom, m


def _att_fwd(qn, kn, vn, px, n_samp, seq, lc):
    hd = ATT_HEAD_DIM
    tq = 128
    nq = seq // tq
    wblk = ATT_REP * hd
    cb = n_samp * seq // lc
    t_lat = n_samp * seq

    def body(q_ref, kl_ref, kc_ref, vl_ref, vc_ref, g_ref, y_ref, o_ref, lse_ref):
        lane = lax.broadcasted_iota(jnp.int32, (tq, hd), 1)
        lse = jnp.zeros((tq, hd), F32)
        for r in range(ATT_REP):
            cols = slice(r * hd, (r + 1) * hd)
            el, ec, denom, m = _att_scores(q_ref[:, cols], kl_ref[...], kc_ref[...])
            o = (_dot(el, vl_ref[...]) + _dot(ec, vc_ref[...])) / denom
            g = g_ref[:, cols].astype(F32)
            o_ref[:, cols] = o.astype(BF)
            y_ref[:, cols] = (o * (g * _sigmoid(g))).astype(BF)
            lse = jnp.where(lane == r, m + jnp.log(denom), lse)
        lse_ref[...] = lse

    return pl.pallas_call(
        body, name="att_fwd", grid=(n_samp, ATT_KV_HEADS, nq),
        in_specs=[pl.BlockSpec((tq, wblk), lambda b, g, i: (b * nq + i, g)),
                  pl.BlockSpec((seq, hd), lambda b, g, i: (b, g)),
                  pl.BlockSpec((lc, hd), lambda b, g, i: (cb + b, g)),
                  pl.BlockSpec((seq, hd), lambda b, g, i: (b, g)),
                  pl.BlockSpec((lc, hd), lambda b, g, i: (cb + b, g)),
                  pl.BlockSpec((tq, wblk), lambda b, g, i: (b * nq + i, C_AG // wblk + g))],
        out_specs=(pl.BlockSpec((tq, wblk), lambda b, g, i: (b * nq + i, g)),
                   pl.BlockSpec((tq, wblk), lambda b, g, i: (b * nq + i, g)),
                   pl.BlockSpec((tq, hd), lambda b, g, i: (b * nq + i, g))),
        out_shape=(SDS((t_lat, ATT_HEADS * hd), BF), SDS((t_lat, ATT_HEADS * hd), BF),
                   SDS((t_lat, ATT_KV_HEADS * hd), F32)),
        compiler_params=_cp(("parallel", "parallel", "parallel"), 48))(qn, kn, kn, vn, vn, px)


def _att_gate_bwd(dpx, dy_att, o_att, px):
    t_lat = dy_att.shape[0]
    wblk = ATT_REP * ATT_HEAD_DIM

    def body(dpx_hbm, dy_ref, o_ref, g_ref, out_ref):
        g = g_ref[...].astype(F32)
        sg = _sigmoid(g)
        out_ref[...] = (dy_ref[...].astype(F32) * o_ref[...].astype(F32) * (sg * (1.0 + g * (1.0 - sg)))).astype(BF)

    blk = pl.BlockSpec((TM, wblk), lambda i, j: (i, j))
    gcol = pl.BlockSpec((TM, wblk), lambda i, j: (i, C_AG // wblk + j))
    return pl.pallas_call(
        body, name="att_gate_bwd", grid=(t_lat // TM, ATT_KV_HEADS),
        in_specs=[ANY, blk, blk, gcol], out_specs=gcol, out_shape=SDS(dpx.shape, dpx.dtype),
        input_output_aliases={0: 0},
        compiler_params=_cp(("parallel", "parallel")))(dpx, dy_att, o_att, px)


def _att_bwd(dpx, qn, kn, vn, px, o_att, lse, dy_att, cos_all, sin_all, qnw, n_samp, seq, lc, comm=None):
    hd = ATT_HEAD_DIM
    tq = 128
    nq = seq // tq
    wblk = ATT_REP * hd
    cb = n_samp * seq // lc
    t_lat = n_samp * seq
    kvw = ATT_KV_HEADS * hd
    scale = hd ** -0.5

    def body(dpx_hbm, q_ref, kl_ref, kc_ref, vl_ref, vc_ref, g_ref, o_ref, dy_ref, x_ref, cos_ref, sin_ref, w_ref,
             lse_ref, dq_ref, dkl_ref, dkc_ref, dvl_ref, dvc_ref, gw_ref, akl, akc, avl, avc, aw):
        i = pl.program_id(2)

        @pl.when(i == 0)
        def _():
            akl[...] = jnp.zeros_like(akl)
            akc[...] = jnp.zeros_like(akc)
            avl[...] = jnp.zeros_like(avl)
            avc[...] = jnp.zeros_like(avc)
            aw[...] = jnp.zeros_like(aw)

        dobs, pls, pcs, dsls, dscs = [], [], [], [], []
        for r in range(ATT_REP):
            cols = slice(r * hd, (r + 1) * hd)
            g = g_ref[:, cols].astype(F32)
            sg = _sigmoid(g)
            dy = dy_ref[:, cols].astype(F32)
            do = dy * (g * sg)
            delta = jnp.sum(do * o_ref[:, cols].astype(F32), axis=-1, keepdims=True)
            lse = lse_ref[:, r:r + 1]
            p_l = jnp.exp(_dot(q_ref[:, cols], kl_ref[...], 1, 1) - lse).astype(BF)
            p_c = jnp.exp(_dot(q_ref[:, cols], kc_ref[...], 1, 1) - lse).astype(BF)
            dob = do.astype(BF)
            ds_l = (p_l * (_dot(dob, vl_ref[...], 1, 1) - delta)).astype(BF)
            ds_c = (p_c * (_dot(dob, vc_ref[...], 1, 1) - delta)).astype(BF)
            dq = (_dot(ds_l, kl_ref[...]) + _dot(ds_c, kc_ref[...])) * scale
            dx, gw = _norm_rope_bwd(dq, x_ref[:, cols].astype(F32), w_ref[...], cos_ref[...], sin_ref[...])
            dq_ref[:, cols] = dx.astype(BF)
            aw[...] += gw
            dobs.append(dob)
            pls.append(p_l)
            pcs.append(p_c)
            dsls.append(ds_l)
            dscs.append(ds_c)
        do4 = jnp.concatenate(dobs, axis=0)
        q4 = _stack_heads(q_ref)
        avl[...] += _dot(jnp.concatenate(pls, axis=0), do4, 0, 0)
        avc[...] += _dot(jnp.concatenate(pcs, axis=0), do4, 0, 0)
        akl[...] += _dot(jnp.concatenate(dsls, axis=0), q4, 0, 0)
        akc[...] += _dot(jnp.concatenate(dscs, axis=0), q4, 0, 0)

        @pl.when(i == nq - 1)
        def _():
            dkl_ref[...] = akl[...]
            dkc_ref[...] = akc[...]
            dvl_ref[...] = avl[...]
            dvc_ref[...] = avc[...]
            gw_ref[...] = aw[...]

    return _call(
        body, [dpx, qn, kn, kn, vn, vn, px, o_att, dy_att, px, cos_all, sin_all, qnw, lse], comm,
        name="att_bwd", grid=(n_samp, ATT_KV_HEADS, nq), aliases={0: 0},
        in_specs=[ANY,
                  pl.BlockSpec((tq, wblk), lambda b, g, i: (b * nq + i, g)),
                  pl.BlockSpec((seq, hd), lambda b, g, i: (b, g)),
                  pl.BlockSpec((lc, hd), lambda b, g, i: (cb + b, g)),
                  pl.BlockSpec((seq, hd), lambda b, g, i: (b, g)),
                  pl.BlockSpec((lc, hd), lambda b, g, i: (cb + b, g)),
                  pl.BlockSpec((tq, wblk), lambda b, g, i: (b * nq + i, C_AG // wblk + g)),
                  pl.BlockSpec((tq, wblk), lambda b, g, i: (b * nq + i, g)),
                  pl.BlockSpec((tq, wblk), lambda b, g, i: (b * nq + i, g)),
                  pl.BlockSpec((tq, wblk), lambda b, g, i: (b * nq + i, C_AQ // wblk + g)),
                  pl.BlockSpec((tq, hd), lambda b, g, i: (b * nq + i, 0)),
                  pl.BlockSpec((tq, hd), lambda b, g, i: (b * nq + i, 0)),
                  pl.BlockSpec((1, hd), lambda b, g, i: (0, 0)),
                  pl.BlockSpec((tq, hd), lambda b, g, i: (b * nq + i, g))],
        out_specs=(pl.BlockSpec((tq, wblk), lambda b, g, i: (b * nq + i, C_AQ // wblk + g)),
                   pl.BlockSpec((seq, hd), lambda b, g, i: (b, g)),
                   pl.BlockSpec((lc, hd), lambda b, g, i: (b, g)),
                   pl.BlockSpec((seq, hd), lambda b, g, i: (b, g)),
                   pl.BlockSpec((lc, hd), lambda b, g, i: (b, g)),
                   pl.BlockSpec((None, None, 1, hd), lambda b, g, i: (b, g, 0, 0))),
        out_shape=(SDS(dpx.shape, dpx.dtype),
                   SDS((t_lat, kvw), F32), SDS((n_samp * lc, kvw), F32),
                   SDS((t_lat, kvw), F32), SDS((n_samp * lc, kvw), F32),
                   SDS((n_samp, ATT_KV_HEADS, 1, hd), F32)),
        scratch_shapes=[pltpu.VMEM((seq, hd), F32), pltpu.VMEM((lc, hd), F32),
                        pltpu.VMEM((seq, hd), F32), pltpu.VMEM((lc, hd), F32), pltpu.VMEM((1, hd), F32)],
        compiler_params=_cp(("arbitrary", "arbitrary", "arbitrary"), 56))


def _merge(x_lat, target, y_ret, y_att, px, gate3, w_o_ret, w_o_att, w_out, tiles_per_sample):
    t_lat = x_lat.shape[0]
    tm = 256
    n_t = t_lat // tm
    per = tiles_per_sample * (TM // tm)
    d = D_MODEL
    rv = RET_HEADS * RET_DV
    n_samp = gate3.shape[0] - 1

    def body(x_ref, t_ref, yr_ref, ya_ref, mr0, mr1, ma0, ma1, gt_ref, wor_ref, woa_ref, wout_ref,
             gx_ref, dyr_ref, dya_ref, dpx_hbm, loss_ref, dgt_ref, gwor_hbm, gwoa_hbm, gwout_hbm,
             aor, aoa, aout, dmg_ref, dmg_sem):
        i = pl.program_id(0)

        def dmg_copy(step):
            rows = pl.ds(pl.multiple_of(step * tm, tm), tm)
            return pltpu.make_async_copy(dmg_ref, dpx_hbm.at[rows, pl.ds(C_MR, 2 * d)], dmg_sem)

        @pl.when(i == 0)
        def _():
            aor[...] = jnp.zeros_like(aor)
            aoa[...] = jnp.zeros_like(aoa)
            aout[...] = jnp.zeros_like(aout)
            loss_ref[...] = jnp.zeros_like(loss_ref)

        @pl.when(i % per == 0)
        def _():
            dgt_ref[...] = jnp.zeros_like(dgt_ref)

        yr = yr_ref[...]
        ya = ya_ref[...]
        a = jnp.dot(yr, wor_ref[...], preferred_element_type=F32)
        b = jnp.dot(ya, woa_ref[...], preferred_element_type=F32)
        sr = _sigmoid(jnp.concatenate([mr0[...], mr1[...]], axis=1).astype(F32))
        sa = _sigmoid(jnp.concatenate([ma0[...], ma1[...]], axis=1).astype(F32))
        yb = (sr * a + sa * b).astype(BF)
        out = jnp.dot(yb, wout_ref[...], preferred_element_type=F32)
        gate = gt_ref[...]
        err = x_ref[...] + gate * out - t_ref[...]
        loss_ref[...] += 0.5 * _sum_all(err * err) * (1.0 / d)
        dy_tok = err * (1.0 / d)
        gx_ref[...] = dy_tok
        dgt_ref[...] += jnp.sum(dy_tok * out, axis=0, keepdims=True)
        dout = (dy_tok * gate).astype(BF)
        aout[...] += _dot(yb, dout, 0, 0)
        dyy = _dot(dout, wout_ref[...], 1, 1)
        da = (dyy * sr).astype(BF)
        db = (dyy * sa).astype(BF)
        @pl.when(i > 0)
        def _():
            dmg_copy(i - 1).wait()

        dmg_ref[:, :d] = (dyy * a * (sr * (1.0 - sr))).astype(BF)
        dmg_ref[:, d:] = (dyy * b * (sa * (1.0 - sa))).astype(BF)
        dmg_copy(i).start()
        aor[...] += _dot(yr, da, 0, 0)
        aoa[...] += _dot(ya, db, 0, 0)
        dyr_ref[...] = _dot(da, wor_ref[...], 1, 1).astype(BF)
        dya_ref[...] = _dot(db, woa_ref[...], 1, 1).astype(BF)

        @pl.when(i == n_t - 1)
        def _():
            dmg_copy(i).wait()
            pltpu.sync_copy(aor, gwor_hbm)
            pltpu.sync_copy(aoa, gwoa_hbm)
            pltpu.sync_copy(aout, gwout_hbm)

    half = d // 2
    return pl.pallas_call(
        body, name="merge", grid=(n_t,),
        in_specs=[pl.BlockSpec((tm, d), lambda i: (i, 0)),
                  pl.BlockSpec((tm, d), lambda i: (i, 0)),
                  pl.BlockSpec((tm, rv), lambda i: (i, 0)),
                  pl.BlockSpec((tm, d), lambda i: (i, 0)),
                  pl.BlockSpec((tm, half), lambda i: (i, C_MR // half)),
                  pl.BlockSpec((tm, half), lambda i: (i, C_MR // half + 1)),
                  pl.BlockSpec((tm, half), lambda i: (i, C_MA // half)),
                  pl.BlockSpec((tm, half), lambda i: (i, C_MA // half + 1)),
                  pl.BlockSpec((None, 1, d), lambda i: (i // per, 0, 0)),
                  pl.BlockSpec((rv, d), lambda i: (0, 0)),
                  pl.BlockSpec((d, d), lambda i: (0, 0)),
                  pl.BlockSpec((d, d), lambda i: (0, 0))],
        out_specs=(pl.BlockSpec((tm, d), lambda i: (i, 0)),
                   pl.BlockSpec((tm, rv), lambda i: (i, 0)),
                   pl.BlockSpec((tm, d), lambda i: (i, 0)),
                   ANY,
                   pl.BlockSpec((8, 128), lambda i: (0, 0)),
                   pl.BlockSpec((None, 1, d), lambda i: (i // per, 0, 0)),
                   ANY, ANY, ANY),
        out_shape=(SDS((t_lat, d), F32), SDS((t_lat, rv), BF), SDS((t_lat, d), BF),
                   SDS((px.shape[0], IN_COLS), BF),
                   SDS((8, 128), F32), SDS((n_samp, 1, d), F32),
                   SDS((rv, d), F32), SDS((d, d), F32), SDS((d, d), F32)),
        scratch_shapes=[pltpu.VMEM((rv, d), F32), pltpu.VMEM((d, d), F32), pltpu.VMEM((d, d), F32),
                        pltpu.VMEM((tm, 2 * d), BF), pltpu.SemaphoreType.DMA],
        compiler_params=_cp(("arbitrary",), 56))(
            x_lat, target, y_ret, y_att, px, px, px, px, gate3, w_o_ret, w_o_att, w_out)


def _place():
    x, y, c = lax.axis_index("x"), lax.axis_index("y"), lax.axis_index("c")
    chips = [(1 - x, y), (x, 1 - y), (1 - x, 1 - y)]
    return x, y, c, chips


def _remote(src, dst, send_sem, recv_sem, to):
    return pltpu.make_async_remote_copy(src_ref=src, dst_ref=dst, send_sem=send_sem, recv_sem=recv_sem,
                                        device_id=to, device_id_type=MESH)


def _place_ids():
    x, y, c = lax.axis_index("x"), lax.axis_index("y"), lax.axis_index("c")
    me = 2 * x + y
    return jnp.stack([x, y, c, me, me, 2 * (1 - x) + y, 2 * x + 1 - y, 2 * (1 - x) + 1 - y]).astype(jnp.int32)


def _ag_comm(bufs, rels, arg_index=None):
    n, m = len(bufs), len(rels)

    def half(ref, s, which):
        h = ref.shape[1] // 2
        return ref.at[s, pl.ds(which * h, h), :]

    def ici(ins, outs, ssem, rsem, base):
        x, y, c, chips = _place()
        sends, recvs = [], []
        for a in range(n):
            for jj, j in enumerate(rels):
                k, chip = base + a * m + jj, chips[j]
                mine, theirs = half(outs[a], 2 * x + y, c), half(outs[a], 2 * chip[0] + chip[1], c)
                sends.append(_remote(mine, mine, ssem.at[k], rsem.at[k], (*chip, c)))
                recvs.append(_remote(theirs, theirs, ssem.at[k], rsem.at[k], (*chip, c)))
        return sends, recvs

    def d2d(ins, outs, ssem, rsem, base):
        x, y, c, chips = _place()
        sends, recvs = [], []
        for a in range(n):
            for jj, j in enumerate(rels):
                k, s = base + (n + a) * m + jj, 2 * chips[j][0] + chips[j][1]
                sends.append(_remote(half(outs[a], s, c), half(outs[a], s, c), ssem.at[k], rsem.at[k], (x, y, 1 - c)))
                recvs.append(_remote(half(outs[a], s, 1 - c), half(outs[a], s, 1 - c), ssem.at[k], rsem.at[k],
                                     (x, y, 1 - c)))
        return sends, recvs

    shapes = tuple(SDS(b.shape, b.dtype) for b in bufs)
    if arg_index is not None:
        return _Comm("all_gather", (), shapes, {}, 2 * n * m, (ici, d2d), ((arg_index, 0),))
    return _Comm("all_gather", tuple(bufs), shapes, {a: a for a in range(n)}, 2 * n * m, (ici, d2d))


def _swap_comm(grads):
    n = len(grads)

    def phase(ins, outs, ssem, rsem, base):
        x, y, c, _ = _place()
        sends = []
        for a in range(n):
            h = ins[a].shape[1] // 2
            sends.append(_remote(ins[a].at[:, pl.ds((1 - c) * h, h), :], outs[a], ssem.at[base + a],
                                 rsem.at[base + a], (x, y, 1 - c)))
        return sends, sends

    return _Comm("swap_halves", tuple(grads),
                 tuple(SDS((g.shape[0], g.shape[1] // 2, g.shape[2]), g.dtype) for g in grads), {}, n, (phase,))


def _exchange_comm(parts):
    n = len(parts)

    def phase(ins, outs, ssem, rsem, base):
        x, y, c, chips = _place()
        sends = []
        for a in range(n):
            for j, chip in enumerate(chips):
                k = base + 3 * a + j
                sends.append(_remote(ins[a].at[2 * chip[0] + chip[1]], outs[a].at[j], ssem.at[k], rsem.at[k],
                                     (*chip, c)))
        return sends, sends

    return _Comm("exchange_shards", tuple(parts), tuple(SDS((3,) + p.shape[1:], p.dtype) for p in parts), {}, 3 * n,
                 (phase,))


def _join_comm(bufs, n_parts=1):
    n = len(bufs)

    def phase(ins, outs, ssem, rsem, base):
        x, y, c, _ = _place()
        sends, recvs = [], []
        for a in range(n):
            h = outs[a].shape[0] // (2 * n_parts)
            for p in range(n_parts):
                k = base + a * n_parts + p
                mine = outs[a].at[pl.ds((2 * p + c) * h, h), :]
                other = outs[a].at[pl.ds((2 * p + 1 - c) * h, h), :]
                sends.append(_remote(mine, mine, ssem.at[k], rsem.at[k], (x, y, 1 - c)))
                recvs.append(_remote(other, other, ssem.at[k], rsem.at[k], (x, y, 1 - c)))
        return sends, recvs

    return _Comm("join_halves", tuple(bufs), tuple(SDS(b.shape, b.dtype) for b in bufs), {a: a for a in range(n)},
                 n * n_parts, (phase,))


def _cast_place(w, ids):
    rows, cols = w.shape
    tr = min(rows, 256)

    def body(ids_ref, w_ref, o_ref):
        o_ref[...] = w_ref[...].astype(BF)

    return pl.pallas_call(
        body, name="cast_place",
        grid_spec=pltpu.PrefetchScalarGridSpec(
            num_scalar_prefetch=1, grid=(rows // tr,),
            in_specs=[pl.BlockSpec((tr, cols), lambda i, ids_ref: (i, 0))],
            out_specs=pl.BlockSpec((None, tr, cols), lambda i, ids_ref: (ids_ref[3], i, 0))),
        out_shape=SDS((N_SHARD, rows, cols), BF),
        compiler_params=_cp(("parallel",), 40))(ids, w)


def _all_gather_weights(bufs):
    n = len(bufs)

    def body(*refs):
        outs = refs[n:2 * n]
        send_sems, recv_sems = refs[2 * n:]
        x, y, c, chips = _place()
        sibling = (x, y, 1 - c)
        me = 2 * x + y

        def half(ref, s, which):
            h = ref.shape[1] // 2
            return ref.at[s, pl.ds(which * h, h), :]

        first = []
        for a in range(n):
            for j, chip in enumerate(chips):
                k = a * 3 + j
                win = half(outs[a], me, c)
                first.append(_remote(win, win, send_sems.at[k], recv_sems.at[k], (*chip, c)))
        for cp in first:
            cp.start()
        passed = []
        for a in range(n):
            for j, chip in enumerate(chips):
                k = a * 3 + j
                win = half(outs[a], 2 * chip[0] + chip[1], c)
                _remote(win, win, send_sems.at[k], recv_sems.at[k], (*chip, c)).wait_recv()
                fw = _remote(win, win, send_sems.at[3 * n + k], recv_sems.at[3 * n + k], sibling)
                fw.start()
                passed.append(fw)
        for a in range(n):
            for j, chip in enumerate(chips):
                k = a * 3 + j
                win = half(outs[a], 2 * chip[0] + chip[1], 1 - c)
                _remote(win, win, send_sems.at[3 * n + k], recv_sems.at[3 * n + k], sibling).wait_recv()
        for cp in first + passed:
            cp.wait_send()

    return pl.pallas_call(
        body, name="all_gather_weights",
        in_specs=[ANY] * n, out_specs=tuple([ANY] * n),
        out_shape=tuple(SDS(b.shape, b.dtype) for b in bufs),
        input_output_aliases={a: a for a in range(n)},
        scratch_shapes=[pltpu.SemaphoreType.DMA((6 * n,)), pltpu.SemaphoreType.DMA((6 * n,))],
        compiler_params=_cp(has_side_effects=True))(*bufs)


def _swap_halves(grads):
    n = len(grads)

    def body(*refs):
        ins, outs = refs[:n], refs[n:2 * n]
        send_sems, recv_sems = refs[2 * n:]
        x, y, c, _ = _place()
        sibling = (x, y, 1 - c)

        def half(ref, which):
            h = ref.shape[1] // 2
            return ref.at[:, pl.ds(which * h, h), :]

        sends = [_remote(half(ins[a], 1 - c), outs[a], send_sems.at[a], recv_sems.at[a], sibling)
                 for a in range(n)]
        for cp in sends:
            cp.start()
        for cp in sends:
            cp.wait_recv()
        for cp in sends:
            cp.wait_send()

    return pl.pallas_call(
        body, name="swap_halves",
        in_specs=[ANY] * n, out_specs=tuple([ANY] * n),
        out_shape=tuple(SDS((g.shape[0], g.shape[1] // 2, g.shape[2]), g.dtype) for g in grads),
        scratch_shapes=[pltpu.SemaphoreType.DMA((n,)), pltpu.SemaphoreType.DMA((n,))],
        compiler_params=_cp(has_side_effects=True))(*grads)


def _chip_sum(g, p, ids):
    n_s, rows, cols = g.shape
    h = rows // 2
    tr = min(h, 256)
    nb = h // tr

    def body(ids_ref, g_ref, p_ref, o_ref, o16_ref):
        t = g_ref[...] + p_ref[...]
        o_ref[...] = t
        o16_ref[...] = t.astype(BF)

    out_spec = pl.BlockSpec((None, tr, cols), lambda s, i, ids_ref: (s, i, 0))
    return pl.pallas_call(
        body, name="chip_sum",
        grid_spec=pltpu.PrefetchScalarGridSpec(
            num_scalar_prefetch=1, grid=(n_s, nb),
            in_specs=[pl.BlockSpec((None, tr, cols), lambda s, i, ids_ref: (s, ids_ref[2] * nb + i, 0)),
                      pl.BlockSpec((None, tr, cols), lambda s, i, ids_ref: (s, i, 0))],
            out_specs=(out_spec, out_spec)),
        out_shape=(SDS((n_s, h, cols), g.dtype), SDS((n_s, h, cols), BF)),
        compiler_params=_cp(("parallel", "parallel"), 40))(ids, g, p)


def _exchange_shards(parts):
    n = len(parts)

    def body(*refs):
        ins, outs = refs[:n], refs[n:2 * n]
        send_sems, recv_sems = refs[2 * n:]
        x, y, c, chips = _place()
        sends = []
        for a in range(n):
            for j, chip in enumerate(chips):
                k = a * 3 + j
                sends.append(_remote(ins[a].at[2 * chip[0] + chip[1]], outs[a].at[j],
                                     send_sems.at[k], recv_sems.at[k], (*chip, c)))
        for cp in sends:
            cp.start()
        for cp in sends:
            cp.wait_recv()
        for cp in sends:
            cp.wait_send()

    return pl.pallas_call(
        body, name="exchange_shards",
        in_specs=[ANY] * n, out_specs=tuple([ANY] * n),
        out_shape=tuple(SDS((3,) + p.shape[1:], p.dtype) for p in parts),
        scratch_shapes=[pltpu.SemaphoreType.DMA((3 * n,)), pltpu.SemaphoreType.DMA((3 * n,))],
        compiler_params=_cp(has_side_effects=True))(*parts)


def _shard_sum(t, q, ids, part=0, n_parts=1, buf=None):
    _, h, cols = t.shape
    tr = min(h, 256)
    nb = h // tr

    def body(ids_ref, t_ref, q_ref, *rest):
        rest[-1][...] = ((t_ref[...] + q_ref[0].astype(F32)) + q_ref[1].astype(F32)) + q_ref[2].astype(F32)

    args, in_specs, aliases = [t, q], [
        pl.BlockSpec((None, tr, cols), lambda i, ids_ref: (ids_ref[3], i, 0)),
        pl.BlockSpec((3, tr, cols), lambda i, ids_ref: (0, i, 0))], None
    if buf is not None:
        args, in_specs, aliases = args + [buf], in_specs + [ANY], {2: 0}
    return _call(body, args, None, name="shard_sum", grid=(nb,), in_specs=in_specs,
                 out_specs=pl.BlockSpec((tr, cols), lambda i, ids_ref: ((2 * part + ids_ref[2]) * nb + i, 0)),
                 out_shape=SDS((2 * h * n_parts, cols), t.dtype), aliases=aliases, prefetch=ids,
                 compiler_params=_cp(("parallel",), 40))


def _join_halves(bufs):
    n = len(bufs)

    def body(*refs):
        outs = refs[n:2 * n]
        send_sems, recv_sems = refs[2 * n:]
        x, y, c, _ = _place()
        sibling = (x, y, 1 - c)

        def win(ref, which):
            h = ref.shape[0] // 2
            return ref.at[pl.ds(which * h, h), :]

        sends = [_remote(win(outs[a], c), win(outs[a], c), send_sems.at[a], recv_sems.at[a], sibling)
                 for a in range(n)]
        for cp in sends:
            cp.start()
        for a in range(n):
            other = win(outs[a], 1 - c)
            _remote(other, other, send_sems.at[a], recv_sems.at[a], sibling).wait_recv()
        for cp in sends:
            cp.wait_send()

    return pl.pallas_call(
        body, name="join_halves",
        in_specs=[ANY] * n, out_specs=tuple([ANY] * n),
        out_shape=tuple(SDS(b.shape, b.dtype) for b in bufs),
        input_output_aliases={a: a for a in range(n)},
        scratch_shapes=[pltpu.SemaphoreType.DMA((n,)), pltpu.SemaphoreType.DMA((n,))],
        compiler_params=_cp(has_side_effects=True))(*bufs)


def _all_reduce_small(block):
    rows, cols = block.shape
    n_dev = 8

    def body(x_ref, o_ref, buf, send_sems, recv_sems, local_sem):
        x, y, c, chips = _place()
        me, sibling = (x, y, c), (x, y, 1 - c)

        def slot(px_, py_, pc_):
            return buf.at[4 * px_ + 2 * py_ + pc_]

        def copy(k, who, to, src=None):
            return _remote(slot(*who) if src is None else src, slot(*who), send_sems.at[k], recv_sems.at[k], to)

        mine = pltpu.make_async_copy(x_ref, slot(*me), local_sem)
        mine.start()
        first = [copy(0, me, sibling, src=x_ref)]
        first += [copy(1 + j, me, (*chip, c), src=x_ref) for j, chip in enumerate(chips)]
        for cp in first:
            cp.start()
        passed = [copy(4 + j, (*chip, c), sibling) for j, chip in enumerate(chips)]
        for j, chip in enumerate(chips):
            copy(1 + j, (*chip, c), me).wait_recv()
            passed[j].start()
        copy(0, sibling, me).wait_recv()
        for j, chip in enumerate(chips):
            copy(4 + j, (*chip, 1 - c), me).wait_recv()
        for cp in first + passed:
            cp.wait_send()
        mine.wait()
        acc = buf[0]
        for s in range(1, n_dev):
            acc = acc + buf[s]
        o_ref[...] = acc

    return pl.pallas_call(
        body, name="all_reduce_small",
        in_specs=[pl.BlockSpec(memory_space=pltpu.VMEM)],
        out_specs=pl.BlockSpec(memory_space=pltpu.VMEM),
        out_shape=SDS((rows, cols), F32),
        scratch_shapes=[pltpu.VMEM((n_dev, rows, cols), F32), pltpu.SemaphoreType.DMA((7,)),
                        pltpu.SemaphoreType.DMA((7,)), pltpu.SemaphoreType.DMA],
        compiler_params=_cp(has_side_effects=True))(block)


def _adam_math(w, g, m, v):
    m = ADAM_B1 * m + (1.0 - ADAM_B1) * g
    v = ADAM_B2 * v + (1.0 - ADAM_B2) * (g * g)
    m_hat = m / (1.0 - ADAM_B1 ** ADAM_STEP)
    v_hat = v / (1.0 - ADAM_B2 ** ADAM_STEP)
    delta = -ADAM_LR * (m_hat / (jnp.sqrt(v_hat) + ADAM_EPS) + ADAM_WD * w)
    return delta, m, v


def _adamw(w, g, m, v):
    rows, cols = w.shape
    tr = min(rows, 256)

    def body(w_ref, g_ref, m_ref, v_ref, d_ref, nm_ref, nv_ref):
        d_ref[...], nm_ref[...], nv_ref[...] = _adam_math(w_ref[...], g_ref[...], m_ref[...], v_ref[...])

    spec = pl.BlockSpec((tr, cols), lambda i: (i, 0))
    return pl.pallas_call(
        body, name="adamw", grid=(rows // tr,), in_specs=[spec] * 4, out_specs=(spec,) * 3,
        out_shape=(SDS(w.shape, F32),) * 3, compiler_params=_cp(("parallel",), 40))(w, g, m, v)


def _adamw_small(w, g, m, v):
    def body(w_ref, g_ref, m_ref, v_ref, go_ref, d_ref, nm_ref, nv_ref):
        w = w_ref[...]
        g = g_ref[...]
        sub = lax.broadcasted_iota(jnp.int32, w.shape, 0)
        lane = lax.broadcasted_iota(jnp.int32, w.shape, 1)
        is_ret = jnp.logical_and(sub == 5, lane < 2 * RET_HEADS)
        u = jnp.exp(jnp.where(is_ret, w, -1.0) * jnp.log(2.0))
        g = jnp.where(is_ret, g * (-u * jnp.log(2.0) / (1.0 - u)), g)
        go_ref[...] = g
        d_ref[...], nm_ref[...], nv_ref[...] = _adam_math(w, g, m_ref[...], v_ref[...])

    return pl.pallas_call(body, name="adamw_small", out_shape=(SDS(w.shape, F32),) * 4)(w, g, m, v)


def _rope_tables(seq, n_samp, n_ctx_rows):
    rows = seq // GRID_W
    row = jnp.repeat(jnp.arange(rows, dtype=F32), GRID_W)
    col = jnp.tile(jnp.arange(GRID_W, dtype=F32), rows)
    half = ATT_HEAD_DIM // 2
    freqs = ROPE_THETA ** (-jnp.arange(0, half, 2, dtype=F32) / half)
    ang = jnp.concatenate([row[:, None] * freqs, col[:, None] * freqs], axis=-1)
    cos, sin = jnp.cos(ang), jnp.sin(ang)
    cos_f = jnp.repeat(cos, 2, axis=1)
    sin_s = jnp.stack([-sin, sin], axis=-1).reshape(seq, ATT_HEAD_DIM)
    cos_all = jnp.concatenate([jnp.tile(cos_f, (n_samp, 1)), jnp.ones((n_ctx_rows, ATT_HEAD_DIM), F32)], axis=0)
    sin_all = jnp.concatenate([jnp.tile(sin_s, (n_samp, 1)), jnp.zeros((n_ctx_rows, ATT_HEAD_DIM), F32)], axis=0)
    return cos_all, sin_all


def _pack_small(c_ctx, norm_w, b_ada, ret, qn, kn):
    d = D_MODEL
    row5 = jnp.concatenate([ret.reshape(-1), jnp.zeros((128 - 2 * RET_HEADS,), F32), qn.reshape(-1), kn.reshape(-1),
                            jnp.zeros((d - 384,), F32)])
    return jnp.concatenate([c_ctx.reshape(1, d), norm_w.reshape(1, d), b_ada.reshape(3, d), row5.reshape(1, d),
                            jnp.zeros((2, d), F32)], axis=0)


def _unpack_small(p):
    d = D_MODEL
    return (p[0], p[1:2], p[2:5].reshape(1, 3 * d), p[5, :2 * RET_HEADS].reshape(1, 2, RET_HEADS),
            p[5:6, 128:256], p[5:6, 256:384])


def _step(x, c, ctx, c_ctx, norm_w, b_ada, ret_log2_decay, q_norm_w, k_norm_w, loss_target, weights, ids, dist):
    n_samp, seq, d = x.shape
    lc = ctx.shape[1]
    t_lat, t_ctx = n_samp * seq, n_samp * lc
    assert seq % TM == 0 and t_ctx == TM and t_lat % lc == 0 and seq % GRID_W == 0
    tps = seq // TM

    x_lat = x.reshape(t_lat, d)
    x_ctx = ctx.reshape(t_ctx, d)
    cvec8 = jnp.concatenate([c, c_ctx.reshape(1, d), jnp.zeros((8 - n_samp - 1, d), F32)], axis=0)
    lg = jnp.log1p(-jnp.exp2(ret_log2_decay.reshape(2, RET_HEADS)))
    cos_all, sin_all = _rope_tables(seq, n_samp, t_ctx)

    w_ada_b, w_in_b, w_or_b, w_oa_b, w_out_b = weights
    w_ada_g = _run_comm(_ag_comm((w_ada_b,), (0, 1, 2)))[0] if dist else w_ada_b
    mod8 = _adaln_fwd(cvec8, w_ada_g, b_ada)
    mod3 = mod8[:n_samp + 1]
    shift3 = mod3[:, None, 0:d]
    scale3 = mod3[:, None, d:2 * d]
    gate3 = mod3[:, None, 2 * d:3 * d]

    hx, hxt = _norm_fwd(x_lat, x_ctx, norm_w, scale3, shift3, tps, n_samp)
    if dist:
        px, (w_in_1,) = _in_proj(hx, w_in_b, ids, 0, 1, comm=_ag_comm((w_in_b,), (0, 1), arg_index=1))
        px, (w_in_g,) = _in_proj(hx, w_in_1, ids, 1, 2, px=px, comm=_ag_comm((w_in_1,), (2,), arg_index=1))
        px = _in_proj(hx, w_in_g, ids, 3, 1, px=px)
    else:
        w_in_g = w_in_b
        px = _in_proj(hx, w_in_g, ids, 0, N_SHARD)

    states0 = _ctx_state_fwd(px, lg, n_samp, t_lat, lc)
    if dist:
        (o_f, o_b, saved), w_o = _ret_fwd(px, states0, lg, n_samp, seq,
                                          comm=_ag_comm((w_or_b, w_oa_b, w_out_b), (0, 1, 2)))
    else:
        (o_f, o_b, saved), w_o = _ret_fwd(px, states0, lg, n_samp, seq), (w_or_b, w_oa_b, w_out_b)
    w_o_ret, w_o_att, w_out = (w.reshape(-1, d) for w in w_o)
    y_ret = _retnorm_fwd(o_f, o_b, px)

    qn = _att_prep_q(px, cos_all, sin_all, q_norm_w, t_lat)
    kn, vn = _att_prep_kv(px, cos_all, sin_all, k_norm_w)
    y_att, o_att, lse = _att_fwd(qn, kn, vn, px, n_samp, seq, lc)

    (gx_res, dy_ret, dy_att, dpx, loss8, dgate, g_w_o_ret, g_w_o_att, g_w_out) = _merge(
        x_lat, loss_target.reshape(t_lat, d), y_ret, y_att, px, gate3, w_o_ret, w_o_att, w_out, tps)

    g_a = [g.reshape(N_SHARD, -1, d) for g in (g_w_o_ret, g_w_o_att, g_w_out)]
    dpx = _att_gate_bwd(dpx, dy_att, o_att, px)
    res = _att_bwd(dpx, qn, kn, vn, px, o_att, lse, dy_att, cos_all, sin_all, q_norm_w, n_samp, seq, lc,
                   comm=_swap_comm(g_a) if dist else None)
    (dpx, dkl, dkc, dvl, dvc, gqw), sib_a = res if dist else (res, None)
    dpx, gkw = _att_kv_bwd(dpx, dkl, dkc, dvl, dvc, px, cos_all, sin_all, k_norm_w)
    if dist:
        t_a = [_chip_sum(g, p, ids) for g, p in zip(g_a, sib_a)]

    do, dpx = _retnorm_bwd(dpx, dy_ret, o_f, o_b, px)
    res = _ret_bwd(px, do, saved, lg, n_samp, seq,
                   comm=_exchange_comm([t16 for _, t16 in t_a]) if dist else None)
    (dqf, dkf, dvf, dqb, dkb, dvb, dstates, dlg_lat), q_a = res if dist else (res, None)
    if dist:
        r_a = [_shard_sum(t, q, ids) for (t, _), q in zip(t_a, q_a)]
    dpx = _combine_into(dpx, dqf, dqb, C_RQ, 1.0)
    dpx = _combine_into(dpx, dkf, dkb, C_RK, RET_DK ** -0.5)
    dpx = _combine_into(dpx, dvf, dvb, C_RV, 1.0)
    dpx, dlg_ctx = _ctx_state_bwd(dpx, px, dstates, lg, n_samp, t_lat, lc)
    dpx = _zero_ctx_tail(dpx, t_lat)

    n_tiles = dpx.shape[0] // _big_rows(dpx.shape[0])
    if dist:
        g_b1 = _gw_in(hxt, dpx, 0, 2)
        g_b2, (sib_b1, *r_a) = _gw_in(hxt, dpx, 1, 2, comm=_join_comms(_swap_comm([g_b1]), _join_comm(r_a)))
        t_b1, t16_b1 = _chip_sum(g_b1, sib_b1, ids)
        dhx, (q_b1, sib_b2) = _dhx(dpx, w_in_g, 0, 1,
                                   comm=_join_comms(_exchange_comm([t16_b1]), _swap_comm([g_b2])))
        t_b2, t16_b2 = _chip_sum(g_b2, sib_b2, ids)
        r_b = _shard_sum(t_b1, q_b1, ids, 0, 2)
        dhx, (q_b2,) = _dhx(dpx, w_in_g, 1, n_tiles - 1, dhx=dhx, comm=_exchange_comm([t16_b2]))
        r_b = _shard_sum(t_b2, q_b2, ids, 1, 2, buf=r_b)
        (grad_x, dshift, dscale, g_norm_w), (r_b,) = _norm_bwd(
            x_lat, x_ctx, dhx, gx_res, norm_w, scale3, tps, n_samp, comm=_join_comm([r_b], 2))
    else:
        g_w_in = _gw_in(hxt, dpx, 0, 1)
        dhx = _dhx(dpx, w_in_g, 0, n_tiles)
        grad_x, dshift, dscale, g_norm_w = _norm_bwd(x_lat, x_ctx, dhx, gx_res, norm_w, scale3, tps, n_samp)

    dgate_all = jnp.concatenate([dgate, jnp.zeros((1, 1, d), F32)], axis=0)
    dmod3 = jnp.concatenate([dshift, dscale, dgate_all], axis=2).reshape(n_samp + 1, 3 * d)
    dmod8 = jnp.concatenate([dmod3, jnp.zeros((8 - n_samp - 1, 3 * d), F32)], axis=0)
    g_w_ada, g_b_ada, dc8 = _adaln_bwd(cvec8, dmod8, w_ada_g)
    if dist:
        (sib_c,) = _run_comm(_swap_comm([g_w_ada]))
        t_c, t16_c = _chip_sum(g_w_ada, sib_c, ids)
        (q_c,) = _run_comm(_exchange_comm([t16_c]))
        (r_c,) = _run_comm(_join_comm([_shard_sum(t_c, q_c, ids)]))
        big = (r_c, r_b, *r_a)
    else:
        big = (g_w_ada, g_w_in, g_w_o_ret, g_w_o_att, g_w_out)

    g_lg = (jnp.sum(dlg_lat[:, :, 0], axis=0).reshape(2, RET_HEADS)
            + jnp.stack([jnp.sum(dlg_ctx[:, :, 0, 0], axis=0), jnp.sum(dlg_ctx[:, :, 1, 0], axis=0)], axis=0))
    small = _pack_small(dc8[n_samp], g_norm_w, g_b_ada, g_lg, jnp.sum(gqw, axis=(0, 1, 2)), gkw)
    return loss8[0, 0], grad_x.reshape(n_samp, seq, d), big, small


def kernel(x, c, ctx, c_ctx, norm_w, w_ada, b_ada, w_in, ret_log2_decay, q_norm_w, k_norm_w, w_o_ret, w_o_att, w_out, loss_target, m_c_ctx, m_norm_w, m_w_ada, m_b_ada, m_w_in, m_ret_log2_decay, m_q_norm_w, m_k_norm_w, m_w_o_ret, m_w_o_att, m_w_out, v_c_ctx, v_norm_w, v_w_ada, v_b_ada, v_w_in, v_ret_log2_decay, v_q_norm_w, v_k_norm_w, v_w_o_ret, v_w_o_att, v_w_out):
    big_w = (w_ada[0], w_in[0], w_o_ret[0], w_o_att[0], w_out[0])
    big_m = (m_w_ada[0], m_w_in[0], m_w_o_ret[0], m_w_o_att[0], m_w_out[0])
    big_v = (v_w_ada[0], v_w_in[0], v_w_o_ret[0], v_w_o_att[0], v_w_out[0])

    ids = _place_ids()
    loss_local, grad_x, big_grad, small_g = _step(
        x, c, ctx, c_ctx, norm_w[0:1], b_ada[0:1], ret_log2_decay[0], q_norm_w[0:1], k_norm_w[0:1], loss_target,
        tuple(_cast_place(w, ids) for w in big_w), ids, True)
    loss = lax.psum(loss_local, ("x", "y", "c"))

    small_grad_in = _all_reduce_small(small_g)
    small_w = _pack_small(c_ctx, norm_w, b_ada, ret_log2_decay, q_norm_w, k_norm_w)
    small_m = _pack_small(m_c_ctx, m_norm_w, m_b_ada, m_ret_log2_decay, m_q_norm_w, m_k_norm_w)
    small_v = _pack_small(v_c_ctx, v_norm_w, v_b_ada, v_ret_log2_decay, v_q_norm_w, v_k_norm_w)
    small_grad, small_delta, small_nm, small_nv = _adamw_small(small_w, small_grad_in, small_m, small_v)

    big_delta, big_nm, big_nv = [], [], []
    for w, g, m, v in zip(big_w, big_grad, big_m, big_v):
        dlt, nm, nv = _adamw(w, g, m, v)
        big_delta.append(dlt[None])
        big_nm.append(nm[None])
        big_nv.append(nv[None])
    big_grad = [g[None] for g in big_grad]

    def order(small_packed, big):
        s = _unpack_small(small_packed)
        return (s[0], s[1], big[0], s[2], big[1], s[3], s[4], s[5], big[2], big[3], big[4])

    return (loss, grad_x, *order(small_grad, big_grad), *order(small_delta, big_delta),
            *order(small_nm, big_nm), *order(small_nv, big_nv))
```

```python
import functools
from typing import NamedTuple

import jax
import jax.numpy as jnp
from jax import lax
from jax.experimental import pallas as pl
from jax.experimental.pallas import tpu as pltpu

F32 = jnp.float32
BF = jnp.bfloat16
SDS = jax.ShapeDtypeStruct
MESH = pl.DeviceIdType.MESH
ANY = pl.BlockSpec(memory_space=pl.ANY)
SMEM = pl.BlockSpec(memory_space=pltpu.SMEM)

D_MODEL = 1024
GRID_W = 64
RET_HEADS = 4
RET_DK = 256
RET_DV = 512
RET_CHUNK = 128
ATT_HEADS = 8
ATT_KV_HEADS = 2
ATT_REP = ATT_HEADS // ATT_KV_HEADS
ATT_HEAD_DIM = 128
ROPE_THETA = 10000.0
NORM_EPS = 1e-6
IN_COLS = 10752
KV_COLS = 3584
C_RK, C_RV, C_AK, C_AV, C_RQ, C_RG, C_AQ, C_AG, C_MR, C_MA = 0, 1024, 3072, 3328, 3584, 4608, 6656, 7680, 8704, 9728
N_SHARD = 4
ADA_W = 3 * D_MODEL // N_SHARD
IN_W = IN_COLS // N_SHARD
IN_BLK = IN_W // 3
N_IN_BLK = IN_COLS // IN_BLK
TM = 512
ATT_TQ = 256
ADAM_LR, ADAM_B1, ADAM_B2, ADAM_EPS, ADAM_WD, ADAM_STEP = 0.001, 0.9, 0.999, 1e-08, 0.01, 10
MIB = 1024 * 1024


def _cp(sem=None, vmem_mb=None, **kw):
    if sem is not None:
        kw["dimension_semantics"] = sem
    if vmem_mb is not None:
        kw["vmem_limit_bytes"] = vmem_mb * MIB
    return pltpu.CompilerParams(**kw)


def _dot(a, b, ca=1, cb=0):
    return lax.dot_general(a.astype(BF), b.astype(BF), (((ca,), (cb,)), ((), ())), preferred_element_type=F32)


def _sigmoid(x):
    return 1.0 / (1.0 + jnp.exp(-x))


def _sum_all(x):
    return jnp.sum(jnp.sum(x, axis=1, keepdims=True), axis=0, keepdims=True)


def _swap_pairs(x):
    ax = x.ndim - 1
    lane = lax.broadcasted_iota(jnp.int32, x.shape, ax)
    nxt = pltpu.roll(x, x.shape[ax] - 1, ax)
    prv = pltpu.roll(x, 1, ax)
    return jnp.where(lane % 2 == 0, nxt, prv)


def _rms(x):
    return lax.rsqrt(jnp.mean(x * x, axis=-1, keepdims=True) + NORM_EPS)


def _rms_bwd(dxh, xh, r):
    return r * (dxh - xh * jnp.mean(dxh * xh, axis=-1, keepdims=True))


class _Comm(NamedTuple):
    name: str
    ins: tuple
    out_shapes: tuple
    aliases: dict
    n_sems: int
    phases: tuple
    arg_aliases: tuple = ()


def _join_comms(*comms):
    comms = [cm for cm in comms if cm is not None]
    if len(comms) <= 1:
        return comms[0] if comms else None
    offs, i_off, o_off, s_off = [], 0, 0, 0
    for cm in comms:
        offs.append((i_off, o_off, s_off))
        i_off, o_off, s_off = i_off + len(cm.ins), o_off + len(cm.out_shapes), s_off + cm.n_sems

    def phase(k):
        def run(ins, outs, ssem, rsem, base):
            sends, recvs = [], []
            for cm, (io, oo, so) in zip(comms, offs):
                if k < len(cm.phases):
                    s, r = cm.phases[k](ins[io:io + len(cm.ins)], outs[oo:oo + len(cm.out_shapes)], ssem, rsem,
                                        base + so)
                    sends += s
                    recvs += r
            return sends, recvs
        return run

    aliases, arg_aliases = {}, ()
    for cm, (io, oo, _) in zip(comms, offs):
        aliases.update({io + a: oo + b for a, b in cm.aliases.items()})
        arg_aliases += tuple((a, oo + b) for a, b in cm.arg_aliases)
    return _Comm("+".join(cm.name for cm in comms), sum((cm.ins for cm in comms), ()),
                 sum((cm.out_shapes for cm in comms), ()), aliases, s_off,
                 tuple(phase(k) for k in range(max(len(cm.phases) for cm in comms))), arg_aliases)


def _run_phases(comm, cins, couts, ssem, rsem, first_started):
    for k, phase in enumerate(comm.phases):
        sends, recvs = phase(cins, couts, ssem, rsem, 0)
        if k > 0 or not first_started:
            for cp in sends:
                cp.start()
        for cp in recvs:
            cp.wait_recv()
        for cp in sends:
            cp.wait_send()


def _call(body, args, comm=None, *, name, grid, in_specs, out_specs, out_shape, scratch_shapes=(),
          compiler_params, aliases=None, prefetch=None):
    single = not isinstance(out_shape, (tuple, list))
    out_specs_t = (out_specs,) if single else tuple(out_specs)
    out_shape_t = (out_shape,) if single else tuple(out_shape)
    n_pre = 0 if prefetch is None else 1
    n_in, n_out, n_sc = len(in_specs), len(out_specs_t), len(scratch_shapes)
    io_alias = {n_pre + a: b for a, b in (aliases or {}).items()}
    if comm is None:
        kernel_body, cin, cout, csems = body, [], [], []
    else:
        n_ci, n_co = len(comm.ins), len(comm.out_shapes)
        cin, cout = [ANY] * n_ci, [ANY] * n_co
        csems = [pltpu.SemaphoreType.DMA((comm.n_sems,)), pltpu.SemaphoreType.DMA((comm.n_sems,))]
        io_alias.update({n_pre + n_in + a: n_out + b for a, b in comm.aliases.items()})
        io_alias.update({n_pre + a: n_out + b for a, b in comm.arg_aliases})

        def kernel_body(*refs):
            pre, refs = refs[:n_pre], refs[n_pre:]
            ins, cins = refs[:n_in], refs[n_in:n_in + n_ci]
            outs = refs[n_in + n_ci:n_in + n_ci + n_out]
            couts = refs[n_in + n_ci + n_out:n_in + n_ci + n_out + n_co]
            scratch = refs[n_in + n_ci + n_out + n_co:n_in + n_ci + n_out + n_co + n_sc]
            ssem, rsem = refs[-2:]
            first = functools.reduce(jnp.logical_and, [pl.program_id(k) == 0 for k in range(len(grid))])
            last = functools.reduce(jnp.logical_and, [pl.program_id(k) == grid[k] - 1 for k in range(len(grid))])

            @pl.when(first)
            def _():
                for cp in comm.phases[0](cins, couts, ssem, rsem, 0)[0]:
                    cp.start()

            body(*pre, *ins, *outs, *scratch)

            @pl.when(last)
            def _():
                _run_phases(comm, cins, couts, ssem, rsem, True)

        name = name + "+" + comm.name

    all_in, all_out = list(in_specs) + cin, out_specs_t + tuple(cout)
    shapes = out_shape_t + (tuple(comm.out_shapes) if comm is not None else ())
    scratch = list(scratch_shapes) + csems
    if prefetch is None:
        res = pl.pallas_call(kernel_body, name=name, grid=grid, in_specs=all_in, out_specs=all_out, out_shape=shapes,
                             scratch_shapes=scratch, input_output_aliases=io_alias,
                             compiler_params=compiler_params)(*args, *(comm.ins if comm is not None else ()))
    else:
        res = pl.pallas_call(
            kernel_body, name=name, out_shape=shapes, input_output_aliases=io_alias, compiler_params=compiler_params,
            grid_spec=pltpu.PrefetchScalarGridSpec(num_scalar_prefetch=1, grid=grid, in_specs=all_in,
                                                   out_specs=all_out, scratch_shapes=scratch))(
                                                       prefetch, *args, *(comm.ins if comm is not None else ()))
    own = res[0] if single else tuple(res[:n_out])
    return own if comm is None else (own, tuple(res[n_out:]))


def _run_comm(comm):
    n_ci, n_co = len(comm.ins), len(comm.out_shapes)

    def body(*refs):
        _run_phases(comm, refs[:n_ci], refs[n_ci:n_ci + n_co], refs[-2], refs[-1], False)

    return pl.pallas_call(
        body, name=comm.name, in_specs=[ANY] * n_ci, out_specs=tuple([ANY] * n_co), out_shape=tuple(comm.out_shapes),
        input_output_aliases=dict(comm.aliases),
        scratch_shapes=[pltpu.SemaphoreType.DMA((comm.n_sems,)), pltpu.SemaphoreType.DMA((comm.n_sems,))],
        compiler_params=_cp(has_side_effects=True))(*comm.ins)


def _adaln_fwd(cvec8, w_ada_g, b_ada):
    def body(c_ref, w_ref, b_ref, o_ref):
        cv = c_ref[...]
        sc = (cv * _sigmoid(cv)).astype(BF)
        for s in range(N_SHARD):
            cols = slice(s * ADA_W, (s + 1) * ADA_W)
            o_ref[:, cols] = jnp.dot(sc, w_ref[s], preferred_element_type=F32) + b_ref[:, cols]

    return pl.pallas_call(body, out_shape=SDS((8, 3 * D_MODEL), F32), name="adaln_fwd",
                          compiler_params=_cp(vmem_mb=32))(cvec8, w_ada_g, b_ada)


def _adaln_bwd(cvec8, dmod8, w_ada_g):
    def body(c_ref, d_ref, w_ref, gw_ref, gb_ref, dc_ref):
        cv = c_ref[...]
        sg = _sigmoid(cv)
        sc = cv * sg
        dm = d_ref[...]
        gb_ref[...] = jnp.sum(dm, axis=0, keepdims=True)
        dsc = jnp.zeros((8, D_MODEL), F32)
        for s in range(N_SHARD):
            cols = slice(s * ADA_W, (s + 1) * ADA_W)
            gw_ref[s] = _dot(sc, dm[:, cols], 0, 0)
            dsc = dsc + _dot(dm[:, cols], w_ref[s], 1, 1)
        dc_ref[...] = dsc * (sg * (1.0 + cv * (1.0 - sg)))

    return pl.pallas_call(
        body, name="adaln_bwd",
        out_shape=(SDS((N_SHARD, D_MODEL, ADA_W), F32), SDS((1, 3 * D_MODEL), F32), SDS((8, D_MODEL), F32)),
        compiler_params=_cp(vmem_mb=48))(cvec8, dmod8, w_ada_g)


def _big_rows(rows):
    return 1536 if rows % 1536 == 0 else TM


def _norm_fwd(x_lat, x_ctx, norm_w, scale3, shift3, tiles_per_sample, n_samp):
    n_lat = x_lat.shape[0] // TM
    rows = x_lat.shape[0] + x_ctx.shape[0]

    def samp(i):
        return jnp.minimum(i // tiles_per_sample, n_samp)

    def body(x_ref, c_ref, nw_ref, sc_ref, sh_ref, hx_ref, hxt_ref):
        x = jnp.where(pl.program_id(0) < n_lat, x_ref[...], c_ref[...])
        h = x * _rms(x) * nw_ref[...] * (1.0 + sc_ref[...]) + sh_ref[...]
        hx_ref[...] = h.astype(BF)
        hxt_ref[...] = h.T.astype(BF)

    return pl.pallas_call(
        body, name="norm_fwd", grid=(rows // TM,),
        in_specs=[pl.BlockSpec((TM, D_MODEL), lambda i: (jnp.minimum(i, n_lat - 1), 0)),
                  pl.BlockSpec((TM, D_MODEL), lambda i: (jnp.maximum(i - n_lat, 0), 0)),
                  pl.BlockSpec((1, D_MODEL), lambda i: (0, 0)),
                  pl.BlockSpec((None, 1, D_MODEL), lambda i: (samp(i), 0, 0)),
                  pl.BlockSpec((None, 1, D_MODEL), lambda i: (samp(i), 0, 0))],
        out_specs=(pl.BlockSpec((TM, D_MODEL), lambda i: (i, 0)),
                   pl.BlockSpec((D_MODEL, TM), lambda i: (0, i))),
        out_shape=(SDS((rows, D_MODEL), BF), SDS((D_MODEL, rows), BF)),
        compiler_params=_cp(("parallel",), 40))(x_lat, x_ctx, norm_w, scale3, shift3)


def _in_proj(hx, w_in_g, ids, first, count, px=None, comm=None):
    rows = hx.shape[0]
    tb = _big_rows(rows)

    def shard(j, ids_ref):
        return ids_ref[4 + first + j // 3]

    def body(ids_ref, h_ref, w_ref, *rest):
        px_ref = rest[-1]
        px_ref[...] = jnp.dot(h_ref[...], w_ref[...], preferred_element_type=F32).astype(BF)

    args, in_specs, aliases = [hx, w_in_g], [
        pl.BlockSpec((tb, D_MODEL), lambda j, i, ids_ref: (i, 0)),
        pl.BlockSpec((None, D_MODEL, IN_BLK), lambda j, i, ids_ref: (shard(j, ids_ref), 0, j % 3))], None
    if px is not None:
        args, in_specs, aliases = args + [px], in_specs + [ANY], {2: 0}
    return _call(body, args, comm, name="in_proj", grid=(3 * count, rows // tb), in_specs=in_specs,
                 out_specs=pl.BlockSpec((tb, IN_BLK), lambda j, i, ids_ref: (i, 3 * shard(j, ids_ref) + j % 3)),
                 out_shape=SDS((rows, IN_COLS), BF), aliases=aliases, prefetch=ids,
                 compiler_params=_cp(("arbitrary", "arbitrary"), 40))


def _norm_bwd(x_lat, x_ctx, dhx, gx_res, norm_w, scale3, tiles_per_sample, n_samp, comm=None):
    rows = x_lat.shape[0] + x_ctx.shape[0]
    n_lat = tiles_per_sample * n_samp

    def samp(i):
        return jnp.minimum(i // tiles_per_sample, n_samp)

    def lat(i):
        return jnp.minimum(i, n_lat - 1)

    def body(x_ref, c_ref, dh_ref, gr_ref, nw_ref, sc_ref, gx_ref, dsh_ref, dsc_ref, dnw_ref):
        i = pl.program_id(0)
        x = jnp.where(i < n_lat, x_ref[...], c_ref[...])
        r = _rms(x)
        xh = x * r
        nw = nw_ref[...]
        dh = dh_ref[...]
        first = jnp.logical_or(i % tiles_per_sample == 0, i >= n_lat)

        @pl.when(first)
        def _():
            dsh_ref[...] = jnp.zeros_like(dsh_ref)
            dsc_ref[...] = jnp.zeros_like(dsc_ref)

        @pl.when(i == 0)
        def _():
            dnw_ref[...] = jnp.zeros_like(dnw_ref)

        dsh_ref[...] += jnp.sum(dh, axis=0, keepdims=True)
        dsc_ref[...] += jnp.sum(dh * (xh * nw), axis=0, keepdims=True)
        du = dh * (1.0 + sc_ref[...])
        dnw_ref[...] += jnp.sum(du * xh, axis=0, keepdims=True)

        @pl.when(i < n_lat)
        def _():
            gx_ref[...] = gr_ref[...] + _rms_bwd(du * nw, xh, r)

    return _call(
        body, [x_lat, x_ctx, dhx, gx_res, norm_w, scale3], comm, name="norm_bwd", grid=(rows // TM,),
        in_specs=[pl.BlockSpec((TM, D_MODEL), lambda i: (lat(i), 0)),
                  pl.BlockSpec((TM, D_MODEL), lambda i: (jnp.maximum(i - n_lat, 0), 0)),
                  pl.BlockSpec((TM, D_MODEL), lambda i: (i, 0)),
                  pl.BlockSpec((TM, D_MODEL), lambda i: (lat(i), 0)),
                  pl.BlockSpec((1, D_MODEL), lambda i: (0, 0)),
                  pl.BlockSpec((None, 1, D_MODEL), lambda i: (samp(i), 0, 0))],
        out_specs=(pl.BlockSpec((TM, D_MODEL), lambda i: (lat(i), 0)),
                   pl.BlockSpec((None, 1, D_MODEL), lambda i: (samp(i), 0, 0)),
                   pl.BlockSpec((None, 1, D_MODEL), lambda i: (samp(i), 0, 0)),
                   pl.BlockSpec((1, D_MODEL), lambda i: (0, 0))),
        out_shape=(SDS((n_lat * TM, D_MODEL), F32), SDS((n_samp + 1, 1, D_MODEL), F32),
                   SDS((n_samp + 1, 1, D_MODEL), F32), SDS((1, D_MODEL), F32)),
        compiler_params=_cp(("arbitrary",), 40))


def _gw_in(hxt, dpx_all, part, n_parts, comm=None):
    rows = dpx_all.shape[0]
    tb = _big_rows(rows)
    dp = D_MODEL // n_parts

    def body(h_ref, d_ref, o_ref):
        @pl.when(pl.program_id(1) == 0)
        def _():
            o_ref[...] = jnp.zeros_like(o_ref)

        o_ref[...] += jnp.dot(h_ref[...], d_ref[...], preferred_element_type=F32)

    return _call(body, [hxt, dpx_all], comm, name="gw_in", grid=(N_IN_BLK, rows // tb),
                 in_specs=[pl.BlockSpec((dp, tb), lambda j, i: (part, i)),
                           pl.BlockSpec((tb, IN_BLK), lambda j, i: (i, j))],
                 out_specs=pl.BlockSpec((None, dp, IN_BLK), lambda j, i: (j // 3, 0, j % 3)),
                 out_shape=SDS((N_SHARD, dp, IN_W), F32),
                 compiler_params=_cp(("arbitrary", "arbitrary"), 40))


def _dhx(dpx_all, w_in_g, tile0, n_tiles, dhx=None, comm=None):
    rows = dpx_all.shape[0]
    tb = _big_rows(rows)

    def body(d_ref, w_ref, *rest):
        o_ref = rest[-1]

        @pl.when(pl.program_id(1) == 0)
        def _():
            o_ref[...] = jnp.zeros_like(o_ref)

        o_ref[...] += lax.dot_general(d_ref[...], w_ref[...], (((1,), (1,)), ((), ())), preferred_element_type=F32)

    args, in_specs, aliases = [dpx_all, w_in_g], [
        pl.BlockSpec((tb, IN_BLK), lambda i, j: (tile0 + i, j)),
        pl.BlockSpec((None, D_MODEL, IN_BLK), lambda i, j: (j // 3, 0, j % 3))], None
    if dhx is not None:
        args, in_specs, aliases = args + [dhx], in_specs + [ANY], {2: 0}
    return _call(body, args, comm, name="dhx", grid=(n_tiles, N_IN_BLK), in_specs=in_specs,
                 out_specs=pl.BlockSpec((tb, D_MODEL), lambda i, j: (tile0 + i, 0)),
                 out_shape=SDS((rows, D_MODEL), F32), aliases=aliases,
                 compiler_params=_cp(("arbitrary", "arbitrary"), 40))


def _decays(lgv, d):
    c = RET_CHUNK
    ii = lax.broadcasted_iota(jnp.int32, (c, 1), 0).astype(F32)
    jj = lax.broadcasted_iota(jnp.int32, (1, c), 1).astype(F32)
    a_i = jnp.where(d == 0, ii, c - 1.0 - ii)
    a_j = jnp.where(d == 0, jj, c - 1.0 - jj)
    rel = a_i - a_j
    mask = jnp.where(rel >= 0, jnp.exp(lgv * jnp.maximum(rel, 0.0)), 0.0)
    qd = jnp.exp(lgv * (a_i + 1.0))
    kd = jnp.exp(lgv * (c - 1.0 - a_i))
    gc = jnp.exp(jnp.full((1, 1), lgv * c, F32))
    return a_i, rel, mask, qd, kd, gc


def _ctx_state_fwd(px, lg, n_samp, t_lat, lc):
    rb = t_lat // lc

    def body(lg_ref, k_ref, v_ref, o_ref):
        h = pl.program_id(1)
        k = k_ref[...].astype(F32) * (RET_DK ** -0.5)
        v = v_ref[...]
        pos = lax.broadcasted_iota(jnp.int32, (lc, 1), 0).astype(F32)
        o_ref[0] = _dot(k * jnp.exp(lg_ref[0, h] * (lc - 1.0 - pos)), v, 0, 0)
        o_ref[1] = _dot(k * jnp.exp(lg_ref[1, h] * pos), v, 0, 0)

    return pl.pallas_call(
        body, name="ctx_state_fwd", grid=(n_samp, RET_HEADS),
        in_specs=[SMEM,
                  pl.BlockSpec((lc, RET_DK), lambda b, h: (rb + b, C_RK // RET_DK + h)),
                  pl.BlockSpec((lc, RET_DV), lambda b, h: (rb + b, C_RV // RET_DV + h))],
        out_specs=pl.BlockSpec((None, 2, None, RET_DK, RET_DV), lambda b, h: (b, 0, h, 0, 0)),
        out_shape=SDS((n_samp, 2, RET_HEADS, RET_DK, RET_DV), F32),
        compiler_params=_cp(("parallel", "parallel")))(lg, px, px)


def _ctx_state_bwd(dpx, px, dstates, lg, n_samp, t_lat, lc):
    rb = t_lat // lc
    kspec = pl.BlockSpec((lc, RET_DK), lambda b, h: (rb + b, C_RK // RET_DK + h))
    vspec = pl.BlockSpec((lc, RET_DV), lambda b, h: (rb + b, C_RV // RET_DV + h))
    sspec = pl.BlockSpec((None, 2, None, RET_DK, RET_DV), lambda b, h: (b, 0, h, 0, 0))

    def weights(lg_ref, h):
        pos = lax.broadcasted_iota(jnp.int32, (lc, 1), 0).astype(F32)
        e_f = lc - 1.0 - pos
        return pos, e_f, jnp.exp(lg_ref[0, h] * e_f), jnp.exp(lg_ref[1, h] * pos)

    def k_body(lg_ref, dpx_hbm, k_ref, v_ref, ds_ref, dk_ref, dlg_ref):
        pos, e_f, w_f, w_b = weights(lg_ref, pl.program_id(1))
        k = k_ref[...].astype(F32) * (RET_DK ** -0.5)
        y_f = _dot(v_ref[...], ds_ref[0], 1, 1) * w_f
        y_b = _dot(v_ref[...], ds_ref[1], 1, 1) * w_b
        dk_ref[...] = ((y_f + y_b) * (RET_DK ** -0.5)).astype(BF)
        t_f = _sum_all(e_f * k * y_f)
        t_b = _sum_all(pos * k * y_b)
        sub = lax.broadcasted_iota(jnp.int32, (8, 128), 0)
        dlg_ref[...] = jnp.where(sub == 0, t_f, jnp.where(sub == 1, t_b, 0.0))

    def v_body(lg_ref, dpx_hbm, k_ref, ds_ref, dv_ref):
        _, _, w_f, w_b = weights(lg_ref, pl.program_id(1))
        k = k_ref[...].astype(F32) * (RET_DK ** -0.5)
        dv_ref[...] = (_dot(k * w_f, ds_ref[0]) + _dot(k * w_b, ds_ref[1])).astype(BF)

    dpx, dlg = pl.pallas_call(
        k_body, name="ctx_state_bwd_k", grid=(n_samp, RET_HEADS), input_output_aliases={1: 0},
        in_specs=[SMEM, ANY, kspec, vspec, sspec],
        out_specs=(kspec, pl.BlockSpec((None, None, 8, 128), lambda b, h: (b, h, 0, 0))),
        out_shape=(SDS(dpx.shape, dpx.dtype), SDS((n_samp, RET_HEADS, 8, 128), F32)),
        compiler_params=_cp(("parallel", "parallel")))(lg, dpx, px, px, dstates)
    dpx = pl.pallas_call(
        v_body, name="ctx_state_bwd_v", grid=(n_samp, RET_HEADS), input_output_aliases={1: 0},
        in_specs=[SMEM, ANY, kspec, sspec], out_specs=vspec, out_shape=SDS(dpx.shape, dpx.dtype),
        compiler_params=_cp(("parallel", "parallel")))(lg, dpx, px, dstates)
    return dpx, dlg


def _zero_ctx_tail(dpx, t_lat):
    wb = 512
    n_ctx = (dpx.shape[0] - t_lat) // TM

    def body(dpx_hbm, o_ref):
        o_ref[...] = jnp.zeros_like(o_ref)

    return pl.pallas_call(
        body, name="zero_ctx_tail", grid=(n_ctx, (IN_COLS - KV_COLS) // wb), input_output_aliases={0: 0},
        in_specs=[ANY], out_specs=pl.BlockSpec((TM, wb), lambda i, j: (t_lat // TM + i, KV_COLS // wb + j)),
        out_shape=SDS(dpx.shape, dpx.dtype),
        compiler_params=_cp(("parallel", "parallel")))(dpx)


def _ret_specs(row_f, row_b):
    c = RET_CHUNK
    wq = RET_HEADS * RET_DK // 2
    wv = RET_HEADS * RET_DV // 2
    specs = []
    for row in (row_f, row_b):
        specs += [pl.BlockSpec((c, wq), lambda b, n, row=row: (row(b, n), C_RQ // wq)),
                  pl.BlockSpec((c, wq), lambda b, n, row=row: (row(b, n), C_RQ // wq + 1)),
                  pl.BlockSpec((c, 2 * wq), lambda b, n, row=row: (row(b, n), C_RK // (2 * wq))),
                  pl.BlockSpec((c, wv), lambda b, n, row=row: (row(b, n), C_RV // wv)),
                  pl.BlockSpec((c, wv), lambda b, n, row=row: (row(b, n), C_RV // wv + 1))]
    return specs


def _ret_head(refs, h):
    q0, q1, k_ref, v0, v1 = refs
    hh = h % 2
    q = (q0, q1)[h // 2][:, hh * RET_DK:(hh + 1) * RET_DK].astype(F32)
    k = k_ref[:, h * RET_DK:(h + 1) * RET_DK].astype(F32) * (RET_DK ** -0.5)
    v = (v0, v1)[h // 2][:, hh * RET_DV:(hh + 1) * RET_DV]
    return q, k, v


def _ret_fwd(px, states0, lg, n_samp, seq, comm=None):
    c = RET_CHUNK
    nc = seq // c
    t_lat = n_samp * seq
    wo = RET_HEADS * RET_DV

    def row_f(b, n):
        return b * nc + n

    def row_b(b, n):
        return b * nc + nc - 1 - n

    def body(lg_ref, *refs):
        ins, (s0_ref, of_ref, ob_ref, st_ref, s_s) = refs[:10], refs[10:]

        @pl.when(pl.program_id(1) == 0)
        def _():
            s_s[...] = s0_ref[...]

        for d, o_ref in ((0, of_ref), (1, ob_ref)):
            for h in range(RET_HEADS):
                _, _, mask, qd, kd, gc = _decays(lg_ref[d, h], d)
                q, k, v = _ret_head(ins[5 * d:5 * d + 5], h)
                s = s_s[d, h]
                st_ref[h, d] = s.astype(BF)
                sc = _dot(q, k, 1, 1) * mask
                o_ref[:, h * RET_DV:(h + 1) * RET_DV] = (_dot(sc, v) + _dot(q * qd, s)).astype(BF)
                s_s[d, h] = s * gc + _dot(k * kd, v, 0, 0)

    return _call(
        body, [lg] + [px] * 10 + [states0], comm, name="ret_fwd", grid=(n_samp, nc),
        in_specs=[SMEM] + _ret_specs(row_f, row_b) + [
            pl.BlockSpec((None, 2, RET_HEADS, RET_DK, RET_DV), lambda b, n: (b, 0, 0, 0, 0))],
        out_specs=(pl.BlockSpec((c, wo), lambda b, n: (row_f(b, n), 0)),
                   pl.BlockSpec((c, wo), lambda b, n: (row_b(b, n), 0)),
                   pl.BlockSpec((None, RET_HEADS, 2, None, RET_DK, RET_DV), lambda b, n: (b, 0, 0, n, 0, 0))),
        out_shape=(SDS((t_lat, wo), BF), SDS((t_lat, wo), BF),
                   SDS((n_samp, RET_HEADS, 2, nc, RET_DK, RET_DV), BF)),
        scratch_shapes=[pltpu.VMEM((2, RET_HEADS, RET_DK, RET_DV), F32)],
        compiler_params=_cp(("arbitrary", "arbitrary"), 48))


def _ret_bwd(px, do, saved, lg, n_samp, seq, comm=None):
    c = RET_CHUNK
    nc = seq // c
    t_lat = n_samp * seq
    wq, wo = RET_HEADS * RET_DK, RET_HEADS * RET_DV

    def row_f(b, n):
        return b * nc + nc - 1 - n

    def row_b(b, n):
        return b * nc + n

    def body(lg_ref, *refs):
        ins = refs[:10]
        (dof_ref, dob_ref, st_ref, dqf, dkf, dvf, dqb, dkb, dvb, ds0_ref, dlg_ref, ds_s, acc_s) = refs[10:]
        n = pl.program_id(1)

        @pl.when(n == 0)
        def _():
            ds_s[...] = jnp.zeros_like(ds_s)
            acc_s[...] = jnp.zeros_like(acc_s)

        for d, (do_ref, dq_ref, dk_ref, dv_ref) in enumerate(((dof_ref, dqf, dkf, dvf), (dob_ref, dqb, dkb, dvb))):
            for h in range(RET_HEADS):
                a_i, rel, mask, qd, kd, gc = _decays(lg_ref[d, h], d)
                q, k, v = _ret_head(ins[5 * d:5 * d + 5], h)
                qb, kb, vb = q.astype(BF), k.astype(BF), v.astype(BF)
                dob = do_ref[:, h * RET_DV:(h + 1) * RET_DV].astype(BF)
                sb = st_ref[h, d]
                ds = ds_s[d, h]
                dsb = ds.astype(BF)
                raw = _dot(qb, kb, 1, 1)
                sc = raw * mask
                dsc = _dot(dob, vb, 1, 1) * mask
                dscb = dsc.astype(BF)
                x = _dot(dob, sb, 1, 1)
                y = _dot(vb, dsb, 1, 1)
                qq = q * qd
                kk = k * kd
                dq_ref[:, h * RET_DK:(h + 1) * RET_DK] = (_dot(dscb, kb) + x * qd).astype(BF)
                dk_ref[:, h * RET_DK:(h + 1) * RET_DK] = (_dot(dscb, qb, 0, 0) + y * kd).astype(BF)
                dv_ref[:, h * RET_DV:(h + 1) * RET_DV] = (_dot(sc, dob, 0, 0) + _dot(kk, dsb)).astype(BF)
                t = (_sum_all(dsc * raw * rel) + _sum_all((a_i + 1.0) * qq * x)
                     + _sum_all((c - 1.0 - a_i) * kk * y) + c * gc * _sum_all(ds * sb.astype(F32)))
                acc_s[4 * d + h:4 * d + h + 1, :] += t
                ds_s[d, h] = ds * gc + _dot(qq, dob, 0, 0)

        @pl.when(n == nc - 1)
        def _():
            ds0_ref[...] = ds_s[...]
            dlg_ref[...] = acc_s[...]

    do_spec_f = pl.BlockSpec((c, wo), lambda b, n: (row_f(b, n), 0))
    do_spec_b = pl.BlockSpec((c, wo), lambda b, n: (row_b(b, n), 0))
    dq_spec_f = pl.BlockSpec((c, wq), lambda b, n: (row_f(b, n), 0))
    dq_spec_b = pl.BlockSpec((c, wq), lambda b, n: (row_b(b, n), 0))
    return _call(
        body, [lg] + [px] * 10 + [do, do, saved], comm, name="ret_bwd", grid=(n_samp, nc),
        in_specs=[SMEM] + _ret_specs(row_f, row_b) + [
            do_spec_f, do_spec_b,
            pl.BlockSpec((None, RET_HEADS, 2, None, RET_DK, RET_DV), lambda b, n: (b, 0, 0, nc - 1 - n, 0, 0))],
        out_specs=(dq_spec_f, dq_spec_f, do_spec_f, dq_spec_b, dq_spec_b, do_spec_b,
                   pl.BlockSpec((None, 2, RET_HEADS, RET_DK, RET_DV), lambda b, n: (b, 0, 0, 0, 0)),
                   pl.BlockSpec((None, 8, 128), lambda b, n: (b, 0, 0))),
        out_shape=(SDS((t_lat, wq), BF), SDS((t_lat, wq), BF), SDS((t_lat, wo), BF),
                   SDS((t_lat, wq), BF), SDS((t_lat, wq), BF), SDS((t_lat, wo), BF),
                   SDS((n_samp, 2, RET_HEADS, RET_DK, RET_DV), F32), SDS((n_samp, 8, 128), F32)),
        scratch_shapes=[pltpu.VMEM((2, RET_HEADS, RET_DK, RET_DV), F32), pltpu.VMEM((8, 128), F32)],
        compiler_params=_cp(("arbitrary", "arbitrary"), 56))


def _combine_into(dpx, a, b, col0, scale):
    t_lat, width = a.shape
    wb = 512
    assert col0 % wb == 0 and width % wb == 0

    def body(dpx_hbm, a_ref, b_ref, o_ref):
        o_ref[...] = ((a_ref[...].astype(F32) + b_ref[...].astype(F32)) * scale).astype(BF)

    src = pl.BlockSpec((TM, wb), lambda i, j: (i, j))
    return pl.pallas_call(
        body, name="combine_into", grid=(t_lat // TM, width // wb), input_output_aliases={0: 0},
        in_specs=[ANY, src, src], out_specs=pl.BlockSpec((TM, wb), lambda i, j: (i, col0 // wb + j)),
        out_shape=SDS(dpx.shape, dpx.dtype),
        compiler_params=_cp(("parallel", "parallel")))(dpx, a, b)


def _retnorm_fwd(o_f, o_b, px):
    t_lat = o_f.shape[0]

    def body(of_ref, ob_ref, g_ref, y_ref):
        o = of_ref[...].astype(F32) + ob_ref[...].astype(F32)
        g = g_ref[...].astype(F32)
        y_ref[...] = (o * _rms(o) * (g * _sigmoid(g))).astype(BF)

    so = pl.BlockSpec((TM, RET_DV), lambda i, h: (i, h))
    return pl.pallas_call(
        body, name="retnorm_fwd", grid=(t_lat // TM, RET_HEADS),
        in_specs=[so, so, pl.BlockSpec((TM, RET_DV), lambda i, h: (i, C_RG // RET_DV + h))],
        out_specs=so,
        out_shape=SDS((t_lat, RET_HEADS * RET_DV), BF),
        compiler_params=_cp(("parallel", "parallel")))(o_f, o_b, px)


def _retnorm_bwd(dpx, dy, o_f, o_b, px):
    t_lat = o_f.shape[0]

    def body(dpx_hbm, dy_ref, of_ref, ob_ref, g_ref, do_ref, dg_ref):
        o = of_ref[...].astype(F32) + ob_ref[...].astype(F32)
        r = _rms(o)
        on = o * r
        g = g_ref[...].astype(F32)
        sg = _sigmoid(g)
        dy_ = dy_ref[...].astype(F32)
        dg_ref[...] = (dy_ * on * (sg * (1.0 + g * (1.0 - sg)))).astype(BF)
        do_ref[...] = _rms_bwd(dy_ * (g * sg), on, r).astype(BF)

    so = pl.BlockSpec((TM, RET_DV), lambda i, h: (i, h))
    gcol = pl.BlockSpec((TM, RET_DV), lambda i, h: (i, C_RG // RET_DV + h))
    return pl.pallas_call(
        body, name="retnorm_bwd", grid=(t_lat // TM, RET_HEADS), input_output_aliases={0: 1},
        in_specs=[ANY, so, so, so, gcol],
        out_specs=(so, gcol),
        out_shape=(SDS((t_lat, RET_HEADS * RET_DV), BF), SDS(dpx.shape, dpx.dtype)),
        compiler_params=_cp(("parallel", "parallel")))(dpx, dy, o_f, o_b, px)


def _norm_rope(x, w, cos, sin):
    xn = x * _rms(x) * w
    return xn * cos + _swap_pairs(xn) * sin


def _norm_rope_bwd(dy, x, w, cos, sin):
    dxn = dy * cos + _swap_pairs(dy * sin)
    r = _rms(x)
    xh = x * r
    return _rms_bwd(dxn * w, xh, r), jnp.sum(dxn * xh, axis=0, keepdims=True)


def _att_prep_q(px, cos_all, sin_all, qnw, t_lat):
    hd = ATT_HEAD_DIM
    wblk = ATT_REP * hd

    def body(x_ref, cos_ref, sin_ref, w_ref, o_ref):
        for r in range(ATT_REP):
            cols = slice(r * hd, (r + 1) * hd)
            qr = _norm_rope(x_ref[:, cols].astype(F32), w_ref[...], cos_ref[...], sin_ref[...])
            o_ref[:, cols] = (qr * (hd ** -0.5)).astype(BF)

    return pl.pallas_call(
        body, name="att_prep_q", grid=(t_lat // TM, ATT_KV_HEADS),
        in_specs=[pl.BlockSpec((TM, wblk), lambda i, g: (i, C_AQ // wblk + g)),
                  pl.BlockSpec((TM, hd), lambda i, g: (i, 0)),
                  pl.BlockSpec((TM, hd), lambda i, g: (i, 0)),
                  pl.BlockSpec((1, hd), lambda i, g: (0, 0))],
        out_specs=pl.BlockSpec((TM, wblk), lambda i, g: (i, g)),
        out_shape=SDS((t_lat, ATT_HEADS * hd), BF),
        compiler_params=_cp(("parallel", "parallel")))(px, cos_all, sin_all, qnw)


def _att_prep_kv(px, cos_all, sin_all, knw):
    rows = px.shape[0]
    hd = ATT_HEAD_DIM
    kvw = ATT_KV_HEADS * hd

    def body(x_ref, cos_ref, sin_ref, w_ref, k_ref, v_ref):
        for g in range(ATT_KV_HEADS):
            cols = slice(g * hd, (g + 1) * hd)
            k_ref[:, cols] = _norm_rope(x_ref[:, cols].astype(F32), w_ref[...], cos_ref[...],
                                        sin_ref[...]).astype(BF)
        v_ref[...] = x_ref[:, kvw:].astype(BF)

    return pl.pallas_call(
        body, name="att_prep_kv", grid=(rows // TM,),
        in_specs=[pl.BlockSpec((TM, 2 * kvw), lambda i: (i, C_AK // (2 * kvw))),
                  pl.BlockSpec((TM, hd), lambda i: (i, 0)),
                  pl.BlockSpec((TM, hd), lambda i: (i, 0)),
                  pl.BlockSpec((1, hd), lambda i: (0, 0))],
        out_specs=(pl.BlockSpec((TM, kvw), lambda i: (i, 0)), pl.BlockSpec((TM, kvw), lambda i: (i, 0))),
        out_shape=(SDS((rows, kvw), BF), SDS((rows, kvw), BF)),
        compiler_params=_cp(("parallel",)))(px, cos_all, sin_all, knw)


def _att_kv_bwd(dpx, dkl, dkc, dvl, dvc, px, cos_all, sin_all, knw):
    rows = px.shape[0]
    hd = ATT_HEAD_DIM
    kvw = ATT_KV_HEADS * hd
    n_lat = dkl.shape[0] // TM
    assert dkc.shape[0] == TM

    def body(dpx_hbm, dkl_ref, dkc_ref, dvl_ref, dvc_ref, x_ref, cos_ref, sin_ref, w_ref, o_ref, gw_ref):
        i = pl.program_id(0)

        @pl.when(i == 0)
        def _():
            gw_ref[...] = jnp.zeros_like(gw_ref)

        is_lat = i < n_lat
        dk = jnp.where(is_lat, dkl_ref[...], dkc_ref[...])
        dv = jnp.where(is_lat, dvl_ref[...], dvc_ref[...])
        for g in range(ATT_KV_HEADS):
            cols = slice(g * hd, (g + 1) * hd)
            dx, gw = _norm_rope_bwd(dk[:, cols], x_ref[:, cols].astype(F32), w_ref[...], cos_ref[...], sin_ref[...])
            o_ref[:, cols] = dx.astype(BF)
            gw_ref[...] += gw
        o_ref[:, kvw:] = dv.astype(BF)

    lat = pl.BlockSpec((TM, kvw), lambda i: (jnp.minimum(i, n_lat - 1), 0))
    ctx = pl.BlockSpec((TM, kvw), lambda i: (0, 0))
    kvcol = pl.BlockSpec((TM, 2 * kvw), lambda i: (i, C_AK // (2 * kvw)))
    return pl.pallas_call(
        body, name="att_kv_bwd", grid=(rows // TM,), input_output_aliases={0: 0},
        in_specs=[ANY, lat, ctx, lat, ctx, kvcol,
                  pl.BlockSpec((TM, hd), lambda i: (i, 0)),
                  pl.BlockSpec((TM, hd), lambda i: (i, 0)),
                  pl.BlockSpec((1, hd), lambda i: (0, 0))],
        out_specs=(kvcol, pl.BlockSpec((1, hd), lambda i: (0, 0))),
        out_shape=(SDS(dpx.shape, dpx.dtype), SDS((1, hd), F32)),
        compiler_params=_cp(("arbitrary",)))(dpx, dkl, dkc, dvl, dvc, px, cos_all, sin_all, knw)


def _stack_heads(ref_or_val):
    hd = ATT_HEAD_DIM
    return jnp.concatenate([ref_or_val[:, r * hd:(r + 1) * hd] for r in range(ATT_REP)], axis=0)


def _att_scores(q, kl, kc):
    sl = _dot(q, kl, 1, 1)
    sc = _dot(q, kc, 1, 1)
    m = jnp.maximum(jnp.max(sl, axis=-1, keepdims=True), jnp.max(sc, axis=-1, keepdims=True))
    el = jnp.exp(sl - m)
    ec = jnp.exp(sc - m)
    denom = jnp.sum(el, axis=-1, keepdims=True) + jnp.sum(ec, axis=-1, keepdims=True)
    return el, ec, denom, m


def _att_fwd(qn, kn, vn, px, n_samp, seq, lc):
    hd = ATT_HEAD_DIM
    tq = ATT_TQ
    nq = seq // tq
    wblk = ATT_REP * hd
    cb = n_samp * seq // lc
    t_lat = n_samp * seq

    def body(q_ref, kl_ref, kc_ref, vl_ref, vc_ref, g_ref, y_ref, o_ref, lse_ref):
        lane = lax.broadcasted_iota(jnp.int32, (tq, hd), 1)
        lse = jnp.zeros((tq, hd), F32)
        for r in range(ATT_REP):
            cols = slice(r * hd, (r + 1) * hd)
            el, ec, denom, m = _att_scores(q_ref[:, cols], kl_ref[...], kc_ref[...])
            o = (_dot(el, vl_ref[...]) + _dot(ec, vc_ref[...])) / denom
            g = g_ref[:, cols].astype(F32)
            o_ref[:, cols] = o.astype(BF)
            y_ref[:, cols] = (o * (g * _sigmoid(g))).astype(BF)
            lse = jnp.where(lane == r, m + jnp.log(denom), lse)
        lse_ref[...] = lse

    return pl.pallas_call(
        body, name="att_fwd", grid=(n_samp, ATT_KV_HEADS, nq),
        in_specs=[pl.BlockSpec((tq, wblk), lambda b, g, i: (b * nq + i, g)),
                  pl.BlockSpec((seq, hd), lambda b, g, i: (b, g)),
                  pl.BlockSpec((lc, hd), lambda b, g, i: (cb + b, g)),
                  pl.BlockSpec((seq, hd), lambda b, g, i: (b, g)),
                  pl.BlockSpec((lc, hd), lambda b, g, i: (cb + b, g)),
                  pl.BlockSpec((tq, wblk), lambda b, g, i: (b * nq + i, C_AG // wblk + g))],
        out_specs=(pl.BlockSpec((tq, wblk), lambda b, g, i: (b * nq + i, g)),
                   pl.BlockSpec((tq, wblk), lambda b, g, i: (b * nq + i, g)),
                   pl.BlockSpec((tq, hd), lambda b, g, i: (b * nq + i, g))),
        out_shape=(SDS((t_lat, ATT_HEADS * hd), BF), SDS((t_lat, ATT_HEADS * hd), BF),
                   SDS((t_lat, ATT_KV_HEADS * hd), F32)),
        compiler_params=_cp(("parallel", "parallel", "parallel"), 48))(qn, kn, kn, vn, vn, px)


def _att_gate_bwd(dpx, dy_att, o_att, px):
    t_lat = dy_att.shape[0]
    wblk = ATT_REP * ATT_HEAD_DIM

    def body(dpx_hbm, dy_ref, o_ref, g_ref, out_ref):
        g = g_ref[...].astype(F32)
        sg = _sigmoid(g)
        out_ref[...] = (dy_ref[...].astype(F32) * o_ref[...].astype(F32) * (sg * (1.0 + g * (1.0 - sg)))).astype(BF)

    blk = pl.BlockSpec((TM, wblk), lambda i, j: (i, j))
    gcol = pl.BlockSpec((TM, wblk), lambda i, j: (i, C_AG // wblk + j))
    return pl.pallas_call(
        body, name="att_gate_bwd", grid=(t_lat // TM, ATT_KV_HEADS),
        in_specs=[ANY, blk, blk, gcol], out_specs=gcol, out_shape=SDS(dpx.shape, dpx.dtype),
        input_output_aliases={0: 0},
        compiler_params=_cp(("parallel", "parallel")))(dpx, dy_att, o_att, px)


def _att_bwd(dpx, qn, kn, vn, px, o_att, lse, dy_att, cos_all, sin_all, qnw, n_samp, seq, lc, comm=None):
    hd = ATT_HEAD_DIM
    tq = ATT_TQ
    nq = seq // tq
    wblk = ATT_REP * hd
    cb = n_samp * seq // lc
    t_lat = n_samp * seq
    kvw = ATT_KV_HEADS * hd
    scale = hd ** -0.5

    def body(dpx_hbm, q_ref, kl_ref, kc_ref, vl_ref, vc_ref, g_ref, o_ref, dy_ref, x_ref, cos_ref, sin_ref, w_ref,
             lse_ref, dq_ref, dkl_ref, dkc_ref, dvl_ref, dvc_ref, gw_ref, akl, akc, avl, avc, aw):
        i = pl.program_id(2)

        @pl.when(i == 0)
        def _():
            akl[...] = jnp.zeros_like(akl)
            akc[...] = jnp.zeros_like(akc)
            avl[...] = jnp.zeros_like(avl)
            avc[...] = jnp.zeros_like(avc)
            aw[...] = jnp.zeros_like(aw)

        dobs, pls, pcs, dsls, dscs = [], [], [], [], []
        for r in range(ATT_REP):
            cols = slice(r * hd, (r + 1) * hd)
            g = g_ref[:, cols].astype(F32)
            sg = _sigmoid(g)
            dy = dy_ref[:, cols].astype(F32)
            do = dy * (g * sg)
            delta = jnp.sum(do * o_ref[:, cols].astype(F32), axis=-1, keepdims=True)
            lse = lse_ref[:, r:r + 1]
            p_l = jnp.exp(_dot(q_ref[:, cols], kl_ref[...], 1, 1) - lse).astype(BF)
            p_c = jnp.exp(_dot(q_ref[:, cols], kc_ref[...], 1, 1) - lse).astype(BF)
            dob = do.astype(BF)
            ds_l = (p_l * (_dot(dob, vl_ref[...], 1, 1) - delta)).astype(BF)
            ds_c = (p_c * (_dot(dob, vc_ref[...], 1, 1) - delta)).astype(BF)
            dq = (_dot(ds_l, kl_ref[...]) + _dot(ds_c, kc_ref[...])) * scale
            dx, gw = _norm_rope_bwd(dq, x_ref[:, cols].astype(F32), w_ref[...], cos_ref[...], sin_ref[...])
            dq_ref[:, cols] = dx.astype(BF)
            aw[...] += gw
            dobs.append(dob)
            pls.append(p_l)
            pcs.append(p_c)
            dsls.append(ds_l)
            dscs.append(ds_c)
        do4 = jnp.concatenate(dobs, axis=0)
        q4 = _stack_heads(q_ref)
        avl[...] += _dot(jnp.concatenate(pls, axis=0), do4, 0, 0)
        avc[...] += _dot(jnp.concatenate(pcs, axis=0), do4, 0, 0)
        akl[...] += _dot(jnp.concatenate(dsls, axis=0), q4, 0, 0)
        akc[...] += _dot(jnp.concatenate(dscs, axis=0), q4, 0, 0)

        @pl.when(i == nq - 1)
        def _():
            dkl_ref[...] = akl[...]
            dkc_ref[...] = akc[...]
            dvl_ref[...] = avl[...]
            dvc_ref[...] = avc[...]
            gw_ref[...] = aw[...]

    return _call(
        body, [dpx, qn, kn, kn, vn, vn, px, o_att, dy_att, px, cos_all, sin_all, qnw, lse], comm,
        name="att_bwd", grid=(n_samp, ATT_KV_HEADS, nq), aliases={0: 0},
        in_specs=[ANY,
                  pl.BlockSpec((tq, wblk), lambda b, g, i: (b * nq + i, g)),
                  pl.BlockSpec((seq, hd), lambda b, g, i: (b, g)),
                  pl.BlockSpec((lc, hd), lambda b, g, i: (cb + b, g)),
                  pl.BlockSpec((seq, hd), lambda b, g, i: (b, g)),
                  pl.BlockSpec((lc, hd), lambda b, g, i: (cb + b, g)),
                  pl.BlockSpec((tq, wblk), lambda b, g, i: (b * nq + i, C_AG // wblk + g)),
                  pl.BlockSpec((tq, wblk), lambda b, g, i: (b * nq + i, g)),
                  pl.BlockSpec((tq, wblk), lambda b, g, i: (b * nq + i, g)),
                  pl.BlockSpec((tq, wblk), lambda b, g, i: (b * nq + i, C_AQ // wblk + g)),
                  pl.BlockSpec((tq, hd), lambda b, g, i: (b * nq + i, 0)),
                  pl.BlockSpec((tq, hd), lambda b, g, i: (b * nq + i, 0)),
                  pl.BlockSpec((1, hd), lambda b, g, i: (0, 0)),
                  pl.BlockSpec((tq, hd), lambda b, g, i: (b * nq + i, g))],
        out_specs=(pl.BlockSpec((tq, wblk), lambda b, g, i: (b * nq + i, C_AQ // wblk + g)),
                   pl.BlockSpec((seq, hd), lambda b, g, i: (b, g)),
                   pl.BlockSpec((lc, hd), lambda b, g, i: (b, g)),
                   pl.BlockSpec((seq, hd), lambda b, g, i: (b, g)),
                   pl.BlockSpec((lc, hd), lambda b, g, i: (b, g)),
                   pl.BlockSpec((None, None, 1, hd), lambda b, g, i: (b, g, 0, 0))),
        out_shape=(SDS(dpx.shape, dpx.dtype),
                   SDS((t_lat, kvw), F32), SDS((n_samp * lc, kvw), F32),
                   SDS((t_lat, kvw), F32), SDS((n_samp * lc, kvw), F32),
                   SDS((n_samp, ATT_KV_HEADS, 1, hd), F32)),
        scratch_shapes=[pltpu.VMEM((seq, hd), F32), pltpu.VMEM((lc, hd), F32),
                        pltpu.VMEM((seq, hd), F32), pltpu.VMEM((lc, hd), F32), pltpu.VMEM((1, hd), F32)],
        compiler_params=_cp(("arbitrary", "arbitrary", "arbitrary"), 56))


def _merge(x_lat, target, y_ret, y_att, px, gate3, w_o_ret, w_o_att, w_out, tiles_per_sample):
    t_lat = x_lat.shape[0]
    tm = 256
    n_t = t_lat // tm
    per = tiles_per_sample * (TM // tm)
    d = D_MODEL
    rv = RET_HEADS * RET_DV
    n_samp = gate3.shape[0] - 1

    def body(x_ref, t_ref, yr_ref, ya_ref, mr0, mr1, ma0, ma1, gt_ref, wor_ref, woa_ref, wout_ref,
             gx_ref, dyr_ref, dya_ref, dpx_hbm, loss_ref, dgt_ref, gwor_hbm, gwoa_hbm, gwout_hbm,
             aor, aoa, aout, dmg_ref, dmg_sem):
        i = pl.program_id(0)

        def dmg_copy(step):
            rows = pl.ds(pl.multiple_of(step * tm, tm), tm)
            return pltpu.make_async_copy(dmg_ref, dpx_hbm.at[rows, pl.ds(C_MR, 2 * d)], dmg_sem)

        @pl.when(i == 0)
        def _():
            aor[...] = jnp.zeros_like(aor)
            aoa[...] = jnp.zeros_like(aoa)
            aout[...] = jnp.zeros_like(aout)
            loss_ref[...] = jnp.zeros_like(loss_ref)

        @pl.when(i % per == 0)
        def _():
            dgt_ref[...] = jnp.zeros_like(dgt_ref)

        yr = yr_ref[...]
        ya = ya_ref[...]
        a = jnp.dot(yr, wor_ref[...], preferred_element_type=F32)
        b = jnp.dot(ya, woa_ref[...], preferred_element_type=F32)
        sr = _sigmoid(jnp.concatenate([mr0[...], mr1[...]], axis=1).astype(F32))
        sa = _sigmoid(jnp.concatenate([ma0[...], ma1[...]], axis=1).astype(F32))
        yb = (sr * a + sa * b).astype(BF)
        out = jnp.dot(yb, wout_ref[...], preferred_element_type=F32)
        gate = gt_ref[...]
        err = x_ref[...] + gate * out - t_ref[...]
        loss_ref[...] += 0.5 * _sum_all(err * err) * (1.0 / d)
        dy_tok = err * (1.0 / d)
        gx_ref[...] = dy_tok
        dgt_ref[...] += jnp.sum(dy_tok * out, axis=0, keepdims=True)
        dout = (dy_tok * gate).astype(BF)
        aout[...] += _dot(yb, dout, 0, 0)
        dyy = _dot(dout, wout_ref[...], 1, 1)
        da = (dyy * sr).astype(BF)
        db = (dyy * sa).astype(BF)
        @pl.when(i > 0)
        def _():
            dmg_copy(i - 1).wait()

        dmg_ref[:, :d] = (dyy * a * (sr * (1.0 - sr))).astype(BF)
        dmg_ref[:, d:] = (dyy * b * (sa * (1.0 - sa))).astype(BF)
        dmg_copy(i).start()
        aor[...] += _dot(yr, da, 0, 0)
        aoa[...] += _dot(ya, db, 0, 0)
        dyr_ref[...] = _dot(da, wor_ref[...], 1, 1).astype(BF)
        dya_ref[...] = _dot(db, woa_ref[...], 1, 1).astype(BF)

        @pl.when(i == n_t - 1)
        def _():
            dmg_copy(i).wait()
            pltpu.sync_copy(aor, gwor_hbm)
            pltpu.sync_copy(aoa, gwoa_hbm)
            pltpu.sync_copy(aout, gwout_hbm)

    half = d // 2
    return pl.pallas_call(
        body, name="merge", grid=(n_t,),
        in_specs=[pl.BlockSpec((tm, d), lambda i: (i, 0)),
                  pl.BlockSpec((tm, d), lambda i: (i, 0)),
                  pl.BlockSpec((tm, rv), lambda i: (i, 0)),
                  pl.BlockSpec((tm, d), lambda i: (i, 0)),
                  pl.BlockSpec((tm, half), lambda i: (i, C_MR // half)),
                  pl.BlockSpec((tm, half), lambda i: (i, C_MR // half + 1)),
                  pl.BlockSpec((tm, half), lambda i: (i, C_MA // half)),
                  pl.BlockSpec((tm, half), lambda i: (i, C_MA // half + 1)),
                  pl.BlockSpec((None, 1, d), lambda i: (i // per, 0, 0)),
                  pl.BlockSpec((rv, d), lambda i: (0, 0)),
                  pl.BlockSpec((d, d), lambda i: (0, 0)),
                  pl.BlockSpec((d, d), lambda i: (0, 0))],
        out_specs=(pl.BlockSpec((tm, d), lambda i: (i, 0)),
                   pl.BlockSpec((tm, rv), lambda i: (i, 0)),
                   pl.BlockSpec((tm, d), lambda i: (i, 0)),
                   ANY,
                   pl.BlockSpec((8, 128), lambda i: (0, 0)),
                   pl.BlockSpec((None, 1, d), lambda i: (i // per, 0, 0)),
                   ANY, ANY, ANY),
        out_shape=(SDS((t_lat, d), F32), SDS((t_lat, rv), BF), SDS((t_lat, d), BF),
                   SDS((px.shape[0], IN_COLS), BF),
                   SDS((8, 128), F32), SDS((n_samp, 1, d), F32),
                   SDS((rv, d), F32), SDS((d, d), F32), SDS((d, d), F32)),
        scratch_shapes=[pltpu.VMEM((rv, d), F32), pltpu.VMEM((d, d), F32), pltpu.VMEM((d, d), F32),
                        pltpu.VMEM((tm, 2 * d), BF), pltpu.SemaphoreType.DMA],
        compiler_params=_cp(("arbitrary",), 56))(
            x_lat, target, y_ret, y_att, px, px, px, px, gate3, w_o_ret, w_o_att, w_out)


def _place():
    x, y, c = lax.axis_index("x"), lax.axis_index("y"), lax.axis_index("c")
    chips = [(1 - x, y), (x, 1 - y), (1 - x, 1 - y)]
    return x, y, c, chips


def _remote(src, dst, send_sem, recv_sem, to):
    return pltpu.make_async_remote_copy(src_ref=src, dst_ref=dst, send_sem=send_sem, recv_sem=recv_sem,
                                        device_id=to, device_id_type=MESH)


def _place_ids():
    x, y, c = lax.axis_index("x"), lax.axis_index("y"), lax.axis_index("c")
    me = 2 * x + y
    return jnp.stack([x, y, c, me, me, 2 * (1 - x) + y, 2 * x + 1 - y, 2 * (1 - x) + 1 - y]).astype(jnp.int32)


def _ag_comm(bufs, rels, arg_index=None):
    n, m = len(bufs), len(rels)

    def half(ref, s, which):
        h = ref.shape[1] // 2
        return ref.at[s, pl.ds(which * h, h), :]

    def ici(ins, outs, ssem, rsem, base):
        x, y, c, chips = _place()
        sends, recvs = [], []
        for a in range(n):
            for jj, j in enumerate(rels):
                k, chip = base + a * m + jj, chips[j]
                mine, theirs = half(outs[a], 2 * x + y, c), half(outs[a], 2 * chip[0] + chip[1], c)
                sends.append(_remote(mine, mine, ssem.at[k], rsem.at[k], (*chip, c)))
                recvs.append(_remote(theirs, theirs, ssem.at[k], rsem.at[k], (*chip, c)))
        return sends, recvs

    def d2d(ins, outs, ssem, rsem, base):
        x, y, c, chips = _place()
        sends, recvs = [], []
        for a in range(n):
            for jj, j in enumerate(rels):
                k, s = base + (n + a) * m + jj, 2 * chips[j][0] + chips[j][1]
                sends.append(_remote(half(outs[a], s, c), half(outs[a], s, c), ssem.at[k], rsem.at[k], (x, y, 1 - c)))
                recvs.append(_remote(half(outs[a], s, 1 - c), half(outs[a], s, 1 - c), ssem.at[k], rsem.at[k],
                                     (x, y, 1 - c)))
        return sends, recvs

    shapes = tuple(SDS(b.shape, b.dtype) for b in bufs)
    if arg_index is not None:
        return _Comm("all_gather", (), shapes, {}, 2 * n * m, (ici, d2d), ((arg_index, 0),))
    return _Comm("all_gather", tuple(bufs), shapes, {a: a for a in range(n)}, 2 * n * m, (ici, d2d))


def _swap_comm(grads):
    n = len(grads)

    def phase(ins, outs, ssem, rsem, base):
        x, y, c, _ = _place()
        sends = []
        for a in range(n):
            h = ins[a].shape[1] // 2
            sends.append(_remote(ins[a].at[:, pl.ds((1 - c) * h, h), :], outs[a], ssem.at[base + a],
                                 rsem.at[base + a], (x, y, 1 - c)))
        return sends, sends

    return _Comm("swap_halves", tuple(grads),
                 tuple(SDS((g.shape[0], g.shape[1] // 2, g.shape[2]), g.dtype) for g in grads), {}, n, (phase,))


def _exchange_comm(parts):
    n = len(parts)

    def phase(ins, outs, ssem, rsem, base):
        x, y, c, chips = _place()
        sends = []
        for a in range(n):
            for j, chip in enumerate(chips):
                k = base + 3 * a + j
                sends.append(_remote(ins[a].at[2 * chip[0] + chip[1]], outs[a].at[j], ssem.at[k], rsem.at[k],
                                     (*chip, c)))
        return sends, sends

    return _Comm("exchange_shards", tuple(parts), tuple(SDS((3,) + p.shape[1:], p.dtype) for p in parts), {}, 3 * n,
                 (phase,))


def _join_comm(bufs, n_parts=1):
    n = len(bufs)

    def phase(ins, outs, ssem, rsem, base):
        x, y, c, _ = _place()
        sends, recvs = [], []
        for a in range(n):
            h = outs[a].shape[0] // (2 * n_parts)
            for p in range(n_parts):
                k = base + a * n_parts + p
                mine = outs[a].at[pl.ds((2 * p + c) * h, h), :]
                other = outs[a].at[pl.ds((2 * p + 1 - c) * h, h), :]
                sends.append(_remote(mine, mine, ssem.at[k], rsem.at[k], (x, y, 1 - c)))
                recvs.append(_remote(other, other, ssem.at[k], rsem.at[k], (x, y, 1 - c)))
        return sends, recvs

    return _Comm("join_halves", tuple(bufs), tuple(SDS(b.shape, b.dtype) for b in bufs), {a: a for a in range(n)},
                 n * n_parts, (phase,))


def _cast_place(w, ids):
    rows, cols = w.shape
    tr = min(rows, 256)

    def body(ids_ref, w_ref, o_ref):
        o_ref[...] = w_ref[...].astype(BF)

    return pl.pallas_call(
        body, name="cast_place",
        grid_spec=pltpu.PrefetchScalarGridSpec(
            num_scalar_prefetch=1, grid=(rows // tr,),
            in_specs=[pl.BlockSpec((tr, cols), lambda i, ids_ref: (i, 0))],
            out_specs=pl.BlockSpec((None, tr, cols), lambda i, ids_ref: (ids_ref[3], i, 0))),
        out_shape=SDS((N_SHARD, rows, cols), BF),
        compiler_params=_cp(("parallel",), 40))(ids, w)


def _all_gather_weights(bufs):
    n = len(bufs)

    def body(*refs):
        outs = refs[n:2 * n]
        send_sems, recv_sems = refs[2 * n:]
        x, y, c, chips = _place()
        sibling = (x, y, 1 - c)
        me = 2 * x + y

        def half(ref, s, which):
            h = ref.shape[1] // 2
            return ref.at[s, pl.ds(which * h, h), :]

        first = []
        for a in range(n):
            for j, chip in enumerate(chips):
                k = a * 3 + j
                win = half(outs[a], me, c)
                first.append(_remote(win, win, send_sems.at[k], recv_sems.at[k], (*chip, c)))
        for cp in first:
            cp.start()
        passed = []
        for a in range(n):
            for j, chip in enumerate(chips):
                k = a * 3 + j
                win = half(outs[a], 2 * chip[0] + chip[1], c)
                _remote(win, win, send_sems.at[k], recv_sems.at[k], (*chip, c)).wait_recv()
                fw = _remote(win, win, send_sems.at[3 * n + k], recv_sems.at[3 * n + k], sibling)
                fw.start()
                passed.append(fw)
        for a in range(n):
            for j, chip in enumerate(chips):
                k = a * 3 + j
                win = half(outs[a], 2 * chip[0] + chip[1], 1 - c)
                _remote(win, win, send_sems.at[3 * n + k], recv_sems.at[3 * n + k], sibling).wait_recv()
        for cp in first + passed:
            cp.wait_send()

    return pl.pallas_call(
        body, name="all_gather_weights",
        in_specs=[ANY] * n, out_specs=tuple([ANY] * n),
        out_shape=tuple(SDS(b.shape, b.dtype) for b in bufs),
        input_output_aliases={a: a for a in range(n)},
        scratch_shapes=[pltpu.SemaphoreType.DMA((6 * n,)), pltpu.SemaphoreType.DMA((6 * n,))],
        compiler_params=_cp(has_side_effects=True))(*bufs)


def _swap_halves(grads):
    n = len(grads)

    def body(*refs):
        ins, outs = refs[:n], refs[n:2 * n]
        send_sems, recv_sems = refs[2 * n:]
        x, y, c, _ = _place()
        sibling = (x, y, 1 - c)

        def half(ref, which):
            h = ref.shape[1] // 2
            return ref.at[:, pl.ds(which * h, h), :]

        sends = [_remote(half(ins[a], 1 - c), outs[a], send_sems.at[a], recv_sems.at[a], sibling)
                 for a in range(n)]
        for cp in sends:
            cp.start()
        for cp in sends:
            cp.wait_recv()
        for cp in sends:
            cp.wait_send()

    return pl.pallas_call(
        body, name="swap_halves",
        in_specs=[ANY] * n, out_specs=tuple([ANY] * n),
        out_shape=tuple(SDS((g.shape[0], g.shape[1] // 2, g.shape[2]), g.dtype) for g in grads),
        scratch_shapes=[pltpu.SemaphoreType.DMA((n,)), pltpu.SemaphoreType.DMA((n,))],
        compiler_params=_cp(has_side_effects=True))(*grads)


def _chip_sum(g, p, ids):
    n_s, rows, cols = g.shape
    h = rows // 2
    tr = min(h, 256)
    nb = h // tr

    def body(ids_ref, g_ref, p_ref, o_ref, o16_ref):
        t = g_ref[...] + p_ref[...]
        o_ref[...] = t
        o16_ref[...] = t.astype(BF)

    out_spec = pl.BlockSpec((None, tr, cols), lambda s, i, ids_ref: (s, i, 0))
    return pl.pallas_call(
        body, name="chip_sum",
        grid_spec=pltpu.PrefetchScalarGridSpec(
            num_scalar_prefetch=1, grid=(n_s, nb),
            in_specs=[pl.BlockSpec((None, tr, cols), lambda s, i, ids_ref: (s, ids_ref[2] * nb + i, 0)),
                      pl.BlockSpec((None, tr, cols), lambda s, i, ids_ref: (s, i, 0))],
            out_specs=(out_spec, out_spec)),
        out_shape=(SDS((n_s, h, cols), g.dtype), SDS((n_s, h, cols), BF)),
        compiler_params=_cp(("parallel", "parallel"), 40))(ids, g, p)


def _exchange_shards(parts):
    n = len(parts)

    def body(*refs):
        ins, outs = refs[:n], refs[n:2 * n]
        send_sems, recv_sems = refs[2 * n:]
        x, y, c, chips = _place()
        sends = []
        for a in range(n):
            for j, chip in enumerate(chips):
                k = a * 3 + j
                sends.append(_remote(ins[a].at[2 * chip[0] + chip[1]], outs[a].at[j],
                                     send_sems.at[k], recv_sems.at[k], (*chip, c)))
        for cp in sends:
            cp.start()
        for cp in sends:
            cp.wait_recv()
        for cp in sends:
            cp.wait_send()

    return pl.pallas_call(
        body, name="exchange_shards",
        in_specs=[ANY] * n, out_specs=tuple([ANY] * n),
        out_shape=tuple(SDS((3,) + p.shape[1:], p.dtype) for p in parts),
        scratch_shapes=[pltpu.SemaphoreType.DMA((3 * n,)), pltpu.SemaphoreType.DMA((3 * n,))],
        compiler_params=_cp(has_side_effects=True))(*parts)


def _shard_sum(t, q, ids, part=0, n_parts=1, buf=None):
    _, h, cols = t.shape
    tr = min(h, 256)
    nb = h // tr

    def body(ids_ref, t_ref, q_ref, *rest):
        rest[-1][...] = ((t_ref[...] + q_ref[0].astype(F32)) + q_ref[1].astype(F32)) + q_ref[2].astype(F32)

    args, in_specs, aliases = [t, q], [
        pl.BlockSpec((None, tr, cols), lambda i, ids_ref: (ids_ref[3], i, 0)),
        pl.BlockSpec((3, tr, cols), lambda i, ids_ref: (0, i, 0))], None
    if buf is not None:
        args, in_specs, aliases = args + [buf], in_specs + [ANY], {2: 0}
    return _call(body, args, None, name="shard_sum", grid=(nb,), in_specs=in_specs,
                 out_specs=pl.BlockSpec((tr, cols), lambda i, ids_ref: ((2 * part + ids_ref[2]) * nb + i, 0)),
                 out_shape=SDS((2 * h * n_parts, cols), t.dtype), aliases=aliases, prefetch=ids,
                 compiler_params=_cp(("parallel",), 40))


def _join_halves(bufs):
    n = len(bufs)

    def body(*refs):
        outs = refs[n:2 * n]
        send_sems, recv_sems = refs[2 * n:]
        x, y, c, _ = _place()
        sibling = (x, y, 1 - c)

        def win(ref, which):
            h = ref.shape[0] // 2
            return ref.at[pl.ds(which * h, h), :]

        sends = [_remote(win(outs[a], c), win(outs[a], c), send_sems.at[a], recv_sems.at[a], sibling)
                 for a in range(n)]
        for cp in sends:
            cp.start()
        for a in range(n):
            other = win(outs[a], 1 - c)
            _remote(other, other, send_sems.at[a], recv_sems.at[a], sibling).wait_recv()
        for cp in sends:
            cp.wait_send()

    return pl.pallas_call(
        body, name="join_halves",
        in_specs=[ANY] * n, out_specs=tuple([ANY] * n),
        out_shape=tuple(SDS(b.shape, b.dtype) for b in bufs),
        input_output_aliases={a: a for a in range(n)},
        scratch_shapes=[pltpu.SemaphoreType.DMA((n,)), pltpu.SemaphoreType.DMA((n,))],
        compiler_params=_cp(has_side_effects=True))(*bufs)


def _all_reduce_small(block):
    rows, cols = block.shape
    n_dev = 8

    def body(x_ref, o_ref, buf, send_sems, recv_sems, local_sem):
        x, y, c, chips = _place()
        me, sibling = (x, y, c), (x, y, 1 - c)

        def slot(px_, py_, pc_):
            return buf.at[4 * px_ + 2 * py_ + pc_]

        def copy(k, who, to, src=None):
            return _remote(slot(*who) if src is None else src, slot(*who), send_sems.at[k], recv_sems.at[k], to)

        mine = pltpu.make_async_copy(x_ref, slot(*me), local_sem)
        mine.start()
        first = [copy(0, me, sibling, src=x_ref)]
        first += [copy(1 + j, me, (*chip, c), src=x_ref) for j, chip in enumerate(chips)]
        for cp in first:
            cp.start()
        passed = [copy(4 + j, (*chip, c), sibling) for j, chip in enumerate(chips)]
        for j, chip in enumerate(chips):
            copy(1 + j, (*chip, c), me).wait_recv()
            passed[j].start()
        copy(0, sibling, me).wait_recv()
        for j, chip in enumerate(chips):
            copy(4 + j, (*chip, 1 - c), me).wait_recv()
        for cp in first + passed:
            cp.wait_send()
        mine.wait()
        acc = buf[0]
        for s in range(1, n_dev):
            acc = acc + buf[s]
        o_ref[...] = acc

    return pl.pallas_call(
        body, name="all_reduce_small",
        in_specs=[pl.BlockSpec(memory_space=pltpu.VMEM)],
        out_specs=pl.BlockSpec(memory_space=pltpu.VMEM),
        out_shape=SDS((rows, cols), F32),
        scratch_shapes=[pltpu.VMEM((n_dev, rows, cols), F32), pltpu.SemaphoreType.DMA((7,)),
                        pltpu.SemaphoreType.DMA((7,)), pltpu.SemaphoreType.DMA],
        compiler_params=_cp(has_side_effects=True))(block)


def _adam_math(w, g, m, v):
    m = ADAM_B1 * m + (1.0 - ADAM_B1) * g
    v = ADAM_B2 * v + (1.0 - ADAM_B2) * (g * g)
    m_hat = m / (1.0 - ADAM_B1 ** ADAM_STEP)
    v_hat = v / (1.0 - ADAM_B2 ** ADAM_STEP)
    delta = -ADAM_LR * (m_hat / (jnp.sqrt(v_hat) + ADAM_EPS) + ADAM_WD * w)
    return delta, m, v


def _adamw(w, g, m, v):
    rows, cols = w.shape
    tr = min(rows, 256)

    def body(w_ref, g_ref, m_ref, v_ref, d_ref, nm_ref, nv_ref):
        d_ref[...], nm_ref[...], nv_ref[...] = _adam_math(w_ref[...], g_ref[...], m_ref[...], v_ref[...])

    spec = pl.BlockSpec((tr, cols), lambda i: (i, 0))
    return pl.pallas_call(
        body, name="adamw", grid=(rows // tr,), in_specs=[spec] * 4, out_specs=(spec,) * 3,
        out_shape=(SDS(w.shape, F32),) * 3, compiler_params=_cp(("parallel",), 40))(w, g, m, v)


def _adamw_small(w, g, m, v):
    def body(w_ref, g_ref, m_ref, v_ref, go_ref, d_ref, nm_ref, nv_ref):
        w = w_ref[...]
        g = g_ref[...]
        sub = lax.broadcasted_iota(jnp.int32, w.shape, 0)
        lane = lax.broadcasted_iota(jnp.int32, w.shape, 1)
        is_ret = jnp.logical_and(sub == 5, lane < 2 * RET_HEADS)
        u = jnp.exp(jnp.where(is_ret, w, -1.0) * jnp.log(2.0))
        g = jnp.where(is_ret, g * (-u * jnp.log(2.0) / (1.0 - u)), g)
        go_ref[...] = g
        d_ref[...], nm_ref[...], nv_ref[...] = _adam_math(w, g, m_ref[...], v_ref[...])

    return pl.pallas_call(body, name="adamw_small", out_shape=(SDS(w.shape, F32),) * 4)(w, g, m, v)


def _rope_tables(seq, n_samp, n_ctx_rows):
    rows = seq // GRID_W
    row = jnp.repeat(jnp.arange(rows, dtype=F32), GRID_W)
    col = jnp.tile(jnp.arange(GRID_W, dtype=F32), rows)
    half = ATT_HEAD_DIM // 2
    freqs = ROPE_THETA ** (-jnp.arange(0, half, 2, dtype=F32) / half)
    ang = jnp.concatenate([row[:, None] * freqs, col[:, None] * freqs], axis=-1)
    cos, sin = jnp.cos(ang), jnp.sin(ang)
    cos_f = jnp.repeat(cos, 2, axis=1)
    sin_s = jnp.stack([-sin, sin], axis=-1).reshape(seq, ATT_HEAD_DIM)
    cos_all = jnp.concatenate([jnp.tile(cos_f, (n_samp, 1)), jnp.ones((n_ctx_rows, ATT_HEAD_DIM), F32)], axis=0)
    sin_all = jnp.concatenate([jnp.tile(sin_s, (n_samp, 1)), jnp.zeros((n_ctx_rows, ATT_HEAD_DIM), F32)], axis=0)
    return cos_all, sin_all


def _pack_small(c_ctx, norm_w, b_ada, ret, qn, kn):
    d = D_MODEL
    row5 = jnp.concatenate([ret.reshape(-1), jnp.zeros((128 - 2 * RET_HEADS,), F32), qn.reshape(-1), kn.reshape(-1),
                            jnp.zeros((d - 384,), F32)])
    return jnp.concatenate([c_ctx.reshape(1, d), norm_w.reshape(1, d), b_ada.reshape(3, d), row5.reshape(1, d),
                            jnp.zeros((2, d), F32)], axis=0)


def _unpack_small(p):
    d = D_MODEL
    return (p[0], p[1:2], p[2:5].reshape(1, 3 * d), p[5, :2 * RET_HEADS].reshape(1, 2, RET_HEADS),
            p[5:6, 128:256], p[5:6, 256:384])


def _step(x, c, ctx, c_ctx, norm_w, b_ada, ret_log2_decay, q_norm_w, k_norm_w, loss_target, weights, ids, dist):
    n_samp, seq, d = x.shape
    lc = ctx.shape[1]
    t_lat, t_ctx = n_samp * seq, n_samp * lc
    assert seq % TM == 0 and t_ctx == TM and t_lat % lc == 0 and seq % GRID_W == 0
    tps = seq // TM

    x_lat = x.reshape(t_lat, d)
    x_ctx = ctx.reshape(t_ctx, d)
    cvec8 = jnp.concatenate([c, c_ctx.reshape(1, d), jnp.zeros((8 - n_samp - 1, d), F32)], axis=0)
    lg = jnp.log1p(-jnp.exp2(ret_log2_decay.reshape(2, RET_HEADS)))
    cos_all, sin_all = _rope_tables(seq, n_samp, t_ctx)

    w_ada_b, w_in_b, w_or_b, w_oa_b, w_out_b = weights
    w_ada_g = _run_comm(_ag_comm((w_ada_b,), (0, 1, 2)))[0] if dist else w_ada_b
    mod8 = _adaln_fwd(cvec8, w_ada_g, b_ada)
    mod3 = mod8[:n_samp + 1]
    shift3 = mod3[:, None, 0:d]
    scale3 = mod3[:, None, d:2 * d]
    gate3 = mod3[:, None, 2 * d:3 * d]

    hx, hxt = _norm_fwd(x_lat, x_ctx, norm_w, scale3, shift3, tps, n_samp)
    if dist:
        px, (w_in_1,) = _in_proj(hx, w_in_b, ids, 0, 1, comm=_ag_comm((w_in_b,), (0, 1), arg_index=1))
        px, (w_in_g,) = _in_proj(hx, w_in_1, ids, 1, 2, px=px, comm=_ag_comm((w_in_1,), (2,), arg_index=1))
        px = _in_proj(hx, w_in_g, ids, 3, 1, px=px)
    else:
        w_in_g = w_in_b
        px = _in_proj(hx, w_in_g, ids, 0, N_SHARD)

    states0 = _ctx_state_fwd(px, lg, n_samp, t_lat, lc)
    if dist:
        (o_f, o_b, saved), w_o = _ret_fwd(px, states0, lg, n_samp, seq,
                                          comm=_ag_comm((w_or_b, w_oa_b, w_out_b), (0, 1, 2)))
    else:
        (o_f, o_b, saved), w_o = _ret_fwd(px, states0, lg, n_samp, seq), (w_or_b, w_oa_b, w_out_b)
    w_o_ret, w_o_att, w_out = (w.reshape(-1, d) for w in w_o)
    y_ret = _retnorm_fwd(o_f, o_b, px)

    qn = _att_prep_q(px, cos_all, sin_all, q_norm_w, t_lat)
    kn, vn = _att_prep_kv(px, cos_all, sin_all, k_norm_w)
    y_att, o_att, lse = _att_fwd(qn, kn, vn, px, n_samp, seq, lc)

    (gx_res, dy_ret, dy_att, dpx, loss8, dgate, g_w_o_ret, g_w_o_att, g_w_out) = _merge(
        x_lat, loss_target.reshape(t_lat, d), y_ret, y_att, px, gate3, w_o_ret, w_o_att, w_out, tps)

    g_a = [g.reshape(N_SHARD, -1, d) for g in (g_w_o_ret, g_w_o_att, g_w_out)]
    dpx = _att_gate_bwd(dpx, dy_att, o_att, px)
    res = _att_bwd(dpx, qn, kn, vn, px, o_att, lse, dy_att, cos_all, sin_all, q_norm_w, n_samp, seq, lc,
                   comm=_swap_comm(g_a) if dist else None)
    (dpx, dkl, dkc, dvl, dvc, gqw), sib_a = res if dist else (res, None)
    dpx, gkw = _att_kv_bwd(dpx, dkl, dkc, dvl, dvc, px, cos_all, sin_all, k_norm_w)
    if dist:
        t_a = [_chip_sum(g, p, ids) for g, p in zip(g_a, sib_a)]

    do, dpx = _retnorm_bwd(dpx, dy_ret, o_f, o_b, px)
    res = _ret_bwd(px, do, saved, lg, n_samp, seq,
                   comm=_exchange_comm([t16 for _, t16 in t_a]) if dist else None)
    (dqf, dkf, dvf, dqb, dkb, dvb, dstates, dlg_lat), q_a = res if dist else (res, None)
    if dist:
        r_a = [_shard_sum(t, q, ids) for (t, _), q in zip(t_a, q_a)]
    dpx = _combine_into(dpx, dqf, dqb, C_RQ, 1.0)
    dpx = _combine_into(dpx, dkf, dkb, C_RK, RET_DK ** -0.5)
    dpx = _combine_into(dpx, dvf, dvb, C_RV, 1.0)
    dpx, dlg_ctx = _ctx_state_bwd(dpx, px, dstates, lg, n_samp, t_lat, lc)
    dpx = _zero_ctx_tail(dpx, t_lat)

    n_tiles = dpx.shape[0] // _big_rows(dpx.shape[0])
    if dist:
        g_b = _gw_in(hxt, dpx, 0, 1)
        dhx, (sib_b, *r_a) = _dhx(dpx, w_in_g, 0, 1, comm=_join_comms(_swap_comm([g_b]), _join_comm(r_a)))
        t_b, t16_b = _chip_sum(g_b, sib_b, ids)
        dhx, (q_b,) = _dhx(dpx, w_in_g, 1, n_tiles - 1, dhx=dhx, comm=_exchange_comm([t16_b]))
        (grad_x, dshift, dscale, g_norm_w), (r_b,) = _norm_bwd(
            x_lat, x_ctx, dhx, gx_res, norm_w, scale3, tps, n_samp,
            comm=_join_comm([_shard_sum(t_b, q_b, ids)]))
    else:
        g_w_in = _gw_in(hxt, dpx, 0, 1)
        dhx = _dhx(dpx, w_in_g, 0, n_tiles)
        grad_x, dshift, dscale, g_norm_w = _norm_bwd(x_lat, x_ctx, dhx, gx_res, norm_w, scale3, tps, n_samp)

    dgate_all = jnp.concatenate([dgate, jnp.zeros((1, 1, d), F32)], axis=0)
    dmod3 = jnp.concatenate([dshift, dscale, dgate_all], axis=2).reshape(n_samp + 1, 3 * d)
    dmod8 = jnp.concatenate([dmod3, jnp.zeros((8 - n_samp - 1, 3 * d), F32)], axis=0)
    g_w_ada, g_b_ada, dc8 = _adaln_bwd(cvec8, dmod8, w_ada_g)
    if dist:
        (sib_c,) = _run_comm(_swap_comm([g_w_ada]))
        t_c, t16_c = _chip_sum(g_w_ada, sib_c, ids)
        (q_c,) = _run_comm(_exchange_comm([t16_c]))
        (r_c,) = _run_comm(_join_comm([_shard_sum(t_c, q_c, ids)]))
        big = (r_c, r_b, *r_a)
    else:
        big = (g_w_ada, g_w_in, g_w_o_ret, g_w_o_att, g_w_out)

    g_lg = (jnp.sum(dlg_lat[:, :, 0], axis=0).reshape(2, RET_HEADS)
            + jnp.stack([jnp.sum(dlg_ctx[:, :, 0, 0], axis=0), jnp.sum(dlg_ctx[:, :, 1, 0], axis=0)], axis=0))
    small = _pack_small(dc8[n_samp], g_norm_w, g_b_ada, g_lg, jnp.sum(gqw, axis=(0, 1, 2)), gkw)
    return loss8[0, 0], grad_x.reshape(n_samp, seq, d), big, small


def kernel(x, c, ctx, c_ctx, norm_w, w_ada, b_ada, w_in, ret_log2_decay, q_norm_w, k_norm_w, w_o_ret, w_o_att, w_out, loss_target, m_c_ctx, m_norm_w, m_w_ada, m_b_ada, m_w_in, m_ret_log2_decay, m_q_norm_w, m_k_norm_w, m_w_o_ret, m_w_o_att, m_w_out, v_c_ctx, v_norm_w, v_w_ada, v_b_ada, v_w_in, v_ret_log2_decay, v_q_norm_w, v_k_norm_w, v_w_o_ret, v_w_o_att, v_w_out):
    big_w = (w_ada[0], w_in[0], w_o_ret[0], w_o_att[0], w_out[0])
    big_m = (m_w_ada[0], m_w_in[0], m_w_o_ret[0], m_w_o_att[0], m_w_out[0])
    big_v = (v_w_ada[0], v_w_in[0], v_w_o_ret[0], v_w_o_att[0], v_w_out[0])

    ids = _place_ids()
    loss_local, grad_x, big_grad, small_g = _step(
        x, c, ctx, c_ctx, norm_w[0:1], b_ada[0:1], ret_log2_decay[0], q_norm_w[0:1], k_norm_w[0:1], loss_target,
        tuple(_cast_place(w, ids) for w in big_w), ids, True)
    small_grad_in = _all_reduce_small(small_g.at[6, 0].set(loss_local))
    loss = small_grad_in[6, 0]
    small_w = _pack_small(c_ctx, norm_w, b_ada, ret_log2_decay, q_norm_w, k_norm_w)
    small_m = _pack_small(m_c_ctx, m_norm_w, m_b_ada, m_ret_log2_decay, m_q_norm_w, m_k_norm_w)
    small_v = _pack_small(v_c_ctx, v_norm_w, v_b_ada, v_ret_log2_decay, v_q_norm_w, v_k_norm_w)
    small_grad, small_delta, small_nm, small_nv = _adamw_small(small_w, small_grad_in, small_m, small_v)

    big_delta, big_nm, big_nv = [], [], []
    for w, g, m, v in zip(big_w, big_grad, big_m, big_v):
        dlt, nm, nv = _adamw(w, g, m, v)
        big_delta.append(dlt[None])
        big_nm.append(nm[None])
        big_nv.append(nv[None])
    big_grad = [g[None] for g in big_grad]

    def order(small_packed, big):
        s = _unpack_small(small_packed)
        return (s[0], s[1], big[0], s[2], big[1], s[3], s[4], s[5], big[2], big[3], big[4])

    return (loss, grad_x, *order(small_grad, big_grad), *order(small_delta, big_delta),
            *order(small_nm, big_nm), *order(small_nv, big_nv))
```

```python
import functools
from typing import NamedTuple

import jax
import jax.numpy as jnp
from jax import lax
from jax.experimental import pallas as pl
from jax.experimental.pallas import tpu as pltpu

F32 = jnp.float32
BF = jnp.bfloat16
SDS = jax.ShapeDtypeStruct
MESH = pl.DeviceIdType.MESH
ANY = pl.BlockSpec(memory_space=pl.ANY)
SMEM = pl.BlockSpec(memory_space=pltpu.SMEM)

D_MODEL = 1024
GRID_W = 64
RET_HEADS = 4
RET_DK = 256
RET_DV = 512
RET_CHUNK = 128
ATT_HEADS = 8
ATT_KV_HEADS = 2
ATT_REP = ATT_HEADS // ATT_KV_HEADS
ATT_HEAD_DIM = 128
ROPE_THETA = 10000.0
NORM_EPS = 1e-6
IN_COLS = 10752
KV_COLS = 3584
C_RK, C_RV, C_AK, C_AV, C_RQ, C_RG, C_AQ, C_AG, C_MR, C_MA = 0, 1024, 3072, 3328, 3584, 4608, 6656, 7680, 8704, 9728
N_SHARD = 4
ADA_W = 3 * D_MODEL // N_SHARD
IN_W = IN_COLS // N_SHARD
IN_BLK = IN_W // 3
N_IN_BLK = IN_COLS // IN_BLK
TM = 512
ATT_TQ = 512
ADAM_LR, ADAM_B1, ADAM_B2, ADAM_EPS, ADAM_WD, ADAM_STEP = 0.001, 0.9, 0.999, 1e-08, 0.01, 10
MIB = 1024 * 1024


def _cp(sem=None, vmem_mb=None, **kw):
    if sem is not None:
        kw["dimension_semantics"] = sem
    if vmem_mb is not None:
        kw["vmem_limit_bytes"] = vmem_mb * MIB
    return pltpu.CompilerParams(**kw)


def _dot(a, b, ca=1, cb=0):
    return lax.dot_general(a.astype(BF), b.astype(BF), (((ca,), (cb,)), ((), ())), preferred_element_type=F32)


def _sigmoid(x):
    return 1.0 / (1.0 + jnp.exp(-x))


def _sum_all(x):
    return jnp.sum(jnp.sum(x, axis=1, keepdims=True), axis=0, keepdims=True)


def _swap_pairs(x):
    ax = x.ndim - 1
    lane = lax.broadcasted_iota(jnp.int32, x.shape, ax)
    nxt = pltpu.roll(x, x.shape[ax] - 1, ax)
    prv = pltpu.roll(x, 1, ax)
    return jnp.where(lane % 2 == 0, nxt, prv)


def _rms(x):
    return lax.rsqrt(jnp.mean(x * x, axis=-1, keepdims=True) + NORM_EPS)


def _rms_bwd(dxh, xh, r):
    return r * (dxh - xh * jnp.mean(dxh * xh, axis=-1, keepdims=True))


class _Comm(NamedTuple):
    name: str
    ins: tuple
    out_shapes: tuple
    aliases: dict
    n_sems: int
    phases: tuple
    arg_aliases: tuple = ()


def _join_comms(*comms):
    comms = [cm for cm in comms if cm is not None]
    if len(comms) <= 1:
        return comms[0] if comms else None
    offs, i_off, o_off, s_off = [], 0, 0, 0
    for cm in comms:
        offs.append((i_off, o_off, s_off))
        i_off, o_off, s_off = i_off + len(cm.ins), o_off + len(cm.out_shapes), s_off + cm.n_sems

    def phase(k):
        def run(ins, outs, ssem, rsem, base):
            sends, recvs = [], []
            for cm, (io, oo, so) in zip(comms, offs):
                if k < len(cm.phases):
                    s, r = cm.phases[k](ins[io:io + len(cm.ins)], outs[oo:oo + len(cm.out_shapes)], ssem, rsem,
                                        base + so)
                    sends += s
                    recvs += r
            return sends, recvs
        return run

    aliases, arg_aliases = {}, ()
    for cm, (io, oo, _) in zip(comms, offs):
        aliases.update({io + a: oo + b for a, b in cm.aliases.items()})
        arg_aliases += tuple((a, oo + b) for a, b in cm.arg_aliases)
    return _Comm("+".join(cm.name for cm in comms), sum((cm.ins for cm in comms), ()),
                 sum((cm.out_shapes for cm in comms), ()), aliases, s_off,
                 tuple(phase(k) for k in range(max(len(cm.phases) for cm in comms))), arg_aliases)


def _run_phases(comm, cins, couts, ssem, rsem, first_started):
    for k, phase in enumerate(comm.phases):
        sends, recvs = phase(cins, couts, ssem, rsem, 0)
        if k > 0 or not first_started:
            for cp in sends:
                cp.start()
        for cp in recvs:
            cp.wait_recv()
        for cp in sends:
            cp.wait_send()


def _call(body, args, comm=None, *, name, grid, in_specs, out_specs, out_shape, scratch_shapes=(),
          compiler_params, aliases=None, prefetch=None):
    single = not isinstance(out_shape, (tuple, list))
    out_specs_t = (out_specs,) if single else tuple(out_specs)
    out_shape_t = (out_shape,) if single else tuple(out_shape)
    n_pre = 0 if prefetch is None else 1
    n_in, n_out, n_sc = len(in_specs), len(out_specs_t), len(scratch_shapes)
    io_alias = {n_pre + a: b for a, b in (aliases or {}).items()}
    if comm is None:
        kernel_body, cin, cout, csems = body, [], [], []
    else:
        n_ci, n_co = len(comm.ins), len(comm.out_shapes)
        cin, cout = [ANY] * n_ci, [ANY] * n_co
        csems = [pltpu.SemaphoreType.DMA((comm.n_sems,)), pltpu.SemaphoreType.DMA((comm.n_sems,))]
        io_alias.update({n_pre + n_in + a: n_out + b for a, b in comm.aliases.items()})
        io_alias.update({n_pre + a: n_out + b for a, b in comm.arg_aliases})

        def kernel_body(*refs):
            pre, refs = refs[:n_pre], refs[n_pre:]
            ins, cins = refs[:n_in], refs[n_in:n_in + n_ci]
            outs = refs[n_in + n_ci:n_in + n_ci + n_out]
            couts = refs[n_in + n_ci + n_out:n_in + n_ci + n_out + n_co]
            scratch = refs[n_in + n_ci + n_out + n_co:n_in + n_ci + n_out + n_co + n_sc]
            ssem, rsem = refs[-2:]
            first = functools.reduce(jnp.logical_and, [pl.program_id(k) == 0 for k in range(len(grid))])
            last = functools.reduce(jnp.logical_and, [pl.program_id(k) == grid[k] - 1 for k in range(len(grid))])

            @pl.when(first)
            def _():
                for cp in comm.phases[0](cins, couts, ssem, rsem, 0)[0]:
                    cp.start()

            body(*pre, *ins, *outs, *scratch)

            @pl.when(last)
            def _():
                _run_phases(comm, cins, couts, ssem, rsem, True)

        name = name + "+" + comm.name

    all_in, all_out = list(in_specs) + cin, out_specs_t + tuple(cout)
    shapes = out_shape_t + (tuple(comm.out_shapes) if comm is not None else ())
    scratch = list(scratch_shapes) + csems
    if prefetch is None:
        res = pl.pallas_call(kernel_body, name=name, grid=grid, in_specs=all_in, out_specs=all_out, out_shape=shapes,
                             scratch_shapes=scratch, input_output_aliases=io_alias,
                             compiler_params=compiler_params)(*args, *(comm.ins if comm is not None else ()))
    else:
        res = pl.pallas_call(
            kernel_body, name=name, out_shape=shapes, input_output_aliases=io_alias, compiler_params=compiler_params,
            grid_spec=pltpu.PrefetchScalarGridSpec(num_scalar_prefetch=1, grid=grid, in_specs=all_in,
                                                   out_specs=all_out, scratch_shapes=scratch))(
                                                       prefetch, *args, *(comm.ins if comm is not None else ()))
    own = res[0] if single else tuple(res[:n_out])
    return own if comm is None else (own, tuple(res[n_out:]))


def _run_comm(comm):
    n_ci, n_co = len(comm.ins), len(comm.out_shapes)

    def body(*refs):
        _run_phases(comm, refs[:n_ci], refs[n_ci:n_ci + n_co], refs[-2], refs[-1], False)

    return pl.pallas_call(
        body, name=comm.name, in_specs=[ANY] * n_ci, out_specs=tuple([ANY] * n_co), out_shape=tuple(comm.out_shapes),
        input_output_aliases=dict(comm.aliases),
        scratch_shapes=[pltpu.SemaphoreType.DMA((comm.n_sems,)), pltpu.SemaphoreType.DMA((comm.n_sems,))],
        compiler_params=_cp(has_side_effects=True))(*comm.ins)


def _adaln_fwd(cvec8, w_ada_g, b_ada):
    def body(c_ref, w_ref, b_ref, o_ref):
        cv = c_ref[...]
        sc = (cv * _sigmoid(cv)).astype(BF)
        for s in range(N_SHARD):
            cols = slice(s * ADA_W, (s + 1) * ADA_W)
            o_ref[:, cols] = jnp.dot(sc, w_ref[s], preferred_element_type=F32) + b_ref[:, cols]

    return pl.pallas_call(body, out_shape=SDS((8, 3 * D_MODEL), F32), name="adaln_fwd",
                          compiler_params=_cp(vmem_mb=32))(cvec8, w_ada_g, b_ada)


def _adaln_bwd(cvec, dmod, w_ada_g):
    def body(c_ref, d_ref, w_ref, gw_ref, gb_ref, dc_ref):
        cv = c_ref[...]
        sg = _sigmoid(cv)
        sc = cv * sg
        dm = d_ref[...]
        gb_ref[...] = jnp.sum(dm, axis=0, keepdims=True)
        dsc = jnp.zeros(cv.shape, F32)
        for s in range(N_SHARD):
            cols = slice(s * ADA_W, (s + 1) * ADA_W)
            gw_ref[s] = _dot(sc, dm[:, cols], 0, 0)
            dsc = dsc + _dot(dm[:, cols], w_ref[s], 1, 1)
        dc_ref[...] = dsc * (sg * (1.0 + cv * (1.0 - sg)))

    return pl.pallas_call(
        body, name="adaln_bwd",
        out_shape=(SDS((N_SHARD, D_MODEL, ADA_W), F32), SDS((1, 3 * D_MODEL), F32), SDS(cvec.shape, F32)),
        compiler_params=_cp(vmem_mb=48))(cvec, dmod, w_ada_g)


def _big_rows(rows):
    return 1536 if rows % 1536 == 0 else TM


def _norm_fwd(x_lat, x_ctx, norm_w, scale3, shift3, tiles_per_sample, n_samp):
    n_lat = x_lat.shape[0] // TM
    rows = x_lat.shape[0] + x_ctx.shape[0]

    def samp(i):
        return jnp.minimum(i // tiles_per_sample, n_samp)

    def body(x_ref, c_ref, nw_ref, sc_ref, sh_ref, hx_ref, hxt_ref):
        x = jnp.where(pl.program_id(0) < n_lat, x_ref[...], c_ref[...])
        h = x * _rms(x) * nw_ref[...] * (1.0 + sc_ref[...]) + sh_ref[...]
        hx_ref[...] = h.astype(BF)
        hxt_ref[...] = h.T.astype(BF)

    return pl.pallas_call(
        body, name="norm_fwd", grid=(rows // TM,),
        in_specs=[pl.BlockSpec((TM, D_MODEL), lambda i: (jnp.minimum(i, n_lat - 1), 0)),
                  pl.BlockSpec((TM, D_MODEL), lambda i: (jnp.maximum(i - n_lat, 0), 0)),
                  pl.BlockSpec((1, D_MODEL), lambda i: (0, 0)),
                  pl.BlockSpec((None, 1, D_MODEL), lambda i: (samp(i), 0, 0)),
                  pl.BlockSpec((None, 1, D_MODEL), lambda i: (samp(i), 0, 0))],
        out_specs=(pl.BlockSpec((TM, D_MODEL), lambda i: (i, 0)),
                   pl.BlockSpec((D_MODEL, TM), lambda i: (0, i))),
        out_shape=(SDS((rows, D_MODEL), BF), SDS((D_MODEL, rows), BF)),
        compiler_params=_cp(("parallel",), 40))(x_lat, x_ctx, norm_w, scale3, shift3)


def _in_proj(hx, w_in_g, ids, first, count, px=None, comm=None):
    rows = hx.shape[0]
    tb = _big_rows(rows)

    def shard(j, ids_ref):
        return ids_ref[4 + first + j // 3]

    def body(ids_ref, h_ref, w_ref, *rest):
        px_ref = rest[-1]
        px_ref[...] = jnp.dot(h_ref[...], w_ref[...], preferred_element_type=F32).astype(BF)

    args, in_specs, aliases = [hx, w_in_g], [
        pl.BlockSpec((tb, D_MODEL), lambda j, i, ids_ref: (i, 0)),
        pl.BlockSpec((None, D_MODEL, IN_BLK), lambda j, i, ids_ref: (shard(j, ids_ref), 0, j % 3))], None
    if px is not None:
        args, in_specs, aliases = args + [px], in_specs + [ANY], {2: 0}
    return _call(body, args, comm, name="in_proj", grid=(3 * count, rows // tb), in_specs=in_specs,
                 out_specs=pl.BlockSpec((tb, IN_BLK), lambda j, i, ids_ref: (i, 3 * shard(j, ids_ref) + j % 3)),
                 out_shape=SDS((rows, IN_COLS), BF), aliases=aliases, prefetch=ids,
                 compiler_params=_cp(("arbitrary", "arbitrary"), 40))


def _norm_bwd(x_lat, x_ctx, dhx, gx_res, norm_w, scale3, tiles_per_sample, n_samp, comm=None):
    rows = x_lat.shape[0] + x_ctx.shape[0]
    n_lat = tiles_per_sample * n_samp

    def samp(i):
        return jnp.minimum(i // tiles_per_sample, n_samp)

    def lat(i):
        return jnp.minimum(i, n_lat - 1)

    def body(x_ref, c_ref, dh_ref, gr_ref, nw_ref, sc_ref, gx_ref, dsh_ref, dsc_ref, dnw_ref):
        i = pl.program_id(0)
        x = jnp.where(i < n_lat, x_ref[...], c_ref[...])
        r = _rms(x)
        xh = x * r
        nw = nw_ref[...]
        dh = dh_ref[...]
        first = jnp.logical_or(i % tiles_per_sample == 0, i >= n_lat)

        @pl.when(first)
        def _():
            dsh_ref[...] = jnp.zeros_like(dsh_ref)
            dsc_ref[...] = jnp.zeros_like(dsc_ref)

        @pl.when(i == 0)
        def _():
            dnw_ref[...] = jnp.zeros_like(dnw_ref)

        dsh_ref[...] += jnp.sum(dh, axis=0, keepdims=True)
        dsc_ref[...] += jnp.sum(dh * (xh * nw), axis=0, keepdims=True)
        du = dh * (1.0 + sc_ref[...])
        dnw_ref[...] += jnp.sum(du * xh, axis=0, keepdims=True)

        @pl.when(i < n_lat)
        def _():
            gx_ref[...] = gr_ref[...] + _rms_bwd(du * nw, xh, r)

    return _call(
        body, [x_lat, x_ctx, dhx, gx_res, norm_w, scale3], comm, name="norm_bwd", grid=(rows // TM,),
        in_specs=[pl.BlockSpec((TM, D_MODEL), lambda i: (lat(i), 0)),
                  pl.BlockSpec((TM, D_MODEL), lambda i: (jnp.maximum(i - n_lat, 0), 0)),
                  pl.BlockSpec((TM, D_MODEL), lambda i: (i, 0)),
                  pl.BlockSpec((TM, D_MODEL), lambda i: (lat(i), 0)),
                  pl.BlockSpec((1, D_MODEL), lambda i: (0, 0)),
                  pl.BlockSpec((None, 1, D_MODEL), lambda i: (samp(i), 0, 0))],
        out_specs=(pl.BlockSpec((TM, D_MODEL), lambda i: (lat(i), 0)),
                   pl.BlockSpec((None, 1, D_MODEL), lambda i: (samp(i), 0, 0)),
                   pl.BlockSpec((None, 1, D_MODEL), lambda i: (samp(i), 0, 0)),
                   pl.BlockSpec((1, D_MODEL), lambda i: (0, 0))),
        out_shape=(SDS((n_lat * TM, D_MODEL), F32), SDS((n_samp + 1, 1, D_MODEL), F32),
                   SDS((n_samp + 1, 1, D_MODEL), F32), SDS((1, D_MODEL), F32)),
        compiler_params=_cp(("arbitrary",), 40))


def _gw_in(hxt, dpx_all, part, n_parts, comm=None):
    rows = dpx_all.shape[0]
    tb = _big_rows(rows)
    dp = D_MODEL // n_parts

    def body(h_ref, d_ref, o_ref):
        @pl.when(pl.program_id(1) == 0)
        def _():
            o_ref[...] = jnp.zeros_like(o_ref)

        o_ref[...] += jnp.dot(h_ref[...], d_ref[...], preferred_element_type=F32)

    return _call(body, [hxt, dpx_all], comm, name="gw_in", grid=(N_IN_BLK, rows // tb),
                 in_specs=[pl.BlockSpec((dp, tb), lambda j, i: (part, i)),
                           pl.BlockSpec((tb, IN_BLK), lambda j, i: (i, j))],
                 out_specs=pl.BlockSpec((None, dp, IN_BLK), lambda j, i: (j // 3, 0, j % 3)),
                 out_shape=SDS((N_SHARD, dp, IN_W), F32),
                 compiler_params=_cp(("arbitrary", "arbitrary"), 40))


def _dhx(dpx_all, w_in_g, tile0, n_tiles, dhx=None, comm=None):
    rows = dpx_all.shape[0]
    tb = _big_rows(rows)

    def body(d_ref, w_ref, *rest):
        o_ref = rest[-1]

        @pl.when(pl.program_id(1) == 0)
        def _():
            o_ref[...] = jnp.zeros_like(o_ref)

        o_ref[...] += lax.dot_general(d_ref[...], w_ref[...], (((1,), (1,)), ((), ())), preferred_element_type=F32)

    args, in_specs, aliases = [dpx_all, w_in_g], [
        pl.BlockSpec((tb, IN_BLK), lambda i, j: (tile0 + i, j)),
        pl.BlockSpec((None, D_MODEL, IN_BLK), lambda i, j: (j // 3, 0, j % 3))], None
    if dhx is not None:
        args, in_specs, aliases = args + [dhx], in_specs + [ANY], {2: 0}
    return _call(body, args, comm, name="dhx", grid=(n_tiles, N_IN_BLK), in_specs=in_specs,
                 out_specs=pl.BlockSpec((tb, D_MODEL), lambda i, j: (tile0 + i, 0)),
                 out_shape=SDS((rows, D_MODEL), F32), aliases=aliases,
                 compiler_params=_cp(("arbitrary", "arbitrary"), 40))


def _decays(lgv, d):
    c = RET_CHUNK
    ii = lax.broadcasted_iota(jnp.int32, (c, 1), 0).astype(F32)
    jj = lax.broadcasted_iota(jnp.int32, (1, c), 1).astype(F32)
    a_i = jnp.where(d == 0, ii, c - 1.0 - ii)
    a_j = jnp.where(d == 0, jj, c - 1.0 - jj)
    rel = a_i - a_j
    mask = jnp.where(rel >= 0, jnp.exp(lgv * jnp.maximum(rel, 0.0)), 0.0)
    qd = jnp.exp(lgv * (a_i + 1.0))
    kd = jnp.exp(lgv * (c - 1.0 - a_i))
    gc = jnp.exp(jnp.full((1, 1), lgv * c, F32))
    return a_i, rel, mask, qd, kd, gc


def _ctx_state_fwd(px, lg, n_samp, t_lat, lc):
    rb = t_lat // lc

    def body(lg_ref, k_ref, v_ref, o_ref):
        h = pl.program_id(1)
        k = k_ref[...].astype(F32) * (RET_DK ** -0.5)
        v = v_ref[...]
        pos = lax.broadcasted_iota(jnp.int32, (lc, 1), 0).astype(F32)
        o_ref[0] = _dot(k * jnp.exp(lg_ref[0, h] * (lc - 1.0 - pos)), v, 0, 0)
        o_ref[1] = _dot(k * jnp.exp(lg_ref[1, h] * pos), v, 0, 0)

    return pl.pallas_call(
        body, name="ctx_state_fwd", grid=(n_samp, RET_HEADS),
        in_specs=[SMEM,
                  pl.BlockSpec((lc, RET_DK), lambda b, h: (rb + b, C_RK // RET_DK + h)),
                  pl.BlockSpec((lc, RET_DV), lambda b, h: (rb + b, C_RV // RET_DV + h))],
        out_specs=pl.BlockSpec((None, 2, None, RET_DK, RET_DV), lambda b, h: (b, 0, h, 0, 0)),
        out_shape=SDS((n_samp, 2, RET_HEADS, RET_DK, RET_DV), F32),
        compiler_params=_cp(("parallel", "parallel")))(lg, px, px)


def _ctx_state_bwd(dpx, px, dstates, lg, n_samp, t_lat, lc):
    rb = t_lat // lc
    kspec = pl.BlockSpec((lc, RET_DK), lambda b, h: (rb + b, C_RK // RET_DK + h))
    vspec = pl.BlockSpec((lc, RET_DV), lambda b, h: (rb + b, C_RV // RET_DV + h))
    sspec = pl.BlockSpec((None, 2, None, RET_DK, RET_DV), lambda b, h: (b, 0, h, 0, 0))

    def weights(lg_ref, h):
        pos = lax.broadcasted_iota(jnp.int32, (lc, 1), 0).astype(F32)
        e_f = lc - 1.0 - pos
        return pos, e_f, jnp.exp(lg_ref[0, h] * e_f), jnp.exp(lg_ref[1, h] * pos)

    def k_body(lg_ref, dpx_hbm, k_ref, v_ref, ds_ref, dk_ref, dlg_ref):
        pos, e_f, w_f, w_b = weights(lg_ref, pl.program_id(1))
        k = k_ref[...].astype(F32) * (RET_DK ** -0.5)
        y_f = _dot(v_ref[...], ds_ref[0], 1, 1) * w_f
        y_b = _dot(v_ref[...], ds_ref[1], 1, 1) * w_b
        dk_ref[...] = ((y_f + y_b) * (RET_DK ** -0.5)).astype(BF)
        t_f = _sum_all(e_f * k * y_f)
        t_b = _sum_all(pos * k * y_b)
        sub = lax.broadcasted_iota(jnp.int32, (8, 128), 0)
        dlg_ref[...] = jnp.where(sub == 0, t_f, jnp.where(sub == 1, t_b, 0.0))

    def v_body(lg_ref, dpx_hbm, k_ref, ds_ref, dv_ref):
        _, _, w_f, w_b = weights(lg_ref, pl.program_id(1))
        k = k_ref[...].astype(F32) * (RET_DK ** -0.5)
        dv_ref[...] = (_dot(k * w_f, ds_ref[0]) + _dot(k * w_b, ds_ref[1])).astype(BF)

    dpx, dlg = pl.pallas_call(
        k_body, name="ctx_state_bwd_k", grid=(n_samp, RET_HEADS), input_output_aliases={1: 0},
        in_specs=[SMEM, ANY, kspec, vspec, sspec],
        out_specs=(kspec, pl.BlockSpec((None, None, 8, 128), lambda b, h: (b, h, 0, 0))),
        out_shape=(SDS(dpx.shape, dpx.dtype), SDS((n_samp, RET_HEADS, 8, 128), F32)),
        compiler_params=_cp(("parallel", "parallel")))(lg, dpx, px, px, dstates)
    dpx = pl.pallas_call(
        v_body, name="ctx_state_bwd_v", grid=(n_samp, RET_HEADS), input_output_aliases={1: 0},
        in_specs=[SMEM, ANY, kspec, sspec], out_specs=vspec, out_shape=SDS(dpx.shape, dpx.dtype),
        compiler_params=_cp(("parallel", "parallel")))(lg, dpx, px, dstates)
    return dpx, dlg


def _zero_ctx_tail(dpx, t_lat):
    wb = 512
    n_ctx = (dpx.shape[0] - t_lat) // TM

    def body(dpx_hbm, o_ref):
        o_ref[...] = jnp.zeros_like(o_ref)

    return pl.pallas_call(
        body, name="zero_ctx_tail", grid=(n_ctx, (IN_COLS - KV_COLS) // wb), input_output_aliases={0: 0},
        in_specs=[ANY], out_specs=pl.BlockSpec((TM, wb), lambda i, j: (t_lat // TM + i, KV_COLS // wb + j)),
        out_shape=SDS(dpx.shape, dpx.dtype),
        compiler_params=_cp(("parallel", "parallel")))(dpx)


def _ret_specs(row_f, row_b):
    c = RET_CHUNK
    wq = RET_HEADS * RET_DK // 2
    wv = RET_HEADS * RET_DV // 2
    specs = []
    for row in (row_f, row_b):
        specs += [pl.BlockSpec((c, wq), lambda b, n, row=row: (row(b, n), C_RQ // wq)),
                  pl.BlockSpec((c, wq), lambda b, n, row=row: (row(b, n), C_RQ // wq + 1)),
                  pl.BlockSpec((c, 2 * wq), lambda b, n, row=row: (row(b, n), C_RK // (2 * wq))),
                  pl.BlockSpec((c, wv), lambda b, n, row=row: (row(b, n), C_RV // wv)),
                  pl.BlockSpec((c, wv), lambda b, n, row=row: (row(b, n), C_RV // wv + 1))]
    return specs


def _ret_head(refs, h):
    q0, q1, k_ref, v0, v1 = refs
    hh = h % 2
    q = (q0, q1)[h // 2][:, hh * RET_DK:(hh + 1) * RET_DK].astype(F32)
    k = k_ref[:, h * RET_DK:(h + 1) * RET_DK].astype(F32) * (RET_DK ** -0.5)
    v = (v0, v1)[h // 2][:, hh * RET_DV:(hh + 1) * RET_DV]
    return q, k, v


def _ret_fwd(px, states0, lg, n_samp, seq, comm=None):
    c = RET_CHUNK
    nc = seq // c
    t_lat = n_samp * seq
    wo = RET_HEADS * RET_DV

    def row_f(b, n):
        return b * nc + n

    def row_b(b, n):
        return b * nc + nc - 1 - n

    def body(lg_ref, *refs):
        ins, (s0_ref, of_ref, ob_ref, st_ref, s_s) = refs[:10], refs[10:]

        @pl.when(pl.program_id(1) == 0)
        def _():
            s_s[...] = s0_ref[...]

        for d, o_ref in ((0, of_ref), (1, ob_ref)):
            for h in range(RET_HEADS):
                _, _, mask, qd, kd, gc = _decays(lg_ref[d, h], d)
                q, k, v = _ret_head(ins[5 * d:5 * d + 5], h)
                s = s_s[d, h]
                st_ref[h, d] = s.astype(BF)
                sc = _dot(q, k, 1, 1) * mask
                o_ref[:, h * RET_DV:(h + 1) * RET_DV] = (_dot(sc, v) + _dot(q * qd, s)).astype(BF)
                s_s[d, h] = s * gc + _dot(k * kd, v, 0, 0)

    return _call(
        body, [lg] + [px] * 10 + [states0], comm, name="ret_fwd", grid=(n_samp, nc),
        in_specs=[SMEM] + _ret_specs(row_f, row_b) + [
            pl.BlockSpec((None, 2, RET_HEADS, RET_DK, RET_DV), lambda b, n: (b, 0, 0, 0, 0))],
        out_specs=(pl.BlockSpec((c, wo), lambda b, n: (row_f(b, n), 0)),
                   pl.BlockSpec((c, wo), lambda b, n: (row_b(b, n), 0)),
                   pl.BlockSpec((None, RET_HEADS, 2, None, RET_DK, RET_DV), lambda b, n: (b, 0, 0, n, 0, 0))),
        out_shape=(SDS((t_lat, wo), BF), SDS((t_lat, wo), BF),
                   SDS((n_samp, RET_HEADS, 2, nc, RET_DK, RET_DV), BF)),
        scratch_shapes=[pltpu.VMEM((2, RET_HEADS, RET_DK, RET_DV), F32)],
        compiler_params=_cp(("arbitrary", "arbitrary"), 48))


def _ret_bwd(px, do, saved, lg, n_samp, seq, comm=None):
    c = RET_CHUNK
    nc = seq // c
    t_lat = n_samp * seq
    wq, wo = RET_HEADS * RET_DK, RET_HEADS * RET_DV

    def row_f(b, n):
        return b * nc + nc - 1 - n

    def row_b(b, n):
        return b * nc + n

    def body(lg_ref, *refs):
        ins = refs[:10]
        (dof_ref, dob_ref, st_ref, dqf, dkf, dvf, dqb, dkb, dvb, ds0_ref, dlg_ref, ds_s, acc_s) = refs[10:]
        n = pl.program_id(1)

        @pl.when(n == 0)
        def _():
            ds_s[...] = jnp.zeros_like(ds_s)
            acc_s[...] = jnp.zeros_like(acc_s)

        for d, (do_ref, dq_ref, dk_ref, dv_ref) in enumerate(((dof_ref, dqf, dkf, dvf), (dob_ref, dqb, dkb, dvb))):
            for h in range(RET_HEADS):
                a_i, rel, mask, qd, kd, gc = _decays(lg_ref[d, h], d)
                q, k, v = _ret_head(ins[5 * d:5 * d + 5], h)
                qb, kb, vb = q.astype(BF), k.astype(BF), v.astype(BF)
                dob = do_ref[:, h * RET_DV:(h + 1) * RET_DV].astype(BF)
                sb = st_ref[h, d]
                ds = ds_s[d, h]
                dsb = ds.astype(BF)
                raw = _dot(qb, kb, 1, 1)
                sc = raw * mask
                dsc = _dot(dob, vb, 1, 1) * mask
                dscb = dsc.astype(BF)
                x = _dot(dob, sb, 1, 1)
                y = _dot(vb, dsb, 1, 1)
                qq = q * qd
                kk = k * kd
                dq_ref[:, h * RET_DK:(h + 1) * RET_DK] = (_dot(dscb, kb) + x * qd).astype(BF)
                dk_ref[:, h * RET_DK:(h + 1) * RET_DK] = (_dot(dscb, qb, 0, 0) + y * kd).astype(BF)
                dv_ref[:, h * RET_DV:(h + 1) * RET_DV] = (_dot(sc, dob, 0, 0) + _dot(kk, dsb)).astype(BF)
                t = (_sum_all(dsc * raw * rel) + _sum_all((a_i + 1.0) * qq * x)
                     + _sum_all((c - 1.0 - a_i) * kk * y) + c * gc * _sum_all(ds * sb.astype(F32)))
                acc_s[4 * d + h:4 * d + h + 1, :] += t
                ds_s[d, h] = ds * gc + _dot(qq, dob, 0, 0)

        @pl.when(n == nc - 1)
        def _():
            ds0_ref[...] = ds_s[...]
            dlg_ref[...] = acc_s[...]

    do_spec_f = pl.BlockSpec((c, wo), lambda b, n: (row_f(b, n), 0))
    do_spec_b = pl.BlockSpec((c, wo), lambda b, n: (row_b(b, n), 0))
    dq_spec_f = pl.BlockSpec((c, wq), lambda b, n: (row_f(b, n), 0))
    dq_spec_b = pl.BlockSpec((c, wq), lambda b, n: (row_b(b, n), 0))
    return _call(
        body, [lg] + [px] * 10 + [do, do, saved], comm, name="ret_bwd", grid=(n_samp, nc),
        in_specs=[SMEM] + _ret_specs(row_f, row_b) + [
            do_spec_f, do_spec_b,
            pl.BlockSpec((None, RET_HEADS, 2, None, RET_DK, RET_DV), lambda b, n: (b, 0, 0, nc - 1 - n, 0, 0))],
        out_specs=(dq_spec_f, dq_spec_f, do_spec_f, dq_spec_b, dq_spec_b, do_spec_b,
                   pl.BlockSpec((None, 2, RET_HEADS, RET_DK, RET_DV), lambda b, n: (b, 0, 0, 0, 0)),
                   pl.BlockSpec((None, 8, 128), lambda b, n: (b, 0, 0))),
        out_shape=(SDS((t_lat, wq), BF), SDS((t_lat, wq), BF), SDS((t_lat, wo), BF),
                   SDS((t_lat, wq), BF), SDS((t_lat, wq), BF), SDS((t_lat, wo), BF),
                   SDS((n_samp, 2, RET_HEADS, RET_DK, RET_DV), F32), SDS((n_samp, 8, 128), F32)),
        scratch_shapes=[pltpu.VMEM((2, RET_HEADS, RET_DK, RET_DV), F32), pltpu.VMEM((8, 128), F32)],
        compiler_params=_cp(("arbitrary", "arbitrary"), 56))


def _combine_into(dpx, a, b, col0, scale):
    t_lat, width = a.shape
    wb = 512
    assert col0 % wb == 0 and width % wb == 0

    def body(dpx_hbm, a_ref, b_ref, o_ref):
        o_ref[...] = ((a_ref[...].astype(F32) + b_ref[...].astype(F32)) * scale).astype(BF)

    src = pl.BlockSpec((TM, wb), lambda i, j: (i, j))
    return pl.pallas_call(
        body, name="combine_into", grid=(t_lat // TM, width // wb), input_output_aliases={0: 0},
        in_specs=[ANY, src, src], out_specs=pl.BlockSpec((TM, wb), lambda i, j: (i, col0 // wb + j)),
        out_shape=SDS(dpx.shape, dpx.dtype),
        compiler_params=_cp(("parallel", "parallel")))(dpx, a, b)


def _retnorm_fwd(o_f, o_b, px):
    t_lat = o_f.shape[0]

    def body(of_ref, ob_ref, g_ref, y_ref):
        o = of_ref[...].astype(F32) + ob_ref[...].astype(F32)
        g = g_ref[...].astype(F32)
        y_ref[...] = (o * _rms(o) * (g * _sigmoid(g))).astype(BF)

    so = pl.BlockSpec((TM, RET_DV), lambda i, h: (i, h))
    return pl.pallas_call(
        body, name="retnorm_fwd", grid=(t_lat // TM, RET_HEADS),
        in_specs=[so, so, pl.BlockSpec((TM, RET_DV), lambda i, h: (i, C_RG // RET_DV + h))],
        out_specs=so,
        out_shape=SDS((t_lat, RET_HEADS * RET_DV), BF),
        compiler_params=_cp(("parallel", "parallel")))(o_f, o_b, px)


def _retnorm_bwd(dpx, dy, o_f, o_b, px):
    t_lat = o_f.shape[0]

    def body(dpx_hbm, dy_ref, of_ref, ob_ref, g_ref, do_ref, dg_ref):
        o = of_ref[...].astype(F32) + ob_ref[...].astype(F32)
        r = _rms(o)
        on = o * r
        g = g_ref[...].astype(F32)
        sg = _sigmoid(g)
        dy_ = dy_ref[...].astype(F32)
        dg_ref[...] = (dy_ * on * (sg * (1.0 + g * (1.0 - sg)))).astype(BF)
        do_ref[...] = _rms_bwd(dy_ * (g * sg), on, r).astype(BF)

    so = pl.BlockSpec((TM, RET_DV), lambda i, h: (i, h))
    gcol = pl.BlockSpec((TM, RET_DV), lambda i, h: (i, C_RG // RET_DV + h))
    return pl.pallas_call(
        body, name="retnorm_bwd", grid=(t_lat // TM, RET_HEADS), input_output_aliases={0: 1},
        in_specs=[ANY, so, so, so, gcol],
        out_specs=(so, gcol),
        out_shape=(SDS((t_lat, RET_HEADS * RET_DV), BF), SDS(dpx.shape, dpx.dtype)),
        compiler_params=_cp(("parallel", "parallel")))(dpx, dy, o_f, o_b, px)


def _norm_rope(x, w, cos, sin):
    xn = x * _rms(x) * w
    return xn * cos + _swap_pairs(xn) * sin


def _norm_rope_bwd(dy, x, w, cos, sin):
    dxn = dy * cos + _swap_pairs(dy * sin)
    r = _rms(x)
    xh = x * r
    return _rms_bwd(dxn * w, xh, r), jnp.sum(dxn * xh, axis=0, keepdims=True)


def _att_prep_q(px, cos_all, sin_all, qnw, t_lat):
    hd = ATT_HEAD_DIM
    wblk = ATT_REP * hd

    def body(x_ref, cos_ref, sin_ref, w_ref, o_ref):
        for r in range(ATT_REP):
            cols = slice(r * hd, (r + 1) * hd)
            qr = _norm_rope(x_ref[:, cols].astype(F32), w_ref[...], cos_ref[...], sin_ref[...])
            o_ref[:, cols] = (qr * (hd ** -0.5)).astype(BF)

    return pl.pallas_call(
        body, name="att_prep_q", grid=(t_lat // TM, ATT_KV_HEADS),
        in_specs=[pl.BlockSpec((TM, wblk), lambda i, g: (i, C_AQ // wblk + g)),
                  pl.BlockSpec((TM, hd), lambda i, g: (i, 0)),
                  pl.BlockSpec((TM, hd), lambda i, g: (i, 0)),
                  pl.BlockSpec((1, hd), lambda i, g: (0, 0))],
        out_specs=pl.BlockSpec((TM, wblk), lambda i, g: (i, g)),
        out_shape=SDS((t_lat, ATT_HEADS * hd), BF),
        compiler_params=_cp(("parallel", "parallel")))(px, cos_all, sin_all, qnw)


def _att_prep_kv(px, cos_all, sin_all, knw):
    rows = px.shape[0]
    hd = ATT_HEAD_DIM
    kvw = ATT_KV_HEADS * hd

    def body(x_ref, cos_ref, sin_ref, w_ref, k_ref, v_ref):
        for g in range(ATT_KV_HEADS):
            cols = slice(g * hd, (g + 1) * hd)
            k_ref[:, cols] = _norm_rope(x_ref[:, cols].astype(F32), w_ref[...], cos_ref[...],
                                        sin_ref[...]).astype(BF)
        v_ref[...] = x_ref[:, kvw:].astype(BF)

    return pl.pallas_call(
        body, name="att_prep_kv", grid=(rows // TM,),
        in_specs=[pl.BlockSpec((TM, 2 * kvw), lambda i: (i, C_AK // (2 * kvw))),
                  pl.BlockSpec((TM, hd), lambda i: (i, 0)),
                  pl.BlockSpec((TM, hd), lambda i: (i, 0)),
                  pl.BlockSpec((1, hd), lambda i: (0, 0))],
        out_specs=(pl.BlockSpec((TM, kvw), lambda i: (i, 0)), pl.BlockSpec((TM, kvw), lambda i: (i, 0))),
        out_shape=(SDS((rows, kvw), BF), SDS((rows, kvw), BF)),
        compiler_params=_cp(("parallel",)))(px, cos_all, sin_all, knw)


def _att_kv_bwd(dpx, dkl, dkc, dvl, dvc, px, cos_all, sin_all, knw):
    rows = px.shape[0]
    hd = ATT_HEAD_DIM
    kvw = ATT_KV_HEADS * hd
    n_lat = dkl.shape[0] // TM
    assert dkc.shape[0] == TM

    def body(dpx_hbm, dkl_ref, dkc_ref, dvl_ref, dvc_ref, x_ref, cos_ref, sin_ref, w_ref, o_ref, gw_ref):
        i = pl.program_id(0)

        @pl.when(i == 0)
        def _():
            gw_ref[...] = jnp.zeros_like(gw_ref)

        is_lat = i < n_lat
        dk = jnp.where(is_lat, dkl_ref[...], dkc_ref[...])
        dv = jnp.where(is_lat, dvl_ref[...], dvc_ref[...])
        for g in range(ATT_KV_HEADS):
            cols = slice(g * hd, (g + 1) * hd)
            dx, gw = _norm_rope_bwd(dk[:, cols], x_ref[:, cols].astype(F32), w_ref[...], cos_ref[...], sin_ref[...])
            o_ref[:, cols] = dx.astype(BF)
            gw_ref[...] += gw
        o_ref[:, kvw:] = dv.astype(BF)

    lat = pl.BlockSpec((TM, kvw), lambda i: (jnp.minimum(i, n_lat - 1), 0))
    ctx = pl.BlockSpec((TM, kvw), lambda i: (0, 0))
    kvcol = pl.BlockSpec((TM, 2 * kvw), lambda i: (i, C_AK // (2 * kvw)))
    return pl.pallas_call(
        body, name="att_kv_bwd", grid=(rows // TM,), input_output_aliases={0: 0},
        in_specs=[ANY, lat, ctx, lat, ctx, kvcol,
                  pl.BlockSpec((TM, hd), lambda i: (i, 0)),
                  pl.BlockSpec((TM, hd), lambda i: (i, 0)),
                  pl.BlockSpec((1, hd), lambda i: (0, 0))],
        out_specs=(kvcol, pl.BlockSpec((1, hd), lambda i: (0, 0))),
        out_shape=(SDS(dpx.shape, dpx.dtype), SDS((1, hd), F32)),
        compiler_params=_cp(("arbitrary",)))(dpx, dkl, dkc, dvl, dvc, px, cos_all, sin_all, knw)


def _stack_heads(ref_or_val):
    hd = ATT_HEAD_DIM
    return jnp.concatenate([ref_or_val[:, r * hd:(r + 1) * hd] for r in range(ATT_REP)], axis=0)


def _att_scores(q, kl, kc):
    sl = _dot(q, kl, 1, 1)
    sc = _dot(q, kc, 1, 1)
    m = jnp.maximum(jnp.max(sl, axis=-1, keepdims=True), jnp.max(sc, axis=-1, keepdims=True))
    el = jnp.exp(sl - m)
    ec = jnp.exp(sc - m)
    denom = jnp.sum(el, axis=-1, keepdims=True) + jnp.sum(ec, axis=-1, keepdims=True)
    return el, ec, denom, m


def _att_fwd(qn, kn, vn, px, n_samp, seq, lc):
    hd = ATT_HEAD_DIM
    tq = ATT_TQ
    nq = seq // tq
    wblk = ATT_REP * hd
    cb = n_samp * seq // lc
    t_lat = n_samp * seq

    def body(q_ref, kl_ref, kc_ref, vl_ref, vc_ref, g_ref, y_ref, o_ref, lse_ref):
        lane = lax.broadcasted_iota(jnp.int32, (tq, hd), 1)
        lse = jnp.zeros((tq, hd), F32)
        for r in range(ATT_REP):
            cols = slice(r * hd, (r + 1) * hd)
            el, ec, denom, m = _att_scores(q_ref[:, cols], kl_ref[...], kc_ref[...])
            o = (_dot(el, vl_ref[...]) + _dot(ec, vc_ref[...])) / denom
            g = g_ref[:, cols].astype(F32)
            o_ref[:, cols] = o.astype(BF)
            y_ref[:, cols] = (o * (g * _sigmoid(g))).astype(BF)
            lse = jnp.where(lane == r, m + jnp.log(denom), lse)
        lse_ref[...] = lse

    return pl.pallas_call(
        body, name="att_fwd", grid=(n_samp, ATT_KV_HEADS, nq),
        in_specs=[pl.BlockSpec((tq, wblk), lambda b, g, i: (b * nq + i, g)),
                  pl.BlockSpec((seq, hd), lambda b, g, i: (b, g)),
                  pl.BlockSpec((lc, hd), lambda b, g, i: (cb + b, g)),
                  pl.BlockSpec((seq, hd), lambda b, g, i: (b, g)),
                  pl.BlockSpec((lc, hd), lambda b, g, i: (cb + b, g)),
                  pl.BlockSpec((tq, wblk), lambda b, g, i: (b * nq + i, C_AG // wblk + g))],
        out_specs=(pl.BlockSpec((tq, wblk), lambda b, g, i: (b * nq + i, g)),
                   pl.BlockSpec((tq, wblk), lambda b, g, i: (b * nq + i, g)),
                   pl.BlockSpec((tq, hd), lambda b, g, i: (b * nq + i, g))),
        out_shape=(SDS((t_lat, ATT_HEADS * hd), BF), SDS((t_lat, ATT_HEADS * hd), BF),
                   SDS((t_lat, ATT_KV_HEADS * hd), F32)),
        compiler_params=_cp(("parallel", "parallel", "parallel"), 48))(qn, kn, kn, vn, vn, px)


def _att_gate_bwd(dpx, dy_att, o_att, px):
    t_lat = dy_att.shape[0]
    wblk = ATT_REP * ATT_HEAD_DIM

    def body(dpx_hbm, dy_ref, o_ref, g_ref, out_ref):
        g = g_ref[...].astype(F32)
        sg = _sigmoid(g)
        out_ref[...] = (dy_ref[...].astype(F32) * o_ref[...].astype(F32) * (sg * (1.0 + g * (1.0 - sg)))).astype(BF)

    blk = pl.BlockSpec((TM, wblk), lambda i, j: (i, j))
    gcol = pl.BlockSpec((TM, wblk), lambda i, j: (i, C_AG // wblk + j))
    return pl.pallas_call(
        body, name="att_gate_bwd", grid=(t_lat // TM, ATT_KV_HEADS),
        in_specs=[ANY, blk, blk, gcol], out_specs=gcol, out_shape=SDS(dpx.shape, dpx.dtype),
        input_output_aliases={0: 0},
        compiler_params=_cp(("parallel", "parallel")))(dpx, dy_att, o_att, px)


def _att_bwd(dpx, qn, kn, vn, px, o_att, lse, dy_att, cos_all, sin_all, qnw, n_samp, seq, lc, comm=None):
    hd = ATT_HEAD_DIM
    tq = ATT_TQ
    nq = seq // tq
    wblk = ATT_REP * hd
    cb = n_samp * seq // lc
    t_lat = n_samp * seq
    kvw = ATT_KV_HEADS * hd
    scale = hd ** -0.5

    def body(dpx_hbm, q_ref, kl_ref, kc_ref, vl_ref, vc_ref, g_ref, o_ref, dy_ref, x_ref, cos_ref, sin_ref, w_ref,
             lse_ref, dq_ref, dkl_ref, dkc_ref, dvl_ref, dvc_ref, gw_ref, akl, akc, avl, avc, aw):
        i = pl.program_id(2)

        @pl.when(i == 0)
        def _():
            akl[...] = jnp.zeros_like(akl)
            akc[...] = jnp.zeros_like(akc)
            avl[...] = jnp.zeros_like(avl)
            avc[...] = jnp.zeros_like(avc)
            aw[...] = jnp.zeros_like(aw)

        dobs, pls, pcs, dsls, dscs = [], [], [], [], []
        for r in range(ATT_REP):
            cols = slice(r * hd, (r + 1) * hd)
            g = g_ref[:, cols].astype(F32)
            sg = _sigmoid(g)
            dy = dy_ref[:, cols].astype(F32)
            do = dy * (g * sg)
            delta = jnp.sum(do * o_ref[:, cols].astype(F32), axis=-1, keepdims=True)
            lse = lse_ref[:, r:r + 1]
            p_l = jnp.exp(_dot(q_ref[:, cols], kl_ref[...], 1, 1) - lse).astype(BF)
            p_c = jnp.exp(_dot(q_ref[:, cols], kc_ref[...], 1, 1) - lse).astype(BF)
            dob = do.astype(BF)
            ds_l = (p_l * (_dot(dob, vl_ref[...], 1, 1) - delta)).astype(BF)
            ds_c = (p_c * (_dot(dob, vc_ref[...], 1, 1) - delta)).astype(BF)
            dq = (_dot(ds_l, kl_ref[...]) + _dot(ds_c, kc_ref[...])) * scale
            dx, gw = _norm_rope_bwd(dq, x_ref[:, cols].astype(F32), w_ref[...], cos_ref[...], sin_ref[...])
            dq_ref[:, cols] = dx.astype(BF)
            aw[...] += gw
            dobs.append(dob)
            pls.append(p_l)
            pcs.append(p_c)
            dsls.append(ds_l)
            dscs.append(ds_c)
        do4 = jnp.concatenate(dobs, axis=0)
        q4 = _stack_heads(q_ref)
        avl[...] += _dot(jnp.concatenate(pls, axis=0), do4, 0, 0)
        avc[...] += _dot(jnp.concatenate(pcs, axis=0), do4, 0, 0)
        akl[...] += _dot(jnp.concatenate(dsls, axis=0), q4, 0, 0)
        akc[...] += _dot(jnp.concatenate(dscs, axis=0), q4, 0, 0)

        @pl.when(i == nq - 1)
        def _():
            dkl_ref[...] = akl[...]
            dkc_ref[...] = akc[...]
            dvl_ref[...] = avl[...]
            dvc_ref[...] = avc[...]
            gw_ref[...] = aw[...]

    return _call(
        body, [dpx, qn, kn, kn, vn, vn, px, o_att, dy_att, px, cos_all, sin_all, qnw, lse], comm,
        name="att_bwd", grid=(n_samp, ATT_KV_HEADS, nq), aliases={0: 0},
        in_specs=[ANY,
                  pl.BlockSpec((tq, wblk), lambda b, g, i: (b * nq + i, g)),
                  pl.BlockSpec((seq, hd), lambda b, g, i: (b, g)),
                  pl.BlockSpec((lc, hd), lambda b, g, i: (cb + b, g)),
                  pl.BlockSpec((seq, hd), lambda b, g, i: (b, g)),
                  pl.BlockSpec((lc, hd), lambda b, g, i: (cb + b, g)),
                  pl.BlockSpec((tq, wblk), lambda b, g, i: (b * nq + i, C_AG // wblk + g)),
                  pl.BlockSpec((tq, wblk), lambda b, g, i: (b * nq + i, g)),
                  pl.BlockSpec((tq, wblk), lambda b, g, i: (b * nq + i, g)),
                  pl.BlockSpec((tq, wblk), lambda b, g, i: (b * nq + i, C_AQ // wblk + g)),
                  pl.BlockSpec((tq, hd), lambda b, g, i: (b * nq + i, 0)),
                  pl.BlockSpec((tq, hd), lambda b, g, i: (b * nq + i, 0)),
                  pl.BlockSpec((1, hd), lambda b, g, i: (0, 0)),
                  pl.BlockSpec((tq, hd), lambda b, g, i: (b * nq + i, g))],
        out_specs=(pl.BlockSpec((tq, wblk), lambda b, g, i: (b * nq + i, C_AQ // wblk + g)),
                   pl.BlockSpec((seq, hd), lambda b, g, i: (b, g)),
                   pl.BlockSpec((lc, hd), lambda b, g, i: (b, g)),
                   pl.BlockSpec((seq, hd), lambda b, g, i: (b, g)),
                   pl.BlockSpec((lc, hd), lambda b, g, i: (b, g)),
                   pl.BlockSpec((None, None, 1, hd), lambda b, g, i: (b, g, 0, 0))),
        out_shape=(SDS(dpx.shape, dpx.dtype),
                   SDS((t_lat, kvw), F32), SDS((n_samp * lc, kvw), F32),
                   SDS((t_lat, kvw), F32), SDS((n_samp * lc, kvw), F32),
                   SDS((n_samp, ATT_KV_HEADS, 1, hd), F32)),
        scratch_shapes=[pltpu.VMEM((seq, hd), F32), pltpu.VMEM((lc, hd), F32),
                        pltpu.VMEM((seq, hd), F32), pltpu.VMEM((lc, hd), F32), pltpu.VMEM((1, hd), F32)],
        compiler_params=_cp(("arbitrary", "arbitrary", "arbitrary"), 56))


def _merge(x_lat, target, y_ret, y_att, px, gate3, w_o_ret, w_o_att, w_out, tiles_per_sample):
    t_lat = x_lat.shape[0]
    tm = 256
    n_t = t_lat // tm
    per = tiles_per_sample * (TM // tm)
    d = D_MODEL
    rv = RET_HEADS * RET_DV
    n_samp = gate3.shape[0] - 1

    def body(x_ref, t_ref, yr_ref, ya_ref, mr0, mr1, ma0, ma1, gt_ref, wor_ref, woa_ref, wout_ref,
             gx_ref, dyr_ref, dya_ref, dpx_hbm, loss_ref, dgt_ref, gwor_hbm, gwoa_hbm, gwout_hbm,
             aor, aoa, aout, dmg_ref, dmg_sem):
        i = pl.program_id(0)

        def dmg_copy(step):
            rows = pl.ds(pl.multiple_of(step * tm, tm), tm)
            return pltpu.make_async_copy(dmg_ref, dpx_hbm.at[rows, pl.ds(C_MR, 2 * d)], dmg_sem)

        @pl.when(i == 0)
        def _():
            aor[...] = jnp.zeros_like(aor)
            aoa[...] = jnp.zeros_like(aoa)
            aout[...] = jnp.zeros_like(aout)
            loss_ref[...] = jnp.zeros_like(loss_ref)

        @pl.when(i % per == 0)
        def _():
            dgt_ref[...] = jnp.zeros_like(dgt_ref)

        yr = yr_ref[...]
        ya = ya_ref[...]
        a = jnp.dot(yr, wor_ref[...], preferred_element_type=F32)
        b = jnp.dot(ya, woa_ref[...], preferred_element_type=F32)
        sr = _sigmoid(jnp.concatenate([mr0[...], mr1[...]], axis=1).astype(F32))
        sa = _sigmoid(jnp.concatenate([ma0[...], ma1[...]], axis=1).astype(F32))
        yb = (sr * a + sa * b).astype(BF)
        out = jnp.dot(yb, wout_ref[...], preferred_element_type=F32)
        gate = gt_ref[...]
        err = x_ref[...] + gate * out - t_ref[...]
        loss_ref[...] += 0.5 * _sum_all(err * err) * (1.0 / d)
        dy_tok = err * (1.0 / d)
        gx_ref[...] = dy_tok
        dgt_ref[...] += jnp.sum(dy_tok * out, axis=0, keepdims=True)
        dout = (dy_tok * gate).astype(BF)
        aout[...] += _dot(yb, dout, 0, 0)
        dyy = _dot(dout, wout_ref[...], 1, 1)
        da = (dyy * sr).astype(BF)
        db = (dyy * sa).astype(BF)
        @pl.when(i > 0)
        def _():
            dmg_copy(i - 1).wait()

        dmg_ref[:, :d] = (dyy * a * (sr * (1.0 - sr))).astype(BF)
        dmg_ref[:, d:] = (dyy * b * (sa * (1.0 - sa))).astype(BF)
        dmg_copy(i).start()
        aor[...] += _dot(yr, da, 0, 0)
        aoa[...] += _dot(ya, db, 0, 0)
        dyr_ref[...] = _dot(da, wor_ref[...], 1, 1).astype(BF)
        dya_ref[...] = _dot(db, woa_ref[...], 1, 1).astype(BF)

        @pl.when(i == n_t - 1)
        def _():
            dmg_copy(i).wait()
            pltpu.sync_copy(aor, gwor_hbm)
            pltpu.sync_copy(aoa, gwoa_hbm)
            pltpu.sync_copy(aout, gwout_hbm)

    half = d // 2
    return pl.pallas_call(
        body, name="merge", grid=(n_t,),
        in_specs=[pl.BlockSpec((tm, d), lambda i: (i, 0)),
                  pl.BlockSpec((tm, d), lambda i: (i, 0)),
                  pl.BlockSpec((tm, rv), lambda i: (i, 0)),
                  pl.BlockSpec((tm, d), lambda i: (i, 0)),
                  pl.BlockSpec((tm, half), lambda i: (i, C_MR // half)),
                  pl.BlockSpec((tm, half), lambda i: (i, C_MR // half + 1)),
                  pl.BlockSpec((tm, half), lambda i: (i, C_MA // half)),
                  pl.BlockSpec((tm, half), lambda i: (i, C_MA // half + 1)),
                  pl.BlockSpec((None, 1, d), lambda i: (i // per, 0, 0)),
                  pl.BlockSpec((rv, d), lambda i: (0, 0)),
                  pl.BlockSpec((d, d), lambda i: (0, 0)),
                  pl.BlockSpec((d, d), lambda i: (0, 0))],
        out_specs=(pl.BlockSpec((tm, d), lambda i: (i, 0)),
                   pl.BlockSpec((tm, rv), lambda i: (i, 0)),
                   pl.BlockSpec((tm, d), lambda i: (i, 0)),
                   ANY,
                   pl.BlockSpec((8, 128), lambda i: (0, 0)),
                   pl.BlockSpec((None, 1, d), lambda i: (i // per, 0, 0)),
                   ANY, ANY, ANY),
        out_shape=(SDS((t_lat, d), F32), SDS((t_lat, rv), BF), SDS((t_lat, d), BF),
                   SDS((px.shape[0], IN_COLS), BF),
                   SDS((8, 128), F32), SDS((n_samp, 1, d), F32),
                   SDS((rv, d), F32), SDS((d, d), F32), SDS((d, d), F32)),
        scratch_shapes=[pltpu.VMEM((rv, d), F32), pltpu.VMEM((d, d), F32), pltpu.VMEM((d, d), F32),
                        pltpu.VMEM((tm, 2 * d), BF), pltpu.SemaphoreType.DMA],
        compiler_params=_cp(("arbitrary",), 56))(
            x_lat, target, y_ret, y_att, px, px, px, px, gate3, w_o_ret, w_o_att, w_out)


def _place():
    x, y, c = lax.axis_index("x"), lax.axis_index("y"), lax.axis_index("c")
    chips = [(1 - x, y), (x, 1 - y), (1 - x, 1 - y)]
    return x, y, c, chips


def _remote(src, dst, send_sem, recv_sem, to):
    return pltpu.make_async_remote_copy(src_ref=src, dst_ref=dst, send_sem=send_sem, recv_sem=recv_sem,
                                        device_id=to, device_id_type=MESH)


def _place_ids():
    x, y, c = lax.axis_index("x"), lax.axis_index("y"), lax.axis_index("c")
    me = 2 * x + y
    return jnp.stack([x, y, c, me, me, 2 * (1 - x) + y, 2 * x + 1 - y, 2 * (1 - x) + 1 - y]).astype(jnp.int32)


def _ag_comm(bufs, rels, arg_index=None):
    n, m = len(bufs), len(rels)

    def half(ref, s, which):
        h = ref.shape[1] // 2
        return ref.at[s, pl.ds(which * h, h), :]

    def ici(ins, outs, ssem, rsem, base):
        x, y, c, chips = _place()
        sends, recvs = [], []
        for a in range(n):
            for jj, j in enumerate(rels):
                k, chip = base + a * m + jj, chips[j]
                mine, theirs = half(outs[a], 2 * x + y, c), half(outs[a], 2 * chip[0] + chip[1], c)
                sends.append(_remote(mine, mine, ssem.at[k], rsem.at[k], (*chip, c)))
                recvs.append(_remote(theirs, theirs, ssem.at[k], rsem.at[k], (*chip, c)))
        return sends, recvs

    def d2d(ins, outs, ssem, rsem, base):
        x, y, c, chips = _place()
        sends, recvs = [], []
        for a in range(n):
            for jj, j in enumerate(rels):
                k, s = base + (n + a) * m + jj, 2 * chips[j][0] + chips[j][1]
                sends.append(_remote(half(outs[a], s, c), half(outs[a], s, c), ssem.at[k], rsem.at[k], (x, y, 1 - c)))
                recvs.append(_remote(half(outs[a], s, 1 - c), half(outs[a], s, 1 - c), ssem.at[k], rsem.at[k],
                                     (x, y, 1 - c)))
        return sends, recvs

    shapes = tuple(SDS(b.shape, b.dtype) for b in bufs)
    if arg_index is not None:
        return _Comm("all_gather", (), shapes, {}, 2 * n * m, (ici, d2d), ((arg_index, 0),))
    return _Comm("all_gather", tuple(bufs), shapes, {a: a for a in range(n)}, 2 * n * m, (ici, d2d))


def _swap_comm(grads):
    n = len(grads)

    def phase(ins, outs, ssem, rsem, base):
        x, y, c, _ = _place()
        sends = []
        for a in range(n):
            h = ins[a].shape[1] // 2
            sends.append(_remote(ins[a].at[:, pl.ds((1 - c) * h, h), :], outs[a], ssem.at[base + a],
                                 rsem.at[base + a], (x, y, 1 - c)))
        return sends, sends

    return _Comm("swap_halves", tuple(grads),
                 tuple(SDS((g.shape[0], g.shape[1] // 2, g.shape[2]), g.dtype) for g in grads), {}, n, (phase,))


def _exchange_comm(parts):
    n = len(parts)

    def phase(ins, outs, ssem, rsem, base):
        x, y, c, chips = _place()
        sends = []
        for a in range(n):
            for j, chip in enumerate(chips):
                k = base + 3 * a + j
                sends.append(_remote(ins[a].at[2 * chip[0] + chip[1]], outs[a].at[j], ssem.at[k], rsem.at[k],
                                     (*chip, c)))
        return sends, sends

    return _Comm("exchange_shards", tuple(parts), tuple(SDS((3,) + p.shape[1:], p.dtype) for p in parts), {}, 3 * n,
                 (phase,))


def _join_comm(bufs, n_parts=1):
    n = len(bufs)

    def phase(ins, outs, ssem, rsem, base):
        x, y, c, _ = _place()
        sends, recvs = [], []
        for a in range(n):
            h = outs[a].shape[0] // (2 * n_parts)
            for p in range(n_parts):
                k = base + a * n_parts + p
                mine = outs[a].at[pl.ds((2 * p + c) * h, h), :]
                other = outs[a].at[pl.ds((2 * p + 1 - c) * h, h), :]
                sends.append(_remote(mine, mine, ssem.at[k], rsem.at[k], (x, y, 1 - c)))
                recvs.append(_remote(other, other, ssem.at[k], rsem.at[k], (x, y, 1 - c)))
        return sends, recvs

    return _Comm("join_halves", tuple(bufs), tuple(SDS(b.shape, b.dtype) for b in bufs), {a: a for a in range(n)},
                 n * n_parts, (phase,))


def _cast_place(w, ids):
    rows, cols = w.shape
    tr = min(rows, 256)

    def body(ids_ref, w_ref, o_ref):
        o_ref[...] = w_ref[...].astype(BF)

    return pl.pallas_call(
        body, name="cast_place",
        grid_spec=pltpu.PrefetchScalarGridSpec(
            num_scalar_prefetch=1, grid=(rows // tr,),
            in_specs=[pl.BlockSpec((tr, cols), lambda i, ids_ref: (i, 0))],
            out_specs=pl.BlockSpec((None, tr, cols), lambda i, ids_ref: (ids_ref[3], i, 0))),
        out_shape=SDS((N_SHARD, rows, cols), BF),
        compiler_params=_cp(("parallel",), 40))(ids, w)


def _all_gather_weights(bufs):
    n = len(bufs)

    def body(*refs):
        outs = refs[n:2 * n]
        send_sems, recv_sems = refs[2 * n:]
        x, y, c, chips = _place()
        sibling = (x, y, 1 - c)
        me = 2 * x + y

        def half(ref, s, which):
            h = ref.shape[1] // 2
            return ref.at[s, pl.ds(which * h, h), :]

        first = []
        for a in range(n):
            for j, chip in enumerate(chips):
                k = a * 3 + j
                win = half(outs[a], me, c)
                first.append(_remote(win, win, send_sems.at[k], recv_sems.at[k], (*chip, c)))
        for cp in first:
            cp.start()
        passed = []
        for a in range(n):
            for j, chip in enumerate(chips):
                k = a * 3 + j
                win = half(outs[a], 2 * chip[0] + chip[1], c)
                _remote(win, win, send_sems.at[k], recv_sems.at[k], (*chip, c)).wait_recv()
                fw = _remote(win, win, send_sems.at[3 * n + k], recv_sems.at[3 * n + k], sibling)
                fw.start()
                passed.append(fw)
        for a in range(n):
            for j, chip in enumerate(chips):
                k = a * 3 + j
                win = half(outs[a], 2 * chip[0] + chip[1], 1 - c)
                _remote(win, win, send_sems.at[3 * n + k], recv_sems.at[3 * n + k], sibling).wait_recv()
        for cp in first + passed:
            cp.wait_send()

    return pl.pallas_call(
        body, name="all_gather_weights",
        in_specs=[ANY] * n, out_specs=tuple([ANY] * n),
        out_shape=tuple(SDS(b.shape, b.dtype) for b in bufs),
        input_output_aliases={a: a for a in range(n)},
        scratch_shapes=[pltpu.SemaphoreType.DMA((6 * n,)), pltpu.SemaphoreType.DMA((6 * n,))],
        compiler_params=_cp(has_side_effects=True))(*bufs)


def _swap_halves(grads):
    n = len(grads)

    def body(*refs):
        ins, outs = refs[:n], refs[n:2 * n]
        send_sems, recv_sems = refs[2 * n:]
        x, y, c, _ = _place()
        sibling = (x, y, 1 - c)

        def half(ref, which):
            h = ref.shape[1] // 2
            return ref.at[:, pl.ds(which * h, h), :]

        sends = [_remote(half(ins[a], 1 - c), outs[a], send_sems.at[a], recv_sems.at[a], sibling)
                 for a in range(n)]
        for cp in sends:
            cp.start()
        for cp in sends:
            cp.wait_recv()
        for cp in sends:
            cp.wait_send()

    return pl.pallas_call(
        body, name="swap_halves",
        in_specs=[ANY] * n, out_specs=tuple([ANY] * n),
        out_shape=tuple(SDS((g.shape[0], g.shape[1] // 2, g.shape[2]), g.dtype) for g in grads),
        scratch_shapes=[pltpu.SemaphoreType.DMA((n,)), pltpu.SemaphoreType.DMA((n,))],
        compiler_params=_cp(has_side_effects=True))(*grads)


def _chip_sum(g, p, ids):
    n_s, rows, cols = g.shape
    h = rows // 2
    tr = min(h, 256)
    nb = h // tr

    def body(ids_ref, g_ref, p_ref, o_ref, o16_ref):
        t = g_ref[...] + p_ref[...]
        o_ref[...] = t
        o16_ref[...] = t.astype(BF)

    out_spec = pl.BlockSpec((None, tr, cols), lambda s, i, ids_ref: (s, i, 0))
    return pl.pallas_call(
        body, name="chip_sum",
        grid_spec=pltpu.PrefetchScalarGridSpec(
            num_scalar_prefetch=1, grid=(n_s, nb),
            in_specs=[pl.BlockSpec((None, tr, cols), lambda s, i, ids_ref: (s, ids_ref[2] * nb + i, 0)),
                      pl.BlockSpec((None, tr, cols), lambda s, i, ids_ref: (s, i, 0))],
            out_specs=(out_spec, out_spec)),
        out_shape=(SDS((n_s, h, cols), g.dtype), SDS((n_s, h, cols), BF)),
        compiler_params=_cp(("parallel", "parallel"), 40))(ids, g, p)


def _exchange_shards(parts):
    n = len(parts)

    def body(*refs):
        ins, outs = refs[:n], refs[n:2 * n]
        send_sems, recv_sems = refs[2 * n:]
        x, y, c, chips = _place()
        sends = []
        for a in range(n):
            for j, chip in enumerate(chips):
                k = a * 3 + j
                sends.append(_remote(ins[a].at[2 * chip[0] + chip[1]], outs[a].at[j],
                                     send_sems.at[k], recv_sems.at[k], (*chip, c)))
        for cp in sends:
            cp.start()
        for cp in sends:
            cp.wait_recv()
        for cp in sends:
            cp.wait_send()

    return pl.pallas_call(
        body, name="exchange_shards",
        in_specs=[ANY] * n, out_specs=tuple([ANY] * n),
        out_shape=tuple(SDS((3,) + p.shape[1:], p.dtype) for p in parts),
        scratch_shapes=[pltpu.SemaphoreType.DMA((3 * n,)), pltpu.SemaphoreType.DMA((3 * n,))],
        compiler_params=_cp(has_side_effects=True))(*parts)


def _shard_sum(t, q, ids, part=0, n_parts=1, buf=None):
    _, h, cols = t.shape
    tr = min(h, 256)
    nb = h // tr

    def body(ids_ref, t_ref, q_ref, *rest):
        rest[-1][...] = ((t_ref[...] + q_ref[0].astype(F32)) + q_ref[1].astype(F32)) + q_ref[2].astype(F32)

    args, in_specs, aliases = [t, q], [
        pl.BlockSpec((None, tr, cols), lambda i, ids_ref: (ids_ref[3], i, 0)),
        pl.BlockSpec((3, tr, cols), lambda i, ids_ref: (0, i, 0))], None
    if buf is not None:
        args, in_specs, aliases = args + [buf], in_specs + [ANY], {2: 0}
    return _call(body, args, None, name="shard_sum", grid=(nb,), in_specs=in_specs,
                 out_specs=pl.BlockSpec((tr, cols), lambda i, ids_ref: ((2 * part + ids_ref[2]) * nb + i, 0)),
                 out_shape=SDS((2 * h * n_parts, cols), t.dtype), aliases=aliases, prefetch=ids,
                 compiler_params=_cp(("parallel",), 40))


def _join_halves(bufs):
    n = len(bufs)

    def body(*refs):
        outs = refs[n:2 * n]
        send_sems, recv_sems = refs[2 * n:]
        x, y, c, _ = _place()
        sibling = (x, y, 1 - c)

        def win(ref, which):
            h = ref.shape[0] // 2
            return ref.at[pl.ds(which * h, h), :]

        sends = [_remote(win(outs[a], c), win(outs[a], c), send_sems.at[a], recv_sems.at[a], sibling)
                 for a in range(n)]
        for cp in sends:
            cp.start()
        for a in range(n):
            other = win(outs[a], 1 - c)
            _remote(other, other, send_sems.at[a], recv_sems.at[a], sibling).wait_recv()
        for cp in sends:
            cp.wait_send()

    return pl.pallas_call(
        body, name="join_halves",
        in_specs=[ANY] * n, out_specs=tuple([ANY] * n),
        out_shape=tuple(SDS(b.shape, b.dtype) for b in bufs),
        input_output_aliases={a: a for a in range(n)},
        scratch_shapes=[pltpu.SemaphoreType.DMA((n,)), pltpu.SemaphoreType.DMA((n,))],
        compiler_params=_cp(has_side_effects=True))(*bufs)


def _gather_small(block, n_sum):
    rows, cols = block.shape
    n_dev = 8

    def body(x_ref, o_ref, g_ref, buf, send_sems, recv_sems, local_sem):
        x, y, c, chips = _place()
        me, sibling = (x, y, c), (x, y, 1 - c)

        def slot(px_, py_, pc_):
            return buf.at[4 * px_ + 2 * py_ + pc_]

        def copy(k, who, to, src=None):
            return _remote(slot(*who) if src is None else src, slot(*who), send_sems.at[k], recv_sems.at[k], to)

        mine = pltpu.make_async_copy(x_ref, slot(*me), local_sem)
        mine.start()
        first = [copy(0, me, sibling, src=x_ref)]
        first += [copy(1 + j, me, (*chip, c), src=x_ref) for j, chip in enumerate(chips)]
        for cp in first:
            cp.start()
        passed = [copy(4 + j, (*chip, c), sibling) for j, chip in enumerate(chips)]
        for j, chip in enumerate(chips):
            copy(1 + j, (*chip, c), me).wait_recv()
            passed[j].start()
        copy(0, sibling, me).wait_recv()
        for j, chip in enumerate(chips):
            copy(4 + j, (*chip, 1 - c), me).wait_recv()
        for cp in first + passed:
            cp.wait_send()
        mine.wait()
        acc = buf[0, :, :n_sum]
        for s in range(1, n_dev):
            acc = acc + buf[s, :, :n_sum]
        o_ref[...] = acc
        for s in range(n_dev):
            g_ref[s * rows:(s + 1) * rows, :] = buf[s, :, n_sum:]

    return pl.pallas_call(
        body, name="gather_small",
        in_specs=[pl.BlockSpec(memory_space=pltpu.VMEM)],
        out_specs=(pl.BlockSpec(memory_space=pltpu.VMEM), pl.BlockSpec(memory_space=pltpu.VMEM)),
        out_shape=(SDS((rows, n_sum), F32), SDS((n_dev * rows, cols - n_sum), F32)),
        scratch_shapes=[pltpu.VMEM((n_dev, rows, cols), F32), pltpu.SemaphoreType.DMA((7,)),
                        pltpu.SemaphoreType.DMA((7,)), pltpu.SemaphoreType.DMA],
        compiler_params=_cp(has_side_effects=True))(block)


def _adam_math(w, g, m, v):
    m = ADAM_B1 * m + (1.0 - ADAM_B1) * g
    v = ADAM_B2 * v + (1.0 - ADAM_B2) * (g * g)
    m_hat = m / (1.0 - ADAM_B1 ** ADAM_STEP)
    v_hat = v / (1.0 - ADAM_B2 ** ADAM_STEP)
    delta = -ADAM_LR * (m_hat / (jnp.sqrt(v_hat) + ADAM_EPS) + ADAM_WD * w)
    return delta, m, v


def _adamw(w, g, m, v):
    rows, cols = w.shape
    tr = min(rows, 256 if cols <= 2048 else 128)

    def body(w_ref, g_ref, m_ref, v_ref, go_ref, d_ref, nm_ref, nv_ref):
        g = g_ref[...]
        go_ref[...] = g
        d_ref[...], nm_ref[...], nv_ref[...] = _adam_math(w_ref[...], g, m_ref[...], v_ref[...])

    spec = pl.BlockSpec((tr, cols), lambda i: (i, 0))
    return pl.pallas_call(
        body, name="adamw", grid=(rows // tr,), in_specs=[spec] * 4, out_specs=(spec,) * 4,
        out_shape=(SDS(w.shape, F32),) * 4, compiler_params=_cp(("parallel",), 40))(w, g, m, v)


def _adamw_small(w, g, m, v):
    def body(w_ref, g_ref, m_ref, v_ref, go_ref, d_ref, nm_ref, nv_ref):
        w = w_ref[...]
        g = g_ref[...]
        sub = lax.broadcasted_iota(jnp.int32, w.shape, 0)
        lane = lax.broadcasted_iota(jnp.int32, w.shape, 1)
        is_ret = jnp.logical_and(sub == 5, lane < 2 * RET_HEADS)
        u = jnp.exp(jnp.where(is_ret, w, -1.0) * jnp.log(2.0))
        g = jnp.where(is_ret, g * (-u * jnp.log(2.0) / (1.0 - u)), g)
        go_ref[...] = g
        d_ref[...], nm_ref[...], nv_ref[...] = _adam_math(w, g, m_ref[...], v_ref[...])

    return pl.pallas_call(body, name="adamw_small", out_shape=(SDS(w.shape, F32),) * 4)(w, g, m, v)


def _rope_tables(seq, n_samp, n_ctx_rows):
    rows = seq // GRID_W
    row = jnp.repeat(jnp.arange(rows, dtype=F32), GRID_W)
    col = jnp.tile(jnp.arange(GRID_W, dtype=F32), rows)
    half = ATT_HEAD_DIM // 2
    freqs = ROPE_THETA ** (-jnp.arange(0, half, 2, dtype=F32) / half)
    ang = jnp.concatenate([row[:, None] * freqs, col[:, None] * freqs], axis=-1)
    cos, sin = jnp.cos(ang), jnp.sin(ang)
    cos_f = jnp.repeat(cos, 2, axis=1)
    sin_s = jnp.stack([-sin, sin], axis=-1).reshape(seq, ATT_HEAD_DIM)
    cos_all = jnp.concatenate([jnp.tile(cos_f, (n_samp, 1)), jnp.ones((n_ctx_rows, ATT_HEAD_DIM), F32)], axis=0)
    sin_all = jnp.concatenate([jnp.tile(sin_s, (n_samp, 1)), jnp.zeros((n_ctx_rows, ATT_HEAD_DIM), F32)], axis=0)
    return cos_all, sin_all


def _pack_small(c_ctx, norm_w, b_ada, ret, qn, kn):
    d = D_MODEL
    row5 = jnp.concatenate([ret.reshape(-1), jnp.zeros((128 - 2 * RET_HEADS,), F32), qn.reshape(-1), kn.reshape(-1),
                            jnp.zeros((d - 384,), F32)])
    return jnp.concatenate([c_ctx.reshape(1, d), norm_w.reshape(1, d), b_ada.reshape(3, d), row5.reshape(1, d),
                            jnp.zeros((2, d), F32)], axis=0)


def _unpack_small(p):
    d = D_MODEL
    return (p[0], p[1:2], p[2:5].reshape(1, 3 * d), p[5, :2 * RET_HEADS].reshape(1, 2, RET_HEADS),
            p[5:6, 128:256], p[5:6, 256:384])


def _step(x, c, ctx, c_ctx, norm_w, b_ada, ret_log2_decay, q_norm_w, k_norm_w, loss_target, weights, ids, dist):
    n_samp, seq, d = x.shape
    lc = ctx.shape[1]
    t_lat, t_ctx = n_samp * seq, n_samp * lc
    assert seq % TM == 0 and t_ctx == TM and t_lat % lc == 0 and seq % GRID_W == 0
    tps = seq // TM

    x_lat = x.reshape(t_lat, d)
    x_ctx = ctx.reshape(t_ctx, d)
    cvec8 = jnp.concatenate([c, c_ctx.reshape(1, d), jnp.zeros((8 - n_samp - 1, d), F32)], axis=0)
    lg = jnp.log1p(-jnp.exp2(ret_log2_decay.reshape(2, RET_HEADS)))
    cos_all, sin_all = _rope_tables(seq, n_samp, t_ctx)

    w_ada_b, w_in_b, w_or_b, w_oa_b, w_out_b = weights
    w_ada_g = _run_comm(_ag_comm((w_ada_b,), (0, 1, 2)))[0] if dist else w_ada_b
    mod8 = _adaln_fwd(cvec8, w_ada_g, b_ada)
    mod3 = mod8[:n_samp + 1]
    shift3 = mod3[:, None, 0:d]
    scale3 = mod3[:, None, d:2 * d]
    gate3 = mod3[:, None, 2 * d:3 * d]

    hx, hxt = _norm_fwd(x_lat, x_ctx, norm_w, scale3, shift3, tps, n_samp)
    if dist:
        px, (w_in_1,) = _in_proj(hx, w_in_b, ids, 0, 1, comm=_ag_comm((w_in_b,), (0, 1), arg_index=1))
        px, (w_in_g,) = _in_proj(hx, w_in_1, ids, 1, 2, px=px, comm=_ag_comm((w_in_1,), (2,), arg_index=1))
        px = _in_proj(hx, w_in_g, ids, 3, 1, px=px)
    else:
        w_in_g = w_in_b
        px = _in_proj(hx, w_in_g, ids, 0, N_SHARD)

    states0 = _ctx_state_fwd(px, lg, n_samp, t_lat, lc)
    if dist:
        (o_f, o_b, saved), w_o = _ret_fwd(px, states0, lg, n_samp, seq,
                                          comm=_ag_comm((w_or_b, w_oa_b, w_out_b), (0, 1, 2)))
    else:
        (o_f, o_b, saved), w_o = _ret_fwd(px, states0, lg, n_samp, seq), (w_or_b, w_oa_b, w_out_b)
    w_o_ret, w_o_att, w_out = (w.reshape(-1, d) for w in w_o)
    y_ret = _retnorm_fwd(o_f, o_b, px)

    qn = _att_prep_q(px, cos_all, sin_all, q_norm_w, t_lat)
    kn, vn = _att_prep_kv(px, cos_all, sin_all, k_norm_w)
    y_att, o_att, lse = _att_fwd(qn, kn, vn, px, n_samp, seq, lc)

    (gx_res, dy_ret, dy_att, dpx, loss8, dgate, g_w_o_ret, g_w_o_att, g_w_out) = _merge(
        x_lat, loss_target.reshape(t_lat, d), y_ret, y_att, px, gate3, w_o_ret, w_o_att, w_out, tps)

    g_a = [g.reshape(N_SHARD, -1, d) for g in (g_w_o_ret, g_w_o_att, g_w_out)]
    dpx = _att_gate_bwd(dpx, dy_att, o_att, px)
    res = _att_bwd(dpx, qn, kn, vn, px, o_att, lse, dy_att, cos_all, sin_all, q_norm_w, n_samp, seq, lc,
                   comm=_swap_comm(g_a) if dist else None)
    (dpx, dkl, dkc, dvl, dvc, gqw), sib_a = res if dist else (res, None)
    dpx, gkw = _att_kv_bwd(dpx, dkl, dkc, dvl, dvc, px, cos_all, sin_all, k_norm_w)
    if dist:
        t_a = [_chip_sum(g, p, ids) for g, p in zip(g_a, sib_a)]

    do, dpx = _retnorm_bwd(dpx, dy_ret, o_f, o_b, px)
    res = _ret_bwd(px, do, saved, lg, n_samp, seq,
                   comm=_exchange_comm([t16 for _, t16 in t_a]) if dist else None)
    (dqf, dkf, dvf, dqb, dkb, dvb, dstates, dlg_lat), q_a = res if dist else (res, None)
    if dist:
        r_a = [_shard_sum(t, q, ids) for (t, _), q in zip(t_a, q_a)]
    dpx = _combine_into(dpx, dqf, dqb, C_RQ, 1.0)
    dpx = _combine_into(dpx, dkf, dkb, C_RK, RET_DK ** -0.5)
    dpx = _combine_into(dpx, dvf, dvb, C_RV, 1.0)
    dpx, dlg_ctx = _ctx_state_bwd(dpx, px, dstates, lg, n_samp, t_lat, lc)
    dpx = _zero_ctx_tail(dpx, t_lat)

    n_tiles = dpx.shape[0] // _big_rows(dpx.shape[0])
    if dist:
        g_b = _gw_in(hxt, dpx, 0, 1)
        dhx, (sib_b, *r_a) = _dhx(dpx, w_in_g, 0, 1, comm=_join_comms(_swap_comm([g_b]), _join_comm(r_a)))
        t_b, t16_b = _chip_sum(g_b, sib_b, ids)
        dhx, (q_b,) = _dhx(dpx, w_in_g, 1, n_tiles - 1, dhx=dhx, comm=_exchange_comm([t16_b]))
        (grad_x, dshift, dscale, g_norm_w), (r_b,) = _norm_bwd(
            x_lat, x_ctx, dhx, gx_res, norm_w, scale3, tps, n_samp,
            comm=_join_comm([_shard_sum(t_b, q_b, ids)]))
    else:
        g_w_in = _gw_in(hxt, dpx, 0, 1)
        dhx = _dhx(dpx, w_in_g, 0, n_tiles)
        grad_x, dshift, dscale, g_norm_w = _norm_bwd(x_lat, x_ctx, dhx, gx_res, norm_w, scale3, tps, n_samp)

    dgate_all = jnp.concatenate([dgate, jnp.zeros((1, 1, d), F32)], axis=0)
    dmod3 = jnp.concatenate([dshift, dscale, dgate_all], axis=2).reshape(n_samp + 1, 3 * d)
    dmod8 = jnp.concatenate([dmod3, jnp.zeros((8 - n_samp - 1, 3 * d), F32)], axis=0)
    g_lg = (jnp.sum(dlg_lat[:, :, 0], axis=0).reshape(2, RET_HEADS)
            + jnp.stack([jnp.sum(dlg_ctx[:, :, 0, 0], axis=0), jnp.sum(dlg_ctx[:, :, 1, 0], axis=0)], axis=0))
    g_qw = jnp.sum(gqw, axis=(0, 1, 2))
    zero = jnp.zeros((d,), F32)
    if not dist:
        g_w_ada, g_b_ada, dc8 = _adaln_bwd(cvec8, dmod8, w_ada_g)
        small = _pack_small(dc8[n_samp], g_norm_w, g_b_ada, g_lg, g_qw, gkw)
        return (loss8[0, 0], grad_x.reshape(n_samp, seq, d),
                (g_w_ada, g_w_in, g_w_o_ret, g_w_o_att, g_w_out), small)

    local = _pack_small(zero, g_norm_w, jnp.zeros((3 * d,), F32), g_lg, g_qw, gkw).at[6, 0].set(loss8[0, 0])
    small_sum, gathered = _gather_small(jnp.concatenate([local, cvec8, dmod8], axis=1), d)
    g_w_ada, g_b_ada, dc_all = _adaln_bwd(gathered[:, :d], gathered[:, d:], w_ada_g)
    dc_ctx = jnp.sum(dc_all.reshape(-1, 8, d)[:, n_samp], axis=0)
    small = small_sum + _pack_small(dc_ctx, zero, g_b_ada, jnp.zeros((2, RET_HEADS), F32), zero[:128], zero[:128])
    r_c = lax.dynamic_index_in_dim(g_w_ada, ids[3], 0, keepdims=False)
    return small[6, 0], grad_x.reshape(n_samp, seq, d), (r_c, r_b, *r_a), small


def kernel(x, c, ctx, c_ctx, norm_w, w_ada, b_ada, w_in, ret_log2_decay, q_norm_w, k_norm_w, w_o_ret, w_o_att, w_out, loss_target, m_c_ctx, m_norm_w, m_w_ada, m_b_ada, m_w_in, m_ret_log2_decay, m_q_norm_w, m_k_norm_w, m_w_o_ret, m_w_o_att, m_w_out, v_c_ctx, v_norm_w, v_w_ada, v_b_ada, v_w_in, v_ret_log2_decay, v_q_norm_w, v_k_norm_w, v_w_o_ret, v_w_o_att, v_w_out):
    big_w = (w_ada[0], w_in[0], w_o_ret[0], w_o_att[0], w_out[0])
    big_m = (m_w_ada[0], m_w_in[0], m_w_o_ret[0], m_w_o_att[0], m_w_out[0])
    big_v = (v_w_ada[0], v_w_in[0], v_w_o_ret[0], v_w_o_att[0], v_w_out[0])

    ids = _place_ids()
    loss, grad_x, big_grad, small_grad_in = _step(
        x, c, ctx, c_ctx, norm_w[0:1], b_ada[0:1], ret_log2_decay[0], q_norm_w[0:1], k_norm_w[0:1], loss_target,
        tuple(_cast_place(w, ids) for w in big_w), ids, True)
    small_w = _pack_small(c_ctx, norm_w, b_ada, ret_log2_decay, q_norm_w, k_norm_w)
    small_m = _pack_small(m_c_ctx, m_norm_w, m_b_ada, m_ret_log2_decay, m_q_norm_w, m_k_norm_w)
    small_v = _pack_small(v_c_ctx, v_norm_w, v_b_ada, v_ret_log2_decay, v_q_norm_w, v_k_norm_w)
    small_grad, small_delta, small_nm, small_nv = _adamw_small(small_w, small_grad_in, small_m, small_v)

    big_g, big_delta, big_nm, big_nv = [], [], [], []
    for w, g, m, v in zip(big_w, big_grad, big_m, big_v):
        go, dlt, nm, nv = _adamw(w, g, m, v)
        big_g.append(go[None])
        big_delta.append(dlt[None])
        big_nm.append(nm[None])
        big_nv.append(nv[None])
    big_grad = big_g

    def order(small_packed, big):
        s = _unpack_small(small_packed)
        return (s[0], s[1], big[0], s[2], big[1], s[3], s[4], s[5], big[2], big[3], big[4])

    return (loss, grad_x, *order(small_grad, big_grad), *order(small_delta, big_delta),
            *order(small_nm, big_nm), *order(small_nv, big_nv))
```

```python
import functools
from typing import NamedTuple

import jax
import jax.numpy as jnp
from jax import lax
from jax.experimental import pallas as pl
from jax.experimental.pallas import tpu as pltpu

F32 = jnp.float32
BF = jnp.bfloat16
SDS = jax.ShapeDtypeStruct
MESH = pl.DeviceIdType.MESH
ANY = pl.BlockSpec(memory_space=pl.ANY)
SMEM = pl.BlockSpec(memory_space=pltpu.SMEM)

D_MODEL = 1024
GRID_W = 64
RET_HEADS = 4
RET_DK = 256
RET_DV = 512
RET_CHUNK = 128
ATT_HEADS = 8
ATT_KV_HEADS = 2
ATT_REP = ATT_HEADS // ATT_KV_HEADS
ATT_HEAD_DIM = 128
ROPE_THETA = 10000.0
NORM_EPS = 1e-6
IN_COLS = 10752
KV_COLS = 3584
C_RK, C_RV, C_AK, C_AV, C_RQ, C_RG, C_AQ, C_AG, C_MR, C_MA = 0, 1024, 3072, 3328, 3584, 4608, 6656, 7680, 8704, 9728
N_SHARD = 4
ADA_W = 3 * D_MODEL // N_SHARD
IN_W = IN_COLS // N_SHARD
IN_BLK = IN_W // 3
N_IN_BLK = IN_COLS // IN_BLK
TM = 512
ATT_TQ = 512
ADAM_LR, ADAM_B1, ADAM_B2, ADAM_EPS, ADAM_WD, ADAM_STEP = 0.001, 0.9, 0.999, 1e-08, 0.01, 10
MIB = 1024 * 1024


def _cp(sem=None, vmem_mb=None, **kw):
    if sem is not None:
        kw["dimension_semantics"] = sem
    if vmem_mb is not None:
        kw["vmem_limit_bytes"] = vmem_mb * MIB
    return pltpu.CompilerParams(**kw)


def _dot(a, b, ca=1, cb=0):
    return lax.dot_general(a.astype(BF), b.astype(BF), (((ca,), (cb,)), ((), ())), preferred_element_type=F32)


def _sigmoid(x):
    return 1.0 / (1.0 + jnp.exp(-x))


def _sum_all(x):
    return jnp.sum(jnp.sum(x, axis=1, keepdims=True), axis=0, keepdims=True)


def _swap_pairs(x):
    ax = x.ndim - 1
    lane = lax.broadcasted_iota(jnp.int32, x.shape, ax)
    nxt = pltpu.roll(x, x.shape[ax] - 1, ax)
    prv = pltpu.roll(x, 1, ax)
    return jnp.where(lane % 2 == 0, nxt, prv)


def _rms(x):
    return lax.rsqrt(jnp.mean(x * x, axis=-1, keepdims=True) + NORM_EPS)


def _rms_bwd(dxh, xh, r):
    return r * (dxh - xh * jnp.mean(dxh * xh, axis=-1, keepdims=True))


class _Comm(NamedTuple):
    name: str
    ins: tuple
    out_shapes: tuple
    aliases: dict
    n_sems: int
    phases: tuple
    arg_aliases: tuple = ()


def _join_comms(*comms):
    comms = [cm for cm in comms if cm is not None]
    if len(comms) <= 1:
        return comms[0] if comms else None
    offs, i_off, o_off, s_off = [], 0, 0, 0
    for cm in comms:
        offs.append((i_off, o_off, s_off))
        i_off, o_off, s_off = i_off + len(cm.ins), o_off + len(cm.out_shapes), s_off + cm.n_sems

    def phase(k):
        def run(ins, outs, ssem, rsem, base):
            sends, recvs = [], []
            for cm, (io, oo, so) in zip(comms, offs):
                if k < len(cm.phases):
                    s, r = cm.phases[k](ins[io:io + len(cm.ins)], outs[oo:oo + len(cm.out_shapes)], ssem, rsem,
                                        base + so)
                    sends += s
                    recvs += r
            return sends, recvs
        return run

    aliases, arg_aliases = {}, ()
    for cm, (io, oo, _) in zip(comms, offs):
        aliases.update({io + a: oo + b for a, b in cm.aliases.items()})
        arg_aliases += tuple((a, oo + b) for a, b in cm.arg_aliases)
    return _Comm("+".join(cm.name for cm in comms), sum((cm.ins for cm in comms), ()),
                 sum((cm.out_shapes for cm in comms), ()), aliases, s_off,
                 tuple(phase(k) for k in range(max(len(cm.phases) for cm in comms))), arg_aliases)


def _run_phases(comm, cins, couts, ssem, rsem, first_started):
    for k, phase in enumerate(comm.phases):
        sends, recvs = phase(cins, couts, ssem, rsem, 0)
        if k > 0 or not first_started:
            for cp in sends:
                cp.start()
        for cp in recvs:
            cp.wait_recv()
        for cp in sends:
            cp.wait_send()


def _call(body, args, comm=None, *, name, grid, in_specs, out_specs, out_shape, scratch_shapes=(),
          compiler_params, aliases=None, prefetch=None):
    single = not isinstance(out_shape, (tuple, list))
    out_specs_t = (out_specs,) if single else tuple(out_specs)
    out_shape_t = (out_shape,) if single else tuple(out_shape)
    n_pre = 0 if prefetch is None else 1
    n_in, n_out, n_sc = len(in_specs), len(out_specs_t), len(scratch_shapes)
    io_alias = {n_pre + a: b for a, b in (aliases or {}).items()}
    if comm is None:
        kernel_body, cin, cout, csems = body, [], [], []
    else:
        n_ci, n_co = len(comm.ins), len(comm.out_shapes)
        cin, cout = [ANY] * n_ci, [ANY] * n_co
        csems = [pltpu.SemaphoreType.DMA((comm.n_sems,)), pltpu.SemaphoreType.DMA((comm.n_sems,))]
        io_alias.update({n_pre + n_in + a: n_out + b for a, b in comm.aliases.items()})
        io_alias.update({n_pre + a: n_out + b for a, b in comm.arg_aliases})

        def kernel_body(*refs):
            pre, refs = refs[:n_pre], refs[n_pre:]
            ins, cins = refs[:n_in], refs[n_in:n_in + n_ci]
            outs = refs[n_in + n_ci:n_in + n_ci + n_out]
            couts = refs[n_in + n_ci + n_out:n_in + n_ci + n_out + n_co]
            scratch = refs[n_in + n_ci + n_out + n_co:n_in + n_ci + n_out + n_co + n_sc]
            ssem, rsem = refs[-2:]
            first = functools.reduce(jnp.logical_and, [pl.program_id(k) == 0 for k in range(len(grid))])
            last = functools.reduce(jnp.logical_and, [pl.program_id(k) == grid[k] - 1 for k in range(len(grid))])

            @pl.when(first)
            def _():
                for cp in comm.phases[0](cins, couts, ssem, rsem, 0)[0]:
                    cp.start()

            body(*pre, *ins, *outs, *scratch)

            @pl.when(last)
            def _():
                _run_phases(comm, cins, couts, ssem, rsem, True)

        name = name + "+" + comm.name

    all_in, all_out = list(in_specs) + cin, out_specs_t + tuple(cout)
    shapes = out_shape_t + (tuple(comm.out_shapes) if comm is not None else ())
    scratch = list(scratch_shapes) + csems
    if prefetch is None:
        res = pl.pallas_call(kernel_body, name=name, grid=grid, in_specs=all_in, out_specs=all_out, out_shape=shapes,
                             scratch_shapes=scratch, input_output_aliases=io_alias,
                             compiler_params=compiler_params)(*args, *(comm.ins if comm is not None else ()))
    else:
        res = pl.pallas_call(
            kernel_body, name=name, out_shape=shapes, input_output_aliases=io_alias, compiler_params=compiler_params,
            grid_spec=pltpu.PrefetchScalarGridSpec(num_scalar_prefetch=1, grid=grid, in_specs=all_in,
                                                   out_specs=all_out, scratch_shapes=scratch))(
                                                       prefetch, *args, *(comm.ins if comm is not None else ()))
    own = res[0] if single else tuple(res[:n_out])
    return own if comm is None else (own, tuple(res[n_out:]))


def _run_comm(comm):
    n_ci, n_co = len(comm.ins), len(comm.out_shapes)

    def body(*refs):
        _run_phases(comm, refs[:n_ci], refs[n_ci:n_ci + n_co], refs[-2], refs[-1], False)

    return pl.pallas_call(
        body, name=comm.name, in_specs=[ANY] * n_ci, out_specs=tuple([ANY] * n_co), out_shape=tuple(comm.out_shapes),
        input_output_aliases=dict(comm.aliases),
        scratch_shapes=[pltpu.SemaphoreType.DMA((comm.n_sems,)), pltpu.SemaphoreType.DMA((comm.n_sems,))],
        compiler_params=_cp(has_side_effects=True))(*comm.ins)


def _adaln_fwd(cvec8, w_ada_g, b_ada):
    def body(c_ref, w_ref, b_ref, o_ref):
        cv = c_ref[...]
        sc = (cv * _sigmoid(cv)).astype(BF)
        for s in range(N_SHARD):
            cols = slice(s * ADA_W, (s + 1) * ADA_W)
            o_ref[:, cols] = jnp.dot(sc, w_ref[s], preferred_element_type=F32) + b_ref[:, cols]

    return pl.pallas_call(body, out_shape=SDS((8, 3 * D_MODEL), F32), name="adaln_fwd",
                          compiler_params=_cp(vmem_mb=32))(cvec8, w_ada_g, b_ada)


def _adaln_bwd(cvec, dmod, w_ada_g):
    def body(c_ref, d_ref, w_ref, gw_ref, gb_ref, dc_ref):
        cv = c_ref[...]
        sg = _sigmoid(cv)
        sc = cv * sg
        dm = d_ref[...]
        gb_ref[...] = jnp.sum(dm, axis=0, keepdims=True)
        dsc = jnp.zeros(cv.shape, F32)
        for s in range(N_SHARD):
            cols = slice(s * ADA_W, (s + 1) * ADA_W)
            gw_ref[s] = _dot(sc, dm[:, cols], 0, 0)
            dsc = dsc + _dot(dm[:, cols], w_ref[s], 1, 1)
        dc_ref[...] = dsc * (sg * (1.0 + cv * (1.0 - sg)))

    return pl.pallas_call(
        body, name="adaln_bwd",
        out_shape=(SDS((N_SHARD, D_MODEL, ADA_W), F32), SDS((1, 3 * D_MODEL), F32), SDS(cvec.shape, F32)),
        compiler_params=_cp(vmem_mb=48))(cvec, dmod, w_ada_g)


def _big_rows(rows):
    return 1536 if rows % 1536 == 0 else TM


def _norm_fwd(x_lat, x_ctx, norm_w, scale3, shift3, tiles_per_sample, n_samp):
    n_lat = x_lat.shape[0] // TM
    rows = x_lat.shape[0] + x_ctx.shape[0]

    def samp(i):
        return jnp.minimum(i // tiles_per_sample, n_samp)

    def body(x_ref, c_ref, nw_ref, sc_ref, sh_ref, hx_ref, hxt_ref):
        x = jnp.where(pl.program_id(0) < n_lat, x_ref[...], c_ref[...])
        h = x * _rms(x) * nw_ref[...] * (1.0 + sc_ref[...]) + sh_ref[...]
        hx_ref[...] = h.astype(BF)
        hxt_ref[...] = h.T.astype(BF)

    return pl.pallas_call(
        body, name="norm_fwd", grid=(rows // TM,),
        in_specs=[pl.BlockSpec((TM, D_MODEL), lambda i: (jnp.minimum(i, n_lat - 1), 0)),
                  pl.BlockSpec((TM, D_MODEL), lambda i: (jnp.maximum(i - n_lat, 0), 0)),
                  pl.BlockSpec((1, D_MODEL), lambda i: (0, 0)),
                  pl.BlockSpec((None, 1, D_MODEL), lambda i: (samp(i), 0, 0)),
                  pl.BlockSpec((None, 1, D_MODEL), lambda i: (samp(i), 0, 0))],
        out_specs=(pl.BlockSpec((TM, D_MODEL), lambda i: (i, 0)),
                   pl.BlockSpec((D_MODEL, TM), lambda i: (0, i))),
        out_shape=(SDS((rows, D_MODEL), BF), SDS((D_MODEL, rows), BF)),
        compiler_params=_cp(("parallel",), 40))(x_lat, x_ctx, norm_w, scale3, shift3)


def _in_proj(hx, w_in_g, ids, first, count, px=None, comm=None):
    rows = hx.shape[0]
    tb = _big_rows(rows)

    def shard(j, ids_ref):
        return ids_ref[4 + first + j // 3]

    def body(ids_ref, h_ref, w_ref, *rest):
        px_ref = rest[-1]
        px_ref[...] = jnp.dot(h_ref[...], w_ref[...], preferred_element_type=F32).astype(BF)

    args, in_specs, aliases = [hx, w_in_g], [
        pl.BlockSpec((tb, D_MODEL), lambda j, i, ids_ref: (i, 0)),
        pl.BlockSpec((None, D_MODEL, IN_BLK), lambda j, i, ids_ref: (shard(j, ids_ref), 0, j % 3))], None
    if px is not None:
        args, in_specs, aliases = args + [px], in_specs + [ANY], {2: 0}
    return _call(body, args, comm, name="in_proj", grid=(3 * count, rows // tb), in_specs=in_specs,
                 out_specs=pl.BlockSpec((tb, IN_BLK), lambda j, i, ids_ref: (i, 3 * shard(j, ids_ref) + j % 3)),
                 out_shape=SDS((rows, IN_COLS), BF), aliases=aliases, prefetch=ids,
                 compiler_params=_cp(("arbitrary", "arbitrary"), 40))


def _norm_bwd(x_lat, x_ctx, dhx, gx_res, norm_w, scale3, tiles_per_sample, n_samp, comm=None):
    rows = x_lat.shape[0] + x_ctx.shape[0]
    n_lat = tiles_per_sample * n_samp

    def samp(i):
        return jnp.minimum(i // tiles_per_sample, n_samp)

    def lat(i):
        return jnp.minimum(i, n_lat - 1)

    def body(x_ref, c_ref, dh_ref, gr_ref, nw_ref, sc_ref, gx_ref, dsh_ref, dsc_ref, dnw_ref):
        i = pl.program_id(0)
        x = jnp.where(i < n_lat, x_ref[...], c_ref[...])
        r = _rms(x)
        xh = x * r
        nw = nw_ref[...]
        dh = dh_ref[...]
        first = jnp.logical_or(i % tiles_per_sample == 0, i >= n_lat)

        @pl.when(first)
        def _():
            dsh_ref[...] = jnp.zeros_like(dsh_ref)
            dsc_ref[...] = jnp.zeros_like(dsc_ref)

        @pl.when(i == 0)
        def _():
            dnw_ref[...] = jnp.zeros_like(dnw_ref)

        dsh_ref[...] += jnp.sum(dh, axis=0, keepdims=True)
        dsc_ref[...] += jnp.sum(dh * (xh * nw), axis=0, keepdims=True)
        du = dh * (1.0 + sc_ref[...])
        dnw_ref[...] += jnp.sum(du * xh, axis=0, keepdims=True)

        @pl.when(i < n_lat)
        def _():
            gx_ref[...] = gr_ref[...] + _rms_bwd(du * nw, xh, r)

    return _call(
        body, [x_lat, x_ctx, dhx, gx_res, norm_w, scale3], comm, name="norm_bwd", grid=(rows // TM,),
        in_specs=[pl.BlockSpec((TM, D_MODEL), lambda i: (lat(i), 0)),
                  pl.BlockSpec((TM, D_MODEL), lambda i: (jnp.maximum(i - n_lat, 0), 0)),
                  pl.BlockSpec((TM, D_MODEL), lambda i: (i, 0)),
                  pl.BlockSpec((TM, D_MODEL), lambda i: (lat(i), 0)),
                  pl.BlockSpec((1, D_MODEL), lambda i: (0, 0)),
                  pl.BlockSpec((None, 1, D_MODEL), lambda i: (samp(i), 0, 0))],
        out_specs=(pl.BlockSpec((TM, D_MODEL), lambda i: (lat(i), 0)),
                   pl.BlockSpec((None, 1, D_MODEL), lambda i: (samp(i), 0, 0)),
                   pl.BlockSpec((None, 1, D_MODEL), lambda i: (samp(i), 0, 0)),
                   pl.BlockSpec((1, D_MODEL), lambda i: (0, 0))),
        out_shape=(SDS((n_lat * TM, D_MODEL), F32), SDS((n_samp + 1, 1, D_MODEL), F32),
                   SDS((n_samp + 1, 1, D_MODEL), F32), SDS((1, D_MODEL), F32)),
        compiler_params=_cp(("arbitrary",), 40))


def _gw_in(hxt, dpx_all, part, n_parts, comm=None):
    rows = dpx_all.shape[0]
    tb = _big_rows(rows)
    dp = D_MODEL // n_parts

    def body(h_ref, d_ref, o_ref):
        @pl.when(pl.program_id(1) == 0)
        def _():
            o_ref[...] = jnp.zeros_like(o_ref)

        o_ref[...] += jnp.dot(h_ref[...], d_ref[...], preferred_element_type=F32)

    return _call(body, [hxt, dpx_all], comm, name="gw_in", grid=(N_IN_BLK, rows // tb),
                 in_specs=[pl.BlockSpec((dp, tb), lambda j, i: (part, i)),
                           pl.BlockSpec((tb, IN_BLK), lambda j, i: (i, j))],
                 out_specs=pl.BlockSpec((None, dp, IN_BLK), lambda j, i: (j // 3, 0, j % 3)),
                 out_shape=SDS((N_SHARD, dp, IN_W), F32),
                 compiler_params=_cp(("arbitrary", "arbitrary"), 40))


def _dhx(dpx_all, w_in_g, tile0, n_tiles, dhx=None, comm=None):
    rows = dpx_all.shape[0]
    tb = _big_rows(rows)

    def body(d_ref, w_ref, *rest):
        o_ref = rest[-1]

        @pl.when(pl.program_id(1) == 0)
        def _():
            o_ref[...] = jnp.zeros_like(o_ref)

        o_ref[...] += lax.dot_general(d_ref[...], w_ref[...], (((1,), (1,)), ((), ())), preferred_element_type=F32)

    args, in_specs, aliases = [dpx_all, w_in_g], [
        pl.BlockSpec((tb, IN_BLK), lambda i, j: (tile0 + i, j)),
        pl.BlockSpec((None, D_MODEL, IN_BLK), lambda i, j: (j // 3, 0, j % 3))], None
    if dhx is not None:
        args, in_specs, aliases = args + [dhx], in_specs + [ANY], {2: 0}
    return _call(body, args, comm, name="dhx", grid=(n_tiles, N_IN_BLK), in_specs=in_specs,
                 out_specs=pl.BlockSpec((tb, D_MODEL), lambda i, j: (tile0 + i, 0)),
                 out_shape=SDS((rows, D_MODEL), F32), aliases=aliases,
                 compiler_params=_cp(("arbitrary", "arbitrary"), 40))


def _decays(lgv, d):
    c = RET_CHUNK
    ii = lax.broadcasted_iota(jnp.int32, (c, 1), 0).astype(F32)
    jj = lax.broadcasted_iota(jnp.int32, (1, c), 1).astype(F32)
    a_i = jnp.where(d == 0, ii, c - 1.0 - ii)
    a_j = jnp.where(d == 0, jj, c - 1.0 - jj)
    rel = a_i - a_j
    mask = jnp.where(rel >= 0, jnp.exp(lgv * jnp.maximum(rel, 0.0)), 0.0)
    qd = jnp.exp(lgv * (a_i + 1.0))
    kd = jnp.exp(lgv * (c - 1.0 - a_i))
    gc = jnp.exp(jnp.full((1, 1), lgv * c, F32))
    return a_i, rel, mask, qd, kd, gc


def _ctx_state_fwd(px, lg, n_samp, t_lat, lc):
    rb = t_lat // lc

    def body(lg_ref, k_ref, v_ref, o_ref):
        h = pl.program_id(1)
        k = k_ref[...].astype(F32) * (RET_DK ** -0.5)
        v = v_ref[...]
        pos = lax.broadcasted_iota(jnp.int32, (lc, 1), 0).astype(F32)
        o_ref[0] = _dot(k * jnp.exp(lg_ref[0, h] * (lc - 1.0 - pos)), v, 0, 0)
        o_ref[1] = _dot(k * jnp.exp(lg_ref[1, h] * pos), v, 0, 0)

    return pl.pallas_call(
        body, name="ctx_state_fwd", grid=(n_samp, RET_HEADS),
        in_specs=[SMEM,
                  pl.BlockSpec((lc, RET_DK), lambda b, h: (rb + b, C_RK // RET_DK + h)),
                  pl.BlockSpec((lc, RET_DV), lambda b, h: (rb + b, C_RV // RET_DV + h))],
        out_specs=pl.BlockSpec((None, 2, None, RET_DK, RET_DV), lambda b, h: (b, 0, h, 0, 0)),
        out_shape=SDS((n_samp, 2, RET_HEADS, RET_DK, RET_DV), F32),
        compiler_params=_cp(("parallel", "parallel")))(lg, px, px)


def _ctx_state_bwd(dpx, px, dstates, lg, n_samp, t_lat, lc):
    rb = t_lat // lc
    kspec = pl.BlockSpec((lc, RET_DK), lambda b, h: (rb + b, C_RK // RET_DK + h))
    vspec = pl.BlockSpec((lc, RET_DV), lambda b, h: (rb + b, C_RV // RET_DV + h))
    sspec = pl.BlockSpec((None, 2, None, RET_DK, RET_DV), lambda b, h: (b, 0, h, 0, 0))

    def weights(lg_ref, h):
        pos = lax.broadcasted_iota(jnp.int32, (lc, 1), 0).astype(F32)
        e_f = lc - 1.0 - pos
        return pos, e_f, jnp.exp(lg_ref[0, h] * e_f), jnp.exp(lg_ref[1, h] * pos)

    def k_body(lg_ref, dpx_hbm, k_ref, v_ref, ds_ref, dk_ref, dlg_ref):
        pos, e_f, w_f, w_b = weights(lg_ref, pl.program_id(1))
        k = k_ref[...].astype(F32) * (RET_DK ** -0.5)
        y_f = _dot(v_ref[...], ds_ref[0], 1, 1) * w_f
        y_b = _dot(v_ref[...], ds_ref[1], 1, 1) * w_b
        dk_ref[...] = ((y_f + y_b) * (RET_DK ** -0.5)).astype(BF)
        t_f = _sum_all(e_f * k * y_f)
        t_b = _sum_all(pos * k * y_b)
        sub = lax.broadcasted_iota(jnp.int32, (8, 128), 0)
        dlg_ref[...] = jnp.where(sub == 0, t_f, jnp.where(sub == 1, t_b, 0.0))

    def v_body(lg_ref, dpx_hbm, k_ref, ds_ref, dv_ref):
        _, _, w_f, w_b = weights(lg_ref, pl.program_id(1))
        k = k_ref[...].astype(F32) * (RET_DK ** -0.5)
        dv_ref[...] = (_dot(k * w_f, ds_ref[0]) + _dot(k * w_b, ds_ref[1])).astype(BF)

    dpx, dlg = pl.pallas_call(
        k_body, name="ctx_state_bwd_k", grid=(n_samp, RET_HEADS), input_output_aliases={1: 0},
        in_specs=[SMEM, ANY, kspec, vspec, sspec],
        out_specs=(kspec, pl.BlockSpec((None, None, 8, 128), lambda b, h: (b, h, 0, 0))),
        out_shape=(SDS(dpx.shape, dpx.dtype), SDS((n_samp, RET_HEADS, 8, 128), F32)),
        compiler_params=_cp(("parallel", "parallel")))(lg, dpx, px, px, dstates)
    dpx = pl.pallas_call(
        v_body, name="ctx_state_bwd_v", grid=(n_samp, RET_HEADS), input_output_aliases={1: 0},
        in_specs=[SMEM, ANY, kspec, sspec], out_specs=vspec, out_shape=SDS(dpx.shape, dpx.dtype),
        compiler_params=_cp(("parallel", "parallel")))(lg, dpx, px, dstates)
    return dpx, dlg


def _zero_ctx_tail(dpx, t_lat):
    wb = 512
    n_ctx = (dpx.shape[0] - t_lat) // TM

    def body(dpx_hbm, o_ref):
        o_ref[...] = jnp.zeros_like(o_ref)

    return pl.pallas_call(
        body, name="zero_ctx_tail", grid=(n_ctx, (IN_COLS - KV_COLS) // wb), input_output_aliases={0: 0},
        in_specs=[ANY], out_specs=pl.BlockSpec((TM, wb), lambda i, j: (t_lat // TM + i, KV_COLS // wb + j)),
        out_shape=SDS(dpx.shape, dpx.dtype),
        compiler_params=_cp(("parallel", "parallel")))(dpx)


def _ret_specs(row_f, row_b):
    c = RET_CHUNK
    wq = RET_HEADS * RET_DK // 2
    wv = RET_HEADS * RET_DV // 2
    specs = []
    for row in (row_f, row_b):
        specs += [pl.BlockSpec((c, wq), lambda b, n, row=row: (row(b, n), C_RQ // wq)),
                  pl.BlockSpec((c, wq), lambda b, n, row=row: (row(b, n), C_RQ // wq + 1)),
                  pl.BlockSpec((c, 2 * wq), lambda b, n, row=row: (row(b, n), C_RK // (2 * wq))),
                  pl.BlockSpec((c, wv), lambda b, n, row=row: (row(b, n), C_RV // wv)),
                  pl.BlockSpec((c, wv), lambda b, n, row=row: (row(b, n), C_RV // wv + 1))]
    return specs


def _ret_head(refs, h):
    q0, q1, k_ref, v0, v1 = refs
    hh = h % 2
    q = (q0, q1)[h // 2][:, hh * RET_DK:(hh + 1) * RET_DK].astype(F32)
    k = k_ref[:, h * RET_DK:(h + 1) * RET_DK].astype(F32) * (RET_DK ** -0.5)
    v = (v0, v1)[h // 2][:, hh * RET_DV:(hh + 1) * RET_DV]
    return q, k, v


def _ret_fwd(px, states0, lg, n_samp, seq, comm=None):
    c = RET_CHUNK
    nc = seq // c
    t_lat = n_samp * seq
    wo = RET_HEADS * RET_DV

    def row_f(b, n):
        return b * nc + n

    def row_b(b, n):
        return b * nc + nc - 1 - n

    def body(lg_ref, *refs):
        ins, (s0_ref, of_ref, ob_ref, st_ref, s_s) = refs[:10], refs[10:]

        @pl.when(pl.program_id(1) == 0)
        def _():
            s_s[...] = s0_ref[...]

        for d, o_ref in ((0, of_ref), (1, ob_ref)):
            for h in range(RET_HEADS):
                _, _, mask, qd, kd, gc = _decays(lg_ref[d, h], d)
                q, k, v = _ret_head(ins[5 * d:5 * d + 5], h)
                s = s_s[d, h]
                st_ref[h, d] = s.astype(BF)
                sc = _dot(q, k, 1, 1) * mask
                o_ref[:, h * RET_DV:(h + 1) * RET_DV] = (_dot(sc, v) + _dot(q * qd, s)).astype(BF)
                s_s[d, h] = s * gc + _dot(k * kd, v, 0, 0)

    return _call(
        body, [lg] + [px] * 10 + [states0], comm, name="ret_fwd", grid=(n_samp, nc),
        in_specs=[SMEM] + _ret_specs(row_f, row_b) + [
            pl.BlockSpec((None, 2, RET_HEADS, RET_DK, RET_DV), lambda b, n: (b, 0, 0, 0, 0))],
        out_specs=(pl.BlockSpec((c, wo), lambda b, n: (row_f(b, n), 0)),
                   pl.BlockSpec((c, wo), lambda b, n: (row_b(b, n), 0)),
                   pl.BlockSpec((None, RET_HEADS, 2, None, RET_DK, RET_DV), lambda b, n: (b, 0, 0, n, 0, 0))),
        out_shape=(SDS((t_lat, wo), BF), SDS((t_lat, wo), BF),
                   SDS((n_samp, RET_HEADS, 2, nc, RET_DK, RET_DV), BF)),
        scratch_shapes=[pltpu.VMEM((2, RET_HEADS, RET_DK, RET_DV), F32)],
        compiler_params=_cp(("arbitrary", "arbitrary"), 48))


def _ret_bwd(px, do, saved, lg, n_samp, seq, comm=None):
    c = RET_CHUNK
    nc = seq // c
    t_lat = n_samp * seq
    wq, wo = RET_HEADS * RET_DK, RET_HEADS * RET_DV

    def row_f(b, n):
        return b * nc + nc - 1 - n

    def row_b(b, n):
        return b * nc + n

    def body(lg_ref, *refs):
        ins = refs[:10]
        (dof_ref, dob_ref, st_ref, dqf, dkf, dvf, dqb, dkb, dvb, ds0_ref, dlg_ref, ds_s, acc_s) = refs[10:]
        n = pl.program_id(1)

        @pl.when(n == 0)
        def _():
            ds_s[...] = jnp.zeros_like(ds_s)
            acc_s[...] = jnp.zeros_like(acc_s)

        for d, (do_ref, dq_ref, dk_ref, dv_ref) in enumerate(((dof_ref, dqf, dkf, dvf), (dob_ref, dqb, dkb, dvb))):
            for h in range(RET_HEADS):
                a_i, rel, mask, qd, kd, gc = _decays(lg_ref[d, h], d)
                q, k, v = _ret_head(ins[5 * d:5 * d + 5], h)
                qb, kb, vb = q.astype(BF), k.astype(BF), v.astype(BF)
                dob = do_ref[:, h * RET_DV:(h + 1) * RET_DV].astype(BF)
                sb = st_ref[h, d]
                ds = ds_s[d, h]
                dsb = ds.astype(BF)
                raw = _dot(qb, kb, 1, 1)
                sc = raw * mask
                dsc = _dot(dob, vb, 1, 1) * mask
                dscb = dsc.astype(BF)
                x = _dot(dob, sb, 1, 1)
                y = _dot(vb, dsb, 1, 1)
                qq = q * qd
                kk = k * kd
                dq_ref[:, h * RET_DK:(h + 1) * RET_DK] = (_dot(dscb, kb) + x * qd).astype(BF)
                dk_ref[:, h * RET_DK:(h + 1) * RET_DK] = (_dot(dscb, qb, 0, 0) + y * kd).astype(BF)
                dv_ref[:, h * RET_DV:(h + 1) * RET_DV] = (_dot(sc, dob, 0, 0) + _dot(kk, dsb)).astype(BF)
                t = (_sum_all(dsc * raw * rel) + _sum_all((a_i + 1.0) * qq * x)
                     + _sum_all((c - 1.0 - a_i) * kk * y) + c * gc * _sum_all(ds * sb.astype(F32)))
                acc_s[4 * d + h:4 * d + h + 1, :] += t
                ds_s[d, h] = ds * gc + _dot(qq, dob, 0, 0)

        @pl.when(n == nc - 1)
        def _():
            ds0_ref[...] = ds_s[...]
            dlg_ref[...] = acc_s[...]

    do_spec_f = pl.BlockSpec((c, wo), lambda b, n: (row_f(b, n), 0))
    do_spec_b = pl.BlockSpec((c, wo), lambda b, n: (row_b(b, n), 0))
    dq_spec_f = pl.BlockSpec((c, wq), lambda b, n: (row_f(b, n), 0))
    dq_spec_b = pl.BlockSpec((c, wq), lambda b, n: (row_b(b, n), 0))
    return _call(
        body, [lg] + [px] * 10 + [do, do, saved], comm, name="ret_bwd", grid=(n_samp, nc),
        in_specs=[SMEM] + _ret_specs(row_f, row_b) + [
            do_spec_f, do_spec_b,
            pl.BlockSpec((None, RET_HEADS, 2, None, RET_DK, RET_DV), lambda b, n: (b, 0, 0, nc - 1 - n, 0, 0))],
        out_specs=(dq_spec_f, dq_spec_f, do_spec_f, dq_spec_b, dq_spec_b, do_spec_b,
                   pl.BlockSpec((None, 2, RET_HEADS, RET_DK, RET_DV), lambda b, n: (b, 0, 0, 0, 0)),
                   pl.BlockSpec((None, 8, 128), lambda b, n: (b, 0, 0))),
        out_shape=(SDS((t_lat, wq), BF), SDS((t_lat, wq), BF), SDS((t_lat, wo), BF),
                   SDS((t_lat, wq), BF), SDS((t_lat, wq), BF), SDS((t_lat, wo), BF),
                   SDS((n_samp, 2, RET_HEADS, RET_DK, RET_DV), F32), SDS((n_samp, 8, 128), F32)),
        scratch_shapes=[pltpu.VMEM((2, RET_HEADS, RET_DK, RET_DV), F32), pltpu.VMEM((8, 128), F32)],
        compiler_params=_cp(("arbitrary", "arbitrary"), 56))


def _combine_into(dpx, a, b, col0, scale):
    t_lat, width = a.shape
    wb = 512
    assert col0 % wb == 0 and width % wb == 0

    def body(dpx_hbm, a_ref, b_ref, o_ref):
        o_ref[...] = ((a_ref[...].astype(F32) + b_ref[...].astype(F32)) * scale).astype(BF)

    src = pl.BlockSpec((TM, wb), lambda i, j: (i, j))
    return pl.pallas_call(
        body, name="combine_into", grid=(t_lat // TM, width // wb), input_output_aliases={0: 0},
        in_specs=[ANY, src, src], out_specs=pl.BlockSpec((TM, wb), lambda i, j: (i, col0 // wb + j)),
        out_shape=SDS(dpx.shape, dpx.dtype),
        compiler_params=_cp(("parallel", "parallel")))(dpx, a, b)


def _retnorm_fwd(o_f, o_b, px):
    t_lat = o_f.shape[0]

    def body(of_ref, ob_ref, g_ref, y_ref):
        o = of_ref[...].astype(F32) + ob_ref[...].astype(F32)
        g = g_ref[...].astype(F32)
        y_ref[...] = (o * _rms(o) * (g * _sigmoid(g))).astype(BF)

    so = pl.BlockSpec((TM, RET_DV), lambda i, h: (i, h))
    return pl.pallas_call(
        body, name="retnorm_fwd", grid=(t_lat // TM, RET_HEADS),
        in_specs=[so, so, pl.BlockSpec((TM, RET_DV), lambda i, h: (i, C_RG // RET_DV + h))],
        out_specs=so,
        out_shape=SDS((t_lat, RET_HEADS * RET_DV), BF),
        compiler_params=_cp(("parallel", "parallel")))(o_f, o_b, px)


def _retnorm_bwd(dpx, dy, o_f, o_b, px):
    t_lat = o_f.shape[0]

    def body(dpx_hbm, dy_ref, of_ref, ob_ref, g_ref, do_ref, dg_ref):
        o = of_ref[...].astype(F32) + ob_ref[...].astype(F32)
        r = _rms(o)
        on = o * r
        g = g_ref[...].astype(F32)
        sg = _sigmoid(g)
        dy_ = dy_ref[...].astype(F32)
        dg_ref[...] = (dy_ * on * (sg * (1.0 + g * (1.0 - sg)))).astype(BF)
        do_ref[...] = _rms_bwd(dy_ * (g * sg), on, r).astype(BF)

    so = pl.BlockSpec((TM, RET_DV), lambda i, h: (i, h))
    gcol = pl.BlockSpec((TM, RET_DV), lambda i, h: (i, C_RG // RET_DV + h))
    return pl.pallas_call(
        body, name="retnorm_bwd", grid=(t_lat // TM, RET_HEADS), input_output_aliases={0: 1},
        in_specs=[ANY, so, so, so, gcol],
        out_specs=(so, gcol),
        out_shape=(SDS((t_lat, RET_HEADS * RET_DV), BF), SDS(dpx.shape, dpx.dtype)),
        compiler_params=_cp(("parallel", "parallel")))(dpx, dy, o_f, o_b, px)


def _norm_rope(x, w, cos, sin):
    xn = x * _rms(x) * w
    return xn * cos + _swap_pairs(xn) * sin


def _norm_rope_bwd(dy, x, w, cos, sin):
    dxn = dy * cos + _swap_pairs(dy * sin)
    r = _rms(x)
    xh = x * r
    return _rms_bwd(dxn * w, xh, r), jnp.sum(dxn * xh, axis=0, keepdims=True)


def _att_prep_q(px, cos_all, sin_all, qnw, t_lat):
    hd = ATT_HEAD_DIM
    wblk = ATT_REP * hd

    def body(x_ref, cos_ref, sin_ref, w_ref, o_ref):
        for r in range(ATT_REP):
            cols = slice(r * hd, (r + 1) * hd)
            qr = _norm_rope(x_ref[:, cols].astype(F32), w_ref[...], cos_ref[...], sin_ref[...])
            o_ref[:, cols] = (qr * (hd ** -0.5)).astype(BF)

    return pl.pallas_call(
        body, name="att_prep_q", grid=(t_lat // TM, ATT_KV_HEADS),
        in_specs=[pl.BlockSpec((TM, wblk), lambda i, g: (i, C_AQ // wblk + g)),
                  pl.BlockSpec((TM, hd), lambda i, g: (i, 0)),
                  pl.BlockSpec((TM, hd), lambda i, g: (i, 0)),
                  pl.BlockSpec((1, hd), lambda i, g: (0, 0))],
        out_specs=pl.BlockSpec((TM, wblk), lambda i, g: (i, g)),
        out_shape=SDS((t_lat, ATT_HEADS * hd), BF),
        compiler_params=_cp(("parallel", "parallel")))(px, cos_all, sin_all, qnw)


def _att_prep_kv(px, cos_all, sin_all, knw):
    rows = px.shape[0]
    hd = ATT_HEAD_DIM
    kvw = ATT_KV_HEADS * hd

    def body(x_ref, cos_ref, sin_ref, w_ref, k_ref, v_ref):
        for g in range(ATT_KV_HEADS):
            cols = slice(g * hd, (g + 1) * hd)
            k_ref[:, cols] = _norm_rope(x_ref[:, cols].astype(F32), w_ref[...], cos_ref[...],
                                        sin_ref[...]).astype(BF)
        v_ref[...] = x_ref[:, kvw:].astype(BF)

    return pl.pallas_call(
        body, name="att_prep_kv", grid=(rows // TM,),
        in_specs=[pl.BlockSpec((TM, 2 * kvw), lambda i: (i, C_AK // (2 * kvw))),
                  pl.BlockSpec((TM, hd), lambda i: (i, 0)),
                  pl.BlockSpec((TM, hd), lambda i: (i, 0)),
                  pl.BlockSpec((1, hd), lambda i: (0, 0))],
        out_specs=(pl.BlockSpec((TM, kvw), lambda i: (i, 0)), pl.BlockSpec((TM, kvw), lambda i: (i, 0))),
        out_shape=(SDS((rows, kvw), BF), SDS((rows, kvw), BF)),
        compiler_params=_cp(("parallel",)))(px, cos_all, sin_all, knw)


def _att_kv_bwd(dpx, dkl, dkc, dvl, dvc, px, cos_all, sin_all, knw):
    rows = px.shape[0]
    hd = ATT_HEAD_DIM
    kvw = ATT_KV_HEADS * hd
    n_lat = dkl.shape[0] // TM
    assert dkc.shape[0] == TM

    def body(dpx_hbm, dkl_ref, dkc_ref, dvl_ref, dvc_ref, x_ref, cos_ref, sin_ref, w_ref, o_ref, gw_ref):
        i = pl.program_id(0)

        @pl.when(i == 0)
        def _():
            gw_ref[...] = jnp.zeros_like(gw_ref)

        is_lat = i < n_lat
        dk = jnp.where(is_lat, dkl_ref[...], dkc_ref[...])
        dv = jnp.where(is_lat, dvl_ref[...], dvc_ref[...])
        for g in range(ATT_KV_HEADS):
            cols = slice(g * hd, (g + 1) * hd)
            dx, gw = _norm_rope_bwd(dk[:, cols], x_ref[:, cols].astype(F32), w_ref[...], cos_ref[...], sin_ref[...])
            o_ref[:, cols] = dx.astype(BF)
            gw_ref[...] += gw
        o_ref[:, kvw:] = dv.astype(BF)

    lat = pl.BlockSpec((TM, kvw), lambda i: (jnp.minimum(i, n_lat - 1), 0))
    ctx = pl.BlockSpec((TM, kvw), lambda i: (0, 0))
    kvcol = pl.BlockSpec((TM, 2 * kvw), lambda i: (i, C_AK // (2 * kvw)))
    return pl.pallas_call(
        body, name="att_kv_bwd", grid=(rows // TM,), input_output_aliases={0: 0},
        in_specs=[ANY, lat, ctx, lat, ctx, kvcol,
                  pl.BlockSpec((TM, hd), lambda i: (i, 0)),
                  pl.BlockSpec((TM, hd), lambda i: (i, 0)),
                  pl.BlockSpec((1, hd), lambda i: (0, 0))],
        out_specs=(kvcol, pl.BlockSpec((1, hd), lambda i: (0, 0))),
        out_shape=(SDS(dpx.shape, dpx.dtype), SDS((1, hd), F32)),
        compiler_params=_cp(("arbitrary",)))(dpx, dkl, dkc, dvl, dvc, px, cos_all, sin_all, knw)


def _stack_heads(ref_or_val):
    hd = ATT_HEAD_DIM
    return jnp.concatenate([ref_or_val[:, r * hd:(r + 1) * hd] for r in range(ATT_REP)], axis=0)


def _att_scores(q, kl, kc):
    sl = _dot(q, kl, 1, 1)
    sc = _dot(q, kc, 1, 1)
    m = jnp.maximum(jnp.max(sl, axis=-1, keepdims=True), jnp.max(sc, axis=-1, keepdims=True))
    el = jnp.exp(sl - m)
    ec = jnp.exp(sc - m)
    denom = jnp.sum(el, axis=-1, keepdims=True) + jnp.sum(ec, axis=-1, keepdims=True)
    return el, ec, denom, m


def _att_fwd(qn, kn, vn, n_samp, seq, lc):
    hd = ATT_HEAD_DIM
    tq = ATT_TQ
    nq = seq // tq
    wblk = ATT_REP * hd
    cb = n_samp * seq // lc
    t_lat = n_samp * seq

    def body(q_ref, kl_ref, kc_ref, vl_ref, vc_ref, o_ref, lse_ref):
        lane = lax.broadcasted_iota(jnp.int32, (tq, hd), 1)
        lse = jnp.zeros((tq, hd), F32)
        for r in range(ATT_REP):
            cols = slice(r * hd, (r + 1) * hd)
            el, ec, denom, m = _att_scores(q_ref[:, cols], kl_ref[...], kc_ref[...])
            o_ref[:, cols] = ((_dot(el, vl_ref[...]) + _dot(ec, vc_ref[...])) / denom).astype(BF)
            lse = jnp.where(lane == r, m + jnp.log(denom), lse)
        lse_ref[...] = lse

    return pl.pallas_call(
        body, name="att_fwd", grid=(n_samp, ATT_KV_HEADS, nq),
        in_specs=[pl.BlockSpec((tq, wblk), lambda b, g, i: (b * nq + i, g)),
                  pl.BlockSpec((seq, hd), lambda b, g, i: (b, g)),
                  pl.BlockSpec((lc, hd), lambda b, g, i: (cb + b, g)),
                  pl.BlockSpec((seq, hd), lambda b, g, i: (b, g)),
                  pl.BlockSpec((lc, hd), lambda b, g, i: (cb + b, g))],
        out_specs=(pl.BlockSpec((tq, wblk), lambda b, g, i: (b * nq + i, g)),
                   pl.BlockSpec((tq, hd), lambda b, g, i: (b * nq + i, g))),
        out_shape=(SDS((t_lat, ATT_HEADS * hd), BF), SDS((t_lat, ATT_KV_HEADS * hd), F32)),
        compiler_params=_cp(("parallel", "parallel", "parallel"), 48))(qn, kn, kn, vn, vn)


def _att_gate_bwd(dpx, dy_att, o_att, px):
    t_lat = dy_att.shape[0]
    wblk = ATT_REP * ATT_HEAD_DIM

    def body(dpx_hbm, dy_ref, o_ref, g_ref, out_ref):
        g = g_ref[...].astype(F32)
        sg = _sigmoid(g)
        out_ref[...] = (dy_ref[...].astype(F32) * o_ref[...].astype(F32) * (sg * (1.0 + g * (1.0 - sg)))).astype(BF)

    blk = pl.BlockSpec((TM, wblk), lambda i, j: (i, j))
    gcol = pl.BlockSpec((TM, wblk), lambda i, j: (i, C_AG // wblk + j))
    return pl.pallas_call(
        body, name="att_gate_bwd", grid=(t_lat // TM, ATT_KV_HEADS),
        in_specs=[ANY, blk, blk, gcol], out_specs=gcol, out_shape=SDS(dpx.shape, dpx.dtype),
        input_output_aliases={0: 0},
        compiler_params=_cp(("parallel", "parallel")))(dpx, dy_att, o_att, px)


def _att_bwd(dpx, qn, kn, vn, px, o_att, lse, do_att, cos_all, sin_all, qnw, n_samp, seq, lc, comm=None):
    hd = ATT_HEAD_DIM
    tq = ATT_TQ
    nq = seq // tq
    wblk = ATT_REP * hd
    cb = n_samp * seq // lc
    t_lat = n_samp * seq
    kvw = ATT_KV_HEADS * hd
    scale = hd ** -0.5

    def body(dpx_hbm, q_ref, kl_ref, kc_ref, vl_ref, vc_ref, o_ref, do_ref, x_ref, cos_ref, sin_ref, w_ref,
             lse_ref, dq_ref, dkl_ref, dkc_ref, dvl_ref, dvc_ref, gw_ref, akl, akc, avl, avc, aw):
        i = pl.program_id(2)

        @pl.when(i == 0)
        def _():
            akl[...] = jnp.zeros_like(akl)
            akc[...] = jnp.zeros_like(akc)
            avl[...] = jnp.zeros_like(avl)
            avc[...] = jnp.zeros_like(avc)
            aw[...] = jnp.zeros_like(aw)

        dobs, pls, pcs, dsls, dscs = [], [], [], [], []
        for r in range(ATT_REP):
            cols = slice(r * hd, (r + 1) * hd)
            dob = do_ref[:, cols]
            delta = jnp.sum(dob.astype(F32) * o_ref[:, cols].astype(F32), axis=-1, keepdims=True)
            lse = lse_ref[:, r:r + 1]
            p_l = jnp.exp(_dot(q_ref[:, cols], kl_ref[...], 1, 1) - lse).astype(BF)
            p_c = jnp.exp(_dot(q_ref[:, cols], kc_ref[...], 1, 1) - lse).astype(BF)
            ds_l = (p_l * (_dot(dob, vl_ref[...], 1, 1) - delta)).astype(BF)
            ds_c = (p_c * (_dot(dob, vc_ref[...], 1, 1) - delta)).astype(BF)
            dq = (_dot(ds_l, kl_ref[...]) + _dot(ds_c, kc_ref[...])) * scale
            dx, gw = _norm_rope_bwd(dq, x_ref[:, cols].astype(F32), w_ref[...], cos_ref[...], sin_ref[...])
            dq_ref[:, cols] = dx.astype(BF)
            aw[...] += gw
            dobs.append(dob)
            pls.append(p_l)
            pcs.append(p_c)
            dsls.append(ds_l)
            dscs.append(ds_c)
        do4 = jnp.concatenate(dobs, axis=0)
        q4 = _stack_heads(q_ref)
        avl[...] += _dot(jnp.concatenate(pls, axis=0), do4, 0, 0)
        avc[...] += _dot(jnp.concatenate(pcs, axis=0), do4, 0, 0)
        akl[...] += _dot(jnp.concatenate(dsls, axis=0), q4, 0, 0)
        akc[...] += _dot(jnp.concatenate(dscs, axis=0), q4, 0, 0)

        @pl.when(i == nq - 1)
        def _():
            dkl_ref[...] = akl[...]
            dkc_ref[...] = akc[...]
            dvl_ref[...] = avl[...]
            dvc_ref[...] = avc[...]
            gw_ref[...] = aw[...]

    return _call(
        body, [dpx, qn, kn, kn, vn, vn, o_att, do_att, px, cos_all, sin_all, qnw, lse], comm,
        name="att_bwd", grid=(n_samp, ATT_KV_HEADS, nq), aliases={0: 0},
        in_specs=[ANY,
                  pl.BlockSpec((tq, wblk), lambda b, g, i: (b * nq + i, g)),
                  pl.BlockSpec((seq, hd), lambda b, g, i: (b, g)),
                  pl.BlockSpec((lc, hd), lambda b, g, i: (cb + b, g)),
                  pl.BlockSpec((seq, hd), lambda b, g, i: (b, g)),
                  pl.BlockSpec((lc, hd), lambda b, g, i: (cb + b, g)),
                  pl.BlockSpec((tq, wblk), lambda b, g, i: (b * nq + i, g)),
                  pl.BlockSpec((tq, wblk), lambda b, g, i: (b * nq + i, g)),
                  pl.BlockSpec((tq, wblk), lambda b, g, i: (b * nq + i, C_AQ // wblk + g)),
                  pl.BlockSpec((tq, hd), lambda b, g, i: (b * nq + i, 0)),
                  pl.BlockSpec((tq, hd), lambda b, g, i: (b * nq + i, 0)),
                  pl.BlockSpec((1, hd), lambda b, g, i: (0, 0)),
                  pl.BlockSpec((tq, hd), lambda b, g, i: (b * nq + i, g))],
        out_specs=(pl.BlockSpec((tq, wblk), lambda b, g, i: (b * nq + i, C_AQ // wblk + g)),
                   pl.BlockSpec((seq, hd), lambda b, g, i: (b, g)),
                   pl.BlockSpec((lc, hd), lambda b, g, i: (b, g)),
                   pl.BlockSpec((seq, hd), lambda b, g, i: (b, g)),
                   pl.BlockSpec((lc, hd), lambda b, g, i: (b, g)),
                   pl.BlockSpec((None, None, 1, hd), lambda b, g, i: (b, g, 0, 0))),
        out_shape=(SDS(dpx.shape, dpx.dtype),
                   SDS((t_lat, kvw), F32), SDS((n_samp * lc, kvw), F32),
                   SDS((t_lat, kvw), F32), SDS((n_samp * lc, kvw), F32),
                   SDS((n_samp, ATT_KV_HEADS, 1, hd), F32)),
        scratch_shapes=[pltpu.VMEM((seq, hd), F32), pltpu.VMEM((lc, hd), F32),
                        pltpu.VMEM((seq, hd), F32), pltpu.VMEM((lc, hd), F32), pltpu.VMEM((1, hd), F32)],
        compiler_params=_cp(("arbitrary", "arbitrary", "arbitrary"), 56))


def _merge(x_lat, target, o_f, o_b, o_att, px, gate3, w_o_ret, w_o_att, w_out, tiles_per_sample):
    t_lat = x_lat.shape[0]
    tm = 256
    n_t = t_lat // tm
    per = tiles_per_sample * (TM // tm)
    d = D_MODEL
    rv = RET_HEADS * RET_DV
    n_samp = gate3.shape[0] - 1

    half = d // 2
    n_px = 10

    def body(x_ref, t_ref, of_ref, ob_ref, oa_ref, *rest):
        pxs, rest = rest[:n_px], rest[n_px:]
        (gt_ref, wor_ref, woa_ref, wout_ref,
         gx_ref, dor_ref, doa_ref, dpx_hbm, loss_ref, dgt_ref, gwor_hbm, gwoa_hbm, gwout_hbm,
         aor, aoa, aout, drg_ref, dtail_ref, sems) = rest
        i = pl.program_id(0)

        def copies(step):
            rows = pl.ds(pl.multiple_of(step * tm, tm), tm)
            return (pltpu.make_async_copy(drg_ref, dpx_hbm.at[rows, pl.ds(C_RG, rv)], sems.at[0]),
                    pltpu.make_async_copy(dtail_ref, dpx_hbm.at[rows, pl.ds(C_AG, 3 * d)], sems.at[1]))

        @pl.when(i == 0)
        def _():
            aor[...] = jnp.zeros_like(aor)
            aoa[...] = jnp.zeros_like(aoa)
            aout[...] = jnp.zeros_like(aout)
            loss_ref[...] = jnp.zeros_like(loss_ref)

        @pl.when(i % per == 0)
        def _():
            dgt_ref[...] = jnp.zeros_like(dgt_ref)

        def cat(refs):
            return jnp.concatenate([r[...] for r in refs], axis=1).astype(F32)

        def ret_head(h):
            cols = slice(h * RET_DV, (h + 1) * RET_DV)
            o = of_ref[:, cols].astype(F32) + ob_ref[:, cols].astype(F32)
            r = _rms(o)
            g = pxs[h][...].astype(F32)
            return o * r, r, g, _sigmoid(g)

        def att_half(k):
            o = oa_ref[:, k * half:(k + 1) * half].astype(F32)
            g = pxs[4 + k][...].astype(F32)
            return o, g, _sigmoid(g)

        yrs = []
        for h in range(RET_HEADS):
            on, _, g, sg = ret_head(h)
            yrs.append((on * (g * sg)).astype(BF))
        yr = jnp.concatenate(yrs, axis=1)
        yas = []
        for k in range(2):
            o, g, sg = att_half(k)
            yas.append((o * (g * sg)).astype(BF))
        ya = jnp.concatenate(yas, axis=1)

        a = jnp.dot(yr, wor_ref[...], preferred_element_type=F32)
        b = jnp.dot(ya, woa_ref[...], preferred_element_type=F32)
        sr = _sigmoid(cat(pxs[6:8]))
        sa = _sigmoid(cat(pxs[8:10]))
        yb = (sr * a + sa * b).astype(BF)
        out = jnp.dot(yb, wout_ref[...], preferred_element_type=F32)
        gate = gt_ref[...]
        err = x_ref[...] + gate * out - t_ref[...]
        loss_ref[...] += 0.5 * _sum_all(err * err) * (1.0 / d)
        dy_tok = err * (1.0 / d)
        gx_ref[...] = dy_tok
        dgt_ref[...] += jnp.sum(dy_tok * out, axis=0, keepdims=True)
        dout = (dy_tok * gate).astype(BF)
        aout[...] += _dot(yb, dout, 0, 0)
        dyy = _dot(dout, wout_ref[...], 1, 1)
        da = (dyy * sr).astype(BF)
        db = (dyy * sa).astype(BF)
        aor[...] += _dot(yr, da, 0, 0)
        aoa[...] += _dot(ya, db, 0, 0)
        dyr = _dot(da, wor_ref[...], 1, 1)
        dya = _dot(db, woa_ref[...], 1, 1)

        @pl.when(i > 0)
        def _():
            for cp in copies(i - 1):
                cp.wait()

        dtail_ref[:, d:2 * d] = (dyy * a * (sr * (1.0 - sr))).astype(BF)
        dtail_ref[:, 2 * d:] = (dyy * b * (sa * (1.0 - sa))).astype(BF)
        for h in range(RET_HEADS):
            cols = slice(h * RET_DV, (h + 1) * RET_DV)
            on, r, g, sg = ret_head(h)
            dy = dyr[:, cols]
            drg_ref[:, cols] = (dy * on * (sg * (1.0 + g * (1.0 - sg)))).astype(BF)
            dor_ref[:, cols] = _rms_bwd(dy * (g * sg), on, r).astype(BF)
        for k in range(2):
            cols = slice(k * half, (k + 1) * half)
            o, g, sg = att_half(k)
            dy = dya[:, cols]
            dtail_ref[:, cols] = (dy * o * (sg * (1.0 + g * (1.0 - sg)))).astype(BF)
            doa_ref[:, cols] = (dy * (g * sg)).astype(BF)
        for cp in copies(i):
            cp.start()

        @pl.when(i == n_t - 1)
        def _():
            for cp in copies(i):
                cp.wait()
            pltpu.sync_copy(aor, gwor_hbm)
            pltpu.sync_copy(aoa, gwoa_hbm)
            pltpu.sync_copy(aout, gwout_hbm)

    def px_blk(col):
        return pl.BlockSpec((tm, half), lambda i: (i, col // half))

    def resident(shape):
        return pl.BlockSpec(shape, lambda i: (0, 0), pipeline_mode=pl.Buffered(1))

    px_cols = ([C_RG + k * half for k in range(4)] + [C_AG, C_AG + half]
               + [C_MR, C_MR + half, C_MA, C_MA + half])
    return pl.pallas_call(
        body, name="merge", grid=(n_t,),
        in_specs=[pl.BlockSpec((tm, d), lambda i: (i, 0)),
                  pl.BlockSpec((tm, d), lambda i: (i, 0)),
                  pl.BlockSpec((tm, rv), lambda i: (i, 0)),
                  pl.BlockSpec((tm, rv), lambda i: (i, 0)),
                  pl.BlockSpec((tm, d), lambda i: (i, 0))]
        + [px_blk(col) for col in px_cols]
        + [pl.BlockSpec((None, 1, d), lambda i: (i // per, 0, 0)),
           resident((rv, d)), resident((d, d)), resident((d, d))],
        out_specs=(pl.BlockSpec((tm, d), lambda i: (i, 0)),
                   pl.BlockSpec((tm, rv), lambda i: (i, 0)),
                   pl.BlockSpec((tm, d), lambda i: (i, 0)),
                   ANY,
                   pl.BlockSpec((8, 128), lambda i: (0, 0)),
                   pl.BlockSpec((None, 1, d), lambda i: (i // per, 0, 0)),
                   ANY, ANY, ANY),
        out_shape=(SDS((t_lat, d), F32), SDS((t_lat, rv), BF), SDS((t_lat, d), BF),
                   SDS((px.shape[0], IN_COLS), BF),
                   SDS((8, 128), F32), SDS((n_samp, 1, d), F32),
                   SDS((rv, d), F32), SDS((d, d), F32), SDS((d, d), F32)),
        scratch_shapes=[pltpu.VMEM((rv, d), F32), pltpu.VMEM((d, d), F32), pltpu.VMEM((d, d), F32),
                        pltpu.VMEM((tm, rv), BF), pltpu.VMEM((tm, 3 * d), BF), pltpu.SemaphoreType.DMA((2,))],
        compiler_params=_cp(("arbitrary",), 56))(
            x_lat, target, o_f, o_b, o_att, *([px] * n_px), gate3, w_o_ret, w_o_att, w_out)


def _place():
    x, y, c = lax.axis_index("x"), lax.axis_index("y"), lax.axis_index("c")
    chips = [(1 - x, y), (x, 1 - y), (1 - x, 1 - y)]
    return x, y, c, chips


def _remote(src, dst, send_sem, recv_sem, to):
    return pltpu.make_async_remote_copy(src_ref=src, dst_ref=dst, send_sem=send_sem, recv_sem=recv_sem,
                                        device_id=to, device_id_type=MESH)


def _place_ids():
    x, y, c = lax.axis_index("x"), lax.axis_index("y"), lax.axis_index("c")
    me = 2 * x + y
    return jnp.stack([x, y, c, me, me, 2 * (1 - x) + y, 2 * x + 1 - y, 2 * (1 - x) + 1 - y]).astype(jnp.int32)


def _ag_comm(bufs, rels, arg_index=None):
    n, m = len(bufs), len(rels)

    def half(ref, s, which):
        h = ref.shape[1] // 2
        return ref.at[s, pl.ds(which * h, h), :]

    def ici(ins, outs, ssem, rsem, base):
        x, y, c, chips = _place()
        sends, recvs = [], []
        for a in range(n):
            for jj, j in enumerate(rels):
                k, chip = base + a * m + jj, chips[j]
                mine, theirs = half(outs[a], 2 * x + y, c), half(outs[a], 2 * chip[0] + chip[1], c)
                sends.append(_remote(mine, mine, ssem.at[k], rsem.at[k], (*chip, c)))
                recvs.append(_remote(theirs, theirs, ssem.at[k], rsem.at[k], (*chip, c)))
        return sends, recvs

    def d2d(ins, outs, ssem, rsem, base):
        x, y, c, chips = _place()
        sends, recvs = [], []
        for a in range(n):
            for jj, j in enumerate(rels):
                k, s = base + (n + a) * m + jj, 2 * chips[j][0] + chips[j][1]
                sends.append(_remote(half(outs[a], s, c), half(outs[a], s, c), ssem.at[k], rsem.at[k], (x, y, 1 - c)))
                recvs.append(_remote(half(outs[a], s, 1 - c), half(outs[a], s, 1 - c), ssem.at[k], rsem.at[k],
                                     (x, y, 1 - c)))
        return sends, recvs

    shapes = tuple(SDS(b.shape, b.dtype) for b in bufs)
    if arg_index is not None:
        return _Comm("all_gather", (), shapes, {}, 2 * n * m, (ici, d2d), ((arg_index, 0),))
    return _Comm("all_gather", tuple(bufs), shapes, {a: a for a in range(n)}, 2 * n * m, (ici, d2d))


def _swap_comm(grads):
    n = len(grads)

    def phase(ins, outs, ssem, rsem, base):
        x, y, c, _ = _place()
        sends = []
        for a in range(n):
            h = ins[a].shape[1] // 2
            sends.append(_remote(ins[a].at[:, pl.ds((1 - c) * h, h), :], outs[a], ssem.at[base + a],
                                 rsem.at[base + a], (x, y, 1 - c)))
        return sends, sends

    return _Comm("swap_halves", tuple(grads),
                 tuple(SDS((g.shape[0], g.shape[1] // 2, g.shape[2]), g.dtype) for g in grads), {}, n, (phase,))


def _exchange_comm(parts):
    n = len(parts)

    def phase(ins, outs, ssem, rsem, base):
        x, y, c, chips = _place()
        sends = []
        for a in range(n):
            for j, chip in enumerate(chips):
                k = base + 3 * a + j
                sends.append(_remote(ins[a].at[2 * chip[0] + chip[1]], outs[a].at[j], ssem.at[k], rsem.at[k],
                                     (*chip, c)))
        return sends, sends

    return _Comm("exchange_shards", tuple(parts), tuple(SDS((3,) + p.shape[1:], p.dtype) for p in parts), {}, 3 * n,
                 (phase,))


def _join_comm(bufs, n_parts=1):
    n = len(bufs)

    def phase(ins, outs, ssem, rsem, base):
        x, y, c, _ = _place()
        sends, recvs = [], []
        for a in range(n):
            h = outs[a].shape[0] // (2 * n_parts)
            for p in range(n_parts):
                k = base + a * n_parts + p
                mine = outs[a].at[pl.ds((2 * p + c) * h, h), :]
                other = outs[a].at[pl.ds((2 * p + 1 - c) * h, h), :]
                sends.append(_remote(mine, mine, ssem.at[k], rsem.at[k], (x, y, 1 - c)))
                recvs.append(_remote(other, other, ssem.at[k], rsem.at[k], (x, y, 1 - c)))
        return sends, recvs

    return _Comm("join_halves", tuple(bufs), tuple(SDS(b.shape, b.dtype) for b in bufs), {a: a for a in range(n)},
                 n * n_parts, (phase,))


def _cast_place(w, ids):
    rows, cols = w.shape
    tr = min(rows, 256)

    def body(ids_ref, w_ref, o_ref):
        o_ref[...] = w_ref[...].astype(BF)

    return pl.pallas_call(
        body, name="cast_place",
        grid_spec=pltpu.PrefetchScalarGridSpec(
            num_scalar_prefetch=1, grid=(rows // tr,),
            in_specs=[pl.BlockSpec((tr, cols), lambda i, ids_ref: (i, 0))],
            out_specs=pl.BlockSpec((None, tr, cols), lambda i, ids_ref: (ids_ref[3], i, 0))),
        out_shape=SDS((N_SHARD, rows, cols), BF),
        compiler_params=_cp(("parallel",), 40))(ids, w)


def _all_gather_weights(bufs):
    n = len(bufs)

    def body(*refs):
        outs = refs[n:2 * n]
        send_sems, recv_sems = refs[2 * n:]
        x, y, c, chips = _place()
        sibling = (x, y, 1 - c)
        me = 2 * x + y

        def half(ref, s, which):
            h = ref.shape[1] // 2
            return ref.at[s, pl.ds(which * h, h), :]

        first = []
        for a in range(n):
            for j, chip in enumerate(chips):
                k = a * 3 + j
                win = half(outs[a], me, c)
                first.append(_remote(win, win, send_sems.at[k], recv_sems.at[k], (*chip, c)))
        for cp in first:
            cp.start()
        passed = []
        for a in range(n):
            for j, chip in enumerate(chips):
                k = a * 3 + j
                win = half(outs[a], 2 * chip[0] + chip[1], c)
                _remote(win, win, send_sems.at[k], recv_sems.at[k], (*chip, c)).wait_recv()
                fw = _remote(win, win, send_sems.at[3 * n + k], recv_sems.at[3 * n + k], sibling)
                fw.start()
                passed.append(fw)
        for a in range(n):
            for j, chip in enumerate(chips):
                k = a * 3 + j
                win = half(outs[a], 2 * chip[0] + chip[1], 1 - c)
                _remote(win, win, send_sems.at[3 * n + k], recv_sems.at[3 * n + k], sibling).wait_recv()
        for cp in first + passed:
            cp.wait_send()

    return pl.pallas_call(
        body, name="all_gather_weights",
        in_specs=[ANY] * n, out_specs=tuple([ANY] * n),
        out_shape=tuple(SDS(b.shape, b.dtype) for b in bufs),
        input_output_aliases={a: a for a in range(n)},
        scratch_shapes=[pltpu.SemaphoreType.DMA((6 * n,)), pltpu.SemaphoreType.DMA((6 * n,))],
        compiler_params=_cp(has_side_effects=True))(*bufs)


def _swap_halves(grads):
    n = len(grads)

    def body(*refs):
        ins, outs = refs[:n], refs[n:2 * n]
        send_sems, recv_sems = refs[2 * n:]
        x, y, c, _ = _place()
        sibling = (x, y, 1 - c)

        def half(ref, which):
            h = ref.shape[1] // 2
            return ref.at[:, pl.ds(which * h, h), :]

        sends = [_remote(half(ins[a], 1 - c), outs[a], send_sems.at[a], recv_sems.at[a], sibling)
                 for a in range(n)]
        for cp in sends:
            cp.start()
        for cp in sends:
            cp.wait_recv()
        for cp in sends:
            cp.wait_send()

    return pl.pallas_call(
        body, name="swap_halves",
        in_specs=[ANY] * n, out_specs=tuple([ANY] * n),
        out_shape=tuple(SDS((g.shape[0], g.shape[1] // 2, g.shape[2]), g.dtype) for g in grads),
        scratch_shapes=[pltpu.SemaphoreType.DMA((n,)), pltpu.SemaphoreType.DMA((n,))],
        compiler_params=_cp(has_side_effects=True))(*grads)


def _chip_sum(g, p, ids):
    n_s, rows, cols = g.shape
    h = rows // 2
    tr = min(h, 256)
    nb = h // tr

    def body(ids_ref, g_ref, p_ref, o_ref, o16_ref):
        t = g_ref[...] + p_ref[...]
        o_ref[...] = t
        o16_ref[...] = t.astype(BF)

    out_spec = pl.BlockSpec((None, tr, cols), lambda s, i, ids_ref: (s, i, 0))
    return pl.pallas_call(
        body, name="chip_sum",
        grid_spec=pltpu.PrefetchScalarGridSpec(
            num_scalar_prefetch=1, grid=(n_s, nb),
            in_specs=[pl.BlockSpec((None, tr, cols), lambda s, i, ids_ref: (s, ids_ref[2] * nb + i, 0)),
                      pl.BlockSpec((None, tr, cols), lambda s, i, ids_ref: (s, i, 0))],
            out_specs=(out_spec, out_spec)),
        out_shape=(SDS((n_s, h, cols), g.dtype), SDS((n_s, h, cols), BF)),
        compiler_params=_cp(("parallel", "parallel"), 40))(ids, g, p)


def _exchange_shards(parts):
    n = len(parts)

    def body(*refs):
        ins, outs = refs[:n], refs[n:2 * n]
        send_sems, recv_sems = refs[2 * n:]
        x, y, c, chips = _place()
        sends = []
        for a in range(n):
            for j, chip in enumerate(chips):
                k = a * 3 + j
                sends.append(_remote(ins[a].at[2 * chip[0] + chip[1]], outs[a].at[j],
                                     send_sems.at[k], recv_sems.at[k], (*chip, c)))
        for cp in sends:
            cp.start()
        for cp in sends:
            cp.wait_recv()
        for cp in sends:
            cp.wait_send()

    return pl.pallas_call(
        body, name="exchange_shards",
        in_specs=[ANY] * n, out_specs=tuple([ANY] * n),
        out_shape=tuple(SDS((3,) + p.shape[1:], p.dtype) for p in parts),
        scratch_shapes=[pltpu.SemaphoreType.DMA((3 * n,)), pltpu.SemaphoreType.DMA((3 * n,))],
        compiler_params=_cp(has_side_effects=True))(*parts)


def _shard_sum(t, q, ids, part=0, n_parts=1, buf=None):
    _, h, cols = t.shape
    tr = min(h, 256)
    nb = h // tr

    def body(ids_ref, t_ref, q_ref, *rest):
        rest[-1][...] = ((t_ref[...] + q_ref[0].astype(F32)) + q_ref[1].astype(F32)) + q_ref[2].astype(F32)

    args, in_specs, aliases = [t, q], [
        pl.BlockSpec((None, tr, cols), lambda i, ids_ref: (ids_ref[3], i, 0)),
        pl.BlockSpec((3, tr, cols), lambda i, ids_ref: (0, i, 0))], None
    if buf is not None:
        args, in_specs, aliases = args + [buf], in_specs + [ANY], {2: 0}
    return _call(body, args, None, name="shard_sum", grid=(nb,), in_specs=in_specs,
                 out_specs=pl.BlockSpec((tr, cols), lambda i, ids_ref: ((2 * part + ids_ref[2]) * nb + i, 0)),
                 out_shape=SDS((2 * h * n_parts, cols), t.dtype), aliases=aliases, prefetch=ids,
                 compiler_params=_cp(("parallel",), 40))


def _join_halves(bufs):
    n = len(bufs)

    def body(*refs):
        outs = refs[n:2 * n]
        send_sems, recv_sems = refs[2 * n:]
        x, y, c, _ = _place()
        sibling = (x, y, 1 - c)

        def win(ref, which):
            h = ref.shape[0] // 2
            return ref.at[pl.ds(which * h, h), :]

        sends = [_remote(win(outs[a], c), win(outs[a], c), send_sems.at[a], recv_sems.at[a], sibling)
                 for a in range(n)]
        for cp in sends:
            cp.start()
        for a in range(n):
            other = win(outs[a], 1 - c)
            _remote(other, other, send_sems.at[a], recv_sems.at[a], sibling).wait_recv()
        for cp in sends:
            cp.wait_send()

    return pl.pallas_call(
        body, name="join_halves",
        in_specs=[ANY] * n, out_specs=tuple([ANY] * n),
        out_shape=tuple(SDS(b.shape, b.dtype) for b in bufs),
        input_output_aliases={a: a for a in range(n)},
        scratch_shapes=[pltpu.SemaphoreType.DMA((n,)), pltpu.SemaphoreType.DMA((n,))],
        compiler_params=_cp(has_side_effects=True))(*bufs)


def _gather_small(block, n_sum):
    rows, cols = block.shape
    n_dev = 8

    def body(x_ref, o_ref, g_ref, buf, send_sems, recv_sems, local_sem):
        x, y, c, chips = _place()
        me, sibling = (x, y, c), (x, y, 1 - c)

        def slot(px_, py_, pc_):
            return buf.at[4 * px_ + 2 * py_ + pc_]

        def copy(k, who, to, src=None):
            return _remote(slot(*who) if src is None else src, slot(*who), send_sems.at[k], recv_sems.at[k], to)

        mine = pltpu.make_async_copy(x_ref, slot(*me), local_sem)
        mine.start()
        first = [copy(0, me, sibling, src=x_ref)]
        first += [copy(1 + j, me, (*chip, c), src=x_ref) for j, chip in enumerate(chips)]
        for cp in first:
            cp.start()
        passed = [copy(4 + j, (*chip, c), sibling) for j, chip in enumerate(chips)]
        for j, chip in enumerate(chips):
            copy(1 + j, (*chip, c), me).wait_recv()
            passed[j].start()
        copy(0, sibling, me).wait_recv()
        for j, chip in enumerate(chips):
            copy(4 + j, (*chip, 1 - c), me).wait_recv()
        for cp in first + passed:
            cp.wait_send()
        mine.wait()
        acc = buf[0, :, :n_sum]
        for s in range(1, n_dev):
            acc = acc + buf[s, :, :n_sum]
        o_ref[...] = acc
        for s in range(n_dev):
            g_ref[s * rows:(s + 1) * rows, :] = buf[s, :, n_sum:]

    return pl.pallas_call(
        body, name="gather_small",
        in_specs=[pl.BlockSpec(memory_space=pltpu.VMEM)],
        out_specs=(pl.BlockSpec(memory_space=pltpu.VMEM), pl.BlockSpec(memory_space=pltpu.VMEM)),
        out_shape=(SDS((rows, n_sum), F32), SDS((n_dev * rows, cols - n_sum), F32)),
        scratch_shapes=[pltpu.VMEM((n_dev, rows, cols), F32), pltpu.SemaphoreType.DMA((7,)),
                        pltpu.SemaphoreType.DMA((7,)), pltpu.SemaphoreType.DMA],
        compiler_params=_cp(has_side_effects=True))(block)


def _adam_math(w, g, m, v):
    m = ADAM_B1 * m + (1.0 - ADAM_B1) * g
    v = ADAM_B2 * v + (1.0 - ADAM_B2) * (g * g)
    m_hat = m / (1.0 - ADAM_B1 ** ADAM_STEP)
    v_hat = v / (1.0 - ADAM_B2 ** ADAM_STEP)
    delta = -ADAM_LR * (m_hat / (jnp.sqrt(v_hat) + ADAM_EPS) + ADAM_WD * w)
    return delta, m, v


def _adamw(w, g, m, v):
    rows, cols = w.shape
    tr = min(rows, 256 if cols <= 2048 else 128)

    def body(w_ref, g_ref, m_ref, v_ref, go_ref, d_ref, nm_ref, nv_ref):
        g = g_ref[...]
        go_ref[...] = g
        d_ref[...], nm_ref[...], nv_ref[...] = _adam_math(w_ref[...], g, m_ref[...], v_ref[...])

    spec = pl.BlockSpec((tr, cols), lambda i: (i, 0))
    return pl.pallas_call(
        body, name="adamw", grid=(rows // tr,), in_specs=[spec] * 4, out_specs=(spec,) * 4,
        out_shape=(SDS(w.shape, F32),) * 4, compiler_params=_cp(("parallel",), 40))(w, g, m, v)


def _adamw_small(w, g, m, v):
    def body(w_ref, g_ref, m_ref, v_ref, go_ref, d_ref, nm_ref, nv_ref):
        w = w_ref[...]
        g = g_ref[...]
        sub = lax.broadcasted_iota(jnp.int32, w.shape, 0)
        lane = lax.broadcasted_iota(jnp.int32, w.shape, 1)
        is_ret = jnp.logical_and(sub == 5, lane < 2 * RET_HEADS)
        u = jnp.exp(jnp.where(is_ret, w, -1.0) * jnp.log(2.0))
        g = jnp.where(is_ret, g * (-u * jnp.log(2.0) / (1.0 - u)), g)
        go_ref[...] = g
        d_ref[...], nm_ref[...], nv_ref[...] = _adam_math(w, g, m_ref[...], v_ref[...])

    return pl.pallas_call(body, name="adamw_small", out_shape=(SDS(w.shape, F32),) * 4)(w, g, m, v)


def _rope_tables(seq, n_samp, n_ctx_rows):
    rows = seq // GRID_W
    row = jnp.repeat(jnp.arange(rows, dtype=F32), GRID_W)
    col = jnp.tile(jnp.arange(GRID_W, dtype=F32), rows)
    half = ATT_HEAD_DIM // 2
    freqs = ROPE_THETA ** (-jnp.arange(0, half, 2, dtype=F32) / half)
    ang = jnp.concatenate([row[:, None] * freqs, col[:, None] * freqs], axis=-1)
    cos, sin = jnp.cos(ang), jnp.sin(ang)
    cos_f = jnp.repeat(cos, 2, axis=1)
    sin_s = jnp.stack([-sin, sin], axis=-1).reshape(seq, ATT_HEAD_DIM)
    cos_all = jnp.concatenate([jnp.tile(cos_f, (n_samp, 1)), jnp.ones((n_ctx_rows, ATT_HEAD_DIM), F32)], axis=0)
    sin_all = jnp.concatenate([jnp.tile(sin_s, (n_samp, 1)), jnp.zeros((n_ctx_rows, ATT_HEAD_DIM), F32)], axis=0)
    return cos_all, sin_all


def _pack_small(c_ctx, norm_w, b_ada, ret, qn, kn):
    d = D_MODEL
    row5 = jnp.concatenate([ret.reshape(-1), jnp.zeros((128 - 2 * RET_HEADS,), F32), qn.reshape(-1), kn.reshape(-1),
                            jnp.zeros((d - 384,), F32)])
    return jnp.concatenate([c_ctx.reshape(1, d), norm_w.reshape(1, d), b_ada.reshape(3, d), row5.reshape(1, d),
                            jnp.zeros((2, d), F32)], axis=0)


def _unpack_small(p):
    d = D_MODEL
    return (p[0], p[1:2], p[2:5].reshape(1, 3 * d), p[5, :2 * RET_HEADS].reshape(1, 2, RET_HEADS),
            p[5:6, 128:256], p[5:6, 256:384])


def _step(x, c, ctx, c_ctx, norm_w, b_ada, ret_log2_decay, q_norm_w, k_norm_w, loss_target, weights, ids, dist):
    n_samp, seq, d = x.shape
    lc = ctx.shape[1]
    t_lat, t_ctx = n_samp * seq, n_samp * lc
    assert seq % TM == 0 and t_ctx == TM and t_lat % lc == 0 and seq % GRID_W == 0
    tps = seq // TM

    x_lat = x.reshape(t_lat, d)
    x_ctx = ctx.reshape(t_ctx, d)
    cvec8 = jnp.concatenate([c, c_ctx.reshape(1, d), jnp.zeros((8 - n_samp - 1, d), F32)], axis=0)
    lg = jnp.log1p(-jnp.exp2(ret_log2_decay.reshape(2, RET_HEADS)))
    cos_all, sin_all = _rope_tables(seq, n_samp, t_ctx)

    w_ada_b, w_in_b, w_or_b, w_oa_b, w_out_b = weights
    w_ada_g = _run_comm(_ag_comm((w_ada_b,), (0, 1, 2)))[0] if dist else w_ada_b
    mod8 = _adaln_fwd(cvec8, w_ada_g, b_ada)
    mod3 = mod8[:n_samp + 1]
    shift3 = mod3[:, None, 0:d]
    scale3 = mod3[:, None, d:2 * d]
    gate3 = mod3[:, None, 2 * d:3 * d]

    hx, hxt = _norm_fwd(x_lat, x_ctx, norm_w, scale3, shift3, tps, n_samp)
    if dist:
        px, (w_in_1,) = _in_proj(hx, w_in_b, ids, 0, 1, comm=_ag_comm((w_in_b,), (0, 1), arg_index=1))
        px, (w_in_g,) = _in_proj(hx, w_in_1, ids, 1, 2, px=px, comm=_ag_comm((w_in_1,), (2,), arg_index=1))
        px = _in_proj(hx, w_in_g, ids, 3, 1, px=px)
    else:
        w_in_g = w_in_b
        px = _in_proj(hx, w_in_g, ids, 0, N_SHARD)

    states0 = _ctx_state_fwd(px, lg, n_samp, t_lat, lc)
    if dist:
        (o_f, o_b, saved), w_o = _ret_fwd(px, states0, lg, n_samp, seq,
                                          comm=_ag_comm((w_or_b, w_oa_b, w_out_b), (0, 1, 2)))
    else:
        (o_f, o_b, saved), w_o = _ret_fwd(px, states0, lg, n_samp, seq), (w_or_b, w_oa_b, w_out_b)
    w_o_ret, w_o_att, w_out = (w.reshape(-1, d) for w in w_o)

    qn = _att_prep_q(px, cos_all, sin_all, q_norm_w, t_lat)
    kn, vn = _att_prep_kv(px, cos_all, sin_all, k_norm_w)
    o_att, lse = _att_fwd(qn, kn, vn, n_samp, seq, lc)

    (gx_res, do, do_att, dpx, loss8, dgate, g_w_o_ret, g_w_o_att, g_w_out) = _merge(
        x_lat, loss_target.reshape(t_lat, d), o_f, o_b, o_att, px, gate3, w_o_ret, w_o_att, w_out, tps)

    g_a = [g.reshape(N_SHARD, -1, d) for g in (g_w_o_ret, g_w_o_att, g_w_out)]
    res = _att_bwd(dpx, qn, kn, vn, px, o_att, lse, do_att, cos_all, sin_all, q_norm_w, n_samp, seq, lc,
                   comm=_swap_comm(g_a) if dist else None)
    (dpx, dkl, dkc, dvl, dvc, gqw), sib_a = res if dist else (res, None)
    dpx, gkw = _att_kv_bwd(dpx, dkl, dkc, dvl, dvc, px, cos_all, sin_all, k_norm_w)
    if dist:
        t_a = [_chip_sum(g, p, ids) for g, p in zip(g_a, sib_a)]

    res = _ret_bwd(px, do, saved, lg, n_samp, seq,
                   comm=_exchange_comm([t16 for _, t16 in t_a]) if dist else None)
    (dqf, dkf, dvf, dqb, dkb, dvb, dstates, dlg_lat), q_a = res if dist else (res, None)
    if dist:
        r_a = [_shard_sum(t, q, ids) for (t, _), q in zip(t_a, q_a)]
    dpx = _combine_into(dpx, dqf, dqb, C_RQ, 1.0)
    dpx = _combine_into(dpx, dkf, dkb, C_RK, RET_DK ** -0.5)
    dpx = _combine_into(dpx, dvf, dvb, C_RV, 1.0)
    dpx, dlg_ctx = _ctx_state_bwd(dpx, px, dstates, lg, n_samp, t_lat, lc)
    dpx = _zero_ctx_tail(dpx, t_lat)

    n_tiles = dpx.shape[0] // _big_rows(dpx.shape[0])
    if dist:
        g_b = _gw_in(hxt, dpx, 0, 1)
        dhx, (sib_b, *r_a) = _dhx(dpx, w_in_g, 0, 1, comm=_join_comms(_swap_comm([g_b]), _join_comm(r_a)))
        t_b, t16_b = _chip_sum(g_b, sib_b, ids)
        dhx, (q_b,) = _dhx(dpx, w_in_g, 1, n_tiles - 1, dhx=dhx, comm=_exchange_comm([t16_b]))
        (grad_x, dshift, dscale, g_norm_w), (r_b,) = _norm_bwd(
            x_lat, x_ctx, dhx, gx_res, norm_w, scale3, tps, n_samp,
            comm=_join_comm([_shard_sum(t_b, q_b, ids)]))
    else:
        g_w_in = _gw_in(hxt, dpx, 0, 1)
        dhx = _dhx(dpx, w_in_g, 0, n_tiles)
        grad_x, dshift, dscale, g_norm_w = _norm_bwd(x_lat, x_ctx, dhx, gx_res, norm_w, scale3, tps, n_samp)

    dgate_all = jnp.concatenate([dgate, jnp.zeros((1, 1, d), F32)], axis=0)
    dmod3 = jnp.concatenate([dshift, dscale, dgate_all], axis=2).reshape(n_samp + 1, 3 * d)
    dmod8 = jnp.concatenate([dmod3, jnp.zeros((8 - n_samp - 1, 3 * d), F32)], axis=0)
    g_lg = (jnp.sum(dlg_lat[:, :, 0], axis=0).reshape(2, RET_HEADS)
            + jnp.stack([jnp.sum(dlg_ctx[:, :, 0, 0], axis=0), jnp.sum(dlg_ctx[:, :, 1, 0], axis=0)], axis=0))
    g_qw = jnp.sum(gqw, axis=(0, 1, 2))
    zero = jnp.zeros((d,), F32)
    if not dist:
        g_w_ada, g_b_ada, dc8 = _adaln_bwd(cvec8, dmod8, w_ada_g)
        small = _pack_small(dc8[n_samp], g_norm_w, g_b_ada, g_lg, g_qw, gkw)
        return (loss8[0, 0], grad_x.reshape(n_samp, seq, d),
                (g_w_ada, g_w_in, g_w_o_ret, g_w_o_att, g_w_out), small)

    local = _pack_small(zero, g_norm_w, jnp.zeros((3 * d,), F32), g_lg, g_qw, gkw).at[6, 0].set(loss8[0, 0])
    small_sum, gathered = _gather_small(jnp.concatenate([local, cvec8, dmod8], axis=1), d)
    g_w_ada, g_b_ada, dc_all = _adaln_bwd(gathered[:, :d], gathered[:, d:], w_ada_g)
    dc_ctx = jnp.sum(dc_all.reshape(-1, 8, d)[:, n_samp], axis=0)
    small = small_sum + _pack_small(dc_ctx, zero, g_b_ada, jnp.zeros((2, RET_HEADS), F32), zero[:128], zero[:128])
    r_c = lax.dynamic_index_in_dim(g_w_ada, ids[3], 0, keepdims=False)
    return small[6, 0], grad_x.reshape(n_samp, seq, d), (r_c, r_b, *r_a), small


def kernel(x, c, ctx, c_ctx, norm_w, w_ada, b_ada, w_in, ret_log2_decay, q_norm_w, k_norm_w, w_o_ret, w_o_att, w_out, loss_target, m_c_ctx, m_norm_w, m_w_ada, m_b_ada, m_w_in, m_ret_log2_decay, m_q_norm_w, m_k_norm_w, m_w_o_ret, m_w_o_att, m_w_out, v_c_ctx, v_norm_w, v_w_ada, v_b_ada, v_w_in, v_ret_log2_decay, v_q_norm_w, v_k_norm_w, v_w_o_ret, v_w_o_att, v_w_out):
    big_w = (w_ada[0], w_in[0], w_o_ret[0], w_o_att[0], w_out[0])
    big_m = (m_w_ada[0], m_w_in[0], m_w_o_ret[0], m_w_o_att[0], m_w_out[0])
    big_v = (v_w_ada[0], v_w_in[0], v_w_o_ret[0], v_w_o_att[0], v_w_out[0])

    ids = _place_ids()
    loss, grad_x, big_grad, small_grad_in = _step(
        x, c, ctx, c_ctx, norm_w[0:1], b_ada[0:1], ret_log2_decay[0], q_norm_w[0:1], k_norm_w[0:1], loss_target,
        tuple(_cast_place(w, ids) for w in big_w), ids, True)
    small_w = _pack_small(c_ctx, norm_w, b_ada, ret_log2_decay, q_norm_w, k_norm_w)
    small_m = _pack_small(m_c_ctx, m_norm_w, m_b_ada, m_ret_log2_decay, m_q_norm_w, m_k_norm_w)
    small_v = _pack_small(v_c_ctx, v_norm_w, v_b_ada, v_ret_log2_decay, v_q_norm_w, v_k_norm_w)
    small_grad, small_delta, small_nm, small_nv = _adamw_small(small_w, small_grad_in, small_m, small_v)

    big_g, big_delta, big_nm, big_nv = [], [], [], []
    for w, g, m, v in zip(big_w, big_grad, big_m, big_v):
        go, dlt, nm, nv = _adamw(w, g, m, v)
        big_g.append(go[None])
        big_delta.append(dlt[None])
        big_nm.append(nm[None])
        big_nv.append(nv[None])
    big_grad = big_g

    def order(small_packed, big):
        s = _unpack_small(small_packed)
        return (s[0], s[1], big[0], s[2], big[1], s[3], s[4], s[5], big[2], big[3], big[4])

    return (loss, grad_x, *order(small_grad, big_grad), *order(small_delta, big_delta),
            *order(small_nm, big_nm), *order(small_nv, big_nv))
```

```python
import functools
from typing import NamedTuple

import jax
import jax.numpy as jnp
from jax import lax
from jax.experimental import pallas as pl
from jax.experimental.pallas import tpu as pltpu

F32 = jnp.float32
BF = jnp.bfloat16
SDS = jax.ShapeDtypeStruct
MESH = pl.DeviceIdType.MESH
ANY = pl.BlockSpec(memory_space=pl.ANY)
SMEM = pl.BlockSpec(memory_space=pltpu.SMEM)

D_MODEL = 1024
GRID_W = 64
RET_HEADS = 4
RET_DK = 256
RET_DV = 512
RET_CHUNK = 128
ATT_HEADS = 8
ATT_KV_HEADS = 2
ATT_REP = ATT_HEADS // ATT_KV_HEADS
ATT_HEAD_DIM = 128
ROPE_THETA = 10000.0
NORM_EPS = 1e-6
IN_COLS = 10752
KV_COLS = 3584
C_RK, C_RV, C_AK, C_AV, C_RQ, C_RG, C_AQ, C_AG, C_MR, C_MA = 0, 1024, 3072, 3328, 3584, 4608, 6656, 7680, 8704, 9728
N_SHARD = 4
ADA_W = 3 * D_MODEL // N_SHARD
IN_W = IN_COLS // N_SHARD
IN_BLK = IN_W
BPS = IN_W // IN_BLK
N_IN_BLK = IN_COLS // IN_BLK
TM = 512
ATT_TQ = 512
ADAM_LR, ADAM_B1, ADAM_B2, ADAM_EPS, ADAM_WD, ADAM_STEP = 0.001, 0.9, 0.999, 1e-08, 0.01, 10
MIB = 1024 * 1024


def _cp(sem=None, vmem_mb=None, **kw):
    if sem is not None:
        kw["dimension_semantics"] = sem
    if vmem_mb is not None:
        kw["vmem_limit_bytes"] = vmem_mb * MIB
    return pltpu.CompilerParams(**kw)


def _dot(a, b, ca=1, cb=0):
    return lax.dot_general(a.astype(BF), b.astype(BF), (((ca,), (cb,)), ((), ())), preferred_element_type=F32)


def _sigmoid(x):
    return 1.0 / (1.0 + jnp.exp(-x))


def _sum_all(x):
    return jnp.sum(jnp.sum(x, axis=1, keepdims=True), axis=0, keepdims=True)


def _swap_pairs(x):
    ax = x.ndim - 1
    lane = lax.broadcasted_iota(jnp.int32, x.shape, ax)
    nxt = pltpu.roll(x, x.shape[ax] - 1, ax)
    prv = pltpu.roll(x, 1, ax)
    return jnp.where(lane % 2 == 0, nxt, prv)


def _rms(x):
    return lax.rsqrt(jnp.mean(x * x, axis=-1, keepdims=True) + NORM_EPS)


def _rms_bwd(dxh, xh, r):
    return r * (dxh - xh * jnp.mean(dxh * xh, axis=-1, keepdims=True))


class _Comm(NamedTuple):
    name: str
    ins: tuple
    out_shapes: tuple
    aliases: dict
    n_sems: int
    phases: tuple
    arg_aliases: tuple = ()


def _join_comms(*comms):
    comms = [cm for cm in comms if cm is not None]
    if len(comms) <= 1:
        return comms[0] if comms else None
    offs, i_off, o_off, s_off = [], 0, 0, 0
    for cm in comms:
        offs.append((i_off, o_off, s_off))
        i_off, o_off, s_off = i_off + len(cm.ins), o_off + len(cm.out_shapes), s_off + cm.n_sems

    def phase(k):
        def run(ins, outs, ssem, rsem, base):
            sends, recvs = [], []
            for cm, (io, oo, so) in zip(comms, offs):
                if k < len(cm.phases):
                    s, r = cm.phases[k](ins[io:io + len(cm.ins)], outs[oo:oo + len(cm.out_shapes)], ssem, rsem,
                                        base + so)
                    sends += s
                    recvs += r
            return sends, recvs
        return run

    aliases, arg_aliases = {}, ()
    for cm, (io, oo, _) in zip(comms, offs):
        aliases.update({io + a: oo + b for a, b in cm.aliases.items()})
        arg_aliases += tuple((a, oo + b) for a, b in cm.arg_aliases)
    return _Comm("+".join(cm.name for cm in comms), sum((cm.ins for cm in comms), ()),
                 sum((cm.out_shapes for cm in comms), ()), aliases, s_off,
                 tuple(phase(k) for k in range(max(len(cm.phases) for cm in comms))), arg_aliases)


def _run_phases(comm, cins, couts, ssem, rsem, first_started):
    for k, phase in enumerate(comm.phases):
        sends, recvs = phase(cins, couts, ssem, rsem, 0)
        if k > 0 or not first_started:
            for cp in sends:
                cp.start()
        for cp in recvs:
            cp.wait_recv()
        for cp in sends:
            cp.wait_send()


def _call(body, args, comm=None, *, name, grid, in_specs, out_specs, out_shape, scratch_shapes=(),
          compiler_params, aliases=None, prefetch=None):
    single = not isinstance(out_shape, (tuple, list))
    out_specs_t = (out_specs,) if single else tuple(out_specs)
    out_shape_t = (out_shape,) if single else tuple(out_shape)
    n_pre = 0 if prefetch is None else 1
    n_in, n_out, n_sc = len(in_specs), len(out_specs_t), len(scratch_shapes)
    io_alias = {n_pre + a: b for a, b in (aliases or {}).items()}
    if comm is None:
        kernel_body, cin, cout, csems = body, [], [], []
    else:
        n_ci, n_co = len(comm.ins), len(comm.out_shapes)
        cin, cout = [ANY] * n_ci, [ANY] * n_co
        csems = [pltpu.SemaphoreType.DMA((comm.n_sems,)), pltpu.SemaphoreType.DMA((comm.n_sems,))]
        io_alias.update({n_pre + n_in + a: n_out + b for a, b in comm.aliases.items()})
        io_alias.update({n_pre + a: n_out + b for a, b in comm.arg_aliases})

        def kernel_body(*refs):
            pre, refs = refs[:n_pre], refs[n_pre:]
            ins, cins = refs[:n_in], refs[n_in:n_in + n_ci]
            outs = refs[n_in + n_ci:n_in + n_ci + n_out]
            couts = refs[n_in + n_ci + n_out:n_in + n_ci + n_out + n_co]
            scratch = refs[n_in + n_ci + n_out + n_co:n_in + n_ci + n_out + n_co + n_sc]
            ssem, rsem = refs[-2:]
            first = functools.reduce(jnp.logical_and, [pl.program_id(k) == 0 for k in range(len(grid))])
            last = functools.reduce(jnp.logical_and, [pl.program_id(k) == grid[k] - 1 for k in range(len(grid))])

            @pl.when(first)
            def _():
                for cp in comm.phases[0](cins, couts, ssem, rsem, 0)[0]:
                    cp.start()

            body(*pre, *ins, *outs, *scratch)

            @pl.when(last)
            def _():
                _run_phases(comm, cins, couts, ssem, rsem, True)

        name = name + "+" + comm.name

    all_in, all_out = list(in_specs) + cin, out_specs_t + tuple(cout)
    shapes = out_shape_t + (tuple(comm.out_shapes) if comm is not None else ())
    scratch = list(scratch_shapes) + csems
    if prefetch is None:
        res = pl.pallas_call(kernel_body, name=name, grid=grid, in_specs=all_in, out_specs=all_out, out_shape=shapes,
                             scratch_shapes=scratch, input_output_aliases=io_alias,
                             compiler_params=compiler_params)(*args, *(comm.ins if comm is not None else ()))
    else:
        res = pl.pallas_call(
            kernel_body, name=name, out_shape=shapes, input_output_aliases=io_alias, compiler_params=compiler_params,
            grid_spec=pltpu.PrefetchScalarGridSpec(num_scalar_prefetch=1, grid=grid, in_specs=all_in,
                                                   out_specs=all_out, scratch_shapes=scratch))(
                                                       prefetch, *args, *(comm.ins if comm is not None else ()))
    own = res[0] if single else tuple(res[:n_out])
    return own if comm is None else (own, tuple(res[n_out:]))


def _run_comm(comm):
    n_ci, n_co = len(comm.ins), len(comm.out_shapes)

    def body(*refs):
        _run_phases(comm, refs[:n_ci], refs[n_ci:n_ci + n_co], refs[-2], refs[-1], False)

    return pl.pallas_call(
        body, name=comm.name, in_specs=[ANY] * n_ci, out_specs=tuple([ANY] * n_co), out_shape=tuple(comm.out_shapes),
        input_output_aliases=dict(comm.aliases),
        scratch_shapes=[pltpu.SemaphoreType.DMA((comm.n_sems,)), pltpu.SemaphoreType.DMA((comm.n_sems,))],
        compiler_params=_cp(has_side_effects=True))(*comm.ins)


def _adaln_fwd(cvec8, w_ada_g, b_ada):
    def body(c_ref, w_ref, b_ref, o_ref):
        cv = c_ref[...]
        sc = (cv * _sigmoid(cv)).astype(BF)
        for s in range(N_SHARD):
            cols = slice(s * ADA_W, (s + 1) * ADA_W)
            o_ref[:, cols] = jnp.dot(sc, w_ref[s], preferred_element_type=F32) + b_ref[:, cols]

    return pl.pallas_call(body, out_shape=SDS((8, 3 * D_MODEL), F32), name="adaln_fwd",
                          compiler_params=_cp(vmem_mb=32))(cvec8, w_ada_g, b_ada)


def _adaln_bwd(cvec, dmod, w_ada_g):
    def body(c_ref, d_ref, w_ref, gw_ref, gb_ref, dc_ref):
        cv = c_ref[...]
        sg = _sigmoid(cv)
        sc = cv * sg
        dm = d_ref[...]
        gb_ref[...] = jnp.sum(dm, axis=0, keepdims=True)
        dsc = jnp.zeros(cv.shape, F32)
        for s in range(N_SHARD):
            cols = slice(s * ADA_W, (s + 1) * ADA_W)
            gw_ref[s] = _dot(sc, dm[:, cols], 0, 0)
            dsc = dsc + _dot(dm[:, cols], w_ref[s], 1, 1)
        dc_ref[...] = dsc * (sg * (1.0 + cv * (1.0 - sg)))

    return pl.pallas_call(
        body, name="adaln_bwd",
        out_shape=(SDS((N_SHARD, D_MODEL, ADA_W), F32), SDS((1, 3 * D_MODEL), F32), SDS(cvec.shape, F32)),
        compiler_params=_cp(vmem_mb=48))(cvec, dmod, w_ada_g)


def _big_rows(rows):
    return 1536 if rows % 1536 == 0 else TM


def _norm_fwd(x_lat, x_ctx, norm_w, scale3, shift3, tiles_per_sample, n_samp):
    n_lat = x_lat.shape[0] // TM
    rows = x_lat.shape[0] + x_ctx.shape[0]

    def samp(i):
        return jnp.minimum(i // tiles_per_sample, n_samp)

    def body(x_ref, c_ref, nw_ref, sc_ref, sh_ref, hx_ref, hxt_ref):
        x = jnp.where(pl.program_id(0) < n_lat, x_ref[...], c_ref[...])
        h = x * _rms(x) * nw_ref[...] * (1.0 + sc_ref[...]) + sh_ref[...]
        hx_ref[...] = h.astype(BF)
        hxt_ref[...] = h.T.astype(BF)

    return pl.pallas_call(
        body, name="norm_fwd", grid=(rows // TM,),
        in_specs=[pl.BlockSpec((TM, D_MODEL), lambda i: (jnp.minimum(i, n_lat - 1), 0)),
                  pl.BlockSpec((TM, D_MODEL), lambda i: (jnp.maximum(i - n_lat, 0), 0)),
                  pl.BlockSpec((1, D_MODEL), lambda i: (0, 0)),
                  pl.BlockSpec((None, 1, D_MODEL), lambda i: (samp(i), 0, 0)),
                  pl.BlockSpec((None, 1, D_MODEL), lambda i: (samp(i), 0, 0))],
        out_specs=(pl.BlockSpec((TM, D_MODEL), lambda i: (i, 0)),
                   pl.BlockSpec((D_MODEL, TM), lambda i: (0, i))),
        out_shape=(SDS((rows, D_MODEL), BF), SDS((D_MODEL, rows), BF)),
        compiler_params=_cp(("parallel",), 40))(x_lat, x_ctx, norm_w, scale3, shift3)


def _in_proj(hx, w_in_g, ids, first, count, px=None, comm=None):
    rows = hx.shape[0]
    tb = _big_rows(rows)

    def shard(j, ids_ref):
        return ids_ref[4 + first + j // BPS]

    def body(ids_ref, h_ref, w_ref, *rest):
        px_ref = rest[-1]
        px_ref[...] = jnp.dot(h_ref[...], w_ref[...], preferred_element_type=F32).astype(BF)

    args, in_specs, aliases = [hx, w_in_g], [
        pl.BlockSpec((tb, D_MODEL), lambda j, i, ids_ref: (i, 0)),
        pl.BlockSpec((None, D_MODEL, IN_BLK), lambda j, i, ids_ref: (shard(j, ids_ref), 0, j % BPS))], None
    if px is not None:
        args, in_specs, aliases = args + [px], in_specs + [ANY], {2: 0}
    return _call(body, args, comm, name="in_proj", grid=(BPS * count, rows // tb), in_specs=in_specs,
                 out_specs=pl.BlockSpec((tb, IN_BLK),
                                        lambda j, i, ids_ref: (i, BPS * shard(j, ids_ref) + j % BPS)),
                 out_shape=SDS((rows, IN_COLS), BF), aliases=aliases, prefetch=ids,
                 compiler_params=_cp(("arbitrary", "arbitrary"), 56))


def _norm_bwd(x_lat, x_ctx, dhx, gx_res, norm_w, scale3, tiles_per_sample, n_samp, comm=None):
    rows = x_lat.shape[0] + x_ctx.shape[0]
    n_lat = tiles_per_sample * n_samp

    def samp(i):
        return jnp.minimum(i // tiles_per_sample, n_samp)

    def lat(i):
        return jnp.minimum(i, n_lat - 1)

    def body(x_ref, c_ref, dh_ref, gr_ref, nw_ref, sc_ref, gx_ref, dsh_ref, dsc_ref, dnw_ref):
        i = pl.program_id(0)
        x = jnp.where(i < n_lat, x_ref[...], c_ref[...])
        r = _rms(x)
        xh = x * r
        nw = nw_ref[...]
        dh = dh_ref[...]
        first = jnp.logical_or(i % tiles_per_sample == 0, i >= n_lat)

        @pl.when(first)
        def _():
            dsh_ref[...] = jnp.zeros_like(dsh_ref)
            dsc_ref[...] = jnp.zeros_like(dsc_ref)

        @pl.when(i == 0)
        def _():
            dnw_ref[...] = jnp.zeros_like(dnw_ref)

        dsh_ref[...] += jnp.sum(dh, axis=0, keepdims=True)
        dsc_ref[...] += jnp.sum(dh * (xh * nw), axis=0, keepdims=True)
        du = dh * (1.0 + sc_ref[...])
        dnw_ref[...] += jnp.sum(du * xh, axis=0, keepdims=True)

        @pl.when(i < n_lat)
        def _():
            gx_ref[...] = gr_ref[...] + _rms_bwd(du * nw, xh, r)

    return _call(
        body, [x_lat, x_ctx, dhx, gx_res, norm_w, scale3], comm, name="norm_bwd", grid=(rows // TM,),
        in_specs=[pl.BlockSpec((TM, D_MODEL), lambda i: (lat(i), 0)),
                  pl.BlockSpec((TM, D_MODEL), lambda i: (jnp.maximum(i - n_lat, 0), 0)),
                  pl.BlockSpec((TM, D_MODEL), lambda i: (i, 0)),
                  pl.BlockSpec((TM, D_MODEL), lambda i: (lat(i), 0)),
                  pl.BlockSpec((1, D_MODEL), lambda i: (0, 0)),
                  pl.BlockSpec((None, 1, D_MODEL), lambda i: (samp(i), 0, 0))],
        out_specs=(pl.BlockSpec((TM, D_MODEL), lambda i: (lat(i), 0)),
                   pl.BlockSpec((None, 1, D_MODEL), lambda i: (samp(i), 0, 0)),
                   pl.BlockSpec((None, 1, D_MODEL), lambda i: (samp(i), 0, 0)),
                   pl.BlockSpec((1, D_MODEL), lambda i: (0, 0))),
        out_shape=(SDS((n_lat * TM, D_MODEL), F32), SDS((n_samp + 1, 1, D_MODEL), F32),
                   SDS((n_samp + 1, 1, D_MODEL), F32), SDS((1, D_MODEL), F32)),
        compiler_params=_cp(("arbitrary",), 40))


def _gw_in(hxt, dpx_all, part, n_parts, comm=None):
    rows = dpx_all.shape[0]
    tb = _big_rows(rows)
    dp = D_MODEL // n_parts

    def body(h_ref, d_ref, o_ref):
        @pl.when(pl.program_id(1) == 0)
        def _():
            o_ref[...] = jnp.zeros_like(o_ref)

        o_ref[...] += jnp.dot(h_ref[...], d_ref[...], preferred_element_type=F32)

    return _call(body, [hxt, dpx_all], comm, name="gw_in", grid=(N_IN_BLK, rows // tb),
                 in_specs=[pl.BlockSpec((dp, tb), lambda j, i: (part, i)),
                           pl.BlockSpec((tb, IN_BLK), lambda j, i: (i, j))],
                 out_specs=pl.BlockSpec((None, dp, IN_BLK), lambda j, i: (j // BPS, 0, j % BPS)),
                 out_shape=SDS((N_SHARD, dp, IN_W), F32),
                 compiler_params=_cp(("arbitrary", "arbitrary"), 56))


def _dhx(dpx_all, w_in_g, tile0, n_tiles, dhx=None, comm=None):
    rows = dpx_all.shape[0]
    tb = _big_rows(rows)

    def body(d_ref, w_ref, *rest):
        o_ref = rest[-1]

        @pl.when(pl.program_id(1) == 0)
        def _():
            o_ref[...] = jnp.zeros_like(o_ref)

        o_ref[...] += lax.dot_general(d_ref[...], w_ref[...], (((1,), (1,)), ((), ())), preferred_element_type=F32)

    args, in_specs, aliases = [dpx_all, w_in_g], [
        pl.BlockSpec((tb, IN_BLK), lambda i, j: (tile0 + i, j)),
        pl.BlockSpec((None, D_MODEL, IN_BLK), lambda i, j: (j // BPS, 0, j % BPS))], None
    if dhx is not None:
        args, in_specs, aliases = args + [dhx], in_specs + [ANY], {2: 0}
    return _call(body, args, comm, name="dhx", grid=(n_tiles, N_IN_BLK), in_specs=in_specs,
                 out_specs=pl.BlockSpec((tb, D_MODEL), lambda i, j: (tile0 + i, 0)),
                 out_shape=SDS((rows, D_MODEL), F32), aliases=aliases,
                 compiler_params=_cp(("arbitrary", "arbitrary"), 56))


def _decays(lgv, d):
    c = RET_CHUNK
    ii = lax.broadcasted_iota(jnp.int32, (c, 1), 0).astype(F32)
    jj = lax.broadcasted_iota(jnp.int32, (1, c), 1).astype(F32)
    a_i = jnp.where(d == 0, ii, c - 1.0 - ii)
    a_j = jnp.where(d == 0, jj, c - 1.0 - jj)
    rel = a_i - a_j
    mask = jnp.where(rel >= 0, jnp.exp(lgv * jnp.maximum(rel, 0.0)), 0.0)
    qd = jnp.exp(lgv * (a_i + 1.0))
    kd = jnp.exp(lgv * (c - 1.0 - a_i))
    gc = jnp.exp(jnp.full((1, 1), lgv * c, F32))
    return a_i, rel, mask, qd, kd, gc


def _ctx_state_fwd(px, lg, n_samp, t_lat, lc):
    rb = t_lat // lc

    def body(lg_ref, k_ref, v_ref, o_ref):
        h = pl.program_id(1)
        k = k_ref[...].astype(F32) * (RET_DK ** -0.5)
        v = v_ref[...]
        pos = lax.broadcasted_iota(jnp.int32, (lc, 1), 0).astype(F32)
        o_ref[0] = _dot(k * jnp.exp(lg_ref[0, h] * (lc - 1.0 - pos)), v, 0, 0)
        o_ref[1] = _dot(k * jnp.exp(lg_ref[1, h] * pos), v, 0, 0)

    return pl.pallas_call(
        body, name="ctx_state_fwd", grid=(n_samp, RET_HEADS),
        in_specs=[SMEM,
                  pl.BlockSpec((lc, RET_DK), lambda b, h: (rb + b, C_RK // RET_DK + h)),
                  pl.BlockSpec((lc, RET_DV), lambda b, h: (rb + b, C_RV // RET_DV + h))],
        out_specs=pl.BlockSpec((None, 2, None, RET_DK, RET_DV), lambda b, h: (b, 0, h, 0, 0)),
        out_shape=SDS((n_samp, 2, RET_HEADS, RET_DK, RET_DV), F32),
        compiler_params=_cp(("parallel", "parallel")))(lg, px, px)


def _ctx_state_bwd(dpx, px, dstates, lg, n_samp, t_lat, lc):
    rb = t_lat // lc
    kspec = pl.BlockSpec((lc, RET_DK), lambda b, h: (rb + b, C_RK // RET_DK + h))
    vspec = pl.BlockSpec((lc, RET_DV), lambda b, h: (rb + b, C_RV // RET_DV + h))
    sspec = pl.BlockSpec((None, 2, None, RET_DK, RET_DV), lambda b, h: (b, 0, h, 0, 0))

    def weights(lg_ref, h):
        pos = lax.broadcasted_iota(jnp.int32, (lc, 1), 0).astype(F32)
        e_f = lc - 1.0 - pos
        return pos, e_f, jnp.exp(lg_ref[0, h] * e_f), jnp.exp(lg_ref[1, h] * pos)

    def k_body(lg_ref, dpx_hbm, k_ref, v_ref, ds_ref, dk_ref, dlg_ref):
        pos, e_f, w_f, w_b = weights(lg_ref, pl.program_id(1))
        k = k_ref[...].astype(F32) * (RET_DK ** -0.5)
        y_f = _dot(v_ref[...], ds_ref[0], 1, 1) * w_f
        y_b = _dot(v_ref[...], ds_ref[1], 1, 1) * w_b
        dk_ref[...] = ((y_f + y_b) * (RET_DK ** -0.5)).astype(BF)
        t_f = _sum_all(e_f * k * y_f)
        t_b = _sum_all(pos * k * y_b)
        sub = lax.broadcasted_iota(jnp.int32, (8, 128), 0)
        dlg_ref[...] = jnp.where(sub == 0, t_f, jnp.where(sub == 1, t_b, 0.0))

    def v_body(lg_ref, dpx_hbm, k_ref, ds_ref, dv_ref):
        _, _, w_f, w_b = weights(lg_ref, pl.program_id(1))
        k = k_ref[...].astype(F32) * (RET_DK ** -0.5)
        dv_ref[...] = (_dot(k * w_f, ds_ref[0]) + _dot(k * w_b, ds_ref[1])).astype(BF)

    dpx, dlg = pl.pallas_call(
        k_body, name="ctx_state_bwd_k", grid=(n_samp, RET_HEADS), input_output_aliases={1: 0},
        in_specs=[SMEM, ANY, kspec, vspec, sspec],
        out_specs=(kspec, pl.BlockSpec((None, None, 8, 128), lambda b, h: (b, h, 0, 0))),
        out_shape=(SDS(dpx.shape, dpx.dtype), SDS((n_samp, RET_HEADS, 8, 128), F32)),
        compiler_params=_cp(("parallel", "parallel")))(lg, dpx, px, px, dstates)
    dpx = pl.pallas_call(
        v_body, name="ctx_state_bwd_v", grid=(n_samp, RET_HEADS), input_output_aliases={1: 0},
        in_specs=[SMEM, ANY, kspec, sspec], out_specs=vspec, out_shape=SDS(dpx.shape, dpx.dtype),
        compiler_params=_cp(("parallel", "parallel")))(lg, dpx, px, dstates)
    return dpx, dlg


def _zero_ctx_tail(dpx, t_lat):
    wb = 512
    n_ctx = (dpx.shape[0] - t_lat) // TM

    def body(dpx_hbm, o_ref):
        o_ref[...] = jnp.zeros_like(o_ref)

    return pl.pallas_call(
        body, name="zero_ctx_tail", grid=(n_ctx, (IN_COLS - KV_COLS) // wb), input_output_aliases={0: 0},
        in_specs=[ANY], out_specs=pl.BlockSpec((TM, wb), lambda i, j: (t_lat // TM + i, KV_COLS // wb + j)),
        out_shape=SDS(dpx.shape, dpx.dtype),
        compiler_params=_cp(("parallel", "parallel")))(dpx)


def _ret_specs(row_f, row_b):
    c = RET_CHUNK
    wq = RET_HEADS * RET_DK // 2
    wv = RET_HEADS * RET_DV // 2
    specs = []
    for row in (row_f, row_b):
        specs += [pl.BlockSpec((c, wq), lambda b, n, row=row: (row(b, n), C_RQ // wq)),
                  pl.BlockSpec((c, wq), lambda b, n, row=row: (row(b, n), C_RQ // wq + 1)),
                  pl.BlockSpec((c, 2 * wq), lambda b, n, row=row: (row(b, n), C_RK // (2 * wq))),
                  pl.BlockSpec((c, wv), lambda b, n, row=row: (row(b, n), C_RV // wv)),
                  pl.BlockSpec((c, wv), lambda b, n, row=row: (row(b, n), C_RV // wv + 1))]
    return specs


def _ret_head(refs, h):
    q0, q1, k_ref, v0, v1 = refs
    hh = h % 2
    q = (q0, q1)[h // 2][:, hh * RET_DK:(hh + 1) * RET_DK].astype(F32)
    k = k_ref[:, h * RET_DK:(h + 1) * RET_DK].astype(F32) * (RET_DK ** -0.5)
    v = (v0, v1)[h // 2][:, hh * RET_DV:(hh + 1) * RET_DV]
    return q, k, v


def _ret_fwd(px, states0, lg, n_samp, seq, comm=None):
    c = RET_CHUNK
    nc = seq // c
    t_lat = n_samp * seq
    wo = RET_HEADS * RET_DV

    def row_f(b, n):
        return b * nc + n

    def row_b(b, n):
        return b * nc + nc - 1 - n

    def body(lg_ref, *refs):
        ins, (s0_ref, of_ref, ob_ref, st_ref, s_s) = refs[:10], refs[10:]

        @pl.when(pl.program_id(1) == 0)
        def _():
            s_s[...] = s0_ref[...]

        for d, o_ref in ((0, of_ref), (1, ob_ref)):
            for h in range(RET_HEADS):
                _, _, mask, qd, kd, gc = _decays(lg_ref[d, h], d)
                q, k, v = _ret_head(ins[5 * d:5 * d + 5], h)
                s = s_s[d, h]
                st_ref[h, d] = s.astype(BF)
                sc = _dot(q, k, 1, 1) * mask
                o_ref[:, h * RET_DV:(h + 1) * RET_DV] = (_dot(sc, v) + _dot(q * qd, s)).astype(BF)
                s_s[d, h] = s * gc + _dot(k * kd, v, 0, 0)

    return _call(
        body, [lg] + [px] * 10 + [states0], comm, name="ret_fwd", grid=(n_samp, nc),
        in_specs=[SMEM] + _ret_specs(row_f, row_b) + [
            pl.BlockSpec((None, 2, RET_HEADS, RET_DK, RET_DV), lambda b, n: (b, 0, 0, 0, 0))],
        out_specs=(pl.BlockSpec((c, wo), lambda b, n: (row_f(b, n), 0)),
                   pl.BlockSpec((c, wo), lambda b, n: (row_b(b, n), 0)),
                   pl.BlockSpec((None, RET_HEADS, 2, None, RET_DK, RET_DV), lambda b, n: (b, 0, 0, n, 0, 0))),
        out_shape=(SDS((t_lat, wo), BF), SDS((t_lat, wo), BF),
                   SDS((n_samp, RET_HEADS, 2, nc, RET_DK, RET_DV), BF)),
        scratch_shapes=[pltpu.VMEM((2, RET_HEADS, RET_DK, RET_DV), F32)],
        compiler_params=_cp(("arbitrary", "arbitrary"), 48))


def _ret_bwd(px, do, saved, lg, n_samp, seq, comm=None):
    c = RET_CHUNK
    nc = seq // c
    t_lat = n_samp * seq
    wq, wo = RET_HEADS * RET_DK, RET_HEADS * RET_DV

    def row_f(b, n):
        return b * nc + nc - 1 - n

    def row_b(b, n):
        return b * nc + n

    def body(lg_ref, *refs):
        ins = refs[:10]
        (dof_ref, dob_ref, st_ref, dqf, dkf, dvf, dqb, dkb, dvb, ds0_ref, dlg_ref, ds_s, acc_s) = refs[10:]
        n = pl.program_id(1)

        @pl.when(n == 0)
        def _():
            ds_s[...] = jnp.zeros_like(ds_s)
            acc_s[...] = jnp.zeros_like(acc_s)

        for d, (do_ref, dq_ref, dk_ref, dv_ref) in enumerate(((dof_ref, dqf, dkf, dvf), (dob_ref, dqb, dkb, dvb))):
            for h in range(RET_HEADS):
                a_i, rel, mask, qd, kd, gc = _decays(lg_ref[d, h], d)
                q, k, v = _ret_head(ins[5 * d:5 * d + 5], h)
                qb, kb, vb = q.astype(BF), k.astype(BF), v.astype(BF)
                dob = do_ref[:, h * RET_DV:(h + 1) * RET_DV].astype(BF)
                sb = st_ref[h, d]
                ds = ds_s[d, h]
                dsb = ds.astype(BF)
                raw = _dot(qb, kb, 1, 1)
                sc = raw * mask
                dsc = _dot(dob, vb, 1, 1) * mask
                dscb = dsc.astype(BF)
                x = _dot(dob, sb, 1, 1)
                y = _dot(vb, dsb, 1, 1)
                qq = q * qd
                kk = k * kd
                dq_ref[:, h * RET_DK:(h + 1) * RET_DK] = (_dot(dscb, kb) + x * qd).astype(BF)
                dk_ref[:, h * RET_DK:(h + 1) * RET_DK] = (_dot(dscb, qb, 0, 0) + y * kd).astype(BF)
                dv_ref[:, h * RET_DV:(h + 1) * RET_DV] = (_dot(sc, dob, 0, 0) + _dot(kk, dsb)).astype(BF)
                t = (_sum_all(dsc * raw * rel) + _sum_all((a_i + 1.0) * qq * x)
                     + _sum_all((c - 1.0 - a_i) * kk * y) + c * gc * _sum_all(ds * sb.astype(F32)))
                acc_s[4 * d + h:4 * d + h + 1, :] += t
                ds_s[d, h] = ds * gc + _dot(qq, dob, 0, 0)

        @pl.when(n == nc - 1)
        def _():
            ds0_ref[...] = ds_s[...]
            dlg_ref[...] = acc_s[...]

    do_spec_f = pl.BlockSpec((c, wo), lambda b, n: (row_f(b, n), 0))
    do_spec_b = pl.BlockSpec((c, wo), lambda b, n: (row_b(b, n), 0))
    dq_spec_f = pl.BlockSpec((c, wq), lambda b, n: (row_f(b, n), 0))
    dq_spec_b = pl.BlockSpec((c, wq), lambda b, n: (row_b(b, n), 0))
    return _call(
        body, [lg] + [px] * 10 + [do, do, saved], comm, name="ret_bwd", grid=(n_samp, nc),
        in_specs=[SMEM] + _ret_specs(row_f, row_b) + [
            do_spec_f, do_spec_b,
            pl.BlockSpec((None, RET_HEADS, 2, None, RET_DK, RET_DV), lambda b, n: (b, 0, 0, nc - 1 - n, 0, 0))],
        out_specs=(dq_spec_f, dq_spec_f, do_spec_f, dq_spec_b, dq_spec_b, do_spec_b,
                   pl.BlockSpec((None, 2, RET_HEADS, RET_DK, RET_DV), lambda b, n: (b, 0, 0, 0, 0)),
                   pl.BlockSpec((None, 8, 128), lambda b, n: (b, 0, 0))),
        out_shape=(SDS((t_lat, wq), BF), SDS((t_lat, wq), BF), SDS((t_lat, wo), BF),
                   SDS((t_lat, wq), BF), SDS((t_lat, wq), BF), SDS((t_lat, wo), BF),
                   SDS((n_samp, 2, RET_HEADS, RET_DK, RET_DV), F32), SDS((n_samp, 8, 128), F32)),
        scratch_shapes=[pltpu.VMEM((2, RET_HEADS, RET_DK, RET_DV), F32), pltpu.VMEM((8, 128), F32)],
        compiler_params=_cp(("arbitrary", "arbitrary"), 56))


def _combine_into(dpx, a, b, col0, scale):
    t_lat, width = a.shape
    wb = 512
    assert col0 % wb == 0 and width % wb == 0

    def body(dpx_hbm, a_ref, b_ref, o_ref):
        o_ref[...] = ((a_ref[...].astype(F32) + b_ref[...].astype(F32)) * scale).astype(BF)

    src = pl.BlockSpec((TM, wb), lambda i, j: (i, j))
    return pl.pallas_call(
        body, name="combine_into", grid=(t_lat // TM, width // wb), input_output_aliases={0: 0},
        in_specs=[ANY, src, src], out_specs=pl.BlockSpec((TM, wb), lambda i, j: (i, col0 // wb + j)),
        out_shape=SDS(dpx.shape, dpx.dtype),
        compiler_params=_cp(("parallel", "parallel")))(dpx, a, b)


def _retnorm_fwd(o_f, o_b, px):
    t_lat = o_f.shape[0]

    def body(of_ref, ob_ref, g_ref, y_ref):
        o = of_ref[...].astype(F32) + ob_ref[...].astype(F32)
        g = g_ref[...].astype(F32)
        y_ref[...] = (o * _rms(o) * (g * _sigmoid(g))).astype(BF)

    so = pl.BlockSpec((TM, RET_DV), lambda i, h: (i, h))
    return pl.pallas_call(
        body, name="retnorm_fwd", grid=(t_lat // TM, RET_HEADS),
        in_specs=[so, so, pl.BlockSpec((TM, RET_DV), lambda i, h: (i, C_RG // RET_DV + h))],
        out_specs=so,
        out_shape=SDS((t_lat, RET_HEADS * RET_DV), BF),
        compiler_params=_cp(("parallel", "parallel")))(o_f, o_b, px)


def _retnorm_bwd(dpx, dy, o_f, o_b, px):
    t_lat = o_f.shape[0]

    def body(dpx_hbm, dy_ref, of_ref, ob_ref, g_ref, do_ref, dg_ref):
        o = of_ref[...].astype(F32) + ob_ref[...].astype(F32)
        r = _rms(o)
        on = o * r
        g = g_ref[...].astype(F32)
        sg = _sigmoid(g)
        dy_ = dy_ref[...].astype(F32)
        dg_ref[...] = (dy_ * on * (sg * (1.0 + g * (1.0 - sg)))).astype(BF)
        do_ref[...] = _rms_bwd(dy_ * (g * sg), on, r).astype(BF)

    so = pl.BlockSpec((TM, RET_DV), lambda i, h: (i, h))
    gcol = pl.BlockSpec((TM, RET_DV), lambda i, h: (i, C_RG // RET_DV + h))
    return pl.pallas_call(
        body, name="retnorm_bwd", grid=(t_lat // TM, RET_HEADS), input_output_aliases={0: 1},
        in_specs=[ANY, so, so, so, gcol],
        out_specs=(so, gcol),
        out_shape=(SDS((t_lat, RET_HEADS * RET_DV), BF), SDS(dpx.shape, dpx.dtype)),
        compiler_params=_cp(("parallel", "parallel")))(dpx, dy, o_f, o_b, px)


def _norm_rope(x, w, cos, sin):
    xn = x * _rms(x) * w
    return xn * cos + _swap_pairs(xn) * sin


def _norm_rope_bwd(dy, x, w, cos, sin):
    dxn = dy * cos + _swap_pairs(dy * sin)
    r = _rms(x)
    xh = x * r
    return _rms_bwd(dxn * w, xh, r), jnp.sum(dxn * xh, axis=0, keepdims=True)


def _att_prep_q(px, cos_all, sin_all, qnw, t_lat):
    hd = ATT_HEAD_DIM
    wblk = ATT_REP * hd

    def body(x_ref, cos_ref, sin_ref, w_ref, o_ref):
        for r in range(ATT_REP):
            cols = slice(r * hd, (r + 1) * hd)
            qr = _norm_rope(x_ref[:, cols].astype(F32), w_ref[...], cos_ref[...], sin_ref[...])
            o_ref[:, cols] = (qr * (hd ** -0.5)).astype(BF)

    return pl.pallas_call(
        body, name="att_prep_q", grid=(t_lat // TM, ATT_KV_HEADS),
        in_specs=[pl.BlockSpec((TM, wblk), lambda i, g: (i, C_AQ // wblk + g)),
                  pl.BlockSpec((TM, hd), lambda i, g: (i, 0)),
                  pl.BlockSpec((TM, hd), lambda i, g: (i, 0)),
                  pl.BlockSpec((1, hd), lambda i, g: (0, 0))],
        out_specs=pl.BlockSpec((TM, wblk), lambda i, g: (i, g)),
        out_shape=SDS((t_lat, ATT_HEADS * hd), BF),
        compiler_params=_cp(("parallel", "parallel")))(px, cos_all, sin_all, qnw)


def _att_prep_kv(px, cos_all, sin_all, knw):
    rows = px.shape[0]
    hd = ATT_HEAD_DIM
    kvw = ATT_KV_HEADS * hd

    def body(x_ref, cos_ref, sin_ref, w_ref, k_ref, v_ref):
        for g in range(ATT_KV_HEADS):
            cols = slice(g * hd, (g + 1) * hd)
            k_ref[:, cols] = _norm_rope(x_ref[:, cols].astype(F32), w_ref[...], cos_ref[...],
                                        sin_ref[...]).astype(BF)
        v_ref[...] = x_ref[:, kvw:].astype(BF)

    return pl.pallas_call(
        body, name="att_prep_kv", grid=(rows // TM,),
        in_specs=[pl.BlockSpec((TM, 2 * kvw), lambda i: (i, C_AK // (2 * kvw))),
                  pl.BlockSpec((TM, hd), lambda i: (i, 0)),
                  pl.BlockSpec((TM, hd), lambda i: (i, 0)),
                  pl.BlockSpec((1, hd), lambda i: (0, 0))],
        out_specs=(pl.BlockSpec((TM, kvw), lambda i: (i, 0)), pl.BlockSpec((TM, kvw), lambda i: (i, 0))),
        out_shape=(SDS((rows, kvw), BF), SDS((rows, kvw), BF)),
        compiler_params=_cp(("parallel",)))(px, cos_all, sin_all, knw)


def _att_kv_bwd(dpx, dkl, dkc, dvl, dvc, px, cos_all, sin_all, knw):
    rows = px.shape[0]
    hd = ATT_HEAD_DIM
    kvw = ATT_KV_HEADS * hd
    n_lat = dkl.shape[0] // TM
    assert dkc.shape[0] == TM

    def body(dpx_hbm, dkl_ref, dkc_ref, dvl_ref, dvc_ref, x_ref, cos_ref, sin_ref, w_ref, o_ref, gw_ref):
        i = pl.program_id(0)

        @pl.when(i == 0)
        def _():
            gw_ref[...] = jnp.zeros_like(gw_ref)

        is_lat = i < n_lat
        dk = jnp.where(is_lat, dkl_ref[...], dkc_ref[...])
        dv = jnp.where(is_lat, dvl_ref[...], dvc_ref[...])
        for g in range(ATT_KV_HEADS):
            cols = slice(g * hd, (g + 1) * hd)
            dx, gw = _norm_rope_bwd(dk[:, cols], x_ref[:, cols].astype(F32), w_ref[...], cos_ref[...], sin_ref[...])
            o_ref[:, cols] = dx.astype(BF)
            gw_ref[...] += gw
        o_ref[:, kvw:] = dv.astype(BF)

    lat = pl.BlockSpec((TM, kvw), lambda i: (jnp.minimum(i, n_lat - 1), 0))
    ctx = pl.BlockSpec((TM, kvw), lambda i: (0, 0))
    kvcol = pl.BlockSpec((TM, 2 * kvw), lambda i: (i, C_AK // (2 * kvw)))
    return pl.pallas_call(
        body, name="att_kv_bwd", grid=(rows // TM,), input_output_aliases={0: 0},
        in_specs=[ANY, lat, ctx, lat, ctx, kvcol,
                  pl.BlockSpec((TM, hd), lambda i: (i, 0)),
                  pl.BlockSpec((TM, hd), lambda i: (i, 0)),
                  pl.BlockSpec((1, hd), lambda i: (0, 0))],
        out_specs=(kvcol, pl.BlockSpec((1, hd), lambda i: (0, 0))),
        out_shape=(SDS(dpx.shape, dpx.dtype), SDS((1, hd), F32)),
        compiler_params=_cp(("arbitrary",)))(dpx, dkl, dkc, dvl, dvc, px, cos_all, sin_all, knw)


def _stack_heads(ref_or_val):
    hd = ATT_HEAD_DIM
    return jnp.concatenate([ref_or_val[:, r * hd:(r + 1) * hd] for r in range(ATT_REP)], axis=0)


def _att_scores(q, kl, kc):
    sl = _dot(q, kl, 1, 1)
    sc = _dot(q, kc, 1, 1)
    m = jnp.maximum(jnp.max(sl, axis=-1, keepdims=True), jnp.max(sc, axis=-1, keepdims=True))
    el = jnp.exp(sl - m)
    ec = jnp.exp(sc - m)
    denom = jnp.sum(el, axis=-1, keepdims=True) + jnp.sum(ec, axis=-1, keepdims=True)
    return el, ec, denom, m


def _att_fwd(qn, kn, vn, n_samp, seq, lc):
    hd = ATT_HEAD_DIM
    tq = ATT_TQ
    nq = seq // tq
    wblk = ATT_REP * hd
    cb = n_samp * seq // lc
    t_lat = n_samp * seq

    def body(q_ref, kl_ref, kc_ref, vl_ref, vc_ref, o_ref, lse_ref):
        lane = lax.broadcasted_iota(jnp.int32, (tq, hd), 1)
        lse = jnp.zeros((tq, hd), F32)
        for r in range(ATT_REP):
            cols = slice(r * hd, (r + 1) * hd)
            el, ec, denom, m = _att_scores(q_ref[:, cols], kl_ref[...], kc_ref[...])
            o_ref[:, cols] = ((_dot(el, vl_ref[...]) + _dot(ec, vc_ref[...])) / denom).astype(BF)
            lse = jnp.where(lane == r, m + jnp.log(denom), lse)
        lse_ref[...] = lse

    return pl.pallas_call(
        body, name="att_fwd", grid=(n_samp, ATT_KV_HEADS, nq),
        in_specs=[pl.BlockSpec((tq, wblk), lambda b, g, i: (b * nq + i, g)),
                  pl.BlockSpec((seq, hd), lambda b, g, i: (b, g)),
                  pl.BlockSpec((lc, hd), lambda b, g, i: (cb + b, g)),
                  pl.BlockSpec((seq, hd), lambda b, g, i: (b, g)),
                  pl.BlockSpec((lc, hd), lambda b, g, i: (cb + b, g))],
        out_specs=(pl.BlockSpec((tq, wblk), lambda b, g, i: (b * nq + i, g)),
                   pl.BlockSpec((tq, hd), lambda b, g, i: (b * nq + i, g))),
        out_shape=(SDS((t_lat, ATT_HEADS * hd), BF), SDS((t_lat, ATT_KV_HEADS * hd), F32)),
        compiler_params=_cp(("parallel", "parallel", "parallel"), 48))(qn, kn, kn, vn, vn)


def _att_gate_bwd(dpx, dy_att, o_att, px):
    t_lat = dy_att.shape[0]
    wblk = ATT_REP * ATT_HEAD_DIM

    def body(dpx_hbm, dy_ref, o_ref, g_ref, out_ref):
        g = g_ref[...].astype(F32)
        sg = _sigmoid(g)
        out_ref[...] = (dy_ref[...].astype(F32) * o_ref[...].astype(F32) * (sg * (1.0 + g * (1.0 - sg)))).astype(BF)

    blk = pl.BlockSpec((TM, wblk), lambda i, j: (i, j))
    gcol = pl.BlockSpec((TM, wblk), lambda i, j: (i, C_AG // wblk + j))
    return pl.pallas_call(
        body, name="att_gate_bwd", grid=(t_lat // TM, ATT_KV_HEADS),
        in_specs=[ANY, blk, blk, gcol], out_specs=gcol, out_shape=SDS(dpx.shape, dpx.dtype),
        input_output_aliases={0: 0},
        compiler_params=_cp(("parallel", "parallel")))(dpx, dy_att, o_att, px)


def _att_bwd(dpx, qn, kn, vn, px, o_att, lse, do_att, cos_all, sin_all, qnw, n_samp, seq, lc, comm=None):
    hd = ATT_HEAD_DIM
    tq = ATT_TQ
    nq = seq // tq
    wblk = ATT_REP * hd
    cb = n_samp * seq // lc
    t_lat = n_samp * seq
    kvw = ATT_KV_HEADS * hd
    scale = hd ** -0.5

    def body(dpx_hbm, q_ref, kl_ref, kc_ref, vl_ref, vc_ref, o_ref, do_ref, x_ref, cos_ref, sin_ref, w_ref,
             lse_ref, dq_ref, dkl_ref, dkc_ref, dvl_ref, dvc_ref, gw_ref, akl, akc, avl, avc, aw):
        i = pl.program_id(2)

        @pl.when(i == 0)
        def _():
            akl[...] = jnp.zeros_like(akl)
            akc[...] = jnp.zeros_like(akc)
            avl[...] = jnp.zeros_like(avl)
            avc[...] = jnp.zeros_like(avc)
            aw[...] = jnp.zeros_like(aw)

        dobs, pls, pcs, dsls, dscs = [], [], [], [], []
        for r in range(ATT_REP):
            cols = slice(r * hd, (r + 1) * hd)
            dob = do_ref[:, cols]
            delta = jnp.sum(dob.astype(F32) * o_ref[:, cols].astype(F32), axis=-1, keepdims=True)
            lse = lse_ref[:, r:r + 1]
            p_l = jnp.exp(_dot(q_ref[:, cols], kl_ref[...], 1, 1) - lse).astype(BF)
            p_c = jnp.exp(_dot(q_ref[:, cols], kc_ref[...], 1, 1) - lse).astype(BF)
            ds_l = (p_l * (_dot(dob, vl_ref[...], 1, 1) - delta)).astype(BF)
            ds_c = (p_c * (_dot(dob, vc_ref[...], 1, 1) - delta)).astype(BF)
            dq = (_dot(ds_l, kl_ref[...]) + _dot(ds_c, kc_ref[...])) * scale
            dx, gw = _norm_rope_bwd(dq, x_ref[:, cols].astype(F32), w_ref[...], cos_ref[...], sin_ref[...])
            dq_ref[:, cols] = dx.astype(BF)
            aw[...] += gw
            dobs.append(dob)
            pls.append(p_l)
            pcs.append(p_c)
            dsls.append(ds_l)
            dscs.append(ds_c)
        do4 = jnp.concatenate(dobs, axis=0)
        q4 = _stack_heads(q_ref)
        avl[...] += _dot(jnp.concatenate(pls, axis=0), do4, 0, 0)
        avc[...] += _dot(jnp.concatenate(pcs, axis=0), do4, 0, 0)
        akl[...] += _dot(jnp.concatenate(dsls, axis=0), q4, 0, 0)
        akc[...] += _dot(jnp.concatenate(dscs, axis=0), q4, 0, 0)

        @pl.when(i == nq - 1)
        def _():
            dkl_ref[...] = akl[...]
            dkc_ref[...] = akc[...]
            dvl_ref[...] = avl[...]
            dvc_ref[...] = avc[...]
            gw_ref[...] = aw[...]

    return _call(
        body, [dpx, qn, kn, kn, vn, vn, o_att, do_att, px, cos_all, sin_all, qnw, lse], comm,
        name="att_bwd", grid=(n_samp, ATT_KV_HEADS, nq), aliases={0: 0},
        in_specs=[ANY,
                  pl.BlockSpec((tq, wblk), lambda b, g, i: (b * nq + i, g)),
                  pl.BlockSpec((seq, hd), lambda b, g, i: (b, g)),
                  pl.BlockSpec((lc, hd), lambda b, g, i: (cb + b, g)),
                  pl.BlockSpec((seq, hd), lambda b, g, i: (b, g)),
                  pl.BlockSpec((lc, hd), lambda b, g, i: (cb + b, g)),
                  pl.BlockSpec((tq, wblk), lambda b, g, i: (b * nq + i, g)),
                  pl.BlockSpec((tq, wblk), lambda b, g, i: (b * nq + i, g)),
                  pl.BlockSpec((tq, wblk), lambda b, g, i: (b * nq + i, C_AQ // wblk + g)),
                  pl.BlockSpec((tq, hd), lambda b, g, i: (b * nq + i, 0)),
                  pl.BlockSpec((tq, hd), lambda b, g, i: (b * nq + i, 0)),
                  pl.BlockSpec((1, hd), lambda b, g, i: (0, 0)),
                  pl.BlockSpec((tq, hd), lambda b, g, i: (b * nq + i, g))],
        out_specs=(pl.BlockSpec((tq, wblk), lambda b, g, i: (b * nq + i, C_AQ // wblk + g)),
                   pl.BlockSpec((seq, hd), lambda b, g, i: (b, g)),
                   pl.BlockSpec((lc, hd), lambda b, g, i: (b, g)),
                   pl.BlockSpec((seq, hd), lambda b, g, i: (b, g)),
                   pl.BlockSpec((lc, hd), lambda b, g, i: (b, g)),
                   pl.BlockSpec((None, None, 1, hd), lambda b, g, i: (b, g, 0, 0))),
        out_shape=(SDS(dpx.shape, dpx.dtype),
                   SDS((t_lat, kvw), F32), SDS((n_samp * lc, kvw), F32),
                   SDS((t_lat, kvw), F32), SDS((n_samp * lc, kvw), F32),
                   SDS((n_samp, ATT_KV_HEADS, 1, hd), F32)),
        scratch_shapes=[pltpu.VMEM((seq, hd), F32), pltpu.VMEM((lc, hd), F32),
                        pltpu.VMEM((seq, hd), F32), pltpu.VMEM((lc, hd), F32), pltpu.VMEM((1, hd), F32)],
        compiler_params=_cp(("arbitrary", "arbitrary", "arbitrary"), 56))


def _merge(x_lat, target, o_f, o_b, o_att, px, gate3, w_o_ret, w_o_att, w_out, tiles_per_sample):
    t_lat = x_lat.shape[0]
    tm = 256
    n_t = t_lat // tm
    per = tiles_per_sample * (TM // tm)
    d = D_MODEL
    rv = RET_HEADS * RET_DV
    n_samp = gate3.shape[0] - 1

    half = d // 2
    n_px = 10

    def body(x_ref, t_ref, of_ref, ob_ref, oa_ref, *rest):
        pxs, rest = rest[:n_px], rest[n_px:]
        (gt_ref, wor_ref, woa_ref, wout_ref,
         gx_ref, dor_ref, doa_ref, dpx_hbm, loss_ref, dgt_ref, gwor_hbm, gwoa_hbm, gwout_hbm,
         aor, aoa, aout, drg_ref, dtail_ref, sems) = rest
        i = pl.program_id(0)

        def copies(step):
            rows = pl.ds(pl.multiple_of(step * tm, tm), tm)
            return (pltpu.make_async_copy(drg_ref, dpx_hbm.at[rows, pl.ds(C_RG, rv)], sems.at[0]),
                    pltpu.make_async_copy(dtail_ref, dpx_hbm.at[rows, pl.ds(C_AG, 3 * d)], sems.at[1]))

        @pl.when(i == 0)
        def _():
            aor[...] = jnp.zeros_like(aor)
            aoa[...] = jnp.zeros_like(aoa)
            aout[...] = jnp.zeros_like(aout)
            loss_ref[...] = jnp.zeros_like(loss_ref)

        @pl.when(i % per == 0)
        def _():
            dgt_ref[...] = jnp.zeros_like(dgt_ref)

        def cat(refs):
            return jnp.concatenate([r[...] for r in refs], axis=1).astype(F32)

        def ret_head(h):
            cols = slice(h * RET_DV, (h + 1) * RET_DV)
            o = of_ref[:, cols].astype(F32) + ob_ref[:, cols].astype(F32)
            r = _rms(o)
            g = pxs[h][...].astype(F32)
            return o * r, r, g, _sigmoid(g)

        def att_half(k):
            o = oa_ref[:, k * half:(k + 1) * half].astype(F32)
            g = pxs[4 + k][...].astype(F32)
            return o, g, _sigmoid(g)

        yrs = []
        for h in range(RET_HEADS):
            on, _, g, sg = ret_head(h)
            yrs.append((on * (g * sg)).astype(BF))
        yr = jnp.concatenate(yrs, axis=1)
        yas = []
        for k in range(2):
            o, g, sg = att_half(k)
            yas.append((o * (g * sg)).astype(BF))
        ya = jnp.concatenate(yas, axis=1)

        a = jnp.dot(yr, wor_ref[...], preferred_element_type=F32)
        b = jnp.dot(ya, woa_ref[...], preferred_element_type=F32)
        sr = _sigmoid(cat(pxs[6:8]))
        sa = _sigmoid(cat(pxs[8:10]))
        yb = (sr * a + sa * b).astype(BF)
        out = jnp.dot(yb, wout_ref[...], preferred_element_type=F32)
        gate = gt_ref[...]
        err = x_ref[...] + gate * out - t_ref[...]
        loss_ref[...] += 0.5 * _sum_all(err * err) * (1.0 / d)
        dy_tok = err * (1.0 / d)
        gx_ref[...] = dy_tok
        dgt_ref[...] += jnp.sum(dy_tok * out, axis=0, keepdims=True)
        dout = (dy_tok * gate).astype(BF)
        aout[...] += _dot(yb, dout, 0, 0)
        dyy = _dot(dout, wout_ref[...], 1, 1)
        da = (dyy * sr).astype(BF)
        db = (dyy * sa).astype(BF)
        aor[...] += _dot(yr, da, 0, 0)
        aoa[...] += _dot(ya, db, 0, 0)
        dyr = _dot(da, wor_ref[...], 1, 1)
        dya = _dot(db, woa_ref[...], 1, 1)

        @pl.when(i > 0)
        def _():
            for cp in copies(i - 1):
                cp.wait()

        dtail_ref[:, d:2 * d] = (dyy * a * (sr * (1.0 - sr))).astype(BF)
        dtail_ref[:, 2 * d:] = (dyy * b * (sa * (1.0 - sa))).astype(BF)
        for h in range(RET_HEADS):
            cols = slice(h * RET_DV, (h + 1) * RET_DV)
            on, r, g, sg = ret_head(h)
            dy = dyr[:, cols]
            drg_ref[:, cols] = (dy * on * (sg * (1.0 + g * (1.0 - sg)))).astype(BF)
            dor_ref[:, cols] = _rms_bwd(dy * (g * sg), on, r).astype(BF)
        for k in range(2):
            cols = slice(k * half, (k + 1) * half)
            o, g, sg = att_half(k)
            dy = dya[:, cols]
            dtail_ref[:, cols] = (dy * o * (sg * (1.0 + g * (1.0 - sg)))).astype(BF)
            doa_ref[:, cols] = (dy * (g * sg)).astype(BF)
        for cp in copies(i):
            cp.start()

        @pl.when(i == n_t - 1)
        def _():
            for cp in copies(i):
                cp.wait()
            pltpu.sync_copy(aor, gwor_hbm)
            pltpu.sync_copy(aoa, gwoa_hbm)
            pltpu.sync_copy(aout, gwout_hbm)

    def px_blk(col):
        return pl.BlockSpec((tm, half), lambda i: (i, col // half))

    def resident(shape):
        return pl.BlockSpec(shape, lambda i: (0, 0), pipeline_mode=pl.Buffered(1))

    px_cols = ([C_RG + k * half for k in range(4)] + [C_AG, C_AG + half]
               + [C_MR, C_MR + half, C_MA, C_MA + half])
    return pl.pallas_call(
        body, name="merge", grid=(n_t,),
        in_specs=[pl.BlockSpec((tm, d), lambda i: (i, 0)),
                  pl.BlockSpec((tm, d), lambda i: (i, 0)),
                  pl.BlockSpec((tm, rv), lambda i: (i, 0)),
                  pl.BlockSpec((tm, rv), lambda i: (i, 0)),
                  pl.BlockSpec((tm, d), lambda i: (i, 0))]
        + [px_blk(col) for col in px_cols]
        + [pl.BlockSpec((None, 1, d), lambda i: (i // per, 0, 0)),
           resident((rv, d)), resident((d, d)), resident((d, d))],
        out_specs=(pl.BlockSpec((tm, d), lambda i: (i, 0)),
                   pl.BlockSpec((tm, rv), lambda i: (i, 0)),
                   pl.BlockSpec((tm, d), lambda i: (i, 0)),
                   ANY,
                   pl.BlockSpec((8, 128), lambda i: (0, 0)),
                   pl.BlockSpec((None, 1, d), lambda i: (i // per, 0, 0)),
                   ANY, ANY, ANY),
        out_shape=(SDS((t_lat, d), F32), SDS((t_lat, rv), BF), SDS((t_lat, d), BF),
                   SDS((px.shape[0], IN_COLS), BF),
                   SDS((8, 128), F32), SDS((n_samp, 1, d), F32),
                   SDS((rv, d), F32), SDS((d, d), F32), SDS((d, d), F32)),
        scratch_shapes=[pltpu.VMEM((rv, d), F32), pltpu.VMEM((d, d), F32), pltpu.VMEM((d, d), F32),
                        pltpu.VMEM((tm, rv), BF), pltpu.VMEM((tm, 3 * d), BF), pltpu.SemaphoreType.DMA((2,))],
        compiler_params=_cp(("arbitrary",), 56))(
            x_lat, target, o_f, o_b, o_att, *([px] * n_px), gate3, w_o_ret, w_o_att, w_out)


def _place():
    x, y, c = lax.axis_index("x"), lax.axis_index("y"), lax.axis_index("c")
    chips = [(1 - x, y), (x, 1 - y), (1 - x, 1 - y)]
    return x, y, c, chips


def _remote(src, dst, send_sem, recv_sem, to):
    return pltpu.make_async_remote_copy(src_ref=src, dst_ref=dst, send_sem=send_sem, recv_sem=recv_sem,
                                        device_id=to, device_id_type=MESH)


def _place_ids():
    x, y, c = lax.axis_index("x"), lax.axis_index("y"), lax.axis_index("c")
    me = 2 * x + y
    return jnp.stack([x, y, c, me, me, 2 * (1 - x) + y, 2 * x + 1 - y, 2 * (1 - x) + 1 - y]).astype(jnp.int32)


def _ag_comm(bufs, rels, arg_index=None):
    n, m = len(bufs), len(rels)

    def half(ref, s, which):
        h = ref.shape[1] // 2
        return ref.at[s, pl.ds(which * h, h), :]

    def ici(ins, outs, ssem, rsem, base):
        x, y, c, chips = _place()
        sends, recvs = [], []
        for a in range(n):
            for jj, j in enumerate(rels):
                k, chip = base + a * m + jj, chips[j]
                mine, theirs = half(outs[a], 2 * x + y, c), half(outs[a], 2 * chip[0] + chip[1], c)
                sends.append(_remote(mine, mine, ssem.at[k], rsem.at[k], (*chip, c)))
                recvs.append(_remote(theirs, theirs, ssem.at[k], rsem.at[k], (*chip, c)))
        return sends, recvs

    def d2d(ins, outs, ssem, rsem, base):
        x, y, c, chips = _place()
        sends, recvs = [], []
        for a in range(n):
            for jj, j in enumerate(rels):
                k, s = base + (n + a) * m + jj, 2 * chips[j][0] + chips[j][1]
                sends.append(_remote(half(outs[a], s, c), half(outs[a], s, c), ssem.at[k], rsem.at[k], (x, y, 1 - c)))
                recvs.append(_remote(half(outs[a], s, 1 - c), half(outs[a], s, 1 - c), ssem.at[k], rsem.at[k],
                                     (x, y, 1 - c)))
        return sends, recvs

    shapes = tuple(SDS(b.shape, b.dtype) for b in bufs)
    if arg_index is not None:
        return _Comm("all_gather", (), shapes, {}, 2 * n * m, (ici, d2d), ((arg_index, 0),))
    return _Comm("all_gather", tuple(bufs), shapes, {a: a for a in range(n)}, 2 * n * m, (ici, d2d))


def _swap_comm(grads):
    n = len(grads)

    def phase(ins, outs, ssem, rsem, base):
        x, y, c, _ = _place()
        sends = []
        for a in range(n):
            h = ins[a].shape[1] // 2
            sends.append(_remote(ins[a].at[:, pl.ds((1 - c) * h, h), :], outs[a], ssem.at[base + a],
                                 rsem.at[base + a], (x, y, 1 - c)))
        return sends, sends

    return _Comm("swap_halves", tuple(grads),
                 tuple(SDS((g.shape[0], g.shape[1] // 2, g.shape[2]), g.dtype) for g in grads), {}, n, (phase,))


def _exchange_comm(parts):
    n = len(parts)

    def phase(ins, outs, ssem, rsem, base):
        x, y, c, chips = _place()
        sends = []
        for a in range(n):
            for j, chip in enumerate(chips):
                k = base + 3 * a + j
                sends.append(_remote(ins[a].at[2 * chip[0] + chip[1]], outs[a].at[j], ssem.at[k], rsem.at[k],
                                     (*chip, c)))
        return sends, sends

    return _Comm("exchange_shards", tuple(parts), tuple(SDS((3,) + p.shape[1:], p.dtype) for p in parts), {}, 3 * n,
                 (phase,))


def _join_comm(bufs, n_parts=1):
    n = len(bufs)

    def phase(ins, outs, ssem, rsem, base):
        x, y, c, _ = _place()
        sends, recvs = [], []
        for a in range(n):
            h = outs[a].shape[0] // (2 * n_parts)
            for p in range(n_parts):
                k = base + a * n_parts + p
                mine = outs[a].at[pl.ds((2 * p + c) * h, h), :]
                other = outs[a].at[pl.ds((2 * p + 1 - c) * h, h), :]
                sends.append(_remote(mine, mine, ssem.at[k], rsem.at[k], (x, y, 1 - c)))
                recvs.append(_remote(other, other, ssem.at[k], rsem.at[k], (x, y, 1 - c)))
        return sends, recvs

    return _Comm("join_halves", tuple(bufs), tuple(SDS(b.shape, b.dtype) for b in bufs), {a: a for a in range(n)},
                 n * n_parts, (phase,))


def _cast_place(w, ids):
    rows, cols = w.shape
    tr = min(rows, 256)

    def body(ids_ref, w_ref, o_ref):
        o_ref[...] = w_ref[...].astype(BF)

    return pl.pallas_call(
        body, name="cast_place",
        grid_spec=pltpu.PrefetchScalarGridSpec(
            num_scalar_prefetch=1, grid=(rows // tr,),
            in_specs=[pl.BlockSpec((tr, cols), lambda i, ids_ref: (i, 0))],
            out_specs=pl.BlockSpec((None, tr, cols), lambda i, ids_ref: (ids_ref[3], i, 0))),
        out_shape=SDS((N_SHARD, rows, cols), BF),
        compiler_params=_cp(("parallel",), 40))(ids, w)


def _all_gather_weights(bufs):
    n = len(bufs)

    def body(*refs):
        outs = refs[n:2 * n]
        send_sems, recv_sems = refs[2 * n:]
        x, y, c, chips = _place()
        sibling = (x, y, 1 - c)
        me = 2 * x + y

        def half(ref, s, which):
            h = ref.shape[1] // 2
            return ref.at[s, pl.ds(which * h, h), :]

        first = []
        for a in range(n):
            for j, chip in enumerate(chips):
                k = a * 3 + j
                win = half(outs[a], me, c)
                first.append(_remote(win, win, send_sems.at[k], recv_sems.at[k], (*chip, c)))
        for cp in first:
            cp.start()
        passed = []
        for a in range(n):
            for j, chip in enumerate(chips):
                k = a * 3 + j
                win = half(outs[a], 2 * chip[0] + chip[1], c)
                _remote(win, win, send_sems.at[k], recv_sems.at[k], (*chip, c)).wait_recv()
                fw = _remote(win, win, send_sems.at[3 * n + k], recv_sems.at[3 * n + k], sibling)
                fw.start()
                passed.append(fw)
        for a in range(n):
            for j, chip in enumerate(chips):
                k = a * 3 + j
                win = half(outs[a], 2 * chip[0] + chip[1], 1 - c)
                _remote(win, win, send_sems.at[3 * n + k], recv_sems.at[3 * n + k], sibling).wait_recv()
        for cp in first + passed:
            cp.wait_send()

    return pl.pallas_call(
        body, name="all_gather_weights",
        in_specs=[ANY] * n, out_specs=tuple([ANY] * n),
        out_shape=tuple(SDS(b.shape, b.dtype) for b in bufs),
        input_output_aliases={a: a for a in range(n)},
        scratch_shapes=[pltpu.SemaphoreType.DMA((6 * n,)), pltpu.SemaphoreType.DMA((6 * n,))],
        compiler_params=_cp(has_side_effects=True))(*bufs)


def _swap_halves(grads):
    n = len(grads)

    def body(*refs):
        ins, outs = refs[:n], refs[n:2 * n]
        send_sems, recv_sems = refs[2 * n:]
        x, y, c, _ = _place()
        sibling = (x, y, 1 - c)

        def half(ref, which):
            h = ref.shape[1] // 2
            return ref.at[:, pl.ds(which * h, h), :]

        sends = [_remote(half(ins[a], 1 - c), outs[a], send_sems.at[a], recv_sems.at[a], sibling)
                 for a in range(n)]
        for cp in sends:
            cp.start()
        for cp in sends:
            cp.wait_recv()
        for cp in sends:
            cp.wait_send()

    return pl.pallas_call(
        body, name="swap_halves",
        in_specs=[ANY] * n, out_specs=tuple([ANY] * n),
        out_shape=tuple(SDS((g.shape[0], g.shape[1] // 2, g.shape[2]), g.dtype) for g in grads),
        scratch_shapes=[pltpu.SemaphoreType.DMA((n,)), pltpu.SemaphoreType.DMA((n,))],
        compiler_params=_cp(has_side_effects=True))(*grads)


def _chip_sum(g, p, ids):
    n_s, rows, cols = g.shape
    h = rows // 2
    tr = min(h, 256)
    nb = h // tr

    def body(ids_ref, g_ref, p_ref, o_ref, o16_ref):
        t = g_ref[...] + p_ref[...]
        o_ref[...] = t
        o16_ref[...] = t.astype(BF)

    out_spec = pl.BlockSpec((None, tr, cols), lambda s, i, ids_ref: (s, i, 0))
    return pl.pallas_call(
        body, name="chip_sum",
        grid_spec=pltpu.PrefetchScalarGridSpec(
            num_scalar_prefetch=1, grid=(n_s, nb),
            in_specs=[pl.BlockSpec((None, tr, cols), lambda s, i, ids_ref: (s, ids_ref[2] * nb + i, 0)),
                      pl.BlockSpec((None, tr, cols), lambda s, i, ids_ref: (s, i, 0))],
            out_specs=(out_spec, out_spec)),
        out_shape=(SDS((n_s, h, cols), g.dtype), SDS((n_s, h, cols), BF)),
        compiler_params=_cp(("parallel", "parallel"), 40))(ids, g, p)


def _exchange_shards(parts):
    n = len(parts)

    def body(*refs):
        ins, outs = refs[:n], refs[n:2 * n]
        send_sems, recv_sems = refs[2 * n:]
        x, y, c, chips = _place()
        sends = []
        for a in range(n):
            for j, chip in enumerate(chips):
                k = a * 3 + j
                sends.append(_remote(ins[a].at[2 * chip[0] + chip[1]], outs[a].at[j],
                                     send_sems.at[k], recv_sems.at[k], (*chip, c)))
        for cp in sends:
            cp.start()
        for cp in sends:
            cp.wait_recv()
        for cp in sends:
            cp.wait_send()

    return pl.pallas_call(
        body, name="exchange_shards",
        in_specs=[ANY] * n, out_specs=tuple([ANY] * n),
        out_shape=tuple(SDS((3,) + p.shape[1:], p.dtype) for p in parts),
        scratch_shapes=[pltpu.SemaphoreType.DMA((3 * n,)), pltpu.SemaphoreType.DMA((3 * n,))],
        compiler_params=_cp(has_side_effects=True))(*parts)


def _shard_sum(t, q, ids, part=0, n_parts=1, buf=None):
    _, h, cols = t.shape
    tr = min(h, 256)
    nb = h // tr

    def body(ids_ref, t_ref, q_ref, *rest):
        rest[-1][...] = ((t_ref[...] + q_ref[0].astype(F32)) + q_ref[1].astype(F32)) + q_ref[2].astype(F32)

    args, in_specs, aliases = [t, q], [
        pl.BlockSpec((None, tr, cols), lambda i, ids_ref: (ids_ref[3], i, 0)),
        pl.BlockSpec((3, tr, cols), lambda i, ids_ref: (0, i, 0))], None
    if buf is not None:
        args, in_specs, aliases = args + [buf], in_specs + [ANY], {2: 0}
    return _call(body, args, None, name="shard_sum", grid=(nb,), in_specs=in_specs,
                 out_specs=pl.BlockSpec((tr, cols), lambda i, ids_ref: ((2 * part + ids_ref[2]) * nb + i, 0)),
                 out_shape=SDS((2 * h * n_parts, cols), t.dtype), aliases=aliases, prefetch=ids,
                 compiler_params=_cp(("parallel",), 40))


def _join_halves(bufs):
    n = len(bufs)

    def body(*refs):
        outs = refs[n:2 * n]
        send_sems, recv_sems = refs[2 * n:]
        x, y, c, _ = _place()
        sibling = (x, y, 1 - c)

        def win(ref, which):
            h = ref.shape[0] // 2
            return ref.at[pl.ds(which * h, h), :]

        sends = [_remote(win(outs[a], c), win(outs[a], c), send_sems.at[a], recv_sems.at[a], sibling)
                 for a in range(n)]
        for cp in sends:
            cp.start()
        for a in range(n):
            other = win(outs[a], 1 - c)
            _remote(other, other, send_sems.at[a], recv_sems.at[a], sibling).wait_recv()
        for cp in sends:
            cp.wait_send()

    return pl.pallas_call(
        body, name="join_halves",
        in_specs=[ANY] * n, out_specs=tuple([ANY] * n),
        out_shape=tuple(SDS(b.shape, b.dtype) for b in bufs),
        input_output_aliases={a: a for a in range(n)},
        scratch_shapes=[pltpu.SemaphoreType.DMA((n,)), pltpu.SemaphoreType.DMA((n,))],
        compiler_params=_cp(has_side_effects=True))(*bufs)


def _gather_small(block, n_sum):
    rows, cols = block.shape
    n_dev = 8

    def body(x_ref, o_ref, g_ref, buf, send_sems, recv_sems, local_sem):
        x, y, c, chips = _place()
        me, sibling = (x, y, c), (x, y, 1 - c)

        def slot(px_, py_, pc_):
            return buf.at[4 * px_ + 2 * py_ + pc_]

        def copy(k, who, to, src=None):
            return _remote(slot(*who) if src is None else src, slot(*who), send_sems.at[k], recv_sems.at[k], to)

        mine = pltpu.make_async_copy(x_ref, slot(*me), local_sem)
        mine.start()
        first = [copy(0, me, sibling, src=x_ref)]
        first += [copy(1 + j, me, (*chip, c), src=x_ref) for j, chip in enumerate(chips)]
        for cp in first:
            cp.start()
        passed = [copy(4 + j, (*chip, c), sibling) for j, chip in enumerate(chips)]
        for j, chip in enumerate(chips):
            copy(1 + j, (*chip, c), me).wait_recv()
            passed[j].start()
        copy(0, sibling, me).wait_recv()
        for j, chip in enumerate(chips):
            copy(4 + j, (*chip, 1 - c), me).wait_recv()
        for cp in first + passed:
            cp.wait_send()
        mine.wait()
        acc = buf[0, :, :n_sum]
        for s in range(1, n_dev):
            acc = acc + buf[s, :, :n_sum]
        o_ref[...] = acc
        for s in range(n_dev):
            g_ref[s * rows:(s + 1) * rows, :] = buf[s, :, n_sum:]

    return pl.pallas_call(
        body, name="gather_small",
        in_specs=[pl.BlockSpec(memory_space=pltpu.VMEM)],
        out_specs=(pl.BlockSpec(memory_space=pltpu.VMEM), pl.BlockSpec(memory_space=pltpu.VMEM)),
        out_shape=(SDS((rows, n_sum), F32), SDS((n_dev * rows, cols - n_sum), F32)),
        scratch_shapes=[pltpu.VMEM((n_dev, rows, cols), F32), pltpu.SemaphoreType.DMA((7,)),
                        pltpu.SemaphoreType.DMA((7,)), pltpu.SemaphoreType.DMA],
        compiler_params=_cp(has_side_effects=True))(block)


def _adam_math(w, g, m, v):
    m = ADAM_B1 * m + (1.0 - ADAM_B1) * g
    v = ADAM_B2 * v + (1.0 - ADAM_B2) * (g * g)
    m_hat = m / (1.0 - ADAM_B1 ** ADAM_STEP)
    v_hat = v / (1.0 - ADAM_B2 ** ADAM_STEP)
    delta = -ADAM_LR * (m_hat / (jnp.sqrt(v_hat) + ADAM_EPS) + ADAM_WD * w)
    return delta, m, v


def _adamw(w, g, m, v):
    rows, cols = w.shape
    tr = min(rows, 256 if cols <= 2048 else 128)

    def body(w_ref, g_ref, m_ref, v_ref, go_ref, d_ref, nm_ref, nv_ref):
        g = g_ref[...]
        go_ref[...] = g
        d_ref[...], nm_ref[...], nv_ref[...] = _adam_math(w_ref[...], g, m_ref[...], v_ref[...])

    spec = pl.BlockSpec((tr, cols), lambda i: (i, 0))
    return pl.pallas_call(
        body, name="adamw", grid=(rows // tr,), in_specs=[spec] * 4, out_specs=(spec,) * 4,
        out_shape=(SDS(w.shape, F32),) * 4, compiler_params=_cp(("parallel",), 40))(w, g, m, v)


def _adamw_small(w, g, m, v):
    def body(w_ref, g_ref, m_ref, v_ref, go_ref, d_ref, nm_ref, nv_ref):
        w = w_ref[...]
        g = g_ref[...]
        sub = lax.broadcasted_iota(jnp.int32, w.shape, 0)
        lane = lax.broadcasted_iota(jnp.int32, w.shape, 1)
        is_ret = jnp.logical_and(sub == 5, lane < 2 * RET_HEADS)
        u = jnp.exp(jnp.where(is_ret, w, -1.0) * jnp.log(2.0))
        g = jnp.where(is_ret, g * (-u * jnp.log(2.0) / (1.0 - u)), g)
        go_ref[...] = g
        d_ref[...], nm_ref[...], nv_ref[...] = _adam_math(w, g, m_ref[...], v_ref[...])

    return pl.pallas_call(body, name="adamw_small", out_shape=(SDS(w.shape, F32),) * 4)(w, g, m, v)


def _rope_tables(seq, n_samp, n_ctx_rows):
    rows = seq // GRID_W
    row = jnp.repeat(jnp.arange(rows, dtype=F32), GRID_W)
    col = jnp.tile(jnp.arange(GRID_W, dtype=F32), rows)
    half = ATT_HEAD_DIM // 2
    freqs = ROPE_THETA ** (-jnp.arange(0, half, 2, dtype=F32) / half)
    ang = jnp.concatenate([row[:, None] * freqs, col[:, None] * freqs], axis=-1)
    cos, sin = jnp.cos(ang), jnp.sin(ang)
    cos_f = jnp.repeat(cos, 2, axis=1)
    sin_s = jnp.stack([-sin, sin], axis=-1).reshape(seq, ATT_HEAD_DIM)
    cos_all = jnp.concatenate([jnp.tile(cos_f, (n_samp, 1)), jnp.ones((n_ctx_rows, ATT_HEAD_DIM), F32)], axis=0)
    sin_all = jnp.concatenate([jnp.tile(sin_s, (n_samp, 1)), jnp.zeros((n_ctx_rows, ATT_HEAD_DIM), F32)], axis=0)
    return cos_all, sin_all


def _pack_small(c_ctx, norm_w, b_ada, ret, qn, kn):
    d = D_MODEL
    row5 = jnp.concatenate([ret.reshape(-1), jnp.zeros((128 - 2 * RET_HEADS,), F32), qn.reshape(-1), kn.reshape(-1),
                            jnp.zeros((d - 384,), F32)])
    return jnp.concatenate([c_ctx.reshape(1, d), norm_w.reshape(1, d), b_ada.reshape(3, d), row5.reshape(1, d),
                            jnp.zeros((2, d), F32)], axis=0)


def _unpack_small(p):
    d = D_MODEL
    return (p[0], p[1:2], p[2:5].reshape(1, 3 * d), p[5, :2 * RET_HEADS].reshape(1, 2, RET_HEADS),
            p[5:6, 128:256], p[5:6, 256:384])


def _step(x, c, ctx, c_ctx, norm_w, b_ada, ret_log2_decay, q_norm_w, k_norm_w, loss_target, weights, ids, dist):
    n_samp, seq, d = x.shape
    lc = ctx.shape[1]
    t_lat, t_ctx = n_samp * seq, n_samp * lc
    assert seq % TM == 0 and t_ctx == TM and t_lat % lc == 0 and seq % GRID_W == 0
    tps = seq // TM

    x_lat = x.reshape(t_lat, d)
    x_ctx = ctx.reshape(t_ctx, d)
    cvec8 = jnp.concatenate([c, c_ctx.reshape(1, d), jnp.zeros((8 - n_samp - 1, d), F32)], axis=0)
    lg = jnp.log1p(-jnp.exp2(ret_log2_decay.reshape(2, RET_HEADS)))
    cos_all, sin_all = _rope_tables(seq, n_samp, t_ctx)

    w_ada_b, w_in_b, w_or_b, w_oa_b, w_out_b = weights
    w_ada_g = _run_comm(_ag_comm((w_ada_b,), (0, 1, 2)))[0] if dist else w_ada_b
    mod8 = _adaln_fwd(cvec8, w_ada_g, b_ada)
    mod3 = mod8[:n_samp + 1]
    shift3 = mod3[:, None, 0:d]
    scale3 = mod3[:, None, d:2 * d]
    gate3 = mod3[:, None, 2 * d:3 * d]

    hx, hxt = _norm_fwd(x_lat, x_ctx, norm_w, scale3, shift3, tps, n_samp)
    if dist:
        px, (w_in_1,) = _in_proj(hx, w_in_b, ids, 0, 1, comm=_ag_comm((w_in_b,), (0, 1), arg_index=1))
        px, (w_in_g,) = _in_proj(hx, w_in_1, ids, 1, 2, px=px, comm=_ag_comm((w_in_1,), (2,), arg_index=1))
        px = _in_proj(hx, w_in_g, ids, 3, 1, px=px)
    else:
        w_in_g = w_in_b
        px = _in_proj(hx, w_in_g, ids, 0, N_SHARD)

    states0 = _ctx_state_fwd(px, lg, n_samp, t_lat, lc)
    if dist:
        (o_f, o_b, saved), w_o = _ret_fwd(px, states0, lg, n_samp, seq,
                                          comm=_ag_comm((w_or_b, w_oa_b, w_out_b), (0, 1, 2)))
    else:
        (o_f, o_b, saved), w_o = _ret_fwd(px, states0, lg, n_samp, seq), (w_or_b, w_oa_b, w_out_b)
    w_o_ret, w_o_att, w_out = (w.reshape(-1, d) for w in w_o)

    qn = _att_prep_q(px, cos_all, sin_all, q_norm_w, t_lat)
    kn, vn = _att_prep_kv(px, cos_all, sin_all, k_norm_w)
    o_att, lse = _att_fwd(qn, kn, vn, n_samp, seq, lc)

    (gx_res, do, do_att, dpx, loss8, dgate, g_w_o_ret, g_w_o_att, g_w_out) = _merge(
        x_lat, loss_target.reshape(t_lat, d), o_f, o_b, o_att, px, gate3, w_o_ret, w_o_att, w_out, tps)

    g_a = [g.reshape(N_SHARD, -1, d) for g in (g_w_o_ret, g_w_o_att, g_w_out)]
    res = _att_bwd(dpx, qn, kn, vn, px, o_att, lse, do_att, cos_all, sin_all, q_norm_w, n_samp, seq, lc,
                   comm=_swap_comm(g_a) if dist else None)
    (dpx, dkl, dkc, dvl, dvc, gqw), sib_a = res if dist else (res, None)
    dpx, gkw = _att_kv_bwd(dpx, dkl, dkc, dvl, dvc, px, cos_all, sin_all, k_norm_w)
    if dist:
        t_a = [_chip_sum(g, p, ids) for g, p in zip(g_a, sib_a)]

    res = _ret_bwd(px, do, saved, lg, n_samp, seq,
                   comm=_exchange_comm([t16 for _, t16 in t_a]) if dist else None)
    (dqf, dkf, dvf, dqb, dkb, dvb, dstates, dlg_lat), q_a = res if dist else (res, None)
    if dist:
        r_a = [_shard_sum(t, q, ids) for (t, _), q in zip(t_a, q_a)]
    dpx = _combine_into(dpx, dqf, dqb, C_RQ, 1.0)
    dpx = _combine_into(dpx, dkf, dkb, C_RK, RET_DK ** -0.5)
    dpx = _combine_into(dpx, dvf, dvb, C_RV, 1.0)
    dpx, dlg_ctx = _ctx_state_bwd(dpx, px, dstates, lg, n_samp, t_lat, lc)
    dpx = _zero_ctx_tail(dpx, t_lat)

    n_tiles = dpx.shape[0] // _big_rows(dpx.shape[0])
    if dist:
        g_b = _gw_in(hxt, dpx, 0, 1)
        dhx, (sib_b, *r_a) = _dhx(dpx, w_in_g, 0, 1, comm=_join_comms(_swap_comm([g_b]), _join_comm(r_a)))
        t_b, t16_b = _chip_sum(g_b, sib_b, ids)
        dhx, (q_b,) = _dhx(dpx, w_in_g, 1, n_tiles - 1, dhx=dhx, comm=_exchange_comm([t16_b]))
        (grad_x, dshift, dscale, g_norm_w), (r_b,) = _norm_bwd(
            x_lat, x_ctx, dhx, gx_res, norm_w, scale3, tps, n_samp,
            comm=_join_comm([_shard_sum(t_b, q_b, ids)]))
    else:
        g_w_in = _gw_in(hxt, dpx, 0, 1)
        dhx = _dhx(dpx, w_in_g, 0, n_tiles)
        grad_x, dshift, dscale, g_norm_w = _norm_bwd(x_lat, x_ctx, dhx, gx_res, norm_w, scale3, tps, n_samp)

    dgate_all = jnp.concatenate([dgate, jnp.zeros((1, 1, d), F32)], axis=0)
    dmod3 = jnp.concatenate([dshift, dscale, dgate_all], axis=2).reshape(n_samp + 1, 3 * d)
    dmod8 = jnp.concatenate([dmod3, jnp.zeros((8 - n_samp - 1, 3 * d), F32)], axis=0)
    g_lg = (jnp.sum(dlg_lat[:, :, 0], axis=0).reshape(2, RET_HEADS)
            + jnp.stack([jnp.sum(dlg_ctx[:, :, 0, 0], axis=0), jnp.sum(dlg_ctx[:, :, 1, 0], axis=0)], axis=0))
    g_qw = jnp.sum(gqw, axis=(0, 1, 2))
    zero = jnp.zeros((d,), F32)
    if not dist:
        g_w_ada, g_b_ada, dc8 = _adaln_bwd(cvec8, dmod8, w_ada_g)
        small = _pack_small(dc8[n_samp], g_norm_w, g_b_ada, g_lg, g_qw, gkw)
        return (loss8[0, 0], grad_x.reshape(n_samp, seq, d),
                (g_w_ada, g_w_in, g_w_o_ret, g_w_o_att, g_w_out), small)

    local = _pack_small(zero, g_norm_w, jnp.zeros((3 * d,), F32), g_lg, g_qw, gkw).at[6, 0].set(loss8[0, 0])
    small_sum, gathered = _gather_small(jnp.concatenate([local, cvec8, dmod8], axis=1), d)
    g_w_ada, g_b_ada, dc_all = _adaln_bwd(gathered[:, :d], gathered[:, d:], w_ada_g)
    dc_ctx = jnp.sum(dc_all.reshape(-1, 8, d)[:, n_samp], axis=0)
    small = small_sum + _pack_small(dc_ctx, zero, g_b_ada, jnp.zeros((2, RET_HEADS), F32), zero[:128], zero[:128])
    r_c = lax.dynamic_index_in_dim(g_w_ada, ids[3], 0, keepdims=False)
    return small[6, 0], grad_x.reshape(n_samp, seq, d), (r_c, r_b, *r_a), small


def kernel(x, c, ctx, c_ctx, norm_w, w_ada, b_ada, w_in, ret_log2_decay, q_norm_w, k_norm_w, w_o_ret, w_o_att, w_out, loss_target, m_c_ctx, m_norm_w, m_w_ada, m_b_ada, m_w_in, m_ret_log2_decay, m_q_norm_w, m_k_norm_w, m_w_o_ret, m_w_o_att, m_w_out, v_c_ctx, v_norm_w, v_w_ada, v_b_ada, v_w_in, v_ret_log2_decay, v_q_norm_w, v_k_norm_w, v_w_o_ret, v_w_o_att, v_w_out):
    big_w = (w_ada[0], w_in[0], w_o_ret[0], w_o_att[0], w_out[0])
    big_m = (m_w_ada[0], m_w_in[0], m_w_o_ret[0], m_w_o_att[0], m_w_out[0])
    big_v = (v_w_ada[0], v_w_in[0], v_w_o_ret[0], v_w_o_att[0], v_w_out[0])

    ids = _place_ids()
    loss, grad_x, big_grad, small_grad_in = _step(
        x, c, ctx, c_ctx, norm_w[0:1], b_ada[0:1], ret_log2_decay[0], q_norm_w[0:1], k_norm_w[0:1], loss_target,
        tuple(_cast_place(w, ids) for w in big_w), ids, True)
    small_w = _pack_small(c_ctx, norm_w, b_ada, ret_log2_decay, q_norm_w, k_norm_w)
    small_m = _pack_small(m_c_ctx, m_norm_w, m_b_ada, m_ret_log2_decay, m_q_norm_w, m_k_norm_w)
    small_v = _pack_small(v_c_ctx, v_norm_w, v_b_ada, v_ret_log2_decay, v_q_norm_w, v_k_norm_w)
    small_grad, small_delta, small_nm, small_nv = _adamw_small(small_w, small_grad_in, small_m, small_v)

    big_g, big_delta, big_nm, big_nv = [], [], [], []
    for w, g, m, v in zip(big_w, big_grad, big_m, big_v):
        go, dlt, nm, nv = _adamw(w, g, m, v)
        big_g.append(go[None])
        big_delta.append(dlt[None])
        big_nm.append(nm[None])
        big_nv.append(nv[None])
    big_grad = big_g

    def order(small_packed, big):
        s = _unpack_small(small_packed)
        return (s[0], s[1], big[0], s[2], big[1], s[3], s[4], s[5], big[2], big[3], big[4])

    return (loss, grad_x, *order(small_grad, big_grad), *order(small_delta, big_delta),
            *order(small_nm, big_nm), *order(small_nv, big_nv))
```

```python
import functools
from typing import NamedTuple

import jax
import jax.numpy as jnp
from jax import lax
from jax.experimental import pallas as pl
from jax.experimental.pallas import tpu as pltpu

F32 = jnp.float32
BF = jnp.bfloat16
SDS = jax.ShapeDtypeStruct
MESH = pl.DeviceIdType.MESH
ANY = pl.BlockSpec(memory_space=pl.ANY)
SMEM = pl.BlockSpec(memory_space=pltpu.SMEM)

D_MODEL = 1024
GRID_W = 64
RET_HEADS = 4
RET_DK = 256
RET_DV = 512
RET_CHUNK = 128
ATT_HEADS = 8
ATT_KV_HEADS = 2
ATT_REP = ATT_HEADS // ATT_KV_HEADS
ATT_HEAD_DIM = 128
ROPE_THETA = 10000.0
NORM_EPS = 1e-6
IN_COLS = 10752
KV_COLS = 3584
C_RK, C_RV, C_AK, C_AV, C_RQ, C_RG, C_AQ, C_AG, C_MR, C_MA = 0, 1024, 3072, 3328, 3584, 4608, 6656, 7680, 8704, 9728
N_SHARD = 4
ADA_W = 3 * D_MODEL // N_SHARD
IN_W = IN_COLS // N_SHARD
IN_BLK = IN_W
BPS = IN_W // IN_BLK
N_IN_BLK = IN_COLS // IN_BLK
TM = 512
ATT_TQ = 512
ADAM_LR, ADAM_B1, ADAM_B2, ADAM_EPS, ADAM_WD, ADAM_STEP = 0.001, 0.9, 0.999, 1e-08, 0.01, 10
MIB = 1024 * 1024


def _cp(sem=None, vmem_mb=None, **kw):
    if sem is not None:
        kw["dimension_semantics"] = sem
    if vmem_mb is not None:
        kw["vmem_limit_bytes"] = vmem_mb * MIB
    return pltpu.CompilerParams(**kw)


def _dot(a, b, ca=1, cb=0):
    return lax.dot_general(a.astype(BF), b.astype(BF), (((ca,), (cb,)), ((), ())), preferred_element_type=F32)


def _sigmoid(x):
    return 1.0 / (1.0 + jnp.exp(-x))


def _sum_all(x):
    return jnp.sum(jnp.sum(x, axis=1, keepdims=True), axis=0, keepdims=True)


def _swap_pairs(x):
    ax = x.ndim - 1
    lane = lax.broadcasted_iota(jnp.int32, x.shape, ax)
    nxt = pltpu.roll(x, x.shape[ax] - 1, ax)
    prv = pltpu.roll(x, 1, ax)
    return jnp.where(lane % 2 == 0, nxt, prv)


def _rms(x):
    return lax.rsqrt(jnp.mean(x * x, axis=-1, keepdims=True) + NORM_EPS)


def _rms_bwd(dxh, xh, r):
    return r * (dxh - xh * jnp.mean(dxh * xh, axis=-1, keepdims=True))


class _Comm(NamedTuple):
    name: str
    ins: tuple
    out_shapes: tuple
    aliases: dict
    n_sems: int
    phases: tuple
    arg_aliases: tuple = ()


def _join_comms(*comms):
    comms = [cm for cm in comms if cm is not None]
    if len(comms) <= 1:
        return comms[0] if comms else None
    offs, i_off, o_off, s_off = [], 0, 0, 0
    for cm in comms:
        offs.append((i_off, o_off, s_off))
        i_off, o_off, s_off = i_off + len(cm.ins), o_off + len(cm.out_shapes), s_off + cm.n_sems

    def phase(k):
        def run(ins, outs, ssem, rsem, base):
            sends, recvs = [], []
            for cm, (io, oo, so) in zip(comms, offs):
                if k < len(cm.phases):
                    s, r = cm.phases[k](ins[io:io + len(cm.ins)], outs[oo:oo + len(cm.out_shapes)], ssem, rsem,
                                        base + so)
                    sends += s
                    recvs += r
            return sends, recvs
        return run

    aliases, arg_aliases = {}, ()
    for cm, (io, oo, _) in zip(comms, offs):
        aliases.update({io + a: oo + b for a, b in cm.aliases.items()})
        arg_aliases += tuple((a, oo + b) for a, b in cm.arg_aliases)
    return _Comm("+".join(cm.name for cm in comms), sum((cm.ins for cm in comms), ()),
                 sum((cm.out_shapes for cm in comms), ()), aliases, s_off,
                 tuple(phase(k) for k in range(max(len(cm.phases) for cm in comms))), arg_aliases)


def _run_phases(comm, cins, couts, ssem, rsem, first_started):
    for k, phase in enumerate(comm.phases):
        sends, recvs = phase(cins, couts, ssem, rsem, 0)
        if k > 0 or not first_started:
            for cp in sends:
                cp.start()
        for cp in recvs:
            cp.wait_recv()
        for cp in sends:
            cp.wait_send()


def _call(body, args, comm=None, *, name, grid, in_specs, out_specs, out_shape, scratch_shapes=(),
          compiler_params, aliases=None, prefetch=None):
    single = not isinstance(out_shape, (tuple, list))
    out_specs_t = (out_specs,) if single else tuple(out_specs)
    out_shape_t = (out_shape,) if single else tuple(out_shape)
    n_pre = 0 if prefetch is None else 1
    n_in, n_out, n_sc = len(in_specs), len(out_specs_t), len(scratch_shapes)
    io_alias = {n_pre + a: b for a, b in (aliases or {}).items()}
    if comm is None:
        kernel_body, cin, cout, csems = body, [], [], []
    else:
        n_ci, n_co = len(comm.ins), len(comm.out_shapes)
        cin, cout = [ANY] * n_ci, [ANY] * n_co
        csems = [pltpu.SemaphoreType.DMA((comm.n_sems,)), pltpu.SemaphoreType.DMA((comm.n_sems,))]
        io_alias.update({n_pre + n_in + a: n_out + b for a, b in comm.aliases.items()})
        io_alias.update({n_pre + a: n_out + b for a, b in comm.arg_aliases})

        def kernel_body(*refs):
            pre, refs = refs[:n_pre], refs[n_pre:]
            ins, cins = refs[:n_in], refs[n_in:n_in + n_ci]
            outs = refs[n_in + n_ci:n_in + n_ci + n_out]
            couts = refs[n_in + n_ci + n_out:n_in + n_ci + n_out + n_co]
            scratch = refs[n_in + n_ci + n_out + n_co:n_in + n_ci + n_out + n_co + n_sc]
            ssem, rsem = refs[-2:]
            first = functools.reduce(jnp.logical_and, [pl.program_id(k) == 0 for k in range(len(grid))])
            last = functools.reduce(jnp.logical_and, [pl.program_id(k) == grid[k] - 1 for k in range(len(grid))])

            @pl.when(first)
            def _():
                for cp in comm.phases[0](cins, couts, ssem, rsem, 0)[0]:
                    cp.start()

            body(*pre, *ins, *outs, *scratch)

            @pl.when(last)
            def _():
                _run_phases(comm, cins, couts, ssem, rsem, True)

        name = name + "+" + comm.name

    all_in, all_out = list(in_specs) + cin, out_specs_t + tuple(cout)
    shapes = out_shape_t + (tuple(comm.out_shapes) if comm is not None else ())
    scratch = list(scratch_shapes) + csems
    if prefetch is None:
        res = pl.pallas_call(kernel_body, name=name, grid=grid, in_specs=all_in, out_specs=all_out, out_shape=shapes,
                             scratch_shapes=scratch, input_output_aliases=io_alias,
                             compiler_params=compiler_params)(*args, *(comm.ins if comm is not None else ()))
    else:
        res = pl.pallas_call(
            kernel_body, name=name, out_shape=shapes, input_output_aliases=io_alias, compiler_params=compiler_params,
            grid_spec=pltpu.PrefetchScalarGridSpec(num_scalar_prefetch=1, grid=grid, in_specs=all_in,
                                                   out_specs=all_out, scratch_shapes=scratch))(
                                                       prefetch, *args, *(comm.ins if comm is not None else ()))
    own = res[0] if single else tuple(res[:n_out])
    return own if comm is None else (own, tuple(res[n_out:]))


def _run_comm(comm):
    n_ci, n_co = len(comm.ins), len(comm.out_shapes)

    def body(*refs):
        _run_phases(comm, refs[:n_ci], refs[n_ci:n_ci + n_co], refs[-2], refs[-1], False)

    return pl.pallas_call(
        body, name=comm.name, in_specs=[ANY] * n_ci, out_specs=tuple([ANY] * n_co), out_shape=tuple(comm.out_shapes),
        input_output_aliases=dict(comm.aliases),
        scratch_shapes=[pltpu.SemaphoreType.DMA((comm.n_sems,)), pltpu.SemaphoreType.DMA((comm.n_sems,))],
        compiler_params=_cp(has_side_effects=True))(*comm.ins)


def _adaln_fwd(cvec8, w_ada_g, b_ada):
    def body(c_ref, w_ref, b_ref, o_ref):
        cv = c_ref[...]
        sc = (cv * _sigmoid(cv)).astype(BF)
        for s in range(N_SHARD):
            cols = slice(s * ADA_W, (s + 1) * ADA_W)
            o_ref[:, cols] = jnp.dot(sc, w_ref[s], preferred_element_type=F32) + b_ref[:, cols]

    return pl.pallas_call(body, out_shape=SDS((8, 3 * D_MODEL), F32), name="adaln_fwd",
                          compiler_params=_cp(vmem_mb=32))(cvec8, w_ada_g, b_ada)


def _adaln_bwd(cvec, dmod, w_ada_g, comm=None):
    n_rows = cvec.shape[0]
    def body(c_ref, d_ref, w_ref, gw_ref, gb_ref, dc_ref):
        cv = c_ref[...]
        sg = _sigmoid(cv)
        sc = cv * sg
        dm = d_ref[...]
        gb_ref[...] = jnp.sum(dm, axis=0, keepdims=True)
        dsc = jnp.zeros(cv.shape, F32)
        for s in range(N_SHARD):
            cols = slice(s * ADA_W, (s + 1) * ADA_W)
            gw_ref[s] = _dot(sc, dm[:, cols], 0, 0)
            dsc = dsc + _dot(dm[:, cols], w_ref[s], 1, 1)
        dc_ref[...] = dsc * (sg * (1.0 + cv * (1.0 - sg)))

    def whole(shape):
        return pl.BlockSpec(shape, lambda i: (0,) * len(shape))

    shapes = ((N_SHARD, D_MODEL, ADA_W), (1, 3 * D_MODEL), (n_rows, D_MODEL))
    return _call(body, [cvec, dmod, w_ada_g], comm, name="adaln_bwd", grid=(1,),
                 in_specs=[whole(cvec.shape), whole(dmod.shape), whole(w_ada_g.shape)],
                 out_specs=tuple(whole(s) for s in shapes), out_shape=tuple(SDS(s, F32) for s in shapes),
                 compiler_params=_cp(("arbitrary",), 56))


def _big_rows(rows):
    return 1536 if rows % 1536 == 0 else TM


def _norm_fwd(x_lat, x_ctx, norm_w, scale3, shift3, tiles_per_sample, n_samp):
    n_lat = x_lat.shape[0] // TM
    rows = x_lat.shape[0] + x_ctx.shape[0]

    def samp(i):
        return jnp.minimum(i // tiles_per_sample, n_samp)

    def body(x_ref, c_ref, nw_ref, sc_ref, sh_ref, hx_ref, hxt_ref):
        x = jnp.where(pl.program_id(0) < n_lat, x_ref[...], c_ref[...])
        h = x * _rms(x) * nw_ref[...] * (1.0 + sc_ref[...]) + sh_ref[...]
        hx_ref[...] = h.astype(BF)
        hxt_ref[...] = h.T.astype(BF)

    return pl.pallas_call(
        body, name="norm_fwd", grid=(rows // TM,),
        in_specs=[pl.BlockSpec((TM, D_MODEL), lambda i: (jnp.minimum(i, n_lat - 1), 0)),
                  pl.BlockSpec((TM, D_MODEL), lambda i: (jnp.maximum(i - n_lat, 0), 0)),
                  pl.BlockSpec((1, D_MODEL), lambda i: (0, 0)),
                  pl.BlockSpec((None, 1, D_MODEL), lambda i: (samp(i), 0, 0)),
                  pl.BlockSpec((None, 1, D_MODEL), lambda i: (samp(i), 0, 0))],
        out_specs=(pl.BlockSpec((TM, D_MODEL), lambda i: (i, 0)),
                   pl.BlockSpec((D_MODEL, TM), lambda i: (0, i))),
        out_shape=(SDS((rows, D_MODEL), BF), SDS((D_MODEL, rows), BF)),
        compiler_params=_cp(("parallel",), 40))(x_lat, x_ctx, norm_w, scale3, shift3)


def _in_proj(hx, w_in_g, ids, first, count, px=None, comm=None):
    rows = hx.shape[0]
    tb = _big_rows(rows)

    def shard(j, ids_ref):
        return ids_ref[4 + first + j // BPS]

    def body(ids_ref, h_ref, w_ref, *rest):
        px_ref = rest[-1]
        px_ref[...] = jnp.dot(h_ref[...], w_ref[...], preferred_element_type=F32).astype(BF)

    args, in_specs, aliases = [hx, w_in_g], [
        pl.BlockSpec((tb, D_MODEL), lambda j, i, ids_ref: (i, 0)),
        pl.BlockSpec((None, D_MODEL, IN_BLK), lambda j, i, ids_ref: (shard(j, ids_ref), 0, j % BPS))], None
    if px is not None:
        args, in_specs, aliases = args + [px], in_specs + [ANY], {2: 0}
    return _call(body, args, comm, name="in_proj", grid=(BPS * count, rows // tb), in_specs=in_specs,
                 out_specs=pl.BlockSpec((tb, IN_BLK),
                                        lambda j, i, ids_ref: (i, BPS * shard(j, ids_ref) + j % BPS)),
                 out_shape=SDS((rows, IN_COLS), BF), aliases=aliases, prefetch=ids,
                 compiler_params=_cp(("arbitrary", "arbitrary"), 56))


def _norm_bwd(x_lat, x_ctx, dhx, gx_res, norm_w, scale3, tiles_per_sample, n_samp, comm=None):
    rows = x_lat.shape[0] + x_ctx.shape[0]
    n_lat = tiles_per_sample * n_samp

    def samp(i):
        return jnp.minimum(i // tiles_per_sample, n_samp)

    def lat(i):
        return jnp.minimum(i, n_lat - 1)

    def body(x_ref, c_ref, dh_ref, gr_ref, nw_ref, sc_ref, gx_ref, dsh_ref, dsc_ref, dnw_ref):
        i = pl.program_id(0)
        x = jnp.where(i < n_lat, x_ref[...], c_ref[...])
        r = _rms(x)
        xh = x * r
        nw = nw_ref[...]
        dh = dh_ref[...]
        first = jnp.logical_or(i % tiles_per_sample == 0, i >= n_lat)

        @pl.when(first)
        def _():
            dsh_ref[...] = jnp.zeros_like(dsh_ref)
            dsc_ref[...] = jnp.zeros_like(dsc_ref)

        @pl.when(i == 0)
        def _():
            dnw_ref[...] = jnp.zeros_like(dnw_ref)

        dsh_ref[...] += jnp.sum(dh, axis=0, keepdims=True)
        dsc_ref[...] += jnp.sum(dh * (xh * nw), axis=0, keepdims=True)
        du = dh * (1.0 + sc_ref[...])
        dnw_ref[...] += jnp.sum(du * xh, axis=0, keepdims=True)

        @pl.when(i < n_lat)
        def _():
            gx_ref[...] = gr_ref[...] + _rms_bwd(du * nw, xh, r)

    return _call(
        body, [x_lat, x_ctx, dhx, gx_res, norm_w, scale3], comm, name="norm_bwd", grid=(rows // TM,),
        in_specs=[pl.BlockSpec((TM, D_MODEL), lambda i: (lat(i), 0)),
                  pl.BlockSpec((TM, D_MODEL), lambda i: (jnp.maximum(i - n_lat, 0), 0)),
                  pl.BlockSpec((TM, D_MODEL), lambda i: (i, 0)),
                  pl.BlockSpec((TM, D_MODEL), lambda i: (lat(i), 0)),
                  pl.BlockSpec((1, D_MODEL), lambda i: (0, 0)),
                  pl.BlockSpec((None, 1, D_MODEL), lambda i: (samp(i), 0, 0))],
        out_specs=(pl.BlockSpec((TM, D_MODEL), lambda i: (lat(i), 0)),
                   pl.BlockSpec((None, 1, D_MODEL), lambda i: (samp(i), 0, 0)),
                   pl.BlockSpec((None, 1, D_MODEL), lambda i: (samp(i), 0, 0)),
                   pl.BlockSpec((1, D_MODEL), lambda i: (0, 0))),
        out_shape=(SDS((n_lat * TM, D_MODEL), F32), SDS((n_samp + 1, 1, D_MODEL), F32),
                   SDS((n_samp + 1, 1, D_MODEL), F32), SDS((1, D_MODEL), F32)),
        compiler_params=_cp(("arbitrary",), 40))


def _in_proj_gather(hx, w_buf, ids):
    rows = hx.shape[0]
    tb = _big_rows(rows)
    n_i = rows // tb
    hrows = D_MODEL // 2

    def body(ids_ref, h_ref, w_in_hbm, px_ref, w_hbm, wv, lsem, ssem, rsem):
        j, i = pl.program_id(0), pl.program_id(1)
        x, y, c, chips = _place()
        sibling = (x, y, 1 - c)

        def half(s, which):
            return w_hbm.at[s, pl.ds(which * hrows, hrows), :]

        def over_ici(rel):
            chip = chips[rel]
            mine, theirs = half(2 * x + y, c), half(2 * chip[0] + chip[1], c)
            return (_remote(mine, mine, ssem.at[rel], rsem.at[rel], (*chip, c)),
                    _remote(theirs, theirs, ssem.at[rel], rsem.at[rel], (*chip, c)))

        def over_d2d(rel):
            s = 2 * chips[rel][0] + chips[rel][1]
            return (_remote(half(s, c), half(s, c), ssem.at[3 + rel], rsem.at[3 + rel], sibling),
                    _remote(half(s, 1 - c), half(s, 1 - c), ssem.at[3 + rel], rsem.at[3 + rel], sibling))

        first_row_tile = i == 0

        @pl.when(jnp.logical_and(j == 0, first_row_tile))
        def _():
            over_ici(0)[0].start()
            over_ici(1)[0].start()

        for rel in range(3):
            @pl.when(jnp.logical_and(j == rel + 1, first_row_tile))
            def _(rel=rel):
                over_ici(rel)[1].wait_recv()
                passed, landing = over_d2d(rel)
                passed.start()
                if rel == 0:
                    over_ici(2)[0].start()
                landing.wait_recv()

        @pl.when(first_row_tile)
        def _():
            cp = pltpu.make_async_copy(w_hbm.at[ids_ref[4 + j]], wv, lsem)
            cp.start()
            cp.wait()

        px_ref[...] = jnp.dot(h_ref[...], wv[...], preferred_element_type=F32).astype(BF)

        @pl.when(jnp.logical_and(j == N_SHARD - 1, i == n_i - 1))
        def _():
            for rel in range(3):
                over_ici(rel)[0].wait_send()
                over_d2d(rel)[0].wait_send()

    return pl.pallas_call(
        body, name="in_proj_gather", input_output_aliases={2: 1},
        grid_spec=pltpu.PrefetchScalarGridSpec(
            num_scalar_prefetch=1, grid=(N_SHARD, n_i),
            in_specs=[pl.BlockSpec((tb, D_MODEL), lambda j, i, ids_ref: (i, 0)), ANY],
            out_specs=(pl.BlockSpec((tb, IN_W), lambda j, i, ids_ref: (i, ids_ref[4 + j])), ANY),
            scratch_shapes=[pltpu.VMEM((D_MODEL, IN_W), BF), pltpu.SemaphoreType.DMA,
                            pltpu.SemaphoreType.DMA((6,)), pltpu.SemaphoreType.DMA((6,))]),
        out_shape=(SDS((rows, IN_COLS), BF), SDS(w_buf.shape, w_buf.dtype)),
        compiler_params=_cp(("arbitrary", "arbitrary"), 56))(ids, hx, w_buf)


def _gw_in(hxt, dpx_all, part, n_parts, comm=None):
    rows = dpx_all.shape[0]
    tb = _big_rows(rows)
    dp = D_MODEL // n_parts

    def body(h_ref, d_ref, o_ref):
        @pl.when(pl.program_id(1) == 0)
        def _():
            o_ref[...] = jnp.zeros_like(o_ref)

        o_ref[...] += jnp.dot(h_ref[...], d_ref[...], preferred_element_type=F32)

    return _call(body, [hxt, dpx_all], comm, name="gw_in", grid=(N_IN_BLK, rows // tb),
                 in_specs=[pl.BlockSpec((dp, tb), lambda j, i: (part, i)),
                           pl.BlockSpec((tb, IN_BLK), lambda j, i: (i, j))],
                 out_specs=pl.BlockSpec((None, dp, IN_BLK), lambda j, i: (j // BPS, 0, j % BPS)),
                 out_shape=SDS((N_SHARD, dp, IN_W), F32),
                 compiler_params=_cp(("arbitrary", "arbitrary"), 56))


def _dhx(dpx_all, w_in_g, tile0, n_tiles, dhx=None, comm=None):
    rows = dpx_all.shape[0]
    tb = _big_rows(rows)

    def body(d_ref, w_ref, *rest):
        o_ref = rest[-1]

        @pl.when(pl.program_id(1) == 0)
        def _():
            o_ref[...] = jnp.zeros_like(o_ref)

        o_ref[...] += lax.dot_general(d_ref[...], w_ref[...], (((1,), (1,)), ((), ())), preferred_element_type=F32)

    args, in_specs, aliases = [dpx_all, w_in_g], [
        pl.BlockSpec((tb, IN_BLK), lambda i, j: (tile0 + i, j)),
        pl.BlockSpec((None, D_MODEL, IN_BLK), lambda i, j: (j // BPS, 0, j % BPS))], None
    if dhx is not None:
        args, in_specs, aliases = args + [dhx], in_specs + [ANY], {2: 0}
    return _call(body, args, comm, name="dhx", grid=(n_tiles, N_IN_BLK), in_specs=in_specs,
                 out_specs=pl.BlockSpec((tb, D_MODEL), lambda i, j: (tile0 + i, 0)),
                 out_shape=SDS((rows, D_MODEL), F32), aliases=aliases,
                 compiler_params=_cp(("arbitrary", "arbitrary"), 56))


def _decays(lgv, d):
    c = RET_CHUNK
    ii = lax.broadcasted_iota(jnp.int32, (c, 1), 0).astype(F32)
    jj = lax.broadcasted_iota(jnp.int32, (1, c), 1).astype(F32)
    a_i = jnp.where(d == 0, ii, c - 1.0 - ii)
    a_j = jnp.where(d == 0, jj, c - 1.0 - jj)
    rel = a_i - a_j
    mask = jnp.where(rel >= 0, jnp.exp(lgv * jnp.maximum(rel, 0.0)), 0.0)
    qd = jnp.exp(lgv * (a_i + 1.0))
    kd = jnp.exp(lgv * (c - 1.0 - a_i))
    gc = jnp.exp(jnp.full((1, 1), lgv * c, F32))
    return a_i, rel, mask, qd, kd, gc


def _ctx_state_fwd(px, lg, n_samp, t_lat, lc):
    rb = t_lat // lc

    def body(lg_ref, k_ref, v_ref, o_ref):
        h = pl.program_id(1)
        k = k_ref[...].astype(F32) * (RET_DK ** -0.5)
        v = v_ref[...]
        pos = lax.broadcasted_iota(jnp.int32, (lc, 1), 0).astype(F32)
        o_ref[0] = _dot(k * jnp.exp(lg_ref[0, h] * (lc - 1.0 - pos)), v, 0, 0)
        o_ref[1] = _dot(k * jnp.exp(lg_ref[1, h] * pos), v, 0, 0)

    return pl.pallas_call(
        body, name="ctx_state_fwd", grid=(n_samp, RET_HEADS),
        in_specs=[SMEM,
                  pl.BlockSpec((lc, RET_DK), lambda b, h: (rb + b, C_RK // RET_DK + h)),
                  pl.BlockSpec((lc, RET_DV), lambda b, h: (rb + b, C_RV // RET_DV + h))],
        out_specs=pl.BlockSpec((None, 2, None, RET_DK, RET_DV), lambda b, h: (b, 0, h, 0, 0)),
        out_shape=SDS((n_samp, 2, RET_HEADS, RET_DK, RET_DV), F32),
        compiler_params=_cp(("parallel", "parallel")))(lg, px, px)


def _ctx_state_bwd(dpx, px, dstates, lg, n_samp, t_lat, lc):
    rb = t_lat // lc
    kspec = pl.BlockSpec((lc, RET_DK), lambda b, h: (rb + b, C_RK // RET_DK + h))
    vspec = pl.BlockSpec((lc, RET_DV), lambda b, h: (rb + b, C_RV // RET_DV + h))
    sspec = pl.BlockSpec((None, 2, None, RET_DK, RET_DV), lambda b, h: (b, 0, h, 0, 0))

    def weights(lg_ref, h):
        pos = lax.broadcasted_iota(jnp.int32, (lc, 1), 0).astype(F32)
        e_f = lc - 1.0 - pos
        return pos, e_f, jnp.exp(lg_ref[0, h] * e_f), jnp.exp(lg_ref[1, h] * pos)

    def k_body(lg_ref, dpx_hbm, k_ref, v_ref, ds_ref, dk_ref, dlg_ref):
        pos, e_f, w_f, w_b = weights(lg_ref, pl.program_id(1))
        k = k_ref[...].astype(F32) * (RET_DK ** -0.5)
        y_f = _dot(v_ref[...], ds_ref[0], 1, 1) * w_f
        y_b = _dot(v_ref[...], ds_ref[1], 1, 1) * w_b
        dk_ref[...] = ((y_f + y_b) * (RET_DK ** -0.5)).astype(BF)
        t_f = _sum_all(e_f * k * y_f)
        t_b = _sum_all(pos * k * y_b)
        sub = lax.broadcasted_iota(jnp.int32, (8, 128), 0)
        dlg_ref[...] = jnp.where(sub == 0, t_f, jnp.where(sub == 1, t_b, 0.0))

    def v_body(lg_ref, dpx_hbm, k_ref, ds_ref, dv_ref):
        _, _, w_f, w_b = weights(lg_ref, pl.program_id(1))
        k = k_ref[...].astype(F32) * (RET_DK ** -0.5)
        dv_ref[...] = (_dot(k * w_f, ds_ref[0]) + _dot(k * w_b, ds_ref[1])).astype(BF)

    dpx, dlg = pl.pallas_call(
        k_body, name="ctx_state_bwd_k", grid=(n_samp, RET_HEADS), input_output_aliases={1: 0},
        in_specs=[SMEM, ANY, kspec, vspec, sspec],
        out_specs=(kspec, pl.BlockSpec((None, None, 8, 128), lambda b, h: (b, h, 0, 0))),
        out_shape=(SDS(dpx.shape, dpx.dtype), SDS((n_samp, RET_HEADS, 8, 128), F32)),
        compiler_params=_cp(("parallel", "parallel")))(lg, dpx, px, px, dstates)
    dpx = pl.pallas_call(
        v_body, name="ctx_state_bwd_v", grid=(n_samp, RET_HEADS), input_output_aliases={1: 0},
        in_specs=[SMEM, ANY, kspec, sspec], out_specs=vspec, out_shape=SDS(dpx.shape, dpx.dtype),
        compiler_params=_cp(("parallel", "parallel")))(lg, dpx, px, dstates)
    return dpx, dlg


def _zero_ctx_tail(dpx, t_lat):
    wb = 512
    n_ctx = (dpx.shape[0] - t_lat) // TM

    def body(dpx_hbm, o_ref):
        o_ref[...] = jnp.zeros_like(o_ref)

    return pl.pallas_call(
        body, name="zero_ctx_tail", grid=(n_ctx, (IN_COLS - KV_COLS) // wb), input_output_aliases={0: 0},
        in_specs=[ANY], out_specs=pl.BlockSpec((TM, wb), lambda i, j: (t_lat // TM + i, KV_COLS // wb + j)),
        out_shape=SDS(dpx.shape, dpx.dtype),
        compiler_params=_cp(("parallel", "parallel")))(dpx)


def _ret_specs(row_f, row_b):
    c = RET_CHUNK
    wq = RET_HEADS * RET_DK // 2
    wv = RET_HEADS * RET_DV // 2
    specs = []
    for row in (row_f, row_b):
        specs += [pl.BlockSpec((c, wq), lambda b, n, row=row: (row(b, n), C_RQ // wq)),
                  pl.BlockSpec((c, wq), lambda b, n, row=row: (row(b, n), C_RQ // wq + 1)),
                  pl.BlockSpec((c, 2 * wq), lambda b, n, row=row: (row(b, n), C_RK // (2 * wq))),
                  pl.BlockSpec((c, wv), lambda b, n, row=row: (row(b, n), C_RV // wv)),
                  pl.BlockSpec((c, wv), lambda b, n, row=row: (row(b, n), C_RV // wv + 1))]
    return specs


def _ret_head(refs, h):
    q0, q1, k_ref, v0, v1 = refs
    hh = h % 2
    q = (q0, q1)[h // 2][:, hh * RET_DK:(hh + 1) * RET_DK].astype(F32)
    k = k_ref[:, h * RET_DK:(h + 1) * RET_DK].astype(F32) * (RET_DK ** -0.5)
    v = (v0, v1)[h // 2][:, hh * RET_DV:(hh + 1) * RET_DV]
    return q, k, v


def _ret_fwd(px, states0, lg, n_samp, seq, comm=None):
    c = RET_CHUNK
    nc = seq // c
    t_lat = n_samp * seq
    wo = RET_HEADS * RET_DV

    def row_f(b, n):
        return b * nc + n

    def row_b(b, n):
        return b * nc + nc - 1 - n

    def body(lg_ref, *refs):
        ins, (s0_ref, of_ref, ob_ref, st_ref, s_s) = refs[:10], refs[10:]

        @pl.when(pl.program_id(1) == 0)
        def _():
            s_s[...] = s0_ref[...]

        for d, o_ref in ((0, of_ref), (1, ob_ref)):
            for h in range(RET_HEADS):
                _, _, mask, qd, kd, gc = _decays(lg_ref[d, h], d)
                q, k, v = _ret_head(ins[5 * d:5 * d + 5], h)
                s = s_s[d, h]
                st_ref[h, d] = s.astype(BF)
                sc = _dot(q, k, 1, 1) * mask
                o_ref[:, h * RET_DV:(h + 1) * RET_DV] = (_dot(sc, v) + _dot(q * qd, s)).astype(BF)
                s_s[d, h] = s * gc + _dot(k * kd, v, 0, 0)

    return _call(
        body, [lg] + [px] * 10 + [states0], comm, name="ret_fwd", grid=(n_samp, nc),
        in_specs=[SMEM] + _ret_specs(row_f, row_b) + [
            pl.BlockSpec((None, 2, RET_HEADS, RET_DK, RET_DV), lambda b, n: (b, 0, 0, 0, 0))],
        out_specs=(pl.BlockSpec((c, wo), lambda b, n: (row_f(b, n), 0)),
                   pl.BlockSpec((c, wo), lambda b, n: (row_b(b, n), 0)),
                   pl.BlockSpec((None, RET_HEADS, 2, None, RET_DK, RET_DV), lambda b, n: (b, 0, 0, n, 0, 0))),
        out_shape=(SDS((t_lat, wo), BF), SDS((t_lat, wo), BF),
                   SDS((n_samp, RET_HEADS, 2, nc, RET_DK, RET_DV), BF)),
        scratch_shapes=[pltpu.VMEM((2, RET_HEADS, RET_DK, RET_DV), F32)],
        compiler_params=_cp(("arbitrary", "arbitrary"), 48))


def _ret_bwd(px, do, saved, lg, n_samp, seq, comm=None):
    c = RET_CHUNK
    nc = seq // c
    t_lat = n_samp * seq
    wq, wo = RET_HEADS * RET_DK, RET_HEADS * RET_DV

    def row_f(b, n):
        return b * nc + nc - 1 - n

    def row_b(b, n):
        return b * nc + n

    def body(lg_ref, *refs):
        ins = refs[:10]
        (dof_ref, dob_ref, st_ref, dqf, dkf, dvf, dqb, dkb, dvb, ds0_ref, dlg_ref, ds_s, acc_s) = refs[10:]
        n = pl.program_id(1)

        @pl.when(n == 0)
        def _():
            ds_s[...] = jnp.zeros_like(ds_s)
            acc_s[...] = jnp.zeros_like(acc_s)

        for d, (do_ref, dq_ref, dk_ref, dv_ref) in enumerate(((dof_ref, dqf, dkf, dvf), (dob_ref, dqb, dkb, dvb))):
            for h in range(RET_HEADS):
                a_i, rel, mask, qd, kd, gc = _decays(lg_ref[d, h], d)
                q, k, v = _ret_head(ins[5 * d:5 * d + 5], h)
                qb, kb, vb = q.astype(BF), k.astype(BF), v.astype(BF)
                dob = do_ref[:, h * RET_DV:(h + 1) * RET_DV].astype(BF)
                sb = st_ref[h, d]
                ds = ds_s[d, h]
                dsb = ds.astype(BF)
                raw = _dot(qb, kb, 1, 1)
                sc = raw * mask
                dsc = _dot(dob, vb, 1, 1) * mask
                dscb = dsc.astype(BF)
                x = _dot(dob, sb, 1, 1)
                y = _dot(vb, dsb, 1, 1)
                qq = q * qd
                kk = k * kd
                dq_ref[:, h * RET_DK:(h + 1) * RET_DK] = (_dot(dscb, kb) + x * qd).astype(BF)
                dk_ref[:, h * RET_DK:(h + 1) * RET_DK] = (_dot(dscb, qb, 0, 0) + y * kd).astype(BF)
                dv_ref[:, h * RET_DV:(h + 1) * RET_DV] = (_dot(sc, dob, 0, 0) + _dot(kk, dsb)).astype(BF)
                t = (_sum_all(dsc * raw * rel) + _sum_all((a_i + 1.0) * qq * x)
                     + _sum_all((c - 1.0 - a_i) * kk * y) + c * gc * _sum_all(ds * sb.astype(F32)))
                acc_s[4 * d + h:4 * d + h + 1, :] += t
                ds_s[d, h] = ds * gc + _dot(qq, dob, 0, 0)

        @pl.when(n == nc - 1)
        def _():
            ds0_ref[...] = ds_s[...]
            dlg_ref[...] = acc_s[...]

    do_spec_f = pl.BlockSpec((c, wo), lambda b, n: (row_f(b, n), 0))
    do_spec_b = pl.BlockSpec((c, wo), lambda b, n: (row_b(b, n), 0))
    dq_spec_f = pl.BlockSpec((c, wq), lambda b, n: (row_f(b, n), 0))
    dq_spec_b = pl.BlockSpec((c, wq), lambda b, n: (row_b(b, n), 0))
    return _call(
        body, [lg] + [px] * 10 + [do, do, saved], comm, name="ret_bwd", grid=(n_samp, nc),
        in_specs=[SMEM] + _ret_specs(row_f, row_b) + [
            do_spec_f, do_spec_b,
            pl.BlockSpec((None, RET_HEADS, 2, None, RET_DK, RET_DV), lambda b, n: (b, 0, 0, nc - 1 - n, 0, 0))],
        out_specs=(dq_spec_f, dq_spec_f, do_spec_f, dq_spec_b, dq_spec_b, do_spec_b,
                   pl.BlockSpec((None, 2, RET_HEADS, RET_DK, RET_DV), lambda b, n: (b, 0, 0, 0, 0)),
                   pl.BlockSpec((None, 8, 128), lambda b, n: (b, 0, 0))),
        out_shape=(SDS((t_lat, wq), BF), SDS((t_lat, wq), BF), SDS((t_lat, wo), BF),
                   SDS((t_lat, wq), BF), SDS((t_lat, wq), BF), SDS((t_lat, wo), BF),
                   SDS((n_samp, 2, RET_HEADS, RET_DK, RET_DV), F32), SDS((n_samp, 8, 128), F32)),
        scratch_shapes=[pltpu.VMEM((2, RET_HEADS, RET_DK, RET_DV), F32), pltpu.VMEM((8, 128), F32)],
        compiler_params=_cp(("arbitrary", "arbitrary"), 56))


def _combine_into(dpx, a, b, col0, scale):
    t_lat, width = a.shape
    wb = 512
    assert col0 % wb == 0 and width % wb == 0

    def body(dpx_hbm, a_ref, b_ref, o_ref):
        o_ref[...] = ((a_ref[...].astype(F32) + b_ref[...].astype(F32)) * scale).astype(BF)

    src = pl.BlockSpec((TM, wb), lambda i, j: (i, j))
    return pl.pallas_call(
        body, name="combine_into", grid=(t_lat // TM, width // wb), input_output_aliases={0: 0},
        in_specs=[ANY, src, src], out_specs=pl.BlockSpec((TM, wb), lambda i, j: (i, col0 // wb + j)),
        out_shape=SDS(dpx.shape, dpx.dtype),
        compiler_params=_cp(("parallel", "parallel")))(dpx, a, b)


def _retnorm_fwd(o_f, o_b, px):
    t_lat = o_f.shape[0]

    def body(of_ref, ob_ref, g_ref, y_ref):
        o = of_ref[...].astype(F32) + ob_ref[...].astype(F32)
        g = g_ref[...].astype(F32)
        y_ref[...] = (o * _rms(o) * (g * _sigmoid(g))).astype(BF)

    so = pl.BlockSpec((TM, RET_DV), lambda i, h: (i, h))
    return pl.pallas_call(
        body, name="retnorm_fwd", grid=(t_lat // TM, RET_HEADS),
        in_specs=[so, so, pl.BlockSpec((TM, RET_DV), lambda i, h: (i, C_RG // RET_DV + h))],
        out_specs=so,
        out_shape=SDS((t_lat, RET_HEADS * RET_DV), BF),
        compiler_params=_cp(("parallel", "parallel")))(o_f, o_b, px)


def _retnorm_bwd(dpx, dy, o_f, o_b, px):
    t_lat = o_f.shape[0]

    def body(dpx_hbm, dy_ref, of_ref, ob_ref, g_ref, do_ref, dg_ref):
        o = of_ref[...].astype(F32) + ob_ref[...].astype(F32)
        r = _rms(o)
        on = o * r
        g = g_ref[...].astype(F32)
        sg = _sigmoid(g)
        dy_ = dy_ref[...].astype(F32)
        dg_ref[...] = (dy_ * on * (sg * (1.0 + g * (1.0 - sg)))).astype(BF)
        do_ref[...] = _rms_bwd(dy_ * (g * sg), on, r).astype(BF)

    so = pl.BlockSpec((TM, RET_DV), lambda i, h: (i, h))
    gcol = pl.BlockSpec((TM, RET_DV), lambda i, h: (i, C_RG // RET_DV + h))
    return pl.pallas_call(
        body, name="retnorm_bwd", grid=(t_lat // TM, RET_HEADS), input_output_aliases={0: 1},
        in_specs=[ANY, so, so, so, gcol],
        out_specs=(so, gcol),
        out_shape=(SDS((t_lat, RET_HEADS * RET_DV), BF), SDS(dpx.shape, dpx.dtype)),
        compiler_params=_cp(("parallel", "parallel")))(dpx, dy, o_f, o_b, px)


def _norm_rope(x, w, cos, sin):
    xn = x * _rms(x) * w
    return xn * cos + _swap_pairs(xn) * sin


def _norm_rope_bwd(dy, x, w, cos, sin):
    dxn = dy * cos + _swap_pairs(dy * sin)
    r = _rms(x)
    xh = x * r
    return _rms_bwd(dxn * w, xh, r), jnp.sum(dxn * xh, axis=0, keepdims=True)


def _att_prep_q(px, cos_all, sin_all, qnw, t_lat):
    hd = ATT_HEAD_DIM
    wblk = ATT_REP * hd

    def body(x_ref, cos_ref, sin_ref, w_ref, o_ref):
        for r in range(ATT_REP):
            cols = slice(r * hd, (r + 1) * hd)
            qr = _norm_rope(x_ref[:, cols].astype(F32), w_ref[...], cos_ref[...], sin_ref[...])
            o_ref[:, cols] = (qr * (hd ** -0.5)).astype(BF)

    return pl.pallas_call(
        body, name="att_prep_q", grid=(t_lat // TM, ATT_KV_HEADS),
        in_specs=[pl.BlockSpec((TM, wblk), lambda i, g: (i, C_AQ // wblk + g)),
                  pl.BlockSpec((TM, hd), lambda i, g: (i, 0)),
                  pl.BlockSpec((TM, hd), lambda i, g: (i, 0)),
                  pl.BlockSpec((1, hd), lambda i, g: (0, 0))],
        out_specs=pl.BlockSpec((TM, wblk), lambda i, g: (i, g)),
        out_shape=SDS((t_lat, ATT_HEADS * hd), BF),
        compiler_params=_cp(("parallel", "parallel")))(px, cos_all, sin_all, qnw)


def _att_prep_kv(px, cos_all, sin_all, knw):
    rows = px.shape[0]
    hd = ATT_HEAD_DIM
    kvw = ATT_KV_HEADS * hd

    def body(x_ref, cos_ref, sin_ref, w_ref, k_ref, v_ref):
        for g in range(ATT_KV_HEADS):
            cols = slice(g * hd, (g + 1) * hd)
            k_ref[:, cols] = _norm_rope(x_ref[:, cols].astype(F32), w_ref[...], cos_ref[...],
                                        sin_ref[...]).astype(BF)
        v_ref[...] = x_ref[:, kvw:].astype(BF)

    return pl.pallas_call(
        body, name="att_prep_kv", grid=(rows // TM,),
        in_specs=[pl.BlockSpec((TM, 2 * kvw), lambda i: (i, C_AK // (2 * kvw))),
                  pl.BlockSpec((TM, hd), lambda i: (i, 0)),
                  pl.BlockSpec((TM, hd), lambda i: (i, 0)),
                  pl.BlockSpec((1, hd), lambda i: (0, 0))],
        out_specs=(pl.BlockSpec((TM, kvw), lambda i: (i, 0)), pl.BlockSpec((TM, kvw), lambda i: (i, 0))),
        out_shape=(SDS((rows, kvw), BF), SDS((rows, kvw), BF)),
        compiler_params=_cp(("parallel",)))(px, cos_all, sin_all, knw)


def _att_kv_bwd(dpx, dkl, dkc, dvl, dvc, px, cos_all, sin_all, knw):
    rows = px.shape[0]
    hd = ATT_HEAD_DIM
    kvw = ATT_KV_HEADS * hd
    n_lat = dkl.shape[0] // TM
    assert dkc.shape[0] == TM

    def body(dpx_hbm, dkl_ref, dkc_ref, dvl_ref, dvc_ref, x_ref, cos_ref, sin_ref, w_ref, o_ref, gw_ref):
        i = pl.program_id(0)

        @pl.when(i == 0)
        def _():
            gw_ref[...] = jnp.zeros_like(gw_ref)

        is_lat = i < n_lat
        dk = jnp.where(is_lat, dkl_ref[...], dkc_ref[...])
        dv = jnp.where(is_lat, dvl_ref[...], dvc_ref[...])
        for g in range(ATT_KV_HEADS):
            cols = slice(g * hd, (g + 1) * hd)
            dx, gw = _norm_rope_bwd(dk[:, cols], x_ref[:, cols].astype(F32), w_ref[...], cos_ref[...], sin_ref[...])
            o_ref[:, cols] = dx.astype(BF)
            gw_ref[...] += gw
        o_ref[:, kvw:] = dv.astype(BF)

    lat = pl.BlockSpec((TM, kvw), lambda i: (jnp.minimum(i, n_lat - 1), 0))
    ctx = pl.BlockSpec((TM, kvw), lambda i: (0, 0))
    kvcol = pl.BlockSpec((TM, 2 * kvw), lambda i: (i, C_AK // (2 * kvw)))
    return pl.pallas_call(
        body, name="att_kv_bwd", grid=(rows // TM,), input_output_aliases={0: 0},
        in_specs=[ANY, lat, ctx, lat, ctx, kvcol,
                  pl.BlockSpec((TM, hd), lambda i: (i, 0)),
                  pl.BlockSpec((TM, hd), lambda i: (i, 0)),
                  pl.BlockSpec((1, hd), lambda i: (0, 0))],
        out_specs=(kvcol, pl.BlockSpec((1, hd), lambda i: (0, 0))),
        out_shape=(SDS(dpx.shape, dpx.dtype), SDS((1, hd), F32)),
        compiler_params=_cp(("arbitrary",)))(dpx, dkl, dkc, dvl, dvc, px, cos_all, sin_all, knw)


def _stack_heads(ref_or_val):
    hd = ATT_HEAD_DIM
    return jnp.concatenate([ref_or_val[:, r * hd:(r + 1) * hd] for r in range(ATT_REP)], axis=0)


def _att_scores(q, kl, kc):
    sl = _dot(q, kl, 1, 1)
    sc = _dot(q, kc, 1, 1)
    m = jnp.maximum(jnp.max(sl, axis=-1, keepdims=True), jnp.max(sc, axis=-1, keepdims=True))
    el = jnp.exp(sl - m)
    ec = jnp.exp(sc - m)
    denom = jnp.sum(el, axis=-1, keepdims=True) + jnp.sum(ec, axis=-1, keepdims=True)
    return el, ec, denom, m


def _att_fwd(qn, kn, vn, n_samp, seq, lc):
    hd = ATT_HEAD_DIM
    tq = ATT_TQ
    nq = seq // tq
    wblk = ATT_REP * hd
    cb = n_samp * seq // lc
    t_lat = n_samp * seq

    def body(q_ref, kl_ref, kc_ref, vl_ref, vc_ref, o_ref, lse_ref):
        lane = lax.broadcasted_iota(jnp.int32, (tq, hd), 1)
        lse = jnp.zeros((tq, hd), F32)
        for r in range(ATT_REP):
            cols = slice(r * hd, (r + 1) * hd)
            el, ec, denom, m = _att_scores(q_ref[:, cols], kl_ref[...], kc_ref[...])
            o_ref[:, cols] = ((_dot(el, vl_ref[...]) + _dot(ec, vc_ref[...])) / denom).astype(BF)
            lse = jnp.where(lane == r, m + jnp.log(denom), lse)
        lse_ref[...] = lse

    return pl.pallas_call(
        body, name="att_fwd", grid=(n_samp, ATT_KV_HEADS, nq),
        in_specs=[pl.BlockSpec((tq, wblk), lambda b, g, i: (b * nq + i, g)),
                  pl.BlockSpec((seq, hd), lambda b, g, i: (b, g)),
                  pl.BlockSpec((lc, hd), lambda b, g, i: (cb + b, g)),
                  pl.BlockSpec((seq, hd), lambda b, g, i: (b, g)),
                  pl.BlockSpec((lc, hd), lambda b, g, i: (cb + b, g))],
        out_specs=(pl.BlockSpec((tq, wblk), lambda b, g, i: (b * nq + i, g)),
                   pl.BlockSpec((tq, hd), lambda b, g, i: (b * nq + i, g))),
        out_shape=(SDS((t_lat, ATT_HEADS * hd), BF), SDS((t_lat, ATT_KV_HEADS * hd), F32)),
        compiler_params=_cp(("parallel", "parallel", "parallel"), 48))(qn, kn, kn, vn, vn)


def _att_gate_bwd(dpx, dy_att, o_att, px):
    t_lat = dy_att.shape[0]
    wblk = ATT_REP * ATT_HEAD_DIM

    def body(dpx_hbm, dy_ref, o_ref, g_ref, out_ref):
        g = g_ref[...].astype(F32)
        sg = _sigmoid(g)
        out_ref[...] = (dy_ref[...].astype(F32) * o_ref[...].astype(F32) * (sg * (1.0 + g * (1.0 - sg)))).astype(BF)

    blk = pl.BlockSpec((TM, wblk), lambda i, j: (i, j))
    gcol = pl.BlockSpec((TM, wblk), lambda i, j: (i, C_AG // wblk + j))
    return pl.pallas_call(
        body, name="att_gate_bwd", grid=(t_lat // TM, ATT_KV_HEADS),
        in_specs=[ANY, blk, blk, gcol], out_specs=gcol, out_shape=SDS(dpx.shape, dpx.dtype),
        input_output_aliases={0: 0},
        compiler_params=_cp(("parallel", "parallel")))(dpx, dy_att, o_att, px)


def _att_bwd(dpx, qn, kn, vn, px, o_att, lse, do_att, cos_all, sin_all, qnw, n_samp, seq, lc, comm=None):
    hd = ATT_HEAD_DIM
    tq = ATT_TQ
    nq = seq // tq
    wblk = ATT_REP * hd
    cb = n_samp * seq // lc
    t_lat = n_samp * seq
    kvw = ATT_KV_HEADS * hd
    scale = hd ** -0.5

    def body(dpx_hbm, q_ref, kl_ref, kc_ref, vl_ref, vc_ref, o_ref, do_ref, x_ref, cos_ref, sin_ref, w_ref,
             lse_ref, dq_ref, dkl_ref, dkc_ref, dvl_ref, dvc_ref, gw_ref, akl, akc, avl, avc, aw):
        i = pl.program_id(2)

        @pl.when(i == 0)
        def _():
            akl[...] = jnp.zeros_like(akl)
            akc[...] = jnp.zeros_like(akc)
            avl[...] = jnp.zeros_like(avl)
            avc[...] = jnp.zeros_like(avc)
            aw[...] = jnp.zeros_like(aw)

        dobs, pls, pcs, dsls, dscs = [], [], [], [], []
        for r in range(ATT_REP):
            cols = slice(r * hd, (r + 1) * hd)
            dob = do_ref[:, cols]
            delta = jnp.sum(dob.astype(F32) * o_ref[:, cols].astype(F32), axis=-1, keepdims=True)
            lse = lse_ref[:, r:r + 1]
            p_l = jnp.exp(_dot(q_ref[:, cols], kl_ref[...], 1, 1) - lse).astype(BF)
            p_c = jnp.exp(_dot(q_ref[:, cols], kc_ref[...], 1, 1) - lse).astype(BF)
            ds_l = (p_l * (_dot(dob, vl_ref[...], 1, 1) - delta)).astype(BF)
            ds_c = (p_c * (_dot(dob, vc_ref[...], 1, 1) - delta)).astype(BF)
            dq = (_dot(ds_l, kl_ref[...]) + _dot(ds_c, kc_ref[...])) * scale
            dx, gw = _norm_rope_bwd(dq, x_ref[:, cols].astype(F32), w_ref[...], cos_ref[...], sin_ref[...])
            dq_ref[:, cols] = dx.astype(BF)
            aw[...] += gw
            dobs.append(dob)
            pls.append(p_l)
            pcs.append(p_c)
            dsls.append(ds_l)
            dscs.append(ds_c)
        do4 = jnp.concatenate(dobs, axis=0)
        q4 = _stack_heads(q_ref)
        avl[...] += _dot(jnp.concatenate(pls, axis=0), do4, 0, 0)
        avc[...] += _dot(jnp.concatenate(pcs, axis=0), do4, 0, 0)
        akl[...] += _dot(jnp.concatenate(dsls, axis=0), q4, 0, 0)
        akc[...] += _dot(jnp.concatenate(dscs, axis=0), q4, 0, 0)

        @pl.when(i == nq - 1)
        def _():
            dkl_ref[...] = akl[...]
            dkc_ref[...] = akc[...]
            dvl_ref[...] = avl[...]
            dvc_ref[...] = avc[...]
            gw_ref[...] = aw[...]

    return _call(
        body, [dpx, qn, kn, kn, vn, vn, o_att, do_att, px, cos_all, sin_all, qnw, lse], comm,
        name="att_bwd", grid=(n_samp, ATT_KV_HEADS, nq), aliases={0: 0},
        in_specs=[ANY,
                  pl.BlockSpec((tq, wblk), lambda b, g, i: (b * nq + i, g)),
                  pl.BlockSpec((seq, hd), lambda b, g, i: (b, g)),
                  pl.BlockSpec((lc, hd), lambda b, g, i: (cb + b, g)),
                  pl.BlockSpec((seq, hd), lambda b, g, i: (b, g)),
                  pl.BlockSpec((lc, hd), lambda b, g, i: (cb + b, g)),
                  pl.BlockSpec((tq, wblk), lambda b, g, i: (b * nq + i, g)),
                  pl.BlockSpec((tq, wblk), lambda b, g, i: (b * nq + i, g)),
                  pl.BlockSpec((tq, wblk), lambda b, g, i: (b * nq + i, C_AQ // wblk + g)),
                  pl.BlockSpec((tq, hd), lambda b, g, i: (b * nq + i, 0)),
                  pl.BlockSpec((tq, hd), lambda b, g, i: (b * nq + i, 0)),
                  pl.BlockSpec((1, hd), lambda b, g, i: (0, 0)),
                  pl.BlockSpec((tq, hd), lambda b, g, i: (b * nq + i, g))],
        out_specs=(pl.BlockSpec((tq, wblk), lambda b, g, i: (b * nq + i, C_AQ // wblk + g)),
                   pl.BlockSpec((seq, hd), lambda b, g, i: (b, g)),
                   pl.BlockSpec((lc, hd), lambda b, g, i: (b, g)),
                   pl.BlockSpec((seq, hd), lambda b, g, i: (b, g)),
                   pl.BlockSpec((lc, hd), lambda b, g, i: (b, g)),
                   pl.BlockSpec((None, None, 1, hd), lambda b, g, i: (b, g, 0, 0))),
        out_shape=(SDS(dpx.shape, dpx.dtype),
                   SDS((t_lat, kvw), F32), SDS((n_samp * lc, kvw), F32),
                   SDS((t_lat, kvw), F32), SDS((n_samp * lc, kvw), F32),
                   SDS((n_samp, ATT_KV_HEADS, 1, hd), F32)),
        scratch_shapes=[pltpu.VMEM((seq, hd), F32), pltpu.VMEM((lc, hd), F32),
                        pltpu.VMEM((seq, hd), F32), pltpu.VMEM((lc, hd), F32), pltpu.VMEM((1, hd), F32)],
        compiler_params=_cp(("arbitrary", "arbitrary", "arbitrary"), 56))


def _merge(x_lat, target, o_f, o_b, o_att, px, gate3, w_o_ret, w_o_att, w_out, tiles_per_sample):
    t_lat = x_lat.shape[0]
    tm = 256
    n_t = t_lat // tm
    per = tiles_per_sample * (TM // tm)
    d = D_MODEL
    rv = RET_HEADS * RET_DV
    n_samp = gate3.shape[0] - 1

    half = d // 2
    n_px = 10

    def body(x_ref, t_ref, of_ref, ob_ref, oa_ref, *rest):
        pxs, rest = rest[:n_px], rest[n_px:]
        (gt_ref, wor_ref, woa_ref, wout_ref,
         gx_ref, dor_ref, doa_ref, dpx_hbm, loss_ref, dgt_ref, gwor_hbm, gwoa_hbm, gwout_hbm,
         aor, aoa, aout, drg_ref, dtail_ref, sems) = rest
        i = pl.program_id(0)

        def copies(step):
            rows = pl.ds(pl.multiple_of(step * tm, tm), tm)
            return (pltpu.make_async_copy(drg_ref, dpx_hbm.at[rows, pl.ds(C_RG, rv)], sems.at[0]),
                    pltpu.make_async_copy(dtail_ref, dpx_hbm.at[rows, pl.ds(C_AG, 3 * d)], sems.at[1]))

        @pl.when(i == 0)
        def _():
            aor[...] = jnp.zeros_like(aor)
            aoa[...] = jnp.zeros_like(aoa)
            aout[...] = jnp.zeros_like(aout)
            loss_ref[...] = jnp.zeros_like(loss_ref)

        @pl.when(i % per == 0)
        def _():
            dgt_ref[...] = jnp.zeros_like(dgt_ref)

        def cat(refs):
            return jnp.concatenate([r[...] for r in refs], axis=1).astype(F32)

        def ret_head(h):
            cols = slice(h * RET_DV, (h + 1) * RET_DV)
            o = of_ref[:, cols].astype(F32) + ob_ref[:, cols].astype(F32)
            r = _rms(o)
            g = pxs[h][...].astype(F32)
            return o * r, r, g, _sigmoid(g)

        def att_half(k):
            o = oa_ref[:, k * half:(k + 1) * half].astype(F32)
            g = pxs[4 + k][...].astype(F32)
            return o, g, _sigmoid(g)

        yrs = []
        for h in range(RET_HEADS):
            on, _, g, sg = ret_head(h)
            yrs.append((on * (g * sg)).astype(BF))
        yr = jnp.concatenate(yrs, axis=1)
        yas = []
        for k in range(2):
            o, g, sg = att_half(k)
            yas.append((o * (g * sg)).astype(BF))
        ya = jnp.concatenate(yas, axis=1)

        a = jnp.dot(yr, wor_ref[...], preferred_element_type=F32)
        b = jnp.dot(ya, woa_ref[...], preferred_element_type=F32)
        sr = _sigmoid(cat(pxs[6:8]))
        sa = _sigmoid(cat(pxs[8:10]))
        yb = (sr * a + sa * b).astype(BF)
        out = jnp.dot(yb, wout_ref[...], preferred_element_type=F32)
        gate = gt_ref[...]
        err = x_ref[...] + gate * out - t_ref[...]
        loss_ref[...] += 0.5 * _sum_all(err * err) * (1.0 / d)
        dy_tok = err * (1.0 / d)
        gx_ref[...] = dy_tok
        dgt_ref[...] += jnp.sum(dy_tok * out, axis=0, keepdims=True)
        dout = (dy_tok * gate).astype(BF)
        aout[...] += _dot(yb, dout, 0, 0)
        dyy = _dot(dout, wout_ref[...], 1, 1)
        da = (dyy * sr).astype(BF)
        db = (dyy * sa).astype(BF)
        aor[...] += _dot(yr, da, 0, 0)
        aoa[...] += _dot(ya, db, 0, 0)
        dyr = _dot(da, wor_ref[...], 1, 1)
        dya = _dot(db, woa_ref[...], 1, 1)

        @pl.when(i > 0)
        def _():
            for cp in copies(i - 1):
                cp.wait()

        dtail_ref[:, d:2 * d] = (dyy * a * (sr * (1.0 - sr))).astype(BF)
        dtail_ref[:, 2 * d:] = (dyy * b * (sa * (1.0 - sa))).astype(BF)
        for h in range(RET_HEADS):
            cols = slice(h * RET_DV, (h + 1) * RET_DV)
            on, r, g, sg = ret_head(h)
            dy = dyr[:, cols]
            drg_ref[:, cols] = (dy * on * (sg * (1.0 + g * (1.0 - sg)))).astype(BF)
            dor_ref[:, cols] = _rms_bwd(dy * (g * sg), on, r).astype(BF)
        for k in range(2):
            cols = slice(k * half, (k + 1) * half)
            o, g, sg = att_half(k)
            dy = dya[:, cols]
            dtail_ref[:, cols] = (dy * o * (sg * (1.0 + g * (1.0 - sg)))).astype(BF)
            doa_ref[:, cols] = (dy * (g * sg)).astype(BF)
        for cp in copies(i):
            cp.start()

        @pl.when(i == n_t - 1)
        def _():
            for cp in copies(i):
                cp.wait()
            pltpu.sync_copy(aor, gwor_hbm)
            pltpu.sync_copy(aoa, gwoa_hbm)
            pltpu.sync_copy(aout, gwout_hbm)

    def px_blk(col):
        return pl.BlockSpec((tm, half), lambda i: (i, col // half))

    def resident(shape):
        return pl.BlockSpec(shape, lambda i: (0, 0), pipeline_mode=pl.Buffered(1))

    px_cols = ([C_RG + k * half for k in range(4)] + [C_AG, C_AG + half]
               + [C_MR, C_MR + half, C_MA, C_MA + half])
    return pl.pallas_call(
        body, name="merge", grid=(n_t,),
        in_specs=[pl.BlockSpec((tm, d), lambda i: (i, 0)),
                  pl.BlockSpec((tm, d), lambda i: (i, 0)),
                  pl.BlockSpec((tm, rv), lambda i: (i, 0)),
                  pl.BlockSpec((tm, rv), lambda i: (i, 0)),
                  pl.BlockSpec((tm, d), lambda i: (i, 0))]
        + [px_blk(col) for col in px_cols]
        + [pl.BlockSpec((None, 1, d), lambda i: (i // per, 0, 0)),
           resident((rv, d)), resident((d, d)), resident((d, d))],
        out_specs=(pl.BlockSpec((tm, d), lambda i: (i, 0)),
                   pl.BlockSpec((tm, rv), lambda i: (i, 0)),
                   pl.BlockSpec((tm, d), lambda i: (i, 0)),
                   ANY,
                   pl.BlockSpec((8, 128), lambda i: (0, 0)),
                   pl.BlockSpec((None, 1, d), lambda i: (i // per, 0, 0)),
                   ANY, ANY, ANY),
        out_shape=(SDS((t_lat, d), F32), SDS((t_lat, rv), BF), SDS((t_lat, d), BF),
                   SDS((px.shape[0], IN_COLS), BF),
                   SDS((8, 128), F32), SDS((n_samp, 1, d), F32),
                   SDS((rv, d), F32), SDS((d, d), F32), SDS((d, d), F32)),
        scratch_shapes=[pltpu.VMEM((rv, d), F32), pltpu.VMEM((d, d), F32), pltpu.VMEM((d, d), F32),
                        pltpu.VMEM((tm, rv), BF), pltpu.VMEM((tm, 3 * d), BF), pltpu.SemaphoreType.DMA((2,))],
        compiler_params=_cp(("arbitrary",), 56))(
            x_lat, target, o_f, o_b, o_att, *([px] * n_px), gate3, w_o_ret, w_o_att, w_out)


def _place():
    x, y, c = lax.axis_index("x"), lax.axis_index("y"), lax.axis_index("c")
    chips = [(1 - x, y), (x, 1 - y), (1 - x, 1 - y)]
    return x, y, c, chips


def _remote(src, dst, send_sem, recv_sem, to):
    return pltpu.make_async_remote_copy(src_ref=src, dst_ref=dst, send_sem=send_sem, recv_sem=recv_sem,
                                        device_id=to, device_id_type=MESH)


def _place_ids():
    x, y, c = lax.axis_index("x"), lax.axis_index("y"), lax.axis_index("c")
    me = 2 * x + y
    return jnp.stack([x, y, c, me, me, 2 * (1 - x) + y, 2 * x + 1 - y, 2 * (1 - x) + 1 - y]).astype(jnp.int32)


def _ag_comm(bufs, rels, arg_index=None):
    n, m = len(bufs), len(rels)

    def half(ref, s, which):
        h = ref.shape[1] // 2
        return ref.at[s, pl.ds(which * h, h), :]

    def ici(ins, outs, ssem, rsem, base):
        x, y, c, chips = _place()
        sends, recvs = [], []
        for a in range(n):
            for jj, j in enumerate(rels):
                k, chip = base + a * m + jj, chips[j]
                mine, theirs = half(outs[a], 2 * x + y, c), half(outs[a], 2 * chip[0] + chip[1], c)
                sends.append(_remote(mine, mine, ssem.at[k], rsem.at[k], (*chip, c)))
                recvs.append(_remote(theirs, theirs, ssem.at[k], rsem.at[k], (*chip, c)))
        return sends, recvs

    def d2d(ins, outs, ssem, rsem, base):
        x, y, c, chips = _place()
        sends, recvs = [], []
        for a in range(n):
            for jj, j in enumerate(rels):
                k, s = base + (n + a) * m + jj, 2 * chips[j][0] + chips[j][1]
                sends.append(_remote(half(outs[a], s, c), half(outs[a], s, c), ssem.at[k], rsem.at[k], (x, y, 1 - c)))
                recvs.append(_remote(half(outs[a], s, 1 - c), half(outs[a], s, 1 - c), ssem.at[k], rsem.at[k],
                                     (x, y, 1 - c)))
        return sends, recvs

    shapes = tuple(SDS(b.shape, b.dtype) for b in bufs)
    if arg_index is not None:
        return _Comm("all_gather", (), shapes, {}, 2 * n * m, (ici, d2d), ((arg_index, 0),))
    return _Comm("all_gather", tuple(bufs), shapes, {a: a for a in range(n)}, 2 * n * m, (ici, d2d))


def _swap_comm(grads):
    n = len(grads)

    def phase(ins, outs, ssem, rsem, base):
        x, y, c, _ = _place()
        sends = []
        for a in range(n):
            h = ins[a].shape[1] // 2
            sends.append(_remote(ins[a].at[:, pl.ds((1 - c) * h, h), :], outs[a], ssem.at[base + a],
                                 rsem.at[base + a], (x, y, 1 - c)))
        return sends, sends

    return _Comm("swap_halves", tuple(grads),
                 tuple(SDS((g.shape[0], g.shape[1] // 2, g.shape[2]), g.dtype) for g in grads), {}, n, (phase,))


def _exchange_comm(parts):
    n = len(parts)

    def phase(ins, outs, ssem, rsem, base):
        x, y, c, chips = _place()
        sends = []
        for a in range(n):
            for j, chip in enumerate(chips):
                k = base + 3 * a + j
                sends.append(_remote(ins[a].at[2 * chip[0] + chip[1]], outs[a].at[j], ssem.at[k], rsem.at[k],
                                     (*chip, c)))
        return sends, sends

    return _Comm("exchange_shards", tuple(parts), tuple(SDS((3,) + p.shape[1:], p.dtype) for p in parts), {}, 3 * n,
                 (phase,))


def _join_comm(bufs, n_parts=1):
    n = len(bufs)

    def phase(ins, outs, ssem, rsem, base):
        x, y, c, _ = _place()
        sends, recvs = [], []
        for a in range(n):
            h = outs[a].shape[0] // (2 * n_parts)
            for p in range(n_parts):
                k = base + a * n_parts + p
                mine = outs[a].at[pl.ds((2 * p + c) * h, h), :]
                other = outs[a].at[pl.ds((2 * p + 1 - c) * h, h), :]
                sends.append(_remote(mine, mine, ssem.at[k], rsem.at[k], (x, y, 1 - c)))
                recvs.append(_remote(other, other, ssem.at[k], rsem.at[k], (x, y, 1 - c)))
        return sends, recvs

    return _Comm("join_halves", tuple(bufs), tuple(SDS(b.shape, b.dtype) for b in bufs), {a: a for a in range(n)},
                 n * n_parts, (phase,))


def _cast_place(w, ids):
    rows, cols = w.shape
    tr = min(rows, 256)

    def body(ids_ref, w_ref, o_ref):
        o_ref[...] = w_ref[...].astype(BF)

    return pl.pallas_call(
        body, name="cast_place",
        grid_spec=pltpu.PrefetchScalarGridSpec(
            num_scalar_prefetch=1, grid=(rows // tr,),
            in_specs=[pl.BlockSpec((tr, cols), lambda i, ids_ref: (i, 0))],
            out_specs=pl.BlockSpec((None, tr, cols), lambda i, ids_ref: (ids_ref[3], i, 0))),
        out_shape=SDS((N_SHARD, rows, cols), BF),
        compiler_params=_cp(("parallel",), 40))(ids, w)


def _all_gather_weights(bufs):
    n = len(bufs)

    def body(*refs):
        outs = refs[n:2 * n]
        send_sems, recv_sems = refs[2 * n:]
        x, y, c, chips = _place()
        sibling = (x, y, 1 - c)
        me = 2 * x + y

        def half(ref, s, which):
            h = ref.shape[1] // 2
            return ref.at[s, pl.ds(which * h, h), :]

        first = []
        for a in range(n):
            for j, chip in enumerate(chips):
                k = a * 3 + j
                win = half(outs[a], me, c)
                first.append(_remote(win, win, send_sems.at[k], recv_sems.at[k], (*chip, c)))
        for cp in first:
            cp.start()
        passed = []
        for a in range(n):
            for j, chip in enumerate(chips):
                k = a * 3 + j
                win = half(outs[a], 2 * chip[0] + chip[1], c)
                _remote(win, win, send_sems.at[k], recv_sems.at[k], (*chip, c)).wait_recv()
                fw = _remote(win, win, send_sems.at[3 * n + k], recv_sems.at[3 * n + k], sibling)
                fw.start()
                passed.append(fw)
        for a in range(n):
            for j, chip in enumerate(chips):
                k = a * 3 + j
                win = half(outs[a], 2 * chip[0] + chip[1], 1 - c)
                _remote(win, win, send_sems.at[3 * n + k], recv_sems.at[3 * n + k], sibling).wait_recv()
        for cp in first + passed:
            cp.wait_send()

    return pl.pallas_call(
        body, name="all_gather_weights",
        in_specs=[ANY] * n, out_specs=tuple([ANY] * n),
        out_shape=tuple(SDS(b.shape, b.dtype) for b in bufs),
        input_output_aliases={a: a for a in range(n)},
        scratch_shapes=[pltpu.SemaphoreType.DMA((6 * n,)), pltpu.SemaphoreType.DMA((6 * n,))],
        compiler_params=_cp(has_side_effects=True))(*bufs)


def _swap_halves(grads):
    n = len(grads)

    def body(*refs):
        ins, outs = refs[:n], refs[n:2 * n]
        send_sems, recv_sems = refs[2 * n:]
        x, y, c, _ = _place()
        sibling = (x, y, 1 - c)

        def half(ref, which):
            h = ref.shape[1] // 2
            return ref.at[:, pl.ds(which * h, h), :]

        sends = [_remote(half(ins[a], 1 - c), outs[a], send_sems.at[a], recv_sems.at[a], sibling)
                 for a in range(n)]
        for cp in sends:
            cp.start()
        for cp in sends:
            cp.wait_recv()
        for cp in sends:
            cp.wait_send()

    return pl.pallas_call(
        body, name="swap_halves",
        in_specs=[ANY] * n, out_specs=tuple([ANY] * n),
        out_shape=tuple(SDS((g.shape[0], g.shape[1] // 2, g.shape[2]), g.dtype) for g in grads),
        scratch_shapes=[pltpu.SemaphoreType.DMA((n,)), pltpu.SemaphoreType.DMA((n,))],
        compiler_params=_cp(has_side_effects=True))(*grads)


def _chip_sum(g, p, ids):
    n_s, rows, cols = g.shape
    h = rows // 2
    tr = min(h, 256)
    nb = h // tr

    def body(ids_ref, g_ref, p_ref, o_ref, o16_ref):
        t = g_ref[...] + p_ref[...]
        o_ref[...] = t
        o16_ref[...] = t.astype(BF)

    out_spec = pl.BlockSpec((None, tr, cols), lambda s, i, ids_ref: (s, i, 0))
    return pl.pallas_call(
        body, name="chip_sum",
        grid_spec=pltpu.PrefetchScalarGridSpec(
            num_scalar_prefetch=1, grid=(n_s, nb),
            in_specs=[pl.BlockSpec((None, tr, cols), lambda s, i, ids_ref: (s, ids_ref[2] * nb + i, 0)),
                      pl.BlockSpec((None, tr, cols), lambda s, i, ids_ref: (s, i, 0))],
            out_specs=(out_spec, out_spec)),
        out_shape=(SDS((n_s, h, cols), g.dtype), SDS((n_s, h, cols), BF)),
        compiler_params=_cp(("parallel", "parallel"), 40))(ids, g, p)


def _exchange_shards(parts):
    n = len(parts)

    def body(*refs):
        ins, outs = refs[:n], refs[n:2 * n]
        send_sems, recv_sems = refs[2 * n:]
        x, y, c, chips = _place()
        sends = []
        for a in range(n):
            for j, chip in enumerate(chips):
                k = a * 3 + j
                sends.append(_remote(ins[a].at[2 * chip[0] + chip[1]], outs[a].at[j],
                                     send_sems.at[k], recv_sems.at[k], (*chip, c)))
        for cp in sends:
            cp.start()
        for cp in sends:
            cp.wait_recv()
        for cp in sends:
            cp.wait_send()

    return pl.pallas_call(
        body, name="exchange_shards",
        in_specs=[ANY] * n, out_specs=tuple([ANY] * n),
        out_shape=tuple(SDS((3,) + p.shape[1:], p.dtype) for p in parts),
        scratch_shapes=[pltpu.SemaphoreType.DMA((3 * n,)), pltpu.SemaphoreType.DMA((3 * n,))],
        compiler_params=_cp(has_side_effects=True))(*parts)


def _shard_sum(t, q, ids, part=0, n_parts=1, buf=None):
    _, h, cols = t.shape
    tr = min(h, 256)
    nb = h // tr

    def body(ids_ref, t_ref, q_ref, *rest):
        rest[-1][...] = ((t_ref[...] + q_ref[0].astype(F32)) + q_ref[1].astype(F32)) + q_ref[2].astype(F32)

    args, in_specs, aliases = [t, q], [
        pl.BlockSpec((None, tr, cols), lambda i, ids_ref: (ids_ref[3], i, 0)),
        pl.BlockSpec((3, tr, cols), lambda i, ids_ref: (0, i, 0))], None
    if buf is not None:
        args, in_specs, aliases = args + [buf], in_specs + [ANY], {2: 0}
    return _call(body, args, None, name="shard_sum", grid=(nb,), in_specs=in_specs,
                 out_specs=pl.BlockSpec((tr, cols), lambda i, ids_ref: ((2 * part + ids_ref[2]) * nb + i, 0)),
                 out_shape=SDS((2 * h * n_parts, cols), t.dtype), aliases=aliases, prefetch=ids,
                 compiler_params=_cp(("parallel",), 40))


def _join_halves(bufs):
    n = len(bufs)

    def body(*refs):
        outs = refs[n:2 * n]
        send_sems, recv_sems = refs[2 * n:]
        x, y, c, _ = _place()
        sibling = (x, y, 1 - c)

        def win(ref, which):
            h = ref.shape[0] // 2
            return ref.at[pl.ds(which * h, h), :]

        sends = [_remote(win(outs[a], c), win(outs[a], c), send_sems.at[a], recv_sems.at[a], sibling)
                 for a in range(n)]
        for cp in sends:
            cp.start()
        for a in range(n):
            other = win(outs[a], 1 - c)
            _remote(other, other, send_sems.at[a], recv_sems.at[a], sibling).wait_recv()
        for cp in sends:
            cp.wait_send()

    return pl.pallas_call(
        body, name="join_halves",
        in_specs=[ANY] * n, out_specs=tuple([ANY] * n),
        out_shape=tuple(SDS(b.shape, b.dtype) for b in bufs),
        input_output_aliases={a: a for a in range(n)},
        scratch_shapes=[pltpu.SemaphoreType.DMA((n,)), pltpu.SemaphoreType.DMA((n,))],
        compiler_params=_cp(has_side_effects=True))(*bufs)


def _gather_small(block, n_sum):
    rows, cols = block.shape
    n_dev = 8

    def body(x_ref, o_ref, g_ref, buf, send_sems, recv_sems, local_sem):
        x, y, c, chips = _place()
        me, sibling = (x, y, c), (x, y, 1 - c)

        def slot(px_, py_, pc_):
            return buf.at[4 * px_ + 2 * py_ + pc_]

        def copy(k, who, to, src=None):
            return _remote(slot(*who) if src is None else src, slot(*who), send_sems.at[k], recv_sems.at[k], to)

        mine = pltpu.make_async_copy(x_ref, slot(*me), local_sem)
        mine.start()
        first = [copy(0, me, sibling, src=x_ref)]
        first += [copy(1 + j, me, (*chip, c), src=x_ref) for j, chip in enumerate(chips)]
        for cp in first:
            cp.start()
        passed = [copy(4 + j, (*chip, c), sibling) for j, chip in enumerate(chips)]
        for j, chip in enumerate(chips):
            copy(1 + j, (*chip, c), me).wait_recv()
            passed[j].start()
        copy(0, sibling, me).wait_recv()
        for j, chip in enumerate(chips):
            copy(4 + j, (*chip, 1 - c), me).wait_recv()
        for cp in first + passed:
            cp.wait_send()
        mine.wait()
        acc = buf[0, :, :n_sum]
        for s in range(1, n_dev):
            acc = acc + buf[s, :, :n_sum]
        o_ref[...] = acc
        for s in range(n_dev):
            g_ref[s * rows:(s + 1) * rows, :] = buf[s, :, n_sum:]

    return pl.pallas_call(
        body, name="gather_small",
        in_specs=[pl.BlockSpec(memory_space=pltpu.VMEM)],
        out_specs=(pl.BlockSpec(memory_space=pltpu.VMEM), pl.BlockSpec(memory_space=pltpu.VMEM)),
        out_shape=(SDS((rows, n_sum), F32), SDS((n_dev * rows, cols - n_sum), F32)),
        scratch_shapes=[pltpu.VMEM((n_dev, rows, cols), F32), pltpu.SemaphoreType.DMA((7,)),
                        pltpu.SemaphoreType.DMA((7,)), pltpu.SemaphoreType.DMA],
        compiler_params=_cp(has_side_effects=True))(block)


def _adam_math(w, g, m, v):
    m = ADAM_B1 * m + (1.0 - ADAM_B1) * g
    v = ADAM_B2 * v + (1.0 - ADAM_B2) * (g * g)
    m_hat = m / (1.0 - ADAM_B1 ** ADAM_STEP)
    v_hat = v / (1.0 - ADAM_B2 ** ADAM_STEP)
    delta = -ADAM_LR * (m_hat / (jnp.sqrt(v_hat) + ADAM_EPS) + ADAM_WD * w)
    return delta, m, v


def _adamw(w, g, m, v):
    rows, cols = w.shape
    tr = min(rows, 256 if cols <= 2048 else 128)

    def body(w_ref, g_ref, m_ref, v_ref, go_ref, d_ref, nm_ref, nv_ref):
        g = g_ref[...]
        go_ref[...] = g
        d_ref[...], nm_ref[...], nv_ref[...] = _adam_math(w_ref[...], g, m_ref[...], v_ref[...])

    spec = pl.BlockSpec((tr, cols), lambda i: (i, 0))
    return pl.pallas_call(
        body, name="adamw", grid=(rows // tr,), in_specs=[spec] * 4, out_specs=(spec,) * 4,
        out_shape=(SDS(w.shape, F32),) * 4, compiler_params=_cp(("parallel",), 40))(w, g, m, v)


def _adamw_small(w, g, m, v):
    def body(w_ref, g_ref, m_ref, v_ref, go_ref, d_ref, nm_ref, nv_ref):
        w = w_ref[...]
        g = g_ref[...]
        sub = lax.broadcasted_iota(jnp.int32, w.shape, 0)
        lane = lax.broadcasted_iota(jnp.int32, w.shape, 1)
        is_ret = jnp.logical_and(sub == 5, lane < 2 * RET_HEADS)
        u = jnp.exp(jnp.where(is_ret, w, -1.0) * jnp.log(2.0))
        g = jnp.where(is_ret, g * (-u * jnp.log(2.0) / (1.0 - u)), g)
        go_ref[...] = g
        d_ref[...], nm_ref[...], nv_ref[...] = _adam_math(w, g, m_ref[...], v_ref[...])

    return pl.pallas_call(body, name="adamw_small", out_shape=(SDS(w.shape, F32),) * 4)(w, g, m, v)


def _rope_tables(seq, n_samp, n_ctx_rows):
    rows = seq // GRID_W
    row = jnp.repeat(jnp.arange(rows, dtype=F32), GRID_W)
    col = jnp.tile(jnp.arange(GRID_W, dtype=F32), rows)
    half = ATT_HEAD_DIM // 2
    freqs = ROPE_THETA ** (-jnp.arange(0, half, 2, dtype=F32) / half)
    ang = jnp.concatenate([row[:, None] * freqs, col[:, None] * freqs], axis=-1)
    cos, sin = jnp.cos(ang), jnp.sin(ang)
    cos_f = jnp.repeat(cos, 2, axis=1)
    sin_s = jnp.stack([-sin, sin], axis=-1).reshape(seq, ATT_HEAD_DIM)
    cos_all = jnp.concatenate([jnp.tile(cos_f, (n_samp, 1)), jnp.ones((n_ctx_rows, ATT_HEAD_DIM), F32)], axis=0)
    sin_all = jnp.concatenate([jnp.tile(sin_s, (n_samp, 1)), jnp.zeros((n_ctx_rows, ATT_HEAD_DIM), F32)], axis=0)
    return cos_all, sin_all


def _pack_small(c_ctx, norm_w, b_ada, ret, qn, kn):
    d = D_MODEL
    row5 = jnp.concatenate([ret.reshape(-1), jnp.zeros((128 - 2 * RET_HEADS,), F32), qn.reshape(-1), kn.reshape(-1),
                            jnp.zeros((d - 384,), F32)])
    return jnp.concatenate([c_ctx.reshape(1, d), norm_w.reshape(1, d), b_ada.reshape(3, d), row5.reshape(1, d),
                            jnp.zeros((2, d), F32)], axis=0)


def _unpack_small(p):
    d = D_MODEL
    return (p[0], p[1:2], p[2:5].reshape(1, 3 * d), p[5, :2 * RET_HEADS].reshape(1, 2, RET_HEADS),
            p[5:6, 128:256], p[5:6, 256:384])


def _step(x, c, ctx, c_ctx, norm_w, b_ada, ret_log2_decay, q_norm_w, k_norm_w, loss_target, weights, ids, dist):
    n_samp, seq, d = x.shape
    lc = ctx.shape[1]
    t_lat, t_ctx = n_samp * seq, n_samp * lc
    assert seq % TM == 0 and t_ctx == TM and t_lat % lc == 0 and seq % GRID_W == 0
    tps = seq // TM

    x_lat = x.reshape(t_lat, d)
    x_ctx = ctx.reshape(t_ctx, d)
    cvec8 = jnp.concatenate([c, c_ctx.reshape(1, d), jnp.zeros((8 - n_samp - 1, d), F32)], axis=0)
    lg = jnp.log1p(-jnp.exp2(ret_log2_decay.reshape(2, RET_HEADS)))
    cos_all, sin_all = _rope_tables(seq, n_samp, t_ctx)

    w_ada_b, w_in_b, w_or_b, w_oa_b, w_out_b = weights
    w_ada_g = _run_comm(_ag_comm((w_ada_b,), (0, 1, 2)))[0] if dist else w_ada_b
    mod8 = _adaln_fwd(cvec8, w_ada_g, b_ada)
    mod3 = mod8[:n_samp + 1]
    shift3 = mod3[:, None, 0:d]
    scale3 = mod3[:, None, d:2 * d]
    gate3 = mod3[:, None, 2 * d:3 * d]

    hx, hxt = _norm_fwd(x_lat, x_ctx, norm_w, scale3, shift3, tps, n_samp)
    if dist:
        px, w_in_g = _in_proj_gather(hx, w_in_b, ids)
    else:
        w_in_g = w_in_b
        px = _in_proj(hx, w_in_g, ids, 0, N_SHARD)

    states0 = _ctx_state_fwd(px, lg, n_samp, t_lat, lc)
    if dist:
        (o_f, o_b, saved), w_o = _ret_fwd(px, states0, lg, n_samp, seq,
                                          comm=_ag_comm((w_or_b, w_oa_b, w_out_b), (0, 1, 2)))
    else:
        (o_f, o_b, saved), w_o = _ret_fwd(px, states0, lg, n_samp, seq), (w_or_b, w_oa_b, w_out_b)
    w_o_ret, w_o_att, w_out = (w.reshape(-1, d) for w in w_o)

    qn = _att_prep_q(px, cos_all, sin_all, q_norm_w, t_lat)
    kn, vn = _att_prep_kv(px, cos_all, sin_all, k_norm_w)
    o_att, lse = _att_fwd(qn, kn, vn, n_samp, seq, lc)

    (gx_res, do, do_att, dpx, loss8, dgate, g_w_o_ret, g_w_o_att, g_w_out) = _merge(
        x_lat, loss_target.reshape(t_lat, d), o_f, o_b, o_att, px, gate3, w_o_ret, w_o_att, w_out, tps)

    g_a = [g.reshape(N_SHARD, -1, d) for g in (g_w_o_ret, g_w_o_att, g_w_out)]
    res = _att_bwd(dpx, qn, kn, vn, px, o_att, lse, do_att, cos_all, sin_all, q_norm_w, n_samp, seq, lc,
                   comm=_swap_comm(g_a) if dist else None)
    (dpx, dkl, dkc, dvl, dvc, gqw), sib_a = res if dist else (res, None)
    dpx, gkw = _att_kv_bwd(dpx, dkl, dkc, dvl, dvc, px, cos_all, sin_all, k_norm_w)
    if dist:
        t_a = [_chip_sum(g, p, ids) for g, p in zip(g_a, sib_a)]

    res = _ret_bwd(px, do, saved, lg, n_samp, seq,
                   comm=_exchange_comm([t16 for _, t16 in t_a]) if dist else None)
    (dqf, dkf, dvf, dqb, dkb, dvb, dstates, dlg_lat), q_a = res if dist else (res, None)
    if dist:
        r_a = [_shard_sum(t, q, ids) for (t, _), q in zip(t_a, q_a)]
    dpx = _combine_into(dpx, dqf, dqb, C_RQ, 1.0)
    dpx = _combine_into(dpx, dkf, dkb, C_RK, RET_DK ** -0.5)
    dpx = _combine_into(dpx, dvf, dvb, C_RV, 1.0)
    dpx, dlg_ctx = _ctx_state_bwd(dpx, px, dstates, lg, n_samp, t_lat, lc)
    dpx = _zero_ctx_tail(dpx, t_lat)

    n_tiles = dpx.shape[0] // _big_rows(dpx.shape[0])
    if dist:
        g_b = _gw_in(hxt, dpx, 0, 1)
        dhx, (sib_b, *r_a) = _dhx(dpx, w_in_g, 0, 1, comm=_join_comms(_swap_comm([g_b]), _join_comm(r_a)))
        t_b, t16_b = _chip_sum(g_b, sib_b, ids)
        dhx, (q_b,) = _dhx(dpx, w_in_g, 1, n_tiles - 1, dhx=dhx, comm=_exchange_comm([t16_b]))
        r_b_half = _shard_sum(t_b, q_b, ids)
    else:
        g_w_in = _gw_in(hxt, dpx, 0, 1)
        dhx = _dhx(dpx, w_in_g, 0, n_tiles)
    grad_x, dshift, dscale, g_norm_w = _norm_bwd(x_lat, x_ctx, dhx, gx_res, norm_w, scale3, tps, n_samp)

    dgate_all = jnp.concatenate([dgate, jnp.zeros((1, 1, d), F32)], axis=0)
    dmod3 = jnp.concatenate([dshift, dscale, dgate_all], axis=2).reshape(n_samp + 1, 3 * d)
    dmod8 = jnp.concatenate([dmod3, jnp.zeros((8 - n_samp - 1, 3 * d), F32)], axis=0)
    g_lg = (jnp.sum(dlg_lat[:, :, 0], axis=0).reshape(2, RET_HEADS)
            + jnp.stack([jnp.sum(dlg_ctx[:, :, 0, 0], axis=0), jnp.sum(dlg_ctx[:, :, 1, 0], axis=0)], axis=0))
    g_qw = jnp.sum(gqw, axis=(0, 1, 2))
    zero = jnp.zeros((d,), F32)
    if not dist:
        g_w_ada, g_b_ada, dc8 = _adaln_bwd(cvec8, dmod8, w_ada_g)
        small = _pack_small(dc8[n_samp], g_norm_w, g_b_ada, g_lg, g_qw, gkw)
        return (loss8[0, 0], grad_x.reshape(n_samp, seq, d),
                (g_w_ada, g_w_in, g_w_o_ret, g_w_o_att, g_w_out), small)

    local = _pack_small(zero, g_norm_w, jnp.zeros((3 * d,), F32), g_lg, g_qw, gkw).at[6, 0].set(loss8[0, 0])
    small_sum, gathered = _gather_small(jnp.concatenate([local, cvec8, dmod8], axis=1), d)
    (g_w_ada, g_b_ada, dc_all), (r_b,) = _adaln_bwd(gathered[:, :d], gathered[:, d:], w_ada_g,
                                                     comm=_join_comm([r_b_half]))
    dc_ctx = jnp.sum(dc_all.reshape(-1, 8, d)[:, n_samp], axis=0)
    small = small_sum + _pack_small(dc_ctx, zero, g_b_ada, jnp.zeros((2, RET_HEADS), F32), zero[:128], zero[:128])
    r_c = lax.dynamic_index_in_dim(g_w_ada, ids[3], 0, keepdims=False)
    return small[6, 0], grad_x.reshape(n_samp, seq, d), (r_c, r_b, *r_a), small


def kernel(x, c, ctx, c_ctx, norm_w, w_ada, b_ada, w_in, ret_log2_decay, q_norm_w, k_norm_w, w_o_ret, w_o_att, w_out, loss_target, m_c_ctx, m_norm_w, m_w_ada, m_b_ada, m_w_in, m_ret_log2_decay, m_q_norm_w, m_k_norm_w, m_w_o_ret, m_w_o_att, m_w_out, v_c_ctx, v_norm_w, v_w_ada, v_b_ada, v_w_in, v_ret_log2_decay, v_q_norm_w, v_k_norm_w, v_w_o_ret, v_w_o_att, v_w_out):
    big_w = (w_ada[0], w_in[0], w_o_ret[0], w_o_att[0], w_out[0])
    big_m = (m_w_ada[0], m_w_in[0], m_w_o_ret[0], m_w_o_att[0], m_w_out[0])
    big_v = (v_w_ada[0], v_w_in[0], v_w_o_ret[0], v_w_o_att[0], v_w_out[0])

    ids = _place_ids()
    loss, grad_x, big_grad, small_grad_in = _step(
        x, c, ctx, c_ctx, norm_w[0:1], b_ada[0:1], ret_log2_decay[0], q_norm_w[0:1], k_norm_w[0:1], loss_target,
        tuple(_cast_place(w, ids) for w in big_w), ids, True)
    small_w = _pack_small(c_ctx, norm_w, b_ada, ret_log2_decay, q_norm_w, k_norm_w)
    small_m = _pack_small(m_c_ctx, m_norm_w, m_b_ada, m_ret_log2_decay, m_q_norm_w, m_k_norm_w)
    small_v = _pack_small(v_c_ctx, v_norm_w, v_b_ada, v_ret_log2_decay, v_q_norm_w, v_k_norm_w)
    small_grad, small_delta, small_nm, small_nv = _adamw_small(small_w, small_grad_in, small_m, small_v)

    big_g, big_delta, big_nm, big_nv = [], [], [], []
    for w, g, m, v in zip(big_w, big_grad, big_m, big_v):
        go, dlt, nm, nv = _adamw(w, g, m, v)
        big_g.append(go[None])
        big_delta.append(dlt[None])
        big_nm.append(nm[None])
        big_nv.append(nv[None])
    big_grad = big_g

    def order(small_packed, big):
        s = _unpack_small(small_packed)
        return (s[0], s[1], big[0], s[2], big[1], s[3], s[4], s[5], big[2], big[3], big[4])

    return (loss, grad_x, *order(small_grad, big_grad), *order(small_delta, big_delta),
            *order(small_nm, big_nm), *order(small_nv, big_nv))
```

```python
import functools
from typing import NamedTuple

import jax
import jax.numpy as jnp
from jax import lax
from jax.experimental import pallas as pl
from jax.experimental.pallas import tpu as pltpu

F32 = jnp.float32
BF = jnp.bfloat16
SDS = jax.ShapeDtypeStruct
MESH = pl.DeviceIdType.MESH
ANY = pl.BlockSpec(memory_space=pl.ANY)
SMEM = pl.BlockSpec(memory_space=pltpu.SMEM)

D_MODEL = 1024
GRID_W = 64
RET_HEADS = 4
RET_DK = 256
RET_DV = 512
RET_CHUNK = 128
ATT_HEADS = 8
ATT_KV_HEADS = 2
ATT_REP = ATT_HEADS // ATT_KV_HEADS
ATT_HEAD_DIM = 128
ROPE_THETA = 10000.0
NORM_EPS = 1e-6
IN_COLS = 10752
KV_COLS = 3584
C_RK, C_RV, C_AK, C_AV, C_RQ, C_RG, C_AQ, C_AG, C_MR, C_MA = 0, 1024, 3072, 3328, 3584, 4608, 6656, 7680, 8704, 9728
N_SHARD = 4
ADA_W = 3 * D_MODEL // N_SHARD
IN_W = IN_COLS // N_SHARD
IN_BLK = IN_W
BPS = IN_W // IN_BLK
N_IN_BLK = IN_COLS // IN_BLK
TM = 512
ATT_TQ = 512
ADAM_LR, ADAM_B1, ADAM_B2, ADAM_EPS, ADAM_WD, ADAM_STEP = 0.001, 0.9, 0.999, 1e-08, 0.01, 10
MIB = 1024 * 1024


def _cp(sem=None, vmem_mb=None, **kw):
    if sem is not None:
        kw["dimension_semantics"] = sem
    if vmem_mb is not None:
        kw["vmem_limit_bytes"] = vmem_mb * MIB
    return pltpu.CompilerParams(**kw)


def _dot(a, b, ca=1, cb=0):
    return lax.dot_general(a.astype(BF), b.astype(BF), (((ca,), (cb,)), ((), ())), preferred_element_type=F32)


def _sigmoid(x):
    return 0.5 * jnp.tanh(0.5 * x) + 0.5


def _sum_all(x):
    return jnp.sum(jnp.sum(x, axis=1, keepdims=True), axis=0, keepdims=True)


def _swap_pairs(x):
    ax = x.ndim - 1
    lane = lax.broadcasted_iota(jnp.int32, x.shape, ax)
    nxt = pltpu.roll(x, x.shape[ax] - 1, ax)
    prv = pltpu.roll(x, 1, ax)
    return jnp.where(lane % 2 == 0, nxt, prv)


def _rms(x):
    return lax.rsqrt(jnp.mean(x * x, axis=-1, keepdims=True) + NORM_EPS)


def _rms_bwd(dxh, xh, r):
    return r * (dxh - xh * jnp.mean(dxh * xh, axis=-1, keepdims=True))


class _Comm(NamedTuple):
    name: str
    ins: tuple
    out_shapes: tuple
    aliases: dict
    n_sems: int
    phases: tuple
    arg_aliases: tuple = ()


def _join_comms(*comms):
    comms = [cm for cm in comms if cm is not None]
    if len(comms) <= 1:
        return comms[0] if comms else None
    offs, i_off, o_off, s_off = [], 0, 0, 0
    for cm in comms:
        offs.append((i_off, o_off, s_off))
        i_off, o_off, s_off = i_off + len(cm.ins), o_off + len(cm.out_shapes), s_off + cm.n_sems

    def phase(k):
        def run(ins, outs, ssem, rsem, base):
            sends, recvs = [], []
            for cm, (io, oo, so) in zip(comms, offs):
                if k < len(cm.phases):
                    s, r = cm.phases[k](ins[io:io + len(cm.ins)], outs[oo:oo + len(cm.out_shapes)], ssem, rsem,
                                        base + so)
                    sends += s
                    recvs += r
            return sends, recvs
        return run

    aliases, arg_aliases = {}, ()
    for cm, (io, oo, _) in zip(comms, offs):
        aliases.update({io + a: oo + b for a, b in cm.aliases.items()})
        arg_aliases += tuple((a, oo + b) for a, b in cm.arg_aliases)
    return _Comm("+".join(cm.name for cm in comms), sum((cm.ins for cm in comms), ()),
                 sum((cm.out_shapes for cm in comms), ()), aliases, s_off,
                 tuple(phase(k) for k in range(max(len(cm.phases) for cm in comms))), arg_aliases)


def _run_phases(comm, cins, couts, ssem, rsem, first_started):
    for k, phase in enumerate(comm.phases):
        sends, recvs = phase(cins, couts, ssem, rsem, 0)
        if k > 0 or not first_started:
            for cp in sends:
                cp.start()
        for cp in recvs:
            cp.wait_recv()
        for cp in sends:
            cp.wait_send()


def _call(body, args, comm=None, *, name, grid, in_specs, out_specs, out_shape, scratch_shapes=(),
          compiler_params, aliases=None, prefetch=None):
    single = not isinstance(out_shape, (tuple, list))
    out_specs_t = (out_specs,) if single else tuple(out_specs)
    out_shape_t = (out_shape,) if single else tuple(out_shape)
    n_pre = 0 if prefetch is None else 1
    n_in, n_out, n_sc = len(in_specs), len(out_specs_t), len(scratch_shapes)
    io_alias = {n_pre + a: b for a, b in (aliases or {}).items()}
    if comm is None:
        kernel_body, cin, cout, csems = body, [], [], []
    else:
        n_ci, n_co = len(comm.ins), len(comm.out_shapes)
        cin, cout = [ANY] * n_ci, [ANY] * n_co
        csems = [pltpu.SemaphoreType.DMA((comm.n_sems,)), pltpu.SemaphoreType.DMA((comm.n_sems,))]
        io_alias.update({n_pre + n_in + a: n_out + b for a, b in comm.aliases.items()})
        io_alias.update({n_pre + a: n_out + b for a, b in comm.arg_aliases})

        def kernel_body(*refs):
            pre, refs = refs[:n_pre], refs[n_pre:]
            ins, cins = refs[:n_in], refs[n_in:n_in + n_ci]
            outs = refs[n_in + n_ci:n_in + n_ci + n_out]
            couts = refs[n_in + n_ci + n_out:n_in + n_ci + n_out + n_co]
            scratch = refs[n_in + n_ci + n_out + n_co:n_in + n_ci + n_out + n_co + n_sc]
            ssem, rsem = refs[-2:]
            first = functools.reduce(jnp.logical_and, [pl.program_id(k) == 0 for k in range(len(grid))])
            last = functools.reduce(jnp.logical_and, [pl.program_id(k) == grid[k] - 1 for k in range(len(grid))])

            @pl.when(first)
            def _():
                for cp in comm.phases[0](cins, couts, ssem, rsem, 0)[0]:
                    cp.start()

            body(*pre, *ins, *outs, *scratch)

            @pl.when(last)
            def _():
                _run_phases(comm, cins, couts, ssem, rsem, True)

        name = name + "+" + comm.name

    all_in, all_out = list(in_specs) + cin, out_specs_t + tuple(cout)
    shapes = out_shape_t + (tuple(comm.out_shapes) if comm is not None else ())
    scratch = list(scratch_shapes) + csems
    if prefetch is None:
        res = pl.pallas_call(kernel_body, name=name, grid=grid, in_specs=all_in, out_specs=all_out, out_shape=shapes,
                             scratch_shapes=scratch, input_output_aliases=io_alias,
                             compiler_params=compiler_params)(*args, *(comm.ins if comm is not None else ()))
    else:
        res = pl.pallas_call(
            kernel_body, name=name, out_shape=shapes, input_output_aliases=io_alias, compiler_params=compiler_params,
            grid_spec=pltpu.PrefetchScalarGridSpec(num_scalar_prefetch=1, grid=grid, in_specs=all_in,
                                                   out_specs=all_out, scratch_shapes=scratch))(
                                                       prefetch, *args, *(comm.ins if comm is not None else ()))
    own = res[0] if single else tuple(res[:n_out])
    return own if comm is None else (own, tuple(res[n_out:]))


def _run_comm(comm):
    n_ci, n_co = len(comm.ins), len(comm.out_shapes)

    def body(*refs):
        _run_phases(comm, refs[:n_ci], refs[n_ci:n_ci + n_co], refs[-2], refs[-1], False)

    return pl.pallas_call(
        body, name=comm.name, in_specs=[ANY] * n_ci, out_specs=tuple([ANY] * n_co), out_shape=tuple(comm.out_shapes),
        input_output_aliases=dict(comm.aliases),
        scratch_shapes=[pltpu.SemaphoreType.DMA((comm.n_sems,)), pltpu.SemaphoreType.DMA((comm.n_sems,))],
        compiler_params=_cp(has_side_effects=True))(*comm.ins)


def _adaln_fwd(cvec8, w_ada_g, b_ada):
    def body(c_ref, w_ref, b_ref, o_ref):
        cv = c_ref[...]
        sc = (cv * _sigmoid(cv)).astype(BF)
        for s in range(N_SHARD):
            cols = slice(s * ADA_W, (s + 1) * ADA_W)
            o_ref[:, cols] = jnp.dot(sc, w_ref[s], preferred_element_type=F32) + b_ref[:, cols]

    return pl.pallas_call(body, out_shape=SDS((8, 3 * D_MODEL), F32), name="adaln_fwd",
                          compiler_params=_cp(vmem_mb=32))(cvec8, w_ada_g, b_ada)


def _adaln_bwd(cvec, dmod, w_ada_g, comm=None):
    n_rows = cvec.shape[0]
    def body(c_ref, d_ref, w_ref, gw_ref, gb_ref, dc_ref):
        cv = c_ref[...]
        sg = _sigmoid(cv)
        sc = cv * sg
        dm = d_ref[...]
        gb_ref[...] = jnp.sum(dm, axis=0, keepdims=True)
        dsc = jnp.zeros(cv.shape, F32)
        for s in range(N_SHARD):
            cols = slice(s * ADA_W, (s + 1) * ADA_W)
            gw_ref[s] = _dot(sc, dm[:, cols], 0, 0)
            dsc = dsc + _dot(dm[:, cols], w_ref[s], 1, 1)
        dc_ref[...] = dsc * (sg * (1.0 + cv * (1.0 - sg)))

    def whole(shape):
        return pl.BlockSpec(shape, lambda i: (0,) * len(shape))

    shapes = ((N_SHARD, D_MODEL, ADA_W), (1, 3 * D_MODEL), (n_rows, D_MODEL))
    return _call(body, [cvec, dmod, w_ada_g], comm, name="adaln_bwd", grid=(1,),
                 in_specs=[whole(cvec.shape), whole(dmod.shape), whole(w_ada_g.shape)],
                 out_specs=tuple(whole(s) for s in shapes), out_shape=tuple(SDS(s, F32) for s in shapes),
                 compiler_params=_cp(("arbitrary",), 56))


def _big_rows(rows):
    return 1536 if rows % 1536 == 0 else TM


def _norm_fwd(x_lat, x_ctx, norm_w, scale3, shift3, tiles_per_sample, n_samp):
    n_lat = x_lat.shape[0] // TM
    rows = x_lat.shape[0] + x_ctx.shape[0]

    def samp(i):
        return jnp.minimum(i // tiles_per_sample, n_samp)

    def body(x_ref, c_ref, nw_ref, sc_ref, sh_ref, hx_ref, hxt_ref):
        x = jnp.where(pl.program_id(0) < n_lat, x_ref[...], c_ref[...])
        h = x * _rms(x) * nw_ref[...] * (1.0 + sc_ref[...]) + sh_ref[...]
        hx_ref[...] = h.astype(BF)
        hxt_ref[...] = h.T.astype(BF)

    return pl.pallas_call(
        body, name="norm_fwd", grid=(rows // TM,),
        in_specs=[pl.BlockSpec((TM, D_MODEL), lambda i: (jnp.minimum(i, n_lat - 1), 0)),
                  pl.BlockSpec((TM, D_MODEL), lambda i: (jnp.maximum(i - n_lat, 0), 0)),
                  pl.BlockSpec((1, D_MODEL), lambda i: (0, 0)),
                  pl.BlockSpec((None, 1, D_MODEL), lambda i: (samp(i), 0, 0)),
                  pl.BlockSpec((None, 1, D_MODEL), lambda i: (samp(i), 0, 0))],
        out_specs=(pl.BlockSpec((TM, D_MODEL), lambda i: (i, 0)),
                   pl.BlockSpec((D_MODEL, TM), lambda i: (0, i))),
        out_shape=(SDS((rows, D_MODEL), BF), SDS((D_MODEL, rows), BF)),
        compiler_params=_cp(("parallel",), 40))(x_lat, x_ctx, norm_w, scale3, shift3)


def _in_proj(hx, w_in_g, ids, first, count, px=None, comm=None):
    rows = hx.shape[0]
    tb = _big_rows(rows)

    def shard(j, ids_ref):
        return ids_ref[4 + first + j // BPS]

    def body(ids_ref, h_ref, w_ref, *rest):
        px_ref = rest[-1]
        px_ref[...] = jnp.dot(h_ref[...], w_ref[...], preferred_element_type=F32).astype(BF)

    args, in_specs, aliases = [hx, w_in_g], [
        pl.BlockSpec((tb, D_MODEL), lambda j, i, ids_ref: (i, 0)),
        pl.BlockSpec((None, D_MODEL, IN_BLK), lambda j, i, ids_ref: (shard(j, ids_ref), 0, j % BPS))], None
    if px is not None:
        args, in_specs, aliases = args + [px], in_specs + [ANY], {2: 0}
    return _call(body, args, comm, name="in_proj", grid=(BPS * count, rows // tb), in_specs=in_specs,
                 out_specs=pl.BlockSpec((tb, IN_BLK),
                                        lambda j, i, ids_ref: (i, BPS * shard(j, ids_ref) + j % BPS)),
                 out_shape=SDS((rows, IN_COLS), BF), aliases=aliases, prefetch=ids,
                 compiler_params=_cp(("arbitrary", "arbitrary"), 56))


def _norm_bwd(x_lat, x_ctx, dhx, gx_res, norm_w, scale3, tiles_per_sample, n_samp, comm=None):
    rows = x_lat.shape[0] + x_ctx.shape[0]
    n_lat = tiles_per_sample * n_samp

    def samp(i):
        return jnp.minimum(i // tiles_per_sample, n_samp)

    def lat(i):
        return jnp.minimum(i, n_lat - 1)

    def body(x_ref, c_ref, dh_ref, gr_ref, nw_ref, sc_ref, gx_ref, dsh_ref, dsc_ref, dnw_ref):
        i = pl.program_id(0)
        x = jnp.where(i < n_lat, x_ref[...], c_ref[...])
        r = _rms(x)
        xh = x * r
        nw = nw_ref[...]
        dh = dh_ref[...]
        first = jnp.logical_or(i % tiles_per_sample == 0, i >= n_lat)

        @pl.when(first)
        def _():
            dsh_ref[...] = jnp.zeros_like(dsh_ref)
            dsc_ref[...] = jnp.zeros_like(dsc_ref)

        @pl.when(i == 0)
        def _():
            dnw_ref[...] = jnp.zeros_like(dnw_ref)

        dsh_ref[...] += jnp.sum(dh, axis=0, keepdims=True)
        dsc_ref[...] += jnp.sum(dh * (xh * nw), axis=0, keepdims=True)
        du = dh * (1.0 + sc_ref[...])
        dnw_ref[...] += jnp.sum(du * xh, axis=0, keepdims=True)

        @pl.when(i < n_lat)
        def _():
            gx_ref[...] = gr_ref[...] + _rms_bwd(du * nw, xh, r)

    return _call(
        body, [x_lat, x_ctx, dhx, gx_res, norm_w, scale3], comm, name="norm_bwd", grid=(rows // TM,),
        in_specs=[pl.BlockSpec((TM, D_MODEL), lambda i: (lat(i), 0)),
                  pl.BlockSpec((TM, D_MODEL), lambda i: (jnp.maximum(i - n_lat, 0), 0)),
                  pl.BlockSpec((TM, D_MODEL), lambda i: (i, 0)),
                  pl.BlockSpec((TM, D_MODEL), lambda i: (lat(i), 0)),
                  pl.BlockSpec((1, D_MODEL), lambda i: (0, 0)),
                  pl.BlockSpec((None, 1, D_MODEL), lambda i: (samp(i), 0, 0))],
        out_specs=(pl.BlockSpec((TM, D_MODEL), lambda i: (lat(i), 0)),
                   pl.BlockSpec((None, 1, D_MODEL), lambda i: (samp(i), 0, 0)),
                   pl.BlockSpec((None, 1, D_MODEL), lambda i: (samp(i), 0, 0)),
                   pl.BlockSpec((1, D_MODEL), lambda i: (0, 0))),
        out_shape=(SDS((n_lat * TM, D_MODEL), F32), SDS((n_samp + 1, 1, D_MODEL), F32),
                   SDS((n_samp + 1, 1, D_MODEL), F32), SDS((1, D_MODEL), F32)),
        compiler_params=_cp(("arbitrary",), 40))


def _in_proj_gather(hx, w_buf, ids):
    rows = hx.shape[0]
    tb = _big_rows(rows)
    n_i = rows // tb
    hrows = D_MODEL // 2

    def body(ids_ref, h_ref, w_in_hbm, px_ref, w_hbm, wv, lsem, ssem, rsem):
        j, i = pl.program_id(0), pl.program_id(1)
        x, y, c, chips = _place()
        sibling = (x, y, 1 - c)

        def half(s, which):
            return w_hbm.at[s, pl.ds(which * hrows, hrows), :]

        def over_ici(rel):
            chip = chips[rel]
            mine, theirs = half(2 * x + y, c), half(2 * chip[0] + chip[1], c)
            return (_remote(mine, mine, ssem.at[rel], rsem.at[rel], (*chip, c)),
                    _remote(theirs, theirs, ssem.at[rel], rsem.at[rel], (*chip, c)))

        def over_d2d(rel):
            s = 2 * chips[rel][0] + chips[rel][1]
            return (_remote(half(s, c), half(s, c), ssem.at[3 + rel], rsem.at[3 + rel], sibling),
                    _remote(half(s, 1 - c), half(s, 1 - c), ssem.at[3 + rel], rsem.at[3 + rel], sibling))

        first_row_tile = i == 0

        @pl.when(jnp.logical_and(j == 0, first_row_tile))
        def _():
            over_ici(0)[0].start()
            over_ici(1)[0].start()

        for rel in range(3):
            @pl.when(jnp.logical_and(j == rel + 1, first_row_tile))
            def _(rel=rel):
                over_ici(rel)[1].wait_recv()
                passed, landing = over_d2d(rel)
                passed.start()
                if rel == 0:
                    over_ici(2)[0].start()
                landing.wait_recv()

        @pl.when(first_row_tile)
        def _():
            cp = pltpu.make_async_copy(w_hbm.at[ids_ref[4 + j]], wv, lsem)
            cp.start()
            cp.wait()

        px_ref[...] = jnp.dot(h_ref[...], wv[...], preferred_element_type=F32).astype(BF)

        @pl.when(jnp.logical_and(j == N_SHARD - 1, i == n_i - 1))
        def _():
            for rel in range(3):
                over_ici(rel)[0].wait_send()
                over_d2d(rel)[0].wait_send()

    return pl.pallas_call(
        body, name="in_proj_gather", input_output_aliases={2: 1},
        grid_spec=pltpu.PrefetchScalarGridSpec(
            num_scalar_prefetch=1, grid=(N_SHARD, n_i),
            in_specs=[pl.BlockSpec((tb, D_MODEL), lambda j, i, ids_ref: (i, 0)), ANY],
            out_specs=(pl.BlockSpec((tb, IN_W), lambda j, i, ids_ref: (i, ids_ref[4 + j])), ANY),
            scratch_shapes=[pltpu.VMEM((D_MODEL, IN_W), BF), pltpu.SemaphoreType.DMA,
                            pltpu.SemaphoreType.DMA((6,)), pltpu.SemaphoreType.DMA((6,))]),
        out_shape=(SDS((rows, IN_COLS), BF), SDS(w_buf.shape, w_buf.dtype)),
        compiler_params=_cp(("arbitrary", "arbitrary"), 56))(ids, hx, w_buf)


def _gw_in(hxt, dpx_all, part, n_parts, comm=None):
    rows = dpx_all.shape[0]
    tb = _big_rows(rows)
    dp = D_MODEL // n_parts

    def body(h_ref, d_ref, o_ref):
        @pl.when(pl.program_id(1) == 0)
        def _():
            o_ref[...] = jnp.zeros_like(o_ref)

        o_ref[...] += jnp.dot(h_ref[...], d_ref[...], preferred_element_type=F32)

    return _call(body, [hxt, dpx_all], comm, name="gw_in", grid=(N_IN_BLK, rows // tb),
                 in_specs=[pl.BlockSpec((dp, tb), lambda j, i: (part, i)),
                           pl.BlockSpec((tb, IN_BLK), lambda j, i: (i, j))],
                 out_specs=pl.BlockSpec((None, dp, IN_BLK), lambda j, i: (j // BPS, 0, j % BPS)),
                 out_shape=SDS((N_SHARD, dp, IN_W), F32),
                 compiler_params=_cp(("arbitrary", "arbitrary"), 56))


def _dhx(dpx_all, w_in_g, tile0, n_tiles, dhx=None, comm=None):
    rows = dpx_all.shape[0]
    tb = _big_rows(rows)

    def body(d_ref, w_ref, *rest):
        o_ref = rest[-1]

        @pl.when(pl.program_id(1) == 0)
        def _():
            o_ref[...] = jnp.zeros_like(o_ref)

        o_ref[...] += lax.dot_general(d_ref[...], w_ref[...], (((1,), (1,)), ((), ())), preferred_element_type=F32)

    args, in_specs, aliases = [dpx_all, w_in_g], [
        pl.BlockSpec((tb, IN_BLK), lambda i, j: (tile0 + i, j)),
        pl.BlockSpec((None, D_MODEL, IN_BLK), lambda i, j: (j // BPS, 0, j % BPS))], None
    if dhx is not None:
        args, in_specs, aliases = args + [dhx], in_specs + [ANY], {2: 0}
    return _call(body, args, comm, name="dhx", grid=(n_tiles, N_IN_BLK), in_specs=in_specs,
                 out_specs=pl.BlockSpec((tb, D_MODEL), lambda i, j: (tile0 + i, 0)),
                 out_shape=SDS((rows, D_MODEL), F32), aliases=aliases,
                 compiler_params=_cp(("arbitrary", "arbitrary"), 56))


def _decays(lgv, d):
    c = RET_CHUNK
    ii = lax.broadcasted_iota(jnp.int32, (c, 1), 0).astype(F32)
    jj = lax.broadcasted_iota(jnp.int32, (1, c), 1).astype(F32)
    a_i = jnp.where(d == 0, ii, c - 1.0 - ii)
    a_j = jnp.where(d == 0, jj, c - 1.0 - jj)
    rel = a_i - a_j
    mask = jnp.where(rel >= 0, jnp.exp(lgv * jnp.maximum(rel, 0.0)), 0.0)
    qd = jnp.exp(lgv * (a_i + 1.0))
    kd = jnp.exp(lgv * (c - 1.0 - a_i))
    gc = jnp.exp(jnp.full((1, 1), lgv * c, F32))
    return a_i, rel, mask, qd, kd, gc


def _ctx_state_fwd(px, lg, n_samp, t_lat, lc):
    rb = t_lat // lc

    def body(lg_ref, k_ref, v_ref, o_ref):
        h = pl.program_id(1)
        k = k_ref[...].astype(F32) * (RET_DK ** -0.5)
        v = v_ref[...]
        pos = lax.broadcasted_iota(jnp.int32, (lc, 1), 0).astype(F32)
        o_ref[0] = _dot(k * jnp.exp(lg_ref[0, h] * (lc - 1.0 - pos)), v, 0, 0)
        o_ref[1] = _dot(k * jnp.exp(lg_ref[1, h] * pos), v, 0, 0)

    return pl.pallas_call(
        body, name="ctx_state_fwd", grid=(n_samp, RET_HEADS),
        in_specs=[SMEM,
                  pl.BlockSpec((lc, RET_DK), lambda b, h: (rb + b, C_RK // RET_DK + h)),
                  pl.BlockSpec((lc, RET_DV), lambda b, h: (rb + b, C_RV // RET_DV + h))],
        out_specs=pl.BlockSpec((None, 2, None, RET_DK, RET_DV), lambda b, h: (b, 0, h, 0, 0)),
        out_shape=SDS((n_samp, 2, RET_HEADS, RET_DK, RET_DV), F32),
        compiler_params=_cp(("parallel", "parallel")))(lg, px, px)


def _ctx_state_bwd(dpx, px, dstates, lg, n_samp, t_lat, lc):
    rb = t_lat // lc
    kspec = pl.BlockSpec((lc, RET_DK), lambda b, h: (rb + b, C_RK // RET_DK + h))
    vspec = pl.BlockSpec((lc, RET_DV), lambda b, h: (rb + b, C_RV // RET_DV + h))
    sspec = pl.BlockSpec((None, 2, None, RET_DK, RET_DV), lambda b, h: (b, 0, h, 0, 0))

    def weights(lg_ref, h):
        pos = lax.broadcasted_iota(jnp.int32, (lc, 1), 0).astype(F32)
        e_f = lc - 1.0 - pos
        return pos, e_f, jnp.exp(lg_ref[0, h] * e_f), jnp.exp(lg_ref[1, h] * pos)

    def k_body(lg_ref, dpx_hbm, k_ref, v_ref, ds_ref, dk_ref, dlg_ref):
        pos, e_f, w_f, w_b = weights(lg_ref, pl.program_id(1))
        k = k_ref[...].astype(F32) * (RET_DK ** -0.5)
        y_f = _dot(v_ref[...], ds_ref[0], 1, 1) * w_f
        y_b = _dot(v_ref[...], ds_ref[1], 1, 1) * w_b
        dk_ref[...] = ((y_f + y_b) * (RET_DK ** -0.5)).astype(BF)
        t_f = _sum_all(e_f * k * y_f)
        t_b = _sum_all(pos * k * y_b)
        sub = lax.broadcasted_iota(jnp.int32, (8, 128), 0)
        dlg_ref[...] = jnp.where(sub == 0, t_f, jnp.where(sub == 1, t_b, 0.0))

    def v_body(lg_ref, dpx_hbm, k_ref, ds_ref, dv_ref):
        _, _, w_f, w_b = weights(lg_ref, pl.program_id(1))
        k = k_ref[...].astype(F32) * (RET_DK ** -0.5)
        dv_ref[...] = (_dot(k * w_f, ds_ref[0]) + _dot(k * w_b, ds_ref[1])).astype(BF)

    dpx, dlg = pl.pallas_call(
        k_body, name="ctx_state_bwd_k", grid=(n_samp, RET_HEADS), input_output_aliases={1: 0},
        in_specs=[SMEM, ANY, kspec, vspec, sspec],
        out_specs=(kspec, pl.BlockSpec((None, None, 8, 128), lambda b, h: (b, h, 0, 0))),
        out_shape=(SDS(dpx.shape, dpx.dtype), SDS((n_samp, RET_HEADS, 8, 128), F32)),
        compiler_params=_cp(("parallel", "parallel")))(lg, dpx, px, px, dstates)
    dpx = pl.pallas_call(
        v_body, name="ctx_state_bwd_v", grid=(n_samp, RET_HEADS), input_output_aliases={1: 0},
        in_specs=[SMEM, ANY, kspec, sspec], out_specs=vspec, out_shape=SDS(dpx.shape, dpx.dtype),
        compiler_params=_cp(("parallel", "parallel")))(lg, dpx, px, dstates)
    return dpx, dlg


def _zero_ctx_tail(dpx, t_lat):
    wb = 512
    n_ctx = (dpx.shape[0] - t_lat) // TM

    def body(dpx_hbm, o_ref):
        o_ref[...] = jnp.zeros_like(o_ref)

    return pl.pallas_call(
        body, name="zero_ctx_tail", grid=(n_ctx, (IN_COLS - KV_COLS) // wb), input_output_aliases={0: 0},
        in_specs=[ANY], out_specs=pl.BlockSpec((TM, wb), lambda i, j: (t_lat // TM + i, KV_COLS // wb + j)),
        out_shape=SDS(dpx.shape, dpx.dtype),
        compiler_params=_cp(("parallel", "parallel")))(dpx)


def _ret_specs(row_f, row_b):
    c = RET_CHUNK
    wq = RET_HEADS * RET_DK // 2
    wv = RET_HEADS * RET_DV // 2
    specs = []
    for row in (row_f, row_b):
        specs += [pl.BlockSpec((c, wq), lambda b, n, row=row: (row(b, n), C_RQ // wq)),
                  pl.BlockSpec((c, wq), lambda b, n, row=row: (row(b, n), C_RQ // wq + 1)),
                  pl.BlockSpec((c, 2 * wq), lambda b, n, row=row: (row(b, n), C_RK // (2 * wq))),
                  pl.BlockSpec((c, wv), lambda b, n, row=row: (row(b, n), C_RV // wv)),
                  pl.BlockSpec((c, wv), lambda b, n, row=row: (row(b, n), C_RV // wv + 1))]
    return specs


def _ret_head(refs, h):
    q0, q1, k_ref, v0, v1 = refs
    hh = h % 2
    q = (q0, q1)[h // 2][:, hh * RET_DK:(hh + 1) * RET_DK].astype(F32)
    k = k_ref[:, h * RET_DK:(h + 1) * RET_DK].astype(F32) * (RET_DK ** -0.5)
    v = (v0, v1)[h // 2][:, hh * RET_DV:(hh + 1) * RET_DV]
    return q, k, v


def _ret_fwd(px, states0, lg, n_samp, seq, comm=None):
    c = RET_CHUNK
    nc = seq // c
    t_lat = n_samp * seq
    wo = RET_HEADS * RET_DV

    def row_f(b, n):
        return b * nc + n

    def row_b(b, n):
        return b * nc + nc - 1 - n

    def body(lg_ref, *refs):
        ins, (s0_ref, of_ref, ob_ref, st_ref, s_s) = refs[:10], refs[10:]

        @pl.when(pl.program_id(1) == 0)
        def _():
            s_s[...] = s0_ref[...]

        for d, o_ref in ((0, of_ref), (1, ob_ref)):
            for h in range(RET_HEADS):
                _, _, mask, qd, kd, gc = _decays(lg_ref[d, h], d)
                q, k, v = _ret_head(ins[5 * d:5 * d + 5], h)
                s = s_s[d, h]
                st_ref[h, d] = s.astype(BF)
                sc = _dot(q, k, 1, 1) * mask
                o_ref[:, h * RET_DV:(h + 1) * RET_DV] = (_dot(sc, v) + _dot(q * qd, s)).astype(BF)
                s_s[d, h] = s * gc + _dot(k * kd, v, 0, 0)

    return _call(
        body, [lg] + [px] * 10 + [states0], comm, name="ret_fwd", grid=(n_samp, nc),
        in_specs=[SMEM] + _ret_specs(row_f, row_b) + [
            pl.BlockSpec((None, 2, RET_HEADS, RET_DK, RET_DV), lambda b, n: (b, 0, 0, 0, 0))],
        out_specs=(pl.BlockSpec((c, wo), lambda b, n: (row_f(b, n), 0)),
                   pl.BlockSpec((c, wo), lambda b, n: (row_b(b, n), 0)),
                   pl.BlockSpec((None, RET_HEADS, 2, None, RET_DK, RET_DV), lambda b, n: (b, 0, 0, n, 0, 0))),
        out_shape=(SDS((t_lat, wo), BF), SDS((t_lat, wo), BF),
                   SDS((n_samp, RET_HEADS, 2, nc, RET_DK, RET_DV), BF)),
        scratch_shapes=[pltpu.VMEM((2, RET_HEADS, RET_DK, RET_DV), F32)],
        compiler_params=_cp(("arbitrary", "arbitrary"), 48))


def _ret_bwd(dpx, px, do, saved, lg, n_samp, seq, comm=None):
    c = RET_CHUNK
    nc = seq // c
    assert nc % 2 == 0
    wq, wo = RET_HEADS * RET_DK, RET_HEADS * RET_DV

    def row_f(b, n):
        return b * nc + nc - 1 - n

    def row_b(b, n):
        return b * nc + n

    def body(lg_ref, *refs):
        ins = refs[:10]
        (dof_ref, dob_ref, st_ref, dpx_in, dpx_hbm, ds0_ref, dlg_ref,
         ds_s, acc_s, tq_s, tk_s, tv_s, sq_s, sk_s, sv_s, sems) = refs[10:]
        b, n = pl.program_id(0), pl.program_id(1)
        second = n >= nc // 2
        chunks = (nc - 1 - n, n)

        def parked(ch):
            return pl.ds(pl.multiple_of(ch * c, c), c)

        def flush():
            cps = []
            for d, ch in enumerate(chunks):
                rows = pl.ds(pl.multiple_of((b * nc + ch) * c, c), c)
                cps += [pltpu.make_async_copy(sq_s.at[parked(ch), :], dpx_hbm.at[rows, pl.ds(C_RQ, wq)], sems.at[3 * d]),
                        pltpu.make_async_copy(sk_s.at[parked(ch), :], dpx_hbm.at[rows, pl.ds(C_RK, wq)],
                                              sems.at[3 * d + 1]),
                        pltpu.make_async_copy(sv_s.at[parked(ch), :], dpx_hbm.at[rows, pl.ds(C_RV, wo)],
                                              sems.at[3 * d + 2])]
            return cps

        @pl.when(jnp.logical_or(n > nc // 2, jnp.logical_and(n == 0, b > 0)))
        def _():
            for cp in flush():
                cp.wait()

        @pl.when(n == 0)
        def _():
            ds_s[...] = jnp.zeros_like(ds_s)
            acc_s[...] = jnp.zeros_like(acc_s)

        for d, do_ref in enumerate((dof_ref, dob_ref)):
            for h in range(RET_HEADS):
                a_i, rel, mask, qd, kd, gc = _decays(lg_ref[d, h], d)
                q, k, v = _ret_head(ins[5 * d:5 * d + 5], h)
                qb, kb, vb = q.astype(BF), k.astype(BF), v.astype(BF)
                dob = do_ref[:, h * RET_DV:(h + 1) * RET_DV].astype(BF)
                sb = st_ref[h, d]
                ds = ds_s[d, h]
                dsb = ds.astype(BF)
                raw = _dot(qb, kb, 1, 1)
                sc = raw * mask
                dsc = _dot(dob, vb, 1, 1) * mask
                dscb = dsc.astype(BF)
                x = _dot(dob, sb, 1, 1)
                y = _dot(vb, dsb, 1, 1)
                qq = q * qd
                kk = k * kd
                tq_s[d, :, h * RET_DK:(h + 1) * RET_DK] = _dot(dscb, kb) + x * qd
                tk_s[d, :, h * RET_DK:(h + 1) * RET_DK] = _dot(dscb, qb, 0, 0) + y * kd
                tv_s[d, :, h * RET_DV:(h + 1) * RET_DV] = _dot(sc, dob, 0, 0) + _dot(kk, dsb)
                t = (_sum_all(dsc * raw * rel) + _sum_all((a_i + 1.0) * qq * x)
                     + _sum_all((c - 1.0 - a_i) * kk * y) + c * gc * _sum_all(ds * sb.astype(F32)))
                acc_s[4 * d + h:4 * d + h + 1, :] += t
                ds_s[d, h] = ds * gc + _dot(qq, dob, 0, 0)

        @pl.when(jnp.logical_not(second))
        def _():
            for d, ch in enumerate(chunks):
                sq_s[parked(ch), :] = tq_s[d].astype(BF)
                sk_s[parked(ch), :] = tk_s[d].astype(BF)
                sv_s[parked(ch), :] = tv_s[d].astype(BF)

        @pl.when(second)
        def _():
            for d, ch in enumerate(chunks):
                sq_s[parked(ch), :] = (sq_s[parked(ch), :].astype(F32) + tq_s[d]).astype(BF)
                sk_s[parked(ch), :] = ((sk_s[parked(ch), :].astype(F32) + tk_s[d]) * (RET_DK ** -0.5)).astype(BF)
                sv_s[parked(ch), :] = (sv_s[parked(ch), :].astype(F32) + tv_s[d]).astype(BF)
            for cp in flush():
                cp.start()

        @pl.when(n == nc - 1)
        def _():
            ds0_ref[...] = ds_s[...]
            dlg_ref[...] = acc_s[...]

        @pl.when(jnp.logical_and(b == n_samp - 1, n == nc - 1))
        def _():
            for cp in flush():
                cp.wait()

    do_spec_f = pl.BlockSpec((c, wo), lambda b, n: (row_f(b, n), 0))
    do_spec_b = pl.BlockSpec((c, wo), lambda b, n: (row_b(b, n), 0))
    return _call(
        body, [lg] + [px] * 10 + [do, do, saved, dpx], comm, name="ret_bwd", grid=(n_samp, nc), aliases={14: 0},
        in_specs=[SMEM] + _ret_specs(row_f, row_b) + [
            do_spec_f, do_spec_b,
            pl.BlockSpec((None, RET_HEADS, 2, None, RET_DK, RET_DV), lambda b, n: (b, 0, 0, nc - 1 - n, 0, 0)),
            ANY],
        out_specs=(ANY,
                   pl.BlockSpec((None, 2, RET_HEADS, RET_DK, RET_DV), lambda b, n: (b, 0, 0, 0, 0)),
                   pl.BlockSpec((None, 8, 128), lambda b, n: (b, 0, 0))),
        out_shape=(SDS(dpx.shape, dpx.dtype),
                   SDS((n_samp, 2, RET_HEADS, RET_DK, RET_DV), F32), SDS((n_samp, 8, 128), F32)),
        scratch_shapes=[pltpu.VMEM((2, RET_HEADS, RET_DK, RET_DV), F32), pltpu.VMEM((8, 128), F32),
                        pltpu.VMEM((2, c, wq), F32), pltpu.VMEM((2, c, wq), F32), pltpu.VMEM((2, c, wo), F32),
                        pltpu.VMEM((seq, wq), BF), pltpu.VMEM((seq, wq), BF), pltpu.VMEM((seq, wo), BF),
                        pltpu.SemaphoreType.DMA((6,))],
        compiler_params=_cp(("arbitrary", "arbitrary"), 60))


def _combine_into(dpx, a, b, col0, scale):
    t_lat, width = a.shape
    wb = 512
    assert col0 % wb == 0 and width % wb == 0

    def body(dpx_hbm, a_ref, b_ref, o_ref):
        o_ref[...] = ((a_ref[...].astype(F32) + b_ref[...].astype(F32)) * scale).astype(BF)

    src = pl.BlockSpec((TM, wb), lambda i, j: (i, j))
    return pl.pallas_call(
        body, name="combine_into", grid=(t_lat // TM, width // wb), input_output_aliases={0: 0},
        in_specs=[ANY, src, src], out_specs=pl.BlockSpec((TM, wb), lambda i, j: (i, col0 // wb + j)),
        out_shape=SDS(dpx.shape, dpx.dtype),
        compiler_params=_cp(("parallel", "parallel")))(dpx, a, b)


def _retnorm_fwd(o_f, o_b, px):
    t_lat = o_f.shape[0]

    def body(of_ref, ob_ref, g_ref, y_ref):
        o = of_ref[...].astype(F32) + ob_ref[...].astype(F32)
        g = g_ref[...].astype(F32)
        y_ref[...] = (o * _rms(o) * (g * _sigmoid(g))).astype(BF)

    so = pl.BlockSpec((TM, RET_DV), lambda i, h: (i, h))
    return pl.pallas_call(
        body, name="retnorm_fwd", grid=(t_lat // TM, RET_HEADS),
        in_specs=[so, so, pl.BlockSpec((TM, RET_DV), lambda i, h: (i, C_RG // RET_DV + h))],
        out_specs=so,
        out_shape=SDS((t_lat, RET_HEADS * RET_DV), BF),
        compiler_params=_cp(("parallel", "parallel")))(o_f, o_b, px)


def _retnorm_bwd(dpx, dy, o_f, o_b, px):
    t_lat = o_f.shape[0]

    def body(dpx_hbm, dy_ref, of_ref, ob_ref, g_ref, do_ref, dg_ref):
        o = of_ref[...].astype(F32) + ob_ref[...].astype(F32)
        r = _rms(o)
        on = o * r
        g = g_ref[...].astype(F32)
        sg = _sigmoid(g)
        dy_ = dy_ref[...].astype(F32)
        dg_ref[...] = (dy_ * on * (sg * (1.0 + g * (1.0 - sg)))).astype(BF)
        do_ref[...] = _rms_bwd(dy_ * (g * sg), on, r).astype(BF)

    so = pl.BlockSpec((TM, RET_DV), lambda i, h: (i, h))
    gcol = pl.BlockSpec((TM, RET_DV), lambda i, h: (i, C_RG // RET_DV + h))
    return pl.pallas_call(
        body, name="retnorm_bwd", grid=(t_lat // TM, RET_HEADS), input_output_aliases={0: 1},
        in_specs=[ANY, so, so, so, gcol],
        out_specs=(so, gcol),
        out_shape=(SDS((t_lat, RET_HEADS * RET_DV), BF), SDS(dpx.shape, dpx.dtype)),
        compiler_params=_cp(("parallel", "parallel")))(dpx, dy, o_f, o_b, px)


def _norm_rope(x, w, cos, sin):
    xn = x * _rms(x) * w
    return xn * cos + _swap_pairs(xn) * sin


def _norm_rope_bwd(dy, x, w, cos, sin):
    dxn = dy * cos + _swap_pairs(dy * sin)
    r = _rms(x)
    xh = x * r
    return _rms_bwd(dxn * w, xh, r), jnp.sum(dxn * xh, axis=0, keepdims=True)


def _att_prep_q(px, cos_all, sin_all, qnw, t_lat):
    hd = ATT_HEAD_DIM
    wblk = ATT_REP * hd

    def body(x_ref, cos_ref, sin_ref, w_ref, o_ref):
        for r in range(ATT_REP):
            cols = slice(r * hd, (r + 1) * hd)
            qr = _norm_rope(x_ref[:, cols].astype(F32), w_ref[...], cos_ref[...], sin_ref[...])
            o_ref[:, cols] = (qr * (hd ** -0.5)).astype(BF)

    return pl.pallas_call(
        body, name="att_prep_q", grid=(t_lat // TM, ATT_KV_HEADS),
        in_specs=[pl.BlockSpec((TM, wblk), lambda i, g: (i, C_AQ // wblk + g)),
                  pl.BlockSpec((TM, hd), lambda i, g: (i, 0)),
                  pl.BlockSpec((TM, hd), lambda i, g: (i, 0)),
                  pl.BlockSpec((1, hd), lambda i, g: (0, 0))],
        out_specs=pl.BlockSpec((TM, wblk), lambda i, g: (i, g)),
        out_shape=SDS((t_lat, ATT_HEADS * hd), BF),
        compiler_params=_cp(("parallel", "parallel")))(px, cos_all, sin_all, qnw)


def _att_prep_kv(px, cos_all, sin_all, knw):
    rows = px.shape[0]
    hd = ATT_HEAD_DIM
    kvw = ATT_KV_HEADS * hd

    def body(x_ref, cos_ref, sin_ref, w_ref, k_ref, v_ref):
        for g in range(ATT_KV_HEADS):
            cols = slice(g * hd, (g + 1) * hd)
            k_ref[:, cols] = _norm_rope(x_ref[:, cols].astype(F32), w_ref[...], cos_ref[...],
                                        sin_ref[...]).astype(BF)
        v_ref[...] = x_ref[:, kvw:].astype(BF)

    return pl.pallas_call(
        body, name="att_prep_kv", grid=(rows // TM,),
        in_specs=[pl.BlockSpec((TM, 2 * kvw), lambda i: (i, C_AK // (2 * kvw))),
                  pl.BlockSpec((TM, hd), lambda i: (i, 0)),
                  pl.BlockSpec((TM, hd), lambda i: (i, 0)),
                  pl.BlockSpec((1, hd), lambda i: (0, 0))],
        out_specs=(pl.BlockSpec((TM, kvw), lambda i: (i, 0)), pl.BlockSpec((TM, kvw), lambda i: (i, 0))),
        out_shape=(SDS((rows, kvw), BF), SDS((rows, kvw), BF)),
        compiler_params=_cp(("parallel",)))(px, cos_all, sin_all, knw)


def _att_kv_bwd(dpx, dkl, dkc, dvl, dvc, px, cos_all, sin_all, knw):
    rows = px.shape[0]
    hd = ATT_HEAD_DIM
    kvw = ATT_KV_HEADS * hd
    n_lat = dkl.shape[0] // TM
    assert dkc.shape[0] == TM

    def body(dpx_hbm, dkl_ref, dkc_ref, dvl_ref, dvc_ref, x_ref, cos_ref, sin_ref, w_ref, o_ref, gw_ref):
        i = pl.program_id(0)

        @pl.when(i == 0)
        def _():
            gw_ref[...] = jnp.zeros_like(gw_ref)

        is_lat = i < n_lat
        dk = jnp.where(is_lat, dkl_ref[...], dkc_ref[...])
        dv = jnp.where(is_lat, dvl_ref[...], dvc_ref[...])
        for g in range(ATT_KV_HEADS):
            cols = slice(g * hd, (g + 1) * hd)
            dx, gw = _norm_rope_bwd(dk[:, cols], x_ref[:, cols].astype(F32), w_ref[...], cos_ref[...], sin_ref[...])
            o_ref[:, cols] = dx.astype(BF)
            gw_ref[...] += gw
        o_ref[:, kvw:] = dv.astype(BF)

    lat = pl.BlockSpec((TM, kvw), lambda i: (jnp.minimum(i, n_lat - 1), 0))
    ctx = pl.BlockSpec((TM, kvw), lambda i: (0, 0))
    kvcol = pl.BlockSpec((TM, 2 * kvw), lambda i: (i, C_AK // (2 * kvw)))
    return pl.pallas_call(
        body, name="att_kv_bwd", grid=(rows // TM,), input_output_aliases={0: 0},
        in_specs=[ANY, lat, ctx, lat, ctx, kvcol,
                  pl.BlockSpec((TM, hd), lambda i: (i, 0)),
                  pl.BlockSpec((TM, hd), lambda i: (i, 0)),
                  pl.BlockSpec((1, hd), lambda i: (0, 0))],
        out_specs=(kvcol, pl.BlockSpec((1, hd), lambda i: (0, 0))),
        out_shape=(SDS(dpx.shape, dpx.dtype), SDS((1, hd), F32)),
        compiler_params=_cp(("arbitrary",)))(dpx, dkl, dkc, dvl, dvc, px, cos_all, sin_all, knw)


def _stack_heads(ref_or_val):
    hd = ATT_HEAD_DIM
    return jnp.concatenate([ref_or_val[:, r * hd:(r + 1) * hd] for r in range(ATT_REP)], axis=0)


def _att_scores(q, kl, kc):
    sl = _dot(q, kl, 1, 1)
    sc = _dot(q, kc, 1, 1)
    m = jnp.maximum(jnp.max(sl, axis=-1, keepdims=True), jnp.max(sc, axis=-1, keepdims=True))
    el = jnp.exp(sl - m)
    ec = jnp.exp(sc - m)
    denom = jnp.sum(el, axis=-1, keepdims=True) + jnp.sum(ec, axis=-1, keepdims=True)
    return el, ec, denom, m


def _att_fwd(qn, kn, vn, n_samp, seq, lc):
    hd = ATT_HEAD_DIM
    tq = ATT_TQ
    nq = seq // tq
    wblk = ATT_REP * hd
    cb = n_samp * seq // lc
    t_lat = n_samp * seq

    def body(q_ref, kl_ref, kc_ref, vl_ref, vc_ref, o_ref, lse_ref):
        lane = lax.broadcasted_iota(jnp.int32, (tq, hd), 1)
        lse = jnp.zeros((tq, hd), F32)
        for r in range(ATT_REP):
            cols = slice(r * hd, (r + 1) * hd)
            el, ec, denom, m = _att_scores(q_ref[:, cols], kl_ref[...], kc_ref[...])
            o_ref[:, cols] = ((_dot(el, vl_ref[...]) + _dot(ec, vc_ref[...])) / denom).astype(BF)
            lse = jnp.where(lane == r, m + jnp.log(denom), lse)
        lse_ref[...] = lse

    return pl.pallas_call(
        body, name="att_fwd", grid=(n_samp, ATT_KV_HEADS, nq),
        in_specs=[pl.BlockSpec((tq, wblk), lambda b, g, i: (b * nq + i, g)),
                  pl.BlockSpec((seq, hd), lambda b, g, i: (b, g)),
                  pl.BlockSpec((lc, hd), lambda b, g, i: (cb + b, g)),
                  pl.BlockSpec((seq, hd), lambda b, g, i: (b, g)),
                  pl.BlockSpec((lc, hd), lambda b, g, i: (cb + b, g))],
        out_specs=(pl.BlockSpec((tq, wblk), lambda b, g, i: (b * nq + i, g)),
                   pl.BlockSpec((tq, hd), lambda b, g, i: (b * nq + i, g))),
        out_shape=(SDS((t_lat, ATT_HEADS * hd), BF), SDS((t_lat, ATT_KV_HEADS * hd), F32)),
        compiler_params=_cp(("parallel", "parallel", "parallel"), 48))(qn, kn, kn, vn, vn)


def _att_gate_bwd(dpx, dy_att, o_att, px):
    t_lat = dy_att.shape[0]
    wblk = ATT_REP * ATT_HEAD_DIM

    def body(dpx_hbm, dy_ref, o_ref, g_ref, out_ref):
        g = g_ref[...].astype(F32)
        sg = _sigmoid(g)
        out_ref[...] = (dy_ref[...].astype(F32) * o_ref[...].astype(F32) * (sg * (1.0 + g * (1.0 - sg)))).astype(BF)

    blk = pl.BlockSpec((TM, wblk), lambda i, j: (i, j))
    gcol = pl.BlockSpec((TM, wblk), lambda i, j: (i, C_AG // wblk + j))
    return pl.pallas_call(
        body, name="att_gate_bwd", grid=(t_lat // TM, ATT_KV_HEADS),
        in_specs=[ANY, blk, blk, gcol], out_specs=gcol, out_shape=SDS(dpx.shape, dpx.dtype),
        input_output_aliases={0: 0},
        compiler_params=_cp(("parallel", "parallel")))(dpx, dy_att, o_att, px)


def _att_bwd(dpx, qn, kn, vn, px, o_att, lse, do_att, cos_all, sin_all, qnw, n_samp, seq, lc, comm=None):
    hd = ATT_HEAD_DIM
    tq = ATT_TQ
    nq = seq // tq
    wblk = ATT_REP * hd
    cb = n_samp * seq // lc
    t_lat = n_samp * seq
    kvw = ATT_KV_HEADS * hd
    scale = hd ** -0.5

    def body(dpx_hbm, q_ref, kl_ref, kc_ref, vl_ref, vc_ref, o_ref, do_ref, x_ref, cos_ref, sin_ref, w_ref,
             lse_ref, dq_ref, dkl_ref, dkc_ref, dvl_ref, dvc_ref, gw_ref, akl, akc, avl, avc, aw):
        i = pl.program_id(2)

        @pl.when(i == 0)
        def _():
            akl[...] = jnp.zeros_like(akl)
            akc[...] = jnp.zeros_like(akc)
            avl[...] = jnp.zeros_like(avl)
            avc[...] = jnp.zeros_like(avc)
            aw[...] = jnp.zeros_like(aw)

        dobs, pls, pcs, dsls, dscs = [], [], [], [], []
        for r in range(ATT_REP):
            cols = slice(r * hd, (r + 1) * hd)
            dob = do_ref[:, cols]
            delta = jnp.sum(dob.astype(F32) * o_ref[:, cols].astype(F32), axis=-1, keepdims=True)
            lse = lse_ref[:, r:r + 1]
            p_l = jnp.exp(_dot(q_ref[:, cols], kl_ref[...], 1, 1) - lse).astype(BF)
            p_c = jnp.exp(_dot(q_ref[:, cols], kc_ref[...], 1, 1) - lse).astype(BF)
            ds_l = (p_l * (_dot(dob, vl_ref[...], 1, 1) - delta)).astype(BF)
            ds_c = (p_c * (_dot(dob, vc_ref[...], 1, 1) - delta)).astype(BF)
            dq = (_dot(ds_l, kl_ref[...]) + _dot(ds_c, kc_ref[...])) * scale
            dx, gw = _norm_rope_bwd(dq, x_ref[:, cols].astype(F32), w_ref[...], cos_ref[...], sin_ref[...])
            dq_ref[:, cols] = dx.astype(BF)
            aw[...] += gw
            dobs.append(dob)
            pls.append(p_l)
            pcs.append(p_c)
            dsls.append(ds_l)
            dscs.append(ds_c)
        do4 = jnp.concatenate(dobs, axis=0)
        q4 = _stack_heads(q_ref)
        avl[...] += _dot(jnp.concatenate(pls, axis=0), do4, 0, 0)
        avc[...] += _dot(jnp.concatenate(pcs, axis=0), do4, 0, 0)
        akl[...] += _dot(jnp.concatenate(dsls, axis=0), q4, 0, 0)
        akc[...] += _dot(jnp.concatenate(dscs, axis=0), q4, 0, 0)

        @pl.when(i == nq - 1)
        def _():
            dkl_ref[...] = akl[...]
            dkc_ref[...] = akc[...]
            dvl_ref[...] = avl[...]
            dvc_ref[...] = avc[...]
            gw_ref[...] = aw[...]

    return _call(
        body, [dpx, qn, kn, kn, vn, vn, o_att, do_att, px, cos_all, sin_all, qnw, lse], comm,
        name="att_bwd", grid=(n_samp, ATT_KV_HEADS, nq), aliases={0: 0},
        in_specs=[ANY,
                  pl.BlockSpec((tq, wblk), lambda b, g, i: (b * nq + i, g)),
                  pl.BlockSpec((seq, hd), lambda b, g, i: (b, g)),
                  pl.BlockSpec((lc, hd), lambda b, g, i: (cb + b, g)),
                  pl.BlockSpec((seq, hd), lambda b, g, i: (b, g)),
                  pl.BlockSpec((lc, hd), lambda b, g, i: (cb + b, g)),
                  pl.BlockSpec((tq, wblk), lambda b, g, i: (b * nq + i, g)),
                  pl.BlockSpec((tq, wblk), lambda b, g, i: (b * nq + i, g)),
                  pl.BlockSpec((tq, wblk), lambda b, g, i: (b * nq + i, C_AQ // wblk + g)),
                  pl.BlockSpec((tq, hd), lambda b, g, i: (b * nq + i, 0)),
                  pl.BlockSpec((tq, hd), lambda b, g, i: (b * nq + i, 0)),
                  pl.BlockSpec((1, hd), lambda b, g, i: (0, 0)),
                  pl.BlockSpec((tq, hd), lambda b, g, i: (b * nq + i, g))],
        out_specs=(pl.BlockSpec((tq, wblk), lambda b, g, i: (b * nq + i, C_AQ // wblk + g)),
                   pl.BlockSpec((seq, hd), lambda b, g, i: (b, g)),
                   pl.BlockSpec((lc, hd), lambda b, g, i: (b, g)),
                   pl.BlockSpec((seq, hd), lambda b, g, i: (b, g)),
                   pl.BlockSpec((lc, hd), lambda b, g, i: (b, g)),
                   pl.BlockSpec((None, None, 1, hd), lambda b, g, i: (b, g, 0, 0))),
        out_shape=(SDS(dpx.shape, dpx.dtype),
                   SDS((t_lat, kvw), F32), SDS((n_samp * lc, kvw), F32),
                   SDS((t_lat, kvw), F32), SDS((n_samp * lc, kvw), F32),
                   SDS((n_samp, ATT_KV_HEADS, 1, hd), F32)),
        scratch_shapes=[pltpu.VMEM((seq, hd), F32), pltpu.VMEM((lc, hd), F32),
                        pltpu.VMEM((seq, hd), F32), pltpu.VMEM((lc, hd), F32), pltpu.VMEM((1, hd), F32)],
        compiler_params=_cp(("arbitrary", "arbitrary", "arbitrary"), 56))


def _merge(x_lat, target, o_f, o_b, o_att, px, gate3, w_o_ret, w_o_att, w_out, tiles_per_sample):
    t_lat = x_lat.shape[0]
    tm = 256
    n_t = t_lat // tm
    per = tiles_per_sample * (TM // tm)
    d = D_MODEL
    rv = RET_HEADS * RET_DV
    n_samp = gate3.shape[0] - 1

    half = d // 2
    n_px = 10

    def body(x_ref, t_ref, of_ref, ob_ref, oa_ref, *rest):
        pxs, rest = rest[:n_px], rest[n_px:]
        (gt_ref, wor_ref, woa_ref, wout_ref,
         gx_ref, dor_ref, doa_ref, dpx_hbm, loss_ref, dgt_ref, gwor_hbm, gwoa_hbm, gwout_hbm,
         aor, aoa, aout, drg_ref, dtail_ref, sems) = rest
        i = pl.program_id(0)

        def copies(step):
            rows = pl.ds(pl.multiple_of(step * tm, tm), tm)
            return (pltpu.make_async_copy(drg_ref, dpx_hbm.at[rows, pl.ds(C_RG, rv)], sems.at[0]),
                    pltpu.make_async_copy(dtail_ref, dpx_hbm.at[rows, pl.ds(C_AG, 3 * d)], sems.at[1]))

        @pl.when(i == 0)
        def _():
            aor[...] = jnp.zeros_like(aor)
            aoa[...] = jnp.zeros_like(aoa)
            aout[...] = jnp.zeros_like(aout)
            loss_ref[...] = jnp.zeros_like(loss_ref)

        @pl.when(i % per == 0)
        def _():
            dgt_ref[...] = jnp.zeros_like(dgt_ref)

        def cat(refs):
            return jnp.concatenate([r[...] for r in refs], axis=1).astype(F32)

        def ret_head(h):
            cols = slice(h * RET_DV, (h + 1) * RET_DV)
            o = of_ref[:, cols].astype(F32) + ob_ref[:, cols].astype(F32)
            r = _rms(o)
            g = pxs[h][...].astype(F32)
            return o * r, r, g, _sigmoid(g)

        def att_half(k):
            o = oa_ref[:, k * half:(k + 1) * half].astype(F32)
            g = pxs[4 + k][...].astype(F32)
            return o, g, _sigmoid(g)

        yrs = []
        for h in range(RET_HEADS):
            on, _, g, sg = ret_head(h)
            yrs.append((on * (g * sg)).astype(BF))
        yr = jnp.concatenate(yrs, axis=1)
        yas = []
        for k in range(2):
            o, g, sg = att_half(k)
            yas.append((o * (g * sg)).astype(BF))
        ya = jnp.concatenate(yas, axis=1)

        a = jnp.dot(yr, wor_ref[...], preferred_element_type=F32)
        b = jnp.dot(ya, woa_ref[...], preferred_element_type=F32)
        sr = _sigmoid(cat(pxs[6:8]))
        sa = _sigmoid(cat(pxs[8:10]))
        yb = (sr * a + sa * b).astype(BF)
        out = jnp.dot(yb, wout_ref[...], preferred_element_type=F32)
        gate = gt_ref[...]
        err = x_ref[...] + gate * out - t_ref[...]
        loss_ref[...] += 0.5 * _sum_all(err * err) * (1.0 / d)
        dy_tok = err * (1.0 / d)
        gx_ref[...] = dy_tok
        dgt_ref[...] += jnp.sum(dy_tok * out, axis=0, keepdims=True)
        dout = (dy_tok * gate).astype(BF)
        aout[...] += _dot(yb, dout, 0, 0)
        dyy = _dot(dout, wout_ref[...], 1, 1)
        da = (dyy * sr).astype(BF)
        db = (dyy * sa).astype(BF)
        aor[...] += _dot(yr, da, 0, 0)
        aoa[...] += _dot(ya, db, 0, 0)
        dyr = _dot(da, wor_ref[...], 1, 1)
        dya = _dot(db, woa_ref[...], 1, 1)

        @pl.when(i > 0)
        def _():
            for cp in copies(i - 1):
                cp.wait()

        dtail_ref[:, d:2 * d] = (dyy * a * (sr * (1.0 - sr))).astype(BF)
        dtail_ref[:, 2 * d:] = (dyy * b * (sa * (1.0 - sa))).astype(BF)
        for h in range(RET_HEADS):
            cols = slice(h * RET_DV, (h + 1) * RET_DV)
            on, r, g, sg = ret_head(h)
            dy = dyr[:, cols]
            drg_ref[:, cols] = (dy * on * (sg * (1.0 + g * (1.0 - sg)))).astype(BF)
            dor_ref[:, cols] = _rms_bwd(dy * (g * sg), on, r).astype(BF)
        for k in range(2):
            cols = slice(k * half, (k + 1) * half)
            o, g, sg = att_half(k)
            dy = dya[:, cols]
            dtail_ref[:, cols] = (dy * o * (sg * (1.0 + g * (1.0 - sg)))).astype(BF)
            doa_ref[:, cols] = (dy * (g * sg)).astype(BF)
        for cp in copies(i):
            cp.start()

        @pl.when(i == n_t - 1)
        def _():
            for cp in copies(i):
                cp.wait()
            pltpu.sync_copy(aor, gwor_hbm)
            pltpu.sync_copy(aoa, gwoa_hbm)
            pltpu.sync_copy(aout, gwout_hbm)

    def px_blk(col):
        return pl.BlockSpec((tm, half), lambda i: (i, col // half))

    def resident(shape):
        return pl.BlockSpec(shape, lambda i: (0, 0), pipeline_mode=pl.Buffered(1))

    px_cols = ([C_RG + k * half for k in range(4)] + [C_AG, C_AG + half]
               + [C_MR, C_MR + half, C_MA, C_MA + half])
    return pl.pallas_call(
        body, name="merge", grid=(n_t,),
        in_specs=[pl.BlockSpec((tm, d), lambda i: (i, 0)),
                  pl.BlockSpec((tm, d), lambda i: (i, 0)),
                  pl.BlockSpec((tm, rv), lambda i: (i, 0)),
                  pl.BlockSpec((tm, rv), lambda i: (i, 0)),
                  pl.BlockSpec((tm, d), lambda i: (i, 0))]
        + [px_blk(col) for col in px_cols]
        + [pl.BlockSpec((None, 1, d), lambda i: (i // per, 0, 0)),
           resident((rv, d)), resident((d, d)), resident((d, d))],
        out_specs=(pl.BlockSpec((tm, d), lambda i: (i, 0)),
                   pl.BlockSpec((tm, rv), lambda i: (i, 0)),
                   pl.BlockSpec((tm, d), lambda i: (i, 0)),
                   ANY,
                   pl.BlockSpec((8, 128), lambda i: (0, 0)),
                   pl.BlockSpec((None, 1, d), lambda i: (i // per, 0, 0)),
                   ANY, ANY, ANY),
        out_shape=(SDS((t_lat, d), F32), SDS((t_lat, rv), BF), SDS((t_lat, d), BF),
                   SDS((px.shape[0], IN_COLS), BF),
                   SDS((8, 128), F32), SDS((n_samp, 1, d), F32),
                   SDS((rv, d), F32), SDS((d, d), F32), SDS((d, d), F32)),
        scratch_shapes=[pltpu.VMEM((rv, d), F32), pltpu.VMEM((d, d), F32), pltpu.VMEM((d, d), F32),
                        pltpu.VMEM((tm, rv), BF), pltpu.VMEM((tm, 3 * d), BF), pltpu.SemaphoreType.DMA((2,))],
        compiler_params=_cp(("arbitrary",), 56))(
            x_lat, target, o_f, o_b, o_att, *([px] * n_px), gate3, w_o_ret, w_o_att, w_out)


def _place():
    x, y, c = lax.axis_index("x"), lax.axis_index("y"), lax.axis_index("c")
    chips = [(1 - x, y), (x, 1 - y), (1 - x, 1 - y)]
    return x, y, c, chips


def _remote(src, dst, send_sem, recv_sem, to):
    return pltpu.make_async_remote_copy(src_ref=src, dst_ref=dst, send_sem=send_sem, recv_sem=recv_sem,
                                        device_id=to, device_id_type=MESH)


def _place_ids():
    x, y, c = lax.axis_index("x"), lax.axis_index("y"), lax.axis_index("c")
    me = 2 * x + y
    return jnp.stack([x, y, c, me, me, 2 * (1 - x) + y, 2 * x + 1 - y, 2 * (1 - x) + 1 - y]).astype(jnp.int32)


def _ag_comm(bufs, rels, arg_index=None):
    n, m = len(bufs), len(rels)

    def half(ref, s, which):
        h = ref.shape[1] // 2
        return ref.at[s, pl.ds(which * h, h), :]

    def ici(ins, outs, ssem, rsem, base):
        x, y, c, chips = _place()
        sends, recvs = [], []
        for a in range(n):
            for jj, j in enumerate(rels):
                k, chip = base + a * m + jj, chips[j]
                mine, theirs = half(outs[a], 2 * x + y, c), half(outs[a], 2 * chip[0] + chip[1], c)
                sends.append(_remote(mine, mine, ssem.at[k], rsem.at[k], (*chip, c)))
                recvs.append(_remote(theirs, theirs, ssem.at[k], rsem.at[k], (*chip, c)))
        return sends, recvs

    def d2d(ins, outs, ssem, rsem, base):
        x, y, c, chips = _place()
        sends, recvs = [], []
        for a in range(n):
            for jj, j in enumerate(rels):
                k, s = base + (n + a) * m + jj, 2 * chips[j][0] + chips[j][1]
                sends.append(_remote(half(outs[a], s, c), half(outs[a], s, c), ssem.at[k], rsem.at[k], (x, y, 1 - c)))
                recvs.append(_remote(half(outs[a], s, 1 - c), half(outs[a], s, 1 - c), ssem.at[k], rsem.at[k],
                                     (x, y, 1 - c)))
        return sends, recvs

    shapes = tuple(SDS(b.shape, b.dtype) for b in bufs)
    if arg_index is not None:
        return _Comm("all_gather", (), shapes, {}, 2 * n * m, (ici, d2d), ((arg_index, 0),))
    return _Comm("all_gather", tuple(bufs), shapes, {a: a for a in range(n)}, 2 * n * m, (ici, d2d))


def _swap_comm(grads):
    n = len(grads)

    def phase(ins, outs, ssem, rsem, base):
        x, y, c, _ = _place()
        sends = []
        for a in range(n):
            h = ins[a].shape[1] // 2
            sends.append(_remote(ins[a].at[:, pl.ds((1 - c) * h, h), :], outs[a], ssem.at[base + a],
                                 rsem.at[base + a], (x, y, 1 - c)))
        return sends, sends

    return _Comm("swap_halves", tuple(grads),
                 tuple(SDS((g.shape[0], g.shape[1] // 2, g.shape[2]), g.dtype) for g in grads), {}, n, (phase,))


def _exchange_comm(parts):
    n = len(parts)

    def phase(ins, outs, ssem, rsem, base):
        x, y, c, chips = _place()
        sends = []
        for a in range(n):
            for j, chip in enumerate(chips):
                k = base + 3 * a + j
                sends.append(_remote(ins[a].at[2 * chip[0] + chip[1]], outs[a].at[j], ssem.at[k], rsem.at[k],
                                     (*chip, c)))
        return sends, sends

    return _Comm("exchange_shards", tuple(parts), tuple(SDS((3,) + p.shape[1:], p.dtype) for p in parts), {}, 3 * n,
                 (phase,))


def _join_comm(bufs, n_parts=1):
    n = len(bufs)

    def phase(ins, outs, ssem, rsem, base):
        x, y, c, _ = _place()
        sends, recvs = [], []
        for a in range(n):
            h = outs[a].shape[0] // (2 * n_parts)
            for p in range(n_parts):
                k = base + a * n_parts + p
                mine = outs[a].at[pl.ds((2 * p + c) * h, h), :]
                other = outs[a].at[pl.ds((2 * p + 1 - c) * h, h), :]
                sends.append(_remote(mine, mine, ssem.at[k], rsem.at[k], (x, y, 1 - c)))
                recvs.append(_remote(other, other, ssem.at[k], rsem.at[k], (x, y, 1 - c)))
        return sends, recvs

    return _Comm("join_halves", tuple(bufs), tuple(SDS(b.shape, b.dtype) for b in bufs), {a: a for a in range(n)},
                 n * n_parts, (phase,))


def _cast_place(w, ids):
    rows, cols = w.shape
    tr = min(rows, 256)

    def body(ids_ref, w_ref, o_ref):
        o_ref[...] = w_ref[...].astype(BF)

    return pl.pallas_call(
        body, name="cast_place",
        grid_spec=pltpu.PrefetchScalarGridSpec(
            num_scalar_prefetch=1, grid=(rows // tr,),
            in_specs=[pl.BlockSpec((tr, cols), lambda i, ids_ref: (i, 0))],
            out_specs=pl.BlockSpec((None, tr, cols), lambda i, ids_ref: (ids_ref[3], i, 0))),
        out_shape=SDS((N_SHARD, rows, cols), BF),
        compiler_params=_cp(("parallel",), 40))(ids, w)


def _all_gather_weights(bufs):
    n = len(bufs)

    def body(*refs):
        outs = refs[n:2 * n]
        send_sems, recv_sems = refs[2 * n:]
        x, y, c, chips = _place()
        sibling = (x, y, 1 - c)
        me = 2 * x + y

        def half(ref, s, which):
            h = ref.shape[1] // 2
            return ref.at[s, pl.ds(which * h, h), :]

        first = []
        for a in range(n):
            for j, chip in enumerate(chips):
                k = a * 3 + j
                win = half(outs[a], me, c)
                first.append(_remote(win, win, send_sems.at[k], recv_sems.at[k], (*chip, c)))
        for cp in first:
            cp.start()
        passed = []
        for a in range(n):
            for j, chip in enumerate(chips):
                k = a * 3 + j
                win = half(outs[a], 2 * chip[0] + chip[1], c)
                _remote(win, win, send_sems.at[k], recv_sems.at[k], (*chip, c)).wait_recv()
                fw = _remote(win, win, send_sems.at[3 * n + k], recv_sems.at[3 * n + k], sibling)
                fw.start()
                passed.append(fw)
        for a in range(n):
            for j, chip in enumerate(chips):
                k = a * 3 + j
                win = half(outs[a], 2 * chip[0] + chip[1], 1 - c)
                _remote(win, win, send_sems.at[3 * n + k], recv_sems.at[3 * n + k], sibling).wait_recv()
        for cp in first + passed:
            cp.wait_send()

    return pl.pallas_call(
        body, name="all_gather_weights",
        in_specs=[ANY] * n, out_specs=tuple([ANY] * n),
        out_shape=tuple(SDS(b.shape, b.dtype) for b in bufs),
        input_output_aliases={a: a for a in range(n)},
        scratch_shapes=[pltpu.SemaphoreType.DMA((6 * n,)), pltpu.SemaphoreType.DMA((6 * n,))],
        compiler_params=_cp(has_side_effects=True))(*bufs)


def _swap_halves(grads):
    n = len(grads)

    def body(*refs):
        ins, outs = refs[:n], refs[n:2 * n]
        send_sems, recv_sems = refs[2 * n:]
        x, y, c, _ = _place()
        sibling = (x, y, 1 - c)

        def half(ref, which):
            h = ref.shape[1] // 2
            return ref.at[:, pl.ds(which * h, h), :]

        sends = [_remote(half(ins[a], 1 - c), outs[a], send_sems.at[a], recv_sems.at[a], sibling)
                 for a in range(n)]
        for cp in sends:
            cp.start()
        for cp in sends:
            cp.wait_recv()
        for cp in sends:
            cp.wait_send()

    return pl.pallas_call(
        body, name="swap_halves",
        in_specs=[ANY] * n, out_specs=tuple([ANY] * n),
        out_shape=tuple(SDS((g.shape[0], g.shape[1] // 2, g.shape[2]), g.dtype) for g in grads),
        scratch_shapes=[pltpu.SemaphoreType.DMA((n,)), pltpu.SemaphoreType.DMA((n,))],
        compiler_params=_cp(has_side_effects=True))(*grads)


def _chip_sum(g, p, ids):
    n_s, rows, cols = g.shape
    h = rows // 2
    tr = min(h, 256)
    nb = h // tr

    def body(ids_ref, g_ref, p_ref, o_ref, o16_ref):
        t = g_ref[...] + p_ref[...]
        o_ref[...] = t
        o16_ref[...] = t.astype(BF)

    out_spec = pl.BlockSpec((None, tr, cols), lambda s, i, ids_ref: (s, i, 0))
    return pl.pallas_call(
        body, name="chip_sum",
        grid_spec=pltpu.PrefetchScalarGridSpec(
            num_scalar_prefetch=1, grid=(n_s, nb),
            in_specs=[pl.BlockSpec((None, tr, cols), lambda s, i, ids_ref: (s, ids_ref[2] * nb + i, 0)),
                      pl.BlockSpec((None, tr, cols), lambda s, i, ids_ref: (s, i, 0))],
            out_specs=(out_spec, out_spec)),
        out_shape=(SDS((n_s, h, cols), g.dtype), SDS((n_s, h, cols), BF)),
        compiler_params=_cp(("parallel", "parallel"), 40))(ids, g, p)


def _exchange_shards(parts):
    n = len(parts)

    def body(*refs):
        ins, outs = refs[:n], refs[n:2 * n]
        send_sems, recv_sems = refs[2 * n:]
        x, y, c, chips = _place()
        sends = []
        for a in range(n):
            for j, chip in enumerate(chips):
                k = a * 3 + j
                sends.append(_remote(ins[a].at[2 * chip[0] + chip[1]], outs[a].at[j],
                                     send_sems.at[k], recv_sems.at[k], (*chip, c)))
        for cp in sends:
            cp.start()
        for cp in sends:
            cp.wait_recv()
        for cp in sends:
            cp.wait_send()

    return pl.pallas_call(
        body, name="exchange_shards",
        in_specs=[ANY] * n, out_specs=tuple([ANY] * n),
        out_shape=tuple(SDS((3,) + p.shape[1:], p.dtype) for p in parts),
        scratch_shapes=[pltpu.SemaphoreType.DMA((3 * n,)), pltpu.SemaphoreType.DMA((3 * n,))],
        compiler_params=_cp(has_side_effects=True))(*parts)


def _shard_sum(t, q, ids, part=0, n_parts=1, buf=None):
    _, h, cols = t.shape
    tr = min(h, 256)
    nb = h // tr

    def body(ids_ref, t_ref, q_ref, *rest):
        rest[-1][...] = ((t_ref[...] + q_ref[0].astype(F32)) + q_ref[1].astype(F32)) + q_ref[2].astype(F32)

    args, in_specs, aliases = [t, q], [
        pl.BlockSpec((None, tr, cols), lambda i, ids_ref: (ids_ref[3], i, 0)),
        pl.BlockSpec((3, tr, cols), lambda i, ids_ref: (0, i, 0))], None
    if buf is not None:
        args, in_specs, aliases = args + [buf], in_specs + [ANY], {2: 0}
    return _call(body, args, None, name="shard_sum", grid=(nb,), in_specs=in_specs,
                 out_specs=pl.BlockSpec((tr, cols), lambda i, ids_ref: ((2 * part + ids_ref[2]) * nb + i, 0)),
                 out_shape=SDS((2 * h * n_parts, cols), t.dtype), aliases=aliases, prefetch=ids,
                 compiler_params=_cp(("parallel",), 40))


def _join_halves(bufs):
    n = len(bufs)

    def body(*refs):
        outs = refs[n:2 * n]
        send_sems, recv_sems = refs[2 * n:]
        x, y, c, _ = _place()
        sibling = (x, y, 1 - c)

        def win(ref, which):
            h = ref.shape[0] // 2
            return ref.at[pl.ds(which * h, h), :]

        sends = [_remote(win(outs[a], c), win(outs[a], c), send_sems.at[a], recv_sems.at[a], sibling)
                 for a in range(n)]
        for cp in sends:
            cp.start()
        for a in range(n):
            other = win(outs[a], 1 - c)
            _remote(other, other, send_sems.at[a], recv_sems.at[a], sibling).wait_recv()
        for cp in sends:
            cp.wait_send()

    return pl.pallas_call(
        body, name="join_halves",
        in_specs=[ANY] * n, out_specs=tuple([ANY] * n),
        out_shape=tuple(SDS(b.shape, b.dtype) for b in bufs),
        input_output_aliases={a: a for a in range(n)},
        scratch_shapes=[pltpu.SemaphoreType.DMA((n,)), pltpu.SemaphoreType.DMA((n,))],
        compiler_params=_cp(has_side_effects=True))(*bufs)


def _gather_small(block, n_sum):
    rows, cols = block.shape
    n_dev = 8

    def body(x_ref, o_ref, g_ref, buf, send_sems, recv_sems, local_sem):
        x, y, c, chips = _place()
        me, sibling = (x, y, c), (x, y, 1 - c)

        def slot(px_, py_, pc_):
            return buf.at[4 * px_ + 2 * py_ + pc_]

        def copy(k, who, to, src=None):
            return _remote(slot(*who) if src is None else src, slot(*who), send_sems.at[k], recv_sems.at[k], to)

        mine = pltpu.make_async_copy(x_ref, slot(*me), local_sem)
        mine.start()
        first = [copy(0, me, sibling, src=x_ref)]
        first += [copy(1 + j, me, (*chip, c), src=x_ref) for j, chip in enumerate(chips)]
        for cp in first:
            cp.start()
        passed = [copy(4 + j, (*chip, c), sibling) for j, chip in enumerate(chips)]
        for j, chip in enumerate(chips):
            copy(1 + j, (*chip, c), me).wait_recv()
            passed[j].start()
        copy(0, sibling, me).wait_recv()
        for j, chip in enumerate(chips):
            copy(4 + j, (*chip, 1 - c), me).wait_recv()
        for cp in first + passed:
            cp.wait_send()
        mine.wait()
        acc = buf[0, :, :n_sum]
        for s in range(1, n_dev):
            acc = acc + buf[s, :, :n_sum]
        o_ref[...] = acc
        for s in range(n_dev):
            g_ref[s * rows:(s + 1) * rows, :] = buf[s, :, n_sum:]

    return pl.pallas_call(
        body, name="gather_small",
        in_specs=[pl.BlockSpec(memory_space=pltpu.VMEM)],
        out_specs=(pl.BlockSpec(memory_space=pltpu.VMEM), pl.BlockSpec(memory_space=pltpu.VMEM)),
        out_shape=(SDS((rows, n_sum), F32), SDS((n_dev * rows, cols - n_sum), F32)),
        scratch_shapes=[pltpu.VMEM((n_dev, rows, cols), F32), pltpu.SemaphoreType.DMA((7,)),
                        pltpu.SemaphoreType.DMA((7,)), pltpu.SemaphoreType.DMA],
        compiler_params=_cp(has_side_effects=True))(block)


def _adam_math(w, g, m, v):
    m = ADAM_B1 * m + (1.0 - ADAM_B1) * g
    v = ADAM_B2 * v + (1.0 - ADAM_B2) * (g * g)
    m_hat = m / (1.0 - ADAM_B1 ** ADAM_STEP)
    v_hat = v / (1.0 - ADAM_B2 ** ADAM_STEP)
    delta = -ADAM_LR * (m_hat / (jnp.sqrt(v_hat) + ADAM_EPS) + ADAM_WD * w)
    return delta, m, v


def _adamw(w, g, m, v):
    rows, cols = w.shape
    tr = min(rows, 256 if cols <= 2048 else 128)

    def body(w_ref, g_ref, m_ref, v_ref, go_ref, d_ref, nm_ref, nv_ref):
        g = g_ref[...]
        go_ref[...] = g
        d_ref[...], nm_ref[...], nv_ref[...] = _adam_math(w_ref[...], g, m_ref[...], v_ref[...])

    spec = pl.BlockSpec((tr, cols), lambda i: (i, 0))
    return pl.pallas_call(
        body, name="adamw", grid=(rows // tr,), in_specs=[spec] * 4, out_specs=(spec,) * 4,
        out_shape=(SDS(w.shape, F32),) * 4, compiler_params=_cp(("parallel",), 40))(w, g, m, v)


def _adamw_small(w, g, m, v):
    def body(w_ref, g_ref, m_ref, v_ref, go_ref, d_ref, nm_ref, nv_ref):
        w = w_ref[...]
        g = g_ref[...]
        sub = lax.broadcasted_iota(jnp.int32, w.shape, 0)
        lane = lax.broadcasted_iota(jnp.int32, w.shape, 1)
        is_ret = jnp.logical_and(sub == 5, lane < 2 * RET_HEADS)
        u = jnp.exp(jnp.where(is_ret, w, -1.0) * jnp.log(2.0))
        g = jnp.where(is_ret, g * (-u * jnp.log(2.0) / (1.0 - u)), g)
        go_ref[...] = g
        d_ref[...], nm_ref[...], nv_ref[...] = _adam_math(w, g, m_ref[...], v_ref[...])

    return pl.pallas_call(body, name="adamw_small", out_shape=(SDS(w.shape, F32),) * 4)(w, g, m, v)


def _rope_tables(seq, n_samp, n_ctx_rows):
    rows = seq // GRID_W
    row = jnp.repeat(jnp.arange(rows, dtype=F32), GRID_W)
    col = jnp.tile(jnp.arange(GRID_W, dtype=F32), rows)
    half = ATT_HEAD_DIM // 2
    freqs = ROPE_THETA ** (-jnp.arange(0, half, 2, dtype=F32) / half)
    ang = jnp.concatenate([row[:, None] * freqs, col[:, None] * freqs], axis=-1)
    cos, sin = jnp.cos(ang), jnp.sin(ang)
    cos_f = jnp.repeat(cos, 2, axis=1)
    sin_s = jnp.stack([-sin, sin], axis=-1).reshape(seq, ATT_HEAD_DIM)
    cos_all = jnp.concatenate([jnp.tile(cos_f, (n_samp, 1)), jnp.ones((n_ctx_rows, ATT_HEAD_DIM), F32)], axis=0)
    sin_all = jnp.concatenate([jnp.tile(sin_s, (n_samp, 1)), jnp.zeros((n_ctx_rows, ATT_HEAD_DIM), F32)], axis=0)
    return cos_all, sin_all


def _pack_small(c_ctx, norm_w, b_ada, ret, qn, kn):
    d = D_MODEL
    row5 = jnp.concatenate([ret.reshape(-1), jnp.zeros((128 - 2 * RET_HEADS,), F32), qn.reshape(-1), kn.reshape(-1),
                            jnp.zeros((d - 384,), F32)])
    return jnp.concatenate([c_ctx.reshape(1, d), norm_w.reshape(1, d), b_ada.reshape(3, d), row5.reshape(1, d),
                            jnp.zeros((2, d), F32)], axis=0)


def _unpack_small(p):
    d = D_MODEL
    return (p[0], p[1:2], p[2:5].reshape(1, 3 * d), p[5, :2 * RET_HEADS].reshape(1, 2, RET_HEADS),
            p[5:6, 128:256], p[5:6, 256:384])


def _step(x, c, ctx, c_ctx, norm_w, b_ada, ret_log2_decay, q_norm_w, k_norm_w, loss_target, weights, ids, dist):
    n_samp, seq, d = x.shape
    lc = ctx.shape[1]
    t_lat, t_ctx = n_samp * seq, n_samp * lc
    assert seq % TM == 0 and t_ctx == TM and t_lat % lc == 0 and seq % GRID_W == 0
    tps = seq // TM

    x_lat = x.reshape(t_lat, d)
    x_ctx = ctx.reshape(t_ctx, d)
    cvec8 = jnp.concatenate([c, c_ctx.reshape(1, d), jnp.zeros((8 - n_samp - 1, d), F32)], axis=0)
    lg = jnp.log1p(-jnp.exp2(ret_log2_decay.reshape(2, RET_HEADS)))
    cos_all, sin_all = _rope_tables(seq, n_samp, t_ctx)

    w_ada_b, w_in_b, w_or_b, w_oa_b, w_out_b = weights
    w_ada_g = _run_comm(_ag_comm((w_ada_b,), (0, 1, 2)))[0] if dist else w_ada_b
    mod8 = _adaln_fwd(cvec8, w_ada_g, b_ada)
    mod3 = mod8[:n_samp + 1]
    shift3 = mod3[:, None, 0:d]
    scale3 = mod3[:, None, d:2 * d]
    gate3 = mod3[:, None, 2 * d:3 * d]

    hx, hxt = _norm_fwd(x_lat, x_ctx, norm_w, scale3, shift3, tps, n_samp)
    if dist:
        px, w_in_g = _in_proj_gather(hx, w_in_b, ids)
    else:
        w_in_g = w_in_b
        px = _in_proj(hx, w_in_g, ids, 0, N_SHARD)

    states0 = _ctx_state_fwd(px, lg, n_samp, t_lat, lc)
    if dist:
        (o_f, o_b, saved), w_o = _ret_fwd(px, states0, lg, n_samp, seq,
                                          comm=_ag_comm((w_or_b, w_oa_b, w_out_b), (0, 1, 2)))
    else:
        (o_f, o_b, saved), w_o = _ret_fwd(px, states0, lg, n_samp, seq), (w_or_b, w_oa_b, w_out_b)
    w_o_ret, w_o_att, w_out = (w.reshape(-1, d) for w in w_o)

    qn = _att_prep_q(px, cos_all, sin_all, q_norm_w, t_lat)
    kn, vn = _att_prep_kv(px, cos_all, sin_all, k_norm_w)
    o_att, lse = _att_fwd(qn, kn, vn, n_samp, seq, lc)

    (gx_res, do, do_att, dpx, loss8, dgate, g_w_o_ret, g_w_o_att, g_w_out) = _merge(
        x_lat, loss_target.reshape(t_lat, d), o_f, o_b, o_att, px, gate3, w_o_ret, w_o_att, w_out, tps)

    g_a = [g.reshape(N_SHARD, -1, d) for g in (g_w_o_ret, g_w_o_att, g_w_out)]
    res = _att_bwd(dpx, qn, kn, vn, px, o_att, lse, do_att, cos_all, sin_all, q_norm_w, n_samp, seq, lc,
                   comm=_swap_comm(g_a) if dist else None)
    (dpx, dkl, dkc, dvl, dvc, gqw), sib_a = res if dist else (res, None)
    dpx, gkw = _att_kv_bwd(dpx, dkl, dkc, dvl, dvc, px, cos_all, sin_all, k_norm_w)
    if dist:
        t_a = [_chip_sum(g, p, ids) for g, p in zip(g_a, sib_a)]

    res = _ret_bwd(dpx, px, do, saved, lg, n_samp, seq,
                   comm=_exchange_comm([t16 for _, t16 in t_a]) if dist else None)
    (dpx, dstates, dlg_lat), q_a = res if dist else (res, None)
    if dist:
        r_a = [_shard_sum(t, q, ids) for (t, _), q in zip(t_a, q_a)]
    dpx, dlg_ctx = _ctx_state_bwd(dpx, px, dstates, lg, n_samp, t_lat, lc)
    dpx = _zero_ctx_tail(dpx, t_lat)

    n_tiles = dpx.shape[0] // _big_rows(dpx.shape[0])
    if dist:
        g_b = _gw_in(hxt, dpx, 0, 1)
        dhx, (sib_b, *r_a) = _dhx(dpx, w_in_g, 0, 1, comm=_join_comms(_swap_comm([g_b]), _join_comm(r_a)))
        t_b, t16_b = _chip_sum(g_b, sib_b, ids)
        dhx, (q_b,) = _dhx(dpx, w_in_g, 1, n_tiles - 1, dhx=dhx, comm=_exchange_comm([t16_b]))
        r_b_half = _shard_sum(t_b, q_b, ids)
    else:
        g_w_in = _gw_in(hxt, dpx, 0, 1)
        dhx = _dhx(dpx, w_in_g, 0, n_tiles)
    grad_x, dshift, dscale, g_norm_w = _norm_bwd(x_lat, x_ctx, dhx, gx_res, norm_w, scale3, tps, n_samp)

    dgate_all = jnp.concatenate([dgate, jnp.zeros((1, 1, d), F32)], axis=0)
    dmod3 = jnp.concatenate([dshift, dscale, dgate_all], axis=2).reshape(n_samp + 1, 3 * d)
    dmod8 = jnp.concatenate([dmod3, jnp.zeros((8 - n_samp - 1, 3 * d), F32)], axis=0)
    g_lg = (jnp.sum(dlg_lat[:, :, 0], axis=0).reshape(2, RET_HEADS)
            + jnp.stack([jnp.sum(dlg_ctx[:, :, 0, 0], axis=0), jnp.sum(dlg_ctx[:, :, 1, 0], axis=0)], axis=0))
    g_qw = jnp.sum(gqw, axis=(0, 1, 2))
    zero = jnp.zeros((d,), F32)
    if not dist:
        g_w_ada, g_b_ada, dc8 = _adaln_bwd(cvec8, dmod8, w_ada_g)
        small = _pack_small(dc8[n_samp], g_norm_w, g_b_ada, g_lg, g_qw, gkw)
        return (loss8[0, 0], grad_x.reshape(n_samp, seq, d),
                (g_w_ada, g_w_in, g_w_o_ret, g_w_o_att, g_w_out), small)

    local = _pack_small(zero, g_norm_w, jnp.zeros((3 * d,), F32), g_lg, g_qw, gkw).at[6, 0].set(loss8[0, 0])
    small_sum, gathered = _gather_small(jnp.concatenate([local, cvec8, dmod8], axis=1), d)
    (g_w_ada, g_b_ada, dc_all), (r_b,) = _adaln_bwd(gathered[:, :d], gathered[:, d:], w_ada_g,
                                                     comm=_join_comm([r_b_half]))
    dc_ctx = jnp.sum(dc_all.reshape(-1, 8, d)[:, n_samp], axis=0)
    small = small_sum + _pack_small(dc_ctx, zero, g_b_ada, jnp.zeros((2, RET_HEADS), F32), zero[:128], zero[:128])
    r_c = lax.dynamic_index_in_dim(g_w_ada, ids[3], 0, keepdims=False)
    return small[6, 0], grad_x.reshape(n_samp, seq, d), (r_c, r_b, *r_a), small


def kernel(x, c, ctx, c_ctx, norm_w, w_ada, b_ada, w_in, ret_log2_decay, q_norm_w, k_norm_w, w_o_ret, w_o_att, w_out, loss_target, m_c_ctx, m_norm_w, m_w_ada, m_b_ada, m_w_in, m_ret_log2_decay, m_q_norm_w, m_k_norm_w, m_w_o_ret, m_w_o_att, m_w_out, v_c_ctx, v_norm_w, v_w_ada, v_b_ada, v_w_in, v_ret_log2_decay, v_q_norm_w, v_k_norm_w, v_w_o_ret, v_w_o_att, v_w_out):
    big_w = (w_ada[0], w_in[0], w_o_ret[0], w_o_att[0], w_out[0])
    big_m = (m_w_ada[0], m_w_in[0], m_w_o_ret[0], m_w_o_att[0], m_w_out[0])
    big_v = (v_w_ada[0], v_w_in[0], v_w_o_ret[0], v_w_o_att[0], v_w_out[0])

    ids = _place_ids()
    loss, grad_x, big_grad, small_grad_in = _step(
        x, c, ctx, c_ctx, norm_w[0:1], b_ada[0:1], ret_log2_decay[0], q_norm_w[0:1], k_norm_w[0:1], loss_target,
        tuple(_cast_place(w, ids) for w in big_w), ids, True)
    small_w = _pack_small(c_ctx, norm_w, b_ada, ret_log2_decay, q_norm_w, k_norm_w)
    small_m = _pack_small(m_c_ctx, m_norm_w, m_b_ada, m_ret_log2_decay, m_q_norm_w, m_k_norm_w)
    small_v = _pack_small(v_c_ctx, v_norm_w, v_b_ada, v_ret_log2_decay, v_q_norm_w, v_k_norm_w)
    small_grad, small_delta, small_nm, small_nv = _adamw_small(small_w, small_grad_in, small_m, small_v)

    big_g, big_delta, big_nm, big_nv = [], [], [], []
    for w, g, m, v in zip(big_w, big_grad, big_m, big_v):
        go, dlt, nm, nv = _adamw(w, g, m, v)
        big_g.append(go[None])
        big_delta.append(dlt[None])
        big_nm.append(nm[None])
        big_nv.append(nv[None])
    big_grad = big_g

    def order(small_packed, big):
        s = _unpack_small(small_packed)
        return (s[0], s[1], big[0], s[2], big[1], s[3], s[4], s[5], big[2], big[3], big[4])

    return (loss, grad_x, *order(small_grad, big_grad), *order(small_delta, big_delta),
            *order(small_nm, big_nm), *order(small_nv, big_nv))
```

```python
import functools
from typing import NamedTuple

import jax
import jax.numpy as jnp
from jax import lax
from jax.experimental import pallas as pl
from jax.experimental.pallas import tpu as pltpu

F32 = jnp.float32
BF = jnp.bfloat16
SDS = jax.ShapeDtypeStruct
MESH = pl.DeviceIdType.MESH
ANY = pl.BlockSpec(memory_space=pl.ANY)
SMEM = pl.BlockSpec(memory_space=pltpu.SMEM)

D_MODEL = 1024
GRID_W = 64
RET_HEADS = 4
RET_DK = 256
RET_DV = 512
RET_CHUNK = 128
ATT_HEADS = 8
ATT_KV_HEADS = 2
ATT_REP = ATT_HEADS // ATT_KV_HEADS
ATT_HEAD_DIM = 128
ROPE_THETA = 10000.0
NORM_EPS = 1e-6
IN_COLS = 10752
KV_COLS = 3584
C_RK, C_RV, C_AK, C_AV, C_RQ, C_RG, C_AQ, C_AG, C_MR, C_MA = 0, 1024, 3072, 3328, 3584, 4608, 6656, 7680, 8704, 9728
N_SHARD = 4
ADA_W = 3 * D_MODEL // N_SHARD
IN_W = IN_COLS // N_SHARD
IN_BLK = IN_W
BPS = IN_W // IN_BLK
N_IN_BLK = IN_COLS // IN_BLK
TM = 512
ATT_TQ = 512
ADAM_LR, ADAM_B1, ADAM_B2, ADAM_EPS, ADAM_WD, ADAM_STEP = 0.001, 0.9, 0.999, 1e-08, 0.01, 10
MIB = 1024 * 1024


def _cp(sem=None, vmem_mb=None, **kw):
    if sem is not None:
        kw["dimension_semantics"] = sem
    if vmem_mb is not None:
        kw["vmem_limit_bytes"] = vmem_mb * MIB
    return pltpu.CompilerParams(**kw)


def _dot(a, b, ca=1, cb=0):
    return lax.dot_general(a.astype(BF), b.astype(BF), (((ca,), (cb,)), ((), ())), preferred_element_type=F32)


def _sigmoid(x):
    return 0.5 * jnp.tanh(0.5 * x) + 0.5


def _sum_all(x):
    return jnp.sum(jnp.sum(x, axis=1, keepdims=True), axis=0, keepdims=True)


def _swap_pairs(x):
    ax = x.ndim - 1
    lane = lax.broadcasted_iota(jnp.int32, x.shape, ax)
    nxt = pltpu.roll(x, x.shape[ax] - 1, ax)
    prv = pltpu.roll(x, 1, ax)
    return jnp.where(lane % 2 == 0, nxt, prv)


def _rms(x):
    return lax.rsqrt(jnp.mean(x * x, axis=-1, keepdims=True) + NORM_EPS)


def _rms_bwd(dxh, xh, r):
    return r * (dxh - xh * jnp.mean(dxh * xh, axis=-1, keepdims=True))


class _Comm(NamedTuple):
    name: str
    ins: tuple
    out_shapes: tuple
    aliases: dict
    n_sems: int
    phases: tuple
    arg_aliases: tuple = ()


def _join_comms(*comms):
    comms = [cm for cm in comms if cm is not None]
    if len(comms) <= 1:
        return comms[0] if comms else None
    offs, i_off, o_off, s_off = [], 0, 0, 0
    for cm in comms:
        offs.append((i_off, o_off, s_off))
        i_off, o_off, s_off = i_off + len(cm.ins), o_off + len(cm.out_shapes), s_off + cm.n_sems

    def phase(k):
        def run(ins, outs, ssem, rsem, base):
            sends, recvs = [], []
            for cm, (io, oo, so) in zip(comms, offs):
                if k < len(cm.phases):
                    s, r = cm.phases[k](ins[io:io + len(cm.ins)], outs[oo:oo + len(cm.out_shapes)], ssem, rsem,
                                        base + so)
                    sends += s
                    recvs += r
            return sends, recvs
        return run

    aliases, arg_aliases = {}, ()
    for cm, (io, oo, _) in zip(comms, offs):
        aliases.update({io + a: oo + b for a, b in cm.aliases.items()})
        arg_aliases += tuple((a, oo + b) for a, b in cm.arg_aliases)
    return _Comm("+".join(cm.name for cm in comms), sum((cm.ins for cm in comms), ()),
                 sum((cm.out_shapes for cm in comms), ()), aliases, s_off,
                 tuple(phase(k) for k in range(max(len(cm.phases) for cm in comms))), arg_aliases)


def _run_phases(comm, cins, couts, ssem, rsem, first_started):
    for k, phase in enumerate(comm.phases):
        sends, recvs = phase(cins, couts, ssem, rsem, 0)
        if k > 0 or not first_started:
            for cp in sends:
                cp.start()
        for cp in recvs:
            cp.wait_recv()
        for cp in sends:
            cp.wait_send()


def _call(body, args, comm=None, *, name, grid, in_specs, out_specs, out_shape, scratch_shapes=(),
          compiler_params, aliases=None, prefetch=None):
    single = not isinstance(out_shape, (tuple, list))
    out_specs_t = (out_specs,) if single else tuple(out_specs)
    out_shape_t = (out_shape,) if single else tuple(out_shape)
    n_pre = 0 if prefetch is None else 1
    n_in, n_out, n_sc = len(in_specs), len(out_specs_t), len(scratch_shapes)
    io_alias = {n_pre + a: b for a, b in (aliases or {}).items()}
    if comm is None:
        kernel_body, cin, cout, csems = body, [], [], []
    else:
        n_ci, n_co = len(comm.ins), len(comm.out_shapes)
        cin, cout = [ANY] * n_ci, [ANY] * n_co
        csems = [pltpu.SemaphoreType.DMA((comm.n_sems,)), pltpu.SemaphoreType.DMA((comm.n_sems,))]
        io_alias.update({n_pre + n_in + a: n_out + b for a, b in comm.aliases.items()})
        io_alias.update({n_pre + a: n_out + b for a, b in comm.arg_aliases})

        def kernel_body(*refs):
            pre, refs = refs[:n_pre], refs[n_pre:]
            ins, cins = refs[:n_in], refs[n_in:n_in + n_ci]
            outs = refs[n_in + n_ci:n_in + n_ci + n_out]
            couts = refs[n_in + n_ci + n_out:n_in + n_ci + n_out + n_co]
            scratch = refs[n_in + n_ci + n_out + n_co:n_in + n_ci + n_out + n_co + n_sc]
            ssem, rsem = refs[-2:]
            first = functools.reduce(jnp.logical_and, [pl.program_id(k) == 0 for k in range(len(grid))])
            last = functools.reduce(jnp.logical_and, [pl.program_id(k) == grid[k] - 1 for k in range(len(grid))])

            @pl.when(first)
            def _():
                for cp in comm.phases[0](cins, couts, ssem, rsem, 0)[0]:
                    cp.start()

            body(*pre, *ins, *outs, *scratch)

            @pl.when(last)
            def _():
                _run_phases(comm, cins, couts, ssem, rsem, True)

        name = name + "+" + comm.name

    all_in, all_out = list(in_specs) + cin, out_specs_t + tuple(cout)
    shapes = out_shape_t + (tuple(comm.out_shapes) if comm is not None else ())
    scratch = list(scratch_shapes) + csems
    if prefetch is None:
        res = pl.pallas_call(kernel_body, name=name, grid=grid, in_specs=all_in, out_specs=all_out, out_shape=shapes,
                             scratch_shapes=scratch, input_output_aliases=io_alias,
                             compiler_params=compiler_params)(*args, *(comm.ins if comm is not None else ()))
    else:
        res = pl.pallas_call(
            kernel_body, name=name, out_shape=shapes, input_output_aliases=io_alias, compiler_params=compiler_params,
            grid_spec=pltpu.PrefetchScalarGridSpec(num_scalar_prefetch=1, grid=grid, in_specs=all_in,
                                                   out_specs=all_out, scratch_shapes=scratch))(
                                                       prefetch, *args, *(comm.ins if comm is not None else ()))
    own = res[0] if single else tuple(res[:n_out])
    return own if comm is None else (own, tuple(res[n_out:]))


def _run_comm(comm):
    n_ci, n_co = len(comm.ins), len(comm.out_shapes)

    def body(*refs):
        _run_phases(comm, refs[:n_ci], refs[n_ci:n_ci + n_co], refs[-2], refs[-1], False)

    return pl.pallas_call(
        body, name=comm.name, in_specs=[ANY] * n_ci, out_specs=tuple([ANY] * n_co), out_shape=tuple(comm.out_shapes),
        input_output_aliases=dict(comm.aliases),
        scratch_shapes=[pltpu.SemaphoreType.DMA((comm.n_sems,)), pltpu.SemaphoreType.DMA((comm.n_sems,))],
        compiler_params=_cp(has_side_effects=True))(*comm.ins)


def _adaln_fwd(cvec8, w_ada_g, b_ada):
    def body(c_ref, w_ref, b_ref, o_ref):
        cv = c_ref[...]
        sc = (cv * _sigmoid(cv)).astype(BF)
        for s in range(N_SHARD):
            cols = slice(s * ADA_W, (s + 1) * ADA_W)
            o_ref[:, cols] = jnp.dot(sc, w_ref[s], preferred_element_type=F32) + b_ref[:, cols]

    return pl.pallas_call(body, out_shape=SDS((8, 3 * D_MODEL), F32), name="adaln_fwd",
                          compiler_params=_cp(vmem_mb=32))(cvec8, w_ada_g, b_ada)


def _adaln_bwd(cvec, dmod, w_ada_g, comm=None):
    n_rows = cvec.shape[0]
    def body(c_ref, d_ref, w_ref, gw_ref, gb_ref, dc_ref):
        cv = c_ref[...]
        sg = _sigmoid(cv)
        sc = cv * sg
        dm = d_ref[...]
        gb_ref[...] = jnp.sum(dm, axis=0, keepdims=True)
        dsc = jnp.zeros(cv.shape, F32)
        for s in range(N_SHARD):
            cols = slice(s * ADA_W, (s + 1) * ADA_W)
            gw_ref[s] = _dot(sc, dm[:, cols], 0, 0)
            dsc = dsc + _dot(dm[:, cols], w_ref[s], 1, 1)
        dc_ref[...] = dsc * (sg * (1.0 + cv * (1.0 - sg)))

    def whole(shape):
        return pl.BlockSpec(shape, lambda i: (0,) * len(shape))

    shapes = ((N_SHARD, D_MODEL, ADA_W), (1, 3 * D_MODEL), (n_rows, D_MODEL))
    return _call(body, [cvec, dmod, w_ada_g], comm, name="adaln_bwd", grid=(1,),
                 in_specs=[whole(cvec.shape), whole(dmod.shape), whole(w_ada_g.shape)],
                 out_specs=tuple(whole(s) for s in shapes), out_shape=tuple(SDS(s, F32) for s in shapes),
                 compiler_params=_cp(("arbitrary",), 56))


def _big_rows(rows):
    return 1536 if rows % 1536 == 0 else TM


def _norm_fwd(x_lat, x_ctx, norm_w, scale3, shift3, tiles_per_sample, n_samp):
    n_lat = x_lat.shape[0] // TM
    rows = x_lat.shape[0] + x_ctx.shape[0]

    def samp(i):
        return jnp.minimum(i // tiles_per_sample, n_samp)

    def body(x_ref, c_ref, nw_ref, sc_ref, sh_ref, hx_ref, hxt_ref):
        x = jnp.where(pl.program_id(0) < n_lat, x_ref[...], c_ref[...])
        h = x * _rms(x) * nw_ref[...] * (1.0 + sc_ref[...]) + sh_ref[...]
        hx_ref[...] = h.astype(BF)
        hxt_ref[...] = h.T.astype(BF)

    return pl.pallas_call(
        body, name="norm_fwd", grid=(rows // TM,),
        in_specs=[pl.BlockSpec((TM, D_MODEL), lambda i: (jnp.minimum(i, n_lat - 1), 0)),
                  pl.BlockSpec((TM, D_MODEL), lambda i: (jnp.maximum(i - n_lat, 0), 0)),
                  pl.BlockSpec((1, D_MODEL), lambda i: (0, 0)),
                  pl.BlockSpec((None, 1, D_MODEL), lambda i: (samp(i), 0, 0)),
                  pl.BlockSpec((None, 1, D_MODEL), lambda i: (samp(i), 0, 0))],
        out_specs=(pl.BlockSpec((TM, D_MODEL), lambda i: (i, 0)),
                   pl.BlockSpec((D_MODEL, TM), lambda i: (0, i))),
        out_shape=(SDS((rows, D_MODEL), BF), SDS((D_MODEL, rows), BF)),
        compiler_params=_cp(("parallel",), 40))(x_lat, x_ctx, norm_w, scale3, shift3)


def _in_proj(hx, w_in_g, ids, first, count, px=None, comm=None):
    rows = hx.shape[0]
    tb = _big_rows(rows)

    def shard(j, ids_ref):
        return ids_ref[4 + first + j // BPS]

    def body(ids_ref, h_ref, w_ref, *rest):
        px_ref = rest[-1]
        px_ref[...] = jnp.dot(h_ref[...], w_ref[...], preferred_element_type=F32).astype(BF)

    args, in_specs, aliases = [hx, w_in_g], [
        pl.BlockSpec((tb, D_MODEL), lambda j, i, ids_ref: (i, 0)),
        pl.BlockSpec((None, D_MODEL, IN_BLK), lambda j, i, ids_ref: (shard(j, ids_ref), 0, j % BPS))], None
    if px is not None:
        args, in_specs, aliases = args + [px], in_specs + [ANY], {2: 0}
    return _call(body, args, comm, name="in_proj", grid=(BPS * count, rows // tb), in_specs=in_specs,
                 out_specs=pl.BlockSpec((tb, IN_BLK),
                                        lambda j, i, ids_ref: (i, BPS * shard(j, ids_ref) + j % BPS)),
                 out_shape=SDS((rows, IN_COLS), BF), aliases=aliases, prefetch=ids,
                 compiler_params=_cp(("arbitrary", "arbitrary"), 56))


def _norm_bwd(x_lat, x_ctx, dhx, gx_res, norm_w, scale3, tiles_per_sample, n_samp, comm=None):
    rows = x_lat.shape[0] + x_ctx.shape[0]
    n_lat = tiles_per_sample * n_samp

    def samp(i):
        return jnp.minimum(i // tiles_per_sample, n_samp)

    def lat(i):
        return jnp.minimum(i, n_lat - 1)

    def body(x_ref, c_ref, dh_ref, gr_ref, nw_ref, sc_ref, gx_ref, dsh_ref, dsc_ref, dnw_ref):
        i = pl.program_id(0)
        x = jnp.where(i < n_lat, x_ref[...], c_ref[...])
        r = _rms(x)
        xh = x * r
        nw = nw_ref[...]
        dh = dh_ref[...]
        first = jnp.logical_or(i % tiles_per_sample == 0, i >= n_lat)

        @pl.when(first)
        def _():
            dsh_ref[...] = jnp.zeros_like(dsh_ref)
            dsc_ref[...] = jnp.zeros_like(dsc_ref)

        @pl.when(i == 0)
        def _():
            dnw_ref[...] = jnp.zeros_like(dnw_ref)

        dsh_ref[...] += jnp.sum(dh, axis=0, keepdims=True)
        dsc_ref[...] += jnp.sum(dh * (xh * nw), axis=0, keepdims=True)
        du = dh * (1.0 + sc_ref[...])
        dnw_ref[...] += jnp.sum(du * xh, axis=0, keepdims=True)

        @pl.when(i < n_lat)
        def _():
            gx_ref[...] = gr_ref[...] + _rms_bwd(du * nw, xh, r)

    return _call(
        body, [x_lat, x_ctx, dhx, gx_res, norm_w, scale3], comm, name="norm_bwd", grid=(rows // TM,),
        in_specs=[pl.BlockSpec((TM, D_MODEL), lambda i: (lat(i), 0)),
                  pl.BlockSpec((TM, D_MODEL), lambda i: (jnp.maximum(i - n_lat, 0), 0)),
                  pl.BlockSpec((TM, D_MODEL), lambda i: (i, 0)),
                  pl.BlockSpec((TM, D_MODEL), lambda i: (lat(i), 0)),
                  pl.BlockSpec((1, D_MODEL), lambda i: (0, 0)),
                  pl.BlockSpec((None, 1, D_MODEL), lambda i: (samp(i), 0, 0))],
        out_specs=(pl.BlockSpec((TM, D_MODEL), lambda i: (lat(i), 0)),
                   pl.BlockSpec((None, 1, D_MODEL), lambda i: (samp(i), 0, 0)),
                   pl.BlockSpec((None, 1, D_MODEL), lambda i: (samp(i), 0, 0)),
                   pl.BlockSpec((1, D_MODEL), lambda i: (0, 0))),
        out_shape=(SDS((n_lat * TM, D_MODEL), F32), SDS((n_samp + 1, 1, D_MODEL), F32),
                   SDS((n_samp + 1, 1, D_MODEL), F32), SDS((1, D_MODEL), F32)),
        compiler_params=_cp(("arbitrary",), 40))


def _in_proj_gather(hx, w_buf, ids):
    rows = hx.shape[0]
    tb = _big_rows(rows)
    n_i = rows // tb
    hrows = D_MODEL // 2

    def body(ids_ref, h_ref, w_in_hbm, px_ref, w_hbm, wv, lsem, ssem, rsem):
        j, i = pl.program_id(0), pl.program_id(1)
        x, y, c, chips = _place()
        sibling = (x, y, 1 - c)

        def half(s, which):
            return w_hbm.at[s, pl.ds(which * hrows, hrows), :]

        def over_ici(rel):
            chip = chips[rel]
            mine, theirs = half(2 * x + y, c), half(2 * chip[0] + chip[1], c)
            return (_remote(mine, mine, ssem.at[rel], rsem.at[rel], (*chip, c)),
                    _remote(theirs, theirs, ssem.at[rel], rsem.at[rel], (*chip, c)))

        def over_d2d(rel):
            s = 2 * chips[rel][0] + chips[rel][1]
            return (_remote(half(s, c), half(s, c), ssem.at[3 + rel], rsem.at[3 + rel], sibling),
                    _remote(half(s, 1 - c), half(s, 1 - c), ssem.at[3 + rel], rsem.at[3 + rel], sibling))

        first_row_tile = i == 0

        @pl.when(jnp.logical_and(j == 0, first_row_tile))
        def _():
            over_ici(0)[0].start()
            over_ici(1)[0].start()

        for rel in range(3):
            @pl.when(jnp.logical_and(j == rel + 1, first_row_tile))
            def _(rel=rel):
                over_ici(rel)[1].wait_recv()
                passed, landing = over_d2d(rel)
                passed.start()
                if rel == 0:
                    over_ici(2)[0].start()
                landing.wait_recv()

        @pl.when(first_row_tile)
        def _():
            cp = pltpu.make_async_copy(w_hbm.at[ids_ref[4 + j]], wv, lsem)
            cp.start()
            cp.wait()

        px_ref[...] = jnp.dot(h_ref[...], wv[...], preferred_element_type=F32).astype(BF)

        @pl.when(jnp.logical_and(j == N_SHARD - 1, i == n_i - 1))
        def _():
            for rel in range(3):
                over_ici(rel)[0].wait_send()
                over_d2d(rel)[0].wait_send()

    return pl.pallas_call(
        body, name="in_proj_gather", input_output_aliases={2: 1},
        grid_spec=pltpu.PrefetchScalarGridSpec(
            num_scalar_prefetch=1, grid=(N_SHARD, n_i),
            in_specs=[pl.BlockSpec((tb, D_MODEL), lambda j, i, ids_ref: (i, 0)), ANY],
            out_specs=(pl.BlockSpec((tb, IN_W), lambda j, i, ids_ref: (i, ids_ref[4 + j])), ANY),
            scratch_shapes=[pltpu.VMEM((D_MODEL, IN_W), BF), pltpu.SemaphoreType.DMA,
                            pltpu.SemaphoreType.DMA((6,)), pltpu.SemaphoreType.DMA((6,))]),
        out_shape=(SDS((rows, IN_COLS), BF), SDS(w_buf.shape, w_buf.dtype)),
        compiler_params=_cp(("arbitrary", "arbitrary"), 56))(ids, hx, w_buf)


def _gw_in(hxt, dpx_all, part, n_parts, comm=None):
    rows = dpx_all.shape[0]
    tb = _big_rows(rows)
    dp = D_MODEL // n_parts

    def body(h_ref, d_ref, o_ref):
        @pl.when(pl.program_id(1) == 0)
        def _():
            o_ref[...] = jnp.zeros_like(o_ref)

        o_ref[...] += jnp.dot(h_ref[...], d_ref[...], preferred_element_type=F32)

    return _call(body, [hxt, dpx_all], comm, name="gw_in", grid=(N_IN_BLK, rows // tb),
                 in_specs=[pl.BlockSpec((dp, tb), lambda j, i: (part, i)),
                           pl.BlockSpec((tb, IN_BLK), lambda j, i: (i, j))],
                 out_specs=pl.BlockSpec((None, dp, IN_BLK), lambda j, i: (j // BPS, 0, j % BPS)),
                 out_shape=SDS((N_SHARD, dp, IN_W), F32),
                 compiler_params=_cp(("arbitrary", "arbitrary"), 56))


def _dhx(dpx_all, w_in_g, tile0, n_tiles, dhx=None, comm=None):
    rows = dpx_all.shape[0]
    tb = _big_rows(rows)

    def body(d_ref, w_ref, *rest):
        o_ref = rest[-1]

        @pl.when(pl.program_id(1) == 0)
        def _():
            o_ref[...] = jnp.zeros_like(o_ref)

        o_ref[...] += lax.dot_general(d_ref[...], w_ref[...], (((1,), (1,)), ((), ())), preferred_element_type=F32)

    args, in_specs, aliases = [dpx_all, w_in_g], [
        pl.BlockSpec((tb, IN_BLK), lambda i, j: (tile0 + i, j)),
        pl.BlockSpec((None, D_MODEL, IN_BLK), lambda i, j: (j // BPS, 0, j % BPS))], None
    if dhx is not None:
        args, in_specs, aliases = args + [dhx], in_specs + [ANY], {2: 0}
    return _call(body, args, comm, name="dhx", grid=(n_tiles, N_IN_BLK), in_specs=in_specs,
                 out_specs=pl.BlockSpec((tb, D_MODEL), lambda i, j: (tile0 + i, 0)),
                 out_shape=SDS((rows, D_MODEL), F32), aliases=aliases,
                 compiler_params=_cp(("arbitrary", "arbitrary"), 56))


def _decays(lgv, d):
    c = RET_CHUNK
    ii = lax.broadcasted_iota(jnp.int32, (c, 1), 0).astype(F32)
    jj = lax.broadcasted_iota(jnp.int32, (1, c), 1).astype(F32)
    a_i = jnp.where(d == 0, ii, c - 1.0 - ii)
    a_j = jnp.where(d == 0, jj, c - 1.0 - jj)
    rel = a_i - a_j
    mask = jnp.where(rel >= 0, jnp.exp(lgv * jnp.maximum(rel, 0.0)), 0.0)
    qd = jnp.exp(lgv * (a_i + 1.0))
    kd = jnp.exp(lgv * (c - 1.0 - a_i))
    gc = jnp.exp(jnp.full((1, 1), lgv * c, F32))
    return a_i, rel, mask, qd, kd, gc


def _ctx_state_fwd(px, lg, n_samp, t_lat, lc):
    rb = t_lat // lc

    def body(lg_ref, k_ref, v_ref, o_ref):
        h = pl.program_id(1)
        k = k_ref[...].astype(F32) * (RET_DK ** -0.5)
        v = v_ref[...]
        pos = lax.broadcasted_iota(jnp.int32, (lc, 1), 0).astype(F32)
        o_ref[0] = _dot(k * jnp.exp(lg_ref[0, h] * (lc - 1.0 - pos)), v, 0, 0)
        o_ref[1] = _dot(k * jnp.exp(lg_ref[1, h] * pos), v, 0, 0)

    return pl.pallas_call(
        body, name="ctx_state_fwd", grid=(n_samp, RET_HEADS),
        in_specs=[SMEM,
                  pl.BlockSpec((lc, RET_DK), lambda b, h: (rb + b, C_RK // RET_DK + h)),
                  pl.BlockSpec((lc, RET_DV), lambda b, h: (rb + b, C_RV // RET_DV + h))],
        out_specs=pl.BlockSpec((None, 2, None, RET_DK, RET_DV), lambda b, h: (b, 0, h, 0, 0)),
        out_shape=SDS((n_samp, 2, RET_HEADS, RET_DK, RET_DV), F32),
        compiler_params=_cp(("parallel", "parallel")))(lg, px, px)


def _ctx_state_bwd(dpx, px, dstates, lg, n_samp, t_lat, lc):
    rb = t_lat // lc
    kspec = pl.BlockSpec((lc, RET_DK), lambda b, h: (rb + b, C_RK // RET_DK + h))
    vspec = pl.BlockSpec((lc, RET_DV), lambda b, h: (rb + b, C_RV // RET_DV + h))
    sspec = pl.BlockSpec((None, 2, None, RET_DK, RET_DV), lambda b, h: (b, 0, h, 0, 0))

    def weights(lg_ref, h):
        pos = lax.broadcasted_iota(jnp.int32, (lc, 1), 0).astype(F32)
        e_f = lc - 1.0 - pos
        return pos, e_f, jnp.exp(lg_ref[0, h] * e_f), jnp.exp(lg_ref[1, h] * pos)

    def k_body(lg_ref, dpx_hbm, k_ref, v_ref, ds_ref, dk_ref, dlg_ref):
        pos, e_f, w_f, w_b = weights(lg_ref, pl.program_id(1))
        k = k_ref[...].astype(F32) * (RET_DK ** -0.5)
        y_f = _dot(v_ref[...], ds_ref[0], 1, 1) * w_f
        y_b = _dot(v_ref[...], ds_ref[1], 1, 1) * w_b
        dk_ref[...] = ((y_f + y_b) * (RET_DK ** -0.5)).astype(BF)
        t_f = _sum_all(e_f * k * y_f)
        t_b = _sum_all(pos * k * y_b)
        sub = lax.broadcasted_iota(jnp.int32, (8, 128), 0)
        dlg_ref[...] = jnp.where(sub == 0, t_f, jnp.where(sub == 1, t_b, 0.0))

    def v_body(lg_ref, dpx_hbm, k_ref, ds_ref, dv_ref):
        _, _, w_f, w_b = weights(lg_ref, pl.program_id(1))
        k = k_ref[...].astype(F32) * (RET_DK ** -0.5)
        dv_ref[...] = (_dot(k * w_f, ds_ref[0]) + _dot(k * w_b, ds_ref[1])).astype(BF)

    dpx, dlg = pl.pallas_call(
        k_body, name="ctx_state_bwd_k", grid=(n_samp, RET_HEADS), input_output_aliases={1: 0},
        in_specs=[SMEM, ANY, kspec, vspec, sspec],
        out_specs=(kspec, pl.BlockSpec((None, None, 8, 128), lambda b, h: (b, h, 0, 0))),
        out_shape=(SDS(dpx.shape, dpx.dtype), SDS((n_samp, RET_HEADS, 8, 128), F32)),
        compiler_params=_cp(("parallel", "parallel")))(lg, dpx, px, px, dstates)
    dpx = pl.pallas_call(
        v_body, name="ctx_state_bwd_v", grid=(n_samp, RET_HEADS), input_output_aliases={1: 0},
        in_specs=[SMEM, ANY, kspec, sspec], out_specs=vspec, out_shape=SDS(dpx.shape, dpx.dtype),
        compiler_params=_cp(("parallel", "parallel")))(lg, dpx, px, dstates)
    return dpx, dlg


def _zero_ctx_tail(dpx, t_lat):
    wb = 512
    n_ctx = (dpx.shape[0] - t_lat) // TM

    def body(dpx_hbm, o_ref):
        o_ref[...] = jnp.zeros_like(o_ref)

    return pl.pallas_call(
        body, name="zero_ctx_tail", grid=(n_ctx, (IN_COLS - KV_COLS) // wb), input_output_aliases={0: 0},
        in_specs=[ANY], out_specs=pl.BlockSpec((TM, wb), lambda i, j: (t_lat // TM + i, KV_COLS // wb + j)),
        out_shape=SDS(dpx.shape, dpx.dtype),
        compiler_params=_cp(("parallel", "parallel")))(dpx)


def _ret_specs(row_f, row_b):
    c = RET_CHUNK
    wq = RET_HEADS * RET_DK // 2
    wv = RET_HEADS * RET_DV // 2
    specs = []
    for row in (row_f, row_b):
        specs += [pl.BlockSpec((c, wq), lambda b, n, row=row: (row(b, n), C_RQ // wq)),
                  pl.BlockSpec((c, wq), lambda b, n, row=row: (row(b, n), C_RQ // wq + 1)),
                  pl.BlockSpec((c, 2 * wq), lambda b, n, row=row: (row(b, n), C_RK // (2 * wq))),
                  pl.BlockSpec((c, wv), lambda b, n, row=row: (row(b, n), C_RV // wv)),
                  pl.BlockSpec((c, wv), lambda b, n, row=row: (row(b, n), C_RV // wv + 1))]
    return specs


def _ret_head(refs, h):
    q0, q1, k_ref, v0, v1 = refs
    hh = h % 2
    q = (q0, q1)[h // 2][:, hh * RET_DK:(hh + 1) * RET_DK].astype(F32)
    k = k_ref[:, h * RET_DK:(h + 1) * RET_DK].astype(F32) * (RET_DK ** -0.5)
    v = (v0, v1)[h // 2][:, hh * RET_DV:(hh + 1) * RET_DV]
    return q, k, v


def _ret_fwd(px, states0, lg, n_samp, seq, comm=None):
    c = RET_CHUNK
    nc = seq // c
    t_lat = n_samp * seq
    wo = RET_HEADS * RET_DV

    def row_f(b, n):
        return b * nc + n

    def row_b(b, n):
        return b * nc + nc - 1 - n

    def body(lg_ref, *refs):
        ins, (s0_ref, of_ref, ob_ref, st_ref, s_s) = refs[:10], refs[10:]

        @pl.when(pl.program_id(1) == 0)
        def _():
            s_s[...] = s0_ref[...]

        for d, o_ref in ((0, of_ref), (1, ob_ref)):
            for h in range(RET_HEADS):
                _, _, mask, qd, kd, gc = _decays(lg_ref[d, h], d)
                q, k, v = _ret_head(ins[5 * d:5 * d + 5], h)
                s = s_s[d, h]
                st_ref[h, d] = s.astype(BF)
                sc = _dot(q, k, 1, 1) * mask
                o_ref[:, h * RET_DV:(h + 1) * RET_DV] = (_dot(sc, v) + _dot(q * qd, s)).astype(BF)
                s_s[d, h] = s * gc + _dot(k * kd, v, 0, 0)

    return _call(
        body, [lg] + [px] * 10 + [states0], comm, name="ret_fwd", grid=(n_samp, nc),
        in_specs=[SMEM] + _ret_specs(row_f, row_b) + [
            pl.BlockSpec((None, 2, RET_HEADS, RET_DK, RET_DV), lambda b, n: (b, 0, 0, 0, 0))],
        out_specs=(pl.BlockSpec((c, wo), lambda b, n: (row_f(b, n), 0)),
                   pl.BlockSpec((c, wo), lambda b, n: (row_b(b, n), 0)),
                   pl.BlockSpec((None, RET_HEADS, 2, None, RET_DK, RET_DV), lambda b, n: (b, 0, 0, n, 0, 0))),
        out_shape=(SDS((t_lat, wo), BF), SDS((t_lat, wo), BF),
                   SDS((n_samp, RET_HEADS, 2, nc, RET_DK, RET_DV), BF)),
        scratch_shapes=[pltpu.VMEM((2, RET_HEADS, RET_DK, RET_DV), F32)],
        compiler_params=_cp(("arbitrary", "arbitrary"), 48))


def _ret_bwd(dpx, px, do, saved, lg, n_samp, seq, comm=None):
    c = RET_CHUNK
    nc = seq // c
    assert nc % 2 == 0
    wq, wo = RET_HEADS * RET_DK, RET_HEADS * RET_DV

    def row_f(b, n):
        return b * nc + nc - 1 - n

    def row_b(b, n):
        return b * nc + n

    def body(lg_ref, *refs):
        ins = refs[:10]
        (dof_ref, dob_ref, st_ref, dpx_in, dpx_hbm, ds0_ref, dlg_ref,
         ds_s, acc_s, tq_s, tk_s, tv_s, sq_s, sk_s, sv_s, sems) = refs[10:]
        b, n = pl.program_id(0), pl.program_id(1)
        second = n >= nc // 2
        chunks = (nc - 1 - n, n)

        def parked(ch):
            return pl.ds(pl.multiple_of(ch * c, c), c)

        def flush():
            cps = []
            for d, ch in enumerate(chunks):
                rows = pl.ds(pl.multiple_of((b * nc + ch) * c, c), c)
                cps += [pltpu.make_async_copy(sq_s.at[parked(ch), :], dpx_hbm.at[rows, pl.ds(C_RQ, wq)], sems.at[3 * d]),
                        pltpu.make_async_copy(sk_s.at[parked(ch), :], dpx_hbm.at[rows, pl.ds(C_RK, wq)],
                                              sems.at[3 * d + 1]),
                        pltpu.make_async_copy(sv_s.at[parked(ch), :], dpx_hbm.at[rows, pl.ds(C_RV, wo)],
                                              sems.at[3 * d + 2])]
            return cps

        @pl.when(jnp.logical_or(n > nc // 2, jnp.logical_and(n == 0, b > 0)))
        def _():
            for cp in flush():
                cp.wait()

        @pl.when(n == 0)
        def _():
            ds_s[...] = jnp.zeros_like(ds_s)
            acc_s[...] = jnp.zeros_like(acc_s)

        for d, do_ref in enumerate((dof_ref, dob_ref)):
            for h in range(RET_HEADS):
                a_i, rel, mask, qd, kd, gc = _decays(lg_ref[d, h], d)
                q, k, v = _ret_head(ins[5 * d:5 * d + 5], h)
                qb, kb, vb = q.astype(BF), k.astype(BF), v.astype(BF)
                dob = do_ref[:, h * RET_DV:(h + 1) * RET_DV].astype(BF)
                sb = st_ref[h, d]
                ds = ds_s[d, h]
                dsb = ds.astype(BF)
                raw = _dot(qb, kb, 1, 1)
                sc = raw * mask
                dsc = _dot(dob, vb, 1, 1) * mask
                dscb = dsc.astype(BF)
                x = _dot(dob, sb, 1, 1)
                y = _dot(vb, dsb, 1, 1)
                qq = q * qd
                kk = k * kd
                tq_s[d, :, h * RET_DK:(h + 1) * RET_DK] = (_dot(dscb, kb) + x * qd).astype(BF)
                tk_s[d, :, h * RET_DK:(h + 1) * RET_DK] = (_dot(dscb, qb, 0, 0) + y * kd).astype(BF)
                tv_s[d, :, h * RET_DV:(h + 1) * RET_DV] = (_dot(sc, dob, 0, 0) + _dot(kk, dsb)).astype(BF)
                t = (_sum_all(dsc * raw * rel) + _sum_all((a_i + 1.0) * qq * x)
                     + _sum_all((c - 1.0 - a_i) * kk * y) + c * gc * _sum_all(ds * sb.astype(F32)))
                acc_s[4 * d + h:4 * d + h + 1, :] += t
                ds_s[d, h] = ds * gc + _dot(qq, dob, 0, 0)

        @pl.when(jnp.logical_not(second))
        def _():
            for d, ch in enumerate(chunks):
                sq_s[parked(ch), :] = tq_s[d]
                sk_s[parked(ch), :] = tk_s[d]
                sv_s[parked(ch), :] = tv_s[d]

        @pl.when(second)
        def _():
            for d, ch in enumerate(chunks):
                sq_s[parked(ch), :] = (sq_s[parked(ch), :].astype(F32) + tq_s[d].astype(F32)).astype(BF)
                sk_s[parked(ch), :] = ((sk_s[parked(ch), :].astype(F32) + tk_s[d].astype(F32))
                                       * (RET_DK ** -0.5)).astype(BF)
                sv_s[parked(ch), :] = (sv_s[parked(ch), :].astype(F32) + tv_s[d].astype(F32)).astype(BF)
            for cp in flush():
                cp.start()

        @pl.when(n == nc - 1)
        def _():
            ds0_ref[...] = ds_s[...]
            dlg_ref[...] = acc_s[...]

        @pl.when(jnp.logical_and(b == n_samp - 1, n == nc - 1))
        def _():
            for cp in flush():
                cp.wait()

    do_spec_f = pl.BlockSpec((c, wo), lambda b, n: (row_f(b, n), 0))
    do_spec_b = pl.BlockSpec((c, wo), lambda b, n: (row_b(b, n), 0))
    return _call(
        body, [lg] + [px] * 10 + [do, do, saved, dpx], comm, name="ret_bwd", grid=(n_samp, nc), aliases={14: 0},
        in_specs=[SMEM] + _ret_specs(row_f, row_b) + [
            do_spec_f, do_spec_b,
            pl.BlockSpec((None, RET_HEADS, 2, None, RET_DK, RET_DV), lambda b, n: (b, 0, 0, nc - 1 - n, 0, 0)),
            ANY],
        out_specs=(ANY,
                   pl.BlockSpec((None, 2, RET_HEADS, RET_DK, RET_DV), lambda b, n: (b, 0, 0, 0, 0)),
                   pl.BlockSpec((None, 8, 128), lambda b, n: (b, 0, 0))),
        out_shape=(SDS(dpx.shape, dpx.dtype),
                   SDS((n_samp, 2, RET_HEADS, RET_DK, RET_DV), F32), SDS((n_samp, 8, 128), F32)),
        scratch_shapes=[pltpu.VMEM((2, RET_HEADS, RET_DK, RET_DV), F32), pltpu.VMEM((8, 128), F32),
                        pltpu.VMEM((2, c, wq), BF), pltpu.VMEM((2, c, wq), BF), pltpu.VMEM((2, c, wo), BF),
                        pltpu.VMEM((seq, wq), BF), pltpu.VMEM((seq, wq), BF), pltpu.VMEM((seq, wo), BF),
                        pltpu.SemaphoreType.DMA((6,))],
        compiler_params=_cp(("arbitrary", "arbitrary"), 60))


def _combine_into(dpx, a, b, col0, scale):
    t_lat, width = a.shape
    wb = 512
    assert col0 % wb == 0 and width % wb == 0

    def body(dpx_hbm, a_ref, b_ref, o_ref):
        o_ref[...] = ((a_ref[...].astype(F32) + b_ref[...].astype(F32)) * scale).astype(BF)

    src = pl.BlockSpec((TM, wb), lambda i, j: (i, j))
    return pl.pallas_call(
        body, name="combine_into", grid=(t_lat // TM, width // wb), input_output_aliases={0: 0},
        in_specs=[ANY, src, src], out_specs=pl.BlockSpec((TM, wb), lambda i, j: (i, col0 // wb + j)),
        out_shape=SDS(dpx.shape, dpx.dtype),
        compiler_params=_cp(("parallel", "parallel")))(dpx, a, b)


def _retnorm_fwd(o_f, o_b, px):
    t_lat = o_f.shape[0]

    def body(of_ref, ob_ref, g_ref, y_ref):
        o = of_ref[...].astype(F32) + ob_ref[...].astype(F32)
        g = g_ref[...].astype(F32)
        y_ref[...] = (o * _rms(o) * (g * _sigmoid(g))).astype(BF)

    so = pl.BlockSpec((TM, RET_DV), lambda i, h: (i, h))
    return pl.pallas_call(
        body, name="retnorm_fwd", grid=(t_lat // TM, RET_HEADS),
        in_specs=[so, so, pl.BlockSpec((TM, RET_DV), lambda i, h: (i, C_RG // RET_DV + h))],
        out_specs=so,
        out_shape=SDS((t_lat, RET_HEADS * RET_DV), BF),
        compiler_params=_cp(("parallel", "parallel")))(o_f, o_b, px)


def _retnorm_bwd(dpx, dy, o_f, o_b, px):
    t_lat = o_f.shape[0]

    def body(dpx_hbm, dy_ref, of_ref, ob_ref, g_ref, do_ref, dg_ref):
        o = of_ref[...].astype(F32) + ob_ref[...].astype(F32)
        r = _rms(o)
        on = o * r
        g = g_ref[...].astype(F32)
        sg = _sigmoid(g)
        dy_ = dy_ref[...].astype(F32)
        dg_ref[...] = (dy_ * on * (sg * (1.0 + g * (1.0 - sg)))).astype(BF)
        do_ref[...] = _rms_bwd(dy_ * (g * sg), on, r).astype(BF)

    so = pl.BlockSpec((TM, RET_DV), lambda i, h: (i, h))
    gcol = pl.BlockSpec((TM, RET_DV), lambda i, h: (i, C_RG // RET_DV + h))
    return pl.pallas_call(
        body, name="retnorm_bwd", grid=(t_lat // TM, RET_HEADS), input_output_aliases={0: 1},
        in_specs=[ANY, so, so, so, gcol],
        out_specs=(so, gcol),
        out_shape=(SDS((t_lat, RET_HEADS * RET_DV), BF), SDS(dpx.shape, dpx.dtype)),
        compiler_params=_cp(("parallel", "parallel")))(dpx, dy, o_f, o_b, px)


def _norm_rope(x, w, cos, sin):
    xn = x * _rms(x) * w
    return xn * cos + _swap_pairs(xn) * sin


def _norm_rope_bwd(dy, x, w, cos, sin):
    dxn = dy * cos + _swap_pairs(dy * sin)
    r = _rms(x)
    xh = x * r
    return _rms_bwd(dxn * w, xh, r), jnp.sum(dxn * xh, axis=0, keepdims=True)


def _att_prep_q(px, cos_all, sin_all, qnw, t_lat):
    hd = ATT_HEAD_DIM
    wblk = ATT_REP * hd

    def body(x_ref, cos_ref, sin_ref, w_ref, o_ref):
        for r in range(ATT_REP):
            cols = slice(r * hd, (r + 1) * hd)
            qr = _norm_rope(x_ref[:, cols].astype(F32), w_ref[...], cos_ref[...], sin_ref[...])
            o_ref[:, cols] = (qr * (hd ** -0.5)).astype(BF)

    return pl.pallas_call(
        body, name="att_prep_q", grid=(t_lat // TM, ATT_KV_HEADS),
        in_specs=[pl.BlockSpec((TM, wblk), lambda i, g: (i, C_AQ // wblk + g)),
                  pl.BlockSpec((TM, hd), lambda i, g: (i, 0)),
                  pl.BlockSpec((TM, hd), lambda i, g: (i, 0)),
                  pl.BlockSpec((1, hd), lambda i, g: (0, 0))],
        out_specs=pl.BlockSpec((TM, wblk), lambda i, g: (i, g)),
        out_shape=SDS((t_lat, ATT_HEADS * hd), BF),
        compiler_params=_cp(("parallel", "parallel")))(px, cos_all, sin_all, qnw)


def _att_prep_kv(px, cos_all, sin_all, knw):
    rows = px.shape[0]
    hd = ATT_HEAD_DIM
    kvw = ATT_KV_HEADS * hd

    def body(x_ref, cos_ref, sin_ref, w_ref, k_ref, v_ref):
        for g in range(ATT_KV_HEADS):
            cols = slice(g * hd, (g + 1) * hd)
            k_ref[:, cols] = _norm_rope(x_ref[:, cols].astype(F32), w_ref[...], cos_ref[...],
                                        sin_ref[...]).astype(BF)
        v_ref[...] = x_ref[:, kvw:].astype(BF)

    return pl.pallas_call(
        body, name="att_prep_kv", grid=(rows // TM,),
        in_specs=[pl.BlockSpec((TM, 2 * kvw), lambda i: (i, C_AK // (2 * kvw))),
                  pl.BlockSpec((TM, hd), lambda i: (i, 0)),
                  pl.BlockSpec((TM, hd), lambda i: (i, 0)),
                  pl.BlockSpec((1, hd), lambda i: (0, 0))],
        out_specs=(pl.BlockSpec((TM, kvw), lambda i: (i, 0)), pl.BlockSpec((TM, kvw), lambda i: (i, 0))),
        out_shape=(SDS((rows, kvw), BF), SDS((rows, kvw), BF)),
        compiler_params=_cp(("parallel",)))(px, cos_all, sin_all, knw)


def _att_kv_bwd(dpx, dkl, dkc, dvl, dvc, px, cos_all, sin_all, knw):
    rows = px.shape[0]
    hd = ATT_HEAD_DIM
    kvw = ATT_KV_HEADS * hd
    n_lat = dkl.shape[0] // TM
    assert dkc.shape[0] == TM

    def body(dpx_hbm, dkl_ref, dkc_ref, dvl_ref, dvc_ref, x_ref, cos_ref, sin_ref, w_ref, o_ref, gw_ref):
        i = pl.program_id(0)

        @pl.when(i == 0)
        def _():
            gw_ref[...] = jnp.zeros_like(gw_ref)

        is_lat = i < n_lat
        dk = jnp.where(is_lat, dkl_ref[...], dkc_ref[...])
        dv = jnp.where(is_lat, dvl_ref[...], dvc_ref[...])
        for g in range(ATT_KV_HEADS):
            cols = slice(g * hd, (g + 1) * hd)
            dx, gw = _norm_rope_bwd(dk[:, cols], x_ref[:, cols].astype(F32), w_ref[...], cos_ref[...], sin_ref[...])
            o_ref[:, cols] = dx.astype(BF)
            gw_ref[...] += gw
        o_ref[:, kvw:] = dv.astype(BF)

    lat = pl.BlockSpec((TM, kvw), lambda i: (jnp.minimum(i, n_lat - 1), 0))
    ctx = pl.BlockSpec((TM, kvw), lambda i: (0, 0))
    kvcol = pl.BlockSpec((TM, 2 * kvw), lambda i: (i, C_AK // (2 * kvw)))
    return pl.pallas_call(
        body, name="att_kv_bwd", grid=(rows // TM,), input_output_aliases={0: 0},
        in_specs=[ANY, lat, ctx, lat, ctx, kvcol,
                  pl.BlockSpec((TM, hd), lambda i: (i, 0)),
                  pl.BlockSpec((TM, hd), lambda i: (i, 0)),
                  pl.BlockSpec((1, hd), lambda i: (0, 0))],
        out_specs=(kvcol, pl.BlockSpec((1, hd), lambda i: (0, 0))),
        out_shape=(SDS(dpx.shape, dpx.dtype), SDS((1, hd), F32)),
        compiler_params=_cp(("arbitrary",)))(dpx, dkl, dkc, dvl, dvc, px, cos_all, sin_all, knw)


def _stack_heads(ref_or_val):
    hd = ATT_HEAD_DIM
    return jnp.concatenate([ref_or_val[:, r * hd:(r + 1) * hd] for r in range(ATT_REP)], axis=0)


def _att_scores(q, kl, kc):
    sl = _dot(q, kl, 1, 1)
    sc = _dot(q, kc, 1, 1)
    m = jnp.maximum(jnp.max(sl, axis=-1, keepdims=True), jnp.max(sc, axis=-1, keepdims=True))
    el = jnp.exp(sl - m)
    ec = jnp.exp(sc - m)
    denom = jnp.sum(el, axis=-1, keepdims=True) + jnp.sum(ec, axis=-1, keepdims=True)
    return el, ec, denom, m


def _query(x_ref, cols, w_ref, cos_ref, sin_ref):
    q = _norm_rope(x_ref[:, cols].astype(F32), w_ref[...], cos_ref[...], sin_ref[...])
    return (q * (ATT_HEAD_DIM ** -0.5)).astype(BF)


def _att_fwd(px, kn, vn, cos_all, sin_all, qnw, n_samp, seq, lc):
    hd = ATT_HEAD_DIM
    tq = ATT_TQ
    nq = seq // tq
    wblk = ATT_REP * hd
    cb = n_samp * seq // lc
    t_lat = n_samp * seq

    def body(x_ref, cos_ref, sin_ref, w_ref, kl_ref, kc_ref, vl_ref, vc_ref, o_ref, lse_ref):
        lane = lax.broadcasted_iota(jnp.int32, (tq, hd), 1)
        lse = jnp.zeros((tq, hd), F32)
        for r in range(ATT_REP):
            cols = slice(r * hd, (r + 1) * hd)
            q = _query(x_ref, cols, w_ref, cos_ref, sin_ref)
            el, ec, denom, m = _att_scores(q, kl_ref[...], kc_ref[...])
            o_ref[:, cols] = ((_dot(el, vl_ref[...]) + _dot(ec, vc_ref[...])) / denom).astype(BF)
            lse = jnp.where(lane == r, m + jnp.log(denom), lse)
        lse_ref[...] = lse

    return pl.pallas_call(
        body, name="att_fwd", grid=(n_samp, ATT_KV_HEADS, nq),
        in_specs=[pl.BlockSpec((tq, wblk), lambda b, g, i: (b * nq + i, C_AQ // wblk + g)),
                  pl.BlockSpec((tq, hd), lambda b, g, i: (b * nq + i, 0)),
                  pl.BlockSpec((tq, hd), lambda b, g, i: (b * nq + i, 0)),
                  pl.BlockSpec((1, hd), lambda b, g, i: (0, 0)),
                  pl.BlockSpec((seq, hd), lambda b, g, i: (b, g)),
                  pl.BlockSpec((lc, hd), lambda b, g, i: (cb + b, g)),
                  pl.BlockSpec((seq, hd), lambda b, g, i: (b, g)),
                  pl.BlockSpec((lc, hd), lambda b, g, i: (cb + b, g))],
        out_specs=(pl.BlockSpec((tq, wblk), lambda b, g, i: (b * nq + i, g)),
                   pl.BlockSpec((tq, hd), lambda b, g, i: (b * nq + i, g))),
        out_shape=(SDS((t_lat, ATT_HEADS * hd), BF), SDS((t_lat, ATT_KV_HEADS * hd), F32)),
        compiler_params=_cp(("parallel", "parallel", "parallel"), 48))(
            px, cos_all, sin_all, qnw, kn, kn, vn, vn)


def _att_gate_bwd(dpx, dy_att, o_att, px):
    t_lat = dy_att.shape[0]
    wblk = ATT_REP * ATT_HEAD_DIM

    def body(dpx_hbm, dy_ref, o_ref, g_ref, out_ref):
        g = g_ref[...].astype(F32)
        sg = _sigmoid(g)
        out_ref[...] = (dy_ref[...].astype(F32) * o_ref[...].astype(F32) * (sg * (1.0 + g * (1.0 - sg)))).astype(BF)

    blk = pl.BlockSpec((TM, wblk), lambda i, j: (i, j))
    gcol = pl.BlockSpec((TM, wblk), lambda i, j: (i, C_AG // wblk + j))
    return pl.pallas_call(
        body, name="att_gate_bwd", grid=(t_lat // TM, ATT_KV_HEADS),
        in_specs=[ANY, blk, blk, gcol], out_specs=gcol, out_shape=SDS(dpx.shape, dpx.dtype),
        input_output_aliases={0: 0},
        compiler_params=_cp(("parallel", "parallel")))(dpx, dy_att, o_att, px)


def _att_bwd(dpx, kn, vn, px, o_att, lse, do_att, cos_all, sin_all, qnw, n_samp, seq, lc, comm=None):
    hd = ATT_HEAD_DIM
    tq = ATT_TQ
    nq = seq // tq
    wblk = ATT_REP * hd
    cb = n_samp * seq // lc
    t_lat = n_samp * seq
    kvw = ATT_KV_HEADS * hd
    scale = hd ** -0.5

    def body(dpx_hbm, kl_ref, kc_ref, vl_ref, vc_ref, o_ref, do_ref, x_ref, cos_ref, sin_ref, w_ref,
             lse_ref, dq_ref, dkl_ref, dkc_ref, dvl_ref, dvc_ref, gw_ref, akl, akc, avl, avc, aw):
        i = pl.program_id(2)

        @pl.when(i == 0)
        def _():
            akl[...] = jnp.zeros_like(akl)
            akc[...] = jnp.zeros_like(akc)
            avl[...] = jnp.zeros_like(avl)
            avc[...] = jnp.zeros_like(avc)
            aw[...] = jnp.zeros_like(aw)

        qs, dobs, pls, pcs, dsls, dscs = [], [], [], [], [], []
        for r in range(ATT_REP):
            cols = slice(r * hd, (r + 1) * hd)
            dob = do_ref[:, cols]
            delta = jnp.sum(dob.astype(F32) * o_ref[:, cols].astype(F32), axis=-1, keepdims=True)
            lse = lse_ref[:, r:r + 1]
            q = _query(x_ref, cols, w_ref, cos_ref, sin_ref)
            qs.append(q)
            p_l = jnp.exp(_dot(q, kl_ref[...], 1, 1) - lse).astype(BF)
            p_c = jnp.exp(_dot(q, kc_ref[...], 1, 1) - lse).astype(BF)
            ds_l = (p_l * (_dot(dob, vl_ref[...], 1, 1) - delta)).astype(BF)
            ds_c = (p_c * (_dot(dob, vc_ref[...], 1, 1) - delta)).astype(BF)
            dq = (_dot(ds_l, kl_ref[...]) + _dot(ds_c, kc_ref[...])) * scale
            dx, gw = _norm_rope_bwd(dq, x_ref[:, cols].astype(F32), w_ref[...], cos_ref[...], sin_ref[...])
            dq_ref[:, cols] = dx.astype(BF)
            aw[...] += gw
            dobs.append(dob)
            pls.append(p_l)
            pcs.append(p_c)
            dsls.append(ds_l)
            dscs.append(ds_c)
        do4 = jnp.concatenate(dobs, axis=0)
        q4 = jnp.concatenate(qs, axis=0)
        avl[...] += _dot(jnp.concatenate(pls, axis=0), do4, 0, 0)
        avc[...] += _dot(jnp.concatenate(pcs, axis=0), do4, 0, 0)
        akl[...] += _dot(jnp.concatenate(dsls, axis=0), q4, 0, 0)
        akc[...] += _dot(jnp.concatenate(dscs, axis=0), q4, 0, 0)

        @pl.when(i == nq - 1)
        def _():
            dkl_ref[...] = akl[...]
            dkc_ref[...] = akc[...]
            dvl_ref[...] = avl[...]
            dvc_ref[...] = avc[...]
            gw_ref[...] = aw[...]

    return _call(
        body, [dpx, kn, kn, vn, vn, o_att, do_att, px, cos_all, sin_all, qnw, lse], comm,
        name="att_bwd", grid=(n_samp, ATT_KV_HEADS, nq), aliases={0: 0},
        in_specs=[ANY,
                  pl.BlockSpec((seq, hd), lambda b, g, i: (b, g)),
                  pl.BlockSpec((lc, hd), lambda b, g, i: (cb + b, g)),
                  pl.BlockSpec((seq, hd), lambda b, g, i: (b, g)),
                  pl.BlockSpec((lc, hd), lambda b, g, i: (cb + b, g)),
                  pl.BlockSpec((tq, wblk), lambda b, g, i: (b * nq + i, g)),
                  pl.BlockSpec((tq, wblk), lambda b, g, i: (b * nq + i, g)),
                  pl.BlockSpec((tq, wblk), lambda b, g, i: (b * nq + i, C_AQ // wblk + g)),
                  pl.BlockSpec((tq, hd), lambda b, g, i: (b * nq + i, 0)),
                  pl.BlockSpec((tq, hd), lambda b, g, i: (b * nq + i, 0)),
                  pl.BlockSpec((1, hd), lambda b, g, i: (0, 0)),
                  pl.BlockSpec((tq, hd), lambda b, g, i: (b * nq + i, g))],
        out_specs=(pl.BlockSpec((tq, wblk), lambda b, g, i: (b * nq + i, C_AQ // wblk + g)),
                   pl.BlockSpec((seq, hd), lambda b, g, i: (b, g)),
                   pl.BlockSpec((lc, hd), lambda b, g, i: (b, g)),
                   pl.BlockSpec((seq, hd), lambda b, g, i: (b, g)),
                   pl.BlockSpec((lc, hd), lambda b, g, i: (b, g)),
                   pl.BlockSpec((None, None, 1, hd), lambda b, g, i: (b, g, 0, 0))),
        out_shape=(SDS(dpx.shape, dpx.dtype),
                   SDS((t_lat, kvw), F32), SDS((n_samp * lc, kvw), F32),
                   SDS((t_lat, kvw), F32), SDS((n_samp * lc, kvw), F32),
                   SDS((n_samp, ATT_KV_HEADS, 1, hd), F32)),
        scratch_shapes=[pltpu.VMEM((seq, hd), F32), pltpu.VMEM((lc, hd), F32),
                        pltpu.VMEM((seq, hd), F32), pltpu.VMEM((lc, hd), F32), pltpu.VMEM((1, hd), F32)],
        compiler_params=_cp(("arbitrary", "arbitrary", "arbitrary"), 56))


def _merge(x_lat, target, o_f, o_b, o_att, px, gate3, w_o_ret, w_o_att, w_out, tiles_per_sample):
    t_lat = x_lat.shape[0]
    tm = 256
    n_t = t_lat // tm
    per = tiles_per_sample * (TM // tm)
    d = D_MODEL
    rv = RET_HEADS * RET_DV
    n_samp = gate3.shape[0] - 1

    half = d // 2
    n_px = 10

    def body(x_ref, t_ref, of_ref, ob_ref, oa_ref, *rest):
        pxs, rest = rest[:n_px], rest[n_px:]
        (gt_ref, wor_ref, woa_ref, wout_ref,
         gx_ref, dor_ref, doa_ref, dpx_hbm, loss_ref, dgt_ref, gwor_hbm, gwoa_hbm, gwout_hbm,
         aor, aoa, aout, drg_ref, dtail_ref, sems) = rest
        i = pl.program_id(0)

        def copies(step):
            rows = pl.ds(pl.multiple_of(step * tm, tm), tm)
            return (pltpu.make_async_copy(drg_ref, dpx_hbm.at[rows, pl.ds(C_RG, rv)], sems.at[0]),
                    pltpu.make_async_copy(dtail_ref, dpx_hbm.at[rows, pl.ds(C_AG, 3 * d)], sems.at[1]))

        @pl.when(i == 0)
        def _():
            aor[...] = jnp.zeros_like(aor)
            aoa[...] = jnp.zeros_like(aoa)
            aout[...] = jnp.zeros_like(aout)
            loss_ref[...] = jnp.zeros_like(loss_ref)

        @pl.when(i % per == 0)
        def _():
            dgt_ref[...] = jnp.zeros_like(dgt_ref)

        def cat(refs):
            return jnp.concatenate([r[...] for r in refs], axis=1).astype(F32)

        def ret_head(h):
            cols = slice(h * RET_DV, (h + 1) * RET_DV)
            o = of_ref[:, cols].astype(F32) + ob_ref[:, cols].astype(F32)
            r = _rms(o)
            g = pxs[h][...].astype(F32)
            return o * r, r, g, _sigmoid(g)

        def att_half(k):
            o = oa_ref[:, k * half:(k + 1) * half].astype(F32)
            g = pxs[4 + k][...].astype(F32)
            return o, g, _sigmoid(g)

        yrs = []
        for h in range(RET_HEADS):
            on, _, g, sg = ret_head(h)
            yrs.append((on * (g * sg)).astype(BF))
        yr = jnp.concatenate(yrs, axis=1)
        yas = []
        for k in range(2):
            o, g, sg = att_half(k)
            yas.append((o * (g * sg)).astype(BF))
        ya = jnp.concatenate(yas, axis=1)

        a = jnp.dot(yr, wor_ref[...], preferred_element_type=F32)
        b = jnp.dot(ya, woa_ref[...], preferred_element_type=F32)
        sr = _sigmoid(cat(pxs[6:8]))
        sa = _sigmoid(cat(pxs[8:10]))
        yb = (sr * a + sa * b).astype(BF)
        out = jnp.dot(yb, wout_ref[...], preferred_element_type=F32)
        gate = gt_ref[...]
        err = x_ref[...] + gate * out - t_ref[...]
        loss_ref[...] += 0.5 * _sum_all(err * err) * (1.0 / d)
        dy_tok = err * (1.0 / d)
        gx_ref[...] = dy_tok
        dgt_ref[...] += jnp.sum(dy_tok * out, axis=0, keepdims=True)
        dout = (dy_tok * gate).astype(BF)
        aout[...] += _dot(yb, dout, 0, 0)
        dyy = _dot(dout, wout_ref[...], 1, 1)
        da = (dyy * sr).astype(BF)
        db = (dyy * sa).astype(BF)
        aor[...] += _dot(yr, da, 0, 0)
        aoa[...] += _dot(ya, db, 0, 0)
        dyr = _dot(da, wor_ref[...], 1, 1)
        dya = _dot(db, woa_ref[...], 1, 1)

        @pl.when(i > 0)
        def _():
            for cp in copies(i - 1):
                cp.wait()

        dtail_ref[:, d:2 * d] = (dyy * a * (sr * (1.0 - sr))).astype(BF)
        dtail_ref[:, 2 * d:] = (dyy * b * (sa * (1.0 - sa))).astype(BF)
        for h in range(RET_HEADS):
            cols = slice(h * RET_DV, (h + 1) * RET_DV)
            on, r, g, sg = ret_head(h)
            dy = dyr[:, cols]
            drg_ref[:, cols] = (dy * on * (sg * (1.0 + g * (1.0 - sg)))).astype(BF)
            dor_ref[:, cols] = _rms_bwd(dy * (g * sg), on, r).astype(BF)
        for k in range(2):
            cols = slice(k * half, (k + 1) * half)
            o, g, sg = att_half(k)
            dy = dya[:, cols]
            dtail_ref[:, cols] = (dy * o * (sg * (1.0 + g * (1.0 - sg)))).astype(BF)
            doa_ref[:, cols] = (dy * (g * sg)).astype(BF)
        for cp in copies(i):
            cp.start()

        @pl.when(i == n_t - 1)
        def _():
            for cp in copies(i):
                cp.wait()
            pltpu.sync_copy(aor, gwor_hbm)
            pltpu.sync_copy(aoa, gwoa_hbm)
            pltpu.sync_copy(aout, gwout_hbm)

    def px_blk(col):
        return pl.BlockSpec((tm, half), lambda i: (i, col // half))

    def resident(shape):
        return pl.BlockSpec(shape, lambda i: (0, 0), pipeline_mode=pl.Buffered(1))

    px_cols = ([C_RG + k * half for k in range(4)] + [C_AG, C_AG + half]
               + [C_MR, C_MR + half, C_MA, C_MA + half])
    return pl.pallas_call(
        body, name="merge", grid=(n_t,),
        in_specs=[pl.BlockSpec((tm, d), lambda i: (i, 0)),
                  pl.BlockSpec((tm, d), lambda i: (i, 0)),
                  pl.BlockSpec((tm, rv), lambda i: (i, 0)),
                  pl.BlockSpec((tm, rv), lambda i: (i, 0)),
                  pl.BlockSpec((tm, d), lambda i: (i, 0))]
        + [px_blk(col) for col in px_cols]
        + [pl.BlockSpec((None, 1, d), lambda i: (i // per, 0, 0)),
           resident((rv, d)), resident((d, d)), resident((d, d))],
        out_specs=(pl.BlockSpec((tm, d), lambda i: (i, 0)),
                   pl.BlockSpec((tm, rv), lambda i: (i, 0)),
                   pl.BlockSpec((tm, d), lambda i: (i, 0)),
                   ANY,
                   pl.BlockSpec((8, 128), lambda i: (0, 0)),
                   pl.BlockSpec((None, 1, d), lambda i: (i // per, 0, 0)),
                   ANY, ANY, ANY),
        out_shape=(SDS((t_lat, d), F32), SDS((t_lat, rv), BF), SDS((t_lat, d), BF),
                   SDS((px.shape[0], IN_COLS), BF),
                   SDS((8, 128), F32), SDS((n_samp, 1, d), F32),
                   SDS((rv, d), F32), SDS((d, d), F32), SDS((d, d), F32)),
        scratch_shapes=[pltpu.VMEM((rv, d), F32), pltpu.VMEM((d, d), F32), pltpu.VMEM((d, d), F32),
                        pltpu.VMEM((tm, rv), BF), pltpu.VMEM((tm, 3 * d), BF), pltpu.SemaphoreType.DMA((2,))],
        compiler_params=_cp(("arbitrary",), 56))(
            x_lat, target, o_f, o_b, o_att, *([px] * n_px), gate3, w_o_ret, w_o_att, w_out)


def _place():
    x, y, c = lax.axis_index("x"), lax.axis_index("y"), lax.axis_index("c")
    chips = [(1 - x, y), (x, 1 - y), (1 - x, 1 - y)]
    return x, y, c, chips


def _remote(src, dst, send_sem, recv_sem, to):
    return pltpu.make_async_remote_copy(src_ref=src, dst_ref=dst, send_sem=send_sem, recv_sem=recv_sem,
                                        device_id=to, device_id_type=MESH)


def _place_ids():
    x, y, c = lax.axis_index("x"), lax.axis_index("y"), lax.axis_index("c")
    me = 2 * x + y
    return jnp.stack([x, y, c, me, me, 2 * (1 - x) + y, 2 * x + 1 - y, 2 * (1 - x) + 1 - y]).astype(jnp.int32)


def _ag_comm(bufs, rels, arg_index=None):
    n, m = len(bufs), len(rels)

    def half(ref, s, which):
        h = ref.shape[1] // 2
        return ref.at[s, pl.ds(which * h, h), :]

    def ici(ins, outs, ssem, rsem, base):
        x, y, c, chips = _place()
        sends, recvs = [], []
        for a in range(n):
            for jj, j in enumerate(rels):
                k, chip = base + a * m + jj, chips[j]
                mine, theirs = half(outs[a], 2 * x + y, c), half(outs[a], 2 * chip[0] + chip[1], c)
                sends.append(_remote(mine, mine, ssem.at[k], rsem.at[k], (*chip, c)))
                recvs.append(_remote(theirs, theirs, ssem.at[k], rsem.at[k], (*chip, c)))
        return sends, recvs

    def d2d(ins, outs, ssem, rsem, base):
        x, y, c, chips = _place()
        sends, recvs = [], []
        for a in range(n):
            for jj, j in enumerate(rels):
                k, s = base + (n + a) * m + jj, 2 * chips[j][0] + chips[j][1]
                sends.append(_remote(half(outs[a], s, c), half(outs[a], s, c), ssem.at[k], rsem.at[k], (x, y, 1 - c)))
                recvs.append(_remote(half(outs[a], s, 1 - c), half(outs[a], s, 1 - c), ssem.at[k], rsem.at[k],
                                     (x, y, 1 - c)))
        return sends, recvs

    shapes = tuple(SDS(b.shape, b.dtype) for b in bufs)
    if arg_index is not None:
        return _Comm("all_gather", (), shapes, {}, 2 * n * m, (ici, d2d), ((arg_index, 0),))
    return _Comm("all_gather", tuple(bufs), shapes, {a: a for a in range(n)}, 2 * n * m, (ici, d2d))


def _swap_comm(grads):
    n = len(grads)

    def phase(ins, outs, ssem, rsem, base):
        x, y, c, _ = _place()
        sends = []
        for a in range(n):
            h = ins[a].shape[1] // 2
            sends.append(_remote(ins[a].at[:, pl.ds((1 - c) * h, h), :], outs[a], ssem.at[base + a],
                                 rsem.at[base + a], (x, y, 1 - c)))
        return sends, sends

    return _Comm("swap_halves", tuple(grads),
                 tuple(SDS((g.shape[0], g.shape[1] // 2, g.shape[2]), g.dtype) for g in grads), {}, n, (phase,))


def _exchange_comm(parts):
    n = len(parts)

    def phase(ins, outs, ssem, rsem, base):
        x, y, c, chips = _place()
        sends = []
        for a in range(n):
            for j, chip in enumerate(chips):
                k = base + 3 * a + j
                sends.append(_remote(ins[a].at[2 * chip[0] + chip[1]], outs[a].at[j], ssem.at[k], rsem.at[k],
                                     (*chip, c)))
        return sends, sends

    return _Comm("exchange_shards", tuple(parts), tuple(SDS((3,) + p.shape[1:], p.dtype) for p in parts), {}, 3 * n,
                 (phase,))


def _join_comm(bufs, n_parts=1):
    n = len(bufs)

    def phase(ins, outs, ssem, rsem, base):
        x, y, c, _ = _place()
        sends, recvs = [], []
        for a in range(n):
            h = outs[a].shape[0] // (2 * n_parts)
            for p in range(n_parts):
                k = base + a * n_parts + p
                mine = outs[a].at[pl.ds((2 * p + c) * h, h), :]
                other = outs[a].at[pl.ds((2 * p + 1 - c) * h, h), :]
                sends.append(_remote(mine, mine, ssem.at[k], rsem.at[k], (x, y, 1 - c)))
                recvs.append(_remote(other, other, ssem.at[k], rsem.at[k], (x, y, 1 - c)))
        return sends, recvs

    return _Comm("join_halves", tuple(bufs), tuple(SDS(b.shape, b.dtype) for b in bufs), {a: a for a in range(n)},
                 n * n_parts, (phase,))


def _cast_place(w, ids):
    rows, cols = w.shape
    tr = min(rows, 256)

    def body(ids_ref, w_ref, o_ref):
        o_ref[...] = w_ref[...].astype(BF)

    return pl.pallas_call(
        body, name="cast_place",
        grid_spec=pltpu.PrefetchScalarGridSpec(
            num_scalar_prefetch=1, grid=(rows // tr,),
            in_specs=[pl.BlockSpec((tr, cols), lambda i, ids_ref: (i, 0))],
            out_specs=pl.BlockSpec((None, tr, cols), lambda i, ids_ref: (ids_ref[3], i, 0))),
        out_shape=SDS((N_SHARD, rows, cols), BF),
        compiler_params=_cp(("parallel",), 40))(ids, w)


def _all_gather_weights(bufs):
    n = len(bufs)

    def body(*refs):
        outs = refs[n:2 * n]
        send_sems, recv_sems = refs[2 * n:]
        x, y, c, chips = _place()
        sibling = (x, y, 1 - c)
        me = 2 * x + y

        def half(ref, s, which):
            h = ref.shape[1] // 2
            return ref.at[s, pl.ds(which * h, h), :]

        first = []
        for a in range(n):
            for j, chip in enumerate(chips):
                k = a * 3 + j
                win = half(outs[a], me, c)
                first.append(_remote(win, win, send_sems.at[k], recv_sems.at[k], (*chip, c)))
        for cp in first:
            cp.start()
        passed = []
        for a in range(n):
            for j, chip in enumerate(chips):
                k = a * 3 + j
                win = half(outs[a], 2 * chip[0] + chip[1], c)
                _remote(win, win, send_sems.at[k], recv_sems.at[k], (*chip, c)).wait_recv()
                fw = _remote(win, win, send_sems.at[3 * n + k], recv_sems.at[3 * n + k], sibling)
                fw.start()
                passed.append(fw)
        for a in range(n):
            for j, chip in enumerate(chips):
                k = a * 3 + j
                win = half(outs[a], 2 * chip[0] + chip[1], 1 - c)
                _remote(win, win, send_sems.at[3 * n + k], recv_sems.at[3 * n + k], sibling).wait_recv()
        for cp in first + passed:
            cp.wait_send()

    return pl.pallas_call(
        body, name="all_gather_weights",
        in_specs=[ANY] * n, out_specs=tuple([ANY] * n),
        out_shape=tuple(SDS(b.shape, b.dtype) for b in bufs),
        input_output_aliases={a: a for a in range(n)},
        scratch_shapes=[pltpu.SemaphoreType.DMA((6 * n,)), pltpu.SemaphoreType.DMA((6 * n,))],
        compiler_params=_cp(has_side_effects=True))(*bufs)


def _swap_halves(grads):
    n = len(grads)

    def body(*refs):
        ins, outs = refs[:n], refs[n:2 * n]
        send_sems, recv_sems = refs[2 * n:]
        x, y, c, _ = _place()
        sibling = (x, y, 1 - c)

        def half(ref, which):
            h = ref.shape[1] // 2
            return ref.at[:, pl.ds(which * h, h), :]

        sends = [_remote(half(ins[a], 1 - c), outs[a], send_sems.at[a], recv_sems.at[a], sibling)
                 for a in range(n)]
        for cp in sends:
            cp.start()
        for cp in sends:
            cp.wait_recv()
        for cp in sends:
            cp.wait_send()

    return pl.pallas_call(
        body, name="swap_halves",
        in_specs=[ANY] * n, out_specs=tuple([ANY] * n),
        out_shape=tuple(SDS((g.shape[0], g.shape[1] // 2, g.shape[2]), g.dtype) for g in grads),
        scratch_shapes=[pltpu.SemaphoreType.DMA((n,)), pltpu.SemaphoreType.DMA((n,))],
        compiler_params=_cp(has_side_effects=True))(*grads)


def _chip_sum(g, p, ids):
    n_s, rows, cols = g.shape
    h = rows // 2
    tr = min(h, 256)
    nb = h // tr

    def body(ids_ref, g_ref, p_ref, o_ref, o16_ref):
        t = g_ref[...] + p_ref[...]
        o_ref[...] = t
        o16_ref[...] = t.astype(BF)

    out_spec = pl.BlockSpec((None, tr, cols), lambda s, i, ids_ref: (s, i, 0))
    return pl.pallas_call(
        body, name="chip_sum",
        grid_spec=pltpu.PrefetchScalarGridSpec(
            num_scalar_prefetch=1, grid=(n_s, nb),
            in_specs=[pl.BlockSpec((None, tr, cols), lambda s, i, ids_ref: (s, ids_ref[2] * nb + i, 0)),
                      pl.BlockSpec((None, tr, cols), lambda s, i, ids_ref: (s, i, 0))],
            out_specs=(out_spec, out_spec)),
        out_shape=(SDS((n_s, h, cols), g.dtype), SDS((n_s, h, cols), BF)),
        compiler_params=_cp(("parallel", "parallel"), 40))(ids, g, p)


def _exchange_shards(parts):
    n = len(parts)

    def body(*refs):
        ins, outs = refs[:n], refs[n:2 * n]
        send_sems, recv_sems = refs[2 * n:]
        x, y, c, chips = _place()
        sends = []
        for a in range(n):
            for j, chip in enumerate(chips):
                k = a * 3 + j
                sends.append(_remote(ins[a].at[2 * chip[0] + chip[1]], outs[a].at[j],
                                     send_sems.at[k], recv_sems.at[k], (*chip, c)))
        for cp in sends:
            cp.start()
        for cp in sends:
            cp.wait_recv()
        for cp in sends:
            cp.wait_send()

    return pl.pallas_call(
        body, name="exchange_shards",
        in_specs=[ANY] * n, out_specs=tuple([ANY] * n),
        out_shape=tuple(SDS((3,) + p.shape[1:], p.dtype) for p in parts),
        scratch_shapes=[pltpu.SemaphoreType.DMA((3 * n,)), pltpu.SemaphoreType.DMA((3 * n,))],
        compiler_params=_cp(has_side_effects=True))(*parts)


def _shard_sum(t, q, ids, part=0, n_parts=1, buf=None):
    _, h, cols = t.shape
    tr = min(h, 256)
    nb = h // tr

    def body(ids_ref, t_ref, q_ref, *rest):
        rest[-1][...] = ((t_ref[...] + q_ref[0].astype(F32)) + q_ref[1].astype(F32)) + q_ref[2].astype(F32)

    args, in_specs, aliases = [t, q], [
        pl.BlockSpec((None, tr, cols), lambda i, ids_ref: (ids_ref[3], i, 0)),
        pl.BlockSpec((3, tr, cols), lambda i, ids_ref: (0, i, 0))], None
    if buf is not None:
        args, in_specs, aliases = args + [buf], in_specs + [ANY], {2: 0}
    return _call(body, args, None, name="shard_sum", grid=(nb,), in_specs=in_specs,
                 out_specs=pl.BlockSpec((tr, cols), lambda i, ids_ref: ((2 * part + ids_ref[2]) * nb + i, 0)),
                 out_shape=SDS((2 * h * n_parts, cols), t.dtype), aliases=aliases, prefetch=ids,
                 compiler_params=_cp(("parallel",), 40))


def _join_halves(bufs):
    n = len(bufs)

    def body(*refs):
        outs = refs[n:2 * n]
        send_sems, recv_sems = refs[2 * n:]
        x, y, c, _ = _place()
        sibling = (x, y, 1 - c)

        def win(ref, which):
            h = ref.shape[0] // 2
            return ref.at[pl.ds(which * h, h), :]

        sends = [_remote(win(outs[a], c), win(outs[a], c), send_sems.at[a], recv_sems.at[a], sibling)
                 for a in range(n)]
        for cp in sends:
            cp.start()
        for a in range(n):
            other = win(outs[a], 1 - c)
            _remote(other, other, send_sems.at[a], recv_sems.at[a], sibling).wait_recv()
        for cp in sends:
            cp.wait_send()

    return pl.pallas_call(
        body, name="join_halves",
        in_specs=[ANY] * n, out_specs=tuple([ANY] * n),
        out_shape=tuple(SDS(b.shape, b.dtype) for b in bufs),
        input_output_aliases={a: a for a in range(n)},
        scratch_shapes=[pltpu.SemaphoreType.DMA((n,)), pltpu.SemaphoreType.DMA((n,))],
        compiler_params=_cp(has_side_effects=True))(*bufs)


def _gather_small(block, n_sum):
    rows, cols = block.shape
    n_dev = 8

    def body(x_ref, o_ref, g_ref, buf, send_sems, recv_sems, local_sem):
        x, y, c, chips = _place()
        me, sibling = (x, y, c), (x, y, 1 - c)

        def slot(px_, py_, pc_):
            return buf.at[4 * px_ + 2 * py_ + pc_]

        def copy(k, who, to, src=None):
            return _remote(slot(*who) if src is None else src, slot(*who), send_sems.at[k], recv_sems.at[k], to)

        mine = pltpu.make_async_copy(x_ref, slot(*me), local_sem)
        mine.start()
        first = [copy(0, me, sibling, src=x_ref)]
        first += [copy(1 + j, me, (*chip, c), src=x_ref) for j, chip in enumerate(chips)]
        for cp in first:
            cp.start()
        passed = [copy(4 + j, (*chip, c), sibling) for j, chip in enumerate(chips)]
        for j, chip in enumerate(chips):
            copy(1 + j, (*chip, c), me).wait_recv()
            passed[j].start()
        copy(0, sibling, me).wait_recv()
        for j, chip in enumerate(chips):
            copy(4 + j, (*chip, 1 - c), me).wait_recv()
        for cp in first + passed:
            cp.wait_send()
        mine.wait()
        acc = buf[0, :, :n_sum]
        for s in range(1, n_dev):
            acc = acc + buf[s, :, :n_sum]
        o_ref[...] = acc
        for s in range(n_dev):
            g_ref[s * rows:(s + 1) * rows, :] = buf[s, :, n_sum:]

    return pl.pallas_call(
        body, name="gather_small",
        in_specs=[pl.BlockSpec(memory_space=pltpu.VMEM)],
        out_specs=(pl.BlockSpec(memory_space=pltpu.VMEM), pl.BlockSpec(memory_space=pltpu.VMEM)),
        out_shape=(SDS((rows, n_sum), F32), SDS((n_dev * rows, cols - n_sum), F32)),
        scratch_shapes=[pltpu.VMEM((n_dev, rows, cols), F32), pltpu.SemaphoreType.DMA((7,)),
                        pltpu.SemaphoreType.DMA((7,)), pltpu.SemaphoreType.DMA],
        compiler_params=_cp(has_side_effects=True))(block)


def _adam_math(w, g, m, v):
    m = ADAM_B1 * m + (1.0 - ADAM_B1) * g
    v = ADAM_B2 * v + (1.0 - ADAM_B2) * (g * g)
    m_hat = m / (1.0 - ADAM_B1 ** ADAM_STEP)
    v_hat = v / (1.0 - ADAM_B2 ** ADAM_STEP)
    delta = -ADAM_LR * (m_hat / (jnp.sqrt(v_hat) + ADAM_EPS) + ADAM_WD * w)
    return delta, m, v


def _adamw(w, g, m, v):
    rows, cols = w.shape
    tr = min(rows, 256 if cols <= 2048 else 128)

    def body(w_ref, g_ref, m_ref, v_ref, go_ref, d_ref, nm_ref, nv_ref):
        g = g_ref[...]
        go_ref[...] = g
        d_ref[...], nm_ref[...], nv_ref[...] = _adam_math(w_ref[...], g, m_ref[...], v_ref[...])

    spec = pl.BlockSpec((tr, cols), lambda i: (i, 0))
    return pl.pallas_call(
        body, name="adamw", grid=(rows // tr,), in_specs=[spec] * 4, out_specs=(spec,) * 4,
        out_shape=(SDS(w.shape, F32),) * 4, compiler_params=_cp(("parallel",), 40))(w, g, m, v)


def _adamw_small(w, g, m, v):
    def body(w_ref, g_ref, m_ref, v_ref, go_ref, d_ref, nm_ref, nv_ref):
        w = w_ref[...]
        g = g_ref[...]
        sub = lax.broadcasted_iota(jnp.int32, w.shape, 0)
        lane = lax.broadcasted_iota(jnp.int32, w.shape, 1)
        is_ret = jnp.logical_and(sub == 5, lane < 2 * RET_HEADS)
        u = jnp.exp(jnp.where(is_ret, w, -1.0) * jnp.log(2.0))
        g = jnp.where(is_ret, g * (-u * jnp.log(2.0) / (1.0 - u)), g)
        go_ref[...] = g
        d_ref[...], nm_ref[...], nv_ref[...] = _adam_math(w, g, m_ref[...], v_ref[...])

    return pl.pallas_call(body, name="adamw_small", out_shape=(SDS(w.shape, F32),) * 4)(w, g, m, v)


def _rope_tables(seq, n_samp, n_ctx_rows):
    rows = seq // GRID_W
    row = jnp.repeat(jnp.arange(rows, dtype=F32), GRID_W)
    col = jnp.tile(jnp.arange(GRID_W, dtype=F32), rows)
    half = ATT_HEAD_DIM // 2
    freqs = ROPE_THETA ** (-jnp.arange(0, half, 2, dtype=F32) / half)
    ang = jnp.concatenate([row[:, None] * freqs, col[:, None] * freqs], axis=-1)
    cos, sin = jnp.cos(ang), jnp.sin(ang)
    cos_f = jnp.repeat(cos, 2, axis=1)
    sin_s = jnp.stack([-sin, sin], axis=-1).reshape(seq, ATT_HEAD_DIM)
    cos_all = jnp.concatenate([jnp.tile(cos_f, (n_samp, 1)), jnp.ones((n_ctx_rows, ATT_HEAD_DIM), F32)], axis=0)
    sin_all = jnp.concatenate([jnp.tile(sin_s, (n_samp, 1)), jnp.zeros((n_ctx_rows, ATT_HEAD_DIM), F32)], axis=0)
    return cos_all, sin_all


def _pack_small(c_ctx, norm_w, b_ada, ret, qn, kn):
    d = D_MODEL
    row5 = jnp.concatenate([ret.reshape(-1), jnp.zeros((128 - 2 * RET_HEADS,), F32), qn.reshape(-1), kn.reshape(-1),
                            jnp.zeros((d - 384,), F32)])
    return jnp.concatenate([c_ctx.reshape(1, d), norm_w.reshape(1, d), b_ada.reshape(3, d), row5.reshape(1, d),
                            jnp.zeros((2, d), F32)], axis=0)


def _unpack_small(p):
    d = D_MODEL
    return (p[0], p[1:2], p[2:5].reshape(1, 3 * d), p[5, :2 * RET_HEADS].reshape(1, 2, RET_HEADS),
            p[5:6, 128:256], p[5:6, 256:384])


def _step(x, c, ctx, c_ctx, norm_w, b_ada, ret_log2_decay, q_norm_w, k_norm_w, loss_target, weights, ids, dist):
    n_samp, seq, d = x.shape
    lc = ctx.shape[1]
    t_lat, t_ctx = n_samp * seq, n_samp * lc
    assert seq % TM == 0 and t_ctx == TM and t_lat % lc == 0 and seq % GRID_W == 0
    tps = seq // TM

    x_lat = x.reshape(t_lat, d)
    x_ctx = ctx.reshape(t_ctx, d)
    cvec8 = jnp.concatenate([c, c_ctx.reshape(1, d), jnp.zeros((8 - n_samp - 1, d), F32)], axis=0)
    lg = jnp.log1p(-jnp.exp2(ret_log2_decay.reshape(2, RET_HEADS)))
    cos_all, sin_all = _rope_tables(seq, n_samp, t_ctx)

    w_ada_b, w_in_b, w_or_b, w_oa_b, w_out_b = weights
    w_ada_g = _run_comm(_ag_comm((w_ada_b,), (0, 1, 2)))[0] if dist else w_ada_b
    mod8 = _adaln_fwd(cvec8, w_ada_g, b_ada)
    mod3 = mod8[:n_samp + 1]
    shift3 = mod3[:, None, 0:d]
    scale3 = mod3[:, None, d:2 * d]
    gate3 = mod3[:, None, 2 * d:3 * d]

    hx, hxt = _norm_fwd(x_lat, x_ctx, norm_w, scale3, shift3, tps, n_samp)
    if dist:
        px, w_in_g = _in_proj_gather(hx, w_in_b, ids)
    else:
        w_in_g = w_in_b
        px = _in_proj(hx, w_in_g, ids, 0, N_SHARD)

    states0 = _ctx_state_fwd(px, lg, n_samp, t_lat, lc)
    if dist:
        (o_f, o_b, saved), w_o = _ret_fwd(px, states0, lg, n_samp, seq,
                                          comm=_ag_comm((w_or_b, w_oa_b, w_out_b), (0, 1, 2)))
    else:
        (o_f, o_b, saved), w_o = _ret_fwd(px, states0, lg, n_samp, seq), (w_or_b, w_oa_b, w_out_b)
    w_o_ret, w_o_att, w_out = (w.reshape(-1, d) for w in w_o)

    kn, vn = _att_prep_kv(px, cos_all, sin_all, k_norm_w)
    o_att, lse = _att_fwd(px, kn, vn, cos_all, sin_all, q_norm_w, n_samp, seq, lc)

    (gx_res, do, do_att, dpx, loss8, dgate, g_w_o_ret, g_w_o_att, g_w_out) = _merge(
        x_lat, loss_target.reshape(t_lat, d), o_f, o_b, o_att, px, gate3, w_o_ret, w_o_att, w_out, tps)

    g_a = [g.reshape(N_SHARD, -1, d) for g in (g_w_o_ret, g_w_o_att, g_w_out)]
    res = _att_bwd(dpx, kn, vn, px, o_att, lse, do_att, cos_all, sin_all, q_norm_w, n_samp, seq, lc,
                   comm=_swap_comm(g_a) if dist else None)
    (dpx, dkl, dkc, dvl, dvc, gqw), sib_a = res if dist else (res, None)
    dpx, gkw = _att_kv_bwd(dpx, dkl, dkc, dvl, dvc, px, cos_all, sin_all, k_norm_w)
    if dist:
        t_a = [_chip_sum(g, p, ids) for g, p in zip(g_a, sib_a)]

    res = _ret_bwd(dpx, px, do, saved, lg, n_samp, seq,
                   comm=_exchange_comm([t16 for _, t16 in t_a]) if dist else None)
    (dpx, dstates, dlg_lat), q_a = res if dist else (res, None)
    if dist:
        r_a = [_shard_sum(t, q, ids) for (t, _), q in zip(t_a, q_a)]
    dpx, dlg_ctx = _ctx_state_bwd(dpx, px, dstates, lg, n_samp, t_lat, lc)
    dpx = _zero_ctx_tail(dpx, t_lat)

    n_tiles = dpx.shape[0] // _big_rows(dpx.shape[0])
    if dist:
        g_b = _gw_in(hxt, dpx, 0, 1)
        dhx, (sib_b, *r_a) = _dhx(dpx, w_in_g, 0, 1, comm=_join_comms(_swap_comm([g_b]), _join_comm(r_a)))
        t_b, t16_b = _chip_sum(g_b, sib_b, ids)
        dhx, (q_b,) = _dhx(dpx, w_in_g, 1, n_tiles - 1, dhx=dhx, comm=_exchange_comm([t16_b]))
        r_b_half = _shard_sum(t_b, q_b, ids)
    else:
        g_w_in = _gw_in(hxt, dpx, 0, 1)
        dhx = _dhx(dpx, w_in_g, 0, n_tiles)
    grad_x, dshift, dscale, g_norm_w = _norm_bwd(x_lat, x_ctx, dhx, gx_res, norm_w, scale3, tps, n_samp)

    dgate_all = jnp.concatenate([dgate, jnp.zeros((1, 1, d), F32)], axis=0)
    dmod3 = jnp.concatenate([dshift, dscale, dgate_all], axis=2).reshape(n_samp + 1, 3 * d)
    dmod8 = jnp.concatenate([dmod3, jnp.zeros((8 - n_samp - 1, 3 * d), F32)], axis=0)
    g_lg = (jnp.sum(dlg_lat[:, :, 0], axis=0).reshape(2, RET_HEADS)
            + jnp.stack([jnp.sum(dlg_ctx[:, :, 0, 0], axis=0), jnp.sum(dlg_ctx[:, :, 1, 0], axis=0)], axis=0))
    g_qw = jnp.sum(gqw, axis=(0, 1, 2))
    zero = jnp.zeros((d,), F32)
    if not dist:
        g_w_ada, g_b_ada, dc8 = _adaln_bwd(cvec8, dmod8, w_ada_g)
        small = _pack_small(dc8[n_samp], g_norm_w, g_b_ada, g_lg, g_qw, gkw)
        return (loss8[0, 0], grad_x.reshape(n_samp, seq, d),
                (g_w_ada, g_w_in, g_w_o_ret, g_w_o_att, g_w_out), small)

    local = _pack_small(zero, g_norm_w, jnp.zeros((3 * d,), F32), g_lg, g_qw, gkw).at[6, 0].set(loss8[0, 0])
    small_sum, gathered = _gather_small(jnp.concatenate([local, cvec8, dmod8], axis=1), d)
    (g_w_ada, g_b_ada, dc_all), (r_b,) = _adaln_bwd(gathered[:, :d], gathered[:, d:], w_ada_g,
                                                     comm=_join_comm([r_b_half]))
    dc_ctx = jnp.sum(dc_all.reshape(-1, 8, d)[:, n_samp], axis=0)
    small = small_sum + _pack_small(dc_ctx, zero, g_b_ada, jnp.zeros((2, RET_HEADS), F32), zero[:128], zero[:128])
    r_c = lax.dynamic_index_in_dim(g_w_ada, ids[3], 0, keepdims=False)
    return small[6, 0], grad_x.reshape(n_samp, seq, d), (r_c, r_b, *r_a), small


def kernel(x, c, ctx, c_ctx, norm_w, w_ada, b_ada, w_in, ret_log2_decay, q_norm_w, k_norm_w, w_o_ret, w_o_att, w_out, loss_target, m_c_ctx, m_norm_w, m_w_ada, m_b_ada, m_w_in, m_ret_log2_decay, m_q_norm_w, m_k_norm_w, m_w_o_ret, m_w_o_att, m_w_out, v_c_ctx, v_norm_w, v_w_ada, v_b_ada, v_w_in, v_ret_log2_decay, v_q_norm_w, v_k_norm_w, v_w_o_ret, v_w_o_att, v_w_out):
    big_w = (w_ada[0], w_in[0], w_o_ret[0], w_o_att[0], w_out[0])
    big_m = (m_w_ada[0], m_w_in[0], m_w_o_ret[0], m_w_o_att[0], m_w_out[0])
    big_v = (v_w_ada[0], v_w_in[0], v_w_o_ret[0], v_w_o_att[0], v_w_out[0])

    ids = _place_ids()
    loss, grad_x, big_grad, small_grad_in = _step(
        x, c, ctx, c_ctx, norm_w[0:1], b_ada[0:1], ret_log2_decay[0], q_norm_w[0:1], k_norm_w[0:1], loss_target,
        tuple(_cast_place(w, ids) for w in big_w), ids, True)
    small_w = _pack_small(c_ctx, norm_w, b_ada, ret_log2_decay, q_norm_w, k_norm_w)
    small_m = _pack_small(m_c_ctx, m_norm_w, m_b_ada, m_ret_log2_decay, m_q_norm_w, m_k_norm_w)
    small_v = _pack_small(v_c_ctx, v_norm_w, v_b_ada, v_ret_log2_decay, v_q_norm_w, v_k_norm_w)
    small_grad, small_delta, small_nm, small_nv = _adamw_small(small_w, small_grad_in, small_m, small_v)

    big_g, big_delta, big_nm, big_nv = [], [], [], []
    for w, g, m, v in zip(big_w, big_grad, big_m, big_v):
        go, dlt, nm, nv = _adamw(w, g, m, v)
        big_g.append(go[None])
        big_delta.append(dlt[None])
        big_nm.append(nm[None])
        big_nv.append(nv[None])
    big_grad = big_g

    def order(small_packed, big):
        s = _unpack_small(small_packed)
        return (s[0], s[1], big[0], s[2], big[1], s[3], s[4], s[5], big[2], big[3], big[4])

    return (loss, grad_x, *order(small_grad, big_grad), *order(small_delta, big_delta),
            *order(small_nm, big_nm), *order(small_nv, big_nv))
```

```python
import functools
from typing import NamedTuple

import jax
import jax.numpy as jnp
from jax import lax
from jax.experimental import pallas as pl
from jax.experimental.pallas import tpu as pltpu

F32 = jnp.float32
BF = jnp.bfloat16
SDS = jax.ShapeDtypeStruct
MESH = pl.DeviceIdType.MESH
ANY = pl.BlockSpec(memory_space=pl.ANY)
SMEM = pl.BlockSpec(memory_space=pltpu.SMEM)

D_MODEL = 1024
GRID_W = 64
RET_HEADS = 4
RET_DK = 256
RET_DV = 512
RET_CHUNK = 128
ATT_HEADS = 8
ATT_KV_HEADS = 2
ATT_REP = ATT_HEADS // ATT_KV_HEADS
ATT_HEAD_DIM = 128
ROPE_THETA = 10000.0
NORM_EPS = 1e-6
IN_COLS = 10752
KV_COLS = 3584
C_RK, C_RV, C_AK, C_AV, C_RQ, C_RG, C_AQ, C_AG, C_MR, C_MA = 0, 1024, 3072, 3328, 3584, 4608, 6656, 7680, 8704, 9728
N_SHARD = 4
ADA_W = 3 * D_MODEL // N_SHARD
IN_W = IN_COLS // N_SHARD
IN_BLK = IN_W
BPS = IN_W // IN_BLK
N_IN_BLK = IN_COLS // IN_BLK
GATHER_BLOCKS = 3
GATHER_ORDER = ((0, 0), (0, 1), (0, 2), (1, 0), (2, 0), (1, 1), (2, 1), (1, 2), (2, 2), (3, 0), (3, 1), (3, 2))
TM = 512
ATT_TQ = 512
ADAM_LR, ADAM_B1, ADAM_B2, ADAM_EPS, ADAM_WD, ADAM_STEP = 0.001, 0.9, 0.999, 1e-08, 0.01, 10
MIB = 1024 * 1024


def _cp(sem=None, vmem_mb=None, **kw):
    if sem is not None:
        kw["dimension_semantics"] = sem
    if vmem_mb is not None:
        kw["vmem_limit_bytes"] = vmem_mb * MIB
    return pltpu.CompilerParams(**kw)


def _dot(a, b, ca=1, cb=0):
    return lax.dot_general(a.astype(BF), b.astype(BF), (((ca,), (cb,)), ((), ())), preferred_element_type=F32)


def _sigmoid(x):
    return 0.5 * jnp.tanh(0.5 * x) + 0.5


def _sum_all(x):
    return jnp.sum(jnp.sum(x, axis=1, keepdims=True), axis=0, keepdims=True)


def _swap_pairs(x):
    ax = x.ndim - 1
    lane = lax.broadcasted_iota(jnp.int32, x.shape, ax)
    nxt = pltpu.roll(x, x.shape[ax] - 1, ax)
    prv = pltpu.roll(x, 1, ax)
    return jnp.where(lane % 2 == 0, nxt, prv)


def _rms(x):
    return lax.rsqrt(jnp.mean(x * x, axis=-1, keepdims=True) + NORM_EPS)


def _rms_bwd(dxh, xh, r):
    return r * (dxh - xh * jnp.mean(dxh * xh, axis=-1, keepdims=True))


class _Comm(NamedTuple):
    name: str
    ins: tuple
    out_shapes: tuple
    aliases: dict
    n_sems: int
    phases: tuple
    arg_aliases: tuple = ()


def _join_comms(*comms):
    comms = [cm for cm in comms if cm is not None]
    if len(comms) <= 1:
        return comms[0] if comms else None
    offs, i_off, o_off, s_off = [], 0, 0, 0
    for cm in comms:
        offs.append((i_off, o_off, s_off))
        i_off, o_off, s_off = i_off + len(cm.ins), o_off + len(cm.out_shapes), s_off + cm.n_sems

    def phase(k):
        def run(ins, outs, ssem, rsem, base):
            sends, recvs = [], []
            for cm, (io, oo, so) in zip(comms, offs):
                if k < len(cm.phases):
                    s, r = cm.phases[k](ins[io:io + len(cm.ins)], outs[oo:oo + len(cm.out_shapes)], ssem, rsem,
                                        base + so)
                    sends += s
                    recvs += r
            return sends, recvs
        return run

    aliases, arg_aliases = {}, ()
    for cm, (io, oo, _) in zip(comms, offs):
        aliases.update({io + a: oo + b for a, b in cm.aliases.items()})
        arg_aliases += tuple((a, oo + b) for a, b in cm.arg_aliases)
    return _Comm("+".join(cm.name for cm in comms), sum((cm.ins for cm in comms), ()),
                 sum((cm.out_shapes for cm in comms), ()), aliases, s_off,
                 tuple(phase(k) for k in range(max(len(cm.phases) for cm in comms))), arg_aliases)


def _run_phases(comm, cins, couts, ssem, rsem, first_started):
    for k, phase in enumerate(comm.phases):
        sends, recvs = phase(cins, couts, ssem, rsem, 0)
        if k > 0 or not first_started:
            for cp in sends:
                cp.start()
        for cp in recvs:
            cp.wait_recv()
        for cp in sends:
            cp.wait_send()


def _call(body, args, comm=None, *, name, grid, in_specs, out_specs, out_shape, scratch_shapes=(),
          compiler_params, aliases=None, prefetch=None):
    single = not isinstance(out_shape, (tuple, list))
    out_specs_t = (out_specs,) if single else tuple(out_specs)
    out_shape_t = (out_shape,) if single else tuple(out_shape)
    n_pre = 0 if prefetch is None else 1
    n_in, n_out, n_sc = len(in_specs), len(out_specs_t), len(scratch_shapes)
    io_alias = {n_pre + a: b for a, b in (aliases or {}).items()}
    if comm is None:
        kernel_body, cin, cout, csems = body, [], [], []
    else:
        n_ci, n_co = len(comm.ins), len(comm.out_shapes)
        cin, cout = [ANY] * n_ci, [ANY] * n_co
        csems = [pltpu.SemaphoreType.DMA((comm.n_sems,)), pltpu.SemaphoreType.DMA((comm.n_sems,))]
        io_alias.update({n_pre + n_in + a: n_out + b for a, b in comm.aliases.items()})
        io_alias.update({n_pre + a: n_out + b for a, b in comm.arg_aliases})

        def kernel_body(*refs):
            pre, refs = refs[:n_pre], refs[n_pre:]
            ins, cins = refs[:n_in], refs[n_in:n_in + n_ci]
            outs = refs[n_in + n_ci:n_in + n_ci + n_out]
            couts = refs[n_in + n_ci + n_out:n_in + n_ci + n_out + n_co]
            scratch = refs[n_in + n_ci + n_out + n_co:n_in + n_ci + n_out + n_co + n_sc]
            ssem, rsem = refs[-2:]
            first = functools.reduce(jnp.logical_and, [pl.program_id(k) == 0 for k in range(len(grid))])
            last = functools.reduce(jnp.logical_and, [pl.program_id(k) == grid[k] - 1 for k in range(len(grid))])

            @pl.when(first)
            def _():
                for cp in comm.phases[0](cins, couts, ssem, rsem, 0)[0]:
                    cp.start()

            body(*pre, *ins, *outs, *scratch)

            @pl.when(last)
            def _():
                _run_phases(comm, cins, couts, ssem, rsem, True)

        name = name + "+" + comm.name

    all_in, all_out = list(in_specs) + cin, out_specs_t + tuple(cout)
    shapes = out_shape_t + (tuple(comm.out_shapes) if comm is not None else ())
    scratch = list(scratch_shapes) + csems
    if prefetch is None:
        res = pl.pallas_call(kernel_body, name=name, grid=grid, in_specs=all_in, out_specs=all_out, out_shape=shapes,
                             scratch_shapes=scratch, input_output_aliases=io_alias,
                             compiler_params=compiler_params)(*args, *(comm.ins if comm is not None else ()))
    else:
        res = pl.pallas_call(
            kernel_body, name=name, out_shape=shapes, input_output_aliases=io_alias, compiler_params=compiler_params,
            grid_spec=pltpu.PrefetchScalarGridSpec(num_scalar_prefetch=1, grid=grid, in_specs=all_in,
                                                   out_specs=all_out, scratch_shapes=scratch))(
                                                       prefetch, *args, *(comm.ins if comm is not None else ()))
    own = res[0] if single else tuple(res[:n_out])
    return own if comm is None else (own, tuple(res[n_out:]))


def _run_comm(comm):
    n_ci, n_co = len(comm.ins), len(comm.out_shapes)

    def body(*refs):
        _run_phases(comm, refs[:n_ci], refs[n_ci:n_ci + n_co], refs[-2], refs[-1], False)

    return pl.pallas_call(
        body, name=comm.name, in_specs=[ANY] * n_ci, out_specs=tuple([ANY] * n_co), out_shape=tuple(comm.out_shapes),
        input_output_aliases=dict(comm.aliases),
        scratch_shapes=[pltpu.SemaphoreType.DMA((comm.n_sems,)), pltpu.SemaphoreType.DMA((comm.n_sems,))],
        compiler_params=_cp(has_side_effects=True))(*comm.ins)


def _adaln_fwd(cvec8, w_ada_g, b_ada):
    def body(c_ref, w_ref, b_ref, o_ref):
        cv = c_ref[...]
        sc = (cv * _sigmoid(cv)).astype(BF)
        for s in range(N_SHARD):
            cols = slice(s * ADA_W, (s + 1) * ADA_W)
            o_ref[:, cols] = jnp.dot(sc, w_ref[s], preferred_element_type=F32) + b_ref[:, cols]

    return pl.pallas_call(body, out_shape=SDS((8, 3 * D_MODEL), F32), name="adaln_fwd",
                          compiler_params=_cp(vmem_mb=32))(cvec8, w_ada_g, b_ada)


def _adaln_bwd(cvec, dmod, w_ada_g, comm=None):
    n_rows = cvec.shape[0]
    def body(c_ref, d_ref, w_ref, gw_ref, gb_ref, dc_ref):
        cv = c_ref[...]
        sg = _sigmoid(cv)
        sc = cv * sg
        dm = d_ref[...]
        gb_ref[...] = jnp.sum(dm, axis=0, keepdims=True)
        dsc = jnp.zeros(cv.shape, F32)
        for s in range(N_SHARD):
            cols = slice(s * ADA_W, (s + 1) * ADA_W)
            gw_ref[s] = _dot(sc, dm[:, cols], 0, 0)
            dsc = dsc + _dot(dm[:, cols], w_ref[s], 1, 1)
        dc_ref[...] = dsc * (sg * (1.0 + cv * (1.0 - sg)))

    def whole(shape):
        return pl.BlockSpec(shape, lambda i: (0,) * len(shape))

    shapes = ((N_SHARD, D_MODEL, ADA_W), (1, 3 * D_MODEL), (n_rows, D_MODEL))
    return _call(body, [cvec, dmod, w_ada_g], comm, name="adaln_bwd", grid=(1,),
                 in_specs=[whole(cvec.shape), whole(dmod.shape), whole(w_ada_g.shape)],
                 out_specs=tuple(whole(s) for s in shapes), out_shape=tuple(SDS(s, F32) for s in shapes),
                 compiler_params=_cp(("arbitrary",), 56))


def _big_rows(rows):
    return 1536 if rows % 1536 == 0 else TM


def _norm_fwd(x_lat, x_ctx, norm_w, scale3, shift3, tiles_per_sample, n_samp):
    n_lat = x_lat.shape[0] // TM
    rows = x_lat.shape[0] + x_ctx.shape[0]

    def samp(i):
        return jnp.minimum(i // tiles_per_sample, n_samp)

    def body(x_ref, c_ref, nw_ref, sc_ref, sh_ref, hx_ref, hxt_ref):
        x = jnp.where(pl.program_id(0) < n_lat, x_ref[...], c_ref[...])
        h = x * _rms(x) * nw_ref[...] * (1.0 + sc_ref[...]) + sh_ref[...]
        hx_ref[...] = h.astype(BF)
        hxt_ref[...] = h.T.astype(BF)

    return pl.pallas_call(
        body, name="norm_fwd", grid=(rows // TM,),
        in_specs=[pl.BlockSpec((TM, D_MODEL), lambda i: (jnp.minimum(i, n_lat - 1), 0)),
                  pl.BlockSpec((TM, D_MODEL), lambda i: (jnp.maximum(i - n_lat, 0), 0)),
                  pl.BlockSpec((1, D_MODEL), lambda i: (0, 0)),
                  pl.BlockSpec((None, 1, D_MODEL), lambda i: (samp(i), 0, 0)),
                  pl.BlockSpec((None, 1, D_MODEL), lambda i: (samp(i), 0, 0))],
        out_specs=(pl.BlockSpec((TM, D_MODEL), lambda i: (i, 0)),
                   pl.BlockSpec((D_MODEL, TM), lambda i: (0, i))),
        out_shape=(SDS((rows, D_MODEL), BF), SDS((D_MODEL, rows), BF)),
        compiler_params=_cp(("parallel",), 40))(x_lat, x_ctx, norm_w, scale3, shift3)


def _in_proj(hx, w_in_g, ids, first, count, px=None, comm=None):
    rows = hx.shape[0]
    tb = _big_rows(rows)

    def shard(j, ids_ref):
        return ids_ref[4 + first + j // BPS]

    def body(ids_ref, h_ref, w_ref, *rest):
        px_ref = rest[-1]
        px_ref[...] = jnp.dot(h_ref[...], w_ref[...], preferred_element_type=F32).astype(BF)

    args, in_specs, aliases = [hx, w_in_g], [
        pl.BlockSpec((tb, D_MODEL), lambda j, i, ids_ref: (i, 0)),
        pl.BlockSpec((None, D_MODEL, IN_BLK), lambda j, i, ids_ref: (shard(j, ids_ref), 0, j % BPS))], None
    if px is not None:
        args, in_specs, aliases = args + [px], in_specs + [ANY], {2: 0}
    return _call(body, args, comm, name="in_proj", grid=(BPS * count, rows // tb), in_specs=in_specs,
                 out_specs=pl.BlockSpec((tb, IN_BLK),
                                        lambda j, i, ids_ref: (i, BPS * shard(j, ids_ref) + j % BPS)),
                 out_shape=SDS((rows, IN_COLS), BF), aliases=aliases, prefetch=ids,
                 compiler_params=_cp(("arbitrary", "arbitrary"), 56))


def _norm_bwd(x_lat, x_ctx, dhx, gx_res, norm_w, scale3, tiles_per_sample, n_samp, comm=None):
    rows = x_lat.shape[0] + x_ctx.shape[0]
    n_lat = tiles_per_sample * n_samp

    def samp(i):
        return jnp.minimum(i // tiles_per_sample, n_samp)

    def lat(i):
        return jnp.minimum(i, n_lat - 1)

    def body(x_ref, c_ref, dh_ref, gr_ref, nw_ref, sc_ref, gx_ref, dsh_ref, dsc_ref, dnw_ref):
        i = pl.program_id(0)
        x = jnp.where(i < n_lat, x_ref[...], c_ref[...])
        r = _rms(x)
        xh = x * r
        nw = nw_ref[...]
        dh = dh_ref[...]
        first = jnp.logical_or(i % tiles_per_sample == 0, i >= n_lat)

        @pl.when(first)
        def _():
            dsh_ref[...] = jnp.zeros_like(dsh_ref)
            dsc_ref[...] = jnp.zeros_like(dsc_ref)

        @pl.when(i == 0)
        def _():
            dnw_ref[...] = jnp.zeros_like(dnw_ref)

        dsh_ref[...] += jnp.sum(dh, axis=0, keepdims=True)
        dsc_ref[...] += jnp.sum(dh * (xh * nw), axis=0, keepdims=True)
        du = dh * (1.0 + sc_ref[...])
        dnw_ref[...] += jnp.sum(du * xh, axis=0, keepdims=True)

        @pl.when(i < n_lat)
        def _():
            gx_ref[...] = gr_ref[...] + _rms_bwd(du * nw, xh, r)

    return _call(
        body, [x_lat, x_ctx, dhx, gx_res, norm_w, scale3], comm, name="norm_bwd", grid=(rows // TM,),
        in_specs=[pl.BlockSpec((TM, D_MODEL), lambda i: (lat(i), 0)),
                  pl.BlockSpec((TM, D_MODEL), lambda i: (jnp.maximum(i - n_lat, 0), 0)),
                  pl.BlockSpec((TM, D_MODEL), lambda i: (i, 0)),
                  pl.BlockSpec((TM, D_MODEL), lambda i: (lat(i), 0)),
                  pl.BlockSpec((1, D_MODEL), lambda i: (0, 0)),
                  pl.BlockSpec((None, 1, D_MODEL), lambda i: (samp(i), 0, 0))],
        out_specs=(pl.BlockSpec((TM, D_MODEL), lambda i: (lat(i), 0)),
                   pl.BlockSpec((None, 1, D_MODEL), lambda i: (samp(i), 0, 0)),
                   pl.BlockSpec((None, 1, D_MODEL), lambda i: (samp(i), 0, 0)),
                   pl.BlockSpec((1, D_MODEL), lambda i: (0, 0))),
        out_shape=(SDS((n_lat * TM, D_MODEL), F32), SDS((n_samp + 1, 1, D_MODEL), F32),
                   SDS((n_samp + 1, 1, D_MODEL), F32), SDS((1, D_MODEL), F32)),
        compiler_params=_cp(("arbitrary",), 40))


def _in_proj_gather(hx, w_buf, ids):
    rows = hx.shape[0]
    tb = _big_rows(rows)
    n_i = rows // tb
    hrows = D_MODEL // 2
    cb = IN_W // GATHER_BLOCKS
    n_blk = len(GATHER_ORDER)

    def body(ids_ref, h_ref, w_in_hbm, px_ref, w_hbm, wv, lsem, ssem, rsem):
        j, i = pl.program_id(0), pl.program_id(1)
        x, y, c, chips = _place()
        sibling = (x, y, 1 - c)

        def blk(s, which, k):
            return w_hbm.at[s, pl.ds(which * hrows, hrows), pl.ds(k * cb, cb)]

        def over_ici(rel, k):
            chip, n = chips[rel], GATHER_BLOCKS * rel + k
            mine, theirs = blk(2 * x + y, c, k), blk(2 * chip[0] + chip[1], c, k)
            return (_remote(mine, mine, ssem.at[n], rsem.at[n], (*chip, c)),
                    _remote(theirs, theirs, ssem.at[n], rsem.at[n], (*chip, c)))

        def over_d2d(rel, k):
            s, n = 2 * chips[rel][0] + chips[rel][1], GATHER_BLOCKS * (3 + rel) + k
            return (_remote(blk(s, c, k), blk(s, c, k), ssem.at[n], rsem.at[n], sibling),
                    _remote(blk(s, 1 - c, k), blk(s, 1 - c, k), ssem.at[n], rsem.at[n], sibling))

        @pl.when(jnp.logical_and(j == 0, i == 0))
        def _():
            for k in range(GATHER_BLOCKS):
                over_ici(0, k)[0].start()
                over_ici(1, k)[0].start()

        for jj, (pos, k) in enumerate(GATHER_ORDER):
            if jj + 1 < n_blk and GATHER_ORDER[jj + 1][0] > 0:
                nrel, nk = GATHER_ORDER[jj + 1][0] - 1, GATHER_ORDER[jj + 1][1]

                @pl.when(jnp.logical_and(j == jj, i == min(1, n_i - 1)))
                def _(nrel=nrel, nk=nk):
                    over_ici(nrel, nk)[1].wait_recv()
                    over_d2d(nrel, nk)[0].start()
                    if (nrel, nk) == (1, GATHER_BLOCKS - 1):
                        for kk in range(GATHER_BLOCKS):
                            over_ici(2, kk)[0].start()

            @pl.when(jnp.logical_and(j == jj, i == 0))
            def _(pos=pos, k=k):
                if pos > 0:
                    over_d2d(pos - 1, k)[1].wait_recv()
                cp = pltpu.make_async_copy(w_hbm.at[ids_ref[4 + pos], :, pl.ds(k * cb, cb)], wv, lsem)
                cp.start()
                cp.wait()

        px_ref[...] = jnp.dot(h_ref[...], wv[...], preferred_element_type=F32).astype(BF)

        @pl.when(jnp.logical_and(j == n_blk - 1, i == n_i - 1))
        def _():
            for rel in range(3):
                for k in range(GATHER_BLOCKS):
                    over_ici(rel, k)[0].wait_send()
                    over_d2d(rel, k)[0].wait_send()

    def col_block(j, ids_ref):
        return GATHER_BLOCKS * ids_ref[4 + ids_ref[8 + j]] + ids_ref[8 + n_blk + j]

    n_sem = 6 * GATHER_BLOCKS
    return pl.pallas_call(
        body, name="in_proj_gather", input_output_aliases={2: 1},
        grid_spec=pltpu.PrefetchScalarGridSpec(
            num_scalar_prefetch=1, grid=(n_blk, n_i),
            in_specs=[pl.BlockSpec((tb, D_MODEL), lambda j, i, ids_ref: (i, 0)), ANY],
            out_specs=(pl.BlockSpec((tb, cb), lambda j, i, ids_ref: (i, col_block(j, ids_ref))), ANY),
            scratch_shapes=[pltpu.VMEM((D_MODEL, cb), BF), pltpu.SemaphoreType.DMA,
                            pltpu.SemaphoreType.DMA((n_sem,)), pltpu.SemaphoreType.DMA((n_sem,))]),
        out_shape=(SDS((rows, IN_COLS), BF), SDS(w_buf.shape, w_buf.dtype)),
        compiler_params=_cp(("arbitrary", "arbitrary"), 56))(ids, hx, w_buf)


def _gw_in(hxt, dpx_all, part, n_parts, comm=None):
    rows = dpx_all.shape[0]
    tb = _big_rows(rows)
    dp = D_MODEL // n_parts

    def body(h_ref, d_ref, o_ref):
        @pl.when(pl.program_id(1) == 0)
        def _():
            o_ref[...] = jnp.zeros_like(o_ref)

        o_ref[...] += jnp.dot(h_ref[...], d_ref[...], preferred_element_type=F32)

    return _call(body, [hxt, dpx_all], comm, name="gw_in", grid=(N_IN_BLK, rows // tb),
                 in_specs=[pl.BlockSpec((dp, tb), lambda j, i: (part, i)),
                           pl.BlockSpec((tb, IN_BLK), lambda j, i: (i, j))],
                 out_specs=pl.BlockSpec((None, dp, IN_BLK), lambda j, i: (j // BPS, 0, j % BPS)),
                 out_shape=SDS((N_SHARD, dp, IN_W), F32),
                 compiler_params=_cp(("arbitrary", "arbitrary"), 56))


def _dhx(dpx_all, w_in_g, tile0, n_tiles, dhx=None, comm=None):
    rows = dpx_all.shape[0]
    tb = _big_rows(rows)

    def body(d_ref, w_ref, *rest):
        o_ref = rest[-1]

        @pl.when(pl.program_id(1) == 0)
        def _():
            o_ref[...] = jnp.zeros_like(o_ref)

        o_ref[...] += lax.dot_general(d_ref[...], w_ref[...], (((1,), (1,)), ((), ())), preferred_element_type=F32)

    args, in_specs, aliases = [dpx_all, w_in_g], [
        pl.BlockSpec((tb, IN_BLK), lambda i, j: (tile0 + i, j)),
        pl.BlockSpec((None, D_MODEL, IN_BLK), lambda i, j: (j // BPS, 0, j % BPS))], None
    if dhx is not None:
        args, in_specs, aliases = args + [dhx], in_specs + [ANY], {2: 0}
    return _call(body, args, comm, name="dhx", grid=(n_tiles, N_IN_BLK), in_specs=in_specs,
                 out_specs=pl.BlockSpec((tb, D_MODEL), lambda i, j: (tile0 + i, 0)),
                 out_shape=SDS((rows, D_MODEL), F32), aliases=aliases,
                 compiler_params=_cp(("arbitrary", "arbitrary"), 56))


def _decays(lgv, d):
    c = RET_CHUNK
    ii = lax.broadcasted_iota(jnp.int32, (c, 1), 0).astype(F32)
    jj = lax.broadcasted_iota(jnp.int32, (1, c), 1).astype(F32)
    a_i = jnp.where(d == 0, ii, c - 1.0 - ii)
    a_j = jnp.where(d == 0, jj, c - 1.0 - jj)
    rel = a_i - a_j
    mask = jnp.where(rel >= 0, jnp.exp(lgv * jnp.maximum(rel, 0.0)), 0.0)
    qd = jnp.exp(lgv * (a_i + 1.0))
    kd = jnp.exp(lgv * (c - 1.0 - a_i))
    gc = jnp.exp(jnp.full((1, 1), lgv * c, F32))
    return a_i, rel, mask, qd, kd, gc


def _ctx_state_fwd(px, lg, n_samp, t_lat, lc):
    rb = t_lat // lc

    def body(lg_ref, k_ref, v_ref, o_ref):
        h = pl.program_id(1)
        k = k_ref[...].astype(F32) * (RET_DK ** -0.5)
        v = v_ref[...]
        pos = lax.broadcasted_iota(jnp.int32, (lc, 1), 0).astype(F32)
        o_ref[0] = _dot(k * jnp.exp(lg_ref[0, h] * (lc - 1.0 - pos)), v, 0, 0)
        o_ref[1] = _dot(k * jnp.exp(lg_ref[1, h] * pos), v, 0, 0)

    return pl.pallas_call(
        body, name="ctx_state_fwd", grid=(n_samp, RET_HEADS),
        in_specs=[SMEM,
                  pl.BlockSpec((lc, RET_DK), lambda b, h: (rb + b, C_RK // RET_DK + h)),
                  pl.BlockSpec((lc, RET_DV), lambda b, h: (rb + b, C_RV // RET_DV + h))],
        out_specs=pl.BlockSpec((None, 2, None, RET_DK, RET_DV), lambda b, h: (b, 0, h, 0, 0)),
        out_shape=SDS((n_samp, 2, RET_HEADS, RET_DK, RET_DV), F32),
        compiler_params=_cp(("parallel", "parallel")))(lg, px, px)


def _ctx_state_bwd(dpx, px, dstates, lg, n_samp, t_lat, lc):
    rb = t_lat // lc
    kspec = pl.BlockSpec((lc, RET_DK), lambda b, h: (rb + b, C_RK // RET_DK + h))
    vspec = pl.BlockSpec((lc, RET_DV), lambda b, h: (rb + b, C_RV // RET_DV + h))
    sspec = pl.BlockSpec((None, 2, None, RET_DK, RET_DV), lambda b, h: (b, 0, h, 0, 0))

    def weights(lg_ref, h):
        pos = lax.broadcasted_iota(jnp.int32, (lc, 1), 0).astype(F32)
        e_f = lc - 1.0 - pos
        return pos, e_f, jnp.exp(lg_ref[0, h] * e_f), jnp.exp(lg_ref[1, h] * pos)

    def k_body(lg_ref, dpx_hbm, k_ref, v_ref, ds_ref, dk_ref, dlg_ref):
        pos, e_f, w_f, w_b = weights(lg_ref, pl.program_id(1))
        k = k_ref[...].astype(F32) * (RET_DK ** -0.5)
        y_f = _dot(v_ref[...], ds_ref[0], 1, 1) * w_f
        y_b = _dot(v_ref[...], ds_ref[1], 1, 1) * w_b
        dk_ref[...] = ((y_f + y_b) * (RET_DK ** -0.5)).astype(BF)
        t_f = _sum_all(e_f * k * y_f)
        t_b = _sum_all(pos * k * y_b)
        sub = lax.broadcasted_iota(jnp.int32, (8, 128), 0)
        dlg_ref[...] = jnp.where(sub == 0, t_f, jnp.where(sub == 1, t_b, 0.0))

    def v_body(lg_ref, dpx_hbm, k_ref, ds_ref, dv_ref):
        _, _, w_f, w_b = weights(lg_ref, pl.program_id(1))
        k = k_ref[...].astype(F32) * (RET_DK ** -0.5)
        dv_ref[...] = (_dot(k * w_f, ds_ref[0]) + _dot(k * w_b, ds_ref[1])).astype(BF)

    dpx, dlg = pl.pallas_call(
        k_body, name="ctx_state_bwd_k", grid=(n_samp, RET_HEADS), input_output_aliases={1: 0},
        in_specs=[SMEM, ANY, kspec, vspec, sspec],
        out_specs=(kspec, pl.BlockSpec((None, None, 8, 128), lambda b, h: (b, h, 0, 0))),
        out_shape=(SDS(dpx.shape, dpx.dtype), SDS((n_samp, RET_HEADS, 8, 128), F32)),
        compiler_params=_cp(("parallel", "parallel")))(lg, dpx, px, px, dstates)
    dpx = pl.pallas_call(
        v_body, name="ctx_state_bwd_v", grid=(n_samp, RET_HEADS), input_output_aliases={1: 0},
        in_specs=[SMEM, ANY, kspec, sspec], out_specs=vspec, out_shape=SDS(dpx.shape, dpx.dtype),
        compiler_params=_cp(("parallel", "parallel")))(lg, dpx, px, dstates)
    return dpx, dlg


def _zero_ctx_tail(dpx, t_lat):
    wb = 512
    n_ctx = (dpx.shape[0] - t_lat) // TM

    def body(dpx_hbm, o_ref):
        o_ref[...] = jnp.zeros_like(o_ref)

    return pl.pallas_call(
        body, name="zero_ctx_tail", grid=(n_ctx, (IN_COLS - KV_COLS) // wb), input_output_aliases={0: 0},
        in_specs=[ANY], out_specs=pl.BlockSpec((TM, wb), lambda i, j: (t_lat // TM + i, KV_COLS // wb + j)),
        out_shape=SDS(dpx.shape, dpx.dtype),
        compiler_params=_cp(("parallel", "parallel")))(dpx)


def _ret_specs(row_f, row_b):
    c = RET_CHUNK
    wq = RET_HEADS * RET_DK // 2
    wv = RET_HEADS * RET_DV // 2
    specs = []
    for row in (row_f, row_b):
        specs += [pl.BlockSpec((c, wq), lambda b, n, row=row: (row(b, n), C_RQ // wq)),
                  pl.BlockSpec((c, wq), lambda b, n, row=row: (row(b, n), C_RQ // wq + 1)),
                  pl.BlockSpec((c, 2 * wq), lambda b, n, row=row: (row(b, n), C_RK // (2 * wq))),
                  pl.BlockSpec((c, wv), lambda b, n, row=row: (row(b, n), C_RV // wv)),
                  pl.BlockSpec((c, wv), lambda b, n, row=row: (row(b, n), C_RV // wv + 1))]
    return specs


def _ret_head(refs, h):
    q0, q1, k_ref, v0, v1 = refs
    hh = h % 2
    q = (q0, q1)[h // 2][:, hh * RET_DK:(hh + 1) * RET_DK].astype(F32)
    k = k_ref[:, h * RET_DK:(h + 1) * RET_DK].astype(F32) * (RET_DK ** -0.5)
    v = (v0, v1)[h // 2][:, hh * RET_DV:(hh + 1) * RET_DV]
    return q, k, v


def _ret_fwd(px, states0, lg, n_samp, seq, comm=None):
    c = RET_CHUNK
    nc = seq // c
    t_lat = n_samp * seq
    wo = RET_HEADS * RET_DV

    def row_f(b, n):
        return b * nc + n

    def row_b(b, n):
        return b * nc + nc - 1 - n

    def body(lg_ref, *refs):
        ins, (s0_ref, of_ref, ob_ref, st_ref, s_s) = refs[:10], refs[10:]

        @pl.when(pl.program_id(1) == 0)
        def _():
            s_s[...] = s0_ref[...]

        for d, o_ref in ((0, of_ref), (1, ob_ref)):
            for h in range(RET_HEADS):
                _, _, mask, qd, kd, gc = _decays(lg_ref[d, h], d)
                q, k, v = _ret_head(ins[5 * d:5 * d + 5], h)
                s = s_s[d, h]
                st_ref[h, d] = s.astype(BF)
                sc = _dot(q, k, 1, 1) * mask
                o_ref[:, h * RET_DV:(h + 1) * RET_DV] = (_dot(sc, v) + _dot(q * qd, s)).astype(BF)
                s_s[d, h] = s * gc + _dot(k * kd, v, 0, 0)

    return _call(
        body, [lg] + [px] * 10 + [states0], comm, name="ret_fwd", grid=(n_samp, nc),
        in_specs=[SMEM] + _ret_specs(row_f, row_b) + [
            pl.BlockSpec((None, 2, RET_HEADS, RET_DK, RET_DV), lambda b, n: (b, 0, 0, 0, 0))],
        out_specs=(pl.BlockSpec((c, wo), lambda b, n: (row_f(b, n), 0)),
                   pl.BlockSpec((c, wo), lambda b, n: (row_b(b, n), 0)),
                   pl.BlockSpec((None, RET_HEADS, 2, None, RET_DK, RET_DV), lambda b, n: (b, 0, 0, n, 0, 0))),
        out_shape=(SDS((t_lat, wo), BF), SDS((t_lat, wo), BF),
                   SDS((n_samp, RET_HEADS, 2, nc, RET_DK, RET_DV), BF)),
        scratch_shapes=[pltpu.VMEM((2, RET_HEADS, RET_DK, RET_DV), F32)],
        compiler_params=_cp(("arbitrary", "arbitrary"), 48))


def _ret_bwd(dpx, px, do, saved, lg, n_samp, seq, comm=None):
    c = RET_CHUNK
    nc = seq // c
    assert nc % 2 == 0
    wq, wo = RET_HEADS * RET_DK, RET_HEADS * RET_DV

    def row_f(b, n):
        return b * nc + nc - 1 - n

    def row_b(b, n):
        return b * nc + n

    def body(lg_ref, *refs):
        ins = refs[:10]
        (dof_ref, dob_ref, st_ref, dpx_in, dpx_hbm, ds0_ref, dlg_ref,
         ds_s, acc_s, tq_s, tk_s, tv_s, sq_s, sk_s, sv_s, sems) = refs[10:]
        b, n = pl.program_id(0), pl.program_id(1)
        second = n >= nc // 2
        chunks = (nc - 1 - n, n)

        def parked(ch):
            return pl.ds(pl.multiple_of(ch * c, c), c)

        def flush():
            cps = []
            for d, ch in enumerate(chunks):
                rows = pl.ds(pl.multiple_of((b * nc + ch) * c, c), c)
                cps += [pltpu.make_async_copy(sq_s.at[parked(ch), :], dpx_hbm.at[rows, pl.ds(C_RQ, wq)], sems.at[3 * d]),
                        pltpu.make_async_copy(sk_s.at[parked(ch), :], dpx_hbm.at[rows, pl.ds(C_RK, wq)],
                                              sems.at[3 * d + 1]),
                        pltpu.make_async_copy(sv_s.at[parked(ch), :], dpx_hbm.at[rows, pl.ds(C_RV, wo)],
                                              sems.at[3 * d + 2])]
            return cps

        @pl.when(jnp.logical_or(n > nc // 2, jnp.logical_and(n == 0, b > 0)))
        def _():
            for cp in flush():
                cp.wait()

        @pl.when(n == 0)
        def _():
            ds_s[...] = jnp.zeros_like(ds_s)
            acc_s[...] = jnp.zeros_like(acc_s)

        for d, do_ref in enumerate((dof_ref, dob_ref)):
            for h in range(RET_HEADS):
                a_i, rel, mask, qd, kd, gc = _decays(lg_ref[d, h], d)
                q, k, v = _ret_head(ins[5 * d:5 * d + 5], h)
                qb, kb, vb = q.astype(BF), k.astype(BF), v.astype(BF)
                dob = do_ref[:, h * RET_DV:(h + 1) * RET_DV].astype(BF)
                sb = st_ref[h, d]
                ds = ds_s[d, h]
                dsb = ds.astype(BF)
                raw = _dot(qb, kb, 1, 1)
                sc = raw * mask
                dsc = _dot(dob, vb, 1, 1) * mask
                dscb = dsc.astype(BF)
                x = _dot(dob, sb, 1, 1)
                y = _dot(vb, dsb, 1, 1)
                qq = q * qd
                kk = k * kd
                tq_s[d, :, h * RET_DK:(h + 1) * RET_DK] = (_dot(dscb, kb) + x * qd).astype(BF)
                tk_s[d, :, h * RET_DK:(h + 1) * RET_DK] = (_dot(dscb, qb, 0, 0) + y * kd).astype(BF)
                tv_s[d, :, h * RET_DV:(h + 1) * RET_DV] = (_dot(sc, dob, 0, 0) + _dot(kk, dsb)).astype(BF)
                t = (_sum_all(dsc * raw * rel) + _sum_all((a_i + 1.0) * qq * x)
                     + _sum_all((c - 1.0 - a_i) * kk * y) + c * gc * _sum_all(ds * sb.astype(F32)))
                acc_s[4 * d + h:4 * d + h + 1, :] += t
                ds_s[d, h] = ds * gc + _dot(qq, dob, 0, 0)

        @pl.when(jnp.logical_not(second))
        def _():
            for d, ch in enumerate(chunks):
                sq_s[parked(ch), :] = tq_s[d]
                sk_s[parked(ch), :] = tk_s[d]
                sv_s[parked(ch), :] = tv_s[d]

        @pl.when(second)
        def _():
            for d, ch in enumerate(chunks):
                sq_s[parked(ch), :] = (sq_s[parked(ch), :].astype(F32) + tq_s[d].astype(F32)).astype(BF)
                sk_s[parked(ch), :] = ((sk_s[parked(ch), :].astype(F32) + tk_s[d].astype(F32))
                                       * (RET_DK ** -0.5)).astype(BF)
                sv_s[parked(ch), :] = (sv_s[parked(ch), :].astype(F32) + tv_s[d].astype(F32)).astype(BF)
            for cp in flush():
                cp.start()

        @pl.when(n == nc - 1)
        def _():
            ds0_ref[...] = ds_s[...]
            dlg_ref[...] = acc_s[...]

        @pl.when(jnp.logical_and(b == n_samp - 1, n == nc - 1))
        def _():
            for cp in flush():
                cp.wait()

    do_spec_f = pl.BlockSpec((c, wo), lambda b, n: (row_f(b, n), 0))
    do_spec_b = pl.BlockSpec((c, wo), lambda b, n: (row_b(b, n), 0))
    return _call(
        body, [lg] + [px] * 10 + [do, do, saved, dpx], comm, name="ret_bwd", grid=(n_samp, nc), aliases={14: 0},
        in_specs=[SMEM] + _ret_specs(row_f, row_b) + [
            do_spec_f, do_spec_b,
            pl.BlockSpec((None, RET_HEADS, 2, None, RET_DK, RET_DV), lambda b, n: (b, 0, 0, nc - 1 - n, 0, 0)),
            ANY],
        out_specs=(ANY,
                   pl.BlockSpec((None, 2, RET_HEADS, RET_DK, RET_DV), lambda b, n: (b, 0, 0, 0, 0)),
                   pl.BlockSpec((None, 8, 128), lambda b, n: (b, 0, 0))),
        out_shape=(SDS(dpx.shape, dpx.dtype),
                   SDS((n_samp, 2, RET_HEADS, RET_DK, RET_DV), F32), SDS((n_samp, 8, 128), F32)),
        scratch_shapes=[pltpu.VMEM((2, RET_HEADS, RET_DK, RET_DV), F32), pltpu.VMEM((8, 128), F32),
                        pltpu.VMEM((2, c, wq), BF), pltpu.VMEM((2, c, wq), BF), pltpu.VMEM((2, c, wo), BF),
                        pltpu.VMEM((seq, wq), BF), pltpu.VMEM((seq, wq), BF), pltpu.VMEM((seq, wo), BF),
                        pltpu.SemaphoreType.DMA((6,))],
        compiler_params=_cp(("arbitrary", "arbitrary"), 60))


def _combine_into(dpx, a, b, col0, scale):
    t_lat, width = a.shape
    wb = 512
    assert col0 % wb == 0 and width % wb == 0

    def body(dpx_hbm, a_ref, b_ref, o_ref):
        o_ref[...] = ((a_ref[...].astype(F32) + b_ref[...].astype(F32)) * scale).astype(BF)

    src = pl.BlockSpec((TM, wb), lambda i, j: (i, j))
    return pl.pallas_call(
        body, name="combine_into", grid=(t_lat // TM, width // wb), input_output_aliases={0: 0},
        in_specs=[ANY, src, src], out_specs=pl.BlockSpec((TM, wb), lambda i, j: (i, col0 // wb + j)),
        out_shape=SDS(dpx.shape, dpx.dtype),
        compiler_params=_cp(("parallel", "parallel")))(dpx, a, b)


def _retnorm_fwd(o_f, o_b, px):
    t_lat = o_f.shape[0]

    def body(of_ref, ob_ref, g_ref, y_ref):
        o = of_ref[...].astype(F32) + ob_ref[...].astype(F32)
        g = g_ref[...].astype(F32)
        y_ref[...] = (o * _rms(o) * (g * _sigmoid(g))).astype(BF)

    so = pl.BlockSpec((TM, RET_DV), lambda i, h: (i, h))
    return pl.pallas_call(
        body, name="retnorm_fwd", grid=(t_lat // TM, RET_HEADS),
        in_specs=[so, so, pl.BlockSpec((TM, RET_DV), lambda i, h: (i, C_RG // RET_DV + h))],
        out_specs=so,
        out_shape=SDS((t_lat, RET_HEADS * RET_DV), BF),
        compiler_params=_cp(("parallel", "parallel")))(o_f, o_b, px)


def _retnorm_bwd(dpx, dy, o_f, o_b, px):
    t_lat = o_f.shape[0]

    def body(dpx_hbm, dy_ref, of_ref, ob_ref, g_ref, do_ref, dg_ref):
        o = of_ref[...].astype(F32) + ob_ref[...].astype(F32)
        r = _rms(o)
        on = o * r
        g = g_ref[...].astype(F32)
        sg = _sigmoid(g)
        dy_ = dy_ref[...].astype(F32)
        dg_ref[...] = (dy_ * on * (sg * (1.0 + g * (1.0 - sg)))).astype(BF)
        do_ref[...] = _rms_bwd(dy_ * (g * sg), on, r).astype(BF)

    so = pl.BlockSpec((TM, RET_DV), lambda i, h: (i, h))
    gcol = pl.BlockSpec((TM, RET_DV), lambda i, h: (i, C_RG // RET_DV + h))
    return pl.pallas_call(
        body, name="retnorm_bwd", grid=(t_lat // TM, RET_HEADS), input_output_aliases={0: 1},
        in_specs=[ANY, so, so, so, gcol],
        out_specs=(so, gcol),
        out_shape=(SDS((t_lat, RET_HEADS * RET_DV), BF), SDS(dpx.shape, dpx.dtype)),
        compiler_params=_cp(("parallel", "parallel")))(dpx, dy, o_f, o_b, px)


def _norm_rope(x, w, cos, sin):
    xn = x * _rms(x) * w
    return xn * cos + _swap_pairs(xn) * sin


def _norm_rope_bwd(dy, x, w, cos, sin):
    dxn = dy * cos + _swap_pairs(dy * sin)
    r = _rms(x)
    xh = x * r
    return _rms_bwd(dxn * w, xh, r), jnp.sum(dxn * xh, axis=0, keepdims=True)


def _att_prep_q(px, cos_all, sin_all, qnw, t_lat):
    hd = ATT_HEAD_DIM
    wblk = ATT_REP * hd

    def body(x_ref, cos_ref, sin_ref, w_ref, o_ref):
        for r in range(ATT_REP):
            cols = slice(r * hd, (r + 1) * hd)
            qr = _norm_rope(x_ref[:, cols].astype(F32), w_ref[...], cos_ref[...], sin_ref[...])
            o_ref[:, cols] = (qr * (hd ** -0.5)).astype(BF)

    return pl.pallas_call(
        body, name="att_prep_q", grid=(t_lat // TM, ATT_KV_HEADS),
        in_specs=[pl.BlockSpec((TM, wblk), lambda i, g: (i, C_AQ // wblk + g)),
                  pl.BlockSpec((TM, hd), lambda i, g: (i, 0)),
                  pl.BlockSpec((TM, hd), lambda i, g: (i, 0)),
                  pl.BlockSpec((1, hd), lambda i, g: (0, 0))],
        out_specs=pl.BlockSpec((TM, wblk), lambda i, g: (i, g)),
        out_shape=SDS((t_lat, ATT_HEADS * hd), BF),
        compiler_params=_cp(("parallel", "parallel")))(px, cos_all, sin_all, qnw)


def _att_prep_kv(px, cos_all, sin_all, knw):
    rows = px.shape[0]
    hd = ATT_HEAD_DIM
    kvw = ATT_KV_HEADS * hd

    def body(x_ref, cos_ref, sin_ref, w_ref, k_ref, v_ref):
        for g in range(ATT_KV_HEADS):
            cols = slice(g * hd, (g + 1) * hd)
            k_ref[:, cols] = _norm_rope(x_ref[:, cols].astype(F32), w_ref[...], cos_ref[...],
                                        sin_ref[...]).astype(BF)
        v_ref[...] = x_ref[:, kvw:].astype(BF)

    return pl.pallas_call(
        body, name="att_prep_kv", grid=(rows // TM,),
        in_specs=[pl.BlockSpec((TM, 2 * kvw), lambda i: (i, C_AK // (2 * kvw))),
                  pl.BlockSpec((TM, hd), lambda i: (i, 0)),
                  pl.BlockSpec((TM, hd), lambda i: (i, 0)),
                  pl.BlockSpec((1, hd), lambda i: (0, 0))],
        out_specs=(pl.BlockSpec((TM, kvw), lambda i: (i, 0)), pl.BlockSpec((TM, kvw), lambda i: (i, 0))),
        out_shape=(SDS((rows, kvw), BF), SDS((rows, kvw), BF)),
        compiler_params=_cp(("parallel",)))(px, cos_all, sin_all, knw)


def _att_kv_bwd(dpx, dkl, dkc, dvl, dvc, px, cos_all, sin_all, knw):
    rows = px.shape[0]
    hd = ATT_HEAD_DIM
    kvw = ATT_KV_HEADS * hd
    n_lat = dkl.shape[0] // TM
    assert dkc.shape[0] == TM

    def body(dpx_hbm, dkl_ref, dkc_ref, dvl_ref, dvc_ref, x_ref, cos_ref, sin_ref, w_ref, o_ref, gw_ref):
        i = pl.program_id(0)

        @pl.when(i == 0)
        def _():
            gw_ref[...] = jnp.zeros_like(gw_ref)

        is_lat = i < n_lat
        dk = jnp.where(is_lat, dkl_ref[...], dkc_ref[...])
        dv = jnp.where(is_lat, dvl_ref[...], dvc_ref[...])
        for g in range(ATT_KV_HEADS):
            cols = slice(g * hd, (g + 1) * hd)
            dx, gw = _norm_rope_bwd(dk[:, cols], x_ref[:, cols].astype(F32), w_ref[...], cos_ref[...], sin_ref[...])
            o_ref[:, cols] = dx.astype(BF)
            gw_ref[...] += gw
        o_ref[:, kvw:] = dv.astype(BF)

    lat = pl.BlockSpec((TM, kvw), lambda i: (jnp.minimum(i, n_lat - 1), 0))
    ctx = pl.BlockSpec((TM, kvw), lambda i: (0, 0))
    kvcol = pl.BlockSpec((TM, 2 * kvw), lambda i: (i, C_AK // (2 * kvw)))
    return pl.pallas_call(
        body, name="att_kv_bwd", grid=(rows // TM,), input_output_aliases={0: 0},
        in_specs=[ANY, lat, ctx, lat, ctx, kvcol,
                  pl.BlockSpec((TM, hd), lambda i: (i, 0)),
                  pl.BlockSpec((TM, hd), lambda i: (i, 0)),
                  pl.BlockSpec((1, hd), lambda i: (0, 0))],
        out_specs=(kvcol, pl.BlockSpec((1, hd), lambda i: (0, 0))),
        out_shape=(SDS(dpx.shape, dpx.dtype), SDS((1, hd), F32)),
        compiler_params=_cp(("arbitrary",)))(dpx, dkl, dkc, dvl, dvc, px, cos_all, sin_all, knw)


def _stack_heads(ref_or_val):
    hd = ATT_HEAD_DIM
    return jnp.concatenate([ref_or_val[:, r * hd:(r + 1) * hd] for r in range(ATT_REP)], axis=0)


def _att_scores(q, kl, kc):
    sl = _dot(q, kl, 1, 1)
    sc = _dot(q, kc, 1, 1)
    m = jnp.maximum(jnp.max(sl, axis=-1, keepdims=True), jnp.max(sc, axis=-1, keepdims=True))
    el = jnp.exp(sl - m)
    ec = jnp.exp(sc - m)
    denom = jnp.sum(el, axis=-1, keepdims=True) + jnp.sum(ec, axis=-1, keepdims=True)
    return el, ec, denom, m


def _att_fwd(qn, kn, vn, n_samp, seq, lc):
    hd = ATT_HEAD_DIM
    tq = ATT_TQ
    nq = seq // tq
    wblk = ATT_REP * hd
    cb = n_samp * seq // lc
    t_lat = n_samp * seq

    def body(q_ref, kl_ref, kc_ref, vl_ref, vc_ref, o_ref, lse_ref):
        lane = lax.broadcasted_iota(jnp.int32, (tq, hd), 1)
        lse = jnp.zeros((tq, hd), F32)
        for r in range(ATT_REP):
            cols = slice(r * hd, (r + 1) * hd)
            el, ec, denom, m = _att_scores(q_ref[:, cols], kl_ref[...], kc_ref[...])
            o_ref[:, cols] = ((_dot(el, vl_ref[...]) + _dot(ec, vc_ref[...])) / denom).astype(BF)
            lse = jnp.where(lane == r, m + jnp.log(denom), lse)
        lse_ref[...] = lse

    return pl.pallas_call(
        body, name="att_fwd", grid=(n_samp, ATT_KV_HEADS, nq),
        in_specs=[pl.BlockSpec((tq, wblk), lambda b, g, i: (b * nq + i, g)),
                  pl.BlockSpec((seq, hd), lambda b, g, i: (b, g)),
                  pl.BlockSpec((lc, hd), lambda b, g, i: (cb + b, g)),
                  pl.BlockSpec((seq, hd), lambda b, g, i: (b, g)),
                  pl.BlockSpec((lc, hd), lambda b, g, i: (cb + b, g))],
        out_specs=(pl.BlockSpec((tq, wblk), lambda b, g, i: (b * nq + i, g)),
                   pl.BlockSpec((tq, hd), lambda b, g, i: (b * nq + i, g))),
        out_shape=(SDS((t_lat, ATT_HEADS * hd), BF), SDS((t_lat, ATT_KV_HEADS * hd), F32)),
        compiler_params=_cp(("parallel", "parallel", "parallel"), 48))(qn, kn, kn, vn, vn)


def _att_gate_bwd(dpx, dy_att, o_att, px):
    t_lat = dy_att.shape[0]
    wblk = ATT_REP * ATT_HEAD_DIM

    def body(dpx_hbm, dy_ref, o_ref, g_ref, out_ref):
        g = g_ref[...].astype(F32)
        sg = _sigmoid(g)
        out_ref[...] = (dy_ref[...].astype(F32) * o_ref[...].astype(F32) * (sg * (1.0 + g * (1.0 - sg)))).astype(BF)

    blk = pl.BlockSpec((TM, wblk), lambda i, j: (i, j))
    gcol = pl.BlockSpec((TM, wblk), lambda i, j: (i, C_AG // wblk + j))
    return pl.pallas_call(
        body, name="att_gate_bwd", grid=(t_lat // TM, ATT_KV_HEADS),
        in_specs=[ANY, blk, blk, gcol], out_specs=gcol, out_shape=SDS(dpx.shape, dpx.dtype),
        input_output_aliases={0: 0},
        compiler_params=_cp(("parallel", "parallel")))(dpx, dy_att, o_att, px)


def _att_bwd(dpx, qn, kn, vn, px, o_att, lse, do_att, cos_all, sin_all, qnw, n_samp, seq, lc, comm=None):
    hd = ATT_HEAD_DIM
    tq = ATT_TQ
    nq = seq // tq
    wblk = ATT_REP * hd
    cb = n_samp * seq // lc
    t_lat = n_samp * seq
    kvw = ATT_KV_HEADS * hd
    scale = hd ** -0.5

    def body(dpx_hbm, q_ref, kl_ref, kc_ref, vl_ref, vc_ref, o_ref, do_ref, x_ref, cos_ref, sin_ref, w_ref,
             lse_ref, dq_ref, dkl_ref, dkc_ref, dvl_ref, dvc_ref, gw_ref, akl, akc, avl, avc, aw):
        i = pl.program_id(2)

        @pl.when(i == 0)
        def _():
            akl[...] = jnp.zeros_like(akl)
            akc[...] = jnp.zeros_like(akc)
            avl[...] = jnp.zeros_like(avl)
            avc[...] = jnp.zeros_like(avc)
            aw[...] = jnp.zeros_like(aw)

        dobs, pls, pcs, dsls, dscs = [], [], [], [], []
        for r in range(ATT_REP):
            cols = slice(r * hd, (r + 1) * hd)
            dob = do_ref[:, cols]
            delta = jnp.sum(dob.astype(F32) * o_ref[:, cols].astype(F32), axis=-1, keepdims=True)
            lse = lse_ref[:, r:r + 1]
            p_l = jnp.exp(_dot(q_ref[:, cols], kl_ref[...], 1, 1) - lse).astype(BF)
            p_c = jnp.exp(_dot(q_ref[:, cols], kc_ref[...], 1, 1) - lse).astype(BF)
            ds_l = (p_l * (_dot(dob, vl_ref[...], 1, 1) - delta)).astype(BF)
            ds_c = (p_c * (_dot(dob, vc_ref[...], 1, 1) - delta)).astype(BF)
            dq = (_dot(ds_l, kl_ref[...]) + _dot(ds_c, kc_ref[...])) * scale
            dx, gw = _norm_rope_bwd(dq, x_ref[:, cols].astype(F32), w_ref[...], cos_ref[...], sin_ref[...])
            dq_ref[:, cols] = dx.astype(BF)
            aw[...] += gw
            dobs.append(dob)
            pls.append(p_l)
            pcs.append(p_c)
            dsls.append(ds_l)
            dscs.append(ds_c)
        do4 = jnp.concatenate(dobs, axis=0)
        q4 = _stack_heads(q_ref)
        avl[...] += _dot(jnp.concatenate(pls, axis=0), do4, 0, 0)
        avc[...] += _dot(jnp.concatenate(pcs, axis=0), do4, 0, 0)
        akl[...] += _dot(jnp.concatenate(dsls, axis=0), q4, 0, 0)
        akc[...] += _dot(jnp.concatenate(dscs, axis=0), q4, 0, 0)

        @pl.when(i == nq - 1)
        def _():
            dkl_ref[...] = akl[...]
            dkc_ref[...] = akc[...]
            dvl_ref[...] = avl[...]
            dvc_ref[...] = avc[...]
            gw_ref[...] = aw[...]

    return _call(
        body, [dpx, qn, kn, kn, vn, vn, o_att, do_att, px, cos_all, sin_all, qnw, lse], comm,
        name="att_bwd", grid=(n_samp, ATT_KV_HEADS, nq), aliases={0: 0},
        in_specs=[ANY,
                  pl.BlockSpec((tq, wblk), lambda b, g, i: (b * nq + i, g)),
                  pl.BlockSpec((seq, hd), lambda b, g, i: (b, g)),
                  pl.BlockSpec((lc, hd), lambda b, g, i: (cb + b, g)),
                  pl.BlockSpec((seq, hd), lambda b, g, i: (b, g)),
                  pl.BlockSpec((lc, hd), lambda b, g, i: (cb + b, g)),
                  pl.BlockSpec((tq, wblk), lambda b, g, i: (b * nq + i, g)),
                  pl.BlockSpec((tq, wblk), lambda b, g, i: (b * nq + i, g)),
                  pl.BlockSpec((tq, wblk), lambda b, g, i: (b * nq + i, C_AQ // wblk + g)),
                  pl.BlockSpec((tq, hd), lambda b, g, i: (b * nq + i, 0)),
                  pl.BlockSpec((tq, hd), lambda b, g, i: (b * nq + i, 0)),
                  pl.BlockSpec((1, hd), lambda b, g, i: (0, 0)),
                  pl.BlockSpec((tq, hd), lambda b, g, i: (b * nq + i, g))],
        out_specs=(pl.BlockSpec((tq, wblk), lambda b, g, i: (b * nq + i, C_AQ // wblk + g)),
                   pl.BlockSpec((seq, hd), lambda b, g, i: (b, g)),
                   pl.BlockSpec((lc, hd), lambda b, g, i: (b, g)),
                   pl.BlockSpec((seq, hd), lambda b, g, i: (b, g)),
                   pl.BlockSpec((lc, hd), lambda b, g, i: (b, g)),
                   pl.BlockSpec((None, None, 1, hd), lambda b, g, i: (b, g, 0, 0))),
        out_shape=(SDS(dpx.shape, dpx.dtype),
                   SDS((t_lat, kvw), F32), SDS((n_samp * lc, kvw), F32),
                   SDS((t_lat, kvw), F32), SDS((n_samp * lc, kvw), F32),
                   SDS((n_samp, ATT_KV_HEADS, 1, hd), F32)),
        scratch_shapes=[pltpu.VMEM((seq, hd), F32), pltpu.VMEM((lc, hd), F32),
                        pltpu.VMEM((seq, hd), F32), pltpu.VMEM((lc, hd), F32), pltpu.VMEM((1, hd), F32)],
        compiler_params=_cp(("arbitrary", "arbitrary", "arbitrary"), 56))


def _merge(x_lat, target, o_f, o_b, o_att, px, gate3, w_o_ret, w_o_att, w_out, tiles_per_sample):
    t_lat = x_lat.shape[0]
    tm = 256
    n_t = t_lat // tm
    per = tiles_per_sample * (TM // tm)
    d = D_MODEL
    rv = RET_HEADS * RET_DV
    n_samp = gate3.shape[0] - 1

    half = d // 2
    n_px = 10

    def body(x_ref, t_ref, of_ref, ob_ref, oa_ref, *rest):
        pxs, rest = rest[:n_px], rest[n_px:]
        (gt_ref, wor_ref, woa_ref, wout_ref,
         gx_ref, dor_ref, doa_ref, dpx_hbm, loss_ref, dgt_ref, gwor_hbm, gwoa_hbm, gwout_hbm,
         aor, aoa, aout, drg_ref, dtail_ref, sems) = rest
        i = pl.program_id(0)

        def copies(step):
            rows = pl.ds(pl.multiple_of(step * tm, tm), tm)
            return (pltpu.make_async_copy(drg_ref, dpx_hbm.at[rows, pl.ds(C_RG, rv)], sems.at[0]),
                    pltpu.make_async_copy(dtail_ref, dpx_hbm.at[rows, pl.ds(C_AG, 3 * d)], sems.at[1]))

        @pl.when(i == 0)
        def _():
            aor[...] = jnp.zeros_like(aor)
            aoa[...] = jnp.zeros_like(aoa)
            aout[...] = jnp.zeros_like(aout)
            loss_ref[...] = jnp.zeros_like(loss_ref)

        @pl.when(i % per == 0)
        def _():
            dgt_ref[...] = jnp.zeros_like(dgt_ref)

        def cat(refs):
            return jnp.concatenate([r[...] for r in refs], axis=1).astype(F32)

        def ret_head(h):
            cols = slice(h * RET_DV, (h + 1) * RET_DV)
            o = of_ref[:, cols].astype(F32) + ob_ref[:, cols].astype(F32)
            r = _rms(o)
            g = pxs[h][...].astype(F32)
            return o * r, r, g, _sigmoid(g)

        def att_half(k):
            o = oa_ref[:, k * half:(k + 1) * half].astype(F32)
            g = pxs[4 + k][...].astype(F32)
            return o, g, _sigmoid(g)

        yrs = []
        for h in range(RET_HEADS):
            on, _, g, sg = ret_head(h)
            yrs.append((on * (g * sg)).astype(BF))
        yr = jnp.concatenate(yrs, axis=1)
        yas = []
        for k in range(2):
            o, g, sg = att_half(k)
            yas.append((o * (g * sg)).astype(BF))
        ya = jnp.concatenate(yas, axis=1)

        a = jnp.dot(yr, wor_ref[...], preferred_element_type=F32)
        b = jnp.dot(ya, woa_ref[...], preferred_element_type=F32)
        sr = _sigmoid(cat(pxs[6:8]))
        sa = _sigmoid(cat(pxs[8:10]))
        yb = (sr * a + sa * b).astype(BF)
        out = jnp.dot(yb, wout_ref[...], preferred_element_type=F32)
        gate = gt_ref[...]
        err = x_ref[...] + gate * out - t_ref[...]
        loss_ref[...] += 0.5 * _sum_all(err * err) * (1.0 / d)
        dy_tok = err * (1.0 / d)
        gx_ref[...] = dy_tok
        dgt_ref[...] += jnp.sum(dy_tok * out, axis=0, keepdims=True)
        dout = (dy_tok * gate).astype(BF)
        aout[...] += _dot(yb, dout, 0, 0)
        dyy = _dot(dout, wout_ref[...], 1, 1)
        da = (dyy * sr).astype(BF)
        db = (dyy * sa).astype(BF)
        aor[...] += _dot(yr, da, 0, 0)
        aoa[...] += _dot(ya, db, 0, 0)
        dyr = _dot(da, wor_ref[...], 1, 1)
        dya = _dot(db, woa_ref[...], 1, 1)

        @pl.when(i > 0)
        def _():
            for cp in copies(i - 1):
                cp.wait()

        dtail_ref[:, d:2 * d] = (dyy * a * (sr * (1.0 - sr))).astype(BF)
        dtail_ref[:, 2 * d:] = (dyy * b * (sa * (1.0 - sa))).astype(BF)
        for h in range(RET_HEADS):
            cols = slice(h * RET_DV, (h + 1) * RET_DV)
            on, r, g, sg = ret_head(h)
            dy = dyr[:, cols]
            drg_ref[:, cols] = (dy * on * (sg * (1.0 + g * (1.0 - sg)))).astype(BF)
            dor_ref[:, cols] = _rms_bwd(dy * (g * sg), on, r).astype(BF)
        for k in range(2):
            cols = slice(k * half, (k + 1) * half)
            o, g, sg = att_half(k)
            dy = dya[:, cols]
            dtail_ref[:, cols] = (dy * o * (sg * (1.0 + g * (1.0 - sg)))).astype(BF)
            doa_ref[:, cols] = (dy * (g * sg)).astype(BF)
        for cp in copies(i):
            cp.start()

        @pl.when(i == n_t - 1)
        def _():
            for cp in copies(i):
                cp.wait()
            pltpu.sync_copy(aor, gwor_hbm)
            pltpu.sync_copy(aoa, gwoa_hbm)
            pltpu.sync_copy(aout, gwout_hbm)

    def px_blk(col):
        return pl.BlockSpec((tm, half), lambda i: (i, col // half))

    def resident(shape):
        return pl.BlockSpec(shape, lambda i: (0, 0), pipeline_mode=pl.Buffered(1))

    px_cols = ([C_RG + k * half for k in range(4)] + [C_AG, C_AG + half]
               + [C_MR, C_MR + half, C_MA, C_MA + half])
    return pl.pallas_call(
        body, name="merge", grid=(n_t,),
        in_specs=[pl.BlockSpec((tm, d), lambda i: (i, 0)),
                  pl.BlockSpec((tm, d), lambda i: (i, 0)),
                  pl.BlockSpec((tm, rv), lambda i: (i, 0)),
                  pl.BlockSpec((tm, rv), lambda i: (i, 0)),
                  pl.BlockSpec((tm, d), lambda i: (i, 0))]
        + [px_blk(col) for col in px_cols]
        + [pl.BlockSpec((None, 1, d), lambda i: (i // per, 0, 0)),
           resident((rv, d)), resident((d, d)), resident((d, d))],
        out_specs=(pl.BlockSpec((tm, d), lambda i: (i, 0)),
                   pl.BlockSpec((tm, rv), lambda i: (i, 0)),
                   pl.BlockSpec((tm, d), lambda i: (i, 0)),
                   ANY,
                   pl.BlockSpec((8, 128), lambda i: (0, 0)),
                   pl.BlockSpec((None, 1, d), lambda i: (i // per, 0, 0)),
                   ANY, ANY, ANY),
        out_shape=(SDS((t_lat, d), F32), SDS((t_lat, rv), BF), SDS((t_lat, d), BF),
                   SDS((px.shape[0], IN_COLS), BF),
                   SDS((8, 128), F32), SDS((n_samp, 1, d), F32),
                   SDS((rv, d), F32), SDS((d, d), F32), SDS((d, d), F32)),
        scratch_shapes=[pltpu.VMEM((rv, d), F32), pltpu.VMEM((d, d), F32), pltpu.VMEM((d, d), F32),
                        pltpu.VMEM((tm, rv), BF), pltpu.VMEM((tm, 3 * d), BF), pltpu.SemaphoreType.DMA((2,))],
        compiler_params=_cp(("arbitrary",), 56))(
            x_lat, target, o_f, o_b, o_att, *([px] * n_px), gate3, w_o_ret, w_o_att, w_out)


def _place():
    x, y, c = lax.axis_index("x"), lax.axis_index("y"), lax.axis_index("c")
    chips = [(1 - x, y), (x, 1 - y), (1 - x, 1 - y)]
    return x, y, c, chips


def _remote(src, dst, send_sem, recv_sem, to):
    return pltpu.make_async_remote_copy(src_ref=src, dst_ref=dst, send_sem=send_sem, recv_sem=recv_sem,
                                        device_id=to, device_id_type=MESH)


def _place_ids():
    x, y, c = lax.axis_index("x"), lax.axis_index("y"), lax.axis_index("c")
    me = 2 * x + y
    place = jnp.stack([x, y, c, me, me, 2 * (1 - x) + y, 2 * x + 1 - y, 2 * (1 - x) + 1 - y]).astype(jnp.int32)
    order = jnp.array([p for p, _ in GATHER_ORDER] + [k for _, k in GATHER_ORDER], jnp.int32)
    return jnp.concatenate([place, order])


def _ag_comm(bufs, rels, arg_index=None):
    n, m = len(bufs), len(rels)

    def half(ref, s, which):
        h = ref.shape[1] // 2
        return ref.at[s, pl.ds(which * h, h), :]

    def ici(ins, outs, ssem, rsem, base):
        x, y, c, chips = _place()
        sends, recvs = [], []
        for a in range(n):
            for jj, j in enumerate(rels):
                k, chip = base + a * m + jj, chips[j]
                mine, theirs = half(outs[a], 2 * x + y, c), half(outs[a], 2 * chip[0] + chip[1], c)
                sends.append(_remote(mine, mine, ssem.at[k], rsem.at[k], (*chip, c)))
                recvs.append(_remote(theirs, theirs, ssem.at[k], rsem.at[k], (*chip, c)))
        return sends, recvs

    def d2d(ins, outs, ssem, rsem, base):
        x, y, c, chips = _place()
        sends, recvs = [], []
        for a in range(n):
            for jj, j in enumerate(rels):
                k, s = base + (n + a) * m + jj, 2 * chips[j][0] + chips[j][1]
                sends.append(_remote(half(outs[a], s, c), half(outs[a], s, c), ssem.at[k], rsem.at[k], (x, y, 1 - c)))
                recvs.append(_remote(half(outs[a], s, 1 - c), half(outs[a], s, 1 - c), ssem.at[k], rsem.at[k],
                                     (x, y, 1 - c)))
        return sends, recvs

    shapes = tuple(SDS(b.shape, b.dtype) for b in bufs)
    if arg_index is not None:
        return _Comm("all_gather", (), shapes, {}, 2 * n * m, (ici, d2d), ((arg_index, 0),))
    return _Comm("all_gather", tuple(bufs), shapes, {a: a for a in range(n)}, 2 * n * m, (ici, d2d))


def _swap_comm(grads):
    n = len(grads)

    def phase(ins, outs, ssem, rsem, base):
        x, y, c, _ = _place()
        sends = []
        for a in range(n):
            h = ins[a].shape[1] // 2
            sends.append(_remote(ins[a].at[:, pl.ds((1 - c) * h, h), :], outs[a], ssem.at[base + a],
                                 rsem.at[base + a], (x, y, 1 - c)))
        return sends, sends

    return _Comm("swap_halves", tuple(grads),
                 tuple(SDS((g.shape[0], g.shape[1] // 2, g.shape[2]), g.dtype) for g in grads), {}, n, (phase,))


def _exchange_comm(parts):
    n = len(parts)

    def phase(ins, outs, ssem, rsem, base):
        x, y, c, chips = _place()
        sends = []
        for a in range(n):
            for j, chip in enumerate(chips):
                k = base + 3 * a + j
                sends.append(_remote(ins[a].at[2 * chip[0] + chip[1]], outs[a].at[j], ssem.at[k], rsem.at[k],
                                     (*chip, c)))
        return sends, sends

    return _Comm("exchange_shards", tuple(parts), tuple(SDS((3,) + p.shape[1:], p.dtype) for p in parts), {}, 3 * n,
                 (phase,))


def _join_comm(bufs, n_parts=1):
    n = len(bufs)

    def phase(ins, outs, ssem, rsem, base):
        x, y, c, _ = _place()
        sends, recvs = [], []
        for a in range(n):
            h = outs[a].shape[0] // (2 * n_parts)
            for p in range(n_parts):
                k = base + a * n_parts + p
                mine = outs[a].at[pl.ds((2 * p + c) * h, h), :]
                other = outs[a].at[pl.ds((2 * p + 1 - c) * h, h), :]
                sends.append(_remote(mine, mine, ssem.at[k], rsem.at[k], (x, y, 1 - c)))
                recvs.append(_remote(other, other, ssem.at[k], rsem.at[k], (x, y, 1 - c)))
        return sends, recvs

    return _Comm("join_halves", tuple(bufs), tuple(SDS(b.shape, b.dtype) for b in bufs), {a: a for a in range(n)},
                 n * n_parts, (phase,))


def _cast_place(w, ids):
    rows, cols = w.shape
    tr = min(rows, 256)

    def body(ids_ref, w_ref, o_ref):
        o_ref[...] = w_ref[...].astype(BF)

    return pl.pallas_call(
        body, name="cast_place",
        grid_spec=pltpu.PrefetchScalarGridSpec(
            num_scalar_prefetch=1, grid=(rows // tr,),
            in_specs=[pl.BlockSpec((tr, cols), lambda i, ids_ref: (i, 0))],
            out_specs=pl.BlockSpec((None, tr, cols), lambda i, ids_ref: (ids_ref[3], i, 0))),
        out_shape=SDS((N_SHARD, rows, cols), BF),
        compiler_params=_cp(("parallel",), 40))(ids, w)


def _all_gather_weights(bufs):
    n = len(bufs)

    def body(*refs):
        outs = refs[n:2 * n]
        send_sems, recv_sems = refs[2 * n:]
        x, y, c, chips = _place()
        sibling = (x, y, 1 - c)
        me = 2 * x + y

        def half(ref, s, which):
            h = ref.shape[1] // 2
            return ref.at[s, pl.ds(which * h, h), :]

        first = []
        for a in range(n):
            for j, chip in enumerate(chips):
                k = a * 3 + j
                win = half(outs[a], me, c)
                first.append(_remote(win, win, send_sems.at[k], recv_sems.at[k], (*chip, c)))
        for cp in first:
            cp.start()
        passed = []
        for a in range(n):
            for j, chip in enumerate(chips):
                k = a * 3 + j
                win = half(outs[a], 2 * chip[0] + chip[1], c)
                _remote(win, win, send_sems.at[k], recv_sems.at[k], (*chip, c)).wait_recv()
                fw = _remote(win, win, send_sems.at[3 * n + k], recv_sems.at[3 * n + k], sibling)
                fw.start()
                passed.append(fw)
        for a in range(n):
            for j, chip in enumerate(chips):
                k = a * 3 + j
                win = half(outs[a], 2 * chip[0] + chip[1], 1 - c)
                _remote(win, win, send_sems.at[3 * n + k], recv_sems.at[3 * n + k], sibling).wait_recv()
        for cp in first + passed:
            cp.wait_send()

    return pl.pallas_call(
        body, name="all_gather_weights",
        in_specs=[ANY] * n, out_specs=tuple([ANY] * n),
        out_shape=tuple(SDS(b.shape, b.dtype) for b in bufs),
        input_output_aliases={a: a for a in range(n)},
        scratch_shapes=[pltpu.SemaphoreType.DMA((6 * n,)), pltpu.SemaphoreType.DMA((6 * n,))],
        compiler_params=_cp(has_side_effects=True))(*bufs)


def _swap_halves(grads):
    n = len(grads)

    def body(*refs):
        ins, outs = refs[:n], refs[n:2 * n]
        send_sems, recv_sems = refs[2 * n:]
        x, y, c, _ = _place()
        sibling = (x, y, 1 - c)

        def half(ref, which):
            h = ref.shape[1] // 2
            return ref.at[:, pl.ds(which * h, h), :]

        sends = [_remote(half(ins[a], 1 - c), outs[a], send_sems.at[a], recv_sems.at[a], sibling)
                 for a in range(n)]
        for cp in sends:
            cp.start()
        for cp in sends:
            cp.wait_recv()
        for cp in sends:
            cp.wait_send()

    return pl.pallas_call(
        body, name="swap_halves",
        in_specs=[ANY] * n, out_specs=tuple([ANY] * n),
        out_shape=tuple(SDS((g.shape[0], g.shape[1] // 2, g.shape[2]), g.dtype) for g in grads),
        scratch_shapes=[pltpu.SemaphoreType.DMA((n,)), pltpu.SemaphoreType.DMA((n,))],
        compiler_params=_cp(has_side_effects=True))(*grads)


def _chip_sum(g, p, ids):
    n_s, rows, cols = g.shape
    h = rows // 2
    tr = min(h, 256)
    nb = h // tr

    def body(ids_ref, g_ref, p_ref, o_ref, o16_ref):
        t = g_ref[...] + p_ref[...]
        o_ref[...] = t
        o16_ref[...] = t.astype(BF)

    out_spec = pl.BlockSpec((None, tr, cols), lambda s, i, ids_ref: (s, i, 0))
    return pl.pallas_call(
        body, name="chip_sum",
        grid_spec=pltpu.PrefetchScalarGridSpec(
            num_scalar_prefetch=1, grid=(n_s, nb),
            in_specs=[pl.BlockSpec((None, tr, cols), lambda s, i, ids_ref: (s, ids_ref[2] * nb + i, 0)),
                      pl.BlockSpec((None, tr, cols), lambda s, i, ids_ref: (s, i, 0))],
            out_specs=(out_spec, out_spec)),
        out_shape=(SDS((n_s, h, cols), g.dtype), SDS((n_s, h, cols), BF)),
        compiler_params=_cp(("parallel", "parallel"), 40))(ids, g, p)


def _exchange_shards(parts):
    n = len(parts)

    def body(*refs):
        ins, outs = refs[:n], refs[n:2 * n]
        send_sems, recv_sems = refs[2 * n:]
        x, y, c, chips = _place()
        sends = []
        for a in range(n):
            for j, chip in enumerate(chips):
                k = a * 3 + j
                sends.append(_remote(ins[a].at[2 * chip[0] + chip[1]], outs[a].at[j],
                                     send_sems.at[k], recv_sems.at[k], (*chip, c)))
        for cp in sends:
            cp.start()
        for cp in sends:
            cp.wait_recv()
        for cp in sends:
            cp.wait_send()

    return pl.pallas_call(
        body, name="exchange_shards",
        in_specs=[ANY] * n, out_specs=tuple([ANY] * n),
        out_shape=tuple(SDS((3,) + p.shape[1:], p.dtype) for p in parts),
        scratch_shapes=[pltpu.SemaphoreType.DMA((3 * n,)), pltpu.SemaphoreType.DMA((3 * n,))],
        compiler_params=_cp(has_side_effects=True))(*parts)


def _shard_sum(t, q, ids, part=0, n_parts=1, buf=None):
    _, h, cols = t.shape
    tr = min(h, 256)
    nb = h // tr

    def body(ids_ref, t_ref, q_ref, *rest):
        rest[-1][...] = ((t_ref[...] + q_ref[0].astype(F32)) + q_ref[1].astype(F32)) + q_ref[2].astype(F32)

    args, in_specs, aliases = [t, q], [
        pl.BlockSpec((None, tr, cols), lambda i, ids_ref: (ids_ref[3], i, 0)),
        pl.BlockSpec((3, tr, cols), lambda i, ids_ref: (0, i, 0))], None
    if buf is not None:
        args, in_specs, aliases = args + [buf], in_specs + [ANY], {2: 0}
    return _call(body, args, None, name="shard_sum", grid=(nb,), in_specs=in_specs,
                 out_specs=pl.BlockSpec((tr, cols), lambda i, ids_ref: ((2 * part + ids_ref[2]) * nb + i, 0)),
                 out_shape=SDS((2 * h * n_parts, cols), t.dtype), aliases=aliases, prefetch=ids,
                 compiler_params=_cp(("parallel",), 40))


def _join_halves(bufs):
    n = len(bufs)

    def body(*refs):
        outs = refs[n:2 * n]
        send_sems, recv_sems = refs[2 * n:]
        x, y, c, _ = _place()
        sibling = (x, y, 1 - c)

        def win(ref, which):
            h = ref.shape[0] // 2
            return ref.at[pl.ds(which * h, h), :]

        sends = [_remote(win(outs[a], c), win(outs[a], c), send_sems.at[a], recv_sems.at[a], sibling)
                 for a in range(n)]
        for cp in sends:
            cp.start()
        for a in range(n):
            other = win(outs[a], 1 - c)
            _remote(other, other, send_sems.at[a], recv_sems.at[a], sibling).wait_recv()
        for cp in sends:
            cp.wait_send()

    return pl.pallas_call(
        body, name="join_halves",
        in_specs=[ANY] * n, out_specs=tuple([ANY] * n),
        out_shape=tuple(SDS(b.shape, b.dtype) for b in bufs),
        input_output_aliases={a: a for a in range(n)},
        scratch_shapes=[pltpu.SemaphoreType.DMA((n,)), pltpu.SemaphoreType.DMA((n,))],
        compiler_params=_cp(has_side_effects=True))(*bufs)


def _gather_small(block, n_sum):
    rows, cols = block.shape
    n_dev = 8

    def body(x_ref, o_ref, g_ref, buf, send_sems, recv_sems, local_sem):
        x, y, c, chips = _place()
        me, sibling = (x, y, c), (x, y, 1 - c)

        def slot(px_, py_, pc_):
            return buf.at[4 * px_ + 2 * py_ + pc_]

        def copy(k, who, to, src=None):
            return _remote(slot(*who) if src is None else src, slot(*who), send_sems.at[k], recv_sems.at[k], to)

        mine = pltpu.make_async_copy(x_ref, slot(*me), local_sem)
        mine.start()
        first = [copy(0, me, sibling, src=x_ref)]
        first += [copy(1 + j, me, (*chip, c), src=x_ref) for j, chip in enumerate(chips)]
        for cp in first:
            cp.start()
        passed = [copy(4 + j, (*chip, c), sibling) for j, chip in enumerate(chips)]
        for j, chip in enumerate(chips):
            copy(1 + j, (*chip, c), me).wait_recv()
            passed[j].start()
        copy(0, sibling, me).wait_recv()
        for j, chip in enumerate(chips):
            copy(4 + j, (*chip, 1 - c), me).wait_recv()
        for cp in first + passed:
            cp.wait_send()
        mine.wait()
        acc = buf[0, :, :n_sum]
        for s in range(1, n_dev):
            acc = acc + buf[s, :, :n_sum]
        o_ref[...] = acc
        for s in range(n_dev):
            g_ref[s * rows:(s + 1) * rows, :] = buf[s, :, n_sum:]

    return pl.pallas_call(
        body, name="gather_small",
        in_specs=[pl.BlockSpec(memory_space=pltpu.VMEM)],
        out_specs=(pl.BlockSpec(memory_space=pltpu.VMEM), pl.BlockSpec(memory_space=pltpu.VMEM)),
        out_shape=(SDS((rows, n_sum), F32), SDS((n_dev * rows, cols - n_sum), F32)),
        scratch_shapes=[pltpu.VMEM((n_dev, rows, cols), F32), pltpu.SemaphoreType.DMA((7,)),
                        pltpu.SemaphoreType.DMA((7,)), pltpu.SemaphoreType.DMA],
        compiler_params=_cp(has_side_effects=True))(block)


def _adam_math(w, g, m, v):
    m = ADAM_B1 * m + (1.0 - ADAM_B1) * g
    v = ADAM_B2 * v + (1.0 - ADAM_B2) * (g * g)
    m_hat = m / (1.0 - ADAM_B1 ** ADAM_STEP)
    v_hat = v / (1.0 - ADAM_B2 ** ADAM_STEP)
    delta = -ADAM_LR * (m_hat / (jnp.sqrt(v_hat) + ADAM_EPS) + ADAM_WD * w)
    return delta, m, v


def _adamw(w, g, m, v):
    rows, cols = w.shape
    tr = min(rows, 256 if cols <= 2048 else 128)

    def body(w_ref, g_ref, m_ref, v_ref, go_ref, d_ref, nm_ref, nv_ref):
        g = g_ref[...]
        go_ref[...] = g
        d_ref[...], nm_ref[...], nv_ref[...] = _adam_math(w_ref[...], g, m_ref[...], v_ref[...])

    spec = pl.BlockSpec((tr, cols), lambda i: (i, 0))
    return pl.pallas_call(
        body, name="adamw", grid=(rows // tr,), in_specs=[spec] * 4, out_specs=(spec,) * 4,
        out_shape=(SDS(w.shape, F32),) * 4, compiler_params=_cp(("parallel",), 40))(w, g, m, v)


def _adamw_small(w, g, m, v):
    def body(w_ref, g_ref, m_ref, v_ref, go_ref, d_ref, nm_ref, nv_ref):
        w = w_ref[...]
        g = g_ref[...]
        sub = lax.broadcasted_iota(jnp.int32, w.shape, 0)
        lane = lax.broadcasted_iota(jnp.int32, w.shape, 1)
        is_ret = jnp.logical_and(sub == 5, lane < 2 * RET_HEADS)
        u = jnp.exp(jnp.where(is_ret, w, -1.0) * jnp.log(2.0))
        g = jnp.where(is_ret, g * (-u * jnp.log(2.0) / (1.0 - u)), g)
        go_ref[...] = g
        d_ref[...], nm_ref[...], nv_ref[...] = _adam_math(w, g, m_ref[...], v_ref[...])

    return pl.pallas_call(body, name="adamw_small", out_shape=(SDS(w.shape, F32),) * 4)(w, g, m, v)


def _rope_tables(seq, n_samp, n_ctx_rows):
    rows = seq // GRID_W
    row = jnp.repeat(jnp.arange(rows, dtype=F32), GRID_W)
    col = jnp.tile(jnp.arange(GRID_W, dtype=F32), rows)
    half = ATT_HEAD_DIM // 2
    freqs = ROPE_THETA ** (-jnp.arange(0, half, 2, dtype=F32) / half)
    ang = jnp.concatenate([row[:, None] * freqs, col[:, None] * freqs], axis=-1)
    cos, sin = jnp.cos(ang), jnp.sin(ang)
    cos_f = jnp.repeat(cos, 2, axis=1)
    sin_s = jnp.stack([-sin, sin], axis=-1).reshape(seq, ATT_HEAD_DIM)
    cos_all = jnp.concatenate([jnp.tile(cos_f, (n_samp, 1)), jnp.ones((n_ctx_rows, ATT_HEAD_DIM), F32)], axis=0)
    sin_all = jnp.concatenate([jnp.tile(sin_s, (n_samp, 1)), jnp.zeros((n_ctx_rows, ATT_HEAD_DIM), F32)], axis=0)
    return cos_all, sin_all


def _pack_small(c_ctx, norm_w, b_ada, ret, qn, kn):
    d = D_MODEL
    row5 = jnp.concatenate([ret.reshape(-1), jnp.zeros((128 - 2 * RET_HEADS,), F32), qn.reshape(-1), kn.reshape(-1),
                            jnp.zeros((d - 384,), F32)])
    return jnp.concatenate([c_ctx.reshape(1, d), norm_w.reshape(1, d), b_ada.reshape(3, d), row5.reshape(1, d),
                            jnp.zeros((2, d), F32)], axis=0)


def _unpack_small(p):
    d = D_MODEL
    return (p[0], p[1:2], p[2:5].reshape(1, 3 * d), p[5, :2 * RET_HEADS].reshape(1, 2, RET_HEADS),
            p[5:6, 128:256], p[5:6, 256:384])


def _step(x, c, ctx, c_ctx, norm_w, b_ada, ret_log2_decay, q_norm_w, k_norm_w, loss_target, weights, ids, dist):
    n_samp, seq, d = x.shape
    lc = ctx.shape[1]
    t_lat, t_ctx = n_samp * seq, n_samp * lc
    assert seq % TM == 0 and t_ctx == TM and t_lat % lc == 0 and seq % GRID_W == 0
    tps = seq // TM

    x_lat = x.reshape(t_lat, d)
    x_ctx = ctx.reshape(t_ctx, d)
    cvec8 = jnp.concatenate([c, c_ctx.reshape(1, d), jnp.zeros((8 - n_samp - 1, d), F32)], axis=0)
    lg = jnp.log1p(-jnp.exp2(ret_log2_decay.reshape(2, RET_HEADS)))
    cos_all, sin_all = _rope_tables(seq, n_samp, t_ctx)

    w_ada_b, w_in_b, w_or_b, w_oa_b, w_out_b = weights
    w_ada_g = _run_comm(_ag_comm((w_ada_b,), (0, 1, 2)))[0] if dist else w_ada_b
    mod8 = _adaln_fwd(cvec8, w_ada_g, b_ada)
    mod3 = mod8[:n_samp + 1]
    shift3 = mod3[:, None, 0:d]
    scale3 = mod3[:, None, d:2 * d]
    gate3 = mod3[:, None, 2 * d:3 * d]

    hx, hxt = _norm_fwd(x_lat, x_ctx, norm_w, scale3, shift3, tps, n_samp)
    if dist:
        px, w_in_g = _in_proj_gather(hx, w_in_b, ids)
    else:
        w_in_g = w_in_b
        px = _in_proj(hx, w_in_g, ids, 0, N_SHARD)

    states0 = _ctx_state_fwd(px, lg, n_samp, t_lat, lc)
    if dist:
        (o_f, o_b, saved), w_o = _ret_fwd(px, states0, lg, n_samp, seq,
                                          comm=_ag_comm((w_or_b, w_oa_b, w_out_b), (0, 1, 2)))
    else:
        (o_f, o_b, saved), w_o = _ret_fwd(px, states0, lg, n_samp, seq), (w_or_b, w_oa_b, w_out_b)
    w_o_ret, w_o_att, w_out = (w.reshape(-1, d) for w in w_o)

    qn = _att_prep_q(px, cos_all, sin_all, q_norm_w, t_lat)
    kn, vn = _att_prep_kv(px, cos_all, sin_all, k_norm_w)
    o_att, lse = _att_fwd(qn, kn, vn, n_samp, seq, lc)

    (gx_res, do, do_att, dpx, loss8, dgate, g_w_o_ret, g_w_o_att, g_w_out) = _merge(
        x_lat, loss_target.reshape(t_lat, d), o_f, o_b, o_att, px, gate3, w_o_ret, w_o_att, w_out, tps)

    g_a = [g.reshape(N_SHARD, -1, d) for g in (g_w_o_ret, g_w_o_att, g_w_out)]
    res = _att_bwd(dpx, qn, kn, vn, px, o_att, lse, do_att, cos_all, sin_all, q_norm_w, n_samp, seq, lc,
                   comm=_swap_comm(g_a) if dist else None)
    (dpx, dkl, dkc, dvl, dvc, gqw), sib_a = res if dist else (res, None)
    dpx, gkw = _att_kv_bwd(dpx, dkl, dkc, dvl, dvc, px, cos_all, sin_all, k_norm_w)
    if dist:
        t_a = [_chip_sum(g, p, ids) for g, p in zip(g_a, sib_a)]

    res = _ret_bwd(dpx, px, do, saved, lg, n_samp, seq,
                   comm=_exchange_comm([t16 for _, t16 in t_a]) if dist else None)
    (dpx, dstates, dlg_lat), q_a = res if dist else (res, None)
    if dist:
        r_a = [_shard_sum(t, q, ids) for (t, _), q in zip(t_a, q_a)]
    dpx, dlg_ctx = _ctx_state_bwd(dpx, px, dstates, lg, n_samp, t_lat, lc)
    dpx = _zero_ctx_tail(dpx, t_lat)

    n_tiles = dpx.shape[0] // _big_rows(dpx.shape[0])
    if dist:
        g_b = _gw_in(hxt, dpx, 0, 1)
        dhx, (sib_b, *r_a) = _dhx(dpx, w_in_g, 0, 1, comm=_join_comms(_swap_comm([g_b]), _join_comm(r_a)))
        t_b, t16_b = _chip_sum(g_b, sib_b, ids)
        dhx, (q_b,) = _dhx(dpx, w_in_g, 1, n_tiles - 1, dhx=dhx, comm=_exchange_comm([t16_b]))
        r_b_half = _shard_sum(t_b, q_b, ids)
    else:
        g_w_in = _gw_in(hxt, dpx, 0, 1)
        dhx = _dhx(dpx, w_in_g, 0, n_tiles)
    grad_x, dshift, dscale, g_norm_w = _norm_bwd(x_lat, x_ctx, dhx, gx_res, norm_w, scale3, tps, n_samp)

    dgate_all = jnp.concatenate([dgate, jnp.zeros((1, 1, d), F32)], axis=0)
    dmod3 = jnp.concatenate([dshift, dscale, dgate_all], axis=2).reshape(n_samp + 1, 3 * d)
    dmod8 = jnp.concatenate([dmod3, jnp.zeros((8 - n_samp - 1, 3 * d), F32)], axis=0)
    g_lg = (jnp.sum(dlg_lat[:, :, 0], axis=0).reshape(2, RET_HEADS)
            + jnp.stack([jnp.sum(dlg_ctx[:, :, 0, 0], axis=0), jnp.sum(dlg_ctx[:, :, 1, 0], axis=0)], axis=0))
    g_qw = jnp.sum(gqw, axis=(0, 1, 2))
    zero = jnp.zeros((d,), F32)
    if not dist:
        g_w_ada, g_b_ada, dc8 = _adaln_bwd(cvec8, dmod8, w_ada_g)
        small = _pack_small(dc8[n_samp], g_norm_w, g_b_ada, g_lg, g_qw, gkw)
        return (loss8[0, 0], grad_x.reshape(n_samp, seq, d),
                (g_w_ada, g_w_in, g_w_o_ret, g_w_o_att, g_w_out), small)

    local = _pack_small(zero, g_norm_w, jnp.zeros((3 * d,), F32), g_lg, g_qw, gkw).at[6, 0].set(loss8[0, 0])
    small_sum, gathered = _gather_small(jnp.concatenate([local, cvec8, dmod8], axis=1), d)
    (g_w_ada, g_b_ada, dc_all), (r_b,) = _adaln_bwd(gathered[:, :d], gathered[:, d:], w_ada_g,
                                                     comm=_join_comm([r_b_half]))
    dc_ctx = jnp.sum(dc_all.reshape(-1, 8, d)[:, n_samp], axis=0)
    small = small_sum + _pack_small(dc_ctx, zero, g_b_ada, jnp.zeros((2, RET_HEADS), F32), zero[:128], zero[:128])
    r_c = lax.dynamic_index_in_dim(g_w_ada, ids[3], 0, keepdims=False)
    return small[6, 0], grad_x.reshape(n_samp, seq, d), (r_c, r_b, *r_a), small


def kernel(x, c, ctx, c_ctx, norm_w, w_ada, b_ada, w_in, ret_log2_decay, q_norm_w, k_norm_w, w_o_ret, w_o_att, w_out, loss_target, m_c_ctx, m_norm_w, m_w_ada, m_b_ada, m_w_in, m_ret_log2_decay, m_q_norm_w, m_k_norm_w, m_w_o_ret, m_w_o_att, m_w_out, v_c_ctx, v_norm_w, v_w_ada, v_b_ada, v_w_in, v_ret_log2_decay, v_q_norm_w, v_k_norm_w, v_w_o_ret, v_w_o_att, v_w_out):
    big_w = (w_ada[0], w_in[0], w_o_ret[0], w_o_att[0], w_out[0])
    big_m = (m_w_ada[0], m_w_in[0], m_w_o_ret[0], m_w_o_att[0], m_w_out[0])
    big_v = (v_w_ada[0], v_w_in[0], v_w_o_ret[0], v_w_o_att[0], v_w_out[0])

    ids = _place_ids()
    loss, grad_x, big_grad, small_grad_in = _step(
        x, c, ctx, c_ctx, norm_w[0:1], b_ada[0:1], ret_log2_decay[0], q_norm_w[0:1], k_norm_w[0:1], loss_target,
        tuple(_cast_place(w, ids) for w in big_w), ids, True)
    small_w = _pack_small(c_ctx, norm_w, b_ada, ret_log2_decay, q_norm_w, k_norm_w)
    small_m = _pack_small(m_c_ctx, m_norm_w, m_b_ada, m_ret_log2_decay, m_q_norm_w, m_k_norm_w)
    small_v = _pack_small(v_c_ctx, v_norm_w, v_b_ada, v_ret_log2_decay, v_q_norm_w, v_k_norm_w)
    small_grad, small_delta, small_nm, small_nv = _adamw_small(small_w, small_grad_in, small_m, small_v)

    big_g, big_delta, big_nm, big_nv = [], [], [], []
    for w, g, m, v in zip(big_w, big_grad, big_m, big_v):
        go, dlt, nm, nv = _adamw(w, g, m, v)
        big_g.append(go[None])
        big_delta.append(dlt[None])
        big_nm.append(nm[None])
        big_nv.append(nv[None])
    big_grad = big_g

    def order(small_packed, big):
        s = _unpack_small(small_packed)
        return (s[0], s[1], big[0], s[2], big[1], s[3], s[4], s[5], big[2], big[3], big[4])

    return (loss, grad_x, *order(small_grad, big_grad), *order(small_delta, big_delta),
            *order(small_nm, big_nm), *order(small_nv, big_nv))
```

```python
import functools
from typing import NamedTuple

import jax
import jax.numpy as jnp
from jax import lax
from jax.experimental import pallas as pl
from jax.experimental.pallas import tpu as pltpu

F32 = jnp.float32
BF = jnp.bfloat16
SDS = jax.ShapeDtypeStruct
MESH = pl.DeviceIdType.MESH
ANY = pl.BlockSpec(memory_space=pl.ANY)
SMEM = pl.BlockSpec(memory_space=pltpu.SMEM)

D_MODEL = 1024
GRID_W = 64
RET_HEADS = 4
RET_DK = 256
RET_DV = 512
RET_CHUNK = 128
ATT_HEADS = 8
ATT_KV_HEADS = 2
ATT_REP = ATT_HEADS // ATT_KV_HEADS
ATT_HEAD_DIM = 128
ROPE_THETA = 10000.0
NORM_EPS = 1e-6
IN_COLS = 10752
KV_COLS = 3584
C_RK, C_RV, C_AK, C_AV, C_RQ, C_RG, C_AQ, C_AG, C_MR, C_MA = 0, 1024, 3072, 3328, 3584, 4608, 6656, 7680, 8704, 9728
N_SHARD = 4
ADA_W = 3 * D_MODEL // N_SHARD
IN_W = IN_COLS // N_SHARD
IN_BLK = IN_W
BPS = IN_W // IN_BLK
N_IN_BLK = IN_COLS // IN_BLK
TM = 512
ATT_TQ = 512
ADAM_LR, ADAM_B1, ADAM_B2, ADAM_EPS, ADAM_WD, ADAM_STEP = 0.001, 0.9, 0.999, 1e-08, 0.01, 10
MIB = 1024 * 1024


def _cp(sem=None, vmem_mb=None, **kw):
    if sem is not None:
        kw["dimension_semantics"] = sem
    if vmem_mb is not None:
        kw["vmem_limit_bytes"] = vmem_mb * MIB
    return pltpu.CompilerParams(**kw)


def _dot(a, b, ca=1, cb=0):
    return lax.dot_general(a.astype(BF), b.astype(BF), (((ca,), (cb,)), ((), ())), preferred_element_type=F32)


def _sigmoid(x):
    return 0.5 * jnp.tanh(0.5 * x) + 0.5


def _sum_all(x):
    return jnp.sum(jnp.sum(x, axis=1, keepdims=True), axis=0, keepdims=True)


def _swap_pairs(x):
    ax = x.ndim - 1
    lane = lax.broadcasted_iota(jnp.int32, x.shape, ax)
    nxt = pltpu.roll(x, x.shape[ax] - 1, ax)
    prv = pltpu.roll(x, 1, ax)
    return jnp.where(lane % 2 == 0, nxt, prv)


def _rms(x):
    return lax.rsqrt(jnp.mean(x * x, axis=-1, keepdims=True) + NORM_EPS)


def _rms_bwd(dxh, xh, r):
    return r * (dxh - xh * jnp.mean(dxh * xh, axis=-1, keepdims=True))


class _Comm(NamedTuple):
    name: str
    ins: tuple
    out_shapes: tuple
    aliases: dict
    n_sems: int
    phases: tuple


def _join_comms(*comms):
    offs, i_off, o_off, s_off = [], 0, 0, 0
    for cm in comms:
        offs.append((i_off, o_off, s_off))
        i_off, o_off, s_off = i_off + len(cm.ins), o_off + len(cm.out_shapes), s_off + cm.n_sems

    def phase(k):
        def run(ins, outs, ssem, rsem, base):
            sends, recvs = [], []
            for cm, (io, oo, so) in zip(comms, offs):
                if k < len(cm.phases):
                    s, r = cm.phases[k](ins[io:io + len(cm.ins)], outs[oo:oo + len(cm.out_shapes)], ssem, rsem,
                                        base + so)
                    sends += s
                    recvs += r
            return sends, recvs
        return run

    aliases = {}
    for cm, (io, oo, _) in zip(comms, offs):
        aliases.update({io + a: oo + b for a, b in cm.aliases.items()})
    return _Comm("+".join(cm.name for cm in comms), sum((cm.ins for cm in comms), ()),
                 sum((cm.out_shapes for cm in comms), ()), aliases, s_off,
                 tuple(phase(k) for k in range(max(len(cm.phases) for cm in comms))))


def _run_phases(comm, cins, couts, ssem, rsem, first_started):
    for k, phase in enumerate(comm.phases):
        sends, recvs = phase(cins, couts, ssem, rsem, 0)
        if k > 0 or not first_started:
            for cp in sends:
                cp.start()
        for cp in recvs:
            cp.wait_recv()
        for cp in sends:
            cp.wait_send()


def _call(body, args, comm, *, name, grid, in_specs, out_specs, out_shape, scratch_shapes=(),
          compiler_params, aliases=None):
    n_in, n_out, n_sc = len(in_specs), len(out_specs), len(scratch_shapes)
    n_ci, n_co = len(comm.ins), len(comm.out_shapes)
    io_alias = dict(aliases or {})
    io_alias.update({n_in + a: n_out + b for a, b in comm.aliases.items()})

    def kernel_body(*refs):
        ins, cins = refs[:n_in], refs[n_in:n_in + n_ci]
        outs = refs[n_in + n_ci:n_in + n_ci + n_out]
        couts = refs[n_in + n_ci + n_out:n_in + n_ci + n_out + n_co]
        scratch = refs[n_in + n_ci + n_out + n_co:n_in + n_ci + n_out + n_co + n_sc]
        ssem, rsem = refs[-2:]
        first = functools.reduce(jnp.logical_and, [pl.program_id(k) == 0 for k in range(len(grid))])
        last = functools.reduce(jnp.logical_and, [pl.program_id(k) == grid[k] - 1 for k in range(len(grid))])

        @pl.when(first)
        def _():
            for cp in comm.phases[0](cins, couts, ssem, rsem, 0)[0]:
                cp.start()

        body(*ins, *outs, *scratch)

        @pl.when(last)
        def _():
            _run_phases(comm, cins, couts, ssem, rsem, True)

    res = pl.pallas_call(
        kernel_body, name=name + "+" + comm.name, grid=grid, in_specs=list(in_specs) + [ANY] * n_ci,
        out_specs=tuple(out_specs) + tuple([ANY] * n_co), out_shape=tuple(out_shape) + tuple(comm.out_shapes),
        scratch_shapes=list(scratch_shapes) + [pltpu.SemaphoreType.DMA((comm.n_sems,)),
                                               pltpu.SemaphoreType.DMA((comm.n_sems,))],
        input_output_aliases=io_alias, compiler_params=compiler_params)(*args, *comm.ins)
    return tuple(res[:n_out]), tuple(res[n_out:])


def _run_comm(comm):
    n_ci, n_co = len(comm.ins), len(comm.out_shapes)

    def body(*refs):
        _run_phases(comm, refs[:n_ci], refs[n_ci:n_ci + n_co], refs[-2], refs[-1], False)

    return pl.pallas_call(
        body, name=comm.name, in_specs=[ANY] * n_ci, out_specs=tuple([ANY] * n_co), out_shape=tuple(comm.out_shapes),
        input_output_aliases=dict(comm.aliases),
        scratch_shapes=[pltpu.SemaphoreType.DMA((comm.n_sems,)), pltpu.SemaphoreType.DMA((comm.n_sems,))],
        compiler_params=_cp(has_side_effects=True))(*comm.ins)


def _adaln_fwd(cvec8, w_ada_g, b_ada):
    def body(c_ref, w_ref, b_ref, o_ref):
        cv = c_ref[...]
        sc = (cv * _sigmoid(cv)).astype(BF)
        for s in range(N_SHARD):
            cols = slice(s * ADA_W, (s + 1) * ADA_W)
            o_ref[:, cols] = jnp.dot(sc, w_ref[s], preferred_element_type=F32) + b_ref[:, cols]

    return pl.pallas_call(body, out_shape=SDS((8, 3 * D_MODEL), F32), name="adaln_fwd",
                          compiler_params=_cp(vmem_mb=32))(cvec8, w_ada_g, b_ada)


def _adaln_bwd(cvec, dmod, w_ada_g, comm):
    n_rows = cvec.shape[0]
    def body(c_ref, d_ref, w_ref, gw_ref, gb_ref, dc_ref):
        cv = c_ref[...]
        sg = _sigmoid(cv)
        sc = cv * sg
        dm = d_ref[...]
        gb_ref[...] = jnp.sum(dm, axis=0, keepdims=True)
        dsc = jnp.zeros(cv.shape, F32)
        for s in range(N_SHARD):
            cols = slice(s * ADA_W, (s + 1) * ADA_W)
            gw_ref[s] = _dot(sc, dm[:, cols], 0, 0)
            dsc = dsc + _dot(dm[:, cols], w_ref[s], 1, 1)
        dc_ref[...] = dsc * (sg * (1.0 + cv * (1.0 - sg)))

    def whole(shape):
        return pl.BlockSpec(shape, lambda i: (0,) * len(shape))

    shapes = ((N_SHARD, D_MODEL, ADA_W), (1, 3 * D_MODEL), (n_rows, D_MODEL))
    return _call(body, [cvec, dmod, w_ada_g], comm, name="adaln_bwd", grid=(1,),
                 in_specs=[whole(cvec.shape), whole(dmod.shape), whole(w_ada_g.shape)],
                 out_specs=tuple(whole(s) for s in shapes), out_shape=tuple(SDS(s, F32) for s in shapes),
                 compiler_params=_cp(("arbitrary",), 56))


def _big_rows(rows):
    return 1536 if rows % 1536 == 0 else TM


def _norm_fwd(x_lat, x_ctx, norm_w, scale3, shift3, tiles_per_sample, n_samp):
    n_lat = x_lat.shape[0] // TM
    rows = x_lat.shape[0] + x_ctx.shape[0]

    def samp(i):
        return jnp.minimum(i // tiles_per_sample, n_samp)

    def body(x_ref, c_ref, nw_ref, sc_ref, sh_ref, hx_ref, hxt_ref):
        x = jnp.where(pl.program_id(0) < n_lat, x_ref[...], c_ref[...])
        h = x * _rms(x) * nw_ref[...] * (1.0 + sc_ref[...]) + sh_ref[...]
        hx_ref[...] = h.astype(BF)
        hxt_ref[...] = h.T.astype(BF)

    return pl.pallas_call(
        body, name="norm_fwd", grid=(rows // TM,),
        in_specs=[pl.BlockSpec((TM, D_MODEL), lambda i: (jnp.minimum(i, n_lat - 1), 0)),
                  pl.BlockSpec((TM, D_MODEL), lambda i: (jnp.maximum(i - n_lat, 0), 0)),
                  pl.BlockSpec((1, D_MODEL), lambda i: (0, 0)),
                  pl.BlockSpec((None, 1, D_MODEL), lambda i: (samp(i), 0, 0)),
                  pl.BlockSpec((None, 1, D_MODEL), lambda i: (samp(i), 0, 0))],
        out_specs=(pl.BlockSpec((TM, D_MODEL), lambda i: (i, 0)),
                   pl.BlockSpec((D_MODEL, TM), lambda i: (0, i))),
        out_shape=(SDS((rows, D_MODEL), BF), SDS((D_MODEL, rows), BF)),
        compiler_params=_cp(("parallel",), 40))(x_lat, x_ctx, norm_w, scale3, shift3)


def _norm_bwd(x_lat, x_ctx, dhx, gx_res, norm_w, scale3, tiles_per_sample, n_samp):
    rows = x_lat.shape[0] + x_ctx.shape[0]
    n_lat = tiles_per_sample * n_samp

    def samp(i):
        return jnp.minimum(i // tiles_per_sample, n_samp)

    def lat(i):
        return jnp.minimum(i, n_lat - 1)

    def body(x_ref, c_ref, dh_ref, gr_ref, nw_ref, sc_ref, gx_ref, dsh_ref, dsc_ref, dnw_ref):
        i = pl.program_id(0)
        x = jnp.where(i < n_lat, x_ref[...], c_ref[...])
        r = _rms(x)
        xh = x * r
        nw = nw_ref[...]
        dh = dh_ref[...]
        first = jnp.logical_or(i % tiles_per_sample == 0, i >= n_lat)

        @pl.when(first)
        def _():
            dsh_ref[...] = jnp.zeros_like(dsh_ref)
            dsc_ref[...] = jnp.zeros_like(dsc_ref)

        @pl.when(i == 0)
        def _():
            dnw_ref[...] = jnp.zeros_like(dnw_ref)

        dsh_ref[...] += jnp.sum(dh, axis=0, keepdims=True)
        dsc_ref[...] += jnp.sum(dh * (xh * nw), axis=0, keepdims=True)
        du = dh * (1.0 + sc_ref[...])
        dnw_ref[...] += jnp.sum(du * xh, axis=0, keepdims=True)

        @pl.when(i < n_lat)
        def _():
            gx_ref[...] = gr_ref[...] + _rms_bwd(du * nw, xh, r)

    return pl.pallas_call(
        body, name="norm_bwd", grid=(rows // TM,),
        in_specs=[pl.BlockSpec((TM, D_MODEL), lambda i: (lat(i), 0)),
                  pl.BlockSpec((TM, D_MODEL), lambda i: (jnp.maximum(i - n_lat, 0), 0)),
                  pl.BlockSpec((TM, D_MODEL), lambda i: (i, 0)),
                  pl.BlockSpec((TM, D_MODEL), lambda i: (lat(i), 0)),
                  pl.BlockSpec((1, D_MODEL), lambda i: (0, 0)),
                  pl.BlockSpec((None, 1, D_MODEL), lambda i: (samp(i), 0, 0))],
        out_specs=(pl.BlockSpec((TM, D_MODEL), lambda i: (lat(i), 0)),
                   pl.BlockSpec((None, 1, D_MODEL), lambda i: (samp(i), 0, 0)),
                   pl.BlockSpec((None, 1, D_MODEL), lambda i: (samp(i), 0, 0)),
                   pl.BlockSpec((1, D_MODEL), lambda i: (0, 0))),
        out_shape=(SDS((n_lat * TM, D_MODEL), F32), SDS((n_samp + 1, 1, D_MODEL), F32),
                   SDS((n_samp + 1, 1, D_MODEL), F32), SDS((1, D_MODEL), F32)),
        compiler_params=_cp(("arbitrary",), 40))(x_lat, x_ctx, dhx, gx_res, norm_w, scale3)


def _in_proj_gather(hx, w_buf, ids):
    rows = hx.shape[0]
    tb = _big_rows(rows)
    n_i = rows // tb
    hrows = D_MODEL // 2

    def body(ids_ref, h_ref, w_in_hbm, px_ref, w_hbm, wv, lsem, ssem, rsem):
        j, i = pl.program_id(0), pl.program_id(1)
        x, y, c, chips = _place()
        sibling = (x, y, 1 - c)

        def half(s, which):
            return w_hbm.at[s, pl.ds(which * hrows, hrows), :]

        def over_ici(rel):
            chip = chips[rel]
            mine, theirs = half(2 * x + y, c), half(2 * chip[0] + chip[1], c)
            return (_remote(mine, mine, ssem.at[rel], rsem.at[rel], (*chip, c)),
                    _remote(theirs, theirs, ssem.at[rel], rsem.at[rel], (*chip, c)))

        def over_d2d(rel):
            s = 2 * chips[rel][0] + chips[rel][1]
            return (_remote(half(s, c), half(s, c), ssem.at[3 + rel], rsem.at[3 + rel], sibling),
                    _remote(half(s, 1 - c), half(s, 1 - c), ssem.at[3 + rel], rsem.at[3 + rel], sibling))

        first_row_tile = i == 0

        @pl.when(jnp.logical_and(j == 0, first_row_tile))
        def _():
            over_ici(0)[0].start()
            over_ici(1)[0].start()

        for rel in range(3):
            @pl.when(jnp.logical_and(j == rel + 1, first_row_tile))
            def _(rel=rel):
                over_ici(rel)[1].wait_recv()
                passed, landing = over_d2d(rel)
                passed.start()
                if rel == 0:
                    over_ici(2)[0].start()
                landing.wait_recv()

        @pl.when(first_row_tile)
        def _():
            cp = pltpu.make_async_copy(w_hbm.at[ids_ref[4 + j]], wv, lsem)
            cp.start()
            cp.wait()

        px_ref[...] = jnp.dot(h_ref[...], wv[...], preferred_element_type=F32).astype(BF)

        @pl.when(jnp.logical_and(j == N_SHARD - 1, i == n_i - 1))
        def _():
            for rel in range(3):
                over_ici(rel)[0].wait_send()
                over_d2d(rel)[0].wait_send()

    return pl.pallas_call(
        body, name="in_proj_gather", input_output_aliases={2: 1},
        grid_spec=pltpu.PrefetchScalarGridSpec(
            num_scalar_prefetch=1, grid=(N_SHARD, n_i),
            in_specs=[pl.BlockSpec((tb, D_MODEL), lambda j, i, ids_ref: (i, 0)), ANY],
            out_specs=(pl.BlockSpec((tb, IN_W), lambda j, i, ids_ref: (i, ids_ref[4 + j])), ANY),
            scratch_shapes=[pltpu.VMEM((D_MODEL, IN_W), BF), pltpu.SemaphoreType.DMA,
                            pltpu.SemaphoreType.DMA((6,)), pltpu.SemaphoreType.DMA((6,))]),
        out_shape=(SDS((rows, IN_COLS), BF), SDS(w_buf.shape, w_buf.dtype)),
        compiler_params=_cp(("arbitrary", "arbitrary"), 56))(ids, hx, w_buf)


def _gw_in(hxt, dpx_all):
    rows = dpx_all.shape[0]
    tb = _big_rows(rows)

    def body(h_ref, d_ref, o_ref):
        @pl.when(pl.program_id(1) == 0)
        def _():
            o_ref[...] = jnp.zeros_like(o_ref)

        o_ref[...] += jnp.dot(h_ref[...], d_ref[...], preferred_element_type=F32)

    return pl.pallas_call(
        body, name="gw_in", grid=(N_IN_BLK, rows // tb),
        in_specs=[pl.BlockSpec((D_MODEL, tb), lambda j, i: (0, i)),
                  pl.BlockSpec((tb, IN_BLK), lambda j, i: (i, j))],
        out_specs=pl.BlockSpec((None, D_MODEL, IN_BLK), lambda j, i: (j // BPS, 0, j % BPS)),
        out_shape=SDS((N_SHARD, D_MODEL, IN_W), F32),
        compiler_params=_cp(("arbitrary", "arbitrary"), 56))(hxt, dpx_all)


def _dhx(dpx_all, w_in_g, tile0, n_tiles, dhx, comm):
    rows = dpx_all.shape[0]
    tb = _big_rows(rows)

    def body(d_ref, w_ref, *rest):
        o_ref = rest[-1]

        @pl.when(pl.program_id(1) == 0)
        def _():
            o_ref[...] = jnp.zeros_like(o_ref)

        o_ref[...] += lax.dot_general(d_ref[...], w_ref[...], (((1,), (1,)), ((), ())), preferred_element_type=F32)

    args, in_specs, aliases = [dpx_all, w_in_g], [
        pl.BlockSpec((tb, IN_BLK), lambda i, j: (tile0 + i, j)),
        pl.BlockSpec((None, D_MODEL, IN_BLK), lambda i, j: (j // BPS, 0, j % BPS))], None
    if dhx is not None:
        args, in_specs, aliases = args + [dhx], in_specs + [ANY], {2: 0}
    (out,), got = _call(body, args, comm, name="dhx", grid=(n_tiles, N_IN_BLK), in_specs=in_specs,
                        out_specs=(pl.BlockSpec((tb, D_MODEL), lambda i, j: (tile0 + i, 0)),),
                        out_shape=(SDS((rows, D_MODEL), F32),), aliases=aliases,
                        compiler_params=_cp(("arbitrary", "arbitrary"), 56))
    return out, got


def _decays(lgv, d):
    c = RET_CHUNK
    ii = lax.broadcasted_iota(jnp.int32, (c, 1), 0).astype(F32)
    jj = lax.broadcasted_iota(jnp.int32, (1, c), 1).astype(F32)
    a_i = jnp.where(d == 0, ii, c - 1.0 - ii)
    a_j = jnp.where(d == 0, jj, c - 1.0 - jj)
    rel = a_i - a_j
    mask = jnp.where(rel >= 0, jnp.exp(lgv * jnp.maximum(rel, 0.0)), 0.0)
    qd = jnp.exp(lgv * (a_i + 1.0))
    kd = jnp.exp(lgv * (c - 1.0 - a_i))
    gc = jnp.exp(jnp.full((1, 1), lgv * c, F32))
    return a_i, rel, mask, qd, kd, gc


def _ctx_state_fwd(px, lg, n_samp, t_lat, lc):
    rb = t_lat // lc

    def body(lg_ref, k_ref, v_ref, o_ref):
        h = pl.program_id(1)
        k = k_ref[...].astype(F32) * (RET_DK ** -0.5)
        v = v_ref[...]
        pos = lax.broadcasted_iota(jnp.int32, (lc, 1), 0).astype(F32)
        o_ref[0] = _dot(k * jnp.exp(lg_ref[0, h] * (lc - 1.0 - pos)), v, 0, 0)
        o_ref[1] = _dot(k * jnp.exp(lg_ref[1, h] * pos), v, 0, 0)

    return pl.pallas_call(
        body, name="ctx_state_fwd", grid=(n_samp, RET_HEADS),
        in_specs=[SMEM,
                  pl.BlockSpec((lc, RET_DK), lambda b, h: (rb + b, C_RK // RET_DK + h)),
                  pl.BlockSpec((lc, RET_DV), lambda b, h: (rb + b, C_RV // RET_DV + h))],
        out_specs=pl.BlockSpec((None, 2, None, RET_DK, RET_DV), lambda b, h: (b, 0, h, 0, 0)),
        out_shape=SDS((n_samp, 2, RET_HEADS, RET_DK, RET_DV), F32),
        compiler_params=_cp(("parallel", "parallel")))(lg, px, px)


def _ctx_state_bwd(dpx, px, dstates, lg, n_samp, t_lat, lc):
    rb = t_lat // lc
    kspec = pl.BlockSpec((lc, RET_DK), lambda b, h: (rb + b, C_RK // RET_DK + h))
    vspec = pl.BlockSpec((lc, RET_DV), lambda b, h: (rb + b, C_RV // RET_DV + h))
    sspec = pl.BlockSpec((None, 2, None, RET_DK, RET_DV), lambda b, h: (b, 0, h, 0, 0))

    def weights(lg_ref, h):
        pos = lax.broadcasted_iota(jnp.int32, (lc, 1), 0).astype(F32)
        e_f = lc - 1.0 - pos
        return pos, e_f, jnp.exp(lg_ref[0, h] * e_f), jnp.exp(lg_ref[1, h] * pos)

    def k_body(lg_ref, dpx_hbm, k_ref, v_ref, ds_ref, dk_ref, dlg_ref):
        pos, e_f, w_f, w_b = weights(lg_ref, pl.program_id(1))
        k = k_ref[...].astype(F32) * (RET_DK ** -0.5)
        y_f = _dot(v_ref[...], ds_ref[0], 1, 1) * w_f
        y_b = _dot(v_ref[...], ds_ref[1], 1, 1) * w_b
        dk_ref[...] = ((y_f + y_b) * (RET_DK ** -0.5)).astype(BF)
        t_f = _sum_all(e_f * k * y_f)
        t_b = _sum_all(pos * k * y_b)
        sub = lax.broadcasted_iota(jnp.int32, (8, 128), 0)
        dlg_ref[...] = jnp.where(sub == 0, t_f, jnp.where(sub == 1, t_b, 0.0))

    def v_body(lg_ref, dpx_hbm, k_ref, ds_ref, dv_ref):
        _, _, w_f, w_b = weights(lg_ref, pl.program_id(1))
        k = k_ref[...].astype(F32) * (RET_DK ** -0.5)
        dv_ref[...] = (_dot(k * w_f, ds_ref[0]) + _dot(k * w_b, ds_ref[1])).astype(BF)

    dpx, dlg = pl.pallas_call(
        k_body, name="ctx_state_bwd_k", grid=(n_samp, RET_HEADS), input_output_aliases={1: 0},
        in_specs=[SMEM, ANY, kspec, vspec, sspec],
        out_specs=(kspec, pl.BlockSpec((None, None, 8, 128), lambda b, h: (b, h, 0, 0))),
        out_shape=(SDS(dpx.shape, dpx.dtype), SDS((n_samp, RET_HEADS, 8, 128), F32)),
        compiler_params=_cp(("parallel", "parallel")))(lg, dpx, px, px, dstates)
    dpx = pl.pallas_call(
        v_body, name="ctx_state_bwd_v", grid=(n_samp, RET_HEADS), input_output_aliases={1: 0},
        in_specs=[SMEM, ANY, kspec, sspec], out_specs=vspec, out_shape=SDS(dpx.shape, dpx.dtype),
        compiler_params=_cp(("parallel", "parallel")))(lg, dpx, px, dstates)
    return dpx, dlg


def _zero_ctx_tail(dpx, t_lat):
    wb = 512
    n_ctx = (dpx.shape[0] - t_lat) // TM

    def body(dpx_hbm, o_ref):
        o_ref[...] = jnp.zeros_like(o_ref)

    return pl.pallas_call(
        body, name="zero_ctx_tail", grid=(n_ctx, (IN_COLS - KV_COLS) // wb), input_output_aliases={0: 0},
        in_specs=[ANY], out_specs=pl.BlockSpec((TM, wb), lambda i, j: (t_lat // TM + i, KV_COLS // wb + j)),
        out_shape=SDS(dpx.shape, dpx.dtype),
        compiler_params=_cp(("parallel", "parallel")))(dpx)


def _ret_specs(row_f, row_b):
    c = RET_CHUNK
    wq = RET_HEADS * RET_DK // 2
    wv = RET_HEADS * RET_DV // 2
    specs = []
    for row in (row_f, row_b):
        specs += [pl.BlockSpec((c, wq), lambda b, n, row=row: (row(b, n), C_RQ // wq)),
                  pl.BlockSpec((c, wq), lambda b, n, row=row: (row(b, n), C_RQ // wq + 1)),
                  pl.BlockSpec((c, 2 * wq), lambda b, n, row=row: (row(b, n), C_RK // (2 * wq))),
                  pl.BlockSpec((c, wv), lambda b, n, row=row: (row(b, n), C_RV // wv)),
                  pl.BlockSpec((c, wv), lambda b, n, row=row: (row(b, n), C_RV // wv + 1))]
    return specs


def _ret_head(refs, h):
    q0, q1, k_ref, v0, v1 = refs
    hh = h % 2
    q = (q0, q1)[h // 2][:, hh * RET_DK:(hh + 1) * RET_DK].astype(F32)
    k = k_ref[:, h * RET_DK:(h + 1) * RET_DK].astype(F32) * (RET_DK ** -0.5)
    v = (v0, v1)[h // 2][:, hh * RET_DV:(hh + 1) * RET_DV]
    return q, k, v


def _ret_fwd(px, states0, lg, n_samp, seq, comm):
    c = RET_CHUNK
    nc = seq // c
    t_lat = n_samp * seq
    wo = RET_HEADS * RET_DV

    def row_f(b, n):
        return b * nc + n

    def row_b(b, n):
        return b * nc + nc - 1 - n

    def body(lg_ref, *refs):
        ins, (s0_ref, of_ref, ob_ref, st_ref, s_s) = refs[:10], refs[10:]

        @pl.when(pl.program_id(1) == 0)
        def _():
            s_s[...] = s0_ref[...]

        for d, o_ref in ((0, of_ref), (1, ob_ref)):
            for h in range(RET_HEADS):
                _, _, mask, qd, kd, gc = _decays(lg_ref[d, h], d)
                q, k, v = _ret_head(ins[5 * d:5 * d + 5], h)
                s = s_s[d, h]
                st_ref[h, d] = s.astype(BF)
                sc = _dot(q, k, 1, 1) * mask
                o_ref[:, h * RET_DV:(h + 1) * RET_DV] = (_dot(sc, v) + _dot(q * qd, s)).astype(BF)
                s_s[d, h] = s * gc + _dot(k * kd, v, 0, 0)

    return _call(
        body, [lg] + [px] * 10 + [states0], comm, name="ret_fwd", grid=(n_samp, nc),
        in_specs=[SMEM] + _ret_specs(row_f, row_b) + [
            pl.BlockSpec((None, 2, RET_HEADS, RET_DK, RET_DV), lambda b, n: (b, 0, 0, 0, 0))],
        out_specs=(pl.BlockSpec((c, wo), lambda b, n: (row_f(b, n), 0)),
                   pl.BlockSpec((c, wo), lambda b, n: (row_b(b, n), 0)),
                   pl.BlockSpec((None, RET_HEADS, 2, None, RET_DK, RET_DV), lambda b, n: (b, 0, 0, n, 0, 0))),
        out_shape=(SDS((t_lat, wo), BF), SDS((t_lat, wo), BF),
                   SDS((n_samp, RET_HEADS, 2, nc, RET_DK, RET_DV), BF)),
        scratch_shapes=[pltpu.VMEM((2, RET_HEADS, RET_DK, RET_DV), F32)],
        compiler_params=_cp(("arbitrary", "arbitrary"), 48))


def _ret_bwd(dpx, px, do, saved, lg, n_samp, seq, comm):
    c = RET_CHUNK
    nc = seq // c
    assert nc % 2 == 0
    wq, wo = RET_HEADS * RET_DK, RET_HEADS * RET_DV

    def row_f(b, n):
        return b * nc + nc - 1 - n

    def row_b(b, n):
        return b * nc + n

    def body(lg_ref, *refs):
        ins = refs[:10]
        (dof_ref, dob_ref, st_ref, dpx_in, dpx_hbm, ds0_ref, dlg_ref,
         ds_s, acc_s, tq_s, tk_s, tv_s, sq_s, sk_s, sv_s, sems) = refs[10:]
        b, n = pl.program_id(0), pl.program_id(1)
        second = n >= nc // 2
        chunks = (nc - 1 - n, n)

        def parked(ch):
            return pl.ds(pl.multiple_of(ch * c, c), c)

        def flush():
            cps = []
            for d, ch in enumerate(chunks):
                rows = pl.ds(pl.multiple_of((b * nc + ch) * c, c), c)
                cps += [pltpu.make_async_copy(sq_s.at[parked(ch), :], dpx_hbm.at[rows, pl.ds(C_RQ, wq)], sems.at[3 * d]),
                        pltpu.make_async_copy(sk_s.at[parked(ch), :], dpx_hbm.at[rows, pl.ds(C_RK, wq)],
                                              sems.at[3 * d + 1]),
                        pltpu.make_async_copy(sv_s.at[parked(ch), :], dpx_hbm.at[rows, pl.ds(C_RV, wo)],
                                              sems.at[3 * d + 2])]
            return cps

        @pl.when(jnp.logical_or(n > nc // 2, jnp.logical_and(n == 0, b > 0)))
        def _():
            for cp in flush():
                cp.wait()

        @pl.when(n == 0)
        def _():
            ds_s[...] = jnp.zeros_like(ds_s)
            acc_s[...] = jnp.zeros_like(acc_s)

        for d, do_ref in enumerate((dof_ref, dob_ref)):
            for h in range(RET_HEADS):
                a_i, rel, mask, qd, kd, gc = _decays(lg_ref[d, h], d)
                q, k, v = _ret_head(ins[5 * d:5 * d + 5], h)
                qb, kb, vb = q.astype(BF), k.astype(BF), v.astype(BF)
                dob = do_ref[:, h * RET_DV:(h + 1) * RET_DV].astype(BF)
                sb = st_ref[h, d]
                ds = ds_s[d, h]
                dsb = ds.astype(BF)
                raw = _dot(qb, kb, 1, 1)
                sc = raw * mask
                dsc = _dot(dob, vb, 1, 1) * mask
                dscb = dsc.astype(BF)
                x = _dot(dob, sb, 1, 1)
                y = _dot(vb, dsb, 1, 1)
                qq = q * qd
                kk = k * kd
                tq_s[d, :, h * RET_DK:(h + 1) * RET_DK] = _dot(dscb, kb) + x * qd
                tk_s[d, :, h * RET_DK:(h + 1) * RET_DK] = _dot(dscb, qb, 0, 0) + y * kd
                tv_s[d, :, h * RET_DV:(h + 1) * RET_DV] = _dot(sc, dob, 0, 0) + _dot(kk, dsb)
                t = (_sum_all(dsc * raw * rel) + _sum_all((a_i + 1.0) * qq * x)
                     + _sum_all((c - 1.0 - a_i) * kk * y) + c * gc * _sum_all(ds * sb.astype(F32)))
                acc_s[4 * d + h:4 * d + h + 1, :] += t
                ds_s[d, h] = ds * gc + _dot(qq, dob, 0, 0)

        @pl.when(jnp.logical_not(second))
        def _():
            for d, ch in enumerate(chunks):
                sq_s[parked(ch), :] = tq_s[d].astype(BF)
                sk_s[parked(ch), :] = tk_s[d].astype(BF)
                sv_s[parked(ch), :] = tv_s[d].astype(BF)

        @pl.when(second)
        def _():
            for d, ch in enumerate(chunks):
                sq_s[parked(ch), :] = (sq_s[parked(ch), :].astype(F32) + tq_s[d]).astype(BF)
                sk_s[parked(ch), :] = ((sk_s[parked(ch), :].astype(F32) + tk_s[d]) * (RET_DK ** -0.5)).astype(BF)
                sv_s[parked(ch), :] = (sv_s[parked(ch), :].astype(F32) + tv_s[d]).astype(BF)
            for cp in flush():
                cp.start()

        @pl.when(n == nc - 1)
        def _():
            ds0_ref[...] = ds_s[...]
            dlg_ref[...] = acc_s[...]

        @pl.when(jnp.logical_and(b == n_samp - 1, n == nc - 1))
        def _():
            for cp in flush():
                cp.wait()

    do_spec_f = pl.BlockSpec((c, wo), lambda b, n: (row_f(b, n), 0))
    do_spec_b = pl.BlockSpec((c, wo), lambda b, n: (row_b(b, n), 0))
    return _call(
        body, [lg] + [px] * 10 + [do, do, saved, dpx], comm, name="ret_bwd", grid=(n_samp, nc), aliases={14: 0},
        in_specs=[SMEM] + _ret_specs(row_f, row_b) + [
            do_spec_f, do_spec_b,
            pl.BlockSpec((None, RET_HEADS, 2, None, RET_DK, RET_DV), lambda b, n: (b, 0, 0, nc - 1 - n, 0, 0)),
            ANY],
        out_specs=(ANY,
                   pl.BlockSpec((None, 2, RET_HEADS, RET_DK, RET_DV), lambda b, n: (b, 0, 0, 0, 0)),
                   pl.BlockSpec((None, 8, 128), lambda b, n: (b, 0, 0))),
        out_shape=(SDS(dpx.shape, dpx.dtype),
                   SDS((n_samp, 2, RET_HEADS, RET_DK, RET_DV), F32), SDS((n_samp, 8, 128), F32)),
        scratch_shapes=[pltpu.VMEM((2, RET_HEADS, RET_DK, RET_DV), F32), pltpu.VMEM((8, 128), F32),
                        pltpu.VMEM((2, c, wq), F32), pltpu.VMEM((2, c, wq), F32), pltpu.VMEM((2, c, wo), F32),
                        pltpu.VMEM((seq, wq), BF), pltpu.VMEM((seq, wq), BF), pltpu.VMEM((seq, wo), BF),
                        pltpu.SemaphoreType.DMA((6,))],
        compiler_params=_cp(("arbitrary", "arbitrary"), 60))


def _norm_rope(x, w, cos, sin):
    xn = x * _rms(x) * w
    return xn * cos + _swap_pairs(xn) * sin


def _norm_rope_bwd(dy, x, w, cos, sin):
    dxn = dy * cos + _swap_pairs(dy * sin)
    r = _rms(x)
    xh = x * r
    return _rms_bwd(dxn * w, xh, r), jnp.sum(dxn * xh, axis=0, keepdims=True)


def _att_prep_q(px, cos_all, sin_all, qnw, t_lat):
    hd = ATT_HEAD_DIM
    wblk = ATT_REP * hd

    def body(x_ref, cos_ref, sin_ref, w_ref, o_ref):
        for r in range(ATT_REP):
            cols = slice(r * hd, (r + 1) * hd)
            qr = _norm_rope(x_ref[:, cols].astype(F32), w_ref[...], cos_ref[...], sin_ref[...])
            o_ref[:, cols] = (qr * (hd ** -0.5)).astype(BF)

    return pl.pallas_call(
        body, name="att_prep_q", grid=(t_lat // TM, ATT_KV_HEADS),
        in_specs=[pl.BlockSpec((TM, wblk), lambda i, g: (i, C_AQ // wblk + g)),
                  pl.BlockSpec((TM, hd), lambda i, g: (i, 0)),
                  pl.BlockSpec((TM, hd), lambda i, g: (i, 0)),
                  pl.BlockSpec((1, hd), lambda i, g: (0, 0))],
        out_specs=pl.BlockSpec((TM, wblk), lambda i, g: (i, g)),
        out_shape=SDS((t_lat, ATT_HEADS * hd), BF),
        compiler_params=_cp(("parallel", "parallel")))(px, cos_all, sin_all, qnw)


def _att_prep_kv(px, cos_all, sin_all, knw):
    rows = px.shape[0]
    hd = ATT_HEAD_DIM
    kvw = ATT_KV_HEADS * hd

    def body(x_ref, cos_ref, sin_ref, w_ref, k_ref, v_ref):
        for g in range(ATT_KV_HEADS):
            cols = slice(g * hd, (g + 1) * hd)
            k_ref[:, cols] = _norm_rope(x_ref[:, cols].astype(F32), w_ref[...], cos_ref[...],
                                        sin_ref[...]).astype(BF)
        v_ref[...] = x_ref[:, kvw:].astype(BF)

    return pl.pallas_call(
        body, name="att_prep_kv", grid=(rows // TM,),
        in_specs=[pl.BlockSpec((TM, 2 * kvw), lambda i: (i, C_AK // (2 * kvw))),
                  pl.BlockSpec((TM, hd), lambda i: (i, 0)),
                  pl.BlockSpec((TM, hd), lambda i: (i, 0)),
                  pl.BlockSpec((1, hd), lambda i: (0, 0))],
        out_specs=(pl.BlockSpec((TM, kvw), lambda i: (i, 0)), pl.BlockSpec((TM, kvw), lambda i: (i, 0))),
        out_shape=(SDS((rows, kvw), BF), SDS((rows, kvw), BF)),
        compiler_params=_cp(("parallel",)))(px, cos_all, sin_all, knw)


def _att_kv_bwd(dpx, dkl, dkc, dvl, dvc, px, cos_all, sin_all, knw):
    rows = px.shape[0]
    hd = ATT_HEAD_DIM
    kvw = ATT_KV_HEADS * hd
    n_lat = dkl.shape[0] // TM
    assert dkc.shape[0] == TM

    def body(dpx_hbm, dkl_ref, dkc_ref, dvl_ref, dvc_ref, x_ref, cos_ref, sin_ref, w_ref, o_ref, gw_ref):
        i = pl.program_id(0)

        @pl.when(i == 0)
        def _():
            gw_ref[...] = jnp.zeros_like(gw_ref)

        is_lat = i < n_lat
        dk = jnp.where(is_lat, dkl_ref[...], dkc_ref[...])
        dv = jnp.where(is_lat, dvl_ref[...], dvc_ref[...])
        for g in range(ATT_KV_HEADS):
            cols = slice(g * hd, (g + 1) * hd)
            dx, gw = _norm_rope_bwd(dk[:, cols], x_ref[:, cols].astype(F32), w_ref[...], cos_ref[...], sin_ref[...])
            o_ref[:, cols] = dx.astype(BF)
            gw_ref[...] += gw
        o_ref[:, kvw:] = dv.astype(BF)

    lat = pl.BlockSpec((TM, kvw), lambda i: (jnp.minimum(i, n_lat - 1), 0))
    ctx = pl.BlockSpec((TM, kvw), lambda i: (0, 0))
    kvcol = pl.BlockSpec((TM, 2 * kvw), lambda i: (i, C_AK // (2 * kvw)))
    return pl.pallas_call(
        body, name="att_kv_bwd", grid=(rows // TM,), input_output_aliases={0: 0},
        in_specs=[ANY, lat, ctx, lat, ctx, kvcol,
                  pl.BlockSpec((TM, hd), lambda i: (i, 0)),
                  pl.BlockSpec((TM, hd), lambda i: (i, 0)),
                  pl.BlockSpec((1, hd), lambda i: (0, 0))],
        out_specs=(kvcol, pl.BlockSpec((1, hd), lambda i: (0, 0))),
        out_shape=(SDS(dpx.shape, dpx.dtype), SDS((1, hd), F32)),
        compiler_params=_cp(("arbitrary",)))(dpx, dkl, dkc, dvl, dvc, px, cos_all, sin_all, knw)


def _stack_heads(ref_or_val):
    hd = ATT_HEAD_DIM
    return jnp.concatenate([ref_or_val[:, r * hd:(r + 1) * hd] for r in range(ATT_REP)], axis=0)


def _att_scores(q, kl, kc):
    sl = _dot(q, kl, 1, 1)
    sc = _dot(q, kc, 1, 1)
    m = jnp.maximum(jnp.max(sl, axis=-1, keepdims=True), jnp.max(sc, axis=-1, keepdims=True))
    el = jnp.exp(sl - m)
    ec = jnp.exp(sc - m)
    denom = jnp.sum(el, axis=-1, keepdims=True) + jnp.sum(ec, axis=-1, keepdims=True)
    return el, ec, denom, m


def _att_fwd(qn, kn, vn, n_samp, seq, lc):
    hd = ATT_HEAD_DIM
    tq = ATT_TQ
    nq = seq // tq
    wblk = ATT_REP * hd
    cb = n_samp * seq // lc
    t_lat = n_samp * seq

    def body(q_ref, kl_ref, kc_ref, vl_ref, vc_ref, o_ref, lse_ref):
        lane = lax.broadcasted_iota(jnp.int32, (tq, hd), 1)
        lse = jnp.zeros((tq, hd), F32)
        for r in range(ATT_REP):
            cols = slice(r * hd, (r + 1) * hd)
            el, ec, denom, m = _att_scores(q_ref[:, cols], kl_ref[...], kc_ref[...])
            o_ref[:, cols] = ((_dot(el, vl_ref[...]) + _dot(ec, vc_ref[...])) / denom).astype(BF)
            lse = jnp.where(lane == r, m + jnp.log(denom), lse)
        lse_ref[...] = lse

    return pl.pallas_call(
        body, name="att_fwd", grid=(n_samp, ATT_KV_HEADS, nq),
        in_specs=[pl.BlockSpec((tq, wblk), lambda b, g, i: (b * nq + i, g)),
                  pl.BlockSpec((seq, hd), lambda b, g, i: (b, g)),
                  pl.BlockSpec((lc, hd), lambda b, g, i: (cb + b, g)),
                  pl.BlockSpec((seq, hd), lambda b, g, i: (b, g)),
                  pl.BlockSpec((lc, hd), lambda b, g, i: (cb + b, g))],
        out_specs=(pl.BlockSpec((tq, wblk), lambda b, g, i: (b * nq + i, g)),
                   pl.BlockSpec((tq, hd), lambda b, g, i: (b * nq + i, g))),
        out_shape=(SDS((t_lat, ATT_HEADS * hd), BF), SDS((t_lat, ATT_KV_HEADS * hd), F32)),
        compiler_params=_cp(("parallel", "parallel", "parallel"), 48))(qn, kn, kn, vn, vn)


def _att_bwd(dpx, qn, kn, vn, px, o_att, lse, do_att, cos_all, sin_all, qnw, n_samp, seq, lc, comm):
    hd = ATT_HEAD_DIM
    tq = ATT_TQ
    nq = seq // tq
    wblk = ATT_REP * hd
    cb = n_samp * seq // lc
    t_lat = n_samp * seq
    kvw = ATT_KV_HEADS * hd
    scale = hd ** -0.5

    def body(dpx_hbm, q_ref, kl_ref, kc_ref, vl_ref, vc_ref, o_ref, do_ref, x_ref, cos_ref, sin_ref, w_ref,
             lse_ref, dq_ref, dkl_ref, dkc_ref, dvl_ref, dvc_ref, gw_ref, akl, akc, avl, avc, aw):
        i = pl.program_id(2)

        @pl.when(i == 0)
        def _():
            akl[...] = jnp.zeros_like(akl)
            akc[...] = jnp.zeros_like(akc)
            avl[...] = jnp.zeros_like(avl)
            avc[...] = jnp.zeros_like(avc)
            aw[...] = jnp.zeros_like(aw)

        dobs, pls, pcs, dsls, dscs = [], [], [], [], []
        for r in range(ATT_REP):
            cols = slice(r * hd, (r + 1) * hd)
            dob = do_ref[:, cols]
            delta = jnp.sum(dob.astype(F32) * o_ref[:, cols].astype(F32), axis=-1, keepdims=True)
            lse = lse_ref[:, r:r + 1]
            p_l = jnp.exp(_dot(q_ref[:, cols], kl_ref[...], 1, 1) - lse).astype(BF)
            p_c = jnp.exp(_dot(q_ref[:, cols], kc_ref[...], 1, 1) - lse).astype(BF)
            ds_l = (p_l * (_dot(dob, vl_ref[...], 1, 1) - delta)).astype(BF)
            ds_c = (p_c * (_dot(dob, vc_ref[...], 1, 1) - delta)).astype(BF)
            dq = (_dot(ds_l, kl_ref[...]) + _dot(ds_c, kc_ref[...])) * scale
            dx, gw = _norm_rope_bwd(dq, x_ref[:, cols].astype(F32), w_ref[...], cos_ref[...], sin_ref[...])
            dq_ref[:, cols] = dx.astype(BF)
            aw[...] += gw
            dobs.append(dob)
            pls.append(p_l)
            pcs.append(p_c)
            dsls.append(ds_l)
            dscs.append(ds_c)
        do4 = jnp.concatenate(dobs, axis=0)
        q4 = _stack_heads(q_ref)
        avl[...] += _dot(jnp.concatenate(pls, axis=0), do4, 0, 0)
        avc[...] += _dot(jnp.concatenate(pcs, axis=0), do4, 0, 0)
        akl[...] += _dot(jnp.concatenate(dsls, axis=0), q4, 0, 0)
        akc[...] += _dot(jnp.concatenate(dscs, axis=0), q4, 0, 0)

        @pl.when(i == nq - 1)
        def _():
            dkl_ref[...] = akl[...]
            dkc_ref[...] = akc[...]
            dvl_ref[...] = avl[...]
            dvc_ref[...] = avc[...]
            gw_ref[...] = aw[...]

    return _call(
        body, [dpx, qn, kn, kn, vn, vn, o_att, do_att, px, cos_all, sin_all, qnw, lse], comm,
        name="att_bwd", grid=(n_samp, ATT_KV_HEADS, nq), aliases={0: 0},
        in_specs=[ANY,
                  pl.BlockSpec((tq, wblk), lambda b, g, i: (b * nq + i, g)),
                  pl.BlockSpec((seq, hd), lambda b, g, i: (b, g)),
                  pl.BlockSpec((lc, hd), lambda b, g, i: (cb + b, g)),
                  pl.BlockSpec((seq, hd), lambda b, g, i: (b, g)),
                  pl.BlockSpec((lc, hd), lambda b, g, i: (cb + b, g)),
                  pl.BlockSpec((tq, wblk), lambda b, g, i: (b * nq + i, g)),
                  pl.BlockSpec((tq, wblk), lambda b, g, i: (b * nq + i, g)),
                  pl.BlockSpec((tq, wblk), lambda b, g, i: (b * nq + i, C_AQ // wblk + g)),
                  pl.BlockSpec((tq, hd), lambda b, g, i: (b * nq + i, 0)),
                  pl.BlockSpec((tq, hd), lambda b, g, i: (b * nq + i, 0)),
                  pl.BlockSpec((1, hd), lambda b, g, i: (0, 0)),
                  pl.BlockSpec((tq, hd), lambda b, g, i: (b * nq + i, g))],
        out_specs=(pl.BlockSpec((tq, wblk), lambda b, g, i: (b * nq + i, C_AQ // wblk + g)),
                   pl.BlockSpec((seq, hd), lambda b, g, i: (b, g)),
                   pl.BlockSpec((lc, hd), lambda b, g, i: (b, g)),
                   pl.BlockSpec((seq, hd), lambda b, g, i: (b, g)),
                   pl.BlockSpec((lc, hd), lambda b, g, i: (b, g)),
                   pl.BlockSpec((None, None, 1, hd), lambda b, g, i: (b, g, 0, 0))),
        out_shape=(SDS(dpx.shape, dpx.dtype),
                   SDS((t_lat, kvw), F32), SDS((n_samp * lc, kvw), F32),
                   SDS((t_lat, kvw), F32), SDS((n_samp * lc, kvw), F32),
                   SDS((n_samp, ATT_KV_HEADS, 1, hd), F32)),
        scratch_shapes=[pltpu.VMEM((seq, hd), F32), pltpu.VMEM((lc, hd), F32),
                        pltpu.VMEM((seq, hd), F32), pltpu.VMEM((lc, hd), F32), pltpu.VMEM((1, hd), F32)],
        compiler_params=_cp(("arbitrary", "arbitrary", "arbitrary"), 56))


def _merge(x_lat, target, o_f, o_b, o_att, px, gate3, w_o_ret, w_o_att, w_out, tiles_per_sample):
    t_lat = x_lat.shape[0]
    tm = 256
    n_t = t_lat // tm
    per = tiles_per_sample * (TM // tm)
    d = D_MODEL
    rv = RET_HEADS * RET_DV
    n_samp = gate3.shape[0] - 1

    half = d // 2
    n_px = 10

    def body(x_ref, t_ref, of_ref, ob_ref, oa_ref, *rest):
        pxs, rest = rest[:n_px], rest[n_px:]
        (gt_ref, wor_ref, woa_ref, wout_ref,
         gx_ref, dor_ref, doa_ref, dpx_hbm, loss_ref, dgt_ref, gwor_hbm, gwoa_hbm, gwout_hbm,
         aor, aoa, aout, drg_ref, dtail_ref, sems) = rest
        i = pl.program_id(0)

        def copies(step):
            rows = pl.ds(pl.multiple_of(step * tm, tm), tm)
            return (pltpu.make_async_copy(drg_ref, dpx_hbm.at[rows, pl.ds(C_RG, rv)], sems.at[0]),
                    pltpu.make_async_copy(dtail_ref, dpx_hbm.at[rows, pl.ds(C_AG, 3 * d)], sems.at[1]))

        @pl.when(i == 0)
        def _():
            aor[...] = jnp.zeros_like(aor)
            aoa[...] = jnp.zeros_like(aoa)
            aout[...] = jnp.zeros_like(aout)
            loss_ref[...] = jnp.zeros_like(loss_ref)

        @pl.when(i % per == 0)
        def _():
            dgt_ref[...] = jnp.zeros_like(dgt_ref)

        def cat(refs):
            return jnp.concatenate([r[...] for r in refs], axis=1).astype(F32)

        def ret_head(h):
            cols = slice(h * RET_DV, (h + 1) * RET_DV)
            o = of_ref[:, cols].astype(F32) + ob_ref[:, cols].astype(F32)
            r = _rms(o)
            g = pxs[h][...].astype(F32)
            return o * r, r, g, _sigmoid(g)

        def att_half(k):
            o = oa_ref[:, k * half:(k + 1) * half].astype(F32)
            g = pxs[4 + k][...].astype(F32)
            return o, g, _sigmoid(g)

        yrs = []
        for h in range(RET_HEADS):
            on, _, g, sg = ret_head(h)
            yrs.append((on * (g * sg)).astype(BF))
        yr = jnp.concatenate(yrs, axis=1)
        yas = []
        for k in range(2):
            o, g, sg = att_half(k)
            yas.append((o * (g * sg)).astype(BF))
        ya = jnp.concatenate(yas, axis=1)

        a = jnp.dot(yr, wor_ref[...], preferred_element_type=F32)
        b = jnp.dot(ya, woa_ref[...], preferred_element_type=F32)
        sr = _sigmoid(cat(pxs[6:8]))
        sa = _sigmoid(cat(pxs[8:10]))
        yb = (sr * a + sa * b).astype(BF)
        out = jnp.dot(yb, wout_ref[...], preferred_element_type=F32)
        gate = gt_ref[...]
        err = x_ref[...] + gate * out - t_ref[...]
        loss_ref[...] += 0.5 * _sum_all(err * err) * (1.0 / d)
        dy_tok = err * (1.0 / d)
        gx_ref[...] = dy_tok
        dgt_ref[...] += jnp.sum(dy_tok * out, axis=0, keepdims=True)
        dout = (dy_tok * gate).astype(BF)
        aout[...] += _dot(yb, dout, 0, 0)
        dyy = _dot(dout, wout_ref[...], 1, 1)
        da = (dyy * sr).astype(BF)
        db = (dyy * sa).astype(BF)
        aor[...] += _dot(yr, da, 0, 0)
        aoa[...] += _dot(ya, db, 0, 0)
        dyr = _dot(da, wor_ref[...], 1, 1)
        dya = _dot(db, woa_ref[...], 1, 1)

        @pl.when(i > 0)
        def _():
            for cp in copies(i - 1):
                cp.wait()

        dtail_ref[:, d:2 * d] = (dyy * a * (sr * (1.0 - sr))).astype(BF)
        dtail_ref[:, 2 * d:] = (dyy * b * (sa * (1.0 - sa))).astype(BF)
        for h in range(RET_HEADS):
            cols = slice(h * RET_DV, (h + 1) * RET_DV)
            on, r, g, sg = ret_head(h)
            dy = dyr[:, cols]
            drg_ref[:, cols] = (dy * on * (sg * (1.0 + g * (1.0 - sg)))).astype(BF)
            dor_ref[:, cols] = _rms_bwd(dy * (g * sg), on, r).astype(BF)
        for k in range(2):
            cols = slice(k * half, (k + 1) * half)
            o, g, sg = att_half(k)
            dy = dya[:, cols]
            dtail_ref[:, cols] = (dy * o * (sg * (1.0 + g * (1.0 - sg)))).astype(BF)
            doa_ref[:, cols] = (dy * (g * sg)).astype(BF)
        for cp in copies(i):
            cp.start()

        @pl.when(i == n_t - 1)
        def _():
            for cp in copies(i):
                cp.wait()
            pltpu.sync_copy(aor, gwor_hbm)
            pltpu.sync_copy(aoa, gwoa_hbm)
            pltpu.sync_copy(aout, gwout_hbm)

    def px_blk(col):
        return pl.BlockSpec((tm, half), lambda i: (i, col // half))

    def resident(shape):
        return pl.BlockSpec(shape, lambda i: (0, 0), pipeline_mode=pl.Buffered(1))

    px_cols = ([C_RG + k * half for k in range(4)] + [C_AG, C_AG + half]
               + [C_MR, C_MR + half, C_MA, C_MA + half])
    return pl.pallas_call(
        body, name="merge", grid=(n_t,),
        in_specs=[pl.BlockSpec((tm, d), lambda i: (i, 0)),
                  pl.BlockSpec((tm, d), lambda i: (i, 0)),
                  pl.BlockSpec((tm, rv), lambda i: (i, 0)),
                  pl.BlockSpec((tm, rv), lambda i: (i, 0)),
                  pl.BlockSpec((tm, d), lambda i: (i, 0))]
        + [px_blk(col) for col in px_cols]
        + [pl.BlockSpec((None, 1, d), lambda i: (i // per, 0, 0)),
           resident((rv, d)), resident((d, d)), resident((d, d))],
        out_specs=(pl.BlockSpec((tm, d), lambda i: (i, 0)),
                   pl.BlockSpec((tm, rv), lambda i: (i, 0)),
                   pl.BlockSpec((tm, d), lambda i: (i, 0)),
                   ANY,
                   pl.BlockSpec((8, 128), lambda i: (0, 0)),
                   pl.BlockSpec((None, 1, d), lambda i: (i // per, 0, 0)),
                   ANY, ANY, ANY),
        out_shape=(SDS((t_lat, d), F32), SDS((t_lat, rv), BF), SDS((t_lat, d), BF),
                   SDS((px.shape[0], IN_COLS), BF),
                   SDS((8, 128), F32), SDS((n_samp, 1, d), F32),
                   SDS((rv, d), F32), SDS((d, d), F32), SDS((d, d), F32)),
        scratch_shapes=[pltpu.VMEM((rv, d), F32), pltpu.VMEM((d, d), F32), pltpu.VMEM((d, d), F32),
                        pltpu.VMEM((tm, rv), BF), pltpu.VMEM((tm, 3 * d), BF), pltpu.SemaphoreType.DMA((2,))],
        compiler_params=_cp(("arbitrary",), 56))(
            x_lat, target, o_f, o_b, o_att, *([px] * n_px), gate3, w_o_ret, w_o_att, w_out)


def _place():
    x, y, c = lax.axis_index("x"), lax.axis_index("y"), lax.axis_index("c")
    chips = [(1 - x, y), (x, 1 - y), (1 - x, 1 - y)]
    return x, y, c, chips


def _remote(src, dst, send_sem, recv_sem, to):
    return pltpu.make_async_remote_copy(src_ref=src, dst_ref=dst, send_sem=send_sem, recv_sem=recv_sem,
                                        device_id=to, device_id_type=MESH)


def _place_ids():
    x, y, c = lax.axis_index("x"), lax.axis_index("y"), lax.axis_index("c")
    me = 2 * x + y
    return jnp.stack([x, y, c, me, me, 2 * (1 - x) + y, 2 * x + 1 - y, 2 * (1 - x) + 1 - y]).astype(jnp.int32)


def _ag_comm(bufs):
    n, m = len(bufs), 3

    def half(ref, s, which):
        h = ref.shape[1] // 2
        return ref.at[s, pl.ds(which * h, h), :]

    def ici(ins, outs, ssem, rsem, base):
        x, y, c, chips = _place()
        sends, recvs = [], []
        for a in range(n):
            for j in range(m):
                k, chip = base + a * m + j, chips[j]
                mine, theirs = half(outs[a], 2 * x + y, c), half(outs[a], 2 * chip[0] + chip[1], c)
                sends.append(_remote(mine, mine, ssem.at[k], rsem.at[k], (*chip, c)))
                recvs.append(_remote(theirs, theirs, ssem.at[k], rsem.at[k], (*chip, c)))
        return sends, recvs

    def d2d(ins, outs, ssem, rsem, base):
        x, y, c, chips = _place()
        sends, recvs = [], []
        for a in range(n):
            for j in range(m):
                k, s = base + (n + a) * m + j, 2 * chips[j][0] + chips[j][1]
                sends.append(_remote(half(outs[a], s, c), half(outs[a], s, c), ssem.at[k], rsem.at[k], (x, y, 1 - c)))
                recvs.append(_remote(half(outs[a], s, 1 - c), half(outs[a], s, 1 - c), ssem.at[k], rsem.at[k],
                                     (x, y, 1 - c)))
        return sends, recvs

    return _Comm("all_gather", tuple(bufs), tuple(SDS(b.shape, b.dtype) for b in bufs), {a: a for a in range(n)},
                 2 * n * m, (ici, d2d))


def _swap_comm(grads):
    n = len(grads)

    def phase(ins, outs, ssem, rsem, base):
        x, y, c, _ = _place()
        sends = []
        for a in range(n):
            h = ins[a].shape[1] // 2
            sends.append(_remote(ins[a].at[:, pl.ds((1 - c) * h, h), :], outs[a], ssem.at[base + a],
                                 rsem.at[base + a], (x, y, 1 - c)))
        return sends, sends

    return _Comm("swap_halves", tuple(grads),
                 tuple(SDS((g.shape[0], g.shape[1] // 2, g.shape[2]), g.dtype) for g in grads), {}, n, (phase,))


def _exchange_comm(parts):
    n = len(parts)

    def phase(ins, outs, ssem, rsem, base):
        x, y, c, chips = _place()
        sends = []
        for a in range(n):
            for j, chip in enumerate(chips):
                k = base + 3 * a + j
                sends.append(_remote(ins[a].at[2 * chip[0] + chip[1]], outs[a].at[j], ssem.at[k], rsem.at[k],
                                     (*chip, c)))
        return sends, sends

    return _Comm("exchange_shards", tuple(parts), tuple(SDS((3,) + p.shape[1:], p.dtype) for p in parts), {}, 3 * n,
                 (phase,))


def _join_comm(bufs):
    n = len(bufs)

    def phase(ins, outs, ssem, rsem, base):
        x, y, c, _ = _place()
        sends, recvs = [], []
        for a in range(n):
            h = outs[a].shape[0] // 2
            mine, other = outs[a].at[pl.ds(c * h, h), :], outs[a].at[pl.ds((1 - c) * h, h), :]
            sends.append(_remote(mine, mine, ssem.at[base + a], rsem.at[base + a], (x, y, 1 - c)))
            recvs.append(_remote(other, other, ssem.at[base + a], rsem.at[base + a], (x, y, 1 - c)))
        return sends, recvs

    return _Comm("join_halves", tuple(bufs), tuple(SDS(b.shape, b.dtype) for b in bufs), {a: a for a in range(n)},
                 n, (phase,))


def _cast_place(w, ids):
    rows, cols = w.shape
    tr = min(rows, 256)

    def body(ids_ref, w_ref, o_ref):
        o_ref[...] = w_ref[...].astype(BF)

    return pl.pallas_call(
        body, name="cast_place",
        grid_spec=pltpu.PrefetchScalarGridSpec(
            num_scalar_prefetch=1, grid=(rows // tr,),
            in_specs=[pl.BlockSpec((tr, cols), lambda i, ids_ref: (i, 0))],
            out_specs=pl.BlockSpec((None, tr, cols), lambda i, ids_ref: (ids_ref[3], i, 0))),
        out_shape=SDS((N_SHARD, rows, cols), BF),
        compiler_params=_cp(("parallel",), 40))(ids, w)


def _chip_sum(g, p, ids):
    n_s, rows, cols = g.shape
    h = rows // 2
    tr = min(h, 256)
    nb = h // tr

    def body(ids_ref, g_ref, p_ref, o_ref, o16_ref):
        t = g_ref[...] + p_ref[...]
        o_ref[...] = t
        o16_ref[...] = t.astype(BF)

    out_spec = pl.BlockSpec((None, tr, cols), lambda s, i, ids_ref: (s, i, 0))
    return pl.pallas_call(
        body, name="chip_sum",
        grid_spec=pltpu.PrefetchScalarGridSpec(
            num_scalar_prefetch=1, grid=(n_s, nb),
            in_specs=[pl.BlockSpec((None, tr, cols), lambda s, i, ids_ref: (s, ids_ref[2] * nb + i, 0)),
                      pl.BlockSpec((None, tr, cols), lambda s, i, ids_ref: (s, i, 0))],
            out_specs=(out_spec, out_spec)),
        out_shape=(SDS((n_s, h, cols), g.dtype), SDS((n_s, h, cols), BF)),
        compiler_params=_cp(("parallel", "parallel"), 40))(ids, g, p)


def _shard_sum(t, q, ids):
    _, h, cols = t.shape
    tr = min(h, 256)
    nb = h // tr

    def body(ids_ref, t_ref, q_ref, o_ref):
        o_ref[...] = ((t_ref[...] + q_ref[0].astype(F32)) + q_ref[1].astype(F32)) + q_ref[2].astype(F32)

    return pl.pallas_call(
        body, name="shard_sum",
        grid_spec=pltpu.PrefetchScalarGridSpec(
            num_scalar_prefetch=1, grid=(nb,),
            in_specs=[pl.BlockSpec((None, tr, cols), lambda i, ids_ref: (ids_ref[3], i, 0)),
                      pl.BlockSpec((3, tr, cols), lambda i, ids_ref: (0, i, 0))],
            out_specs=pl.BlockSpec((tr, cols), lambda i, ids_ref: (ids_ref[2] * nb + i, 0))),
        out_shape=SDS((2 * h, cols), t.dtype),
        compiler_params=_cp(("parallel",), 40))(ids, t, q)


def _gather_small(block, n_sum):
    rows, cols = block.shape
    n_dev = 8

    def body(x_ref, o_ref, g_ref, buf, send_sems, recv_sems, local_sem):
        x, y, c, chips = _place()
        me, sibling = (x, y, c), (x, y, 1 - c)

        def slot(px_, py_, pc_):
            return buf.at[4 * px_ + 2 * py_ + pc_]

        def copy(k, who, to, src=None):
            return _remote(slot(*who) if src is None else src, slot(*who), send_sems.at[k], recv_sems.at[k], to)

        mine = pltpu.make_async_copy(x_ref, slot(*me), local_sem)
        mine.start()
        first = [copy(0, me, sibling, src=x_ref)]
        first += [copy(1 + j, me, (*chip, c), src=x_ref) for j, chip in enumerate(chips)]
        for cp in first:
            cp.start()
        passed = [copy(4 + j, (*chip, c), sibling) for j, chip in enumerate(chips)]
        for j, chip in enumerate(chips):
            copy(1 + j, (*chip, c), me).wait_recv()
            passed[j].start()
        copy(0, sibling, me).wait_recv()
        for j, chip in enumerate(chips):
            copy(4 + j, (*chip, 1 - c), me).wait_recv()
        for cp in first + passed:
            cp.wait_send()
        mine.wait()
        acc = buf[0, :, :n_sum]
        for s in range(1, n_dev):
            acc = acc + buf[s, :, :n_sum]
        o_ref[...] = acc
        for s in range(n_dev):
            g_ref[s * rows:(s + 1) * rows, :] = buf[s, :, n_sum:]

    return pl.pallas_call(
        body, name="gather_small",
        in_specs=[pl.BlockSpec(memory_space=pltpu.VMEM)],
        out_specs=(pl.BlockSpec(memory_space=pltpu.VMEM), pl.BlockSpec(memory_space=pltpu.VMEM)),
        out_shape=(SDS((rows, n_sum), F32), SDS((n_dev * rows, cols - n_sum), F32)),
        scratch_shapes=[pltpu.VMEM((n_dev, rows, cols), F32), pltpu.SemaphoreType.DMA((7,)),
                        pltpu.SemaphoreType.DMA((7,)), pltpu.SemaphoreType.DMA],
        compiler_params=_cp(has_side_effects=True))(block)


def _adam_math(w, g, m, v):
    m = ADAM_B1 * m + (1.0 - ADAM_B1) * g
    v = ADAM_B2 * v + (1.0 - ADAM_B2) * (g * g)
    m_hat = m / (1.0 - ADAM_B1 ** ADAM_STEP)
    v_hat = v / (1.0 - ADAM_B2 ** ADAM_STEP)
    delta = -ADAM_LR * (m_hat / (jnp.sqrt(v_hat) + ADAM_EPS) + ADAM_WD * w)
    return delta, m, v


def _adamw(w, g, m, v):
    rows, cols = w.shape
    tr = min(rows, 256 if cols <= 2048 else 128)

    def body(w_ref, g_ref, m_ref, v_ref, go_ref, d_ref, nm_ref, nv_ref):
        g = g_ref[...]
        go_ref[...] = g
        d_ref[...], nm_ref[...], nv_ref[...] = _adam_math(w_ref[...], g, m_ref[...], v_ref[...])

    spec = pl.BlockSpec((tr, cols), lambda i: (i, 0))
    return pl.pallas_call(
        body, name="adamw", grid=(rows // tr,), in_specs=[spec] * 4, out_specs=(spec,) * 4,
        out_shape=(SDS(w.shape, F32),) * 4, compiler_params=_cp(("parallel",), 40))(w, g, m, v)


def _adamw_small(w, g, m, v):
    def body(w_ref, g_ref, m_ref, v_ref, go_ref, d_ref, nm_ref, nv_ref):
        w = w_ref[...]
        g = g_ref[...]
        sub = lax.broadcasted_iota(jnp.int32, w.shape, 0)
        lane = lax.broadcasted_iota(jnp.int32, w.shape, 1)
        is_ret = jnp.logical_and(sub == 5, lane < 2 * RET_HEADS)
        u = jnp.exp(jnp.where(is_ret, w, -1.0) * jnp.log(2.0))
        g = jnp.where(is_ret, g * (-u * jnp.log(2.0) / (1.0 - u)), g)
        go_ref[...] = g
        d_ref[...], nm_ref[...], nv_ref[...] = _adam_math(w, g, m_ref[...], v_ref[...])

    return pl.pallas_call(body, name="adamw_small", out_shape=(SDS(w.shape, F32),) * 4)(w, g, m, v)


def _rope_tables(seq, n_samp, n_ctx_rows):
    rows = seq // GRID_W
    row = jnp.repeat(jnp.arange(rows, dtype=F32), GRID_W)
    col = jnp.tile(jnp.arange(GRID_W, dtype=F32), rows)
    half = ATT_HEAD_DIM // 2
    freqs = ROPE_THETA ** (-jnp.arange(0, half, 2, dtype=F32) / half)
    ang = jnp.concatenate([row[:, None] * freqs, col[:, None] * freqs], axis=-1)
    cos, sin = jnp.cos(ang), jnp.sin(ang)
    cos_f = jnp.repeat(cos, 2, axis=1)
    sin_s = jnp.stack([-sin, sin], axis=-1).reshape(seq, ATT_HEAD_DIM)
    cos_all = jnp.concatenate([jnp.tile(cos_f, (n_samp, 1)), jnp.ones((n_ctx_rows, ATT_HEAD_DIM), F32)], axis=0)
    sin_all = jnp.concatenate([jnp.tile(sin_s, (n_samp, 1)), jnp.zeros((n_ctx_rows, ATT_HEAD_DIM), F32)], axis=0)
    return cos_all, sin_all


def _pack_small(c_ctx, norm_w, b_ada, ret, qn, kn):
    d = D_MODEL
    row5 = jnp.concatenate([ret.reshape(-1), jnp.zeros((128 - 2 * RET_HEADS,), F32), qn.reshape(-1), kn.reshape(-1),
                            jnp.zeros((d - 384,), F32)])
    return jnp.concatenate([c_ctx.reshape(1, d), norm_w.reshape(1, d), b_ada.reshape(3, d), row5.reshape(1, d),
                            jnp.zeros((2, d), F32)], axis=0)


def _unpack_small(p):
    d = D_MODEL
    return (p[0], p[1:2], p[2:5].reshape(1, 3 * d), p[5, :2 * RET_HEADS].reshape(1, 2, RET_HEADS),
            p[5:6, 128:256], p[5:6, 256:384])


def _step(x, c, ctx, c_ctx, norm_w, b_ada, ret_log2_decay, q_norm_w, k_norm_w, loss_target, weights, ids):
    n_samp, seq, d = x.shape
    lc = ctx.shape[1]
    t_lat, t_ctx = n_samp * seq, n_samp * lc
    assert seq % TM == 0 and t_ctx == TM and t_lat % lc == 0 and seq % GRID_W == 0
    tps = seq // TM

    x_lat = x.reshape(t_lat, d)
    x_ctx = ctx.reshape(t_ctx, d)
    cvec8 = jnp.concatenate([c, c_ctx.reshape(1, d), jnp.zeros((8 - n_samp - 1, d), F32)], axis=0)
    lg = jnp.log1p(-jnp.exp2(ret_log2_decay.reshape(2, RET_HEADS)))
    cos_all, sin_all = _rope_tables(seq, n_samp, t_ctx)

    w_ada_b, w_in_b, w_or_b, w_oa_b, w_out_b = weights
    (w_ada_g,) = _run_comm(_ag_comm((w_ada_b,)))
    mod8 = _adaln_fwd(cvec8, w_ada_g, b_ada)
    mod3 = mod8[:n_samp + 1]
    shift3 = mod3[:, None, 0:d]
    scale3 = mod3[:, None, d:2 * d]
    gate3 = mod3[:, None, 2 * d:3 * d]

    hx, hxt = _norm_fwd(x_lat, x_ctx, norm_w, scale3, shift3, tps, n_samp)
    px, w_in_g = _in_proj_gather(hx, w_in_b, ids)

    states0 = _ctx_state_fwd(px, lg, n_samp, t_lat, lc)
    (o_f, o_b, saved), w_o = _ret_fwd(px, states0, lg, n_samp, seq,
                                      comm=_ag_comm((w_or_b, w_oa_b, w_out_b)))
    w_o_ret, w_o_att, w_out = (w.reshape(-1, d) for w in w_o)

    qn = _att_prep_q(px, cos_all, sin_all, q_norm_w, t_lat)
    kn, vn = _att_prep_kv(px, cos_all, sin_all, k_norm_w)
    o_att, lse = _att_fwd(qn, kn, vn, n_samp, seq, lc)

    (gx_res, do, do_att, dpx, loss8, dgate, g_w_o_ret, g_w_o_att, g_w_out) = _merge(
        x_lat, loss_target.reshape(t_lat, d), o_f, o_b, o_att, px, gate3, w_o_ret, w_o_att, w_out, tps)

    g_a = [g.reshape(N_SHARD, -1, d) for g in (g_w_o_ret, g_w_o_att, g_w_out)]
    (dpx, dkl, dkc, dvl, dvc, gqw), sib_a = _att_bwd(
        dpx, qn, kn, vn, px, o_att, lse, do_att, cos_all, sin_all, q_norm_w, n_samp, seq, lc, comm=_swap_comm(g_a))
    dpx, gkw = _att_kv_bwd(dpx, dkl, dkc, dvl, dvc, px, cos_all, sin_all, k_norm_w)
    t_a = [_chip_sum(g, p, ids) for g, p in zip(g_a, sib_a)]

    (dpx, dstates, dlg_lat), q_a = _ret_bwd(dpx, px, do, saved, lg, n_samp, seq,
                                            comm=_exchange_comm([t16 for _, t16 in t_a]))
    r_a = [_shard_sum(t, q, ids) for (t, _), q in zip(t_a, q_a)]
    dpx, dlg_ctx = _ctx_state_bwd(dpx, px, dstates, lg, n_samp, t_lat, lc)
    dpx = _zero_ctx_tail(dpx, t_lat)

    n_tiles = dpx.shape[0] // _big_rows(dpx.shape[0])
    g_b = _gw_in(hxt, dpx)
    dhx, (sib_b, *r_a) = _dhx(dpx, w_in_g, 0, 1, None, _join_comms(_swap_comm([g_b]), _join_comm(r_a)))
    t_b, t16_b = _chip_sum(g_b, sib_b, ids)
    dhx, (q_b,) = _dhx(dpx, w_in_g, 1, n_tiles - 1, dhx, _exchange_comm([t16_b]))
    r_b_half = _shard_sum(t_b, q_b, ids)
    grad_x, dshift, dscale, g_norm_w = _norm_bwd(x_lat, x_ctx, dhx, gx_res, norm_w, scale3, tps, n_samp)

    dgate_all = jnp.concatenate([dgate, jnp.zeros((1, 1, d), F32)], axis=0)
    dmod3 = jnp.concatenate([dshift, dscale, dgate_all], axis=2).reshape(n_samp + 1, 3 * d)
    dmod8 = jnp.concatenate([dmod3, jnp.zeros((8 - n_samp - 1, 3 * d), F32)], axis=0)
    g_lg = (jnp.sum(dlg_lat[:, :, 0], axis=0).reshape(2, RET_HEADS)
            + jnp.stack([jnp.sum(dlg_ctx[:, :, 0, 0], axis=0), jnp.sum(dlg_ctx[:, :, 1, 0], axis=0)], axis=0))
    g_qw = jnp.sum(gqw, axis=(0, 1, 2))
    zero = jnp.zeros((d,), F32)

    local = _pack_small(zero, g_norm_w, jnp.zeros((3 * d,), F32), g_lg, g_qw, gkw).at[6, 0].set(loss8[0, 0])
    small_sum, gathered = _gather_small(jnp.concatenate([local, cvec8, dmod8], axis=1), d)
    (g_w_ada, g_b_ada, dc_all), (r_b,) = _adaln_bwd(gathered[:, :d], gathered[:, d:], w_ada_g,
                                                     comm=_join_comm([r_b_half]))
    dc_ctx = jnp.sum(dc_all.reshape(-1, 8, d)[:, n_samp], axis=0)
    small = small_sum + _pack_small(dc_ctx, zero, g_b_ada, jnp.zeros((2, RET_HEADS), F32), zero[:128], zero[:128])
    r_c = lax.dynamic_index_in_dim(g_w_ada, ids[3], 0, keepdims=False)
    return small[6, 0], grad_x.reshape(n_samp, seq, d), (r_c, r_b, *r_a), small


def kernel(x, c, ctx, c_ctx, norm_w, w_ada, b_ada, w_in, ret_log2_decay, q_norm_w, k_norm_w, w_o_ret, w_o_att, w_out, loss_target, m_c_ctx, m_norm_w, m_w_ada, m_b_ada, m_w_in, m_ret_log2_decay, m_q_norm_w, m_k_norm_w, m_w_o_ret, m_w_o_att, m_w_out, v_c_ctx, v_norm_w, v_w_ada, v_b_ada, v_w_in, v_ret_log2_decay, v_q_norm_w, v_k_norm_w, v_w_o_ret, v_w_o_att, v_w_out):
    big_w = (w_ada[0], w_in[0], w_o_ret[0], w_o_att[0], w_out[0])
    big_m = (m_w_ada[0], m_w_in[0], m_w_o_ret[0], m_w_o_att[0], m_w_out[0])
    big_v = (v_w_ada[0], v_w_in[0], v_w_o_ret[0], v_w_o_att[0], v_w_out[0])

    ids = _place_ids()
    loss, grad_x, big_grad, small_grad_in = _step(
        x, c, ctx, c_ctx, norm_w[0:1], b_ada[0:1], ret_log2_decay[0], q_norm_w[0:1], k_norm_w[0:1], loss_target,
        tuple(_cast_place(w, ids) for w in big_w), ids)
    small_w = _pack_small(c_ctx, norm_w, b_ada, ret_log2_decay, q_norm_w, k_norm_w)
    small_m = _pack_small(m_c_ctx, m_norm_w, m_b_ada, m_ret_log2_decay, m_q_norm_w, m_k_norm_w)
    small_v = _pack_small(v_c_ctx, v_norm_w, v_b_ada, v_ret_log2_decay, v_q_norm_w, v_k_norm_w)
    small_grad, small_delta, small_nm, small_nv = _adamw_small(small_w, small_grad_in, small_m, small_v)

    big_g, big_delta, big_nm, big_nv = [], [], [], []
    for w, g, m, v in zip(big_w, big_grad, big_m, big_v):
        go, dlt, nm, nv = _adamw(w, g, m, v)
        big_g.append(go[None])
        big_delta.append(dlt[None])
        big_nm.append(nm[None])
        big_nv.append(nv[None])
    big_grad = big_g

    def order(small_packed, big):
        s = _unpack_small(small_packed)
        return (s[0], s[1], big[0], s[2], big[1], s[3], s[4], s[5], big[2], big[3], big[4])

    return (loss, grad_x, *order(small_grad, big_grad), *order(small_delta, big_delta),
            *order(small_nm, big_nm), *order(small_nv, big_nv))
```

```python
import functools
from typing import NamedTuple

import jax
import jax.numpy as jnp
from jax import lax
from jax.experimental import pallas as pl
from jax.experimental.pallas import tpu as pltpu

F32 = jnp.float32
BF = jnp.bfloat16
SDS = jax.ShapeDtypeStruct
MESH = pl.DeviceIdType.MESH
ANY = pl.BlockSpec(memory_space=pl.ANY)
SMEM = pl.BlockSpec(memory_space=pltpu.SMEM)

D_MODEL = 1024
GRID_W = 64
RET_HEADS = 4
RET_DK = 256
RET_DV = 512
RET_CHUNK = 128
ATT_HEADS = 8
ATT_KV_HEADS = 2
ATT_REP = ATT_HEADS // ATT_KV_HEADS
ATT_HEAD_DIM = 128
ROPE_THETA = 10000.0
NORM_EPS = 1e-6
IN_COLS = 10752
KV_COLS = 3584
C_RK, C_RV, C_AK, C_AV, C_RQ, C_RG, C_AQ, C_AG, C_MR, C_MA = 0, 1024, 3072, 3328, 3584, 4608, 6656, 7680, 8704, 9728
N_SHARD = 4
ADA_W = 3 * D_MODEL // N_SHARD
IN_W = IN_COLS // N_SHARD
IN_BLK = IN_W
BPS = IN_W // IN_BLK
N_IN_BLK = IN_COLS // IN_BLK
TM = 512
ATT_TQ = 512
ADAM_LR, ADAM_B1, ADAM_B2, ADAM_EPS, ADAM_WD, ADAM_STEP = 0.001, 0.9, 0.999, 1e-08, 0.01, 10
MIB = 1024 * 1024


def _cp(sem=None, vmem_mb=None, **kw):
    if sem is not None:
        kw["dimension_semantics"] = sem
    if vmem_mb is not None:
        kw["vmem_limit_bytes"] = vmem_mb * MIB
    return pltpu.CompilerParams(**kw)


def _dot(a, b, ca=1, cb=0):
    return lax.dot_general(a.astype(BF), b.astype(BF), (((ca,), (cb,)), ((), ())), preferred_element_type=F32)


def _sigmoid(x):
    return 0.5 * jnp.tanh(0.5 * x) + 0.5


def _sum_all(x):
    return jnp.sum(jnp.sum(x, axis=1, keepdims=True), axis=0, keepdims=True)


def _swap_pairs(x):
    ax = x.ndim - 1
    lane = lax.broadcasted_iota(jnp.int32, x.shape, ax)
    nxt = pltpu.roll(x, x.shape[ax] - 1, ax)
    prv = pltpu.roll(x, 1, ax)
    return jnp.where(lane % 2 == 0, nxt, prv)


def _rms(x):
    return lax.rsqrt(jnp.mean(x * x, axis=-1, keepdims=True) + NORM_EPS)


def _rms_bwd(dxh, xh, r):
    return r * (dxh - xh * jnp.mean(dxh * xh, axis=-1, keepdims=True))


class _Comm(NamedTuple):
    name: str
    ins: tuple
    out_shapes: tuple
    aliases: dict
    n_sems: int
    phases: tuple


def _join_comms(*comms):
    offs, i_off, o_off, s_off = [], 0, 0, 0
    for cm in comms:
        offs.append((i_off, o_off, s_off))
        i_off, o_off, s_off = i_off + len(cm.ins), o_off + len(cm.out_shapes), s_off + cm.n_sems

    def phase(k):
        def run(ins, outs, ssem, rsem, base):
            sends, recvs = [], []
            for cm, (io, oo, so) in zip(comms, offs):
                if k < len(cm.phases):
                    s, r = cm.phases[k](ins[io:io + len(cm.ins)], outs[oo:oo + len(cm.out_shapes)], ssem, rsem,
                                        base + so)
                    sends += s
                    recvs += r
            return sends, recvs
        return run

    aliases = {}
    for cm, (io, oo, _) in zip(comms, offs):
        aliases.update({io + a: oo + b for a, b in cm.aliases.items()})
    return _Comm("+".join(cm.name for cm in comms), sum((cm.ins for cm in comms), ()),
                 sum((cm.out_shapes for cm in comms), ()), aliases, s_off,
                 tuple(phase(k) for k in range(max(len(cm.phases) for cm in comms))))


def _run_phases(comm, cins, couts, ssem, rsem, first_started):
    for k, phase in enumerate(comm.phases):
        sends, recvs = phase(cins, couts, ssem, rsem, 0)
        if k > 0 or not first_started:
            for cp in sends:
                cp.start()
        for cp in recvs:
            cp.wait_recv()
        for cp in sends:
            cp.wait_send()


def _call(body, args, comm, *, name, grid, in_specs, out_specs, out_shape, scratch_shapes=(),
          compiler_params, aliases=None):
    n_in, n_out, n_sc = len(in_specs), len(out_specs), len(scratch_shapes)
    n_ci, n_co = len(comm.ins), len(comm.out_shapes)
    io_alias = dict(aliases or {})
    io_alias.update({n_in + a: n_out + b for a, b in comm.aliases.items()})

    def kernel_body(*refs):
        ins, cins = refs[:n_in], refs[n_in:n_in + n_ci]
        outs = refs[n_in + n_ci:n_in + n_ci + n_out]
        couts = refs[n_in + n_ci + n_out:n_in + n_ci + n_out + n_co]
        scratch = refs[n_in + n_ci + n_out + n_co:n_in + n_ci + n_out + n_co + n_sc]
        ssem, rsem = refs[-2:]
        first = functools.reduce(jnp.logical_and, [pl.program_id(k) == 0 for k in range(len(grid))])
        last = functools.reduce(jnp.logical_and, [pl.program_id(k) == grid[k] - 1 for k in range(len(grid))])

        @pl.when(first)
        def _():
            for cp in comm.phases[0](cins, couts, ssem, rsem, 0)[0]:
                cp.start()

        body(*ins, *outs, *scratch)

        @pl.when(last)
        def _():
            _run_phases(comm, cins, couts, ssem, rsem, True)

    res = pl.pallas_call(
        kernel_body, name=name + "+" + comm.name, grid=grid, in_specs=list(in_specs) + [ANY] * n_ci,
        out_specs=tuple(out_specs) + tuple([ANY] * n_co), out_shape=tuple(out_shape) + tuple(comm.out_shapes),
        scratch_shapes=list(scratch_shapes) + [pltpu.SemaphoreType.DMA((comm.n_sems,)),
                                               pltpu.SemaphoreType.DMA((comm.n_sems,))],
        input_output_aliases=io_alias, compiler_params=compiler_params)(*args, *comm.ins)
    return tuple(res[:n_out]), tuple(res[n_out:])


def _run_comm(comm):
    n_ci, n_co = len(comm.ins), len(comm.out_shapes)

    def body(*refs):
        _run_phases(comm, refs[:n_ci], refs[n_ci:n_ci + n_co], refs[-2], refs[-1], False)

    return pl.pallas_call(
        body, name=comm.name, in_specs=[ANY] * n_ci, out_specs=tuple([ANY] * n_co), out_shape=tuple(comm.out_shapes),
        input_output_aliases=dict(comm.aliases),
        scratch_shapes=[pltpu.SemaphoreType.DMA((comm.n_sems,)), pltpu.SemaphoreType.DMA((comm.n_sems,))],
        compiler_params=_cp(has_side_effects=True))(*comm.ins)


def _adaln_fwd(cvec8, w_ada_g, b_ada):
    def body(c_ref, w_ref, b_ref, o_ref):
        cv = c_ref[...]
        sc = (cv * _sigmoid(cv)).astype(BF)
        for s in range(N_SHARD):
            cols = slice(s * ADA_W, (s + 1) * ADA_W)
            o_ref[:, cols] = jnp.dot(sc, w_ref[s], preferred_element_type=F32) + b_ref[:, cols]

    return pl.pallas_call(body, out_shape=SDS((8, 3 * D_MODEL), F32), name="adaln_fwd",
                          compiler_params=_cp(vmem_mb=32))(cvec8, w_ada_g, b_ada)


def _adaln_bwd(cvec, dmod, w_ada_g, comm):
    n_rows = cvec.shape[0]
    def body(c_ref, d_ref, w_ref, gw_ref, gb_ref, dc_ref):
        cv = c_ref[...]
        sg = _sigmoid(cv)
        sc = cv * sg
        dm = d_ref[...]
        gb_ref[...] = jnp.sum(dm, axis=0, keepdims=True)
        dsc = jnp.zeros(cv.shape, F32)
        for s in range(N_SHARD):
            cols = slice(s * ADA_W, (s + 1) * ADA_W)
            gw_ref[s] = _dot(sc, dm[:, cols], 0, 0)
            dsc = dsc + _dot(dm[:, cols], w_ref[s], 1, 1)
        dc_ref[...] = dsc * (sg * (1.0 + cv * (1.0 - sg)))

    def whole(shape):
        return pl.BlockSpec(shape, lambda i: (0,) * len(shape))

    shapes = ((N_SHARD, D_MODEL, ADA_W), (1, 3 * D_MODEL), (n_rows, D_MODEL))
    return _call(body, [cvec, dmod, w_ada_g], comm, name="adaln_bwd", grid=(1,),
                 in_specs=[whole(cvec.shape), whole(dmod.shape), whole(w_ada_g.shape)],
                 out_specs=tuple(whole(s) for s in shapes), out_shape=tuple(SDS(s, F32) for s in shapes),
                 compiler_params=_cp(("arbitrary",), 56))


def _big_rows(rows):
    return 1536 if rows % 1536 == 0 else TM


def _norm_fwd(x_lat, x_ctx, norm_w, scale3, shift3, tiles_per_sample, n_samp):
    n_lat = x_lat.shape[0] // TM
    rows = x_lat.shape[0] + x_ctx.shape[0]

    def samp(i):
        return jnp.minimum(i // tiles_per_sample, n_samp)

    def body(x_ref, c_ref, nw_ref, sc_ref, sh_ref, hx_ref, hxt_ref):
        x = jnp.where(pl.program_id(0) < n_lat, x_ref[...], c_ref[...])
        h = x * _rms(x) * nw_ref[...] * (1.0 + sc_ref[...]) + sh_ref[...]
        hx_ref[...] = h.astype(BF)
        hxt_ref[...] = h.T.astype(BF)

    return pl.pallas_call(
        body, name="norm_fwd", grid=(rows // TM,),
        in_specs=[pl.BlockSpec((TM, D_MODEL), lambda i: (jnp.minimum(i, n_lat - 1), 0)),
                  pl.BlockSpec((TM, D_MODEL), lambda i: (jnp.maximum(i - n_lat, 0), 0)),
                  pl.BlockSpec((1, D_MODEL), lambda i: (0, 0)),
                  pl.BlockSpec((None, 1, D_MODEL), lambda i: (samp(i), 0, 0)),
                  pl.BlockSpec((None, 1, D_MODEL), lambda i: (samp(i), 0, 0))],
        out_specs=(pl.BlockSpec((TM, D_MODEL), lambda i: (i, 0)),
                   pl.BlockSpec((D_MODEL, TM), lambda i: (0, i))),
        out_shape=(SDS((rows, D_MODEL), BF), SDS((D_MODEL, rows), BF)),
        compiler_params=_cp(("parallel",), 40))(x_lat, x_ctx, norm_w, scale3, shift3)


def _norm_bwd(x_lat, x_ctx, dhx, gx_res, norm_w, scale3, tiles_per_sample, n_samp):
    rows = x_lat.shape[0] + x_ctx.shape[0]
    n_lat = tiles_per_sample * n_samp

    def samp(i):
        return jnp.minimum(i // tiles_per_sample, n_samp)

    def lat(i):
        return jnp.minimum(i, n_lat - 1)

    def body(x_ref, c_ref, dh_ref, gr_ref, nw_ref, sc_ref, gx_ref, dsh_ref, dsc_ref, dnw_ref):
        i = pl.program_id(0)
        x = jnp.where(i < n_lat, x_ref[...], c_ref[...])
        r = _rms(x)
        xh = x * r
        nw = nw_ref[...]
        dh = dh_ref[...]
        first = jnp.logical_or(i % tiles_per_sample == 0, i >= n_lat)

        @pl.when(first)
        def _():
            dsh_ref[...] = jnp.zeros_like(dsh_ref)
            dsc_ref[...] = jnp.zeros_like(dsc_ref)

        @pl.when(i == 0)
        def _():
            dnw_ref[...] = jnp.zeros_like(dnw_ref)

        dsh_ref[...] += jnp.sum(dh, axis=0, keepdims=True)
        dsc_ref[...] += jnp.sum(dh * (xh * nw), axis=0, keepdims=True)
        du = dh * (1.0 + sc_ref[...])
        dnw_ref[...] += jnp.sum(du * xh, axis=0, keepdims=True)

        @pl.when(i < n_lat)
        def _():
            gx_ref[...] = gr_ref[...] + _rms_bwd(du * nw, xh, r)

    return pl.pallas_call(
        body, name="norm_bwd", grid=(rows // TM,),
        in_specs=[pl.BlockSpec((TM, D_MODEL), lambda i: (lat(i), 0)),
                  pl.BlockSpec((TM, D_MODEL), lambda i: (jnp.maximum(i - n_lat, 0), 0)),
                  pl.BlockSpec((TM, D_MODEL), lambda i: (i, 0)),
                  pl.BlockSpec((TM, D_MODEL), lambda i: (lat(i), 0)),
                  pl.BlockSpec((1, D_MODEL), lambda i: (0, 0)),
                  pl.BlockSpec((None, 1, D_MODEL), lambda i: (samp(i), 0, 0))],
        out_specs=(pl.BlockSpec((TM, D_MODEL), lambda i: (lat(i), 0)),
                   pl.BlockSpec((None, 1, D_MODEL), lambda i: (samp(i), 0, 0)),
                   pl.BlockSpec((None, 1, D_MODEL), lambda i: (samp(i), 0, 0)),
                   pl.BlockSpec((1, D_MODEL), lambda i: (0, 0))),
        out_shape=(SDS((n_lat * TM, D_MODEL), F32), SDS((n_samp + 1, 1, D_MODEL), F32),
                   SDS((n_samp + 1, 1, D_MODEL), F32), SDS((1, D_MODEL), F32)),
        compiler_params=_cp(("arbitrary",), 40))(x_lat, x_ctx, dhx, gx_res, norm_w, scale3)


def _in_proj_gather(hx, w_buf, ids):
    rows = hx.shape[0]
    tb = _big_rows(rows)
    n_i = rows // tb
    hrows = D_MODEL // 2

    def body(ids_ref, h_ref, w_in_hbm, px_ref, w_hbm, wv, lsem, ssem, rsem):
        j, i = pl.program_id(0), pl.program_id(1)
        x, y, c, chips = _place()
        sibling = (x, y, 1 - c)

        def half(s, which):
            return w_hbm.at[s, pl.ds(which * hrows, hrows), :]

        def over_ici(rel):
            chip = chips[rel]
            mine, theirs = half(2 * x + y, c), half(2 * chip[0] + chip[1], c)
            return (_remote(mine, mine, ssem.at[rel], rsem.at[rel], (*chip, c)),
                    _remote(theirs, theirs, ssem.at[rel], rsem.at[rel], (*chip, c)))

        def over_d2d(rel):
            s = 2 * chips[rel][0] + chips[rel][1]
            return (_remote(half(s, c), half(s, c), ssem.at[3 + rel], rsem.at[3 + rel], sibling),
                    _remote(half(s, 1 - c), half(s, 1 - c), ssem.at[3 + rel], rsem.at[3 + rel], sibling))

        first_row_tile = i == 0

        @pl.when(jnp.logical_and(j == 0, first_row_tile))
        def _():
            over_ici(0)[0].start()
            over_ici(1)[0].start()

        for rel in range(3):
            @pl.when(jnp.logical_and(j == rel + 1, first_row_tile))
            def _(rel=rel):
                over_ici(rel)[1].wait_recv()
                passed, landing = over_d2d(rel)
                passed.start()
                if rel == 0:
                    over_ici(2)[0].start()
                landing.wait_recv()

        @pl.when(first_row_tile)
        def _():
            cp = pltpu.make_async_copy(w_hbm.at[ids_ref[4 + j]], wv, lsem)
            cp.start()
            cp.wait()

        px_ref[...] = jnp.dot(h_ref[...], wv[...], preferred_element_type=F32).astype(BF)

        @pl.when(jnp.logical_and(j == N_SHARD - 1, i == n_i - 1))
        def _():
            for rel in range(3):
                over_ici(rel)[0].wait_send()
                over_d2d(rel)[0].wait_send()

    return pl.pallas_call(
        body, name="in_proj_gather", input_output_aliases={2: 1},
        grid_spec=pltpu.PrefetchScalarGridSpec(
            num_scalar_prefetch=1, grid=(N_SHARD, n_i),
            in_specs=[pl.BlockSpec((tb, D_MODEL), lambda j, i, ids_ref: (i, 0)), ANY],
            out_specs=(pl.BlockSpec((tb, IN_W), lambda j, i, ids_ref: (i, ids_ref[4 + j])), ANY),
            scratch_shapes=[pltpu.VMEM((D_MODEL, IN_W), BF), pltpu.SemaphoreType.DMA,
                            pltpu.SemaphoreType.DMA((6,)), pltpu.SemaphoreType.DMA((6,))]),
        out_shape=(SDS((rows, IN_COLS), BF), SDS(w_buf.shape, w_buf.dtype)),
        compiler_params=_cp(("arbitrary", "arbitrary"), 56))(ids, hx, w_buf)


def _gw_in(hxt, dpx_all):
    rows = dpx_all.shape[0]
    tb = _big_rows(rows)

    def body(h_ref, d_ref, o_ref):
        @pl.when(pl.program_id(1) == 0)
        def _():
            o_ref[...] = jnp.zeros_like(o_ref)

        o_ref[...] += jnp.dot(h_ref[...], d_ref[...], preferred_element_type=F32)

    return pl.pallas_call(
        body, name="gw_in", grid=(N_IN_BLK, rows // tb),
        in_specs=[pl.BlockSpec((D_MODEL, tb), lambda j, i: (0, i)),
                  pl.BlockSpec((tb, IN_BLK), lambda j, i: (i, j))],
        out_specs=pl.BlockSpec((None, D_MODEL, IN_BLK), lambda j, i: (j // BPS, 0, j % BPS)),
        out_shape=SDS((N_SHARD, D_MODEL, IN_W), F32),
        compiler_params=_cp(("arbitrary", "arbitrary"), 56))(hxt, dpx_all)


def _dhx(dpx_all, w_in_g, tile0, n_tiles, dhx, comm):
    rows = dpx_all.shape[0]
    tb = _big_rows(rows)

    def body(d_ref, w_ref, *rest):
        o_ref = rest[-1]

        @pl.when(pl.program_id(1) == 0)
        def _():
            o_ref[...] = jnp.zeros_like(o_ref)

        o_ref[...] += lax.dot_general(d_ref[...], w_ref[...], (((1,), (1,)), ((), ())), preferred_element_type=F32)

    args, in_specs, aliases = [dpx_all, w_in_g], [
        pl.BlockSpec((tb, IN_BLK), lambda i, j: (tile0 + i, j)),
        pl.BlockSpec((None, D_MODEL, IN_BLK), lambda i, j: (j // BPS, 0, j % BPS))], None
    if dhx is not None:
        args, in_specs, aliases = args + [dhx], in_specs + [ANY], {2: 0}
    (out,), got = _call(body, args, comm, name="dhx", grid=(n_tiles, N_IN_BLK), in_specs=in_specs,
                        out_specs=(pl.BlockSpec((tb, D_MODEL), lambda i, j: (tile0 + i, 0)),),
                        out_shape=(SDS((rows, D_MODEL), F32),), aliases=aliases,
                        compiler_params=_cp(("arbitrary", "arbitrary"), 56))
    return out, got


def _decays(lgv, d):
    c = RET_CHUNK
    ii = lax.broadcasted_iota(jnp.int32, (c, 1), 0).astype(F32)
    jj = lax.broadcasted_iota(jnp.int32, (1, c), 1).astype(F32)
    a_i = jnp.where(d == 0, ii, c - 1.0 - ii)
    a_j = jnp.where(d == 0, jj, c - 1.0 - jj)
    rel = a_i - a_j
    mask = jnp.where(rel >= 0, jnp.exp(lgv * jnp.maximum(rel, 0.0)), 0.0)
    qd = jnp.exp(lgv * (a_i + 1.0))
    kd = jnp.exp(lgv * (c - 1.0 - a_i))
    gc = jnp.exp(jnp.full((1, 1), lgv * c, F32))
    return a_i, rel, mask, qd, kd, gc


def _ctx_state_fwd(px, lg, n_samp, t_lat, lc):
    rb = t_lat // lc

    def body(lg_ref, k_ref, v_ref, o_ref):
        h = pl.program_id(1)
        k = k_ref[...].astype(F32) * (RET_DK ** -0.5)
        v = v_ref[...]
        pos = lax.broadcasted_iota(jnp.int32, (lc, 1), 0).astype(F32)
        o_ref[0] = _dot(k * jnp.exp(lg_ref[0, h] * (lc - 1.0 - pos)), v, 0, 0)
        o_ref[1] = _dot(k * jnp.exp(lg_ref[1, h] * pos), v, 0, 0)

    return pl.pallas_call(
        body, name="ctx_state_fwd", grid=(n_samp, RET_HEADS),
        in_specs=[SMEM,
                  pl.BlockSpec((lc, RET_DK), lambda b, h: (rb + b, C_RK // RET_DK + h)),
                  pl.BlockSpec((lc, RET_DV), lambda b, h: (rb + b, C_RV // RET_DV + h))],
        out_specs=pl.BlockSpec((None, 2, None, RET_DK, RET_DV), lambda b, h: (b, 0, h, 0, 0)),
        out_shape=SDS((n_samp, 2, RET_HEADS, RET_DK, RET_DV), F32),
        compiler_params=_cp(("parallel", "parallel")))(lg, px, px)


def _ctx_state_bwd(dpx, px, dstates, lg, n_samp, t_lat, lc):
    rb = t_lat // lc
    kspec = pl.BlockSpec((lc, RET_DK), lambda b, h: (rb + b, C_RK // RET_DK + h))
    vspec = pl.BlockSpec((lc, RET_DV), lambda b, h: (rb + b, C_RV // RET_DV + h))
    sspec = pl.BlockSpec((None, 2, None, RET_DK, RET_DV), lambda b, h: (b, 0, h, 0, 0))

    def weights(lg_ref, h):
        pos = lax.broadcasted_iota(jnp.int32, (lc, 1), 0).astype(F32)
        e_f = lc - 1.0 - pos
        return pos, e_f, jnp.exp(lg_ref[0, h] * e_f), jnp.exp(lg_ref[1, h] * pos)

    def k_body(lg_ref, dpx_hbm, k_ref, v_ref, ds_ref, dk_ref, dlg_ref):
        pos, e_f, w_f, w_b = weights(lg_ref, pl.program_id(1))
        k = k_ref[...].astype(F32) * (RET_DK ** -0.5)
        y_f = _dot(v_ref[...], ds_ref[0], 1, 1) * w_f
        y_b = _dot(v_ref[...], ds_ref[1], 1, 1) * w_b
        dk_ref[...] = ((y_f + y_b) * (RET_DK ** -0.5)).astype(BF)
        t_f = _sum_all(e_f * k * y_f)
        t_b = _sum_all(pos * k * y_b)
        sub = lax.broadcasted_iota(jnp.int32, (8, 128), 0)
        dlg_ref[...] = jnp.where(sub == 0, t_f, jnp.where(sub == 1, t_b, 0.0))

    def v_body(lg_ref, dpx_hbm, k_ref, ds_ref, dv_ref):
        _, _, w_f, w_b = weights(lg_ref, pl.program_id(1))
        k = k_ref[...].astype(F32) * (RET_DK ** -0.5)
        dv_ref[...] = (_dot(k * w_f, ds_ref[0]) + _dot(k * w_b, ds_ref[1])).astype(BF)

    dpx, dlg = pl.pallas_call(
        k_body, name="ctx_state_bwd_k", grid=(n_samp, RET_HEADS), input_output_aliases={1: 0},
        in_specs=[SMEM, ANY, kspec, vspec, sspec],
        out_specs=(kspec, pl.BlockSpec((None, None, 8, 128), lambda b, h: (b, h, 0, 0))),
        out_shape=(SDS(dpx.shape, dpx.dtype), SDS((n_samp, RET_HEADS, 8, 128), F32)),
        compiler_params=_cp(("parallel", "parallel")))(lg, dpx, px, px, dstates)
    dpx = pl.pallas_call(
        v_body, name="ctx_state_bwd_v", grid=(n_samp, RET_HEADS), input_output_aliases={1: 0},
        in_specs=[SMEM, ANY, kspec, sspec], out_specs=vspec, out_shape=SDS(dpx.shape, dpx.dtype),
        compiler_params=_cp(("parallel", "parallel")))(lg, dpx, px, dstates)
    return dpx, dlg


def _zero_ctx_tail(dpx, t_lat):
    wb = 512
    n_ctx = (dpx.shape[0] - t_lat) // TM

    def body(dpx_hbm, o_ref):
        o_ref[...] = jnp.zeros_like(o_ref)

    return pl.pallas_call(
        body, name="zero_ctx_tail", grid=(n_ctx, (IN_COLS - KV_COLS) // wb), input_output_aliases={0: 0},
        in_specs=[ANY], out_specs=pl.BlockSpec((TM, wb), lambda i, j: (t_lat // TM + i, KV_COLS // wb + j)),
        out_shape=SDS(dpx.shape, dpx.dtype),
        compiler_params=_cp(("parallel", "parallel")))(dpx)


def _ret_specs(row_f, row_b):
    c = RET_CHUNK
    wq = RET_HEADS * RET_DK // 2
    wv = RET_HEADS * RET_DV // 2
    specs = []
    for row in (row_f, row_b):
        specs += [pl.BlockSpec((c, wq), lambda b, n, row=row: (row(b, n), C_RQ // wq)),
                  pl.BlockSpec((c, wq), lambda b, n, row=row: (row(b, n), C_RQ // wq + 1)),
                  pl.BlockSpec((c, 2 * wq), lambda b, n, row=row: (row(b, n), C_RK // (2 * wq))),
                  pl.BlockSpec((c, wv), lambda b, n, row=row: (row(b, n), C_RV // wv)),
                  pl.BlockSpec((c, wv), lambda b, n, row=row: (row(b, n), C_RV // wv + 1))]
    return specs


def _ret_head(refs, h):
    q0, q1, k_ref, v0, v1 = refs
    hh = h % 2
    q = (q0, q1)[h // 2][:, hh * RET_DK:(hh + 1) * RET_DK].astype(F32)
    k = k_ref[:, h * RET_DK:(h + 1) * RET_DK].astype(F32) * (RET_DK ** -0.5)
    v = (v0, v1)[h // 2][:, hh * RET_DV:(hh + 1) * RET_DV]
    return q, k, v


def _ret_fwd(px, states0, lg, n_samp, seq, comm):
    c = RET_CHUNK
    nc = seq // c
    t_lat = n_samp * seq
    wo = RET_HEADS * RET_DV

    def row_f(b, n):
        return b * nc + n

    def row_b(b, n):
        return b * nc + nc - 1 - n

    def body(lg_ref, *refs):
        ins, (s0_ref, of_ref, ob_ref, st_ref, s_s) = refs[:10], refs[10:]

        @pl.when(pl.program_id(1) == 0)
        def _():
            s_s[...] = s0_ref[...]

        for d, o_ref in ((0, of_ref), (1, ob_ref)):
            for h in range(RET_HEADS):
                _, _, mask, qd, kd, gc = _decays(lg_ref[d, h], d)
                q, k, v = _ret_head(ins[5 * d:5 * d + 5], h)
                s = s_s[d, h]
                st_ref[h, d] = s.astype(BF)
                sc = _dot(q, k, 1, 1) * mask
                o_ref[:, h * RET_DV:(h + 1) * RET_DV] = (_dot(sc, v) + _dot(q * qd, s)).astype(BF)
                s_s[d, h] = s * gc + _dot(k * kd, v, 0, 0)

    return _call(
        body, [lg] + [px] * 10 + [states0], comm, name="ret_fwd", grid=(n_samp, nc),
        in_specs=[SMEM] + _ret_specs(row_f, row_b) + [
            pl.BlockSpec((None, 2, RET_HEADS, RET_DK, RET_DV), lambda b, n: (b, 0, 0, 0, 0))],
        out_specs=(pl.BlockSpec((c, wo), lambda b, n: (row_f(b, n), 0)),
                   pl.BlockSpec((c, wo), lambda b, n: (row_b(b, n), 0)),
                   pl.BlockSpec((None, RET_HEADS, 2, None, RET_DK, RET_DV), lambda b, n: (b, 0, 0, n, 0, 0))),
        out_shape=(SDS((t_lat, wo), BF), SDS((t_lat, wo), BF),
                   SDS((n_samp, RET_HEADS, 2, nc, RET_DK, RET_DV), BF)),
        scratch_shapes=[pltpu.VMEM((2, RET_HEADS, RET_DK, RET_DV), F32)],
        compiler_params=_cp(("arbitrary", "arbitrary"), 48))


def _ret_bwd(dpx, px, do, saved, lg, n_samp, seq, comm):
    c = RET_CHUNK
    nc = seq // c
    assert nc % 2 == 0
    wq, wo = RET_HEADS * RET_DK, RET_HEADS * RET_DV

    def row_f(b, n):
        return b * nc + nc - 1 - n

    def row_b(b, n):
        return b * nc + n

    def body(lg_ref, *refs):
        ins = refs[:10]
        (dof_ref, dob_ref, st_ref, dpx_in, dpx_hbm, ds0_ref, dlg_ref,
         ds_s, acc_s, sq_s, sk_s, sv_s, sems) = refs[10:]
        b, n = pl.program_id(0), pl.program_id(1)
        second = n >= nc // 2
        chunks = (nc - 1 - n, n)

        def parked(ch):
            return pl.ds(pl.multiple_of(ch * c, c), c)

        def flush():
            cps = []
            for d, ch in enumerate(chunks):
                rows = pl.ds(pl.multiple_of((b * nc + ch) * c, c), c)
                cps += [pltpu.make_async_copy(sq_s.at[parked(ch), :], dpx_hbm.at[rows, pl.ds(C_RQ, wq)], sems.at[3 * d]),
                        pltpu.make_async_copy(sk_s.at[parked(ch), :], dpx_hbm.at[rows, pl.ds(C_RK, wq)],
                                              sems.at[3 * d + 1]),
                        pltpu.make_async_copy(sv_s.at[parked(ch), :], dpx_hbm.at[rows, pl.ds(C_RV, wo)],
                                              sems.at[3 * d + 2])]
            return cps

        @pl.when(jnp.logical_or(n > nc // 2, jnp.logical_and(n == 0, b > 0)))
        def _():
            for cp in flush():
                cp.wait()

        @pl.when(n == 0)
        def _():
            ds_s[...] = jnp.zeros_like(ds_s)
            acc_s[...] = jnp.zeros_like(acc_s)

        def chains(first_visit):
            for d, do_ref in enumerate((dof_ref, dob_ref)):
                rows = parked(chunks[d])
                for h in range(RET_HEADS):
                    a_i, rel, mask, qd, kd, gc = _decays(lg_ref[d, h], d)
                    q, k, v = _ret_head(ins[5 * d:5 * d + 5], h)
                    qb, kb, vb = q.astype(BF), k.astype(BF), v.astype(BF)
                    cq, cv = slice(h * RET_DK, (h + 1) * RET_DK), slice(h * RET_DV, (h + 1) * RET_DV)
                    dob = do_ref[:, cv].astype(BF)
                    sb = st_ref[h, d]
                    ds = ds_s[d, h]
                    dsb = ds.astype(BF)
                    raw = _dot(qb, kb, 1, 1)
                    sc = raw * mask
                    dsc = _dot(dob, vb, 1, 1) * mask
                    dscb = dsc.astype(BF)
                    x = _dot(dob, sb, 1, 1)
                    y = _dot(vb, dsb, 1, 1)
                    qq = q * qd
                    kk = k * kd
                    dq = _dot(dscb, kb) + x * qd
                    dk = _dot(dscb, qb, 0, 0) + y * kd
                    dv = _dot(sc, dob, 0, 0) + _dot(kk, dsb)
                    if first_visit:
                        sq_s[rows, cq] = dq.astype(BF)
                        sk_s[rows, cq] = dk.astype(BF)
                        sv_s[rows, cv] = dv.astype(BF)
                    else:
                        sq_s[rows, cq] = (sq_s[rows, cq].astype(F32) + dq).astype(BF)
                        sk_s[rows, cq] = ((sk_s[rows, cq].astype(F32) + dk) * (RET_DK ** -0.5)).astype(BF)
                        sv_s[rows, cv] = (sv_s[rows, cv].astype(F32) + dv).astype(BF)
                    t = (_sum_all(dsc * raw * rel) + _sum_all((a_i + 1.0) * qq * x)
                         + _sum_all((c - 1.0 - a_i) * kk * y) + c * gc * _sum_all(ds * sb.astype(F32)))
                    acc_s[4 * d + h:4 * d + h + 1, :] += t
                    ds_s[d, h] = ds * gc + _dot(qq, dob, 0, 0)

        @pl.when(jnp.logical_not(second))
        def _():
            chains(True)

        @pl.when(second)
        def _():
            chains(False)
            for cp in flush():
                cp.start()

        @pl.when(n == nc - 1)
        def _():
            ds0_ref[...] = ds_s[...]
            dlg_ref[...] = acc_s[...]

        @pl.when(jnp.logical_and(b == n_samp - 1, n == nc - 1))
        def _():
            for cp in flush():
                cp.wait()

    do_spec_f = pl.BlockSpec((c, wo), lambda b, n: (row_f(b, n), 0))
    do_spec_b = pl.BlockSpec((c, wo), lambda b, n: (row_b(b, n), 0))
    return _call(
        body, [lg] + [px] * 10 + [do, do, saved, dpx], comm, name="ret_bwd", grid=(n_samp, nc), aliases={14: 0},
        in_specs=[SMEM] + _ret_specs(row_f, row_b) + [
            do_spec_f, do_spec_b,
            pl.BlockSpec((None, RET_HEADS, 2, None, RET_DK, RET_DV), lambda b, n: (b, 0, 0, nc - 1 - n, 0, 0)),
            ANY],
        out_specs=(ANY,
                   pl.BlockSpec((None, 2, RET_HEADS, RET_DK, RET_DV), lambda b, n: (b, 0, 0, 0, 0)),
                   pl.BlockSpec((None, 8, 128), lambda b, n: (b, 0, 0))),
        out_shape=(SDS(dpx.shape, dpx.dtype),
                   SDS((n_samp, 2, RET_HEADS, RET_DK, RET_DV), F32), SDS((n_samp, 8, 128), F32)),
        scratch_shapes=[pltpu.VMEM((2, RET_HEADS, RET_DK, RET_DV), F32), pltpu.VMEM((8, 128), F32),
                        pltpu.VMEM((seq, wq), BF), pltpu.VMEM((seq, wq), BF), pltpu.VMEM((seq, wo), BF),
                        pltpu.SemaphoreType.DMA((6,))],
        compiler_params=_cp(("arbitrary", "arbitrary"), 60))


def _norm_rope(x, w, cos, sin):
    xn = x * _rms(x) * w
    return xn * cos + _swap_pairs(xn) * sin


def _norm_rope_bwd(dy, x, w, cos, sin):
    dxn = dy * cos + _swap_pairs(dy * sin)
    r = _rms(x)
    xh = x * r
    return _rms_bwd(dxn * w, xh, r), jnp.sum(dxn * xh, axis=0, keepdims=True)


def _att_prep_q(px, cos_all, sin_all, qnw, t_lat):
    hd = ATT_HEAD_DIM
    wblk = ATT_REP * hd

    def body(x_ref, cos_ref, sin_ref, w_ref, o_ref):
        for r in range(ATT_REP):
            cols = slice(r * hd, (r + 1) * hd)
            qr = _norm_rope(x_ref[:, cols].astype(F32), w_ref[...], cos_ref[...], sin_ref[...])
            o_ref[:, cols] = (qr * (hd ** -0.5)).astype(BF)

    return pl.pallas_call(
        body, name="att_prep_q", grid=(t_lat // TM, ATT_KV_HEADS),
        in_specs=[pl.BlockSpec((TM, wblk), lambda i, g: (i, C_AQ // wblk + g)),
                  pl.BlockSpec((TM, hd), lambda i, g: (i, 0)),
                  pl.BlockSpec((TM, hd), lambda i, g: (i, 0)),
                  pl.BlockSpec((1, hd), lambda i, g: (0, 0))],
        out_specs=pl.BlockSpec((TM, wblk), lambda i, g: (i, g)),
        out_shape=SDS((t_lat, ATT_HEADS * hd), BF),
        compiler_params=_cp(("parallel", "parallel")))(px, cos_all, sin_all, qnw)


def _att_prep_kv(px, cos_all, sin_all, knw):
    rows = px.shape[0]
    hd = ATT_HEAD_DIM
    kvw = ATT_KV_HEADS * hd

    def body(x_ref, cos_ref, sin_ref, w_ref, k_ref, v_ref):
        for g in range(ATT_KV_HEADS):
            cols = slice(g * hd, (g + 1) * hd)
            k_ref[:, cols] = _norm_rope(x_ref[:, cols].astype(F32), w_ref[...], cos_ref[...],
                                        sin_ref[...]).astype(BF)
            v_ref[:, 2 * g * hd:(2 * g + 1) * hd] = x_ref[:, kvw + g * hd:kvw + (g + 1) * hd].astype(BF)
            v_ref[:, (2 * g + 1) * hd:(2 * g + 2) * hd] = jnp.ones((TM, hd), BF)

    return pl.pallas_call(
        body, name="att_prep_kv", grid=(rows // TM,),
        in_specs=[pl.BlockSpec((TM, 2 * kvw), lambda i: (i, C_AK // (2 * kvw))),
                  pl.BlockSpec((TM, hd), lambda i: (i, 0)),
                  pl.BlockSpec((TM, hd), lambda i: (i, 0)),
                  pl.BlockSpec((1, hd), lambda i: (0, 0))],
        out_specs=(pl.BlockSpec((TM, kvw), lambda i: (i, 0)), pl.BlockSpec((TM, 2 * kvw), lambda i: (i, 0))),
        out_shape=(SDS((rows, kvw), BF), SDS((rows, 2 * kvw), BF)),
        compiler_params=_cp(("parallel",)))(px, cos_all, sin_all, knw)


def _att_kv_bwd(dpx, dkl, dkc, dvl, dvc, px, cos_all, sin_all, knw):
    rows = px.shape[0]
    hd = ATT_HEAD_DIM
    kvw = ATT_KV_HEADS * hd
    n_lat = dkl.shape[0] // TM
    assert dkc.shape[0] == TM

    def body(dpx_hbm, dkl_ref, dkc_ref, dvl_ref, dvc_ref, x_ref, cos_ref, sin_ref, w_ref, o_ref, gw_ref):
        i = pl.program_id(0)

        @pl.when(i == 0)
        def _():
            gw_ref[...] = jnp.zeros_like(gw_ref)

        is_lat = i < n_lat
        dk = jnp.where(is_lat, dkl_ref[...], dkc_ref[...])
        dv = jnp.where(is_lat, dvl_ref[...], dvc_ref[...])
        for g in range(ATT_KV_HEADS):
            cols = slice(g * hd, (g + 1) * hd)
            dx, gw = _norm_rope_bwd(dk[:, cols], x_ref[:, cols].astype(F32), w_ref[...], cos_ref[...], sin_ref[...])
            o_ref[:, cols] = dx.astype(BF)
            gw_ref[...] += gw
        o_ref[:, kvw:] = dv.astype(BF)

    lat = pl.BlockSpec((TM, kvw), lambda i: (jnp.minimum(i, n_lat - 1), 0))
    ctx = pl.BlockSpec((TM, kvw), lambda i: (0, 0))
    kvcol = pl.BlockSpec((TM, 2 * kvw), lambda i: (i, C_AK // (2 * kvw)))
    return pl.pallas_call(
        body, name="att_kv_bwd", grid=(rows // TM,), input_output_aliases={0: 0},
        in_specs=[ANY, lat, ctx, lat, ctx, kvcol,
                  pl.BlockSpec((TM, hd), lambda i: (i, 0)),
                  pl.BlockSpec((TM, hd), lambda i: (i, 0)),
                  pl.BlockSpec((1, hd), lambda i: (0, 0))],
        out_specs=(kvcol, pl.BlockSpec((1, hd), lambda i: (0, 0))),
        out_shape=(SDS(dpx.shape, dpx.dtype), SDS((1, hd), F32)),
        compiler_params=_cp(("arbitrary",)))(dpx, dkl, dkc, dvl, dvc, px, cos_all, sin_all, knw)


def _stack_heads(ref_or_val):
    hd = ATT_HEAD_DIM
    return jnp.concatenate([ref_or_val[:, r * hd:(r + 1) * hd] for r in range(ATT_REP)], axis=0)


def _att_scores(q, kl, kc):
    sl = _dot(q, kl, 1, 1)
    sc = _dot(q, kc, 1, 1)
    m = jnp.maximum(jnp.max(sl, axis=-1, keepdims=True), jnp.max(sc, axis=-1, keepdims=True))
    return jnp.exp(sl - m), jnp.exp(sc - m), m


def _att_fwd(qn, kn, vn, n_samp, seq, lc):
    hd = ATT_HEAD_DIM
    tq = ATT_TQ
    nq = seq // tq
    wblk = ATT_REP * hd
    cb = n_samp * seq // lc
    t_lat = n_samp * seq

    def body(q_ref, kl_ref, kc_ref, vl_ref, vc_ref, o_ref, lse_ref):
        lane = lax.broadcasted_iota(jnp.int32, (tq, hd), 1)
        lse = jnp.zeros((tq, hd), F32)
        for r in range(ATT_REP):
            cols = slice(r * hd, (r + 1) * hd)
            el, ec, m = _att_scores(q_ref[:, cols], kl_ref[...], kc_ref[...])
            pv = _dot(el, vl_ref[...]) + _dot(ec, vc_ref[...])
            denom = pv[:, hd:hd + 1]
            o_ref[:, cols] = (pv[:, :hd] / denom).astype(BF)
            lse = jnp.where(lane == r, m + jnp.log(denom), lse)
        lse_ref[...] = lse

    return pl.pallas_call(
        body, name="att_fwd", grid=(n_samp, ATT_KV_HEADS, nq),
        in_specs=[pl.BlockSpec((tq, wblk), lambda b, g, i: (b * nq + i, g)),
                  pl.BlockSpec((seq, hd), lambda b, g, i: (b, g)),
                  pl.BlockSpec((lc, hd), lambda b, g, i: (cb + b, g)),
                  pl.BlockSpec((seq, 2 * hd), lambda b, g, i: (b, g)),
                  pl.BlockSpec((lc, 2 * hd), lambda b, g, i: (cb + b, g))],
        out_specs=(pl.BlockSpec((tq, wblk), lambda b, g, i: (b * nq + i, g)),
                   pl.BlockSpec((tq, hd), lambda b, g, i: (b * nq + i, g))),
        out_shape=(SDS((t_lat, ATT_HEADS * hd), BF), SDS((t_lat, ATT_KV_HEADS * hd), F32)),
        compiler_params=_cp(("parallel", "parallel", "parallel"), 48))(qn, kn, kn, vn, vn)


def _att_bwd(dpx, qn, kn, vn, px, o_att, lse, do_att, cos_all, sin_all, qnw, n_samp, seq, lc, comm):
    hd = ATT_HEAD_DIM
    tq = ATT_TQ
    nq = seq // tq
    wblk = ATT_REP * hd
    cb = n_samp * seq // lc
    t_lat = n_samp * seq
    kvw = ATT_KV_HEADS * hd
    scale = hd ** -0.5

    def body(dpx_hbm, q_ref, kl_ref, kc_ref, vl_ref, vc_ref, o_ref, do_ref, x_ref, cos_ref, sin_ref, w_ref,
             lse_ref, dq_ref, dkl_ref, dkc_ref, dvl_ref, dvc_ref, gw_ref, akl, akc, avl, avc, aw):
        i = pl.program_id(2)

        @pl.when(i == 0)
        def _():
            akl[...] = jnp.zeros_like(akl)
            akc[...] = jnp.zeros_like(akc)
            avl[...] = jnp.zeros_like(avl)
            avc[...] = jnp.zeros_like(avc)
            aw[...] = jnp.zeros_like(aw)

        dobs, pls, pcs, dsls, dscs = [], [], [], [], []
        for r in range(ATT_REP):
            cols = slice(r * hd, (r + 1) * hd)
            dob = do_ref[:, cols]
            delta = jnp.sum(dob.astype(F32) * o_ref[:, cols].astype(F32), axis=-1, keepdims=True)
            lse = lse_ref[:, r:r + 1]
            p_l = jnp.exp(_dot(q_ref[:, cols], kl_ref[...], 1, 1) - lse).astype(BF)
            p_c = jnp.exp(_dot(q_ref[:, cols], kc_ref[...], 1, 1) - lse).astype(BF)
            ds_l = (p_l * (_dot(dob, vl_ref[...], 1, 1) - delta)).astype(BF)
            ds_c = (p_c * (_dot(dob, vc_ref[...], 1, 1) - delta)).astype(BF)
            dq = (_dot(ds_l, kl_ref[...]) + _dot(ds_c, kc_ref[...])) * scale
            dx, gw = _norm_rope_bwd(dq, x_ref[:, cols].astype(F32), w_ref[...], cos_ref[...], sin_ref[...])
            dq_ref[:, cols] = dx.astype(BF)
            aw[...] += gw
            dobs.append(dob)
            pls.append(p_l)
            pcs.append(p_c)
            dsls.append(ds_l)
            dscs.append(ds_c)
        do4 = jnp.concatenate(dobs, axis=0)
        q4 = _stack_heads(q_ref)
        avl[...] += _dot(jnp.concatenate(pls, axis=0), do4, 0, 0)
        avc[...] += _dot(jnp.concatenate(pcs, axis=0), do4, 0, 0)
        akl[...] += _dot(jnp.concatenate(dsls, axis=0), q4, 0, 0)
        akc[...] += _dot(jnp.concatenate(dscs, axis=0), q4, 0, 0)

        @pl.when(i == nq - 1)
        def _():
            dkl_ref[...] = akl[...]
            dkc_ref[...] = akc[...]
            dvl_ref[...] = avl[...]
            dvc_ref[...] = avc[...]
            gw_ref[...] = aw[...]

    return _call(
        body, [dpx, qn, kn, kn, vn, vn, o_att, do_att, px, cos_all, sin_all, qnw, lse], comm,
        name="att_bwd", grid=(n_samp, ATT_KV_HEADS, nq), aliases={0: 0},
        in_specs=[ANY,
                  pl.BlockSpec((tq, wblk), lambda b, g, i: (b * nq + i, g)),
                  pl.BlockSpec((seq, hd), lambda b, g, i: (b, g)),
                  pl.BlockSpec((lc, hd), lambda b, g, i: (cb + b, g)),
                  pl.BlockSpec((seq, hd), lambda b, g, i: (b, 2 * g)),
                  pl.BlockSpec((lc, hd), lambda b, g, i: (cb + b, 2 * g)),
                  pl.BlockSpec((tq, wblk), lambda b, g, i: (b * nq + i, g)),
                  pl.BlockSpec((tq, wblk), lambda b, g, i: (b * nq + i, g)),
                  pl.BlockSpec((tq, wblk), lambda b, g, i: (b * nq + i, C_AQ // wblk + g)),
                  pl.BlockSpec((tq, hd), lambda b, g, i: (b * nq + i, 0)),
                  pl.BlockSpec((tq, hd), lambda b, g, i: (b * nq + i, 0)),
                  pl.BlockSpec((1, hd), lambda b, g, i: (0, 0)),
                  pl.BlockSpec((tq, hd), lambda b, g, i: (b * nq + i, g))],
        out_specs=(pl.BlockSpec((tq, wblk), lambda b, g, i: (b * nq + i, C_AQ // wblk + g)),
                   pl.BlockSpec((seq, hd), lambda b, g, i: (b, g)),
                   pl.BlockSpec((lc, hd), lambda b, g, i: (b, g)),
                   pl.BlockSpec((seq, hd), lambda b, g, i: (b, g)),
                   pl.BlockSpec((lc, hd), lambda b, g, i: (b, g)),
                   pl.BlockSpec((None, None, 1, hd), lambda b, g, i: (b, g, 0, 0))),
        out_shape=(SDS(dpx.shape, dpx.dtype),
                   SDS((t_lat, kvw), F32), SDS((n_samp * lc, kvw), F32),
                   SDS((t_lat, kvw), F32), SDS((n_samp * lc, kvw), F32),
                   SDS((n_samp, ATT_KV_HEADS, 1, hd), F32)),
        scratch_shapes=[pltpu.VMEM((seq, hd), F32), pltpu.VMEM((lc, hd), F32),
                        pltpu.VMEM((seq, hd), F32), pltpu.VMEM((lc, hd), F32), pltpu.VMEM((1, hd), F32)],
        compiler_params=_cp(("arbitrary", "arbitrary", "arbitrary"), 56))


def _merge(x_lat, target, o_f, o_b, o_att, px, gate3, w_o_ret, w_o_att, w_out, tiles_per_sample):
    t_lat = x_lat.shape[0]
    tm = 256
    n_t = t_lat // tm
    per = tiles_per_sample * (TM // tm)
    d = D_MODEL
    rv = RET_HEADS * RET_DV
    n_samp = gate3.shape[0] - 1

    half = d // 2
    n_px = 10

    def body(x_ref, t_ref, of_ref, ob_ref, oa_ref, *rest):
        pxs, rest = rest[:n_px], rest[n_px:]
        (gt_ref, wor_ref, woa_ref, wout_ref,
         gx_ref, dor_ref, doa_ref, dpx_hbm, loss_ref, dgt_ref, gwor_hbm, gwoa_hbm, gwout_hbm,
         aor, aoa, aout, drg_ref, dtail_ref, sems) = rest
        i = pl.program_id(0)

        def copies(step):
            rows = pl.ds(pl.multiple_of(step * tm, tm), tm)
            return (pltpu.make_async_copy(drg_ref, dpx_hbm.at[rows, pl.ds(C_RG, rv)], sems.at[0]),
                    pltpu.make_async_copy(dtail_ref, dpx_hbm.at[rows, pl.ds(C_AG, 3 * d)], sems.at[1]))

        @pl.when(i == 0)
        def _():
            aor[...] = jnp.zeros_like(aor)
            aoa[...] = jnp.zeros_like(aoa)
            aout[...] = jnp.zeros_like(aout)
            loss_ref[...] = jnp.zeros_like(loss_ref)

        @pl.when(i % per == 0)
        def _():
            dgt_ref[...] = jnp.zeros_like(dgt_ref)

        def cat(refs):
            return jnp.concatenate([r[...] for r in refs], axis=1).astype(F32)

        def ret_head(h):
            cols = slice(h * RET_DV, (h + 1) * RET_DV)
            o = of_ref[:, cols].astype(F32) + ob_ref[:, cols].astype(F32)
            r = _rms(o)
            g = pxs[h][...].astype(F32)
            return o * r, r, g, _sigmoid(g)

        def att_half(k):
            o = oa_ref[:, k * half:(k + 1) * half].astype(F32)
            g = pxs[4 + k][...].astype(F32)
            return o, g, _sigmoid(g)

        yrs = []
        for h in range(RET_HEADS):
            on, _, g, sg = ret_head(h)
            yrs.append((on * (g * sg)).astype(BF))
        yr = jnp.concatenate(yrs, axis=1)
        yas = []
        for k in range(2):
            o, g, sg = att_half(k)
            yas.append((o * (g * sg)).astype(BF))
        ya = jnp.concatenate(yas, axis=1)

        a = jnp.dot(yr, wor_ref[...], preferred_element_type=F32)
        b = jnp.dot(ya, woa_ref[...], preferred_element_type=F32)
        sr = _sigmoid(cat(pxs[6:8]))
        sa = _sigmoid(cat(pxs[8:10]))
        yb = (sr * a + sa * b).astype(BF)
        out = jnp.dot(yb, wout_ref[...], preferred_element_type=F32)
        gate = gt_ref[...]
        err = x_ref[...] + gate * out - t_ref[...]
        loss_ref[...] += 0.5 * _sum_all(err * err) * (1.0 / d)
        dy_tok = err * (1.0 / d)
        gx_ref[...] = dy_tok
        dgt_ref[...] += jnp.sum(dy_tok * out, axis=0, keepdims=True)
        dout = (dy_tok * gate).astype(BF)
        aout[...] += _dot(yb, dout, 0, 0)
        dyy = _dot(dout, wout_ref[...], 1, 1)
        da = (dyy * sr).astype(BF)
        db = (dyy * sa).astype(BF)
        aor[...] += _dot(yr, da, 0, 0)
        aoa[...] += _dot(ya, db, 0, 0)
        dyr = _dot(da, wor_ref[...], 1, 1)
        dya = _dot(db, woa_ref[...], 1, 1)

        @pl.when(i > 0)
        def _():
            for cp in copies(i - 1):
                cp.wait()

        dtail_ref[:, d:2 * d] = (dyy * a * (sr * (1.0 - sr))).astype(BF)
        dtail_ref[:, 2 * d:] = (dyy * b * (sa * (1.0 - sa))).astype(BF)
        for h in range(RET_HEADS):
            cols = slice(h * RET_DV, (h + 1) * RET_DV)
            on, r, g, sg = ret_head(h)
            dy = dyr[:, cols]
            drg_ref[:, cols] = (dy * on * (sg * (1.0 + g * (1.0 - sg)))).astype(BF)
            dor_ref[:, cols] = _rms_bwd(dy * (g * sg), on, r).astype(BF)
        for k in range(2):
            cols = slice(k * half, (k + 1) * half)
            o, g, sg = att_half(k)
            dy = dya[:, cols]
            dtail_ref[:, cols] = (dy * o * (sg * (1.0 + g * (1.0 - sg)))).astype(BF)
            doa_ref[:, cols] = (dy * (g * sg)).astype(BF)
        for cp in copies(i):
            cp.start()

        @pl.when(i == n_t - 1)
        def _():
            for cp in copies(i):
                cp.wait()
            pltpu.sync_copy(aor, gwor_hbm)
            pltpu.sync_copy(aoa, gwoa_hbm)
            pltpu.sync_copy(aout, gwout_hbm)

    def px_blk(col):
        return pl.BlockSpec((tm, half), lambda i: (i, col // half))

    def resident(shape):
        return pl.BlockSpec(shape, lambda i: (0, 0), pipeline_mode=pl.Buffered(1))

    px_cols = ([C_RG + k * half for k in range(4)] + [C_AG, C_AG + half]
               + [C_MR, C_MR + half, C_MA, C_MA + half])
    return pl.pallas_call(
        body, name="merge", grid=(n_t,),
        in_specs=[pl.BlockSpec((tm, d), lambda i: (i, 0)),
                  pl.BlockSpec((tm, d), lambda i: (i, 0)),
                  pl.BlockSpec((tm, rv), lambda i: (i, 0)),
                  pl.BlockSpec((tm, rv), lambda i: (i, 0)),
                  pl.BlockSpec((tm, d), lambda i: (i, 0))]
        + [px_blk(col) for col in px_cols]
        + [pl.BlockSpec((None, 1, d), lambda i: (i // per, 0, 0)),
           resident((rv, d)), resident((d, d)), resident((d, d))],
        out_specs=(pl.BlockSpec((tm, d), lambda i: (i, 0)),
                   pl.BlockSpec((tm, rv), lambda i: (i, 0)),
                   pl.BlockSpec((tm, d), lambda i: (i, 0)),
                   ANY,
                   pl.BlockSpec((8, 128), lambda i: (0, 0)),
                   pl.BlockSpec((None, 1, d), lambda i: (i // per, 0, 0)),
                   ANY, ANY, ANY),
        out_shape=(SDS((t_lat, d), F32), SDS((t_lat, rv), BF), SDS((t_lat, d), BF),
                   SDS((px.shape[0], IN_COLS), BF),
                   SDS((8, 128), F32), SDS((n_samp, 1, d), F32),
                   SDS((rv, d), F32), SDS((d, d), F32), SDS((d, d), F32)),
        scratch_shapes=[pltpu.VMEM((rv, d), F32), pltpu.VMEM((d, d), F32), pltpu.VMEM((d, d), F32),
                        pltpu.VMEM((tm, rv), BF), pltpu.VMEM((tm, 3 * d), BF), pltpu.SemaphoreType.DMA((2,))],
        compiler_params=_cp(("arbitrary",), 56))(
            x_lat, target, o_f, o_b, o_att, *([px] * n_px), gate3, w_o_ret, w_o_att, w_out)


def _place():
    x, y, c = lax.axis_index("x"), lax.axis_index("y"), lax.axis_index("c")
    chips = [(1 - x, y), (x, 1 - y), (1 - x, 1 - y)]
    return x, y, c, chips


def _remote(src, dst, send_sem, recv_sem, to):
    return pltpu.make_async_remote_copy(src_ref=src, dst_ref=dst, send_sem=send_sem, recv_sem=recv_sem,
                                        device_id=to, device_id_type=MESH)


def _place_ids():
    x, y, c = lax.axis_index("x"), lax.axis_index("y"), lax.axis_index("c")
    me = 2 * x + y
    return jnp.stack([x, y, c, me, me, 2 * (1 - x) + y, 2 * x + 1 - y, 2 * (1 - x) + 1 - y]).astype(jnp.int32)


def _ag_comm(bufs):
    n, m = len(bufs), 3

    def half(ref, s, which):
        h = ref.shape[1] // 2
        return ref.at[s, pl.ds(which * h, h), :]

    def ici(ins, outs, ssem, rsem, base):
        x, y, c, chips = _place()
        sends, recvs = [], []
        for a in range(n):
            for j in range(m):
                k, chip = base + a * m + j, chips[j]
                mine, theirs = half(outs[a], 2 * x + y, c), half(outs[a], 2 * chip[0] + chip[1], c)
                sends.append(_remote(mine, mine, ssem.at[k], rsem.at[k], (*chip, c)))
                recvs.append(_remote(theirs, theirs, ssem.at[k], rsem.at[k], (*chip, c)))
        return sends, recvs

    def d2d(ins, outs, ssem, rsem, base):
        x, y, c, chips = _place()
        sends, recvs = [], []
        for a in range(n):
            for j in range(m):
                k, s = base + (n + a) * m + j, 2 * chips[j][0] + chips[j][1]
                sends.append(_remote(half(outs[a], s, c), half(outs[a], s, c), ssem.at[k], rsem.at[k], (x, y, 1 - c)))
                recvs.append(_remote(half(outs[a], s, 1 - c), half(outs[a], s, 1 - c), ssem.at[k], rsem.at[k],
                                     (x, y, 1 - c)))
        return sends, recvs

    return _Comm("all_gather", tuple(bufs), tuple(SDS(b.shape, b.dtype) for b in bufs), {a: a for a in range(n)},
                 2 * n * m, (ici, d2d))


def _swap_comm(grads):
    n = len(grads)

    def phase(ins, outs, ssem, rsem, base):
        x, y, c, _ = _place()
        sends = []
        for a in range(n):
            h = ins[a].shape[1] // 2
            sends.append(_remote(ins[a].at[:, pl.ds((1 - c) * h, h), :], outs[a], ssem.at[base + a],
                                 rsem.at[base + a], (x, y, 1 - c)))
        return sends, sends

    return _Comm("swap_halves", tuple(grads),
                 tuple(SDS((g.shape[0], g.shape[1] // 2, g.shape[2]), g.dtype) for g in grads), {}, n, (phase,))


def _exchange_comm(parts):
    n = len(parts)

    def phase(ins, outs, ssem, rsem, base):
        x, y, c, chips = _place()
        sends = []
        for a in range(n):
            for j, chip in enumerate(chips):
                k = base + 3 * a + j
                sends.append(_remote(ins[a].at[2 * chip[0] + chip[1]], outs[a].at[j], ssem.at[k], rsem.at[k],
                                     (*chip, c)))
        return sends, sends

    return _Comm("exchange_shards", tuple(parts), tuple(SDS((3,) + p.shape[1:], p.dtype) for p in parts), {}, 3 * n,
                 (phase,))


def _join_comm(bufs):
    n = len(bufs)

    def phase(ins, outs, ssem, rsem, base):
        x, y, c, _ = _place()
        sends, recvs = [], []
        for a in range(n):
            h = outs[a].shape[0] // 2
            mine, other = outs[a].at[pl.ds(c * h, h), :], outs[a].at[pl.ds((1 - c) * h, h), :]
            sends.append(_remote(mine, mine, ssem.at[base + a], rsem.at[base + a], (x, y, 1 - c)))
            recvs.append(_remote(other, other, ssem.at[base + a], rsem.at[base + a], (x, y, 1 - c)))
        return sends, recvs

    return _Comm("join_halves", tuple(bufs), tuple(SDS(b.shape, b.dtype) for b in bufs), {a: a for a in range(n)},
                 n, (phase,))


def _cast_place(w, ids):
    rows, cols = w.shape
    tr = min(rows, 256)

    def body(ids_ref, w_ref, o_ref):
        o_ref[...] = w_ref[...].astype(BF)

    return pl.pallas_call(
        body, name="cast_place",
        grid_spec=pltpu.PrefetchScalarGridSpec(
            num_scalar_prefetch=1, grid=(rows // tr,),
            in_specs=[pl.BlockSpec((tr, cols), lambda i, ids_ref: (i, 0))],
            out_specs=pl.BlockSpec((None, tr, cols), lambda i, ids_ref: (ids_ref[3], i, 0))),
        out_shape=SDS((N_SHARD, rows, cols), BF),
        compiler_params=_cp(("parallel",), 40))(ids, w)


def _chip_sum(g, p, ids):
    n_s, rows, cols = g.shape
    h = rows // 2
    tr = min(h, 256)
    nb = h // tr

    def body(ids_ref, g_ref, p_ref, o_ref, o16_ref):
        t = g_ref[...] + p_ref[...]
        o_ref[...] = t
        o16_ref[...] = t.astype(BF)

    out_spec = pl.BlockSpec((None, tr, cols), lambda s, i, ids_ref: (s, i, 0))
    return pl.pallas_call(
        body, name="chip_sum",
        grid_spec=pltpu.PrefetchScalarGridSpec(
            num_scalar_prefetch=1, grid=(n_s, nb),
            in_specs=[pl.BlockSpec((None, tr, cols), lambda s, i, ids_ref: (s, ids_ref[2] * nb + i, 0)),
                      pl.BlockSpec((None, tr, cols), lambda s, i, ids_ref: (s, i, 0))],
            out_specs=(out_spec, out_spec)),
        out_shape=(SDS((n_s, h, cols), g.dtype), SDS((n_s, h, cols), BF)),
        compiler_params=_cp(("parallel", "parallel"), 40))(ids, g, p)


def _shard_sum(t, q, ids):
    _, h, cols = t.shape
    tr = min(h, 256)
    nb = h // tr

    def body(ids_ref, t_ref, q_ref, o_ref):
        o_ref[...] = ((t_ref[...] + q_ref[0].astype(F32)) + q_ref[1].astype(F32)) + q_ref[2].astype(F32)

    return pl.pallas_call(
        body, name="shard_sum",
        grid_spec=pltpu.PrefetchScalarGridSpec(
            num_scalar_prefetch=1, grid=(nb,),
            in_specs=[pl.BlockSpec((None, tr, cols), lambda i, ids_ref: (ids_ref[3], i, 0)),
                      pl.BlockSpec((3, tr, cols), lambda i, ids_ref: (0, i, 0))],
            out_specs=pl.BlockSpec((tr, cols), lambda i, ids_ref: (ids_ref[2] * nb + i, 0))),
        out_shape=SDS((2 * h, cols), t.dtype),
        compiler_params=_cp(("parallel",), 40))(ids, t, q)


def _gather_small(block, n_sum):
    rows, cols = block.shape
    n_dev = 8

    def body(x_ref, o_ref, g_ref, buf, send_sems, recv_sems, local_sem):
        x, y, c, chips = _place()
        me, sibling = (x, y, c), (x, y, 1 - c)

        def slot(px_, py_, pc_):
            return buf.at[4 * px_ + 2 * py_ + pc_]

        def copy(k, who, to, src=None):
            return _remote(slot(*who) if src is None else src, slot(*who), send_sems.at[k], recv_sems.at[k], to)

        mine = pltpu.make_async_copy(x_ref, slot(*me), local_sem)
        mine.start()
        first = [copy(0, me, sibling, src=x_ref)]
        first += [copy(1 + j, me, (*chip, c), src=x_ref) for j, chip in enumerate(chips)]
        for cp in first:
            cp.start()
        passed = [copy(4 + j, (*chip, c), sibling) for j, chip in enumerate(chips)]
        for j, chip in enumerate(chips):
            copy(1 + j, (*chip, c), me).wait_recv()
            passed[j].start()
        copy(0, sibling, me).wait_recv()
        for j, chip in enumerate(chips):
            copy(4 + j, (*chip, 1 - c), me).wait_recv()
        for cp in first + passed:
            cp.wait_send()
        mine.wait()
        acc = buf[0, :, :n_sum]
        for s in range(1, n_dev):
            acc = acc + buf[s, :, :n_sum]
        o_ref[...] = acc
        for s in range(n_dev):
            g_ref[s * rows:(s + 1) * rows, :] = buf[s, :, n_sum:]

    return pl.pallas_call(
        body, name="gather_small",
        in_specs=[pl.BlockSpec(memory_space=pltpu.VMEM)],
        out_specs=(pl.BlockSpec(memory_space=pltpu.VMEM), pl.BlockSpec(memory_space=pltpu.VMEM)),
        out_shape=(SDS((rows, n_sum), F32), SDS((n_dev * rows, cols - n_sum), F32)),
        scratch_shapes=[pltpu.VMEM((n_dev, rows, cols), F32), pltpu.SemaphoreType.DMA((7,)),
                        pltpu.SemaphoreType.DMA((7,)), pltpu.SemaphoreType.DMA],
        compiler_params=_cp(has_side_effects=True))(block)


def _adam_math(w, g, m, v):
    m = ADAM_B1 * m + (1.0 - ADAM_B1) * g
    v = ADAM_B2 * v + (1.0 - ADAM_B2) * (g * g)
    m_hat = m / (1.0 - ADAM_B1 ** ADAM_STEP)
    v_hat = v / (1.0 - ADAM_B2 ** ADAM_STEP)
    delta = -ADAM_LR * (m_hat / (jnp.sqrt(v_hat) + ADAM_EPS) + ADAM_WD * w)
    return delta, m, v


def _adamw(w, g, m, v):
    rows, cols = w.shape
    tr = min(rows, 256 if cols <= 2048 else 128)

    def body(w_ref, g_ref, m_ref, v_ref, go_ref, d_ref, nm_ref, nv_ref):
        g = g_ref[...]
        go_ref[...] = g
        d_ref[...], nm_ref[...], nv_ref[...] = _adam_math(w_ref[...], g, m_ref[...], v_ref[...])

    spec = pl.BlockSpec((tr, cols), lambda i: (i, 0))
    return pl.pallas_call(
        body, name="adamw", grid=(rows // tr,), in_specs=[spec] * 4, out_specs=(spec,) * 4,
        out_shape=(SDS(w.shape, F32),) * 4, compiler_params=_cp(("parallel",), 40))(w, g, m, v)


def _adamw_small(w, g, m, v):
    def body(w_ref, g_ref, m_ref, v_ref, go_ref, d_ref, nm_ref, nv_ref):
        w = w_ref[...]
        g = g_ref[...]
        sub = lax.broadcasted_iota(jnp.int32, w.shape, 0)
        lane = lax.broadcasted_iota(jnp.int32, w.shape, 1)
        is_ret = jnp.logical_and(sub == 5, lane < 2 * RET_HEADS)
        u = jnp.exp(jnp.where(is_ret, w, -1.0) * jnp.log(2.0))
        g = jnp.where(is_ret, g * (-u * jnp.log(2.0) / (1.0 - u)), g)
        go_ref[...] = g
        d_ref[...], nm_ref[...], nv_ref[...] = _adam_math(w, g, m_ref[...], v_ref[...])

    return pl.pallas_call(body, name="adamw_small", out_shape=(SDS(w.shape, F32),) * 4)(w, g, m, v)


def _rope_tables(seq, n_samp, n_ctx_rows):
    rows = seq // GRID_W
    row = jnp.repeat(jnp.arange(rows, dtype=F32), GRID_W)
    col = jnp.tile(jnp.arange(GRID_W, dtype=F32), rows)
    half = ATT_HEAD_DIM // 2
    freqs = ROPE_THETA ** (-jnp.arange(0, half, 2, dtype=F32) / half)
    ang = jnp.concatenate([row[:, None] * freqs, col[:, None] * freqs], axis=-1)
    cos, sin = jnp.cos(ang), jnp.sin(ang)
    cos_f = jnp.repeat(cos, 2, axis=1)
    sin_s = jnp.stack([-sin, sin], axis=-1).reshape(seq, ATT_HEAD_DIM)
    cos_all = jnp.concatenate([jnp.tile(cos_f, (n_samp, 1)), jnp.ones((n_ctx_rows, ATT_HEAD_DIM), F32)], axis=0)
    sin_all = jnp.concatenate([jnp.tile(sin_s, (n_samp, 1)), jnp.zeros((n_ctx_rows, ATT_HEAD_DIM), F32)], axis=0)
    return cos_all, sin_all


def _pack_small(c_ctx, norm_w, b_ada, ret, qn, kn):
    d = D_MODEL
    row5 = jnp.concatenate([ret.reshape(-1), jnp.zeros((128 - 2 * RET_HEADS,), F32), qn.reshape(-1), kn.reshape(-1),
                            jnp.zeros((d - 384,), F32)])
    return jnp.concatenate([c_ctx.reshape(1, d), norm_w.reshape(1, d), b_ada.reshape(3, d), row5.reshape(1, d),
                            jnp.zeros((2, d), F32)], axis=0)


def _unpack_small(p):
    d = D_MODEL
    return (p[0], p[1:2], p[2:5].reshape(1, 3 * d), p[5, :2 * RET_HEADS].reshape(1, 2, RET_HEADS),
            p[5:6, 128:256], p[5:6, 256:384])


def _step(x, c, ctx, c_ctx, norm_w, b_ada, ret_log2_decay, q_norm_w, k_norm_w, loss_target, weights, ids):
    n_samp, seq, d = x.shape
    lc = ctx.shape[1]
    t_lat, t_ctx = n_samp * seq, n_samp * lc
    assert seq % TM == 0 and t_ctx == TM and t_lat % lc == 0 and seq % GRID_W == 0
    tps = seq // TM

    x_lat = x.reshape(t_lat, d)
    x_ctx = ctx.reshape(t_ctx, d)
    cvec8 = jnp.concatenate([c, c_ctx.reshape(1, d), jnp.zeros((8 - n_samp - 1, d), F32)], axis=0)
    lg = jnp.log1p(-jnp.exp2(ret_log2_decay.reshape(2, RET_HEADS)))
    cos_all, sin_all = _rope_tables(seq, n_samp, t_ctx)

    w_ada_b, w_in_b, w_or_b, w_oa_b, w_out_b = weights
    (w_ada_g,) = _run_comm(_ag_comm((w_ada_b,)))
    mod8 = _adaln_fwd(cvec8, w_ada_g, b_ada)
    mod3 = mod8[:n_samp + 1]
    shift3 = mod3[:, None, 0:d]
    scale3 = mod3[:, None, d:2 * d]
    gate3 = mod3[:, None, 2 * d:3 * d]

    hx, hxt = _norm_fwd(x_lat, x_ctx, norm_w, scale3, shift3, tps, n_samp)
    px, w_in_g = _in_proj_gather(hx, w_in_b, ids)

    states0 = _ctx_state_fwd(px, lg, n_samp, t_lat, lc)
    (o_f, o_b, saved), w_o = _ret_fwd(px, states0, lg, n_samp, seq,
                                      comm=_ag_comm((w_or_b, w_oa_b, w_out_b)))
    w_o_ret, w_o_att, w_out = (w.reshape(-1, d) for w in w_o)

    qn = _att_prep_q(px, cos_all, sin_all, q_norm_w, t_lat)
    kn, vn = _att_prep_kv(px, cos_all, sin_all, k_norm_w)
    o_att, lse = _att_fwd(qn, kn, vn, n_samp, seq, lc)

    (gx_res, do, do_att, dpx, loss8, dgate, g_w_o_ret, g_w_o_att, g_w_out) = _merge(
        x_lat, loss_target.reshape(t_lat, d), o_f, o_b, o_att, px, gate3, w_o_ret, w_o_att, w_out, tps)

    g_a = [g.reshape(N_SHARD, -1, d) for g in (g_w_o_ret, g_w_o_att, g_w_out)]
    (dpx, dkl, dkc, dvl, dvc, gqw), sib_a = _att_bwd(
        dpx, qn, kn, vn, px, o_att, lse, do_att, cos_all, sin_all, q_norm_w, n_samp, seq, lc, comm=_swap_comm(g_a))
    dpx, gkw = _att_kv_bwd(dpx, dkl, dkc, dvl, dvc, px, cos_all, sin_all, k_norm_w)
    t_a = [_chip_sum(g, p, ids) for g, p in zip(g_a, sib_a)]

    (dpx, dstates, dlg_lat), q_a = _ret_bwd(dpx, px, do, saved, lg, n_samp, seq,
                                            comm=_exchange_comm([t16 for _, t16 in t_a]))
    r_a = [_shard_sum(t, q, ids) for (t, _), q in zip(t_a, q_a)]
    dpx, dlg_ctx = _ctx_state_bwd(dpx, px, dstates, lg, n_samp, t_lat, lc)
    dpx = _zero_ctx_tail(dpx, t_lat)

    n_tiles = dpx.shape[0] // _big_rows(dpx.shape[0])
    g_b = _gw_in(hxt, dpx)
    dhx, (sib_b, *r_a) = _dhx(dpx, w_in_g, 0, 1, None, _join_comms(_swap_comm([g_b]), _join_comm(r_a)))
    t_b, t16_b = _chip_sum(g_b, sib_b, ids)
    dhx, (q_b,) = _dhx(dpx, w_in_g, 1, n_tiles - 1, dhx, _exchange_comm([t16_b]))
    r_b_half = _shard_sum(t_b, q_b, ids)
    grad_x, dshift, dscale, g_norm_w = _norm_bwd(x_lat, x_ctx, dhx, gx_res, norm_w, scale3, tps, n_samp)

    dgate_all = jnp.concatenate([dgate, jnp.zeros((1, 1, d), F32)], axis=0)
    dmod3 = jnp.concatenate([dshift, dscale, dgate_all], axis=2).reshape(n_samp + 1, 3 * d)
    dmod8 = jnp.concatenate([dmod3, jnp.zeros((8 - n_samp - 1, 3 * d), F32)], axis=0)
    g_lg = (jnp.sum(dlg_lat[:, :, 0], axis=0).reshape(2, RET_HEADS)
            + jnp.stack([jnp.sum(dlg_ctx[:, :, 0, 0], axis=0), jnp.sum(dlg_ctx[:, :, 1, 0], axis=0)], axis=0))
    g_qw = jnp.sum(gqw, axis=(0, 1, 2))
    zero = jnp.zeros((d,), F32)

    local = _pack_small(zero, g_norm_w, jnp.zeros((3 * d,), F32), g_lg, g_qw, gkw).at[6, 0].set(loss8[0, 0])
    small_sum, gathered = _gather_small(jnp.concatenate([local, cvec8, dmod8], axis=1), d)
    (g_w_ada, g_b_ada, dc_all), (r_b,) = _adaln_bwd(gathered[:, :d], gathered[:, d:], w_ada_g,
                                                     comm=_join_comm([r_b_half]))
    dc_ctx = jnp.sum(dc_all.reshape(-1, 8, d)[:, n_samp], axis=0)
    small = small_sum + _pack_small(dc_ctx, zero, g_b_ada, jnp.zeros((2, RET_HEADS), F32), zero[:128], zero[:128])
    r_c = lax.dynamic_index_in_dim(g_w_ada, ids[3], 0, keepdims=False)
    return small[6, 0], grad_x.reshape(n_samp, seq, d), (r_c, r_b, *r_a), small


def kernel(x, c, ctx, c_ctx, norm_w, w_ada, b_ada, w_in, ret_log2_decay, q_norm_w, k_norm_w, w_o_ret, w_o_att, w_out, loss_target, m_c_ctx, m_norm_w, m_w_ada, m_b_ada, m_w_in, m_ret_log2_decay, m_q_norm_w, m_k_norm_w, m_w_o_ret, m_w_o_att, m_w_out, v_c_ctx, v_norm_w, v_w_ada, v_b_ada, v_w_in, v_ret_log2_decay, v_q_norm_w, v_k_norm_w, v_w_o_ret, v_w_o_att, v_w_out):
    big_w = (w_ada[0], w_in[0], w_o_ret[0], w_o_att[0], w_out[0])
    big_m = (m_w_ada[0], m_w_in[0], m_w_o_ret[0], m_w_o_att[0], m_w_out[0])
    big_v = (v_w_ada[0], v_w_in[0], v_w_o_ret[0], v_w_o_att[0], v_w_out[0])

    ids = _place_ids()
    loss, grad_x, big_grad, small_grad_in = _step(
        x, c, ctx, c_ctx, norm_w[0:1], b_ada[0:1], ret_log2_decay[0], q_norm_w[0:1], k_norm_w[0:1], loss_target,
        tuple(_cast_place(w, ids) for w in big_w), ids)
    small_w = _pack_small(c_ctx, norm_w, b_ada, ret_log2_decay, q_norm_w, k_norm_w)
    small_m = _pack_small(m_c_ctx, m_norm_w, m_b_ada, m_ret_log2_decay, m_q_norm_w, m_k_norm_w)
    small_v = _pack_small(v_c_ctx, v_norm_w, v_b_ada, v_ret_log2_decay, v_q_norm_w, v_k_norm_w)
    small_grad, small_delta, small_nm, small_nv = _adamw_small(small_w, small_grad_in, small_m, small_v)

    big_g, big_delta, big_nm, big_nv = [], [], [], []
    for w, g, m, v in zip(big_w, big_grad, big_m, big_v):
        go, dlt, nm, nv = _adamw(w, g, m, v)
        big_g.append(go[None])
        big_delta.append(dlt[None])
        big_nm.append(nm[None])
        big_nv.append(nv[None])
    big_grad = big_g

    def order(small_packed, big):
        s = _unpack_small(small_packed)
        return (s[0], s[1], big[0], s[2], big[1], s[3], s[4], s[5], big[2], big[3], big[4])

    return (loss, grad_x, *order(small_grad, big_grad), *order(small_delta, big_delta),
            *order(small_nm, big_nm), *order(small_nv, big_nv))
```

```python
import functools
from typing import NamedTuple

import jax
import jax.numpy as jnp
from jax import lax
from jax.experimental import pallas as pl
from jax.experimental.pallas import tpu as pltpu

F32 = jnp.float32
BF = jnp.bfloat16
SDS = jax.ShapeDtypeStruct
MESH = pl.DeviceIdType.MESH
ANY = pl.BlockSpec(memory_space=pl.ANY)
SMEM = pl.BlockSpec(memory_space=pltpu.SMEM)

D_MODEL = 1024
GRID_W = 64
RET_HEADS = 4
RET_DK = 256
RET_DV = 512
RET_CHUNK = 128
ATT_HEADS = 8
ATT_KV_HEADS = 2
ATT_REP = ATT_HEADS // ATT_KV_HEADS
ATT_HEAD_DIM = 128
ROPE_THETA = 10000.0
NORM_EPS = 1e-6
IN_COLS = 10752
KV_COLS = 3584
C_RK, C_RV, C_AK, C_AV, C_RQ, C_RG, C_AQ, C_AG, C_MR, C_MA = 0, 1024, 3072, 3328, 3584, 4608, 6656, 7680, 8704, 9728
N_SHARD = 4
ADA_W = 3 * D_MODEL // N_SHARD
IN_W = IN_COLS // N_SHARD
IN_BLK = IN_W
BPS = IN_W // IN_BLK
N_IN_BLK = IN_COLS // IN_BLK
TM = 512
DHX_TILES = 4
ATT_TQ = 512
ADAM_LR, ADAM_B1, ADAM_B2, ADAM_EPS, ADAM_WD, ADAM_STEP = 0.001, 0.9, 0.999, 1e-08, 0.01, 10
MIB = 1024 * 1024


def _cp(sem=None, vmem_mb=None, **kw):
    if sem is not None:
        kw["dimension_semantics"] = sem
    if vmem_mb is not None:
        kw["vmem_limit_bytes"] = vmem_mb * MIB
    return pltpu.CompilerParams(**kw)


def _dot(a, b, ca=1, cb=0):
    return lax.dot_general(a.astype(BF), b.astype(BF), (((ca,), (cb,)), ((), ())), preferred_element_type=F32)


def _sigmoid(x):
    return 0.5 * jnp.tanh(0.5 * x) + 0.5


def _sum_all(x):
    return jnp.sum(jnp.sum(x, axis=1, keepdims=True), axis=0, keepdims=True)


def _swap_pairs(x):
    ax = x.ndim - 1
    lane = lax.broadcasted_iota(jnp.int32, x.shape, ax)
    nxt = pltpu.roll(x, x.shape[ax] - 1, ax)
    prv = pltpu.roll(x, 1, ax)
    return jnp.where(lane % 2 == 0, nxt, prv)


def _rms(x):
    return lax.rsqrt(jnp.mean(x * x, axis=-1, keepdims=True) + NORM_EPS)


def _rms_bwd(dxh, xh, r):
    return r * (dxh - xh * jnp.mean(dxh * xh, axis=-1, keepdims=True))


class _Comm(NamedTuple):
    name: str
    ins: tuple
    out_shapes: tuple
    aliases: dict
    n_sems: int
    phases: tuple


def _join_comms(*comms):
    offs, i_off, o_off, s_off = [], 0, 0, 0
    for cm in comms:
        offs.append((i_off, o_off, s_off))
        i_off, o_off, s_off = i_off + len(cm.ins), o_off + len(cm.out_shapes), s_off + cm.n_sems

    def phase(k):
        def run(ins, outs, ssem, rsem, base):
            sends, recvs = [], []
            for cm, (io, oo, so) in zip(comms, offs):
                if k < len(cm.phases):
                    s, r = cm.phases[k](ins[io:io + len(cm.ins)], outs[oo:oo + len(cm.out_shapes)], ssem, rsem,
                                        base + so)
                    sends += s
                    recvs += r
            return sends, recvs
        return run

    aliases = {}
    for cm, (io, oo, _) in zip(comms, offs):
        aliases.update({io + a: oo + b for a, b in cm.aliases.items()})
    return _Comm("+".join(cm.name for cm in comms), sum((cm.ins for cm in comms), ()),
                 sum((cm.out_shapes for cm in comms), ()), aliases, s_off,
                 tuple(phase(k) for k in range(max(len(cm.phases) for cm in comms))))


def _run_phases(comm, cins, couts, ssem, rsem, first_started):
    for k, phase in enumerate(comm.phases):
        sends, recvs = phase(cins, couts, ssem, rsem, 0)
        if k > 0 or not first_started:
            for cp in sends:
                cp.start()
        for cp in recvs:
            cp.wait_recv()
        for cp in sends:
            cp.wait_send()


def _call(body, args, comm, *, name, grid, in_specs, out_specs, out_shape, scratch_shapes=(),
          compiler_params, aliases=None):
    n_in, n_out, n_sc = len(in_specs), len(out_specs), len(scratch_shapes)
    n_ci, n_co = len(comm.ins), len(comm.out_shapes)
    io_alias = dict(aliases or {})
    io_alias.update({n_in + a: n_out + b for a, b in comm.aliases.items()})

    def kernel_body(*refs):
        ins, cins = refs[:n_in], refs[n_in:n_in + n_ci]
        outs = refs[n_in + n_ci:n_in + n_ci + n_out]
        couts = refs[n_in + n_ci + n_out:n_in + n_ci + n_out + n_co]
        scratch = refs[n_in + n_ci + n_out + n_co:n_in + n_ci + n_out + n_co + n_sc]
        ssem, rsem = refs[-2:]
        first = functools.reduce(jnp.logical_and, [pl.program_id(k) == 0 for k in range(len(grid))])
        last = functools.reduce(jnp.logical_and, [pl.program_id(k) == grid[k] - 1 for k in range(len(grid))])

        @pl.when(first)
        def _():
            for cp in comm.phases[0](cins, couts, ssem, rsem, 0)[0]:
                cp.start()

        body(*ins, *outs, *scratch)

        @pl.when(last)
        def _():
            _run_phases(comm, cins, couts, ssem, rsem, True)

    res = pl.pallas_call(
        kernel_body, name=name + "+" + comm.name, grid=grid, in_specs=list(in_specs) + [ANY] * n_ci,
        out_specs=tuple(out_specs) + tuple([ANY] * n_co), out_shape=tuple(out_shape) + tuple(comm.out_shapes),
        scratch_shapes=list(scratch_shapes) + [pltpu.SemaphoreType.DMA((comm.n_sems,)),
                                               pltpu.SemaphoreType.DMA((comm.n_sems,))],
        input_output_aliases=io_alias, compiler_params=compiler_params)(*args, *comm.ins)
    return tuple(res[:n_out]), tuple(res[n_out:])


def _run_comm(comm):
    n_ci, n_co = len(comm.ins), len(comm.out_shapes)

    def body(*refs):
        _run_phases(comm, refs[:n_ci], refs[n_ci:n_ci + n_co], refs[-2], refs[-1], False)

    return pl.pallas_call(
        body, name=comm.name, in_specs=[ANY] * n_ci, out_specs=tuple([ANY] * n_co), out_shape=tuple(comm.out_shapes),
        input_output_aliases=dict(comm.aliases),
        scratch_shapes=[pltpu.SemaphoreType.DMA((comm.n_sems,)), pltpu.SemaphoreType.DMA((comm.n_sems,))],
        compiler_params=_cp(has_side_effects=True))(*comm.ins)


def _adaln_fwd(cvec8, w_ada_g, b_ada):
    def body(c_ref, w_ref, b_ref, o_ref):
        cv = c_ref[...]
        sc = (cv * _sigmoid(cv)).astype(BF)
        for s in range(N_SHARD):
            cols = slice(s * ADA_W, (s + 1) * ADA_W)
            o_ref[:, cols] = jnp.dot(sc, w_ref[s], preferred_element_type=F32) + b_ref[:, cols]

    return pl.pallas_call(body, out_shape=SDS((8, 3 * D_MODEL), F32), name="adaln_fwd",
                          compiler_params=_cp(vmem_mb=32))(cvec8, w_ada_g, b_ada)


def _adaln_bwd(cvec, dmod, w_ada_g, comm):
    n_rows = cvec.shape[0]
    def body(c_ref, d_ref, w_ref, gw_ref, gb_ref, dc_ref):
        cv = c_ref[...]
        sg = _sigmoid(cv)
        sc = cv * sg
        dm = d_ref[...]
        gb_ref[...] = jnp.sum(dm, axis=0, keepdims=True)
        dsc = jnp.zeros(cv.shape, F32)
        for s in range(N_SHARD):
            cols = slice(s * ADA_W, (s + 1) * ADA_W)
            gw_ref[s] = _dot(sc, dm[:, cols], 0, 0)
            dsc = dsc + _dot(dm[:, cols], w_ref[s], 1, 1)
        dc_ref[...] = dsc * (sg * (1.0 + cv * (1.0 - sg)))

    def whole(shape):
        return pl.BlockSpec(shape, lambda i: (0,) * len(shape))

    shapes = ((N_SHARD, D_MODEL, ADA_W), (1, 3 * D_MODEL), (n_rows, D_MODEL))
    return _call(body, [cvec, dmod, w_ada_g], comm, name="adaln_bwd", grid=(1,),
                 in_specs=[whole(cvec.shape), whole(dmod.shape), whole(w_ada_g.shape)],
                 out_specs=tuple(whole(s) for s in shapes), out_shape=tuple(SDS(s, F32) for s in shapes),
                 compiler_params=_cp(("arbitrary",), 56))


def _big_rows(rows):
    return 1536 if rows % 1536 == 0 else TM


def _norm_fwd(x_lat, x_ctx, norm_w, scale3, shift3, tiles_per_sample, n_samp):
    n_lat = x_lat.shape[0] // TM
    rows = x_lat.shape[0] + x_ctx.shape[0]

    def samp(i):
        return jnp.minimum(i // tiles_per_sample, n_samp)

    def body(x_ref, c_ref, nw_ref, sc_ref, sh_ref, hx_ref, hxt_ref):
        x = jnp.where(pl.program_id(0) < n_lat, x_ref[...], c_ref[...])
        h = x * _rms(x) * nw_ref[...] * (1.0 + sc_ref[...]) + sh_ref[...]
        hx_ref[...] = h.astype(BF)
        hxt_ref[...] = h.T.astype(BF)

    return pl.pallas_call(
        body, name="norm_fwd", grid=(rows // TM,),
        in_specs=[pl.BlockSpec((TM, D_MODEL), lambda i: (jnp.minimum(i, n_lat - 1), 0)),
                  pl.BlockSpec((TM, D_MODEL), lambda i: (jnp.maximum(i - n_lat, 0), 0)),
                  pl.BlockSpec((1, D_MODEL), lambda i: (0, 0)),
                  pl.BlockSpec((None, 1, D_MODEL), lambda i: (samp(i), 0, 0)),
                  pl.BlockSpec((None, 1, D_MODEL), lambda i: (samp(i), 0, 0))],
        out_specs=(pl.BlockSpec((TM, D_MODEL), lambda i: (i, 0)),
                   pl.BlockSpec((D_MODEL, TM), lambda i: (0, i))),
        out_shape=(SDS((rows, D_MODEL), BF), SDS((D_MODEL, rows), BF)),
        compiler_params=_cp(("parallel",), 40))(x_lat, x_ctx, norm_w, scale3, shift3)


def _norm_bwd(x_lat, x_ctx, dhx, gx_res, norm_w, scale3, tiles_per_sample, n_samp):
    rows = x_lat.shape[0] + x_ctx.shape[0]
    n_lat = tiles_per_sample * n_samp

    def samp(i):
        return jnp.minimum(i // tiles_per_sample, n_samp)

    def lat(i):
        return jnp.minimum(i, n_lat - 1)

    def body(x_ref, c_ref, dh_ref, gr_ref, nw_ref, sc_ref, gx_ref, dsh_ref, dsc_ref, dnw_ref):
        i = pl.program_id(0)
        x = jnp.where(i < n_lat, x_ref[...], c_ref[...])
        r = _rms(x)
        xh = x * r
        nw = nw_ref[...]
        dh = dh_ref[...]
        first = jnp.logical_or(i % tiles_per_sample == 0, i >= n_lat)

        @pl.when(first)
        def _():
            dsh_ref[...] = jnp.zeros_like(dsh_ref)
            dsc_ref[...] = jnp.zeros_like(dsc_ref)

        @pl.when(i == 0)
        def _():
            dnw_ref[...] = jnp.zeros_like(dnw_ref)

        dsh_ref[...] += jnp.sum(dh, axis=0, keepdims=True)
        dsc_ref[...] += jnp.sum(dh * (xh * nw), axis=0, keepdims=True)
        du = dh * (1.0 + sc_ref[...])
        dnw_ref[...] += jnp.sum(du * xh, axis=0, keepdims=True)

        @pl.when(i < n_lat)
        def _():
            gx_ref[...] = gr_ref[...] + _rms_bwd(du * nw, xh, r)

    return pl.pallas_call(
        body, name="norm_bwd", grid=(rows // TM,),
        in_specs=[pl.BlockSpec((TM, D_MODEL), lambda i: (lat(i), 0)),
                  pl.BlockSpec((TM, D_MODEL), lambda i: (jnp.maximum(i - n_lat, 0), 0)),
                  pl.BlockSpec((TM, D_MODEL), lambda i: (i, 0)),
                  pl.BlockSpec((TM, D_MODEL), lambda i: (lat(i), 0)),
                  pl.BlockSpec((1, D_MODEL), lambda i: (0, 0)),
                  pl.BlockSpec((None, 1, D_MODEL), lambda i: (samp(i), 0, 0))],
        out_specs=(pl.BlockSpec((TM, D_MODEL), lambda i: (lat(i), 0)),
                   pl.BlockSpec((None, 1, D_MODEL), lambda i: (samp(i), 0, 0)),
                   pl.BlockSpec((None, 1, D_MODEL), lambda i: (samp(i), 0, 0)),
                   pl.BlockSpec((1, D_MODEL), lambda i: (0, 0))),
        out_shape=(SDS((n_lat * TM, D_MODEL), F32), SDS((n_samp + 1, 1, D_MODEL), F32),
                   SDS((n_samp + 1, 1, D_MODEL), F32), SDS((1, D_MODEL), F32)),
        compiler_params=_cp(("arbitrary",), 40))(x_lat, x_ctx, dhx, gx_res, norm_w, scale3)


def _in_proj_gather(hx, w_buf, ids):
    rows = hx.shape[0]
    tb = _big_rows(rows)
    n_i = rows // tb
    hrows = D_MODEL // 2

    def body(ids_ref, h_ref, w_in_hbm, px_ref, w_hbm, wv, lsem, ssem, rsem):
        j, i = pl.program_id(0), pl.program_id(1)
        x, y, c, chips = _place()
        sibling = (x, y, 1 - c)

        def half(s, which):
            return w_hbm.at[s, pl.ds(which * hrows, hrows), :]

        def over_ici(rel):
            chip = chips[rel]
            mine, theirs = half(2 * x + y, c), half(2 * chip[0] + chip[1], c)
            return (_remote(mine, mine, ssem.at[rel], rsem.at[rel], (*chip, c)),
                    _remote(theirs, theirs, ssem.at[rel], rsem.at[rel], (*chip, c)))

        def over_d2d(rel):
            s = 2 * chips[rel][0] + chips[rel][1]
            return (_remote(half(s, c), half(s, c), ssem.at[3 + rel], rsem.at[3 + rel], sibling),
                    _remote(half(s, 1 - c), half(s, 1 - c), ssem.at[3 + rel], rsem.at[3 + rel], sibling))

        first_row_tile = i == 0

        @pl.when(jnp.logical_and(j == 0, first_row_tile))
        def _():
            over_ici(0)[0].start()
            over_ici(1)[0].start()

        for rel in range(3):
            @pl.when(jnp.logical_and(j == rel + 1, first_row_tile))
            def _(rel=rel):
                over_ici(rel)[1].wait_recv()
                passed, landing = over_d2d(rel)
                passed.start()
                if rel == 0:
                    over_ici(2)[0].start()
                landing.wait_recv()

        @pl.when(first_row_tile)
        def _():
            cp = pltpu.make_async_copy(w_hbm.at[ids_ref[4 + j]], wv, lsem)
            cp.start()
            cp.wait()

        px_ref[...] = jnp.dot(h_ref[...], wv[...], preferred_element_type=F32).astype(BF)

        @pl.when(jnp.logical_and(j == N_SHARD - 1, i == n_i - 1))
        def _():
            for rel in range(3):
                over_ici(rel)[0].wait_send()
                over_d2d(rel)[0].wait_send()

    return pl.pallas_call(
        body, name="in_proj_gather", input_output_aliases={2: 1},
        grid_spec=pltpu.PrefetchScalarGridSpec(
            num_scalar_prefetch=1, grid=(N_SHARD, n_i),
            in_specs=[pl.BlockSpec((tb, D_MODEL), lambda j, i, ids_ref: (i, 0)), ANY],
            out_specs=(pl.BlockSpec((tb, IN_W), lambda j, i, ids_ref: (i, ids_ref[4 + j])), ANY),
            scratch_shapes=[pltpu.VMEM((D_MODEL, IN_W), BF), pltpu.SemaphoreType.DMA,
                            pltpu.SemaphoreType.DMA((6,)), pltpu.SemaphoreType.DMA((6,))]),
        out_shape=(SDS((rows, IN_COLS), BF), SDS(w_buf.shape, w_buf.dtype)),
        compiler_params=_cp(("arbitrary", "arbitrary"), 56))(ids, hx, w_buf)


def _gw_in(hxt, dpx_all):
    rows = dpx_all.shape[0]
    tb = _big_rows(rows)

    def body(h_ref, d_ref, o_ref):
        @pl.when(pl.program_id(1) == 0)
        def _():
            o_ref[...] = jnp.zeros_like(o_ref)

        o_ref[...] += jnp.dot(h_ref[...], d_ref[...], preferred_element_type=F32)

    return pl.pallas_call(
        body, name="gw_in", grid=(N_IN_BLK, rows // tb),
        in_specs=[pl.BlockSpec((D_MODEL, tb), lambda j, i: (0, i)),
                  pl.BlockSpec((tb, IN_BLK), lambda j, i: (i, j))],
        out_specs=pl.BlockSpec((None, D_MODEL, IN_BLK), lambda j, i: (j // BPS, 0, j % BPS)),
        out_shape=SDS((N_SHARD, D_MODEL, IN_W), F32),
        compiler_params=_cp(("arbitrary", "arbitrary"), 56))(hxt, dpx_all)


def _dhx(dpx_all, w_in_g, tile0, n_tiles, dhx, comm):
    rows = dpx_all.shape[0]
    tb = rows // DHX_TILES

    def body(d_ref, w_ref, *rest):
        o_ref = rest[-1]

        @pl.when(pl.program_id(1) == 0)
        def _():
            o_ref[...] = jnp.zeros_like(o_ref)

        o_ref[...] += lax.dot_general(d_ref[...], w_ref[...], (((1,), (1,)), ((), ())), preferred_element_type=F32)

    args, in_specs, aliases = [dpx_all, w_in_g], [
        pl.BlockSpec((tb, IN_BLK), lambda i, j: (tile0 + i, j)),
        pl.BlockSpec((None, D_MODEL, IN_BLK), lambda i, j: (j // BPS, 0, j % BPS))], None
    if dhx is not None:
        args, in_specs, aliases = args + [dhx], in_specs + [ANY], {2: 0}
    (out,), got = _call(body, args, comm, name="dhx", grid=(n_tiles, N_IN_BLK), in_specs=in_specs,
                        out_specs=(pl.BlockSpec((tb, D_MODEL), lambda i, j: (tile0 + i, 0)),),
                        out_shape=(SDS((rows, D_MODEL), F32),), aliases=aliases,
                        compiler_params=_cp(("arbitrary", "arbitrary"), 56))
    return out, got


def _decays(lgv, d):
    c = RET_CHUNK
    ii = lax.broadcasted_iota(jnp.int32, (c, 1), 0).astype(F32)
    jj = lax.broadcasted_iota(jnp.int32, (1, c), 1).astype(F32)
    a_i = jnp.where(d == 0, ii, c - 1.0 - ii)
    a_j = jnp.where(d == 0, jj, c - 1.0 - jj)
    rel = a_i - a_j
    mask = jnp.where(rel >= 0, jnp.exp(lgv * jnp.maximum(rel, 0.0)), 0.0)
    qd = jnp.exp(lgv * (a_i + 1.0))
    kd = jnp.exp(lgv * (c - 1.0 - a_i))
    gc = jnp.exp(jnp.full((1, 1), lgv * c, F32))
    return a_i, rel, mask, qd, kd, gc


def _ctx_state_fwd(px, lg, n_samp, t_lat, lc):
    rb = t_lat // lc

    def body(lg_ref, k_ref, v_ref, o_ref):
        h = pl.program_id(1)
        k = k_ref[...].astype(F32) * (RET_DK ** -0.5)
        v = v_ref[...]
        pos = lax.broadcasted_iota(jnp.int32, (lc, 1), 0).astype(F32)
        o_ref[0] = _dot(k * jnp.exp(lg_ref[0, h] * (lc - 1.0 - pos)), v, 0, 0)
        o_ref[1] = _dot(k * jnp.exp(lg_ref[1, h] * pos), v, 0, 0)

    return pl.pallas_call(
        body, name="ctx_state_fwd", grid=(n_samp, RET_HEADS),
        in_specs=[SMEM,
                  pl.BlockSpec((lc, RET_DK), lambda b, h: (rb + b, C_RK // RET_DK + h)),
                  pl.BlockSpec((lc, RET_DV), lambda b, h: (rb + b, C_RV // RET_DV + h))],
        out_specs=pl.BlockSpec((None, 2, None, RET_DK, RET_DV), lambda b, h: (b, 0, h, 0, 0)),
        out_shape=SDS((n_samp, 2, RET_HEADS, RET_DK, RET_DV), F32),
        compiler_params=_cp(("parallel", "parallel")))(lg, px, px)


def _ctx_state_bwd(dpx, px, dstates, lg, n_samp, t_lat, lc):
    rb = t_lat // lc
    kspec = pl.BlockSpec((lc, RET_DK), lambda b, h: (rb + b, C_RK // RET_DK + h))
    vspec = pl.BlockSpec((lc, RET_DV), lambda b, h: (rb + b, C_RV // RET_DV + h))
    sspec = pl.BlockSpec((None, 2, None, RET_DK, RET_DV), lambda b, h: (b, 0, h, 0, 0))

    def weights(lg_ref, h):
        pos = lax.broadcasted_iota(jnp.int32, (lc, 1), 0).astype(F32)
        e_f = lc - 1.0 - pos
        return pos, e_f, jnp.exp(lg_ref[0, h] * e_f), jnp.exp(lg_ref[1, h] * pos)

    def k_body(lg_ref, dpx_hbm, k_ref, v_ref, ds_ref, dk_ref, dlg_ref):
        pos, e_f, w_f, w_b = weights(lg_ref, pl.program_id(1))
        k = k_ref[...].astype(F32) * (RET_DK ** -0.5)
        y_f = _dot(v_ref[...], ds_ref[0], 1, 1) * w_f
        y_b = _dot(v_ref[...], ds_ref[1], 1, 1) * w_b
        dk_ref[...] = ((y_f + y_b) * (RET_DK ** -0.5)).astype(BF)
        t_f = _sum_all(e_f * k * y_f)
        t_b = _sum_all(pos * k * y_b)
        sub = lax.broadcasted_iota(jnp.int32, (8, 128), 0)
        dlg_ref[...] = jnp.where(sub == 0, t_f, jnp.where(sub == 1, t_b, 0.0))

    def v_body(lg_ref, dpx_hbm, k_ref, ds_ref, dv_ref):
        _, _, w_f, w_b = weights(lg_ref, pl.program_id(1))
        k = k_ref[...].astype(F32) * (RET_DK ** -0.5)
        dv_ref[...] = (_dot(k * w_f, ds_ref[0]) + _dot(k * w_b, ds_ref[1])).astype(BF)

    dpx, dlg = pl.pallas_call(
        k_body, name="ctx_state_bwd_k", grid=(n_samp, RET_HEADS), input_output_aliases={1: 0},
        in_specs=[SMEM, ANY, kspec, vspec, sspec],
        out_specs=(kspec, pl.BlockSpec((None, None, 8, 128), lambda b, h: (b, h, 0, 0))),
        out_shape=(SDS(dpx.shape, dpx.dtype), SDS((n_samp, RET_HEADS, 8, 128), F32)),
        compiler_params=_cp(("parallel", "parallel")))(lg, dpx, px, px, dstates)
    dpx = pl.pallas_call(
        v_body, name="ctx_state_bwd_v", grid=(n_samp, RET_HEADS), input_output_aliases={1: 0},
        in_specs=[SMEM, ANY, kspec, sspec], out_specs=vspec, out_shape=SDS(dpx.shape, dpx.dtype),
        compiler_params=_cp(("parallel", "parallel")))(lg, dpx, px, dstates)
    return dpx, dlg


def _zero_ctx_tail(dpx, t_lat):
    wb = 512
    n_ctx = (dpx.shape[0] - t_lat) // TM

    def body(dpx_hbm, o_ref):
        o_ref[...] = jnp.zeros_like(o_ref)

    return pl.pallas_call(
        body, name="zero_ctx_tail", grid=(n_ctx, (IN_COLS - KV_COLS) // wb), input_output_aliases={0: 0},
        in_specs=[ANY], out_specs=pl.BlockSpec((TM, wb), lambda i, j: (t_lat // TM + i, KV_COLS // wb + j)),
        out_shape=SDS(dpx.shape, dpx.dtype),
        compiler_params=_cp(("parallel", "parallel")))(dpx)


def _ret_specs(row_f, row_b):
    c = RET_CHUNK
    wq = RET_HEADS * RET_DK // 2
    wv = RET_HEADS * RET_DV // 2
    specs = []
    for row in (row_f, row_b):
        specs += [pl.BlockSpec((c, wq), lambda b, n, row=row: (row(b, n), C_RQ // wq)),
                  pl.BlockSpec((c, wq), lambda b, n, row=row: (row(b, n), C_RQ // wq + 1)),
                  pl.BlockSpec((c, 2 * wq), lambda b, n, row=row: (row(b, n), C_RK // (2 * wq))),
                  pl.BlockSpec((c, wv), lambda b, n, row=row: (row(b, n), C_RV // wv)),
                  pl.BlockSpec((c, wv), lambda b, n, row=row: (row(b, n), C_RV // wv + 1))]
    return specs


def _ret_head(refs, h):
    q0, q1, k_ref, v0, v1 = refs
    hh = h % 2
    q = (q0, q1)[h // 2][:, hh * RET_DK:(hh + 1) * RET_DK].astype(F32)
    k = k_ref[:, h * RET_DK:(h + 1) * RET_DK].astype(F32) * (RET_DK ** -0.5)
    v = (v0, v1)[h // 2][:, hh * RET_DV:(hh + 1) * RET_DV]
    return q, k, v


def _ret_fwd(px, states0, lg, n_samp, seq, comm):
    c = RET_CHUNK
    nc = seq // c
    t_lat = n_samp * seq
    wo = RET_HEADS * RET_DV

    def row_f(b, n):
        return b * nc + n

    def row_b(b, n):
        return b * nc + nc - 1 - n

    def body(lg_ref, *refs):
        ins, (s0_ref, of_ref, ob_ref, st_ref, s_s) = refs[:10], refs[10:]

        @pl.when(pl.program_id(1) == 0)
        def _():
            s_s[...] = s0_ref[...]

        for d, o_ref in ((0, of_ref), (1, ob_ref)):
            for h in range(RET_HEADS):
                _, _, mask, qd, kd, gc = _decays(lg_ref[d, h], d)
                q, k, v = _ret_head(ins[5 * d:5 * d + 5], h)
                s = s_s[d, h]
                st_ref[h, d] = s.astype(BF)
                sc = _dot(q, k, 1, 1) * mask
                o_ref[:, h * RET_DV:(h + 1) * RET_DV] = (_dot(sc, v) + _dot(q * qd, s)).astype(BF)
                s_s[d, h] = s * gc + _dot(k * kd, v, 0, 0)

    return _call(
        body, [lg] + [px] * 10 + [states0], comm, name="ret_fwd", grid=(n_samp, nc),
        in_specs=[SMEM] + _ret_specs(row_f, row_b) + [
            pl.BlockSpec((None, 2, RET_HEADS, RET_DK, RET_DV), lambda b, n: (b, 0, 0, 0, 0))],
        out_specs=(pl.BlockSpec((c, wo), lambda b, n: (row_f(b, n), 0)),
                   pl.BlockSpec((c, wo), lambda b, n: (row_b(b, n), 0)),
                   pl.BlockSpec((None, RET_HEADS, 2, None, RET_DK, RET_DV), lambda b, n: (b, 0, 0, n, 0, 0))),
        out_shape=(SDS((t_lat, wo), BF), SDS((t_lat, wo), BF),
                   SDS((n_samp, RET_HEADS, 2, nc, RET_DK, RET_DV), BF)),
        scratch_shapes=[pltpu.VMEM((2, RET_HEADS, RET_DK, RET_DV), F32)],
        compiler_params=_cp(("arbitrary", "arbitrary"), 48))


def _ret_bwd(dpx, px, do, saved, lg, n_samp, seq, comm):
    c = RET_CHUNK
    nc = seq // c
    assert nc % 2 == 0
    wq, wo = RET_HEADS * RET_DK, RET_HEADS * RET_DV

    def row_f(b, n):
        return b * nc + nc - 1 - n

    def row_b(b, n):
        return b * nc + n

    def body(lg_ref, *refs):
        ins = refs[:10]
        (dof_ref, dob_ref, st_ref, dpx_in, dpx_hbm, ds0_ref, dlg_ref,
         ds_s, acc_s, sq_s, sk_s, sv_s, sems) = refs[10:]
        b, n = pl.program_id(0), pl.program_id(1)
        second = n >= nc // 2
        chunks = (nc - 1 - n, n)

        def parked(ch):
            return pl.ds(pl.multiple_of(ch * c, c), c)

        def flush():
            cps = []
            for d, ch in enumerate(chunks):
                rows = pl.ds(pl.multiple_of((b * nc + ch) * c, c), c)
                cps += [pltpu.make_async_copy(sq_s.at[parked(ch), :], dpx_hbm.at[rows, pl.ds(C_RQ, wq)], sems.at[3 * d]),
                        pltpu.make_async_copy(sk_s.at[parked(ch), :], dpx_hbm.at[rows, pl.ds(C_RK, wq)],
                                              sems.at[3 * d + 1]),
                        pltpu.make_async_copy(sv_s.at[parked(ch), :], dpx_hbm.at[rows, pl.ds(C_RV, wo)],
                                              sems.at[3 * d + 2])]
            return cps

        @pl.when(jnp.logical_or(n > nc // 2, jnp.logical_and(n == 0, b > 0)))
        def _():
            for cp in flush():
                cp.wait()

        @pl.when(n == 0)
        def _():
            ds_s[...] = jnp.zeros_like(ds_s)
            acc_s[...] = jnp.zeros_like(acc_s)

        def chains(first_visit):
            for d, do_ref in enumerate((dof_ref, dob_ref)):
                rows = parked(chunks[d])
                for h in range(RET_HEADS):
                    a_i, rel, mask, qd, kd, gc = _decays(lg_ref[d, h], d)
                    q, k, v = _ret_head(ins[5 * d:5 * d + 5], h)
                    qb, kb, vb = q.astype(BF), k.astype(BF), v.astype(BF)
                    cq, cv = slice(h * RET_DK, (h + 1) * RET_DK), slice(h * RET_DV, (h + 1) * RET_DV)
                    dob = do_ref[:, cv].astype(BF)
                    sb = st_ref[h, d]
                    ds = ds_s[d, h]
                    dsb = ds.astype(BF)
                    raw = _dot(qb, kb, 1, 1)
                    sc = raw * mask
                    dsc = _dot(dob, vb, 1, 1) * mask
                    dscb = dsc.astype(BF)
                    x = _dot(dob, sb, 1, 1)
                    y = _dot(vb, dsb, 1, 1)
                    qq = q * qd
                    kk = k * kd
                    dq = _dot(dscb, kb) + x * qd
                    dk = _dot(dscb, qb, 0, 0) + y * kd
                    dv = _dot(sc, dob, 0, 0) + _dot(kk, dsb)
                    if first_visit:
                        sq_s[rows, cq] = dq.astype(BF)
                        sk_s[rows, cq] = dk.astype(BF)
                        sv_s[rows, cv] = dv.astype(BF)
                    else:
                        sq_s[rows, cq] = (sq_s[rows, cq].astype(F32) + dq).astype(BF)
                        sk_s[rows, cq] = ((sk_s[rows, cq].astype(F32) + dk) * (RET_DK ** -0.5)).astype(BF)
                        sv_s[rows, cv] = (sv_s[rows, cv].astype(F32) + dv).astype(BF)
                    t = (_sum_all(dsc * raw * rel) + _sum_all((a_i + 1.0) * qq * x)
                         + _sum_all((c - 1.0 - a_i) * kk * y) + c * gc * _sum_all(ds * sb.astype(F32)))
                    acc_s[4 * d + h:4 * d + h + 1, :] += t
                    ds_s[d, h] = ds * gc + _dot(qq, dob, 0, 0)

        @pl.when(jnp.logical_not(second))
        def _():
            chains(True)

        @pl.when(second)
        def _():
            chains(False)
            for cp in flush():
                cp.start()

        @pl.when(n == nc - 1)
        def _():
            ds0_ref[...] = ds_s[...]
            dlg_ref[...] = acc_s[...]

        @pl.when(jnp.logical_and(b == n_samp - 1, n == nc - 1))
        def _():
            for cp in flush():
                cp.wait()

    do_spec_f = pl.BlockSpec((c, wo), lambda b, n: (row_f(b, n), 0))
    do_spec_b = pl.BlockSpec((c, wo), lambda b, n: (row_b(b, n), 0))
    return _call(
        body, [lg] + [px] * 10 + [do, do, saved, dpx], comm, name="ret_bwd", grid=(n_samp, nc), aliases={14: 0},
        in_specs=[SMEM] + _ret_specs(row_f, row_b) + [
            do_spec_f, do_spec_b,
            pl.BlockSpec((None, RET_HEADS, 2, None, RET_DK, RET_DV), lambda b, n: (b, 0, 0, nc - 1 - n, 0, 0)),
            ANY],
        out_specs=(ANY,
                   pl.BlockSpec((None, 2, RET_HEADS, RET_DK, RET_DV), lambda b, n: (b, 0, 0, 0, 0)),
                   pl.BlockSpec((None, 8, 128), lambda b, n: (b, 0, 0))),
        out_shape=(SDS(dpx.shape, dpx.dtype),
                   SDS((n_samp, 2, RET_HEADS, RET_DK, RET_DV), F32), SDS((n_samp, 8, 128), F32)),
        scratch_shapes=[pltpu.VMEM((2, RET_HEADS, RET_DK, RET_DV), F32), pltpu.VMEM((8, 128), F32),
                        pltpu.VMEM((seq, wq), BF), pltpu.VMEM((seq, wq), BF), pltpu.VMEM((seq, wo), BF),
                        pltpu.SemaphoreType.DMA((6,))],
        compiler_params=_cp(("arbitrary", "arbitrary"), 60))


def _norm_rope(x, w, cos, sin):
    xn = x * _rms(x) * w
    return xn * cos + _swap_pairs(xn) * sin


def _norm_rope_bwd(dy, x, w, cos, sin):
    dxn = dy * cos + _swap_pairs(dy * sin)
    r = _rms(x)
    xh = x * r
    return _rms_bwd(dxn * w, xh, r), jnp.sum(dxn * xh, axis=0, keepdims=True)


def _att_prep_q(px, cos_all, sin_all, qnw, t_lat):
    hd = ATT_HEAD_DIM
    wblk = ATT_REP * hd

    def body(x_ref, cos_ref, sin_ref, w_ref, o_ref):
        for r in range(ATT_REP):
            cols = slice(r * hd, (r + 1) * hd)
            qr = _norm_rope(x_ref[:, cols].astype(F32), w_ref[...], cos_ref[...], sin_ref[...])
            o_ref[:, cols] = (qr * (hd ** -0.5)).astype(BF)

    return pl.pallas_call(
        body, name="att_prep_q", grid=(t_lat // TM, ATT_KV_HEADS),
        in_specs=[pl.BlockSpec((TM, wblk), lambda i, g: (i, C_AQ // wblk + g)),
                  pl.BlockSpec((TM, hd), lambda i, g: (i, 0)),
                  pl.BlockSpec((TM, hd), lambda i, g: (i, 0)),
                  pl.BlockSpec((1, hd), lambda i, g: (0, 0))],
        out_specs=pl.BlockSpec((TM, wblk), lambda i, g: (i, g)),
        out_shape=SDS((t_lat, ATT_HEADS * hd), BF),
        compiler_params=_cp(("parallel", "parallel")))(px, cos_all, sin_all, qnw)


def _att_prep_kv(px, cos_all, sin_all, knw):
    rows = px.shape[0]
    hd = ATT_HEAD_DIM
    kvw = ATT_KV_HEADS * hd

    def body(x_ref, cos_ref, sin_ref, w_ref, k_ref, v_ref):
        for g in range(ATT_KV_HEADS):
            cols = slice(g * hd, (g + 1) * hd)
            k_ref[:, cols] = _norm_rope(x_ref[:, cols].astype(F32), w_ref[...], cos_ref[...],
                                        sin_ref[...]).astype(BF)
            v_ref[:, 2 * g * hd:(2 * g + 1) * hd] = x_ref[:, kvw + g * hd:kvw + (g + 1) * hd].astype(BF)
            v_ref[:, (2 * g + 1) * hd:(2 * g + 2) * hd] = jnp.ones((TM, hd), BF)

    return pl.pallas_call(
        body, name="att_prep_kv", grid=(rows // TM,),
        in_specs=[pl.BlockSpec((TM, 2 * kvw), lambda i: (i, C_AK // (2 * kvw))),
                  pl.BlockSpec((TM, hd), lambda i: (i, 0)),
                  pl.BlockSpec((TM, hd), lambda i: (i, 0)),
                  pl.BlockSpec((1, hd), lambda i: (0, 0))],
        out_specs=(pl.BlockSpec((TM, kvw), lambda i: (i, 0)), pl.BlockSpec((TM, 2 * kvw), lambda i: (i, 0))),
        out_shape=(SDS((rows, kvw), BF), SDS((rows, 2 * kvw), BF)),
        compiler_params=_cp(("parallel",)))(px, cos_all, sin_all, knw)


def _att_kv_bwd(dpx, dkl, dkc, dvl, dvc, px, cos_all, sin_all, knw):
    rows = px.shape[0]
    hd = ATT_HEAD_DIM
    kvw = ATT_KV_HEADS * hd
    n_lat = dkl.shape[0] // TM
    assert dkc.shape[0] == TM

    def body(dpx_hbm, dkl_ref, dkc_ref, dvl_ref, dvc_ref, x_ref, cos_ref, sin_ref, w_ref, o_ref, gw_ref):
        i = pl.program_id(0)

        @pl.when(i == 0)
        def _():
            gw_ref[...] = jnp.zeros_like(gw_ref)

        is_lat = i < n_lat
        dk = jnp.where(is_lat, dkl_ref[...], dkc_ref[...])
        dv = jnp.where(is_lat, dvl_ref[...], dvc_ref[...])
        for g in range(ATT_KV_HEADS):
            cols = slice(g * hd, (g + 1) * hd)
            dx, gw = _norm_rope_bwd(dk[:, cols], x_ref[:, cols].astype(F32), w_ref[...], cos_ref[...], sin_ref[...])
            o_ref[:, cols] = dx.astype(BF)
            gw_ref[...] += gw
        o_ref[:, kvw:] = dv.astype(BF)

    lat = pl.BlockSpec((TM, kvw), lambda i: (jnp.minimum(i, n_lat - 1), 0))
    ctx = pl.BlockSpec((TM, kvw), lambda i: (0, 0))
    kvcol = pl.BlockSpec((TM, 2 * kvw), lambda i: (i, C_AK // (2 * kvw)))
    return pl.pallas_call(
        body, name="att_kv_bwd", grid=(rows // TM,), input_output_aliases={0: 0},
        in_specs=[ANY, lat, ctx, lat, ctx, kvcol,
                  pl.BlockSpec((TM, hd), lambda i: (i, 0)),
                  pl.BlockSpec((TM, hd), lambda i: (i, 0)),
                  pl.BlockSpec((1, hd), lambda i: (0, 0))],
        out_specs=(kvcol, pl.BlockSpec((1, hd), lambda i: (0, 0))),
        out_shape=(SDS(dpx.shape, dpx.dtype), SDS((1, hd), F32)),
        compiler_params=_cp(("arbitrary",)))(dpx, dkl, dkc, dvl, dvc, px, cos_all, sin_all, knw)


def _stack_heads(ref_or_val):
    hd = ATT_HEAD_DIM
    return jnp.concatenate([ref_or_val[:, r * hd:(r + 1) * hd] for r in range(ATT_REP)], axis=0)


def _att_scores(q, kl, kc):
    sl = _dot(q, kl, 1, 1)
    sc = _dot(q, kc, 1, 1)
    m = jnp.maximum(jnp.max(sl, axis=-1, keepdims=True), jnp.max(sc, axis=-1, keepdims=True))
    return jnp.exp(sl - m), jnp.exp(sc - m), m


def _att_fwd(qn, kn, vn, n_samp, seq, lc):
    hd = ATT_HEAD_DIM
    tq = ATT_TQ
    nq = seq // tq
    wblk = ATT_REP * hd
    cb = n_samp * seq // lc
    t_lat = n_samp * seq

    def body(q_ref, kl_ref, kc_ref, vl_ref, vc_ref, o_ref, lse_ref):
        lane = lax.broadcasted_iota(jnp.int32, (tq, hd), 1)
        lse = jnp.zeros((tq, hd), F32)
        for r in range(ATT_REP):
            cols = slice(r * hd, (r + 1) * hd)
            el, ec, m = _att_scores(q_ref[:, cols], kl_ref[...], kc_ref[...])
            pv = _dot(el, vl_ref[...]) + _dot(ec, vc_ref[...])
            denom = pv[:, hd:hd + 1]
            o_ref[:, cols] = (pv[:, :hd] / denom).astype(BF)
            lse = jnp.where(lane == r, m + jnp.log(denom), lse)
        lse_ref[...] = lse

    return pl.pallas_call(
        body, name="att_fwd", grid=(n_samp, ATT_KV_HEADS, nq),
        in_specs=[pl.BlockSpec((tq, wblk), lambda b, g, i: (b * nq + i, g)),
                  pl.BlockSpec((seq, hd), lambda b, g, i: (b, g)),
                  pl.BlockSpec((lc, hd), lambda b, g, i: (cb + b, g)),
                  pl.BlockSpec((seq, 2 * hd), lambda b, g, i: (b, g)),
                  pl.BlockSpec((lc, 2 * hd), lambda b, g, i: (cb + b, g))],
        out_specs=(pl.BlockSpec((tq, wblk), lambda b, g, i: (b * nq + i, g)),
                   pl.BlockSpec((tq, hd), lambda b, g, i: (b * nq + i, g))),
        out_shape=(SDS((t_lat, ATT_HEADS * hd), BF), SDS((t_lat, ATT_KV_HEADS * hd), F32)),
        compiler_params=_cp(("parallel", "parallel", "parallel"), 48))(qn, kn, kn, vn, vn)


def _att_bwd(dpx, qn, kn, vn, px, o_att, lse, do_att, cos_all, sin_all, qnw, n_samp, seq, lc, comm):
    hd = ATT_HEAD_DIM
    tq = ATT_TQ
    nq = seq // tq
    wblk = ATT_REP * hd
    cb = n_samp * seq // lc
    t_lat = n_samp * seq
    kvw = ATT_KV_HEADS * hd
    scale = hd ** -0.5

    def body(dpx_hbm, q_ref, kl_ref, kc_ref, vl_ref, vc_ref, o_ref, do_ref, x_ref, cos_ref, sin_ref, w_ref,
             lse_ref, dq_ref, dkl_ref, dkc_ref, dvl_ref, dvc_ref, gw_ref, akl, akc, avl, avc, aw):
        i = pl.program_id(2)

        @pl.when(i == 0)
        def _():
            akl[...] = jnp.zeros_like(akl)
            akc[...] = jnp.zeros_like(akc)
            avl[...] = jnp.zeros_like(avl)
            avc[...] = jnp.zeros_like(avc)
            aw[...] = jnp.zeros_like(aw)

        dobs, pls, pcs, dsls, dscs = [], [], [], [], []
        for r in range(ATT_REP):
            cols = slice(r * hd, (r + 1) * hd)
            dob = do_ref[:, cols]
            delta = jnp.sum(dob.astype(F32) * o_ref[:, cols].astype(F32), axis=-1, keepdims=True)
            lse = lse_ref[:, r:r + 1]
            p_l = jnp.exp(_dot(q_ref[:, cols], kl_ref[...], 1, 1) - lse).astype(BF)
            p_c = jnp.exp(_dot(q_ref[:, cols], kc_ref[...], 1, 1) - lse).astype(BF)
            ds_l = (p_l * (_dot(dob, vl_ref[...], 1, 1) - delta)).astype(BF)
            ds_c = (p_c * (_dot(dob, vc_ref[...], 1, 1) - delta)).astype(BF)
            dq = (_dot(ds_l, kl_ref[...]) + _dot(ds_c, kc_ref[...])) * scale
            dx, gw = _norm_rope_bwd(dq, x_ref[:, cols].astype(F32), w_ref[...], cos_ref[...], sin_ref[...])
            dq_ref[:, cols] = dx.astype(BF)
            aw[...] += gw
            dobs.append(dob)
            pls.append(p_l)
            pcs.append(p_c)
            dsls.append(ds_l)
            dscs.append(ds_c)
        do4 = jnp.concatenate(dobs, axis=0)
        q4 = _stack_heads(q_ref)
        avl[...] += _dot(jnp.concatenate(pls, axis=0), do4, 0, 0)
        avc[...] += _dot(jnp.concatenate(pcs, axis=0), do4, 0, 0)
        akl[...] += _dot(jnp.concatenate(dsls, axis=0), q4, 0, 0)
        akc[...] += _dot(jnp.concatenate(dscs, axis=0), q4, 0, 0)

        @pl.when(i == nq - 1)
        def _():
            dkl_ref[...] = akl[...]
            dkc_ref[...] = akc[...]
            dvl_ref[...] = avl[...]
            dvc_ref[...] = avc[...]
            gw_ref[...] = aw[...]

    return _call(
        body, [dpx, qn, kn, kn, vn, vn, o_att, do_att, px, cos_all, sin_all, qnw, lse], comm,
        name="att_bwd", grid=(n_samp, ATT_KV_HEADS, nq), aliases={0: 0},
        in_specs=[ANY,
                  pl.BlockSpec((tq, wblk), lambda b, g, i: (b * nq + i, g)),
                  pl.BlockSpec((seq, hd), lambda b, g, i: (b, g)),
                  pl.BlockSpec((lc, hd), lambda b, g, i: (cb + b, g)),
                  pl.BlockSpec((seq, hd), lambda b, g, i: (b, 2 * g)),
                  pl.BlockSpec((lc, hd), lambda b, g, i: (cb + b, 2 * g)),
                  pl.BlockSpec((tq, wblk), lambda b, g, i: (b * nq + i, g)),
                  pl.BlockSpec((tq, wblk), lambda b, g, i: (b * nq + i, g)),
                  pl.BlockSpec((tq, wblk), lambda b, g, i: (b * nq + i, C_AQ // wblk + g)),
                  pl.BlockSpec((tq, hd), lambda b, g, i: (b * nq + i, 0)),
                  pl.BlockSpec((tq, hd), lambda b, g, i: (b * nq + i, 0)),
                  pl.BlockSpec((1, hd), lambda b, g, i: (0, 0)),
                  pl.BlockSpec((tq, hd), lambda b, g, i: (b * nq + i, g))],
        out_specs=(pl.BlockSpec((tq, wblk), lambda b, g, i: (b * nq + i, C_AQ // wblk + g)),
                   pl.BlockSpec((seq, hd), lambda b, g, i: (b, g)),
                   pl.BlockSpec((lc, hd), lambda b, g, i: (b, g)),
                   pl.BlockSpec((seq, hd), lambda b, g, i: (b, g)),
                   pl.BlockSpec((lc, hd), lambda b, g, i: (b, g)),
                   pl.BlockSpec((None, None, 1, hd), lambda b, g, i: (b, g, 0, 0))),
        out_shape=(SDS(dpx.shape, dpx.dtype),
                   SDS((t_lat, kvw), F32), SDS((n_samp * lc, kvw), F32),
                   SDS((t_lat, kvw), F32), SDS((n_samp * lc, kvw), F32),
                   SDS((n_samp, ATT_KV_HEADS, 1, hd), F32)),
        scratch_shapes=[pltpu.VMEM((seq, hd), F32), pltpu.VMEM((lc, hd), F32),
                        pltpu.VMEM((seq, hd), F32), pltpu.VMEM((lc, hd), F32), pltpu.VMEM((1, hd), F32)],
        compiler_params=_cp(("arbitrary", "arbitrary", "arbitrary"), 56))


def _merge(x_lat, target, o_f, o_b, o_att, px, gate3, w_o_ret, w_o_att, w_out, tiles_per_sample):
    t_lat = x_lat.shape[0]
    tm = 256
    n_t = t_lat // tm
    per = tiles_per_sample * (TM // tm)
    d = D_MODEL
    rv = RET_HEADS * RET_DV
    n_samp = gate3.shape[0] - 1

    half = d // 2
    n_px = 10

    def body(x_ref, t_ref, of_ref, ob_ref, oa_ref, *rest):
        pxs, rest = rest[:n_px], rest[n_px:]
        (gt_ref, wor_ref, woa_ref, wout_ref,
         gx_ref, dor_ref, doa_ref, dpx_hbm, loss_ref, dgt_ref, gwor_hbm, gwoa_hbm, gwout_hbm,
         aor, aoa, aout, drg_ref, dtail_ref, sems) = rest
        i = pl.program_id(0)

        def copies(step):
            rows = pl.ds(pl.multiple_of(step * tm, tm), tm)
            return (pltpu.make_async_copy(drg_ref, dpx_hbm.at[rows, pl.ds(C_RG, rv)], sems.at[0]),
                    pltpu.make_async_copy(dtail_ref, dpx_hbm.at[rows, pl.ds(C_AG, 3 * d)], sems.at[1]))

        @pl.when(i == 0)
        def _():
            aor[...] = jnp.zeros_like(aor)
            aoa[...] = jnp.zeros_like(aoa)
            aout[...] = jnp.zeros_like(aout)
            loss_ref[...] = jnp.zeros_like(loss_ref)

        @pl.when(i % per == 0)
        def _():
            dgt_ref[...] = jnp.zeros_like(dgt_ref)

        def cat(refs):
            return jnp.concatenate([r[...] for r in refs], axis=1).astype(F32)

        def ret_head(h):
            cols = slice(h * RET_DV, (h + 1) * RET_DV)
            o = of_ref[:, cols].astype(F32) + ob_ref[:, cols].astype(F32)
            r = _rms(o)
            g = pxs[h][...].astype(F32)
            return o * r, r, g, _sigmoid(g)

        def att_half(k):
            o = oa_ref[:, k * half:(k + 1) * half].astype(F32)
            g = pxs[4 + k][...].astype(F32)
            return o, g, _sigmoid(g)

        yrs = []
        for h in range(RET_HEADS):
            on, _, g, sg = ret_head(h)
            yrs.append((on * (g * sg)).astype(BF))
        yr = jnp.concatenate(yrs, axis=1)
        yas = []
        for k in range(2):
            o, g, sg = att_half(k)
            yas.append((o * (g * sg)).astype(BF))
        ya = jnp.concatenate(yas, axis=1)

        a = jnp.dot(yr, wor_ref[...], preferred_element_type=F32)
        b = jnp.dot(ya, woa_ref[...], preferred_element_type=F32)
        sr = _sigmoid(cat(pxs[6:8]))
        sa = _sigmoid(cat(pxs[8:10]))
        yb = (sr * a + sa * b).astype(BF)
        out = jnp.dot(yb, wout_ref[...], preferred_element_type=F32)
        gate = gt_ref[...]
        err = x_ref[...] + gate * out - t_ref[...]
        loss_ref[...] += 0.5 * _sum_all(err * err) * (1.0 / d)
        dy_tok = err * (1.0 / d)
        gx_ref[...] = dy_tok
        dgt_ref[...] += jnp.sum(dy_tok * out, axis=0, keepdims=True)
        dout = (dy_tok * gate).astype(BF)
        aout[...] += _dot(yb, dout, 0, 0)
        dyy = _dot(dout, wout_ref[...], 1, 1)
        da = (dyy * sr).astype(BF)
        db = (dyy * sa).astype(BF)
        aor[...] += _dot(yr, da, 0, 0)
        aoa[...] += _dot(ya, db, 0, 0)
        dyr = _dot(da, wor_ref[...], 1, 1)
        dya = _dot(db, woa_ref[...], 1, 1)

        @pl.when(i > 0)
        def _():
            for cp in copies(i - 1):
                cp.wait()

        dtail_ref[:, d:2 * d] = (dyy * a * (sr * (1.0 - sr))).astype(BF)
        dtail_ref[:, 2 * d:] = (dyy * b * (sa * (1.0 - sa))).astype(BF)
        for h in range(RET_HEADS):
            cols = slice(h * RET_DV, (h + 1) * RET_DV)
            on, r, g, sg = ret_head(h)
            dy = dyr[:, cols]
            drg_ref[:, cols] = (dy * on * (sg * (1.0 + g * (1.0 - sg)))).astype(BF)
            dor_ref[:, cols] = _rms_bwd(dy * (g * sg), on, r).astype(BF)
        for k in range(2):
            cols = slice(k * half, (k + 1) * half)
            o, g, sg = att_half(k)
            dy = dya[:, cols]
            dtail_ref[:, cols] = (dy * o * (sg * (1.0 + g * (1.0 - sg)))).astype(BF)
            doa_ref[:, cols] = (dy * (g * sg)).astype(BF)
        for cp in copies(i):
            cp.start()

        @pl.when(i == n_t - 1)
        def _():
            for cp in copies(i):
                cp.wait()
            pltpu.sync_copy(aor, gwor_hbm)
            pltpu.sync_copy(aoa, gwoa_hbm)
            pltpu.sync_copy(aout, gwout_hbm)

    def px_blk(col):
        return pl.BlockSpec((tm, half), lambda i: (i, col // half))

    def resident(shape):
        return pl.BlockSpec(shape, lambda i: (0, 0), pipeline_mode=pl.Buffered(1))

    px_cols = ([C_RG + k * half for k in range(4)] + [C_AG, C_AG + half]
               + [C_MR, C_MR + half, C_MA, C_MA + half])
    return pl.pallas_call(
        body, name="merge", grid=(n_t,),
        in_specs=[pl.BlockSpec((tm, d), lambda i: (i, 0)),
                  pl.BlockSpec((tm, d), lambda i: (i, 0)),
                  pl.BlockSpec((tm, rv), lambda i: (i, 0)),
                  pl.BlockSpec((tm, rv), lambda i: (i, 0)),
                  pl.BlockSpec((tm, d), lambda i: (i, 0))]
        + [px_blk(col) for col in px_cols]
        + [pl.BlockSpec((None, 1, d), lambda i: (i // per, 0, 0)),
           resident((rv, d)), resident((d, d)), resident((d, d))],
        out_specs=(pl.BlockSpec((tm, d), lambda i: (i, 0)),
                   pl.BlockSpec((tm, rv), lambda i: (i, 0)),
                   pl.BlockSpec((tm, d), lambda i: (i, 0)),
                   ANY,
                   pl.BlockSpec((8, 128), lambda i: (0, 0)),
                   pl.BlockSpec((None, 1, d), lambda i: (i // per, 0, 0)),
                   ANY, ANY, ANY),
        out_shape=(SDS((t_lat, d), F32), SDS((t_lat, rv), BF), SDS((t_lat, d), BF),
                   SDS((px.shape[0], IN_COLS), BF),
                   SDS((8, 128), F32), SDS((n_samp, 1, d), F32),
                   SDS((rv, d), F32), SDS((d, d), F32), SDS((d, d), F32)),
        scratch_shapes=[pltpu.VMEM((rv, d), F32), pltpu.VMEM((d, d), F32), pltpu.VMEM((d, d), F32),
                        pltpu.VMEM((tm, rv), BF), pltpu.VMEM((tm, 3 * d), BF), pltpu.SemaphoreType.DMA((2,))],
        compiler_params=_cp(("arbitrary",), 56))(
            x_lat, target, o_f, o_b, o_att, *([px] * n_px), gate3, w_o_ret, w_o_att, w_out)


def _place():
    x, y, c = lax.axis_index("x"), lax.axis_index("y"), lax.axis_index("c")
    chips = [(1 - x, y), (x, 1 - y), (1 - x, 1 - y)]
    return x, y, c, chips


def _remote(src, dst, send_sem, recv_sem, to):
    return pltpu.make_async_remote_copy(src_ref=src, dst_ref=dst, send_sem=send_sem, recv_sem=recv_sem,
                                        device_id=to, device_id_type=MESH)


def _place_ids():
    x, y, c = lax.axis_index("x"), lax.axis_index("y"), lax.axis_index("c")
    me = 2 * x + y
    return jnp.stack([x, y, c, me, me, 2 * (1 - x) + y, 2 * x + 1 - y, 2 * (1 - x) + 1 - y]).astype(jnp.int32)


def _ag_comm(bufs):
    n, m = len(bufs), 3

    def half(ref, s, which):
        h = ref.shape[1] // 2
        return ref.at[s, pl.ds(which * h, h), :]

    def ici(ins, outs, ssem, rsem, base):
        x, y, c, chips = _place()
        sends, recvs = [], []
        for a in range(n):
            for j in range(m):
                k, chip = base + a * m + j, chips[j]
                mine, theirs = half(outs[a], 2 * x + y, c), half(outs[a], 2 * chip[0] + chip[1], c)
                sends.append(_remote(mine, mine, ssem.at[k], rsem.at[k], (*chip, c)))
                recvs.append(_remote(theirs, theirs, ssem.at[k], rsem.at[k], (*chip, c)))
        return sends, recvs

    def d2d(ins, outs, ssem, rsem, base):
        x, y, c, chips = _place()
        sends, recvs = [], []
        for a in range(n):
            for j in range(m):
                k, s = base + (n + a) * m + j, 2 * chips[j][0] + chips[j][1]
                sends.append(_remote(half(outs[a], s, c), half(outs[a], s, c), ssem.at[k], rsem.at[k], (x, y, 1 - c)))
                recvs.append(_remote(half(outs[a], s, 1 - c), half(outs[a], s, 1 - c), ssem.at[k], rsem.at[k],
                                     (x, y, 1 - c)))
        return sends, recvs

    return _Comm("all_gather", tuple(bufs), tuple(SDS(b.shape, b.dtype) for b in bufs), {a: a for a in range(n)},
                 2 * n * m, (ici, d2d))


def _swap_comm(grads):
    n = len(grads)

    def phase(ins, outs, ssem, rsem, base):
        x, y, c, _ = _place()
        sends = []
        for a in range(n):
            h = ins[a].shape[1] // 2
            sends.append(_remote(ins[a].at[:, pl.ds((1 - c) * h, h), :], outs[a], ssem.at[base + a],
                                 rsem.at[base + a], (x, y, 1 - c)))
        return sends, sends

    return _Comm("swap_halves", tuple(grads),
                 tuple(SDS((g.shape[0], g.shape[1] // 2, g.shape[2]), g.dtype) for g in grads), {}, n, (phase,))


def _exchange_comm(parts, rels=(0, 1, 2), into=None):
    n, m = len(parts), len(rels)

    def phase(ins, outs, ssem, rsem, base):
        x, y, c, chips = _place()
        sends = []
        for a in range(n):
            for jj, j in enumerate(rels):
                k, chip = base + m * a + jj, chips[j]
                sends.append(_remote(ins[a].at[2 * chip[0] + chip[1]], outs[a].at[j], ssem.at[k], rsem.at[k],
                                     (*chip, c)))
        return sends, sends

    shapes = tuple(SDS((3,) + p.shape[1:], p.dtype) for p in parts)
    if into is None:
        return _Comm("exchange_shards", tuple(parts), shapes, {}, m * n, (phase,))
    return _Comm("exchange_shards", tuple(parts) + tuple(into), shapes, {n + a: a for a in range(n)}, m * n, (phase,))


def _join_comm(bufs):
    n = len(bufs)

    def phase(ins, outs, ssem, rsem, base):
        x, y, c, _ = _place()
        sends, recvs = [], []
        for a in range(n):
            h = outs[a].shape[0] // 2
            mine, other = outs[a].at[pl.ds(c * h, h), :], outs[a].at[pl.ds((1 - c) * h, h), :]
            sends.append(_remote(mine, mine, ssem.at[base + a], rsem.at[base + a], (x, y, 1 - c)))
            recvs.append(_remote(other, other, ssem.at[base + a], rsem.at[base + a], (x, y, 1 - c)))
        return sends, recvs

    return _Comm("join_halves", tuple(bufs), tuple(SDS(b.shape, b.dtype) for b in bufs), {a: a for a in range(n)},
                 n, (phase,))


def _cast_place(w, ids):
    rows, cols = w.shape
    tr = min(rows, 256)

    def body(ids_ref, w_ref, o_ref):
        o_ref[...] = w_ref[...].astype(BF)

    return pl.pallas_call(
        body, name="cast_place",
        grid_spec=pltpu.PrefetchScalarGridSpec(
            num_scalar_prefetch=1, grid=(rows // tr,),
            in_specs=[pl.BlockSpec((tr, cols), lambda i, ids_ref: (i, 0))],
            out_specs=pl.BlockSpec((None, tr, cols), lambda i, ids_ref: (ids_ref[3], i, 0))),
        out_shape=SDS((N_SHARD, rows, cols), BF),
        compiler_params=_cp(("parallel",), 40))(ids, w)


def _chip_sum(g, p, ids):
    n_s, rows, cols = g.shape
    h = rows // 2
    tr = min(h, 256)
    nb = h // tr

    def body(ids_ref, g_ref, p_ref, o_ref, o16_ref):
        t = g_ref[...] + p_ref[...]
        o_ref[...] = t
        o16_ref[...] = t.astype(BF)

    out_spec = pl.BlockSpec((None, tr, cols), lambda s, i, ids_ref: (s, i, 0))
    return pl.pallas_call(
        body, name="chip_sum",
        grid_spec=pltpu.PrefetchScalarGridSpec(
            num_scalar_prefetch=1, grid=(n_s, nb),
            in_specs=[pl.BlockSpec((None, tr, cols), lambda s, i, ids_ref: (s, ids_ref[2] * nb + i, 0)),
                      pl.BlockSpec((None, tr, cols), lambda s, i, ids_ref: (s, i, 0))],
            out_specs=(out_spec, out_spec)),
        out_shape=(SDS((n_s, h, cols), g.dtype), SDS((n_s, h, cols), BF)),
        compiler_params=_cp(("parallel", "parallel"), 40))(ids, g, p)


def _shard_sum(t, q, ids):
    _, h, cols = t.shape
    tr = min(h, 256)
    nb = h // tr

    def body(ids_ref, t_ref, q_ref, o_ref):
        o_ref[...] = ((t_ref[...] + q_ref[0].astype(F32)) + q_ref[1].astype(F32)) + q_ref[2].astype(F32)

    return pl.pallas_call(
        body, name="shard_sum",
        grid_spec=pltpu.PrefetchScalarGridSpec(
            num_scalar_prefetch=1, grid=(nb,),
            in_specs=[pl.BlockSpec((None, tr, cols), lambda i, ids_ref: (ids_ref[3], i, 0)),
                      pl.BlockSpec((3, tr, cols), lambda i, ids_ref: (0, i, 0))],
            out_specs=pl.BlockSpec((tr, cols), lambda i, ids_ref: (ids_ref[2] * nb + i, 0))),
        out_shape=SDS((2 * h, cols), t.dtype),
        compiler_params=_cp(("parallel",), 40))(ids, t, q)


def _gather_small(block, n_sum):
    rows, cols = block.shape
    n_dev = 8

    def body(x_ref, o_ref, g_ref, buf, send_sems, recv_sems, local_sem):
        x, y, c, chips = _place()
        me, sibling = (x, y, c), (x, y, 1 - c)

        def slot(px_, py_, pc_):
            return buf.at[4 * px_ + 2 * py_ + pc_]

        def copy(k, who, to, src=None):
            return _remote(slot(*who) if src is None else src, slot(*who), send_sems.at[k], recv_sems.at[k], to)

        mine = pltpu.make_async_copy(x_ref, slot(*me), local_sem)
        mine.start()
        first = [copy(0, me, sibling, src=x_ref)]
        first += [copy(1 + j, me, (*chip, c), src=x_ref) for j, chip in enumerate(chips)]
        for cp in first:
            cp.start()
        passed = [copy(4 + j, (*chip, c), sibling) for j, chip in enumerate(chips)]
        for j, chip in enumerate(chips):
            copy(1 + j, (*chip, c), me).wait_recv()
            passed[j].start()
        copy(0, sibling, me).wait_recv()
        for j, chip in enumerate(chips):
            copy(4 + j, (*chip, 1 - c), me).wait_recv()
        for cp in first + passed:
            cp.wait_send()
        mine.wait()
        acc = buf[0, :, :n_sum]
        for s in range(1, n_dev):
            acc = acc + buf[s, :, :n_sum]
        o_ref[...] = acc
        for s in range(n_dev):
            g_ref[s * rows:(s + 1) * rows, :] = buf[s, :, n_sum:]

    return pl.pallas_call(
        body, name="gather_small",
        in_specs=[pl.BlockSpec(memory_space=pltpu.VMEM)],
        out_specs=(pl.BlockSpec(memory_space=pltpu.VMEM), pl.BlockSpec(memory_space=pltpu.VMEM)),
        out_shape=(SDS((rows, n_sum), F32), SDS((n_dev * rows, cols - n_sum), F32)),
        scratch_shapes=[pltpu.VMEM((n_dev, rows, cols), F32), pltpu.SemaphoreType.DMA((7,)),
                        pltpu.SemaphoreType.DMA((7,)), pltpu.SemaphoreType.DMA],
        compiler_params=_cp(has_side_effects=True))(block)


def _adam_math(w, g, m, v):
    m = ADAM_B1 * m + (1.0 - ADAM_B1) * g
    v = ADAM_B2 * v + (1.0 - ADAM_B2) * (g * g)
    m_hat = m / (1.0 - ADAM_B1 ** ADAM_STEP)
    v_hat = v / (1.0 - ADAM_B2 ** ADAM_STEP)
    delta = -ADAM_LR * (m_hat / (jnp.sqrt(v_hat) + ADAM_EPS) + ADAM_WD * w)
    return delta, m, v


def _adamw(w, g, m, v):
    rows, cols = w.shape
    tr = min(rows, 256 if cols <= 2048 else 128)

    def body(w_ref, g_ref, m_ref, v_ref, go_ref, d_ref, nm_ref, nv_ref):
        g = g_ref[...]
        go_ref[...] = g
        d_ref[...], nm_ref[...], nv_ref[...] = _adam_math(w_ref[...], g, m_ref[...], v_ref[...])

    spec = pl.BlockSpec((tr, cols), lambda i: (i, 0))
    return pl.pallas_call(
        body, name="adamw", grid=(rows // tr,), in_specs=[spec] * 4, out_specs=(spec,) * 4,
        out_shape=(SDS(w.shape, F32),) * 4, compiler_params=_cp(("parallel",), 40))(w, g, m, v)


def _adamw_small(w, g, m, v):
    def body(w_ref, g_ref, m_ref, v_ref, go_ref, d_ref, nm_ref, nv_ref):
        w = w_ref[...]
        g = g_ref[...]
        sub = lax.broadcasted_iota(jnp.int32, w.shape, 0)
        lane = lax.broadcasted_iota(jnp.int32, w.shape, 1)
        is_ret = jnp.logical_and(sub == 5, lane < 2 * RET_HEADS)
        u = jnp.exp(jnp.where(is_ret, w, -1.0) * jnp.log(2.0))
        g = jnp.where(is_ret, g * (-u * jnp.log(2.0) / (1.0 - u)), g)
        go_ref[...] = g
        d_ref[...], nm_ref[...], nv_ref[...] = _adam_math(w, g, m_ref[...], v_ref[...])

    return pl.pallas_call(body, name="adamw_small", out_shape=(SDS(w.shape, F32),) * 4)(w, g, m, v)


def _rope_tables(seq, n_samp, n_ctx_rows):
    rows = seq // GRID_W
    row = jnp.repeat(jnp.arange(rows, dtype=F32), GRID_W)
    col = jnp.tile(jnp.arange(GRID_W, dtype=F32), rows)
    half = ATT_HEAD_DIM // 2
    freqs = ROPE_THETA ** (-jnp.arange(0, half, 2, dtype=F32) / half)
    ang = jnp.concatenate([row[:, None] * freqs, col[:, None] * freqs], axis=-1)
    cos, sin = jnp.cos(ang), jnp.sin(ang)
    cos_f = jnp.repeat(cos, 2, axis=1)
    sin_s = jnp.stack([-sin, sin], axis=-1).reshape(seq, ATT_HEAD_DIM)
    cos_all = jnp.concatenate([jnp.tile(cos_f, (n_samp, 1)), jnp.ones((n_ctx_rows, ATT_HEAD_DIM), F32)], axis=0)
    sin_all = jnp.concatenate([jnp.tile(sin_s, (n_samp, 1)), jnp.zeros((n_ctx_rows, ATT_HEAD_DIM), F32)], axis=0)
    return cos_all, sin_all


def _pack_small(c_ctx, norm_w, b_ada, ret, qn, kn):
    d = D_MODEL
    row5 = jnp.concatenate([ret.reshape(-1), jnp.zeros((128 - 2 * RET_HEADS,), F32), qn.reshape(-1), kn.reshape(-1),
                            jnp.zeros((d - 384,), F32)])
    return jnp.concatenate([c_ctx.reshape(1, d), norm_w.reshape(1, d), b_ada.reshape(3, d), row5.reshape(1, d),
                            jnp.zeros((2, d), F32)], axis=0)


def _unpack_small(p):
    d = D_MODEL
    return (p[0], p[1:2], p[2:5].reshape(1, 3 * d), p[5, :2 * RET_HEADS].reshape(1, 2, RET_HEADS),
            p[5:6, 128:256], p[5:6, 256:384])


def _step(x, c, ctx, c_ctx, norm_w, b_ada, ret_log2_decay, q_norm_w, k_norm_w, loss_target, weights, ids):
    n_samp, seq, d = x.shape
    lc = ctx.shape[1]
    t_lat, t_ctx = n_samp * seq, n_samp * lc
    assert seq % TM == 0 and t_ctx == TM and t_lat % lc == 0 and seq % GRID_W == 0
    tps = seq // TM

    x_lat = x.reshape(t_lat, d)
    x_ctx = ctx.reshape(t_ctx, d)
    cvec8 = jnp.concatenate([c, c_ctx.reshape(1, d), jnp.zeros((8 - n_samp - 1, d), F32)], axis=0)
    lg = jnp.log1p(-jnp.exp2(ret_log2_decay.reshape(2, RET_HEADS)))
    cos_all, sin_all = _rope_tables(seq, n_samp, t_ctx)

    w_ada_b, w_in_b, w_or_b, w_oa_b, w_out_b = weights
    (w_ada_g,) = _run_comm(_ag_comm((w_ada_b,)))
    mod8 = _adaln_fwd(cvec8, w_ada_g, b_ada)
    mod3 = mod8[:n_samp + 1]
    shift3 = mod3[:, None, 0:d]
    scale3 = mod3[:, None, d:2 * d]
    gate3 = mod3[:, None, 2 * d:3 * d]

    hx, hxt = _norm_fwd(x_lat, x_ctx, norm_w, scale3, shift3, tps, n_samp)
    px, w_in_g = _in_proj_gather(hx, w_in_b, ids)

    states0 = _ctx_state_fwd(px, lg, n_samp, t_lat, lc)
    (o_f, o_b, saved), w_o = _ret_fwd(px, states0, lg, n_samp, seq,
                                      comm=_ag_comm((w_or_b, w_oa_b, w_out_b)))
    w_o_ret, w_o_att, w_out = (w.reshape(-1, d) for w in w_o)

    qn = _att_prep_q(px, cos_all, sin_all, q_norm_w, t_lat)
    kn, vn = _att_prep_kv(px, cos_all, sin_all, k_norm_w)
    o_att, lse = _att_fwd(qn, kn, vn, n_samp, seq, lc)

    (gx_res, do, do_att, dpx, loss8, dgate, g_w_o_ret, g_w_o_att, g_w_out) = _merge(
        x_lat, loss_target.reshape(t_lat, d), o_f, o_b, o_att, px, gate3, w_o_ret, w_o_att, w_out, tps)

    g_a = [g.reshape(N_SHARD, -1, d) for g in (g_w_o_ret, g_w_o_att, g_w_out)]
    (dpx, dkl, dkc, dvl, dvc, gqw), sib_a = _att_bwd(
        dpx, qn, kn, vn, px, o_att, lse, do_att, cos_all, sin_all, q_norm_w, n_samp, seq, lc, comm=_swap_comm(g_a))
    dpx, gkw = _att_kv_bwd(dpx, dkl, dkc, dvl, dvc, px, cos_all, sin_all, k_norm_w)
    t_a = [_chip_sum(g, p, ids) for g, p in zip(g_a, sib_a)]

    (dpx, dstates, dlg_lat), q_a = _ret_bwd(dpx, px, do, saved, lg, n_samp, seq,
                                            comm=_exchange_comm([t16 for _, t16 in t_a]))
    r_a = [_shard_sum(t, q, ids) for (t, _), q in zip(t_a, q_a)]
    dpx, dlg_ctx = _ctx_state_bwd(dpx, px, dstates, lg, n_samp, t_lat, lc)
    dpx = _zero_ctx_tail(dpx, t_lat)

    g_b = _gw_in(hxt, dpx)
    dhx, (sib_b, *r_a) = _dhx(dpx, w_in_g, 0, 1, None, _join_comms(_swap_comm([g_b]), _join_comm(r_a)))
    t_b, t16_b = _chip_sum(g_b, sib_b, ids)
    dhx, q_b = _dhx(dpx, w_in_g, 1, 2, dhx, _exchange_comm([t16_b], (0, 1)))
    dhx, (q_b,) = _dhx(dpx, w_in_g, 3, 1, dhx, _exchange_comm([t16_b], (2,), into=q_b))
    r_b_half = _shard_sum(t_b, q_b, ids)
    grad_x, dshift, dscale, g_norm_w = _norm_bwd(x_lat, x_ctx, dhx, gx_res, norm_w, scale3, tps, n_samp)

    dgate_all = jnp.concatenate([dgate, jnp.zeros((1, 1, d), F32)], axis=0)
    dmod3 = jnp.concatenate([dshift, dscale, dgate_all], axis=2).reshape(n_samp + 1, 3 * d)
    dmod8 = jnp.concatenate([dmod3, jnp.zeros((8 - n_samp - 1, 3 * d), F32)], axis=0)
    g_lg = (jnp.sum(dlg_lat[:, :, 0], axis=0).reshape(2, RET_HEADS)
            + jnp.stack([jnp.sum(dlg_ctx[:, :, 0, 0], axis=0), jnp.sum(dlg_ctx[:, :, 1, 0], axis=0)], axis=0))
    g_qw = jnp.sum(gqw, axis=(0, 1, 2))
    zero = jnp.zeros((d,), F32)

    local = _pack_small(zero, g_norm_w, jnp.zeros((3 * d,), F32), g_lg, g_qw, gkw).at[6, 0].set(loss8[0, 0])
    small_sum, gathered = _gather_small(jnp.concatenate([local, cvec8, dmod8], axis=1), d)
    (g_w_ada, g_b_ada, dc_all), (r_b,) = _adaln_bwd(gathered[:, :d], gathered[:, d:], w_ada_g,
                                                     comm=_join_comm([r_b_half]))
    dc_ctx = jnp.sum(dc_all.reshape(-1, 8, d)[:, n_samp], axis=0)
    small = small_sum + _pack_small(dc_ctx, zero, g_b_ada, jnp.zeros((2, RET_HEADS), F32), zero[:128], zero[:128])
    r_c = lax.dynamic_index_in_dim(g_w_ada, ids[3], 0, keepdims=False)
    return small[6, 0], grad_x.reshape(n_samp, seq, d), (r_c, r_b, *r_a), small


def kernel(x, c, ctx, c_ctx, norm_w, w_ada, b_ada, w_in, ret_log2_decay, q_norm_w, k_norm_w, w_o_ret, w_o_att, w_out, loss_target, m_c_ctx, m_norm_w, m_w_ada, m_b_ada, m_w_in, m_ret_log2_decay, m_q_norm_w, m_k_norm_w, m_w_o_ret, m_w_o_att, m_w_out, v_c_ctx, v_norm_w, v_w_ada, v_b_ada, v_w_in, v_ret_log2_decay, v_q_norm_w, v_k_norm_w, v_w_o_ret, v_w_o_att, v_w_out):
    big_w = (w_ada[0], w_in[0], w_o_ret[0], w_o_att[0], w_out[0])
    big_m = (m_w_ada[0], m_w_in[0], m_w_o_ret[0], m_w_o_att[0], m_w_out[0])
    big_v = (v_w_ada[0], v_w_in[0], v_w_o_ret[0], v_w_o_att[0], v_w_out[0])

    ids = _place_ids()
    loss, grad_x, big_grad, small_grad_in = _step(
        x, c, ctx, c_ctx, norm_w[0:1], b_ada[0:1], ret_log2_decay[0], q_norm_w[0:1], k_norm_w[0:1], loss_target,
        tuple(_cast_place(w, ids) for w in big_w), ids)
    small_w = _pack_small(c_ctx, norm_w, b_ada, ret_log2_decay, q_norm_w, k_norm_w)
    small_m = _pack_small(m_c_ctx, m_norm_w, m_b_ada, m_ret_log2_decay, m_q_norm_w, m_k_norm_w)
    small_v = _pack_small(v_c_ctx, v_norm_w, v_b_ada, v_ret_log2_decay, v_q_norm_w, v_k_norm_w)
    small_grad, small_delta, small_nm, small_nv = _adamw_small(small_w, small_grad_in, small_m, small_v)

    big_g, big_delta, big_nm, big_nv = [], [], [], []
    for w, g, m, v in zip(big_w, big_grad, big_m, big_v):
        go, dlt, nm, nv = _adamw(w, g, m, v)
        big_g.append(go[None])
        big_delta.append(dlt[None])
        big_nm.append(nm[None])
        big_nv.append(nv[None])
    big_grad = big_g

    def order(small_packed, big):
        s = _unpack_small(small_packed)
        return (s[0], s[1], big[0], s[2], big[1], s[3], s[4], s[5], big[2], big[3], big[4])

    return (loss, grad_x, *order(small_grad, big_grad), *order(small_delta, big_delta),
            *order(small_nm, big_nm), *order(small_nv, big_nv))
```

```python
import functools
from typing import NamedTuple

import jax
import jax.numpy as jnp
from jax import lax
from jax.experimental import pallas as pl
from jax.experimental.pallas import tpu as pltpu

F32 = jnp.float32
BF = jnp.bfloat16
SDS = jax.ShapeDtypeStruct
MESH = pl.DeviceIdType.MESH
ANY = pl.BlockSpec(memory_space=pl.ANY)
SMEM = pl.BlockSpec(memory_space=pltpu.SMEM)

D_MODEL = 1024
GRID_W = 64
RET_HEADS = 4
RET_DK = 256
RET_DV = 512
RET_CHUNK = 128
ATT_HEADS = 8
ATT_KV_HEADS = 2
ATT_REP = ATT_HEADS // ATT_KV_HEADS
ATT_HEAD_DIM = 128
ROPE_THETA = 10000.0
NORM_EPS = 1e-6
IN_COLS = 10752
KV_COLS = 3584
C_RK, C_RV, C_AK, C_AV, C_RQ, C_RG, C_AQ, C_AG, C_MR, C_MA = 0, 1024, 3072, 3328, 3584, 4608, 6656, 7680, 8704, 9728
N_SHARD = 4
ADA_W = 3 * D_MODEL // N_SHARD
IN_W = IN_COLS // N_SHARD
IN_BLK = IN_W
BPS = IN_W // IN_BLK
N_IN_BLK = IN_COLS // IN_BLK
TM = 512
ATT_TQ = 512
ADAM_LR, ADAM_B1, ADAM_B2, ADAM_EPS, ADAM_WD, ADAM_STEP = 0.001, 0.9, 0.999, 1e-08, 0.01, 10
MIB = 1024 * 1024


def _cp(sem=None, vmem_mb=None, **kw):
    if sem is not None:
        kw["dimension_semantics"] = sem
    if vmem_mb is not None:
        kw["vmem_limit_bytes"] = vmem_mb * MIB
    return pltpu.CompilerParams(**kw)


def _dot(a, b, ca=1, cb=0):
    return lax.dot_general(a.astype(BF), b.astype(BF), (((ca,), (cb,)), ((), ())), preferred_element_type=F32)


def _sigmoid(x):
    return 0.5 * jnp.tanh(0.5 * x) + 0.5


def _sum_all(x):
    return jnp.sum(jnp.sum(x, axis=1, keepdims=True), axis=0, keepdims=True)


def _swap_pairs(x):
    ax = x.ndim - 1
    lane = lax.broadcasted_iota(jnp.int32, x.shape, ax)
    nxt = pltpu.roll(x, x.shape[ax] - 1, ax)
    prv = pltpu.roll(x, 1, ax)
    return jnp.where(lane % 2 == 0, nxt, prv)


def _rms(x):
    return lax.rsqrt(jnp.mean(x * x, axis=-1, keepdims=True) + NORM_EPS)


def _rms_bwd(dxh, xh, r):
    return r * (dxh - xh * jnp.mean(dxh * xh, axis=-1, keepdims=True))


class _Comm(NamedTuple):
    name: str
    ins: tuple
    out_shapes: tuple
    aliases: dict
    n_sems: int
    phases: tuple


def _join_comms(*comms):
    offs, i_off, o_off, s_off = [], 0, 0, 0
    for cm in comms:
        offs.append((i_off, o_off, s_off))
        i_off, o_off, s_off = i_off + len(cm.ins), o_off + len(cm.out_shapes), s_off + cm.n_sems

    def phase(k):
        def run(ins, outs, ssem, rsem, base):
            sends, recvs = [], []
            for cm, (io, oo, so) in zip(comms, offs):
                if k < len(cm.phases):
                    s, r = cm.phases[k](ins[io:io + len(cm.ins)], outs[oo:oo + len(cm.out_shapes)], ssem, rsem,
                                        base + so)
                    sends += s
                    recvs += r
            return sends, recvs
        return run

    aliases = {}
    for cm, (io, oo, _) in zip(comms, offs):
        aliases.update({io + a: oo + b for a, b in cm.aliases.items()})
    return _Comm("+".join(cm.name for cm in comms), sum((cm.ins for cm in comms), ()),
                 sum((cm.out_shapes for cm in comms), ()), aliases, s_off,
                 tuple(phase(k) for k in range(max(len(cm.phases) for cm in comms))))


def _run_phases(comm, cins, couts, ssem, rsem, first_started):
    for k, phase in enumerate(comm.phases):
        sends, recvs = phase(cins, couts, ssem, rsem, 0)
        if k > 0 or not first_started:
            for cp in sends:
                cp.start()
        for cp in recvs:
            cp.wait_recv()
        for cp in sends:
            cp.wait_send()


def _call(body, args, comm, *, name, grid, in_specs, out_specs, out_shape, scratch_shapes=(),
          compiler_params, aliases=None):
    n_in, n_out, n_sc = len(in_specs), len(out_specs), len(scratch_shapes)
    n_ci, n_co = len(comm.ins), len(comm.out_shapes)
    io_alias = dict(aliases or {})
    io_alias.update({n_in + a: n_out + b for a, b in comm.aliases.items()})

    def kernel_body(*refs):
        ins, cins = refs[:n_in], refs[n_in:n_in + n_ci]
        outs = refs[n_in + n_ci:n_in + n_ci + n_out]
        couts = refs[n_in + n_ci + n_out:n_in + n_ci + n_out + n_co]
        scratch = refs[n_in + n_ci + n_out + n_co:n_in + n_ci + n_out + n_co + n_sc]
        ssem, rsem = refs[-2:]
        first = functools.reduce(jnp.logical_and, [pl.program_id(k) == 0 for k in range(len(grid))])
        last = functools.reduce(jnp.logical_and, [pl.program_id(k) == grid[k] - 1 for k in range(len(grid))])

        @pl.when(first)
        def _():
            for cp in comm.phases[0](cins, couts, ssem, rsem, 0)[0]:
                cp.start()

        body(*ins, *outs, *scratch)

        @pl.when(last)
        def _():
            _run_phases(comm, cins, couts, ssem, rsem, True)

    res = pl.pallas_call(
        kernel_body, name=name + "+" + comm.name, grid=grid, in_specs=list(in_specs) + [ANY] * n_ci,
        out_specs=tuple(out_specs) + tuple([ANY] * n_co), out_shape=tuple(out_shape) + tuple(comm.out_shapes),
        scratch_shapes=list(scratch_shapes) + [pltpu.SemaphoreType.DMA((comm.n_sems,)),
                                               pltpu.SemaphoreType.DMA((comm.n_sems,))],
        input_output_aliases=io_alias, compiler_params=compiler_params)(*args, *comm.ins)
    return tuple(res[:n_out]), tuple(res[n_out:])


def _run_comm(comm):
    n_ci, n_co = len(comm.ins), len(comm.out_shapes)

    def body(*refs):
        _run_phases(comm, refs[:n_ci], refs[n_ci:n_ci + n_co], refs[-2], refs[-1], False)

    return pl.pallas_call(
        body, name=comm.name, in_specs=[ANY] * n_ci, out_specs=tuple([ANY] * n_co), out_shape=tuple(comm.out_shapes),
        input_output_aliases=dict(comm.aliases),
        scratch_shapes=[pltpu.SemaphoreType.DMA((comm.n_sems,)), pltpu.SemaphoreType.DMA((comm.n_sems,))],
        compiler_params=_cp(has_side_effects=True))(*comm.ins)


def _adaln_fwd(cvec8, w_ada_g, b_ada):
    def body(c_ref, w_ref, b_ref, o_ref):
        cv = c_ref[...]
        sc = (cv * _sigmoid(cv)).astype(BF)
        for s in range(N_SHARD):
            cols = slice(s * ADA_W, (s + 1) * ADA_W)
            o_ref[:, cols] = jnp.dot(sc, w_ref[s], preferred_element_type=F32) + b_ref[:, cols]

    return pl.pallas_call(body, out_shape=SDS((8, 3 * D_MODEL), F32), name="adaln_fwd",
                          compiler_params=_cp(vmem_mb=32))(cvec8, w_ada_g, b_ada)


def _adaln_bwd(cvec, dmod, w_ada_g, comm):
    n_rows = cvec.shape[0]
    def body(c_ref, d_ref, w_ref, gw_ref, gb_ref, dc_ref):
        cv = c_ref[...]
        sg = _sigmoid(cv)
        sc = cv * sg
        dm = d_ref[...]
        gb_ref[...] = jnp.sum(dm, axis=0, keepdims=True)
        dsc = jnp.zeros(cv.shape, F32)
        for s in range(N_SHARD):
            cols = slice(s * ADA_W, (s + 1) * ADA_W)
            gw_ref[s] = _dot(sc, dm[:, cols], 0, 0)
            dsc = dsc + _dot(dm[:, cols], w_ref[s], 1, 1)
        dc_ref[...] = dsc * (sg * (1.0 + cv * (1.0 - sg)))

    def whole(shape):
        return pl.BlockSpec(shape, lambda i: (0,) * len(shape))

    shapes = ((N_SHARD, D_MODEL, ADA_W), (1, 3 * D_MODEL), (n_rows, D_MODEL))
    return _call(body, [cvec, dmod, w_ada_g], comm, name="adaln_bwd", grid=(1,),
                 in_specs=[whole(cvec.shape), whole(dmod.shape), whole(w_ada_g.shape)],
                 out_specs=tuple(whole(s) for s in shapes), out_shape=tuple(SDS(s, F32) for s in shapes),
                 compiler_params=_cp(("arbitrary",), 56))


def _big_rows(rows):
    return 1536 if rows % 1536 == 0 else TM


def _norm_fwd(x_lat, x_ctx, norm_w, scale3, shift3, tiles_per_sample, n_samp):
    n_lat = x_lat.shape[0] // TM
    rows = x_lat.shape[0] + x_ctx.shape[0]

    def samp(i):
        return jnp.minimum(i // tiles_per_sample, n_samp)

    def body(x_ref, c_ref, nw_ref, sc_ref, sh_ref, hx_ref, hxt_ref):
        x = jnp.where(pl.program_id(0) < n_lat, x_ref[...], c_ref[...])
        h = x * _rms(x) * nw_ref[...] * (1.0 + sc_ref[...]) + sh_ref[...]
        hx_ref[...] = h.astype(BF)
        hxt_ref[...] = h.T.astype(BF)

    return pl.pallas_call(
        body, name="norm_fwd", grid=(rows // TM,),
        in_specs=[pl.BlockSpec((TM, D_MODEL), lambda i: (jnp.minimum(i, n_lat - 1), 0)),
                  pl.BlockSpec((TM, D_MODEL), lambda i: (jnp.maximum(i - n_lat, 0), 0)),
                  pl.BlockSpec((1, D_MODEL), lambda i: (0, 0)),
                  pl.BlockSpec((None, 1, D_MODEL), lambda i: (samp(i), 0, 0)),
                  pl.BlockSpec((None, 1, D_MODEL), lambda i: (samp(i), 0, 0))],
        out_specs=(pl.BlockSpec((TM, D_MODEL), lambda i: (i, 0)),
                   pl.BlockSpec((D_MODEL, TM), lambda i: (0, i))),
        out_shape=(SDS((rows, D_MODEL), BF), SDS((D_MODEL, rows), BF)),
        compiler_params=_cp(("parallel",), 40))(x_lat, x_ctx, norm_w, scale3, shift3)


def _norm_bwd(x_lat, x_ctx, dhx, gx_res, norm_w, scale3, tiles_per_sample, n_samp):
    rows = x_lat.shape[0] + x_ctx.shape[0]
    n_lat = tiles_per_sample * n_samp

    def samp(i):
        return jnp.minimum(i // tiles_per_sample, n_samp)

    def lat(i):
        return jnp.minimum(i, n_lat - 1)

    def body(x_ref, c_ref, dh_ref, gr_ref, nw_ref, sc_ref, gx_ref, dsh_ref, dsc_ref, dnw_ref):
        i = pl.program_id(0)
        x = jnp.where(i < n_lat, x_ref[...], c_ref[...])
        r = _rms(x)
        xh = x * r
        nw = nw_ref[...]
        dh = dh_ref[...]
        first = jnp.logical_or(i % tiles_per_sample == 0, i >= n_lat)

        @pl.when(first)
        def _():
            dsh_ref[...] = jnp.zeros_like(dsh_ref)
            dsc_ref[...] = jnp.zeros_like(dsc_ref)

        @pl.when(i == 0)
        def _():
            dnw_ref[...] = jnp.zeros_like(dnw_ref)

        dsh_ref[...] += jnp.sum(dh, axis=0, keepdims=True)
        dsc_ref[...] += jnp.sum(dh * (xh * nw), axis=0, keepdims=True)
        du = dh * (1.0 + sc_ref[...])
        dnw_ref[...] += jnp.sum(du * xh, axis=0, keepdims=True)

        @pl.when(i < n_lat)
        def _():
            gx_ref[...] = gr_ref[...] + _rms_bwd(du * nw, xh, r)

    return pl.pallas_call(
        body, name="norm_bwd", grid=(rows // TM,),
        in_specs=[pl.BlockSpec((TM, D_MODEL), lambda i: (lat(i), 0)),
                  pl.BlockSpec((TM, D_MODEL), lambda i: (jnp.maximum(i - n_lat, 0), 0)),
                  pl.BlockSpec((TM, D_MODEL), lambda i: (i, 0)),
                  pl.BlockSpec((TM, D_MODEL), lambda i: (lat(i), 0)),
                  pl.BlockSpec((1, D_MODEL), lambda i: (0, 0)),
                  pl.BlockSpec((None, 1, D_MODEL), lambda i: (samp(i), 0, 0))],
        out_specs=(pl.BlockSpec((TM, D_MODEL), lambda i: (lat(i), 0)),
                   pl.BlockSpec((None, 1, D_MODEL), lambda i: (samp(i), 0, 0)),
                   pl.BlockSpec((None, 1, D_MODEL), lambda i: (samp(i), 0, 0)),
                   pl.BlockSpec((1, D_MODEL), lambda i: (0, 0))),
        out_shape=(SDS((n_lat * TM, D_MODEL), F32), SDS((n_samp + 1, 1, D_MODEL), F32),
                   SDS((n_samp + 1, 1, D_MODEL), F32), SDS((1, D_MODEL), F32)),
        compiler_params=_cp(("arbitrary",), 40))(x_lat, x_ctx, dhx, gx_res, norm_w, scale3)


def _in_proj_gather(hx, w_buf, ids):
    rows = hx.shape[0]
    tb = _big_rows(rows)
    n_i = rows // tb
    hrows = D_MODEL // 2

    def body(ids_ref, h_ref, w_in_hbm, px_ref, w_hbm, wv, lsem, ssem, rsem):
        j, i = pl.program_id(0), pl.program_id(1)
        x, y, c, chips = _place()
        sibling = (x, y, 1 - c)

        def half(s, which):
            return w_hbm.at[s, pl.ds(which * hrows, hrows), :]

        def over_ici(rel):
            chip = chips[rel]
            mine, theirs = half(2 * x + y, c), half(2 * chip[0] + chip[1], c)
            return (_remote(mine, mine, ssem.at[rel], rsem.at[rel], (*chip, c)),
                    _remote(theirs, theirs, ssem.at[rel], rsem.at[rel], (*chip, c)))

        def over_d2d(rel):
            s = 2 * chips[rel][0] + chips[rel][1]
            return (_remote(half(s, c), half(s, c), ssem.at[3 + rel], rsem.at[3 + rel], sibling),
                    _remote(half(s, 1 - c), half(s, 1 - c), ssem.at[3 + rel], rsem.at[3 + rel], sibling))

        first_row_tile = i == 0

        @pl.when(jnp.logical_and(j == 0, first_row_tile))
        def _():
            over_ici(0)[0].start()
            over_ici(1)[0].start()

        @pl.when(jnp.logical_and(j == 1, first_row_tile))
        def _():
            for rel in range(2):
                over_ici(rel)[1].wait_recv()
                over_d2d(rel)[0].start()
            over_ici(2)[0].start()
            over_d2d(0)[1].wait_recv()

        @pl.when(jnp.logical_and(j == 2, first_row_tile))
        def _():
            over_d2d(1)[1].wait_recv()

        @pl.when(jnp.logical_and(j == 3, first_row_tile))
        def _():
            over_ici(2)[1].wait_recv()
            passed, landing = over_d2d(2)
            passed.start()
            landing.wait_recv()

        @pl.when(first_row_tile)
        def _():
            cp = pltpu.make_async_copy(w_hbm.at[ids_ref[4 + j]], wv, lsem)
            cp.start()
            cp.wait()

        px_ref[...] = jnp.dot(h_ref[...], wv[...], preferred_element_type=F32).astype(BF)

        @pl.when(jnp.logical_and(j == N_SHARD - 1, i == n_i - 1))
        def _():
            for rel in range(3):
                over_ici(rel)[0].wait_send()
                over_d2d(rel)[0].wait_send()

    return pl.pallas_call(
        body, name="in_proj_gather", input_output_aliases={2: 1},
        grid_spec=pltpu.PrefetchScalarGridSpec(
            num_scalar_prefetch=1, grid=(N_SHARD, n_i),
            in_specs=[pl.BlockSpec((tb, D_MODEL), lambda j, i, ids_ref: (i, 0)), ANY],
            out_specs=(pl.BlockSpec((tb, IN_W), lambda j, i, ids_ref: (i, ids_ref[4 + j])), ANY),
            scratch_shapes=[pltpu.VMEM((D_MODEL, IN_W), BF), pltpu.SemaphoreType.DMA,
                            pltpu.SemaphoreType.DMA((6,)), pltpu.SemaphoreType.DMA((6,))]),
        out_shape=(SDS((rows, IN_COLS), BF), SDS(w_buf.shape, w_buf.dtype)),
        compiler_params=_cp(("arbitrary", "arbitrary"), 56))(ids, hx, w_buf)


def _gw_in(hxt, dpx_all):
    rows = dpx_all.shape[0]
    tb = _big_rows(rows)

    def body(h_ref, d_ref, o_ref):
        @pl.when(pl.program_id(1) == 0)
        def _():
            o_ref[...] = jnp.zeros_like(o_ref)

        o_ref[...] += jnp.dot(h_ref[...], d_ref[...], preferred_element_type=F32)

    return pl.pallas_call(
        body, name="gw_in", grid=(N_IN_BLK, rows // tb),
        in_specs=[pl.BlockSpec((D_MODEL, tb), lambda j, i: (0, i)),
                  pl.BlockSpec((tb, IN_BLK), lambda j, i: (i, j))],
        out_specs=pl.BlockSpec((None, D_MODEL, IN_BLK), lambda j, i: (j // BPS, 0, j % BPS)),
        out_shape=SDS((N_SHARD, D_MODEL, IN_W), F32),
        compiler_params=_cp(("arbitrary", "arbitrary"), 56))(hxt, dpx_all)


def _dhx(dpx_all, w_in_g, tile0, n_tiles, dhx, comm):
    rows = dpx_all.shape[0]
    tb = _big_rows(rows)

    def body(d_ref, w_ref, *rest):
        o_ref = rest[-1]

        @pl.when(pl.program_id(1) == 0)
        def _():
            o_ref[...] = jnp.zeros_like(o_ref)

        o_ref[...] += lax.dot_general(d_ref[...], w_ref[...], (((1,), (1,)), ((), ())), preferred_element_type=F32)

    args, in_specs, aliases = [dpx_all, w_in_g], [
        pl.BlockSpec((tb, IN_BLK), lambda i, j: (tile0 + i, j)),
        pl.BlockSpec((None, D_MODEL, IN_BLK), lambda i, j: (j // BPS, 0, j % BPS))], None
    if dhx is not None:
        args, in_specs, aliases = args + [dhx], in_specs + [ANY], {2: 0}
    (out,), got = _call(body, args, comm, name="dhx", grid=(n_tiles, N_IN_BLK), in_specs=in_specs,
                        out_specs=(pl.BlockSpec((tb, D_MODEL), lambda i, j: (tile0 + i, 0)),),
                        out_shape=(SDS((rows, D_MODEL), F32),), aliases=aliases,
                        compiler_params=_cp(("arbitrary", "arbitrary"), 56))
    return out, got


def _decays(lgv, d):
    c = RET_CHUNK
    ii = lax.broadcasted_iota(jnp.int32, (c, 1), 0).astype(F32)
    jj = lax.broadcasted_iota(jnp.int32, (1, c), 1).astype(F32)
    a_i = jnp.where(d == 0, ii, c - 1.0 - ii)
    a_j = jnp.where(d == 0, jj, c - 1.0 - jj)
    rel = a_i - a_j
    mask = jnp.where(rel >= 0, jnp.exp(lgv * jnp.maximum(rel, 0.0)), 0.0)
    qd = jnp.exp(lgv * (a_i + 1.0))
    kd = jnp.exp(lgv * (c - 1.0 - a_i))
    gc = jnp.exp(jnp.full((1, 1), lgv * c, F32))
    return a_i, rel, mask, qd, kd, gc


def _ctx_state_fwd(px, lg, n_samp, t_lat, lc):
    rb = t_lat // lc

    def body(lg_ref, k_ref, v_ref, o_ref):
        h = pl.program_id(1)
        k = k_ref[...].astype(F32) * (RET_DK ** -0.5)
        v = v_ref[...]
        pos = lax.broadcasted_iota(jnp.int32, (lc, 1), 0).astype(F32)
        o_ref[0] = _dot(k * jnp.exp(lg_ref[0, h] * (lc - 1.0 - pos)), v, 0, 0)
        o_ref[1] = _dot(k * jnp.exp(lg_ref[1, h] * pos), v, 0, 0)

    return pl.pallas_call(
        body, name="ctx_state_fwd", grid=(n_samp, RET_HEADS),
        in_specs=[SMEM,
                  pl.BlockSpec((lc, RET_DK), lambda b, h: (rb + b, C_RK // RET_DK + h)),
                  pl.BlockSpec((lc, RET_DV), lambda b, h: (rb + b, C_RV // RET_DV + h))],
        out_specs=pl.BlockSpec((None, 2, None, RET_DK, RET_DV), lambda b, h: (b, 0, h, 0, 0)),
        out_shape=SDS((n_samp, 2, RET_HEADS, RET_DK, RET_DV), F32),
        compiler_params=_cp(("parallel", "parallel")))(lg, px, px)


def _ctx_state_bwd(dpx, px, dstates, lg, n_samp, t_lat, lc):
    rb = t_lat // lc
    kspec = pl.BlockSpec((lc, RET_DK), lambda b, h: (rb + b, C_RK // RET_DK + h))
    vspec = pl.BlockSpec((lc, RET_DV), lambda b, h: (rb + b, C_RV // RET_DV + h))
    sspec = pl.BlockSpec((None, 2, None, RET_DK, RET_DV), lambda b, h: (b, 0, h, 0, 0))

    def weights(lg_ref, h):
        pos = lax.broadcasted_iota(jnp.int32, (lc, 1), 0).astype(F32)
        e_f = lc - 1.0 - pos
        return pos, e_f, jnp.exp(lg_ref[0, h] * e_f), jnp.exp(lg_ref[1, h] * pos)

    def k_body(lg_ref, dpx_hbm, k_ref, v_ref, ds_ref, dk_ref, dlg_ref):
        pos, e_f, w_f, w_b = weights(lg_ref, pl.program_id(1))
        k = k_ref[...].astype(F32) * (RET_DK ** -0.5)
        y_f = _dot(v_ref[...], ds_ref[0], 1, 1) * w_f
        y_b = _dot(v_ref[...], ds_ref[1], 1, 1) * w_b
        dk_ref[...] = ((y_f + y_b) * (RET_DK ** -0.5)).astype(BF)
        t_f = _sum_all(e_f * k * y_f)
        t_b = _sum_all(pos * k * y_b)
        sub = lax.broadcasted_iota(jnp.int32, (8, 128), 0)
        dlg_ref[...] = jnp.where(sub == 0, t_f, jnp.where(sub == 1, t_b, 0.0))

    def v_body(lg_ref, dpx_hbm, k_ref, ds_ref, dv_ref):
        _, _, w_f, w_b = weights(lg_ref, pl.program_id(1))
        k = k_ref[...].astype(F32) * (RET_DK ** -0.5)
        dv_ref[...] = (_dot(k * w_f, ds_ref[0]) + _dot(k * w_b, ds_ref[1])).astype(BF)

    dpx, dlg = pl.pallas_call(
        k_body, name="ctx_state_bwd_k", grid=(n_samp, RET_HEADS), input_output_aliases={1: 0},
        in_specs=[SMEM, ANY, kspec, vspec, sspec],
        out_specs=(kspec, pl.BlockSpec((None, None, 8, 128), lambda b, h: (b, h, 0, 0))),
        out_shape=(SDS(dpx.shape, dpx.dtype), SDS((n_samp, RET_HEADS, 8, 128), F32)),
        compiler_params=_cp(("parallel", "parallel")))(lg, dpx, px, px, dstates)
    dpx = pl.pallas_call(
        v_body, name="ctx_state_bwd_v", grid=(n_samp, RET_HEADS), input_output_aliases={1: 0},
        in_specs=[SMEM, ANY, kspec, sspec], out_specs=vspec, out_shape=SDS(dpx.shape, dpx.dtype),
        compiler_params=_cp(("parallel", "parallel")))(lg, dpx, px, dstates)
    return dpx, dlg


def _zero_ctx_tail(dpx, t_lat):
    wb = 512
    n_ctx = (dpx.shape[0] - t_lat) // TM

    def body(dpx_hbm, o_ref):
        o_ref[...] = jnp.zeros_like(o_ref)

    return pl.pallas_call(
        body, name="zero_ctx_tail", grid=(n_ctx, (IN_COLS - KV_COLS) // wb), input_output_aliases={0: 0},
        in_specs=[ANY], out_specs=pl.BlockSpec((TM, wb), lambda i, j: (t_lat // TM + i, KV_COLS // wb + j)),
        out_shape=SDS(dpx.shape, dpx.dtype),
        compiler_params=_cp(("parallel", "parallel")))(dpx)


def _ret_specs(row_f, row_b):
    c = RET_CHUNK
    wq = RET_HEADS * RET_DK // 2
    wv = RET_HEADS * RET_DV // 2
    specs = []
    for row in (row_f, row_b):
        specs += [pl.BlockSpec((c, wq), lambda b, n, row=row: (row(b, n), C_RQ // wq)),
                  pl.BlockSpec((c, wq), lambda b, n, row=row: (row(b, n), C_RQ // wq + 1)),
                  pl.BlockSpec((c, 2 * wq), lambda b, n, row=row: (row(b, n), C_RK // (2 * wq))),
                  pl.BlockSpec((c, wv), lambda b, n, row=row: (row(b, n), C_RV // wv)),
                  pl.BlockSpec((c, wv), lambda b, n, row=row: (row(b, n), C_RV // wv + 1))]
    return specs


def _ret_head(refs, h):
    q0, q1, k_ref, v0, v1 = refs
    hh = h % 2
    q = (q0, q1)[h // 2][:, hh * RET_DK:(hh + 1) * RET_DK].astype(F32)
    k = k_ref[:, h * RET_DK:(h + 1) * RET_DK].astype(F32) * (RET_DK ** -0.5)
    v = (v0, v1)[h // 2][:, hh * RET_DV:(hh + 1) * RET_DV]
    return q, k, v


def _ret_fwd(px, states0, lg, n_samp, seq, comm):
    c = RET_CHUNK
    nc = seq // c
    t_lat = n_samp * seq
    wo = RET_HEADS * RET_DV

    def row_f(b, n):
        return b * nc + n

    def row_b(b, n):
        return b * nc + nc - 1 - n

    def body(lg_ref, *refs):
        ins, (s0_ref, of_ref, ob_ref, st_ref, s_s) = refs[:10], refs[10:]

        @pl.when(pl.program_id(1) == 0)
        def _():
            s_s[...] = s0_ref[...]

        for d, o_ref in ((0, of_ref), (1, ob_ref)):
            for h in range(RET_HEADS):
                _, _, mask, qd, kd, gc = _decays(lg_ref[d, h], d)
                q, k, v = _ret_head(ins[5 * d:5 * d + 5], h)
                s = s_s[d, h]
                st_ref[h, d] = s.astype(BF)
                sc = _dot(q, k, 1, 1) * mask
                o_ref[:, h * RET_DV:(h + 1) * RET_DV] = (_dot(sc, v) + _dot(q * qd, s)).astype(BF)
                s_s[d, h] = s * gc + _dot(k * kd, v, 0, 0)

    return _call(
        body, [lg] + [px] * 10 + [states0], comm, name="ret_fwd", grid=(n_samp, nc),
        in_specs=[SMEM] + _ret_specs(row_f, row_b) + [
            pl.BlockSpec((None, 2, RET_HEADS, RET_DK, RET_DV), lambda b, n: (b, 0, 0, 0, 0))],
        out_specs=(pl.BlockSpec((c, wo), lambda b, n: (row_f(b, n), 0)),
                   pl.BlockSpec((c, wo), lambda b, n: (row_b(b, n), 0)),
                   pl.BlockSpec((None, RET_HEADS, 2, None, RET_DK, RET_DV), lambda b, n: (b, 0, 0, n, 0, 0))),
        out_shape=(SDS((t_lat, wo), BF), SDS((t_lat, wo), BF),
                   SDS((n_samp, RET_HEADS, 2, nc, RET_DK, RET_DV), BF)),
        scratch_shapes=[pltpu.VMEM((2, RET_HEADS, RET_DK, RET_DV), F32)],
        compiler_params=_cp(("arbitrary", "arbitrary"), 48))


def _ret_bwd(dpx, px, do, saved, lg, n_samp, seq, comm):
    c = RET_CHUNK
    nc = seq // c
    assert nc % 2 == 0
    wq, wo = RET_HEADS * RET_DK, RET_HEADS * RET_DV

    def row_f(b, n):
        return b * nc + nc - 1 - n

    def row_b(b, n):
        return b * nc + n

    def body(lg_ref, *refs):
        ins = refs[:10]
        (dof_ref, dob_ref, st_ref, dpx_in, dpx_hbm, ds0_ref, dlg_ref,
         ds_s, acc_s, sq_s, sk_s, sv_s, sems) = refs[10:]
        b, n = pl.program_id(0), pl.program_id(1)
        second = n >= nc // 2
        chunks = (nc - 1 - n, n)

        def parked(ch):
            return pl.ds(pl.multiple_of(ch * c, c), c)

        def flush():
            cps = []
            for d, ch in enumerate(chunks):
                rows = pl.ds(pl.multiple_of((b * nc + ch) * c, c), c)
                cps += [pltpu.make_async_copy(sq_s.at[parked(ch), :], dpx_hbm.at[rows, pl.ds(C_RQ, wq)], sems.at[3 * d]),
                        pltpu.make_async_copy(sk_s.at[parked(ch), :], dpx_hbm.at[rows, pl.ds(C_RK, wq)],
                                              sems.at[3 * d + 1]),
                        pltpu.make_async_copy(sv_s.at[parked(ch), :], dpx_hbm.at[rows, pl.ds(C_RV, wo)],
                                              sems.at[3 * d + 2])]
            return cps

        @pl.when(jnp.logical_or(n > nc // 2, jnp.logical_and(n == 0, b > 0)))
        def _():
            for cp in flush():
                cp.wait()

        @pl.when(n == 0)
        def _():
            ds_s[...] = jnp.zeros_like(ds_s)
            acc_s[...] = jnp.zeros_like(acc_s)

        def chains(first_visit):
            for d, do_ref in enumerate((dof_ref, dob_ref)):
                rows = parked(chunks[d])
                for h in range(RET_HEADS):
                    a_i, rel, mask, qd, kd, gc = _decays(lg_ref[d, h], d)
                    q, k, v = _ret_head(ins[5 * d:5 * d + 5], h)
                    qb, kb, vb = q.astype(BF), k.astype(BF), v.astype(BF)
                    cq, cv = slice(h * RET_DK, (h + 1) * RET_DK), slice(h * RET_DV, (h + 1) * RET_DV)
                    dob = do_ref[:, cv].astype(BF)
                    sb = st_ref[h, d]
                    ds = ds_s[d, h]
                    dsb = ds.astype(BF)
                    raw = _dot(qb, kb, 1, 1)
                    sc = raw * mask
                    dsc = _dot(dob, vb, 1, 1) * mask
                    dscb = dsc.astype(BF)
                    x = _dot(dob, sb, 1, 1)
                    y = _dot(vb, dsb, 1, 1)
                    qq = q * qd
                    kk = k * kd
                    dq = _dot(dscb, kb) + x * qd
                    dk = _dot(dscb, qb, 0, 0) + y * kd
                    dv = _dot(sc, dob, 0, 0) + _dot(kk, dsb)
                    if first_visit:
                        sq_s[rows, cq] = dq.astype(BF)
                        sk_s[rows, cq] = dk.astype(BF)
                        sv_s[rows, cv] = dv.astype(BF)
                    else:
                        sq_s[rows, cq] = (sq_s[rows, cq].astype(F32) + dq).astype(BF)
                        sk_s[rows, cq] = ((sk_s[rows, cq].astype(F32) + dk) * (RET_DK ** -0.5)).astype(BF)
                        sv_s[rows, cv] = (sv_s[rows, cv].astype(F32) + dv).astype(BF)
                    t = (_sum_all(dsc * raw * rel) + _sum_all((a_i + 1.0) * qq * x)
                         + _sum_all((c - 1.0 - a_i) * kk * y) + c * gc * _sum_all(ds * sb.astype(F32)))
                    acc_s[4 * d + h:4 * d + h + 1, :] += t
                    ds_s[d, h] = ds * gc + _dot(qq, dob, 0, 0)

        @pl.when(jnp.logical_not(second))
        def _():
            chains(True)

        @pl.when(second)
        def _():
            chains(False)
            for cp in flush():
                cp.start()

        @pl.when(n == nc - 1)
        def _():
            ds0_ref[...] = ds_s[...]
            dlg_ref[...] = acc_s[...]

        @pl.when(jnp.logical_and(b == n_samp - 1, n == nc - 1))
        def _():
            for cp in flush():
                cp.wait()

    do_spec_f = pl.BlockSpec((c, wo), lambda b, n: (row_f(b, n), 0))
    do_spec_b = pl.BlockSpec((c, wo), lambda b, n: (row_b(b, n), 0))
    return _call(
        body, [lg] + [px] * 10 + [do, do, saved, dpx], comm, name="ret_bwd", grid=(n_samp, nc), aliases={14: 0},
        in_specs=[SMEM] + _ret_specs(row_f, row_b) + [
            do_spec_f, do_spec_b,
            pl.BlockSpec((None, RET_HEADS, 2, None, RET_DK, RET_DV), lambda b, n: (b, 0, 0, nc - 1 - n, 0, 0)),
            ANY],
        out_specs=(ANY,
                   pl.BlockSpec((None, 2, RET_HEADS, RET_DK, RET_DV), lambda b, n: (b, 0, 0, 0, 0)),
                   pl.BlockSpec((None, 8, 128), lambda b, n: (b, 0, 0))),
        out_shape=(SDS(dpx.shape, dpx.dtype),
                   SDS((n_samp, 2, RET_HEADS, RET_DK, RET_DV), F32), SDS((n_samp, 8, 128), F32)),
        scratch_shapes=[pltpu.VMEM((2, RET_HEADS, RET_DK, RET_DV), F32), pltpu.VMEM((8, 128), F32),
                        pltpu.VMEM((seq, wq), BF), pltpu.VMEM((seq, wq), BF), pltpu.VMEM((seq, wo), BF),
                        pltpu.SemaphoreType.DMA((6,))],
        compiler_params=_cp(("arbitrary", "arbitrary"), 60))


def _norm_rope(x, w, cos, sin):
    xn = x * _rms(x) * w
    return xn * cos + _swap_pairs(xn) * sin


def _norm_rope_bwd(dy, x, w, cos, sin):
    dxn = dy * cos + _swap_pairs(dy * sin)
    r = _rms(x)
    xh = x * r
    return _rms_bwd(dxn * w, xh, r), jnp.sum(dxn * xh, axis=0, keepdims=True)


def _att_prep_q(px, cos_all, sin_all, qnw, t_lat):
    hd = ATT_HEAD_DIM
    wblk = ATT_REP * hd

    def body(x_ref, cos_ref, sin_ref, w_ref, o_ref):
        for r in range(ATT_REP):
            cols = slice(r * hd, (r + 1) * hd)
            qr = _norm_rope(x_ref[:, cols].astype(F32), w_ref[...], cos_ref[...], sin_ref[...])
            o_ref[:, cols] = (qr * (hd ** -0.5)).astype(BF)

    return pl.pallas_call(
        body, name="att_prep_q", grid=(t_lat // TM, ATT_KV_HEADS),
        in_specs=[pl.BlockSpec((TM, wblk), lambda i, g: (i, C_AQ // wblk + g)),
                  pl.BlockSpec((TM, hd), lambda i, g: (i, 0)),
                  pl.BlockSpec((TM, hd), lambda i, g: (i, 0)),
                  pl.BlockSpec((1, hd), lambda i, g: (0, 0))],
        out_specs=pl.BlockSpec((TM, wblk), lambda i, g: (i, g)),
        out_shape=SDS((t_lat, ATT_HEADS * hd), BF),
        compiler_params=_cp(("parallel", "parallel")))(px, cos_all, sin_all, qnw)


def _att_prep_kv(px, cos_all, sin_all, knw):
    rows = px.shape[0]
    hd = ATT_HEAD_DIM
    kvw = ATT_KV_HEADS * hd

    def body(x_ref, cos_ref, sin_ref, w_ref, k_ref, v_ref):
        for g in range(ATT_KV_HEADS):
            cols = slice(g * hd, (g + 1) * hd)
            k_ref[:, cols] = _norm_rope(x_ref[:, cols].astype(F32), w_ref[...], cos_ref[...],
                                        sin_ref[...]).astype(BF)
            v_ref[:, 2 * g * hd:(2 * g + 1) * hd] = x_ref[:, kvw + g * hd:kvw + (g + 1) * hd].astype(BF)
            v_ref[:, (2 * g + 1) * hd:(2 * g + 2) * hd] = jnp.ones((TM, hd), BF)

    return pl.pallas_call(
        body, name="att_prep_kv", grid=(rows // TM,),
        in_specs=[pl.BlockSpec((TM, 2 * kvw), lambda i: (i, C_AK // (2 * kvw))),
                  pl.BlockSpec((TM, hd), lambda i: (i, 0)),
                  pl.BlockSpec((TM, hd), lambda i: (i, 0)),
                  pl.BlockSpec((1, hd), lambda i: (0, 0))],
        out_specs=(pl.BlockSpec((TM, kvw), lambda i: (i, 0)), pl.BlockSpec((TM, 2 * kvw), lambda i: (i, 0))),
        out_shape=(SDS((rows, kvw), BF), SDS((rows, 2 * kvw), BF)),
        compiler_params=_cp(("parallel",)))(px, cos_all, sin_all, knw)


def _att_kv_bwd(dpx, dkl, dkc, dvl, dvc, px, cos_all, sin_all, knw):
    rows = px.shape[0]
    hd = ATT_HEAD_DIM
    kvw = ATT_KV_HEADS * hd
    n_lat = dkl.shape[0] // TM
    assert dkc.shape[0] == TM

    def body(dpx_hbm, dkl_ref, dkc_ref, dvl_ref, dvc_ref, x_ref, cos_ref, sin_ref, w_ref, o_ref, gw_ref):
        i = pl.program_id(0)

        @pl.when(i == 0)
        def _():
            gw_ref[...] = jnp.zeros_like(gw_ref)

        is_lat = i < n_lat
        dk = jnp.where(is_lat, dkl_ref[...], dkc_ref[...])
        dv = jnp.where(is_lat, dvl_ref[...], dvc_ref[...])
        for g in range(ATT_KV_HEADS):
            cols = slice(g * hd, (g + 1) * hd)
            dx, gw = _norm_rope_bwd(dk[:, cols], x_ref[:, cols].astype(F32), w_ref[...], cos_ref[...], sin_ref[...])
            o_ref[:, cols] = dx.astype(BF)
            gw_ref[...] += gw
        o_ref[:, kvw:] = dv.astype(BF)

    lat = pl.BlockSpec((TM, kvw), lambda i: (jnp.minimum(i, n_lat - 1), 0))
    ctx = pl.BlockSpec((TM, kvw), lambda i: (0, 0))
    kvcol = pl.BlockSpec((TM, 2 * kvw), lambda i: (i, C_AK // (2 * kvw)))
    return pl.pallas_call(
        body, name="att_kv_bwd", grid=(rows // TM,), input_output_aliases={0: 0},
        in_specs=[ANY, lat, ctx, lat, ctx, kvcol,
                  pl.BlockSpec((TM, hd), lambda i: (i, 0)),
                  pl.BlockSpec((TM, hd), lambda i: (i, 0)),
                  pl.BlockSpec((1, hd), lambda i: (0, 0))],
        out_specs=(kvcol, pl.BlockSpec((1, hd), lambda i: (0, 0))),
        out_shape=(SDS(dpx.shape, dpx.dtype), SDS((1, hd), F32)),
        compiler_params=_cp(("arbitrary",)))(dpx, dkl, dkc, dvl, dvc, px, cos_all, sin_all, knw)


def _stack_heads(ref_or_val):
    hd = ATT_HEAD_DIM
    return jnp.concatenate([ref_or_val[:, r * hd:(r + 1) * hd] for r in range(ATT_REP)], axis=0)


def _att_scores(q, kl, kc):
    sl = _dot(q, kl, 1, 1)
    sc = _dot(q, kc, 1, 1)
    m = jnp.maximum(jnp.max(sl, axis=-1, keepdims=True), jnp.max(sc, axis=-1, keepdims=True))
    return jnp.exp(sl - m), jnp.exp(sc - m), m


def _att_fwd(qn, kn, vn, n_samp, seq, lc):
    hd = ATT_HEAD_DIM
    tq = ATT_TQ
    nq = seq // tq
    wblk = ATT_REP * hd
    cb = n_samp * seq // lc
    t_lat = n_samp * seq

    def body(q_ref, kl_ref, kc_ref, vl_ref, vc_ref, o_ref, lse_ref):
        lane = lax.broadcasted_iota(jnp.int32, (tq, hd), 1)
        lse = jnp.zeros((tq, hd), F32)
        for r in range(ATT_REP):
            cols = slice(r * hd, (r + 1) * hd)
            el, ec, m = _att_scores(q_ref[:, cols], kl_ref[...], kc_ref[...])
            pv = _dot(el, vl_ref[...]) + _dot(ec, vc_ref[...])
            denom = pv[:, hd:hd + 1]
            o_ref[:, cols] = (pv[:, :hd] / denom).astype(BF)
            lse = jnp.where(lane == r, m + jnp.log(denom), lse)
        lse_ref[...] = lse

    return pl.pallas_call(
        body, name="att_fwd", grid=(n_samp, ATT_KV_HEADS, nq),
        in_specs=[pl.BlockSpec((tq, wblk), lambda b, g, i: (b * nq + i, g)),
                  pl.BlockSpec((seq, hd), lambda b, g, i: (b, g)),
                  pl.BlockSpec((lc, hd), lambda b, g, i: (cb + b, g)),
                  pl.BlockSpec((seq, 2 * hd), lambda b, g, i: (b, g)),
                  pl.BlockSpec((lc, 2 * hd), lambda b, g, i: (cb + b, g))],
        out_specs=(pl.BlockSpec((tq, wblk), lambda b, g, i: (b * nq + i, g)),
                   pl.BlockSpec((tq, hd), lambda b, g, i: (b * nq + i, g))),
        out_shape=(SDS((t_lat, ATT_HEADS * hd), BF), SDS((t_lat, ATT_KV_HEADS * hd), F32)),
        compiler_params=_cp(("parallel", "parallel", "parallel"), 48))(qn, kn, kn, vn, vn)


def _att_bwd(dpx, qn, kn, vn, px, o_att, lse, do_att, cos_all, sin_all, qnw, n_samp, seq, lc, comm):
    hd = ATT_HEAD_DIM
    tq = ATT_TQ
    nq = seq // tq
    wblk = ATT_REP * hd
    cb = n_samp * seq // lc
    t_lat = n_samp * seq
    kvw = ATT_KV_HEADS * hd
    scale = hd ** -0.5

    def body(dpx_hbm, q_ref, kl_ref, kc_ref, vl_ref, vc_ref, o_ref, do_ref, x_ref, cos_ref, sin_ref, w_ref,
             lse_ref, dq_ref, dkl_ref, dkc_ref, dvl_ref, dvc_ref, gw_ref, akl, akc, avl, avc, aw):
        i = pl.program_id(2)

        @pl.when(i == 0)
        def _():
            akl[...] = jnp.zeros_like(akl)
            akc[...] = jnp.zeros_like(akc)
            avl[...] = jnp.zeros_like(avl)
            avc[...] = jnp.zeros_like(avc)
            aw[...] = jnp.zeros_like(aw)

        dobs, pls, pcs, dsls, dscs = [], [], [], [], []
        for r in range(ATT_REP):
            cols = slice(r * hd, (r + 1) * hd)
            dob = do_ref[:, cols]
            delta = jnp.sum(dob.astype(F32) * o_ref[:, cols].astype(F32), axis=-1, keepdims=True)
            lse = lse_ref[:, r:r + 1]
            p_l = jnp.exp(_dot(q_ref[:, cols], kl_ref[...], 1, 1) - lse).astype(BF)
            p_c = jnp.exp(_dot(q_ref[:, cols], kc_ref[...], 1, 1) - lse).astype(BF)
            ds_l = (p_l * (_dot(dob, vl_ref[...], 1, 1) - delta)).astype(BF)
            ds_c = (p_c * (_dot(dob, vc_ref[...], 1, 1) - delta)).astype(BF)
            dq = (_dot(ds_l, kl_ref[...]) + _dot(ds_c, kc_ref[...])) * scale
            dx, gw = _norm_rope_bwd(dq, x_ref[:, cols].astype(F32), w_ref[...], cos_ref[...], sin_ref[...])
            dq_ref[:, cols] = dx.astype(BF)
            aw[...] += gw
            dobs.append(dob)
            pls.append(p_l)
            pcs.append(p_c)
            dsls.append(ds_l)
            dscs.append(ds_c)
        do4 = jnp.concatenate(dobs, axis=0)
        q4 = _stack_heads(q_ref)
        avl[...] += _dot(jnp.concatenate(pls, axis=0), do4, 0, 0)
        avc[...] += _dot(jnp.concatenate(pcs, axis=0), do4, 0, 0)
        akl[...] += _dot(jnp.concatenate(dsls, axis=0), q4, 0, 0)
        akc[...] += _dot(jnp.concatenate(dscs, axis=0), q4, 0, 0)

        @pl.when(i == nq - 1)
        def _():
            dkl_ref[...] = akl[...]
            dkc_ref[...] = akc[...]
            dvl_ref[...] = avl[...]
            dvc_ref[...] = avc[...]
            gw_ref[...] = aw[...]

    return _call(
        body, [dpx, qn, kn, kn, vn, vn, o_att, do_att, px, cos_all, sin_all, qnw, lse], comm,
        name="att_bwd", grid=(n_samp, ATT_KV_HEADS, nq), aliases={0: 0},
        in_specs=[ANY,
                  pl.BlockSpec((tq, wblk), lambda b, g, i: (b * nq + i, g)),
                  pl.BlockSpec((seq, hd), lambda b, g, i: (b, g)),
                  pl.BlockSpec((lc, hd), lambda b, g, i: (cb + b, g)),
                  pl.BlockSpec((seq, hd), lambda b, g, i: (b, 2 * g)),
                  pl.BlockSpec((lc, hd), lambda b, g, i: (cb + b, 2 * g)),
                  pl.BlockSpec((tq, wblk), lambda b, g, i: (b * nq + i, g)),
                  pl.BlockSpec((tq, wblk), lambda b, g, i: (b * nq + i, g)),
                  pl.BlockSpec((tq, wblk), lambda b, g, i: (b * nq + i, C_AQ // wblk + g)),
                  pl.BlockSpec((tq, hd), lambda b, g, i: (b * nq + i, 0)),
                  pl.BlockSpec((tq, hd), lambda b, g, i: (b * nq + i, 0)),
                  pl.BlockSpec((1, hd), lambda b, g, i: (0, 0)),
                  pl.BlockSpec((tq, hd), lambda b, g, i: (b * nq + i, g))],
        out_specs=(pl.BlockSpec((tq, wblk), lambda b, g, i: (b * nq + i, C_AQ // wblk + g)),
                   pl.BlockSpec((seq, hd), lambda b, g, i: (b, g)),
                   pl.BlockSpec((lc, hd), lambda b, g, i: (b, g)),
                   pl.BlockSpec((seq, hd), lambda b, g, i: (b, g)),
                   pl.BlockSpec((lc, hd), lambda b, g, i: (b, g)),
                   pl.BlockSpec((None, None, 1, hd), lambda b, g, i: (b, g, 0, 0))),
        out_shape=(SDS(dpx.shape, dpx.dtype),
                   SDS((t_lat, kvw), F32), SDS((n_samp * lc, kvw), F32),
                   SDS((t_lat, kvw), F32), SDS((n_samp * lc, kvw), F32),
                   SDS((n_samp, ATT_KV_HEADS, 1, hd), F32)),
        scratch_shapes=[pltpu.VMEM((seq, hd), F32), pltpu.VMEM((lc, hd), F32),
                        pltpu.VMEM((seq, hd), F32), pltpu.VMEM((lc, hd), F32), pltpu.VMEM((1, hd), F32)],
        compiler_params=_cp(("arbitrary", "arbitrary", "arbitrary"), 56))


def _merge(x_lat, target, o_f, o_b, o_att, px, gate3, w_o_ret, w_o_att, w_out, tiles_per_sample):
    t_lat = x_lat.shape[0]
    tm = 256
    n_t = t_lat // tm
    per = tiles_per_sample * (TM // tm)
    d = D_MODEL
    rv = RET_HEADS * RET_DV
    n_samp = gate3.shape[0] - 1

    half = d // 2
    n_px = 10

    def body(x_ref, t_ref, of_ref, ob_ref, oa_ref, *rest):
        pxs, rest = rest[:n_px], rest[n_px:]
        (gt_ref, wor_ref, woa_ref, wout_ref,
         gx_ref, dor_ref, doa_ref, dpx_hbm, loss_ref, dgt_ref, gwor_hbm, gwoa_hbm, gwout_hbm,
         aor, aoa, aout, drg_ref, dtail_ref, sems) = rest
        i = pl.program_id(0)

        def copies(step):
            rows = pl.ds(pl.multiple_of(step * tm, tm), tm)
            return (pltpu.make_async_copy(drg_ref, dpx_hbm.at[rows, pl.ds(C_RG, rv)], sems.at[0]),
                    pltpu.make_async_copy(dtail_ref, dpx_hbm.at[rows, pl.ds(C_AG, 3 * d)], sems.at[1]))

        @pl.when(i == 0)
        def _():
            aor[...] = jnp.zeros_like(aor)
            aoa[...] = jnp.zeros_like(aoa)
            aout[...] = jnp.zeros_like(aout)
            loss_ref[...] = jnp.zeros_like(loss_ref)

        @pl.when(i % per == 0)
        def _():
            dgt_ref[...] = jnp.zeros_like(dgt_ref)

        def cat(refs):
            return jnp.concatenate([r[...] for r in refs], axis=1).astype(F32)

        def ret_head(h):
            cols = slice(h * RET_DV, (h + 1) * RET_DV)
            o = of_ref[:, cols].astype(F32) + ob_ref[:, cols].astype(F32)
            r = _rms(o)
            g = pxs[h][...].astype(F32)
            return o * r, r, g, _sigmoid(g)

        def att_half(k):
            o = oa_ref[:, k * half:(k + 1) * half].astype(F32)
            g = pxs[4 + k][...].astype(F32)
            return o, g, _sigmoid(g)

        yrs = []
        for h in range(RET_HEADS):
            on, _, g, sg = ret_head(h)
            yrs.append((on * (g * sg)).astype(BF))
        yr = jnp.concatenate(yrs, axis=1)
        yas = []
        for k in range(2):
            o, g, sg = att_half(k)
            yas.append((o * (g * sg)).astype(BF))
        ya = jnp.concatenate(yas, axis=1)

        a = jnp.dot(yr, wor_ref[...], preferred_element_type=F32)
        b = jnp.dot(ya, woa_ref[...], preferred_element_type=F32)
        sr = _sigmoid(cat(pxs[6:8]))
        sa = _sigmoid(cat(pxs[8:10]))
        yb = (sr * a + sa * b).astype(BF)
        out = jnp.dot(yb, wout_ref[...], preferred_element_type=F32)
        gate = gt_ref[...]
        err = x_ref[...] + gate * out - t_ref[...]
        loss_ref[...] += 0.5 * _sum_all(err * err) * (1.0 / d)
        dy_tok = err * (1.0 / d)
        gx_ref[...] = dy_tok
        dgt_ref[...] += jnp.sum(dy_tok * out, axis=0, keepdims=True)
        dout = (dy_tok * gate).astype(BF)
        aout[...] += _dot(yb, dout, 0, 0)
        dyy = _dot(dout, wout_ref[...], 1, 1)
        da = (dyy * sr).astype(BF)
        db = (dyy * sa).astype(BF)
        aor[...] += _dot(yr, da, 0, 0)
        aoa[...] += _dot(ya, db, 0, 0)
        dyr = _dot(da, wor_ref[...], 1, 1)
        dya = _dot(db, woa_ref[...], 1, 1)

        @pl.when(i > 0)
        def _():
            for cp in copies(i - 1):
                cp.wait()

        dtail_ref[:, d:2 * d] = (dyy * a * (sr * (1.0 - sr))).astype(BF)
        dtail_ref[:, 2 * d:] = (dyy * b * (sa * (1.0 - sa))).astype(BF)
        for h in range(RET_HEADS):
            cols = slice(h * RET_DV, (h + 1) * RET_DV)
            on, r, g, sg = ret_head(h)
            dy = dyr[:, cols]
            drg_ref[:, cols] = (dy * on * (sg * (1.0 + g * (1.0 - sg)))).astype(BF)
            dor_ref[:, cols] = _rms_bwd(dy * (g * sg), on, r).astype(BF)
        for k in range(2):
            cols = slice(k * half, (k + 1) * half)
            o, g, sg = att_half(k)
            dy = dya[:, cols]
            dtail_ref[:, cols] = (dy * o * (sg * (1.0 + g * (1.0 - sg)))).astype(BF)
            doa_ref[:, cols] = (dy * (g * sg)).astype(BF)
        for cp in copies(i):
            cp.start()

        @pl.when(i == n_t - 1)
        def _():
            for cp in copies(i):
                cp.wait()
            pltpu.sync_copy(aor, gwor_hbm)
            pltpu.sync_copy(aoa, gwoa_hbm)
            pltpu.sync_copy(aout, gwout_hbm)

    def px_blk(col):
        return pl.BlockSpec((tm, half), lambda i: (i, col // half))

    def resident(shape):
        return pl.BlockSpec(shape, lambda i: (0, 0), pipeline_mode=pl.Buffered(1))

    px_cols = ([C_RG + k * half for k in range(4)] + [C_AG, C_AG + half]
               + [C_MR, C_MR + half, C_MA, C_MA + half])
    return pl.pallas_call(
        body, name="merge", grid=(n_t,),
        in_specs=[pl.BlockSpec((tm, d), lambda i: (i, 0)),
                  pl.BlockSpec((tm, d), lambda i: (i, 0)),
                  pl.BlockSpec((tm, rv), lambda i: (i, 0)),
                  pl.BlockSpec((tm, rv), lambda i: (i, 0)),
                  pl.BlockSpec((tm, d), lambda i: (i, 0))]
        + [px_blk(col) for col in px_cols]
        + [pl.BlockSpec((None, 1, d), lambda i: (i // per, 0, 0)),
           resident((rv, d)), resident((d, d)), resident((d, d))],
        out_specs=(pl.BlockSpec((tm, d), lambda i: (i, 0)),
                   pl.BlockSpec((tm, rv), lambda i: (i, 0)),
                   pl.BlockSpec((tm, d), lambda i: (i, 0)),
                   ANY,
                   pl.BlockSpec((8, 128), lambda i: (0, 0)),
                   pl.BlockSpec((None, 1, d), lambda i: (i // per, 0, 0)),
                   ANY, ANY, ANY),
        out_shape=(SDS((t_lat, d), F32), SDS((t_lat, rv), BF), SDS((t_lat, d), BF),
                   SDS((px.shape[0], IN_COLS), BF),
                   SDS((8, 128), F32), SDS((n_samp, 1, d), F32),
                   SDS((rv, d), F32), SDS((d, d), F32), SDS((d, d), F32)),
        scratch_shapes=[pltpu.VMEM((rv, d), F32), pltpu.VMEM((d, d), F32), pltpu.VMEM((d, d), F32),
                        pltpu.VMEM((tm, rv), BF), pltpu.VMEM((tm, 3 * d), BF), pltpu.SemaphoreType.DMA((2,))],
        compiler_params=_cp(("arbitrary",), 56))(
            x_lat, target, o_f, o_b, o_att, *([px] * n_px), gate3, w_o_ret, w_o_att, w_out)


def _place():
    x, y, c = lax.axis_index("x"), lax.axis_index("y"), lax.axis_index("c")
    chips = [(1 - x, y), (x, 1 - y), (1 - x, 1 - y)]
    return x, y, c, chips


def _remote(src, dst, send_sem, recv_sem, to):
    return pltpu.make_async_remote_copy(src_ref=src, dst_ref=dst, send_sem=send_sem, recv_sem=recv_sem,
                                        device_id=to, device_id_type=MESH)


def _place_ids():
    x, y, c = lax.axis_index("x"), lax.axis_index("y"), lax.axis_index("c")
    me = 2 * x + y
    return jnp.stack([x, y, c, me, me, 2 * (1 - x) + y, 2 * x + 1 - y, 2 * (1 - x) + 1 - y]).astype(jnp.int32)


def _ag_comm(bufs):
    n, m = len(bufs), 3

    def half(ref, s, which):
        h = ref.shape[1] // 2
        return ref.at[s, pl.ds(which * h, h), :]

    def ici(ins, outs, ssem, rsem, base):
        x, y, c, chips = _place()
        sends, recvs = [], []
        for a in range(n):
            for j in range(m):
                k, chip = base + a * m + j, chips[j]
                mine, theirs = half(outs[a], 2 * x + y, c), half(outs[a], 2 * chip[0] + chip[1], c)
                sends.append(_remote(mine, mine, ssem.at[k], rsem.at[k], (*chip, c)))
                recvs.append(_remote(theirs, theirs, ssem.at[k], rsem.at[k], (*chip, c)))
        return sends, recvs

    def d2d(ins, outs, ssem, rsem, base):
        x, y, c, chips = _place()
        sends, recvs = [], []
        for a in range(n):
            for j in range(m):
                k, s = base + (n + a) * m + j, 2 * chips[j][0] + chips[j][1]
                sends.append(_remote(half(outs[a], s, c), half(outs[a], s, c), ssem.at[k], rsem.at[k], (x, y, 1 - c)))
                recvs.append(_remote(half(outs[a], s, 1 - c), half(outs[a], s, 1 - c), ssem.at[k], rsem.at[k],
                                     (x, y, 1 - c)))
        return sends, recvs

    return _Comm("all_gather", tuple(bufs), tuple(SDS(b.shape, b.dtype) for b in bufs), {a: a for a in range(n)},
                 2 * n * m, (ici, d2d))


def _swap_comm(grads):
    n = len(grads)

    def phase(ins, outs, ssem, rsem, base):
        x, y, c, _ = _place()
        sends = []
        for a in range(n):
            h = ins[a].shape[1] // 2
            sends.append(_remote(ins[a].at[:, pl.ds((1 - c) * h, h), :], outs[a], ssem.at[base + a],
                                 rsem.at[base + a], (x, y, 1 - c)))
        return sends, sends

    return _Comm("swap_halves", tuple(grads),
                 tuple(SDS((g.shape[0], g.shape[1] // 2, g.shape[2]), g.dtype) for g in grads), {}, n, (phase,))


def _exchange_comm(parts):
    n = len(parts)

    def phase(ins, outs, ssem, rsem, base):
        x, y, c, chips = _place()
        sends = []
        for a in range(n):
            for j, chip in enumerate(chips):
                k = base + 3 * a + j
                sends.append(_remote(ins[a].at[2 * chip[0] + chip[1]], outs[a].at[j], ssem.at[k], rsem.at[k],
                                     (*chip, c)))
        return sends, sends

    return _Comm("exchange_shards", tuple(parts), tuple(SDS((3,) + p.shape[1:], p.dtype) for p in parts), {}, 3 * n,
                 (phase,))


def _join_comm(bufs):
    n = len(bufs)

    def phase(ins, outs, ssem, rsem, base):
        x, y, c, _ = _place()
        sends, recvs = [], []
        for a in range(n):
            h = outs[a].shape[0] // 2
            mine, other = outs[a].at[pl.ds(c * h, h), :], outs[a].at[pl.ds((1 - c) * h, h), :]
            sends.append(_remote(mine, mine, ssem.at[base + a], rsem.at[base + a], (x, y, 1 - c)))
            recvs.append(_remote(other, other, ssem.at[base + a], rsem.at[base + a], (x, y, 1 - c)))
        return sends, recvs

    return _Comm("join_halves", tuple(bufs), tuple(SDS(b.shape, b.dtype) for b in bufs), {a: a for a in range(n)},
                 n, (phase,))


def _cast_place(w, ids):
    rows, cols = w.shape
    tr = min(rows, 256)

    def body(ids_ref, w_ref, o_ref):
        o_ref[...] = w_ref[...].astype(BF)

    return pl.pallas_call(
        body, name="cast_place",
        grid_spec=pltpu.PrefetchScalarGridSpec(
            num_scalar_prefetch=1, grid=(rows // tr,),
            in_specs=[pl.BlockSpec((tr, cols), lambda i, ids_ref: (i, 0))],
            out_specs=pl.BlockSpec((None, tr, cols), lambda i, ids_ref: (ids_ref[3], i, 0))),
        out_shape=SDS((N_SHARD, rows, cols), BF),
        compiler_params=_cp(("parallel",), 40))(ids, w)


def _chip_sum(g, p, ids):
    n_s, rows, cols = g.shape
    h = rows // 2
    tr = min(h, 256)
    nb = h // tr

    def body(ids_ref, g_ref, p_ref, o_ref, o16_ref):
        t = g_ref[...] + p_ref[...]
        o_ref[...] = t
        o16_ref[...] = t.astype(BF)

    out_spec = pl.BlockSpec((None, tr, cols), lambda s, i, ids_ref: (s, i, 0))
    return pl.pallas_call(
        body, name="chip_sum",
        grid_spec=pltpu.PrefetchScalarGridSpec(
            num_scalar_prefetch=1, grid=(n_s, nb),
            in_specs=[pl.BlockSpec((None, tr, cols), lambda s, i, ids_ref: (s, ids_ref[2] * nb + i, 0)),
                      pl.BlockSpec((None, tr, cols), lambda s, i, ids_ref: (s, i, 0))],
            out_specs=(out_spec, out_spec)),
        out_shape=(SDS((n_s, h, cols), g.dtype), SDS((n_s, h, cols), BF)),
        compiler_params=_cp(("parallel", "parallel"), 40))(ids, g, p)


def _shard_sum(t, q, ids):
    _, h, cols = t.shape
    tr = min(h, 256)
    nb = h // tr

    def body(ids_ref, t_ref, q_ref, o_ref):
        o_ref[...] = ((t_ref[...] + q_ref[0].astype(F32)) + q_ref[1].astype(F32)) + q_ref[2].astype(F32)

    return pl.pallas_call(
        body, name="shard_sum",
        grid_spec=pltpu.PrefetchScalarGridSpec(
            num_scalar_prefetch=1, grid=(nb,),
            in_specs=[pl.BlockSpec((None, tr, cols), lambda i, ids_ref: (ids_ref[3], i, 0)),
                      pl.BlockSpec((3, tr, cols), lambda i, ids_ref: (0, i, 0))],
            out_specs=pl.BlockSpec((tr, cols), lambda i, ids_ref: (ids_ref[2] * nb + i, 0))),
        out_shape=SDS((2 * h, cols), t.dtype),
        compiler_params=_cp(("parallel",), 40))(ids, t, q)


def _gather_small(block, n_sum):
    rows, cols = block.shape
    n_dev = 8

    def body(x_ref, o_ref, g_ref, buf, send_sems, recv_sems, local_sem):
        x, y, c, chips = _place()
        me, sibling = (x, y, c), (x, y, 1 - c)

        def slot(px_, py_, pc_):
            return buf.at[4 * px_ + 2 * py_ + pc_]

        def copy(k, who, to, src=None):
            return _remote(slot(*who) if src is None else src, slot(*who), send_sems.at[k], recv_sems.at[k], to)

        mine = pltpu.make_async_copy(x_ref, slot(*me), local_sem)
        mine.start()
        first = [copy(0, me, sibling, src=x_ref)]
        first += [copy(1 + j, me, (*chip, c), src=x_ref) for j, chip in enumerate(chips)]
        for cp in first:
            cp.start()
        passed = [copy(4 + j, (*chip, c), sibling) for j, chip in enumerate(chips)]
        for j, chip in enumerate(chips):
            copy(1 + j, (*chip, c), me).wait_recv()
            passed[j].start()
        copy(0, sibling, me).wait_recv()
        for j, chip in enumerate(chips):
            copy(4 + j, (*chip, 1 - c), me).wait_recv()
        for cp in first + passed:
            cp.wait_send()
        mine.wait()
        acc = buf[0, :, :n_sum]
        for s in range(1, n_dev):
            acc = acc + buf[s, :, :n_sum]
        o_ref[...] = acc
        for s in range(n_dev):
            g_ref[s * rows:(s + 1) * rows, :] = buf[s, :, n_sum:]

    return pl.pallas_call(
        body, name="gather_small",
        in_specs=[pl.BlockSpec(memory_space=pltpu.VMEM)],
        out_specs=(pl.BlockSpec(memory_space=pltpu.VMEM), pl.BlockSpec(memory_space=pltpu.VMEM)),
        out_shape=(SDS((rows, n_sum), F32), SDS((n_dev * rows, cols - n_sum), F32)),
        scratch_shapes=[pltpu.VMEM((n_dev, rows, cols), F32), pltpu.SemaphoreType.DMA((7,)),
                        pltpu.SemaphoreType.DMA((7,)), pltpu.SemaphoreType.DMA],
        compiler_params=_cp(has_side_effects=True))(block)


def _adam_math(w, g, m, v):
    m = ADAM_B1 * m + (1.0 - ADAM_B1) * g
    v = ADAM_B2 * v + (1.0 - ADAM_B2) * (g * g)
    m_hat = m / (1.0 - ADAM_B1 ** ADAM_STEP)
    v_hat = v / (1.0 - ADAM_B2 ** ADAM_STEP)
    delta = -ADAM_LR * (m_hat / (jnp.sqrt(v_hat) + ADAM_EPS) + ADAM_WD * w)
    return delta, m, v


def _adamw(w, g, m, v):
    rows, cols = w.shape
    tr = min(rows, 256 if cols <= 2048 else 128)

    def body(w_ref, g_ref, m_ref, v_ref, go_ref, d_ref, nm_ref, nv_ref):
        g = g_ref[...]
        go_ref[...] = g
        d_ref[...], nm_ref[...], nv_ref[...] = _adam_math(w_ref[...], g, m_ref[...], v_ref[...])

    spec = pl.BlockSpec((tr, cols), lambda i: (i, 0))
    return pl.pallas_call(
        body, name="adamw", grid=(rows // tr,), in_specs=[spec] * 4, out_specs=(spec,) * 4,
        out_shape=(SDS(w.shape, F32),) * 4, compiler_params=_cp(("parallel",), 40))(w, g, m, v)


def _adamw_small(w, g, m, v):
    def body(w_ref, g_ref, m_ref, v_ref, go_ref, d_ref, nm_ref, nv_ref):
        w = w_ref[...]
        g = g_ref[...]
        sub = lax.broadcasted_iota(jnp.int32, w.shape, 0)
        lane = lax.broadcasted_iota(jnp.int32, w.shape, 1)
        is_ret = jnp.logical_and(sub == 5, lane < 2 * RET_HEADS)
        u = jnp.exp(jnp.where(is_ret, w, -1.0) * jnp.log(2.0))
        g = jnp.where(is_ret, g * (-u * jnp.log(2.0) / (1.0 - u)), g)
        go_ref[...] = g
        d_ref[...], nm_ref[...], nv_ref[...] = _adam_math(w, g, m_ref[...], v_ref[...])

    return pl.pallas_call(body, name="adamw_small", out_shape=(SDS(w.shape, F32),) * 4)(w, g, m, v)


def _rope_tables(seq, n_samp, n_ctx_rows):
    rows = seq // GRID_W
    row = jnp.repeat(jnp.arange(rows, dtype=F32), GRID_W)
    col = jnp.tile(jnp.arange(GRID_W, dtype=F32), rows)
    half = ATT_HEAD_DIM // 2
    freqs = ROPE_THETA ** (-jnp.arange(0, half, 2, dtype=F32) / half)
    ang = jnp.concatenate([row[:, None] * freqs, col[:, None] * freqs], axis=-1)
    cos, sin = jnp.cos(ang), jnp.sin(ang)
    cos_f = jnp.repeat(cos, 2, axis=1)
    sin_s = jnp.stack([-sin, sin], axis=-1).reshape(seq, ATT_HEAD_DIM)
    cos_all = jnp.concatenate([jnp.tile(cos_f, (n_samp, 1)), jnp.ones((n_ctx_rows, ATT_HEAD_DIM), F32)], axis=0)
    sin_all = jnp.concatenate([jnp.tile(sin_s, (n_samp, 1)), jnp.zeros((n_ctx_rows, ATT_HEAD_DIM), F32)], axis=0)
    return cos_all, sin_all


def _pack_small(c_ctx, norm_w, b_ada, ret, qn, kn):
    d = D_MODEL
    row5 = jnp.concatenate([ret.reshape(-1), jnp.zeros((128 - 2 * RET_HEADS,), F32), qn.reshape(-1), kn.reshape(-1),
                            jnp.zeros((d - 384,), F32)])
    return jnp.concatenate([c_ctx.reshape(1, d), norm_w.reshape(1, d), b_ada.reshape(3, d), row5.reshape(1, d),
                            jnp.zeros((2, d), F32)], axis=0)


def _unpack_small(p):
    d = D_MODEL
    return (p[0], p[1:2], p[2:5].reshape(1, 3 * d), p[5, :2 * RET_HEADS].reshape(1, 2, RET_HEADS),
            p[5:6, 128:256], p[5:6, 256:384])


def _step(x, c, ctx, c_ctx, norm_w, b_ada, ret_log2_decay, q_norm_w, k_norm_w, loss_target, weights, ids):
    n_samp, seq, d = x.shape
    lc = ctx.shape[1]
    t_lat, t_ctx = n_samp * seq, n_samp * lc
    assert seq % TM == 0 and t_ctx == TM and t_lat % lc == 0 and seq % GRID_W == 0
    tps = seq // TM

    x_lat = x.reshape(t_lat, d)
    x_ctx = ctx.reshape(t_ctx, d)
    cvec8 = jnp.concatenate([c, c_ctx.reshape(1, d), jnp.zeros((8 - n_samp - 1, d), F32)], axis=0)
    lg = jnp.log1p(-jnp.exp2(ret_log2_decay.reshape(2, RET_HEADS)))
    cos_all, sin_all = _rope_tables(seq, n_samp, t_ctx)

    w_ada_b, w_in_b, w_or_b, w_oa_b, w_out_b = weights
    (w_ada_g,) = _run_comm(_ag_comm((w_ada_b,)))
    mod8 = _adaln_fwd(cvec8, w_ada_g, b_ada)
    mod3 = mod8[:n_samp + 1]
    shift3 = mod3[:, None, 0:d]
    scale3 = mod3[:, None, d:2 * d]
    gate3 = mod3[:, None, 2 * d:3 * d]

    hx, hxt = _norm_fwd(x_lat, x_ctx, norm_w, scale3, shift3, tps, n_samp)
    px, w_in_g = _in_proj_gather(hx, w_in_b, ids)

    states0 = _ctx_state_fwd(px, lg, n_samp, t_lat, lc)
    (o_f, o_b, saved), w_o = _ret_fwd(px, states0, lg, n_samp, seq,
                                      comm=_ag_comm((w_or_b, w_oa_b, w_out_b)))
    w_o_ret, w_o_att, w_out = (w.reshape(-1, d) for w in w_o)

    qn = _att_prep_q(px, cos_all, sin_all, q_norm_w, t_lat)
    kn, vn = _att_prep_kv(px, cos_all, sin_all, k_norm_w)
    o_att, lse = _att_fwd(qn, kn, vn, n_samp, seq, lc)

    (gx_res, do, do_att, dpx, loss8, dgate, g_w_o_ret, g_w_o_att, g_w_out) = _merge(
        x_lat, loss_target.reshape(t_lat, d), o_f, o_b, o_att, px, gate3, w_o_ret, w_o_att, w_out, tps)

    g_a = [g.reshape(N_SHARD, -1, d) for g in (g_w_o_ret, g_w_o_att, g_w_out)]
    (dpx, dkl, dkc, dvl, dvc, gqw), sib_a = _att_bwd(
        dpx, qn, kn, vn, px, o_att, lse, do_att, cos_all, sin_all, q_norm_w, n_samp, seq, lc, comm=_swap_comm(g_a))
    dpx, gkw = _att_kv_bwd(dpx, dkl, dkc, dvl, dvc, px, cos_all, sin_all, k_norm_w)
    t_a = [_chip_sum(g, p, ids) for g, p in zip(g_a, sib_a)]

    (dpx, dstates, dlg_lat), q_a = _ret_bwd(dpx, px, do, saved, lg, n_samp, seq,
                                            comm=_exchange_comm([t16 for _, t16 in t_a]))
    r_a = [_shard_sum(t, q, ids) for (t, _), q in zip(t_a, q_a)]
    dpx, dlg_ctx = _ctx_state_bwd(dpx, px, dstates, lg, n_samp, t_lat, lc)
    dpx = _zero_ctx_tail(dpx, t_lat)

    n_tiles = dpx.shape[0] // _big_rows(dpx.shape[0])
    g_b = _gw_in(hxt, dpx)
    dhx, (sib_b, *r_a) = _dhx(dpx, w_in_g, 0, 1, None, _join_comms(_swap_comm([g_b]), _join_comm(r_a)))
    t_b, t16_b = _chip_sum(g_b, sib_b, ids)
    dhx, (q_b,) = _dhx(dpx, w_in_g, 1, n_tiles - 1, dhx, _exchange_comm([t16_b]))
    r_b_half = _shard_sum(t_b, q_b, ids)
    grad_x, dshift, dscale, g_norm_w = _norm_bwd(x_lat, x_ctx, dhx, gx_res, norm_w, scale3, tps, n_samp)

    dgate_all = jnp.concatenate([dgate, jnp.zeros((1, 1, d), F32)], axis=0)
    dmod3 = jnp.concatenate([dshift, dscale, dgate_all], axis=2).reshape(n_samp + 1, 3 * d)
    dmod8 = jnp.concatenate([dmod3, jnp.zeros((8 - n_samp - 1, 3 * d), F32)], axis=0)
    g_lg = (jnp.sum(dlg_lat[:, :, 0], axis=0).reshape(2, RET_HEADS)
            + jnp.stack([jnp.sum(dlg_ctx[:, :, 0, 0], axis=0), jnp.sum(dlg_ctx[:, :, 1, 0], axis=0)], axis=0))
    g_qw = jnp.sum(gqw, axis=(0, 1, 2))
    zero = jnp.zeros((d,), F32)

    local = _pack_small(zero, g_norm_w, jnp.zeros((3 * d,), F32), g_lg, g_qw, gkw).at[6, 0].set(loss8[0, 0])
    small_sum, gathered = _gather_small(jnp.concatenate([local, cvec8, dmod8], axis=1), d)
    (g_w_ada, g_b_ada, dc_all), (r_b,) = _adaln_bwd(gathered[:, :d], gathered[:, d:], w_ada_g,
                                                     comm=_join_comm([r_b_half]))
    dc_ctx = jnp.sum(dc_all.reshape(-1, 8, d)[:, n_samp], axis=0)
    small = small_sum + _pack_small(dc_ctx, zero, g_b_ada, jnp.zeros((2, RET_HEADS), F32), zero[:128], zero[:128])
    r_c = lax.dynamic_index_in_dim(g_w_ada, ids[3], 0, keepdims=False)
    return small[6, 0], grad_x.reshape(n_samp, seq, d), (r_c, r_b, *r_a), small


def kernel(x, c, ctx, c_ctx, norm_w, w_ada, b_ada, w_in, ret_log2_decay, q_norm_w, k_norm_w, w_o_ret, w_o_att, w_out, loss_target, m_c_ctx, m_norm_w, m_w_ada, m_b_ada, m_w_in, m_ret_log2_decay, m_q_norm_w, m_k_norm_w, m_w_o_ret, m_w_o_att, m_w_out, v_c_ctx, v_norm_w, v_w_ada, v_b_ada, v_w_in, v_ret_log2_decay, v_q_norm_w, v_k_norm_w, v_w_o_ret, v_w_o_att, v_w_out):
    big_w = (w_ada[0], w_in[0], w_o_ret[0], w_o_att[0], w_out[0])
    big_m = (m_w_ada[0], m_w_in[0], m_w_o_ret[0], m_w_o_att[0], m_w_out[0])
    big_v = (v_w_ada[0], v_w_in[0], v_w_o_ret[0], v_w_o_att[0], v_w_out[0])

    ids = _place_ids()
    loss, grad_x, big_grad, small_grad_in = _step(
        x, c, ctx, c_ctx, norm_w[0:1], b_ada[0:1], ret_log2_decay[0], q_norm_w[0:1], k_norm_w[0:1], loss_target,
        tuple(_cast_place(w, ids) for w in big_w), ids)
    small_w = _pack_small(c_ctx, norm_w, b_ada, ret_log2_decay, q_norm_w, k_norm_w)
    small_m = _pack_small(m_c_ctx, m_norm_w, m_b_ada, m_ret_log2_decay, m_q_norm_w, m_k_norm_w)
    small_v = _pack_small(v_c_ctx, v_norm_w, v_b_ada, v_ret_log2_decay, v_q_norm_w, v_k_norm_w)
    small_grad, small_delta, small_nm, small_nv = _adamw_small(small_w, small_grad_in, small_m, small_v)

    big_g, big_delta, big_nm, big_nv = [], [], [], []
    for w, g, m, v in zip(big_w, big_grad, big_m, big_v):
        go, dlt, nm, nv = _adamw(w, g, m, v)
        big_g.append(go[None])
        big_delta.append(dlt[None])
        big_nm.append(nm[None])
        big_nv.append(nv[None])
    big_grad = big_g

    def order(small_packed, big):
        s = _unpack_small(small_packed)
        return (s[0], s[1], big[0], s[2], big[1], s[3], s[4], s[5], big[2], big[3], big[4])

    return (loss, grad_x, *order(small_grad, big_grad), *order(small_delta, big_delta),
            *order(small_nm, big_nm), *order(small_nv, big_nv))
```

```python
import functools
from typing import NamedTuple

import jax
import jax.numpy as jnp
from jax import lax
from jax.experimental import pallas as pl
from jax.experimental.pallas import tpu as pltpu

F32 = jnp.float32
BF = jnp.bfloat16
SDS = jax.ShapeDtypeStruct
MESH = pl.DeviceIdType.MESH
ANY = pl.BlockSpec(memory_space=pl.ANY)
SMEM = pl.BlockSpec(memory_space=pltpu.SMEM)

D_MODEL = 1024
GRID_W = 64
RET_HEADS = 4
RET_DK = 256
RET_DV = 512
RET_CHUNK = 128
ATT_HEADS = 8
ATT_KV_HEADS = 2
ATT_REP = ATT_HEADS // ATT_KV_HEADS
ATT_HEAD_DIM = 128
ROPE_THETA = 10000.0
NORM_EPS = 1e-6
IN_COLS = 10752
KV_COLS = 3584
C_RK, C_RV, C_AK, C_AV, C_RQ, C_RG, C_AQ, C_AG, C_MR, C_MA = 0, 1024, 3072, 3328, 3584, 4608, 6656, 7680, 8704, 9728
N_SHARD = 4
ADA_W = 3 * D_MODEL // N_SHARD
IN_W = IN_COLS // N_SHARD
IN_BLK = IN_W
BPS = IN_W // IN_BLK
N_IN_BLK = IN_COLS // IN_BLK
TM = 512
ATT_TQ = 512
ADAM_LR, ADAM_B1, ADAM_B2, ADAM_EPS, ADAM_WD, ADAM_STEP = 0.001, 0.9, 0.999, 1e-08, 0.01, 10
MIB = 1024 * 1024


def _cp(sem=None, vmem_mb=None, **kw):
    if sem is not None:
        kw["dimension_semantics"] = sem
    if vmem_mb is not None:
        kw["vmem_limit_bytes"] = vmem_mb * MIB
    return pltpu.CompilerParams(**kw)


def _dot(a, b, ca=1, cb=0):
    return lax.dot_general(a.astype(BF), b.astype(BF), (((ca,), (cb,)), ((), ())), preferred_element_type=F32)


def _hbm(*arrays):
    pinned = tuple(pltpu.with_memory_space_constraint(a, pltpu.HBM) for a in arrays)
    return pinned[0] if len(pinned) == 1 else pinned


def _sigmoid(x):
    return 0.5 * jnp.tanh(0.5 * x) + 0.5


def _sum_all(x):
    return jnp.sum(jnp.sum(x, axis=1, keepdims=True), axis=0, keepdims=True)


def _swap_pairs(x):
    ax = x.ndim - 1
    lane = lax.broadcasted_iota(jnp.int32, x.shape, ax)
    nxt = pltpu.roll(x, x.shape[ax] - 1, ax)
    prv = pltpu.roll(x, 1, ax)
    return jnp.where(lane % 2 == 0, nxt, prv)


def _rms(x):
    return lax.rsqrt(jnp.mean(x * x, axis=-1, keepdims=True) + NORM_EPS)


def _rms_bwd(dxh, xh, r):
    return r * (dxh - xh * jnp.mean(dxh * xh, axis=-1, keepdims=True))


class _Comm(NamedTuple):
    name: str
    ins: tuple
    out_shapes: tuple
    aliases: dict
    n_sems: int
    phases: tuple


def _join_comms(*comms):
    offs, i_off, o_off, s_off = [], 0, 0, 0
    for cm in comms:
        offs.append((i_off, o_off, s_off))
        i_off, o_off, s_off = i_off + len(cm.ins), o_off + len(cm.out_shapes), s_off + cm.n_sems

    def phase(k):
        def run(ins, outs, ssem, rsem, base):
            sends, recvs = [], []
            for cm, (io, oo, so) in zip(comms, offs):
                if k < len(cm.phases):
                    s, r = cm.phases[k](ins[io:io + len(cm.ins)], outs[oo:oo + len(cm.out_shapes)], ssem, rsem,
                                        base + so)
                    sends += s
                    recvs += r
            return sends, recvs
        return run

    aliases = {}
    for cm, (io, oo, _) in zip(comms, offs):
        aliases.update({io + a: oo + b for a, b in cm.aliases.items()})
    return _Comm("+".join(cm.name for cm in comms), sum((cm.ins for cm in comms), ()),
                 sum((cm.out_shapes for cm in comms), ()), aliases, s_off,
                 tuple(phase(k) for k in range(max(len(cm.phases) for cm in comms))))


def _run_phases(comm, cins, couts, ssem, rsem, first_started):
    for k, phase in enumerate(comm.phases):
        sends, recvs = phase(cins, couts, ssem, rsem, 0)
        if k > 0 or not first_started:
            for cp in sends:
                cp.start()
        for cp in recvs:
            cp.wait_recv()
        for cp in sends:
            cp.wait_send()


def _call(body, args, comm, *, name, grid, in_specs, out_specs, out_shape, scratch_shapes=(),
          compiler_params, aliases=None):
    n_in, n_out, n_sc = len(in_specs), len(out_specs), len(scratch_shapes)
    n_ci, n_co = len(comm.ins), len(comm.out_shapes)
    io_alias = dict(aliases or {})
    io_alias.update({n_in + a: n_out + b for a, b in comm.aliases.items()})

    def kernel_body(*refs):
        ins, cins = refs[:n_in], refs[n_in:n_in + n_ci]
        outs = refs[n_in + n_ci:n_in + n_ci + n_out]
        couts = refs[n_in + n_ci + n_out:n_in + n_ci + n_out + n_co]
        scratch = refs[n_in + n_ci + n_out + n_co:n_in + n_ci + n_out + n_co + n_sc]
        ssem, rsem = refs[-2:]
        first = functools.reduce(jnp.logical_and, [pl.program_id(k) == 0 for k in range(len(grid))])
        last = functools.reduce(jnp.logical_and, [pl.program_id(k) == grid[k] - 1 for k in range(len(grid))])

        @pl.when(first)
        def _():
            for cp in comm.phases[0](cins, couts, ssem, rsem, 0)[0]:
                cp.start()

        body(*ins, *outs, *scratch)

        @pl.when(last)
        def _():
            _run_phases(comm, cins, couts, ssem, rsem, True)

    res = pl.pallas_call(
        kernel_body, name=name + "+" + comm.name, grid=grid, in_specs=list(in_specs) + [ANY] * n_ci,
        out_specs=tuple(out_specs) + tuple([ANY] * n_co), out_shape=tuple(out_shape) + tuple(comm.out_shapes),
        scratch_shapes=list(scratch_shapes) + [pltpu.SemaphoreType.DMA((comm.n_sems,)),
                                               pltpu.SemaphoreType.DMA((comm.n_sems,))],
        input_output_aliases=io_alias, compiler_params=compiler_params)(*args, *comm.ins)
    return tuple(res[:n_out]), tuple(res[n_out:])


def _run_comm(comm):
    n_ci, n_co = len(comm.ins), len(comm.out_shapes)

    def body(*refs):
        _run_phases(comm, refs[:n_ci], refs[n_ci:n_ci + n_co], refs[-2], refs[-1], False)

    return pl.pallas_call(
        body, name=comm.name, in_specs=[ANY] * n_ci, out_specs=tuple([ANY] * n_co), out_shape=tuple(comm.out_shapes),
        input_output_aliases=dict(comm.aliases),
        scratch_shapes=[pltpu.SemaphoreType.DMA((comm.n_sems,)), pltpu.SemaphoreType.DMA((comm.n_sems,))],
        compiler_params=_cp(has_side_effects=True))(*comm.ins)


def _adaln_fwd(cvec8, w_ada_g, b_ada):
    def body(c_ref, w_ref, b_ref, o_ref):
        cv = c_ref[...]
        sc = (cv * _sigmoid(cv)).astype(BF)
        for s in range(N_SHARD):
            cols = slice(s * ADA_W, (s + 1) * ADA_W)
            o_ref[:, cols] = jnp.dot(sc, w_ref[s], preferred_element_type=F32) + b_ref[:, cols]

    return pl.pallas_call(body, out_shape=SDS((8, 3 * D_MODEL), F32), name="adaln_fwd",
                          compiler_params=_cp(vmem_mb=32))(cvec8, w_ada_g, b_ada)


def _adaln_bwd(cvec, dmod, w_ada_g, comm):
    n_rows = cvec.shape[0]
    def body(c_ref, d_ref, w_ref, gw_ref, gb_ref, dc_ref):
        cv = c_ref[...]
        sg = _sigmoid(cv)
        sc = cv * sg
        dm = d_ref[...]
        gb_ref[...] = jnp.sum(dm, axis=0, keepdims=True)
        dsc = jnp.zeros(cv.shape, F32)
        for s in range(N_SHARD):
            cols = slice(s * ADA_W, (s + 1) * ADA_W)
            gw_ref[s] = _dot(sc, dm[:, cols], 0, 0)
            dsc = dsc + _dot(dm[:, cols], w_ref[s], 1, 1)
        dc_ref[...] = dsc * (sg * (1.0 + cv * (1.0 - sg)))

    def whole(shape):
        return pl.BlockSpec(shape, lambda i: (0,) * len(shape))

    shapes = ((N_SHARD, D_MODEL, ADA_W), (1, 3 * D_MODEL), (n_rows, D_MODEL))
    return _call(body, [cvec, dmod, w_ada_g], comm, name="adaln_bwd", grid=(1,),
                 in_specs=[whole(cvec.shape), whole(dmod.shape), whole(w_ada_g.shape)],
                 out_specs=tuple(whole(s) for s in shapes), out_shape=tuple(SDS(s, F32) for s in shapes),
                 compiler_params=_cp(("arbitrary",), 56))


def _big_rows(rows):
    return 1536 if rows % 1536 == 0 else TM


def _norm_fwd(x_lat, x_ctx, norm_w, scale3, shift3, tiles_per_sample, n_samp):
    n_lat = x_lat.shape[0] // TM
    rows = x_lat.shape[0] + x_ctx.shape[0]

    def samp(i):
        return jnp.minimum(i // tiles_per_sample, n_samp)

    def body(x_ref, c_ref, nw_ref, sc_ref, sh_ref, hx_ref, hxt_ref):
        x = jnp.where(pl.program_id(0) < n_lat, x_ref[...], c_ref[...])
        h = x * _rms(x) * nw_ref[...] * (1.0 + sc_ref[...]) + sh_ref[...]
        hx_ref[...] = h.astype(BF)
        hxt_ref[...] = h.T.astype(BF)

    return pl.pallas_call(
        body, name="norm_fwd", grid=(rows // TM,),
        in_specs=[pl.BlockSpec((TM, D_MODEL), lambda i: (jnp.minimum(i, n_lat - 1), 0)),
                  pl.BlockSpec((TM, D_MODEL), lambda i: (jnp.maximum(i - n_lat, 0), 0)),
                  pl.BlockSpec((1, D_MODEL), lambda i: (0, 0)),
                  pl.BlockSpec((None, 1, D_MODEL), lambda i: (samp(i), 0, 0)),
                  pl.BlockSpec((None, 1, D_MODEL), lambda i: (samp(i), 0, 0))],
        out_specs=(pl.BlockSpec((TM, D_MODEL), lambda i: (i, 0)),
                   pl.BlockSpec((D_MODEL, TM), lambda i: (0, i))),
        out_shape=(SDS((rows, D_MODEL), BF), SDS((D_MODEL, rows), BF)),
        compiler_params=_cp(("parallel",), 40))(x_lat, x_ctx, norm_w, scale3, shift3)


def _norm_bwd(x_lat, x_ctx, dhx, gx_res, norm_w, scale3, tiles_per_sample, n_samp):
    rows = x_lat.shape[0] + x_ctx.shape[0]
    n_lat = tiles_per_sample * n_samp

    def samp(i):
        return jnp.minimum(i // tiles_per_sample, n_samp)

    def lat(i):
        return jnp.minimum(i, n_lat - 1)

    def body(x_ref, c_ref, dh_ref, gr_ref, nw_ref, sc_ref, gx_ref, dsh_ref, dsc_ref, dnw_ref):
        i = pl.program_id(0)
        x = jnp.where(i < n_lat, x_ref[...], c_ref[...])
        r = _rms(x)
        xh = x * r
        nw = nw_ref[...]
        dh = dh_ref[...]
        first = jnp.logical_or(i % tiles_per_sample == 0, i >= n_lat)

        @pl.when(first)
        def _():
            dsh_ref[...] = jnp.zeros_like(dsh_ref)
            dsc_ref[...] = jnp.zeros_like(dsc_ref)

        @pl.when(i == 0)
        def _():
            dnw_ref[...] = jnp.zeros_like(dnw_ref)

        dsh_ref[...] += jnp.sum(dh, axis=0, keepdims=True)
        dsc_ref[...] += jnp.sum(dh * (xh * nw), axis=0, keepdims=True)
        du = dh * (1.0 + sc_ref[...])
        dnw_ref[...] += jnp.sum(du * xh, axis=0, keepdims=True)

        @pl.when(i < n_lat)
        def _():
            gx_ref[...] = gr_ref[...] + _rms_bwd(du * nw, xh, r)

    return pl.pallas_call(
        body, name="norm_bwd", grid=(rows // TM,),
        in_specs=[pl.BlockSpec((TM, D_MODEL), lambda i: (lat(i), 0)),
                  pl.BlockSpec((TM, D_MODEL), lambda i: (jnp.maximum(i - n_lat, 0), 0)),
                  pl.BlockSpec((TM, D_MODEL), lambda i: (i, 0)),
                  pl.BlockSpec((TM, D_MODEL), lambda i: (lat(i), 0)),
                  pl.BlockSpec((1, D_MODEL), lambda i: (0, 0)),
                  pl.BlockSpec((None, 1, D_MODEL), lambda i: (samp(i), 0, 0))],
        out_specs=(pl.BlockSpec((TM, D_MODEL), lambda i: (lat(i), 0)),
                   pl.BlockSpec((None, 1, D_MODEL), lambda i: (samp(i), 0, 0)),
                   pl.BlockSpec((None, 1, D_MODEL), lambda i: (samp(i), 0, 0)),
                   pl.BlockSpec((1, D_MODEL), lambda i: (0, 0))),
        out_shape=(SDS((n_lat * TM, D_MODEL), F32), SDS((n_samp + 1, 1, D_MODEL), F32),
                   SDS((n_samp + 1, 1, D_MODEL), F32), SDS((1, D_MODEL), F32)),
        compiler_params=_cp(("arbitrary",), 40))(x_lat, x_ctx, dhx, gx_res, norm_w, scale3)


def _in_proj_gather(hx, w_buf, ids):
    rows = hx.shape[0]
    tb = _big_rows(rows)
    n_i = rows // tb
    hrows = D_MODEL // 2

    def body(ids_ref, h_ref, w_in_hbm, px_ref, w_hbm, wv, lsem, ssem, rsem):
        j, i = pl.program_id(0), pl.program_id(1)
        x, y, c, chips = _place()
        sibling = (x, y, 1 - c)

        def half(s, which):
            return w_hbm.at[s, pl.ds(which * hrows, hrows), :]

        def over_ici(rel):
            chip = chips[rel]
            mine, theirs = half(2 * x + y, c), half(2 * chip[0] + chip[1], c)
            return (_remote(mine, mine, ssem.at[rel], rsem.at[rel], (*chip, c)),
                    _remote(theirs, theirs, ssem.at[rel], rsem.at[rel], (*chip, c)))

        def over_d2d(rel):
            s = 2 * chips[rel][0] + chips[rel][1]
            return (_remote(half(s, c), half(s, c), ssem.at[3 + rel], rsem.at[3 + rel], sibling),
                    _remote(half(s, 1 - c), half(s, 1 - c), ssem.at[3 + rel], rsem.at[3 + rel], sibling))

        first_row_tile = i == 0

        @pl.when(jnp.logical_and(j == 0, first_row_tile))
        def _():
            over_ici(0)[0].start()
            over_ici(1)[0].start()

        @pl.when(jnp.logical_and(j == 1, first_row_tile))
        def _():
            for rel in range(2):
                over_ici(rel)[1].wait_recv()
                over_d2d(rel)[0].start()
            over_ici(2)[0].start()
            over_d2d(0)[1].wait_recv()

        @pl.when(jnp.logical_and(j == 2, first_row_tile))
        def _():
            over_d2d(1)[1].wait_recv()

        @pl.when(jnp.logical_and(j == 3, first_row_tile))
        def _():
            over_ici(2)[1].wait_recv()
            passed, landing = over_d2d(2)
            passed.start()
            landing.wait_recv()

        @pl.when(first_row_tile)
        def _():
            cp = pltpu.make_async_copy(w_hbm.at[ids_ref[4 + j]], wv, lsem)
            cp.start()
            cp.wait()

        px_ref[...] = jnp.dot(h_ref[...], wv[...], preferred_element_type=F32).astype(BF)

        @pl.when(jnp.logical_and(j == N_SHARD - 1, i == n_i - 1))
        def _():
            for rel in range(3):
                over_ici(rel)[0].wait_send()
                over_d2d(rel)[0].wait_send()

    return pl.pallas_call(
        body, name="in_proj_gather", input_output_aliases={2: 1},
        grid_spec=pltpu.PrefetchScalarGridSpec(
            num_scalar_prefetch=1, grid=(N_SHARD, n_i),
            in_specs=[pl.BlockSpec((tb, D_MODEL), lambda j, i, ids_ref: (i, 0)), ANY],
            out_specs=(pl.BlockSpec((tb, IN_W), lambda j, i, ids_ref: (i, ids_ref[4 + j])), ANY),
            scratch_shapes=[pltpu.VMEM((D_MODEL, IN_W), BF), pltpu.SemaphoreType.DMA,
                            pltpu.SemaphoreType.DMA((6,)), pltpu.SemaphoreType.DMA((6,))]),
        out_shape=(SDS((rows, IN_COLS), BF), SDS(w_buf.shape, w_buf.dtype)),
        compiler_params=_cp(("arbitrary", "arbitrary"), 56))(ids, hx, w_buf)


def _gw_in(hxt, dpx_all):
    rows = dpx_all.shape[0]
    tb = _big_rows(rows)

    def body(h_ref, d_ref, o_ref):
        @pl.when(pl.program_id(1) == 0)
        def _():
            o_ref[...] = jnp.zeros_like(o_ref)

        o_ref[...] += jnp.dot(h_ref[...], d_ref[...], preferred_element_type=F32)

    return pl.pallas_call(
        body, name="gw_in", grid=(N_IN_BLK, rows // tb),
        in_specs=[pl.BlockSpec((D_MODEL, tb), lambda j, i: (0, i)),
                  pl.BlockSpec((tb, IN_BLK), lambda j, i: (i, j))],
        out_specs=pl.BlockSpec((None, D_MODEL, IN_BLK), lambda j, i: (j // BPS, 0, j % BPS)),
        out_shape=SDS((N_SHARD, D_MODEL, IN_W), F32),
        compiler_params=_cp(("arbitrary", "arbitrary"), 56))(hxt, dpx_all)


def _dhx(dpx_all, w_in_g, tile0, n_tiles, dhx, comm):
    rows = dpx_all.shape[0]
    tb = _big_rows(rows)

    def body(d_ref, w_ref, *rest):
        o_ref = rest[-1]

        @pl.when(pl.program_id(1) == 0)
        def _():
            o_ref[...] = jnp.zeros_like(o_ref)

        o_ref[...] += lax.dot_general(d_ref[...], w_ref[...], (((1,), (1,)), ((), ())), preferred_element_type=F32)

    args, in_specs, aliases = [dpx_all, w_in_g], [
        pl.BlockSpec((tb, IN_BLK), lambda i, j: (tile0 + i, j)),
        pl.BlockSpec((None, D_MODEL, IN_BLK), lambda i, j: (j // BPS, 0, j % BPS))], None
    if dhx is not None:
        args, in_specs, aliases = args + [dhx], in_specs + [ANY], {2: 0}
    (out,), got = _call(body, args, comm, name="dhx", grid=(n_tiles, N_IN_BLK), in_specs=in_specs,
                        out_specs=(pl.BlockSpec((tb, D_MODEL), lambda i, j: (tile0 + i, 0)),),
                        out_shape=(SDS((rows, D_MODEL), F32),), aliases=aliases,
                        compiler_params=_cp(("arbitrary", "arbitrary"), 56))
    return out, got


def _decays(lgv, d):
    c = RET_CHUNK
    ii = lax.broadcasted_iota(jnp.int32, (c, 1), 0).astype(F32)
    jj = lax.broadcasted_iota(jnp.int32, (1, c), 1).astype(F32)
    a_i = jnp.where(d == 0, ii, c - 1.0 - ii)
    a_j = jnp.where(d == 0, jj, c - 1.0 - jj)
    rel = a_i - a_j
    mask = jnp.where(rel >= 0, jnp.exp(lgv * jnp.maximum(rel, 0.0)), 0.0)
    qd = jnp.exp(lgv * (a_i + 1.0))
    kd = jnp.exp(lgv * (c - 1.0 - a_i))
    gc = jnp.exp(jnp.full((1, 1), lgv * c, F32))
    return a_i, rel, mask, qd, kd, gc


def _ctx_state_fwd(px, lg, n_samp, t_lat, lc):
    rb = t_lat // lc

    def body(lg_ref, k_ref, v_ref, o_ref):
        h = pl.program_id(1)
        k = k_ref[...].astype(F32) * (RET_DK ** -0.5)
        v = v_ref[...]
        pos = lax.broadcasted_iota(jnp.int32, (lc, 1), 0).astype(F32)
        o_ref[0] = _dot(k * jnp.exp(lg_ref[0, h] * (lc - 1.0 - pos)), v, 0, 0)
        o_ref[1] = _dot(k * jnp.exp(lg_ref[1, h] * pos), v, 0, 0)

    return pl.pallas_call(
        body, name="ctx_state_fwd", grid=(n_samp, RET_HEADS),
        in_specs=[SMEM,
                  pl.BlockSpec((lc, RET_DK), lambda b, h: (rb + b, C_RK // RET_DK + h)),
                  pl.BlockSpec((lc, RET_DV), lambda b, h: (rb + b, C_RV // RET_DV + h))],
        out_specs=pl.BlockSpec((None, 2, None, RET_DK, RET_DV), lambda b, h: (b, 0, h, 0, 0)),
        out_shape=SDS((n_samp, 2, RET_HEADS, RET_DK, RET_DV), F32),
        compiler_params=_cp(("parallel", "parallel")))(lg, px, px)


def _ctx_state_bwd(dpx, px, dstates, lg, n_samp, t_lat, lc):
    rb = t_lat // lc
    kspec = pl.BlockSpec((lc, RET_DK), lambda b, h: (rb + b, C_RK // RET_DK + h))
    vspec = pl.BlockSpec((lc, RET_DV), lambda b, h: (rb + b, C_RV // RET_DV + h))
    sspec = pl.BlockSpec((None, 2, None, RET_DK, RET_DV), lambda b, h: (b, 0, h, 0, 0))

    def weights(lg_ref, h):
        pos = lax.broadcasted_iota(jnp.int32, (lc, 1), 0).astype(F32)
        e_f = lc - 1.0 - pos
        return pos, e_f, jnp.exp(lg_ref[0, h] * e_f), jnp.exp(lg_ref[1, h] * pos)

    def k_body(lg_ref, dpx_hbm, k_ref, v_ref, ds_ref, dk_ref, dlg_ref):
        pos, e_f, w_f, w_b = weights(lg_ref, pl.program_id(1))
        k = k_ref[...].astype(F32) * (RET_DK ** -0.5)
        y_f = _dot(v_ref[...], ds_ref[0], 1, 1) * w_f
        y_b = _dot(v_ref[...], ds_ref[1], 1, 1) * w_b
        dk_ref[...] = ((y_f + y_b) * (RET_DK ** -0.5)).astype(BF)
        t_f = _sum_all(e_f * k * y_f)
        t_b = _sum_all(pos * k * y_b)
        sub = lax.broadcasted_iota(jnp.int32, (8, 128), 0)
        dlg_ref[...] = jnp.where(sub == 0, t_f, jnp.where(sub == 1, t_b, 0.0))

    def v_body(lg_ref, dpx_hbm, k_ref, ds_ref, dv_ref):
        _, _, w_f, w_b = weights(lg_ref, pl.program_id(1))
        k = k_ref[...].astype(F32) * (RET_DK ** -0.5)
        dv_ref[...] = (_dot(k * w_f, ds_ref[0]) + _dot(k * w_b, ds_ref[1])).astype(BF)

    dpx, dlg = pl.pallas_call(
        k_body, name="ctx_state_bwd_k", grid=(n_samp, RET_HEADS), input_output_aliases={1: 0},
        in_specs=[SMEM, ANY, kspec, vspec, sspec],
        out_specs=(kspec, pl.BlockSpec((None, None, 8, 128), lambda b, h: (b, h, 0, 0))),
        out_shape=(SDS(dpx.shape, dpx.dtype), SDS((n_samp, RET_HEADS, 8, 128), F32)),
        compiler_params=_cp(("parallel", "parallel")))(lg, dpx, px, px, dstates)
    dpx = pl.pallas_call(
        v_body, name="ctx_state_bwd_v", grid=(n_samp, RET_HEADS), input_output_aliases={1: 0},
        in_specs=[SMEM, ANY, kspec, sspec], out_specs=vspec, out_shape=SDS(dpx.shape, dpx.dtype),
        compiler_params=_cp(("parallel", "parallel")))(lg, dpx, px, dstates)
    return dpx, dlg


def _zero_ctx_tail(dpx, t_lat):
    wb = 512
    n_ctx = (dpx.shape[0] - t_lat) // TM

    def body(dpx_hbm, o_ref):
        o_ref[...] = jnp.zeros_like(o_ref)

    return pl.pallas_call(
        body, name="zero_ctx_tail", grid=(n_ctx, (IN_COLS - KV_COLS) // wb), input_output_aliases={0: 0},
        in_specs=[ANY], out_specs=pl.BlockSpec((TM, wb), lambda i, j: (t_lat // TM + i, KV_COLS // wb + j)),
        out_shape=SDS(dpx.shape, dpx.dtype),
        compiler_params=_cp(("parallel", "parallel")))(dpx)


def _ret_specs(row_f, row_b):
    c = RET_CHUNK
    wq = RET_HEADS * RET_DK // 2
    wv = RET_HEADS * RET_DV // 2
    specs = []
    for row in (row_f, row_b):
        specs += [pl.BlockSpec((c, wq), lambda b, n, row=row: (row(b, n), C_RQ // wq)),
                  pl.BlockSpec((c, wq), lambda b, n, row=row: (row(b, n), C_RQ // wq + 1)),
                  pl.BlockSpec((c, 2 * wq), lambda b, n, row=row: (row(b, n), C_RK // (2 * wq))),
                  pl.BlockSpec((c, wv), lambda b, n, row=row: (row(b, n), C_RV // wv)),
                  pl.BlockSpec((c, wv), lambda b, n, row=row: (row(b, n), C_RV // wv + 1))]
    return specs


def _ret_head(refs, h):
    q0, q1, k_ref, v0, v1 = refs
    hh = h % 2
    q = (q0, q1)[h // 2][:, hh * RET_DK:(hh + 1) * RET_DK].astype(F32)
    k = k_ref[:, h * RET_DK:(h + 1) * RET_DK].astype(F32) * (RET_DK ** -0.5)
    v = (v0, v1)[h // 2][:, hh * RET_DV:(hh + 1) * RET_DV]
    return q, k, v


def _ret_fwd(px, states0, lg, n_samp, seq, comm):
    c = RET_CHUNK
    nc = seq // c
    t_lat = n_samp * seq
    wo = RET_HEADS * RET_DV

    def row_f(b, n):
        return b * nc + n

    def row_b(b, n):
        return b * nc + nc - 1 - n

    def body(lg_ref, *refs):
        ins, (s0_ref, of_ref, ob_ref, st_ref, s_s) = refs[:10], refs[10:]

        @pl.when(pl.program_id(1) == 0)
        def _():
            s_s[...] = s0_ref[...]

        for d, o_ref in ((0, of_ref), (1, ob_ref)):
            for h in range(RET_HEADS):
                _, _, mask, qd, kd, gc = _decays(lg_ref[d, h], d)
                q, k, v = _ret_head(ins[5 * d:5 * d + 5], h)
                s = s_s[d, h]
                st_ref[h, d] = s.astype(BF)
                sc = _dot(q, k, 1, 1) * mask
                o_ref[:, h * RET_DV:(h + 1) * RET_DV] = (_dot(sc, v) + _dot(q * qd, s)).astype(BF)
                s_s[d, h] = s * gc + _dot(k * kd, v, 0, 0)

    return _call(
        body, [lg] + [px] * 10 + [states0], comm, name="ret_fwd", grid=(n_samp, nc),
        in_specs=[SMEM] + _ret_specs(row_f, row_b) + [
            pl.BlockSpec((None, 2, RET_HEADS, RET_DK, RET_DV), lambda b, n: (b, 0, 0, 0, 0))],
        out_specs=(pl.BlockSpec((c, wo), lambda b, n: (row_f(b, n), 0)),
                   pl.BlockSpec((c, wo), lambda b, n: (row_b(b, n), 0)),
                   pl.BlockSpec((None, RET_HEADS, 2, None, RET_DK, RET_DV), lambda b, n: (b, 0, 0, n, 0, 0))),
        out_shape=(SDS((t_lat, wo), BF), SDS((t_lat, wo), BF),
                   SDS((n_samp, RET_HEADS, 2, nc, RET_DK, RET_DV), BF)),
        scratch_shapes=[pltpu.VMEM((2, RET_HEADS, RET_DK, RET_DV), F32)],
        compiler_params=_cp(("arbitrary", "arbitrary"), 48))


def _ret_bwd(dpx, px, do, saved, lg, n_samp, seq, comm):
    c = RET_CHUNK
    nc = seq // c
    assert nc % 2 == 0
    wq, wo = RET_HEADS * RET_DK, RET_HEADS * RET_DV

    def row_f(b, n):
        return b * nc + nc - 1 - n

    def row_b(b, n):
        return b * nc + n

    def body(lg_ref, *refs):
        ins = refs[:10]
        (dof_ref, dob_ref, st_ref, dpx_in, dpx_hbm, ds0_ref, dlg_ref,
         ds_s, acc_s, sq_s, sk_s, sv_s, sems) = refs[10:]
        b, n = pl.program_id(0), pl.program_id(1)
        second = n >= nc // 2
        chunks = (nc - 1 - n, n)

        def parked(ch):
            return pl.ds(pl.multiple_of(ch * c, c), c)

        def flush():
            cps = []
            for d, ch in enumerate(chunks):
                rows = pl.ds(pl.multiple_of((b * nc + ch) * c, c), c)
                cps += [pltpu.make_async_copy(sq_s.at[parked(ch), :], dpx_hbm.at[rows, pl.ds(C_RQ, wq)], sems.at[3 * d]),
                        pltpu.make_async_copy(sk_s.at[parked(ch), :], dpx_hbm.at[rows, pl.ds(C_RK, wq)],
                                              sems.at[3 * d + 1]),
                        pltpu.make_async_copy(sv_s.at[parked(ch), :], dpx_hbm.at[rows, pl.ds(C_RV, wo)],
                                              sems.at[3 * d + 2])]
            return cps

        @pl.when(jnp.logical_or(n > nc // 2, jnp.logical_and(n == 0, b > 0)))
        def _():
            for cp in flush():
                cp.wait()

        @pl.when(n == 0)
        def _():
            ds_s[...] = jnp.zeros_like(ds_s)
            acc_s[...] = jnp.zeros_like(acc_s)

        def chains(first_visit):
            for d, do_ref in enumerate((dof_ref, dob_ref)):
                rows = parked(chunks[d])
                for h in range(RET_HEADS):
                    a_i, rel, mask, qd, kd, gc = _decays(lg_ref[d, h], d)
                    q, k, v = _ret_head(ins[5 * d:5 * d + 5], h)
                    qb, kb, vb = q.astype(BF), k.astype(BF), v.astype(BF)
                    cq, cv = slice(h * RET_DK, (h + 1) * RET_DK), slice(h * RET_DV, (h + 1) * RET_DV)
                    dob = do_ref[:, cv].astype(BF)
                    sb = st_ref[h, d]
                    ds = ds_s[d, h]
                    dsb = ds.astype(BF)
                    raw = _dot(qb, kb, 1, 1)
                    sc = raw * mask
                    dsc = _dot(dob, vb, 1, 1) * mask
                    dscb = dsc.astype(BF)
                    x = _dot(dob, sb, 1, 1)
                    y = _dot(vb, dsb, 1, 1)
                    qq = q * qd
                    kk = k * kd
                    dq = _dot(dscb, kb) + x * qd
                    dk = _dot(dscb, qb, 0, 0) + y * kd
                    dv = _dot(sc, dob, 0, 0) + _dot(kk, dsb)
                    if first_visit:
                        sq_s[rows, cq] = dq.astype(BF)
                        sk_s[rows, cq] = dk.astype(BF)
                        sv_s[rows, cv] = dv.astype(BF)
                    else:
                        sq_s[rows, cq] = (sq_s[rows, cq].astype(F32) + dq).astype(BF)
                        sk_s[rows, cq] = ((sk_s[rows, cq].astype(F32) + dk) * (RET_DK ** -0.5)).astype(BF)
                        sv_s[rows, cv] = (sv_s[rows, cv].astype(F32) + dv).astype(BF)
                    t = (_sum_all(dsc * raw * rel) + _sum_all((a_i + 1.0) * qq * x)
                         + _sum_all((c - 1.0 - a_i) * kk * y) + c * gc * _sum_all(ds * sb.astype(F32)))
                    acc_s[4 * d + h:4 * d + h + 1, :] += t
                    ds_s[d, h] = ds * gc + _dot(qq, dob, 0, 0)

        @pl.when(jnp.logical_not(second))
        def _():
            chains(True)

        @pl.when(second)
        def _():
            chains(False)
            for cp in flush():
                cp.start()

        @pl.when(n == nc - 1)
        def _():
            ds0_ref[...] = ds_s[...]
            dlg_ref[...] = acc_s[...]

        @pl.when(jnp.logical_and(b == n_samp - 1, n == nc - 1))
        def _():
            for cp in flush():
                cp.wait()

    do_spec_f = pl.BlockSpec((c, wo), lambda b, n: (row_f(b, n), 0))
    do_spec_b = pl.BlockSpec((c, wo), lambda b, n: (row_b(b, n), 0))
    return _call(
        body, [lg] + [px] * 10 + [do, do, saved, dpx], comm, name="ret_bwd", grid=(n_samp, nc), aliases={14: 0},
        in_specs=[SMEM] + _ret_specs(row_f, row_b) + [
            do_spec_f, do_spec_b,
            pl.BlockSpec((None, RET_HEADS, 2, None, RET_DK, RET_DV), lambda b, n: (b, 0, 0, nc - 1 - n, 0, 0)),
            ANY],
        out_specs=(ANY,
                   pl.BlockSpec((None, 2, RET_HEADS, RET_DK, RET_DV), lambda b, n: (b, 0, 0, 0, 0)),
                   pl.BlockSpec((None, 8, 128), lambda b, n: (b, 0, 0))),
        out_shape=(SDS(dpx.shape, dpx.dtype),
                   SDS((n_samp, 2, RET_HEADS, RET_DK, RET_DV), F32), SDS((n_samp, 8, 128), F32)),
        scratch_shapes=[pltpu.VMEM((2, RET_HEADS, RET_DK, RET_DV), F32), pltpu.VMEM((8, 128), F32),
                        pltpu.VMEM((seq, wq), BF), pltpu.VMEM((seq, wq), BF), pltpu.VMEM((seq, wo), BF),
                        pltpu.SemaphoreType.DMA((6,))],
        compiler_params=_cp(("arbitrary", "arbitrary"), 60))


def _norm_rope(x, w, cos, sin):
    xn = x * _rms(x) * w
    return xn * cos + _swap_pairs(xn) * sin


def _norm_rope_bwd(dy, x, w, cos, sin):
    dxn = dy * cos + _swap_pairs(dy * sin)
    r = _rms(x)
    xh = x * r
    return _rms_bwd(dxn * w, xh, r), jnp.sum(dxn * xh, axis=0, keepdims=True)


def _att_prep_q(px, cos_all, sin_all, qnw, t_lat):
    hd = ATT_HEAD_DIM
    wblk = ATT_REP * hd

    def body(x_ref, cos_ref, sin_ref, w_ref, o_ref):
        for r in range(ATT_REP):
            cols = slice(r * hd, (r + 1) * hd)
            qr = _norm_rope(x_ref[:, cols].astype(F32), w_ref[...], cos_ref[...], sin_ref[...])
            o_ref[:, cols] = (qr * (hd ** -0.5)).astype(BF)

    return pl.pallas_call(
        body, name="att_prep_q", grid=(t_lat // TM, ATT_KV_HEADS),
        in_specs=[pl.BlockSpec((TM, wblk), lambda i, g: (i, C_AQ // wblk + g)),
                  pl.BlockSpec((TM, hd), lambda i, g: (i, 0)),
                  pl.BlockSpec((TM, hd), lambda i, g: (i, 0)),
                  pl.BlockSpec((1, hd), lambda i, g: (0, 0))],
        out_specs=pl.BlockSpec((TM, wblk), lambda i, g: (i, g)),
        out_shape=SDS((t_lat, ATT_HEADS * hd), BF),
        compiler_params=_cp(("parallel", "parallel")))(px, cos_all, sin_all, qnw)


def _att_prep_kv(px, cos_all, sin_all, knw):
    rows = px.shape[0]
    hd = ATT_HEAD_DIM
    kvw = ATT_KV_HEADS * hd

    def body(x_ref, cos_ref, sin_ref, w_ref, k_ref, v_ref):
        for g in range(ATT_KV_HEADS):
            cols = slice(g * hd, (g + 1) * hd)
            k_ref[:, cols] = _norm_rope(x_ref[:, cols].astype(F32), w_ref[...], cos_ref[...],
                                        sin_ref[...]).astype(BF)
            v_ref[:, 2 * g * hd:(2 * g + 1) * hd] = x_ref[:, kvw + g * hd:kvw + (g + 1) * hd].astype(BF)
            v_ref[:, (2 * g + 1) * hd:(2 * g + 2) * hd] = jnp.ones((TM, hd), BF)

    return pl.pallas_call(
        body, name="att_prep_kv", grid=(rows // TM,),
        in_specs=[pl.BlockSpec((TM, 2 * kvw), lambda i: (i, C_AK // (2 * kvw))),
                  pl.BlockSpec((TM, hd), lambda i: (i, 0)),
                  pl.BlockSpec((TM, hd), lambda i: (i, 0)),
                  pl.BlockSpec((1, hd), lambda i: (0, 0))],
        out_specs=(pl.BlockSpec((TM, kvw), lambda i: (i, 0)), pl.BlockSpec((TM, 2 * kvw), lambda i: (i, 0))),
        out_shape=(SDS((rows, kvw), BF), SDS((rows, 2 * kvw), BF)),
        compiler_params=_cp(("parallel",)))(px, cos_all, sin_all, knw)


def _att_kv_bwd(dpx, dkl, dkc, dvl, dvc, px, cos_all, sin_all, knw):
    rows = px.shape[0]
    hd = ATT_HEAD_DIM
    kvw = ATT_KV_HEADS * hd
    n_lat = dkl.shape[0] // TM
    assert dkc.shape[0] == TM

    def body(dpx_hbm, dkl_ref, dkc_ref, dvl_ref, dvc_ref, x_ref, cos_ref, sin_ref, w_ref, o_ref, gw_ref):
        i = pl.program_id(0)

        @pl.when(i == 0)
        def _():
            gw_ref[...] = jnp.zeros_like(gw_ref)

        is_lat = i < n_lat
        dk = jnp.where(is_lat, dkl_ref[...], dkc_ref[...])
        dv = jnp.where(is_lat, dvl_ref[...], dvc_ref[...])
        for g in range(ATT_KV_HEADS):
            cols = slice(g * hd, (g + 1) * hd)
            dx, gw = _norm_rope_bwd(dk[:, cols], x_ref[:, cols].astype(F32), w_ref[...], cos_ref[...], sin_ref[...])
            o_ref[:, cols] = dx.astype(BF)
            gw_ref[...] += gw
        o_ref[:, kvw:] = dv.astype(BF)

    lat = pl.BlockSpec((TM, kvw), lambda i: (jnp.minimum(i, n_lat - 1), 0))
    ctx = pl.BlockSpec((TM, kvw), lambda i: (0, 0))
    kvcol = pl.BlockSpec((TM, 2 * kvw), lambda i: (i, C_AK // (2 * kvw)))
    return pl.pallas_call(
        body, name="att_kv_bwd", grid=(rows // TM,), input_output_aliases={0: 0},
        in_specs=[ANY, lat, ctx, lat, ctx, kvcol,
                  pl.BlockSpec((TM, hd), lambda i: (i, 0)),
                  pl.BlockSpec((TM, hd), lambda i: (i, 0)),
                  pl.BlockSpec((1, hd), lambda i: (0, 0))],
        out_specs=(kvcol, pl.BlockSpec((1, hd), lambda i: (0, 0))),
        out_shape=(SDS(dpx.shape, dpx.dtype), SDS((1, hd), F32)),
        compiler_params=_cp(("arbitrary",)))(dpx, dkl, dkc, dvl, dvc, px, cos_all, sin_all, knw)


def _stack_heads(ref_or_val):
    hd = ATT_HEAD_DIM
    return jnp.concatenate([ref_or_val[:, r * hd:(r + 1) * hd] for r in range(ATT_REP)], axis=0)


def _att_scores(q, kl, kc):
    sl = _dot(q, kl, 1, 1)
    sc = _dot(q, kc, 1, 1)
    m = jnp.maximum(jnp.max(sl, axis=-1, keepdims=True), jnp.max(sc, axis=-1, keepdims=True))
    return jnp.exp(sl - m), jnp.exp(sc - m), m


def _att_fwd(qn, kn, vn, n_samp, seq, lc):
    hd = ATT_HEAD_DIM
    tq = ATT_TQ
    nq = seq // tq
    wblk = ATT_REP * hd
    cb = n_samp * seq // lc
    t_lat = n_samp * seq

    def body(q_ref, kl_ref, kc_ref, vl_ref, vc_ref, o_ref, lse_ref):
        lane = lax.broadcasted_iota(jnp.int32, (tq, hd), 1)
        lse = jnp.zeros((tq, hd), F32)
        for r in range(ATT_REP):
            cols = slice(r * hd, (r + 1) * hd)
            el, ec, m = _att_scores(q_ref[:, cols], kl_ref[...], kc_ref[...])
            pv = _dot(el, vl_ref[...]) + _dot(ec, vc_ref[...])
            denom = pv[:, hd:hd + 1]
            o_ref[:, cols] = (pv[:, :hd] / denom).astype(BF)
            lse = jnp.where(lane == r, m + jnp.log(denom), lse)
        lse_ref[...] = lse

    return pl.pallas_call(
        body, name="att_fwd", grid=(n_samp, ATT_KV_HEADS, nq),
        in_specs=[pl.BlockSpec((tq, wblk), lambda b, g, i: (b * nq + i, g)),
                  pl.BlockSpec((seq, hd), lambda b, g, i: (b, g)),
                  pl.BlockSpec((lc, hd), lambda b, g, i: (cb + b, g)),
                  pl.BlockSpec((seq, 2 * hd), lambda b, g, i: (b, g)),
                  pl.BlockSpec((lc, 2 * hd), lambda b, g, i: (cb + b, g))],
        out_specs=(pl.BlockSpec((tq, wblk), lambda b, g, i: (b * nq + i, g)),
                   pl.BlockSpec((tq, hd), lambda b, g, i: (b * nq + i, g))),
        out_shape=(SDS((t_lat, ATT_HEADS * hd), BF), SDS((t_lat, ATT_KV_HEADS * hd), F32)),
        compiler_params=_cp(("parallel", "parallel", "parallel"), 48))(qn, kn, kn, vn, vn)


def _att_bwd(dpx, qn, kn, vn, px, o_att, lse, do_att, cos_all, sin_all, qnw, n_samp, seq, lc, comm):
    hd = ATT_HEAD_DIM
    tq = ATT_TQ
    nq = seq // tq
    wblk = ATT_REP * hd
    cb = n_samp * seq // lc
    t_lat = n_samp * seq
    kvw = ATT_KV_HEADS * hd
    scale = hd ** -0.5

    def body(dpx_hbm, q_ref, kl_ref, kc_ref, vl_ref, vc_ref, o_ref, do_ref, x_ref, cos_ref, sin_ref, w_ref,
             lse_ref, dq_ref, dkl_ref, dkc_ref, dvl_ref, dvc_ref, gw_ref, akl, akc, avl, avc, aw):
        i = pl.program_id(2)

        @pl.when(i == 0)
        def _():
            akl[...] = jnp.zeros_like(akl)
            akc[...] = jnp.zeros_like(akc)
            avl[...] = jnp.zeros_like(avl)
            avc[...] = jnp.zeros_like(avc)
            aw[...] = jnp.zeros_like(aw)

        dobs, pls, pcs, dsls, dscs = [], [], [], [], []
        for r in range(ATT_REP):
            cols = slice(r * hd, (r + 1) * hd)
            dob = do_ref[:, cols]
            delta = jnp.sum(dob.astype(F32) * o_ref[:, cols].astype(F32), axis=-1, keepdims=True)
            lse = lse_ref[:, r:r + 1]
            p_l = jnp.exp(_dot(q_ref[:, cols], kl_ref[...], 1, 1) - lse).astype(BF)
            p_c = jnp.exp(_dot(q_ref[:, cols], kc_ref[...], 1, 1) - lse).astype(BF)
            ds_l = (p_l * (_dot(dob, vl_ref[...], 1, 1) - delta)).astype(BF)
            ds_c = (p_c * (_dot(dob, vc_ref[...], 1, 1) - delta)).astype(BF)
            dq = (_dot(ds_l, kl_ref[...]) + _dot(ds_c, kc_ref[...])) * scale
            dx, gw = _norm_rope_bwd(dq, x_ref[:, cols].astype(F32), w_ref[...], cos_ref[...], sin_ref[...])
            dq_ref[:, cols] = dx.astype(BF)
            aw[...] += gw
            dobs.append(dob)
            pls.append(p_l)
            pcs.append(p_c)
            dsls.append(ds_l)
            dscs.append(ds_c)
        do4 = jnp.concatenate(dobs, axis=0)
        q4 = _stack_heads(q_ref)
        avl[...] += _dot(jnp.concatenate(pls, axis=0), do4, 0, 0)
        avc[...] += _dot(jnp.concatenate(pcs, axis=0), do4, 0, 0)
        akl[...] += _dot(jnp.concatenate(dsls, axis=0), q4, 0, 0)
        akc[...] += _dot(jnp.concatenate(dscs, axis=0), q4, 0, 0)

        @pl.when(i == nq - 1)
        def _():
            dkl_ref[...] = akl[...]
            dkc_ref[...] = akc[...]
            dvl_ref[...] = avl[...]
            dvc_ref[...] = avc[...]
            gw_ref[...] = aw[...]

    return _call(
        body, [dpx, qn, kn, kn, vn, vn, o_att, do_att, px, cos_all, sin_all, qnw, lse], comm,
        name="att_bwd", grid=(n_samp, ATT_KV_HEADS, nq), aliases={0: 0},
        in_specs=[ANY,
                  pl.BlockSpec((tq, wblk), lambda b, g, i: (b * nq + i, g)),
                  pl.BlockSpec((seq, hd), lambda b, g, i: (b, g)),
                  pl.BlockSpec((lc, hd), lambda b, g, i: (cb + b, g)),
                  pl.BlockSpec((seq, hd), lambda b, g, i: (b, 2 * g)),
                  pl.BlockSpec((lc, hd), lambda b, g, i: (cb + b, 2 * g)),
                  pl.BlockSpec((tq, wblk), lambda b, g, i: (b * nq + i, g)),
                  pl.BlockSpec((tq, wblk), lambda b, g, i: (b * nq + i, g)),
                  pl.BlockSpec((tq, wblk), lambda b, g, i: (b * nq + i, C_AQ // wblk + g)),
                  pl.BlockSpec((tq, hd), lambda b, g, i: (b * nq + i, 0)),
                  pl.BlockSpec((tq, hd), lambda b, g, i: (b * nq + i, 0)),
                  pl.BlockSpec((1, hd), lambda b, g, i: (0, 0)),
                  pl.BlockSpec((tq, hd), lambda b, g, i: (b * nq + i, g))],
        out_specs=(pl.BlockSpec((tq, wblk), lambda b, g, i: (b * nq + i, C_AQ // wblk + g)),
                   pl.BlockSpec((seq, hd), lambda b, g, i: (b, g)),
                   pl.BlockSpec((lc, hd), lambda b, g, i: (b, g)),
                   pl.BlockSpec((seq, hd), lambda b, g, i: (b, g)),
                   pl.BlockSpec((lc, hd), lambda b, g, i: (b, g)),
                   pl.BlockSpec((None, None, 1, hd), lambda b, g, i: (b, g, 0, 0))),
        out_shape=(SDS(dpx.shape, dpx.dtype),
                   SDS((t_lat, kvw), F32), SDS((n_samp * lc, kvw), F32),
                   SDS((t_lat, kvw), F32), SDS((n_samp * lc, kvw), F32),
                   SDS((n_samp, ATT_KV_HEADS, 1, hd), F32)),
        scratch_shapes=[pltpu.VMEM((seq, hd), F32), pltpu.VMEM((lc, hd), F32),
                        pltpu.VMEM((seq, hd), F32), pltpu.VMEM((lc, hd), F32), pltpu.VMEM((1, hd), F32)],
        compiler_params=_cp(("arbitrary", "arbitrary", "arbitrary"), 56))


def _merge(x_lat, target, o_f, o_b, o_att, px, gate3, w_o_ret, w_o_att, w_out, tiles_per_sample):
    t_lat = x_lat.shape[0]
    tm = 256
    n_t = t_lat // tm
    per = tiles_per_sample * (TM // tm)
    d = D_MODEL
    rv = RET_HEADS * RET_DV
    n_samp = gate3.shape[0] - 1

    half = d // 2
    n_px = 10

    def body(x_ref, t_ref, of_ref, ob_ref, oa_ref, *rest):
        pxs, rest = rest[:n_px], rest[n_px:]
        (gt_ref, wor_ref, woa_ref, wout_ref,
         gx_ref, dor_ref, doa_ref, dpx_hbm, loss_ref, dgt_ref, gwor_hbm, gwoa_hbm, gwout_hbm,
         aor, aoa, aout, drg_ref, dtail_ref, sems) = rest
        i = pl.program_id(0)

        def copies(step):
            rows = pl.ds(pl.multiple_of(step * tm, tm), tm)
            return (pltpu.make_async_copy(drg_ref, dpx_hbm.at[rows, pl.ds(C_RG, rv)], sems.at[0]),
                    pltpu.make_async_copy(dtail_ref, dpx_hbm.at[rows, pl.ds(C_AG, 3 * d)], sems.at[1]))

        @pl.when(i == 0)
        def _():
            aor[...] = jnp.zeros_like(aor)
            aoa[...] = jnp.zeros_like(aoa)
            aout[...] = jnp.zeros_like(aout)
            loss_ref[...] = jnp.zeros_like(loss_ref)

        @pl.when(i % per == 0)
        def _():
            dgt_ref[...] = jnp.zeros_like(dgt_ref)

        def cat(refs):
            return jnp.concatenate([r[...] for r in refs], axis=1).astype(F32)

        def ret_head(h):
            cols = slice(h * RET_DV, (h + 1) * RET_DV)
            o = of_ref[:, cols].astype(F32) + ob_ref[:, cols].astype(F32)
            r = _rms(o)
            g = pxs[h][...].astype(F32)
            return o * r, r, g, _sigmoid(g)

        def att_half(k):
            o = oa_ref[:, k * half:(k + 1) * half].astype(F32)
            g = pxs[4 + k][...].astype(F32)
            return o, g, _sigmoid(g)

        yrs = []
        for h in range(RET_HEADS):
            on, _, g, sg = ret_head(h)
            yrs.append((on * (g * sg)).astype(BF))
        yr = jnp.concatenate(yrs, axis=1)
        yas = []
        for k in range(2):
            o, g, sg = att_half(k)
            yas.append((o * (g * sg)).astype(BF))
        ya = jnp.concatenate(yas, axis=1)

        a = jnp.dot(yr, wor_ref[...], preferred_element_type=F32)
        b = jnp.dot(ya, woa_ref[...], preferred_element_type=F32)
        sr = _sigmoid(cat(pxs[6:8]))
        sa = _sigmoid(cat(pxs[8:10]))
        yb = (sr * a + sa * b).astype(BF)
        out = jnp.dot(yb, wout_ref[...], preferred_element_type=F32)
        gate = gt_ref[...]
        err = x_ref[...] + gate * out - t_ref[...]
        loss_ref[...] += 0.5 * _sum_all(err * err) * (1.0 / d)
        dy_tok = err * (1.0 / d)
        gx_ref[...] = dy_tok
        dgt_ref[...] += jnp.sum(dy_tok * out, axis=0, keepdims=True)
        dout = (dy_tok * gate).astype(BF)
        aout[...] += _dot(yb, dout, 0, 0)
        dyy = _dot(dout, wout_ref[...], 1, 1)
        da = (dyy * sr).astype(BF)
        db = (dyy * sa).astype(BF)
        aor[...] += _dot(yr, da, 0, 0)
        aoa[...] += _dot(ya, db, 0, 0)
        dyr = _dot(da, wor_ref[...], 1, 1)
        dya = _dot(db, woa_ref[...], 1, 1)

        @pl.when(i > 0)
        def _():
            for cp in copies(i - 1):
                cp.wait()

        dtail_ref[:, d:2 * d] = (dyy * a * (sr * (1.0 - sr))).astype(BF)
        dtail_ref[:, 2 * d:] = (dyy * b * (sa * (1.0 - sa))).astype(BF)
        for h in range(RET_HEADS):
            cols = slice(h * RET_DV, (h + 1) * RET_DV)
            on, r, g, sg = ret_head(h)
            dy = dyr[:, cols]
            drg_ref[:, cols] = (dy * on * (sg * (1.0 + g * (1.0 - sg)))).astype(BF)
            dor_ref[:, cols] = _rms_bwd(dy * (g * sg), on, r).astype(BF)
        for k in range(2):
            cols = slice(k * half, (k + 1) * half)
            o, g, sg = att_half(k)
            dy = dya[:, cols]
            dtail_ref[:, cols] = (dy * o * (sg * (1.0 + g * (1.0 - sg)))).astype(BF)
            doa_ref[:, cols] = (dy * (g * sg)).astype(BF)
        for cp in copies(i):
            cp.start()

        @pl.when(i == n_t - 1)
        def _():
            for cp in copies(i):
                cp.wait()
            pltpu.sync_copy(aor, gwor_hbm)
            pltpu.sync_copy(aoa, gwoa_hbm)
            pltpu.sync_copy(aout, gwout_hbm)

    def px_blk(col):
        return pl.BlockSpec((tm, half), lambda i: (i, col // half))

    def resident(shape):
        return pl.BlockSpec(shape, lambda i: (0, 0), pipeline_mode=pl.Buffered(1))

    px_cols = ([C_RG + k * half for k in range(4)] + [C_AG, C_AG + half]
               + [C_MR, C_MR + half, C_MA, C_MA + half])
    return pl.pallas_call(
        body, name="merge", grid=(n_t,),
        in_specs=[pl.BlockSpec((tm, d), lambda i: (i, 0)),
                  pl.BlockSpec((tm, d), lambda i: (i, 0)),
                  pl.BlockSpec((tm, rv), lambda i: (i, 0)),
                  pl.BlockSpec((tm, rv), lambda i: (i, 0)),
                  pl.BlockSpec((tm, d), lambda i: (i, 0))]
        + [px_blk(col) for col in px_cols]
        + [pl.BlockSpec((None, 1, d), lambda i: (i // per, 0, 0)),
           resident((rv, d)), resident((d, d)), resident((d, d))],
        out_specs=(pl.BlockSpec((tm, d), lambda i: (i, 0)),
                   pl.BlockSpec((tm, rv), lambda i: (i, 0)),
                   pl.BlockSpec((tm, d), lambda i: (i, 0)),
                   ANY,
                   pl.BlockSpec((8, 128), lambda i: (0, 0)),
                   pl.BlockSpec((None, 1, d), lambda i: (i // per, 0, 0)),
                   ANY, ANY, ANY),
        out_shape=(SDS((t_lat, d), F32), SDS((t_lat, rv), BF), SDS((t_lat, d), BF),
                   SDS((px.shape[0], IN_COLS), BF),
                   SDS((8, 128), F32), SDS((n_samp, 1, d), F32),
                   SDS((rv, d), F32), SDS((d, d), F32), SDS((d, d), F32)),
        scratch_shapes=[pltpu.VMEM((rv, d), F32), pltpu.VMEM((d, d), F32), pltpu.VMEM((d, d), F32),
                        pltpu.VMEM((tm, rv), BF), pltpu.VMEM((tm, 3 * d), BF), pltpu.SemaphoreType.DMA((2,))],
        compiler_params=_cp(("arbitrary",), 56))(
            x_lat, target, o_f, o_b, o_att, *([px] * n_px), gate3, w_o_ret, w_o_att, w_out)


def _place():
    x, y, c = lax.axis_index("x"), lax.axis_index("y"), lax.axis_index("c")
    chips = [(1 - x, y), (x, 1 - y), (1 - x, 1 - y)]
    return x, y, c, chips


def _remote(src, dst, send_sem, recv_sem, to):
    return pltpu.make_async_remote_copy(src_ref=src, dst_ref=dst, send_sem=send_sem, recv_sem=recv_sem,
                                        device_id=to, device_id_type=MESH)


def _place_ids():
    x, y, c = lax.axis_index("x"), lax.axis_index("y"), lax.axis_index("c")
    me = 2 * x + y
    return jnp.stack([x, y, c, me, me, 2 * (1 - x) + y, 2 * x + 1 - y, 2 * (1 - x) + 1 - y]).astype(jnp.int32)


def _ag_comm(bufs):
    n, m = len(bufs), 3

    def half(ref, s, which):
        h = ref.shape[1] // 2
        return ref.at[s, pl.ds(which * h, h), :]

    def ici(ins, outs, ssem, rsem, base):
        x, y, c, chips = _place()
        sends, recvs = [], []
        for a in range(n):
            for j in range(m):
                k, chip = base + a * m + j, chips[j]
                mine, theirs = half(outs[a], 2 * x + y, c), half(outs[a], 2 * chip[0] + chip[1], c)
                sends.append(_remote(mine, mine, ssem.at[k], rsem.at[k], (*chip, c)))
                recvs.append(_remote(theirs, theirs, ssem.at[k], rsem.at[k], (*chip, c)))
        return sends, recvs

    def d2d(ins, outs, ssem, rsem, base):
        x, y, c, chips = _place()
        sends, recvs = [], []
        for a in range(n):
            for j in range(m):
                k, s = base + (n + a) * m + j, 2 * chips[j][0] + chips[j][1]
                sends.append(_remote(half(outs[a], s, c), half(outs[a], s, c), ssem.at[k], rsem.at[k], (x, y, 1 - c)))
                recvs.append(_remote(half(outs[a], s, 1 - c), half(outs[a], s, 1 - c), ssem.at[k], rsem.at[k],
                                     (x, y, 1 - c)))
        return sends, recvs

    return _Comm("all_gather", tuple(bufs), tuple(SDS(b.shape, b.dtype) for b in bufs), {a: a for a in range(n)},
                 2 * n * m, (ici, d2d))


def _swap_comm(grads):
    n = len(grads)

    def phase(ins, outs, ssem, rsem, base):
        x, y, c, _ = _place()
        sends = []
        for a in range(n):
            h = ins[a].shape[1] // 2
            sends.append(_remote(ins[a].at[:, pl.ds((1 - c) * h, h), :], outs[a], ssem.at[base + a],
                                 rsem.at[base + a], (x, y, 1 - c)))
        return sends, sends

    return _Comm("swap_halves", tuple(grads),
                 tuple(SDS((g.shape[0], g.shape[1] // 2, g.shape[2]), g.dtype) for g in grads), {}, n, (phase,))


def _exchange_comm(parts):
    n = len(parts)

    def phase(ins, outs, ssem, rsem, base):
        x, y, c, chips = _place()
        sends = []
        for a in range(n):
            for j, chip in enumerate(chips):
                k = base + 3 * a + j
                sends.append(_remote(ins[a].at[2 * chip[0] + chip[1]], outs[a].at[j], ssem.at[k], rsem.at[k],
                                     (*chip, c)))
        return sends, sends

    return _Comm("exchange_shards", tuple(parts), tuple(SDS((3,) + p.shape[1:], p.dtype) for p in parts), {}, 3 * n,
                 (phase,))


def _join_comm(bufs):
    n = len(bufs)

    def phase(ins, outs, ssem, rsem, base):
        x, y, c, _ = _place()
        sends, recvs = [], []
        for a in range(n):
            h = outs[a].shape[0] // 2
            mine, other = outs[a].at[pl.ds(c * h, h), :], outs[a].at[pl.ds((1 - c) * h, h), :]
            sends.append(_remote(mine, mine, ssem.at[base + a], rsem.at[base + a], (x, y, 1 - c)))
            recvs.append(_remote(other, other, ssem.at[base + a], rsem.at[base + a], (x, y, 1 - c)))
        return sends, recvs

    return _Comm("join_halves", tuple(bufs), tuple(SDS(b.shape, b.dtype) for b in bufs), {a: a for a in range(n)},
                 n, (phase,))


def _cast_place(w, ids):
    rows, cols = w.shape
    tr = min(rows, 256)

    def body(ids_ref, w_ref, o_ref):
        o_ref[...] = w_ref[...].astype(BF)

    return pl.pallas_call(
        body, name="cast_place",
        grid_spec=pltpu.PrefetchScalarGridSpec(
            num_scalar_prefetch=1, grid=(rows // tr,),
            in_specs=[pl.BlockSpec((tr, cols), lambda i, ids_ref: (i, 0))],
            out_specs=pl.BlockSpec((None, tr, cols), lambda i, ids_ref: (ids_ref[3], i, 0))),
        out_shape=SDS((N_SHARD, rows, cols), BF),
        compiler_params=_cp(("parallel",), 40))(ids, w)


def _chip_sum(g, p, ids):
    n_s, rows, cols = g.shape
    h = rows // 2
    tr = min(h, 256)
    nb = h // tr

    def body(ids_ref, g_ref, p_ref, o_ref, o16_ref):
        t = g_ref[...] + p_ref[...]
        o_ref[...] = t
        o16_ref[...] = t.astype(BF)

    out_spec = pl.BlockSpec((None, tr, cols), lambda s, i, ids_ref: (s, i, 0))
    return pl.pallas_call(
        body, name="chip_sum",
        grid_spec=pltpu.PrefetchScalarGridSpec(
            num_scalar_prefetch=1, grid=(n_s, nb),
            in_specs=[pl.BlockSpec((None, tr, cols), lambda s, i, ids_ref: (s, ids_ref[2] * nb + i, 0)),
                      pl.BlockSpec((None, tr, cols), lambda s, i, ids_ref: (s, i, 0))],
            out_specs=(out_spec, out_spec)),
        out_shape=(SDS((n_s, h, cols), g.dtype), SDS((n_s, h, cols), BF)),
        compiler_params=_cp(("parallel", "parallel"), 40))(ids, g, p)


def _shard_sum(t, q, ids):
    _, h, cols = t.shape
    tr = min(h, 256)
    nb = h // tr

    def body(ids_ref, t_ref, q_ref, o_ref):
        o_ref[...] = ((t_ref[...] + q_ref[0].astype(F32)) + q_ref[1].astype(F32)) + q_ref[2].astype(F32)

    return pl.pallas_call(
        body, name="shard_sum",
        grid_spec=pltpu.PrefetchScalarGridSpec(
            num_scalar_prefetch=1, grid=(nb,),
            in_specs=[pl.BlockSpec((None, tr, cols), lambda i, ids_ref: (ids_ref[3], i, 0)),
                      pl.BlockSpec((3, tr, cols), lambda i, ids_ref: (0, i, 0))],
            out_specs=pl.BlockSpec((tr, cols), lambda i, ids_ref: (ids_ref[2] * nb + i, 0))),
        out_shape=SDS((2 * h, cols), t.dtype),
        compiler_params=_cp(("parallel",), 40))(ids, t, q)


def _gather_small(block, n_sum):
    rows, cols = block.shape
    n_dev = 8

    def body(x_ref, o_ref, g_ref, buf, send_sems, recv_sems, local_sem):
        x, y, c, chips = _place()
        me, sibling = (x, y, c), (x, y, 1 - c)

        def slot(px_, py_, pc_):
            return buf.at[4 * px_ + 2 * py_ + pc_]

        def copy(k, who, to, src=None):
            return _remote(slot(*who) if src is None else src, slot(*who), send_sems.at[k], recv_sems.at[k], to)

        mine = pltpu.make_async_copy(x_ref, slot(*me), local_sem)
        mine.start()
        first = [copy(0, me, sibling, src=x_ref)]
        first += [copy(1 + j, me, (*chip, c), src=x_ref) for j, chip in enumerate(chips)]
        for cp in first:
            cp.start()
        passed = [copy(4 + j, (*chip, c), sibling) for j, chip in enumerate(chips)]
        for j, chip in enumerate(chips):
            copy(1 + j, (*chip, c), me).wait_recv()
            passed[j].start()
        copy(0, sibling, me).wait_recv()
        for j, chip in enumerate(chips):
            copy(4 + j, (*chip, 1 - c), me).wait_recv()
        for cp in first + passed:
            cp.wait_send()
        mine.wait()
        acc = buf[0, :, :n_sum]
        for s in range(1, n_dev):
            acc = acc + buf[s, :, :n_sum]
        o_ref[...] = acc
        for s in range(n_dev):
            g_ref[s * rows:(s + 1) * rows, :] = buf[s, :, n_sum:]

    return pl.pallas_call(
        body, name="gather_small",
        in_specs=[pl.BlockSpec(memory_space=pltpu.VMEM)],
        out_specs=(pl.BlockSpec(memory_space=pltpu.VMEM), pl.BlockSpec(memory_space=pltpu.VMEM)),
        out_shape=(SDS((rows, n_sum), F32), SDS((n_dev * rows, cols - n_sum), F32)),
        scratch_shapes=[pltpu.VMEM((n_dev, rows, cols), F32), pltpu.SemaphoreType.DMA((7,)),
                        pltpu.SemaphoreType.DMA((7,)), pltpu.SemaphoreType.DMA],
        compiler_params=_cp(has_side_effects=True))(block)


def _adam_math(w, g, m, v):
    m = ADAM_B1 * m + (1.0 - ADAM_B1) * g
    v = ADAM_B2 * v + (1.0 - ADAM_B2) * (g * g)
    m_hat = m / (1.0 - ADAM_B1 ** ADAM_STEP)
    v_hat = v / (1.0 - ADAM_B2 ** ADAM_STEP)
    delta = -ADAM_LR * (m_hat / (jnp.sqrt(v_hat) + ADAM_EPS) + ADAM_WD * w)
    return delta, m, v


def _adamw(w, g, m, v):
    rows, cols = w.shape
    tr = min(rows, 256 if cols <= 2048 else 128)

    def body(w_ref, g_ref, m_ref, v_ref, go_ref, d_ref, nm_ref, nv_ref):
        g = g_ref[...]
        go_ref[...] = g
        d_ref[...], nm_ref[...], nv_ref[...] = _adam_math(w_ref[...], g, m_ref[...], v_ref[...])

    spec = pl.BlockSpec((tr, cols), lambda i: (i, 0))
    return pl.pallas_call(
        body, name="adamw", grid=(rows // tr,), in_specs=[spec] * 4, out_specs=(spec,) * 4,
        out_shape=(SDS(w.shape, F32),) * 4, compiler_params=_cp(("parallel",), 40))(w, g, m, v)


def _adamw_small(w, g, m, v):
    def body(w_ref, g_ref, m_ref, v_ref, go_ref, d_ref, nm_ref, nv_ref):
        w = w_ref[...]
        g = g_ref[...]
        sub = lax.broadcasted_iota(jnp.int32, w.shape, 0)
        lane = lax.broadcasted_iota(jnp.int32, w.shape, 1)
        is_ret = jnp.logical_and(sub == 5, lane < 2 * RET_HEADS)
        u = jnp.exp(jnp.where(is_ret, w, -1.0) * jnp.log(2.0))
        g = jnp.where(is_ret, g * (-u * jnp.log(2.0) / (1.0 - u)), g)
        go_ref[...] = g
        d_ref[...], nm_ref[...], nv_ref[...] = _adam_math(w, g, m_ref[...], v_ref[...])

    return pl.pallas_call(body, name="adamw_small", out_shape=(SDS(w.shape, F32),) * 4)(w, g, m, v)


def _rope_tables(seq, n_samp, n_ctx_rows):
    rows = seq // GRID_W
    row = jnp.repeat(jnp.arange(rows, dtype=F32), GRID_W)
    col = jnp.tile(jnp.arange(GRID_W, dtype=F32), rows)
    half = ATT_HEAD_DIM // 2
    freqs = ROPE_THETA ** (-jnp.arange(0, half, 2, dtype=F32) / half)
    ang = jnp.concatenate([row[:, None] * freqs, col[:, None] * freqs], axis=-1)
    cos, sin = jnp.cos(ang), jnp.sin(ang)
    cos_f = jnp.repeat(cos, 2, axis=1)
    sin_s = jnp.stack([-sin, sin], axis=-1).reshape(seq, ATT_HEAD_DIM)
    cos_all = jnp.concatenate([jnp.tile(cos_f, (n_samp, 1)), jnp.ones((n_ctx_rows, ATT_HEAD_DIM), F32)], axis=0)
    sin_all = jnp.concatenate([jnp.tile(sin_s, (n_samp, 1)), jnp.zeros((n_ctx_rows, ATT_HEAD_DIM), F32)], axis=0)
    return cos_all, sin_all


def _pack_small(c_ctx, norm_w, b_ada, ret, qn, kn):
    d = D_MODEL
    row5 = jnp.concatenate([ret.reshape(-1), jnp.zeros((128 - 2 * RET_HEADS,), F32), qn.reshape(-1), kn.reshape(-1),
                            jnp.zeros((d - 384,), F32)])
    return jnp.concatenate([c_ctx.reshape(1, d), norm_w.reshape(1, d), b_ada.reshape(3, d), row5.reshape(1, d),
                            jnp.zeros((2, d), F32)], axis=0)


def _unpack_small(p):
    d = D_MODEL
    return (p[0], p[1:2], p[2:5].reshape(1, 3 * d), p[5, :2 * RET_HEADS].reshape(1, 2, RET_HEADS),
            p[5:6, 128:256], p[5:6, 256:384])


def _step(x, c, ctx, c_ctx, norm_w, b_ada, ret_log2_decay, q_norm_w, k_norm_w, loss_target, weights, ids):
    n_samp, seq, d = x.shape
    lc = ctx.shape[1]
    t_lat, t_ctx = n_samp * seq, n_samp * lc
    assert seq % TM == 0 and t_ctx == TM and t_lat % lc == 0 and seq % GRID_W == 0
    tps = seq // TM

    x_lat = x.reshape(t_lat, d)
    x_ctx = _hbm(ctx.reshape(t_ctx, d))
    cvec8 = jnp.concatenate([c, c_ctx.reshape(1, d), jnp.zeros((8 - n_samp - 1, d), F32)], axis=0)
    lg = jnp.log1p(-jnp.exp2(ret_log2_decay.reshape(2, RET_HEADS)))
    cos_all, sin_all = _rope_tables(seq, n_samp, t_ctx)

    w_ada_b, w_in_b, w_or_b, w_oa_b, w_out_b = _hbm(*weights)
    w_ada_g = _hbm(*_run_comm(_ag_comm((w_ada_b,))))
    mod8 = _adaln_fwd(cvec8, w_ada_g, b_ada)
    mod3 = mod8[:n_samp + 1]
    shift3 = mod3[:, None, 0:d]
    scale3 = mod3[:, None, d:2 * d]
    gate3 = mod3[:, None, 2 * d:3 * d]

    hx, hxt = _hbm(*_norm_fwd(x_lat, x_ctx, norm_w, scale3, shift3, tps, n_samp))
    px, w_in_g = _in_proj_gather(hx, w_in_b, ids)

    states0 = _ctx_state_fwd(px, lg, n_samp, t_lat, lc)
    (o_f, o_b, saved), w_o = _ret_fwd(px, states0, lg, n_samp, seq,
                                      comm=_ag_comm((w_or_b, w_oa_b, w_out_b)))
    w_o_ret, w_o_att, w_out = (w.reshape(-1, d) for w in w_o)

    qn = _att_prep_q(px, cos_all, sin_all, q_norm_w, t_lat)
    kn, vn = _hbm(*_att_prep_kv(px, cos_all, sin_all, k_norm_w))
    o_att, lse = _att_fwd(qn, kn, vn, n_samp, seq, lc)

    (gx_res, do, do_att, dpx, loss8, dgate, g_w_o_ret, g_w_o_att, g_w_out) = _merge(
        x_lat, loss_target.reshape(t_lat, d), o_f, o_b, o_att, px, gate3, w_o_ret, w_o_att, w_out, tps)

    g_a = [_hbm(g.reshape(N_SHARD, -1, d)) for g in (g_w_o_ret, g_w_o_att, g_w_out)]
    (dpx, dkl, dkc, dvl, dvc, gqw), sib_a = _att_bwd(
        dpx, qn, kn, vn, px, o_att, lse, do_att, cos_all, sin_all, q_norm_w, n_samp, seq, lc, comm=_swap_comm(g_a))
    dpx, gkw = _att_kv_bwd(dpx, dkl, dkc, dvl, dvc, px, cos_all, sin_all, k_norm_w)
    t_a = [_hbm(*_chip_sum(g, _hbm(p), ids)) for g, p in zip(g_a, sib_a)]

    (dpx, dstates, dlg_lat), q_a = _ret_bwd(dpx, px, do, saved, lg, n_samp, seq,
                                            comm=_exchange_comm([t16 for _, t16 in t_a]))
    r_a = [_shard_sum(t, q, ids) for (t, _), q in zip(t_a, q_a)]
    dpx, dlg_ctx = _ctx_state_bwd(dpx, px, dstates, lg, n_samp, t_lat, lc)
    dpx = _zero_ctx_tail(dpx, t_lat)

    n_tiles = dpx.shape[0] // _big_rows(dpx.shape[0])
    g_b = _gw_in(hxt, dpx)
    dhx, (sib_b, *r_a) = _dhx(dpx, w_in_g, 0, 1, None, _join_comms(_swap_comm([g_b]), _join_comm(r_a)))
    t_b, t16_b = _chip_sum(g_b, sib_b, ids)
    dhx, (q_b,) = _dhx(dpx, w_in_g, 1, n_tiles - 1, dhx, _exchange_comm([t16_b]))
    r_b_half = _shard_sum(t_b, q_b, ids)
    grad_x, dshift, dscale, g_norm_w = _norm_bwd(x_lat, x_ctx, dhx, gx_res, norm_w, scale3, tps, n_samp)

    dgate_all = jnp.concatenate([dgate, jnp.zeros((1, 1, d), F32)], axis=0)
    dmod3 = jnp.concatenate([dshift, dscale, dgate_all], axis=2).reshape(n_samp + 1, 3 * d)
    dmod8 = jnp.concatenate([dmod3, jnp.zeros((8 - n_samp - 1, 3 * d), F32)], axis=0)
    g_lg = (jnp.sum(dlg_lat[:, :, 0], axis=0).reshape(2, RET_HEADS)
            + jnp.stack([jnp.sum(dlg_ctx[:, :, 0, 0], axis=0), jnp.sum(dlg_ctx[:, :, 1, 0], axis=0)], axis=0))
    g_qw = jnp.sum(gqw, axis=(0, 1, 2))
    zero = jnp.zeros((d,), F32)

    local = _pack_small(zero, g_norm_w, jnp.zeros((3 * d,), F32), g_lg, g_qw, gkw).at[6, 0].set(loss8[0, 0])
    small_sum, gathered = _gather_small(jnp.concatenate([local, cvec8, dmod8], axis=1), d)
    (g_w_ada, g_b_ada, dc_all), (r_b,) = _adaln_bwd(gathered[:, :d], gathered[:, d:], w_ada_g,
                                                     comm=_join_comm([r_b_half]))
    dc_ctx = jnp.sum(dc_all.reshape(-1, 8, d)[:, n_samp], axis=0)
    small = small_sum + _pack_small(dc_ctx, zero, g_b_ada, jnp.zeros((2, RET_HEADS), F32), zero[:128], zero[:128])
    r_c = lax.dynamic_index_in_dim(g_w_ada, ids[3], 0, keepdims=False)
    return small[6, 0], grad_x.reshape(n_samp, seq, d), (r_c, r_b, *r_a), small


def kernel(x, c, ctx, c_ctx, norm_w, w_ada, b_ada, w_in, ret_log2_decay, q_norm_w, k_norm_w, w_o_ret, w_o_att, w_out, loss_target, m_c_ctx, m_norm_w, m_w_ada, m_b_ada, m_w_in, m_ret_log2_decay, m_q_norm_w, m_k_norm_w, m_w_o_ret, m_w_o_att, m_w_out, v_c_ctx, v_norm_w, v_w_ada, v_b_ada, v_w_in, v_ret_log2_decay, v_q_norm_w, v_k_norm_w, v_w_o_ret, v_w_o_att, v_w_out):
    big_w = _hbm(w_ada[0], w_in[0], w_o_ret[0], w_o_att[0], w_out[0])
    big_m = (m_w_ada[0], m_w_in[0], m_w_o_ret[0], m_w_o_att[0], m_w_out[0])
    big_v = (v_w_ada[0], v_w_in[0], v_w_o_ret[0], v_w_o_att[0], v_w_out[0])

    ids = _place_ids()
    loss, grad_x, big_grad, small_grad_in = _step(
        x, c, ctx, c_ctx, norm_w[0:1], b_ada[0:1], ret_log2_decay[0], q_norm_w[0:1], k_norm_w[0:1], loss_target,
        tuple(_cast_place(w, ids) for w in big_w), ids)
    small_w = _pack_small(c_ctx, norm_w, b_ada, ret_log2_decay, q_norm_w, k_norm_w)
    small_m = _pack_small(m_c_ctx, m_norm_w, m_b_ada, m_ret_log2_decay, m_q_norm_w, m_k_norm_w)
    small_v = _pack_small(v_c_ctx, v_norm_w, v_b_ada, v_ret_log2_decay, v_q_norm_w, v_k_norm_w)
    small_grad, small_delta, small_nm, small_nv = _adamw_small(small_w, small_grad_in, small_m, small_v)

    big_g, big_delta, big_nm, big_nv = [], [], [], []
    for w, g, m, v in zip(big_w, big_grad, big_m, big_v):
        go, dlt, nm, nv = _adamw(w, g, m, v)
        big_g.append(go[None])
        big_delta.append(dlt[None])
        big_nm.append(nm[None])
        big_nv.append(nv[None])
    big_grad = big_g

    def order(small_packed, big):
        s = _unpack_small(small_packed)
        return (s[0], s[1], big[0], s[2], big[1], s[3], s[4], s[5], big[2], big[3], big[4])

    return (loss, grad_x, *order(small_grad, big_grad), *order(small_delta, big_delta),
            *order(small_nm, big_nm), *order(small_nv, big_nv))
```

```python
import functools
from typing import NamedTuple

import jax
import jax.numpy as jnp
from jax import lax
from jax.experimental import pallas as pl
from jax.experimental.pallas import tpu as pltpu

F32 = jnp.float32
BF = jnp.bfloat16
SDS = jax.ShapeDtypeStruct
MESH = pl.DeviceIdType.MESH
ANY = pl.BlockSpec(memory_space=pl.ANY)
SMEM = pl.BlockSpec(memory_space=pltpu.SMEM)

D_MODEL = 1024
GRID_W = 64
RET_HEADS = 4
RET_DK = 256
RET_DV = 512
RET_CHUNK = 128
ATT_HEADS = 8
ATT_KV_HEADS = 2
ATT_REP = ATT_HEADS // ATT_KV_HEADS
ATT_HEAD_DIM = 128
ROPE_THETA = 10000.0
NORM_EPS = 1e-6
IN_COLS = 10752
KV_COLS = 3584
C_RK, C_RV, C_AK, C_AV, C_RQ, C_RG, C_AQ, C_AG, C_MR, C_MA = 0, 1024, 3072, 3328, 3584, 4608, 6656, 7680, 8704, 9728
N_SHARD = 4
ADA_W = 3 * D_MODEL // N_SHARD
IN_W = IN_COLS // N_SHARD
IN_BLK = IN_W
BPS = IN_W // IN_BLK
N_IN_BLK = IN_COLS // IN_BLK
TM = 512
ATT_TQ = 512
ADAM_LR, ADAM_B1, ADAM_B2, ADAM_EPS, ADAM_WD, ADAM_STEP = 0.001, 0.9, 0.999, 1e-08, 0.01, 10
MIB = 1024 * 1024


def _cp(sem=None, vmem_mb=None, **kw):
    if sem is not None:
        kw["dimension_semantics"] = sem
    if vmem_mb is not None:
        kw["vmem_limit_bytes"] = vmem_mb * MIB
    return pltpu.CompilerParams(**kw)


def _dot(a, b, ca=1, cb=0):
    return lax.dot_general(a.astype(BF), b.astype(BF), (((ca,), (cb,)), ((), ())), preferred_element_type=F32)


def _sigmoid(x):
    return 0.5 * jnp.tanh(0.5 * x) + 0.5


def _sum_all(x):
    return jnp.sum(jnp.sum(x, axis=1, keepdims=True), axis=0, keepdims=True)


def _swap_pairs(x):
    ax = x.ndim - 1
    lane = lax.broadcasted_iota(jnp.int32, x.shape, ax)
    nxt = pltpu.roll(x, x.shape[ax] - 1, ax)
    prv = pltpu.roll(x, 1, ax)
    return jnp.where(lane % 2 == 0, nxt, prv)


def _rms(x):
    return lax.rsqrt(jnp.mean(x * x, axis=-1, keepdims=True) + NORM_EPS)


def _rms_bwd(dxh, xh, r):
    return r * (dxh - xh * jnp.mean(dxh * xh, axis=-1, keepdims=True))


class _Comm(NamedTuple):
    name: str
    ins: tuple
    out_shapes: tuple
    aliases: dict
    n_sems: int
    phases: tuple


def _join_comms(*comms):
    offs, i_off, o_off, s_off = [], 0, 0, 0
    for cm in comms:
        offs.append((i_off, o_off, s_off))
        i_off, o_off, s_off = i_off + len(cm.ins), o_off + len(cm.out_shapes), s_off + cm.n_sems

    def phase(k):
        def run(ins, outs, ssem, rsem, base):
            sends, recvs = [], []
            for cm, (io, oo, so) in zip(comms, offs):
                if k < len(cm.phases):
                    s, r = cm.phases[k](ins[io:io + len(cm.ins)], outs[oo:oo + len(cm.out_shapes)], ssem, rsem,
                                        base + so)
                    sends += s
                    recvs += r
            return sends, recvs
        return run

    aliases = {}
    for cm, (io, oo, _) in zip(comms, offs):
        aliases.update({io + a: oo + b for a, b in cm.aliases.items()})
    return _Comm("+".join(cm.name for cm in comms), sum((cm.ins for cm in comms), ()),
                 sum((cm.out_shapes for cm in comms), ()), aliases, s_off,
                 tuple(phase(k) for k in range(max(len(cm.phases) for cm in comms))))


def _run_phases(comm, cins, couts, ssem, rsem):
    for k, phase in enumerate(comm.phases):
        sends, recvs = phase(cins, couts, ssem, rsem, 0)
        if k > 0:
            for cp in sends:
                cp.start()
        for cp in recvs:
            cp.wait_recv()
        for cp in sends:
            cp.wait_send()


def _call(body, args, comm, *, name, grid, in_specs, out_specs, out_shape, scratch_shapes=(),
          compiler_params, aliases=None):
    n_in, n_out, n_sc = len(in_specs), len(out_specs), len(scratch_shapes)
    n_ci, n_co = len(comm.ins), len(comm.out_shapes)
    io_alias = dict(aliases or {})
    io_alias.update({n_in + a: n_out + b for a, b in comm.aliases.items()})

    def kernel_body(*refs):
        ins, cins = refs[:n_in], refs[n_in:n_in + n_ci]
        outs = refs[n_in + n_ci:n_in + n_ci + n_out]
        couts = refs[n_in + n_ci + n_out:n_in + n_ci + n_out + n_co]
        scratch = refs[n_in + n_ci + n_out + n_co:n_in + n_ci + n_out + n_co + n_sc]
        ssem, rsem = refs[-2:]
        first = functools.reduce(jnp.logical_and, [pl.program_id(k) == 0 for k in range(len(grid))])
        last = functools.reduce(jnp.logical_and, [pl.program_id(k) == grid[k] - 1 for k in range(len(grid))])

        @pl.when(first)
        def _():
            for cp in comm.phases[0](cins, couts, ssem, rsem, 0)[0]:
                cp.start()

        body(*ins, *outs, *scratch)

        @pl.when(last)
        def _():
            _run_phases(comm, cins, couts, ssem, rsem)

    res = pl.pallas_call(
        kernel_body, name=name + "+" + comm.name, grid=grid, in_specs=list(in_specs) + [ANY] * n_ci,
        out_specs=tuple(out_specs) + tuple([ANY] * n_co), out_shape=tuple(out_shape) + tuple(comm.out_shapes),
        scratch_shapes=list(scratch_shapes) + [pltpu.SemaphoreType.DMA((comm.n_sems,)),
                                               pltpu.SemaphoreType.DMA((comm.n_sems,))],
        input_output_aliases=io_alias, compiler_params=compiler_params)(*args, *comm.ins)
    return tuple(res[:n_out]), tuple(res[n_out:])


def _adaln_fwd(cvec, w_ada_b, b_ada, ids):
    n_rows = cvec.shape[0]

    def body(ids_ref, c_ref, w_ref, b_ref, o_ref):
        cv = c_ref[...]
        sc = (cv * _sigmoid(cv)).astype(BF)
        o_ref[...] = jnp.dot(sc, w_ref[0], preferred_element_type=F32) + b_ref[0]

    return pl.pallas_call(
        body, name="adaln_fwd",
        grid_spec=pltpu.PrefetchScalarGridSpec(
            num_scalar_prefetch=1, grid=(1,),
            in_specs=[pl.BlockSpec((n_rows, D_MODEL), lambda i, ids_ref: (0, 0)),
                      pl.BlockSpec((1, D_MODEL, ADA_W), lambda i, ids_ref: (ids_ref[3], 0, 0)),
                      pl.BlockSpec((1, 1, ADA_W), lambda i, ids_ref: (ids_ref[3], 0, 0))],
            out_specs=pl.BlockSpec((n_rows, ADA_W), lambda i, ids_ref: (0, 0))),
        out_shape=SDS((n_rows, ADA_W), F32),
        compiler_params=_cp(("arbitrary",), 32))(ids, cvec, w_ada_b, b_ada.reshape(N_SHARD, 1, ADA_W))


def _adaln_bwd(cvec, dmod, w_ada_g, comm):
    n_rows = cvec.shape[0]
    def body(c_ref, d_ref, w_ref, gw_ref, gb_ref, dc_ref):
        cv = c_ref[...]
        sg = _sigmoid(cv)
        sc = cv * sg
        dm = d_ref[...]
        gb_ref[...] = jnp.sum(dm, axis=0, keepdims=True)
        dsc = jnp.zeros(cv.shape, F32)
        for s in range(N_SHARD):
            cols = slice(s * ADA_W, (s + 1) * ADA_W)
            gw_ref[s] = _dot(sc, dm[:, cols], 0, 0)
            dsc = dsc + _dot(dm[:, cols], w_ref[s], 1, 1)
        dc_ref[...] = dsc * (sg * (1.0 + cv * (1.0 - sg)))

    def whole(shape):
        return pl.BlockSpec(shape, lambda i: (0,) * len(shape))

    shapes = ((N_SHARD, D_MODEL, ADA_W), (1, 3 * D_MODEL), (n_rows, D_MODEL))
    return _call(body, [cvec, dmod, w_ada_g], comm, name="adaln_bwd", grid=(1,),
                 in_specs=[whole(cvec.shape), whole(dmod.shape), whole(w_ada_g.shape)],
                 out_specs=tuple(whole(s) for s in shapes), out_shape=tuple(SDS(s, F32) for s in shapes),
                 compiler_params=_cp(("arbitrary",), 56))


def _big_rows(rows):
    return 1536 if rows % 1536 == 0 else TM


def _norm_fwd(x_lat, x_ctx, norm_w, scale3, shift3, tiles_per_sample, n_samp):
    n_lat = x_lat.shape[0] // TM
    rows = x_lat.shape[0] + x_ctx.shape[0]

    def samp(i):
        return jnp.minimum(i // tiles_per_sample, n_samp)

    def body(x_ref, c_ref, nw_ref, sc_ref, sh_ref, hx_ref, hxt_ref):
        x = jnp.where(pl.program_id(0) < n_lat, x_ref[...], c_ref[...])
        h = x * _rms(x) * nw_ref[...] * (1.0 + sc_ref[...]) + sh_ref[...]
        hx_ref[...] = h.astype(BF)
        hxt_ref[...] = h.T.astype(BF)

    return pl.pallas_call(
        body, name="norm_fwd", grid=(rows // TM,),
        in_specs=[pl.BlockSpec((TM, D_MODEL), lambda i: (jnp.minimum(i, n_lat - 1), 0)),
                  pl.BlockSpec((TM, D_MODEL), lambda i: (jnp.maximum(i - n_lat, 0), 0)),
                  pl.BlockSpec((1, D_MODEL), lambda i: (0, 0)),
                  pl.BlockSpec((None, 1, D_MODEL), lambda i: (samp(i), 0, 0)),
                  pl.BlockSpec((None, 1, D_MODEL), lambda i: (samp(i), 0, 0))],
        out_specs=(pl.BlockSpec((TM, D_MODEL), lambda i: (i, 0)),
                   pl.BlockSpec((D_MODEL, TM), lambda i: (0, i))),
        out_shape=(SDS((rows, D_MODEL), BF), SDS((D_MODEL, rows), BF)),
        compiler_params=_cp(("parallel",), 40))(x_lat, x_ctx, norm_w, scale3, shift3)


def _norm_bwd(x_lat, x_ctx, dhx, gx_res, norm_w, scale3, tiles_per_sample, n_samp):
    rows = x_lat.shape[0] + x_ctx.shape[0]
    n_lat = tiles_per_sample * n_samp

    def samp(i):
        return jnp.minimum(i // tiles_per_sample, n_samp)

    def lat(i):
        return jnp.minimum(i, n_lat - 1)

    def body(x_ref, c_ref, dh_ref, gr_ref, nw_ref, sc_ref, gx_ref, dsh_ref, dsc_ref, dnw_ref):
        i = pl.program_id(0)
        x = jnp.where(i < n_lat, x_ref[...], c_ref[...])
        r = _rms(x)
        xh = x * r
        nw = nw_ref[...]
        dh = dh_ref[...]
        first = jnp.logical_or(i % tiles_per_sample == 0, i >= n_lat)

        @pl.when(first)
        def _():
            dsh_ref[...] = jnp.zeros_like(dsh_ref)
            dsc_ref[...] = jnp.zeros_like(dsc_ref)

        @pl.when(i == 0)
        def _():
            dnw_ref[...] = jnp.zeros_like(dnw_ref)

        dsh_ref[...] += jnp.sum(dh, axis=0, keepdims=True)
        dsc_ref[...] += jnp.sum(dh * (xh * nw), axis=0, keepdims=True)
        du = dh * (1.0 + sc_ref[...])
        dnw_ref[...] += jnp.sum(du * xh, axis=0, keepdims=True)

        @pl.when(i < n_lat)
        def _():
            gx_ref[...] = gr_ref[...] + _rms_bwd(du * nw, xh, r)

    return pl.pallas_call(
        body, name="norm_bwd", grid=(rows // TM,),
        in_specs=[pl.BlockSpec((TM, D_MODEL), lambda i: (lat(i), 0)),
                  pl.BlockSpec((TM, D_MODEL), lambda i: (jnp.maximum(i - n_lat, 0), 0)),
                  pl.BlockSpec((TM, D_MODEL), lambda i: (i, 0)),
                  pl.BlockSpec((TM, D_MODEL), lambda i: (lat(i), 0)),
                  pl.BlockSpec((1, D_MODEL), lambda i: (0, 0)),
                  pl.BlockSpec((None, 1, D_MODEL), lambda i: (samp(i), 0, 0))],
        out_specs=(pl.BlockSpec((TM, D_MODEL), lambda i: (lat(i), 0)),
                   pl.BlockSpec((None, 1, D_MODEL), lambda i: (samp(i), 0, 0)),
                   pl.BlockSpec((None, 1, D_MODEL), lambda i: (samp(i), 0, 0)),
                   pl.BlockSpec((1, D_MODEL), lambda i: (0, 0))),
        out_shape=(SDS((n_lat * TM, D_MODEL), F32), SDS((n_samp + 1, 1, D_MODEL), F32),
                   SDS((n_samp + 1, 1, D_MODEL), F32), SDS((1, D_MODEL), F32)),
        compiler_params=_cp(("arbitrary",), 40))(x_lat, x_ctx, dhx, gx_res, norm_w, scale3)


def _in_proj_gather(hx, w_buf, ids):
    rows = hx.shape[0]
    tb = _big_rows(rows)
    n_i = rows // tb
    hrows = D_MODEL // 2

    def body(ids_ref, h_ref, w_in_hbm, px_ref, w_hbm, wv, lsem, ssem, rsem):
        j, i = pl.program_id(0), pl.program_id(1)
        x, y, c, chips = _place()
        sibling = (x, y, 1 - c)

        def half(s, which):
            return w_hbm.at[s, pl.ds(which * hrows, hrows), :]

        def over_ici(rel):
            chip = chips[rel]
            mine, theirs = half(2 * x + y, c), half(2 * chip[0] + chip[1], c)
            return (_remote(mine, mine, ssem.at[rel], rsem.at[rel], (*chip, c)),
                    _remote(theirs, theirs, ssem.at[rel], rsem.at[rel], (*chip, c)))

        def over_d2d(rel):
            s = 2 * chips[rel][0] + chips[rel][1]
            return (_remote(half(s, c), half(s, c), ssem.at[3 + rel], rsem.at[3 + rel], sibling),
                    _remote(half(s, 1 - c), half(s, 1 - c), ssem.at[3 + rel], rsem.at[3 + rel], sibling))

        first_row_tile = i == 0

        @pl.when(jnp.logical_and(j == 0, first_row_tile))
        def _():
            over_ici(0)[0].start()
            over_ici(1)[0].start()

        @pl.when(jnp.logical_and(j == 1, first_row_tile))
        def _():
            for rel in range(2):
                over_ici(rel)[1].wait_recv()
                over_d2d(rel)[0].start()
            over_ici(2)[0].start()
            over_d2d(0)[1].wait_recv()

        @pl.when(jnp.logical_and(j == 2, first_row_tile))
        def _():
            over_d2d(1)[1].wait_recv()

        @pl.when(jnp.logical_and(j == 3, first_row_tile))
        def _():
            over_ici(2)[1].wait_recv()
            passed, landing = over_d2d(2)
            passed.start()
            landing.wait_recv()

        @pl.when(first_row_tile)
        def _():
            cp = pltpu.make_async_copy(w_hbm.at[ids_ref[4 + j]], wv, lsem)
            cp.start()
            cp.wait()

        px_ref[...] = jnp.dot(h_ref[...], wv[...], preferred_element_type=F32).astype(BF)

        @pl.when(jnp.logical_and(j == N_SHARD - 1, i == n_i - 1))
        def _():
            for rel in range(3):
                over_ici(rel)[0].wait_send()
                over_d2d(rel)[0].wait_send()

    return pl.pallas_call(
        body, name="in_proj_gather", input_output_aliases={2: 1},
        grid_spec=pltpu.PrefetchScalarGridSpec(
            num_scalar_prefetch=1, grid=(N_SHARD, n_i),
            in_specs=[pl.BlockSpec((tb, D_MODEL), lambda j, i, ids_ref: (i, 0)), ANY],
            out_specs=(pl.BlockSpec((tb, IN_W), lambda j, i, ids_ref: (i, ids_ref[4 + j])), ANY),
            scratch_shapes=[pltpu.VMEM((D_MODEL, IN_W), BF), pltpu.SemaphoreType.DMA,
                            pltpu.SemaphoreType.DMA((6,)), pltpu.SemaphoreType.DMA((6,))]),
        out_shape=(SDS((rows, IN_COLS), BF), SDS(w_buf.shape, w_buf.dtype)),
        compiler_params=_cp(("arbitrary", "arbitrary"), 56))(ids, hx, w_buf)


def _gw_in(hxt, dpx_all):
    rows = dpx_all.shape[0]
    tb = _big_rows(rows)

    def body(h_ref, d_ref, o_ref):
        @pl.when(pl.program_id(1) == 0)
        def _():
            o_ref[...] = jnp.zeros_like(o_ref)

        o_ref[...] += jnp.dot(h_ref[...], d_ref[...], preferred_element_type=F32)

    return pl.pallas_call(
        body, name="gw_in", grid=(N_IN_BLK, rows // tb),
        in_specs=[pl.BlockSpec((D_MODEL, tb), lambda j, i: (0, i)),
                  pl.BlockSpec((tb, IN_BLK), lambda j, i: (i, j))],
        out_specs=pl.BlockSpec((None, D_MODEL, IN_BLK), lambda j, i: (j // BPS, 0, j % BPS)),
        out_shape=SDS((N_SHARD, D_MODEL, IN_W), F32),
        compiler_params=_cp(("arbitrary", "arbitrary"), 56))(hxt, dpx_all)


def _dhx(dpx_all, w_in_g, tile0, n_tiles, dhx, comm):
    rows = dpx_all.shape[0]
    tb = _big_rows(rows)

    def body(d_ref, w_ref, *rest):
        o_ref = rest[-1]

        @pl.when(pl.program_id(1) == 0)
        def _():
            o_ref[...] = jnp.zeros_like(o_ref)

        o_ref[...] += lax.dot_general(d_ref[...], w_ref[...], (((1,), (1,)), ((), ())), preferred_element_type=F32)

    args, in_specs, aliases = [dpx_all, w_in_g], [
        pl.BlockSpec((tb, IN_BLK), lambda i, j: (tile0 + i, j)),
        pl.BlockSpec((None, D_MODEL, IN_BLK), lambda i, j: (j // BPS, 0, j % BPS))], None
    if dhx is not None:
        args, in_specs, aliases = args + [dhx], in_specs + [ANY], {2: 0}
    (out,), got = _call(body, args, comm, name="dhx", grid=(n_tiles, N_IN_BLK), in_specs=in_specs,
                        out_specs=(pl.BlockSpec((tb, D_MODEL), lambda i, j: (tile0 + i, 0)),),
                        out_shape=(SDS((rows, D_MODEL), F32),), aliases=aliases,
                        compiler_params=_cp(("arbitrary", "arbitrary"), 56))
    return out, got


def _decays(lgv, d):
    c = RET_CHUNK
    ii = lax.broadcasted_iota(jnp.int32, (c, 1), 0).astype(F32)
    jj = lax.broadcasted_iota(jnp.int32, (1, c), 1).astype(F32)
    a_i = jnp.where(d == 0, ii, c - 1.0 - ii)
    a_j = jnp.where(d == 0, jj, c - 1.0 - jj)
    rel = a_i - a_j
    mask = jnp.where(rel >= 0, jnp.exp(lgv * jnp.maximum(rel, 0.0)), 0.0)
    qd = jnp.exp(lgv * (a_i + 1.0))
    kd = jnp.exp(lgv * (c - 1.0 - a_i))
    gc = jnp.exp(jnp.full((1, 1), lgv * c, F32))
    return a_i, rel, mask, qd, kd, gc


def _ctx_state_fwd(px, lg, n_samp, t_lat, lc):
    rb = t_lat // lc

    def body(lg_ref, k_ref, v_ref, o_ref):
        h = pl.program_id(1)
        k = k_ref[...].astype(F32) * (RET_DK ** -0.5)
        v = v_ref[...]
        pos = lax.broadcasted_iota(jnp.int32, (lc, 1), 0).astype(F32)
        o_ref[0] = _dot(k * jnp.exp(lg_ref[0, h] * (lc - 1.0 - pos)), v, 0, 0)
        o_ref[1] = _dot(k * jnp.exp(lg_ref[1, h] * pos), v, 0, 0)

    return pl.pallas_call(
        body, name="ctx_state_fwd", grid=(n_samp, RET_HEADS),
        in_specs=[SMEM,
                  pl.BlockSpec((lc, RET_DK), lambda b, h: (rb + b, C_RK // RET_DK + h)),
                  pl.BlockSpec((lc, RET_DV), lambda b, h: (rb + b, C_RV // RET_DV + h))],
        out_specs=pl.BlockSpec((None, 2, None, RET_DK, RET_DV), lambda b, h: (b, 0, h, 0, 0)),
        out_shape=SDS((n_samp, 2, RET_HEADS, RET_DK, RET_DV), F32),
        compiler_params=_cp(("parallel", "parallel")))(lg, px, px)


def _ctx_state_bwd(dpx, px, dstates, lg, n_samp, t_lat, lc):
    rb = t_lat // lc
    kspec = pl.BlockSpec((lc, RET_DK), lambda b, h: (rb + b, C_RK // RET_DK + h))
    vspec = pl.BlockSpec((lc, RET_DV), lambda b, h: (rb + b, C_RV // RET_DV + h))
    sspec = pl.BlockSpec((None, 2, None, RET_DK, RET_DV), lambda b, h: (b, 0, h, 0, 0))

    def weights(lg_ref, h):
        pos = lax.broadcasted_iota(jnp.int32, (lc, 1), 0).astype(F32)
        e_f = lc - 1.0 - pos
        return pos, e_f, jnp.exp(lg_ref[0, h] * e_f), jnp.exp(lg_ref[1, h] * pos)

    def k_body(lg_ref, dpx_hbm, k_ref, v_ref, ds_ref, dk_ref, dlg_ref):
        pos, e_f, w_f, w_b = weights(lg_ref, pl.program_id(1))
        k = k_ref[...].astype(F32) * (RET_DK ** -0.5)
        y_f = _dot(v_ref[...], ds_ref[0], 1, 1) * w_f
        y_b = _dot(v_ref[...], ds_ref[1], 1, 1) * w_b
        dk_ref[...] = ((y_f + y_b) * (RET_DK ** -0.5)).astype(BF)
        t_f = _sum_all(e_f * k * y_f)
        t_b = _sum_all(pos * k * y_b)
        sub = lax.broadcasted_iota(jnp.int32, (8, 128), 0)
        dlg_ref[...] = jnp.where(sub == 0, t_f, jnp.where(sub == 1, t_b, 0.0))

    def v_body(lg_ref, dpx_hbm, k_ref, ds_ref, dv_ref):
        _, _, w_f, w_b = weights(lg_ref, pl.program_id(1))
        k = k_ref[...].astype(F32) * (RET_DK ** -0.5)
        dv_ref[...] = (_dot(k * w_f, ds_ref[0]) + _dot(k * w_b, ds_ref[1])).astype(BF)

    dpx, dlg = pl.pallas_call(
        k_body, name="ctx_state_bwd_k", grid=(n_samp, RET_HEADS), input_output_aliases={1: 0},
        in_specs=[SMEM, ANY, kspec, vspec, sspec],
        out_specs=(kspec, pl.BlockSpec((None, None, 8, 128), lambda b, h: (b, h, 0, 0))),
        out_shape=(SDS(dpx.shape, dpx.dtype), SDS((n_samp, RET_HEADS, 8, 128), F32)),
        compiler_params=_cp(("parallel", "parallel")))(lg, dpx, px, px, dstates)
    dpx = pl.pallas_call(
        v_body, name="ctx_state_bwd_v", grid=(n_samp, RET_HEADS), input_output_aliases={1: 0},
        in_specs=[SMEM, ANY, kspec, sspec], out_specs=vspec, out_shape=SDS(dpx.shape, dpx.dtype),
        compiler_params=_cp(("parallel", "parallel")))(lg, dpx, px, dstates)
    return dpx, dlg


def _zero_ctx_tail(dpx, t_lat):
    wb = 512
    n_ctx = (dpx.shape[0] - t_lat) // TM

    def body(dpx_hbm, o_ref):
        o_ref[...] = jnp.zeros_like(o_ref)

    return pl.pallas_call(
        body, name="zero_ctx_tail", grid=(n_ctx, (IN_COLS - KV_COLS) // wb), input_output_aliases={0: 0},
        in_specs=[ANY], out_specs=pl.BlockSpec((TM, wb), lambda i, j: (t_lat // TM + i, KV_COLS // wb + j)),
        out_shape=SDS(dpx.shape, dpx.dtype),
        compiler_params=_cp(("parallel", "parallel")))(dpx)


def _ret_specs(row_f, row_b):
    c = RET_CHUNK
    wq = RET_HEADS * RET_DK // 2
    wv = RET_HEADS * RET_DV // 2
    specs = []
    for row in (row_f, row_b):
        specs += [pl.BlockSpec((c, wq), lambda b, n, row=row: (row(b, n), C_RQ // wq)),
                  pl.BlockSpec((c, wq), lambda b, n, row=row: (row(b, n), C_RQ // wq + 1)),
                  pl.BlockSpec((c, 2 * wq), lambda b, n, row=row: (row(b, n), C_RK // (2 * wq))),
                  pl.BlockSpec((c, wv), lambda b, n, row=row: (row(b, n), C_RV // wv)),
                  pl.BlockSpec((c, wv), lambda b, n, row=row: (row(b, n), C_RV // wv + 1))]
    return specs


def _ret_head(refs, h):
    q0, q1, k_ref, v0, v1 = refs
    hh = h % 2
    q = (q0, q1)[h // 2][:, hh * RET_DK:(hh + 1) * RET_DK].astype(F32)
    k = k_ref[:, h * RET_DK:(h + 1) * RET_DK].astype(F32) * (RET_DK ** -0.5)
    v = (v0, v1)[h // 2][:, hh * RET_DV:(hh + 1) * RET_DV]
    return q, k, v


def _ret_fwd(px, states0, lg, n_samp, seq, comm):
    c = RET_CHUNK
    nc = seq // c
    t_lat = n_samp * seq
    wo = RET_HEADS * RET_DV

    def row_f(b, n):
        return b * nc + n

    def row_b(b, n):
        return b * nc + nc - 1 - n

    def body(lg_ref, *refs):
        ins, (s0_ref, of_ref, ob_ref, st_ref, s_s) = refs[:10], refs[10:]

        @pl.when(pl.program_id(1) == 0)
        def _():
            s_s[...] = s0_ref[...]

        for d, o_ref in ((0, of_ref), (1, ob_ref)):
            for h in range(RET_HEADS):
                _, _, mask, qd, kd, gc = _decays(lg_ref[d, h], d)
                q, k, v = _ret_head(ins[5 * d:5 * d + 5], h)
                s = s_s[d, h]
                st_ref[h, d] = s.astype(BF)
                sc = _dot(q, k, 1, 1) * mask
                o_ref[:, h * RET_DV:(h + 1) * RET_DV] = (_dot(sc, v) + _dot(q * qd, s)).astype(BF)
                s_s[d, h] = s * gc + _dot(k * kd, v, 0, 0)

    return _call(
        body, [lg] + [px] * 10 + [states0], comm, name="ret_fwd", grid=(n_samp, nc),
        in_specs=[SMEM] + _ret_specs(row_f, row_b) + [
            pl.BlockSpec((None, 2, RET_HEADS, RET_DK, RET_DV), lambda b, n: (b, 0, 0, 0, 0))],
        out_specs=(pl.BlockSpec((c, wo), lambda b, n: (row_f(b, n), 0)),
                   pl.BlockSpec((c, wo), lambda b, n: (row_b(b, n), 0)),
                   pl.BlockSpec((None, RET_HEADS, 2, None, RET_DK, RET_DV), lambda b, n: (b, 0, 0, n, 0, 0))),
        out_shape=(SDS((t_lat, wo), BF), SDS((t_lat, wo), BF),
                   SDS((n_samp, RET_HEADS, 2, nc, RET_DK, RET_DV), BF)),
        scratch_shapes=[pltpu.VMEM((2, RET_HEADS, RET_DK, RET_DV), F32)],
        compiler_params=_cp(("arbitrary", "arbitrary"), 48))


def _ret_bwd(dpx, px, do, saved, lg, n_samp, seq, comm):
    c = RET_CHUNK
    nc = seq // c
    assert nc % 2 == 0
    wq, wo = RET_HEADS * RET_DK, RET_HEADS * RET_DV

    def row_f(b, n):
        return b * nc + nc - 1 - n

    def row_b(b, n):
        return b * nc + n

    def body(lg_ref, *refs):
        ins = refs[:10]
        (dof_ref, dob_ref, st_ref, dpx_in, dpx_hbm, ds0_ref, dlg_ref,
         ds_s, acc_s, sq_s, sk_s, sv_s, sems) = refs[10:]
        b, n = pl.program_id(0), pl.program_id(1)
        second = n >= nc // 2
        chunks = (nc - 1 - n, n)

        def parked(ch):
            return pl.ds(pl.multiple_of(ch * c, c), c)

        def flush():
            cps = []
            for d, ch in enumerate(chunks):
                rows = pl.ds(pl.multiple_of((b * nc + ch) * c, c), c)
                cps += [pltpu.make_async_copy(sq_s.at[parked(ch), :], dpx_hbm.at[rows, pl.ds(C_RQ, wq)], sems.at[3 * d]),
                        pltpu.make_async_copy(sk_s.at[parked(ch), :], dpx_hbm.at[rows, pl.ds(C_RK, wq)],
                                              sems.at[3 * d + 1]),
                        pltpu.make_async_copy(sv_s.at[parked(ch), :], dpx_hbm.at[rows, pl.ds(C_RV, wo)],
                                              sems.at[3 * d + 2])]
            return cps

        @pl.when(jnp.logical_or(n > nc // 2, jnp.logical_and(n == 0, b > 0)))
        def _():
            for cp in flush():
                cp.wait()

        @pl.when(n == 0)
        def _():
            ds_s[...] = jnp.zeros_like(ds_s)
            acc_s[...] = jnp.zeros_like(acc_s)

        def chains(first_visit):
            for d, do_ref in enumerate((dof_ref, dob_ref)):
                rows = parked(chunks[d])
                for h in range(RET_HEADS):
                    a_i, rel, mask, qd, kd, gc = _decays(lg_ref[d, h], d)
                    q, k, v = _ret_head(ins[5 * d:5 * d + 5], h)
                    qb, kb, vb = q.astype(BF), k.astype(BF), v.astype(BF)
                    cq, cv = slice(h * RET_DK, (h + 1) * RET_DK), slice(h * RET_DV, (h + 1) * RET_DV)
                    dob = do_ref[:, cv].astype(BF)
                    sb = st_ref[h, d]
                    ds = ds_s[d, h]
                    dsb = ds.astype(BF)
                    raw = _dot(qb, kb, 1, 1)
                    sc = raw * mask
                    dsc = _dot(dob, vb, 1, 1) * mask
                    dscb = dsc.astype(BF)
                    x = _dot(dob, sb, 1, 1)
                    y = _dot(vb, dsb, 1, 1)
                    qq = q * qd
                    kk = k * kd
                    dq = _dot(dscb, kb) + x * qd
                    dk = _dot(dscb, qb, 0, 0) + y * kd
                    dv = _dot(sc, dob, 0, 0) + _dot(kk, dsb)
                    if first_visit:
                        sq_s[rows, cq] = dq.astype(BF)
                        sk_s[rows, cq] = dk.astype(BF)
                        sv_s[rows, cv] = dv.astype(BF)
                    else:
                        sq_s[rows, cq] = (sq_s[rows, cq].astype(F32) + dq).astype(BF)
                        sk_s[rows, cq] = ((sk_s[rows, cq].astype(F32) + dk) * (RET_DK ** -0.5)).astype(BF)
                        sv_s[rows, cv] = (sv_s[rows, cv].astype(F32) + dv).astype(BF)
                    t = (_sum_all(dsc * raw * rel) + _sum_all((a_i + 1.0) * qq * x)
                         + _sum_all((c - 1.0 - a_i) * kk * y) + c * gc * _sum_all(ds * sb.astype(F32)))
                    acc_s[4 * d + h:4 * d + h + 1, :] += t
                    ds_s[d, h] = ds * gc + _dot(qq, dob, 0, 0)

        @pl.when(jnp.logical_not(second))
        def _():
            chains(True)

        @pl.when(second)
        def _():
            chains(False)
            for cp in flush():
                cp.start()

        @pl.when(n == nc - 1)
        def _():
            ds0_ref[...] = ds_s[...]
            dlg_ref[...] = acc_s[...]

        @pl.when(jnp.logical_and(b == n_samp - 1, n == nc - 1))
        def _():
            for cp in flush():
                cp.wait()

    do_spec_f = pl.BlockSpec((c, wo), lambda b, n: (row_f(b, n), 0))
    do_spec_b = pl.BlockSpec((c, wo), lambda b, n: (row_b(b, n), 0))
    return _call(
        body, [lg] + [px] * 10 + [do, do, saved, dpx], comm, name="ret_bwd", grid=(n_samp, nc), aliases={14: 0},
        in_specs=[SMEM] + _ret_specs(row_f, row_b) + [
            do_spec_f, do_spec_b,
            pl.BlockSpec((None, RET_HEADS, 2, None, RET_DK, RET_DV), lambda b, n: (b, 0, 0, nc - 1 - n, 0, 0)),
            ANY],
        out_specs=(ANY,
                   pl.BlockSpec((None, 2, RET_HEADS, RET_DK, RET_DV), lambda b, n: (b, 0, 0, 0, 0)),
                   pl.BlockSpec((None, 8, 128), lambda b, n: (b, 0, 0))),
        out_shape=(SDS(dpx.shape, dpx.dtype),
                   SDS((n_samp, 2, RET_HEADS, RET_DK, RET_DV), F32), SDS((n_samp, 8, 128), F32)),
        scratch_shapes=[pltpu.VMEM((2, RET_HEADS, RET_DK, RET_DV), F32), pltpu.VMEM((8, 128), F32),
                        pltpu.VMEM((seq, wq), BF), pltpu.VMEM((seq, wq), BF), pltpu.VMEM((seq, wo), BF),
                        pltpu.SemaphoreType.DMA((6,))],
        compiler_params=_cp(("arbitrary", "arbitrary"), 60))


def _norm_rope(x, w, cos, sin):
    xn = x * _rms(x) * w
    return xn * cos + _swap_pairs(xn) * sin


def _norm_rope_bwd(dy, x, w, cos, sin):
    dxn = dy * cos + _swap_pairs(dy * sin)
    r = _rms(x)
    xh = x * r
    return _rms_bwd(dxn * w, xh, r), jnp.sum(dxn * xh, axis=0, keepdims=True)


def _att_prep_q(px, cos_all, sin_all, qnw, t_lat):
    hd = ATT_HEAD_DIM
    wblk = ATT_REP * hd

    def body(x_ref, cos_ref, sin_ref, w_ref, o_ref):
        for r in range(ATT_REP):
            cols = slice(r * hd, (r + 1) * hd)
            qr = _norm_rope(x_ref[:, cols].astype(F32), w_ref[...], cos_ref[...], sin_ref[...])
            o_ref[:, cols] = (qr * (hd ** -0.5)).astype(BF)

    return pl.pallas_call(
        body, name="att_prep_q", grid=(t_lat // TM, ATT_KV_HEADS),
        in_specs=[pl.BlockSpec((TM, wblk), lambda i, g: (i, C_AQ // wblk + g)),
                  pl.BlockSpec((TM, hd), lambda i, g: (i, 0)),
                  pl.BlockSpec((TM, hd), lambda i, g: (i, 0)),
                  pl.BlockSpec((1, hd), lambda i, g: (0, 0))],
        out_specs=pl.BlockSpec((TM, wblk), lambda i, g: (i, g)),
        out_shape=SDS((t_lat, ATT_HEADS * hd), BF),
        compiler_params=_cp(("parallel", "parallel")))(px, cos_all, sin_all, qnw)


def _att_prep_kv(px, cos_all, sin_all, knw):
    rows = px.shape[0]
    hd = ATT_HEAD_DIM
    kvw = ATT_KV_HEADS * hd

    def body(x_ref, cos_ref, sin_ref, w_ref, k_ref, v_ref):
        for g in range(ATT_KV_HEADS):
            cols = slice(g * hd, (g + 1) * hd)
            k_ref[:, cols] = _norm_rope(x_ref[:, cols].astype(F32), w_ref[...], cos_ref[...],
                                        sin_ref[...]).astype(BF)
            v_ref[:, 2 * g * hd:(2 * g + 1) * hd] = x_ref[:, kvw + g * hd:kvw + (g + 1) * hd].astype(BF)
            v_ref[:, (2 * g + 1) * hd:(2 * g + 2) * hd] = jnp.ones((TM, hd), BF)

    return pl.pallas_call(
        body, name="att_prep_kv", grid=(rows // TM,),
        in_specs=[pl.BlockSpec((TM, 2 * kvw), lambda i: (i, C_AK // (2 * kvw))),
                  pl.BlockSpec((TM, hd), lambda i: (i, 0)),
                  pl.BlockSpec((TM, hd), lambda i: (i, 0)),
                  pl.BlockSpec((1, hd), lambda i: (0, 0))],
        out_specs=(pl.BlockSpec((TM, kvw), lambda i: (i, 0)), pl.BlockSpec((TM, 2 * kvw), lambda i: (i, 0))),
        out_shape=(SDS((rows, kvw), BF), SDS((rows, 2 * kvw), BF)),
        compiler_params=_cp(("parallel",)))(px, cos_all, sin_all, knw)


def _att_kv_bwd(dpx, dkl, dkc, dvl, dvc, px, cos_all, sin_all, knw):
    rows = px.shape[0]
    hd = ATT_HEAD_DIM
    kvw = ATT_KV_HEADS * hd
    n_lat = dkl.shape[0] // TM
    assert dkc.shape[0] == TM

    def body(dpx_hbm, dkl_ref, dkc_ref, dvl_ref, dvc_ref, x_ref, cos_ref, sin_ref, w_ref, o_ref, gw_ref):
        i = pl.program_id(0)

        @pl.when(i == 0)
        def _():
            gw_ref[...] = jnp.zeros_like(gw_ref)

        is_lat = i < n_lat
        dk = jnp.where(is_lat, dkl_ref[...], dkc_ref[...])
        dv = jnp.where(is_lat, dvl_ref[...], dvc_ref[...])
        for g in range(ATT_KV_HEADS):
            cols = slice(g * hd, (g + 1) * hd)
            dx, gw = _norm_rope_bwd(dk[:, cols], x_ref[:, cols].astype(F32), w_ref[...], cos_ref[...], sin_ref[...])
            o_ref[:, cols] = dx.astype(BF)
            gw_ref[...] += gw
        o_ref[:, kvw:] = dv.astype(BF)

    lat = pl.BlockSpec((TM, kvw), lambda i: (jnp.minimum(i, n_lat - 1), 0))
    ctx = pl.BlockSpec((TM, kvw), lambda i: (0, 0))
    kvcol = pl.BlockSpec((TM, 2 * kvw), lambda i: (i, C_AK // (2 * kvw)))
    return pl.pallas_call(
        body, name="att_kv_bwd", grid=(rows // TM,), input_output_aliases={0: 0},
        in_specs=[ANY, lat, ctx, lat, ctx, kvcol,
                  pl.BlockSpec((TM, hd), lambda i: (i, 0)),
                  pl.BlockSpec((TM, hd), lambda i: (i, 0)),
                  pl.BlockSpec((1, hd), lambda i: (0, 0))],
        out_specs=(kvcol, pl.BlockSpec((1, hd), lambda i: (0, 0))),
        out_shape=(SDS(dpx.shape, dpx.dtype), SDS((1, hd), F32)),
        compiler_params=_cp(("arbitrary",)))(dpx, dkl, dkc, dvl, dvc, px, cos_all, sin_all, knw)


def _stack_heads(ref_or_val):
    hd = ATT_HEAD_DIM
    return jnp.concatenate([ref_or_val[:, r * hd:(r + 1) * hd] for r in range(ATT_REP)], axis=0)


def _att_scores(q, kl, kc):
    sl = _dot(q, kl, 1, 1)
    sc = _dot(q, kc, 1, 1)
    m = jnp.maximum(jnp.max(sl, axis=-1, keepdims=True), jnp.max(sc, axis=-1, keepdims=True))
    return jnp.exp(sl - m), jnp.exp(sc - m), m


def _att_fwd(qn, kn, vn, n_samp, seq, lc, comm):
    hd = ATT_HEAD_DIM
    tq = ATT_TQ
    nq = seq // tq
    wblk = ATT_REP * hd
    cb = n_samp * seq // lc
    t_lat = n_samp * seq

    def body(q_ref, kl_ref, kc_ref, vl_ref, vc_ref, o_ref, lse_ref):
        lane = lax.broadcasted_iota(jnp.int32, (tq, hd), 1)
        lse = jnp.zeros((tq, hd), F32)
        for r in range(ATT_REP):
            cols = slice(r * hd, (r + 1) * hd)
            el, ec, m = _att_scores(q_ref[:, cols], kl_ref[...], kc_ref[...])
            pv = _dot(el, vl_ref[...]) + _dot(ec, vc_ref[...])
            denom = pv[:, hd:hd + 1]
            o_ref[:, cols] = (pv[:, :hd] / denom).astype(BF)
            lse = jnp.where(lane == r, m + jnp.log(denom), lse)
        lse_ref[...] = lse

    return _call(
        body, [qn, kn, kn, vn, vn], comm, name="att_fwd", grid=(n_samp, ATT_KV_HEADS, nq),
        in_specs=[pl.BlockSpec((tq, wblk), lambda b, g, i: (b * nq + i, g)),
                  pl.BlockSpec((seq, hd), lambda b, g, i: (b, g)),
                  pl.BlockSpec((lc, hd), lambda b, g, i: (cb + b, g)),
                  pl.BlockSpec((seq, 2 * hd), lambda b, g, i: (b, g)),
                  pl.BlockSpec((lc, 2 * hd), lambda b, g, i: (cb + b, g))],
        out_specs=(pl.BlockSpec((tq, wblk), lambda b, g, i: (b * nq + i, g)),
                   pl.BlockSpec((tq, hd), lambda b, g, i: (b * nq + i, g))),
        out_shape=(SDS((t_lat, ATT_HEADS * hd), BF), SDS((t_lat, ATT_KV_HEADS * hd), F32)),
        compiler_params=_cp(("arbitrary", "arbitrary", "arbitrary"), 48))


def _att_bwd(dpx, qn, kn, vn, px, o_att, lse, do_att, cos_all, sin_all, qnw, n_samp, seq, lc, comm):
    hd = ATT_HEAD_DIM
    tq = ATT_TQ
    nq = seq // tq
    wblk = ATT_REP * hd
    cb = n_samp * seq // lc
    t_lat = n_samp * seq
    kvw = ATT_KV_HEADS * hd
    scale = hd ** -0.5

    def body(dpx_hbm, q_ref, kl_ref, kc_ref, vl_ref, vc_ref, o_ref, do_ref, x_ref, cos_ref, sin_ref, w_ref,
             lse_ref, dq_ref, dkl_ref, dkc_ref, dvl_ref, dvc_ref, gw_ref, akl, akc, avl, avc, aw):
        i = pl.program_id(2)

        @pl.when(i == 0)
        def _():
            akl[...] = jnp.zeros_like(akl)
            akc[...] = jnp.zeros_like(akc)
            avl[...] = jnp.zeros_like(avl)
            avc[...] = jnp.zeros_like(avc)
            aw[...] = jnp.zeros_like(aw)

        dobs, pls, pcs, dsls, dscs = [], [], [], [], []
        for r in range(ATT_REP):
            cols = slice(r * hd, (r + 1) * hd)
            dob = do_ref[:, cols]
            delta = jnp.sum(dob.astype(F32) * o_ref[:, cols].astype(F32), axis=-1, keepdims=True)
            lse = lse_ref[:, r:r + 1]
            p_l = jnp.exp(_dot(q_ref[:, cols], kl_ref[...], 1, 1) - lse).astype(BF)
            p_c = jnp.exp(_dot(q_ref[:, cols], kc_ref[...], 1, 1) - lse).astype(BF)
            ds_l = (p_l * (_dot(dob, vl_ref[...], 1, 1) - delta)).astype(BF)
            ds_c = (p_c * (_dot(dob, vc_ref[...], 1, 1) - delta)).astype(BF)
            dq = (_dot(ds_l, kl_ref[...]) + _dot(ds_c, kc_ref[...])) * scale
            dx, gw = _norm_rope_bwd(dq, x_ref[:, cols].astype(F32), w_ref[...], cos_ref[...], sin_ref[...])
            dq_ref[:, cols] = dx.astype(BF)
            aw[...] += gw
            dobs.append(dob)
            pls.append(p_l)
            pcs.append(p_c)
            dsls.append(ds_l)
            dscs.append(ds_c)
        do4 = jnp.concatenate(dobs, axis=0)
        q4 = _stack_heads(q_ref)
        avl[...] += _dot(jnp.concatenate(pls, axis=0), do4, 0, 0)
        avc[...] += _dot(jnp.concatenate(pcs, axis=0), do4, 0, 0)
        akl[...] += _dot(jnp.concatenate(dsls, axis=0), q4, 0, 0)
        akc[...] += _dot(jnp.concatenate(dscs, axis=0), q4, 0, 0)

        @pl.when(i == nq - 1)
        def _():
            dkl_ref[...] = akl[...]
            dkc_ref[...] = akc[...]
            dvl_ref[...] = avl[...]
            dvc_ref[...] = avc[...]
            gw_ref[...] = aw[...]

    return _call(
        body, [dpx, qn, kn, kn, vn, vn, o_att, do_att, px, cos_all, sin_all, qnw, lse], comm,
        name="att_bwd", grid=(n_samp, ATT_KV_HEADS, nq), aliases={0: 0},
        in_specs=[ANY,
                  pl.BlockSpec((tq, wblk), lambda b, g, i: (b * nq + i, g)),
                  pl.BlockSpec((seq, hd), lambda b, g, i: (b, g)),
                  pl.BlockSpec((lc, hd), lambda b, g, i: (cb + b, g)),
                  pl.BlockSpec((seq, hd), lambda b, g, i: (b, 2 * g)),
                  pl.BlockSpec((lc, hd), lambda b, g, i: (cb + b, 2 * g)),
                  pl.BlockSpec((tq, wblk), lambda b, g, i: (b * nq + i, g)),
                  pl.BlockSpec((tq, wblk), lambda b, g, i: (b * nq + i, g)),
                  pl.BlockSpec((tq, wblk), lambda b, g, i: (b * nq + i, C_AQ // wblk + g)),
                  pl.BlockSpec((tq, hd), lambda b, g, i: (b * nq + i, 0)),
                  pl.BlockSpec((tq, hd), lambda b, g, i: (b * nq + i, 0)),
                  pl.BlockSpec((1, hd), lambda b, g, i: (0, 0)),
                  pl.BlockSpec((tq, hd), lambda b, g, i: (b * nq + i, g))],
        out_specs=(pl.BlockSpec((tq, wblk), lambda b, g, i: (b * nq + i, C_AQ // wblk + g)),
                   pl.BlockSpec((seq, hd), lambda b, g, i: (b, g)),
                   pl.BlockSpec((lc, hd), lambda b, g, i: (b, g)),
                   pl.BlockSpec((seq, hd), lambda b, g, i: (b, g)),
                   pl.BlockSpec((lc, hd), lambda b, g, i: (b, g)),
                   pl.BlockSpec((None, None, 1, hd), lambda b, g, i: (b, g, 0, 0))),
        out_shape=(SDS(dpx.shape, dpx.dtype),
                   SDS((t_lat, kvw), F32), SDS((n_samp * lc, kvw), F32),
                   SDS((t_lat, kvw), F32), SDS((n_samp * lc, kvw), F32),
                   SDS((n_samp, ATT_KV_HEADS, 1, hd), F32)),
        scratch_shapes=[pltpu.VMEM((seq, hd), F32), pltpu.VMEM((lc, hd), F32),
                        pltpu.VMEM((seq, hd), F32), pltpu.VMEM((lc, hd), F32), pltpu.VMEM((1, hd), F32)],
        compiler_params=_cp(("arbitrary", "arbitrary", "arbitrary"), 56))


def _merge(x_lat, target, o_f, o_b, o_att, px, gate3, w_o_ret, w_o_att, w_out, tiles_per_sample):
    t_lat = x_lat.shape[0]
    tm = 256
    n_t = t_lat // tm
    per = tiles_per_sample * (TM // tm)
    d = D_MODEL
    rv = RET_HEADS * RET_DV
    n_samp = gate3.shape[0] - 1

    half = d // 2
    n_px = 10

    def body(x_ref, t_ref, of_ref, ob_ref, oa_ref, *rest):
        pxs, rest = rest[:n_px], rest[n_px:]
        (gt_ref, wor_ref, woa_ref, wout_ref,
         gx_ref, dor_ref, doa_ref, dpx_hbm, loss_ref, dgt_ref, gwor_hbm, gwoa_hbm, gwout_hbm,
         aor, aoa, aout, drg_ref, dtail_ref, sems) = rest
        i = pl.program_id(0)

        def copies(step):
            rows = pl.ds(pl.multiple_of(step * tm, tm), tm)
            return (pltpu.make_async_copy(drg_ref, dpx_hbm.at[rows, pl.ds(C_RG, rv)], sems.at[0]),
                    pltpu.make_async_copy(dtail_ref, dpx_hbm.at[rows, pl.ds(C_AG, 3 * d)], sems.at[1]))

        @pl.when(i == 0)
        def _():
            aor[...] = jnp.zeros_like(aor)
            aoa[...] = jnp.zeros_like(aoa)
            aout[...] = jnp.zeros_like(aout)
            loss_ref[...] = jnp.zeros_like(loss_ref)

        @pl.when(i % per == 0)
        def _():
            dgt_ref[...] = jnp.zeros_like(dgt_ref)

        def cat(refs):
            return jnp.concatenate([r[...] for r in refs], axis=1).astype(F32)

        def ret_head(h):
            cols = slice(h * RET_DV, (h + 1) * RET_DV)
            o = of_ref[:, cols].astype(F32) + ob_ref[:, cols].astype(F32)
            r = _rms(o)
            g = pxs[h][...].astype(F32)
            return o * r, r, g, _sigmoid(g)

        def att_half(k):
            o = oa_ref[:, k * half:(k + 1) * half].astype(F32)
            g = pxs[4 + k][...].astype(F32)
            return o, g, _sigmoid(g)

        yrs = []
        for h in range(RET_HEADS):
            on, _, g, sg = ret_head(h)
            yrs.append((on * (g * sg)).astype(BF))
        yr = jnp.concatenate(yrs, axis=1)
        yas = []
        for k in range(2):
            o, g, sg = att_half(k)
            yas.append((o * (g * sg)).astype(BF))
        ya = jnp.concatenate(yas, axis=1)

        a = jnp.dot(yr, wor_ref[...], preferred_element_type=F32)
        b = jnp.dot(ya, woa_ref[...], preferred_element_type=F32)
        sr = _sigmoid(cat(pxs[6:8]))
        sa = _sigmoid(cat(pxs[8:10]))
        yb = (sr * a + sa * b).astype(BF)
        out = jnp.dot(yb, wout_ref[...], preferred_element_type=F32)
        gate = gt_ref[...]
        err = x_ref[...] + gate * out - t_ref[...]
        loss_ref[...] += 0.5 * _sum_all(err * err) * (1.0 / d)
        dy_tok = err * (1.0 / d)
        gx_ref[...] = dy_tok
        dgt_ref[...] += jnp.sum(dy_tok * out, axis=0, keepdims=True)
        dout = (dy_tok * gate).astype(BF)
        aout[...] += _dot(yb, dout, 0, 0)
        dyy = _dot(dout, wout_ref[...], 1, 1)
        da = (dyy * sr).astype(BF)
        db = (dyy * sa).astype(BF)
        aor[...] += _dot(yr, da, 0, 0)
        aoa[...] += _dot(ya, db, 0, 0)
        dyr = _dot(da, wor_ref[...], 1, 1)
        dya = _dot(db, woa_ref[...], 1, 1)

        @pl.when(i > 0)
        def _():
            for cp in copies(i - 1):
                cp.wait()

        dtail_ref[:, d:2 * d] = (dyy * a * (sr * (1.0 - sr))).astype(BF)
        dtail_ref[:, 2 * d:] = (dyy * b * (sa * (1.0 - sa))).astype(BF)
        for h in range(RET_HEADS):
            cols = slice(h * RET_DV, (h + 1) * RET_DV)
            on, r, g, sg = ret_head(h)
            dy = dyr[:, cols]
            drg_ref[:, cols] = (dy * on * (sg * (1.0 + g * (1.0 - sg)))).astype(BF)
            dor_ref[:, cols] = _rms_bwd(dy * (g * sg), on, r).astype(BF)
        for k in range(2):
            cols = slice(k * half, (k + 1) * half)
            o, g, sg = att_half(k)
            dy = dya[:, cols]
            dtail_ref[:, cols] = (dy * o * (sg * (1.0 + g * (1.0 - sg)))).astype(BF)
            doa_ref[:, cols] = (dy * (g * sg)).astype(BF)
        for cp in copies(i):
            cp.start()

        @pl.when(i == n_t - 1)
        def _():
            for cp in copies(i):
                cp.wait()
            pltpu.sync_copy(aor, gwor_hbm)
            pltpu.sync_copy(aoa, gwoa_hbm)
            pltpu.sync_copy(aout, gwout_hbm)

    def px_blk(col):
        return pl.BlockSpec((tm, half), lambda i: (i, col // half))

    def resident(shape):
        return pl.BlockSpec(shape, lambda i: (0, 0), pipeline_mode=pl.Buffered(1))

    px_cols = ([C_RG + k * half for k in range(4)] + [C_AG, C_AG + half]
               + [C_MR, C_MR + half, C_MA, C_MA + half])
    return pl.pallas_call(
        body, name="merge", grid=(n_t,),
        in_specs=[pl.BlockSpec((tm, d), lambda i: (i, 0)),
                  pl.BlockSpec((tm, d), lambda i: (i, 0)),
                  pl.BlockSpec((tm, rv), lambda i: (i, 0)),
                  pl.BlockSpec((tm, rv), lambda i: (i, 0)),
                  pl.BlockSpec((tm, d), lambda i: (i, 0))]
        + [px_blk(col) for col in px_cols]
        + [pl.BlockSpec((None, 1, d), lambda i: (i // per, 0, 0)),
           resident((rv, d)), resident((d, d)), resident((d, d))],
        out_specs=(pl.BlockSpec((tm, d), lambda i: (i, 0)),
                   pl.BlockSpec((tm, rv), lambda i: (i, 0)),
                   pl.BlockSpec((tm, d), lambda i: (i, 0)),
                   ANY,
                   pl.BlockSpec((8, 128), lambda i: (0, 0)),
                   pl.BlockSpec((None, 1, d), lambda i: (i // per, 0, 0)),
                   ANY, ANY, ANY),
        out_shape=(SDS((t_lat, d), F32), SDS((t_lat, rv), BF), SDS((t_lat, d), BF),
                   SDS((px.shape[0], IN_COLS), BF),
                   SDS((8, 128), F32), SDS((n_samp, 1, d), F32),
                   SDS((rv, d), F32), SDS((d, d), F32), SDS((d, d), F32)),
        scratch_shapes=[pltpu.VMEM((rv, d), F32), pltpu.VMEM((d, d), F32), pltpu.VMEM((d, d), F32),
                        pltpu.VMEM((tm, rv), BF), pltpu.VMEM((tm, 3 * d), BF), pltpu.SemaphoreType.DMA((2,))],
        compiler_params=_cp(("arbitrary",), 56))(
            x_lat, target, o_f, o_b, o_att, *([px] * n_px), gate3, w_o_ret, w_o_att, w_out)


def _place():
    x, y, c = lax.axis_index("x"), lax.axis_index("y"), lax.axis_index("c")
    chips = [(1 - x, y), (x, 1 - y), (1 - x, 1 - y)]
    return x, y, c, chips


def _remote(src, dst, send_sem, recv_sem, to):
    return pltpu.make_async_remote_copy(src_ref=src, dst_ref=dst, send_sem=send_sem, recv_sem=recv_sem,
                                        device_id=to, device_id_type=MESH)


def _place_ids():
    x, y, c = lax.axis_index("x"), lax.axis_index("y"), lax.axis_index("c")
    me = 2 * x + y
    return jnp.stack([x, y, c, me, me, 2 * (1 - x) + y, 2 * x + 1 - y, 2 * (1 - x) + 1 - y]).astype(jnp.int32)


def _ag_comm(bufs):
    n, m = len(bufs), 3

    def half(ref, s, which):
        h = ref.shape[1] // 2
        return ref.at[s, pl.ds(which * h, h), :]

    def ici(ins, outs, ssem, rsem, base):
        x, y, c, chips = _place()
        sends, recvs = [], []
        for a in range(n):
            for j in range(m):
                k, chip = base + a * m + j, chips[j]
                mine, theirs = half(outs[a], 2 * x + y, c), half(outs[a], 2 * chip[0] + chip[1], c)
                sends.append(_remote(mine, mine, ssem.at[k], rsem.at[k], (*chip, c)))
                recvs.append(_remote(theirs, theirs, ssem.at[k], rsem.at[k], (*chip, c)))
        return sends, recvs

    def d2d(ins, outs, ssem, rsem, base):
        x, y, c, chips = _place()
        sends, recvs = [], []
        for a in range(n):
            for j in range(m):
                k, s = base + (n + a) * m + j, 2 * chips[j][0] + chips[j][1]
                sends.append(_remote(half(outs[a], s, c), half(outs[a], s, c), ssem.at[k], rsem.at[k], (x, y, 1 - c)))
                recvs.append(_remote(half(outs[a], s, 1 - c), half(outs[a], s, 1 - c), ssem.at[k], rsem.at[k],
                                     (x, y, 1 - c)))
        return sends, recvs

    return _Comm("all_gather", tuple(bufs), tuple(SDS(b.shape, b.dtype) for b in bufs), {a: a for a in range(n)},
                 2 * n * m, (ici, d2d))


def _swap_comm(grads):
    n = len(grads)

    def phase(ins, outs, ssem, rsem, base):
        x, y, c, _ = _place()
        sends = []
        for a in range(n):
            h = ins[a].shape[1] // 2
            sends.append(_remote(ins[a].at[:, pl.ds((1 - c) * h, h), :], outs[a], ssem.at[base + a],
                                 rsem.at[base + a], (x, y, 1 - c)))
        return sends, sends

    return _Comm("swap_halves", tuple(grads),
                 tuple(SDS((g.shape[0], g.shape[1] // 2, g.shape[2]), g.dtype) for g in grads), {}, n, (phase,))


def _exchange_comm(parts):
    n = len(parts)

    def phase(ins, outs, ssem, rsem, base):
        x, y, c, chips = _place()
        sends = []
        for a in range(n):
            for j, chip in enumerate(chips):
                k = base + 3 * a + j
                sends.append(_remote(ins[a].at[2 * chip[0] + chip[1]], outs[a].at[j], ssem.at[k], rsem.at[k],
                                     (*chip, c)))
        return sends, sends

    return _Comm("exchange_shards", tuple(parts), tuple(SDS((3,) + p.shape[1:], p.dtype) for p in parts), {}, 3 * n,
                 (phase,))


def _join_comm(bufs):
    n = len(bufs)

    def phase(ins, outs, ssem, rsem, base):
        x, y, c, _ = _place()
        sends, recvs = [], []
        for a in range(n):
            h = outs[a].shape[0] // 2
            mine, other = outs[a].at[pl.ds(c * h, h), :], outs[a].at[pl.ds((1 - c) * h, h), :]
            sends.append(_remote(mine, mine, ssem.at[base + a], rsem.at[base + a], (x, y, 1 - c)))
            recvs.append(_remote(other, other, ssem.at[base + a], rsem.at[base + a], (x, y, 1 - c)))
        return sends, recvs

    return _Comm("join_halves", tuple(bufs), tuple(SDS(b.shape, b.dtype) for b in bufs), {a: a for a in range(n)},
                 n, (phase,))


def _cast_place(w, ids):
    rows, cols = w.shape
    tr = min(rows, 256)

    def body(ids_ref, w_ref, o_ref):
        o_ref[...] = w_ref[...].astype(BF)

    return pl.pallas_call(
        body, name="cast_place",
        grid_spec=pltpu.PrefetchScalarGridSpec(
            num_scalar_prefetch=1, grid=(rows // tr,),
            in_specs=[pl.BlockSpec((tr, cols), lambda i, ids_ref: (i, 0))],
            out_specs=pl.BlockSpec((None, tr, cols), lambda i, ids_ref: (ids_ref[3], i, 0))),
        out_shape=SDS((N_SHARD, rows, cols), BF),
        compiler_params=_cp(("parallel",), 40))(ids, w)


def _chip_sum(g, p, ids):
    n_s, rows, cols = g.shape
    h = rows // 2
    tr = min(h, 256)
    nb = h // tr

    def body(ids_ref, g_ref, p_ref, o_ref, o16_ref):
        t = g_ref[...] + p_ref[...]
        o_ref[...] = t
        o16_ref[...] = t.astype(BF)

    out_spec = pl.BlockSpec((None, tr, cols), lambda s, i, ids_ref: (s, i, 0))
    return pl.pallas_call(
        body, name="chip_sum",
        grid_spec=pltpu.PrefetchScalarGridSpec(
            num_scalar_prefetch=1, grid=(n_s, nb),
            in_specs=[pl.BlockSpec((None, tr, cols), lambda s, i, ids_ref: (s, ids_ref[2] * nb + i, 0)),
                      pl.BlockSpec((None, tr, cols), lambda s, i, ids_ref: (s, i, 0))],
            out_specs=(out_spec, out_spec)),
        out_shape=(SDS((n_s, h, cols), g.dtype), SDS((n_s, h, cols), BF)),
        compiler_params=_cp(("parallel", "parallel"), 40))(ids, g, p)


def _shard_sum(t, q, ids):
    _, h, cols = t.shape
    tr = min(h, 256)
    nb = h // tr

    def body(ids_ref, t_ref, q_ref, o_ref):
        o_ref[...] = ((t_ref[...] + q_ref[0].astype(F32)) + q_ref[1].astype(F32)) + q_ref[2].astype(F32)

    return pl.pallas_call(
        body, name="shard_sum",
        grid_spec=pltpu.PrefetchScalarGridSpec(
            num_scalar_prefetch=1, grid=(nb,),
            in_specs=[pl.BlockSpec((None, tr, cols), lambda i, ids_ref: (ids_ref[3], i, 0)),
                      pl.BlockSpec((3, tr, cols), lambda i, ids_ref: (0, i, 0))],
            out_specs=pl.BlockSpec((tr, cols), lambda i, ids_ref: (ids_ref[2] * nb + i, 0))),
        out_shape=SDS((2 * h, cols), t.dtype),
        compiler_params=_cp(("parallel",), 40))(ids, t, q)


def _gather_small(block, n_sum):
    rows, cols = block.shape
    n_dev = 8

    def body(x_ref, *refs):
        g_ref, buf, send_sems, recv_sems, local_sem = refs[-5:]
        x, y, c, chips = _place()
        me, sibling = (x, y, c), (x, y, 1 - c)

        def slot(px_, py_, pc_):
            return buf.at[4 * px_ + 2 * py_ + pc_]

        def copy(k, who, to, src=None):
            return _remote(slot(*who) if src is None else src, slot(*who), send_sems.at[k], recv_sems.at[k], to)

        mine = pltpu.make_async_copy(x_ref, slot(*me), local_sem)
        mine.start()
        first = [copy(0, me, sibling, src=x_ref)]
        first += [copy(1 + j, me, (*chip, c), src=x_ref) for j, chip in enumerate(chips)]
        for cp in first:
            cp.start()
        passed = [copy(4 + j, (*chip, c), sibling) for j, chip in enumerate(chips)]
        for j, chip in enumerate(chips):
            copy(1 + j, (*chip, c), me).wait_recv()
            passed[j].start()
        copy(0, sibling, me).wait_recv()
        for j, chip in enumerate(chips):
            copy(4 + j, (*chip, 1 - c), me).wait_recv()
        for cp in first + passed:
            cp.wait_send()
        mine.wait()
        if n_sum:
            acc = buf[0, :, :n_sum]
            for s in range(1, n_dev):
                acc = acc + buf[s, :, :n_sum]
            refs[0][...] = acc
        for s in range(n_dev):
            g_ref[s * rows:(s + 1) * rows, :] = buf[s, :, n_sum:]

    sum_shape = (SDS((rows, n_sum), F32),) if n_sum else ()
    return pl.pallas_call(
        body, name="gather_small",
        in_specs=[pl.BlockSpec(memory_space=pltpu.VMEM)],
        out_specs=tuple(pl.BlockSpec(memory_space=pltpu.VMEM) for _ in (*sum_shape, None)),
        out_shape=(*sum_shape, SDS((n_dev * rows, cols - n_sum), F32)),
        scratch_shapes=[pltpu.VMEM((n_dev, rows, cols), F32), pltpu.SemaphoreType.DMA((7,)),
                        pltpu.SemaphoreType.DMA((7,)), pltpu.SemaphoreType.DMA],
        compiler_params=_cp(has_side_effects=True))(block)


def _scatter_mod(part):
    _, rows, width = part.shape

    def body(p_ref, o_ref, send_sems, recv_sems):
        x, y, c, chips = _place()
        sends = [_remote(p_ref.at[4 * chip[0] + 2 * chip[1] + c], o_ref.at[2 * x + y], send_sems.at[j],
                         recv_sems.at[j], (*chip, c)) for j, chip in enumerate(chips)]
        recvs = [_remote(o_ref.at[2 * chip[0] + chip[1]], o_ref.at[2 * chip[0] + chip[1]], send_sems.at[j],
                         recv_sems.at[j], (*chip, c)) for j, chip in enumerate(chips)]
        for cp in sends:
            cp.start()
        o_ref[2 * x + y] = p_ref[4 * x + 2 * y + c]
        for cp in recvs:
            cp.wait_recv()
        for cp in sends:
            cp.wait_send()

    return pl.pallas_call(
        body, name="scatter_mod",
        in_specs=[pl.BlockSpec(memory_space=pltpu.VMEM)], out_specs=pl.BlockSpec(memory_space=pltpu.VMEM),
        out_shape=SDS((N_SHARD, rows, width), F32),
        scratch_shapes=[pltpu.SemaphoreType.DMA((3,)), pltpu.SemaphoreType.DMA((3,))],
        compiler_params=_cp(has_side_effects=True))(part)


def _adam_math(w, g, m, v):
    m = ADAM_B1 * m + (1.0 - ADAM_B1) * g
    v = ADAM_B2 * v + (1.0 - ADAM_B2) * (g * g)
    m_hat = m / (1.0 - ADAM_B1 ** ADAM_STEP)
    v_hat = v / (1.0 - ADAM_B2 ** ADAM_STEP)
    delta = -ADAM_LR * (m_hat / (jnp.sqrt(v_hat) + ADAM_EPS) + ADAM_WD * w)
    return delta, m, v


def _adamw(w, g, m, v):
    rows, cols = w.shape
    tr = min(rows, 256 if cols <= 2048 else 128)

    def body(w_ref, g_ref, m_ref, v_ref, go_ref, d_ref, nm_ref, nv_ref):
        g = g_ref[...]
        go_ref[...] = g
        d_ref[...], nm_ref[...], nv_ref[...] = _adam_math(w_ref[...], g, m_ref[...], v_ref[...])

    spec = pl.BlockSpec((tr, cols), lambda i: (i, 0))
    return pl.pallas_call(
        body, name="adamw", grid=(rows // tr,), in_specs=[spec] * 4, out_specs=(spec,) * 4,
        out_shape=(SDS(w.shape, F32),) * 4, compiler_params=_cp(("parallel",), 40))(w, g, m, v)


def _adamw_small(w, g, m, v):
    def body(w_ref, g_ref, m_ref, v_ref, go_ref, d_ref, nm_ref, nv_ref):
        w = w_ref[...]
        g = g_ref[...]
        sub = lax.broadcasted_iota(jnp.int32, w.shape, 0)
        lane = lax.broadcasted_iota(jnp.int32, w.shape, 1)
        is_ret = jnp.logical_and(sub == 5, lane < 2 * RET_HEADS)
        u = jnp.exp(jnp.where(is_ret, w, -1.0) * jnp.log(2.0))
        g = jnp.where(is_ret, g * (-u * jnp.log(2.0) / (1.0 - u)), g)
        go_ref[...] = g
        d_ref[...], nm_ref[...], nv_ref[...] = _adam_math(w, g, m_ref[...], v_ref[...])

    return pl.pallas_call(body, name="adamw_small", out_shape=(SDS(w.shape, F32),) * 4)(w, g, m, v)


def _rope_tables(seq, n_samp, n_ctx_rows):
    rows = seq // GRID_W
    row = jnp.repeat(jnp.arange(rows, dtype=F32), GRID_W)
    col = jnp.tile(jnp.arange(GRID_W, dtype=F32), rows)
    half = ATT_HEAD_DIM // 2
    freqs = ROPE_THETA ** (-jnp.arange(0, half, 2, dtype=F32) / half)
    ang = jnp.concatenate([row[:, None] * freqs, col[:, None] * freqs], axis=-1)
    cos, sin = jnp.cos(ang), jnp.sin(ang)
    cos_f = jnp.repeat(cos, 2, axis=1)
    sin_s = jnp.stack([-sin, sin], axis=-1).reshape(seq, ATT_HEAD_DIM)
    cos_all = jnp.concatenate([jnp.tile(cos_f, (n_samp, 1)), jnp.ones((n_ctx_rows, ATT_HEAD_DIM), F32)], axis=0)
    sin_all = jnp.concatenate([jnp.tile(sin_s, (n_samp, 1)), jnp.zeros((n_ctx_rows, ATT_HEAD_DIM), F32)], axis=0)
    return cos_all, sin_all


def _pack_small(c_ctx, norm_w, b_ada, ret, qn, kn):
    d = D_MODEL
    row5 = jnp.concatenate([ret.reshape(-1), jnp.zeros((128 - 2 * RET_HEADS,), F32), qn.reshape(-1), kn.reshape(-1),
                            jnp.zeros((d - 384,), F32)])
    return jnp.concatenate([c_ctx.reshape(1, d), norm_w.reshape(1, d), b_ada.reshape(3, d), row5.reshape(1, d),
                            jnp.zeros((2, d), F32)], axis=0)


def _unpack_small(p):
    d = D_MODEL
    return (p[0], p[1:2], p[2:5].reshape(1, 3 * d), p[5, :2 * RET_HEADS].reshape(1, 2, RET_HEADS),
            p[5:6, 128:256], p[5:6, 256:384])


def _step(x, c, ctx, c_ctx, norm_w, b_ada, ret_log2_decay, q_norm_w, k_norm_w, loss_target, weights, ids):
    n_samp, seq, d = x.shape
    lc = ctx.shape[1]
    t_lat, t_ctx = n_samp * seq, n_samp * lc
    assert seq % TM == 0 and t_ctx == TM and t_lat % lc == 0 and seq % GRID_W == 0
    tps = seq // TM

    x_lat = x.reshape(t_lat, d)
    x_ctx = ctx.reshape(t_ctx, d)
    cvec8 = jnp.concatenate([c, c_ctx.reshape(1, d), jnp.zeros((8 - n_samp - 1, d), F32)], axis=0)
    lg = jnp.log1p(-jnp.exp2(ret_log2_decay.reshape(2, RET_HEADS)))
    cos_all, sin_all = _rope_tables(seq, n_samp, t_ctx)

    w_ada_b, w_in_b, w_or_b, w_oa_b, w_out_b = weights
    (c_all,) = _gather_small(cvec8, 0)
    mod_part = _adaln_fwd(c_all, w_ada_b, b_ada, ids)
    mod8 = _scatter_mod(mod_part.reshape(-1, 8, ADA_W)).transpose(1, 0, 2).reshape(8, 3 * d)
    mod3 = mod8[:n_samp + 1]
    shift3 = mod3[:, None, 0:d]
    scale3 = mod3[:, None, d:2 * d]
    gate3 = mod3[:, None, 2 * d:3 * d]

    hx, hxt = _norm_fwd(x_lat, x_ctx, norm_w, scale3, shift3, tps, n_samp)
    px, w_in_g = _in_proj_gather(hx, w_in_b, ids)

    states0 = _ctx_state_fwd(px, lg, n_samp, t_lat, lc)
    (o_f, o_b, saved), w_o = _ret_fwd(px, states0, lg, n_samp, seq,
                                      comm=_ag_comm((w_or_b, w_oa_b, w_out_b)))
    w_o_ret, w_o_att, w_out = (w.reshape(-1, d) for w in w_o)

    qn = _att_prep_q(px, cos_all, sin_all, q_norm_w, t_lat)
    kn, vn = _att_prep_kv(px, cos_all, sin_all, k_norm_w)
    (o_att, lse), (w_ada_g,) = _att_fwd(qn, kn, vn, n_samp, seq, lc, comm=_ag_comm((w_ada_b,)))

    (gx_res, do, do_att, dpx, loss8, dgate, g_w_o_ret, g_w_o_att, g_w_out) = _merge(
        x_lat, loss_target.reshape(t_lat, d), o_f, o_b, o_att, px, gate3, w_o_ret, w_o_att, w_out, tps)

    g_a = [g.reshape(N_SHARD, -1, d) for g in (g_w_o_ret, g_w_o_att, g_w_out)]
    (dpx, dkl, dkc, dvl, dvc, gqw), sib_a = _att_bwd(
        dpx, qn, kn, vn, px, o_att, lse, do_att, cos_all, sin_all, q_norm_w, n_samp, seq, lc, comm=_swap_comm(g_a))
    dpx, gkw = _att_kv_bwd(dpx, dkl, dkc, dvl, dvc, px, cos_all, sin_all, k_norm_w)
    t_a = [_chip_sum(g, p, ids) for g, p in zip(g_a, sib_a)]

    (dpx, dstates, dlg_lat), q_a = _ret_bwd(dpx, px, do, saved, lg, n_samp, seq,
                                            comm=_exchange_comm([t16 for _, t16 in t_a]))
    r_a = [_shard_sum(t, q, ids) for (t, _), q in zip(t_a, q_a)]
    dpx, dlg_ctx = _ctx_state_bwd(dpx, px, dstates, lg, n_samp, t_lat, lc)
    dpx = _zero_ctx_tail(dpx, t_lat)

    n_tiles = dpx.shape[0] // _big_rows(dpx.shape[0])
    g_b = _gw_in(hxt, dpx)
    dhx, (sib_b, *r_a) = _dhx(dpx, w_in_g, 0, 1, None, _join_comms(_swap_comm([g_b]), _join_comm(r_a)))
    t_b, t16_b = _chip_sum(g_b, sib_b, ids)
    dhx, (q_b,) = _dhx(dpx, w_in_g, 1, n_tiles - 1, dhx, _exchange_comm([t16_b]))
    r_b_half = _shard_sum(t_b, q_b, ids)
    grad_x, dshift, dscale, g_norm_w = _norm_bwd(x_lat, x_ctx, dhx, gx_res, norm_w, scale3, tps, n_samp)

    dgate_all = jnp.concatenate([dgate, jnp.zeros((1, 1, d), F32)], axis=0)
    dmod3 = jnp.concatenate([dshift, dscale, dgate_all], axis=2).reshape(n_samp + 1, 3 * d)
    dmod8 = jnp.concatenate([dmod3, jnp.zeros((8 - n_samp - 1, 3 * d), F32)], axis=0)
    g_lg = (jnp.sum(dlg_lat[:, :, 0], axis=0).reshape(2, RET_HEADS)
            + jnp.stack([jnp.sum(dlg_ctx[:, :, 0, 0], axis=0), jnp.sum(dlg_ctx[:, :, 1, 0], axis=0)], axis=0))
    g_qw = jnp.sum(gqw, axis=(0, 1, 2))
    zero = jnp.zeros((d,), F32)

    local = _pack_small(zero, g_norm_w, jnp.zeros((3 * d,), F32), g_lg, g_qw, gkw).at[6, 0].set(loss8[0, 0])
    small_sum, dmod_all = _gather_small(jnp.concatenate([local, dmod8], axis=1), d)
    (g_w_ada, g_b_ada, dc_all), (r_b,) = _adaln_bwd(c_all, dmod_all, w_ada_g, comm=_join_comm([r_b_half]))
    dc_ctx = jnp.sum(dc_all.reshape(-1, 8, d)[:, n_samp], axis=0)
    small = small_sum + _pack_small(dc_ctx, zero, g_b_ada, jnp.zeros((2, RET_HEADS), F32), zero[:128], zero[:128])
    r_c = lax.dynamic_index_in_dim(g_w_ada, ids[3], 0, keepdims=False)
    return small[6, 0], grad_x.reshape(n_samp, seq, d), (r_c, r_b, *r_a), small


def kernel(x, c, ctx, c_ctx, norm_w, w_ada, b_ada, w_in, ret_log2_decay, q_norm_w, k_norm_w, w_o_ret, w_o_att, w_out, loss_target, m_c_ctx, m_norm_w, m_w_ada, m_b_ada, m_w_in, m_ret_log2_decay, m_q_norm_w, m_k_norm_w, m_w_o_ret, m_w_o_att, m_w_out, v_c_ctx, v_norm_w, v_w_ada, v_b_ada, v_w_in, v_ret_log2_decay, v_q_norm_w, v_k_norm_w, v_w_o_ret, v_w_o_att, v_w_out):
    big_w = (w_ada[0], w_in[0], w_o_ret[0], w_o_att[0], w_out[0])
    big_m = (m_w_ada[0], m_w_in[0], m_w_o_ret[0], m_w_o_att[0], m_w_out[0])
    big_v = (v_w_ada[0], v_w_in[0], v_w_o_ret[0], v_w_o_att[0], v_w_out[0])

    ids = _place_ids()
    loss, grad_x, big_grad, small_grad_in = _step(
        x, c, ctx, c_ctx, norm_w[0:1], b_ada[0:1], ret_log2_decay[0], q_norm_w[0:1], k_norm_w[0:1], loss_target,
        tuple(_cast_place(w, ids) for w in big_w), ids)
    small_w = _pack_small(c_ctx, norm_w, b_ada, ret_log2_decay, q_norm_w, k_norm_w)
    small_m = _pack_small(m_c_ctx, m_norm_w, m_b_ada, m_ret_log2_decay, m_q_norm_w, m_k_norm_w)
    small_v = _pack_small(v_c_ctx, v_norm_w, v_b_ada, v_ret_log2_decay, v_q_norm_w, v_k_norm_w)
    small_grad, small_delta, small_nm, small_nv = _adamw_small(small_w, small_grad_in, small_m, small_v)

    big_g, big_delta, big_nm, big_nv = [], [], [], []
    for w, g, m, v in zip(big_w, big_grad, big_m, big_v):
        go, dlt, nm, nv = _adamw(w, g, m, v)
        big_g.append(go[None])
        big_delta.append(dlt[None])
        big_nm.append(nm[None])
        big_nv.append(nv[None])
    big_grad = big_g

    def order(small_packed, big):
        s = _unpack_small(small_packed)
        return (s[0], s[1], big[0], s[2], big[1], s[3], s[4], s[5], big[2], big[3], big[4])

    return (loss, grad_x, *order(small_grad, big_grad), *order(small_delta, big_delta),
            *order(small_nm, big_nm), *order(small_nv, big_nv))
```

```python
import functools
from typing import NamedTuple

import jax
import jax.numpy as jnp
from jax import lax
from jax.experimental import pallas as pl
from jax.experimental.pallas import tpu as pltpu

F32 = jnp.float32
BF = jnp.bfloat16
SDS = jax.ShapeDtypeStruct
MESH = pl.DeviceIdType.MESH
ANY = pl.BlockSpec(memory_space=pl.ANY)
SMEM = pl.BlockSpec(memory_space=pltpu.SMEM)

D_MODEL = 1024
GRID_W = 64
RET_HEADS = 4
RET_DK = 256
RET_DV = 512
RET_CHUNK = 128
ATT_HEADS = 8
ATT_KV_HEADS = 2
ATT_REP = ATT_HEADS // ATT_KV_HEADS
ATT_HEAD_DIM = 128
ROPE_THETA = 10000.0
NORM_EPS = 1e-6
IN_COLS = 10752
KV_COLS = 3584
C_RK, C_RV, C_AK, C_AV, C_RQ, C_RG, C_AQ, C_AG, C_MR, C_MA = 0, 1024, 3072, 3328, 3584, 4608, 6656, 7680, 8704, 9728
N_SHARD = 4
ADA_W = 3 * D_MODEL // N_SHARD
IN_W = IN_COLS // N_SHARD
IN_BLK = IN_W
BPS = IN_W // IN_BLK
N_IN_BLK = IN_COLS // IN_BLK
TM = 512
ATT_TQ = 512
ADAM_LR, ADAM_B1, ADAM_B2, ADAM_EPS, ADAM_WD, ADAM_STEP = 0.001, 0.9, 0.999, 1e-08, 0.01, 10
MIB = 1024 * 1024


def _cp(sem=None, vmem_mb=None, **kw):
    if sem is not None:
        kw["dimension_semantics"] = sem
    if vmem_mb is not None:
        kw["vmem_limit_bytes"] = vmem_mb * MIB
    return pltpu.CompilerParams(**kw)


def _dot(a, b, ca=1, cb=0):
    return lax.dot_general(a.astype(BF), b.astype(BF), (((ca,), (cb,)), ((), ())), preferred_element_type=F32)


def _sigmoid(x):
    return 0.5 * jnp.tanh(0.5 * x) + 0.5


def _sum_all(x):
    return jnp.sum(jnp.sum(x, axis=1, keepdims=True), axis=0, keepdims=True)


def _swap_pairs(x):
    ax = x.ndim - 1
    lane = lax.broadcasted_iota(jnp.int32, x.shape, ax)
    nxt = pltpu.roll(x, x.shape[ax] - 1, ax)
    prv = pltpu.roll(x, 1, ax)
    return jnp.where(lane % 2 == 0, nxt, prv)


def _rms(x):
    return lax.rsqrt(jnp.mean(x * x, axis=-1, keepdims=True) + NORM_EPS)


def _rms_bwd(dxh, xh, r):
    return r * (dxh - xh * jnp.mean(dxh * xh, axis=-1, keepdims=True))


class _Comm(NamedTuple):
    name: str
    ins: tuple
    out_shapes: tuple
    aliases: dict
    n_sems: int
    phases: tuple


def _join_comms(*comms):
    offs, i_off, o_off, s_off = [], 0, 0, 0
    for cm in comms:
        offs.append((i_off, o_off, s_off))
        i_off, o_off, s_off = i_off + len(cm.ins), o_off + len(cm.out_shapes), s_off + cm.n_sems

    def phase(k):
        def run(ins, outs, ssem, rsem, base):
            sends, recvs = [], []
            for cm, (io, oo, so) in zip(comms, offs):
                if k < len(cm.phases):
                    s, r = cm.phases[k](ins[io:io + len(cm.ins)], outs[oo:oo + len(cm.out_shapes)], ssem, rsem,
                                        base + so)
                    sends += s
                    recvs += r
            return sends, recvs
        return run

    aliases = {}
    for cm, (io, oo, _) in zip(comms, offs):
        aliases.update({io + a: oo + b for a, b in cm.aliases.items()})
    return _Comm("+".join(cm.name for cm in comms), sum((cm.ins for cm in comms), ()),
                 sum((cm.out_shapes for cm in comms), ()), aliases, s_off,
                 tuple(phase(k) for k in range(max(len(cm.phases) for cm in comms))))


def _run_phases(comm, cins, couts, ssem, rsem):
    for k, phase in enumerate(comm.phases):
        sends, recvs = phase(cins, couts, ssem, rsem, 0)
        if k > 0:
            for cp in sends:
                cp.start()
        for cp in recvs:
            cp.wait_recv()
        for cp in sends:
            cp.wait_send()


def _call(body, args, comm, *, name, grid, in_specs, out_specs, out_shape, scratch_shapes=(),
          compiler_params, aliases=None):
    n_in, n_out, n_sc = len(in_specs), len(out_specs), len(scratch_shapes)
    n_ci, n_co = len(comm.ins), len(comm.out_shapes)
    io_alias = dict(aliases or {})
    io_alias.update({n_in + a: n_out + b for a, b in comm.aliases.items()})

    def kernel_body(*refs):
        ins, cins = refs[:n_in], refs[n_in:n_in + n_ci]
        outs = refs[n_in + n_ci:n_in + n_ci + n_out]
        couts = refs[n_in + n_ci + n_out:n_in + n_ci + n_out + n_co]
        scratch = refs[n_in + n_ci + n_out + n_co:n_in + n_ci + n_out + n_co + n_sc]
        ssem, rsem = refs[-2:]
        first = functools.reduce(jnp.logical_and, [pl.program_id(k) == 0 for k in range(len(grid))])
        last = functools.reduce(jnp.logical_and, [pl.program_id(k) == grid[k] - 1 for k in range(len(grid))])

        @pl.when(first)
        def _():
            for cp in comm.phases[0](cins, couts, ssem, rsem, 0)[0]:
                cp.start()

        body(*ins, *outs, *scratch)

        @pl.when(last)
        def _():
            _run_phases(comm, cins, couts, ssem, rsem)

    res = pl.pallas_call(
        kernel_body, name=name + "+" + comm.name, grid=grid, in_specs=list(in_specs) + [ANY] * n_ci,
        out_specs=tuple(out_specs) + tuple([ANY] * n_co), out_shape=tuple(out_shape) + tuple(comm.out_shapes),
        scratch_shapes=list(scratch_shapes) + [pltpu.SemaphoreType.DMA((comm.n_sems,)),
                                               pltpu.SemaphoreType.DMA((comm.n_sems,))],
        input_output_aliases=io_alias, compiler_params=compiler_params)(*args, *comm.ins)
    return tuple(res[:n_out]), tuple(res[n_out:])


def _adaln_fwd(cvec, w_ada_b, b_ada, ids):
    n_rows = cvec.shape[0]

    def body(ids_ref, c_ref, w_ref, b_ref, o_ref):
        cv = c_ref[...]
        sc = (cv * _sigmoid(cv)).astype(BF)
        o_ref[...] = jnp.dot(sc, w_ref[0], preferred_element_type=F32) + b_ref[0]

    return pl.pallas_call(
        body, name="adaln_fwd",
        grid_spec=pltpu.PrefetchScalarGridSpec(
            num_scalar_prefetch=1, grid=(1,),
            in_specs=[pl.BlockSpec((n_rows, D_MODEL), lambda i, ids_ref: (0, 0)),
                      pl.BlockSpec((1, D_MODEL, ADA_W), lambda i, ids_ref: (ids_ref[3], 0, 0)),
                      pl.BlockSpec((1, 1, ADA_W), lambda i, ids_ref: (ids_ref[3], 0, 0))],
            out_specs=pl.BlockSpec((n_rows, ADA_W), lambda i, ids_ref: (0, 0))),
        out_shape=SDS((n_rows, ADA_W), F32),
        compiler_params=_cp(("arbitrary",), 32))(ids, cvec, w_ada_b, b_ada.reshape(N_SHARD, 1, ADA_W))


def _adaln_bwd(cvec, dmod, w_ada_g, comm):
    n_rows = cvec.shape[0]
    def body(c_ref, d_ref, w_ref, gw_ref, gb_ref, dc_ref):
        cv = c_ref[...]
        sg = _sigmoid(cv)
        sc = cv * sg
        dm = d_ref[...]
        gb_ref[...] = jnp.sum(dm, axis=0, keepdims=True)
        dsc = jnp.zeros(cv.shape, F32)
        for s in range(N_SHARD):
            cols = slice(s * ADA_W, (s + 1) * ADA_W)
            gw_ref[s] = _dot(sc, dm[:, cols], 0, 0)
            dsc = dsc + _dot(dm[:, cols], w_ref[s], 1, 1)
        dc_ref[...] = dsc * (sg * (1.0 + cv * (1.0 - sg)))

    def whole(shape):
        return pl.BlockSpec(shape, lambda i: (0,) * len(shape))

    shapes = ((N_SHARD, D_MODEL, ADA_W), (1, 3 * D_MODEL), (n_rows, D_MODEL))
    return _call(body, [cvec, dmod, w_ada_g], comm, name="adaln_bwd", grid=(1,),
                 in_specs=[whole(cvec.shape), whole(dmod.shape), whole(w_ada_g.shape)],
                 out_specs=tuple(whole(s) for s in shapes), out_shape=tuple(SDS(s, F32) for s in shapes),
                 compiler_params=_cp(("arbitrary",), 56))


def _big_rows(rows):
    return 1536 if rows % 1536 == 0 else TM


def _norm_fwd(x_lat, x_ctx, norm_w, scale3, shift3, tiles_per_sample, n_samp):
    n_lat = x_lat.shape[0] // TM
    rows = x_lat.shape[0] + x_ctx.shape[0]

    def samp(i):
        return jnp.minimum(i // tiles_per_sample, n_samp)

    def body(x_ref, c_ref, nw_ref, sc_ref, sh_ref, hx_ref, hxt_ref):
        x = jnp.where(pl.program_id(0) < n_lat, x_ref[...], c_ref[...])
        h = x * _rms(x) * nw_ref[...] * (1.0 + sc_ref[...]) + sh_ref[...]
        hx_ref[...] = h.astype(BF)
        hxt_ref[...] = h.T.astype(BF)

    return pl.pallas_call(
        body, name="norm_fwd", grid=(rows // TM,),
        in_specs=[pl.BlockSpec((TM, D_MODEL), lambda i: (jnp.minimum(i, n_lat - 1), 0)),
                  pl.BlockSpec((TM, D_MODEL), lambda i: (jnp.maximum(i - n_lat, 0), 0)),
                  pl.BlockSpec((1, D_MODEL), lambda i: (0, 0)),
                  pl.BlockSpec((None, 1, D_MODEL), lambda i: (samp(i), 0, 0)),
                  pl.BlockSpec((None, 1, D_MODEL), lambda i: (samp(i), 0, 0))],
        out_specs=(pl.BlockSpec((TM, D_MODEL), lambda i: (i, 0)),
                   pl.BlockSpec((D_MODEL, TM), lambda i: (0, i))),
        out_shape=(SDS((rows, D_MODEL), BF), SDS((D_MODEL, rows), BF)),
        compiler_params=_cp(("parallel",), 40))(x_lat, x_ctx, norm_w, scale3, shift3)


def _norm_bwd(x_lat, x_ctx, dhx, gx_res, norm_w, scale3, tiles_per_sample, n_samp):
    rows = x_lat.shape[0] + x_ctx.shape[0]
    n_lat = tiles_per_sample * n_samp

    def samp(i):
        return jnp.minimum(i // tiles_per_sample, n_samp)

    def lat(i):
        return jnp.minimum(i, n_lat - 1)

    def body(x_ref, c_ref, dh_ref, gr_ref, nw_ref, sc_ref, gx_ref, dsh_ref, dsc_ref, dnw_ref):
        i = pl.program_id(0)
        x = jnp.where(i < n_lat, x_ref[...], c_ref[...])
        r = _rms(x)
        xh = x * r
        nw = nw_ref[...]
        dh = dh_ref[...]
        first = jnp.logical_or(i % tiles_per_sample == 0, i >= n_lat)

        @pl.when(first)
        def _():
            dsh_ref[...] = jnp.zeros_like(dsh_ref)
            dsc_ref[...] = jnp.zeros_like(dsc_ref)

        @pl.when(i == 0)
        def _():
            dnw_ref[...] = jnp.zeros_like(dnw_ref)

        dsh_ref[...] += jnp.sum(dh, axis=0, keepdims=True)
        dsc_ref[...] += jnp.sum(dh * (xh * nw), axis=0, keepdims=True)
        du = dh * (1.0 + sc_ref[...])
        dnw_ref[...] += jnp.sum(du * xh, axis=0, keepdims=True)

        @pl.when(i < n_lat)
        def _():
            gx_ref[...] = gr_ref[...] + _rms_bwd(du * nw, xh, r)

    return pl.pallas_call(
        body, name="norm_bwd", grid=(rows // TM,),
        in_specs=[pl.BlockSpec((TM, D_MODEL), lambda i: (lat(i), 0)),
                  pl.BlockSpec((TM, D_MODEL), lambda i: (jnp.maximum(i - n_lat, 0), 0)),
                  pl.BlockSpec((TM, D_MODEL), lambda i: (i, 0)),
                  pl.BlockSpec((TM, D_MODEL), lambda i: (lat(i), 0)),
                  pl.BlockSpec((1, D_MODEL), lambda i: (0, 0)),
                  pl.BlockSpec((None, 1, D_MODEL), lambda i: (samp(i), 0, 0))],
        out_specs=(pl.BlockSpec((TM, D_MODEL), lambda i: (lat(i), 0)),
                   pl.BlockSpec((None, 1, D_MODEL), lambda i: (samp(i), 0, 0)),
                   pl.BlockSpec((None, 1, D_MODEL), lambda i: (samp(i), 0, 0)),
                   pl.BlockSpec((1, D_MODEL), lambda i: (0, 0))),
        out_shape=(SDS((n_lat * TM, D_MODEL), F32), SDS((n_samp + 1, 1, D_MODEL), F32),
                   SDS((n_samp + 1, 1, D_MODEL), F32), SDS((1, D_MODEL), F32)),
        compiler_params=_cp(("arbitrary",), 40))(x_lat, x_ctx, dhx, gx_res, norm_w, scale3)


def _in_proj_gather(hx, w_buf, ids):
    rows = hx.shape[0]
    tb = _big_rows(rows)
    n_i = rows // tb
    hrows = D_MODEL // 2

    def body(ids_ref, h_ref, w_in_hbm, px_ref, w_hbm, wv, lsem, ssem, rsem):
        j, i = pl.program_id(0), pl.program_id(1)
        x, y, c, chips = _place()
        sibling = (x, y, 1 - c)

        def half(s, which):
            return w_hbm.at[s, pl.ds(which * hrows, hrows), :]

        def over_ici(rel):
            chip = chips[rel]
            mine, theirs = half(2 * x + y, c), half(2 * chip[0] + chip[1], c)
            return (_remote(mine, mine, ssem.at[rel], rsem.at[rel], (*chip, c)),
                    _remote(theirs, theirs, ssem.at[rel], rsem.at[rel], (*chip, c)))

        def over_d2d(rel):
            s = 2 * chips[rel][0] + chips[rel][1]
            return (_remote(half(s, c), half(s, c), ssem.at[3 + rel], rsem.at[3 + rel], sibling),
                    _remote(half(s, 1 - c), half(s, 1 - c), ssem.at[3 + rel], rsem.at[3 + rel], sibling))

        first_row_tile = i == 0

        @pl.when(jnp.logical_and(j == 0, first_row_tile))
        def _():
            over_ici(0)[0].start()
            over_ici(1)[0].start()

        @pl.when(jnp.logical_and(j == 1, first_row_tile))
        def _():
            for rel in range(2):
                over_ici(rel)[1].wait_recv()
                over_d2d(rel)[0].start()
            over_ici(2)[0].start()
            over_d2d(0)[1].wait_recv()

        @pl.when(jnp.logical_and(j == 2, first_row_tile))
        def _():
            over_d2d(1)[1].wait_recv()

        @pl.when(jnp.logical_and(j == 3, first_row_tile))
        def _():
            over_ici(2)[1].wait_recv()
            passed, landing = over_d2d(2)
            passed.start()
            landing.wait_recv()

        @pl.when(first_row_tile)
        def _():
            cp = pltpu.make_async_copy(w_hbm.at[ids_ref[4 + j]], wv, lsem)
            cp.start()
            cp.wait()

        px_ref[...] = jnp.dot(h_ref[...], wv[...], preferred_element_type=F32).astype(BF)

        @pl.when(jnp.logical_and(j == N_SHARD - 1, i == n_i - 1))
        def _():
            for rel in range(3):
                over_ici(rel)[0].wait_send()
                over_d2d(rel)[0].wait_send()

    return pl.pallas_call(
        body, name="in_proj_gather", input_output_aliases={2: 1},
        grid_spec=pltpu.PrefetchScalarGridSpec(
            num_scalar_prefetch=1, grid=(N_SHARD, n_i),
            in_specs=[pl.BlockSpec((tb, D_MODEL), lambda j, i, ids_ref: (i, 0)), ANY],
            out_specs=(pl.BlockSpec((tb, IN_W), lambda j, i, ids_ref: (i, ids_ref[4 + j])), ANY),
            scratch_shapes=[pltpu.VMEM((D_MODEL, IN_W), BF), pltpu.SemaphoreType.DMA,
                            pltpu.SemaphoreType.DMA((6,)), pltpu.SemaphoreType.DMA((6,))]),
        out_shape=(SDS((rows, IN_COLS), BF), SDS(w_buf.shape, w_buf.dtype)),
        compiler_params=_cp(("arbitrary", "arbitrary"), 56))(ids, hx, w_buf)


def _gw_in(hxt, dpx_all):
    rows = dpx_all.shape[0]
    tb = _big_rows(rows)

    def body(h_ref, d_ref, o_ref):
        @pl.when(pl.program_id(1) == 0)
        def _():
            o_ref[...] = jnp.zeros_like(o_ref)

        o_ref[...] += jnp.dot(h_ref[...], d_ref[...], preferred_element_type=F32)

    return pl.pallas_call(
        body, name="gw_in", grid=(N_IN_BLK, rows // tb),
        in_specs=[pl.BlockSpec((D_MODEL, tb), lambda j, i: (0, i)),
                  pl.BlockSpec((tb, IN_BLK), lambda j, i: (i, j))],
        out_specs=pl.BlockSpec((None, D_MODEL, IN_BLK), lambda j, i: (j // BPS, 0, j % BPS)),
        out_shape=SDS((N_SHARD, D_MODEL, IN_W), F32),
        compiler_params=_cp(("arbitrary", "arbitrary"), 56))(hxt, dpx_all)


def _dhx(dpx_all, w_in_g, tile0, n_tiles, dhx, comm):
    rows = dpx_all.shape[0]
    tb = _big_rows(rows)

    def body(d_ref, w_ref, *rest):
        o_ref = rest[-1]

        @pl.when(pl.program_id(1) == 0)
        def _():
            o_ref[...] = jnp.zeros_like(o_ref)

        o_ref[...] += lax.dot_general(d_ref[...], w_ref[...], (((1,), (1,)), ((), ())), preferred_element_type=F32)

    args, in_specs, aliases = [dpx_all, w_in_g], [
        pl.BlockSpec((tb, IN_BLK), lambda i, j: (tile0 + i, j)),
        pl.BlockSpec((None, D_MODEL, IN_BLK), lambda i, j: (j // BPS, 0, j % BPS))], None
    if dhx is not None:
        args, in_specs, aliases = args + [dhx], in_specs + [ANY], {2: 0}
    (out,), got = _call(body, args, comm, name="dhx", grid=(n_tiles, N_IN_BLK), in_specs=in_specs,
                        out_specs=(pl.BlockSpec((tb, D_MODEL), lambda i, j: (tile0 + i, 0)),),
                        out_shape=(SDS((rows, D_MODEL), F32),), aliases=aliases,
                        compiler_params=_cp(("arbitrary", "arbitrary"), 56))
    return out, got


def _decays(lgv, d):
    c = RET_CHUNK
    ii = lax.broadcasted_iota(jnp.int32, (c, 1), 0).astype(F32)
    jj = lax.broadcasted_iota(jnp.int32, (1, c), 1).astype(F32)
    a_i = jnp.where(d == 0, ii, c - 1.0 - ii)
    a_j = jnp.where(d == 0, jj, c - 1.0 - jj)
    rel = a_i - a_j
    mask = jnp.where(rel >= 0, jnp.exp(lgv * jnp.maximum(rel, 0.0)), 0.0)
    qd = jnp.exp(lgv * (a_i + 1.0))
    kd = jnp.exp(lgv * (c - 1.0 - a_i))
    gc = jnp.exp(jnp.full((1, 1), lgv * c, F32))
    return a_i, rel, mask, qd, kd, gc


def _ctx_state_fwd(px, lg, n_samp, t_lat, lc):
    rb = t_lat // lc

    def body(lg_ref, k_ref, v_ref, o_ref):
        h = pl.program_id(1)
        k = k_ref[...].astype(F32) * (RET_DK ** -0.5)
        v = v_ref[...]
        pos = lax.broadcasted_iota(jnp.int32, (lc, 1), 0).astype(F32)
        o_ref[0] = _dot(k * jnp.exp(lg_ref[0, h] * (lc - 1.0 - pos)), v, 0, 0)
        o_ref[1] = _dot(k * jnp.exp(lg_ref[1, h] * pos), v, 0, 0)

    return pl.pallas_call(
        body, name="ctx_state_fwd", grid=(n_samp, RET_HEADS),
        in_specs=[SMEM,
                  pl.BlockSpec((lc, RET_DK), lambda b, h: (rb + b, C_RK // RET_DK + h)),
                  pl.BlockSpec((lc, RET_DV), lambda b, h: (rb + b, C_RV // RET_DV + h))],
        out_specs=pl.BlockSpec((None, 2, None, RET_DK, RET_DV), lambda b, h: (b, 0, h, 0, 0)),
        out_shape=SDS((n_samp, 2, RET_HEADS, RET_DK, RET_DV), F32),
        compiler_params=_cp(("parallel", "parallel")))(lg, px, px)


def _ctx_state_bwd(dpx, px, dstates, lg, n_samp, t_lat, lc):
    rb = t_lat // lc
    kspec = pl.BlockSpec((lc, RET_DK), lambda b, h: (rb + b, C_RK // RET_DK + h))
    vspec = pl.BlockSpec((lc, RET_DV), lambda b, h: (rb + b, C_RV // RET_DV + h))
    sspec = pl.BlockSpec((None, 2, None, RET_DK, RET_DV), lambda b, h: (b, 0, h, 0, 0))

    def weights(lg_ref, h):
        pos = lax.broadcasted_iota(jnp.int32, (lc, 1), 0).astype(F32)
        e_f = lc - 1.0 - pos
        return pos, e_f, jnp.exp(lg_ref[0, h] * e_f), jnp.exp(lg_ref[1, h] * pos)

    def k_body(lg_ref, dpx_hbm, k_ref, v_ref, ds_ref, dk_ref, dlg_ref):
        pos, e_f, w_f, w_b = weights(lg_ref, pl.program_id(1))
        k = k_ref[...].astype(F32) * (RET_DK ** -0.5)
        y_f = _dot(v_ref[...], ds_ref[0], 1, 1) * w_f
        y_b = _dot(v_ref[...], ds_ref[1], 1, 1) * w_b
        dk_ref[...] = ((y_f + y_b) * (RET_DK ** -0.5)).astype(BF)
        t_f = _sum_all(e_f * k * y_f)
        t_b = _sum_all(pos * k * y_b)
        sub = lax.broadcasted_iota(jnp.int32, (8, 128), 0)
        dlg_ref[...] = jnp.where(sub == 0, t_f, jnp.where(sub == 1, t_b, 0.0))

    def v_body(lg_ref, dpx_hbm, k_ref, ds_ref, dv_ref):
        _, _, w_f, w_b = weights(lg_ref, pl.program_id(1))
        k = k_ref[...].astype(F32) * (RET_DK ** -0.5)
        dv_ref[...] = (_dot(k * w_f, ds_ref[0]) + _dot(k * w_b, ds_ref[1])).astype(BF)

    dpx, dlg = pl.pallas_call(
        k_body, name="ctx_state_bwd_k", grid=(n_samp, RET_HEADS), input_output_aliases={1: 0},
        in_specs=[SMEM, ANY, kspec, vspec, sspec],
        out_specs=(kspec, pl.BlockSpec((None, None, 8, 128), lambda b, h: (b, h, 0, 0))),
        out_shape=(SDS(dpx.shape, dpx.dtype), SDS((n_samp, RET_HEADS, 8, 128), F32)),
        compiler_params=_cp(("parallel", "parallel")))(lg, dpx, px, px, dstates)
    dpx = pl.pallas_call(
        v_body, name="ctx_state_bwd_v", grid=(n_samp, RET_HEADS), input_output_aliases={1: 0},
        in_specs=[SMEM, ANY, kspec, sspec], out_specs=vspec, out_shape=SDS(dpx.shape, dpx.dtype),
        compiler_params=_cp(("parallel", "parallel")))(lg, dpx, px, dstates)
    return dpx, dlg


def _zero_ctx_tail(dpx, t_lat):
    wb = 512
    n_ctx = (dpx.shape[0] - t_lat) // TM

    def body(dpx_hbm, o_ref):
        o_ref[...] = jnp.zeros_like(o_ref)

    return pl.pallas_call(
        body, name="zero_ctx_tail", grid=(n_ctx, (IN_COLS - KV_COLS) // wb), input_output_aliases={0: 0},
        in_specs=[ANY], out_specs=pl.BlockSpec((TM, wb), lambda i, j: (t_lat // TM + i, KV_COLS // wb + j)),
        out_shape=SDS(dpx.shape, dpx.dtype),
        compiler_params=_cp(("parallel", "parallel")))(dpx)


def _ret_specs(row_f, row_b):
    c = RET_CHUNK
    wq = RET_HEADS * RET_DK // 2
    wv = RET_HEADS * RET_DV // 2
    specs = []
    for row in (row_f, row_b):
        specs += [pl.BlockSpec((c, wq), lambda b, n, row=row: (row(b, n), C_RQ // wq)),
                  pl.BlockSpec((c, wq), lambda b, n, row=row: (row(b, n), C_RQ // wq + 1)),
                  pl.BlockSpec((c, 2 * wq), lambda b, n, row=row: (row(b, n), C_RK // (2 * wq))),
                  pl.BlockSpec((c, wv), lambda b, n, row=row: (row(b, n), C_RV // wv)),
                  pl.BlockSpec((c, wv), lambda b, n, row=row: (row(b, n), C_RV // wv + 1))]
    return specs


def _ret_head(refs, h):
    q0, q1, k_ref, v0, v1 = refs
    hh = h % 2
    q = (q0, q1)[h // 2][:, hh * RET_DK:(hh + 1) * RET_DK].astype(F32)
    k = k_ref[:, h * RET_DK:(h + 1) * RET_DK].astype(F32) * (RET_DK ** -0.5)
    v = (v0, v1)[h // 2][:, hh * RET_DV:(hh + 1) * RET_DV]
    return q, k, v


def _ret_fwd(px, states0, lg, n_samp, seq, comm):
    c = RET_CHUNK
    nc = seq // c
    t_lat = n_samp * seq
    wo = RET_HEADS * RET_DV

    def row_f(b, n):
        return b * nc + n

    def row_b(b, n):
        return b * nc + nc - 1 - n

    def body(lg_ref, *refs):
        ins, (s0_ref, of_ref, ob_ref, st_ref, s_s) = refs[:10], refs[10:]

        @pl.when(pl.program_id(1) == 0)
        def _():
            s_s[...] = s0_ref[...]

        for d, o_ref in ((0, of_ref), (1, ob_ref)):
            for h in range(RET_HEADS):
                _, _, mask, qd, kd, gc = _decays(lg_ref[d, h], d)
                q, k, v = _ret_head(ins[5 * d:5 * d + 5], h)
                s = s_s[d, h]
                st_ref[h, d] = s.astype(BF)
                sc = _dot(q, k, 1, 1) * mask
                o_ref[:, h * RET_DV:(h + 1) * RET_DV] = (_dot(sc, v) + _dot(q * qd, s)).astype(BF)
                s_s[d, h] = s * gc + _dot(k * kd, v, 0, 0)

    return _call(
        body, [lg] + [px] * 10 + [states0], comm, name="ret_fwd", grid=(n_samp, nc),
        in_specs=[SMEM] + _ret_specs(row_f, row_b) + [
            pl.BlockSpec((None, 2, RET_HEADS, RET_DK, RET_DV), lambda b, n: (b, 0, 0, 0, 0))],
        out_specs=(pl.BlockSpec((c, wo), lambda b, n: (row_f(b, n), 0)),
                   pl.BlockSpec((c, wo), lambda b, n: (row_b(b, n), 0)),
                   pl.BlockSpec((None, RET_HEADS, 2, None, RET_DK, RET_DV), lambda b, n: (b, 0, 0, n, 0, 0))),
        out_shape=(SDS((t_lat, wo), BF), SDS((t_lat, wo), BF),
                   SDS((n_samp, RET_HEADS, 2, nc, RET_DK, RET_DV), BF)),
        scratch_shapes=[pltpu.VMEM((2, RET_HEADS, RET_DK, RET_DV), F32)],
        compiler_params=_cp(("arbitrary", "arbitrary"), 48))


def _ret_bwd(dpx, px, do, saved, lg, n_samp, seq, comm):
    c = RET_CHUNK
    nc = seq // c
    assert nc % 2 == 0
    wq, wo = RET_HEADS * RET_DK, RET_HEADS * RET_DV

    def row_f(b, n):
        return b * nc + nc - 1 - n

    def row_b(b, n):
        return b * nc + n

    def body(lg_ref, *refs):
        ins = refs[:10]
        (dof_ref, dob_ref, st_ref, dpx_in, dpx_hbm, ds0_ref, dlg_ref,
         ds_s, acc_s, sq_s, sk_s, sv_s, sems) = refs[10:]
        b, n = pl.program_id(0), pl.program_id(1)
        second = n >= nc // 2
        chunks = (nc - 1 - n, n)

        def parked(ch):
            return pl.ds(pl.multiple_of(ch * c, c), c)

        def flush():
            cps = []
            for d, ch in enumerate(chunks):
                rows = pl.ds(pl.multiple_of((b * nc + ch) * c, c), c)
                cps += [pltpu.make_async_copy(sq_s.at[parked(ch), :], dpx_hbm.at[rows, pl.ds(C_RQ, wq)], sems.at[3 * d]),
                        pltpu.make_async_copy(sk_s.at[parked(ch), :], dpx_hbm.at[rows, pl.ds(C_RK, wq)],
                                              sems.at[3 * d + 1]),
                        pltpu.make_async_copy(sv_s.at[parked(ch), :], dpx_hbm.at[rows, pl.ds(C_RV, wo)],
                                              sems.at[3 * d + 2])]
            return cps

        @pl.when(jnp.logical_or(n > nc // 2, jnp.logical_and(n == 0, b > 0)))
        def _():
            for cp in flush():
                cp.wait()

        @pl.when(n == 0)
        def _():
            ds_s[...] = jnp.zeros_like(ds_s)
            acc_s[...] = jnp.zeros_like(acc_s)

        def chains(first_visit):
            for d, do_ref in enumerate((dof_ref, dob_ref)):
                rows = parked(chunks[d])
                for h in range(RET_HEADS):
                    a_i, rel, mask, qd, kd, gc = _decays(lg_ref[d, h], d)
                    q, k, v = _ret_head(ins[5 * d:5 * d + 5], h)
                    qb, kb, vb = q.astype(BF), k.astype(BF), v.astype(BF)
                    cq, cv = slice(h * RET_DK, (h + 1) * RET_DK), slice(h * RET_DV, (h + 1) * RET_DV)
                    dob = do_ref[:, cv].astype(BF)
                    sb = st_ref[h, d]
                    ds = ds_s[d, h]
                    dsb = ds.astype(BF)
                    raw = _dot(qb, kb, 1, 1)
                    sc = raw * mask
                    dsc = _dot(dob, vb, 1, 1) * mask
                    dscb = dsc.astype(BF)
                    x = _dot(dob, sb, 1, 1)
                    y = _dot(vb, dsb, 1, 1)
                    qq = q * qd
                    kk = k * kd
                    dq = _dot(dscb, kb) + x * qd
                    dk = _dot(dscb, qb, 0, 0) + y * kd
                    dv = _dot(sc, dob, 0, 0) + _dot(kk, dsb)
                    if first_visit:
                        sq_s[rows, cq] = dq.astype(BF)
                        sk_s[rows, cq] = dk.astype(BF)
                        sv_s[rows, cv] = dv.astype(BF)
                    else:
                        sq_s[rows, cq] = (sq_s[rows, cq].astype(F32) + dq).astype(BF)
                        sk_s[rows, cq] = ((sk_s[rows, cq].astype(F32) + dk) * (RET_DK ** -0.5)).astype(BF)
                        sv_s[rows, cv] = (sv_s[rows, cv].astype(F32) + dv).astype(BF)
                    t = (_sum_all(dsc * raw * rel) + _sum_all((a_i + 1.0) * qq * x)
                         + _sum_all((c - 1.0 - a_i) * kk * y) + c * gc * _sum_all(ds * sb.astype(F32)))
                    acc_s[4 * d + h:4 * d + h + 1, :] += t
                    ds_s[d, h] = ds * gc + _dot(qq, dob, 0, 0)

        @pl.when(jnp.logical_not(second))
        def _():
            chains(True)

        @pl.when(second)
        def _():
            chains(False)
            for cp in flush():
                cp.start()

        @pl.when(n == nc - 1)
        def _():
            ds0_ref[...] = ds_s[...]
            dlg_ref[...] = acc_s[...]

        @pl.when(jnp.logical_and(b == n_samp - 1, n == nc - 1))
        def _():
            for cp in flush():
                cp.wait()

    do_spec_f = pl.BlockSpec((c, wo), lambda b, n: (row_f(b, n), 0))
    do_spec_b = pl.BlockSpec((c, wo), lambda b, n: (row_b(b, n), 0))
    return _call(
        body, [lg] + [px] * 10 + [do, do, saved, dpx], comm, name="ret_bwd", grid=(n_samp, nc), aliases={14: 0},
        in_specs=[SMEM] + _ret_specs(row_f, row_b) + [
            do_spec_f, do_spec_b,
            pl.BlockSpec((None, RET_HEADS, 2, None, RET_DK, RET_DV), lambda b, n: (b, 0, 0, nc - 1 - n, 0, 0)),
            ANY],
        out_specs=(ANY,
                   pl.BlockSpec((None, 2, RET_HEADS, RET_DK, RET_DV), lambda b, n: (b, 0, 0, 0, 0)),
                   pl.BlockSpec((None, 8, 128), lambda b, n: (b, 0, 0))),
        out_shape=(SDS(dpx.shape, dpx.dtype),
                   SDS((n_samp, 2, RET_HEADS, RET_DK, RET_DV), F32), SDS((n_samp, 8, 128), F32)),
        scratch_shapes=[pltpu.VMEM((2, RET_HEADS, RET_DK, RET_DV), F32), pltpu.VMEM((8, 128), F32),
                        pltpu.VMEM((seq, wq), BF), pltpu.VMEM((seq, wq), BF), pltpu.VMEM((seq, wo), BF),
                        pltpu.SemaphoreType.DMA((6,))],
        compiler_params=_cp(("arbitrary", "arbitrary"), 60))


def _norm_rope(x, w, cos, sin):
    xn = x * _rms(x) * w
    return xn * cos + _swap_pairs(xn) * sin


def _norm_rope_bwd(dy, x, w, cos, sin):
    dxn = dy * cos + _swap_pairs(dy * sin)
    r = _rms(x)
    xh = x * r
    return _rms_bwd(dxn * w, xh, r), jnp.sum(dxn * xh, axis=0, keepdims=True)


def _att_prep_q(px, cos_all, sin_all, qnw, t_lat):
    hd = ATT_HEAD_DIM
    wblk = ATT_REP * hd

    def body(x_ref, cos_ref, sin_ref, w_ref, o_ref):
        for r in range(ATT_REP):
            cols = slice(r * hd, (r + 1) * hd)
            qr = _norm_rope(x_ref[:, cols].astype(F32), w_ref[...], cos_ref[...], sin_ref[...])
            o_ref[:, cols] = (qr * (hd ** -0.5)).astype(BF)

    return pl.pallas_call(
        body, name="att_prep_q", grid=(t_lat // TM, ATT_KV_HEADS),
        in_specs=[pl.BlockSpec((TM, wblk), lambda i, g: (i, C_AQ // wblk + g)),
                  pl.BlockSpec((TM, hd), lambda i, g: (i, 0)),
                  pl.BlockSpec((TM, hd), lambda i, g: (i, 0)),
                  pl.BlockSpec((1, hd), lambda i, g: (0, 0))],
        out_specs=pl.BlockSpec((TM, wblk), lambda i, g: (i, g)),
        out_shape=SDS((t_lat, ATT_HEADS * hd), BF),
        compiler_params=_cp(("parallel", "parallel")))(px, cos_all, sin_all, qnw)


def _att_prep_kv(px, cos_all, sin_all, knw):
    rows = px.shape[0]
    hd = ATT_HEAD_DIM
    kvw = ATT_KV_HEADS * hd

    def body(x_ref, cos_ref, sin_ref, w_ref, k_ref, v_ref):
        for g in range(ATT_KV_HEADS):
            cols = slice(g * hd, (g + 1) * hd)
            k_ref[:, cols] = _norm_rope(x_ref[:, cols].astype(F32), w_ref[...], cos_ref[...],
                                        sin_ref[...]).astype(BF)
            v_ref[:, 2 * g * hd:(2 * g + 1) * hd] = x_ref[:, kvw + g * hd:kvw + (g + 1) * hd].astype(BF)
            v_ref[:, (2 * g + 1) * hd:(2 * g + 2) * hd] = jnp.ones((TM, hd), BF)

    return pl.pallas_call(
        body, name="att_prep_kv", grid=(rows // TM,),
        in_specs=[pl.BlockSpec((TM, 2 * kvw), lambda i: (i, C_AK // (2 * kvw))),
                  pl.BlockSpec((TM, hd), lambda i: (i, 0)),
                  pl.BlockSpec((TM, hd), lambda i: (i, 0)),
                  pl.BlockSpec((1, hd), lambda i: (0, 0))],
        out_specs=(pl.BlockSpec((TM, kvw), lambda i: (i, 0)), pl.BlockSpec((TM, 2 * kvw), lambda i: (i, 0))),
        out_shape=(SDS((rows, kvw), BF), SDS((rows, 2 * kvw), BF)),
        compiler_params=_cp(("parallel",)))(px, cos_all, sin_all, knw)


def _att_kv_bwd(dpx, dkl, dkc, dvl, dvc, px, cos_all, sin_all, knw):
    rows = px.shape[0]
    hd = ATT_HEAD_DIM
    kvw = ATT_KV_HEADS * hd
    n_lat = dkl.shape[0] // TM
    assert dkc.shape[0] == TM

    def body(dpx_hbm, dkl_ref, dkc_ref, dvl_ref, dvc_ref, x_ref, cos_ref, sin_ref, w_ref, o_ref, gw_ref):
        i = pl.program_id(0)

        @pl.when(i == 0)
        def _():
            gw_ref[...] = jnp.zeros_like(gw_ref)

        is_lat = i < n_lat
        dk = jnp.where(is_lat, dkl_ref[...], dkc_ref[...])
        dv = jnp.where(is_lat, dvl_ref[...], dvc_ref[...])
        for g in range(ATT_KV_HEADS):
            cols = slice(g * hd, (g + 1) * hd)
            dx, gw = _norm_rope_bwd(dk[:, cols], x_ref[:, cols].astype(F32), w_ref[...], cos_ref[...], sin_ref[...])
            o_ref[:, cols] = dx.astype(BF)
            gw_ref[...] += gw
        o_ref[:, kvw:] = dv.astype(BF)

    lat = pl.BlockSpec((TM, kvw), lambda i: (jnp.minimum(i, n_lat - 1), 0))
    ctx = pl.BlockSpec((TM, kvw), lambda i: (0, 0))
    kvcol = pl.BlockSpec((TM, 2 * kvw), lambda i: (i, C_AK // (2 * kvw)))
    return pl.pallas_call(
        body, name="att_kv_bwd", grid=(rows // TM,), input_output_aliases={0: 0},
        in_specs=[ANY, lat, ctx, lat, ctx, kvcol,
                  pl.BlockSpec((TM, hd), lambda i: (i, 0)),
                  pl.BlockSpec((TM, hd), lambda i: (i, 0)),
                  pl.BlockSpec((1, hd), lambda i: (0, 0))],
        out_specs=(kvcol, pl.BlockSpec((1, hd), lambda i: (0, 0))),
        out_shape=(SDS(dpx.shape, dpx.dtype), SDS((1, hd), F32)),
        compiler_params=_cp(("arbitrary",)))(dpx, dkl, dkc, dvl, dvc, px, cos_all, sin_all, knw)


def _stack_heads(ref_or_val):
    hd = ATT_HEAD_DIM
    return jnp.concatenate([ref_or_val[:, r * hd:(r + 1) * hd] for r in range(ATT_REP)], axis=0)


def _att_scores(q, kl, kc):
    sl = _dot(q, kl, 1, 1)
    sc = _dot(q, kc, 1, 1)
    m = jnp.maximum(jnp.max(sl, axis=-1, keepdims=True), jnp.max(sc, axis=-1, keepdims=True))
    return jnp.exp(sl - m), jnp.exp(sc - m), m


def _att_fwd(qn, kn, vn, n_samp, seq, lc):
    hd = ATT_HEAD_DIM
    tq = ATT_TQ
    nq = seq // tq
    wblk = ATT_REP * hd
    cb = n_samp * seq // lc
    t_lat = n_samp * seq

    def body(q_ref, kl_ref, kc_ref, vl_ref, vc_ref, o_ref, lse_ref):
        lane = lax.broadcasted_iota(jnp.int32, (tq, hd), 1)
        lse = jnp.zeros((tq, hd), F32)
        for r in range(ATT_REP):
            cols = slice(r * hd, (r + 1) * hd)
            el, ec, m = _att_scores(q_ref[:, cols], kl_ref[...], kc_ref[...])
            pv = _dot(el, vl_ref[...]) + _dot(ec, vc_ref[...])
            denom = pv[:, hd:hd + 1]
            o_ref[:, cols] = (pv[:, :hd] / denom).astype(BF)
            lse = jnp.where(lane == r, m + jnp.log(denom), lse)
        lse_ref[...] = lse

    return pl.pallas_call(
        body, name="att_fwd", grid=(n_samp, ATT_KV_HEADS, nq),
        in_specs=[pl.BlockSpec((tq, wblk), lambda b, g, i: (b * nq + i, g)),
                  pl.BlockSpec((seq, hd), lambda b, g, i: (b, g)),
                  pl.BlockSpec((lc, hd), lambda b, g, i: (cb + b, g)),
                  pl.BlockSpec((seq, 2 * hd), lambda b, g, i: (b, g)),
                  pl.BlockSpec((lc, 2 * hd), lambda b, g, i: (cb + b, g))],
        out_specs=(pl.BlockSpec((tq, wblk), lambda b, g, i: (b * nq + i, g)),
                   pl.BlockSpec((tq, hd), lambda b, g, i: (b * nq + i, g))),
        out_shape=(SDS((t_lat, ATT_HEADS * hd), BF), SDS((t_lat, ATT_KV_HEADS * hd), F32)),
        compiler_params=_cp(("parallel", "parallel", "parallel"), 48))(qn, kn, kn, vn, vn)


def _att_bwd(dpx, qn, kn, vn, px, o_att, lse, do_att, cos_all, sin_all, qnw, n_samp, seq, lc, comm):
    hd = ATT_HEAD_DIM
    tq = ATT_TQ
    nq = seq // tq
    wblk = ATT_REP * hd
    cb = n_samp * seq // lc
    t_lat = n_samp * seq
    kvw = ATT_KV_HEADS * hd
    scale = hd ** -0.5

    def body(dpx_hbm, q_ref, kl_ref, kc_ref, vl_ref, vc_ref, o_ref, do_ref, x_ref, cos_ref, sin_ref, w_ref,
             lse_ref, dq_ref, dkl_ref, dkc_ref, dvl_ref, dvc_ref, gw_ref, akl, akc, avl, avc, aw):
        i = pl.program_id(2)

        @pl.when(i == 0)
        def _():
            akl[...] = jnp.zeros_like(akl)
            akc[...] = jnp.zeros_like(akc)
            avl[...] = jnp.zeros_like(avl)
            avc[...] = jnp.zeros_like(avc)
            aw[...] = jnp.zeros_like(aw)

        dobs, pls, pcs, dsls, dscs = [], [], [], [], []
        for r in range(ATT_REP):
            cols = slice(r * hd, (r + 1) * hd)
            dob = do_ref[:, cols]
            delta = jnp.sum(dob.astype(F32) * o_ref[:, cols].astype(F32), axis=-1, keepdims=True)
            lse = lse_ref[:, r:r + 1]
            p_l = jnp.exp(_dot(q_ref[:, cols], kl_ref[...], 1, 1) - lse).astype(BF)
            p_c = jnp.exp(_dot(q_ref[:, cols], kc_ref[...], 1, 1) - lse).astype(BF)
            ds_l = (p_l * (_dot(dob, vl_ref[...], 1, 1) - delta)).astype(BF)
            ds_c = (p_c * (_dot(dob, vc_ref[...], 1, 1) - delta)).astype(BF)
            dq = (_dot(ds_l, kl_ref[...]) + _dot(ds_c, kc_ref[...])) * scale
            dx, gw = _norm_rope_bwd(dq, x_ref[:, cols].astype(F32), w_ref[...], cos_ref[...], sin_ref[...])
            dq_ref[:, cols] = dx.astype(BF)
            aw[...] += gw
            dobs.append(dob)
            pls.append(p_l)
            pcs.append(p_c)
            dsls.append(ds_l)
            dscs.append(ds_c)
        do4 = jnp.concatenate(dobs, axis=0)
        q4 = _stack_heads(q_ref)
        avl[...] += _dot(jnp.concatenate(pls, axis=0), do4, 0, 0)
        avc[...] += _dot(jnp.concatenate(pcs, axis=0), do4, 0, 0)
        akl[...] += _dot(jnp.concatenate(dsls, axis=0), q4, 0, 0)
        akc[...] += _dot(jnp.concatenate(dscs, axis=0), q4, 0, 0)

        @pl.when(i == nq - 1)
        def _():
            dkl_ref[...] = akl[...]
            dkc_ref[...] = akc[...]
            dvl_ref[...] = avl[...]
            dvc_ref[...] = avc[...]
            gw_ref[...] = aw[...]

    return _call(
        body, [dpx, qn, kn, kn, vn, vn, o_att, do_att, px, cos_all, sin_all, qnw, lse], comm,
        name="att_bwd", grid=(n_samp, ATT_KV_HEADS, nq), aliases={0: 0},
        in_specs=[ANY,
                  pl.BlockSpec((tq, wblk), lambda b, g, i: (b * nq + i, g)),
                  pl.BlockSpec((seq, hd), lambda b, g, i: (b, g)),
                  pl.BlockSpec((lc, hd), lambda b, g, i: (cb + b, g)),
                  pl.BlockSpec((seq, hd), lambda b, g, i: (b, 2 * g)),
                  pl.BlockSpec((lc, hd), lambda b, g, i: (cb + b, 2 * g)),
                  pl.BlockSpec((tq, wblk), lambda b, g, i: (b * nq + i, g)),
                  pl.BlockSpec((tq, wblk), lambda b, g, i: (b * nq + i, g)),
                  pl.BlockSpec((tq, wblk), lambda b, g, i: (b * nq + i, C_AQ // wblk + g)),
                  pl.BlockSpec((tq, hd), lambda b, g, i: (b * nq + i, 0)),
                  pl.BlockSpec((tq, hd), lambda b, g, i: (b * nq + i, 0)),
                  pl.BlockSpec((1, hd), lambda b, g, i: (0, 0)),
                  pl.BlockSpec((tq, hd), lambda b, g, i: (b * nq + i, g))],
        out_specs=(pl.BlockSpec((tq, wblk), lambda b, g, i: (b * nq + i, C_AQ // wblk + g)),
                   pl.BlockSpec((seq, hd), lambda b, g, i: (b, g)),
                   pl.BlockSpec((lc, hd), lambda b, g, i: (b, g)),
                   pl.BlockSpec((seq, hd), lambda b, g, i: (b, g)),
                   pl.BlockSpec((lc, hd), lambda b, g, i: (b, g)),
                   pl.BlockSpec((None, None, 1, hd), lambda b, g, i: (b, g, 0, 0))),
        out_shape=(SDS(dpx.shape, dpx.dtype),
                   SDS((t_lat, kvw), F32), SDS((n_samp * lc, kvw), F32),
                   SDS((t_lat, kvw), F32), SDS((n_samp * lc, kvw), F32),
                   SDS((n_samp, ATT_KV_HEADS, 1, hd), F32)),
        scratch_shapes=[pltpu.VMEM((seq, hd), F32), pltpu.VMEM((lc, hd), F32),
                        pltpu.VMEM((seq, hd), F32), pltpu.VMEM((lc, hd), F32), pltpu.VMEM((1, hd), F32)],
        compiler_params=_cp(("arbitrary", "arbitrary", "arbitrary"), 56))


def _merge(x_lat, target, o_f, o_b, o_att, px, gate3, w_o_ret, w_o_att, w_out, tiles_per_sample):
    t_lat = x_lat.shape[0]
    tm = 256
    n_t = t_lat // tm
    per = tiles_per_sample * (TM // tm)
    d = D_MODEL
    rv = RET_HEADS * RET_DV
    n_samp = gate3.shape[0] - 1

    half = d // 2
    n_px = 10

    def body(x_ref, t_ref, of_ref, ob_ref, oa_ref, *rest):
        pxs, rest = rest[:n_px], rest[n_px:]
        (gt_ref, wor_ref, woa_ref, wout_ref,
         gx_ref, dor_ref, doa_ref, dpx_hbm, loss_ref, dgt_ref, gwor_hbm, gwoa_hbm, gwout_hbm,
         aor, aoa, aout, drg_ref, dtail_ref, sems) = rest
        i = pl.program_id(0)

        def copies(step):
            rows = pl.ds(pl.multiple_of(step * tm, tm), tm)
            return (pltpu.make_async_copy(drg_ref, dpx_hbm.at[rows, pl.ds(C_RG, rv)], sems.at[0]),
                    pltpu.make_async_copy(dtail_ref, dpx_hbm.at[rows, pl.ds(C_AG, 3 * d)], sems.at[1]))

        @pl.when(i == 0)
        def _():
            aor[...] = jnp.zeros_like(aor)
            aoa[...] = jnp.zeros_like(aoa)
            aout[...] = jnp.zeros_like(aout)
            loss_ref[...] = jnp.zeros_like(loss_ref)

        @pl.when(i % per == 0)
        def _():
            dgt_ref[...] = jnp.zeros_like(dgt_ref)

        def cat(refs):
            return jnp.concatenate([r[...] for r in refs], axis=1).astype(F32)

        def ret_head(h):
            cols = slice(h * RET_DV, (h + 1) * RET_DV)
            o = of_ref[:, cols].astype(F32) + ob_ref[:, cols].astype(F32)
            r = _rms(o)
            g = pxs[h][...].astype(F32)
            return o * r, r, g, _sigmoid(g)

        def att_half(k):
            o = oa_ref[:, k * half:(k + 1) * half].astype(F32)
            g = pxs[4 + k][...].astype(F32)
            return o, g, _sigmoid(g)

        yrs = []
        for h in range(RET_HEADS):
            on, _, g, sg = ret_head(h)
            yrs.append((on * (g * sg)).astype(BF))
        yr = jnp.concatenate(yrs, axis=1)
        yas = []
        for k in range(2):
            o, g, sg = att_half(k)
            yas.append((o * (g * sg)).astype(BF))
        ya = jnp.concatenate(yas, axis=1)

        a = jnp.dot(yr, wor_ref[...], preferred_element_type=F32)
        b = jnp.dot(ya, woa_ref[...], preferred_element_type=F32)
        sr = _sigmoid(cat(pxs[6:8]))
        sa = _sigmoid(cat(pxs[8:10]))
        yb = (sr * a + sa * b).astype(BF)
        out = jnp.dot(yb, wout_ref[...], preferred_element_type=F32)
        gate = gt_ref[...]
        err = x_ref[...] + gate * out - t_ref[...]
        loss_ref[...] += 0.5 * _sum_all(err * err) * (1.0 / d)
        dy_tok = err * (1.0 / d)
        gx_ref[...] = dy_tok
        dgt_ref[...] += jnp.sum(dy_tok * out, axis=0, keepdims=True)
        dout = (dy_tok * gate).astype(BF)
        aout[...] += _dot(yb, dout, 0, 0)
        dyy = _dot(dout, wout_ref[...], 1, 1)
        da = (dyy * sr).astype(BF)
        db = (dyy * sa).astype(BF)
        aor[...] += _dot(yr, da, 0, 0)
        aoa[...] += _dot(ya, db, 0, 0)
        dyr = _dot(da, wor_ref[...], 1, 1)
        dya = _dot(db, woa_ref[...], 1, 1)

        @pl.when(i > 0)
        def _():
            for cp in copies(i - 1):
                cp.wait()

        dtail_ref[:, d:2 * d] = (dyy * a * (sr * (1.0 - sr))).astype(BF)
        dtail_ref[:, 2 * d:] = (dyy * b * (sa * (1.0 - sa))).astype(BF)
        for h in range(RET_HEADS):
            cols = slice(h * RET_DV, (h + 1) * RET_DV)
            on, r, g, sg = ret_head(h)
            dy = dyr[:, cols]
            drg_ref[:, cols] = (dy * on * (sg * (1.0 + g * (1.0 - sg)))).astype(BF)
            dor_ref[:, cols] = _rms_bwd(dy * (g * sg), on, r).astype(BF)
        for k in range(2):
            cols = slice(k * half, (k + 1) * half)
            o, g, sg = att_half(k)
            dy = dya[:, cols]
            dtail_ref[:, cols] = (dy * o * (sg * (1.0 + g * (1.0 - sg)))).astype(BF)
            doa_ref[:, cols] = (dy * (g * sg)).astype(BF)
        for cp in copies(i):
            cp.start()

        @pl.when(i == n_t - 1)
        def _():
            for cp in copies(i):
                cp.wait()
            pltpu.sync_copy(aor, gwor_hbm)
            pltpu.sync_copy(aoa, gwoa_hbm)
            pltpu.sync_copy(aout, gwout_hbm)

    def px_blk(col):
        return pl.BlockSpec((tm, half), lambda i: (i, col // half))

    def resident(shape):
        return pl.BlockSpec(shape, lambda i: (0, 0), pipeline_mode=pl.Buffered(1))

    px_cols = ([C_RG + k * half for k in range(4)] + [C_AG, C_AG + half]
               + [C_MR, C_MR + half, C_MA, C_MA + half])
    return pl.pallas_call(
        body, name="merge", grid=(n_t,),
        in_specs=[pl.BlockSpec((tm, d), lambda i: (i, 0)),
                  pl.BlockSpec((tm, d), lambda i: (i, 0)),
                  pl.BlockSpec((tm, rv), lambda i: (i, 0)),
                  pl.BlockSpec((tm, rv), lambda i: (i, 0)),
                  pl.BlockSpec((tm, d), lambda i: (i, 0))]
        + [px_blk(col) for col in px_cols]
        + [pl.BlockSpec((None, 1, d), lambda i: (i // per, 0, 0)),
           resident((rv, d)), resident((d, d)), resident((d, d))],
        out_specs=(pl.BlockSpec((tm, d), lambda i: (i, 0)),
                   pl.BlockSpec((tm, rv), lambda i: (i, 0)),
                   pl.BlockSpec((tm, d), lambda i: (i, 0)),
                   ANY,
                   pl.BlockSpec((8, 128), lambda i: (0, 0)),
                   pl.BlockSpec((None, 1, d), lambda i: (i // per, 0, 0)),
                   ANY, ANY, ANY),
        out_shape=(SDS((t_lat, d), F32), SDS((t_lat, rv), BF), SDS((t_lat, d), BF),
                   SDS((px.shape[0], IN_COLS), BF),
                   SDS((8, 128), F32), SDS((n_samp, 1, d), F32),
                   SDS((rv, d), F32), SDS((d, d), F32), SDS((d, d), F32)),
        scratch_shapes=[pltpu.VMEM((rv, d), F32), pltpu.VMEM((d, d), F32), pltpu.VMEM((d, d), F32),
                        pltpu.VMEM((tm, rv), BF), pltpu.VMEM((tm, 3 * d), BF), pltpu.SemaphoreType.DMA((2,))],
        compiler_params=_cp(("arbitrary",), 56))(
            x_lat, target, o_f, o_b, o_att, *([px] * n_px), gate3, w_o_ret, w_o_att, w_out)


def _place():
    x, y, c = lax.axis_index("x"), lax.axis_index("y"), lax.axis_index("c")
    chips = [(1 - x, y), (x, 1 - y), (1 - x, 1 - y)]
    return x, y, c, chips


def _remote(src, dst, send_sem, recv_sem, to):
    return pltpu.make_async_remote_copy(src_ref=src, dst_ref=dst, send_sem=send_sem, recv_sem=recv_sem,
                                        device_id=to, device_id_type=MESH)


def _place_ids():
    x, y, c = lax.axis_index("x"), lax.axis_index("y"), lax.axis_index("c")
    me = 2 * x + y
    return jnp.stack([x, y, c, me, me, 2 * (1 - x) + y, 2 * x + 1 - y, 2 * (1 - x) + 1 - y]).astype(jnp.int32)


def _ag_comm(bufs):
    n, m = len(bufs), 3

    def half(ref, s, which):
        h = ref.shape[1] // 2
        return ref.at[s, pl.ds(which * h, h), :]

    def ici(ins, outs, ssem, rsem, base):
        x, y, c, chips = _place()
        sends, recvs = [], []
        for a in range(n):
            for j in range(m):
                k, chip = base + a * m + j, chips[j]
                mine, theirs = half(outs[a], 2 * x + y, c), half(outs[a], 2 * chip[0] + chip[1], c)
                sends.append(_remote(mine, mine, ssem.at[k], rsem.at[k], (*chip, c)))
                recvs.append(_remote(theirs, theirs, ssem.at[k], rsem.at[k], (*chip, c)))
        return sends, recvs

    def d2d(ins, outs, ssem, rsem, base):
        x, y, c, chips = _place()
        sends, recvs = [], []
        for a in range(n):
            for j in range(m):
                k, s = base + (n + a) * m + j, 2 * chips[j][0] + chips[j][1]
                sends.append(_remote(half(outs[a], s, c), half(outs[a], s, c), ssem.at[k], rsem.at[k], (x, y, 1 - c)))
                recvs.append(_remote(half(outs[a], s, 1 - c), half(outs[a], s, 1 - c), ssem.at[k], rsem.at[k],
                                     (x, y, 1 - c)))
        return sends, recvs

    return _Comm("all_gather", tuple(bufs), tuple(SDS(b.shape, b.dtype) for b in bufs), {a: a for a in range(n)},
                 2 * n * m, (ici, d2d))


def _swap_comm(grads):
    n = len(grads)

    def phase(ins, outs, ssem, rsem, base):
        x, y, c, _ = _place()
        sends = []
        for a in range(n):
            h = ins[a].shape[1] // 2
            sends.append(_remote(ins[a].at[:, pl.ds((1 - c) * h, h), :], outs[a], ssem.at[base + a],
                                 rsem.at[base + a], (x, y, 1 - c)))
        return sends, sends

    return _Comm("swap_halves", tuple(grads),
                 tuple(SDS((g.shape[0], g.shape[1] // 2, g.shape[2]), g.dtype) for g in grads), {}, n, (phase,))


def _exchange_comm(parts):
    n = len(parts)

    def phase(ins, outs, ssem, rsem, base):
        x, y, c, chips = _place()
        sends = []
        for a in range(n):
            for j, chip in enumerate(chips):
                k = base + 3 * a + j
                sends.append(_remote(ins[a].at[2 * chip[0] + chip[1]], outs[a].at[j], ssem.at[k], rsem.at[k],
                                     (*chip, c)))
        return sends, sends

    return _Comm("exchange_shards", tuple(parts), tuple(SDS((3,) + p.shape[1:], p.dtype) for p in parts), {}, 3 * n,
                 (phase,))


def _join_comm(bufs):
    n = len(bufs)

    def phase(ins, outs, ssem, rsem, base):
        x, y, c, _ = _place()
        sends, recvs = [], []
        for a in range(n):
            h = outs[a].shape[0] // 2
            mine, other = outs[a].at[pl.ds(c * h, h), :], outs[a].at[pl.ds((1 - c) * h, h), :]
            sends.append(_remote(mine, mine, ssem.at[base + a], rsem.at[base + a], (x, y, 1 - c)))
            recvs.append(_remote(other, other, ssem.at[base + a], rsem.at[base + a], (x, y, 1 - c)))
        return sends, recvs

    return _Comm("join_halves", tuple(bufs), tuple(SDS(b.shape, b.dtype) for b in bufs), {a: a for a in range(n)},
                 n, (phase,))


def _cast_place(w, ids):
    rows, cols = w.shape
    tr = min(rows, 256)

    def body(ids_ref, w_ref, o_ref):
        o_ref[...] = w_ref[...].astype(BF)

    return pl.pallas_call(
        body, name="cast_place",
        grid_spec=pltpu.PrefetchScalarGridSpec(
            num_scalar_prefetch=1, grid=(rows // tr,),
            in_specs=[pl.BlockSpec((tr, cols), lambda i, ids_ref: (i, 0))],
            out_specs=pl.BlockSpec((None, tr, cols), lambda i, ids_ref: (ids_ref[3], i, 0))),
        out_shape=SDS((N_SHARD, rows, cols), BF),
        compiler_params=_cp(("parallel",), 40))(ids, w)


def _chip_sum(g, p, ids):
    n_s, rows, cols = g.shape
    h = rows // 2
    tr = min(h, 256)
    nb = h // tr

    def body(ids_ref, g_ref, p_ref, o_ref, o16_ref):
        t = g_ref[...] + p_ref[...]
        o_ref[...] = t
        o16_ref[...] = t.astype(BF)

    out_spec = pl.BlockSpec((None, tr, cols), lambda s, i, ids_ref: (s, i, 0))
    return pl.pallas_call(
        body, name="chip_sum",
        grid_spec=pltpu.PrefetchScalarGridSpec(
            num_scalar_prefetch=1, grid=(n_s, nb),
            in_specs=[pl.BlockSpec((None, tr, cols), lambda s, i, ids_ref: (s, ids_ref[2] * nb + i, 0)),
                      pl.BlockSpec((None, tr, cols), lambda s, i, ids_ref: (s, i, 0))],
            out_specs=(out_spec, out_spec)),
        out_shape=(SDS((n_s, h, cols), g.dtype), SDS((n_s, h, cols), BF)),
        compiler_params=_cp(("parallel", "parallel"), 40))(ids, g, p)


def _shard_sum(t, q, ids):
    _, h, cols = t.shape
    tr = min(h, 256)
    nb = h // tr

    def body(ids_ref, t_ref, q_ref, o_ref):
        o_ref[...] = ((t_ref[...] + q_ref[0].astype(F32)) + q_ref[1].astype(F32)) + q_ref[2].astype(F32)

    return pl.pallas_call(
        body, name="shard_sum",
        grid_spec=pltpu.PrefetchScalarGridSpec(
            num_scalar_prefetch=1, grid=(nb,),
            in_specs=[pl.BlockSpec((None, tr, cols), lambda i, ids_ref: (ids_ref[3], i, 0)),
                      pl.BlockSpec((3, tr, cols), lambda i, ids_ref: (0, i, 0))],
            out_specs=pl.BlockSpec((tr, cols), lambda i, ids_ref: (ids_ref[2] * nb + i, 0))),
        out_shape=SDS((2 * h, cols), t.dtype),
        compiler_params=_cp(("parallel",), 40))(ids, t, q)


def _gather_small(block, n_sum):
    rows, cols = block.shape
    n_dev = 8

    def body(x_ref, *refs):
        g_ref, buf, send_sems, recv_sems, local_sem = refs[-5:]
        x, y, c, chips = _place()
        me, sibling = (x, y, c), (x, y, 1 - c)

        def slot(px_, py_, pc_):
            return buf.at[4 * px_ + 2 * py_ + pc_]

        def copy(k, who, to, src=None):
            return _remote(slot(*who) if src is None else src, slot(*who), send_sems.at[k], recv_sems.at[k], to)

        mine = pltpu.make_async_copy(x_ref, slot(*me), local_sem)
        mine.start()
        first = [copy(0, me, sibling, src=x_ref)]
        first += [copy(1 + j, me, (*chip, c), src=x_ref) for j, chip in enumerate(chips)]
        for cp in first:
            cp.start()
        passed = [copy(4 + j, (*chip, c), sibling) for j, chip in enumerate(chips)]
        for j, chip in enumerate(chips):
            copy(1 + j, (*chip, c), me).wait_recv()
            passed[j].start()
        copy(0, sibling, me).wait_recv()
        for j, chip in enumerate(chips):
            copy(4 + j, (*chip, 1 - c), me).wait_recv()
        for cp in first + passed:
            cp.wait_send()
        mine.wait()
        if n_sum:
            acc = buf[0, :, :n_sum]
            for s in range(1, n_dev):
                acc = acc + buf[s, :, :n_sum]
            refs[0][...] = acc
        for s in range(n_dev):
            g_ref[s * rows:(s + 1) * rows, :] = buf[s, :, n_sum:]

    sum_shape = (SDS((rows, n_sum), F32),) if n_sum else ()
    return pl.pallas_call(
        body, name="gather_small",
        in_specs=[pl.BlockSpec(memory_space=pltpu.VMEM)],
        out_specs=tuple(pl.BlockSpec(memory_space=pltpu.VMEM) for _ in (*sum_shape, None)),
        out_shape=(*sum_shape, SDS((n_dev * rows, cols - n_sum), F32)),
        scratch_shapes=[pltpu.VMEM((n_dev, rows, cols), F32), pltpu.SemaphoreType.DMA((7,)),
                        pltpu.SemaphoreType.DMA((7,)), pltpu.SemaphoreType.DMA],
        compiler_params=_cp(has_side_effects=True))(block)


def _scatter_mod(part):
    _, rows, width = part.shape

    def body(p_ref, o_ref, send_sems, recv_sems):
        x, y, c, chips = _place()
        sends = [_remote(p_ref.at[4 * chip[0] + 2 * chip[1] + c], o_ref.at[2 * x + y], send_sems.at[j],
                         recv_sems.at[j], (*chip, c)) for j, chip in enumerate(chips)]
        recvs = [_remote(o_ref.at[2 * chip[0] + chip[1]], o_ref.at[2 * chip[0] + chip[1]], send_sems.at[j],
                         recv_sems.at[j], (*chip, c)) for j, chip in enumerate(chips)]
        for cp in sends:
            cp.start()
        o_ref[2 * x + y] = p_ref[4 * x + 2 * y + c]
        for cp in recvs:
            cp.wait_recv()
        for cp in sends:
            cp.wait_send()

    return pl.pallas_call(
        body, name="scatter_mod",
        in_specs=[pl.BlockSpec(memory_space=pltpu.VMEM)], out_specs=pl.BlockSpec(memory_space=pltpu.VMEM),
        out_shape=SDS((N_SHARD, rows, width), F32),
        scratch_shapes=[pltpu.SemaphoreType.DMA((3,)), pltpu.SemaphoreType.DMA((3,))],
        compiler_params=_cp(has_side_effects=True))(part)


def _adam_math(w, g, m, v):
    m = ADAM_B1 * m + (1.0 - ADAM_B1) * g
    v = ADAM_B2 * v + (1.0 - ADAM_B2) * (g * g)
    m_hat = m / (1.0 - ADAM_B1 ** ADAM_STEP)
    v_hat = v / (1.0 - ADAM_B2 ** ADAM_STEP)
    delta = -ADAM_LR * (m_hat / (jnp.sqrt(v_hat) + ADAM_EPS) + ADAM_WD * w)
    return delta, m, v


def _adamw(w, g, m, v):
    rows, cols = w.shape
    tr = min(rows, 256 if cols <= 2048 else 128)

    def body(w_ref, g_ref, m_ref, v_ref, go_ref, d_ref, nm_ref, nv_ref):
        g = g_ref[...]
        go_ref[...] = g
        d_ref[...], nm_ref[...], nv_ref[...] = _adam_math(w_ref[...], g, m_ref[...], v_ref[...])

    spec = pl.BlockSpec((tr, cols), lambda i: (i, 0))
    return pl.pallas_call(
        body, name="adamw", grid=(rows // tr,), in_specs=[spec] * 4, out_specs=(spec,) * 4,
        out_shape=(SDS(w.shape, F32),) * 4, compiler_params=_cp(("parallel",), 40))(w, g, m, v)


def _adamw_small(w, g, m, v):
    def body(w_ref, g_ref, m_ref, v_ref, go_ref, d_ref, nm_ref, nv_ref):
        w = w_ref[...]
        g = g_ref[...]
        sub = lax.broadcasted_iota(jnp.int32, w.shape, 0)
        lane = lax.broadcasted_iota(jnp.int32, w.shape, 1)
        is_ret = jnp.logical_and(sub == 5, lane < 2 * RET_HEADS)
        u = jnp.exp(jnp.where(is_ret, w, -1.0) * jnp.log(2.0))
        g = jnp.where(is_ret, g * (-u * jnp.log(2.0) / (1.0 - u)), g)
        go_ref[...] = g
        d_ref[...], nm_ref[...], nv_ref[...] = _adam_math(w, g, m_ref[...], v_ref[...])

    return pl.pallas_call(body, name="adamw_small", out_shape=(SDS(w.shape, F32),) * 4)(w, g, m, v)


def _rope_tables(seq, n_samp, n_ctx_rows):
    rows = seq // GRID_W
    row = jnp.repeat(jnp.arange(rows, dtype=F32), GRID_W)
    col = jnp.tile(jnp.arange(GRID_W, dtype=F32), rows)
    half = ATT_HEAD_DIM // 2
    freqs = ROPE_THETA ** (-jnp.arange(0, half, 2, dtype=F32) / half)
    ang = jnp.concatenate([row[:, None] * freqs, col[:, None] * freqs], axis=-1)
    cos, sin = jnp.cos(ang), jnp.sin(ang)
    cos_f = jnp.repeat(cos, 2, axis=1)
    sin_s = jnp.stack([-sin, sin], axis=-1).reshape(seq, ATT_HEAD_DIM)
    cos_all = jnp.concatenate([jnp.tile(cos_f, (n_samp, 1)), jnp.ones((n_ctx_rows, ATT_HEAD_DIM), F32)], axis=0)
    sin_all = jnp.concatenate([jnp.tile(sin_s, (n_samp, 1)), jnp.zeros((n_ctx_rows, ATT_HEAD_DIM), F32)], axis=0)
    return cos_all, sin_all


def _pack_small(c_ctx, norm_w, b_ada, ret, qn, kn):
    d = D_MODEL
    row5 = jnp.concatenate([ret.reshape(-1), jnp.zeros((128 - 2 * RET_HEADS,), F32), qn.reshape(-1), kn.reshape(-1),
                            jnp.zeros((d - 384,), F32)])
    return jnp.concatenate([c_ctx.reshape(1, d), norm_w.reshape(1, d), b_ada.reshape(3, d), row5.reshape(1, d),
                            jnp.zeros((2, d), F32)], axis=0)


def _unpack_small(p):
    d = D_MODEL
    return (p[0], p[1:2], p[2:5].reshape(1, 3 * d), p[5, :2 * RET_HEADS].reshape(1, 2, RET_HEADS),
            p[5:6, 128:256], p[5:6, 256:384])


def _step(x, c, ctx, c_ctx, norm_w, b_ada, ret_log2_decay, q_norm_w, k_norm_w, loss_target, weights, ids):
    n_samp, seq, d = x.shape
    lc = ctx.shape[1]
    t_lat, t_ctx = n_samp * seq, n_samp * lc
    assert seq % TM == 0 and t_ctx == TM and t_lat % lc == 0 and seq % GRID_W == 0
    tps = seq // TM

    x_lat = x.reshape(t_lat, d)
    x_ctx = ctx.reshape(t_ctx, d)
    cvec8 = jnp.concatenate([c, c_ctx.reshape(1, d), jnp.zeros((8 - n_samp - 1, d), F32)], axis=0)
    lg = jnp.log1p(-jnp.exp2(ret_log2_decay.reshape(2, RET_HEADS)))
    cos_all, sin_all = _rope_tables(seq, n_samp, t_ctx)

    w_ada_b, w_in_b, w_or_b, w_oa_b, w_out_b = weights
    (c_all,) = _gather_small(cvec8, 0)
    mod_part = _adaln_fwd(c_all, w_ada_b, b_ada, ids)
    mod8 = _scatter_mod(mod_part.reshape(-1, 8, ADA_W)).transpose(1, 0, 2).reshape(8, 3 * d)
    mod3 = mod8[:n_samp + 1]
    shift3 = mod3[:, None, 0:d]
    scale3 = mod3[:, None, d:2 * d]
    gate3 = mod3[:, None, 2 * d:3 * d]

    hx, hxt = _norm_fwd(x_lat, x_ctx, norm_w, scale3, shift3, tps, n_samp)
    px, w_in_g = _in_proj_gather(hx, w_in_b, ids)

    states0 = _ctx_state_fwd(px, lg, n_samp, t_lat, lc)
    (o_f, o_b, saved), (*w_o, w_ada_g) = _ret_fwd(px, states0, lg, n_samp, seq,
                                                  comm=_ag_comm((w_or_b, w_oa_b, w_out_b, w_ada_b)))
    w_o_ret, w_o_att, w_out = (w.reshape(-1, d) for w in w_o)

    qn = _att_prep_q(px, cos_all, sin_all, q_norm_w, t_lat)
    kn, vn = _att_prep_kv(px, cos_all, sin_all, k_norm_w)
    o_att, lse = _att_fwd(qn, kn, vn, n_samp, seq, lc)

    (gx_res, do, do_att, dpx, loss8, dgate, g_w_o_ret, g_w_o_att, g_w_out) = _merge(
        x_lat, loss_target.reshape(t_lat, d), o_f, o_b, o_att, px, gate3, w_o_ret, w_o_att, w_out, tps)

    g_a = [g.reshape(N_SHARD, -1, d) for g in (g_w_o_ret, g_w_o_att, g_w_out)]
    (dpx, dkl, dkc, dvl, dvc, gqw), sib_a = _att_bwd(
        dpx, qn, kn, vn, px, o_att, lse, do_att, cos_all, sin_all, q_norm_w, n_samp, seq, lc, comm=_swap_comm(g_a))
    dpx, gkw = _att_kv_bwd(dpx, dkl, dkc, dvl, dvc, px, cos_all, sin_all, k_norm_w)
    t_a = [_chip_sum(g, p, ids) for g, p in zip(g_a, sib_a)]

    (dpx, dstates, dlg_lat), q_a = _ret_bwd(dpx, px, do, saved, lg, n_samp, seq,
                                            comm=_exchange_comm([t16 for _, t16 in t_a]))
    r_a = [_shard_sum(t, q, ids) for (t, _), q in zip(t_a, q_a)]
    dpx, dlg_ctx = _ctx_state_bwd(dpx, px, dstates, lg, n_samp, t_lat, lc)
    dpx = _zero_ctx_tail(dpx, t_lat)

    n_tiles = dpx.shape[0] // _big_rows(dpx.shape[0])
    g_b = _gw_in(hxt, dpx)
    dhx, (sib_b, *r_a) = _dhx(dpx, w_in_g, 0, 1, None, _join_comms(_swap_comm([g_b]), _join_comm(r_a)))
    t_b, t16_b = _chip_sum(g_b, sib_b, ids)
    dhx, (q_b,) = _dhx(dpx, w_in_g, 1, n_tiles - 1, dhx, _exchange_comm([t16_b]))
    r_b_half = _shard_sum(t_b, q_b, ids)
    grad_x, dshift, dscale, g_norm_w = _norm_bwd(x_lat, x_ctx, dhx, gx_res, norm_w, scale3, tps, n_samp)

    dgate_all = jnp.concatenate([dgate, jnp.zeros((1, 1, d), F32)], axis=0)
    dmod3 = jnp.concatenate([dshift, dscale, dgate_all], axis=2).reshape(n_samp + 1, 3 * d)
    dmod8 = jnp.concatenate([dmod3, jnp.zeros((8 - n_samp - 1, 3 * d), F32)], axis=0)
    g_lg = (jnp.sum(dlg_lat[:, :, 0], axis=0).reshape(2, RET_HEADS)
            + jnp.stack([jnp.sum(dlg_ctx[:, :, 0, 0], axis=0), jnp.sum(dlg_ctx[:, :, 1, 0], axis=0)], axis=0))
    g_qw = jnp.sum(gqw, axis=(0, 1, 2))
    zero = jnp.zeros((d,), F32)

    local = _pack_small(zero, g_norm_w, jnp.zeros((3 * d,), F32), g_lg, g_qw, gkw).at[6, 0].set(loss8[0, 0])
    small_sum, dmod_all = _gather_small(jnp.concatenate([local, dmod8], axis=1), d)
    (g_w_ada, g_b_ada, dc_all), (r_b,) = _adaln_bwd(c_all, dmod_all, w_ada_g, comm=_join_comm([r_b_half]))
    dc_ctx = jnp.sum(dc_all.reshape(-1, 8, d)[:, n_samp], axis=0)
    small = small_sum + _pack_small(dc_ctx, zero, g_b_ada, jnp.zeros((2, RET_HEADS), F32), zero[:128], zero[:128])
    r_c = lax.dynamic_index_in_dim(g_w_ada, ids[3], 0, keepdims=False)
    return small[6, 0], grad_x.reshape(n_samp, seq, d), (r_c, r_b, *r_a), small


def kernel(x, c, ctx, c_ctx, norm_w, w_ada, b_ada, w_in, ret_log2_decay, q_norm_w, k_norm_w, w_o_ret, w_o_att, w_out, loss_target, m_c_ctx, m_norm_w, m_w_ada, m_b_ada, m_w_in, m_ret_log2_decay, m_q_norm_w, m_k_norm_w, m_w_o_ret, m_w_o_att, m_w_out, v_c_ctx, v_norm_w, v_w_ada, v_b_ada, v_w_in, v_ret_log2_decay, v_q_norm_w, v_k_norm_w, v_w_o_ret, v_w_o_att, v_w_out):
    big_w = (w_ada[0], w_in[0], w_o_ret[0], w_o_att[0], w_out[0])
    big_m = (m_w_ada[0], m_w_in[0], m_w_o_ret[0], m_w_o_att[0], m_w_out[0])
    big_v = (v_w_ada[0], v_w_in[0], v_w_o_ret[0], v_w_o_att[0], v_w_out[0])

    ids = _place_ids()
    loss, grad_x, big_grad, small_grad_in = _step(
        x, c, ctx, c_ctx, norm_w[0:1], b_ada[0:1], ret_log2_decay[0], q_norm_w[0:1], k_norm_w[0:1], loss_target,
        tuple(_cast_place(w, ids) for w in big_w), ids)
    small_w = _pack_small(c_ctx, norm_w, b_ada, ret_log2_decay, q_norm_w, k_norm_w)
    small_m = _pack_small(m_c_ctx, m_norm_w, m_b_ada, m_ret_log2_decay, m_q_norm_w, m_k_norm_w)
    small_v = _pack_small(v_c_ctx, v_norm_w, v_b_ada, v_ret_log2_decay, v_q_norm_w, v_k_norm_w)
    small_grad, small_delta, small_nm, small_nv = _adamw_small(small_w, small_grad_in, small_m, small_v)

    big_g, big_delta, big_nm, big_nv = [], [], [], []
    for w, g, m, v in zip(big_w, big_grad, big_m, big_v):
        go, dlt, nm, nv = _adamw(w, g, m, v)
        big_g.append(go[None])
        big_delta.append(dlt[None])
        big_nm.append(nm[None])
        big_nv.append(nv[None])
    big_grad = big_g

    def order(small_packed, big):
        s = _unpack_small(small_packed)
        return (s[0], s[1], big[0], s[2], big[1], s[3], s[4], s[5], big[2], big[3], big[4])

    return (loss, grad_x, *order(small_grad, big_grad), *order(small_delta, big_delta),
            *order(small_nm, big_nm), *order(small_nv, big_nv))
```

```python
import functools
from typing import NamedTuple

import jax
import jax.numpy as jnp
from jax import lax
from jax.experimental import pallas as pl
from jax.experimental.pallas import tpu as pltpu

F32 = jnp.float32
BF = jnp.bfloat16
SDS = jax.ShapeDtypeStruct
MESH = pl.DeviceIdType.MESH
ANY = pl.BlockSpec(memory_space=pl.ANY)
SMEM = pl.BlockSpec(memory_space=pltpu.SMEM)

D_MODEL = 1024
GRID_W = 64
RET_HEADS = 4
RET_DK = 256
RET_DV = 512
RET_CHUNK = 128
ATT_HEADS = 8
ATT_KV_HEADS = 2
ATT_REP = ATT_HEADS // ATT_KV_HEADS
ATT_HEAD_DIM = 128
ROPE_THETA = 10000.0
NORM_EPS = 1e-6
IN_COLS = 10752
KV_COLS = 3584
C_RK, C_RV, C_AK, C_AV, C_RQ, C_RG, C_AQ, C_AG, C_MR, C_MA = 0, 1024, 3072, 3328, 3584, 4608, 6656, 7680, 8704, 9728
N_SHARD = 4
ADA_W = 3 * D_MODEL // N_SHARD
IN_W = IN_COLS // N_SHARD
IN_BLK = IN_W
BPS = IN_W // IN_BLK
N_IN_BLK = IN_COLS // IN_BLK
TM = 512
ATT_TQ = 512
ADAM_LR, ADAM_B1, ADAM_B2, ADAM_EPS, ADAM_WD, ADAM_STEP = 0.001, 0.9, 0.999, 1e-08, 0.01, 10
MIB = 1024 * 1024


def _cp(sem=None, vmem_mb=None, **kw):
    if sem is not None:
        kw["dimension_semantics"] = sem
    if vmem_mb is not None:
        kw["vmem_limit_bytes"] = vmem_mb * MIB
    return pltpu.CompilerParams(**kw)


def _dot(a, b, ca=1, cb=0):
    return lax.dot_general(a.astype(BF), b.astype(BF), (((ca,), (cb,)), ((), ())), preferred_element_type=F32)


def _sigmoid(x):
    return 0.5 * jnp.tanh(0.5 * x) + 0.5


def _sum_all(x):
    return jnp.sum(jnp.sum(x, axis=1, keepdims=True), axis=0, keepdims=True)


def _swap_pairs(x):
    ax = x.ndim - 1
    lane = lax.broadcasted_iota(jnp.int32, x.shape, ax)
    nxt = pltpu.roll(x, x.shape[ax] - 1, ax)
    prv = pltpu.roll(x, 1, ax)
    return jnp.where(lane % 2 == 0, nxt, prv)


def _rms(x):
    return lax.rsqrt(jnp.mean(x * x, axis=-1, keepdims=True) + NORM_EPS)


def _rms_bwd(dxh, xh, r):
    return r * (dxh - xh * jnp.mean(dxh * xh, axis=-1, keepdims=True))


class _Comm(NamedTuple):
    name: str
    ins: tuple
    out_shapes: tuple
    aliases: dict
    n_sems: int
    phases: tuple


def _join_comms(*comms):
    offs, i_off, o_off, s_off = [], 0, 0, 0
    for cm in comms:
        offs.append((i_off, o_off, s_off))
        i_off, o_off, s_off = i_off + len(cm.ins), o_off + len(cm.out_shapes), s_off + cm.n_sems

    def phase(k):
        def run(ins, outs, ssem, rsem, base):
            sends, recvs = [], []
            for cm, (io, oo, so) in zip(comms, offs):
                if k < len(cm.phases):
                    s, r = cm.phases[k](ins[io:io + len(cm.ins)], outs[oo:oo + len(cm.out_shapes)], ssem, rsem,
                                        base + so)
                    sends += s
                    recvs += r
            return sends, recvs
        return run

    aliases = {}
    for cm, (io, oo, _) in zip(comms, offs):
        aliases.update({io + a: oo + b for a, b in cm.aliases.items()})
    return _Comm("+".join(cm.name for cm in comms), sum((cm.ins for cm in comms), ()),
                 sum((cm.out_shapes for cm in comms), ()), aliases, s_off,
                 tuple(phase(k) for k in range(max(len(cm.phases) for cm in comms))))


def _run_phases(comm, cins, couts, ssem, rsem):
    for k, phase in enumerate(comm.phases):
        sends, recvs = phase(cins, couts, ssem, rsem, 0)
        if k > 0:
            for cp in sends:
                cp.start()
        for cp in recvs:
            cp.wait_recv()
        for cp in sends:
            cp.wait_send()


def _call(body, args, comm, *, name, grid, in_specs, out_specs, out_shape, scratch_shapes=(),
          compiler_params, aliases=None):
    n_in, n_out, n_sc = len(in_specs), len(out_specs), len(scratch_shapes)
    n_ci, n_co = len(comm.ins), len(comm.out_shapes)
    io_alias = dict(aliases or {})
    io_alias.update({n_in + a: n_out + b for a, b in comm.aliases.items()})

    def kernel_body(*refs):
        ins, cins = refs[:n_in], refs[n_in:n_in + n_ci]
        outs = refs[n_in + n_ci:n_in + n_ci + n_out]
        couts = refs[n_in + n_ci + n_out:n_in + n_ci + n_out + n_co]
        scratch = refs[n_in + n_ci + n_out + n_co:n_in + n_ci + n_out + n_co + n_sc]
        ssem, rsem = refs[-2:]
        first = functools.reduce(jnp.logical_and, [pl.program_id(k) == 0 for k in range(len(grid))])
        last = functools.reduce(jnp.logical_and, [pl.program_id(k) == grid[k] - 1 for k in range(len(grid))])

        @pl.when(first)
        def _():
            for cp in comm.phases[0](cins, couts, ssem, rsem, 0)[0]:
                cp.start()

        body(*ins, *outs, *scratch)

        @pl.when(last)
        def _():
            _run_phases(comm, cins, couts, ssem, rsem)

    res = pl.pallas_call(
        kernel_body, name=name + "+" + comm.name, grid=grid, in_specs=list(in_specs) + [ANY] * n_ci,
        out_specs=tuple(out_specs) + tuple([ANY] * n_co), out_shape=tuple(out_shape) + tuple(comm.out_shapes),
        scratch_shapes=list(scratch_shapes) + [pltpu.SemaphoreType.DMA((comm.n_sems,)),
                                               pltpu.SemaphoreType.DMA((comm.n_sems,))],
        input_output_aliases=io_alias, compiler_params=compiler_params)(*args, *comm.ins)
    return tuple(res[:n_out]), tuple(res[n_out:])


def _adaln_fwd(cvec8, w_ada_b, b_ada):
    rows, d = cvec8.shape
    n_dev = 8

    def body(x_ref, w_hbm, b_ref, g_ref, o_ref, buf, wv, part, send_sems, recv_sems, local_sems):
        x, y, c, chips = _place()
        me, sibling, shard = (x, y, c), (x, y, 1 - c), 2 * x + y

        def slot(px_, py_, pc_):
            return 4 * px_ + 2 * py_ + pc_

        def copy(k, who, to, src=None):
            dst = buf.at[slot(*who)]
            return _remote(dst if src is None else src, dst, send_sems.at[k], recv_sems.at[k], to)

        weight = pltpu.make_async_copy(w_hbm.at[shard], wv, local_sems.at[0])
        weight.start()
        mine = pltpu.make_async_copy(x_ref, buf.at[slot(*me)], local_sems.at[1])
        mine.start()
        first = [copy(0, me, sibling, src=x_ref)]
        first += [copy(1 + j, me, (*chip, c), src=x_ref) for j, chip in enumerate(chips)]
        for cp in first:
            cp.start()
        passed = [copy(4 + j, (*chip, c), sibling) for j, chip in enumerate(chips)]
        for j, chip in enumerate(chips):
            copy(1 + j, (*chip, c), me).wait_recv()
            passed[j].start()
        copy(0, sibling, me).wait_recv()
        for j, chip in enumerate(chips):
            copy(4 + j, (*chip, 1 - c), me).wait_recv()
        for cp in first + passed:
            cp.wait_send()
        mine.wait()
        weight.wait()

        cv = buf[...].reshape(n_dev * rows, d)
        g_ref[...] = cv
        sc = (cv * _sigmoid(cv)).astype(BF)
        mod = jnp.dot(sc, wv[...], preferred_element_type=F32) + b_ref[shard]
        part[...] = mod.reshape(n_dev, rows, ADA_W)

        def back(j, chip):
            theirs = o_ref.at[2 * chip[0] + chip[1]]
            return (_remote(part.at[slot(*chip, c)], o_ref.at[shard], send_sems.at[7 + j], recv_sems.at[7 + j],
                            (*chip, c)),
                    _remote(theirs, theirs, send_sems.at[7 + j], recv_sems.at[7 + j], (*chip, c)))

        for j, chip in enumerate(chips):
            back(j, chip)[0].start()
        o_ref[shard] = part[slot(*me)]
        for j, chip in enumerate(chips):
            back(j, chip)[1].wait_recv()
        for j, chip in enumerate(chips):
            back(j, chip)[0].wait_send()

    vmem = pl.BlockSpec(memory_space=pltpu.VMEM)
    return pl.pallas_call(
        body, name="adaln_fwd", in_specs=[vmem, ANY, vmem], out_specs=(vmem, vmem),
        out_shape=(SDS((n_dev * rows, d), F32), SDS((N_SHARD, rows, ADA_W), F32)),
        scratch_shapes=[pltpu.VMEM((n_dev, rows, d), F32), pltpu.VMEM((D_MODEL, ADA_W), BF),
                        pltpu.VMEM((n_dev, rows, ADA_W), F32), pltpu.SemaphoreType.DMA((10,)),
                        pltpu.SemaphoreType.DMA((10,)), pltpu.SemaphoreType.DMA((2,))],
        compiler_params=_cp(vmem_mb=32, has_side_effects=True))(cvec8, w_ada_b, b_ada.reshape(N_SHARD, 1, ADA_W))


def _adaln_bwd(cvec, dmod, w_ada_g, comm):
    n_rows = cvec.shape[0]
    def body(c_ref, d_ref, w_ref, gw_ref, gb_ref, dc_ref):
        cv = c_ref[...]
        sg = _sigmoid(cv)
        sc = cv * sg
        dm = d_ref[...]
        gb_ref[...] = jnp.sum(dm, axis=0, keepdims=True)
        dsc = jnp.zeros(cv.shape, F32)
        for s in range(N_SHARD):
            cols = slice(s * ADA_W, (s + 1) * ADA_W)
            gw_ref[s] = _dot(sc, dm[:, cols], 0, 0)
            dsc = dsc + _dot(dm[:, cols], w_ref[s], 1, 1)
        dc_ref[...] = dsc * (sg * (1.0 + cv * (1.0 - sg)))

    def whole(shape):
        return pl.BlockSpec(shape, lambda i: (0,) * len(shape))

    shapes = ((N_SHARD, D_MODEL, ADA_W), (1, 3 * D_MODEL), (n_rows, D_MODEL))
    return _call(body, [cvec, dmod, w_ada_g], comm, name="adaln_bwd", grid=(1,),
                 in_specs=[whole(cvec.shape), whole(dmod.shape), whole(w_ada_g.shape)],
                 out_specs=tuple(whole(s) for s in shapes), out_shape=tuple(SDS(s, F32) for s in shapes),
                 compiler_params=_cp(("arbitrary",), 56))


def _big_rows(rows):
    return 1536 if rows % 1536 == 0 else TM


def _norm_fwd(x_lat, x_ctx, norm_w, scale3, shift3, tiles_per_sample, n_samp):
    n_lat = x_lat.shape[0] // TM
    rows = x_lat.shape[0] + x_ctx.shape[0]

    def samp(i):
        return jnp.minimum(i // tiles_per_sample, n_samp)

    def body(x_ref, c_ref, nw_ref, sc_ref, sh_ref, hx_ref, hxt_ref):
        x = jnp.where(pl.program_id(0) < n_lat, x_ref[...], c_ref[...])
        h = x * _rms(x) * nw_ref[...] * (1.0 + sc_ref[...]) + sh_ref[...]
        hx_ref[...] = h.astype(BF)
        hxt_ref[...] = h.T.astype(BF)

    return pl.pallas_call(
        body, name="norm_fwd", grid=(rows // TM,),
        in_specs=[pl.BlockSpec((TM, D_MODEL), lambda i: (jnp.minimum(i, n_lat - 1), 0)),
                  pl.BlockSpec((TM, D_MODEL), lambda i: (jnp.maximum(i - n_lat, 0), 0)),
                  pl.BlockSpec((1, D_MODEL), lambda i: (0, 0)),
                  pl.BlockSpec((None, 1, D_MODEL), lambda i: (samp(i), 0, 0)),
                  pl.BlockSpec((None, 1, D_MODEL), lambda i: (samp(i), 0, 0))],
        out_specs=(pl.BlockSpec((TM, D_MODEL), lambda i: (i, 0)),
                   pl.BlockSpec((D_MODEL, TM), lambda i: (0, i))),
        out_shape=(SDS((rows, D_MODEL), BF), SDS((D_MODEL, rows), BF)),
        compiler_params=_cp(("parallel",), 40))(x_lat, x_ctx, norm_w, scale3, shift3)


def _norm_bwd(x_lat, x_ctx, dhx, gx_res, norm_w, scale3, tiles_per_sample, n_samp):
    rows = x_lat.shape[0] + x_ctx.shape[0]
    n_lat = tiles_per_sample * n_samp

    def samp(i):
        return jnp.minimum(i // tiles_per_sample, n_samp)

    def lat(i):
        return jnp.minimum(i, n_lat - 1)

    def body(x_ref, c_ref, dh_ref, gr_ref, nw_ref, sc_ref, gx_ref, dsh_ref, dsc_ref, dnw_ref):
        i = pl.program_id(0)
        x = jnp.where(i < n_lat, x_ref[...], c_ref[...])
        r = _rms(x)
        xh = x * r
        nw = nw_ref[...]
        dh = dh_ref[...]
        first = jnp.logical_or(i % tiles_per_sample == 0, i >= n_lat)

        @pl.when(first)
        def _():
            dsh_ref[...] = jnp.zeros_like(dsh_ref)
            dsc_ref[...] = jnp.zeros_like(dsc_ref)

        @pl.when(i == 0)
        def _():
            dnw_ref[...] = jnp.zeros_like(dnw_ref)

        dsh_ref[...] += jnp.sum(dh, axis=0, keepdims=True)
        dsc_ref[...] += jnp.sum(dh * (xh * nw), axis=0, keepdims=True)
        du = dh * (1.0 + sc_ref[...])
        dnw_ref[...] += jnp.sum(du * xh, axis=0, keepdims=True)

        @pl.when(i < n_lat)
        def _():
            gx_ref[...] = gr_ref[...] + _rms_bwd(du * nw, xh, r)

    return pl.pallas_call(
        body, name="norm_bwd", grid=(rows // TM,),
        in_specs=[pl.BlockSpec((TM, D_MODEL), lambda i: (lat(i), 0)),
                  pl.BlockSpec((TM, D_MODEL), lambda i: (jnp.maximum(i - n_lat, 0), 0)),
                  pl.BlockSpec((TM, D_MODEL), lambda i: (i, 0)),
                  pl.BlockSpec((TM, D_MODEL), lambda i: (lat(i), 0)),
                  pl.BlockSpec((1, D_MODEL), lambda i: (0, 0)),
                  pl.BlockSpec((None, 1, D_MODEL), lambda i: (samp(i), 0, 0))],
        out_specs=(pl.BlockSpec((TM, D_MODEL), lambda i: (lat(i), 0)),
                   pl.BlockSpec((None, 1, D_MODEL), lambda i: (samp(i), 0, 0)),
                   pl.BlockSpec((None, 1, D_MODEL), lambda i: (samp(i), 0, 0)),
                   pl.BlockSpec((1, D_MODEL), lambda i: (0, 0))),
        out_shape=(SDS((n_lat * TM, D_MODEL), F32), SDS((n_samp + 1, 1, D_MODEL), F32),
                   SDS((n_samp + 1, 1, D_MODEL), F32), SDS((1, D_MODEL), F32)),
        compiler_params=_cp(("arbitrary",), 40))(x_lat, x_ctx, dhx, gx_res, norm_w, scale3)


def _in_proj_gather(hx, w_buf, ids):
    rows = hx.shape[0]
    tb = _big_rows(rows)
    n_i = rows // tb
    hrows = D_MODEL // 2

    def body(ids_ref, h_ref, w_in_hbm, px_ref, w_hbm, wv, lsem, ssem, rsem):
        j, i = pl.program_id(0), pl.program_id(1)
        x, y, c, chips = _place()
        sibling = (x, y, 1 - c)

        def half(s, which):
            return w_hbm.at[s, pl.ds(which * hrows, hrows), :]

        def over_ici(rel):
            chip = chips[rel]
            mine, theirs = half(2 * x + y, c), half(2 * chip[0] + chip[1], c)
            return (_remote(mine, mine, ssem.at[rel], rsem.at[rel], (*chip, c)),
                    _remote(theirs, theirs, ssem.at[rel], rsem.at[rel], (*chip, c)))

        def over_d2d(rel):
            s = 2 * chips[rel][0] + chips[rel][1]
            return (_remote(half(s, c), half(s, c), ssem.at[3 + rel], rsem.at[3 + rel], sibling),
                    _remote(half(s, 1 - c), half(s, 1 - c), ssem.at[3 + rel], rsem.at[3 + rel], sibling))

        first_row_tile = i == 0

        @pl.when(jnp.logical_and(j == 0, first_row_tile))
        def _():
            over_ici(0)[0].start()
            over_ici(1)[0].start()

        @pl.when(jnp.logical_and(j == 1, first_row_tile))
        def _():
            for rel in range(2):
                over_ici(rel)[1].wait_recv()
                over_d2d(rel)[0].start()
            over_ici(2)[0].start()
            over_d2d(0)[1].wait_recv()

        @pl.when(jnp.logical_and(j == 2, first_row_tile))
        def _():
            over_d2d(1)[1].wait_recv()

        @pl.when(jnp.logical_and(j == 3, first_row_tile))
        def _():
            over_ici(2)[1].wait_recv()
            passed, landing = over_d2d(2)
            passed.start()
            landing.wait_recv()

        @pl.when(first_row_tile)
        def _():
            cp = pltpu.make_async_copy(w_hbm.at[ids_ref[4 + j]], wv, lsem)
            cp.start()
            cp.wait()

        px_ref[...] = jnp.dot(h_ref[...], wv[...], preferred_element_type=F32).astype(BF)

        @pl.when(jnp.logical_and(j == N_SHARD - 1, i == n_i - 1))
        def _():
            for rel in range(3):
                over_ici(rel)[0].wait_send()
                over_d2d(rel)[0].wait_send()

    return pl.pallas_call(
        body, name="in_proj_gather", input_output_aliases={2: 1},
        grid_spec=pltpu.PrefetchScalarGridSpec(
            num_scalar_prefetch=1, grid=(N_SHARD, n_i),
            in_specs=[pl.BlockSpec((tb, D_MODEL), lambda j, i, ids_ref: (i, 0)), ANY],
            out_specs=(pl.BlockSpec((tb, IN_W), lambda j, i, ids_ref: (i, ids_ref[4 + j])), ANY),
            scratch_shapes=[pltpu.VMEM((D_MODEL, IN_W), BF), pltpu.SemaphoreType.DMA,
                            pltpu.SemaphoreType.DMA((6,)), pltpu.SemaphoreType.DMA((6,))]),
        out_shape=(SDS((rows, IN_COLS), BF), SDS(w_buf.shape, w_buf.dtype)),
        compiler_params=_cp(("arbitrary", "arbitrary"), 56))(ids, hx, w_buf)


def _gw_in(hxt, dpx_all):
    rows = dpx_all.shape[0]
    tb = _big_rows(rows)

    def body(h_ref, d_ref, o_ref):
        @pl.when(pl.program_id(1) == 0)
        def _():
            o_ref[...] = jnp.zeros_like(o_ref)

        o_ref[...] += jnp.dot(h_ref[...], d_ref[...], preferred_element_type=F32)

    return pl.pallas_call(
        body, name="gw_in", grid=(N_IN_BLK, rows // tb),
        in_specs=[pl.BlockSpec((D_MODEL, tb), lambda j, i: (0, i)),
                  pl.BlockSpec((tb, IN_BLK), lambda j, i: (i, j))],
        out_specs=pl.BlockSpec((None, D_MODEL, IN_BLK), lambda j, i: (j // BPS, 0, j % BPS)),
        out_shape=SDS((N_SHARD, D_MODEL, IN_W), F32),
        compiler_params=_cp(("arbitrary", "arbitrary"), 56))(hxt, dpx_all)


def _dhx(dpx_all, w_in_g, tile0, n_tiles, dhx, comm):
    rows = dpx_all.shape[0]
    tb = _big_rows(rows)

    def body(d_ref, w_ref, *rest):
        o_ref = rest[-1]

        @pl.when(pl.program_id(1) == 0)
        def _():
            o_ref[...] = jnp.zeros_like(o_ref)

        o_ref[...] += lax.dot_general(d_ref[...], w_ref[...], (((1,), (1,)), ((), ())), preferred_element_type=F32)

    args, in_specs, aliases = [dpx_all, w_in_g], [
        pl.BlockSpec((tb, IN_BLK), lambda i, j: (tile0 + i, j)),
        pl.BlockSpec((None, D_MODEL, IN_BLK), lambda i, j: (j // BPS, 0, j % BPS))], None
    if dhx is not None:
        args, in_specs, aliases = args + [dhx], in_specs + [ANY], {2: 0}
    (out,), got = _call(body, args, comm, name="dhx", grid=(n_tiles, N_IN_BLK), in_specs=in_specs,
                        out_specs=(pl.BlockSpec((tb, D_MODEL), lambda i, j: (tile0 + i, 0)),),
                        out_shape=(SDS((rows, D_MODEL), F32),), aliases=aliases,
                        compiler_params=_cp(("arbitrary", "arbitrary"), 56))
    return out, got


def _decays(lgv, d):
    c = RET_CHUNK
    ii = lax.broadcasted_iota(jnp.int32, (c, 1), 0).astype(F32)
    jj = lax.broadcasted_iota(jnp.int32, (1, c), 1).astype(F32)
    a_i = jnp.where(d == 0, ii, c - 1.0 - ii)
    a_j = jnp.where(d == 0, jj, c - 1.0 - jj)
    rel = a_i - a_j
    mask = jnp.where(rel >= 0, jnp.exp(lgv * jnp.maximum(rel, 0.0)), 0.0)
    qd = jnp.exp(lgv * (a_i + 1.0))
    kd = jnp.exp(lgv * (c - 1.0 - a_i))
    gc = jnp.exp(jnp.full((1, 1), lgv * c, F32))
    return a_i, rel, mask, qd, kd, gc


def _ctx_state_fwd(px, lg, n_samp, t_lat, lc):
    rb = t_lat // lc

    def body(lg_ref, k_ref, v_ref, o_ref):
        h = pl.program_id(1)
        k = k_ref[...].astype(F32) * (RET_DK ** -0.5)
        v = v_ref[...]
        pos = lax.broadcasted_iota(jnp.int32, (lc, 1), 0).astype(F32)
        o_ref[0] = _dot(k * jnp.exp(lg_ref[0, h] * (lc - 1.0 - pos)), v, 0, 0)
        o_ref[1] = _dot(k * jnp.exp(lg_ref[1, h] * pos), v, 0, 0)

    return pl.pallas_call(
        body, name="ctx_state_fwd", grid=(n_samp, RET_HEADS),
        in_specs=[SMEM,
                  pl.BlockSpec((lc, RET_DK), lambda b, h: (rb + b, C_RK // RET_DK + h)),
                  pl.BlockSpec((lc, RET_DV), lambda b, h: (rb + b, C_RV // RET_DV + h))],
        out_specs=pl.BlockSpec((None, 2, None, RET_DK, RET_DV), lambda b, h: (b, 0, h, 0, 0)),
        out_shape=SDS((n_samp, 2, RET_HEADS, RET_DK, RET_DV), F32),
        compiler_params=_cp(("parallel", "parallel")))(lg, px, px)


def _ctx_state_bwd(dpx, px, dstates, lg, n_samp, t_lat, lc):
    rb = t_lat // lc
    kspec = pl.BlockSpec((lc, RET_DK), lambda b, h: (rb + b, C_RK // RET_DK + h))
    vspec = pl.BlockSpec((lc, RET_DV), lambda b, h: (rb + b, C_RV // RET_DV + h))
    sspec = pl.BlockSpec((None, 2, None, RET_DK, RET_DV), lambda b, h: (b, 0, h, 0, 0))

    def weights(lg_ref, h):
        pos = lax.broadcasted_iota(jnp.int32, (lc, 1), 0).astype(F32)
        e_f = lc - 1.0 - pos
        return pos, e_f, jnp.exp(lg_ref[0, h] * e_f), jnp.exp(lg_ref[1, h] * pos)

    def k_body(lg_ref, dpx_hbm, k_ref, v_ref, ds_ref, dk_ref, dlg_ref):
        pos, e_f, w_f, w_b = weights(lg_ref, pl.program_id(1))
        k = k_ref[...].astype(F32) * (RET_DK ** -0.5)
        y_f = _dot(v_ref[...], ds_ref[0], 1, 1) * w_f
        y_b = _dot(v_ref[...], ds_ref[1], 1, 1) * w_b
        dk_ref[...] = ((y_f + y_b) * (RET_DK ** -0.5)).astype(BF)
        t_f = _sum_all(e_f * k * y_f)
        t_b = _sum_all(pos * k * y_b)
        sub = lax.broadcasted_iota(jnp.int32, (8, 128), 0)
        dlg_ref[...] = jnp.where(sub == 0, t_f, jnp.where(sub == 1, t_b, 0.0))

    def v_body(lg_ref, dpx_hbm, k_ref, ds_ref, dv_ref):
        _, _, w_f, w_b = weights(lg_ref, pl.program_id(1))
        k = k_ref[...].astype(F32) * (RET_DK ** -0.5)
        dv_ref[...] = (_dot(k * w_f, ds_ref[0]) + _dot(k * w_b, ds_ref[1])).astype(BF)

    dpx, dlg = pl.pallas_call(
        k_body, name="ctx_state_bwd_k", grid=(n_samp, RET_HEADS), input_output_aliases={1: 0},
        in_specs=[SMEM, ANY, kspec, vspec, sspec],
        out_specs=(kspec, pl.BlockSpec((None, None, 8, 128), lambda b, h: (b, h, 0, 0))),
        out_shape=(SDS(dpx.shape, dpx.dtype), SDS((n_samp, RET_HEADS, 8, 128), F32)),
        compiler_params=_cp(("parallel", "parallel")))(lg, dpx, px, px, dstates)
    dpx = pl.pallas_call(
        v_body, name="ctx_state_bwd_v", grid=(n_samp, RET_HEADS), input_output_aliases={1: 0},
        in_specs=[SMEM, ANY, kspec, sspec], out_specs=vspec, out_shape=SDS(dpx.shape, dpx.dtype),
        compiler_params=_cp(("parallel", "parallel")))(lg, dpx, px, dstates)
    return dpx, dlg


def _zero_ctx_tail(dpx, t_lat):
    wb = 512
    n_ctx = (dpx.shape[0] - t_lat) // TM

    def body(dpx_hbm, o_ref):
        o_ref[...] = jnp.zeros_like(o_ref)

    return pl.pallas_call(
        body, name="zero_ctx_tail", grid=(n_ctx, (IN_COLS - KV_COLS) // wb), input_output_aliases={0: 0},
        in_specs=[ANY], out_specs=pl.BlockSpec((TM, wb), lambda i, j: (t_lat // TM + i, KV_COLS // wb + j)),
        out_shape=SDS(dpx.shape, dpx.dtype),
        compiler_params=_cp(("parallel", "parallel")))(dpx)


def _ret_specs(row_f, row_b):
    c = RET_CHUNK
    wq = RET_HEADS * RET_DK // 2
    wv = RET_HEADS * RET_DV // 2
    specs = []
    for row in (row_f, row_b):
        specs += [pl.BlockSpec((c, wq), lambda b, n, row=row: (row(b, n), C_RQ // wq)),
                  pl.BlockSpec((c, wq), lambda b, n, row=row: (row(b, n), C_RQ // wq + 1)),
                  pl.BlockSpec((c, 2 * wq), lambda b, n, row=row: (row(b, n), C_RK // (2 * wq))),
                  pl.BlockSpec((c, wv), lambda b, n, row=row: (row(b, n), C_RV // wv)),
                  pl.BlockSpec((c, wv), lambda b, n, row=row: (row(b, n), C_RV // wv + 1))]
    return specs


def _ret_head(refs, h):
    q0, q1, k_ref, v0, v1 = refs
    hh = h % 2
    q = (q0, q1)[h // 2][:, hh * RET_DK:(hh + 1) * RET_DK].astype(F32)
    k = k_ref[:, h * RET_DK:(h + 1) * RET_DK].astype(F32) * (RET_DK ** -0.5)
    v = (v0, v1)[h // 2][:, hh * RET_DV:(hh + 1) * RET_DV]
    return q, k, v


def _ret_fwd(px, states0, lg, n_samp, seq, comm):
    c = RET_CHUNK
    nc = seq // c
    t_lat = n_samp * seq
    wo = RET_HEADS * RET_DV

    def row_f(b, n):
        return b * nc + n

    def row_b(b, n):
        return b * nc + nc - 1 - n

    def body(lg_ref, *refs):
        ins, (s0_ref, of_ref, ob_ref, st_ref, s_s) = refs[:10], refs[10:]

        @pl.when(pl.program_id(1) == 0)
        def _():
            s_s[...] = s0_ref[...]

        for d, o_ref in ((0, of_ref), (1, ob_ref)):
            for h in range(RET_HEADS):
                _, _, mask, qd, kd, gc = _decays(lg_ref[d, h], d)
                q, k, v = _ret_head(ins[5 * d:5 * d + 5], h)
                s = s_s[d, h]
                st_ref[h, d] = s.astype(BF)
                sc = _dot(q, k, 1, 1) * mask
                o_ref[:, h * RET_DV:(h + 1) * RET_DV] = (_dot(sc, v) + _dot(q * qd, s)).astype(BF)
                s_s[d, h] = s * gc + _dot(k * kd, v, 0, 0)

    return _call(
        body, [lg] + [px] * 10 + [states0], comm, name="ret_fwd", grid=(n_samp, nc),
        in_specs=[SMEM] + _ret_specs(row_f, row_b) + [
            pl.BlockSpec((None, 2, RET_HEADS, RET_DK, RET_DV), lambda b, n: (b, 0, 0, 0, 0))],
        out_specs=(pl.BlockSpec((c, wo), lambda b, n: (row_f(b, n), 0)),
                   pl.BlockSpec((c, wo), lambda b, n: (row_b(b, n), 0)),
                   pl.BlockSpec((None, RET_HEADS, 2, None, RET_DK, RET_DV), lambda b, n: (b, 0, 0, n, 0, 0))),
        out_shape=(SDS((t_lat, wo), BF), SDS((t_lat, wo), BF),
                   SDS((n_samp, RET_HEADS, 2, nc, RET_DK, RET_DV), BF)),
        scratch_shapes=[pltpu.VMEM((2, RET_HEADS, RET_DK, RET_DV), F32)],
        compiler_params=_cp(("arbitrary", "arbitrary"), 48))


def _ret_bwd(dpx, px, do, saved, lg, n_samp, seq, comm):
    c = RET_CHUNK
    nc = seq // c
    assert nc % 2 == 0
    wq, wo = RET_HEADS * RET_DK, RET_HEADS * RET_DV

    def row_f(b, n):
        return b * nc + nc - 1 - n

    def row_b(b, n):
        return b * nc + n

    def body(lg_ref, *refs):
        ins = refs[:10]
        (dof_ref, dob_ref, st_ref, dpx_in, dpx_hbm, ds0_ref, dlg_ref,
         ds_s, acc_s, sq_s, sk_s, sv_s, sems) = refs[10:]
        b, n = pl.program_id(0), pl.program_id(1)
        second = n >= nc // 2
        chunks = (nc - 1 - n, n)

        def parked(ch):
            return pl.ds(pl.multiple_of(ch * c, c), c)

        def flush():
            cps = []
            for d, ch in enumerate(chunks):
                rows = pl.ds(pl.multiple_of((b * nc + ch) * c, c), c)
                cps += [pltpu.make_async_copy(sq_s.at[parked(ch), :], dpx_hbm.at[rows, pl.ds(C_RQ, wq)], sems.at[3 * d]),
                        pltpu.make_async_copy(sk_s.at[parked(ch), :], dpx_hbm.at[rows, pl.ds(C_RK, wq)],
                                              sems.at[3 * d + 1]),
                        pltpu.make_async_copy(sv_s.at[parked(ch), :], dpx_hbm.at[rows, pl.ds(C_RV, wo)],
                                              sems.at[3 * d + 2])]
            return cps

        @pl.when(jnp.logical_or(n > nc // 2, jnp.logical_and(n == 0, b > 0)))
        def _():
            for cp in flush():
                cp.wait()

        @pl.when(n == 0)
        def _():
            ds_s[...] = jnp.zeros_like(ds_s)
            acc_s[...] = jnp.zeros_like(acc_s)

        def chains(first_visit):
            for d, do_ref in enumerate((dof_ref, dob_ref)):
                rows = parked(chunks[d])
                for h in range(RET_HEADS):
                    a_i, rel, mask, qd, kd, gc = _decays(lg_ref[d, h], d)
                    q, k, v = _ret_head(ins[5 * d:5 * d + 5], h)
                    qb, kb, vb = q.astype(BF), k.astype(BF), v.astype(BF)
                    cq, cv = slice(h * RET_DK, (h + 1) * RET_DK), slice(h * RET_DV, (h + 1) * RET_DV)
                    dob = do_ref[:, cv].astype(BF)
                    sb = st_ref[h, d]
                    ds = ds_s[d, h]
                    dsb = ds.astype(BF)
                    raw = _dot(qb, kb, 1, 1)
                    sc = raw * mask
                    dsc = _dot(dob, vb, 1, 1) * mask
                    dscb = dsc.astype(BF)
                    x = _dot(dob, sb, 1, 1)
                    y = _dot(vb, dsb, 1, 1)
                    qq = q * qd
                    kk = k * kd
                    dq = _dot(dscb, kb) + x * qd
                    dk = _dot(dscb, qb, 0, 0) + y * kd
                    dv = _dot(sc, dob, 0, 0) + _dot(kk, dsb)
                    if first_visit:
                        sq_s[rows, cq] = dq.astype(BF)
                        sk_s[rows, cq] = dk.astype(BF)
                        sv_s[rows, cv] = dv.astype(BF)
                    else:
                        sq_s[rows, cq] = (sq_s[rows, cq].astype(F32) + dq).astype(BF)
                        sk_s[rows, cq] = ((sk_s[rows, cq].astype(F32) + dk) * (RET_DK ** -0.5)).astype(BF)
                        sv_s[rows, cv] = (sv_s[rows, cv].astype(F32) + dv).astype(BF)
                    t = (_sum_all(dsc * raw * rel) + _sum_all((a_i + 1.0) * qq * x)
                         + _sum_all((c - 1.0 - a_i) * kk * y) + c * gc * _sum_all(ds * sb.astype(F32)))
                    acc_s[4 * d + h:4 * d + h + 1, :] += t
                    ds_s[d, h] = ds * gc + _dot(qq, dob, 0, 0)

        @pl.when(jnp.logical_not(second))
        def _():
            chains(True)

        @pl.when(second)
        def _():
            chains(False)
            for cp in flush():
                cp.start()

        @pl.when(n == nc - 1)
        def _():
            ds0_ref[...] = ds_s[...]
            dlg_ref[...] = acc_s[...]

        @pl.when(jnp.logical_and(b == n_samp - 1, n == nc - 1))
        def _():
            for cp in flush():
                cp.wait()

    do_spec_f = pl.BlockSpec((c, wo), lambda b, n: (row_f(b, n), 0))
    do_spec_b = pl.BlockSpec((c, wo), lambda b, n: (row_b(b, n), 0))
    return _call(
        body, [lg] + [px] * 10 + [do, do, saved, dpx], comm, name="ret_bwd", grid=(n_samp, nc), aliases={14: 0},
        in_specs=[SMEM] + _ret_specs(row_f, row_b) + [
            do_spec_f, do_spec_b,
            pl.BlockSpec((None, RET_HEADS, 2, None, RET_DK, RET_DV), lambda b, n: (b, 0, 0, nc - 1 - n, 0, 0)),
            ANY],
        out_specs=(ANY,
                   pl.BlockSpec((None, 2, RET_HEADS, RET_DK, RET_DV), lambda b, n: (b, 0, 0, 0, 0)),
                   pl.BlockSpec((None, 8, 128), lambda b, n: (b, 0, 0))),
        out_shape=(SDS(dpx.shape, dpx.dtype),
                   SDS((n_samp, 2, RET_HEADS, RET_DK, RET_DV), F32), SDS((n_samp, 8, 128), F32)),
        scratch_shapes=[pltpu.VMEM((2, RET_HEADS, RET_DK, RET_DV), F32), pltpu.VMEM((8, 128), F32),
                        pltpu.VMEM((seq, wq), BF), pltpu.VMEM((seq, wq), BF), pltpu.VMEM((seq, wo), BF),
                        pltpu.SemaphoreType.DMA((6,))],
        compiler_params=_cp(("arbitrary", "arbitrary"), 60))


def _norm_rope(x, w, cos, sin):
    xn = x * _rms(x) * w
    return xn * cos + _swap_pairs(xn) * sin


def _norm_rope_bwd(dy, x, w, cos, sin):
    dxn = dy * cos + _swap_pairs(dy * sin)
    r = _rms(x)
    xh = x * r
    return _rms_bwd(dxn * w, xh, r), jnp.sum(dxn * xh, axis=0, keepdims=True)


def _att_prep_q(px, cos_all, sin_all, qnw, t_lat):
    hd = ATT_HEAD_DIM
    wblk = ATT_REP * hd

    def body(x_ref, cos_ref, sin_ref, w_ref, o_ref):
        for r in range(ATT_REP):
            cols = slice(r * hd, (r + 1) * hd)
            qr = _norm_rope(x_ref[:, cols].astype(F32), w_ref[...], cos_ref[...], sin_ref[...])
            o_ref[:, cols] = (qr * (hd ** -0.5)).astype(BF)

    return pl.pallas_call(
        body, name="att_prep_q", grid=(t_lat // TM, ATT_KV_HEADS),
        in_specs=[pl.BlockSpec((TM, wblk), lambda i, g: (i, C_AQ // wblk + g)),
                  pl.BlockSpec((TM, hd), lambda i, g: (i, 0)),
                  pl.BlockSpec((TM, hd), lambda i, g: (i, 0)),
                  pl.BlockSpec((1, hd), lambda i, g: (0, 0))],
        out_specs=pl.BlockSpec((TM, wblk), lambda i, g: (i, g)),
        out_shape=SDS((t_lat, ATT_HEADS * hd), BF),
        compiler_params=_cp(("parallel", "parallel")))(px, cos_all, sin_all, qnw)


def _att_prep_kv(px, cos_all, sin_all, knw):
    rows = px.shape[0]
    hd = ATT_HEAD_DIM
    kvw = ATT_KV_HEADS * hd

    def body(x_ref, cos_ref, sin_ref, w_ref, k_ref, v_ref):
        for g in range(ATT_KV_HEADS):
            cols = slice(g * hd, (g + 1) * hd)
            k_ref[:, cols] = _norm_rope(x_ref[:, cols].astype(F32), w_ref[...], cos_ref[...],
                                        sin_ref[...]).astype(BF)
            v_ref[:, 2 * g * hd:(2 * g + 1) * hd] = x_ref[:, kvw + g * hd:kvw + (g + 1) * hd].astype(BF)
            v_ref[:, (2 * g + 1) * hd:(2 * g + 2) * hd] = jnp.ones((TM, hd), BF)

    return pl.pallas_call(
        body, name="att_prep_kv", grid=(rows // TM,),
        in_specs=[pl.BlockSpec((TM, 2 * kvw), lambda i: (i, C_AK // (2 * kvw))),
                  pl.BlockSpec((TM, hd), lambda i: (i, 0)),
                  pl.BlockSpec((TM, hd), lambda i: (i, 0)),
                  pl.BlockSpec((1, hd), lambda i: (0, 0))],
        out_specs=(pl.BlockSpec((TM, kvw), lambda i: (i, 0)), pl.BlockSpec((TM, 2 * kvw), lambda i: (i, 0))),
        out_shape=(SDS((rows, kvw), BF), SDS((rows, 2 * kvw), BF)),
        compiler_params=_cp(("parallel",)))(px, cos_all, sin_all, knw)


def _att_kv_bwd(dpx, dkl, dkc, dvl, dvc, px, cos_all, sin_all, knw):
    rows = px.shape[0]
    hd = ATT_HEAD_DIM
    kvw = ATT_KV_HEADS * hd
    n_lat = dkl.shape[0] // TM
    assert dkc.shape[0] == TM

    def body(dpx_hbm, dkl_ref, dkc_ref, dvl_ref, dvc_ref, x_ref, cos_ref, sin_ref, w_ref, o_ref, gw_ref):
        i = pl.program_id(0)

        @pl.when(i == 0)
        def _():
            gw_ref[...] = jnp.zeros_like(gw_ref)

        is_lat = i < n_lat
        dk = jnp.where(is_lat, dkl_ref[...], dkc_ref[...])
        dv = jnp.where(is_lat, dvl_ref[...], dvc_ref[...])
        for g in range(ATT_KV_HEADS):
            cols = slice(g * hd, (g + 1) * hd)
            dx, gw = _norm_rope_bwd(dk[:, cols], x_ref[:, cols].astype(F32), w_ref[...], cos_ref[...], sin_ref[...])
            o_ref[:, cols] = dx.astype(BF)
            gw_ref[...] += gw
        o_ref[:, kvw:] = dv.astype(BF)

    lat = pl.BlockSpec((TM, kvw), lambda i: (jnp.minimum(i, n_lat - 1), 0))
    ctx = pl.BlockSpec((TM, kvw), lambda i: (0, 0))
    kvcol = pl.BlockSpec((TM, 2 * kvw), lambda i: (i, C_AK // (2 * kvw)))
    return pl.pallas_call(
        body, name="att_kv_bwd", grid=(rows // TM,), input_output_aliases={0: 0},
        in_specs=[ANY, lat, ctx, lat, ctx, kvcol,
                  pl.BlockSpec((TM, hd), lambda i: (i, 0)),
                  pl.BlockSpec((TM, hd), lambda i: (i, 0)),
                  pl.BlockSpec((1, hd), lambda i: (0, 0))],
        out_specs=(kvcol, pl.BlockSpec((1, hd), lambda i: (0, 0))),
        out_shape=(SDS(dpx.shape, dpx.dtype), SDS((1, hd), F32)),
        compiler_params=_cp(("arbitrary",)))(dpx, dkl, dkc, dvl, dvc, px, cos_all, sin_all, knw)


def _stack_heads(ref_or_val):
    hd = ATT_HEAD_DIM
    return jnp.concatenate([ref_or_val[:, r * hd:(r + 1) * hd] for r in range(ATT_REP)], axis=0)


def _att_scores(q, kl, kc):
    sl = _dot(q, kl, 1, 1)
    sc = _dot(q, kc, 1, 1)
    m = jnp.maximum(jnp.max(sl, axis=-1, keepdims=True), jnp.max(sc, axis=-1, keepdims=True))
    return jnp.exp(sl - m), jnp.exp(sc - m), m


def _att_fwd(qn, kn, vn, n_samp, seq, lc, comm):
    hd = ATT_HEAD_DIM
    tq = ATT_TQ
    nq = seq // tq
    wblk = ATT_REP * hd
    cb = n_samp * seq // lc
    t_lat = n_samp * seq

    def body(q_ref, kl_ref, kc_ref, vl_ref, vc_ref, o_ref, lse_ref):
        lane = lax.broadcasted_iota(jnp.int32, (tq, hd), 1)
        lse = jnp.zeros((tq, hd), F32)
        for r in range(ATT_REP):
            cols = slice(r * hd, (r + 1) * hd)
            el, ec, m = _att_scores(q_ref[:, cols], kl_ref[...], kc_ref[...])
            pv = _dot(el, vl_ref[...]) + _dot(ec, vc_ref[...])
            denom = pv[:, hd:hd + 1]
            o_ref[:, cols] = (pv[:, :hd] / denom).astype(BF)
            lse = jnp.where(lane == r, m + jnp.log(denom), lse)
        lse_ref[...] = lse

    return _call(
        body, [qn, kn, kn, vn, vn], comm, name="att_fwd", grid=(n_samp, ATT_KV_HEADS, nq),
        in_specs=[pl.BlockSpec((tq, wblk), lambda b, g, i: (b * nq + i, g)),
                  pl.BlockSpec((seq, hd), lambda b, g, i: (b, g)),
                  pl.BlockSpec((lc, hd), lambda b, g, i: (cb + b, g)),
                  pl.BlockSpec((seq, 2 * hd), lambda b, g, i: (b, g)),
                  pl.BlockSpec((lc, 2 * hd), lambda b, g, i: (cb + b, g))],
        out_specs=(pl.BlockSpec((tq, wblk), lambda b, g, i: (b * nq + i, g)),
                   pl.BlockSpec((tq, hd), lambda b, g, i: (b * nq + i, g))),
        out_shape=(SDS((t_lat, ATT_HEADS * hd), BF), SDS((t_lat, ATT_KV_HEADS * hd), F32)),
        compiler_params=_cp(("arbitrary", "arbitrary", "arbitrary"), 48))


def _att_bwd(dpx, qn, kn, vn, px, o_att, lse, do_att, cos_all, sin_all, qnw, n_samp, seq, lc, comm):
    hd = ATT_HEAD_DIM
    tq = ATT_TQ
    nq = seq // tq
    wblk = ATT_REP * hd
    cb = n_samp * seq // lc
    t_lat = n_samp * seq
    kvw = ATT_KV_HEADS * hd
    scale = hd ** -0.5

    def body(dpx_hbm, q_ref, kl_ref, kc_ref, vl_ref, vc_ref, o_ref, do_ref, x_ref, cos_ref, sin_ref, w_ref,
             lse_ref, dq_ref, dkl_ref, dkc_ref, dvl_ref, dvc_ref, gw_ref, akl, akc, avl, avc, aw):
        i = pl.program_id(2)

        @pl.when(i == 0)
        def _():
            akl[...] = jnp.zeros_like(akl)
            akc[...] = jnp.zeros_like(akc)
            avl[...] = jnp.zeros_like(avl)
            avc[...] = jnp.zeros_like(avc)
            aw[...] = jnp.zeros_like(aw)

        dobs, pls, pcs, dsls, dscs = [], [], [], [], []
        for r in range(ATT_REP):
            cols = slice(r * hd, (r + 1) * hd)
            dob = do_ref[:, cols]
            delta = jnp.sum(dob.astype(F32) * o_ref[:, cols].astype(F32), axis=-1, keepdims=True)
            lse = lse_ref[:, r:r + 1]
            p_l = jnp.exp(_dot(q_ref[:, cols], kl_ref[...], 1, 1) - lse).astype(BF)
            p_c = jnp.exp(_dot(q_ref[:, cols], kc_ref[...], 1, 1) - lse).astype(BF)
            ds_l = (p_l * (_dot(dob, vl_ref[...], 1, 1) - delta)).astype(BF)
            ds_c = (p_c * (_dot(dob, vc_ref[...], 1, 1) - delta)).astype(BF)
            dq = (_dot(ds_l, kl_ref[...]) + _dot(ds_c, kc_ref[...])) * scale
            dx, gw = _norm_rope_bwd(dq, x_ref[:, cols].astype(F32), w_ref[...], cos_ref[...], sin_ref[...])
            dq_ref[:, cols] = dx.astype(BF)
            aw[...] += gw
            dobs.append(dob)
            pls.append(p_l)
            pcs.append(p_c)
            dsls.append(ds_l)
            dscs.append(ds_c)
        do4 = jnp.concatenate(dobs, axis=0)
        q4 = _stack_heads(q_ref)
        avl[...] += _dot(jnp.concatenate(pls, axis=0), do4, 0, 0)
        avc[...] += _dot(jnp.concatenate(pcs, axis=0), do4, 0, 0)
        akl[...] += _dot(jnp.concatenate(dsls, axis=0), q4, 0, 0)
        akc[...] += _dot(jnp.concatenate(dscs, axis=0), q4, 0, 0)

        @pl.when(i == nq - 1)
        def _():
            dkl_ref[...] = akl[...]
            dkc_ref[...] = akc[...]
            dvl_ref[...] = avl[...]
            dvc_ref[...] = avc[...]
            gw_ref[...] = aw[...]

    return _call(
        body, [dpx, qn, kn, kn, vn, vn, o_att, do_att, px, cos_all, sin_all, qnw, lse], comm,
        name="att_bwd", grid=(n_samp, ATT_KV_HEADS, nq), aliases={0: 0},
        in_specs=[ANY,
                  pl.BlockSpec((tq, wblk), lambda b, g, i: (b * nq + i, g)),
                  pl.BlockSpec((seq, hd), lambda b, g, i: (b, g)),
                  pl.BlockSpec((lc, hd), lambda b, g, i: (cb + b, g)),
                  pl.BlockSpec((seq, hd), lambda b, g, i: (b, 2 * g)),
                  pl.BlockSpec((lc, hd), lambda b, g, i: (cb + b, 2 * g)),
                  pl.BlockSpec((tq, wblk), lambda b, g, i: (b * nq + i, g)),
                  pl.BlockSpec((tq, wblk), lambda b, g, i: (b * nq + i, g)),
                  pl.BlockSpec((tq, wblk), lambda b, g, i: (b * nq + i, C_AQ // wblk + g)),
                  pl.BlockSpec((tq, hd), lambda b, g, i: (b * nq + i, 0)),
                  pl.BlockSpec((tq, hd), lambda b, g, i: (b * nq + i, 0)),
                  pl.BlockSpec((1, hd), lambda b, g, i: (0, 0)),
                  pl.BlockSpec((tq, hd), lambda b, g, i: (b * nq + i, g))],
        out_specs=(pl.BlockSpec((tq, wblk), lambda b, g, i: (b * nq + i, C_AQ // wblk + g)),
                   pl.BlockSpec((seq, hd), lambda b, g, i: (b, g)),
                   pl.BlockSpec((lc, hd), lambda b, g, i: (b, g)),
                   pl.BlockSpec((seq, hd), lambda b, g, i: (b, g)),
                   pl.BlockSpec((lc, hd), lambda b, g, i: (b, g)),
                   pl.BlockSpec((None, None, 1, hd), lambda b, g, i: (b, g, 0, 0))),
        out_shape=(SDS(dpx.shape, dpx.dtype),
                   SDS((t_lat, kvw), F32), SDS((n_samp * lc, kvw), F32),
                   SDS((t_lat, kvw), F32), SDS((n_samp * lc, kvw), F32),
                   SDS((n_samp, ATT_KV_HEADS, 1, hd), F32)),
        scratch_shapes=[pltpu.VMEM((seq, hd), F32), pltpu.VMEM((lc, hd), F32),
                        pltpu.VMEM((seq, hd), F32), pltpu.VMEM((lc, hd), F32), pltpu.VMEM((1, hd), F32)],
        compiler_params=_cp(("arbitrary", "arbitrary", "arbitrary"), 56))


def _merge(x_lat, target, o_f, o_b, o_att, px, gate3, w_o_ret, w_o_att, w_out, tiles_per_sample):
    t_lat = x_lat.shape[0]
    tm = 256
    n_t = t_lat // tm
    per = tiles_per_sample * (TM // tm)
    d = D_MODEL
    rv = RET_HEADS * RET_DV
    n_samp = gate3.shape[0] - 1

    half = d // 2
    n_px = 10

    def body(x_ref, t_ref, of_ref, ob_ref, oa_ref, *rest):
        pxs, rest = rest[:n_px], rest[n_px:]
        (gt_ref, wor_ref, woa_ref, wout_ref,
         gx_ref, dor_ref, doa_ref, dpx_hbm, loss_ref, dgt_ref, gwor_hbm, gwoa_hbm, gwout_hbm,
         aor, aoa, aout, drg_ref, dtail_ref, sems) = rest
        i = pl.program_id(0)

        def copies(step):
            rows = pl.ds(pl.multiple_of(step * tm, tm), tm)
            return (pltpu.make_async_copy(drg_ref, dpx_hbm.at[rows, pl.ds(C_RG, rv)], sems.at[0]),
                    pltpu.make_async_copy(dtail_ref, dpx_hbm.at[rows, pl.ds(C_AG, 3 * d)], sems.at[1]))

        @pl.when(i == 0)
        def _():
            aor[...] = jnp.zeros_like(aor)
            aoa[...] = jnp.zeros_like(aoa)
            aout[...] = jnp.zeros_like(aout)
            loss_ref[...] = jnp.zeros_like(loss_ref)

        @pl.when(i % per == 0)
        def _():
            dgt_ref[...] = jnp.zeros_like(dgt_ref)

        def cat(refs):
            return jnp.concatenate([r[...] for r in refs], axis=1).astype(F32)

        def ret_head(h):
            cols = slice(h * RET_DV, (h + 1) * RET_DV)
            o = of_ref[:, cols].astype(F32) + ob_ref[:, cols].astype(F32)
            r = _rms(o)
            g = pxs[h][...].astype(F32)
            return o * r, r, g, _sigmoid(g)

        def att_half(k):
            o = oa_ref[:, k * half:(k + 1) * half].astype(F32)
            g = pxs[4 + k][...].astype(F32)
            return o, g, _sigmoid(g)

        yrs = []
        for h in range(RET_HEADS):
            on, _, g, sg = ret_head(h)
            yrs.append((on * (g * sg)).astype(BF))
        yr = jnp.concatenate(yrs, axis=1)
        yas = []
        for k in range(2):
            o, g, sg = att_half(k)
            yas.append((o * (g * sg)).astype(BF))
        ya = jnp.concatenate(yas, axis=1)

        a = jnp.dot(yr, wor_ref[...], preferred_element_type=F32)
        b = jnp.dot(ya, woa_ref[...], preferred_element_type=F32)
        sr = _sigmoid(cat(pxs[6:8]))
        sa = _sigmoid(cat(pxs[8:10]))
        yb = (sr * a + sa * b).astype(BF)
        out = jnp.dot(yb, wout_ref[...], preferred_element_type=F32)
        gate = gt_ref[...]
        err = x_ref[...] + gate * out - t_ref[...]
        loss_ref[...] += 0.5 * _sum_all(err * err) * (1.0 / d)
        dy_tok = err * (1.0 / d)
        gx_ref[...] = dy_tok
        dgt_ref[...] += jnp.sum(dy_tok * out, axis=0, keepdims=True)
        dout = (dy_tok * gate).astype(BF)
        aout[...] += _dot(yb, dout, 0, 0)
        dyy = _dot(dout, wout_ref[...], 1, 1)
        da = (dyy * sr).astype(BF)
        db = (dyy * sa).astype(BF)
        aor[...] += _dot(yr, da, 0, 0)
        aoa[...] += _dot(ya, db, 0, 0)
        dyr = _dot(da, wor_ref[...], 1, 1)
        dya = _dot(db, woa_ref[...], 1, 1)

        @pl.when(i > 0)
        def _():
            for cp in copies(i - 1):
                cp.wait()

        dtail_ref[:, d:2 * d] = (dyy * a * (sr * (1.0 - sr))).astype(BF)
        dtail_ref[:, 2 * d:] = (dyy * b * (sa * (1.0 - sa))).astype(BF)
        for h in range(RET_HEADS):
            cols = slice(h * RET_DV, (h + 1) * RET_DV)
            on, r, g, sg = ret_head(h)
            dy = dyr[:, cols]
            drg_ref[:, cols] = (dy * on * (sg * (1.0 + g * (1.0 - sg)))).astype(BF)
            dor_ref[:, cols] = _rms_bwd(dy * (g * sg), on, r).astype(BF)
        for k in range(2):
            cols = slice(k * half, (k + 1) * half)
            o, g, sg = att_half(k)
            dy = dya[:, cols]
            dtail_ref[:, cols] = (dy * o * (sg * (1.0 + g * (1.0 - sg)))).astype(BF)
            doa_ref[:, cols] = (dy * (g * sg)).astype(BF)
        for cp in copies(i):
            cp.start()

        @pl.when(i == n_t - 1)
        def _():
            for cp in copies(i):
                cp.wait()
            pltpu.sync_copy(aor, gwor_hbm)
            pltpu.sync_copy(aoa, gwoa_hbm)
            pltpu.sync_copy(aout, gwout_hbm)

    def px_blk(col):
        return pl.BlockSpec((tm, half), lambda i: (i, col // half))

    def resident(shape):
        return pl.BlockSpec(shape, lambda i: (0, 0), pipeline_mode=pl.Buffered(1))

    px_cols = ([C_RG + k * half for k in range(4)] + [C_AG, C_AG + half]
               + [C_MR, C_MR + half, C_MA, C_MA + half])
    return pl.pallas_call(
        body, name="merge", grid=(n_t,),
        in_specs=[pl.BlockSpec((tm, d), lambda i: (i, 0)),
                  pl.BlockSpec((tm, d), lambda i: (i, 0)),
                  pl.BlockSpec((tm, rv), lambda i: (i, 0)),
                  pl.BlockSpec((tm, rv), lambda i: (i, 0)),
                  pl.BlockSpec((tm, d), lambda i: (i, 0))]
        + [px_blk(col) for col in px_cols]
        + [pl.BlockSpec((None, 1, d), lambda i: (i // per, 0, 0)),
           resident((rv, d)), resident((d, d)), resident((d, d))],
        out_specs=(pl.BlockSpec((tm, d), lambda i: (i, 0)),
                   pl.BlockSpec((tm, rv), lambda i: (i, 0)),
                   pl.BlockSpec((tm, d), lambda i: (i, 0)),
                   ANY,
                   pl.BlockSpec((8, 128), lambda i: (0, 0)),
                   pl.BlockSpec((None, 1, d), lambda i: (i // per, 0, 0)),
                   ANY, ANY, ANY),
        out_shape=(SDS((t_lat, d), F32), SDS((t_lat, rv), BF), SDS((t_lat, d), BF),
                   SDS((px.shape[0], IN_COLS), BF),
                   SDS((8, 128), F32), SDS((n_samp, 1, d), F32),
                   SDS((rv, d), F32), SDS((d, d), F32), SDS((d, d), F32)),
        scratch_shapes=[pltpu.VMEM((rv, d), F32), pltpu.VMEM((d, d), F32), pltpu.VMEM((d, d), F32),
                        pltpu.VMEM((tm, rv), BF), pltpu.VMEM((tm, 3 * d), BF), pltpu.SemaphoreType.DMA((2,))],
        compiler_params=_cp(("arbitrary",), 56))(
            x_lat, target, o_f, o_b, o_att, *([px] * n_px), gate3, w_o_ret, w_o_att, w_out)


def _place():
    x, y, c = lax.axis_index("x"), lax.axis_index("y"), lax.axis_index("c")
    chips = [(1 - x, y), (x, 1 - y), (1 - x, 1 - y)]
    return x, y, c, chips


def _remote(src, dst, send_sem, recv_sem, to):
    return pltpu.make_async_remote_copy(src_ref=src, dst_ref=dst, send_sem=send_sem, recv_sem=recv_sem,
                                        device_id=to, device_id_type=MESH)


def _place_ids():
    x, y, c = lax.axis_index("x"), lax.axis_index("y"), lax.axis_index("c")
    me = 2 * x + y
    return jnp.stack([x, y, c, me, me, 2 * (1 - x) + y, 2 * x + 1 - y, 2 * (1 - x) + 1 - y]).astype(jnp.int32)


def _ag_comm(bufs):
    n, m = len(bufs), 3

    def half(ref, s, which):
        h = ref.shape[1] // 2
        return ref.at[s, pl.ds(which * h, h), :]

    def ici(ins, outs, ssem, rsem, base):
        x, y, c, chips = _place()
        sends, recvs = [], []
        for a in range(n):
            for j in range(m):
                k, chip = base + a * m + j, chips[j]
                mine, theirs = half(outs[a], 2 * x + y, c), half(outs[a], 2 * chip[0] + chip[1], c)
                sends.append(_remote(mine, mine, ssem.at[k], rsem.at[k], (*chip, c)))
                recvs.append(_remote(theirs, theirs, ssem.at[k], rsem.at[k], (*chip, c)))
        return sends, recvs

    def d2d(ins, outs, ssem, rsem, base):
        x, y, c, chips = _place()
        sends, recvs = [], []
        for a in range(n):
            for j in range(m):
                k, s = base + (n + a) * m + j, 2 * chips[j][0] + chips[j][1]
                sends.append(_remote(half(outs[a], s, c), half(outs[a], s, c), ssem.at[k], rsem.at[k], (x, y, 1 - c)))
                recvs.append(_remote(half(outs[a], s, 1 - c), half(outs[a], s, 1 - c), ssem.at[k], rsem.at[k],
                                     (x, y, 1 - c)))
        return sends, recvs

    return _Comm("all_gather", tuple(bufs), tuple(SDS(b.shape, b.dtype) for b in bufs), {a: a for a in range(n)},
                 2 * n * m, (ici, d2d))


def _swap_comm(grads):
    n = len(grads)

    def phase(ins, outs, ssem, rsem, base):
        x, y, c, _ = _place()
        sends = []
        for a in range(n):
            h = ins[a].shape[1] // 2
            sends.append(_remote(ins[a].at[:, pl.ds((1 - c) * h, h), :], outs[a], ssem.at[base + a],
                                 rsem.at[base + a], (x, y, 1 - c)))
        return sends, sends

    return _Comm("swap_halves", tuple(grads),
                 tuple(SDS((g.shape[0], g.shape[1] // 2, g.shape[2]), g.dtype) for g in grads), {}, n, (phase,))


def _exchange_comm(parts):
    n = len(parts)

    def phase(ins, outs, ssem, rsem, base):
        x, y, c, chips = _place()
        sends = []
        for a in range(n):
            for j, chip in enumerate(chips):
                k = base + 3 * a + j
                sends.append(_remote(ins[a].at[2 * chip[0] + chip[1]], outs[a].at[j], ssem.at[k], rsem.at[k],
                                     (*chip, c)))
        return sends, sends

    return _Comm("exchange_shards", tuple(parts), tuple(SDS((3,) + p.shape[1:], p.dtype) for p in parts), {}, 3 * n,
                 (phase,))


def _join_comm(bufs):
    n = len(bufs)

    def phase(ins, outs, ssem, rsem, base):
        x, y, c, _ = _place()
        sends, recvs = [], []
        for a in range(n):
            h = outs[a].shape[0] // 2
            mine, other = outs[a].at[pl.ds(c * h, h), :], outs[a].at[pl.ds((1 - c) * h, h), :]
            sends.append(_remote(mine, mine, ssem.at[base + a], rsem.at[base + a], (x, y, 1 - c)))
            recvs.append(_remote(other, other, ssem.at[base + a], rsem.at[base + a], (x, y, 1 - c)))
        return sends, recvs

    return _Comm("join_halves", tuple(bufs), tuple(SDS(b.shape, b.dtype) for b in bufs), {a: a for a in range(n)},
                 n, (phase,))


def _cast_place(w, ids):
    rows, cols = w.shape
    tr = min(rows, 256)

    def body(ids_ref, w_ref, o_ref):
        o_ref[...] = w_ref[...].astype(BF)

    return pl.pallas_call(
        body, name="cast_place",
        grid_spec=pltpu.PrefetchScalarGridSpec(
            num_scalar_prefetch=1, grid=(rows // tr,),
            in_specs=[pl.BlockSpec((tr, cols), lambda i, ids_ref: (i, 0))],
            out_specs=pl.BlockSpec((None, tr, cols), lambda i, ids_ref: (ids_ref[3], i, 0))),
        out_shape=SDS((N_SHARD, rows, cols), BF),
        compiler_params=_cp(("parallel",), 40))(ids, w)


def _chip_sum(g, p, ids):
    n_s, rows, cols = g.shape
    h = rows // 2
    tr = min(h, 256)
    nb = h // tr

    def body(ids_ref, g_ref, p_ref, o_ref, o16_ref):
        t = g_ref[...] + p_ref[...]
        o_ref[...] = t
        o16_ref[...] = t.astype(BF)

    out_spec = pl.BlockSpec((None, tr, cols), lambda s, i, ids_ref: (s, i, 0))
    return pl.pallas_call(
        body, name="chip_sum",
        grid_spec=pltpu.PrefetchScalarGridSpec(
            num_scalar_prefetch=1, grid=(n_s, nb),
            in_specs=[pl.BlockSpec((None, tr, cols), lambda s, i, ids_ref: (s, ids_ref[2] * nb + i, 0)),
                      pl.BlockSpec((None, tr, cols), lambda s, i, ids_ref: (s, i, 0))],
            out_specs=(out_spec, out_spec)),
        out_shape=(SDS((n_s, h, cols), g.dtype), SDS((n_s, h, cols), BF)),
        compiler_params=_cp(("parallel", "parallel"), 40))(ids, g, p)


def _shard_sum(t, q, ids):
    _, h, cols = t.shape
    tr = min(h, 256)
    nb = h // tr

    def body(ids_ref, t_ref, q_ref, o_ref):
        o_ref[...] = ((t_ref[...] + q_ref[0].astype(F32)) + q_ref[1].astype(F32)) + q_ref[2].astype(F32)

    return pl.pallas_call(
        body, name="shard_sum",
        grid_spec=pltpu.PrefetchScalarGridSpec(
            num_scalar_prefetch=1, grid=(nb,),
            in_specs=[pl.BlockSpec((None, tr, cols), lambda i, ids_ref: (ids_ref[3], i, 0)),
                      pl.BlockSpec((3, tr, cols), lambda i, ids_ref: (0, i, 0))],
            out_specs=pl.BlockSpec((tr, cols), lambda i, ids_ref: (ids_ref[2] * nb + i, 0))),
        out_shape=SDS((2 * h, cols), t.dtype),
        compiler_params=_cp(("parallel",), 40))(ids, t, q)


def _gather_small(block, n_sum):
    rows, cols = block.shape
    n_dev = 8

    def body(x_ref, o_ref, g_ref, buf, send_sems, recv_sems, local_sem):
        x, y, c, chips = _place()
        me, sibling = (x, y, c), (x, y, 1 - c)

        def slot(px_, py_, pc_):
            return buf.at[4 * px_ + 2 * py_ + pc_]

        def copy(k, who, to, src=None):
            return _remote(slot(*who) if src is None else src, slot(*who), send_sems.at[k], recv_sems.at[k], to)

        mine = pltpu.make_async_copy(x_ref, slot(*me), local_sem)
        mine.start()
        first = [copy(0, me, sibling, src=x_ref)]
        first += [copy(1 + j, me, (*chip, c), src=x_ref) for j, chip in enumerate(chips)]
        for cp in first:
            cp.start()
        passed = [copy(4 + j, (*chip, c), sibling) for j, chip in enumerate(chips)]
        for j, chip in enumerate(chips):
            copy(1 + j, (*chip, c), me).wait_recv()
            passed[j].start()
        copy(0, sibling, me).wait_recv()
        for j, chip in enumerate(chips):
            copy(4 + j, (*chip, 1 - c), me).wait_recv()
        for cp in first + passed:
            cp.wait_send()
        mine.wait()
        acc = buf[0, :, :n_sum]
        for s in range(1, n_dev):
            acc = acc + buf[s, :, :n_sum]
        o_ref[...] = acc
        for s in range(n_dev):
            g_ref[s * rows:(s + 1) * rows, :] = buf[s, :, n_sum:]

    return pl.pallas_call(
        body, name="gather_small",
        in_specs=[pl.BlockSpec(memory_space=pltpu.VMEM)],
        out_specs=(pl.BlockSpec(memory_space=pltpu.VMEM), pl.BlockSpec(memory_space=pltpu.VMEM)),
        out_shape=(SDS((rows, n_sum), F32), SDS((n_dev * rows, cols - n_sum), F32)),
        scratch_shapes=[pltpu.VMEM((n_dev, rows, cols), F32), pltpu.SemaphoreType.DMA((7,)),
                        pltpu.SemaphoreType.DMA((7,)), pltpu.SemaphoreType.DMA],
        compiler_params=_cp(has_side_effects=True))(block)


def _adam_math(w, g, m, v):
    m = ADAM_B1 * m + (1.0 - ADAM_B1) * g
    v = ADAM_B2 * v + (1.0 - ADAM_B2) * (g * g)
    m_hat = m / (1.0 - ADAM_B1 ** ADAM_STEP)
    v_hat = v / (1.0 - ADAM_B2 ** ADAM_STEP)
    delta = -ADAM_LR * (m_hat / (jnp.sqrt(v_hat) + ADAM_EPS) + ADAM_WD * w)
    return delta, m, v


def _adamw(w, g, m, v):
    rows, cols = w.shape
    tr = min(rows, 256 if cols <= 2048 else 128)

    def body(w_ref, g_ref, m_ref, v_ref, go_ref, d_ref, nm_ref, nv_ref):
        g = g_ref[...]
        go_ref[...] = g
        d_ref[...], nm_ref[...], nv_ref[...] = _adam_math(w_ref[...], g, m_ref[...], v_ref[...])

    spec = pl.BlockSpec((tr, cols), lambda i: (i, 0))
    return pl.pallas_call(
        body, name="adamw", grid=(rows // tr,), in_specs=[spec] * 4, out_specs=(spec,) * 4,
        out_shape=(SDS(w.shape, F32),) * 4, compiler_params=_cp(("parallel",), 40))(w, g, m, v)


def _adamw_small(w, g, m, v):
    def body(w_ref, g_ref, m_ref, v_ref, go_ref, d_ref, nm_ref, nv_ref):
        w = w_ref[...]
        g = g_ref[...]
        sub = lax.broadcasted_iota(jnp.int32, w.shape, 0)
        lane = lax.broadcasted_iota(jnp.int32, w.shape, 1)
        is_ret = jnp.logical_and(sub == 5, lane < 2 * RET_HEADS)
        u = jnp.exp(jnp.where(is_ret, w, -1.0) * jnp.log(2.0))
        g = jnp.where(is_ret, g * (-u * jnp.log(2.0) / (1.0 - u)), g)
        go_ref[...] = g
        d_ref[...], nm_ref[...], nv_ref[...] = _adam_math(w, g, m_ref[...], v_ref[...])

    return pl.pallas_call(body, name="adamw_small", out_shape=(SDS(w.shape, F32),) * 4)(w, g, m, v)


def _rope_tables(seq, n_samp, n_ctx_rows):
    rows = seq // GRID_W
    row = jnp.repeat(jnp.arange(rows, dtype=F32), GRID_W)
    col = jnp.tile(jnp.arange(GRID_W, dtype=F32), rows)
    half = ATT_HEAD_DIM // 2
    freqs = ROPE_THETA ** (-jnp.arange(0, half, 2, dtype=F32) / half)
    ang = jnp.concatenate([row[:, None] * freqs, col[:, None] * freqs], axis=-1)
    cos, sin = jnp.cos(ang), jnp.sin(ang)
    cos_f = jnp.repeat(cos, 2, axis=1)
    sin_s = jnp.stack([-sin, sin], axis=-1).reshape(seq, ATT_HEAD_DIM)
    cos_all = jnp.concatenate([jnp.tile(cos_f, (n_samp, 1)), jnp.ones((n_ctx_rows, ATT_HEAD_DIM), F32)], axis=0)
    sin_all = jnp.concatenate([jnp.tile(sin_s, (n_samp, 1)), jnp.zeros((n_ctx_rows, ATT_HEAD_DIM), F32)], axis=0)
    return cos_all, sin_all


def _pack_small(c_ctx, norm_w, b_ada, ret, qn, kn):
    d = D_MODEL
    row5 = jnp.concatenate([ret.reshape(-1), jnp.zeros((128 - 2 * RET_HEADS,), F32), qn.reshape(-1), kn.reshape(-1),
                            jnp.zeros((d - 384,), F32)])
    return jnp.concatenate([c_ctx.reshape(1, d), norm_w.reshape(1, d), b_ada.reshape(3, d), row5.reshape(1, d),
                            jnp.zeros((2, d), F32)], axis=0)


def _unpack_small(p):
    d = D_MODEL
    return (p[0], p[1:2], p[2:5].reshape(1, 3 * d), p[5, :2 * RET_HEADS].reshape(1, 2, RET_HEADS),
            p[5:6, 128:256], p[5:6, 256:384])


def _step(x, c, ctx, c_ctx, norm_w, b_ada, ret_log2_decay, q_norm_w, k_norm_w, loss_target, weights, ids):
    n_samp, seq, d = x.shape
    lc = ctx.shape[1]
    t_lat, t_ctx = n_samp * seq, n_samp * lc
    assert seq % TM == 0 and t_ctx == TM and t_lat % lc == 0 and seq % GRID_W == 0
    tps = seq // TM

    x_lat = x.reshape(t_lat, d)
    x_ctx = ctx.reshape(t_ctx, d)
    cvec8 = jnp.concatenate([c, c_ctx.reshape(1, d), jnp.zeros((8 - n_samp - 1, d), F32)], axis=0)
    lg = jnp.log1p(-jnp.exp2(ret_log2_decay.reshape(2, RET_HEADS)))
    cos_all, sin_all = _rope_tables(seq, n_samp, t_ctx)

    w_ada_b, w_in_b, w_or_b, w_oa_b, w_out_b = weights
    c_all, mod_shards = _adaln_fwd(cvec8, w_ada_b, b_ada)
    mod8 = mod_shards.transpose(1, 0, 2).reshape(8, 3 * d)
    mod3 = mod8[:n_samp + 1]
    shift3 = mod3[:, None, 0:d]
    scale3 = mod3[:, None, d:2 * d]
    gate3 = mod3[:, None, 2 * d:3 * d]

    hx, hxt = _norm_fwd(x_lat, x_ctx, norm_w, scale3, shift3, tps, n_samp)
    px, w_in_g = _in_proj_gather(hx, w_in_b, ids)

    states0 = _ctx_state_fwd(px, lg, n_samp, t_lat, lc)
    (o_f, o_b, saved), w_o = _ret_fwd(px, states0, lg, n_samp, seq,
                                      comm=_ag_comm((w_or_b, w_oa_b, w_out_b)))
    w_o_ret, w_o_att, w_out = (w.reshape(-1, d) for w in w_o)

    qn = _att_prep_q(px, cos_all, sin_all, q_norm_w, t_lat)
    kn, vn = _att_prep_kv(px, cos_all, sin_all, k_norm_w)
    (o_att, lse), (w_ada_g,) = _att_fwd(qn, kn, vn, n_samp, seq, lc, comm=_ag_comm((w_ada_b,)))

    (gx_res, do, do_att, dpx, loss8, dgate, g_w_o_ret, g_w_o_att, g_w_out) = _merge(
        x_lat, loss_target.reshape(t_lat, d), o_f, o_b, o_att, px, gate3, w_o_ret, w_o_att, w_out, tps)

    g_a = [g.reshape(N_SHARD, -1, d) for g in (g_w_o_ret, g_w_o_att, g_w_out)]
    (dpx, dkl, dkc, dvl, dvc, gqw), sib_a = _att_bwd(
        dpx, qn, kn, vn, px, o_att, lse, do_att, cos_all, sin_all, q_norm_w, n_samp, seq, lc, comm=_swap_comm(g_a))
    dpx, gkw = _att_kv_bwd(dpx, dkl, dkc, dvl, dvc, px, cos_all, sin_all, k_norm_w)
    t_a = [_chip_sum(g, p, ids) for g, p in zip(g_a, sib_a)]

    (dpx, dstates, dlg_lat), q_a = _ret_bwd(dpx, px, do, saved, lg, n_samp, seq,
                                            comm=_exchange_comm([t16 for _, t16 in t_a]))
    r_a = [_shard_sum(t, q, ids) for (t, _), q in zip(t_a, q_a)]
    dpx, dlg_ctx = _ctx_state_bwd(dpx, px, dstates, lg, n_samp, t_lat, lc)
    dpx = _zero_ctx_tail(dpx, t_lat)

    n_tiles = dpx.shape[0] // _big_rows(dpx.shape[0])
    g_b = _gw_in(hxt, dpx)
    dhx, (sib_b, *r_a) = _dhx(dpx, w_in_g, 0, 1, None, _join_comms(_swap_comm([g_b]), _join_comm(r_a)))
    t_b, t16_b = _chip_sum(g_b, sib_b, ids)
    dhx, (q_b,) = _dhx(dpx, w_in_g, 1, n_tiles - 1, dhx, _exchange_comm([t16_b]))
    r_b_half = _shard_sum(t_b, q_b, ids)
    grad_x, dshift, dscale, g_norm_w = _norm_bwd(x_lat, x_ctx, dhx, gx_res, norm_w, scale3, tps, n_samp)

    dgate_all = jnp.concatenate([dgate, jnp.zeros((1, 1, d), F32)], axis=0)
    dmod3 = jnp.concatenate([dshift, dscale, dgate_all], axis=2).reshape(n_samp + 1, 3 * d)
    dmod8 = jnp.concatenate([dmod3, jnp.zeros((8 - n_samp - 1, 3 * d), F32)], axis=0)
    g_lg = (jnp.sum(dlg_lat[:, :, 0], axis=0).reshape(2, RET_HEADS)
            + jnp.stack([jnp.sum(dlg_ctx[:, :, 0, 0], axis=0), jnp.sum(dlg_ctx[:, :, 1, 0], axis=0)], axis=0))
    g_qw = jnp.sum(gqw, axis=(0, 1, 2))
    zero = jnp.zeros((d,), F32)

    local = _pack_small(zero, g_norm_w, jnp.zeros((3 * d,), F32), g_lg, g_qw, gkw).at[6, 0].set(loss8[0, 0])
    small_sum, dmod_all = _gather_small(jnp.concatenate([local, dmod8], axis=1), d)
    (g_w_ada, g_b_ada, dc_all), (r_b,) = _adaln_bwd(c_all, dmod_all, w_ada_g, comm=_join_comm([r_b_half]))
    dc_ctx = jnp.sum(dc_all.reshape(-1, 8, d)[:, n_samp], axis=0)
    small = small_sum + _pack_small(dc_ctx, zero, g_b_ada, jnp.zeros((2, RET_HEADS), F32), zero[:128], zero[:128])
    r_c = lax.dynamic_index_in_dim(g_w_ada, ids[3], 0, keepdims=False)
    return small[6, 0], grad_x.reshape(n_samp, seq, d), (r_c, r_b, *r_a), small


def kernel(x, c, ctx, c_ctx, norm_w, w_ada, b_ada, w_in, ret_log2_decay, q_norm_w, k_norm_w, w_o_ret, w_o_att, w_out, loss_target, m_c_ctx, m_norm_w, m_w_ada, m_b_ada, m_w_in, m_ret_log2_decay, m_q_norm_w, m_k_norm_w, m_w_o_ret, m_w_o_att, m_w_out, v_c_ctx, v_norm_w, v_w_ada, v_b_ada, v_w_in, v_ret_log2_decay, v_q_norm_w, v_k_norm_w, v_w_o_ret, v_w_o_att, v_w_out):
    big_w = (w_ada[0], w_in[0], w_o_ret[0], w_o_att[0], w_out[0])
    big_m = (m_w_ada[0], m_w_in[0], m_w_o_ret[0], m_w_o_att[0], m_w_out[0])
    big_v = (v_w_ada[0], v_w_in[0], v_w_o_ret[0], v_w_o_att[0], v_w_out[0])

    ids = _place_ids()
    loss, grad_x, big_grad, small_grad_in = _step(
        x, c, ctx, c_ctx, norm_w[0:1], b_ada[0:1], ret_log2_decay[0], q_norm_w[0:1], k_norm_w[0:1], loss_target,
        tuple(_cast_place(w, ids) for w in big_w), ids)
    small_w = _pack_small(c_ctx, norm_w, b_ada, ret_log2_decay, q_norm_w, k_norm_w)
    small_m = _pack_small(m_c_ctx, m_norm_w, m_b_ada, m_ret_log2_decay, m_q_norm_w, m_k_norm_w)
    small_v = _pack_small(v_c_ctx, v_norm_w, v_b_ada, v_ret_log2_decay, v_q_norm_w, v_k_norm_w)
    small_grad, small_delta, small_nm, small_nv = _adamw_small(small_w, small_grad_in, small_m, small_v)

    big_g, big_delta, big_nm, big_nv = [], [], [], []
    for w, g, m, v in zip(big_w, big_grad, big_m, big_v):
        go, dlt, nm, nv = _adamw(w, g, m, v)
        big_g.append(go[None])
        big_delta.append(dlt[None])
        big_nm.append(nm[None])
        big_nv.append(nv[None])
    big_grad = big_g

    def order(small_packed, big):
        s = _unpack_small(small_packed)
        return (s[0], s[1], big[0], s[2], big[1], s[3], s[4], s[5], big[2], big[3], big[4])

    return (loss, grad_x, *order(small_grad, big_grad), *order(small_delta, big_delta),
            *order(small_nm, big_nm), *order(small_nv, big_nv))
```

```python
import functools
from typing import NamedTuple

import jax
import jax.numpy as jnp
from jax import lax
from jax.experimental import pallas as pl
from jax.experimental.pallas import tpu as pltpu

F32 = jnp.float32
BF = jnp.bfloat16
SDS = jax.ShapeDtypeStruct
MESH = pl.DeviceIdType.MESH
ANY = pl.BlockSpec(memory_space=pl.ANY)
SMEM = pl.BlockSpec(memory_space=pltpu.SMEM)

D_MODEL = 1024
GRID_W = 64
RET_HEADS = 4
RET_DK = 256
RET_DV = 512
RET_CHUNK = 128
ATT_HEADS = 8
ATT_KV_HEADS = 2
ATT_REP = ATT_HEADS // ATT_KV_HEADS
ATT_HEAD_DIM = 128
ROPE_THETA = 10000.0
NORM_EPS = 1e-6
IN_COLS = 10752
KV_COLS = 3584
C_RK, C_RV, C_AK, C_AV, C_RQ, C_RG, C_AQ, C_AG, C_MR, C_MA = 0, 1024, 3072, 3328, 3584, 4608, 6656, 7680, 8704, 9728
N_SHARD = 4
ADA_W = 3 * D_MODEL // N_SHARD
IN_W = IN_COLS // N_SHARD
IN_BLK = IN_W
BPS = IN_W // IN_BLK
N_IN_BLK = IN_COLS // IN_BLK
TM = 512
ATT_TQ = 512
ADAM_LR, ADAM_B1, ADAM_B2, ADAM_EPS, ADAM_WD, ADAM_STEP = 0.001, 0.9, 0.999, 1e-08, 0.01, 10
MIB = 1024 * 1024


def _cp(sem=None, vmem_mb=None, **kw):
    if sem is not None:
        kw["dimension_semantics"] = sem
    if vmem_mb is not None:
        kw["vmem_limit_bytes"] = vmem_mb * MIB
    return pltpu.CompilerParams(**kw)


def _dot(a, b, ca=1, cb=0):
    return lax.dot_general(a.astype(BF), b.astype(BF), (((ca,), (cb,)), ((), ())), preferred_element_type=F32)


def _sigmoid(x):
    return 0.5 * jnp.tanh(0.5 * x) + 0.5


def _sum_all(x):
    return jnp.sum(jnp.sum(x, axis=1, keepdims=True), axis=0, keepdims=True)


def _swap_pairs(x):
    ax = x.ndim - 1
    lane = lax.broadcasted_iota(jnp.int32, x.shape, ax)
    nxt = pltpu.roll(x, x.shape[ax] - 1, ax)
    prv = pltpu.roll(x, 1, ax)
    return jnp.where(lane % 2 == 0, nxt, prv)


def _rms(x):
    return lax.rsqrt(jnp.mean(x * x, axis=-1, keepdims=True) + NORM_EPS)


def _rms_bwd(dxh, xh, r):
    return r * (dxh - xh * jnp.mean(dxh * xh, axis=-1, keepdims=True))


class _Comm(NamedTuple):
    name: str
    ins: tuple
    out_shapes: tuple
    aliases: dict
    n_sems: int
    phases: tuple


def _join_comms(*comms):
    offs, i_off, o_off, s_off = [], 0, 0, 0
    for cm in comms:
        offs.append((i_off, o_off, s_off))
        i_off, o_off, s_off = i_off + len(cm.ins), o_off + len(cm.out_shapes), s_off + cm.n_sems

    def phase(k):
        def run(ins, outs, ssem, rsem, base):
            sends, recvs = [], []
            for cm, (io, oo, so) in zip(comms, offs):
                if k < len(cm.phases):
                    s, r = cm.phases[k](ins[io:io + len(cm.ins)], outs[oo:oo + len(cm.out_shapes)], ssem, rsem,
                                        base + so)
                    sends += s
                    recvs += r
            return sends, recvs
        return run

    aliases = {}
    for cm, (io, oo, _) in zip(comms, offs):
        aliases.update({io + a: oo + b for a, b in cm.aliases.items()})
    return _Comm("+".join(cm.name for cm in comms), sum((cm.ins for cm in comms), ()),
                 sum((cm.out_shapes for cm in comms), ()), aliases, s_off,
                 tuple(phase(k) for k in range(max(len(cm.phases) for cm in comms))))


def _run_phases(comm, cins, couts, ssem, rsem):
    for k, phase in enumerate(comm.phases):
        sends, recvs = phase(cins, couts, ssem, rsem, 0)
        if k > 0:
            for cp in sends:
                cp.start()
        for cp in recvs:
            cp.wait_recv()
        for cp in sends:
            cp.wait_send()


def _call(body, args, comm, *, name, grid, in_specs, out_specs, out_shape, scratch_shapes=(),
          compiler_params, aliases=None):
    n_in, n_out, n_sc = len(in_specs), len(out_specs), len(scratch_shapes)
    n_ci, n_co = len(comm.ins), len(comm.out_shapes)
    io_alias = dict(aliases or {})
    io_alias.update({n_in + a: n_out + b for a, b in comm.aliases.items()})

    def kernel_body(*refs):
        ins, cins = refs[:n_in], refs[n_in:n_in + n_ci]
        outs = refs[n_in + n_ci:n_in + n_ci + n_out]
        couts = refs[n_in + n_ci + n_out:n_in + n_ci + n_out + n_co]
        scratch = refs[n_in + n_ci + n_out + n_co:n_in + n_ci + n_out + n_co + n_sc]
        ssem, rsem = refs[-2:]
        first = functools.reduce(jnp.logical_and, [pl.program_id(k) == 0 for k in range(len(grid))])
        last = functools.reduce(jnp.logical_and, [pl.program_id(k) == grid[k] - 1 for k in range(len(grid))])

        @pl.when(first)
        def _():
            for cp in comm.phases[0](cins, couts, ssem, rsem, 0)[0]:
                cp.start()

        body(*ins, *outs, *scratch)

        @pl.when(last)
        def _():
            _run_phases(comm, cins, couts, ssem, rsem)

    res = pl.pallas_call(
        kernel_body, name=name + "+" + comm.name, grid=grid, in_specs=list(in_specs) + [ANY] * n_ci,
        out_specs=tuple(out_specs) + tuple([ANY] * n_co), out_shape=tuple(out_shape) + tuple(comm.out_shapes),
        scratch_shapes=list(scratch_shapes) + [pltpu.SemaphoreType.DMA((comm.n_sems,)),
                                               pltpu.SemaphoreType.DMA((comm.n_sems,))],
        input_output_aliases=io_alias, compiler_params=compiler_params)(*args, *comm.ins)
    return tuple(res[:n_out]), tuple(res[n_out:])


def _adaln_fwd(cvec8, w_ada_b, b_ada):
    rows, d = cvec8.shape
    n_dev = 8

    def body(x_ref, w_hbm, b_ref, g_ref, o_ref, buf, wv, part, send_sems, recv_sems, local_sems):
        x, y, c, chips = _place()
        me, sibling, shard = (x, y, c), (x, y, 1 - c), 2 * x + y

        def slot(px_, py_, pc_):
            return 4 * px_ + 2 * py_ + pc_

        def copy(k, who, to, src=None):
            dst = buf.at[slot(*who)]
            return _remote(dst if src is None else src, dst, send_sems.at[k], recv_sems.at[k], to)

        weight = pltpu.make_async_copy(w_hbm.at[shard], wv, local_sems.at[0])
        weight.start()
        mine = pltpu.make_async_copy(x_ref, buf.at[slot(*me)], local_sems.at[1])
        mine.start()
        first = [copy(0, me, sibling, src=x_ref)]
        first += [copy(1 + j, me, (*chip, c), src=x_ref) for j, chip in enumerate(chips)]
        for cp in first:
            cp.start()
        passed = [copy(4 + j, (*chip, c), sibling) for j, chip in enumerate(chips)]
        for j, chip in enumerate(chips):
            copy(1 + j, (*chip, c), me).wait_recv()
            passed[j].start()
        copy(0, sibling, me).wait_recv()
        for j, chip in enumerate(chips):
            copy(4 + j, (*chip, 1 - c), me).wait_recv()
        for cp in first + passed:
            cp.wait_send()
        mine.wait()
        weight.wait()

        cv = buf[...].reshape(n_dev * rows, d)
        g_ref[...] = cv
        sc = (cv * _sigmoid(cv)).astype(BF)
        mod = jnp.dot(sc, wv[...], preferred_element_type=F32) + b_ref[shard]
        part[...] = mod.reshape(n_dev, rows, ADA_W)

        def back(j, chip):
            theirs = o_ref.at[2 * chip[0] + chip[1]]
            return (_remote(part.at[slot(*chip, c)], o_ref.at[shard], send_sems.at[7 + j], recv_sems.at[7 + j],
                            (*chip, c)),
                    _remote(theirs, theirs, send_sems.at[7 + j], recv_sems.at[7 + j], (*chip, c)))

        for j, chip in enumerate(chips):
            back(j, chip)[0].start()
        o_ref[shard] = part[slot(*me)]
        for j, chip in enumerate(chips):
            back(j, chip)[1].wait_recv()
        for j, chip in enumerate(chips):
            back(j, chip)[0].wait_send()

    vmem = pl.BlockSpec(memory_space=pltpu.VMEM)
    return pl.pallas_call(
        body, name="adaln_fwd", in_specs=[vmem, ANY, vmem], out_specs=(vmem, vmem),
        out_shape=(SDS((n_dev * rows, d), F32), SDS((N_SHARD, rows, ADA_W), F32)),
        scratch_shapes=[pltpu.VMEM((n_dev, rows, d), F32), pltpu.VMEM((D_MODEL, ADA_W), BF),
                        pltpu.VMEM((n_dev, rows, ADA_W), F32), pltpu.SemaphoreType.DMA((10,)),
                        pltpu.SemaphoreType.DMA((10,)), pltpu.SemaphoreType.DMA((2,))],
        compiler_params=_cp(vmem_mb=32, has_side_effects=True))(cvec8, w_ada_b, b_ada.reshape(N_SHARD, 1, ADA_W))


def _adaln_bwd(cvec, dmod, w_ada_g, comm):
    n_rows = cvec.shape[0]
    def body(c_ref, d_ref, w_ref, gw_ref, gb_ref, dc_ref):
        cv = c_ref[...]
        sg = _sigmoid(cv)
        sc = cv * sg
        dm = d_ref[...]
        gb_ref[...] = jnp.sum(dm, axis=0, keepdims=True)
        dsc = jnp.zeros(cv.shape, F32)
        for s in range(N_SHARD):
            cols = slice(s * ADA_W, (s + 1) * ADA_W)
            gw_ref[s] = _dot(sc, dm[:, cols], 0, 0)
            dsc = dsc + _dot(dm[:, cols], w_ref[s], 1, 1)
        dc_ref[...] = dsc * (sg * (1.0 + cv * (1.0 - sg)))

    def whole(shape):
        return pl.BlockSpec(shape, lambda i: (0,) * len(shape))

    shapes = ((N_SHARD, D_MODEL, ADA_W), (1, 3 * D_MODEL), (n_rows, D_MODEL))
    return _call(body, [cvec, dmod, w_ada_g], comm, name="adaln_bwd", grid=(1,),
                 in_specs=[whole(cvec.shape), whole(dmod.shape), whole(w_ada_g.shape)],
                 out_specs=tuple(whole(s) for s in shapes), out_shape=tuple(SDS(s, F32) for s in shapes),
                 compiler_params=_cp(("arbitrary",), 56))


def _big_rows(rows):
    return 1536 if rows % 1536 == 0 else TM


def _norm_fwd(x_lat, x_ctx, norm_w, scale3, shift3, tiles_per_sample, n_samp):
    n_lat = x_lat.shape[0] // TM
    rows = x_lat.shape[0] + x_ctx.shape[0]

    def samp(i):
        return jnp.minimum(i // tiles_per_sample, n_samp)

    def body(x_ref, c_ref, nw_ref, sc_ref, sh_ref, hx_ref, hxt_ref):
        x = jnp.where(pl.program_id(0) < n_lat, x_ref[...], c_ref[...])
        h = x * _rms(x) * nw_ref[...] * (1.0 + sc_ref[...]) + sh_ref[...]
        hx_ref[...] = h.astype(BF)
        hxt_ref[...] = h.T.astype(BF)

    return pl.pallas_call(
        body, name="norm_fwd", grid=(rows // TM,),
        in_specs=[pl.BlockSpec((TM, D_MODEL), lambda i: (jnp.minimum(i, n_lat - 1), 0)),
                  pl.BlockSpec((TM, D_MODEL), lambda i: (jnp.maximum(i - n_lat, 0), 0)),
                  pl.BlockSpec((1, D_MODEL), lambda i: (0, 0)),
                  pl.BlockSpec((None, 1, D_MODEL), lambda i: (samp(i), 0, 0)),
                  pl.BlockSpec((None, 1, D_MODEL), lambda i: (samp(i), 0, 0))],
        out_specs=(pl.BlockSpec((TM, D_MODEL), lambda i: (i, 0)),
                   pl.BlockSpec((D_MODEL, TM), lambda i: (0, i))),
        out_shape=(SDS((rows, D_MODEL), BF), SDS((D_MODEL, rows), BF)),
        compiler_params=_cp(("parallel",), 40))(x_lat, x_ctx, norm_w, scale3, shift3)


def _norm_bwd(x_lat, x_ctx, dhx, gx_res, norm_w, scale3, tiles_per_sample, n_samp):
    rows = x_lat.shape[0] + x_ctx.shape[0]
    n_lat = tiles_per_sample * n_samp

    def samp(i):
        return jnp.minimum(i // tiles_per_sample, n_samp)

    def lat(i):
        return jnp.minimum(i, n_lat - 1)

    def body(x_ref, c_ref, dh_ref, gr_ref, nw_ref, sc_ref, gx_ref, dsh_ref, dsc_ref, dnw_ref):
        i = pl.program_id(0)
        x = jnp.where(i < n_lat, x_ref[...], c_ref[...])
        r = _rms(x)
        xh = x * r
        nw = nw_ref[...]
        dh = dh_ref[...]
        first = jnp.logical_or(i % tiles_per_sample == 0, i >= n_lat)

        @pl.when(first)
        def _():
            dsh_ref[...] = jnp.zeros_like(dsh_ref)
            dsc_ref[...] = jnp.zeros_like(dsc_ref)

        @pl.when(i == 0)
        def _():
            dnw_ref[...] = jnp.zeros_like(dnw_ref)

        dsh_ref[...] += jnp.sum(dh, axis=0, keepdims=True)
        dsc_ref[...] += jnp.sum(dh * (xh * nw), axis=0, keepdims=True)
        du = dh * (1.0 + sc_ref[...])
        dnw_ref[...] += jnp.sum(du * xh, axis=0, keepdims=True)

        @pl.when(i < n_lat)
        def _():
            gx_ref[...] = gr_ref[...] + _rms_bwd(du * nw, xh, r)

    return pl.pallas_call(
        body, name="norm_bwd", grid=(rows // TM,),
        in_specs=[pl.BlockSpec((TM, D_MODEL), lambda i: (lat(i), 0)),
                  pl.BlockSpec((TM, D_MODEL), lambda i: (jnp.maximum(i - n_lat, 0), 0)),
                  pl.BlockSpec((TM, D_MODEL), lambda i: (i, 0)),
                  pl.BlockSpec((TM, D_MODEL), lambda i: (lat(i), 0)),
                  pl.BlockSpec((1, D_MODEL), lambda i: (0, 0)),
                  pl.BlockSpec((None, 1, D_MODEL), lambda i: (samp(i), 0, 0))],
        out_specs=(pl.BlockSpec((TM, D_MODEL), lambda i: (lat(i), 0)),
                   pl.BlockSpec((None, 1, D_MODEL), lambda i: (samp(i), 0, 0)),
                   pl.BlockSpec((None, 1, D_MODEL), lambda i: (samp(i), 0, 0)),
                   pl.BlockSpec((1, D_MODEL), lambda i: (0, 0))),
        out_shape=(SDS((n_lat * TM, D_MODEL), F32), SDS((n_samp + 1, 1, D_MODEL), F32),
                   SDS((n_samp + 1, 1, D_MODEL), F32), SDS((1, D_MODEL), F32)),
        compiler_params=_cp(("arbitrary",), 40))(x_lat, x_ctx, dhx, gx_res, norm_w, scale3)


def _in_proj_gather(hx, w_buf, ids):
    rows = hx.shape[0]
    tb = _big_rows(rows)
    n_i = rows // tb
    hrows = D_MODEL // 2

    def body(ids_ref, h_ref, w_in_hbm, px_ref, w_hbm, wv, lsem, ssem, rsem):
        j, i = pl.program_id(0), pl.program_id(1)
        x, y, c, chips = _place()
        sibling = (x, y, 1 - c)

        def half(s, which):
            return w_hbm.at[s, pl.ds(which * hrows, hrows), :]

        def over_ici(rel):
            chip = chips[rel]
            mine, theirs = half(2 * x + y, c), half(2 * chip[0] + chip[1], c)
            return (_remote(mine, mine, ssem.at[rel], rsem.at[rel], (*chip, c)),
                    _remote(theirs, theirs, ssem.at[rel], rsem.at[rel], (*chip, c)))

        def over_d2d(rel):
            s = 2 * chips[rel][0] + chips[rel][1]
            return (_remote(half(s, c), half(s, c), ssem.at[3 + rel], rsem.at[3 + rel], sibling),
                    _remote(half(s, 1 - c), half(s, 1 - c), ssem.at[3 + rel], rsem.at[3 + rel], sibling))

        first_row_tile = i == 0

        @pl.when(jnp.logical_and(j == 0, first_row_tile))
        def _():
            over_ici(0)[0].start()
            over_ici(1)[0].start()

        @pl.when(jnp.logical_and(j == 1, first_row_tile))
        def _():
            for rel in range(2):
                over_ici(rel)[1].wait_recv()
                over_d2d(rel)[0].start()
            over_ici(2)[0].start()
            over_d2d(0)[1].wait_recv()

        @pl.when(jnp.logical_and(j == 2, first_row_tile))
        def _():
            over_d2d(1)[1].wait_recv()

        @pl.when(jnp.logical_and(j == 3, first_row_tile))
        def _():
            over_ici(2)[1].wait_recv()
            passed, landing = over_d2d(2)
            passed.start()
            landing.wait_recv()

        @pl.when(first_row_tile)
        def _():
            cp = pltpu.make_async_copy(w_hbm.at[ids_ref[4 + j]], wv, lsem)
            cp.start()
            cp.wait()

        px_ref[...] = jnp.dot(h_ref[...], wv[...], preferred_element_type=F32).astype(BF)

        @pl.when(jnp.logical_and(j == N_SHARD - 1, i == n_i - 1))
        def _():
            for rel in range(3):
                over_ici(rel)[0].wait_send()
                over_d2d(rel)[0].wait_send()

    return pl.pallas_call(
        body, name="in_proj_gather", input_output_aliases={2: 1},
        grid_spec=pltpu.PrefetchScalarGridSpec(
            num_scalar_prefetch=1, grid=(N_SHARD, n_i),
            in_specs=[pl.BlockSpec((tb, D_MODEL), lambda j, i, ids_ref: (i, 0)), ANY],
            out_specs=(pl.BlockSpec((tb, IN_W), lambda j, i, ids_ref: (i, ids_ref[4 + j])), ANY),
            scratch_shapes=[pltpu.VMEM((D_MODEL, IN_W), BF), pltpu.SemaphoreType.DMA,
                            pltpu.SemaphoreType.DMA((6,)), pltpu.SemaphoreType.DMA((6,))]),
        out_shape=(SDS((rows, IN_COLS), BF), SDS(w_buf.shape, w_buf.dtype)),
        compiler_params=_cp(("arbitrary", "arbitrary"), 56))(ids, hx, w_buf)


def _gw_in(hxt, dpx_all):
    rows = dpx_all.shape[0]
    tb = _big_rows(rows)

    def body(h_ref, d_ref, o_ref):
        @pl.when(pl.program_id(1) == 0)
        def _():
            o_ref[...] = jnp.zeros_like(o_ref)

        o_ref[...] += jnp.dot(h_ref[...], d_ref[...], preferred_element_type=F32)

    return pl.pallas_call(
        body, name="gw_in", grid=(N_IN_BLK, rows // tb),
        in_specs=[pl.BlockSpec((D_MODEL, tb), lambda j, i: (0, i)),
                  pl.BlockSpec((tb, IN_BLK), lambda j, i: (i, j))],
        out_specs=pl.BlockSpec((None, D_MODEL, IN_BLK), lambda j, i: (j // BPS, 0, j % BPS)),
        out_shape=SDS((N_SHARD, D_MODEL, IN_W), F32),
        compiler_params=_cp(("arbitrary", "arbitrary"), 56))(hxt, dpx_all)


def _dhx(dpx_all, w_in_g, tile0, n_tiles, dhx, comm):
    rows = dpx_all.shape[0]
    tb = _big_rows(rows)

    def body(d_ref, w_ref, *rest):
        o_ref = rest[-1]

        @pl.when(pl.program_id(1) == 0)
        def _():
            o_ref[...] = jnp.zeros_like(o_ref)

        o_ref[...] += lax.dot_general(d_ref[...], w_ref[...], (((1,), (1,)), ((), ())), preferred_element_type=F32)

    args, in_specs, aliases = [dpx_all, w_in_g], [
        pl.BlockSpec((tb, IN_BLK), lambda i, j: (tile0 + i, j)),
        pl.BlockSpec((None, D_MODEL, IN_BLK), lambda i, j: (j // BPS, 0, j % BPS))], None
    if dhx is not None:
        args, in_specs, aliases = args + [dhx], in_specs + [ANY], {2: 0}
    (out,), got = _call(body, args, comm, name="dhx", grid=(n_tiles, N_IN_BLK), in_specs=in_specs,
                        out_specs=(pl.BlockSpec((tb, D_MODEL), lambda i, j: (tile0 + i, 0)),),
                        out_shape=(SDS((rows, D_MODEL), F32),), aliases=aliases,
                        compiler_params=_cp(("arbitrary", "arbitrary"), 56))
    return out, got


def _decays(lgv, d):
    c = RET_CHUNK
    ii = lax.broadcasted_iota(jnp.int32, (c, 1), 0).astype(F32)
    jj = lax.broadcasted_iota(jnp.int32, (1, c), 1).astype(F32)
    a_i = jnp.where(d == 0, ii, c - 1.0 - ii)
    a_j = jnp.where(d == 0, jj, c - 1.0 - jj)
    rel = a_i - a_j
    mask = jnp.where(rel >= 0, jnp.exp(lgv * jnp.maximum(rel, 0.0)), 0.0)
    qd = jnp.exp(lgv * (a_i + 1.0))
    kd = jnp.exp(lgv * (c - 1.0 - a_i))
    gc = jnp.exp(jnp.full((1, 1), lgv * c, F32))
    return a_i, rel, mask, qd, kd, gc


def _ctx_state_fwd(px, lg, n_samp, t_lat, lc):
    rb = t_lat // lc

    def body(lg_ref, k_ref, v_ref, o_ref):
        h = pl.program_id(1)
        k = k_ref[...].astype(F32) * (RET_DK ** -0.5)
        v = v_ref[...]
        pos = lax.broadcasted_iota(jnp.int32, (lc, 1), 0).astype(F32)
        o_ref[0] = _dot(k * jnp.exp(lg_ref[0, h] * (lc - 1.0 - pos)), v, 0, 0)
        o_ref[1] = _dot(k * jnp.exp(lg_ref[1, h] * pos), v, 0, 0)

    return pl.pallas_call(
        body, name="ctx_state_fwd", grid=(n_samp, RET_HEADS),
        in_specs=[SMEM,
                  pl.BlockSpec((lc, RET_DK), lambda b, h: (rb + b, C_RK // RET_DK + h)),
                  pl.BlockSpec((lc, RET_DV), lambda b, h: (rb + b, C_RV // RET_DV + h))],
        out_specs=pl.BlockSpec((None, 2, None, RET_DK, RET_DV), lambda b, h: (b, 0, h, 0, 0)),
        out_shape=SDS((n_samp, 2, RET_HEADS, RET_DK, RET_DV), F32),
        compiler_params=_cp(("parallel", "parallel")))(lg, px, px)


def _ctx_state_bwd(dpx, px, dstates, lg, n_samp, t_lat, lc):
    rb = t_lat // lc
    kspec = pl.BlockSpec((lc, RET_DK), lambda b, h: (rb + b, C_RK // RET_DK + h))
    vspec = pl.BlockSpec((lc, RET_DV), lambda b, h: (rb + b, C_RV // RET_DV + h))
    sspec = pl.BlockSpec((None, 2, None, RET_DK, RET_DV), lambda b, h: (b, 0, h, 0, 0))

    def weights(lg_ref, h):
        pos = lax.broadcasted_iota(jnp.int32, (lc, 1), 0).astype(F32)
        e_f = lc - 1.0 - pos
        return pos, e_f, jnp.exp(lg_ref[0, h] * e_f), jnp.exp(lg_ref[1, h] * pos)

    def k_body(lg_ref, dpx_hbm, k_ref, v_ref, ds_ref, dk_ref, dlg_ref):
        pos, e_f, w_f, w_b = weights(lg_ref, pl.program_id(1))
        k = k_ref[...].astype(F32) * (RET_DK ** -0.5)
        y_f = _dot(v_ref[...], ds_ref[0], 1, 1) * w_f
        y_b = _dot(v_ref[...], ds_ref[1], 1, 1) * w_b
        dk_ref[...] = ((y_f + y_b) * (RET_DK ** -0.5)).astype(BF)
        t_f = _sum_all(e_f * k * y_f)
        t_b = _sum_all(pos * k * y_b)
        sub = lax.broadcasted_iota(jnp.int32, (8, 128), 0)
        dlg_ref[...] = jnp.where(sub == 0, t_f, jnp.where(sub == 1, t_b, 0.0))

    def v_body(lg_ref, dpx_hbm, k_ref, ds_ref, dv_ref):
        _, _, w_f, w_b = weights(lg_ref, pl.program_id(1))
        k = k_ref[...].astype(F32) * (RET_DK ** -0.5)
        dv_ref[...] = (_dot(k * w_f, ds_ref[0]) + _dot(k * w_b, ds_ref[1])).astype(BF)

    dpx, dlg = pl.pallas_call(
        k_body, name="ctx_state_bwd_k", grid=(n_samp, RET_HEADS), input_output_aliases={1: 0},
        in_specs=[SMEM, ANY, kspec, vspec, sspec],
        out_specs=(kspec, pl.BlockSpec((None, None, 8, 128), lambda b, h: (b, h, 0, 0))),
        out_shape=(SDS(dpx.shape, dpx.dtype), SDS((n_samp, RET_HEADS, 8, 128), F32)),
        compiler_params=_cp(("parallel", "parallel")))(lg, dpx, px, px, dstates)
    dpx = pl.pallas_call(
        v_body, name="ctx_state_bwd_v", grid=(n_samp, RET_HEADS), input_output_aliases={1: 0},
        in_specs=[SMEM, ANY, kspec, sspec], out_specs=vspec, out_shape=SDS(dpx.shape, dpx.dtype),
        compiler_params=_cp(("parallel", "parallel")))(lg, dpx, px, dstates)
    return dpx, dlg


def _zero_ctx_tail(dpx, t_lat):
    wb = 512
    n_ctx = (dpx.shape[0] - t_lat) // TM

    def body(dpx_hbm, o_ref):
        o_ref[...] = jnp.zeros_like(o_ref)

    return pl.pallas_call(
        body, name="zero_ctx_tail", grid=(n_ctx, (IN_COLS - KV_COLS) // wb), input_output_aliases={0: 0},
        in_specs=[ANY], out_specs=pl.BlockSpec((TM, wb), lambda i, j: (t_lat // TM + i, KV_COLS // wb + j)),
        out_shape=SDS(dpx.shape, dpx.dtype),
        compiler_params=_cp(("parallel", "parallel")))(dpx)


def _ret_specs(row_f, row_b):
    c = RET_CHUNK
    wq = RET_HEADS * RET_DK // 2
    wv = RET_HEADS * RET_DV // 2
    specs = []
    for row in (row_f, row_b):
        specs += [pl.BlockSpec((c, wq), lambda b, n, row=row: (row(b, n), C_RQ // wq)),
                  pl.BlockSpec((c, wq), lambda b, n, row=row: (row(b, n), C_RQ // wq + 1)),
                  pl.BlockSpec((c, 2 * wq), lambda b, n, row=row: (row(b, n), C_RK // (2 * wq))),
                  pl.BlockSpec((c, wv), lambda b, n, row=row: (row(b, n), C_RV // wv)),
                  pl.BlockSpec((c, wv), lambda b, n, row=row: (row(b, n), C_RV // wv + 1))]
    return specs


def _ret_head(refs, h):
    q0, q1, k_ref, v0, v1 = refs
    hh = h % 2
    q = (q0, q1)[h // 2][:, hh * RET_DK:(hh + 1) * RET_DK].astype(F32)
    k = k_ref[:, h * RET_DK:(h + 1) * RET_DK].astype(F32) * (RET_DK ** -0.5)
    v = (v0, v1)[h // 2][:, hh * RET_DV:(hh + 1) * RET_DV]
    return q, k, v


def _ret_fwd(px, states0, lg, n_samp, seq, comm):
    c = RET_CHUNK
    nc = seq // c
    t_lat = n_samp * seq
    wo = RET_HEADS * RET_DV

    def row_f(b, n):
        return b * nc + n

    def row_b(b, n):
        return b * nc + nc - 1 - n

    def body(lg_ref, *refs):
        ins, (s0_ref, of_ref, ob_ref, st_ref, s_s) = refs[:10], refs[10:]

        @pl.when(pl.program_id(1) == 0)
        def _():
            s_s[...] = s0_ref[...]

        for d, o_ref in ((0, of_ref), (1, ob_ref)):
            for h in range(RET_HEADS):
                _, _, mask, qd, kd, gc = _decays(lg_ref[d, h], d)
                q, k, v = _ret_head(ins[5 * d:5 * d + 5], h)
                s = s_s[d, h]
                st_ref[h, d] = s.astype(BF)
                sc = _dot(q, k, 1, 1) * mask
                o_ref[:, h * RET_DV:(h + 1) * RET_DV] = (_dot(sc, v) + _dot(q * qd, s)).astype(BF)
                s_s[d, h] = s * gc + _dot(k * kd, v, 0, 0)

    return _call(
        body, [lg] + [px] * 10 + [states0], comm, name="ret_fwd", grid=(n_samp, nc),
        in_specs=[SMEM] + _ret_specs(row_f, row_b) + [
            pl.BlockSpec((None, 2, RET_HEADS, RET_DK, RET_DV), lambda b, n: (b, 0, 0, 0, 0))],
        out_specs=(pl.BlockSpec((c, wo), lambda b, n: (row_f(b, n), 0)),
                   pl.BlockSpec((c, wo), lambda b, n: (row_b(b, n), 0)),
                   pl.BlockSpec((None, RET_HEADS, 2, None, RET_DK, RET_DV), lambda b, n: (b, 0, 0, n, 0, 0))),
        out_shape=(SDS((t_lat, wo), BF), SDS((t_lat, wo), BF),
                   SDS((n_samp, RET_HEADS, 2, nc, RET_DK, RET_DV), BF)),
        scratch_shapes=[pltpu.VMEM((2, RET_HEADS, RET_DK, RET_DV), F32)],
        compiler_params=_cp(("arbitrary", "arbitrary"), 48))


def _ret_bwd(dpx, px, do, saved, lg, n_samp, seq, comm):
    c = RET_CHUNK
    nc = seq // c
    assert nc % 2 == 0
    wq, wo = RET_HEADS * RET_DK, RET_HEADS * RET_DV

    def row_f(b, n):
        return b * nc + nc - 1 - n

    def row_b(b, n):
        return b * nc + n

    def body(lg_ref, *refs):
        ins = refs[:10]
        (dof_ref, dob_ref, st_ref, dpx_in, dpx_hbm, ds0_ref, dlg_ref,
         ds_s, acc_s, sq_s, sk_s, sv_s, sems) = refs[10:]
        b, n = pl.program_id(0), pl.program_id(1)
        second = n >= nc // 2
        chunks = (nc - 1 - n, n)

        def parked(ch):
            return pl.ds(pl.multiple_of(ch * c, c), c)

        def flush():
            cps = []
            for d, ch in enumerate(chunks):
                rows = pl.ds(pl.multiple_of((b * nc + ch) * c, c), c)
                cps += [pltpu.make_async_copy(sq_s.at[parked(ch), :], dpx_hbm.at[rows, pl.ds(C_RQ, wq)], sems.at[3 * d]),
                        pltpu.make_async_copy(sk_s.at[parked(ch), :], dpx_hbm.at[rows, pl.ds(C_RK, wq)],
                                              sems.at[3 * d + 1]),
                        pltpu.make_async_copy(sv_s.at[parked(ch), :], dpx_hbm.at[rows, pl.ds(C_RV, wo)],
                                              sems.at[3 * d + 2])]
            return cps

        @pl.when(jnp.logical_or(n > nc // 2, jnp.logical_and(n == 0, b > 0)))
        def _():
            for cp in flush():
                cp.wait()

        @pl.when(n == 0)
        def _():
            ds_s[...] = jnp.zeros_like(ds_s)
            acc_s[...] = jnp.zeros_like(acc_s)

        def chains(first_visit):
            for d, do_ref in enumerate((dof_ref, dob_ref)):
                rows = parked(chunks[d])
                for h in range(RET_HEADS):
                    a_i, rel, mask, qd, kd, gc = _decays(lg_ref[d, h], d)
                    q, k, v = _ret_head(ins[5 * d:5 * d + 5], h)
                    qb, kb, vb = q.astype(BF), k.astype(BF), v.astype(BF)
                    cq, cv = slice(h * RET_DK, (h + 1) * RET_DK), slice(h * RET_DV, (h + 1) * RET_DV)
                    dob = do_ref[:, cv].astype(BF)
                    sb = st_ref[h, d]
                    ds = ds_s[d, h]
                    dsb = ds.astype(BF)
                    raw = _dot(qb, kb, 1, 1)
                    sc = raw * mask
                    dsc = _dot(dob, vb, 1, 1) * mask
                    dscb = dsc.astype(BF)
                    x = _dot(dob, sb, 1, 1)
                    y = _dot(vb, dsb, 1, 1)
                    qq = q * qd
                    kk = k * kd
                    dq = _dot(dscb, kb) + x * qd
                    dk = _dot(dscb, qb, 0, 0) + y * kd
                    dv = _dot(sc, dob, 0, 0) + _dot(kk, dsb)
                    if first_visit:
                        sq_s[rows, cq] = dq.astype(BF)
                        sk_s[rows, cq] = dk.astype(BF)
                        sv_s[rows, cv] = dv.astype(BF)
                    else:
                        sq_s[rows, cq] = (sq_s[rows, cq].astype(F32) + dq).astype(BF)
                        sk_s[rows, cq] = ((sk_s[rows, cq].astype(F32) + dk) * (RET_DK ** -0.5)).astype(BF)
                        sv_s[rows, cv] = (sv_s[rows, cv].astype(F32) + dv).astype(BF)
                    t = (_sum_all(dsc * raw * rel) + _sum_all((a_i + 1.0) * qq * x)
                         + _sum_all((c - 1.0 - a_i) * kk * y) + c * gc * _sum_all(ds * sb.astype(F32)))
                    acc_s[4 * d + h:4 * d + h + 1, :] += t
                    ds_s[d, h] = ds * gc + _dot(qq, dob, 0, 0)

        @pl.when(jnp.logical_not(second))
        def _():
            chains(True)

        @pl.when(second)
        def _():
            chains(False)
            for cp in flush():
                cp.start()

        @pl.when(n == nc - 1)
        def _():
            ds0_ref[...] = ds_s[...]
            dlg_ref[...] = acc_s[...]

        @pl.when(jnp.logical_and(b == n_samp - 1, n == nc - 1))
        def _():
            for cp in flush():
                cp.wait()

    do_spec_f = pl.BlockSpec((c, wo), lambda b, n: (row_f(b, n), 0))
    do_spec_b = pl.BlockSpec((c, wo), lambda b, n: (row_b(b, n), 0))
    return _call(
        body, [lg] + [px] * 10 + [do, do, saved, dpx], comm, name="ret_bwd", grid=(n_samp, nc), aliases={14: 0},
        in_specs=[SMEM] + _ret_specs(row_f, row_b) + [
            do_spec_f, do_spec_b,
            pl.BlockSpec((None, RET_HEADS, 2, None, RET_DK, RET_DV), lambda b, n: (b, 0, 0, nc - 1 - n, 0, 0)),
            ANY],
        out_specs=(ANY,
                   pl.BlockSpec((None, 2, RET_HEADS, RET_DK, RET_DV), lambda b, n: (b, 0, 0, 0, 0)),
                   pl.BlockSpec((None, 8, 128), lambda b, n: (b, 0, 0))),
        out_shape=(SDS(dpx.shape, dpx.dtype),
                   SDS((n_samp, 2, RET_HEADS, RET_DK, RET_DV), F32), SDS((n_samp, 8, 128), F32)),
        scratch_shapes=[pltpu.VMEM((2, RET_HEADS, RET_DK, RET_DV), F32), pltpu.VMEM((8, 128), F32),
                        pltpu.VMEM((seq, wq), BF), pltpu.VMEM((seq, wq), BF), pltpu.VMEM((seq, wo), BF),
                        pltpu.SemaphoreType.DMA((6,))],
        compiler_params=_cp(("arbitrary", "arbitrary"), 60))


def _norm_rope(x, w, cos, sin):
    xn = x * _rms(x) * w
    return xn * cos + _swap_pairs(xn) * sin


def _norm_rope_bwd(dy, x, w, cos, sin):
    dxn = dy * cos + _swap_pairs(dy * sin)
    r = _rms(x)
    xh = x * r
    return _rms_bwd(dxn * w, xh, r), jnp.sum(dxn * xh, axis=0, keepdims=True)


def _att_prep_q(px, cos_all, sin_all, qnw, t_lat):
    hd = ATT_HEAD_DIM
    wblk = ATT_REP * hd

    def body(x_ref, cos_ref, sin_ref, w_ref, o_ref):
        for r in range(ATT_REP):
            cols = slice(r * hd, (r + 1) * hd)
            qr = _norm_rope(x_ref[:, cols].astype(F32), w_ref[...], cos_ref[...], sin_ref[...])
            o_ref[:, cols] = (qr * (hd ** -0.5)).astype(BF)

    return pl.pallas_call(
        body, name="att_prep_q", grid=(t_lat // TM, ATT_KV_HEADS),
        in_specs=[pl.BlockSpec((TM, wblk), lambda i, g: (i, C_AQ // wblk + g)),
                  pl.BlockSpec((TM, hd), lambda i, g: (i, 0)),
                  pl.BlockSpec((TM, hd), lambda i, g: (i, 0)),
                  pl.BlockSpec((1, hd), lambda i, g: (0, 0))],
        out_specs=pl.BlockSpec((TM, wblk), lambda i, g: (i, g)),
        out_shape=SDS((t_lat, ATT_HEADS * hd), BF),
        compiler_params=_cp(("parallel", "parallel")))(px, cos_all, sin_all, qnw)


def _att_prep_kv(px, cos_all, sin_all, knw):
    rows = px.shape[0]
    hd = ATT_HEAD_DIM
    kvw = ATT_KV_HEADS * hd

    def body(x_ref, cos_ref, sin_ref, w_ref, k_ref, v_ref):
        for g in range(ATT_KV_HEADS):
            cols = slice(g * hd, (g + 1) * hd)
            k_ref[:, cols] = _norm_rope(x_ref[:, cols].astype(F32), w_ref[...], cos_ref[...],
                                        sin_ref[...]).astype(BF)
            v_ref[:, 2 * g * hd:(2 * g + 1) * hd] = x_ref[:, kvw + g * hd:kvw + (g + 1) * hd].astype(BF)
            v_ref[:, (2 * g + 1) * hd:(2 * g + 2) * hd] = jnp.ones((TM, hd), BF)

    return pl.pallas_call(
        body, name="att_prep_kv", grid=(rows // TM,),
        in_specs=[pl.BlockSpec((TM, 2 * kvw), lambda i: (i, C_AK // (2 * kvw))),
                  pl.BlockSpec((TM, hd), lambda i: (i, 0)),
                  pl.BlockSpec((TM, hd), lambda i: (i, 0)),
                  pl.BlockSpec((1, hd), lambda i: (0, 0))],
        out_specs=(pl.BlockSpec((TM, kvw), lambda i: (i, 0)), pl.BlockSpec((TM, 2 * kvw), lambda i: (i, 0))),
        out_shape=(SDS((rows, kvw), BF), SDS((rows, 2 * kvw), BF)),
        compiler_params=_cp(("parallel",)))(px, cos_all, sin_all, knw)


def _att_kv_bwd(dpx, dkl, dkc, dvl, dvc, px, cos_all, sin_all, knw):
    rows = px.shape[0]
    hd = ATT_HEAD_DIM
    kvw = ATT_KV_HEADS * hd
    n_lat = dkl.shape[0] // TM
    assert dkc.shape[0] == TM

    def body(dpx_hbm, dkl_ref, dkc_ref, dvl_ref, dvc_ref, x_ref, cos_ref, sin_ref, w_ref, o_ref, gw_ref):
        i = pl.program_id(0)

        @pl.when(i == 0)
        def _():
            gw_ref[...] = jnp.zeros_like(gw_ref)

        is_lat = i < n_lat
        dk = jnp.where(is_lat, dkl_ref[...], dkc_ref[...])
        dv = jnp.where(is_lat, dvl_ref[...], dvc_ref[...])
        for g in range(ATT_KV_HEADS):
            cols = slice(g * hd, (g + 1) * hd)
            dx, gw = _norm_rope_bwd(dk[:, cols], x_ref[:, cols].astype(F32), w_ref[...], cos_ref[...], sin_ref[...])
            o_ref[:, cols] = dx.astype(BF)
            gw_ref[...] += gw
        o_ref[:, kvw:] = dv.astype(BF)

    lat = pl.BlockSpec((TM, kvw), lambda i: (jnp.minimum(i, n_lat - 1), 0))
    ctx = pl.BlockSpec((TM, kvw), lambda i: (0, 0))
    kvcol = pl.BlockSpec((TM, 2 * kvw), lambda i: (i, C_AK // (2 * kvw)))
    return pl.pallas_call(
        body, name="att_kv_bwd", grid=(rows // TM,), input_output_aliases={0: 0},
        in_specs=[ANY, lat, ctx, lat, ctx, kvcol,
                  pl.BlockSpec((TM, hd), lambda i: (i, 0)),
                  pl.BlockSpec((TM, hd), lambda i: (i, 0)),
                  pl.BlockSpec((1, hd), lambda i: (0, 0))],
        out_specs=(kvcol, pl.BlockSpec((1, hd), lambda i: (0, 0))),
        out_shape=(SDS(dpx.shape, dpx.dtype), SDS((1, hd), F32)),
        compiler_params=_cp(("arbitrary",)))(dpx, dkl, dkc, dvl, dvc, px, cos_all, sin_all, knw)


def _stack_heads(ref_or_val):
    hd = ATT_HEAD_DIM
    return jnp.concatenate([ref_or_val[:, r * hd:(r + 1) * hd] for r in range(ATT_REP)], axis=0)


def _att_scores(q, kl, kc):
    sl = _dot(q, kl, 1, 1)
    sc = _dot(q, kc, 1, 1)
    m = jnp.maximum(jnp.max(sl, axis=-1, keepdims=True), jnp.max(sc, axis=-1, keepdims=True))
    return jnp.exp(sl - m), jnp.exp(sc - m), m


def _att_fwd(qn, kn, vn, n_samp, seq, lc):
    hd = ATT_HEAD_DIM
    tq = ATT_TQ
    nq = seq // tq
    wblk = ATT_REP * hd
    cb = n_samp * seq // lc
    t_lat = n_samp * seq

    def body(q_ref, kl_ref, kc_ref, vl_ref, vc_ref, o_ref, lse_ref):
        lane = lax.broadcasted_iota(jnp.int32, (tq, hd), 1)
        lse = jnp.zeros((tq, hd), F32)
        for r in range(ATT_REP):
            cols = slice(r * hd, (r + 1) * hd)
            el, ec, m = _att_scores(q_ref[:, cols], kl_ref[...], kc_ref[...])
            pv = _dot(el, vl_ref[...]) + _dot(ec, vc_ref[...])
            denom = pv[:, hd:hd + 1]
            o_ref[:, cols] = (pv[:, :hd] / denom).astype(BF)
            lse = jnp.where(lane == r, m + jnp.log(denom), lse)
        lse_ref[...] = lse

    return pl.pallas_call(
        body, name="att_fwd", grid=(n_samp, ATT_KV_HEADS, nq),
        in_specs=[pl.BlockSpec((tq, wblk), lambda b, g, i: (b * nq + i, g)),
                  pl.BlockSpec((seq, hd), lambda b, g, i: (b, g)),
                  pl.BlockSpec((lc, hd), lambda b, g, i: (cb + b, g)),
                  pl.BlockSpec((seq, 2 * hd), lambda b, g, i: (b, g)),
                  pl.BlockSpec((lc, 2 * hd), lambda b, g, i: (cb + b, g))],
        out_specs=(pl.BlockSpec((tq, wblk), lambda b, g, i: (b * nq + i, g)),
                   pl.BlockSpec((tq, hd), lambda b, g, i: (b * nq + i, g))),
        out_shape=(SDS((t_lat, ATT_HEADS * hd), BF), SDS((t_lat, ATT_KV_HEADS * hd), F32)),
        compiler_params=_cp(("parallel", "parallel", "parallel"), 48))(qn, kn, kn, vn, vn)


def _att_bwd(dpx, qn, kn, vn, px, o_att, lse, do_att, cos_all, sin_all, qnw, n_samp, seq, lc, comm):
    hd = ATT_HEAD_DIM
    tq = ATT_TQ
    nq = seq // tq
    wblk = ATT_REP * hd
    cb = n_samp * seq // lc
    t_lat = n_samp * seq
    kvw = ATT_KV_HEADS * hd
    scale = hd ** -0.5

    def body(dpx_hbm, q_ref, kl_ref, kc_ref, vl_ref, vc_ref, o_ref, do_ref, x_ref, cos_ref, sin_ref, w_ref,
             lse_ref, dq_ref, dkl_ref, dkc_ref, dvl_ref, dvc_ref, gw_ref, akl, akc, avl, avc, aw):
        i = pl.program_id(2)

        @pl.when(i == 0)
        def _():
            akl[...] = jnp.zeros_like(akl)
            akc[...] = jnp.zeros_like(akc)
            avl[...] = jnp.zeros_like(avl)
            avc[...] = jnp.zeros_like(avc)
            aw[...] = jnp.zeros_like(aw)

        dobs, pls, pcs, dsls, dscs = [], [], [], [], []
        for r in range(ATT_REP):
            cols = slice(r * hd, (r + 1) * hd)
            dob = do_ref[:, cols]
            delta = jnp.sum(dob.astype(F32) * o_ref[:, cols].astype(F32), axis=-1, keepdims=True)
            lse = lse_ref[:, r:r + 1]
            p_l = jnp.exp(_dot(q_ref[:, cols], kl_ref[...], 1, 1) - lse).astype(BF)
            p_c = jnp.exp(_dot(q_ref[:, cols], kc_ref[...], 1, 1) - lse).astype(BF)
            ds_l = (p_l * (_dot(dob, vl_ref[...], 1, 1) - delta)).astype(BF)
            ds_c = (p_c * (_dot(dob, vc_ref[...], 1, 1) - delta)).astype(BF)
            dq = (_dot(ds_l, kl_ref[...]) + _dot(ds_c, kc_ref[...])) * scale
            dx, gw = _norm_rope_bwd(dq, x_ref[:, cols].astype(F32), w_ref[...], cos_ref[...], sin_ref[...])
            dq_ref[:, cols] = dx.astype(BF)
            aw[...] += gw
            dobs.append(dob)
            pls.append(p_l)
            pcs.append(p_c)
            dsls.append(ds_l)
            dscs.append(ds_c)
        do4 = jnp.concatenate(dobs, axis=0)
        q4 = _stack_heads(q_ref)
        avl[...] += _dot(jnp.concatenate(pls, axis=0), do4, 0, 0)
        avc[...] += _dot(jnp.concatenate(pcs, axis=0), do4, 0, 0)
        akl[...] += _dot(jnp.concatenate(dsls, axis=0), q4, 0, 0)
        akc[...] += _dot(jnp.concatenate(dscs, axis=0), q4, 0, 0)

        @pl.when(i == nq - 1)
        def _():
            dkl_ref[...] = akl[...]
            dkc_ref[...] = akc[...]
            dvl_ref[...] = avl[...]
            dvc_ref[...] = avc[...]
            gw_ref[...] = aw[...]

    return _call(
        body, [dpx, qn, kn, kn, vn, vn, o_att, do_att, px, cos_all, sin_all, qnw, lse], comm,
        name="att_bwd", grid=(n_samp, ATT_KV_HEADS, nq), aliases={0: 0},
        in_specs=[ANY,
                  pl.BlockSpec((tq, wblk), lambda b, g, i: (b * nq + i, g)),
                  pl.BlockSpec((seq, hd), lambda b, g, i: (b, g)),
                  pl.BlockSpec((lc, hd), lambda b, g, i: (cb + b, g)),
                  pl.BlockSpec((seq, hd), lambda b, g, i: (b, 2 * g)),
                  pl.BlockSpec((lc, hd), lambda b, g, i: (cb + b, 2 * g)),
                  pl.BlockSpec((tq, wblk), lambda b, g, i: (b * nq + i, g)),
                  pl.BlockSpec((tq, wblk), lambda b, g, i: (b * nq + i, g)),
                  pl.BlockSpec((tq, wblk), lambda b, g, i: (b * nq + i, C_AQ // wblk + g)),
                  pl.BlockSpec((tq, hd), lambda b, g, i: (b * nq + i, 0)),
                  pl.BlockSpec((tq, hd), lambda b, g, i: (b * nq + i, 0)),
                  pl.BlockSpec((1, hd), lambda b, g, i: (0, 0)),
                  pl.BlockSpec((tq, hd), lambda b, g, i: (b * nq + i, g))],
        out_specs=(pl.BlockSpec((tq, wblk), lambda b, g, i: (b * nq + i, C_AQ // wblk + g)),
                   pl.BlockSpec((seq, hd), lambda b, g, i: (b, g)),
                   pl.BlockSpec((lc, hd), lambda b, g, i: (b, g)),
                   pl.BlockSpec((seq, hd), lambda b, g, i: (b, g)),
                   pl.BlockSpec((lc, hd), lambda b, g, i: (b, g)),
                   pl.BlockSpec((None, None, 1, hd), lambda b, g, i: (b, g, 0, 0))),
        out_shape=(SDS(dpx.shape, dpx.dtype),
                   SDS((t_lat, kvw), F32), SDS((n_samp * lc, kvw), F32),
                   SDS((t_lat, kvw), F32), SDS((n_samp * lc, kvw), F32),
                   SDS((n_samp, ATT_KV_HEADS, 1, hd), F32)),
        scratch_shapes=[pltpu.VMEM((seq, hd), F32), pltpu.VMEM((lc, hd), F32),
                        pltpu.VMEM((seq, hd), F32), pltpu.VMEM((lc, hd), F32), pltpu.VMEM((1, hd), F32)],
        compiler_params=_cp(("arbitrary", "arbitrary", "arbitrary"), 56))


def _merge(x_lat, target, o_f, o_b, o_att, px, gate3, w_o_ret, w_o_att, w_out, tiles_per_sample):
    t_lat = x_lat.shape[0]
    tm = 256
    n_t = t_lat // tm
    per = tiles_per_sample * (TM // tm)
    d = D_MODEL
    rv = RET_HEADS * RET_DV
    n_samp = gate3.shape[0] - 1

    half = d // 2
    n_px = 10

    def body(x_ref, t_ref, of_ref, ob_ref, oa_ref, *rest):
        pxs, rest = rest[:n_px], rest[n_px:]
        (gt_ref, wor_ref, woa_ref, wout_ref,
         gx_ref, dor_ref, doa_ref, dpx_hbm, loss_ref, dgt_ref, gwor_hbm, gwoa_hbm, gwout_hbm,
         aor, aoa, aout, drg_ref, dtail_ref, sems) = rest
        i = pl.program_id(0)

        def copies(step):
            rows = pl.ds(pl.multiple_of(step * tm, tm), tm)
            return (pltpu.make_async_copy(drg_ref, dpx_hbm.at[rows, pl.ds(C_RG, rv)], sems.at[0]),
                    pltpu.make_async_copy(dtail_ref, dpx_hbm.at[rows, pl.ds(C_AG, 3 * d)], sems.at[1]))

        @pl.when(i == 0)
        def _():
            aor[...] = jnp.zeros_like(aor)
            aoa[...] = jnp.zeros_like(aoa)
            aout[...] = jnp.zeros_like(aout)
            loss_ref[...] = jnp.zeros_like(loss_ref)

        @pl.when(i % per == 0)
        def _():
            dgt_ref[...] = jnp.zeros_like(dgt_ref)

        def cat(refs):
            return jnp.concatenate([r[...] for r in refs], axis=1).astype(F32)

        def ret_head(h):
            cols = slice(h * RET_DV, (h + 1) * RET_DV)
            o = of_ref[:, cols].astype(F32) + ob_ref[:, cols].astype(F32)
            r = _rms(o)
            g = pxs[h][...].astype(F32)
            return o * r, r, g, _sigmoid(g)

        def att_half(k):
            o = oa_ref[:, k * half:(k + 1) * half].astype(F32)
            g = pxs[4 + k][...].astype(F32)
            return o, g, _sigmoid(g)

        yrs = []
        for h in range(RET_HEADS):
            on, _, g, sg = ret_head(h)
            yrs.append((on * (g * sg)).astype(BF))
        yr = jnp.concatenate(yrs, axis=1)
        yas = []
        for k in range(2):
            o, g, sg = att_half(k)
            yas.append((o * (g * sg)).astype(BF))
        ya = jnp.concatenate(yas, axis=1)

        a = jnp.dot(yr, wor_ref[...], preferred_element_type=F32)
        b = jnp.dot(ya, woa_ref[...], preferred_element_type=F32)
        sr = _sigmoid(cat(pxs[6:8]))
        sa = _sigmoid(cat(pxs[8:10]))
        yb = (sr * a + sa * b).astype(BF)
        out = jnp.dot(yb, wout_ref[...], preferred_element_type=F32)
        gate = gt_ref[...]
        err = x_ref[...] + gate * out - t_ref[...]
        loss_ref[...] += 0.5 * _sum_all(err * err) * (1.0 / d)
        dy_tok = err * (1.0 / d)
        gx_ref[...] = dy_tok
        dgt_ref[...] += jnp.sum(dy_tok * out, axis=0, keepdims=True)
        dout = (dy_tok * gate).astype(BF)
        aout[...] += _dot(yb, dout, 0, 0)
        dyy = _dot(dout, wout_ref[...], 1, 1)
        da = (dyy * sr).astype(BF)
        db = (dyy * sa).astype(BF)
        aor[...] += _dot(yr, da, 0, 0)
        aoa[...] += _dot(ya, db, 0, 0)
        dyr = _dot(da, wor_ref[...], 1, 1)
        dya = _dot(db, woa_ref[...], 1, 1)

        @pl.when(i > 0)
        def _():
            for cp in copies(i - 1):
                cp.wait()

        dtail_ref[:, d:2 * d] = (dyy * a * (sr * (1.0 - sr))).astype(BF)
        dtail_ref[:, 2 * d:] = (dyy * b * (sa * (1.0 - sa))).astype(BF)
        for h in range(RET_HEADS):
            cols = slice(h * RET_DV, (h + 1) * RET_DV)
            on, r, g, sg = ret_head(h)
            dy = dyr[:, cols]
            drg_ref[:, cols] = (dy * on * (sg * (1.0 + g * (1.0 - sg)))).astype(BF)
            dor_ref[:, cols] = _rms_bwd(dy * (g * sg), on, r).astype(BF)
        for k in range(2):
            cols = slice(k * half, (k + 1) * half)
            o, g, sg = att_half(k)
            dy = dya[:, cols]
            dtail_ref[:, cols] = (dy * o * (sg * (1.0 + g * (1.0 - sg)))).astype(BF)
            doa_ref[:, cols] = (dy * (g * sg)).astype(BF)
        for cp in copies(i):
            cp.start()

        @pl.when(i == n_t - 1)
        def _():
            for cp in copies(i):
                cp.wait()
            pltpu.sync_copy(aor, gwor_hbm)
            pltpu.sync_copy(aoa, gwoa_hbm)
            pltpu.sync_copy(aout, gwout_hbm)

    def px_blk(col):
        return pl.BlockSpec((tm, half), lambda i: (i, col // half))

    def resident(shape):
        return pl.BlockSpec(shape, lambda i: (0, 0), pipeline_mode=pl.Buffered(1))

    px_cols = ([C_RG + k * half for k in range(4)] + [C_AG, C_AG + half]
               + [C_MR, C_MR + half, C_MA, C_MA + half])
    return pl.pallas_call(
        body, name="merge", grid=(n_t,),
        in_specs=[pl.BlockSpec((tm, d), lambda i: (i, 0)),
                  pl.BlockSpec((tm, d), lambda i: (i, 0)),
                  pl.BlockSpec((tm, rv), lambda i: (i, 0)),
                  pl.BlockSpec((tm, rv), lambda i: (i, 0)),
                  pl.BlockSpec((tm, d), lambda i: (i, 0))]
        + [px_blk(col) for col in px_cols]
        + [pl.BlockSpec((None, 1, d), lambda i: (i // per, 0, 0)),
           resident((rv, d)), resident((d, d)), resident((d, d))],
        out_specs=(pl.BlockSpec((tm, d), lambda i: (i, 0)),
                   pl.BlockSpec((tm, rv), lambda i: (i, 0)),
                   pl.BlockSpec((tm, d), lambda i: (i, 0)),
                   ANY,
                   pl.BlockSpec((8, 128), lambda i: (0, 0)),
                   pl.BlockSpec((None, 1, d), lambda i: (i // per, 0, 0)),
                   ANY, ANY, ANY),
        out_shape=(SDS((t_lat, d), F32), SDS((t_lat, rv), BF), SDS((t_lat, d), BF),
                   SDS((px.shape[0], IN_COLS), BF),
                   SDS((8, 128), F32), SDS((n_samp, 1, d), F32),
                   SDS((rv, d), F32), SDS((d, d), F32), SDS((d, d), F32)),
        scratch_shapes=[pltpu.VMEM((rv, d), F32), pltpu.VMEM((d, d), F32), pltpu.VMEM((d, d), F32),
                        pltpu.VMEM((tm, rv), BF), pltpu.VMEM((tm, 3 * d), BF), pltpu.SemaphoreType.DMA((2,))],
        compiler_params=_cp(("arbitrary",), 56))(
            x_lat, target, o_f, o_b, o_att, *([px] * n_px), gate3, w_o_ret, w_o_att, w_out)


def _place():
    x, y, c = lax.axis_index("x"), lax.axis_index("y"), lax.axis_index("c")
    chips = [(1 - x, y), (x, 1 - y), (1 - x, 1 - y)]
    return x, y, c, chips


def _remote(src, dst, send_sem, recv_sem, to):
    return pltpu.make_async_remote_copy(src_ref=src, dst_ref=dst, send_sem=send_sem, recv_sem=recv_sem,
                                        device_id=to, device_id_type=MESH)


def _place_ids():
    x, y, c = lax.axis_index("x"), lax.axis_index("y"), lax.axis_index("c")
    me = 2 * x + y
    return jnp.stack([x, y, c, me, me, 2 * (1 - x) + y, 2 * x + 1 - y, 2 * (1 - x) + 1 - y]).astype(jnp.int32)


def _ag_comm(bufs):
    n, m = len(bufs), 3

    def half(ref, s, which):
        h = ref.shape[1] // 2
        return ref.at[s, pl.ds(which * h, h), :]

    def ici(ins, outs, ssem, rsem, base):
        x, y, c, chips = _place()
        sends, recvs = [], []
        for a in range(n):
            for j in range(m):
                k, chip = base + a * m + j, chips[j]
                mine, theirs = half(outs[a], 2 * x + y, c), half(outs[a], 2 * chip[0] + chip[1], c)
                sends.append(_remote(mine, mine, ssem.at[k], rsem.at[k], (*chip, c)))
                recvs.append(_remote(theirs, theirs, ssem.at[k], rsem.at[k], (*chip, c)))
        return sends, recvs

    def d2d(ins, outs, ssem, rsem, base):
        x, y, c, chips = _place()
        sends, recvs = [], []
        for a in range(n):
            for j in range(m):
                k, s = base + (n + a) * m + j, 2 * chips[j][0] + chips[j][1]
                sends.append(_remote(half(outs[a], s, c), half(outs[a], s, c), ssem.at[k], rsem.at[k], (x, y, 1 - c)))
                recvs.append(_remote(half(outs[a], s, 1 - c), half(outs[a], s, 1 - c), ssem.at[k], rsem.at[k],
                                     (x, y, 1 - c)))
        return sends, recvs

    return _Comm("all_gather", tuple(bufs), tuple(SDS(b.shape, b.dtype) for b in bufs), {a: a for a in range(n)},
                 2 * n * m, (ici, d2d))


def _swap_comm(grads):
    n = len(grads)

    def phase(ins, outs, ssem, rsem, base):
        x, y, c, _ = _place()
        sends = []
        for a in range(n):
            h = ins[a].shape[1] // 2
            sends.append(_remote(ins[a].at[:, pl.ds((1 - c) * h, h), :], outs[a], ssem.at[base + a],
                                 rsem.at[base + a], (x, y, 1 - c)))
        return sends, sends

    return _Comm("swap_halves", tuple(grads),
                 tuple(SDS((g.shape[0], g.shape[1] // 2, g.shape[2]), g.dtype) for g in grads), {}, n, (phase,))


def _exchange_comm(parts):
    n = len(parts)

    def phase(ins, outs, ssem, rsem, base):
        x, y, c, chips = _place()
        sends = []
        for a in range(n):
            for j, chip in enumerate(chips):
                k = base + 3 * a + j
                sends.append(_remote(ins[a].at[2 * chip[0] + chip[1]], outs[a].at[j], ssem.at[k], rsem.at[k],
                                     (*chip, c)))
        return sends, sends

    return _Comm("exchange_shards", tuple(parts), tuple(SDS((3,) + p.shape[1:], p.dtype) for p in parts), {}, 3 * n,
                 (phase,))


def _join_comm(bufs):
    n = len(bufs)

    def phase(ins, outs, ssem, rsem, base):
        x, y, c, _ = _place()
        sends, recvs = [], []
        for a in range(n):
            h = outs[a].shape[0] // 2
            mine, other = outs[a].at[pl.ds(c * h, h), :], outs[a].at[pl.ds((1 - c) * h, h), :]
            sends.append(_remote(mine, mine, ssem.at[base + a], rsem.at[base + a], (x, y, 1 - c)))
            recvs.append(_remote(other, other, ssem.at[base + a], rsem.at[base + a], (x, y, 1 - c)))
        return sends, recvs

    return _Comm("join_halves", tuple(bufs), tuple(SDS(b.shape, b.dtype) for b in bufs), {a: a for a in range(n)},
                 n, (phase,))


def _cast_place(w, ids):
    rows, cols = w.shape
    tr = min(rows, 256)

    def body(ids_ref, w_ref, o_ref):
        o_ref[...] = w_ref[...].astype(BF)

    return pl.pallas_call(
        body, name="cast_place",
        grid_spec=pltpu.PrefetchScalarGridSpec(
            num_scalar_prefetch=1, grid=(rows // tr,),
            in_specs=[pl.BlockSpec((tr, cols), lambda i, ids_ref: (i, 0))],
            out_specs=pl.BlockSpec((None, tr, cols), lambda i, ids_ref: (ids_ref[3], i, 0))),
        out_shape=SDS((N_SHARD, rows, cols), BF),
        compiler_params=_cp(("parallel",), 40))(ids, w)


def _chip_sum(g, p, ids):
    n_s, rows, cols = g.shape
    h = rows // 2
    tr = min(h, 256)
    nb = h // tr

    def body(ids_ref, g_ref, p_ref, o_ref, o16_ref):
        t = g_ref[...] + p_ref[...]
        o_ref[...] = t
        o16_ref[...] = t.astype(BF)

    out_spec = pl.BlockSpec((None, tr, cols), lambda s, i, ids_ref: (s, i, 0))
    return pl.pallas_call(
        body, name="chip_sum",
        grid_spec=pltpu.PrefetchScalarGridSpec(
            num_scalar_prefetch=1, grid=(n_s, nb),
            in_specs=[pl.BlockSpec((None, tr, cols), lambda s, i, ids_ref: (s, ids_ref[2] * nb + i, 0)),
                      pl.BlockSpec((None, tr, cols), lambda s, i, ids_ref: (s, i, 0))],
            out_specs=(out_spec, out_spec)),
        out_shape=(SDS((n_s, h, cols), g.dtype), SDS((n_s, h, cols), BF)),
        compiler_params=_cp(("parallel", "parallel"), 40))(ids, g, p)


def _shard_sum(t, q, ids):
    _, h, cols = t.shape
    tr = min(h, 256)
    nb = h // tr

    def body(ids_ref, t_ref, q_ref, o_ref):
        o_ref[...] = ((t_ref[...] + q_ref[0].astype(F32)) + q_ref[1].astype(F32)) + q_ref[2].astype(F32)

    return pl.pallas_call(
        body, name="shard_sum",
        grid_spec=pltpu.PrefetchScalarGridSpec(
            num_scalar_prefetch=1, grid=(nb,),
            in_specs=[pl.BlockSpec((None, tr, cols), lambda i, ids_ref: (ids_ref[3], i, 0)),
                      pl.BlockSpec((3, tr, cols), lambda i, ids_ref: (0, i, 0))],
            out_specs=pl.BlockSpec((tr, cols), lambda i, ids_ref: (ids_ref[2] * nb + i, 0))),
        out_shape=SDS((2 * h, cols), t.dtype),
        compiler_params=_cp(("parallel",), 40))(ids, t, q)


def _gather_small(block, n_sum):
    rows, cols = block.shape
    n_dev = 8

    def body(x_ref, o_ref, g_ref, buf, send_sems, recv_sems, local_sem):
        x, y, c, chips = _place()
        me, sibling = (x, y, c), (x, y, 1 - c)

        def slot(px_, py_, pc_):
            return buf.at[4 * px_ + 2 * py_ + pc_]

        def copy(k, who, to, src=None):
            return _remote(slot(*who) if src is None else src, slot(*who), send_sems.at[k], recv_sems.at[k], to)

        mine = pltpu.make_async_copy(x_ref, slot(*me), local_sem)
        mine.start()
        first = [copy(0, me, sibling, src=x_ref)]
        first += [copy(1 + j, me, (*chip, c), src=x_ref) for j, chip in enumerate(chips)]
        for cp in first:
            cp.start()
        passed = [copy(4 + j, (*chip, c), sibling) for j, chip in enumerate(chips)]
        for j, chip in enumerate(chips):
            copy(1 + j, (*chip, c), me).wait_recv()
            passed[j].start()
        copy(0, sibling, me).wait_recv()
        for j, chip in enumerate(chips):
            copy(4 + j, (*chip, 1 - c), me).wait_recv()
        for cp in first + passed:
            cp.wait_send()
        mine.wait()
        acc = buf[0, :, :n_sum]
        for s in range(1, n_dev):
            acc = acc + buf[s, :, :n_sum]
        o_ref[...] = acc
        for s in range(n_dev):
            g_ref[s * rows:(s + 1) * rows, :] = buf[s, :, n_sum:]

    return pl.pallas_call(
        body, name="gather_small",
        in_specs=[pl.BlockSpec(memory_space=pltpu.VMEM)],
        out_specs=(pl.BlockSpec(memory_space=pltpu.VMEM), pl.BlockSpec(memory_space=pltpu.VMEM)),
        out_shape=(SDS((rows, n_sum), F32), SDS((n_dev * rows, cols - n_sum), F32)),
        scratch_shapes=[pltpu.VMEM((n_dev, rows, cols), F32), pltpu.SemaphoreType.DMA((7,)),
                        pltpu.SemaphoreType.DMA((7,)), pltpu.SemaphoreType.DMA],
        compiler_params=_cp(has_side_effects=True))(block)


def _adam_math(w, g, m, v):
    m = ADAM_B1 * m + (1.0 - ADAM_B1) * g
    v = ADAM_B2 * v + (1.0 - ADAM_B2) * (g * g)
    m_hat = m / (1.0 - ADAM_B1 ** ADAM_STEP)
    v_hat = v / (1.0 - ADAM_B2 ** ADAM_STEP)
    delta = -ADAM_LR * (m_hat / (jnp.sqrt(v_hat) + ADAM_EPS) + ADAM_WD * w)
    return delta, m, v


def _adamw(w, g, m, v):
    rows, cols = w.shape
    tr = min(rows, 256 if cols <= 2048 else 128)

    def body(w_ref, g_ref, m_ref, v_ref, go_ref, d_ref, nm_ref, nv_ref):
        g = g_ref[...]
        go_ref[...] = g
        d_ref[...], nm_ref[...], nv_ref[...] = _adam_math(w_ref[...], g, m_ref[...], v_ref[...])

    spec = pl.BlockSpec((tr, cols), lambda i: (i, 0))
    return pl.pallas_call(
        body, name="adamw", grid=(rows // tr,), in_specs=[spec] * 4, out_specs=(spec,) * 4,
        out_shape=(SDS(w.shape, F32),) * 4, compiler_params=_cp(("parallel",), 40))(w, g, m, v)


def _adamw_small(w, g, m, v):
    def body(w_ref, g_ref, m_ref, v_ref, go_ref, d_ref, nm_ref, nv_ref):
        w = w_ref[...]
        g = g_ref[...]
        sub = lax.broadcasted_iota(jnp.int32, w.shape, 0)
        lane = lax.broadcasted_iota(jnp.int32, w.shape, 1)
        is_ret = jnp.logical_and(sub == 5, lane < 2 * RET_HEADS)
        u = jnp.exp(jnp.where(is_ret, w, -1.0) * jnp.log(2.0))
        g = jnp.where(is_ret, g * (-u * jnp.log(2.0) / (1.0 - u)), g)
        go_ref[...] = g
        d_ref[...], nm_ref[...], nv_ref[...] = _adam_math(w, g, m_ref[...], v_ref[...])

    return pl.pallas_call(body, name="adamw_small", out_shape=(SDS(w.shape, F32),) * 4)(w, g, m, v)


def _rope_tables(seq, n_samp, n_ctx_rows):
    rows = seq // GRID_W
    row = jnp.repeat(jnp.arange(rows, dtype=F32), GRID_W)
    col = jnp.tile(jnp.arange(GRID_W, dtype=F32), rows)
    half = ATT_HEAD_DIM // 2
    freqs = ROPE_THETA ** (-jnp.arange(0, half, 2, dtype=F32) / half)
    ang = jnp.concatenate([row[:, None] * freqs, col[:, None] * freqs], axis=-1)
    cos, sin = jnp.cos(ang), jnp.sin(ang)
    cos_f = jnp.repeat(cos, 2, axis=1)
    sin_s = jnp.stack([-sin, sin], axis=-1).reshape(seq, ATT_HEAD_DIM)
    cos_all = jnp.concatenate([jnp.tile(cos_f, (n_samp, 1)), jnp.ones((n_ctx_rows, ATT_HEAD_DIM), F32)], axis=0)
    sin_all = jnp.concatenate([jnp.tile(sin_s, (n_samp, 1)), jnp.zeros((n_ctx_rows, ATT_HEAD_DIM), F32)], axis=0)
    return cos_all, sin_all


def _pack_small(c_ctx, norm_w, b_ada, ret, qn, kn):
    d = D_MODEL
    row5 = jnp.concatenate([ret.reshape(-1), jnp.zeros((128 - 2 * RET_HEADS,), F32), qn.reshape(-1), kn.reshape(-1),
                            jnp.zeros((d - 384,), F32)])
    return jnp.concatenate([c_ctx.reshape(1, d), norm_w.reshape(1, d), b_ada.reshape(3, d), row5.reshape(1, d),
                            jnp.zeros((2, d), F32)], axis=0)


def _unpack_small(p):
    d = D_MODEL
    return (p[0], p[1:2], p[2:5].reshape(1, 3 * d), p[5, :2 * RET_HEADS].reshape(1, 2, RET_HEADS),
            p[5:6, 128:256], p[5:6, 256:384])


def _step(x, c, ctx, c_ctx, norm_w, b_ada, ret_log2_decay, q_norm_w, k_norm_w, loss_target, weights, ids):
    n_samp, seq, d = x.shape
    lc = ctx.shape[1]
    t_lat, t_ctx = n_samp * seq, n_samp * lc
    assert seq % TM == 0 and t_ctx == TM and t_lat % lc == 0 and seq % GRID_W == 0
    tps = seq // TM

    x_lat = x.reshape(t_lat, d)
    x_ctx = ctx.reshape(t_ctx, d)
    cvec8 = jnp.concatenate([c, c_ctx.reshape(1, d), jnp.zeros((8 - n_samp - 1, d), F32)], axis=0)
    lg = jnp.log1p(-jnp.exp2(ret_log2_decay.reshape(2, RET_HEADS)))
    cos_all, sin_all = _rope_tables(seq, n_samp, t_ctx)

    w_ada_b, w_in_b, w_or_b, w_oa_b, w_out_b = weights
    c_all, mod_shards = _adaln_fwd(cvec8, w_ada_b, b_ada)
    mod8 = mod_shards.transpose(1, 0, 2).reshape(8, 3 * d)
    mod3 = mod8[:n_samp + 1]
    shift3 = mod3[:, None, 0:d]
    scale3 = mod3[:, None, d:2 * d]
    gate3 = mod3[:, None, 2 * d:3 * d]

    hx, hxt = _norm_fwd(x_lat, x_ctx, norm_w, scale3, shift3, tps, n_samp)
    px, w_in_g = _in_proj_gather(hx, w_in_b, ids)

    states0 = _ctx_state_fwd(px, lg, n_samp, t_lat, lc)
    (o_f, o_b, saved), w_o = _ret_fwd(px, states0, lg, n_samp, seq,
                                      comm=_ag_comm((w_or_b, w_oa_b, w_out_b)))
    w_o_ret, w_o_att, w_out = (w.reshape(-1, d) for w in w_o)

    qn = _att_prep_q(px, cos_all, sin_all, q_norm_w, t_lat)
    kn, vn = _att_prep_kv(px, cos_all, sin_all, k_norm_w)
    o_att, lse = _att_fwd(qn, kn, vn, n_samp, seq, lc)

    (gx_res, do, do_att, dpx, loss8, dgate, g_w_o_ret, g_w_o_att, g_w_out) = _merge(
        x_lat, loss_target.reshape(t_lat, d), o_f, o_b, o_att, px, gate3, w_o_ret, w_o_att, w_out, tps)

    g_a = [g.reshape(N_SHARD, -1, d) for g in (g_w_o_ret, g_w_o_att, g_w_out)]
    (dpx, dkl, dkc, dvl, dvc, gqw), (*sib_a, w_ada_g) = _att_bwd(
        dpx, qn, kn, vn, px, o_att, lse, do_att, cos_all, sin_all, q_norm_w, n_samp, seq, lc,
        comm=_join_comms(_swap_comm(g_a), _ag_comm((w_ada_b,))))
    dpx, gkw = _att_kv_bwd(dpx, dkl, dkc, dvl, dvc, px, cos_all, sin_all, k_norm_w)
    t_a = [_chip_sum(g, p, ids) for g, p in zip(g_a, sib_a)]

    (dpx, dstates, dlg_lat), q_a = _ret_bwd(dpx, px, do, saved, lg, n_samp, seq,
                                            comm=_exchange_comm([t16 for _, t16 in t_a]))
    r_a = [_shard_sum(t, q, ids) for (t, _), q in zip(t_a, q_a)]
    dpx, dlg_ctx = _ctx_state_bwd(dpx, px, dstates, lg, n_samp, t_lat, lc)
    dpx = _zero_ctx_tail(dpx, t_lat)

    n_tiles = dpx.shape[0] // _big_rows(dpx.shape[0])
    g_b = _gw_in(hxt, dpx)
    dhx, (sib_b, *r_a) = _dhx(dpx, w_in_g, 0, 1, None, _join_comms(_swap_comm([g_b]), _join_comm(r_a)))
    t_b, t16_b = _chip_sum(g_b, sib_b, ids)
    dhx, (q_b,) = _dhx(dpx, w_in_g, 1, n_tiles - 1, dhx, _exchange_comm([t16_b]))
    r_b_half = _shard_sum(t_b, q_b, ids)
    grad_x, dshift, dscale, g_norm_w = _norm_bwd(x_lat, x_ctx, dhx, gx_res, norm_w, scale3, tps, n_samp)

    dgate_all = jnp.concatenate([dgate, jnp.zeros((1, 1, d), F32)], axis=0)
    dmod3 = jnp.concatenate([dshift, dscale, dgate_all], axis=2).reshape(n_samp + 1, 3 * d)
    dmod8 = jnp.concatenate([dmod3, jnp.zeros((8 - n_samp - 1, 3 * d), F32)], axis=0)
    g_lg = (jnp.sum(dlg_lat[:, :, 0], axis=0).reshape(2, RET_HEADS)
            + jnp.stack([jnp.sum(dlg_ctx[:, :, 0, 0], axis=0), jnp.sum(dlg_ctx[:, :, 1, 0], axis=0)], axis=0))
    g_qw = jnp.sum(gqw, axis=(0, 1, 2))
    zero = jnp.zeros((d,), F32)

    local = _pack_small(zero, g_norm_w, jnp.zeros((3 * d,), F32), g_lg, g_qw, gkw).at[6, 0].set(loss8[0, 0])
    small_sum, dmod_all = _gather_small(jnp.concatenate([local, dmod8], axis=1), d)
    (g_w_ada, g_b_ada, dc_all), (r_b,) = _adaln_bwd(c_all, dmod_all, w_ada_g, comm=_join_comm([r_b_half]))
    dc_ctx = jnp.sum(dc_all.reshape(-1, 8, d)[:, n_samp], axis=0)
    small = small_sum + _pack_small(dc_ctx, zero, g_b_ada, jnp.zeros((2, RET_HEADS), F32), zero[:128], zero[:128])
    r_c = lax.dynamic_index_in_dim(g_w_ada, ids[3], 0, keepdims=False)
    return small[6, 0], grad_x.reshape(n_samp, seq, d), (r_c, r_b, *r_a), small


def kernel(x, c, ctx, c_ctx, norm_w, w_ada, b_ada, w_in, ret_log2_decay, q_norm_w, k_norm_w, w_o_ret, w_o_att, w_out, loss_target, m_c_ctx, m_norm_w, m_w_ada, m_b_ada, m_w_in, m_ret_log2_decay, m_q_norm_w, m_k_norm_w, m_w_o_ret, m_w_o_att, m_w_out, v_c_ctx, v_norm_w, v_w_ada, v_b_ada, v_w_in, v_ret_log2_decay, v_q_norm_w, v_k_norm_w, v_w_o_ret, v_w_o_att, v_w_out):
    big_w = (w_ada[0], w_in[0], w_o_ret[0], w_o_att[0], w_out[0])
    big_m = (m_w_ada[0], m_w_in[0], m_w_o_ret[0], m_w_o_att[0], m_w_out[0])
    big_v = (v_w_ada[0], v_w_in[0], v_w_o_ret[0], v_w_o_att[0], v_w_out[0])

    ids = _place_ids()
    loss, grad_x, big_grad, small_grad_in = _step(
        x, c, ctx, c_ctx, norm_w[0:1], b_ada[0:1], ret_log2_decay[0], q_norm_w[0:1], k_norm_w[0:1], loss_target,
        tuple(_cast_place(w, ids) for w in big_w), ids)
    small_w = _pack_small(c_ctx, norm_w, b_ada, ret_log2_decay, q_norm_w, k_norm_w)
    small_m = _pack_small(m_c_ctx, m_norm_w, m_b_ada, m_ret_log2_decay, m_q_norm_w, m_k_norm_w)
    small_v = _pack_small(v_c_ctx, v_norm_w, v_b_ada, v_ret_log2_decay, v_q_norm_w, v_k_norm_w)
    small_grad, small_delta, small_nm, small_nv = _adamw_small(small_w, small_grad_in, small_m, small_v)

    big_g, big_delta, big_nm, big_nv = [], [], [], []
    for w, g, m, v in zip(big_w, big_grad, big_m, big_v):
        go, dlt, nm, nv = _adamw(w, g, m, v)
        big_g.append(go[None])
        big_delta.append(dlt[None])
        big_nm.append(nm[None])
        big_nv.append(nv[None])
    big_grad = big_g

    def order(small_packed, big):
        s = _unpack_small(small_packed)
        return (s[0], s[1], big[0], s[2], big[1], s[3], s[4], s[5], big[2], big[3], big[4])

    return (loss, grad_x, *order(small_grad, big_grad), *order(small_delta, big_delta),
            *order(small_nm, big_nm), *order(small_nv, big_nv))
```

```python
import functools
from typing import NamedTuple

import jax
import jax.numpy as jnp
from jax import lax
from jax.experimental import pallas as pl
from jax.experimental.pallas import tpu as pltpu

F32 = jnp.float32
BF = jnp.bfloat16
SDS = jax.ShapeDtypeStruct
MESH = pl.DeviceIdType.MESH
ANY = pl.BlockSpec(memory_space=pl.ANY)
SMEM = pl.BlockSpec(memory_space=pltpu.SMEM)

D_MODEL = 1024
GRID_W = 64
RET_HEADS = 4
RET_DK = 256
RET_DV = 512
RET_CHUNK = 128
ATT_HEADS = 8
ATT_KV_HEADS = 2
ATT_REP = ATT_HEADS // ATT_KV_HEADS
ATT_HEAD_DIM = 128
ROPE_THETA = 10000.0
NORM_EPS = 1e-6
IN_COLS = 10752
KV_COLS = 3584
C_RK, C_RV, C_AK, C_AV, C_RQ, C_RG, C_AQ, C_AG, C_MR, C_MA = 0, 1024, 3072, 3328, 3584, 4608, 6656, 7680, 8704, 9728
N_SHARD = 4
ADA_W = 3 * D_MODEL // N_SHARD
IN_W = IN_COLS // N_SHARD
IN_BLK = IN_W
BPS = IN_W // IN_BLK
N_IN_BLK = IN_COLS // IN_BLK
TM = 512
ATT_TQ = 512
ADAM_LR, ADAM_B1, ADAM_B2, ADAM_EPS, ADAM_WD, ADAM_STEP = 0.001, 0.9, 0.999, 1e-08, 0.01, 10
MIB = 1024 * 1024


def _cp(sem=None, vmem_mb=None, **kw):
    if sem is not None:
        kw["dimension_semantics"] = sem
    if vmem_mb is not None:
        kw["vmem_limit_bytes"] = vmem_mb * MIB
    return pltpu.CompilerParams(**kw)


def _dot(a, b, ca=1, cb=0):
    return lax.dot_general(a.astype(BF), b.astype(BF), (((ca,), (cb,)), ((), ())), preferred_element_type=F32)


def _sigmoid(x):
    return 0.5 * jnp.tanh(0.5 * x) + 0.5


def _sum_all(x):
    return jnp.sum(jnp.sum(x, axis=1, keepdims=True), axis=0, keepdims=True)


def _swap_pairs(x):
    ax = x.ndim - 1
    lane = lax.broadcasted_iota(jnp.int32, x.shape, ax)
    nxt = pltpu.roll(x, x.shape[ax] - 1, ax)
    prv = pltpu.roll(x, 1, ax)
    return jnp.where(lane % 2 == 0, nxt, prv)


def _rms(x):
    return lax.rsqrt(jnp.mean(x * x, axis=-1, keepdims=True) + NORM_EPS)


def _rms_bwd(dxh, xh, r):
    return r * (dxh - xh * jnp.mean(dxh * xh, axis=-1, keepdims=True))


class _Comm(NamedTuple):
    name: str
    ins: tuple
    out_shapes: tuple
    aliases: dict
    n_sems: int
    phases: tuple


def _join_comms(*comms):
    offs, i_off, o_off, s_off = [], 0, 0, 0
    for cm in comms:
        offs.append((i_off, o_off, s_off))
        i_off, o_off, s_off = i_off + len(cm.ins), o_off + len(cm.out_shapes), s_off + cm.n_sems

    def phase(k):
        def run(ins, outs, ssem, rsem, base):
            sends, recvs = [], []
            for cm, (io, oo, so) in zip(comms, offs):
                if k < len(cm.phases):
                    s, r = cm.phases[k](ins[io:io + len(cm.ins)], outs[oo:oo + len(cm.out_shapes)], ssem, rsem,
                                        base + so)
                    sends += s
                    recvs += r
            return sends, recvs
        return run

    aliases = {}
    for cm, (io, oo, _) in zip(comms, offs):
        aliases.update({io + a: oo + b for a, b in cm.aliases.items()})
    return _Comm("+".join(cm.name for cm in comms), sum((cm.ins for cm in comms), ()),
                 sum((cm.out_shapes for cm in comms), ()), aliases, s_off,
                 tuple(phase(k) for k in range(max(len(cm.phases) for cm in comms))))


def _run_phases(comm, cins, couts, ssem, rsem):
    for k, phase in enumerate(comm.phases):
        sends, recvs = phase(cins, couts, ssem, rsem, 0)
        if k > 0:
            for cp in sends:
                cp.start()
        for cp in recvs:
            cp.wait_recv()
        for cp in sends:
            cp.wait_send()


def _call(body, args, comm, *, name, grid, in_specs, out_specs, out_shape, scratch_shapes=(),
          compiler_params, aliases=None):
    n_in, n_out, n_sc = len(in_specs), len(out_specs), len(scratch_shapes)
    n_ci, n_co = len(comm.ins), len(comm.out_shapes)
    io_alias = dict(aliases or {})
    io_alias.update({n_in + a: n_out + b for a, b in comm.aliases.items()})

    def kernel_body(*refs):
        ins, cins = refs[:n_in], refs[n_in:n_in + n_ci]
        outs = refs[n_in + n_ci:n_in + n_ci + n_out]
        couts = refs[n_in + n_ci + n_out:n_in + n_ci + n_out + n_co]
        scratch = refs[n_in + n_ci + n_out + n_co:n_in + n_ci + n_out + n_co + n_sc]
        ssem, rsem = refs[-2:]
        first = functools.reduce(jnp.logical_and, [pl.program_id(k) == 0 for k in range(len(grid))])
        last = functools.reduce(jnp.logical_and, [pl.program_id(k) == grid[k] - 1 for k in range(len(grid))])

        @pl.when(first)
        def _():
            for cp in comm.phases[0](cins, couts, ssem, rsem, 0)[0]:
                cp.start()

        body(*ins, *outs, *scratch)

        @pl.when(last)
        def _():
            _run_phases(comm, cins, couts, ssem, rsem)

    res = pl.pallas_call(
        kernel_body, name=name + "+" + comm.name, grid=grid, in_specs=list(in_specs) + [ANY] * n_ci,
        out_specs=tuple(out_specs) + tuple([ANY] * n_co), out_shape=tuple(out_shape) + tuple(comm.out_shapes),
        scratch_shapes=list(scratch_shapes) + [pltpu.SemaphoreType.DMA((comm.n_sems,)),
                                               pltpu.SemaphoreType.DMA((comm.n_sems,))],
        input_output_aliases=io_alias, compiler_params=compiler_params)(*args, *comm.ins)
    return tuple(res[:n_out]), tuple(res[n_out:])


def _adaln_fwd(cvec8, w_ada_b, b_ada):
    rows, d = cvec8.shape
    n_dev = 8

    def body(x_ref, w_hbm, b_ref, g_ref, o_ref, buf, wv, part, send_sems, recv_sems, local_sems):
        x, y, c, chips = _place()
        me, sibling, shard = (x, y, c), (x, y, 1 - c), 2 * x + y

        def slot(px_, py_, pc_):
            return 4 * px_ + 2 * py_ + pc_

        def copy(k, who, to, src=None):
            dst = buf.at[slot(*who)]
            return _remote(dst if src is None else src, dst, send_sems.at[k], recv_sems.at[k], to)

        weight = pltpu.make_async_copy(w_hbm.at[shard], wv, local_sems.at[0])
        weight.start()
        mine = pltpu.make_async_copy(x_ref, buf.at[slot(*me)], local_sems.at[1])
        mine.start()
        first = [copy(0, me, sibling, src=x_ref)]
        first += [copy(1 + j, me, (*chip, c), src=x_ref) for j, chip in enumerate(chips)]
        for cp in first:
            cp.start()
        passed = [copy(4 + j, (*chip, c), sibling) for j, chip in enumerate(chips)]
        for j, chip in enumerate(chips):
            copy(1 + j, (*chip, c), me).wait_recv()
            passed[j].start()
        copy(0, sibling, me).wait_recv()
        for j, chip in enumerate(chips):
            copy(4 + j, (*chip, 1 - c), me).wait_recv()
        for cp in first + passed:
            cp.wait_send()
        mine.wait()
        weight.wait()

        cv = buf[...].reshape(n_dev * rows, d)
        g_ref[...] = cv
        sc = (cv * _sigmoid(cv)).astype(BF)
        mod = jnp.dot(sc, wv[...], preferred_element_type=F32) + b_ref[shard]
        part[...] = mod.reshape(n_dev, rows, ADA_W)

        def back(j, chip):
            theirs = o_ref.at[2 * chip[0] + chip[1]]
            return (_remote(part.at[slot(*chip, c)], o_ref.at[shard], send_sems.at[7 + j], recv_sems.at[7 + j],
                            (*chip, c)),
                    _remote(theirs, theirs, send_sems.at[7 + j], recv_sems.at[7 + j], (*chip, c)))

        for j, chip in enumerate(chips):
            back(j, chip)[0].start()
        o_ref[shard] = part[slot(*me)]
        for j, chip in enumerate(chips):
            back(j, chip)[1].wait_recv()
        for j, chip in enumerate(chips):
            back(j, chip)[0].wait_send()

    vmem = pl.BlockSpec(memory_space=pltpu.VMEM)
    return pl.pallas_call(
        body, name="adaln_fwd", in_specs=[vmem, ANY, vmem], out_specs=(vmem, vmem),
        out_shape=(SDS((n_dev * rows, d), F32), SDS((N_SHARD, rows, ADA_W), F32)),
        scratch_shapes=[pltpu.VMEM((n_dev, rows, d), F32), pltpu.VMEM((D_MODEL, ADA_W), BF),
                        pltpu.VMEM((n_dev, rows, ADA_W), F32), pltpu.SemaphoreType.DMA((10,)),
                        pltpu.SemaphoreType.DMA((10,)), pltpu.SemaphoreType.DMA((2,))],
        compiler_params=_cp(vmem_mb=32, has_side_effects=True))(cvec8, w_ada_b, b_ada.reshape(N_SHARD, 1, ADA_W))


def _adaln_bwd(cvec, dmod, w_ada_g, comm):
    n_rows = cvec.shape[0]
    dmod_shards = dmod.reshape(n_rows, N_SHARD, ADA_W).transpose(1, 0, 2)

    def body(c_ref, d_ref, ds_ref, w_ref, gw_ref, gb_ref, dc_ref):
        x, y, _, _ = _place()
        cv = c_ref[...]
        sg = _sigmoid(cv)
        sc = cv * sg
        dm = d_ref[...]
        gb_ref[...] = jnp.sum(dm, axis=0, keepdims=True)
        gw_ref[...] = _dot(sc, ds_ref[2 * x + y], 0, 0)
        dsc = jnp.zeros(cv.shape, F32)
        for s in range(N_SHARD):
            dsc = dsc + _dot(dm[:, s * ADA_W:(s + 1) * ADA_W], w_ref[s], 1, 1)
        dc_ref[...] = dsc * (sg * (1.0 + cv * (1.0 - sg)))

    def whole(shape):
        return pl.BlockSpec(shape, lambda i: (0,) * len(shape))

    shapes = ((D_MODEL, ADA_W), (1, 3 * D_MODEL), (n_rows, D_MODEL))
    return _call(body, [cvec, dmod, dmod_shards, w_ada_g], comm, name="adaln_bwd", grid=(1,),
                 in_specs=[whole(cvec.shape), whole(dmod.shape), whole(dmod_shards.shape), whole(w_ada_g.shape)],
                 out_specs=tuple(whole(s) for s in shapes), out_shape=tuple(SDS(s, F32) for s in shapes),
                 compiler_params=_cp(("arbitrary",), 56))


def _big_rows(rows):
    return 1536 if rows % 1536 == 0 else TM


def _norm_fwd(x_lat, x_ctx, norm_w, scale3, shift3, tiles_per_sample, n_samp):
    n_lat = x_lat.shape[0] // TM
    rows = x_lat.shape[0] + x_ctx.shape[0]

    def samp(i):
        return jnp.minimum(i // tiles_per_sample, n_samp)

    def body(x_ref, c_ref, nw_ref, sc_ref, sh_ref, hx_ref, hxt_ref):
        x = jnp.where(pl.program_id(0) < n_lat, x_ref[...], c_ref[...])
        h = x * _rms(x) * nw_ref[...] * (1.0 + sc_ref[...]) + sh_ref[...]
        hx_ref[...] = h.astype(BF)
        hxt_ref[...] = h.T.astype(BF)

    return pl.pallas_call(
        body, name="norm_fwd", grid=(rows // TM,),
        in_specs=[pl.BlockSpec((TM, D_MODEL), lambda i: (jnp.minimum(i, n_lat - 1), 0)),
                  pl.BlockSpec((TM, D_MODEL), lambda i: (jnp.maximum(i - n_lat, 0), 0)),
                  pl.BlockSpec((1, D_MODEL), lambda i: (0, 0)),
                  pl.BlockSpec((None, 1, D_MODEL), lambda i: (samp(i), 0, 0)),
                  pl.BlockSpec((None, 1, D_MODEL), lambda i: (samp(i), 0, 0))],
        out_specs=(pl.BlockSpec((TM, D_MODEL), lambda i: (i, 0)),
                   pl.BlockSpec((D_MODEL, TM), lambda i: (0, i))),
        out_shape=(SDS((rows, D_MODEL), BF), SDS((D_MODEL, rows), BF)),
        compiler_params=_cp(("parallel",), 40))(x_lat, x_ctx, norm_w, scale3, shift3)


def _norm_bwd(x_lat, x_ctx, dhx, gx_res, norm_w, scale3, tiles_per_sample, n_samp):
    rows = x_lat.shape[0] + x_ctx.shape[0]
    n_lat = tiles_per_sample * n_samp

    def samp(i):
        return jnp.minimum(i // tiles_per_sample, n_samp)

    def lat(i):
        return jnp.minimum(i, n_lat - 1)

    def body(x_ref, c_ref, dh_ref, gr_ref, nw_ref, sc_ref, gx_ref, dsh_ref, dsc_ref, dnw_ref):
        i = pl.program_id(0)
        x = jnp.where(i < n_lat, x_ref[...], c_ref[...])
        r = _rms(x)
        xh = x * r
        nw = nw_ref[...]
        dh = dh_ref[...]
        first = jnp.logical_or(i % tiles_per_sample == 0, i >= n_lat)

        @pl.when(first)
        def _():
            dsh_ref[...] = jnp.zeros_like(dsh_ref)
            dsc_ref[...] = jnp.zeros_like(dsc_ref)

        @pl.when(i == 0)
        def _():
            dnw_ref[...] = jnp.zeros_like(dnw_ref)

        dsh_ref[...] += jnp.sum(dh, axis=0, keepdims=True)
        dsc_ref[...] += jnp.sum(dh * (xh * nw), axis=0, keepdims=True)
        du = dh * (1.0 + sc_ref[...])
        dnw_ref[...] += jnp.sum(du * xh, axis=0, keepdims=True)

        @pl.when(i < n_lat)
        def _():
            gx_ref[...] = gr_ref[...] + _rms_bwd(du * nw, xh, r)

    return pl.pallas_call(
        body, name="norm_bwd", grid=(rows // TM,),
        in_specs=[pl.BlockSpec((TM, D_MODEL), lambda i: (lat(i), 0)),
                  pl.BlockSpec((TM, D_MODEL), lambda i: (jnp.maximum(i - n_lat, 0), 0)),
                  pl.BlockSpec((TM, D_MODEL), lambda i: (i, 0)),
                  pl.BlockSpec((TM, D_MODEL), lambda i: (lat(i), 0)),
                  pl.BlockSpec((1, D_MODEL), lambda i: (0, 0)),
                  pl.BlockSpec((None, 1, D_MODEL), lambda i: (samp(i), 0, 0))],
        out_specs=(pl.BlockSpec((TM, D_MODEL), lambda i: (lat(i), 0)),
                   pl.BlockSpec((None, 1, D_MODEL), lambda i: (samp(i), 0, 0)),
                   pl.BlockSpec((None, 1, D_MODEL), lambda i: (samp(i), 0, 0)),
                   pl.BlockSpec((1, D_MODEL), lambda i: (0, 0))),
        out_shape=(SDS((n_lat * TM, D_MODEL), F32), SDS((n_samp + 1, 1, D_MODEL), F32),
                   SDS((n_samp + 1, 1, D_MODEL), F32), SDS((1, D_MODEL), F32)),
        compiler_params=_cp(("arbitrary",), 40))(x_lat, x_ctx, dhx, gx_res, norm_w, scale3)


def _in_proj_gather(hx, w_buf, ids):
    rows = hx.shape[0]
    tb = _big_rows(rows)
    n_i = rows // tb
    hrows = D_MODEL // 2

    def body(ids_ref, h_ref, w_in_hbm, px_ref, w_hbm, wv, lsem, ssem, rsem):
        j, i = pl.program_id(0), pl.program_id(1)
        x, y, c, chips = _place()
        sibling = (x, y, 1 - c)

        def half(s, which):
            return w_hbm.at[s, pl.ds(which * hrows, hrows), :]

        def over_ici(rel):
            chip = chips[rel]
            mine, theirs = half(2 * x + y, c), half(2 * chip[0] + chip[1], c)
            return (_remote(mine, mine, ssem.at[rel], rsem.at[rel], (*chip, c)),
                    _remote(theirs, theirs, ssem.at[rel], rsem.at[rel], (*chip, c)))

        def over_d2d(rel):
            s = 2 * chips[rel][0] + chips[rel][1]
            return (_remote(half(s, c), half(s, c), ssem.at[3 + rel], rsem.at[3 + rel], sibling),
                    _remote(half(s, 1 - c), half(s, 1 - c), ssem.at[3 + rel], rsem.at[3 + rel], sibling))

        first_row_tile = i == 0

        @pl.when(jnp.logical_and(j == 0, first_row_tile))
        def _():
            over_ici(0)[0].start()
            over_ici(1)[0].start()

        @pl.when(jnp.logical_and(j == 1, first_row_tile))
        def _():
            for rel in range(2):
                over_ici(rel)[1].wait_recv()
                over_d2d(rel)[0].start()
            over_ici(2)[0].start()
            over_d2d(0)[1].wait_recv()

        @pl.when(jnp.logical_and(j == 2, first_row_tile))
        def _():
            over_d2d(1)[1].wait_recv()

        @pl.when(jnp.logical_and(j == 3, first_row_tile))
        def _():
            over_ici(2)[1].wait_recv()
            passed, landing = over_d2d(2)
            passed.start()
            landing.wait_recv()

        @pl.when(first_row_tile)
        def _():
            cp = pltpu.make_async_copy(w_hbm.at[ids_ref[4 + j]], wv, lsem)
            cp.start()
            cp.wait()

        px_ref[...] = jnp.dot(h_ref[...], wv[...], preferred_element_type=F32).astype(BF)

        @pl.when(jnp.logical_and(j == N_SHARD - 1, i == n_i - 1))
        def _():
            for rel in range(3):
                over_ici(rel)[0].wait_send()
                over_d2d(rel)[0].wait_send()

    return pl.pallas_call(
        body, name="in_proj_gather", input_output_aliases={2: 1},
        grid_spec=pltpu.PrefetchScalarGridSpec(
            num_scalar_prefetch=1, grid=(N_SHARD, n_i),
            in_specs=[pl.BlockSpec((tb, D_MODEL), lambda j, i, ids_ref: (i, 0)), ANY],
            out_specs=(pl.BlockSpec((tb, IN_W), lambda j, i, ids_ref: (i, ids_ref[4 + j])), ANY),
            scratch_shapes=[pltpu.VMEM((D_MODEL, IN_W), BF), pltpu.SemaphoreType.DMA,
                            pltpu.SemaphoreType.DMA((6,)), pltpu.SemaphoreType.DMA((6,))]),
        out_shape=(SDS((rows, IN_COLS), BF), SDS(w_buf.shape, w_buf.dtype)),
        compiler_params=_cp(("arbitrary", "arbitrary"), 56))(ids, hx, w_buf)


def _gw_in(hxt, dpx_all):
    rows = dpx_all.shape[0]
    tb = _big_rows(rows)

    def body(h_ref, d_ref, o_ref):
        @pl.when(pl.program_id(1) == 0)
        def _():
            o_ref[...] = jnp.zeros_like(o_ref)

        o_ref[...] += jnp.dot(h_ref[...], d_ref[...], preferred_element_type=F32)

    return pl.pallas_call(
        body, name="gw_in", grid=(N_IN_BLK, rows // tb),
        in_specs=[pl.BlockSpec((D_MODEL, tb), lambda j, i: (0, i)),
                  pl.BlockSpec((tb, IN_BLK), lambda j, i: (i, j))],
        out_specs=pl.BlockSpec((None, D_MODEL, IN_BLK), lambda j, i: (j // BPS, 0, j % BPS)),
        out_shape=SDS((N_SHARD, D_MODEL, IN_W), F32),
        compiler_params=_cp(("arbitrary", "arbitrary"), 56))(hxt, dpx_all)


def _dhx(dpx_all, w_in_g, tile0, n_tiles, dhx, comm):
    rows = dpx_all.shape[0]
    tb = _big_rows(rows)

    def body(d_ref, w_ref, *rest):
        o_ref = rest[-1]

        @pl.when(pl.program_id(1) == 0)
        def _():
            o_ref[...] = jnp.zeros_like(o_ref)

        o_ref[...] += lax.dot_general(d_ref[...], w_ref[...], (((1,), (1,)), ((), ())), preferred_element_type=F32)

    args, in_specs, aliases = [dpx_all, w_in_g], [
        pl.BlockSpec((tb, IN_BLK), lambda i, j: (tile0 + i, j)),
        pl.BlockSpec((None, D_MODEL, IN_BLK), lambda i, j: (j // BPS, 0, j % BPS))], None
    if dhx is not None:
        args, in_specs, aliases = args + [dhx], in_specs + [ANY], {2: 0}
    (out,), got = _call(body, args, comm, name="dhx", grid=(n_tiles, N_IN_BLK), in_specs=in_specs,
                        out_specs=(pl.BlockSpec((tb, D_MODEL), lambda i, j: (tile0 + i, 0)),),
                        out_shape=(SDS((rows, D_MODEL), F32),), aliases=aliases,
                        compiler_params=_cp(("arbitrary", "arbitrary"), 56))
    return out, got


def _decays(lgv, d):
    c = RET_CHUNK
    ii = lax.broadcasted_iota(jnp.int32, (c, 1), 0).astype(F32)
    jj = lax.broadcasted_iota(jnp.int32, (1, c), 1).astype(F32)
    a_i = jnp.where(d == 0, ii, c - 1.0 - ii)
    a_j = jnp.where(d == 0, jj, c - 1.0 - jj)
    rel = a_i - a_j
    mask = jnp.where(rel >= 0, jnp.exp(lgv * jnp.maximum(rel, 0.0)), 0.0)
    qd = jnp.exp(lgv * (a_i + 1.0))
    kd = jnp.exp(lgv * (c - 1.0 - a_i))
    gc = jnp.exp(jnp.full((1, 1), lgv * c, F32))
    return a_i, rel, mask, qd, kd, gc


def _ctx_state_fwd(px, lg, n_samp, t_lat, lc):
    rb = t_lat // lc

    def body(lg_ref, k_ref, v_ref, o_ref):
        h = pl.program_id(1)
        k = k_ref[...].astype(F32) * (RET_DK ** -0.5)
        v = v_ref[...]
        pos = lax.broadcasted_iota(jnp.int32, (lc, 1), 0).astype(F32)
        o_ref[0] = _dot(k * jnp.exp(lg_ref[0, h] * (lc - 1.0 - pos)), v, 0, 0)
        o_ref[1] = _dot(k * jnp.exp(lg_ref[1, h] * pos), v, 0, 0)

    return pl.pallas_call(
        body, name="ctx_state_fwd", grid=(n_samp, RET_HEADS),
        in_specs=[SMEM,
                  pl.BlockSpec((lc, RET_DK), lambda b, h: (rb + b, C_RK // RET_DK + h)),
                  pl.BlockSpec((lc, RET_DV), lambda b, h: (rb + b, C_RV // RET_DV + h))],
        out_specs=pl.BlockSpec((None, 2, None, RET_DK, RET_DV), lambda b, h: (b, 0, h, 0, 0)),
        out_shape=SDS((n_samp, 2, RET_HEADS, RET_DK, RET_DV), F32),
        compiler_params=_cp(("parallel", "parallel")))(lg, px, px)


def _ctx_state_bwd(dpx, px, dstates, lg, n_samp, t_lat, lc):
    rb = t_lat // lc
    kspec = pl.BlockSpec((lc, RET_DK), lambda b, h: (rb + b, C_RK // RET_DK + h))
    vspec = pl.BlockSpec((lc, RET_DV), lambda b, h: (rb + b, C_RV // RET_DV + h))
    sspec = pl.BlockSpec((None, 2, None, RET_DK, RET_DV), lambda b, h: (b, 0, h, 0, 0))

    def weights(lg_ref, h):
        pos = lax.broadcasted_iota(jnp.int32, (lc, 1), 0).astype(F32)
        e_f = lc - 1.0 - pos
        return pos, e_f, jnp.exp(lg_ref[0, h] * e_f), jnp.exp(lg_ref[1, h] * pos)

    def k_body(lg_ref, dpx_hbm, k_ref, v_ref, ds_ref, dk_ref, dlg_ref):
        pos, e_f, w_f, w_b = weights(lg_ref, pl.program_id(1))
        k = k_ref[...].astype(F32) * (RET_DK ** -0.5)
        y_f = _dot(v_ref[...], ds_ref[0], 1, 1) * w_f
        y_b = _dot(v_ref[...], ds_ref[1], 1, 1) * w_b
        dk_ref[...] = ((y_f + y_b) * (RET_DK ** -0.5)).astype(BF)
        t_f = _sum_all(e_f * k * y_f)
        t_b = _sum_all(pos * k * y_b)
        sub = lax.broadcasted_iota(jnp.int32, (8, 128), 0)
        dlg_ref[...] = jnp.where(sub == 0, t_f, jnp.where(sub == 1, t_b, 0.0))

    def v_body(lg_ref, dpx_hbm, k_ref, ds_ref, dv_ref):
        _, _, w_f, w_b = weights(lg_ref, pl.program_id(1))
        k = k_ref[...].astype(F32) * (RET_DK ** -0.5)
        dv_ref[...] = (_dot(k * w_f, ds_ref[0]) + _dot(k * w_b, ds_ref[1])).astype(BF)

    dpx, dlg = pl.pallas_call(
        k_body, name="ctx_state_bwd_k", grid=(n_samp, RET_HEADS), input_output_aliases={1: 0},
        in_specs=[SMEM, ANY, kspec, vspec, sspec],
        out_specs=(kspec, pl.BlockSpec((None, None, 8, 128), lambda b, h: (b, h, 0, 0))),
        out_shape=(SDS(dpx.shape, dpx.dtype), SDS((n_samp, RET_HEADS, 8, 128), F32)),
        compiler_params=_cp(("parallel", "parallel")))(lg, dpx, px, px, dstates)
    dpx = pl.pallas_call(
        v_body, name="ctx_state_bwd_v", grid=(n_samp, RET_HEADS), input_output_aliases={1: 0},
        in_specs=[SMEM, ANY, kspec, sspec], out_specs=vspec, out_shape=SDS(dpx.shape, dpx.dtype),
        compiler_params=_cp(("parallel", "parallel")))(lg, dpx, px, dstates)
    return dpx, dlg


def _zero_ctx_tail(dpx, t_lat):
    wb = 512
    n_ctx = (dpx.shape[0] - t_lat) // TM

    def body(dpx_hbm, o_ref):
        o_ref[...] = jnp.zeros_like(o_ref)

    return pl.pallas_call(
        body, name="zero_ctx_tail", grid=(n_ctx, (IN_COLS - KV_COLS) // wb), input_output_aliases={0: 0},
        in_specs=[ANY], out_specs=pl.BlockSpec((TM, wb), lambda i, j: (t_lat // TM + i, KV_COLS // wb + j)),
        out_shape=SDS(dpx.shape, dpx.dtype),
        compiler_params=_cp(("parallel", "parallel")))(dpx)


def _ret_specs(row_f, row_b):
    c = RET_CHUNK
    wq = RET_HEADS * RET_DK // 2
    wv = RET_HEADS * RET_DV // 2
    specs = []
    for row in (row_f, row_b):
        specs += [pl.BlockSpec((c, wq), lambda b, n, row=row: (row(b, n), C_RQ // wq)),
                  pl.BlockSpec((c, wq), lambda b, n, row=row: (row(b, n), C_RQ // wq + 1)),
                  pl.BlockSpec((c, 2 * wq), lambda b, n, row=row: (row(b, n), C_RK // (2 * wq))),
                  pl.BlockSpec((c, wv), lambda b, n, row=row: (row(b, n), C_RV // wv)),
                  pl.BlockSpec((c, wv), lambda b, n, row=row: (row(b, n), C_RV // wv + 1))]
    return specs


def _ret_head(refs, h):
    q0, q1, k_ref, v0, v1 = refs
    hh = h % 2
    q = (q0, q1)[h // 2][:, hh * RET_DK:(hh + 1) * RET_DK].astype(F32)
    k = k_ref[:, h * RET_DK:(h + 1) * RET_DK].astype(F32) * (RET_DK ** -0.5)
    v = (v0, v1)[h // 2][:, hh * RET_DV:(hh + 1) * RET_DV]
    return q, k, v


def _ret_fwd(px, states0, lg, n_samp, seq, comm):
    c = RET_CHUNK
    nc = seq // c
    t_lat = n_samp * seq
    wo = RET_HEADS * RET_DV

    def row_f(b, n):
        return b * nc + n

    def row_b(b, n):
        return b * nc + nc - 1 - n

    def body(lg_ref, *refs):
        ins, (s0_ref, of_ref, ob_ref, st_ref, s_s) = refs[:10], refs[10:]

        @pl.when(pl.program_id(1) == 0)
        def _():
            s_s[...] = s0_ref[...]

        for d, o_ref in ((0, of_ref), (1, ob_ref)):
            for h in range(RET_HEADS):
                _, _, mask, qd, kd, gc = _decays(lg_ref[d, h], d)
                q, k, v = _ret_head(ins[5 * d:5 * d + 5], h)
                s = s_s[d, h]
                st_ref[h, d] = s.astype(BF)
                sc = _dot(q, k, 1, 1) * mask
                o_ref[:, h * RET_DV:(h + 1) * RET_DV] = (_dot(sc, v) + _dot(q * qd, s)).astype(BF)
                s_s[d, h] = s * gc + _dot(k * kd, v, 0, 0)

    return _call(
        body, [lg] + [px] * 10 + [states0], comm, name="ret_fwd", grid=(n_samp, nc),
        in_specs=[SMEM] + _ret_specs(row_f, row_b) + [
            pl.BlockSpec((None, 2, RET_HEADS, RET_DK, RET_DV), lambda b, n: (b, 0, 0, 0, 0))],
        out_specs=(pl.BlockSpec((c, wo), lambda b, n: (row_f(b, n), 0)),
                   pl.BlockSpec((c, wo), lambda b, n: (row_b(b, n), 0)),
                   pl.BlockSpec((None, RET_HEADS, 2, None, RET_DK, RET_DV), lambda b, n: (b, 0, 0, n, 0, 0))),
        out_shape=(SDS((t_lat, wo), BF), SDS((t_lat, wo), BF),
                   SDS((n_samp, RET_HEADS, 2, nc, RET_DK, RET_DV), BF)),
        scratch_shapes=[pltpu.VMEM((2, RET_HEADS, RET_DK, RET_DV), F32)],
        compiler_params=_cp(("arbitrary", "arbitrary"), 48))


def _ret_bwd(dpx, px, do, saved, lg, n_samp, seq, comm):
    c = RET_CHUNK
    nc = seq // c
    assert nc % 2 == 0
    wq, wo = RET_HEADS * RET_DK, RET_HEADS * RET_DV

    def row_f(b, n):
        return b * nc + nc - 1 - n

    def row_b(b, n):
        return b * nc + n

    def body(lg_ref, *refs):
        ins = refs[:10]
        (dof_ref, dob_ref, st_ref, dpx_in, dpx_hbm, ds0_ref, dlg_ref,
         ds_s, acc_s, sq_s, sk_s, sv_s, sems) = refs[10:]
        b, n = pl.program_id(0), pl.program_id(1)
        second = n >= nc // 2
        chunks = (nc - 1 - n, n)

        def parked(ch):
            return pl.ds(pl.multiple_of(ch * c, c), c)

        def flush():
            cps = []
            for d, ch in enumerate(chunks):
                rows = pl.ds(pl.multiple_of((b * nc + ch) * c, c), c)
                cps += [pltpu.make_async_copy(sq_s.at[parked(ch), :], dpx_hbm.at[rows, pl.ds(C_RQ, wq)], sems.at[3 * d]),
                        pltpu.make_async_copy(sk_s.at[parked(ch), :], dpx_hbm.at[rows, pl.ds(C_RK, wq)],
                                              sems.at[3 * d + 1]),
                        pltpu.make_async_copy(sv_s.at[parked(ch), :], dpx_hbm.at[rows, pl.ds(C_RV, wo)],
                                              sems.at[3 * d + 2])]
            return cps

        @pl.when(jnp.logical_or(n > nc // 2, jnp.logical_and(n == 0, b > 0)))
        def _():
            for cp in flush():
                cp.wait()

        @pl.when(n == 0)
        def _():
            ds_s[...] = jnp.zeros_like(ds_s)
            acc_s[...] = jnp.zeros_like(acc_s)

        def chains(first_visit):
            for d, do_ref in enumerate((dof_ref, dob_ref)):
                rows = parked(chunks[d])
                for h in range(RET_HEADS):
                    a_i, rel, mask, qd, kd, gc = _decays(lg_ref[d, h], d)
                    q, k, v = _ret_head(ins[5 * d:5 * d + 5], h)
                    qb, kb, vb = q.astype(BF), k.astype(BF), v.astype(BF)
                    cq, cv = slice(h * RET_DK, (h + 1) * RET_DK), slice(h * RET_DV, (h + 1) * RET_DV)
                    dob = do_ref[:, cv].astype(BF)
                    sb = st_ref[h, d]
                    ds = ds_s[d, h]
                    dsb = ds.astype(BF)
                    raw = _dot(qb, kb, 1, 1)
                    sc = raw * mask
                    dsc = _dot(dob, vb, 1, 1) * mask
                    dscb = dsc.astype(BF)
                    x = _dot(dob, sb, 1, 1)
                    y = _dot(vb, dsb, 1, 1)
                    qq = q * qd
                    kk = k * kd
                    dq = _dot(dscb, kb) + x * qd
                    dk = _dot(dscb, qb, 0, 0) + y * kd
                    dv = _dot(sc, dob, 0, 0) + _dot(kk, dsb)
                    if first_visit:
                        sq_s[rows, cq] = dq.astype(BF)
                        sk_s[rows, cq] = dk.astype(BF)
                        sv_s[rows, cv] = dv.astype(BF)
                    else:
                        sq_s[rows, cq] = (sq_s[rows, cq].astype(F32) + dq).astype(BF)
                        sk_s[rows, cq] = ((sk_s[rows, cq].astype(F32) + dk) * (RET_DK ** -0.5)).astype(BF)
                        sv_s[rows, cv] = (sv_s[rows, cv].astype(F32) + dv).astype(BF)
                    t = (_sum_all(dsc * raw * rel) + _sum_all((a_i + 1.0) * qq * x)
                         + _sum_all((c - 1.0 - a_i) * kk * y) + c * gc * _sum_all(ds * sb.astype(F32)))
                    acc_s[4 * d + h:4 * d + h + 1, :] += t
                    ds_s[d, h] = ds * gc + _dot(qq, dob, 0, 0)

        @pl.when(jnp.logical_not(second))
        def _():
            chains(True)

        @pl.when(second)
        def _():
            chains(False)
            for cp in flush():
                cp.start()

        @pl.when(n == nc - 1)
        def _():
            ds0_ref[...] = ds_s[...]
            dlg_ref[...] = acc_s[...]

        @pl.when(jnp.logical_and(b == n_samp - 1, n == nc - 1))
        def _():
            for cp in flush():
                cp.wait()

    do_spec_f = pl.BlockSpec((c, wo), lambda b, n: (row_f(b, n), 0))
    do_spec_b = pl.BlockSpec((c, wo), lambda b, n: (row_b(b, n), 0))
    return _call(
        body, [lg] + [px] * 10 + [do, do, saved, dpx], comm, name="ret_bwd", grid=(n_samp, nc), aliases={14: 0},
        in_specs=[SMEM] + _ret_specs(row_f, row_b) + [
            do_spec_f, do_spec_b,
            pl.BlockSpec((None, RET_HEADS, 2, None, RET_DK, RET_DV), lambda b, n: (b, 0, 0, nc - 1 - n, 0, 0)),
            ANY],
        out_specs=(ANY,
                   pl.BlockSpec((None, 2, RET_HEADS, RET_DK, RET_DV), lambda b, n: (b, 0, 0, 0, 0)),
                   pl.BlockSpec((None, 8, 128), lambda b, n: (b, 0, 0))),
        out_shape=(SDS(dpx.shape, dpx.dtype),
                   SDS((n_samp, 2, RET_HEADS, RET_DK, RET_DV), F32), SDS((n_samp, 8, 128), F32)),
        scratch_shapes=[pltpu.VMEM((2, RET_HEADS, RET_DK, RET_DV), F32), pltpu.VMEM((8, 128), F32),
                        pltpu.VMEM((seq, wq), BF), pltpu.VMEM((seq, wq), BF), pltpu.VMEM((seq, wo), BF),
                        pltpu.SemaphoreType.DMA((6,))],
        compiler_params=_cp(("arbitrary", "arbitrary"), 60))


def _norm_rope(x, w, cos, sin):
    xn = x * _rms(x) * w
    return xn * cos + _swap_pairs(xn) * sin


def _norm_rope_bwd(dy, x, w, cos, sin):
    dxn = dy * cos + _swap_pairs(dy * sin)
    r = _rms(x)
    xh = x * r
    return _rms_bwd(dxn * w, xh, r), jnp.sum(dxn * xh, axis=0, keepdims=True)


def _att_prep_q(px, cos_all, sin_all, qnw, t_lat):
    hd = ATT_HEAD_DIM
    wblk = ATT_REP * hd

    def body(x_ref, cos_ref, sin_ref, w_ref, o_ref):
        for r in range(ATT_REP):
            cols = slice(r * hd, (r + 1) * hd)
            qr = _norm_rope(x_ref[:, cols].astype(F32), w_ref[...], cos_ref[...], sin_ref[...])
            o_ref[:, cols] = (qr * (hd ** -0.5)).astype(BF)

    return pl.pallas_call(
        body, name="att_prep_q", grid=(t_lat // TM, ATT_KV_HEADS),
        in_specs=[pl.BlockSpec((TM, wblk), lambda i, g: (i, C_AQ // wblk + g)),
                  pl.BlockSpec((TM, hd), lambda i, g: (i, 0)),
                  pl.BlockSpec((TM, hd), lambda i, g: (i, 0)),
                  pl.BlockSpec((1, hd), lambda i, g: (0, 0))],
        out_specs=pl.BlockSpec((TM, wblk), lambda i, g: (i, g)),
        out_shape=SDS((t_lat, ATT_HEADS * hd), BF),
        compiler_params=_cp(("parallel", "parallel")))(px, cos_all, sin_all, qnw)


def _att_prep_kv(px, cos_all, sin_all, knw):
    rows = px.shape[0]
    hd = ATT_HEAD_DIM
    kvw = ATT_KV_HEADS * hd

    def body(x_ref, cos_ref, sin_ref, w_ref, k_ref, v_ref):
        for g in range(ATT_KV_HEADS):
            cols = slice(g * hd, (g + 1) * hd)
            k_ref[:, cols] = _norm_rope(x_ref[:, cols].astype(F32), w_ref[...], cos_ref[...],
                                        sin_ref[...]).astype(BF)
            v_ref[:, 2 * g * hd:(2 * g + 1) * hd] = x_ref[:, kvw + g * hd:kvw + (g + 1) * hd].astype(BF)
            v_ref[:, (2 * g + 1) * hd:(2 * g + 2) * hd] = jnp.ones((TM, hd), BF)

    return pl.pallas_call(
        body, name="att_prep_kv", grid=(rows // TM,),
        in_specs=[pl.BlockSpec((TM, 2 * kvw), lambda i: (i, C_AK // (2 * kvw))),
                  pl.BlockSpec((TM, hd), lambda i: (i, 0)),
                  pl.BlockSpec((TM, hd), lambda i: (i, 0)),
                  pl.BlockSpec((1, hd), lambda i: (0, 0))],
        out_specs=(pl.BlockSpec((TM, kvw), lambda i: (i, 0)), pl.BlockSpec((TM, 2 * kvw), lambda i: (i, 0))),
        out_shape=(SDS((rows, kvw), BF), SDS((rows, 2 * kvw), BF)),
        compiler_params=_cp(("parallel",)))(px, cos_all, sin_all, knw)


def _att_kv_bwd(dpx, dkl, dkc, dvl, dvc, px, cos_all, sin_all, knw):
    rows = px.shape[0]
    hd = ATT_HEAD_DIM
    kvw = ATT_KV_HEADS * hd
    n_lat = dkl.shape[0] // TM
    assert dkc.shape[0] == TM

    def body(dpx_hbm, dkl_ref, dkc_ref, dvl_ref, dvc_ref, x_ref, cos_ref, sin_ref, w_ref, o_ref, gw_ref):
        i = pl.program_id(0)

        @pl.when(i == 0)
        def _():
            gw_ref[...] = jnp.zeros_like(gw_ref)

        is_lat = i < n_lat
        dk = jnp.where(is_lat, dkl_ref[...], dkc_ref[...])
        dv = jnp.where(is_lat, dvl_ref[...], dvc_ref[...])
        for g in range(ATT_KV_HEADS):
            cols = slice(g * hd, (g + 1) * hd)
            dx, gw = _norm_rope_bwd(dk[:, cols], x_ref[:, cols].astype(F32), w_ref[...], cos_ref[...], sin_ref[...])
            o_ref[:, cols] = dx.astype(BF)
            gw_ref[...] += gw
        o_ref[:, kvw:] = dv.astype(BF)

    lat = pl.BlockSpec((TM, kvw), lambda i: (jnp.minimum(i, n_lat - 1), 0))
    ctx = pl.BlockSpec((TM, kvw), lambda i: (0, 0))
    kvcol = pl.BlockSpec((TM, 2 * kvw), lambda i: (i, C_AK // (2 * kvw)))
    return pl.pallas_call(
        body, name="att_kv_bwd", grid=(rows // TM,), input_output_aliases={0: 0},
        in_specs=[ANY, lat, ctx, lat, ctx, kvcol,
                  pl.BlockSpec((TM, hd), lambda i: (i, 0)),
                  pl.BlockSpec((TM, hd), lambda i: (i, 0)),
                  pl.BlockSpec((1, hd), lambda i: (0, 0))],
        out_specs=(kvcol, pl.BlockSpec((1, hd), lambda i: (0, 0))),
        out_shape=(SDS(dpx.shape, dpx.dtype), SDS((1, hd), F32)),
        compiler_params=_cp(("arbitrary",)))(dpx, dkl, dkc, dvl, dvc, px, cos_all, sin_all, knw)


def _stack_heads(ref_or_val):
    hd = ATT_HEAD_DIM
    return jnp.concatenate([ref_or_val[:, r * hd:(r + 1) * hd] for r in range(ATT_REP)], axis=0)


def _att_scores(q, kl, kc):
    sl = _dot(q, kl, 1, 1)
    sc = _dot(q, kc, 1, 1)
    m = jnp.maximum(jnp.max(sl, axis=-1, keepdims=True), jnp.max(sc, axis=-1, keepdims=True))
    return jnp.exp(sl - m), jnp.exp(sc - m), m


def _att_fwd(qn, kn, vn, n_samp, seq, lc):
    hd = ATT_HEAD_DIM
    tq = ATT_TQ
    nq = seq // tq
    wblk = ATT_REP * hd
    cb = n_samp * seq // lc
    t_lat = n_samp * seq

    def body(q_ref, kl_ref, kc_ref, vl_ref, vc_ref, o_ref, lse_ref):
        lane = lax.broadcasted_iota(jnp.int32, (tq, hd), 1)
        lse = jnp.zeros((tq, hd), F32)
        for r in range(ATT_REP):
            cols = slice(r * hd, (r + 1) * hd)
            el, ec, m = _att_scores(q_ref[:, cols], kl_ref[...], kc_ref[...])
            pv = _dot(el, vl_ref[...]) + _dot(ec, vc_ref[...])
            denom = pv[:, hd:hd + 1]
            o_ref[:, cols] = (pv[:, :hd] / denom).astype(BF)
            lse = jnp.where(lane == r, m + jnp.log(denom), lse)
        lse_ref[...] = lse

    return pl.pallas_call(
        body, name="att_fwd", grid=(n_samp, ATT_KV_HEADS, nq),
        in_specs=[pl.BlockSpec((tq, wblk), lambda b, g, i: (b * nq + i, g)),
                  pl.BlockSpec((seq, hd), lambda b, g, i: (b, g)),
                  pl.BlockSpec((lc, hd), lambda b, g, i: (cb + b, g)),
                  pl.BlockSpec((seq, 2 * hd), lambda b, g, i: (b, g)),
                  pl.BlockSpec((lc, 2 * hd), lambda b, g, i: (cb + b, g))],
        out_specs=(pl.BlockSpec((tq, wblk), lambda b, g, i: (b * nq + i, g)),
                   pl.BlockSpec((tq, hd), lambda b, g, i: (b * nq + i, g))),
        out_shape=(SDS((t_lat, ATT_HEADS * hd), BF), SDS((t_lat, ATT_KV_HEADS * hd), F32)),
        compiler_params=_cp(("parallel", "parallel", "parallel"), 48))(qn, kn, kn, vn, vn)


def _att_bwd(dpx, qn, kn, vn, px, o_att, lse, do_att, cos_all, sin_all, qnw, n_samp, seq, lc, comm):
    hd = ATT_HEAD_DIM
    tq = ATT_TQ
    nq = seq // tq
    wblk = ATT_REP * hd
    cb = n_samp * seq // lc
    t_lat = n_samp * seq
    kvw = ATT_KV_HEADS * hd
    scale = hd ** -0.5

    def body(dpx_hbm, q_ref, kl_ref, kc_ref, vl_ref, vc_ref, o_ref, do_ref, x_ref, cos_ref, sin_ref, w_ref,
             lse_ref, dq_ref, dkl_ref, dkc_ref, dvl_ref, dvc_ref, gw_ref, akl, akc, avl, avc, aw):
        i = pl.program_id(2)

        @pl.when(i == 0)
        def _():
            akl[...] = jnp.zeros_like(akl)
            akc[...] = jnp.zeros_like(akc)
            avl[...] = jnp.zeros_like(avl)
            avc[...] = jnp.zeros_like(avc)
            aw[...] = jnp.zeros_like(aw)

        dobs, pls, pcs, dsls, dscs = [], [], [], [], []
        for r in range(ATT_REP):
            cols = slice(r * hd, (r + 1) * hd)
            dob = do_ref[:, cols]
            delta = jnp.sum(dob.astype(F32) * o_ref[:, cols].astype(F32), axis=-1, keepdims=True)
            lse = lse_ref[:, r:r + 1]
            p_l = jnp.exp(_dot(q_ref[:, cols], kl_ref[...], 1, 1) - lse).astype(BF)
            p_c = jnp.exp(_dot(q_ref[:, cols], kc_ref[...], 1, 1) - lse).astype(BF)
            ds_l = (p_l * (_dot(dob, vl_ref[...], 1, 1) - delta)).astype(BF)
            ds_c = (p_c * (_dot(dob, vc_ref[...], 1, 1) - delta)).astype(BF)
            dq = (_dot(ds_l, kl_ref[...]) + _dot(ds_c, kc_ref[...])) * scale
            dx, gw = _norm_rope_bwd(dq, x_ref[:, cols].astype(F32), w_ref[...], cos_ref[...], sin_ref[...])
            dq_ref[:, cols] = dx.astype(BF)
            aw[...] += gw
            dobs.append(dob)
            pls.append(p_l)
            pcs.append(p_c)
            dsls.append(ds_l)
            dscs.append(ds_c)
        do4 = jnp.concatenate(dobs, axis=0)
        q4 = _stack_heads(q_ref)
        avl[...] += _dot(jnp.concatenate(pls, axis=0), do4, 0, 0)
        avc[...] += _dot(jnp.concatenate(pcs, axis=0), do4, 0, 0)
        akl[...] += _dot(jnp.concatenate(dsls, axis=0), q4, 0, 0)
        akc[...] += _dot(jnp.concatenate(dscs, axis=0), q4, 0, 0)

        @pl.when(i == nq - 1)
        def _():
            dkl_ref[...] = akl[...]
            dkc_ref[...] = akc[...]
            dvl_ref[...] = avl[...]
            dvc_ref[...] = avc[...]
            gw_ref[...] = aw[...]

    return _call(
        body, [dpx, qn, kn, kn, vn, vn, o_att, do_att, px, cos_all, sin_all, qnw, lse], comm,
        name="att_bwd", grid=(n_samp, ATT_KV_HEADS, nq), aliases={0: 0},
        in_specs=[ANY,
                  pl.BlockSpec((tq, wblk), lambda b, g, i: (b * nq + i, g)),
                  pl.BlockSpec((seq, hd), lambda b, g, i: (b, g)),
                  pl.BlockSpec((lc, hd), lambda b, g, i: (cb + b, g)),
                  pl.BlockSpec((seq, hd), lambda b, g, i: (b, 2 * g)),
                  pl.BlockSpec((lc, hd), lambda b, g, i: (cb + b, 2 * g)),
                  pl.BlockSpec((tq, wblk), lambda b, g, i: (b * nq + i, g)),
                  pl.BlockSpec((tq, wblk), lambda b, g, i: (b * nq + i, g)),
                  pl.BlockSpec((tq, wblk), lambda b, g, i: (b * nq + i, C_AQ // wblk + g)),
                  pl.BlockSpec((tq, hd), lambda b, g, i: (b * nq + i, 0)),
                  pl.BlockSpec((tq, hd), lambda b, g, i: (b * nq + i, 0)),
                  pl.BlockSpec((1, hd), lambda b, g, i: (0, 0)),
                  pl.BlockSpec((tq, hd), lambda b, g, i: (b * nq + i, g))],
        out_specs=(pl.BlockSpec((tq, wblk), lambda b, g, i: (b * nq + i, C_AQ // wblk + g)),
                   pl.BlockSpec((seq, hd), lambda b, g, i: (b, g)),
                   pl.BlockSpec((lc, hd), lambda b, g, i: (b, g)),
                   pl.BlockSpec((seq, hd), lambda b, g, i: (b, g)),
                   pl.BlockSpec((lc, hd), lambda b, g, i: (b, g)),
                   pl.BlockSpec((None, None, 1, hd), lambda b, g, i: (b, g, 0, 0))),
        out_shape=(SDS(dpx.shape, dpx.dtype),
                   SDS((t_lat, kvw), F32), SDS((n_samp * lc, kvw), F32),
                   SDS((t_lat, kvw), F32), SDS((n_samp * lc, kvw), F32),
                   SDS((n_samp, ATT_KV_HEADS, 1, hd), F32)),
        scratch_shapes=[pltpu.VMEM((seq, hd), F32), pltpu.VMEM((lc, hd), F32),
                        pltpu.VMEM((seq, hd), F32), pltpu.VMEM((lc, hd), F32), pltpu.VMEM((1, hd), F32)],
        compiler_params=_cp(("arbitrary", "arbitrary", "arbitrary"), 56))


def _merge(x_lat, target, o_f, o_b, o_att, px, gate3, w_o_ret, w_o_att, w_out, tiles_per_sample):
    t_lat = x_lat.shape[0]
    tm = 256
    n_t = t_lat // tm
    per = tiles_per_sample * (TM // tm)
    d = D_MODEL
    rv = RET_HEADS * RET_DV
    n_samp = gate3.shape[0] - 1

    half = d // 2
    n_px = 10

    def body(x_ref, t_ref, of_ref, ob_ref, oa_ref, *rest):
        pxs, rest = rest[:n_px], rest[n_px:]
        (gt_ref, wor_ref, woa_ref, wout_ref,
         gx_ref, dor_ref, doa_ref, dpx_hbm, loss_ref, dgt_ref, gwor_hbm, gwoa_hbm, gwout_hbm,
         aor, aoa, aout, drg_ref, dtail_ref, sems) = rest
        i = pl.program_id(0)

        def copies(step):
            rows = pl.ds(pl.multiple_of(step * tm, tm), tm)
            return (pltpu.make_async_copy(drg_ref, dpx_hbm.at[rows, pl.ds(C_RG, rv)], sems.at[0]),
                    pltpu.make_async_copy(dtail_ref, dpx_hbm.at[rows, pl.ds(C_AG, 3 * d)], sems.at[1]))

        @pl.when(i == 0)
        def _():
            aor[...] = jnp.zeros_like(aor)
            aoa[...] = jnp.zeros_like(aoa)
            aout[...] = jnp.zeros_like(aout)
            loss_ref[...] = jnp.zeros_like(loss_ref)

        @pl.when(i % per == 0)
        def _():
            dgt_ref[...] = jnp.zeros_like(dgt_ref)

        def cat(refs):
            return jnp.concatenate([r[...] for r in refs], axis=1).astype(F32)

        def ret_head(h):
            cols = slice(h * RET_DV, (h + 1) * RET_DV)
            o = of_ref[:, cols].astype(F32) + ob_ref[:, cols].astype(F32)
            r = _rms(o)
            g = pxs[h][...].astype(F32)
            return o * r, r, g, _sigmoid(g)

        def att_half(k):
            o = oa_ref[:, k * half:(k + 1) * half].astype(F32)
            g = pxs[4 + k][...].astype(F32)
            return o, g, _sigmoid(g)

        yrs = []
        for h in range(RET_HEADS):
            on, _, g, sg = ret_head(h)
            yrs.append((on * (g * sg)).astype(BF))
        yr = jnp.concatenate(yrs, axis=1)
        yas = []
        for k in range(2):
            o, g, sg = att_half(k)
            yas.append((o * (g * sg)).astype(BF))
        ya = jnp.concatenate(yas, axis=1)

        a = jnp.dot(yr, wor_ref[...], preferred_element_type=F32)
        b = jnp.dot(ya, woa_ref[...], preferred_element_type=F32)
        sr = _sigmoid(cat(pxs[6:8]))
        sa = _sigmoid(cat(pxs[8:10]))
        yb = (sr * a + sa * b).astype(BF)
        out = jnp.dot(yb, wout_ref[...], preferred_element_type=F32)
        gate = gt_ref[...]
        err = x_ref[...] + gate * out - t_ref[...]
        loss_ref[...] += 0.5 * _sum_all(err * err) * (1.0 / d)
        dy_tok = err * (1.0 / d)
        gx_ref[...] = dy_tok
        dgt_ref[...] += jnp.sum(dy_tok * out, axis=0, keepdims=True)
        dout = (dy_tok * gate).astype(BF)
        aout[...] += _dot(yb, dout, 0, 0)
        dyy = _dot(dout, wout_ref[...], 1, 1)
        da = (dyy * sr).astype(BF)
        db = (dyy * sa).astype(BF)
        aor[...] += _dot(yr, da, 0, 0)
        aoa[...] += _dot(ya, db, 0, 0)
        dyr = _dot(da, wor_ref[...], 1, 1)
        dya = _dot(db, woa_ref[...], 1, 1)

        @pl.when(i > 0)
        def _():
            for cp in copies(i - 1):
                cp.wait()

        dtail_ref[:, d:2 * d] = (dyy * a * (sr * (1.0 - sr))).astype(BF)
        dtail_ref[:, 2 * d:] = (dyy * b * (sa * (1.0 - sa))).astype(BF)
        for h in range(RET_HEADS):
            cols = slice(h * RET_DV, (h + 1) * RET_DV)
            on, r, g, sg = ret_head(h)
            dy = dyr[:, cols]
            drg_ref[:, cols] = (dy * on * (sg * (1.0 + g * (1.0 - sg)))).astype(BF)
            dor_ref[:, cols] = _rms_bwd(dy * (g * sg), on, r).astype(BF)
        for k in range(2):
            cols = slice(k * half, (k + 1) * half)
            o, g, sg = att_half(k)
            dy = dya[:, cols]
            dtail_ref[:, cols] = (dy * o * (sg * (1.0 + g * (1.0 - sg)))).astype(BF)
            doa_ref[:, cols] = (dy * (g * sg)).astype(BF)
        for cp in copies(i):
            cp.start()

        @pl.when(i == n_t - 1)
        def _():
            for cp in copies(i):
                cp.wait()
            pltpu.sync_copy(aor, gwor_hbm)
            pltpu.sync_copy(aoa, gwoa_hbm)
            pltpu.sync_copy(aout, gwout_hbm)

    def px_blk(col):
        return pl.BlockSpec((tm, half), lambda i: (i, col // half))

    def resident(shape):
        return pl.BlockSpec(shape, lambda i: (0, 0), pipeline_mode=pl.Buffered(1))

    px_cols = ([C_RG + k * half for k in range(4)] + [C_AG, C_AG + half]
               + [C_MR, C_MR + half, C_MA, C_MA + half])
    return pl.pallas_call(
        body, name="merge", grid=(n_t,),
        in_specs=[pl.BlockSpec((tm, d), lambda i: (i, 0)),
                  pl.BlockSpec((tm, d), lambda i: (i, 0)),
                  pl.BlockSpec((tm, rv), lambda i: (i, 0)),
                  pl.BlockSpec((tm, rv), lambda i: (i, 0)),
                  pl.BlockSpec((tm, d), lambda i: (i, 0))]
        + [px_blk(col) for col in px_cols]
        + [pl.BlockSpec((None, 1, d), lambda i: (i // per, 0, 0)),
           resident((rv, d)), resident((d, d)), resident((d, d))],
        out_specs=(pl.BlockSpec((tm, d), lambda i: (i, 0)),
                   pl.BlockSpec((tm, rv), lambda i: (i, 0)),
                   pl.BlockSpec((tm, d), lambda i: (i, 0)),
                   ANY,
                   pl.BlockSpec((8, 128), lambda i: (0, 0)),
                   pl.BlockSpec((None, 1, d), lambda i: (i // per, 0, 0)),
                   ANY, ANY, ANY),
        out_shape=(SDS((t_lat, d), F32), SDS((t_lat, rv), BF), SDS((t_lat, d), BF),
                   SDS((px.shape[0], IN_COLS), BF),
                   SDS((8, 128), F32), SDS((n_samp, 1, d), F32),
                   SDS((rv, d), F32), SDS((d, d), F32), SDS((d, d), F32)),
        scratch_shapes=[pltpu.VMEM((rv, d), F32), pltpu.VMEM((d, d), F32), pltpu.VMEM((d, d), F32),
                        pltpu.VMEM((tm, rv), BF), pltpu.VMEM((tm, 3 * d), BF), pltpu.SemaphoreType.DMA((2,))],
        compiler_params=_cp(("arbitrary",), 56))(
            x_lat, target, o_f, o_b, o_att, *([px] * n_px), gate3, w_o_ret, w_o_att, w_out)


def _place():
    x, y, c = lax.axis_index("x"), lax.axis_index("y"), lax.axis_index("c")
    chips = [(1 - x, y), (x, 1 - y), (1 - x, 1 - y)]
    return x, y, c, chips


def _remote(src, dst, send_sem, recv_sem, to):
    return pltpu.make_async_remote_copy(src_ref=src, dst_ref=dst, send_sem=send_sem, recv_sem=recv_sem,
                                        device_id=to, device_id_type=MESH)


def _place_ids():
    x, y, c = lax.axis_index("x"), lax.axis_index("y"), lax.axis_index("c")
    me = 2 * x + y
    return jnp.stack([x, y, c, me, me, 2 * (1 - x) + y, 2 * x + 1 - y, 2 * (1 - x) + 1 - y]).astype(jnp.int32)


def _ag_comm(bufs):
    n, m = len(bufs), 3

    def half(ref, s, which):
        h = ref.shape[1] // 2
        return ref.at[s, pl.ds(which * h, h), :]

    def ici(ins, outs, ssem, rsem, base):
        x, y, c, chips = _place()
        sends, recvs = [], []
        for a in range(n):
            for j in range(m):
                k, chip = base + a * m + j, chips[j]
                mine, theirs = half(outs[a], 2 * x + y, c), half(outs[a], 2 * chip[0] + chip[1], c)
                sends.append(_remote(mine, mine, ssem.at[k], rsem.at[k], (*chip, c)))
                recvs.append(_remote(theirs, theirs, ssem.at[k], rsem.at[k], (*chip, c)))
        return sends, recvs

    def d2d(ins, outs, ssem, rsem, base):
        x, y, c, chips = _place()
        sends, recvs = [], []
        for a in range(n):
            for j in range(m):
                k, s = base + (n + a) * m + j, 2 * chips[j][0] + chips[j][1]
                sends.append(_remote(half(outs[a], s, c), half(outs[a], s, c), ssem.at[k], rsem.at[k], (x, y, 1 - c)))
                recvs.append(_remote(half(outs[a], s, 1 - c), half(outs[a], s, 1 - c), ssem.at[k], rsem.at[k],
                                     (x, y, 1 - c)))
        return sends, recvs

    return _Comm("all_gather", tuple(bufs), tuple(SDS(b.shape, b.dtype) for b in bufs), {a: a for a in range(n)},
                 2 * n * m, (ici, d2d))


def _swap_comm(grads):
    n = len(grads)

    def phase(ins, outs, ssem, rsem, base):
        x, y, c, _ = _place()
        sends = []
        for a in range(n):
            h = ins[a].shape[1] // 2
            sends.append(_remote(ins[a].at[:, pl.ds((1 - c) * h, h), :], outs[a], ssem.at[base + a],
                                 rsem.at[base + a], (x, y, 1 - c)))
        return sends, sends

    return _Comm("swap_halves", tuple(grads),
                 tuple(SDS((g.shape[0], g.shape[1] // 2, g.shape[2]), g.dtype) for g in grads), {}, n, (phase,))


def _exchange_comm(parts):
    n = len(parts)

    def phase(ins, outs, ssem, rsem, base):
        x, y, c, chips = _place()
        sends = []
        for a in range(n):
            for j, chip in enumerate(chips):
                k = base + 3 * a + j
                sends.append(_remote(ins[a].at[2 * chip[0] + chip[1]], outs[a].at[j], ssem.at[k], rsem.at[k],
                                     (*chip, c)))
        return sends, sends

    return _Comm("exchange_shards", tuple(parts), tuple(SDS((3,) + p.shape[1:], p.dtype) for p in parts), {}, 3 * n,
                 (phase,))


def _join_comm(bufs):
    n = len(bufs)

    def phase(ins, outs, ssem, rsem, base):
        x, y, c, _ = _place()
        sends, recvs = [], []
        for a in range(n):
            h = outs[a].shape[0] // 2
            mine, other = outs[a].at[pl.ds(c * h, h), :], outs[a].at[pl.ds((1 - c) * h, h), :]
            sends.append(_remote(mine, mine, ssem.at[base + a], rsem.at[base + a], (x, y, 1 - c)))
            recvs.append(_remote(other, other, ssem.at[base + a], rsem.at[base + a], (x, y, 1 - c)))
        return sends, recvs

    return _Comm("join_halves", tuple(bufs), tuple(SDS(b.shape, b.dtype) for b in bufs), {a: a for a in range(n)},
                 n, (phase,))


def _cast_place(w, ids):
    rows, cols = w.shape
    tr = min(rows, 256)

    def body(ids_ref, w_ref, o_ref):
        o_ref[...] = w_ref[...].astype(BF)

    return pl.pallas_call(
        body, name="cast_place",
        grid_spec=pltpu.PrefetchScalarGridSpec(
            num_scalar_prefetch=1, grid=(rows // tr,),
            in_specs=[pl.BlockSpec((tr, cols), lambda i, ids_ref: (i, 0))],
            out_specs=pl.BlockSpec((None, tr, cols), lambda i, ids_ref: (ids_ref[3], i, 0))),
        out_shape=SDS((N_SHARD, rows, cols), BF),
        compiler_params=_cp(("parallel",), 40))(ids, w)


def _chip_sum(g, p, ids):
    n_s, rows, cols = g.shape
    h = rows // 2
    tr = min(h, 256)
    nb = h // tr

    def body(ids_ref, g_ref, p_ref, o_ref, o16_ref):
        t = g_ref[...] + p_ref[...]
        o_ref[...] = t
        o16_ref[...] = t.astype(BF)

    out_spec = pl.BlockSpec((None, tr, cols), lambda s, i, ids_ref: (s, i, 0))
    return pl.pallas_call(
        body, name="chip_sum",
        grid_spec=pltpu.PrefetchScalarGridSpec(
            num_scalar_prefetch=1, grid=(n_s, nb),
            in_specs=[pl.BlockSpec((None, tr, cols), lambda s, i, ids_ref: (s, ids_ref[2] * nb + i, 0)),
                      pl.BlockSpec((None, tr, cols), lambda s, i, ids_ref: (s, i, 0))],
            out_specs=(out_spec, out_spec)),
        out_shape=(SDS((n_s, h, cols), g.dtype), SDS((n_s, h, cols), BF)),
        compiler_params=_cp(("parallel", "parallel"), 40))(ids, g, p)


def _shard_sum(t, q, ids):
    _, h, cols = t.shape
    tr = min(h, 256)
    nb = h // tr

    def body(ids_ref, t_ref, q_ref, o_ref):
        o_ref[...] = ((t_ref[...] + q_ref[0].astype(F32)) + q_ref[1].astype(F32)) + q_ref[2].astype(F32)

    return pl.pallas_call(
        body, name="shard_sum",
        grid_spec=pltpu.PrefetchScalarGridSpec(
            num_scalar_prefetch=1, grid=(nb,),
            in_specs=[pl.BlockSpec((None, tr, cols), lambda i, ids_ref: (ids_ref[3], i, 0)),
                      pl.BlockSpec((3, tr, cols), lambda i, ids_ref: (0, i, 0))],
            out_specs=pl.BlockSpec((tr, cols), lambda i, ids_ref: (ids_ref[2] * nb + i, 0))),
        out_shape=SDS((2 * h, cols), t.dtype),
        compiler_params=_cp(("parallel",), 40))(ids, t, q)


def _gather_small(block, n_sum):
    rows, cols = block.shape
    n_dev = 8

    def body(x_ref, o_ref, g_ref, buf, send_sems, recv_sems, local_sem):
        x, y, c, chips = _place()
        me, sibling = (x, y, c), (x, y, 1 - c)

        def slot(px_, py_, pc_):
            return buf.at[4 * px_ + 2 * py_ + pc_]

        def copy(k, who, to, src=None):
            return _remote(slot(*who) if src is None else src, slot(*who), send_sems.at[k], recv_sems.at[k], to)

        mine = pltpu.make_async_copy(x_ref, slot(*me), local_sem)
        mine.start()
        first = [copy(0, me, sibling, src=x_ref)]
        first += [copy(1 + j, me, (*chip, c), src=x_ref) for j, chip in enumerate(chips)]
        for cp in first:
            cp.start()
        passed = [copy(4 + j, (*chip, c), sibling) for j, chip in enumerate(chips)]
        for j, chip in enumerate(chips):
            copy(1 + j, (*chip, c), me).wait_recv()
            passed[j].start()
        copy(0, sibling, me).wait_recv()
        for j, chip in enumerate(chips):
            copy(4 + j, (*chip, 1 - c), me).wait_recv()
        for cp in first + passed:
            cp.wait_send()
        mine.wait()
        acc = buf[0, :, :n_sum]
        for s in range(1, n_dev):
            acc = acc + buf[s, :, :n_sum]
        o_ref[...] = acc
        for s in range(n_dev):
            g_ref[s * rows:(s + 1) * rows, :] = buf[s, :, n_sum:]

    return pl.pallas_call(
        body, name="gather_small",
        in_specs=[pl.BlockSpec(memory_space=pltpu.VMEM)],
        out_specs=(pl.BlockSpec(memory_space=pltpu.VMEM), pl.BlockSpec(memory_space=pltpu.VMEM)),
        out_shape=(SDS((rows, n_sum), F32), SDS((n_dev * rows, cols - n_sum), F32)),
        scratch_shapes=[pltpu.VMEM((n_dev, rows, cols), F32), pltpu.SemaphoreType.DMA((7,)),
                        pltpu.SemaphoreType.DMA((7,)), pltpu.SemaphoreType.DMA],
        compiler_params=_cp(has_side_effects=True))(block)


def _adam_math(w, g, m, v):
    m = ADAM_B1 * m + (1.0 - ADAM_B1) * g
    v = ADAM_B2 * v + (1.0 - ADAM_B2) * (g * g)
    m_hat = m / (1.0 - ADAM_B1 ** ADAM_STEP)
    v_hat = v / (1.0 - ADAM_B2 ** ADAM_STEP)
    delta = -ADAM_LR * (m_hat / (jnp.sqrt(v_hat) + ADAM_EPS) + ADAM_WD * w)
    return delta, m, v


def _adamw(w, g, m, v):
    rows, cols = w.shape
    tr = min(rows, 256 if cols <= 2048 else 128)

    def body(w_ref, g_ref, m_ref, v_ref, go_ref, d_ref, nm_ref, nv_ref):
        g = g_ref[...]
        go_ref[...] = g
        d_ref[...], nm_ref[...], nv_ref[...] = _adam_math(w_ref[...], g, m_ref[...], v_ref[...])

    spec = pl.BlockSpec((tr, cols), lambda i: (i, 0))
    return pl.pallas_call(
        body, name="adamw", grid=(rows // tr,), in_specs=[spec] * 4, out_specs=(spec,) * 4,
        out_shape=(SDS(w.shape, F32),) * 4, compiler_params=_cp(("parallel",), 40))(w, g, m, v)


def _adamw_small(w, g, m, v):
    def body(w_ref, g_ref, m_ref, v_ref, go_ref, d_ref, nm_ref, nv_ref):
        w = w_ref[...]
        g = g_ref[...]
        sub = lax.broadcasted_iota(jnp.int32, w.shape, 0)
        lane = lax.broadcasted_iota(jnp.int32, w.shape, 1)
        is_ret = jnp.logical_and(sub == 5, lane < 2 * RET_HEADS)
        u = jnp.exp(jnp.where(is_ret, w, -1.0) * jnp.log(2.0))
        g = jnp.where(is_ret, g * (-u * jnp.log(2.0) / (1.0 - u)), g)
        go_ref[...] = g
        d_ref[...], nm_ref[...], nv_ref[...] = _adam_math(w, g, m_ref[...], v_ref[...])

    return pl.pallas_call(body, name="adamw_small", out_shape=(SDS(w.shape, F32),) * 4)(w, g, m, v)


def _rope_tables(seq, n_samp, n_ctx_rows):
    rows = seq // GRID_W
    row = jnp.repeat(jnp.arange(rows, dtype=F32), GRID_W)
    col = jnp.tile(jnp.arange(GRID_W, dtype=F32), rows)
    half = ATT_HEAD_DIM // 2
    freqs = ROPE_THETA ** (-jnp.arange(0, half, 2, dtype=F32) / half)
    ang = jnp.concatenate([row[:, None] * freqs, col[:, None] * freqs], axis=-1)
    cos, sin = jnp.cos(ang), jnp.sin(ang)
    cos_f = jnp.repeat(cos, 2, axis=1)
    sin_s = jnp.stack([-sin, sin], axis=-1).reshape(seq, ATT_HEAD_DIM)
    cos_all = jnp.concatenate([jnp.tile(cos_f, (n_samp, 1)), jnp.ones((n_ctx_rows, ATT_HEAD_DIM), F32)], axis=0)
    sin_all = jnp.concatenate([jnp.tile(sin_s, (n_samp, 1)), jnp.zeros((n_ctx_rows, ATT_HEAD_DIM), F32)], axis=0)
    return cos_all, sin_all


def _pack_small(c_ctx, norm_w, b_ada, ret, qn, kn):
    d = D_MODEL
    row5 = jnp.concatenate([ret.reshape(-1), jnp.zeros((128 - 2 * RET_HEADS,), F32), qn.reshape(-1), kn.reshape(-1),
                            jnp.zeros((d - 384,), F32)])
    return jnp.concatenate([c_ctx.reshape(1, d), norm_w.reshape(1, d), b_ada.reshape(3, d), row5.reshape(1, d),
                            jnp.zeros((2, d), F32)], axis=0)


def _unpack_small(p):
    d = D_MODEL
    return (p[0], p[1:2], p[2:5].reshape(1, 3 * d), p[5, :2 * RET_HEADS].reshape(1, 2, RET_HEADS),
            p[5:6, 128:256], p[5:6, 256:384])


def _step(x, c, ctx, c_ctx, norm_w, b_ada, ret_log2_decay, q_norm_w, k_norm_w, loss_target, weights, ids):
    n_samp, seq, d = x.shape
    lc = ctx.shape[1]
    t_lat, t_ctx = n_samp * seq, n_samp * lc
    assert seq % TM == 0 and t_ctx == TM and t_lat % lc == 0 and seq % GRID_W == 0
    tps = seq // TM

    x_lat = x.reshape(t_lat, d)
    x_ctx = ctx.reshape(t_ctx, d)
    cvec8 = jnp.concatenate([c, c_ctx.reshape(1, d), jnp.zeros((8 - n_samp - 1, d), F32)], axis=0)
    lg = jnp.log1p(-jnp.exp2(ret_log2_decay.reshape(2, RET_HEADS)))
    cos_all, sin_all = _rope_tables(seq, n_samp, t_ctx)

    w_ada_b, w_in_b, w_or_b, w_oa_b, w_out_b = weights
    c_all, mod_shards = _adaln_fwd(cvec8, w_ada_b, b_ada)
    mod8 = mod_shards.transpose(1, 0, 2).reshape(8, 3 * d)
    mod3 = mod8[:n_samp + 1]
    shift3 = mod3[:, None, 0:d]
    scale3 = mod3[:, None, d:2 * d]
    gate3 = mod3[:, None, 2 * d:3 * d]

    hx, hxt = _norm_fwd(x_lat, x_ctx, norm_w, scale3, shift3, tps, n_samp)
    px, w_in_g = _in_proj_gather(hx, w_in_b, ids)

    states0 = _ctx_state_fwd(px, lg, n_samp, t_lat, lc)
    (o_f, o_b, saved), w_o = _ret_fwd(px, states0, lg, n_samp, seq,
                                      comm=_ag_comm((w_or_b, w_oa_b, w_out_b)))
    w_o_ret, w_o_att, w_out = (w.reshape(-1, d) for w in w_o)

    qn = _att_prep_q(px, cos_all, sin_all, q_norm_w, t_lat)
    kn, vn = _att_prep_kv(px, cos_all, sin_all, k_norm_w)
    o_att, lse = _att_fwd(qn, kn, vn, n_samp, seq, lc)

    (gx_res, do, do_att, dpx, loss8, dgate, g_w_o_ret, g_w_o_att, g_w_out) = _merge(
        x_lat, loss_target.reshape(t_lat, d), o_f, o_b, o_att, px, gate3, w_o_ret, w_o_att, w_out, tps)

    g_a = [g.reshape(N_SHARD, -1, d) for g in (g_w_o_ret, g_w_o_att, g_w_out)]
    (dpx, dkl, dkc, dvl, dvc, gqw), (*sib_a, w_ada_g) = _att_bwd(
        dpx, qn, kn, vn, px, o_att, lse, do_att, cos_all, sin_all, q_norm_w, n_samp, seq, lc,
        comm=_join_comms(_swap_comm(g_a), _ag_comm((w_ada_b,))))
    dpx, gkw = _att_kv_bwd(dpx, dkl, dkc, dvl, dvc, px, cos_all, sin_all, k_norm_w)
    t_a = [_chip_sum(g, p, ids) for g, p in zip(g_a, sib_a)]

    (dpx, dstates, dlg_lat), q_a = _ret_bwd(dpx, px, do, saved, lg, n_samp, seq,
                                            comm=_exchange_comm([t16 for _, t16 in t_a]))
    r_a = [_shard_sum(t, q, ids) for (t, _), q in zip(t_a, q_a)]
    dpx, dlg_ctx = _ctx_state_bwd(dpx, px, dstates, lg, n_samp, t_lat, lc)
    dpx = _zero_ctx_tail(dpx, t_lat)

    n_tiles = dpx.shape[0] // _big_rows(dpx.shape[0])
    g_b = _gw_in(hxt, dpx)
    dhx, (sib_b, *r_a) = _dhx(dpx, w_in_g, 0, 1, None, _join_comms(_swap_comm([g_b]), _join_comm(r_a)))
    t_b, t16_b = _chip_sum(g_b, sib_b, ids)
    dhx, (q_b,) = _dhx(dpx, w_in_g, 1, n_tiles - 1, dhx, _exchange_comm([t16_b]))
    r_b_half = _shard_sum(t_b, q_b, ids)
    grad_x, dshift, dscale, g_norm_w = _norm_bwd(x_lat, x_ctx, dhx, gx_res, norm_w, scale3, tps, n_samp)

    dgate_all = jnp.concatenate([dgate, jnp.zeros((1, 1, d), F32)], axis=0)
    dmod3 = jnp.concatenate([dshift, dscale, dgate_all], axis=2).reshape(n_samp + 1, 3 * d)
    dmod8 = jnp.concatenate([dmod3, jnp.zeros((8 - n_samp - 1, 3 * d), F32)], axis=0)
    g_lg = (jnp.sum(dlg_lat[:, :, 0], axis=0).reshape(2, RET_HEADS)
            + jnp.stack([jnp.sum(dlg_ctx[:, :, 0, 0], axis=0), jnp.sum(dlg_ctx[:, :, 1, 0], axis=0)], axis=0))
    g_qw = jnp.sum(gqw, axis=(0, 1, 2))
    zero = jnp.zeros((d,), F32)

    local = _pack_small(zero, g_norm_w, jnp.zeros((3 * d,), F32), g_lg, g_qw, gkw).at[6, 0].set(loss8[0, 0])
    small_sum, dmod_all = _gather_small(jnp.concatenate([local, dmod8], axis=1), d)
    (r_c, g_b_ada, dc_all), (r_b,) = _adaln_bwd(c_all, dmod_all, w_ada_g, comm=_join_comm([r_b_half]))
    dc_ctx = jnp.sum(dc_all.reshape(-1, 8, d)[:, n_samp], axis=0)
    small = small_sum + _pack_small(dc_ctx, zero, g_b_ada, jnp.zeros((2, RET_HEADS), F32), zero[:128], zero[:128])
    return small[6, 0], grad_x.reshape(n_samp, seq, d), (r_c, r_b, *r_a), small


def kernel(x, c, ctx, c_ctx, norm_w, w_ada, b_ada, w_in, ret_log2_decay, q_norm_w, k_norm_w, w_o_ret, w_o_att, w_out, loss_target, m_c_ctx, m_norm_w, m_w_ada, m_b_ada, m_w_in, m_ret_log2_decay, m_q_norm_w, m_k_norm_w, m_w_o_ret, m_w_o_att, m_w_out, v_c_ctx, v_norm_w, v_w_ada, v_b_ada, v_w_in, v_ret_log2_decay, v_q_norm_w, v_k_norm_w, v_w_o_ret, v_w_o_att, v_w_out):
    big_w = (w_ada[0], w_in[0], w_o_ret[0], w_o_att[0], w_out[0])
    big_m = (m_w_ada[0], m_w_in[0], m_w_o_ret[0], m_w_o_att[0], m_w_out[0])
    big_v = (v_w_ada[0], v_w_in[0], v_w_o_ret[0], v_w_o_att[0], v_w_out[0])

    ids = _place_ids()
    loss, grad_x, big_grad, small_grad_in = _step(
        x, c, ctx, c_ctx, norm_w[0:1], b_ada[0:1], ret_log2_decay[0], q_norm_w[0:1], k_norm_w[0:1], loss_target,
        tuple(_cast_place(w, ids) for w in big_w), ids)
    small_w = _pack_small(c_ctx, norm_w, b_ada, ret_log2_decay, q_norm_w, k_norm_w)
    small_m = _pack_small(m_c_ctx, m_norm_w, m_b_ada, m_ret_log2_decay, m_q_norm_w, m_k_norm_w)
    small_v = _pack_small(v_c_ctx, v_norm_w, v_b_ada, v_ret_log2_decay, v_q_norm_w, v_k_norm_w)
    small_grad, small_delta, small_nm, small_nv = _adamw_small(small_w, small_grad_in, small_m, small_v)

    big_g, big_delta, big_nm, big_nv = [], [], [], []
    for w, g, m, v in zip(big_w, big_grad, big_m, big_v):
        go, dlt, nm, nv = _adamw(w, g, m, v)
        big_g.append(go[None])
        big_delta.append(dlt[None])
        big_nm.append(nm[None])
        big_nv.append(nv[None])
    big_grad = big_g

    def order(small_packed, big):
        s = _unpack_small(small_packed)
        return (s[0], s[1], big[0], s[2], big[1], s[3], s[4], s[5], big[2], big[3], big[4])

    return (loss, grad_x, *order(small_grad, big_grad), *order(small_delta, big_delta),
            *order(small_nm, big_nm), *order(small_nv, big_nv))
```

```python
import functools
from typing import NamedTuple

import jax
import jax.numpy as jnp
from jax import lax
from jax.experimental import pallas as pl
from jax.experimental.pallas import tpu as pltpu

F32 = jnp.float32
BF = jnp.bfloat16
SDS = jax.ShapeDtypeStruct
MESH = pl.DeviceIdType.MESH
ANY = pl.BlockSpec(memory_space=pl.ANY)
SMEM = pl.BlockSpec(memory_space=pltpu.SMEM)

D_MODEL = 1024
GRID_W = 64
RET_HEADS = 4
RET_DK = 256
RET_DV = 512
RET_CHUNK = 128
ATT_HEADS = 8
ATT_KV_HEADS = 2
ATT_REP = ATT_HEADS // ATT_KV_HEADS
ATT_HEAD_DIM = 128
ROPE_THETA = 10000.0
NORM_EPS = 1e-6
IN_COLS = 10752
KV_COLS = 3584
C_RK, C_RV, C_AK, C_AV, C_RQ, C_RG, C_AQ, C_AG, C_MR, C_MA = 0, 1024, 3072, 3328, 3584, 4608, 6656, 7680, 8704, 9728
N_SHARD = 4
ADA_W = 3 * D_MODEL // N_SHARD
IN_W = IN_COLS // N_SHARD
IN_BLK = IN_W
BPS = IN_W // IN_BLK
N_IN_BLK = IN_COLS // IN_BLK
TM = 512
ATT_TQ = 512
ADAM_LR, ADAM_B1, ADAM_B2, ADAM_EPS, ADAM_WD, ADAM_STEP = 0.001, 0.9, 0.999, 1e-08, 0.01, 10
MIB = 1024 * 1024


def _cp(sem=None, vmem_mb=None, **kw):
    if sem is not None:
        kw["dimension_semantics"] = sem
    if vmem_mb is not None:
        kw["vmem_limit_bytes"] = vmem_mb * MIB
    return pltpu.CompilerParams(**kw)


def _dot(a, b, ca=1, cb=0):
    return lax.dot_general(a.astype(BF), b.astype(BF), (((ca,), (cb,)), ((), ())), preferred_element_type=F32)


def _sigmoid(x):
    return 0.5 * jnp.tanh(0.5 * x) + 0.5


def _sum_all(x):
    return jnp.sum(jnp.sum(x, axis=1, keepdims=True), axis=0, keepdims=True)


def _swap_pairs(x):
    ax = x.ndim - 1
    lane = lax.broadcasted_iota(jnp.int32, x.shape, ax)
    nxt = pltpu.roll(x, x.shape[ax] - 1, ax)
    prv = pltpu.roll(x, 1, ax)
    return jnp.where(lane % 2 == 0, nxt, prv)


def _rms(x):
    return lax.rsqrt(jnp.mean(x * x, axis=-1, keepdims=True) + NORM_EPS)


def _rms_bwd(dxh, xh, r):
    return r * (dxh - xh * jnp.mean(dxh * xh, axis=-1, keepdims=True))


class _Comm(NamedTuple):
    name: str
    ins: tuple
    out_shapes: tuple
    aliases: dict
    n_sems: int
    phases: tuple


def _join_comms(*comms):
    offs, i_off, o_off, s_off = [], 0, 0, 0
    for cm in comms:
        offs.append((i_off, o_off, s_off))
        i_off, o_off, s_off = i_off + len(cm.ins), o_off + len(cm.out_shapes), s_off + cm.n_sems

    def phase(k):
        def run(ins, outs, ssem, rsem, base):
            sends, recvs = [], []
            for cm, (io, oo, so) in zip(comms, offs):
                if k < len(cm.phases):
                    s, r = cm.phases[k](ins[io:io + len(cm.ins)], outs[oo:oo + len(cm.out_shapes)], ssem, rsem,
                                        base + so)
                    sends += s
                    recvs += r
            return sends, recvs
        return run

    aliases = {}
    for cm, (io, oo, _) in zip(comms, offs):
        aliases.update({io + a: oo + b for a, b in cm.aliases.items()})
    return _Comm("+".join(cm.name for cm in comms), sum((cm.ins for cm in comms), ()),
                 sum((cm.out_shapes for cm in comms), ()), aliases, s_off,
                 tuple(phase(k) for k in range(max(len(cm.phases) for cm in comms))))


def _run_phases(comm, cins, couts, ssem, rsem):
    for k, phase in enumerate(comm.phases):
        sends, recvs = phase(cins, couts, ssem, rsem, 0)
        if k > 0:
            for cp in sends:
                cp.start()
        for cp in recvs:
            cp.wait_recv()
        for cp in sends:
            cp.wait_send()


def _call(body, args, comm, *, name, grid, in_specs, out_specs, out_shape, scratch_shapes=(),
          compiler_params, aliases=None):
    n_in, n_out, n_sc = len(in_specs), len(out_specs), len(scratch_shapes)
    n_ci, n_co = len(comm.ins), len(comm.out_shapes)
    io_alias = dict(aliases or {})
    io_alias.update({n_in + a: n_out + b for a, b in comm.aliases.items()})

    def kernel_body(*refs):
        ins, cins = refs[:n_in], refs[n_in:n_in + n_ci]
        outs = refs[n_in + n_ci:n_in + n_ci + n_out]
        couts = refs[n_in + n_ci + n_out:n_in + n_ci + n_out + n_co]
        scratch = refs[n_in + n_ci + n_out + n_co:n_in + n_ci + n_out + n_co + n_sc]
        ssem, rsem = refs[-2:]
        first = functools.reduce(jnp.logical_and, [pl.program_id(k) == 0 for k in range(len(grid))])
        last = functools.reduce(jnp.logical_and, [pl.program_id(k) == grid[k] - 1 for k in range(len(grid))])

        @pl.when(first)
        def _():
            for cp in comm.phases[0](cins, couts, ssem, rsem, 0)[0]:
                cp.start()

        body(*ins, *outs, *scratch)

        @pl.when(last)
        def _():
            _run_phases(comm, cins, couts, ssem, rsem)

    res = pl.pallas_call(
        kernel_body, name=name + "+" + comm.name, grid=grid, in_specs=list(in_specs) + [ANY] * n_ci,
        out_specs=tuple(out_specs) + tuple([ANY] * n_co), out_shape=tuple(out_shape) + tuple(comm.out_shapes),
        scratch_shapes=list(scratch_shapes) + [pltpu.SemaphoreType.DMA((comm.n_sems,)),
                                               pltpu.SemaphoreType.DMA((comm.n_sems,))],
        input_output_aliases=io_alias, compiler_params=compiler_params)(*args, *comm.ins)
    return tuple(res[:n_out]), tuple(res[n_out:])


def _adaln_fwd(cvec8, w_ada_b, b_ada):
    rows, d = cvec8.shape
    n_dev = 8

    def body(x_ref, w_hbm, b_ref, g_ref, o_ref, buf, wv, part, send_sems, recv_sems, local_sems):
        x, y, c, chips = _place()
        me, sibling, shard = (x, y, c), (x, y, 1 - c), 2 * x + y

        def slot(px_, py_, pc_):
            return 4 * px_ + 2 * py_ + pc_

        def copy(k, who, to, src=None):
            dst = buf.at[slot(*who)]
            return _remote(dst if src is None else src, dst, send_sems.at[k], recv_sems.at[k], to)

        weight = pltpu.make_async_copy(w_hbm.at[shard], wv, local_sems.at[0])
        weight.start()
        mine = pltpu.make_async_copy(x_ref, buf.at[slot(*me)], local_sems.at[1])
        mine.start()
        first = [copy(0, me, sibling, src=x_ref)]
        first += [copy(1 + j, me, (*chip, c), src=x_ref) for j, chip in enumerate(chips)]
        for cp in first:
            cp.start()
        passed = [copy(4 + j, (*chip, c), sibling) for j, chip in enumerate(chips)]
        for j, chip in enumerate(chips):
            copy(1 + j, (*chip, c), me).wait_recv()
            passed[j].start()
        copy(0, sibling, me).wait_recv()
        for j, chip in enumerate(chips):
            copy(4 + j, (*chip, 1 - c), me).wait_recv()
        for cp in first + passed:
            cp.wait_send()
        mine.wait()
        weight.wait()

        cv = buf[...].reshape(n_dev * rows, d)
        g_ref[...] = cv
        sc = (cv * _sigmoid(cv)).astype(BF)
        mod = jnp.dot(sc, wv[...], preferred_element_type=F32) + b_ref[shard]
        part[...] = mod.reshape(n_dev, rows, ADA_W)

        def back(j, chip):
            theirs = o_ref.at[2 * chip[0] + chip[1]]
            return (_remote(part.at[slot(*chip, c)], o_ref.at[shard], send_sems.at[7 + j], recv_sems.at[7 + j],
                            (*chip, c)),
                    _remote(theirs, theirs, send_sems.at[7 + j], recv_sems.at[7 + j], (*chip, c)))

        for j, chip in enumerate(chips):
            back(j, chip)[0].start()
        o_ref[shard] = part[slot(*me)]
        for j, chip in enumerate(chips):
            back(j, chip)[1].wait_recv()
        for j, chip in enumerate(chips):
            back(j, chip)[0].wait_send()

    vmem = pl.BlockSpec(memory_space=pltpu.VMEM)
    return pl.pallas_call(
        body, name="adaln_fwd", in_specs=[vmem, ANY, vmem], out_specs=(vmem, vmem),
        out_shape=(SDS((n_dev * rows, d), F32), SDS((N_SHARD, rows, ADA_W), F32)),
        scratch_shapes=[pltpu.VMEM((n_dev, rows, d), F32), pltpu.VMEM((D_MODEL, ADA_W), BF),
                        pltpu.VMEM((n_dev, rows, ADA_W), F32), pltpu.SemaphoreType.DMA((10,)),
                        pltpu.SemaphoreType.DMA((10,)), pltpu.SemaphoreType.DMA((2,))],
        compiler_params=_cp(vmem_mb=32, has_side_effects=True))(cvec8, w_ada_b, b_ada.reshape(N_SHARD, 1, ADA_W))


def _adaln_bwd(cvec, dmod, w_ada_g, comm):
    n_rows = cvec.shape[0]
    def body(c_ref, d_ref, w_ref, gw_ref, gb_ref, dc_ref):
        cv = c_ref[...]
        sg = _sigmoid(cv)
        sc = cv * sg
        dm = d_ref[...]
        gb_ref[...] = jnp.sum(dm, axis=0, keepdims=True)
        dsc = jnp.zeros(cv.shape, F32)
        for s in range(N_SHARD):
            cols = slice(s * ADA_W, (s + 1) * ADA_W)
            gw_ref[s] = _dot(sc, dm[:, cols], 0, 0)
            dsc = dsc + _dot(dm[:, cols], w_ref[s], 1, 1)
        dc_ref[...] = dsc * (sg * (1.0 + cv * (1.0 - sg)))

    def whole(shape):
        return pl.BlockSpec(shape, lambda i: (0,) * len(shape))

    shapes = ((N_SHARD, D_MODEL, ADA_W), (1, 3 * D_MODEL), (n_rows, D_MODEL))
    return _call(body, [cvec, dmod, w_ada_g], comm, name="adaln_bwd", grid=(1,),
                 in_specs=[whole(cvec.shape), whole(dmod.shape), whole(w_ada_g.shape)],
                 out_specs=tuple(whole(s) for s in shapes), out_shape=tuple(SDS(s, F32) for s in shapes),
                 compiler_params=_cp(("arbitrary",), 56))


def _big_rows(rows):
    return 1536 if rows % 1536 == 0 else TM


def _norm_fwd(x_lat, x_ctx, norm_w, scale3, shift3, tiles_per_sample, n_samp):
    n_lat = x_lat.shape[0] // TM
    rows = x_lat.shape[0] + x_ctx.shape[0]

    def samp(i):
        return jnp.minimum(i // tiles_per_sample, n_samp)

    def body(x_ref, c_ref, nw_ref, sc_ref, sh_ref, hx_ref, hxt_ref):
        x = jnp.where(pl.program_id(0) < n_lat, x_ref[...], c_ref[...])
        h = x * _rms(x) * nw_ref[...] * (1.0 + sc_ref[...]) + sh_ref[...]
        hx_ref[...] = h.astype(BF)
        hxt_ref[...] = h.T.astype(BF)

    return pl.pallas_call(
        body, name="norm_fwd", grid=(rows // TM,),
        in_specs=[pl.BlockSpec((TM, D_MODEL), lambda i: (jnp.minimum(i, n_lat - 1), 0)),
                  pl.BlockSpec((TM, D_MODEL), lambda i: (jnp.maximum(i - n_lat, 0), 0)),
                  pl.BlockSpec((1, D_MODEL), lambda i: (0, 0)),
                  pl.BlockSpec((None, 1, D_MODEL), lambda i: (samp(i), 0, 0)),
                  pl.BlockSpec((None, 1, D_MODEL), lambda i: (samp(i), 0, 0))],
        out_specs=(pl.BlockSpec((TM, D_MODEL), lambda i: (i, 0)),
                   pl.BlockSpec((D_MODEL, TM), lambda i: (0, i))),
        out_shape=(SDS((rows, D_MODEL), BF), SDS((D_MODEL, rows), BF)),
        compiler_params=_cp(("parallel",), 40))(x_lat, x_ctx, norm_w, scale3, shift3)


def _norm_bwd(x_lat, x_ctx, dhx, gx_res, norm_w, scale3, tiles_per_sample, n_samp):
    rows = x_lat.shape[0] + x_ctx.shape[0]
    n_lat = tiles_per_sample * n_samp

    def samp(i):
        return jnp.minimum(i // tiles_per_sample, n_samp)

    def lat(i):
        return jnp.minimum(i, n_lat - 1)

    def body(x_ref, c_ref, dh_ref, gr_ref, nw_ref, sc_ref, gx_ref, dsh_ref, dsc_ref, dnw_ref):
        i = pl.program_id(0)
        x = jnp.where(i < n_lat, x_ref[...], c_ref[...])
        r = _rms(x)
        xh = x * r
        nw = nw_ref[...]
        dh = dh_ref[...]
        first = jnp.logical_or(i % tiles_per_sample == 0, i >= n_lat)

        @pl.when(first)
        def _():
            dsh_ref[...] = jnp.zeros_like(dsh_ref)
            dsc_ref[...] = jnp.zeros_like(dsc_ref)

        @pl.when(i == 0)
        def _():
            dnw_ref[...] = jnp.zeros_like(dnw_ref)

        dsh_ref[...] += jnp.sum(dh, axis=0, keepdims=True)
        dsc_ref[...] += jnp.sum(dh * (xh * nw), axis=0, keepdims=True)
        du = dh * (1.0 + sc_ref[...])
        dnw_ref[...] += jnp.sum(du * xh, axis=0, keepdims=True)

        @pl.when(i < n_lat)
        def _():
            gx_ref[...] = gr_ref[...] + _rms_bwd(du * nw, xh, r)

    return pl.pallas_call(
        body, name="norm_bwd", grid=(rows // TM,),
        in_specs=[pl.BlockSpec((TM, D_MODEL), lambda i: (lat(i), 0)),
                  pl.BlockSpec((TM, D_MODEL), lambda i: (jnp.maximum(i - n_lat, 0), 0)),
                  pl.BlockSpec((TM, D_MODEL), lambda i: (i, 0)),
                  pl.BlockSpec((TM, D_MODEL), lambda i: (lat(i), 0)),
                  pl.BlockSpec((1, D_MODEL), lambda i: (0, 0)),
                  pl.BlockSpec((None, 1, D_MODEL), lambda i: (samp(i), 0, 0))],
        out_specs=(pl.BlockSpec((TM, D_MODEL), lambda i: (lat(i), 0)),
                   pl.BlockSpec((None, 1, D_MODEL), lambda i: (samp(i), 0, 0)),
                   pl.BlockSpec((None, 1, D_MODEL), lambda i: (samp(i), 0, 0)),
                   pl.BlockSpec((1, D_MODEL), lambda i: (0, 0))),
        out_shape=(SDS((n_lat * TM, D_MODEL), F32), SDS((n_samp + 1, 1, D_MODEL), F32),
                   SDS((n_samp + 1, 1, D_MODEL), F32), SDS((1, D_MODEL), F32)),
        compiler_params=_cp(("arbitrary",), 40))(x_lat, x_ctx, dhx, gx_res, norm_w, scale3)


def _in_proj_gather(hx, w_buf, ids):
    rows = hx.shape[0]
    tb = _big_rows(rows)
    n_i = rows // tb
    hrows = D_MODEL // 2

    def body(ids_ref, h_ref, w_in_hbm, px_ref, w_hbm, wv, lsem, ssem, rsem):
        j, i = pl.program_id(0), pl.program_id(1)
        x, y, c, chips = _place()
        sibling = (x, y, 1 - c)

        def half(s, which):
            return w_hbm.at[s, pl.ds(which * hrows, hrows), :]

        def over_ici(rel):
            chip = chips[rel]
            mine, theirs = half(2 * x + y, c), half(2 * chip[0] + chip[1], c)
            return (_remote(mine, mine, ssem.at[rel], rsem.at[rel], (*chip, c)),
                    _remote(theirs, theirs, ssem.at[rel], rsem.at[rel], (*chip, c)))

        def over_d2d(rel):
            s = 2 * chips[rel][0] + chips[rel][1]
            return (_remote(half(s, c), half(s, c), ssem.at[3 + rel], rsem.at[3 + rel], sibling),
                    _remote(half(s, 1 - c), half(s, 1 - c), ssem.at[3 + rel], rsem.at[3 + rel], sibling))

        first_row_tile = i == 0

        @pl.when(jnp.logical_and(j == 0, first_row_tile))
        def _():
            over_ici(0)[0].start()
            over_ici(1)[0].start()

        @pl.when(jnp.logical_and(j == 1, first_row_tile))
        def _():
            for rel in range(2):
                over_ici(rel)[1].wait_recv()
                over_d2d(rel)[0].start()
            over_ici(2)[0].start()
            over_d2d(0)[1].wait_recv()

        @pl.when(jnp.logical_and(j == 2, first_row_tile))
        def _():
            over_d2d(1)[1].wait_recv()

        @pl.when(jnp.logical_and(j == 3, first_row_tile))
        def _():
            over_ici(2)[1].wait_recv()
            passed, landing = over_d2d(2)
            passed.start()
            landing.wait_recv()

        @pl.when(first_row_tile)
        def _():
            cp = pltpu.make_async_copy(w_hbm.at[ids_ref[4 + j]], wv, lsem)
            cp.start(priority=1)
            cp.wait()

        px_ref[...] = jnp.dot(h_ref[...], wv[...], preferred_element_type=F32).astype(BF)

        @pl.when(jnp.logical_and(j == N_SHARD - 1, i == n_i - 1))
        def _():
            for rel in range(3):
                over_ici(rel)[0].wait_send()
                over_d2d(rel)[0].wait_send()

    return pl.pallas_call(
        body, name="in_proj_gather", input_output_aliases={2: 1},
        grid_spec=pltpu.PrefetchScalarGridSpec(
            num_scalar_prefetch=1, grid=(N_SHARD, n_i),
            in_specs=[pl.BlockSpec((tb, D_MODEL), lambda j, i, ids_ref: (i, 0)), ANY],
            out_specs=(pl.BlockSpec((tb, IN_W), lambda j, i, ids_ref: (i, ids_ref[4 + j])), ANY),
            scratch_shapes=[pltpu.VMEM((D_MODEL, IN_W), BF), pltpu.SemaphoreType.DMA,
                            pltpu.SemaphoreType.DMA((6,)), pltpu.SemaphoreType.DMA((6,))]),
        out_shape=(SDS((rows, IN_COLS), BF), SDS(w_buf.shape, w_buf.dtype)),
        compiler_params=_cp(("arbitrary", "arbitrary"), 56))(ids, hx, w_buf)


def _gw_in(hxt, dpx_all):
    rows = dpx_all.shape[0]
    tb = _big_rows(rows)

    def body(h_ref, d_ref, o_ref):
        @pl.when(pl.program_id(1) == 0)
        def _():
            o_ref[...] = jnp.zeros_like(o_ref)

        o_ref[...] += jnp.dot(h_ref[...], d_ref[...], preferred_element_type=F32)

    return pl.pallas_call(
        body, name="gw_in", grid=(N_IN_BLK, rows // tb),
        in_specs=[pl.BlockSpec((D_MODEL, tb), lambda j, i: (0, i)),
                  pl.BlockSpec((tb, IN_BLK), lambda j, i: (i, j))],
        out_specs=pl.BlockSpec((None, D_MODEL, IN_BLK), lambda j, i: (j // BPS, 0, j % BPS)),
        out_shape=SDS((N_SHARD, D_MODEL, IN_W), F32),
        compiler_params=_cp(("arbitrary", "arbitrary"), 56))(hxt, dpx_all)


def _dhx(dpx_all, w_in_g, tile0, n_tiles, dhx, comm):
    rows = dpx_all.shape[0]
    tb = _big_rows(rows)

    def body(d_ref, w_ref, *rest):
        o_ref = rest[-1]

        @pl.when(pl.program_id(1) == 0)
        def _():
            o_ref[...] = jnp.zeros_like(o_ref)

        o_ref[...] += lax.dot_general(d_ref[...], w_ref[...], (((1,), (1,)), ((), ())), preferred_element_type=F32)

    args, in_specs, aliases = [dpx_all, w_in_g], [
        pl.BlockSpec((tb, IN_BLK), lambda i, j: (tile0 + i, j)),
        pl.BlockSpec((None, D_MODEL, IN_BLK), lambda i, j: (j // BPS, 0, j % BPS))], None
    if dhx is not None:
        args, in_specs, aliases = args + [dhx], in_specs + [ANY], {2: 0}
    (out,), got = _call(body, args, comm, name="dhx", grid=(n_tiles, N_IN_BLK), in_specs=in_specs,
                        out_specs=(pl.BlockSpec((tb, D_MODEL), lambda i, j: (tile0 + i, 0)),),
                        out_shape=(SDS((rows, D_MODEL), F32),), aliases=aliases,
                        compiler_params=_cp(("arbitrary", "arbitrary"), 56))
    return out, got


def _decays(lgv, d):
    c = RET_CHUNK
    ii = lax.broadcasted_iota(jnp.int32, (c, 1), 0).astype(F32)
    jj = lax.broadcasted_iota(jnp.int32, (1, c), 1).astype(F32)
    a_i = jnp.where(d == 0, ii, c - 1.0 - ii)
    a_j = jnp.where(d == 0, jj, c - 1.0 - jj)
    rel = a_i - a_j
    mask = jnp.where(rel >= 0, jnp.exp(lgv * jnp.maximum(rel, 0.0)), 0.0)
    qd = jnp.exp(lgv * (a_i + 1.0))
    kd = jnp.exp(lgv * (c - 1.0 - a_i))
    gc = jnp.exp(jnp.full((1, 1), lgv * c, F32))
    return a_i, rel, mask, qd, kd, gc


def _ctx_state_fwd(px, lg, n_samp, t_lat, lc):
    rb = t_lat // lc

    def body(lg_ref, k_ref, v_ref, o_ref):
        h = pl.program_id(1)
        k = k_ref[...].astype(F32) * (RET_DK ** -0.5)
        v = v_ref[...]
        pos = lax.broadcasted_iota(jnp.int32, (lc, 1), 0).astype(F32)
        o_ref[0] = _dot(k * jnp.exp(lg_ref[0, h] * (lc - 1.0 - pos)), v, 0, 0)
        o_ref[1] = _dot(k * jnp.exp(lg_ref[1, h] * pos), v, 0, 0)

    return pl.pallas_call(
        body, name="ctx_state_fwd", grid=(n_samp, RET_HEADS),
        in_specs=[SMEM,
                  pl.BlockSpec((lc, RET_DK), lambda b, h: (rb + b, C_RK // RET_DK + h)),
                  pl.BlockSpec((lc, RET_DV), lambda b, h: (rb + b, C_RV // RET_DV + h))],
        out_specs=pl.BlockSpec((None, 2, None, RET_DK, RET_DV), lambda b, h: (b, 0, h, 0, 0)),
        out_shape=SDS((n_samp, 2, RET_HEADS, RET_DK, RET_DV), F32),
        compiler_params=_cp(("parallel", "parallel")))(lg, px, px)


def _ctx_state_bwd(dpx, px, dstates, lg, n_samp, t_lat, lc):
    rb = t_lat // lc
    kspec = pl.BlockSpec((lc, RET_DK), lambda b, h: (rb + b, C_RK // RET_DK + h))
    vspec = pl.BlockSpec((lc, RET_DV), lambda b, h: (rb + b, C_RV // RET_DV + h))
    sspec = pl.BlockSpec((None, 2, None, RET_DK, RET_DV), lambda b, h: (b, 0, h, 0, 0))

    def weights(lg_ref, h):
        pos = lax.broadcasted_iota(jnp.int32, (lc, 1), 0).astype(F32)
        e_f = lc - 1.0 - pos
        return pos, e_f, jnp.exp(lg_ref[0, h] * e_f), jnp.exp(lg_ref[1, h] * pos)

    def k_body(lg_ref, dpx_hbm, k_ref, v_ref, ds_ref, dk_ref, dlg_ref):
        pos, e_f, w_f, w_b = weights(lg_ref, pl.program_id(1))
        k = k_ref[...].astype(F32) * (RET_DK ** -0.5)
        y_f = _dot(v_ref[...], ds_ref[0], 1, 1) * w_f
        y_b = _dot(v_ref[...], ds_ref[1], 1, 1) * w_b
        dk_ref[...] = ((y_f + y_b) * (RET_DK ** -0.5)).astype(BF)
        t_f = _sum_all(e_f * k * y_f)
        t_b = _sum_all(pos * k * y_b)
        sub = lax.broadcasted_iota(jnp.int32, (8, 128), 0)
        dlg_ref[...] = jnp.where(sub == 0, t_f, jnp.where(sub == 1, t_b, 0.0))

    def v_body(lg_ref, dpx_hbm, k_ref, ds_ref, dv_ref):
        _, _, w_f, w_b = weights(lg_ref, pl.program_id(1))
        k = k_ref[...].astype(F32) * (RET_DK ** -0.5)
        dv_ref[...] = (_dot(k * w_f, ds_ref[0]) + _dot(k * w_b, ds_ref[1])).astype(BF)

    dpx, dlg = pl.pallas_call(
        k_body, name="ctx_state_bwd_k", grid=(n_samp, RET_HEADS), input_output_aliases={1: 0},
        in_specs=[SMEM, ANY, kspec, vspec, sspec],
        out_specs=(kspec, pl.BlockSpec((None, None, 8, 128), lambda b, h: (b, h, 0, 0))),
        out_shape=(SDS(dpx.shape, dpx.dtype), SDS((n_samp, RET_HEADS, 8, 128), F32)),
        compiler_params=_cp(("parallel", "parallel")))(lg, dpx, px, px, dstates)
    dpx = pl.pallas_call(
        v_body, name="ctx_state_bwd_v", grid=(n_samp, RET_HEADS), input_output_aliases={1: 0},
        in_specs=[SMEM, ANY, kspec, sspec], out_specs=vspec, out_shape=SDS(dpx.shape, dpx.dtype),
        compiler_params=_cp(("parallel", "parallel")))(lg, dpx, px, dstates)
    return dpx, dlg


def _zero_ctx_tail(dpx, t_lat):
    wb = 512
    n_ctx = (dpx.shape[0] - t_lat) // TM

    def body(dpx_hbm, o_ref):
        o_ref[...] = jnp.zeros_like(o_ref)

    return pl.pallas_call(
        body, name="zero_ctx_tail", grid=(n_ctx, (IN_COLS - KV_COLS) // wb), input_output_aliases={0: 0},
        in_specs=[ANY], out_specs=pl.BlockSpec((TM, wb), lambda i, j: (t_lat // TM + i, KV_COLS // wb + j)),
        out_shape=SDS(dpx.shape, dpx.dtype),
        compiler_params=_cp(("parallel", "parallel")))(dpx)


def _ret_specs(row_f, row_b):
    c = RET_CHUNK
    wq = RET_HEADS * RET_DK // 2
    wv = RET_HEADS * RET_DV // 2
    specs = []
    for row in (row_f, row_b):
        specs += [pl.BlockSpec((c, wq), lambda b, n, row=row: (row(b, n), C_RQ // wq)),
                  pl.BlockSpec((c, wq), lambda b, n, row=row: (row(b, n), C_RQ // wq + 1)),
                  pl.BlockSpec((c, 2 * wq), lambda b, n, row=row: (row(b, n), C_RK // (2 * wq))),
                  pl.BlockSpec((c, wv), lambda b, n, row=row: (row(b, n), C_RV // wv)),
                  pl.BlockSpec((c, wv), lambda b, n, row=row: (row(b, n), C_RV // wv + 1))]
    return specs


def _ret_head(refs, h):
    q0, q1, k_ref, v0, v1 = refs
    hh = h % 2
    q = (q0, q1)[h // 2][:, hh * RET_DK:(hh + 1) * RET_DK].astype(F32)
    k = k_ref[:, h * RET_DK:(h + 1) * RET_DK].astype(F32) * (RET_DK ** -0.5)
    v = (v0, v1)[h // 2][:, hh * RET_DV:(hh + 1) * RET_DV]
    return q, k, v


def _ret_fwd(px, states0, lg, n_samp, seq, comm):
    c = RET_CHUNK
    nc = seq // c
    t_lat = n_samp * seq
    wo = RET_HEADS * RET_DV

    def row_f(b, n):
        return b * nc + n

    def row_b(b, n):
        return b * nc + nc - 1 - n

    def body(lg_ref, *refs):
        ins, (s0_ref, of_ref, ob_ref, st_ref, s_s) = refs[:10], refs[10:]

        @pl.when(pl.program_id(1) == 0)
        def _():
            s_s[...] = s0_ref[...]

        for d, o_ref in ((0, of_ref), (1, ob_ref)):
            for h in range(RET_HEADS):
                _, _, mask, qd, kd, gc = _decays(lg_ref[d, h], d)
                q, k, v = _ret_head(ins[5 * d:5 * d + 5], h)
                s = s_s[d, h]
                st_ref[h, d] = s.astype(BF)
                sc = _dot(q, k, 1, 1) * mask
                o_ref[:, h * RET_DV:(h + 1) * RET_DV] = (_dot(sc, v) + _dot(q * qd, s)).astype(BF)
                s_s[d, h] = s * gc + _dot(k * kd, v, 0, 0)

    return _call(
        body, [lg] + [px] * 10 + [states0], comm, name="ret_fwd", grid=(n_samp, nc),
        in_specs=[SMEM] + _ret_specs(row_f, row_b) + [
            pl.BlockSpec((None, 2, RET_HEADS, RET_DK, RET_DV), lambda b, n: (b, 0, 0, 0, 0))],
        out_specs=(pl.BlockSpec((c, wo), lambda b, n: (row_f(b, n), 0)),
                   pl.BlockSpec((c, wo), lambda b, n: (row_b(b, n), 0)),
                   pl.BlockSpec((None, RET_HEADS, 2, None, RET_DK, RET_DV), lambda b, n: (b, 0, 0, n, 0, 0))),
        out_shape=(SDS((t_lat, wo), BF), SDS((t_lat, wo), BF),
                   SDS((n_samp, RET_HEADS, 2, nc, RET_DK, RET_DV), BF)),
        scratch_shapes=[pltpu.VMEM((2, RET_HEADS, RET_DK, RET_DV), F32)],
        compiler_params=_cp(("arbitrary", "arbitrary"), 48))


def _ret_bwd(dpx, px, do, saved, lg, n_samp, seq, comm):
    c = RET_CHUNK
    nc = seq // c
    assert nc % 2 == 0
    wq, wo = RET_HEADS * RET_DK, RET_HEADS * RET_DV

    def row_f(b, n):
        return b * nc + nc - 1 - n

    def row_b(b, n):
        return b * nc + n

    def body(lg_ref, *refs):
        ins = refs[:10]
        (dof_ref, dob_ref, st_ref, dpx_in, dpx_hbm, ds0_ref, dlg_ref,
         ds_s, acc_s, sq_s, sk_s, sv_s, sems) = refs[10:]
        b, n = pl.program_id(0), pl.program_id(1)
        second = n >= nc // 2
        chunks = (nc - 1 - n, n)

        def parked(ch):
            return pl.ds(pl.multiple_of(ch * c, c), c)

        def flush():
            cps = []
            for d, ch in enumerate(chunks):
                rows = pl.ds(pl.multiple_of((b * nc + ch) * c, c), c)
                cps += [pltpu.make_async_copy(sq_s.at[parked(ch), :], dpx_hbm.at[rows, pl.ds(C_RQ, wq)], sems.at[3 * d]),
                        pltpu.make_async_copy(sk_s.at[parked(ch), :], dpx_hbm.at[rows, pl.ds(C_RK, wq)],
                                              sems.at[3 * d + 1]),
                        pltpu.make_async_copy(sv_s.at[parked(ch), :], dpx_hbm.at[rows, pl.ds(C_RV, wo)],
                                              sems.at[3 * d + 2])]
            return cps

        @pl.when(jnp.logical_or(n > nc // 2, jnp.logical_and(n == 0, b > 0)))
        def _():
            for cp in flush():
                cp.wait()

        @pl.when(n == 0)
        def _():
            ds_s[...] = jnp.zeros_like(ds_s)
            acc_s[...] = jnp.zeros_like(acc_s)

        def chains(first_visit):
            for d, do_ref in enumerate((dof_ref, dob_ref)):
                rows = parked(chunks[d])
                for h in range(RET_HEADS):
                    a_i, rel, mask, qd, kd, gc = _decays(lg_ref[d, h], d)
                    q, k, v = _ret_head(ins[5 * d:5 * d + 5], h)
                    qb, kb, vb = q.astype(BF), k.astype(BF), v.astype(BF)
                    cq, cv = slice(h * RET_DK, (h + 1) * RET_DK), slice(h * RET_DV, (h + 1) * RET_DV)
                    dob = do_ref[:, cv].astype(BF)
                    sb = st_ref[h, d]
                    ds = ds_s[d, h]
                    dsb = ds.astype(BF)
                    raw = _dot(qb, kb, 1, 1)
                    sc = raw * mask
                    dsc = _dot(dob, vb, 1, 1) * mask
                    dscb = dsc.astype(BF)
                    x = _dot(dob, sb, 1, 1)
                    y = _dot(vb, dsb, 1, 1)
                    qq = q * qd
                    kk = k * kd
                    dq = _dot(dscb, kb) + x * qd
                    dk = _dot(dscb, qb, 0, 0) + y * kd
                    dv = _dot(sc, dob, 0, 0) + _dot(kk, dsb)
                    if first_visit:
                        sq_s[rows, cq] = dq.astype(BF)
                        sk_s[rows, cq] = dk.astype(BF)
                        sv_s[rows, cv] = dv.astype(BF)
                    else:
                        sq_s[rows, cq] = (sq_s[rows, cq].astype(F32) + dq).astype(BF)
                        sk_s[rows, cq] = ((sk_s[rows, cq].astype(F32) + dk) * (RET_DK ** -0.5)).astype(BF)
                        sv_s[rows, cv] = (sv_s[rows, cv].astype(F32) + dv).astype(BF)
                    t = (_sum_all(dsc * raw * rel) + _sum_all((a_i + 1.0) * qq * x)
                         + _sum_all((c - 1.0 - a_i) * kk * y) + c * gc * _sum_all(ds * sb.astype(F32)))
                    acc_s[4 * d + h:4 * d + h + 1, :] += t
                    ds_s[d, h] = ds * gc + _dot(qq, dob, 0, 0)

        @pl.when(jnp.logical_not(second))
        def _():
            chains(True)

        @pl.when(second)
        def _():
            chains(False)
            for cp in flush():
                cp.start()

        @pl.when(n == nc - 1)
        def _():
            ds0_ref[...] = ds_s[...]
            dlg_ref[...] = acc_s[...]

        @pl.when(jnp.logical_and(b == n_samp - 1, n == nc - 1))
        def _():
            for cp in flush():
                cp.wait()

    do_spec_f = pl.BlockSpec((c, wo), lambda b, n: (row_f(b, n), 0))
    do_spec_b = pl.BlockSpec((c, wo), lambda b, n: (row_b(b, n), 0))
    return _call(
        body, [lg] + [px] * 10 + [do, do, saved, dpx], comm, name="ret_bwd", grid=(n_samp, nc), aliases={14: 0},
        in_specs=[SMEM] + _ret_specs(row_f, row_b) + [
            do_spec_f, do_spec_b,
            pl.BlockSpec((None, RET_HEADS, 2, None, RET_DK, RET_DV), lambda b, n: (b, 0, 0, nc - 1 - n, 0, 0)),
            ANY],
        out_specs=(ANY,
                   pl.BlockSpec((None, 2, RET_HEADS, RET_DK, RET_DV), lambda b, n: (b, 0, 0, 0, 0)),
                   pl.BlockSpec((None, 8, 128), lambda b, n: (b, 0, 0))),
        out_shape=(SDS(dpx.shape, dpx.dtype),
                   SDS((n_samp, 2, RET_HEADS, RET_DK, RET_DV), F32), SDS((n_samp, 8, 128), F32)),
        scratch_shapes=[pltpu.VMEM((2, RET_HEADS, RET_DK, RET_DV), F32), pltpu.VMEM((8, 128), F32),
                        pltpu.VMEM((seq, wq), BF), pltpu.VMEM((seq, wq), BF), pltpu.VMEM((seq, wo), BF),
                        pltpu.SemaphoreType.DMA((6,))],
        compiler_params=_cp(("arbitrary", "arbitrary"), 60))


def _norm_rope(x, w, cos, sin):
    xn = x * _rms(x) * w
    return xn * cos + _swap_pairs(xn) * sin


def _norm_rope_bwd(dy, x, w, cos, sin):
    dxn = dy * cos + _swap_pairs(dy * sin)
    r = _rms(x)
    xh = x * r
    return _rms_bwd(dxn * w, xh, r), jnp.sum(dxn * xh, axis=0, keepdims=True)


def _att_prep_q(px, cos_all, sin_all, qnw, t_lat):
    hd = ATT_HEAD_DIM
    wblk = ATT_REP * hd

    def body(x_ref, cos_ref, sin_ref, w_ref, o_ref):
        for r in range(ATT_REP):
            cols = slice(r * hd, (r + 1) * hd)
            qr = _norm_rope(x_ref[:, cols].astype(F32), w_ref[...], cos_ref[...], sin_ref[...])
            o_ref[:, cols] = (qr * (hd ** -0.5)).astype(BF)

    return pl.pallas_call(
        body, name="att_prep_q", grid=(t_lat // TM, ATT_KV_HEADS),
        in_specs=[pl.BlockSpec((TM, wblk), lambda i, g: (i, C_AQ // wblk + g)),
                  pl.BlockSpec((TM, hd), lambda i, g: (i, 0)),
                  pl.BlockSpec((TM, hd), lambda i, g: (i, 0)),
                  pl.BlockSpec((1, hd), lambda i, g: (0, 0))],
        out_specs=pl.BlockSpec((TM, wblk), lambda i, g: (i, g)),
        out_shape=SDS((t_lat, ATT_HEADS * hd), BF),
        compiler_params=_cp(("parallel", "parallel")))(px, cos_all, sin_all, qnw)


def _att_prep_kv(px, cos_all, sin_all, knw):
    rows = px.shape[0]
    hd = ATT_HEAD_DIM
    kvw = ATT_KV_HEADS * hd

    def body(x_ref, cos_ref, sin_ref, w_ref, k_ref, v_ref):
        for g in range(ATT_KV_HEADS):
            cols = slice(g * hd, (g + 1) * hd)
            k_ref[:, cols] = _norm_rope(x_ref[:, cols].astype(F32), w_ref[...], cos_ref[...],
                                        sin_ref[...]).astype(BF)
            v_ref[:, 2 * g * hd:(2 * g + 1) * hd] = x_ref[:, kvw + g * hd:kvw + (g + 1) * hd].astype(BF)
            v_ref[:, (2 * g + 1) * hd:(2 * g + 2) * hd] = jnp.ones((TM, hd), BF)

    return pl.pallas_call(
        body, name="att_prep_kv", grid=(rows // TM,),
        in_specs=[pl.BlockSpec((TM, 2 * kvw), lambda i: (i, C_AK // (2 * kvw))),
                  pl.BlockSpec((TM, hd), lambda i: (i, 0)),
                  pl.BlockSpec((TM, hd), lambda i: (i, 0)),
                  pl.BlockSpec((1, hd), lambda i: (0, 0))],
        out_specs=(pl.BlockSpec((TM, kvw), lambda i: (i, 0)), pl.BlockSpec((TM, 2 * kvw), lambda i: (i, 0))),
        out_shape=(SDS((rows, kvw), BF), SDS((rows, 2 * kvw), BF)),
        compiler_params=_cp(("parallel",)))(px, cos_all, sin_all, knw)


def _att_kv_bwd(dpx, dkl, dkc, dvl, dvc, px, cos_all, sin_all, knw):
    rows = px.shape[0]
    hd = ATT_HEAD_DIM
    kvw = ATT_KV_HEADS * hd
    n_lat = dkl.shape[0] // TM
    assert dkc.shape[0] == TM

    def body(dpx_hbm, dkl_ref, dkc_ref, dvl_ref, dvc_ref, x_ref, cos_ref, sin_ref, w_ref, o_ref, gw_ref):
        i = pl.program_id(0)

        @pl.when(i == 0)
        def _():
            gw_ref[...] = jnp.zeros_like(gw_ref)

        is_lat = i < n_lat
        dk = jnp.where(is_lat, dkl_ref[...], dkc_ref[...])
        dv = jnp.where(is_lat, dvl_ref[...], dvc_ref[...])
        for g in range(ATT_KV_HEADS):
            cols = slice(g * hd, (g + 1) * hd)
            dx, gw = _norm_rope_bwd(dk[:, cols], x_ref[:, cols].astype(F32), w_ref[...], cos_ref[...], sin_ref[...])
            o_ref[:, cols] = dx.astype(BF)
            gw_ref[...] += gw
        o_ref[:, kvw:] = dv.astype(BF)

    lat = pl.BlockSpec((TM, kvw), lambda i: (jnp.minimum(i, n_lat - 1), 0))
    ctx = pl.BlockSpec((TM, kvw), lambda i: (0, 0))
    kvcol = pl.BlockSpec((TM, 2 * kvw), lambda i: (i, C_AK // (2 * kvw)))
    return pl.pallas_call(
        body, name="att_kv_bwd", grid=(rows // TM,), input_output_aliases={0: 0},
        in_specs=[ANY, lat, ctx, lat, ctx, kvcol,
                  pl.BlockSpec((TM, hd), lambda i: (i, 0)),
                  pl.BlockSpec((TM, hd), lambda i: (i, 0)),
                  pl.BlockSpec((1, hd), lambda i: (0, 0))],
        out_specs=(kvcol, pl.BlockSpec((1, hd), lambda i: (0, 0))),
        out_shape=(SDS(dpx.shape, dpx.dtype), SDS((1, hd), F32)),
        compiler_params=_cp(("arbitrary",)))(dpx, dkl, dkc, dvl, dvc, px, cos_all, sin_all, knw)


def _stack_heads(ref_or_val):
    hd = ATT_HEAD_DIM
    return jnp.concatenate([ref_or_val[:, r * hd:(r + 1) * hd] for r in range(ATT_REP)], axis=0)


def _att_scores(q, kl, kc):
    sl = _dot(q, kl, 1, 1)
    sc = _dot(q, kc, 1, 1)
    m = jnp.maximum(jnp.max(sl, axis=-1, keepdims=True), jnp.max(sc, axis=-1, keepdims=True))
    return jnp.exp(sl - m), jnp.exp(sc - m), m


def _att_fwd(qn, kn, vn, n_samp, seq, lc):
    hd = ATT_HEAD_DIM
    tq = ATT_TQ
    nq = seq // tq
    wblk = ATT_REP * hd
    cb = n_samp * seq // lc
    t_lat = n_samp * seq

    def body(q_ref, kl_ref, kc_ref, vl_ref, vc_ref, o_ref, lse_ref):
        lane = lax.broadcasted_iota(jnp.int32, (tq, hd), 1)
        lse = jnp.zeros((tq, hd), F32)
        for r in range(ATT_REP):
            cols = slice(r * hd, (r + 1) * hd)
            el, ec, m = _att_scores(q_ref[:, cols], kl_ref[...], kc_ref[...])
            pv = _dot(el, vl_ref[...]) + _dot(ec, vc_ref[...])
            denom = pv[:, hd:hd + 1]
            o_ref[:, cols] = (pv[:, :hd] / denom).astype(BF)
            lse = jnp.where(lane == r, m + jnp.log(denom), lse)
        lse_ref[...] = lse

    return pl.pallas_call(
        body, name="att_fwd", grid=(n_samp, ATT_KV_HEADS, nq),
        in_specs=[pl.BlockSpec((tq, wblk), lambda b, g, i: (b * nq + i, g)),
                  pl.BlockSpec((seq, hd), lambda b, g, i: (b, g)),
                  pl.BlockSpec((lc, hd), lambda b, g, i: (cb + b, g)),
                  pl.BlockSpec((seq, 2 * hd), lambda b, g, i: (b, g)),
                  pl.BlockSpec((lc, 2 * hd), lambda b, g, i: (cb + b, g))],
        out_specs=(pl.BlockSpec((tq, wblk), lambda b, g, i: (b * nq + i, g)),
                   pl.BlockSpec((tq, hd), lambda b, g, i: (b * nq + i, g))),
        out_shape=(SDS((t_lat, ATT_HEADS * hd), BF), SDS((t_lat, ATT_KV_HEADS * hd), F32)),
        compiler_params=_cp(("parallel", "parallel", "parallel"), 48))(qn, kn, kn, vn, vn)


def _att_bwd(dpx, qn, kn, vn, px, o_att, lse, do_att, cos_all, sin_all, qnw, n_samp, seq, lc, comm):
    hd = ATT_HEAD_DIM
    tq = ATT_TQ
    nq = seq // tq
    wblk = ATT_REP * hd
    cb = n_samp * seq // lc
    t_lat = n_samp * seq
    kvw = ATT_KV_HEADS * hd
    scale = hd ** -0.5

    def body(dpx_hbm, q_ref, kl_ref, kc_ref, vl_ref, vc_ref, o_ref, do_ref, x_ref, cos_ref, sin_ref, w_ref,
             lse_ref, dq_ref, dkl_ref, dkc_ref, dvl_ref, dvc_ref, gw_ref, akl, akc, avl, avc, aw):
        i = pl.program_id(2)

        @pl.when(i == 0)
        def _():
            akl[...] = jnp.zeros_like(akl)
            akc[...] = jnp.zeros_like(akc)
            avl[...] = jnp.zeros_like(avl)
            avc[...] = jnp.zeros_like(avc)
            aw[...] = jnp.zeros_like(aw)

        dobs, pls, pcs, dsls, dscs = [], [], [], [], []
        for r in range(ATT_REP):
            cols = slice(r * hd, (r + 1) * hd)
            dob = do_ref[:, cols]
            delta = jnp.sum(dob.astype(F32) * o_ref[:, cols].astype(F32), axis=-1, keepdims=True)
            lse = lse_ref[:, r:r + 1]
            p_l = jnp.exp(_dot(q_ref[:, cols], kl_ref[...], 1, 1) - lse).astype(BF)
            p_c = jnp.exp(_dot(q_ref[:, cols], kc_ref[...], 1, 1) - lse).astype(BF)
            ds_l = (p_l * (_dot(dob, vl_ref[...], 1, 1) - delta)).astype(BF)
            ds_c = (p_c * (_dot(dob, vc_ref[...], 1, 1) - delta)).astype(BF)
            dq = (_dot(ds_l, kl_ref[...]) + _dot(ds_c, kc_ref[...])) * scale
            dx, gw = _norm_rope_bwd(dq, x_ref[:, cols].astype(F32), w_ref[...], cos_ref[...], sin_ref[...])
            dq_ref[:, cols] = dx.astype(BF)
            aw[...] += gw
            dobs.append(dob)
            pls.append(p_l)
            pcs.append(p_c)
            dsls.append(ds_l)
            dscs.append(ds_c)
        do4 = jnp.concatenate(dobs, axis=0)
        q4 = _stack_heads(q_ref)
        avl[...] += _dot(jnp.concatenate(pls, axis=0), do4, 0, 0)
        avc[...] += _dot(jnp.concatenate(pcs, axis=0), do4, 0, 0)
        akl[...] += _dot(jnp.concatenate(dsls, axis=0), q4, 0, 0)
        akc[...] += _dot(jnp.concatenate(dscs, axis=0), q4, 0, 0)

        @pl.when(i == nq - 1)
        def _():
            dkl_ref[...] = akl[...]
            dkc_ref[...] = akc[...]
            dvl_ref[...] = avl[...]
            dvc_ref[...] = avc[...]
            gw_ref[...] = aw[...]

    return _call(
        body, [dpx, qn, kn, kn, vn, vn, o_att, do_att, px, cos_all, sin_all, qnw, lse], comm,
        name="att_bwd", grid=(n_samp, ATT_KV_HEADS, nq), aliases={0: 0},
        in_specs=[ANY,
                  pl.BlockSpec((tq, wblk), lambda b, g, i: (b * nq + i, g)),
                  pl.BlockSpec((seq, hd), lambda b, g, i: (b, g)),
                  pl.BlockSpec((lc, hd), lambda b, g, i: (cb + b, g)),
                  pl.BlockSpec((seq, hd), lambda b, g, i: (b, 2 * g)),
                  pl.BlockSpec((lc, hd), lambda b, g, i: (cb + b, 2 * g)),
                  pl.BlockSpec((tq, wblk), lambda b, g, i: (b * nq + i, g)),
                  pl.BlockSpec((tq, wblk), lambda b, g, i: (b * nq + i, g)),
                  pl.BlockSpec((tq, wblk), lambda b, g, i: (b * nq + i, C_AQ // wblk + g)),
                  pl.BlockSpec((tq, hd), lambda b, g, i: (b * nq + i, 0)),
                  pl.BlockSpec((tq, hd), lambda b, g, i: (b * nq + i, 0)),
                  pl.BlockSpec((1, hd), lambda b, g, i: (0, 0)),
                  pl.BlockSpec((tq, hd), lambda b, g, i: (b * nq + i, g))],
        out_specs=(pl.BlockSpec((tq, wblk), lambda b, g, i: (b * nq + i, C_AQ // wblk + g)),
                   pl.BlockSpec((seq, hd), lambda b, g, i: (b, g)),
                   pl.BlockSpec((lc, hd), lambda b, g, i: (b, g)),
                   pl.BlockSpec((seq, hd), lambda b, g, i: (b, g)),
                   pl.BlockSpec((lc, hd), lambda b, g, i: (b, g)),
                   pl.BlockSpec((None, None, 1, hd), lambda b, g, i: (b, g, 0, 0))),
        out_shape=(SDS(dpx.shape, dpx.dtype),
                   SDS((t_lat, kvw), F32), SDS((n_samp * lc, kvw), F32),
                   SDS((t_lat, kvw), F32), SDS((n_samp * lc, kvw), F32),
                   SDS((n_samp, ATT_KV_HEADS, 1, hd), F32)),
        scratch_shapes=[pltpu.VMEM((seq, hd), F32), pltpu.VMEM((lc, hd), F32),
                        pltpu.VMEM((seq, hd), F32), pltpu.VMEM((lc, hd), F32), pltpu.VMEM((1, hd), F32)],
        compiler_params=_cp(("arbitrary", "arbitrary", "arbitrary"), 56))


def _merge(x_lat, target, o_f, o_b, o_att, px, gate3, w_o_ret, w_o_att, w_out, tiles_per_sample):
    t_lat = x_lat.shape[0]
    tm = 256
    n_t = t_lat // tm
    per = tiles_per_sample * (TM // tm)
    d = D_MODEL
    rv = RET_HEADS * RET_DV
    n_samp = gate3.shape[0] - 1

    half = d // 2
    n_px = 10

    def body(x_ref, t_ref, of_ref, ob_ref, oa_ref, *rest):
        pxs, rest = rest[:n_px], rest[n_px:]
        (gt_ref, wor_ref, woa_ref, wout_ref,
         gx_ref, dor_ref, doa_ref, dpx_hbm, loss_ref, dgt_ref, gwor_hbm, gwoa_hbm, gwout_hbm,
         aor, aoa, aout, drg_ref, dtail_ref, sems) = rest
        i = pl.program_id(0)

        def copies(step):
            rows = pl.ds(pl.multiple_of(step * tm, tm), tm)
            return (pltpu.make_async_copy(drg_ref, dpx_hbm.at[rows, pl.ds(C_RG, rv)], sems.at[0]),
                    pltpu.make_async_copy(dtail_ref, dpx_hbm.at[rows, pl.ds(C_AG, 3 * d)], sems.at[1]))

        @pl.when(i == 0)
        def _():
            aor[...] = jnp.zeros_like(aor)
            aoa[...] = jnp.zeros_like(aoa)
            aout[...] = jnp.zeros_like(aout)
            loss_ref[...] = jnp.zeros_like(loss_ref)

        @pl.when(i % per == 0)
        def _():
            dgt_ref[...] = jnp.zeros_like(dgt_ref)

        def cat(refs):
            return jnp.concatenate([r[...] for r in refs], axis=1).astype(F32)

        def ret_head(h):
            cols = slice(h * RET_DV, (h + 1) * RET_DV)
            o = of_ref[:, cols].astype(F32) + ob_ref[:, cols].astype(F32)
            r = _rms(o)
            g = pxs[h][...].astype(F32)
            return o * r, r, g, _sigmoid(g)

        def att_half(k):
            o = oa_ref[:, k * half:(k + 1) * half].astype(F32)
            g = pxs[4 + k][...].astype(F32)
            return o, g, _sigmoid(g)

        yrs = []
        for h in range(RET_HEADS):
            on, _, g, sg = ret_head(h)
            yrs.append((on * (g * sg)).astype(BF))
        yr = jnp.concatenate(yrs, axis=1)
        yas = []
        for k in range(2):
            o, g, sg = att_half(k)
            yas.append((o * (g * sg)).astype(BF))
        ya = jnp.concatenate(yas, axis=1)

        a = jnp.dot(yr, wor_ref[...], preferred_element_type=F32)
        b = jnp.dot(ya, woa_ref[...], preferred_element_type=F32)
        sr = _sigmoid(cat(pxs[6:8]))
        sa = _sigmoid(cat(pxs[8:10]))
        yb = (sr * a + sa * b).astype(BF)
        out = jnp.dot(yb, wout_ref[...], preferred_element_type=F32)
        gate = gt_ref[...]
        err = x_ref[...] + gate * out - t_ref[...]
        loss_ref[...] += 0.5 * _sum_all(err * err) * (1.0 / d)
        dy_tok = err * (1.0 / d)
        gx_ref[...] = dy_tok
        dgt_ref[...] += jnp.sum(dy_tok * out, axis=0, keepdims=True)
        dout = (dy_tok * gate).astype(BF)
        aout[...] += _dot(yb, dout, 0, 0)
        dyy = _dot(dout, wout_ref[...], 1, 1)
        da = (dyy * sr).astype(BF)
        db = (dyy * sa).astype(BF)
        aor[...] += _dot(yr, da, 0, 0)
        aoa[...] += _dot(ya, db, 0, 0)
        dyr = _dot(da, wor_ref[...], 1, 1)
        dya = _dot(db, woa_ref[...], 1, 1)

        @pl.when(i > 0)
        def _():
            for cp in copies(i - 1):
                cp.wait()

        dtail_ref[:, d:2 * d] = (dyy * a * (sr * (1.0 - sr))).astype(BF)
        dtail_ref[:, 2 * d:] = (dyy * b * (sa * (1.0 - sa))).astype(BF)
        for h in range(RET_HEADS):
            cols = slice(h * RET_DV, (h + 1) * RET_DV)
            on, r, g, sg = ret_head(h)
            dy = dyr[:, cols]
            drg_ref[:, cols] = (dy * on * (sg * (1.0 + g * (1.0 - sg)))).astype(BF)
            dor_ref[:, cols] = _rms_bwd(dy * (g * sg), on, r).astype(BF)
        for k in range(2):
            cols = slice(k * half, (k + 1) * half)
            o, g, sg = att_half(k)
            dy = dya[:, cols]
            dtail_ref[:, cols] = (dy * o * (sg * (1.0 + g * (1.0 - sg)))).astype(BF)
            doa_ref[:, cols] = (dy * (g * sg)).astype(BF)
        for cp in copies(i):
            cp.start()

        @pl.when(i == n_t - 1)
        def _():
            for cp in copies(i):
                cp.wait()
            pltpu.sync_copy(aor, gwor_hbm)
            pltpu.sync_copy(aoa, gwoa_hbm)
            pltpu.sync_copy(aout, gwout_hbm)

    def px_blk(col):
        return pl.BlockSpec((tm, half), lambda i: (i, col // half))

    def resident(shape):
        return pl.BlockSpec(shape, lambda i: (0, 0), pipeline_mode=pl.Buffered(1))

    px_cols = ([C_RG + k * half for k in range(4)] + [C_AG, C_AG + half]
               + [C_MR, C_MR + half, C_MA, C_MA + half])
    return pl.pallas_call(
        body, name="merge", grid=(n_t,),
        in_specs=[pl.BlockSpec((tm, d), lambda i: (i, 0)),
                  pl.BlockSpec((tm, d), lambda i: (i, 0)),
                  pl.BlockSpec((tm, rv), lambda i: (i, 0)),
                  pl.BlockSpec((tm, rv), lambda i: (i, 0)),
                  pl.BlockSpec((tm, d), lambda i: (i, 0))]
        + [px_blk(col) for col in px_cols]
        + [pl.BlockSpec((None, 1, d), lambda i: (i // per, 0, 0)),
           resident((rv, d)), resident((d, d)), resident((d, d))],
        out_specs=(pl.BlockSpec((tm, d), lambda i: (i, 0)),
                   pl.BlockSpec((tm, rv), lambda i: (i, 0)),
                   pl.BlockSpec((tm, d), lambda i: (i, 0)),
                   ANY,
                   pl.BlockSpec((8, 128), lambda i: (0, 0)),
                   pl.BlockSpec((None, 1, d), lambda i: (i // per, 0, 0)),
                   ANY, ANY, ANY),
        out_shape=(SDS((t_lat, d), F32), SDS((t_lat, rv), BF), SDS((t_lat, d), BF),
                   SDS((px.shape[0], IN_COLS), BF),
                   SDS((8, 128), F32), SDS((n_samp, 1, d), F32),
                   SDS((rv, d), F32), SDS((d, d), F32), SDS((d, d), F32)),
        scratch_shapes=[pltpu.VMEM((rv, d), F32), pltpu.VMEM((d, d), F32), pltpu.VMEM((d, d), F32),
                        pltpu.VMEM((tm, rv), BF), pltpu.VMEM((tm, 3 * d), BF), pltpu.SemaphoreType.DMA((2,))],
        compiler_params=_cp(("arbitrary",), 56))(
            x_lat, target, o_f, o_b, o_att, *([px] * n_px), gate3, w_o_ret, w_o_att, w_out)


def _place():
    x, y, c = lax.axis_index("x"), lax.axis_index("y"), lax.axis_index("c")
    chips = [(1 - x, y), (x, 1 - y), (1 - x, 1 - y)]
    return x, y, c, chips


def _remote(src, dst, send_sem, recv_sem, to):
    return pltpu.make_async_remote_copy(src_ref=src, dst_ref=dst, send_sem=send_sem, recv_sem=recv_sem,
                                        device_id=to, device_id_type=MESH)


def _place_ids():
    x, y, c = lax.axis_index("x"), lax.axis_index("y"), lax.axis_index("c")
    me = 2 * x + y
    return jnp.stack([x, y, c, me, me, 2 * (1 - x) + y, 2 * x + 1 - y, 2 * (1 - x) + 1 - y]).astype(jnp.int32)


def _ag_comm(bufs):
    n, m = len(bufs), 3

    def half(ref, s, which):
        h = ref.shape[1] // 2
        return ref.at[s, pl.ds(which * h, h), :]

    def ici(ins, outs, ssem, rsem, base):
        x, y, c, chips = _place()
        sends, recvs = [], []
        for a in range(n):
            for j in range(m):
                k, chip = base + a * m + j, chips[j]
                mine, theirs = half(outs[a], 2 * x + y, c), half(outs[a], 2 * chip[0] + chip[1], c)
                sends.append(_remote(mine, mine, ssem.at[k], rsem.at[k], (*chip, c)))
                recvs.append(_remote(theirs, theirs, ssem.at[k], rsem.at[k], (*chip, c)))
        return sends, recvs

    def d2d(ins, outs, ssem, rsem, base):
        x, y, c, chips = _place()
        sends, recvs = [], []
        for a in range(n):
            for j in range(m):
                k, s = base + (n + a) * m + j, 2 * chips[j][0] + chips[j][1]
                sends.append(_remote(half(outs[a], s, c), half(outs[a], s, c), ssem.at[k], rsem.at[k], (x, y, 1 - c)))
                recvs.append(_remote(half(outs[a], s, 1 - c), half(outs[a], s, 1 - c), ssem.at[k], rsem.at[k],
                                     (x, y, 1 - c)))
        return sends, recvs

    return _Comm("all_gather", tuple(bufs), tuple(SDS(b.shape, b.dtype) for b in bufs), {a: a for a in range(n)},
                 2 * n * m, (ici, d2d))


def _swap_comm(grads):
    n = len(grads)

    def phase(ins, outs, ssem, rsem, base):
        x, y, c, _ = _place()
        sends = []
        for a in range(n):
            h = ins[a].shape[1] // 2
            sends.append(_remote(ins[a].at[:, pl.ds((1 - c) * h, h), :], outs[a], ssem.at[base + a],
                                 rsem.at[base + a], (x, y, 1 - c)))
        return sends, sends

    return _Comm("swap_halves", tuple(grads),
                 tuple(SDS((g.shape[0], g.shape[1] // 2, g.shape[2]), g.dtype) for g in grads), {}, n, (phase,))


def _exchange_comm(parts):
    n = len(parts)

    def phase(ins, outs, ssem, rsem, base):
        x, y, c, chips = _place()
        sends = []
        for a in range(n):
            for j, chip in enumerate(chips):
                k = base + 3 * a + j
                sends.append(_remote(ins[a].at[2 * chip[0] + chip[1]], outs[a].at[j], ssem.at[k], rsem.at[k],
                                     (*chip, c)))
        return sends, sends

    return _Comm("exchange_shards", tuple(parts), tuple(SDS((3,) + p.shape[1:], p.dtype) for p in parts), {}, 3 * n,
                 (phase,))


def _join_comm(bufs):
    n = len(bufs)

    def phase(ins, outs, ssem, rsem, base):
        x, y, c, _ = _place()
        sends, recvs = [], []
        for a in range(n):
            h = outs[a].shape[0] // 2
            mine, other = outs[a].at[pl.ds(c * h, h), :], outs[a].at[pl.ds((1 - c) * h, h), :]
            sends.append(_remote(mine, mine, ssem.at[base + a], rsem.at[base + a], (x, y, 1 - c)))
            recvs.append(_remote(other, other, ssem.at[base + a], rsem.at[base + a], (x, y, 1 - c)))
        return sends, recvs

    return _Comm("join_halves", tuple(bufs), tuple(SDS(b.shape, b.dtype) for b in bufs), {a: a for a in range(n)},
                 n, (phase,))


def _cast_place(w, ids):
    rows, cols = w.shape
    tr = min(rows, 256)

    def body(ids_ref, w_ref, o_ref):
        o_ref[...] = w_ref[...].astype(BF)

    return pl.pallas_call(
        body, name="cast_place",
        grid_spec=pltpu.PrefetchScalarGridSpec(
            num_scalar_prefetch=1, grid=(rows // tr,),
            in_specs=[pl.BlockSpec((tr, cols), lambda i, ids_ref: (i, 0))],
            out_specs=pl.BlockSpec((None, tr, cols), lambda i, ids_ref: (ids_ref[3], i, 0))),
        out_shape=SDS((N_SHARD, rows, cols), BF),
        compiler_params=_cp(("parallel",), 40))(ids, w)


def _chip_sum(g, p, ids):
    n_s, rows, cols = g.shape
    h = rows // 2
    tr = min(h, 256)
    nb = h // tr

    def body(ids_ref, g_ref, p_ref, o_ref, o16_ref):
        t = g_ref[...] + p_ref[...]
        o_ref[...] = t
        o16_ref[...] = t.astype(BF)

    out_spec = pl.BlockSpec((None, tr, cols), lambda s, i, ids_ref: (s, i, 0))
    return pl.pallas_call(
        body, name="chip_sum",
        grid_spec=pltpu.PrefetchScalarGridSpec(
            num_scalar_prefetch=1, grid=(n_s, nb),
            in_specs=[pl.BlockSpec((None, tr, cols), lambda s, i, ids_ref: (s, ids_ref[2] * nb + i, 0)),
                      pl.BlockSpec((None, tr, cols), lambda s, i, ids_ref: (s, i, 0))],
            out_specs=(out_spec, out_spec)),
        out_shape=(SDS((n_s, h, cols), g.dtype), SDS((n_s, h, cols), BF)),
        compiler_params=_cp(("parallel", "parallel"), 40))(ids, g, p)


def _shard_sum(t, q, ids):
    _, h, cols = t.shape
    tr = min(h, 256)
    nb = h // tr

    def body(ids_ref, t_ref, q_ref, o_ref):
        o_ref[...] = ((t_ref[...] + q_ref[0].astype(F32)) + q_ref[1].astype(F32)) + q_ref[2].astype(F32)

    return pl.pallas_call(
        body, name="shard_sum",
        grid_spec=pltpu.PrefetchScalarGridSpec(
            num_scalar_prefetch=1, grid=(nb,),
            in_specs=[pl.BlockSpec((None, tr, cols), lambda i, ids_ref: (ids_ref[3], i, 0)),
                      pl.BlockSpec((3, tr, cols), lambda i, ids_ref: (0, i, 0))],
            out_specs=pl.BlockSpec((tr, cols), lambda i, ids_ref: (ids_ref[2] * nb + i, 0))),
        out_shape=SDS((2 * h, cols), t.dtype),
        compiler_params=_cp(("parallel",), 40))(ids, t, q)


def _gather_small(block, n_sum):
    rows, cols = block.shape
    n_dev = 8

    def body(x_ref, o_ref, g_ref, buf, send_sems, recv_sems, local_sem):
        x, y, c, chips = _place()
        me, sibling = (x, y, c), (x, y, 1 - c)

        def slot(px_, py_, pc_):
            return buf.at[4 * px_ + 2 * py_ + pc_]

        def copy(k, who, to, src=None):
            return _remote(slot(*who) if src is None else src, slot(*who), send_sems.at[k], recv_sems.at[k], to)

        mine = pltpu.make_async_copy(x_ref, slot(*me), local_sem)
        mine.start()
        first = [copy(0, me, sibling, src=x_ref)]
        first += [copy(1 + j, me, (*chip, c), src=x_ref) for j, chip in enumerate(chips)]
        for cp in first:
            cp.start()
        passed = [copy(4 + j, (*chip, c), sibling) for j, chip in enumerate(chips)]
        for j, chip in enumerate(chips):
            copy(1 + j, (*chip, c), me).wait_recv()
            passed[j].start()
        copy(0, sibling, me).wait_recv()
        for j, chip in enumerate(chips):
            copy(4 + j, (*chip, 1 - c), me).wait_recv()
        for cp in first + passed:
            cp.wait_send()
        mine.wait()
        acc = buf[0, :, :n_sum]
        for s in range(1, n_dev):
            acc = acc + buf[s, :, :n_sum]
        o_ref[...] = acc
        for s in range(n_dev):
            g_ref[s * rows:(s + 1) * rows, :] = buf[s, :, n_sum:]

    return pl.pallas_call(
        body, name="gather_small",
        in_specs=[pl.BlockSpec(memory_space=pltpu.VMEM)],
        out_specs=(pl.BlockSpec(memory_space=pltpu.VMEM), pl.BlockSpec(memory_space=pltpu.VMEM)),
        out_shape=(SDS((rows, n_sum), F32), SDS((n_dev * rows, cols - n_sum), F32)),
        scratch_shapes=[pltpu.VMEM((n_dev, rows, cols), F32), pltpu.SemaphoreType.DMA((7,)),
                        pltpu.SemaphoreType.DMA((7,)), pltpu.SemaphoreType.DMA],
        compiler_params=_cp(has_side_effects=True))(block)


def _adam_math(w, g, m, v):
    m = ADAM_B1 * m + (1.0 - ADAM_B1) * g
    v = ADAM_B2 * v + (1.0 - ADAM_B2) * (g * g)
    m_hat = m / (1.0 - ADAM_B1 ** ADAM_STEP)
    v_hat = v / (1.0 - ADAM_B2 ** ADAM_STEP)
    delta = -ADAM_LR * (m_hat / (jnp.sqrt(v_hat) + ADAM_EPS) + ADAM_WD * w)
    return delta, m, v


def _adamw(w, g, m, v):
    rows, cols = w.shape
    tr = min(rows, 256 if cols <= 2048 else 128)

    def body(w_ref, g_ref, m_ref, v_ref, go_ref, d_ref, nm_ref, nv_ref):
        g = g_ref[...]
        go_ref[...] = g
        d_ref[...], nm_ref[...], nv_ref[...] = _adam_math(w_ref[...], g, m_ref[...], v_ref[...])

    spec = pl.BlockSpec((tr, cols), lambda i: (i, 0))
    return pl.pallas_call(
        body, name="adamw", grid=(rows // tr,), in_specs=[spec] * 4, out_specs=(spec,) * 4,
        out_shape=(SDS(w.shape, F32),) * 4, compiler_params=_cp(("parallel",), 40))(w, g, m, v)


def _adamw_small(w, g, m, v):
    def body(w_ref, g_ref, m_ref, v_ref, go_ref, d_ref, nm_ref, nv_ref):
        w = w_ref[...]
        g = g_ref[...]
        sub = lax.broadcasted_iota(jnp.int32, w.shape, 0)
        lane = lax.broadcasted_iota(jnp.int32, w.shape, 1)
        is_ret = jnp.logical_and(sub == 5, lane < 2 * RET_HEADS)
        u = jnp.exp(jnp.where(is_ret, w, -1.0) * jnp.log(2.0))
        g = jnp.where(is_ret, g * (-u * jnp.log(2.0) / (1.0 - u)), g)
        go_ref[...] = g
        d_ref[...], nm_ref[...], nv_ref[...] = _adam_math(w, g, m_ref[...], v_ref[...])

    return pl.pallas_call(body, name="adamw_small", out_shape=(SDS(w.shape, F32),) * 4)(w, g, m, v)


def _rope_tables(seq, n_samp, n_ctx_rows):
    rows = seq // GRID_W
    row = jnp.repeat(jnp.arange(rows, dtype=F32), GRID_W)
    col = jnp.tile(jnp.arange(GRID_W, dtype=F32), rows)
    half = ATT_HEAD_DIM // 2
    freqs = ROPE_THETA ** (-jnp.arange(0, half, 2, dtype=F32) / half)
    ang = jnp.concatenate([row[:, None] * freqs, col[:, None] * freqs], axis=-1)
    cos, sin = jnp.cos(ang), jnp.sin(ang)
    cos_f = jnp.repeat(cos, 2, axis=1)
    sin_s = jnp.stack([-sin, sin], axis=-1).reshape(seq, ATT_HEAD_DIM)
    cos_all = jnp.concatenate([jnp.tile(cos_f, (n_samp, 1)), jnp.ones((n_ctx_rows, ATT_HEAD_DIM), F32)], axis=0)
    sin_all = jnp.concatenate([jnp.tile(sin_s, (n_samp, 1)), jnp.zeros((n_ctx_rows, ATT_HEAD_DIM), F32)], axis=0)
    return cos_all, sin_all


def _pack_small(c_ctx, norm_w, b_ada, ret, qn, kn):
    d = D_MODEL
    row5 = jnp.concatenate([ret.reshape(-1), jnp.zeros((128 - 2 * RET_HEADS,), F32), qn.reshape(-1), kn.reshape(-1),
                            jnp.zeros((d - 384,), F32)])
    return jnp.concatenate([c_ctx.reshape(1, d), norm_w.reshape(1, d), b_ada.reshape(3, d), row5.reshape(1, d),
                            jnp.zeros((2, d), F32)], axis=0)


def _unpack_small(p):
    d = D_MODEL
    return (p[0], p[1:2], p[2:5].reshape(1, 3 * d), p[5, :2 * RET_HEADS].reshape(1, 2, RET_HEADS),
            p[5:6, 128:256], p[5:6, 256:384])


def _step(x, c, ctx, c_ctx, norm_w, b_ada, ret_log2_decay, q_norm_w, k_norm_w, loss_target, weights, ids):
    n_samp, seq, d = x.shape
    lc = ctx.shape[1]
    t_lat, t_ctx = n_samp * seq, n_samp * lc
    assert seq % TM == 0 and t_ctx == TM and t_lat % lc == 0 and seq % GRID_W == 0
    tps = seq // TM

    x_lat = x.reshape(t_lat, d)
    x_ctx = ctx.reshape(t_ctx, d)
    cvec8 = jnp.concatenate([c, c_ctx.reshape(1, d), jnp.zeros((8 - n_samp - 1, d), F32)], axis=0)
    lg = jnp.log1p(-jnp.exp2(ret_log2_decay.reshape(2, RET_HEADS)))
    cos_all, sin_all = _rope_tables(seq, n_samp, t_ctx)

    w_ada_b, w_in_b, w_or_b, w_oa_b, w_out_b = weights
    c_all, mod_shards = _adaln_fwd(cvec8, w_ada_b, b_ada)
    mod8 = mod_shards.transpose(1, 0, 2).reshape(8, 3 * d)
    mod3 = mod8[:n_samp + 1]
    shift3 = mod3[:, None, 0:d]
    scale3 = mod3[:, None, d:2 * d]
    gate3 = mod3[:, None, 2 * d:3 * d]

    hx, hxt = _norm_fwd(x_lat, x_ctx, norm_w, scale3, shift3, tps, n_samp)
    px, w_in_g = _in_proj_gather(hx, w_in_b, ids)

    states0 = _ctx_state_fwd(px, lg, n_samp, t_lat, lc)
    (o_f, o_b, saved), w_o = _ret_fwd(px, states0, lg, n_samp, seq,
                                      comm=_ag_comm((w_or_b, w_oa_b, w_out_b)))
    w_o_ret, w_o_att, w_out = (w.reshape(-1, d) for w in w_o)

    qn = _att_prep_q(px, cos_all, sin_all, q_norm_w, t_lat)
    kn, vn = _att_prep_kv(px, cos_all, sin_all, k_norm_w)
    o_att, lse = _att_fwd(qn, kn, vn, n_samp, seq, lc)

    (gx_res, do, do_att, dpx, loss8, dgate, g_w_o_ret, g_w_o_att, g_w_out) = _merge(
        x_lat, loss_target.reshape(t_lat, d), o_f, o_b, o_att, px, gate3, w_o_ret, w_o_att, w_out, tps)

    g_a = [g.reshape(N_SHARD, -1, d) for g in (g_w_o_ret, g_w_o_att, g_w_out)]
    (dpx, dkl, dkc, dvl, dvc, gqw), (*sib_a, w_ada_g) = _att_bwd(
        dpx, qn, kn, vn, px, o_att, lse, do_att, cos_all, sin_all, q_norm_w, n_samp, seq, lc,
        comm=_join_comms(_swap_comm(g_a), _ag_comm((w_ada_b,))))
    dpx, gkw = _att_kv_bwd(dpx, dkl, dkc, dvl, dvc, px, cos_all, sin_all, k_norm_w)
    t_a = [_chip_sum(g, p, ids) for g, p in zip(g_a, sib_a)]

    (dpx, dstates, dlg_lat), q_a = _ret_bwd(dpx, px, do, saved, lg, n_samp, seq,
                                            comm=_exchange_comm([t16 for _, t16 in t_a]))
    r_a = [_shard_sum(t, q, ids) for (t, _), q in zip(t_a, q_a)]
    dpx, dlg_ctx = _ctx_state_bwd(dpx, px, dstates, lg, n_samp, t_lat, lc)
    dpx = _zero_ctx_tail(dpx, t_lat)

    n_tiles = dpx.shape[0] // _big_rows(dpx.shape[0])
    g_b = _gw_in(hxt, dpx)
    dhx, (sib_b, *r_a) = _dhx(dpx, w_in_g, 0, 1, None, _join_comms(_swap_comm([g_b]), _join_comm(r_a)))
    t_b, t16_b = _chip_sum(g_b, sib_b, ids)
    dhx, (q_b,) = _dhx(dpx, w_in_g, 1, n_tiles - 1, dhx, _exchange_comm([t16_b]))
    r_b_half = _shard_sum(t_b, q_b, ids)
    grad_x, dshift, dscale, g_norm_w = _norm_bwd(x_lat, x_ctx, dhx, gx_res, norm_w, scale3, tps, n_samp)

    dgate_all = jnp.concatenate([dgate, jnp.zeros((1, 1, d), F32)], axis=0)
    dmod3 = jnp.concatenate([dshift, dscale, dgate_all], axis=2).reshape(n_samp + 1, 3 * d)
    dmod8 = jnp.concatenate([dmod3, jnp.zeros((8 - n_samp - 1, 3 * d), F32)], axis=0)
    g_lg = (jnp.sum(dlg_lat[:, :, 0], axis=0).reshape(2, RET_HEADS)
            + jnp.stack([jnp.sum(dlg_ctx[:, :, 0, 0], axis=0), jnp.sum(dlg_ctx[:, :, 1, 0], axis=0)], axis=0))
    g_qw = jnp.sum(gqw, axis=(0, 1, 2))
    zero = jnp.zeros((d,), F32)

    local = _pack_small(zero, g_norm_w, jnp.zeros((3 * d,), F32), g_lg, g_qw, gkw).at[6, 0].set(loss8[0, 0])
    small_sum, dmod_all = _gather_small(jnp.concatenate([local, dmod8], axis=1), d)
    (g_w_ada, g_b_ada, dc_all), (r_b,) = _adaln_bwd(c_all, dmod_all, w_ada_g, comm=_join_comm([r_b_half]))
    dc_ctx = jnp.sum(dc_all.reshape(-1, 8, d)[:, n_samp], axis=0)
    small = small_sum + _pack_small(dc_ctx, zero, g_b_ada, jnp.zeros((2, RET_HEADS), F32), zero[:128], zero[:128])
    r_c = lax.dynamic_index_in_dim(g_w_ada, ids[3], 0, keepdims=False)
    return small[6, 0], grad_x.reshape(n_samp, seq, d), (r_c, r_b, *r_a), small


def kernel(x, c, ctx, c_ctx, norm_w, w_ada, b_ada, w_in, ret_log2_decay, q_norm_w, k_norm_w, w_o_ret, w_o_att, w_out, loss_target, m_c_ctx, m_norm_w, m_w_ada, m_b_ada, m_w_in, m_ret_log2_decay, m_q_norm_w, m_k_norm_w, m_w_o_ret, m_w_o_att, m_w_out, v_c_ctx, v_norm_w, v_w_ada, v_b_ada, v_w_in, v_ret_log2_decay, v_q_norm_w, v_k_norm_w, v_w_o_ret, v_w_o_att, v_w_out):
    big_w = (w_ada[0], w_in[0], w_o_ret[0], w_o_att[0], w_out[0])
    big_m = (m_w_ada[0], m_w_in[0], m_w_o_ret[0], m_w_o_att[0], m_w_out[0])
    big_v = (v_w_ada[0], v_w_in[0], v_w_o_ret[0], v_w_o_att[0], v_w_out[0])

    ids = _place_ids()
    loss, grad_x, big_grad, small_grad_in = _step(
        x, c, ctx, c_ctx, norm_w[0:1], b_ada[0:1], ret_log2_decay[0], q_norm_w[0:1], k_norm_w[0:1], loss_target,
        tuple(_cast_place(w, ids) for w in big_w), ids)
    small_w = _pack_small(c_ctx, norm_w, b_ada, ret_log2_decay, q_norm_w, k_norm_w)
    small_m = _pack_small(m_c_ctx, m_norm_w, m_b_ada, m_ret_log2_decay, m_q_norm_w, m_k_norm_w)
    small_v = _pack_small(v_c_ctx, v_norm_w, v_b_ada, v_ret_log2_decay, v_q_norm_w, v_k_norm_w)
    small_grad, small_delta, small_nm, small_nv = _adamw_small(small_w, small_grad_in, small_m, small_v)

    big_g, big_delta, big_nm, big_nv = [], [], [], []
    for w, g, m, v in zip(big_w, big_grad, big_m, big_v):
        go, dlt, nm, nv = _adamw(w, g, m, v)
        big_g.append(go[None])
        big_delta.append(dlt[None])
        big_nm.append(nm[None])
        big_nv.append(nv[None])
    big_grad = big_g

    def order(small_packed, big):
        s = _unpack_small(small_packed)
        return (s[0], s[1], big[0], s[2], big[1], s[3], s[4], s[5], big[2], big[3], big[4])

    return (loss, grad_x, *order(small_grad, big_grad), *order(small_delta, big_delta),
            *order(small_nm, big_nm), *order(small_nv, big_nv))
```
